```python
import math
import jax, jax.numpy as jnp
from jax import lax
import numpy as np

D_MODEL = 1024
BATCH = 8
SEQ = 2048
DEPTH = 1

RET_HEADS = 4
RET_DK = 64
RET_DV = 128
RET_CHUNK = 128
RET_ROPE_BASE = 10000.0
SWA_HEADS = 8
SWA_KV_HEADS = 2
SWA_HEAD_DIM = 64
SWA_GROUP = SWA_HEADS // SWA_KV_HEADS
WINDOW = 128
SWA_BLOCK = WINDOW
NUM_BUCKETS = 32
MAX_DISTANCE = 128
RET_QK = RET_HEADS * RET_DK
RET_V = RET_HEADS * RET_DV
SWA_Q = SWA_HEADS * SWA_HEAD_DIM
SWA_KV = SWA_KV_HEADS * SWA_HEAD_DIM
D_MIX = RET_V + SWA_Q
SPLIT_SIZES = (RET_QK, RET_QK, RET_V, RET_V, SWA_Q, SWA_KV, SWA_KV, SWA_Q)
D_IN = sum(SPLIT_SIZES)
NORM_EPS = 1e-6
GN_EPS = 1e-5
NEG_INF = -1e30

kernel_name = "hybrid_retention_swa_sink_layer"


def rms_norm(x, w, eps=NORM_EPS):
    xf = x.astype(jnp.float32)
    y = xf * lax.rsqrt(jnp.mean(xf * xf, axis=-1, keepdims=True) + eps)
    return y * w.astype(jnp.float32)


def rotary(t, base):
    S, d = t.shape[1], t.shape[-1]
    half = d // 2
    inv_freq = base ** (-jnp.arange(half, dtype=jnp.float32) / half)
    ang = jnp.arange(S, dtype=jnp.float32)[:, None] * inv_freq[None, :]
    cos = jnp.cos(ang)[None, :, None, :]
    sin = jnp.sin(ang)[None, :, None, :]
    t1, t2 = t[..., :half], t[..., half:]
    return jnp.concatenate([t1 * cos - t2 * sin, t1 * sin + t2 * cos], axis=-1)


def retention(q, k, v):
    B, S = q.shape[0], q.shape[1]
    N, C, H = S // RET_CHUNK, RET_CHUNK, RET_HEADS
    k = k * (RET_DK ** -0.5)

    def chunks(t):
        return t.reshape(B, N, C, H, t.shape[-1]).transpose(0, 3, 1, 2, 4)

    qc, kc, vc = chunks(q), chunks(k), chunks(v)
    gamma = 1.0 - jnp.exp2(-5.0 - jnp.arange(H, dtype=jnp.float32))
    log_g = jnp.log(gamma)
    i = jnp.arange(C, dtype=jnp.float32)
    diff = i[:, None] - i[None, :]
    decay = jnp.where(diff >= 0, jnp.exp(log_g[:, None, None] * jnp.maximum(diff, 0.0)), 0.0)
    scores = jnp.einsum('bhnid,bhnjd->bhnij', qc, kc) * decay[None, :, None]
    intra = jnp.einsum('bhnij,bhnje->bhnie', scores, vc)
    zeta = jnp.exp(log_g[:, None] * (C - 1.0 - i))
    kv = jnp.einsum('bhnjd,bhnje->nbhde', kc * zeta[None, :, None, :, None], vc)
    chunk_decay = jnp.exp(log_g * C)[None, :, None, None]

    def step(state, kv_n):
        return chunk_decay * state + kv_n, state

    _, prev = lax.scan(step, jnp.zeros((B, H, RET_DK, RET_DV), jnp.float32), kv)
    xi = jnp.exp(log_g[:, None] * (i + 1.0))
    cross = jnp.einsum('bhnid,nbhde->bhnie', qc * xi[None, :, None, :, None], prev)
    return (intra + cross).transpose(0, 2, 3, 1, 4).reshape(B, S, H, RET_DV)


def t5_bucket(n):
    max_exact = NUM_BUCKETS // 2
    nf = jnp.maximum(n, 1).astype(jnp.float32)
    large = max_exact + (jnp.log(nf / max_exact) / math.log(MAX_DISTANCE / max_exact)
                         * (NUM_BUCKETS - max_exact)).astype(jnp.int32)
    large = jnp.minimum(large, NUM_BUCKETS - 1)
    return jnp.where(n < max_exact, n, large)


def sliding_window_attention(q, k, v, q_norm_w, k_norm_w, sinks, rel_bias):
    B, S = q.shape[0], q.shape[1]
    N, C = S // SWA_BLOCK, SWA_BLOCK
    q = rms_norm(q, q_norm_w)
    k = rms_norm(k, k_norm_w)
    qb = q.reshape(B, N, C, SWA_KV_HEADS, SWA_GROUP, SWA_HEAD_DIM)

    def band(t):
        tb = t.reshape(B, N, C, SWA_KV_HEADS, SWA_HEAD_DIM)
        prev = jnp.pad(tb, ((0, 0), (1, 0), (0, 0), (0, 0), (0, 0)))[:, :-1]
        return jnp.concatenate([prev, tb], axis=2)

    kband, vband = band(k), band(v)
    logits = jnp.einsum('bnqhgd,bnkhd->bhgnqk', qb, kband) * (SWA_HEAD_DIM ** -0.5)
    qi = jnp.arange(C)[:, None]
    kj = jnp.arange(2 * C)[None, :]
    dist = qi + C - kj
    bucket = t5_bucket(jnp.maximum(dist, 0))
    bias = rel_bias[bucket].astype(jnp.float32).transpose(2, 0, 1)
    bias = bias.reshape(SWA_KV_HEADS, SWA_GROUP, 1, C, 2 * C)
    key_pos = jnp.arange(N)[:, None, None] * C - C + kj[None]
    mask = (dist[None] >= 0) & (dist[None] < WINDOW) & (key_pos >= 0)
    logits = jnp.where(mask, logits + bias, NEG_INF)
    sink = jnp.broadcast_to(sinks.astype(jnp.float32).reshape(SWA_KV_HEADS, SWA_GROUP, 1, 1, 1),
                            logits.shape[:-1] + (1,))
    probs = jax.nn.softmax(jnp.concatenate([logits, sink], axis=-1), axis=-1)[..., :-1]
    out = jnp.einsum('bhgnqk,bnkhd->bnqhgd', probs, vband)
    return out.reshape(B, S, SWA_Q)


def _fwd_setup_inputs(seed: int = 0) -> dict:
    key = jax.random.key(seed)
    ks = jax.random.split(key, 9)
    f32 = jnp.float32
    x = jax.random.normal(ks[0], (BATCH, SEQ, D_MODEL), f32)
    norm_w = 1.0 + 0.02 * jax.random.normal(ks[1], (D_MODEL,), f32)
    w_in = jax.random.normal(ks[2], (D_MODEL, D_IN), f32) * D_MODEL ** -0.5
    ret_norm_w = 1.0 + 0.02 * jax.random.normal(ks[3], (RET_V,), f32)
    q_norm_w = 1.0 + 0.02 * jax.random.normal(ks[4], (SWA_HEAD_DIM,), f32)
    k_norm_w = 1.0 + 0.02 * jax.random.normal(ks[5], (SWA_HEAD_DIM,), f32)
    sinks = 0.5 * jax.random.normal(ks[6], (SWA_HEADS,), f32)
    rel_bias = 0.1 * jax.random.normal(ks[7], (NUM_BUCKETS, SWA_HEADS), f32)
    w_out = jax.random.normal(ks[8], (D_MIX, D_MODEL), f32) * D_MIX ** -0.5
    return {"x": x, "norm_w": norm_w, "w_in": w_in, "ret_norm_w": ret_norm_w,
            "q_norm_w": q_norm_w, "k_norm_w": k_norm_w, "sinks": sinks,
            "rel_bias": rel_bias, "w_out": w_out}


def _fwd_reference(x, norm_w, w_in, ret_norm_w, q_norm_w, k_norm_w, sinks, rel_bias, w_out):
    B, S = x.shape[0], x.shape[1]
    offsets = []
    acc = 0
    for sz in SPLIT_SIZES[:-1]:
        acc += sz
        offsets.append(acc)
    for _ in range(DEPTH):
        h = rms_norm(x, norm_w).astype(x.dtype)
        proj = jnp.einsum('bsd,de->bse', h, w_in).astype(jnp.float32)
        rq, rk, rv, rg, sq, sk, sv, sg = jnp.split(proj, offsets, axis=-1)
        rq = rotary(rq.reshape(B, S, RET_HEADS, RET_DK), RET_ROPE_BASE)
        rk = rotary(rk.reshape(B, S, RET_HEADS, RET_DK), RET_ROPE_BASE)
        ro = retention(rq, rk, rv.reshape(B, S, RET_HEADS, RET_DV))
        mu = jnp.mean(ro, axis=-1, keepdims=True)
        var = jnp.mean(jnp.square(ro - mu), axis=-1, keepdims=True)
        ro = ((ro - mu) * lax.rsqrt(var + GN_EPS)).reshape(B, S, RET_V) * ret_norm_w.astype(jnp.float32)
        ro = ro * jax.nn.silu(rg)
        so = sliding_window_attention(
            sq.reshape(B, S, SWA_HEADS, SWA_HEAD_DIM),
            sk.reshape(B, S, SWA_KV_HEADS, SWA_HEAD_DIM),
            sv.reshape(B, S, SWA_KV_HEADS, SWA_HEAD_DIM),
            q_norm_w, k_norm_w, sinks, rel_bias)
        so = so * jax.nn.silu(sg)
        mixed = jnp.concatenate([ro, so], axis=-1).astype(x.dtype)
        x = x + jnp.einsum('bse,ed->bsd', mixed, w_out)
    return x


import jax as _jax
import jax.numpy as _jnp

TWIN_FORMAT = 'train_step'
FWD_PARAMS = ['x', 'norm_w', 'w_in', 'ret_norm_w', 'q_norm_w', 'k_norm_w', 'sinks', 'rel_bias', 'w_out']
TWIN_WEIGHTS = ['norm_w', 'w_in', 'ret_norm_w', 'q_norm_w', 'k_norm_w', 'sinks', 'rel_bias', 'w_out']
TWIN_DIFF_INPUT = 'x'
TWIN_INPUTS = ['x', 'norm_w', 'w_in', 'ret_norm_w', 'q_norm_w', 'k_norm_w', 'sinks', 'rel_bias', 'w_out', 'loss_target', 'm_norm_w', 'm_w_in', 'm_ret_norm_w', 'm_q_norm_w', 'm_k_norm_w', 'm_sinks', 'm_rel_bias', 'm_w_out', 'v_norm_w', 'v_w_in', 'v_ret_norm_w', 'v_q_norm_w', 'v_k_norm_w', 'v_sinks', 'v_rel_bias', 'v_w_out']
TWIN_OUTPUTS = ['loss', 'grad_x', 'grad_norm_w', 'grad_w_in', 'grad_ret_norm_w', 'grad_q_norm_w', 'grad_k_norm_w', 'grad_sinks', 'grad_rel_bias', 'grad_w_out', 'delta_norm_w', 'delta_w_in', 'delta_ret_norm_w', 'delta_q_norm_w', 'delta_k_norm_w', 'delta_sinks', 'delta_rel_bias', 'delta_w_out', 'new_m_norm_w', 'new_m_w_in', 'new_m_ret_norm_w', 'new_m_q_norm_w', 'new_m_k_norm_w', 'new_m_sinks', 'new_m_rel_bias', 'new_m_w_out', 'new_v_norm_w', 'new_v_w_in', 'new_v_ret_norm_w', 'new_v_q_norm_w', 'new_v_k_norm_w', 'new_v_sinks', 'new_v_rel_bias', 'new_v_w_out']
TWIN_LEAF_KINDS = {'loss': 'loss', 'grad_x': 'grad_x', 'grad_norm_w': 'grad_w', 'grad_w_in': 'grad_w', 'grad_ret_norm_w': 'grad_w', 'grad_q_norm_w': 'grad_w', 'grad_k_norm_w': 'grad_w', 'grad_sinks': 'grad_w', 'grad_rel_bias': 'grad_w', 'grad_w_out': 'grad_w', 'delta_norm_w': 'delta_w', 'delta_w_in': 'delta_w', 'delta_ret_norm_w': 'delta_w', 'delta_q_norm_w': 'delta_w', 'delta_k_norm_w': 'delta_w', 'delta_sinks': 'delta_w', 'delta_rel_bias': 'delta_w', 'delta_w_out': 'delta_w', 'new_m_norm_w': 'new_m', 'new_m_w_in': 'new_m', 'new_m_ret_norm_w': 'new_m', 'new_m_q_norm_w': 'new_m', 'new_m_k_norm_w': 'new_m', 'new_m_sinks': 'new_m', 'new_m_rel_bias': 'new_m', 'new_m_w_out': 'new_m', 'new_v_norm_w': 'new_v', 'new_v_w_in': 'new_v', 'new_v_ret_norm_w': 'new_v', 'new_v_q_norm_w': 'new_v', 'new_v_k_norm_w': 'new_v', 'new_v_sinks': 'new_v', 'new_v_rel_bias': 'new_v', 'new_v_w_out': 'new_v'}


def _forward(args):
    return _fwd_reference(*[args[k] for k in FWD_PARAMS])


def _output_shape():
    out = _jax.eval_shape(lambda: _forward(_fwd_setup_inputs(0)))
    return out.shape, out.dtype

N_MICROBATCH = 1
ADAM_LR = 0.001
ADAM_B1 = 0.9
ADAM_B2 = 0.999
ADAM_EPS = 1e-08
ADAM_WD = 0.01
ADAM_STEP = 10
PER_EXAMPLE_BATCH_AXIS = {'x': 0, 'loss_target': 0}
SHARED_INPUTS = []
_WEIGHT_DTYPES = {'norm_w': _jnp.float32, 'w_in': _jnp.float32, 'ret_norm_w': _jnp.float32, 'q_norm_w': _jnp.float32, 'k_norm_w': _jnp.float32, 'sinks': _jnp.float32, 'rel_bias': _jnp.float32, 'w_out': _jnp.float32}
MOMENT_SCALE = {'norm_w': 3.380832e+00, 'w_in': 1.514681e-01, 'ret_norm_w': 5.725496e+00, 'q_norm_w': 6.146705e-01, 'k_norm_w': 6.106975e-01, 'sinks': 1.955866e-01, 'rel_bias': 3.376804e-02, 'w_out': 1.206191e-01}


def _to_microbatches(a, axis):
    t = _jnp.moveaxis(a, axis, 0)
    t = t.reshape((N_MICROBATCH, t.shape[0] // N_MICROBATCH) + t.shape[1:])
    return _jnp.moveaxis(t, 1, axis + 1)


def setup_inputs(seed: int = 0) -> dict:
    inp = _fwd_setup_inputs(seed)
    key = _jax.random.fold_in(_jax.random.key(seed), 7919)
    shape, _ = _output_shape()
    out = dict(inp)
    out["loss_target"] = _jax.random.normal(_jax.random.fold_in(key, 0), shape, _jnp.float32)
    for i, name in enumerate(TWIN_WEIGHTS):
        w = inp[name].astype(_jnp.float32)
        if MOMENT_SCALE is None:
            s = _jnp.sqrt(_jnp.mean(_jnp.square(w)) + 1e-30)
        else:
            s = MOMENT_SCALE[name]
        km, kv = _jax.random.split(_jax.random.fold_in(key, i + 1))
        out[name] = w
        out["m_" + name] = s * _jax.random.normal(km, w.shape, _jnp.float32)
        out["v_" + name] = (s * s) * _jax.random.uniform(kv, w.shape, _jnp.float32, 0.5, 1.5)
    if N_MICROBATCH > 1:
        for name, axis in PER_EXAMPLE_BATCH_AXIS.items():
            out[name] = _to_microbatches(out[name], axis)
    return {'x': out['x'], 'norm_w': out['norm_w'], 'w_in': out['w_in'], 'ret_norm_w': out['ret_norm_w'], 'q_norm_w': out['q_norm_w'], 'k_norm_w': out['k_norm_w'], 'sinks': out['sinks'], 'rel_bias': out['rel_bias'], 'w_out': out['w_out'], 'loss_target': out['loss_target'], 'm_norm_w': out['m_norm_w'], 'm_w_in': out['m_w_in'], 'm_ret_norm_w': out['m_ret_norm_w'], 'm_q_norm_w': out['m_q_norm_w'], 'm_k_norm_w': out['m_k_norm_w'], 'm_sinks': out['m_sinks'], 'm_rel_bias': out['m_rel_bias'], 'm_w_out': out['m_w_out'], 'v_norm_w': out['v_norm_w'], 'v_w_in': out['v_w_in'], 'v_ret_norm_w': out['v_ret_norm_w'], 'v_q_norm_w': out['v_q_norm_w'], 'v_k_norm_w': out['v_k_norm_w'], 'v_sinks': out['v_sinks'], 'v_rel_bias': out['v_rel_bias'], 'v_w_out': out['v_w_out']}


def _loss(weights, diff, rest, loss_target):
    with _jax.named_scope("forward"):
        args = {**rest, TWIN_DIFF_INPUT: diff, **{k: w.astype(_WEIGHT_DTYPES[k]) for k, w in weights.items()}}
        y = _forward(args)
    with _jax.named_scope("loss_head"):
        err = _jnp.square(y.astype(_jnp.float32) - loss_target)
        return 0.5 * _jnp.sum(_jnp.mean(err, axis=-1)) if err.ndim else 0.5 * err


def _adamw(w, g, m, v):
    m = ADAM_B1 * m + (1.0 - ADAM_B1) * g
    v = ADAM_B2 * v + (1.0 - ADAM_B2) * _jnp.square(g)
    m_hat = m / (1.0 - ADAM_B1 ** ADAM_STEP)
    v_hat = v / (1.0 - ADAM_B2 ** ADAM_STEP)
    delta = -ADAM_LR * (m_hat / (_jnp.sqrt(v_hat) + ADAM_EPS) + ADAM_WD * w)
    return delta, m, v


def reference(x, norm_w, w_in, ret_norm_w, q_norm_w, k_norm_w, sinks, rel_bias, w_out, loss_target, m_norm_w, m_w_in, m_ret_norm_w, m_q_norm_w, m_k_norm_w, m_sinks, m_rel_bias, m_w_out, v_norm_w, v_w_in, v_ret_norm_w, v_q_norm_w, v_k_norm_w, v_sinks, v_rel_bias, v_w_out):
    given = dict(x=x, norm_w=norm_w, w_in=w_in, ret_norm_w=ret_norm_w, q_norm_w=q_norm_w, k_norm_w=k_norm_w, sinks=sinks, rel_bias=rel_bias, w_out=w_out, loss_target=loss_target, m_norm_w=m_norm_w, m_w_in=m_w_in, m_ret_norm_w=m_ret_norm_w, m_q_norm_w=m_q_norm_w, m_k_norm_w=m_k_norm_w, m_sinks=m_sinks, m_rel_bias=m_rel_bias, m_w_out=m_w_out, v_norm_w=v_norm_w, v_w_in=v_w_in, v_ret_norm_w=v_ret_norm_w, v_q_norm_w=v_q_norm_w, v_k_norm_w=v_k_norm_w, v_sinks=v_sinks, v_rel_bias=v_rel_bias, v_w_out=v_w_out)
    weights = {n: given[n] for n in TWIN_WEIGHTS}
    shared = {n: given[n] for n in SHARED_INPUTS}
    per_example = {n: given[n] for n in ['x']}
    grad_fn = _jax.value_and_grad(_loss, argnums=(0, 1))

    def one_microbatch(ex, loss_target):
        ex = dict(ex)
        diff = ex.pop(TWIN_DIFF_INPUT)
        return grad_fn(weights, diff, {**shared, **ex}, loss_target)

    if N_MICROBATCH == 1:
        loss, (grad_w, grad_x) = one_microbatch(per_example, given["loss_target"])
    else:
        def body(carry, xs):
            loss_sum, grad_sum = carry
            l_k, (gw_k, gx_k) = one_microbatch(xs[0], xs[1])
            with _jax.named_scope("update"):
                return (loss_sum + l_k, _jax.tree.map(_jnp.add, grad_sum, gw_k)), gx_k

        init = (_jnp.zeros((), _jnp.float32), _jax.tree.map(_jnp.zeros_like, weights))
        (loss, grad_w), grad_x = _jax.lax.scan(body, init, (per_example, given["loss_target"]))
    with _jax.named_scope("update"):
        delta_w, new_m, new_v = {}, {}, {}
        for n in TWIN_WEIGHTS:
            delta_w[n], new_m[n], new_v[n] = _adamw(weights[n], grad_w[n], given["m_" + n], given["v_" + n])
    return (loss, grad_x, *[grad_w[n] for n in TWIN_WEIGHTS], *[delta_w[n] for n in TWIN_WEIGHTS],
            *[new_m[n] for n in TWIN_WEIGHTS], *[new_v[n] for n in TWIN_WEIGHTS])
```

```python
import functools

import numpy as np
import jax
import jax.numpy as jnp
from jax import lax
from jax.experimental import pallas as pl
from jax.experimental.pallas import tpu as pltpu

F32 = jnp.float32
BF16 = jnp.bfloat16
MESH = pl.DeviceIdType.MESH

T = 2048
D = 1024
D_IN = 2816
N_DEV = 8
WIN_BLK = D_IN // N_DEV
WIN_PAD = 384
WOUT_BLK = D // N_DEV
CH = 128
N_CH = T // CH
RET_H, RET_DK, RET_DV = 4, 64, 128
SWA_H, SWA_KV, SWA_D, SWA_G = 8, 2, 64, 4
N_BUCKETS = 32
NORM_EPS = 1e-6
GN_EPS = 1e-5
NEG_INF = -1e30
PIECES = (256, 256, 512, 512, 512, 128, 128, 512)
OFFS = tuple(int(v) for v in np.cumsum((0,) + PIECES))
TM = 256
SMALL_ROWS = 24

ADAM_LR, ADAM_B1, ADAM_B2, ADAM_EPS, ADAM_WD, ADAM_STEP = 0.001, 0.9, 0.999, 1e-08, 0.01, 10

VMEM_LIMIT = 56 * 1024 * 1024


def _cparams(n_grid=0, vmem=VMEM_LIMIT):
    sem = ("arbitrary",) * n_grid if n_grid else None
    return pltpu.CompilerParams(dimension_semantics=sem, vmem_limit_bytes=vmem)


def _dot(a, b):
    return jnp.dot(a, b, preferred_element_type=F32)


def _dot_nt(a, b):
    return lax.dot_general(a, b, (((1,), (1,)), ((), ())), preferred_element_type=F32)


def _bf(a):
    return a.astype(BF16)


def _sigmoid(x):
    return 1.0 / (1.0 + jnp.exp(-x))


def _adamw(w, g, m, v):
    m = ADAM_B1 * m + (1.0 - ADAM_B1) * g
    v = ADAM_B2 * v + (1.0 - ADAM_B2) * (g * g)
    m_hat = m / (1.0 - ADAM_B1 ** ADAM_STEP)
    v_hat = v / (1.0 - ADAM_B2 ** ADAM_STEP)
    delta = -ADAM_LR * (m_hat / (jnp.sqrt(v_hat) + ADAM_EPS) + ADAM_WD * w)
    return delta, m, v


def _rotary_tables():
    half = RET_DK // 2
    inv_freq = np.float32(10000.0) ** (-np.arange(half, dtype=np.float32) / np.float32(half))
    ang = np.arange(T, dtype=np.float32)[:, None] * inv_freq[None, :]
    cos, sin = np.cos(ang).astype(np.float32), np.sin(ang).astype(np.float32)
    cos64 = np.concatenate([cos, cos], axis=-1)
    sin64 = np.concatenate([-sin, sin], axis=-1)
    return np.tile(cos64, (1, RET_H)), np.tile(sin64, (1, RET_H))


def _retention_tables():
    gamma = (1.0 - np.exp2(-5.0 - np.arange(RET_H, dtype=np.float32))).astype(np.float32)
    log_g = np.log(gamma).astype(np.float32)
    i = np.arange(CH, dtype=np.float32)
    diff = i[:, None] - i[None, :]
    decay = np.where(diff >= 0, np.exp(log_g[:, None, None] * np.maximum(diff, 0.0)), 0.0).astype(np.float32)
    zeta = np.exp(log_g[:, None] * (CH - 1.0 - i)).astype(np.float32)
    xi = np.exp(log_g[:, None] * (i + 1.0)).astype(np.float32)
    chunk_decay = [float(v) for v in np.exp(log_g * np.float32(CH)).astype(np.float32)]
    zeta_b = np.ascontiguousarray(np.broadcast_to(zeta[:, :, None], (RET_H, CH, RET_DK)))
    xi_b = np.ascontiguousarray(np.broadcast_to(xi[:, :, None], (RET_H, CH, RET_DK)))
    return decay, zeta_b, xi_b, chunk_decay


def _bucket_table():
    qi = np.arange(CH)[:, None]
    kj = np.arange(2 * CH)[None, :]
    dist = qi + CH - kj
    n = np.maximum(dist, 0)
    max_exact = N_BUCKETS // 2
    nf = np.maximum(n, 1).astype(np.float32)
    large = max_exact + (np.log(nf / np.float32(max_exact)) / np.float32(np.log(CH / max_exact))
                         * np.float32(N_BUCKETS - max_exact)).astype(np.int32)
    large = np.minimum(large, N_BUCKETS - 1)
    bucket = np.where(n < max_exact, n, large)
    return np.where((dist >= 0) & (dist < CH), bucket, -1).astype(np.int32)


def _swap_halves(t):
    lane = lax.broadcasted_iota(jnp.int32, t.shape, 1)
    n = t.shape[1]
    return jnp.where((lane % RET_DK) < RET_DK // 2, pltpu.roll(t, n - RET_DK // 2, 1), pltpu.roll(t, RET_DK // 2, 1))


def _mesh_pos():
    return lax.axis_index("x"), lax.axis_index("y"), lax.axis_index("c")


def _gather_weights(w_in_blk, w_out_blk):
    def body(win_ref, wout_ref, wt_ref, wo_ref, pad_ref, send_sems, recv_sems):
        x, y, c = _mesh_pos()
        me = 4 * x + 2 * y + c
        sibling = (x, y, 1 - c)
        chips = [(1 - x, y), (x, 1 - y), (1 - x, 1 - y)]

        pad_ref[...] = jnp.zeros_like(pad_ref)
        pad_ref[:, 0:WIN_BLK] = win_ref[...]
        wt_ref[me] = _bf(pad_ref[...].T[0:WIN_BLK, :])
        wo_ref[me] = _bf(wout_ref[...])

        def blk(px, py, pc):
            return 4 * px + 2 * py + pc

        def copies(k, b, to):
            return [pltpu.make_async_remote_copy(src_ref=ref.at[b], dst_ref=ref.at[b],
                                                 send_sem=send_sems.at[a, k], recv_sem=recv_sems.at[a, k],
                                                 device_id=to, device_id_type=MESH)
                    for a, ref in enumerate((wt_ref, wo_ref))]

        first = copies(0, me, sibling)
        for j, chip in enumerate(chips):
            first += copies(1 + j, me, (*chip, c))
        for cp in first:
            cp.start()
        passed = []
        for j, chip in enumerate(chips):
            for cp in copies(1 + j, blk(*chip, c), (x, y, c)):
                cp.wait_recv()
            fwd = copies(4 + j, blk(*chip, c), sibling)
            for cp in fwd:
                cp.start()
            passed += fwd
        for cp in copies(0, blk(x, y, 1 - c), (x, y, c)):
            cp.wait_recv()
        for j, chip in enumerate(chips):
            for cp in copies(4 + j, blk(*chip, 1 - c), (x, y, c)):
                cp.wait_recv()
        for cp in first + passed:
            cp.wait_send()

    vm = pl.BlockSpec(memory_space=pltpu.VMEM)
    wt, wo = pl.pallas_call(
        body, name="gather_weights",
        out_shape=(jax.ShapeDtypeStruct((N_DEV, WIN_BLK, D), BF16), jax.ShapeDtypeStruct((N_DEV, WOUT_BLK, D), BF16)),
        in_specs=[vm, vm], out_specs=(vm, vm),
        scratch_shapes=[pltpu.VMEM((D, WIN_PAD), F32), pltpu.SemaphoreType.DMA((2, 7)), pltpu.SemaphoreType.DMA((2, 7))],
        compiler_params=_cparams(vmem=32 * 1024 * 1024),
    )(w_in_blk, w_out_blk)
    return wt.reshape(D_IN, D), wo.reshape(D, D)


def _in_proj(x, norm_w, wt):
    def body(x_ref, nw_ref, w_ref, *outs):
        piece_refs, rstd_ref = outs[:8], outs[8]
        xv = x_ref[...]
        r = lax.rsqrt(jnp.mean(xv * xv, axis=-1, keepdims=True) + NORM_EPS)
        h = _bf(xv * r * nw_ref[...])
        for i, ref in enumerate(piece_refs):
            ref[...] = _dot_nt(h, w_ref[OFFS[i]:OFFS[i + 1], :])
        rstd_ref[...] = r

    row = lambda w: pl.BlockSpec((TM, w), lambda i: (i, 0))
    return pl.pallas_call(
        body, name="in_proj", grid=(T // TM,),
        in_specs=[row(D), pl.BlockSpec((1, D), lambda i: (0, 0)), pl.BlockSpec((D_IN, D), lambda i: (0, 0))],
        out_specs=tuple(row(w) for w in PIECES) + (row(1),),
        out_shape=tuple(jax.ShapeDtypeStruct((T, w), F32) for w in PIECES) + (jax.ShapeDtypeStruct((T, 1), F32),),
        compiler_params=_cparams(1),
    )(x, norm_w, wt)


def _group_norm(o):
    mu = jnp.mean(o, axis=-1, keepdims=True)
    var = jnp.mean((o - mu) * (o - mu), axis=-1, keepdims=True)
    rstd = lax.rsqrt(var + GN_EPS)
    return (o - mu) * rstd, rstd


def _retention_fwd(rq, rk, rv, rg, ret_norm_w, cos_t, sin_t, decay, zeta_b, xi_b, chunk_decay):
    def body(rq_ref, rk_ref, rv_ref, rg_ref, w_ref, cos_ref, sin_ref, decay_ref, zeta_ref, xi_ref,
             ro_ref, mix_ref, st_ref, state):
        n = pl.program_id(0)

        @pl.when(n == 0)
        def _():
            state[...] = jnp.zeros_like(state)

        cos, sin = cos_ref[...], sin_ref[...]
        q = rq_ref[...]
        q = q * cos + _swap_halves(q) * sin
        k = rk_ref[...]
        k = (k * cos + _swap_halves(k) * sin) * (RET_DK ** -0.5)
        for h in range(RET_H):
            qh = q[:, h * RET_DK:(h + 1) * RET_DK]
            kh = k[:, h * RET_DK:(h + 1) * RET_DK]
            vh = rv_ref[:, h * RET_DV:(h + 1) * RET_DV]
            s_t = state[h]
            st_ref[0, h] = s_t
            scores = _dot_nt(_bf(qh), _bf(kh)) * decay_ref[h]
            o = _dot(_bf(scores), _bf(vh)) + _dot_nt(_bf(qh * xi_ref[h]), _bf(s_t))
            state[h] = chunk_decay[h] * s_t + _dot(_bf(vh.T), _bf(kh * zeta_ref[h]))
            rn, _ = _group_norm(o)
            g = rg_ref[:, h * RET_DV:(h + 1) * RET_DV]
            ro_ref[:, h * RET_DV:(h + 1) * RET_DV] = o
            mix_ref[:, h * RET_DV:(h + 1) * RET_DV] = rn * w_ref[:, h * RET_DV:(h + 1) * RET_DV] * (g * _sigmoid(g))

    row = lambda w: pl.BlockSpec((CH, w), lambda n: (n, 0))
    const = lambda shape: pl.BlockSpec(shape, lambda n: (0,) * len(shape))
    return pl.pallas_call(
        body, name="retention_fwd", grid=(N_CH,),
        in_specs=[row(256), row(256), row(512), row(512), const((1, 512)), row(256), row(256),
                  const((RET_H, CH, CH)), const((RET_H, CH, RET_DK)), const((RET_H, CH, RET_DK))],
        out_specs=(row(512), row(512), pl.BlockSpec((1, RET_H, RET_DV, RET_DK), lambda n: (n, 0, 0, 0))),
        out_shape=(jax.ShapeDtypeStruct((T, 512), F32), jax.ShapeDtypeStruct((T, 512), F32),
                   jax.ShapeDtypeStruct((N_CH, RET_H, RET_DV, RET_DK), F32)),
        scratch_shapes=[pltpu.VMEM((RET_H, RET_DV, RET_DK), F32)],
        compiler_params=_cparams(1),
    )(rq, rk, rv, rg, ret_norm_w, cos_t, sin_t, decay, zeta_b, xi_b)


def _retention_bwd(rq, rk, rv, rg, ro, states, dmix, ret_norm_w, cos_t, sin_t, decay, zeta_b, xi_b, chunk_decay):
    def body(rq_ref, rk_ref, rv_ref, rg_ref, ro_ref, st_ref, dm_ref, w_ref, cos_ref, sin_ref, decay_ref, zeta_ref,
             xi_ref, dq_ref, dk_ref, dv_ref, dg_ref, dw_ref, gstate):
        i = pl.program_id(0)

        @pl.when(i == 0)
        def _():
            gstate[...] = jnp.zeros_like(gstate)
            dw_ref[...] = jnp.zeros_like(dw_ref)

        cos, sin = cos_ref[...], sin_ref[...]
        q = rq_ref[...]
        q = q * cos + _swap_halves(q) * sin
        k = rk_ref[...]
        k = (k * cos + _swap_halves(k) * sin) * (RET_DK ** -0.5)
        dqs, dks = [], []
        for h in range(RET_H):
            cols = slice(h * RET_DV, (h + 1) * RET_DV)
            qh = q[:, h * RET_DK:(h + 1) * RET_DK]
            kh = k[:, h * RET_DK:(h + 1) * RET_DK]
            vh = rv_ref[:, cols]
            g = rg_ref[:, cols]
            w = w_ref[:, cols]
            dm = dm_ref[:, cols]
            rn, rstd = _group_norm(ro_ref[:, cols])
            sig = _sigmoid(g)
            silu = g * sig
            dg_ref[:, cols] = dm * rn * w * (sig * (1.0 + g * (1.0 - sig)))
            dw_ref[:, cols] += jnp.sum(dm * silu * rn, axis=0, keepdims=True)
            drn = dm * silu * w
            do = rstd * (drn - jnp.mean(drn, axis=-1, keepdims=True)
                         - rn * jnp.mean(drn * rn, axis=-1, keepdims=True))
            do_b = _bf(do)
            s_t = st_ref[0, h]
            g_t = gstate[h]
            dec = decay_ref[h]
            scores = _dot_nt(_bf(qh), _bf(kh)) * dec
            dscores = _dot_nt(do_b, _bf(vh)) * dec
            qx = qh * xi_ref[h]
            kz = kh * zeta_ref[h]
            dqh = _dot(_bf(dscores), _bf(kh)) + _dot(do_b, _bf(s_t)) * xi_ref[h]
            dkh = _dot(_bf(dscores.T), _bf(qh)) + _dot(_bf(vh), _bf(g_t)) * zeta_ref[h]
            dv_ref[:, cols] = _dot(_bf(scores.T), do_b) + _dot_nt(_bf(kz), _bf(g_t))
            gstate[h] = chunk_decay[h] * g_t + _dot(_bf(do.T), _bf(qx))
            dqs.append(dqh)
            dks.append(dkh * (RET_DK ** -0.5))
        dq = jnp.concatenate(dqs, axis=1)
        dk = jnp.concatenate(dks, axis=1)
        dq_ref[...] = dq * cos + _swap_halves(dq * sin)
        dk_ref[...] = dk * cos + _swap_halves(dk * sin)

    row = lambda w: pl.BlockSpec((CH, w), lambda i: (N_CH - 1 - i, 0))
    const = lambda shape: pl.BlockSpec(shape, lambda i: (0,) * len(shape))
    return pl.pallas_call(
        body, name="retention_bwd", grid=(N_CH,),
        in_specs=[row(256), row(256), row(512), row(512), row(512),
                  pl.BlockSpec((1, RET_H, RET_DV, RET_DK), lambda i: (N_CH - 1 - i, 0, 0, 0)), row(512),
                  const((1, 512)), row(256), row(256),
                  const((RET_H, CH, CH)), const((RET_H, CH, RET_DK)), const((RET_H, CH, RET_DK))],
        out_specs=(row(256), row(256), row(512), row(512), const((1, 512))),
        out_shape=(jax.ShapeDtypeStruct((T, 256), F32), jax.ShapeDtypeStruct((T, 256), F32),
                   jax.ShapeDtypeStruct((T, 512), F32), jax.ShapeDtypeStruct((T, 512), F32),
                   jax.ShapeDtypeStruct((1, 512), F32)),
        scratch_shapes=[pltpu.VMEM((RET_H, RET_DV, RET_DK), F32)],
        compiler_params=_cparams(1),
    )(rq, rk, rv, rg, ro, states, dmix, ret_norm_w, cos_t, sin_t, decay, zeta_b, xi_b)


def _build_bias(bucket_ref, relb_ref, bias_ref):
    bt = bucket_ref[...]
    for hq in range(SWA_H):
        b = jnp.zeros(bt.shape, F32)
        for bk in range(N_BUCKETS):
            b = jnp.where(bt == bk, relb_ref[bk * SWA_H + hq], b)
        bias_ref[hq] = b


def _rms(t, w):
    r = lax.rsqrt(jnp.mean(t * t, axis=-1, keepdims=True) + NORM_EPS)
    return t * r * w, r


def _rms_bwd(dn, t, r, w):
    u = dn * w
    return r * u - t * (r * r * r) * jnp.mean(u * t, axis=-1, keepdims=True), dn * t * r


def _swa_probs(qn, kn, bias, valid, sink):
    s = _dot_nt(_bf(qn), _bf(kn)) * (SWA_D ** -0.5) + bias
    s = jnp.where(valid, s, NEG_INF)
    m = jnp.maximum(jnp.max(s, axis=-1, keepdims=True), sink)
    p = jnp.exp(s - m)
    e_sink = jnp.exp(sink - m)
    den = jnp.sum(p, axis=-1, keepdims=True) + e_sink
    return p / den, e_sink / den


def _swa_valid(bucket_ref, n):
    bt = bucket_ref[...]
    col = lax.broadcasted_iota(jnp.int32, bt.shape, 1)
    return (bt >= 0) & ((n > 0) | (col >= CH))


def _swa_fwd(sq, sk, sv, sg, q_norm_w, k_norm_w, sinks, rel_bias_flat, bucket):
    def body(sq_ref, skp_ref, skc_ref, svp_ref, svc_ref, sg_ref, qw_ref, kw_ref, sinks_ref, relb_ref, bucket_ref,
             so_ref, mix_ref, bias_ref):
        n = pl.program_id(0)

        @pl.when(n == 0)
        def _():
            _build_bias(bucket_ref, relb_ref, bias_ref)

        valid = _swa_valid(bucket_ref, n)
        kb = jnp.concatenate([skp_ref[...], skc_ref[...]], axis=0)
        vb = jnp.concatenate([svp_ref[...], svc_ref[...]], axis=0)
        qw, kw = qw_ref[...], kw_ref[...]
        for kv in range(SWA_KV):
            kn, _ = _rms(kb[:, kv * SWA_D:(kv + 1) * SWA_D], kw)
            vh = _bf(vb[:, kv * SWA_D:(kv + 1) * SWA_D])
            for gi in range(SWA_G):
                hq = kv * SWA_G + gi
                cols = slice(hq * SWA_D, (hq + 1) * SWA_D)
                qn, _ = _rms(sq_ref[:, cols], qw)
                probs, _ = _swa_probs(qn, kn, bias_ref[hq], valid, sinks_ref[hq])
                o = _dot(_bf(probs), vh)
                g = sg_ref[:, cols]
                so_ref[:, cols] = o
                mix_ref[:, cols] = o * (g * _sigmoid(g))

    row = lambda w: pl.BlockSpec((CH, w), lambda n: (n, 0))
    prev = lambda w: pl.BlockSpec((CH, w), lambda n: (jnp.maximum(n - 1, 0), 0))
    const = lambda shape: pl.BlockSpec(shape, lambda n: (0,) * len(shape))
    smem = pl.BlockSpec(memory_space=pltpu.SMEM)
    return pl.pallas_call(
        body, name="swa_fwd", grid=(N_CH,),
        in_specs=[row(512), prev(128), row(128), prev(128), row(128), row(512), const((1, SWA_D)), const((1, SWA_D)),
                  smem, smem, const((CH, 2 * CH))],
        out_specs=(row(512), row(512)),
        out_shape=(jax.ShapeDtypeStruct((T, 512), F32), jax.ShapeDtypeStruct((T, 512), F32)),
        scratch_shapes=[pltpu.VMEM((SWA_H, CH, 2 * CH), F32)],
        compiler_params=_cparams(1),
    )(sq, sk, sk, sv, sv, sg, q_norm_w, k_norm_w, sinks, rel_bias_flat, bucket)


def _swa_bwd(sq, sk, sv, sg, so, dmix, q_norm_w, k_norm_w, sinks, rel_bias_flat, bucket):
    def body(sq_ref, skp_ref, skc_ref, svp_ref, svc_ref, sg_ref, so_ref, dm_ref, qw_ref, kw_ref, sinks_ref, relb_ref,
             bucket_ref, dq_ref, dg_ref, dk_ref, dv_ref, dbias_ref, dsink_ref, dqw_ref, dkw_ref, bias_ref):
        n = pl.program_id(0)

        @pl.when(n == 0)
        def _():
            _build_bias(bucket_ref, relb_ref, bias_ref)
            for ref in (dk_ref, dv_ref, dbias_ref, dsink_ref, dqw_ref, dkw_ref):
                ref[...] = jnp.zeros_like(ref)

        valid = _swa_valid(bucket_ref, n)
        kb = jnp.concatenate([skp_ref[...], skc_ref[...]], axis=0)
        vb = jnp.concatenate([svp_ref[...], svc_ref[...]], axis=0)
        qw, kw = qw_ref[...], kw_ref[...]
        lane = lax.broadcasted_iota(jnp.int32, (1, CH), 1)
        dsink = jnp.zeros((1, CH), F32)
        dqw = jnp.zeros((1, SWA_D), F32)
        dkw = jnp.zeros((1, SWA_D), F32)
        dk_band, dv_band = [], []
        for kv in range(SWA_KV):
            kraw = kb[:, kv * SWA_D:(kv + 1) * SWA_D]
            kn, rk = _rms(kraw, kw)
            kn_b = _bf(kn)
            vh = _bf(vb[:, kv * SWA_D:(kv + 1) * SWA_D])
            dkn = jnp.zeros((2 * CH, SWA_D), F32)
            dvh = jnp.zeros((2 * CH, SWA_D), F32)
            for gi in range(SWA_G):
                hq = kv * SWA_G + gi
                cols = slice(hq * SWA_D, (hq + 1) * SWA_D)
                qraw = sq_ref[:, cols]
                qn, rq = _rms(qraw, qw)
                probs, p_sink = _swa_probs(qn, kn, bias_ref[hq], valid, sinks_ref[hq])
                g = sg_ref[:, cols]
                sig = _sigmoid(g)
                dm = dm_ref[:, cols]
                dg_ref[:, cols] = dm * so_ref[:, cols] * (sig * (1.0 + g * (1.0 - sig)))
                do_b = _bf(dm * (g * sig))
                dprobs = _dot_nt(do_b, vh)
                dvh = dvh + _dot(_bf(probs.T), do_b)
                t = jnp.sum(probs * dprobs, axis=-1, keepdims=True)
                dlog = probs * (dprobs - t)
                dsink = dsink + jnp.where(lane == hq, jnp.sum(-p_sink * t, axis=0, keepdims=True), 0.0)
                dbias_ref[hq] += dlog
                ds = dlog * (SWA_D ** -0.5)
                dqn = _dot(_bf(ds), kn_b)
                dkn = dkn + _dot(_bf(ds.T), _bf(qn))
                dqraw, dqw_rows = _rms_bwd(dqn, qraw, rq, qw)
                dqw = dqw + jnp.sum(dqw_rows, axis=0, keepdims=True)
                dq_ref[:, cols] = dqraw
            dkraw, dkw_rows = _rms_bwd(dkn, kraw, rk, kw)
            dkw = dkw + jnp.sum(dkw_rows, axis=0, keepdims=True)
            dk_band.append(dkraw)
            dv_band.append(dvh)
        dkb = jnp.concatenate(dk_band, axis=1)
        dvb = jnp.concatenate(dv_band, axis=1)
        r_prev = pl.multiple_of(jnp.maximum(n - 1, 0) * CH, CH)
        r_cur = pl.multiple_of(n * CH, CH)
        dk_ref[pl.ds(r_prev, CH), :] += dkb[0:CH]
        dk_ref[pl.ds(r_cur, CH), :] += dkb[CH:2 * CH]
        dv_ref[pl.ds(r_prev, CH), :] += dvb[0:CH]
        dv_ref[pl.ds(r_cur, CH), :] += dvb[CH:2 * CH]
        dsink_ref[...] += dsink
        dqw_ref[...] += dqw
        dkw_ref[...] += dkw

    row = lambda w: pl.BlockSpec((CH, w), lambda n: (n, 0))
    prev = lambda w: pl.BlockSpec((CH, w), lambda n: (jnp.maximum(n - 1, 0), 0))
    const = lambda shape: pl.BlockSpec(shape, lambda n: (0,) * len(shape))
    smem = pl.BlockSpec(memory_space=pltpu.SMEM)
    return pl.pallas_call(
        body, name="swa_bwd", grid=(N_CH,),
        in_specs=[row(512), prev(128), row(128), prev(128), row(128), row(512), row(512), row(512),
                  const((1, SWA_D)), const((1, SWA_D)), smem, smem, const((CH, 2 * CH))],
        out_specs=(row(512), row(512), const((T, 128)), const((T, 128)), const((SWA_H, CH, 2 * CH)), const((1, CH)),
                   const((1, SWA_D)), const((1, SWA_D))),
        out_shape=(jax.ShapeDtypeStruct((T, 512), F32), jax.ShapeDtypeStruct((T, 512), F32),
                   jax.ShapeDtypeStruct((T, 128), F32), jax.ShapeDtypeStruct((T, 128), F32),
                   jax.ShapeDtypeStruct((SWA_H, CH, 2 * CH), F32), jax.ShapeDtypeStruct((1, CH), F32),
                   jax.ShapeDtypeStruct((1, SWA_D), F32), jax.ShapeDtypeStruct((1, SWA_D), F32)),
        scratch_shapes=[pltpu.VMEM((SWA_H, CH, 2 * CH), F32)],
        compiler_params=_cparams(1),
    )(sq, sk, sk, sv, sv, sg, so, dmix, q_norm_w, k_norm_w, sinks, rel_bias_flat, bucket)


def _rel_bias_grad(dbias, bucket):
    def body(dbias_ref, bucket_ref, out_ref):
        bt = bucket_ref[...]
        pos = (lax.broadcasted_iota(jnp.int32, (2, CH), 0) * CH + lax.broadcasted_iota(jnp.int32, (2, CH), 1))
        acc = jnp.zeros((2, CH), F32)
        for hq in range(SWA_H):
            d = dbias_ref[hq]
            for bk in range(N_BUCKETS):
                s = jnp.sum(jnp.sum(jnp.where(bt == bk, d, 0.0), axis=0, keepdims=True), axis=1, keepdims=True)
                acc = acc + jnp.where(pos == bk * SWA_H + hq, s, 0.0)
        out_ref[...] = acc

    vm = pl.BlockSpec(memory_space=pltpu.VMEM)
    return pl.pallas_call(body, name="rel_bias_grad", out_shape=jax.ShapeDtypeStruct((2, CH), F32),
                          in_specs=[vm, vm], out_specs=vm, compiler_params=_cparams())(dbias, bucket)


def _out_proj(mix_r, mix_s, wo, x, target):
    def body(mr_ref, ms_ref, w_ref, x_ref, t_ref, loss_ref, dy_ref, dmr_ref, dms_ref, gw_ref):
        i = pl.program_id(0)

        @pl.when(i == 0)
        def _():
            loss_ref[...] = jnp.zeros_like(loss_ref)
            gw_ref[...] = jnp.zeros_like(gw_ref)

        mixed = jnp.concatenate([mr_ref[...], ms_ref[...]], axis=1)
        w = w_ref[...]
        err = x_ref[...] + _dot(_bf(mixed), w) - t_ref[...]
        loss_ref[...] += jnp.sum(jnp.sum(err * err, axis=1, keepdims=True), axis=0, keepdims=True)
        dy = err * (1.0 / D)
        dy_ref[...] = dy
        dy_b = _bf(dy)
        dmix = _dot_nt(dy_b, w)
        dmr_ref[...] = dmix[:, 0:512]
        dms_ref[...] = dmix[:, 512:1024]
        gw_ref[...] += _dot(_bf(mixed.T), dy_b)

    row = lambda w: pl.BlockSpec((TM, w), lambda i: (i, 0))
    const = lambda shape: pl.BlockSpec(shape, lambda i: (0,) * len(shape))
    return pl.pallas_call(
        body, name="out_proj", grid=(T // TM,),
        in_specs=[row(512), row(512), const((D, D)), row(D), row(D)],
        out_specs=(const((1, 1)), row(D), row(512), row(512), const((D, D))),
        out_shape=(jax.ShapeDtypeStruct((1, 1), F32), jax.ShapeDtypeStruct((T, D), F32),
                   jax.ShapeDtypeStruct((T, 512), F32), jax.ShapeDtypeStruct((T, 512), F32),
                   jax.ShapeDtypeStruct((D, D), F32)),
        compiler_params=_cparams(1),
    )(mix_r, mix_s, wo, x, target)


def _in_proj_bwd(dpieces, x, rstd, norm_w, dy, wt):
    def body(*refs):
        piece_refs = refs[:8]
        x_ref, r_ref, nw_ref, dy_ref, w_ref, gx_ref, gw_ref, gnw_ref = refs[8:]
        i = pl.program_id(0)

        @pl.when(i == 0)
        def _():
            gw_ref[...] = jnp.zeros_like(gw_ref)
            gnw_ref[...] = jnp.zeros_like(gnw_ref)

        dp = jnp.concatenate([ref[...] for ref in piece_refs], axis=1)
        xv, r, nw = x_ref[...], r_ref[...], nw_ref[...]
        xr = xv * r
        dh = _dot(_bf(dp), w_ref[...])
        gw_ref[...] += _dot(_bf(dp.T), _bf(xr * nw))
        u = dh * nw
        gx_ref[...] = dy_ref[...] + r * u - xv * (r * r * r) * jnp.mean(u * xv, axis=-1, keepdims=True)
        gnw_ref[...] += jnp.sum(dh * xr, axis=0, keepdims=True)

    row = lambda w: pl.BlockSpec((TM, w), lambda i: (i, 0))
    const = lambda shape, **kw: pl.BlockSpec(shape, lambda i: (0,) * len(shape), **kw)
    once = dict(pipeline_mode=pl.Buffered(1))
    return pl.pallas_call(
        body, name="in_proj_bwd", grid=(T // TM,),
        in_specs=[row(w) for w in PIECES] + [row(D), row(1), const((1, D)), row(D), const((D_IN, D), **once)],
        out_specs=(row(D), const((D_IN, D), **once), const((1, D))),
        out_shape=(jax.ShapeDtypeStruct((T, D), F32), jax.ShapeDtypeStruct((D_IN, D), F32),
                   jax.ShapeDtypeStruct((1, D), F32)),
        compiler_params=_cparams(1),
    )(*dpieces, x, rstd, norm_w, dy, wt)


def _reduce_and_update(gwt, gwo, small_g, w_in, m_in, v_in, w_out, m_out, v_out, small_w, small_m, small_v):
    def body(gwt_ref, gwo_ref, sg_ref, w_in_ref, m_in_ref, v_in_ref, w_out_ref, m_out_ref, v_out_ref,
             sw_ref, sm_ref, sv_ref,
             g_in_o, d_in_o, nm_in_o, nv_in_o, g_out_o, d_out_o, nm_out_o, nv_out_o, sg_o, sd_o, snm_o, snv_o,
             own_t, own_o, rcv_t, rcv_o, snd_t, snd_o, got_t, got_o, small_all, pad_ref, tr_ref,
             loc_sems, a_send, a_recv, b_send, b_recv, s_send, s_recv):
        x, y, c = _mesh_pos()
        me = 4 * x + 2 * y + c
        sibling = (x, y, 1 - c)
        all_chips = [(0, 0), (0, 1), (1, 0), (1, 1)]
        others = [(1 - x, y), (x, 1 - y), (1 - x, 1 - y)]

        def blk(px, py, pc):
            return 4 * px + 2 * py + pc

        small_all[me] = sg_ref[...]
        small_copies = []
        for k in range(1, N_DEV):
            px, py, pc = x ^ (k >> 2), y ^ ((k >> 1) & 1), c ^ (k & 1)
            cp = pltpu.make_async_remote_copy(src_ref=sg_ref, dst_ref=small_all.at[me], send_sem=s_send.at[k - 1],
                                              recv_sem=s_recv.at[k - 1], device_id=(px, py, pc), device_id_type=MESH)
            cp.start()
            small_copies.append(cp)

        stage_a, local = [], []
        for j, (px, py) in enumerate(all_chips):
            for a, (src, dst) in enumerate(((gwt_ref, rcv_t), (gwo_ref, rcv_o))):
                cp = pltpu.make_async_remote_copy(src_ref=src.at[blk(px, py, 1 - c)], dst_ref=dst.at[j],
                                                  send_sem=a_send.at[a, j], recv_sem=a_recv.at[a, j],
                                                  device_id=sibling, device_id_type=MESH)
                cp.start()
                stage_a.append(cp)
            for a, (src, dst) in enumerate(((gwt_ref, own_t), (gwo_ref, own_o))):
                cp = pltpu.make_async_copy(src.at[blk(px, py, c)], dst.at[j], loc_sems.at[a, j])
                cp.start()
                local.append(cp)
        for cp in local:
            cp.wait()
        for cp in stage_a:
            cp.wait_recv()

        stage_b = []
        for k, (px, py) in enumerate(others):
            j = 2 * px + py
            snd_t[k] = _bf(own_t[j] + rcv_t[j])
            snd_o[k] = _bf(own_o[j] + rcv_o[j])
            for a, (src, dst) in enumerate(((snd_t, got_t), (snd_o, got_o))):
                cp = pltpu.make_async_remote_copy(src_ref=src.at[k], dst_ref=dst.at[k], send_sem=b_send.at[a, k],
                                                  recv_sem=b_recv.at[a, k], device_id=(px, py, c), device_id_type=MESH)
                cp.start()
                stage_b.append(cp)
        jm = 2 * x + y
        tot_t = own_t[jm] + rcv_t[jm]
        tot_o = own_o[jm] + rcv_o[jm]
        for k in range(3):
            stage_b[2 * k].wait_recv()
            stage_b[2 * k + 1].wait_recv()
            tot_t = tot_t + got_t[k].astype(F32)
            tot_o = tot_o + got_o[k].astype(F32)

        d, nm, nv = _adamw(w_out_ref[...], tot_o, m_out_ref[...], v_out_ref[...])
        g_out_o[...], d_out_o[...], nm_out_o[...], nv_out_o[...] = tot_o, d, nm, nv

        pad_ref[0:WIN_BLK, :] = tot_t
        pad_ref[WIN_BLK:WIN_PAD, :] = jnp.zeros((WIN_PAD - WIN_BLK, D), F32)
        tr_ref[...] = pad_ref[...].T
        g = tr_ref[:, 0:WIN_BLK]
        d, nm, nv = _adamw(w_in_ref[...], g, m_in_ref[...], v_in_ref[...])
        g_in_o[...], d_in_o[...], nm_in_o[...], nv_in_o[...] = g, d, nm, nv

        for cp in small_copies:
            cp.wait_recv()
        tot_s = small_all[0]
        for b in range(1, N_DEV):
            tot_s = tot_s + small_all[b]
        d, nm, nv = _adamw(sw_ref[...], tot_s, sm_ref[...], sv_ref[...])
        sg_o[...], sd_o[...], snm_o[...], snv_o[...] = tot_s, d, nm, nv

        for cp in stage_a + stage_b + small_copies:
            cp.wait_send()

    vm = pl.BlockSpec(memory_space=pltpu.VMEM)
    hbm = pl.BlockSpec(memory_space=pl.ANY)
    win = jax.ShapeDtypeStruct((D, WIN_BLK), F32)
    wout = jax.ShapeDtypeStruct((WOUT_BLK, D), F32)
    small = jax.ShapeDtypeStruct((SMALL_ROWS, CH), F32)
    return pl.pallas_call(
        body, name="reduce_and_update",
        in_specs=[hbm, hbm] + [vm] * 10, out_specs=(vm,) * 12,
        out_shape=(win,) * 4 + (wout,) * 4 + (small,) * 4,
        scratch_shapes=[
            pltpu.VMEM((4, WIN_BLK, D), F32), pltpu.VMEM((4, WOUT_BLK, D), F32),
            pltpu.VMEM((4, WIN_BLK, D), F32), pltpu.VMEM((4, WOUT_BLK, D), F32),
            pltpu.VMEM((3, WIN_BLK, D), BF16), pltpu.VMEM((3, WOUT_BLK, D), BF16),
            pltpu.VMEM((3, WIN_BLK, D), BF16), pltpu.VMEM((3, WOUT_BLK, D), BF16),
            pltpu.VMEM((N_DEV, SMALL_ROWS, CH), F32),
            pltpu.VMEM((WIN_PAD, D), F32), pltpu.VMEM((D, WIN_PAD), F32),
            pltpu.SemaphoreType.DMA((2, 4)),
            pltpu.SemaphoreType.DMA((2, 4)), pltpu.SemaphoreType.DMA((2, 4)),
            pltpu.SemaphoreType.DMA((2, 3)), pltpu.SemaphoreType.DMA((2, 3)),
            pltpu.SemaphoreType.DMA((7,)), pltpu.SemaphoreType.DMA((7,)),
        ],
        compiler_params=_cparams(),
    )(gwt, gwo, small_g, w_in, m_in, v_in, w_out, m_out, v_out, small_w, small_m, small_v)


def _pack_small(norm_w, ret_norm_w, q_norm_w, k_norm_w, sinks, rel_bias):
    def row(v):
        v = v.reshape(1, -1)
        return jnp.pad(v, ((0, 0), (0, CH - v.shape[1])))
    parts = [norm_w.reshape(8, CH), ret_norm_w.reshape(4, CH), row(q_norm_w), row(k_norm_w), row(sinks),
             rel_bias.reshape(2, CH), jnp.zeros((SMALL_ROWS - 17, CH), F32)]
    return jnp.concatenate(parts, axis=0)


def _unpack_small(t):
    return (t[0:8].reshape(D), t[8:12].reshape(512), t[12, 0:SWA_D], t[13, 0:SWA_D], t[14, 0:SWA_H],
            t[15:17].reshape(N_BUCKETS, SWA_H))


def kernel(x, norm_w, w_in, ret_norm_w, q_norm_w, k_norm_w, sinks, rel_bias, w_out, loss_target, m_norm_w, m_w_in, m_ret_norm_w, m_q_norm_w, m_k_norm_w, m_sinks, m_rel_bias, m_w_out, v_norm_w, v_w_in, v_ret_norm_w, v_q_norm_w, v_k_norm_w, v_sinks, v_rel_bias, v_w_out):
    x2 = x.reshape(T, D)
    target = loss_target.reshape(T, D)
    nw = norm_w.reshape(1, D)
    rnw = ret_norm_w.reshape(1, 512)
    qnw = q_norm_w.reshape(1, SWA_D)
    knw = k_norm_w.reshape(1, SWA_D)
    relb = rel_bias.reshape(N_BUCKETS * SWA_H)
    cos_t, sin_t = _rotary_tables()
    decay, zeta_b, xi_b, chunk_decay = _retention_tables()
    bucket = _bucket_table()

    wt, wo = _gather_weights(w_in, w_out)

    rq, rk, rv, rg, sq, sk, sv, sg, rstd = _in_proj(x2, nw, wt)
    ro, mix_r, states = _retention_fwd(rq, rk, rv, rg, rnw, cos_t, sin_t, decay, zeta_b, xi_b, chunk_decay)
    so, mix_s = _swa_fwd(sq, sk, sv, sg, qnw, knw, sinks, relb, bucket)
    sse, dy, dmix_r, dmix_s, gwo = _out_proj(mix_r, mix_s, wo, x2, target)

    drq, drk, drv, drg, d_rnw = _retention_bwd(rq, rk, rv, rg, ro, states, dmix_r, rnw, cos_t, sin_t, decay, zeta_b,
                                               xi_b, chunk_decay)
    dsq, dsg, dsk, dsv, dbias, d_sinks, d_qnw, d_knw = _swa_bwd(sq, sk, sv, sg, so, dmix_s, qnw, knw, sinks, relb,
                                                                bucket)
    d_relb = _rel_bias_grad(dbias, bucket)
    grad_x, gwt, d_nw = _in_proj_bwd((drq, drk, drv, drg, dsq, dsk, dsv, dsg), x2, rstd, nw, dy, wt)

    small_g = jnp.concatenate([d_nw.reshape(8, CH), d_rnw.reshape(4, CH), jnp.pad(d_qnw, ((0, 0), (0, CH - SWA_D))),
                               jnp.pad(d_knw, ((0, 0), (0, CH - SWA_D))), d_sinks, d_relb,
                               jnp.zeros((SMALL_ROWS - 17, CH), F32)], axis=0)
    small_w = _pack_small(norm_w, ret_norm_w, q_norm_w, k_norm_w, sinks, rel_bias)
    small_m = _pack_small(m_norm_w, m_ret_norm_w, m_q_norm_w, m_k_norm_w, m_sinks, m_rel_bias)
    small_v = _pack_small(v_norm_w, v_ret_norm_w, v_q_norm_w, v_k_norm_w, v_sinks, v_rel_bias)

    (g_in, d_in, nm_in, nv_in, g_out, d_out, nm_out, nv_out, s_g, s_d, s_nm, s_nv) = _reduce_and_update(
        gwt.reshape(N_DEV, WIN_BLK, D), gwo.reshape(N_DEV, WOUT_BLK, D), small_g,
        w_in, m_w_in, v_w_in, w_out, m_w_out, v_w_out, small_w, small_m, small_v)

    loss = lax.psum(sse[0, 0] * (0.5 / D), ("x", "y", "c"))

    def leaves(small, big_in, big_out):
        a = _unpack_small(small)
        return (a[0], big_in, a[1], a[2], a[3], a[4], a[5], big_out)

    return (loss, grad_x.reshape(1, T, D), *leaves(s_g, g_in, g_out), *leaves(s_d, d_in, d_out),
            *leaves(s_nm, nm_in, nm_out), *leaves(s_nv, nv_in, nv_out))
```

```python
import functools

import numpy as np
import jax
import jax.numpy as jnp
from jax import lax
from jax.experimental import pallas as pl
from jax.experimental.pallas import tpu as pltpu

F32 = jnp.float32
BF16 = jnp.bfloat16
MESH = pl.DeviceIdType.MESH

T = 2048
D = 1024
D_IN = 2816
N_DEV = 8
WIN_BLK = D_IN // N_DEV
WIN_PAD = 384
WOUT_BLK = D // N_DEV
CH = 128
N_CH = T // CH
RET_H, RET_DK, RET_DV = 4, 64, 128
SWA_H, SWA_KV, SWA_D, SWA_G = 8, 2, 64, 4
N_BUCKETS = 32
NORM_EPS = 1e-6
GN_EPS = 1e-5
NEG_INF = -1e30
PIECES = (256, 256, 512, 512, 512, 128, 128, 512)
OFFS = tuple(int(v) for v in np.cumsum((0,) + PIECES))
TM = 256

ADAM_LR, ADAM_B1, ADAM_B2, ADAM_EPS, ADAM_WD, ADAM_STEP = 0.001, 0.9, 0.999, 1e-08, 0.01, 10

VMEM_LIMIT = 56 * 1024 * 1024


def _cparams(n_grid=0, vmem=VMEM_LIMIT):
    sem = ("arbitrary",) * n_grid if n_grid else None
    return pltpu.CompilerParams(dimension_semantics=sem, vmem_limit_bytes=vmem)


def _dot(a, b):
    return jnp.dot(a, b, preferred_element_type=F32)


def _dot_nt(a, b):
    return lax.dot_general(a, b, (((1,), (1,)), ((), ())), preferred_element_type=F32)


def _bf(a):
    return a.astype(BF16)


def _sigmoid(x):
    return 1.0 / (1.0 + jnp.exp(-x))


def _adamw(w, g, m, v):
    m = ADAM_B1 * m + (1.0 - ADAM_B1) * g
    v = ADAM_B2 * v + (1.0 - ADAM_B2) * (g * g)
    m_hat = m / (1.0 - ADAM_B1 ** ADAM_STEP)
    v_hat = v / (1.0 - ADAM_B2 ** ADAM_STEP)
    delta = -ADAM_LR * (m_hat / (jnp.sqrt(v_hat) + ADAM_EPS) + ADAM_WD * w)
    return delta, m, v


def _rotary_tables():
    half = RET_DK // 2
    inv_freq = np.float32(10000.0) ** (-np.arange(half, dtype=np.float32) / np.float32(half))
    ang = np.arange(T, dtype=np.float32)[:, None] * inv_freq[None, :]
    cos, sin = np.cos(ang).astype(np.float32), np.sin(ang).astype(np.float32)
    cos64 = np.concatenate([cos, cos], axis=-1)
    sin64 = np.concatenate([-sin, sin], axis=-1)
    return np.tile(cos64, (1, RET_H)), np.tile(sin64, (1, RET_H))


def _retention_tables():
    gamma = (1.0 - np.exp2(-5.0 - np.arange(RET_H, dtype=np.float32))).astype(np.float32)
    log_g = np.log(gamma).astype(np.float32)
    i = np.arange(CH, dtype=np.float32)
    diff = i[:, None] - i[None, :]
    decay = np.where(diff >= 0, np.exp(log_g[:, None, None] * np.maximum(diff, 0.0)), 0.0).astype(np.float32)
    zeta = np.exp(log_g[:, None] * (CH - 1.0 - i)).astype(np.float32)
    xi = np.exp(log_g[:, None] * (i + 1.0)).astype(np.float32)
    chunk_decay = [float(v) for v in np.exp(log_g * np.float32(CH)).astype(np.float32)]
    zeta_b = np.ascontiguousarray(np.broadcast_to(zeta[:, :, None], (RET_H, CH, RET_DK)))
    xi_b = np.ascontiguousarray(np.broadcast_to(xi[:, :, None], (RET_H, CH, RET_DK)))
    return decay, zeta_b, xi_b, chunk_decay


def _bucket_table():
    qi = np.arange(CH)[:, None]
    kj = np.arange(2 * CH)[None, :]
    dist = qi + CH - kj
    n = np.maximum(dist, 0)
    max_exact = N_BUCKETS // 2
    nf = np.maximum(n, 1).astype(np.float32)
    large = max_exact + (np.log(nf / np.float32(max_exact)) / np.float32(np.log(CH / max_exact))
                         * np.float32(N_BUCKETS - max_exact)).astype(np.int32)
    large = np.minimum(large, N_BUCKETS - 1)
    bucket = np.where(n < max_exact, n, large)
    return np.where((dist >= 0) & (dist < CH), bucket, -1).astype(np.int32)


def _swap_halves(t):
    lane = lax.broadcasted_iota(jnp.int32, t.shape, 1)
    n = t.shape[1]
    return jnp.where((lane % RET_DK) < RET_DK // 2, pltpu.roll(t, n - RET_DK // 2, 1), pltpu.roll(t, RET_DK // 2, 1))


def _mesh_pos():
    return lax.axis_index("x"), lax.axis_index("y"), lax.axis_index("c")


def _gather_weights(w_in_blk, w_out_blk):
    def body(win_ref, wout_ref, wt_ref, wo_ref, pad_ref, send_sems, recv_sems):
        x, y, c = _mesh_pos()
        me = 4 * x + 2 * y + c
        sibling = (x, y, 1 - c)
        chips = [(1 - x, y), (x, 1 - y), (1 - x, 1 - y)]

        pad_ref[...] = jnp.zeros_like(pad_ref)
        pad_ref[:, 0:WIN_BLK] = win_ref[...]
        wt_ref[me] = _bf(pad_ref[...].T[0:WIN_BLK, :])
        wo_ref[me] = _bf(wout_ref[...])

        def blk(px, py, pc):
            return 4 * px + 2 * py + pc

        def copies(k, b, to):
            return [pltpu.make_async_remote_copy(src_ref=ref.at[b], dst_ref=ref.at[b],
                                                 send_sem=send_sems.at[a, k], recv_sem=recv_sems.at[a, k],
                                                 device_id=to, device_id_type=MESH)
                    for a, ref in enumerate((wt_ref, wo_ref))]

        first = copies(0, me, sibling)
        for j, chip in enumerate(chips):
            first += copies(1 + j, me, (*chip, c))
        for cp in first:
            cp.start()
        passed = []
        for j, chip in enumerate(chips):
            for cp in copies(1 + j, blk(*chip, c), (x, y, c)):
                cp.wait_recv()
            fwd = copies(4 + j, blk(*chip, c), sibling)
            for cp in fwd:
                cp.start()
            passed += fwd
        for cp in copies(0, blk(x, y, 1 - c), (x, y, c)):
            cp.wait_recv()
        for j, chip in enumerate(chips):
            for cp in copies(4 + j, blk(*chip, 1 - c), (x, y, c)):
                cp.wait_recv()
        for cp in first + passed:
            cp.wait_send()

    vm = pl.BlockSpec(memory_space=pltpu.VMEM)
    wt, wo = pl.pallas_call(
        body, name="gather_weights",
        out_shape=(jax.ShapeDtypeStruct((N_DEV, WIN_BLK, D), BF16), jax.ShapeDtypeStruct((N_DEV, WOUT_BLK, D), BF16)),
        in_specs=[vm, vm], out_specs=(vm, vm),
        scratch_shapes=[pltpu.VMEM((D, WIN_PAD), F32), pltpu.SemaphoreType.DMA((2, 7)), pltpu.SemaphoreType.DMA((2, 7))],
        compiler_params=_cparams(vmem=32 * 1024 * 1024),
    )(w_in_blk, w_out_blk)
    return wt.reshape(D_IN, D), wo.reshape(D, D)


def _in_proj(x, norm_w, wt):
    def body(x_ref, nw_ref, w_ref, *outs):
        piece_refs, rstd_ref = outs[:8], outs[8]
        xv = x_ref[...]
        r = lax.rsqrt(jnp.mean(xv * xv, axis=-1, keepdims=True) + NORM_EPS)
        h = _bf(xv * r * nw_ref[...])
        for i, ref in enumerate(piece_refs):
            ref[...] = _dot_nt(h, w_ref[OFFS[i]:OFFS[i + 1], :])
        rstd_ref[...] = r

    row = lambda w: pl.BlockSpec((TM, w), lambda i: (i, 0))
    return pl.pallas_call(
        body, name="in_proj", grid=(T // TM,),
        in_specs=[row(D), pl.BlockSpec((1, D), lambda i: (0, 0)), pl.BlockSpec((D_IN, D), lambda i: (0, 0))],
        out_specs=tuple(row(w) for w in PIECES) + (row(1),),
        out_shape=tuple(jax.ShapeDtypeStruct((T, w), F32) for w in PIECES) + (jax.ShapeDtypeStruct((T, 1), F32),),
        compiler_params=_cparams(1),
    )(x, norm_w, wt)


def _group_norm(o):
    mu = jnp.mean(o, axis=-1, keepdims=True)
    var = jnp.mean((o - mu) * (o - mu), axis=-1, keepdims=True)
    rstd = lax.rsqrt(var + GN_EPS)
    return (o - mu) * rstd, rstd


def _retention_fwd(rq, rk, rv, rg, ret_norm_w, cos_t, sin_t, decay, zeta_b, xi_b, chunk_decay):
    def body(rq_ref, rk_ref, rv_ref, rg_ref, w_ref, cos_ref, sin_ref, decay_ref, zeta_ref, xi_ref,
             ro_ref, mix_ref, st_ref, state):
        n = pl.program_id(0)

        @pl.when(n == 0)
        def _():
            state[...] = jnp.zeros_like(state)

        cos, sin = cos_ref[...], sin_ref[...]
        q = rq_ref[...]
        q = q * cos + _swap_halves(q) * sin
        k = rk_ref[...]
        k = (k * cos + _swap_halves(k) * sin) * (RET_DK ** -0.5)
        for h in range(RET_H):
            qh = q[:, h * RET_DK:(h + 1) * RET_DK]
            kh = k[:, h * RET_DK:(h + 1) * RET_DK]
            vh = rv_ref[:, h * RET_DV:(h + 1) * RET_DV]
            s_t = state[h]
            st_ref[0, h] = s_t
            scores = _dot_nt(_bf(qh), _bf(kh)) * decay_ref[h]
            o = _dot(_bf(scores), _bf(vh)) + _dot_nt(_bf(qh * xi_ref[h]), _bf(s_t))
            state[h] = chunk_decay[h] * s_t + _dot(_bf(vh.T), _bf(kh * zeta_ref[h]))
            rn, _ = _group_norm(o)
            g = rg_ref[:, h * RET_DV:(h + 1) * RET_DV]
            ro_ref[:, h * RET_DV:(h + 1) * RET_DV] = o
            mix_ref[:, h * RET_DV:(h + 1) * RET_DV] = rn * w_ref[:, h * RET_DV:(h + 1) * RET_DV] * (g * _sigmoid(g))

    row = lambda w: pl.BlockSpec((CH, w), lambda n: (n, 0))
    const = lambda shape: pl.BlockSpec(shape, lambda n: (0,) * len(shape))
    return pl.pallas_call(
        body, name="retention_fwd", grid=(N_CH,),
        in_specs=[row(256), row(256), row(512), row(512), const((1, 512)), row(256), row(256),
                  const((RET_H, CH, CH)), const((RET_H, CH, RET_DK)), const((RET_H, CH, RET_DK))],
        out_specs=(row(512), row(512), pl.BlockSpec((1, RET_H, RET_DV, RET_DK), lambda n: (n, 0, 0, 0))),
        out_shape=(jax.ShapeDtypeStruct((T, 512), F32), jax.ShapeDtypeStruct((T, 512), F32),
                   jax.ShapeDtypeStruct((N_CH, RET_H, RET_DV, RET_DK), F32)),
        scratch_shapes=[pltpu.VMEM((RET_H, RET_DV, RET_DK), F32)],
        compiler_params=_cparams(1),
    )(rq, rk, rv, rg, ret_norm_w, cos_t, sin_t, decay, zeta_b, xi_b)


def _retention_bwd(rq, rk, rv, rg, ro, states, dmix, ret_norm_w, cos_t, sin_t, decay, zeta_b, xi_b, chunk_decay):
    def body(rq_ref, rk_ref, rv_ref, rg_ref, ro_ref, st_ref, dm_ref, w_ref, cos_ref, sin_ref, decay_ref, zeta_ref,
             xi_ref, dq_ref, dk_ref, dv_ref, dg_ref, dw_ref, gstate):
        i = pl.program_id(0)

        @pl.when(i == 0)
        def _():
            gstate[...] = jnp.zeros_like(gstate)
            dw_ref[...] = jnp.zeros_like(dw_ref)

        cos, sin = cos_ref[...], sin_ref[...]
        q = rq_ref[...]
        q = q * cos + _swap_halves(q) * sin
        k = rk_ref[...]
        k = (k * cos + _swap_halves(k) * sin) * (RET_DK ** -0.5)
        dqs, dks = [], []
        for h in range(RET_H):
            cols = slice(h * RET_DV, (h + 1) * RET_DV)
            qh = q[:, h * RET_DK:(h + 1) * RET_DK]
            kh = k[:, h * RET_DK:(h + 1) * RET_DK]
            vh = rv_ref[:, cols]
            g = rg_ref[:, cols]
            w = w_ref[:, cols]
            dm = dm_ref[:, cols]
            rn, rstd = _group_norm(ro_ref[:, cols])
            sig = _sigmoid(g)
            silu = g * sig
            dg_ref[:, cols] = dm * rn * w * (sig * (1.0 + g * (1.0 - sig)))
            dw_ref[:, cols] += jnp.sum(dm * silu * rn, axis=0, keepdims=True)
            drn = dm * silu * w
            do = rstd * (drn - jnp.mean(drn, axis=-1, keepdims=True)
                         - rn * jnp.mean(drn * rn, axis=-1, keepdims=True))
            do_b = _bf(do)
            s_t = st_ref[0, h]
            g_t = gstate[h]
            dec = decay_ref[h]
            scores = _dot_nt(_bf(qh), _bf(kh)) * dec
            dscores = _dot_nt(do_b, _bf(vh)) * dec
            qx = qh * xi_ref[h]
            kz = kh * zeta_ref[h]
            dqh = _dot(_bf(dscores), _bf(kh)) + _dot(do_b, _bf(s_t)) * xi_ref[h]
            dkh = _dot(_bf(dscores.T), _bf(qh)) + _dot(_bf(vh), _bf(g_t)) * zeta_ref[h]
            dv_ref[:, cols] = _dot(_bf(scores.T), do_b) + _dot_nt(_bf(kz), _bf(g_t))
            gstate[h] = chunk_decay[h] * g_t + _dot(_bf(do.T), _bf(qx))
            dqs.append(dqh)
            dks.append(dkh * (RET_DK ** -0.5))
        dq = jnp.concatenate(dqs, axis=1)
        dk = jnp.concatenate(dks, axis=1)
        dq_ref[...] = dq * cos + _swap_halves(dq * sin)
        dk_ref[...] = dk * cos + _swap_halves(dk * sin)

    row = lambda w: pl.BlockSpec((CH, w), lambda i: (N_CH - 1 - i, 0))
    const = lambda shape: pl.BlockSpec(shape, lambda i: (0,) * len(shape))
    return pl.pallas_call(
        body, name="retention_bwd", grid=(N_CH,),
        in_specs=[row(256), row(256), row(512), row(512), row(512),
                  pl.BlockSpec((1, RET_H, RET_DV, RET_DK), lambda i: (N_CH - 1 - i, 0, 0, 0)), row(512),
                  const((1, 512)), row(256), row(256),
                  const((RET_H, CH, CH)), const((RET_H, CH, RET_DK)), const((RET_H, CH, RET_DK))],
        out_specs=(row(256), row(256), row(512), row(512), const((1, 512))),
        out_shape=(jax.ShapeDtypeStruct((T, 256), F32), jax.ShapeDtypeStruct((T, 256), F32),
                   jax.ShapeDtypeStruct((T, 512), F32), jax.ShapeDtypeStruct((T, 512), F32),
                   jax.ShapeDtypeStruct((1, 512), F32)),
        scratch_shapes=[pltpu.VMEM((RET_H, RET_DV, RET_DK), F32)],
        compiler_params=_cparams(1),
    )(rq, rk, rv, rg, ro, states, dmix, ret_norm_w, cos_t, sin_t, decay, zeta_b, xi_b)


def _build_bias(bucket_ref, relb_ref, bias_ref):
    bt = bucket_ref[...]
    for hq in range(SWA_H):
        b = jnp.zeros(bt.shape, F32)
        for bk in range(N_BUCKETS):
            b = jnp.where(bt == bk, relb_ref[bk * SWA_H + hq], b)
        bias_ref[hq] = b


def _rms(t, w):
    r = lax.rsqrt(jnp.mean(t * t, axis=-1, keepdims=True) + NORM_EPS)
    return t * r * w, r


def _rms_bwd(dn, t, r, w):
    u = dn * w
    return r * u - t * (r * r * r) * jnp.mean(u * t, axis=-1, keepdims=True), dn * t * r


def _swa_probs(qn, kn, bias, valid, sink):
    s = _dot_nt(_bf(qn), _bf(kn)) * (SWA_D ** -0.5) + bias
    s = jnp.where(valid, s, NEG_INF)
    m = jnp.maximum(jnp.max(s, axis=-1, keepdims=True), sink)
    p = jnp.exp(s - m)
    e_sink = jnp.exp(sink - m)
    den = jnp.sum(p, axis=-1, keepdims=True) + e_sink
    return p / den, e_sink / den


def _swa_valid(bucket_ref, n):
    bt = bucket_ref[...]
    col = lax.broadcasted_iota(jnp.int32, bt.shape, 1)
    return (bt >= 0) & ((n > 0) | (col >= CH))


def _swa_fwd(sq, sk, sv, sg, q_norm_w, k_norm_w, sinks, rel_bias_flat, bucket):
    def body(sq_ref, skp_ref, skc_ref, svp_ref, svc_ref, sg_ref, qw_ref, kw_ref, sinks_ref, relb_ref, bucket_ref,
             so_ref, mix_ref, bias_ref):
        n = pl.program_id(0)

        @pl.when(n == 0)
        def _():
            _build_bias(bucket_ref, relb_ref, bias_ref)

        valid = _swa_valid(bucket_ref, n)
        kb = jnp.concatenate([skp_ref[...], skc_ref[...]], axis=0)
        vb = jnp.concatenate([svp_ref[...], svc_ref[...]], axis=0)
        qw, kw = qw_ref[...], kw_ref[...]
        for kv in range(SWA_KV):
            kn, _ = _rms(kb[:, kv * SWA_D:(kv + 1) * SWA_D], kw)
            vh = _bf(vb[:, kv * SWA_D:(kv + 1) * SWA_D])
            for gi in range(SWA_G):
                hq = kv * SWA_G + gi
                cols = slice(hq * SWA_D, (hq + 1) * SWA_D)
                qn, _ = _rms(sq_ref[:, cols], qw)
                probs, _ = _swa_probs(qn, kn, bias_ref[hq], valid, sinks_ref[hq])
                o = _dot(_bf(probs), vh)
                g = sg_ref[:, cols]
                so_ref[:, cols] = o
                mix_ref[:, cols] = o * (g * _sigmoid(g))

    row = lambda w: pl.BlockSpec((CH, w), lambda n: (n, 0))
    prev = lambda w: pl.BlockSpec((CH, w), lambda n: (jnp.maximum(n - 1, 0), 0))
    const = lambda shape: pl.BlockSpec(shape, lambda n: (0,) * len(shape))
    smem = pl.BlockSpec(memory_space=pltpu.SMEM)
    return pl.pallas_call(
        body, name="swa_fwd", grid=(N_CH,),
        in_specs=[row(512), prev(128), row(128), prev(128), row(128), row(512), const((1, SWA_D)), const((1, SWA_D)),
                  smem, smem, const((CH, 2 * CH))],
        out_specs=(row(512), row(512)),
        out_shape=(jax.ShapeDtypeStruct((T, 512), F32), jax.ShapeDtypeStruct((T, 512), F32)),
        scratch_shapes=[pltpu.VMEM((SWA_H, CH, 2 * CH), F32)],
        compiler_params=_cparams(1),
    )(sq, sk, sk, sv, sv, sg, q_norm_w, k_norm_w, sinks, rel_bias_flat, bucket)


def _swa_bwd(sq, sk, sv, sg, so, dmix, q_norm_w, k_norm_w, sinks, rel_bias_flat, bucket):
    def body(sq_ref, skp_ref, skc_ref, svp_ref, svc_ref, sg_ref, so_ref, dm_ref, qw_ref, kw_ref, sinks_ref, relb_ref,
             bucket_ref, dq_ref, dg_ref, dk_ref, dv_ref, dbias_ref, dsink_ref, dqw_ref, dkw_ref, bias_ref):
        n = pl.program_id(0)

        @pl.when(n == 0)
        def _():
            _build_bias(bucket_ref, relb_ref, bias_ref)
            for ref in (dk_ref, dv_ref, dbias_ref, dsink_ref, dqw_ref, dkw_ref):
                ref[...] = jnp.zeros_like(ref)

        valid = _swa_valid(bucket_ref, n)
        kb = jnp.concatenate([skp_ref[...], skc_ref[...]], axis=0)
        vb = jnp.concatenate([svp_ref[...], svc_ref[...]], axis=0)
        qw, kw = qw_ref[...], kw_ref[...]
        lane = lax.broadcasted_iota(jnp.int32, (1, CH), 1)
        dsink = jnp.zeros((1, CH), F32)
        dqw = jnp.zeros((1, SWA_D), F32)
        dkw = jnp.zeros((1, SWA_D), F32)
        dk_band, dv_band = [], []
        for kv in range(SWA_KV):
            kraw = kb[:, kv * SWA_D:(kv + 1) * SWA_D]
            kn, rk = _rms(kraw, kw)
            kn_b = _bf(kn)
            vh = _bf(vb[:, kv * SWA_D:(kv + 1) * SWA_D])
            dkn = jnp.zeros((2 * CH, SWA_D), F32)
            dvh = jnp.zeros((2 * CH, SWA_D), F32)
            for gi in range(SWA_G):
                hq = kv * SWA_G + gi
                cols = slice(hq * SWA_D, (hq + 1) * SWA_D)
                qraw = sq_ref[:, cols]
                qn, rq = _rms(qraw, qw)
                probs, p_sink = _swa_probs(qn, kn, bias_ref[hq], valid, sinks_ref[hq])
                g = sg_ref[:, cols]
                sig = _sigmoid(g)
                dm = dm_ref[:, cols]
                dg_ref[:, cols] = dm * so_ref[:, cols] * (sig * (1.0 + g * (1.0 - sig)))
                do_b = _bf(dm * (g * sig))
                dprobs = _dot_nt(do_b, vh)
                dvh = dvh + _dot(_bf(probs.T), do_b)
                t = jnp.sum(probs * dprobs, axis=-1, keepdims=True)
                dlog = probs * (dprobs - t)
                dsink = dsink + jnp.where(lane == hq, jnp.sum(-p_sink * t, axis=0, keepdims=True), 0.0)
                dbias_ref[hq] += dlog
                ds = dlog * (SWA_D ** -0.5)
                dqn = _dot(_bf(ds), kn_b)
                dkn = dkn + _dot(_bf(ds.T), _bf(qn))
                dqraw, dqw_rows = _rms_bwd(dqn, qraw, rq, qw)
                dqw = dqw + jnp.sum(dqw_rows, axis=0, keepdims=True)
                dq_ref[:, cols] = dqraw
            dkraw, dkw_rows = _rms_bwd(dkn, kraw, rk, kw)
            dkw = dkw + jnp.sum(dkw_rows, axis=0, keepdims=True)
            dk_band.append(dkraw)
            dv_band.append(dvh)
        dkb = jnp.concatenate(dk_band, axis=1)
        dvb = jnp.concatenate(dv_band, axis=1)
        r_prev = pl.multiple_of(jnp.maximum(n - 1, 0) * CH, CH)
        r_cur = pl.multiple_of(n * CH, CH)
        dk_ref[pl.ds(r_prev, CH), :] += dkb[0:CH]
        dk_ref[pl.ds(r_cur, CH), :] += dkb[CH:2 * CH]
        dv_ref[pl.ds(r_prev, CH), :] += dvb[0:CH]
        dv_ref[pl.ds(r_cur, CH), :] += dvb[CH:2 * CH]
        dsink_ref[...] += dsink
        dqw_ref[...] += dqw
        dkw_ref[...] += dkw

    row = lambda w: pl.BlockSpec((CH, w), lambda n: (n, 0))
    prev = lambda w: pl.BlockSpec((CH, w), lambda n: (jnp.maximum(n - 1, 0), 0))
    const = lambda shape: pl.BlockSpec(shape, lambda n: (0,) * len(shape))
    smem = pl.BlockSpec(memory_space=pltpu.SMEM)
    return pl.pallas_call(
        body, name="swa_bwd", grid=(N_CH,),
        in_specs=[row(512), prev(128), row(128), prev(128), row(128), row(512), row(512), row(512),
                  const((1, SWA_D)), const((1, SWA_D)), smem, smem, const((CH, 2 * CH))],
        out_specs=(row(512), row(512), const((T, 128)), const((T, 128)), const((SWA_H, CH, 2 * CH)), const((1, CH)),
                   const((1, SWA_D)), const((1, SWA_D))),
        out_shape=(jax.ShapeDtypeStruct((T, 512), F32), jax.ShapeDtypeStruct((T, 512), F32),
                   jax.ShapeDtypeStruct((T, 128), F32), jax.ShapeDtypeStruct((T, 128), F32),
                   jax.ShapeDtypeStruct((SWA_H, CH, 2 * CH), F32), jax.ShapeDtypeStruct((1, CH), F32),
                   jax.ShapeDtypeStruct((1, SWA_D), F32), jax.ShapeDtypeStruct((1, SWA_D), F32)),
        scratch_shapes=[pltpu.VMEM((SWA_H, CH, 2 * CH), F32)],
        compiler_params=_cparams(1),
    )(sq, sk, sk, sv, sv, sg, so, dmix, q_norm_w, k_norm_w, sinks, rel_bias_flat, bucket)


def _rel_bias_grad(dbias, bucket):
    def body(dbias_ref, bucket_ref, out_ref):
        bt = bucket_ref[...]
        row = lax.broadcasted_iota(jnp.int32, (N_BUCKETS, SWA_H), 0)
        col = lax.broadcasted_iota(jnp.int32, (N_BUCKETS, SWA_H), 1)
        acc = jnp.zeros((N_BUCKETS, SWA_H), F32)
        for hq in range(SWA_H):
            d = dbias_ref[hq]
            for bk in range(N_BUCKETS):
                s = jnp.sum(jnp.sum(jnp.where(bt == bk, d, 0.0), axis=0, keepdims=True), axis=1, keepdims=True)
                acc = acc + jnp.where((row == bk) & (col == hq), s, 0.0)
        out_ref[...] = acc

    vm = pl.BlockSpec(memory_space=pltpu.VMEM)
    return pl.pallas_call(body, name="rel_bias_grad", out_shape=jax.ShapeDtypeStruct((N_BUCKETS, SWA_H), F32),
                          in_specs=[vm, vm], out_specs=vm, compiler_params=_cparams())(dbias, bucket)


def _out_proj(mix_r, mix_s, wo, x, target):
    def body(mr_ref, ms_ref, w_ref, x_ref, t_ref, loss_ref, dy_ref, dmr_ref, dms_ref, gw_ref):
        i = pl.program_id(0)

        @pl.when(i == 0)
        def _():
            loss_ref[...] = jnp.zeros_like(loss_ref)
            gw_ref[...] = jnp.zeros_like(gw_ref)

        mixed = jnp.concatenate([mr_ref[...], ms_ref[...]], axis=1)
        w = w_ref[...]
        err = x_ref[...] + _dot(_bf(mixed), w) - t_ref[...]
        loss_ref[...] += jnp.sum(jnp.sum(err * err, axis=1, keepdims=True), axis=0, keepdims=True)
        dy = err * (1.0 / D)
        dy_ref[...] = dy
        dy_b = _bf(dy)
        dmix = _dot_nt(dy_b, w)
        dmr_ref[...] = dmix[:, 0:512]
        dms_ref[...] = dmix[:, 512:1024]
        gw_ref[...] += _dot(_bf(mixed.T), dy_b)

    row = lambda w: pl.BlockSpec((TM, w), lambda i: (i, 0))
    const = lambda shape: pl.BlockSpec(shape, lambda i: (0,) * len(shape))
    return pl.pallas_call(
        body, name="out_proj", grid=(T // TM,),
        in_specs=[row(512), row(512), const((D, D)), row(D), row(D)],
        out_specs=(const((1, 1)), row(D), row(512), row(512), const((D, D))),
        out_shape=(jax.ShapeDtypeStruct((1, 1), F32), jax.ShapeDtypeStruct((T, D), F32),
                   jax.ShapeDtypeStruct((T, 512), F32), jax.ShapeDtypeStruct((T, 512), F32),
                   jax.ShapeDtypeStruct((D, D), F32)),
        compiler_params=_cparams(1),
    )(mix_r, mix_s, wo, x, target)


def _in_proj_bwd(dpieces, x, rstd, norm_w, dy, wt):
    def body(*refs):
        piece_refs = refs[:8]
        x_ref, r_ref, nw_ref, dy_ref, w_ref, gx_ref, gw_ref, gnw_ref = refs[8:]
        i = pl.program_id(0)

        @pl.when(i == 0)
        def _():
            gw_ref[...] = jnp.zeros_like(gw_ref)
            gnw_ref[...] = jnp.zeros_like(gnw_ref)

        dp = jnp.concatenate([ref[...] for ref in piece_refs], axis=1)
        xv, r, nw = x_ref[...], r_ref[...], nw_ref[...]
        xr = xv * r
        dh = _dot(_bf(dp), w_ref[...])
        gw_ref[...] += _dot(_bf(dp.T), _bf(xr * nw))
        u = dh * nw
        gx_ref[...] = dy_ref[...] + r * u - xv * (r * r * r) * jnp.mean(u * xv, axis=-1, keepdims=True)
        gnw_ref[...] += jnp.sum(dh * xr, axis=0, keepdims=True)

    row = lambda w: pl.BlockSpec((TM, w), lambda i: (i, 0))
    const = lambda shape, **kw: pl.BlockSpec(shape, lambda i: (0,) * len(shape), **kw)
    once = dict(pipeline_mode=pl.Buffered(1))
    return pl.pallas_call(
        body, name="in_proj_bwd", grid=(T // TM,),
        in_specs=[row(w) for w in PIECES] + [row(D), row(1), const((1, D)), row(D), const((D_IN, D), **once)],
        out_specs=(row(D), const((D_IN, D), **once), const((1, D))),
        out_shape=(jax.ShapeDtypeStruct((T, D), F32), jax.ShapeDtypeStruct((D_IN, D), F32),
                   jax.ShapeDtypeStruct((1, D), F32)),
        compiler_params=_cparams(1),
    )(*dpieces, x, rstd, norm_w, dy, wt)


SMALL_SHAPES = ((1, D), (1, 512), (1, SWA_D), (1, SWA_D), (1, SWA_H), (N_BUCKETS, SWA_H))
LOSS_ROW = 5


def _reduce_and_update(gwt, gwo, sse, small_g, w_in, m_in, v_in, w_out, m_out, v_out, small_w, small_m, small_v):
    n_small = len(SMALL_SHAPES)

    def body(*refs):
        gwt_ref, gwo_ref, sse_ref = refs[:3]
        sg_refs = refs[3:3 + n_small]
        w_in_ref, m_in_ref, v_in_ref, w_out_ref, m_out_ref, v_out_ref = refs[3 + n_small:9 + n_small]
        pos = 9 + n_small
        sw_refs, sm_refs, sv_refs = (refs[pos + i * n_small:pos + (i + 1) * n_small] for i in range(3))
        pos += 3 * n_small
        g_in_o, d_in_o, nm_in_o, nv_in_o, g_out_o, d_out_o, nm_out_o, nv_out_o = refs[pos:pos + 8]
        pos += 8
        sg_o, sd_o, snm_o, snv_o = (refs[pos + i * n_small:pos + (i + 1) * n_small] for i in range(4))
        pos += 4 * n_small
        loss_o = refs[pos]
        (own_t, own_o, rcv_t, rcv_o, snd_t, snd_o, got_t, got_o, tab_a, tab_b, sum_a, sum_b, pad_ref, tr_ref,
         loc_sems, a_send, a_recv, b_send, b_recv, s_send, s_recv) = refs[pos + 1:]
        x, y, c = _mesh_pos()
        me = 4 * x + 2 * y + c
        sibling = (x, y, 1 - c)
        all_chips = [(0, 0), (0, 1), (1, 0), (1, 1)]
        others = [(1 - x, y), (x, 1 - y), (1 - x, 1 - y)]

        def blk(px, py, pc):
            return 4 * px + 2 * py + pc

        mine_a, mine_b = tab_a.at[me], tab_b.at[me]
        mine_a[...] = jnp.zeros_like(mine_a)
        mine_b[...] = jnp.zeros_like(mine_b)
        for r in range(n_small - 1):
            mine_a[r:r + 1, 0:sg_refs[r].shape[1]] = sg_refs[r][...]
        mine_a[LOSS_ROW:LOSS_ROW + 1, 0:1] = sse_ref[...]
        mine_b[:, 0:SWA_H] = sg_refs[n_small - 1][...]
        small_copies = []
        for k in range(1, N_DEV):
            px, py, pc = x ^ (k >> 2), y ^ ((k >> 1) & 1), c ^ (k & 1)
            for a, tab in enumerate((tab_a, tab_b)):
                cp = pltpu.make_async_remote_copy(src_ref=tab.at[me], dst_ref=tab.at[me], send_sem=s_send.at[a, k - 1],
                                                  recv_sem=s_recv.at[a, k - 1], device_id=(px, py, pc),
                                                  device_id_type=MESH)
                cp.start()
                small_copies.append(cp)

        stage_a, local = [], []
        for j, (px, py) in enumerate(all_chips):
            for a, (src, dst) in enumerate(((gwt_ref, rcv_t), (gwo_ref, rcv_o))):
                cp = pltpu.make_async_remote_copy(src_ref=src.at[blk(px, py, 1 - c)], dst_ref=dst.at[j],
                                                  send_sem=a_send.at[a, j], recv_sem=a_recv.at[a, j],
                                                  device_id=sibling, device_id_type=MESH)
                cp.start()
                stage_a.append(cp)
            for a, (src, dst) in enumerate(((gwt_ref, own_t), (gwo_ref, own_o))):
                cp = pltpu.make_async_copy(src.at[blk(px, py, c)], dst.at[j], loc_sems.at[a, j])
                cp.start()
                local.append(cp)
        for cp in local:
            cp.wait()
        for cp in stage_a:
            cp.wait_recv()

        stage_b = []
        for k, (px, py) in enumerate(others):
            j = 2 * px + py
            snd_t[k] = _bf(own_t[j] + rcv_t[j])
            snd_o[k] = _bf(own_o[j] + rcv_o[j])
            for a, (src, dst) in enumerate(((snd_t, got_t), (snd_o, got_o))):
                cp = pltpu.make_async_remote_copy(src_ref=src.at[k], dst_ref=dst.at[k], send_sem=b_send.at[a, k],
                                                  recv_sem=b_recv.at[a, k], device_id=(px, py, c), device_id_type=MESH)
                cp.start()
                stage_b.append(cp)
        jm = 2 * x + y
        tot_t = own_t[jm] + rcv_t[jm]
        tot_o = own_o[jm] + rcv_o[jm]
        for k in range(3):
            stage_b[2 * k].wait_recv()
            stage_b[2 * k + 1].wait_recv()
            tot_t = tot_t + got_t[k].astype(F32)
            tot_o = tot_o + got_o[k].astype(F32)

        d, nm, nv = _adamw(w_out_ref[...], tot_o, m_out_ref[...], v_out_ref[...])
        g_out_o[...], d_out_o[...], nm_out_o[...], nv_out_o[...] = tot_o, d, nm, nv

        pad_ref[0:WIN_BLK, :] = tot_t
        pad_ref[WIN_BLK:WIN_PAD, :] = jnp.zeros((WIN_PAD - WIN_BLK, D), F32)
        tr_ref[...] = pad_ref[...].T
        g = tr_ref[:, 0:WIN_BLK]
        d, nm, nv = _adamw(w_in_ref[...], g, m_in_ref[...], v_in_ref[...])
        g_in_o[...], d_in_o[...], nm_in_o[...], nv_in_o[...] = g, d, nm, nv

        for cp in small_copies:
            cp.wait_recv()
        tot_a, tot_b = tab_a[0], tab_b[0]
        for b in range(1, N_DEV):
            tot_a = tot_a + tab_a[b]
            tot_b = tot_b + tab_b[b]
        sum_a[...] = tot_a
        sum_b[...] = tot_b
        loss_o[...] = sum_a[LOSS_ROW:LOSS_ROW + 1, 0:1] * (0.5 / D)
        for r, (rows, lanes) in enumerate(SMALL_SHAPES):
            g = sum_a[r:r + 1, 0:lanes] if rows == 1 else sum_b[:, 0:lanes]
            d, nm, nv = _adamw(sw_refs[r][...], g, sm_refs[r][...], sv_refs[r][...])
            sg_o[r][...], sd_o[r][...], snm_o[r][...], snv_o[r][...] = g, d, nm, nv

        for cp in stage_a + stage_b + small_copies:
            cp.wait_send()

    vm = pl.BlockSpec(memory_space=pltpu.VMEM)
    hbm = pl.BlockSpec(memory_space=pl.ANY)
    win = jax.ShapeDtypeStruct((D, WIN_BLK), F32)
    wout = jax.ShapeDtypeStruct((WOUT_BLK, D), F32)
    smalls = tuple(jax.ShapeDtypeStruct(s, F32) for s in SMALL_SHAPES)
    outs = pl.pallas_call(
        body, name="reduce_and_update",
        in_specs=[hbm, hbm] + [vm] * (7 + 4 * n_small), out_specs=(vm,) * (9 + 4 * n_small),
        out_shape=(win,) * 4 + (wout,) * 4 + smalls * 4 + (jax.ShapeDtypeStruct((1, 1), F32),),
        scratch_shapes=[
            pltpu.VMEM((4, WIN_BLK, D), F32), pltpu.VMEM((4, WOUT_BLK, D), F32),
            pltpu.VMEM((4, WIN_BLK, D), F32), pltpu.VMEM((4, WOUT_BLK, D), F32),
            pltpu.VMEM((3, WIN_BLK, D), BF16), pltpu.VMEM((3, WOUT_BLK, D), BF16),
            pltpu.VMEM((3, WIN_BLK, D), BF16), pltpu.VMEM((3, WOUT_BLK, D), BF16),
            pltpu.VMEM((N_DEV, 8, D), F32), pltpu.VMEM((N_DEV, N_BUCKETS, CH), F32),
            pltpu.VMEM((8, D), F32), pltpu.VMEM((N_BUCKETS, CH), F32),
            pltpu.VMEM((WIN_PAD, D), F32), pltpu.VMEM((D, WIN_PAD), F32),
            pltpu.SemaphoreType.DMA((2, 4)),
            pltpu.SemaphoreType.DMA((2, 4)), pltpu.SemaphoreType.DMA((2, 4)),
            pltpu.SemaphoreType.DMA((2, 3)), pltpu.SemaphoreType.DMA((2, 3)),
            pltpu.SemaphoreType.DMA((2, 7)), pltpu.SemaphoreType.DMA((2, 7)),
        ],
        compiler_params=_cparams(),
    )(gwt, gwo, sse, *small_g, w_in, m_in, v_in, w_out, m_out, v_out, *small_w, *small_m, *small_v)
    big, rest = outs[:8], outs[8:]
    return big, [rest[i * n_small:(i + 1) * n_small] for i in range(4)], rest[4 * n_small]


def _small_rows(norm_w, ret_norm_w, q_norm_w, k_norm_w, sinks, rel_bias):
    return (norm_w.reshape(1, D), ret_norm_w.reshape(1, 512), q_norm_w.reshape(1, SWA_D), k_norm_w.reshape(1, SWA_D),
            sinks.reshape(1, SWA_H), rel_bias)


def _small_leaves(rows):
    return (rows[0].reshape(D), rows[1].reshape(512), rows[2].reshape(SWA_D), rows[3].reshape(SWA_D),
            rows[4].reshape(SWA_H), rows[5])


def kernel(x, norm_w, w_in, ret_norm_w, q_norm_w, k_norm_w, sinks, rel_bias, w_out, loss_target, m_norm_w, m_w_in, m_ret_norm_w, m_q_norm_w, m_k_norm_w, m_sinks, m_rel_bias, m_w_out, v_norm_w, v_w_in, v_ret_norm_w, v_q_norm_w, v_k_norm_w, v_sinks, v_rel_bias, v_w_out):
    x2 = x.reshape(T, D)
    target = loss_target.reshape(T, D)
    nw = norm_w.reshape(1, D)
    rnw = ret_norm_w.reshape(1, 512)
    qnw = q_norm_w.reshape(1, SWA_D)
    knw = k_norm_w.reshape(1, SWA_D)
    relb = rel_bias.reshape(N_BUCKETS * SWA_H)
    cos_t, sin_t = _rotary_tables()
    decay, zeta_b, xi_b, chunk_decay = _retention_tables()
    bucket = _bucket_table()

    wt, wo = _gather_weights(w_in, w_out)

    rq, rk, rv, rg, sq, sk, sv, sg, rstd = _in_proj(x2, nw, wt)
    ro, mix_r, states = _retention_fwd(rq, rk, rv, rg, rnw, cos_t, sin_t, decay, zeta_b, xi_b, chunk_decay)
    so, mix_s = _swa_fwd(sq, sk, sv, sg, qnw, knw, sinks, relb, bucket)
    sse, dy, dmix_r, dmix_s, gwo = _out_proj(mix_r, mix_s, wo, x2, target)

    drq, drk, drv, drg, d_rnw = _retention_bwd(rq, rk, rv, rg, ro, states, dmix_r, rnw, cos_t, sin_t, decay, zeta_b,
                                               xi_b, chunk_decay)
    dsq, dsg, dsk, dsv, dbias, d_sinks, d_qnw, d_knw = _swa_bwd(sq, sk, sv, sg, so, dmix_s, qnw, knw, sinks, relb,
                                                                bucket)
    d_relb = _rel_bias_grad(dbias, bucket)
    grad_x, gwt, d_nw = _in_proj_bwd((drq, drk, drv, drg, dsq, dsk, dsv, dsg), x2, rstd, nw, dy, wt)

    small_g = (d_nw, d_rnw, d_qnw, d_knw, d_sinks, d_relb)
    small_w = _small_rows(norm_w, ret_norm_w, q_norm_w, k_norm_w, sinks, rel_bias)
    small_m = _small_rows(m_norm_w, m_ret_norm_w, m_q_norm_w, m_k_norm_w, m_sinks, m_rel_bias)
    small_v = _small_rows(v_norm_w, v_ret_norm_w, v_q_norm_w, v_k_norm_w, v_sinks, v_rel_bias)

    big, small, loss = _reduce_and_update(
        gwt.reshape(N_DEV, WIN_BLK, D), gwo.reshape(N_DEV, WOUT_BLK, D), sse, small_g,
        w_in, m_w_in, v_w_in, w_out, m_w_out, v_w_out, small_w, small_m, small_v)

    def leaves(i):
        a = _small_leaves(small[i])
        return (a[0], big[i], a[1], a[2], a[3], a[4], a[5], big[4 + i])

    return (loss.reshape(()), grad_x.reshape(1, T, D), *leaves(0), *leaves(1), *leaves(2), *leaves(3))
```

```python
import numpy as np
import jax
import jax.numpy as jnp
from jax import lax
from jax.experimental import pallas as pl
from jax.experimental.pallas import tpu as pltpu

F32 = jnp.float32
BF16 = jnp.bfloat16
MESH = pl.DeviceIdType.MESH

T = 2048
D = 1024
D_IN = 2816
N_DEV = 8
WIN_BLK = D_IN // N_DEV
WIN_PAD = 384
WOUT_BLK = D // N_DEV
CH = 128
N_CH = T // CH
RET_H, RET_DK, RET_DV = 4, 64, 128
SWA_H, SWA_KV, SWA_D, SWA_G = 8, 2, 64, 4
N_BUCKETS = 32
NORM_EPS = 1e-6
GN_EPS = 1e-5
NEG_INF = -1e30
PIECES = (256, 256, 512, 512, 512, 128, 128, 512)
OFFS = tuple(int(v) for v in np.cumsum((0,) + PIECES))
N_ROWMAJOR = 4
RET_W = OFFS[N_ROWMAJOR]
SWA_W = D_IN - RET_W
TM = 256

ADAM_LR, ADAM_B1, ADAM_B2, ADAM_EPS, ADAM_WD, ADAM_STEP = 0.001, 0.9, 0.999, 1e-08, 0.01, 10

VMEM_LIMIT = 56 * 1024 * 1024


def _cparams(n_grid=0, vmem=VMEM_LIMIT):
    sem = ("arbitrary",) * n_grid if n_grid else None
    return pltpu.CompilerParams(dimension_semantics=sem, vmem_limit_bytes=vmem)


def _dot(a, b):
    return jnp.dot(a, b, preferred_element_type=F32)


def _dot_nt(a, b):
    return lax.dot_general(a, b, (((1,), (1,)), ((), ())), preferred_element_type=F32)


def _bf(a):
    return a.astype(BF16)


def _sigmoid(x):
    return 1.0 / (1.0 + jnp.exp(-x))


def _adamw(w, g, m, v):
    m = ADAM_B1 * m + (1.0 - ADAM_B1) * g
    v = ADAM_B2 * v + (1.0 - ADAM_B2) * (g * g)
    m_hat = m / (1.0 - ADAM_B1 ** ADAM_STEP)
    v_hat = v / (1.0 - ADAM_B2 ** ADAM_STEP)
    delta = -ADAM_LR * (m_hat / (jnp.sqrt(v_hat) + ADAM_EPS) + ADAM_WD * w)
    return delta, m, v


def _rotary_tables():
    half = RET_DK // 2
    inv_freq = np.float32(10000.0) ** (-np.arange(half, dtype=np.float32) / np.float32(half))
    ang = np.arange(T, dtype=np.float32)[:, None] * inv_freq[None, :]
    cos, sin = np.cos(ang).astype(np.float32), np.sin(ang).astype(np.float32)
    cos64 = np.concatenate([cos, cos], axis=-1)
    sin64 = np.concatenate([-sin, sin], axis=-1)
    return np.tile(cos64, (1, RET_H)), np.tile(sin64, (1, RET_H))


def _retention_tables():
    gamma = (1.0 - np.exp2(-5.0 - np.arange(RET_H, dtype=np.float32))).astype(np.float32)
    log_g = np.log(gamma).astype(np.float32)
    i = np.arange(CH, dtype=np.float32)
    diff = i[:, None] - i[None, :]
    decay = np.where(diff >= 0, np.exp(log_g[:, None, None] * np.maximum(diff, 0.0)), 0.0).astype(np.float32)
    zeta = np.exp(log_g[:, None] * (CH - 1.0 - i)).astype(np.float32)
    xi = np.exp(log_g[:, None] * (i + 1.0)).astype(np.float32)
    chunk_decay = [float(v) for v in np.exp(log_g * np.float32(CH)).astype(np.float32)]
    zeta_b = np.ascontiguousarray(np.broadcast_to(zeta[:, :, None], (RET_H, CH, RET_DK)))
    xi_b = np.ascontiguousarray(np.broadcast_to(xi[:, :, None], (RET_H, CH, RET_DK)))
    return decay, zeta_b, xi_b, chunk_decay


def _bucket_table_t():
    qi = np.arange(CH)[:, None]
    kj = np.arange(2 * CH)[None, :]
    dist = qi + CH - kj
    n = np.maximum(dist, 0)
    max_exact = N_BUCKETS // 2
    nf = np.maximum(n, 1).astype(np.float32)
    large = max_exact + (np.log(nf / np.float32(max_exact)) / np.float32(np.log(CH / max_exact))
                         * np.float32(N_BUCKETS - max_exact)).astype(np.int32)
    large = np.minimum(large, N_BUCKETS - 1)
    bucket = np.where(n < max_exact, n, large)
    return np.ascontiguousarray(np.where((dist >= 0) & (dist < CH), bucket, -1).astype(np.int32).T)


def _swap_halves(t):
    lane = lax.broadcasted_iota(jnp.int32, t.shape, 1)
    n = t.shape[1]
    return jnp.where((lane % RET_DK) < RET_DK // 2, pltpu.roll(t, n - RET_DK // 2, 1), pltpu.roll(t, RET_DK // 2, 1))


def _mesh_pos():
    return lax.axis_index("x"), lax.axis_index("y"), lax.axis_index("c")


def _gather_weights(w_in_blk, w_out_blk):
    def body(win_ref, wout_ref, wt_ref, wo_ref, pad_ref, send_sems, recv_sems):
        x, y, c = _mesh_pos()
        me = 4 * x + 2 * y + c
        sibling = (x, y, 1 - c)
        chips = [(1 - x, y), (x, 1 - y), (1 - x, 1 - y)]

        pad_ref[...] = jnp.zeros_like(pad_ref)
        pad_ref[:, 0:WIN_BLK] = win_ref[...]
        wt_ref[me] = _bf(pad_ref[...].T[0:WIN_BLK, :])
        wo_ref[me] = _bf(wout_ref[...])

        def blk(px, py, pc):
            return 4 * px + 2 * py + pc

        def copies(k, b, to):
            return [pltpu.make_async_remote_copy(src_ref=ref.at[b], dst_ref=ref.at[b],
                                                 send_sem=send_sems.at[a, k], recv_sem=recv_sems.at[a, k],
                                                 device_id=to, device_id_type=MESH)
                    for a, ref in enumerate((wt_ref, wo_ref))]

        first = copies(0, me, sibling)
        for j, chip in enumerate(chips):
            first += copies(1 + j, me, (*chip, c))
        for cp in first:
            cp.start()
        passed = []
        for j, chip in enumerate(chips):
            for cp in copies(1 + j, blk(*chip, c), (x, y, c)):
                cp.wait_recv()
            fwd = copies(4 + j, blk(*chip, c), sibling)
            for cp in fwd:
                cp.start()
            passed += fwd
        for cp in copies(0, blk(x, y, 1 - c), (x, y, c)):
            cp.wait_recv()
        for j, chip in enumerate(chips):
            for cp in copies(4 + j, blk(*chip, 1 - c), (x, y, c)):
                cp.wait_recv()
        for cp in first + passed:
            cp.wait_send()

    vm = pl.BlockSpec(memory_space=pltpu.VMEM)
    wt, wo = pl.pallas_call(
        body, name="gather_weights",
        out_shape=(jax.ShapeDtypeStruct((N_DEV, WIN_BLK, D), BF16), jax.ShapeDtypeStruct((N_DEV, WOUT_BLK, D), BF16)),
        in_specs=[vm, vm], out_specs=(vm, vm),
        scratch_shapes=[pltpu.VMEM((D, WIN_PAD), F32), pltpu.SemaphoreType.DMA((2, 7)), pltpu.SemaphoreType.DMA((2, 7))],
        compiler_params=_cparams(vmem=32 * 1024 * 1024),
    )(w_in_blk, w_out_blk)
    return wt.reshape(D_IN, D), wo.reshape(D, D)


def _in_proj(x, norm_w, wt):
    def body(x_ref, nw_ref, w_ref, *outs):
        piece_refs, rstd_ref = outs[:8], outs[8]
        xv = x_ref[...]
        r = lax.rsqrt(jnp.mean(xv * xv, axis=-1, keepdims=True) + NORM_EPS)
        h = _bf(xv * r * nw_ref[...])
        for i, ref in enumerate(piece_refs):
            w = w_ref[OFFS[i]:OFFS[i + 1], :]
            ref[...] = _dot_nt(h, w) if i < N_ROWMAJOR else _dot_nt(w, h)
        rstd_ref[...] = r

    row = lambda w: pl.BlockSpec((TM, w), lambda i: (i, 0))
    col = lambda w: pl.BlockSpec((w, TM), lambda i: (0, i))
    return pl.pallas_call(
        body, name="in_proj", grid=(T // TM,),
        in_specs=[row(D), pl.BlockSpec((1, D), lambda i: (0, 0)), pl.BlockSpec((D_IN, D), lambda i: (0, 0))],
        out_specs=tuple(row(w) for w in PIECES[:N_ROWMAJOR]) + tuple(col(w) for w in PIECES[N_ROWMAJOR:]) + (row(1),),
        out_shape=(tuple(jax.ShapeDtypeStruct((T, w), F32) for w in PIECES[:N_ROWMAJOR])
                   + tuple(jax.ShapeDtypeStruct((w, T), F32) for w in PIECES[N_ROWMAJOR:])
                   + (jax.ShapeDtypeStruct((T, 1), F32),)),
        compiler_params=_cparams(1),
    )(x, norm_w, wt)


def _group_norm(o):
    mu = jnp.mean(o, axis=-1, keepdims=True)
    var = jnp.mean((o - mu) * (o - mu), axis=-1, keepdims=True)
    rstd = lax.rsqrt(var + GN_EPS)
    return (o - mu) * rstd, rstd


def _retention_fwd(rq, rk, rv, rg, ret_norm_w, cos_t, sin_t, decay, zeta_b, xi_b, chunk_decay):
    def body(rq_ref, rk_ref, rv_ref, rg_ref, w_ref, cos_ref, sin_ref, decay_ref, zeta_ref, xi_ref,
             ro_ref, mix_ref, st_ref, state):
        n = pl.program_id(0)

        @pl.when(n == 0)
        def _():
            state[...] = jnp.zeros_like(state)

        cos, sin = cos_ref[...], sin_ref[...]
        q = rq_ref[...]
        q = q * cos + _swap_halves(q) * sin
        k = rk_ref[...]
        k = (k * cos + _swap_halves(k) * sin) * (RET_DK ** -0.5)
        for h in range(RET_H):
            qh = q[:, h * RET_DK:(h + 1) * RET_DK]
            kh = k[:, h * RET_DK:(h + 1) * RET_DK]
            vh = rv_ref[:, h * RET_DV:(h + 1) * RET_DV]
            s_t = state[h]
            st_ref[0, h] = s_t
            scores = _dot_nt(_bf(qh), _bf(kh)) * decay_ref[h]
            o = _dot(_bf(scores), _bf(vh)) + _dot_nt(_bf(qh * xi_ref[h]), _bf(s_t))
            state[h] = chunk_decay[h] * s_t + _dot(_bf(vh.T), _bf(kh * zeta_ref[h]))
            rn, _ = _group_norm(o)
            g = rg_ref[:, h * RET_DV:(h + 1) * RET_DV]
            ro_ref[:, h * RET_DV:(h + 1) * RET_DV] = o
            mix_ref[:, h * RET_DV:(h + 1) * RET_DV] = rn * w_ref[:, h * RET_DV:(h + 1) * RET_DV] * (g * _sigmoid(g))

    row = lambda w: pl.BlockSpec((CH, w), lambda n: (n, 0))
    const = lambda shape: pl.BlockSpec(shape, lambda n: (0,) * len(shape))
    return pl.pallas_call(
        body, name="retention_fwd", grid=(N_CH,),
        in_specs=[row(256), row(256), row(512), row(512), const((1, 512)), row(256), row(256),
                  const((RET_H, CH, CH)), const((RET_H, CH, RET_DK)), const((RET_H, CH, RET_DK))],
        out_specs=(row(512), row(512), pl.BlockSpec((1, RET_H, RET_DV, RET_DK), lambda n: (n, 0, 0, 0))),
        out_shape=(jax.ShapeDtypeStruct((T, 512), F32), jax.ShapeDtypeStruct((T, 512), F32),
                   jax.ShapeDtypeStruct((N_CH, RET_H, RET_DV, RET_DK), F32)),
        scratch_shapes=[pltpu.VMEM((RET_H, RET_DV, RET_DK), F32)],
        compiler_params=_cparams(1),
    )(rq, rk, rv, rg, ret_norm_w, cos_t, sin_t, decay, zeta_b, xi_b)


def _retention_bwd(rq, rk, rv, rg, ro, states, dmix, ret_norm_w, cos_t, sin_t, decay, zeta_b, xi_b, chunk_decay):
    def body(rq_ref, rk_ref, rv_ref, rg_ref, ro_ref, st_ref, dm_ref, w_ref, cos_ref, sin_ref, decay_ref, zeta_ref,
             xi_ref, dq_ref, dk_ref, dv_ref, dg_ref, dw_ref, gstate):
        i = pl.program_id(0)

        @pl.when(i == 0)
        def _():
            gstate[...] = jnp.zeros_like(gstate)
            dw_ref[...] = jnp.zeros_like(dw_ref)

        cos, sin = cos_ref[...], sin_ref[...]
        q = rq_ref[...]
        q = q * cos + _swap_halves(q) * sin
        k = rk_ref[...]
        k = (k * cos + _swap_halves(k) * sin) * (RET_DK ** -0.5)
        dqs, dks = [], []
        for h in range(RET_H):
            cols = slice(h * RET_DV, (h + 1) * RET_DV)
            qh = q[:, h * RET_DK:(h + 1) * RET_DK]
            kh = k[:, h * RET_DK:(h + 1) * RET_DK]
            vh = rv_ref[:, cols]
            g = rg_ref[:, cols]
            w = w_ref[:, cols]
            dm = dm_ref[:, cols]
            rn, rstd = _group_norm(ro_ref[:, cols])
            sig = _sigmoid(g)
            silu = g * sig
            dg_ref[:, cols] = dm * rn * w * (sig * (1.0 + g * (1.0 - sig)))
            dw_ref[:, cols] += jnp.sum(dm * silu * rn, axis=0, keepdims=True)
            drn = dm * silu * w
            do = rstd * (drn - jnp.mean(drn, axis=-1, keepdims=True)
                         - rn * jnp.mean(drn * rn, axis=-1, keepdims=True))
            do_b = _bf(do)
            s_t = st_ref[0, h]
            g_t = gstate[h]
            dec = decay_ref[h]
            scores = _dot_nt(_bf(qh), _bf(kh)) * dec
            dscores = _dot_nt(do_b, _bf(vh)) * dec
            qx = qh * xi_ref[h]
            kz = kh * zeta_ref[h]
            dqh = _dot(_bf(dscores), _bf(kh)) + _dot(do_b, _bf(s_t)) * xi_ref[h]
            dkh = _dot(_bf(dscores.T), _bf(qh)) + _dot(_bf(vh), _bf(g_t)) * zeta_ref[h]
            dv_ref[:, cols] = _dot(_bf(scores.T), do_b) + _dot_nt(_bf(kz), _bf(g_t))
            gstate[h] = chunk_decay[h] * g_t + _dot(_bf(do.T), _bf(qx))
            dqs.append(dqh)
            dks.append(dkh * (RET_DK ** -0.5))
        dq = jnp.concatenate(dqs, axis=1)
        dk = jnp.concatenate(dks, axis=1)
        dq_ref[...] = dq * cos + _swap_halves(dq * sin)
        dk_ref[...] = dk * cos + _swap_halves(dk * sin)

    row = lambda w: pl.BlockSpec((CH, w), lambda i: (N_CH - 1 - i, 0))
    const = lambda shape: pl.BlockSpec(shape, lambda i: (0,) * len(shape))
    return pl.pallas_call(
        body, name="retention_bwd", grid=(N_CH,),
        in_specs=[row(256), row(256), row(512), row(512), row(512),
                  pl.BlockSpec((1, RET_H, RET_DV, RET_DK), lambda i: (N_CH - 1 - i, 0, 0, 0)), row(512),
                  const((1, 512)), row(256), row(256),
                  const((RET_H, CH, CH)), const((RET_H, CH, RET_DK)), const((RET_H, CH, RET_DK))],
        out_specs=(row(256), row(256), row(512), row(512), const((1, 512))),
        out_shape=(jax.ShapeDtypeStruct((T, 256), F32), jax.ShapeDtypeStruct((T, 256), F32),
                   jax.ShapeDtypeStruct((T, 512), F32), jax.ShapeDtypeStruct((T, 512), F32),
                   jax.ShapeDtypeStruct((1, 512), F32)),
        scratch_shapes=[pltpu.VMEM((RET_H, RET_DV, RET_DK), F32)],
        compiler_params=_cparams(1),
    )(rq, rk, rv, rg, ro, states, dmix, ret_norm_w, cos_t, sin_t, decay, zeta_b, xi_b)


def _build_bias_t(bucket_ref, relb_ref, bias_ref):
    bt = bucket_ref[...]
    first = lax.broadcasted_iota(jnp.int32, bt.shape, 0) < CH
    for hq in range(SWA_H):
        b = jnp.full(bt.shape, NEG_INF, F32)
        for bk in range(N_BUCKETS):
            b = jnp.where(bt == bk, relb_ref[bk * SWA_H + hq], b)
        bias_ref[0, hq] = b
        bias_ref[1, hq] = jnp.where(first, NEG_INF, b)


def _rms_t(t, w_col):
    r = lax.rsqrt(jnp.mean(t * t, axis=0, keepdims=True) + NORM_EPS)
    return t * r * w_col, r


def _rms_t_bwd(dn, t, r, w_col):
    u = dn * w_col
    return r * u - t * (r * r * r) * jnp.mean(u * t, axis=0, keepdims=True), dn * t * r


def _norm_kv(k_t, kw):
    return jnp.concatenate([_rms_t(k_t[g * SWA_D:(g + 1) * SWA_D], kw)[0] for g in range(SWA_KV)], axis=0)


def _kv_slot(a, kv):
    z = jnp.zeros_like(a)
    return jnp.concatenate([a, z] if kv == 0 else [z, a], axis=0)


def _swa_probs_t(kb_rm, qs_pad_b, bias, sink):
    s = _dot(kb_rm, qs_pad_b) + bias
    m = jnp.maximum(jnp.max(s, axis=0, keepdims=True), sink)
    p = jnp.exp(s - m)
    e_sink = jnp.exp(sink - m)
    inv = 1.0 / (jnp.sum(p, axis=0, keepdims=True) + e_sink)
    return p * inv, e_sink * inv


def _swa_fwd(sq, sk, sv, sg, q_norm_w, k_norm_w, sinks, rel_bias_flat, bucket_t):
    def body(sq_ref, skp_ref, skc_ref, svp_ref, svc_ref, sg_ref, qw_ref, kw_ref, sinks_ref, relb_ref, bucket_ref,
             so_ref, mix_ref, bias_ref):
        n = pl.program_id(0)

        @pl.when(n == 0)
        def _():
            _build_bias_t(bucket_ref, relb_ref, bias_ref)

        var = (n == 0).astype(jnp.int32)
        qw, kw = qw_ref[...], kw_ref[...]
        kn_band = jnp.concatenate([_norm_kv(skp_ref[...], kw), _norm_kv(skc_ref[...], kw)], axis=1)
        kb_rm = _bf(kn_band.T)
        vband = _bf(jnp.concatenate([svp_ref[...], svc_ref[...]], axis=1))
        for hq in range(SWA_H):
            kv = hq // SWA_G
            rows = slice(hq * SWA_D, (hq + 1) * SWA_D)
            qn, _ = _rms_t(sq_ref[rows, :], qw)
            qs_pad_b = _bf(_kv_slot(qn * (SWA_D ** -0.5), kv))
            probs, _ = _swa_probs_t(kb_rm, qs_pad_b, bias_ref[var, hq], sinks_ref[hq])
            o = _dot(vband[kv * SWA_D:(kv + 1) * SWA_D], _bf(probs))
            g = sg_ref[rows, :]
            so_ref[rows, :] = o
            mix_ref[rows, :] = o * (g * _sigmoid(g))

    col = lambda w: pl.BlockSpec((w, CH), lambda n: (0, n))
    prev = lambda w: pl.BlockSpec((w, CH), lambda n: (0, jnp.maximum(n - 1, 0)))
    const = lambda shape: pl.BlockSpec(shape, lambda n: (0,) * len(shape))
    smem = pl.BlockSpec(memory_space=pltpu.SMEM)
    return pl.pallas_call(
        body, name="swa_fwd", grid=(N_CH,),
        in_specs=[col(512), prev(128), col(128), prev(128), col(128), col(512), const((SWA_D, 1)), const((SWA_D, 1)),
                  smem, smem, const((2 * CH, CH))],
        out_specs=(col(512), col(512)),
        out_shape=(jax.ShapeDtypeStruct((512, T), F32), jax.ShapeDtypeStruct((512, T), F32)),
        scratch_shapes=[pltpu.VMEM((2, SWA_H, 2 * CH, CH), F32)],
        compiler_params=_cparams(1),
    )(sq, sk, sk, sv, sv, sg, q_norm_w, k_norm_w, sinks, rel_bias_flat, bucket_t)


def _swa_bwd(sq, sk, sv, sg, so, dmix, q_norm_w, k_norm_w, sinks, rel_bias_flat, bucket_t):
    def body(sq_ref, skp_ref, skc_ref, svp_ref, svc_ref, sg_ref, so_ref, dm_ref, qw_ref, kw_ref, sinks_ref, relb_ref,
             bucket_ref, dq_ref, dg_ref, dk_ref, dv_ref, dbias_ref, dsink_ref, dqw_ref, dkw_ref,
             bias_ref, band_dk, band_dv, carry_dk, carry_dv):
        n = pl.program_id(0)

        @pl.when(n == 0)
        def _():
            _build_bias_t(bucket_ref, relb_ref, bias_ref)
            for ref in (dbias_ref, dsink_ref, dqw_ref, dkw_ref, carry_dk, carry_dv):
                ref[...] = jnp.zeros_like(ref)

        qw, kw = qw_ref[...], kw_ref[...]

        @pl.when(n < N_CH)
        def _():
            var = (n == 0).astype(jnp.int32)
            kn_band = jnp.concatenate([_norm_kv(skp_ref[...], kw), _norm_kv(skc_ref[...], kw)], axis=1)
            kb_rm = _bf(kn_band.T)
            kn_band_s = _bf(kn_band * (SWA_D ** -0.5))
            vband_f = jnp.concatenate([svp_ref[...], svc_ref[...]], axis=1)
            vb_rm = _bf(vband_f.T)
            dkn = [jnp.zeros((SWA_D, 2 * CH), F32) for _ in range(SWA_KV)]
            dvv = [jnp.zeros((SWA_D, 2 * CH), F32) for _ in range(SWA_KV)]
            for hq in range(SWA_H):
                kv = hq // SWA_G
                rows = slice(hq * SWA_D, (hq + 1) * SWA_D)
                q_t = sq_ref[rows, :]
                qn, rq = _rms_t(q_t, qw)
                qs_b = _bf(qn * (SWA_D ** -0.5))
                probs, p_sink = _swa_probs_t(kb_rm, _kv_slot(qs_b, kv), bias_ref[var, hq], sinks_ref[hq])
                g = sg_ref[rows, :]
                sig = _sigmoid(g)
                dm = dm_ref[rows, :]
                dg_ref[rows, :] = dm * so_ref[rows, :] * (sig * (1.0 + g * (1.0 - sig)))
                do_b = _bf(dm * (g * sig))
                dprobs = _dot(vb_rm, _kv_slot(do_b, kv))
                t = jnp.sum(probs * dprobs, axis=0, keepdims=True)
                dlog = probs * (dprobs - t)
                dsink_ref[hq:hq + 1, :] += -(p_sink * t)
                dbias_ref[hq] += dlog
                dlog_b = _bf(dlog)
                dvv[kv] = dvv[kv] + _dot_nt(do_b, _bf(probs))
                dkn[kv] = dkn[kv] + _dot_nt(qs_b, dlog_b)
                dqn = _dot(kn_band_s[kv * SWA_D:(kv + 1) * SWA_D], dlog_b)
                dq_t, dqw_terms = _rms_t_bwd(dqn, q_t, rq, qw)
                dq_ref[rows, :] = dq_t
                dqw_ref[...] += dqw_terms
            band_dk[...] = jnp.concatenate(dkn, axis=0)
            band_dv[...] = jnp.concatenate(dvv, axis=0)

        @pl.when(n == N_CH)
        def _():
            band_dk[...] = jnp.zeros_like(band_dk)
            band_dv[...] = jnp.zeros_like(band_dv)

        @pl.when(n >= 1)
        def _():
            dkn_prev = carry_dk[...] + band_dk[:, 0:CH]
            k_t = skp_ref[...]
            for kv in range(SWA_KV):
                rows = slice(kv * SWA_D, (kv + 1) * SWA_D)
                _, rk = _rms_t(k_t[rows], kw)
                dk_t, dkw_terms = _rms_t_bwd(dkn_prev[rows], k_t[rows], rk, kw)
                dk_ref[rows, :] = dk_t
                dkw_ref[...] += dkw_terms
            dv_ref[...] = carry_dv[...] + band_dv[:, 0:CH]

        carry_dk[...] = band_dk[:, CH:2 * CH]
        carry_dv[...] = band_dv[:, CH:2 * CH]

    col = lambda w: pl.BlockSpec((w, CH), lambda n: (0, jnp.minimum(n, N_CH - 1)))
    prev = lambda w: pl.BlockSpec((w, CH), lambda n: (0, jnp.maximum(n - 1, 0)))
    const = lambda shape: pl.BlockSpec(shape, lambda n: (0,) * len(shape))
    smem = pl.BlockSpec(memory_space=pltpu.SMEM)
    return pl.pallas_call(
        body, name="swa_bwd", grid=(N_CH + 1,),
        in_specs=[col(512), prev(128), col(128), prev(128), col(128), col(512), col(512), col(512),
                  const((SWA_D, 1)), const((SWA_D, 1)), smem, smem, const((2 * CH, CH))],
        out_specs=(col(512), col(512), prev(128), prev(128), const((SWA_H, 2 * CH, CH)), const((SWA_H, CH)),
                   const((SWA_D, CH)), const((SWA_D, CH))),
        out_shape=(jax.ShapeDtypeStruct((512, T), F32), jax.ShapeDtypeStruct((512, T), F32),
                   jax.ShapeDtypeStruct((128, T), F32), jax.ShapeDtypeStruct((128, T), F32),
                   jax.ShapeDtypeStruct((SWA_H, 2 * CH, CH), F32), jax.ShapeDtypeStruct((SWA_H, CH), F32),
                   jax.ShapeDtypeStruct((SWA_D, CH), F32), jax.ShapeDtypeStruct((SWA_D, CH), F32)),
        scratch_shapes=[pltpu.VMEM((2, SWA_H, 2 * CH, CH), F32),
                        pltpu.VMEM((128, 2 * CH), F32), pltpu.VMEM((128, 2 * CH), F32),
                        pltpu.VMEM((128, CH), F32), pltpu.VMEM((128, CH), F32)],
        compiler_params=_cparams(1),
    )(sq, sk, sk, sv, sv, sg, so, dmix, q_norm_w, k_norm_w, sinks, rel_bias_flat, bucket_t)


def _swa_small_grads(dbias, bucket_t, dsink_acc, dqw_acc, dkw_acc):
    def body(dbias_ref, bucket_ref, dsink_ref, dqw_ref, dkw_ref, relb_o, sink_o, qw_o, kw_o):
        bt = bucket_ref[...]
        row = lax.broadcasted_iota(jnp.int32, (N_BUCKETS, SWA_H), 0)
        col = lax.broadcasted_iota(jnp.int32, (N_BUCKETS, SWA_H), 1)
        lane = lax.broadcasted_iota(jnp.int32, (1, CH), 1)
        acc = jnp.zeros((N_BUCKETS, SWA_H), F32)
        sink = jnp.zeros((1, CH), F32)
        for hq in range(SWA_H):
            d = dbias_ref[hq]
            for bk in range(N_BUCKETS):
                s = jnp.sum(jnp.sum(jnp.where(bt == bk, d, 0.0), axis=0, keepdims=True), axis=1, keepdims=True)
                acc = acc + jnp.where((row == bk) & (col == hq), s, 0.0)
            sink = sink + jnp.where(lane == hq, jnp.sum(dsink_ref[hq:hq + 1, :], axis=1, keepdims=True), 0.0)
        relb_o[...] = acc
        sink_o[...] = sink
        for src, dst in ((dqw_ref, qw_o), (dkw_ref, kw_o)):
            padded = jnp.concatenate([src[...], jnp.zeros((CH - SWA_D, CH), F32)], axis=0)
            dst[...] = jnp.sum(padded.T, axis=0, keepdims=True)

    vm = pl.BlockSpec(memory_space=pltpu.VMEM)
    row128 = jax.ShapeDtypeStruct((1, CH), F32)
    return pl.pallas_call(body, name="swa_small_grads",
                          out_shape=(jax.ShapeDtypeStruct((N_BUCKETS, SWA_H), F32), row128, row128, row128),
                          in_specs=[vm] * 5, out_specs=(vm,) * 4, compiler_params=_cparams(),
                          )(dbias, bucket_t, dsink_acc, dqw_acc, dkw_acc)


def _out_proj(mix_r, mix_s_t, wo, x, target):
    def body(mr_ref, mst_ref, w_ref, x_ref, t_ref, loss_ref, dy_ref, dmr_ref, dmst_ref, gw_ref):
        i = pl.program_id(0)

        @pl.when(i == 0)
        def _():
            loss_ref[...] = jnp.zeros_like(loss_ref)
            gw_ref[...] = jnp.zeros_like(gw_ref)

        mr, mst = mr_ref[...], mst_ref[...]
        w_r, w_s = w_ref[0:512, :], w_ref[512:D, :]
        err = x_ref[...] + _dot(_bf(mr), w_r) + _dot(_bf(mst.T), w_s) - t_ref[...]
        loss_ref[...] += jnp.sum(jnp.sum(err * err, axis=1, keepdims=True), axis=0, keepdims=True)
        dy = err * (1.0 / D)
        dy_ref[...] = dy
        dy_b = _bf(dy)
        dmr_ref[...] = _dot_nt(dy_b, w_r)
        dmst_ref[...] = _dot_nt(w_s, dy_b)
        gw_ref[0:512, :] += _dot(_bf(mr.T), dy_b)
        gw_ref[512:D, :] += _dot(_bf(mst), dy_b)

    row = lambda w: pl.BlockSpec((TM, w), lambda i: (i, 0))
    col = lambda w: pl.BlockSpec((w, TM), lambda i: (0, i))
    const = lambda shape: pl.BlockSpec(shape, lambda i: (0,) * len(shape))
    return pl.pallas_call(
        body, name="out_proj", grid=(T // TM,),
        in_specs=[row(512), col(512), const((D, D)), row(D), row(D)],
        out_specs=(const((1, 1)), row(D), row(512), col(512), const((D, D))),
        out_shape=(jax.ShapeDtypeStruct((1, 1), F32), jax.ShapeDtypeStruct((T, D), F32),
                   jax.ShapeDtypeStruct((T, 512), F32), jax.ShapeDtypeStruct((512, T), F32),
                   jax.ShapeDtypeStruct((D, D), F32)),
        compiler_params=_cparams(1),
    )(mix_r, mix_s_t, wo, x, target)


def _in_proj_bwd(dpieces, x, rstd, norm_w, dy, wt):
    def body(*refs):
        piece_refs = refs[:8]
        x_ref, r_ref, nw_ref, dy_ref, w_ref, gx_ref, gw_ref, gnw_ref = refs[8:]
        i = pl.program_id(0)

        @pl.when(i == 0)
        def _():
            gw_ref[...] = jnp.zeros_like(gw_ref)
            gnw_ref[...] = jnp.zeros_like(gnw_ref)

        dp_r = jnp.concatenate([ref[...] for ref in piece_refs[:N_ROWMAJOR]], axis=1)
        dp_s = jnp.concatenate([ref[...] for ref in piece_refs[N_ROWMAJOR:]], axis=0)
        xv, r, nw = x_ref[...], r_ref[...], nw_ref[...]
        xr = xv * r
        h_b = _bf(xr * nw)
        dh = _dot(_bf(dp_r), w_ref[0:RET_W, :]) + _dot(_bf(dp_s.T), w_ref[RET_W:D_IN, :])
        gw_ref[0:RET_W, :] += _dot(_bf(dp_r.T), h_b)
        gw_ref[RET_W:D_IN, :] += _dot(_bf(dp_s), h_b)
        u = dh * nw
        gx_ref[...] = dy_ref[...] + r * u - xv * (r * r * r) * jnp.mean(u * xv, axis=-1, keepdims=True)
        gnw_ref[...] += jnp.sum(dh * xr, axis=0, keepdims=True)

    row = lambda w: pl.BlockSpec((TM, w), lambda i: (i, 0))
    col = lambda w: pl.BlockSpec((w, TM), lambda i: (0, i))
    const = lambda shape, **kw: pl.BlockSpec(shape, lambda i: (0,) * len(shape), **kw)
    once = dict(pipeline_mode=pl.Buffered(1))
    return pl.pallas_call(
        body, name="in_proj_bwd", grid=(T // TM,),
        in_specs=([row(w) for w in PIECES[:N_ROWMAJOR]] + [col(w) for w in PIECES[N_ROWMAJOR:]]
                  + [row(D), row(1), const((1, D)), row(D), const((D_IN, D), **once)]),
        out_specs=(row(D), const((D_IN, D), **once), const((1, D))),
        out_shape=(jax.ShapeDtypeStruct((T, D), F32), jax.ShapeDtypeStruct((D_IN, D), F32),
                   jax.ShapeDtypeStruct((1, D), F32)),
        compiler_params=_cparams(1),
    )(*dpieces, x, rstd, norm_w, dy, wt)


SMALL_SHAPES = ((1, D), (1, 512), (1, SWA_D), (1, SWA_D), (1, SWA_H), (N_BUCKETS, SWA_H))
LOSS_ROW = 5


def _reduce_and_update(gwt, gwo, sse, small_g, w_in, m_in, v_in, w_out, m_out, v_out, small_w, small_m, small_v):
    n_small = len(SMALL_SHAPES)

    def body(*refs):
        gwt_ref, gwo_ref, sse_ref = refs[:3]
        sg_refs = refs[3:3 + n_small]
        w_in_ref, m_in_ref, v_in_ref, w_out_ref, m_out_ref, v_out_ref = refs[3 + n_small:9 + n_small]
        pos = 9 + n_small
        sw_refs, sm_refs, sv_refs = (refs[pos + i * n_small:pos + (i + 1) * n_small] for i in range(3))
        pos += 3 * n_small
        g_in_o, d_in_o, nm_in_o, nv_in_o, g_out_o, d_out_o, nm_out_o, nv_out_o = refs[pos:pos + 8]
        pos += 8
        sg_o, sd_o, snm_o, snv_o = (refs[pos + i * n_small:pos + (i + 1) * n_small] for i in range(4))
        pos += 4 * n_small
        loss_o = refs[pos]
        (own_t, own_o, rcv_t, rcv_o, snd_t, snd_o, got_t, got_o, tab_a, tab_b, sum_a, sum_b, pad_ref, tr_ref,
         loc_sems, a_send, a_recv, b_send, b_recv, s_send, s_recv) = refs[pos + 1:]
        x, y, c = _mesh_pos()
        me = 4 * x + 2 * y + c
        sibling = (x, y, 1 - c)
        all_chips = [(0, 0), (0, 1), (1, 0), (1, 1)]
        others = [(1 - x, y), (x, 1 - y), (1 - x, 1 - y)]

        def blk(px, py, pc):
            return 4 * px + 2 * py + pc

        mine_a, mine_b = tab_a.at[me], tab_b.at[me]
        mine_a[...] = jnp.zeros_like(mine_a)
        mine_b[...] = jnp.zeros_like(mine_b)
        for r in range(n_small - 1):
            mine_a[r:r + 1, 0:sg_refs[r].shape[1]] = sg_refs[r][...]
        mine_a[LOSS_ROW:LOSS_ROW + 1, 0:1] = sse_ref[...]
        mine_b[:, 0:SWA_H] = sg_refs[n_small - 1][...]
        small_copies = []
        for k in range(1, N_DEV):
            px, py, pc = x ^ (k >> 2), y ^ ((k >> 1) & 1), c ^ (k & 1)
            for a, tab in enumerate((tab_a, tab_b)):
                cp = pltpu.make_async_remote_copy(src_ref=tab.at[me], dst_ref=tab.at[me], send_sem=s_send.at[a, k - 1],
                                                  recv_sem=s_recv.at[a, k - 1], device_id=(px, py, pc),
                                                  device_id_type=MESH)
                cp.start()
                small_copies.append(cp)

        stage_a, local = [], []
        for j, (px, py) in enumerate(all_chips):
            for a, (src, dst) in enumerate(((gwt_ref, rcv_t), (gwo_ref, rcv_o))):
                cp = pltpu.make_async_remote_copy(src_ref=src.at[blk(px, py, 1 - c)], dst_ref=dst.at[j],
                                                  send_sem=a_send.at[a, j], recv_sem=a_recv.at[a, j],
                                                  device_id=sibling, device_id_type=MESH)
                cp.start()
                stage_a.append(cp)
            for a, (src, dst) in enumerate(((gwt_ref, own_t), (gwo_ref, own_o))):
                cp = pltpu.make_async_copy(src.at[blk(px, py, c)], dst.at[j], loc_sems.at[a, j])
                cp.start()
                local.append(cp)
        for cp in local:
            cp.wait()
        for cp in stage_a:
            cp.wait_recv()

        stage_b = []
        for k, (px, py) in enumerate(others):
            j = 2 * px + py
            snd_t[k] = _bf(own_t[j] + rcv_t[j])
            snd_o[k] = _bf(own_o[j] + rcv_o[j])
            for a, (src, dst) in enumerate(((snd_t, got_t), (snd_o, got_o))):
                cp = pltpu.make_async_remote_copy(src_ref=src.at[k], dst_ref=dst.at[k], send_sem=b_send.at[a, k],
                                                  recv_sem=b_recv.at[a, k], device_id=(px, py, c), device_id_type=MESH)
                cp.start()
                stage_b.append(cp)
        jm = 2 * x + y
        tot_t = own_t[jm] + rcv_t[jm]
        tot_o = own_o[jm] + rcv_o[jm]
        for k in range(3):
            stage_b[2 * k].wait_recv()
            stage_b[2 * k + 1].wait_recv()
            tot_t = tot_t + got_t[k].astype(F32)
            tot_o = tot_o + got_o[k].astype(F32)

        d, nm, nv = _adamw(w_out_ref[...], tot_o, m_out_ref[...], v_out_ref[...])
        g_out_o[...], d_out_o[...], nm_out_o[...], nv_out_o[...] = tot_o, d, nm, nv

        pad_ref[0:WIN_BLK, :] = tot_t
        pad_ref[WIN_BLK:WIN_PAD, :] = jnp.zeros((WIN_PAD - WIN_BLK, D), F32)
        tr_ref[...] = pad_ref[...].T
        g = tr_ref[:, 0:WIN_BLK]
        d, nm, nv = _adamw(w_in_ref[...], g, m_in_ref[...], v_in_ref[...])
        g_in_o[...], d_in_o[...], nm_in_o[...], nv_in_o[...] = g, d, nm, nv

        for cp in small_copies:
            cp.wait_recv()
        tot_a, tot_b = tab_a[0], tab_b[0]
        for b in range(1, N_DEV):
            tot_a = tot_a + tab_a[b]
            tot_b = tot_b + tab_b[b]
        sum_a[...] = tot_a
        sum_b[...] = tot_b
        loss_o[...] = sum_a[LOSS_ROW:LOSS_ROW + 1, 0:1] * (0.5 / D)
        for r, (rows, lanes) in enumerate(SMALL_SHAPES):
            g = sum_a[r:r + 1, 0:lanes] if rows == 1 else sum_b[:, 0:lanes]
            d, nm, nv = _adamw(sw_refs[r][...], g, sm_refs[r][...], sv_refs[r][...])
            sg_o[r][...], sd_o[r][...], snm_o[r][...], snv_o[r][...] = g, d, nm, nv

        for cp in stage_a + stage_b + small_copies:
            cp.wait_send()

    vm = pl.BlockSpec(memory_space=pltpu.VMEM)
    hbm = pl.BlockSpec(memory_space=pl.ANY)
    win = jax.ShapeDtypeStruct((D, WIN_BLK), F32)
    wout = jax.ShapeDtypeStruct((WOUT_BLK, D), F32)
    smalls = tuple(jax.ShapeDtypeStruct(s, F32) for s in SMALL_SHAPES)
    outs = pl.pallas_call(
        body, name="reduce_and_update",
        in_specs=[hbm, hbm] + [vm] * (7 + 4 * n_small), out_specs=(vm,) * (9 + 4 * n_small),
        out_shape=(win,) * 4 + (wout,) * 4 + smalls * 4 + (jax.ShapeDtypeStruct((1, 1), F32),),
        scratch_shapes=[
            pltpu.VMEM((4, WIN_BLK, D), F32), pltpu.VMEM((4, WOUT_BLK, D), F32),
            pltpu.VMEM((4, WIN_BLK, D), F32), pltpu.VMEM((4, WOUT_BLK, D), F32),
            pltpu.VMEM((3, WIN_BLK, D), BF16), pltpu.VMEM((3, WOUT_BLK, D), BF16),
            pltpu.VMEM((3, WIN_BLK, D), BF16), pltpu.VMEM((3, WOUT_BLK, D), BF16),
            pltpu.VMEM((N_DEV, 8, D), F32), pltpu.VMEM((N_DEV, N_BUCKETS, CH), F32),
            pltpu.VMEM((8, D), F32), pltpu.VMEM((N_BUCKETS, CH), F32),
            pltpu.VMEM((WIN_PAD, D), F32), pltpu.VMEM((D, WIN_PAD), F32),
            pltpu.SemaphoreType.DMA((2, 4)),
            pltpu.SemaphoreType.DMA((2, 4)), pltpu.SemaphoreType.DMA((2, 4)),
            pltpu.SemaphoreType.DMA((2, 3)), pltpu.SemaphoreType.DMA((2, 3)),
            pltpu.SemaphoreType.DMA((2, 7)), pltpu.SemaphoreType.DMA((2, 7)),
        ],
        compiler_params=_cparams(),
    )(gwt, gwo, sse, *small_g, w_in, m_in, v_in, w_out, m_out, v_out, *small_w, *small_m, *small_v)
    big, rest = outs[:8], outs[8:]
    return big, [rest[i * n_small:(i + 1) * n_small] for i in range(4)], rest[4 * n_small]


def _small_rows(norm_w, ret_norm_w, q_norm_w, k_norm_w, sinks, rel_bias):
    return (norm_w.reshape(1, D), ret_norm_w.reshape(1, 512), q_norm_w.reshape(1, SWA_D), k_norm_w.reshape(1, SWA_D),
            sinks.reshape(1, SWA_H), rel_bias)


def _small_leaves(rows):
    return (rows[0].reshape(D), rows[1].reshape(512), rows[2].reshape(SWA_D), rows[3].reshape(SWA_D),
            rows[4].reshape(SWA_H), rows[5])


def kernel(x, norm_w, w_in, ret_norm_w, q_norm_w, k_norm_w, sinks, rel_bias, w_out, loss_target, m_norm_w, m_w_in, m_ret_norm_w, m_q_norm_w, m_k_norm_w, m_sinks, m_rel_bias, m_w_out, v_norm_w, v_w_in, v_ret_norm_w, v_q_norm_w, v_k_norm_w, v_sinks, v_rel_bias, v_w_out):
    x2 = x.reshape(T, D)
    target = loss_target.reshape(T, D)
    nw = norm_w.reshape(1, D)
    rnw = ret_norm_w.reshape(1, 512)
    qnw_col = q_norm_w.reshape(SWA_D, 1)
    knw_col = k_norm_w.reshape(SWA_D, 1)
    relb = rel_bias.reshape(N_BUCKETS * SWA_H)
    cos_t, sin_t = _rotary_tables()
    decay, zeta_b, xi_b, chunk_decay = _retention_tables()
    bucket_t = _bucket_table_t()

    wt, wo = _gather_weights(w_in, w_out)

    rq, rk, rv, rg, sq, sk, sv, sg, rstd = _in_proj(x2, nw, wt)
    ro, mix_r, states = _retention_fwd(rq, rk, rv, rg, rnw, cos_t, sin_t, decay, zeta_b, xi_b, chunk_decay)
    so, mix_s = _swa_fwd(sq, sk, sv, sg, qnw_col, knw_col, sinks, relb, bucket_t)
    sse, dy, dmix_r, dmix_s, gwo = _out_proj(mix_r, mix_s, wo, x2, target)

    drq, drk, drv, drg, d_rnw = _retention_bwd(rq, rk, rv, rg, ro, states, dmix_r, rnw, cos_t, sin_t, decay, zeta_b,
                                               xi_b, chunk_decay)
    dsq, dsg, dsk, dsv, dbias, dsink_acc, dqw_acc, dkw_acc = _swa_bwd(sq, sk, sv, sg, so, dmix_s, qnw_col, knw_col,
                                                                      sinks, relb, bucket_t)
    d_relb, d_sinks, d_qnw, d_knw = _swa_small_grads(dbias, bucket_t, dsink_acc, dqw_acc, dkw_acc)
    grad_x, gwt, d_nw = _in_proj_bwd((drq, drk, drv, drg, dsq, dsk, dsv, dsg), x2, rstd, nw, dy, wt)

    small_g = (d_nw, d_rnw, d_qnw, d_knw, d_sinks, d_relb)
    small_w = _small_rows(norm_w, ret_norm_w, q_norm_w, k_norm_w, sinks, rel_bias)
    small_m = _small_rows(m_norm_w, m_ret_norm_w, m_q_norm_w, m_k_norm_w, m_sinks, m_rel_bias)
    small_v = _small_rows(v_norm_w, v_ret_norm_w, v_q_norm_w, v_k_norm_w, v_sinks, v_rel_bias)

    big, small, loss = _reduce_and_update(
        gwt.reshape(N_DEV, WIN_BLK, D), gwo.reshape(N_DEV, WOUT_BLK, D), sse, small_g,
        w_in, m_w_in, v_w_in, w_out, m_w_out, v_w_out, small_w, small_m, small_v)

    def leaves(i):
        a = _small_leaves(small[i])
        return (a[0], big[i], a[1], a[2], a[3], a[4], a[5], big[4 + i])

    return (loss.reshape(()), grad_x.reshape(1, T, D), *leaves(0), *leaves(1), *leaves(2), *leaves(3))
```

```python
import numpy as np
import jax
import jax.numpy as jnp
from jax import lax
from jax.experimental import pallas as pl
from jax.experimental.pallas import tpu as pltpu

F32 = jnp.float32
BF16 = jnp.bfloat16
MESH = pl.DeviceIdType.MESH

T = 2048
D = 1024
D_IN = 2816
N_DEV = 8
WIN_BLK = D_IN // N_DEV
WIN_PAD = 384
WOUT_BLK = D // N_DEV
CH = 128
N_CH = T // CH
RET_H, RET_DK, RET_DV = 4, 64, 128
SWA_H, SWA_KV, SWA_D, SWA_G = 8, 2, 64, 4
N_BUCKETS = 32
NORM_EPS = 1e-6
GN_EPS = 1e-5
NEG_INF = -1e30
PIECES = (256, 256, 512, 512, 512, 128, 128, 512)
OFFS = tuple(int(v) for v in np.cumsum((0,) + PIECES))
N_ROWMAJOR = 4
RET_W = OFFS[N_ROWMAJOR]
SWA_W = D_IN - RET_W
TM = 256

ADAM_LR, ADAM_B1, ADAM_B2, ADAM_EPS, ADAM_WD, ADAM_STEP = 0.001, 0.9, 0.999, 1e-08, 0.01, 10

VMEM_LIMIT = 56 * 1024 * 1024


def _cparams(n_grid=0, vmem=VMEM_LIMIT):
    sem = ("arbitrary",) * n_grid if n_grid else None
    return pltpu.CompilerParams(dimension_semantics=sem, vmem_limit_bytes=vmem)


def _dot(a, b):
    return jnp.dot(a, b, preferred_element_type=F32)


def _dot_nt(a, b):
    return lax.dot_general(a, b, (((1,), (1,)), ((), ())), preferred_element_type=F32)


def _bf(a):
    return a.astype(BF16)


def _sigmoid(x):
    return 1.0 / (1.0 + jnp.exp(-x))


def _adamw(w, g, m, v):
    m = ADAM_B1 * m + (1.0 - ADAM_B1) * g
    v = ADAM_B2 * v + (1.0 - ADAM_B2) * (g * g)
    m_hat = m / (1.0 - ADAM_B1 ** ADAM_STEP)
    v_hat = v / (1.0 - ADAM_B2 ** ADAM_STEP)
    delta = -ADAM_LR * (m_hat / (jnp.sqrt(v_hat) + ADAM_EPS) + ADAM_WD * w)
    return delta, m, v


def _rotary_tables():
    half = RET_DK // 2
    inv_freq = np.float32(10000.0) ** (-np.arange(half, dtype=np.float32) / np.float32(half))
    ang = np.arange(T, dtype=np.float32)[:, None] * inv_freq[None, :]
    cos, sin = np.cos(ang).astype(np.float32), np.sin(ang).astype(np.float32)
    cos64 = np.concatenate([cos, cos], axis=-1)
    sin64 = np.concatenate([-sin, sin], axis=-1)
    return np.tile(cos64, (1, RET_H)), np.tile(sin64, (1, RET_H))


def _retention_tables():
    gamma = (1.0 - np.exp2(-5.0 - np.arange(RET_H, dtype=np.float32))).astype(np.float32)
    log_g = np.log(gamma).astype(np.float32)
    i = np.arange(CH, dtype=np.float32)
    diff = i[:, None] - i[None, :]
    decay = np.where(diff >= 0, np.exp(log_g[:, None, None] * np.maximum(diff, 0.0)), 0.0).astype(np.float32)
    zeta = np.exp(log_g[:, None] * (CH - 1.0 - i)).astype(np.float32)
    xi = np.exp(log_g[:, None] * (i + 1.0)).astype(np.float32)
    chunk_decay = [float(v) for v in np.exp(log_g * np.float32(CH)).astype(np.float32)]
    zeta_b = np.ascontiguousarray(np.broadcast_to(zeta[:, :, None], (RET_H, CH, RET_DK)))
    xi_b = np.ascontiguousarray(np.broadcast_to(xi[:, :, None], (RET_H, CH, RET_DK)))
    return decay, zeta_b, xi_b, chunk_decay


def _bucket_table_t():
    qi = np.arange(CH)[:, None]
    kj = np.arange(2 * CH)[None, :]
    dist = qi + CH - kj
    n = np.maximum(dist, 0)
    max_exact = N_BUCKETS // 2
    nf = np.maximum(n, 1).astype(np.float32)
    large = max_exact + (np.log(nf / np.float32(max_exact)) / np.float32(np.log(CH / max_exact))
                         * np.float32(N_BUCKETS - max_exact)).astype(np.int32)
    large = np.minimum(large, N_BUCKETS - 1)
    bucket = np.where(n < max_exact, n, large)
    return np.ascontiguousarray(np.where((dist >= 0) & (dist < CH), bucket, -1).astype(np.int32).T)


def _swap_halves(t):
    lane = lax.broadcasted_iota(jnp.int32, t.shape, 1)
    n = t.shape[1]
    return jnp.where((lane % RET_DK) < RET_DK // 2, pltpu.roll(t, n - RET_DK // 2, 1), pltpu.roll(t, RET_DK // 2, 1))


def _mesh_pos():
    return lax.axis_index("x"), lax.axis_index("y"), lax.axis_index("c")


def _gather_weights(w_in_blk, w_out_blk):
    def body(win_ref, wout_ref, wt_ref, wo_ref, pad_ref, send_sems, recv_sems):
        x, y, c = _mesh_pos()
        me = 4 * x + 2 * y + c
        sibling = (x, y, 1 - c)
        chips = [(1 - x, y), (x, 1 - y), (1 - x, 1 - y)]

        pad_ref[...] = jnp.zeros_like(pad_ref)
        pad_ref[:, 0:WIN_BLK] = win_ref[...]
        wt_ref[me] = _bf(pad_ref[...].T[0:WIN_BLK, :])
        wo_ref[me] = _bf(wout_ref[...])

        def blk(px, py, pc):
            return 4 * px + 2 * py + pc

        def copies(k, b, to):
            return [pltpu.make_async_remote_copy(src_ref=ref.at[b], dst_ref=ref.at[b],
                                                 send_sem=send_sems.at[a, k], recv_sem=recv_sems.at[a, k],
                                                 device_id=to, device_id_type=MESH)
                    for a, ref in enumerate((wt_ref, wo_ref))]

        first = copies(0, me, sibling)
        for j, chip in enumerate(chips):
            first += copies(1 + j, me, (*chip, c))
        for cp in first:
            cp.start()
        passed = []
        for j, chip in enumerate(chips):
            for cp in copies(1 + j, blk(*chip, c), (x, y, c)):
                cp.wait_recv()
            fwd = copies(4 + j, blk(*chip, c), sibling)
            for cp in fwd:
                cp.start()
            passed += fwd
        for cp in copies(0, blk(x, y, 1 - c), (x, y, c)):
            cp.wait_recv()
        for j, chip in enumerate(chips):
            for cp in copies(4 + j, blk(*chip, 1 - c), (x, y, c)):
                cp.wait_recv()
        for cp in first + passed:
            cp.wait_send()

    vm = pl.BlockSpec(memory_space=pltpu.VMEM)
    wt, wo = pl.pallas_call(
        body, name="gather_weights",
        out_shape=(jax.ShapeDtypeStruct((N_DEV, WIN_BLK, D), BF16), jax.ShapeDtypeStruct((N_DEV, WOUT_BLK, D), BF16)),
        in_specs=[vm, vm], out_specs=(vm, vm),
        scratch_shapes=[pltpu.VMEM((D, WIN_PAD), F32), pltpu.SemaphoreType.DMA((2, 7)), pltpu.SemaphoreType.DMA((2, 7))],
        compiler_params=_cparams(vmem=32 * 1024 * 1024),
    )(w_in_blk, w_out_blk)
    return wt.reshape(D_IN, D), wo.reshape(D, D)


def _in_proj(x, norm_w, wt):
    def body(x_ref, nw_ref, w_ref, *outs):
        piece_refs, rstd_ref = outs[:8], outs[8]
        xv = x_ref[...]
        r = lax.rsqrt(jnp.mean(xv * xv, axis=-1, keepdims=True) + NORM_EPS)
        h = _bf(xv * r * nw_ref[...])
        for i, ref in enumerate(piece_refs):
            w = w_ref[OFFS[i]:OFFS[i + 1], :]
            ref[...] = _dot_nt(h, w) if i < N_ROWMAJOR else _dot_nt(w, h)
        rstd_ref[...] = r

    row = lambda w: pl.BlockSpec((TM, w), lambda i: (i, 0))
    col = lambda w: pl.BlockSpec((w, TM), lambda i: (0, i))
    return pl.pallas_call(
        body, name="in_proj", grid=(T // TM,),
        in_specs=[row(D), pl.BlockSpec((1, D), lambda i: (0, 0)), pl.BlockSpec((D_IN, D), lambda i: (0, 0))],
        out_specs=tuple(row(w) for w in PIECES[:N_ROWMAJOR]) + tuple(col(w) for w in PIECES[N_ROWMAJOR:]) + (row(1),),
        out_shape=(tuple(jax.ShapeDtypeStruct((T, w), F32) for w in PIECES[:N_ROWMAJOR])
                   + tuple(jax.ShapeDtypeStruct((w, T), F32) for w in PIECES[N_ROWMAJOR:])
                   + (jax.ShapeDtypeStruct((T, 1), F32),)),
        compiler_params=_cparams(1),
    )(x, norm_w, wt)


def _group_norm(o):
    mu = jnp.mean(o, axis=-1, keepdims=True)
    var = jnp.mean((o - mu) * (o - mu), axis=-1, keepdims=True)
    rstd = lax.rsqrt(var + GN_EPS)
    return (o - mu) * rstd, rstd


def _retention_fwd(rq, rk, rv, rg, ret_norm_w, cos_t, sin_t, decay, zeta_b, xi_b, chunk_decay):
    def body(rq_ref, rk_ref, rv_ref, rg_ref, w_ref, cos_ref, sin_ref, decay_ref, zeta_ref, xi_ref,
             ro_ref, mix_ref, st_ref, state):
        n = pl.program_id(0)

        @pl.when(n == 0)
        def _():
            state[...] = jnp.zeros_like(state)

        cos, sin = cos_ref[...], sin_ref[...]
        q = rq_ref[...]
        q = q * cos + _swap_halves(q) * sin
        k = rk_ref[...]
        k = (k * cos + _swap_halves(k) * sin) * (RET_DK ** -0.5)
        for h in range(RET_H):
            qh = q[:, h * RET_DK:(h + 1) * RET_DK]
            kh = k[:, h * RET_DK:(h + 1) * RET_DK]
            vh = rv_ref[:, h * RET_DV:(h + 1) * RET_DV]
            s_t = state[h]
            st_ref[0, h] = s_t
            scores = _dot_nt(_bf(qh), _bf(kh)) * decay_ref[h]
            o = _dot(_bf(scores), _bf(vh)) + _dot_nt(_bf(qh * xi_ref[h]), _bf(s_t))
            state[h] = chunk_decay[h] * s_t + _dot(_bf(vh.T), _bf(kh * zeta_ref[h]))
            rn, _ = _group_norm(o)
            g = rg_ref[:, h * RET_DV:(h + 1) * RET_DV]
            ro_ref[:, h * RET_DV:(h + 1) * RET_DV] = o
            mix_ref[:, h * RET_DV:(h + 1) * RET_DV] = rn * w_ref[:, h * RET_DV:(h + 1) * RET_DV] * (g * _sigmoid(g))

    row = lambda w: pl.BlockSpec((CH, w), lambda n: (n, 0))
    const = lambda shape: pl.BlockSpec(shape, lambda n: (0,) * len(shape))
    return pl.pallas_call(
        body, name="retention_fwd", grid=(N_CH,),
        in_specs=[row(256), row(256), row(512), row(512), const((1, 512)), row(256), row(256),
                  const((RET_H, CH, CH)), const((RET_H, CH, RET_DK)), const((RET_H, CH, RET_DK))],
        out_specs=(row(512), row(512), pl.BlockSpec((1, RET_H, RET_DV, RET_DK), lambda n: (n, 0, 0, 0))),
        out_shape=(jax.ShapeDtypeStruct((T, 512), F32), jax.ShapeDtypeStruct((T, 512), F32),
                   jax.ShapeDtypeStruct((N_CH, RET_H, RET_DV, RET_DK), F32)),
        scratch_shapes=[pltpu.VMEM((RET_H, RET_DV, RET_DK), F32)],
        compiler_params=_cparams(1),
    )(rq, rk, rv, rg, ret_norm_w, cos_t, sin_t, decay, zeta_b, xi_b)


def _retention_bwd(rq, rk, rv, rg, ro, states, dmix, ret_norm_w, cos_t, sin_t, decay, zeta_b, xi_b, chunk_decay):
    def body(rq_ref, rk_ref, rv_ref, rg_ref, ro_ref, st_ref, dm_ref, w_ref, cos_ref, sin_ref, decay_ref, zeta_ref,
             xi_ref, dq_ref, dk_ref, dv_ref, dg_ref, dw_ref, gstate):
        i = pl.program_id(0)

        @pl.when(i == 0)
        def _():
            gstate[...] = jnp.zeros_like(gstate)
            dw_ref[...] = jnp.zeros_like(dw_ref)

        cos, sin = cos_ref[...], sin_ref[...]
        q = rq_ref[...]
        q = q * cos + _swap_halves(q) * sin
        k = rk_ref[...]
        k = (k * cos + _swap_halves(k) * sin) * (RET_DK ** -0.5)
        dqs, dks = [], []
        for h in range(RET_H):
            cols = slice(h * RET_DV, (h + 1) * RET_DV)
            qh = q[:, h * RET_DK:(h + 1) * RET_DK]
            kh = k[:, h * RET_DK:(h + 1) * RET_DK]
            vh = rv_ref[:, cols]
            g = rg_ref[:, cols]
            w = w_ref[:, cols]
            dm = dm_ref[:, cols]
            rn, rstd = _group_norm(ro_ref[:, cols])
            sig = _sigmoid(g)
            silu = g * sig
            dg_ref[:, cols] = dm * rn * w * (sig * (1.0 + g * (1.0 - sig)))
            dw_ref[:, cols] += jnp.sum(dm * silu * rn, axis=0, keepdims=True)
            drn = dm * silu * w
            do = rstd * (drn - jnp.mean(drn, axis=-1, keepdims=True)
                         - rn * jnp.mean(drn * rn, axis=-1, keepdims=True))
            do_b = _bf(do)
            s_t = st_ref[0, h]
            g_t = gstate[h]
            dec = decay_ref[h]
            scores = _dot_nt(_bf(qh), _bf(kh)) * dec
            dscores = _dot_nt(do_b, _bf(vh)) * dec
            qx = qh * xi_ref[h]
            kz = kh * zeta_ref[h]
            dqh = _dot(_bf(dscores), _bf(kh)) + _dot(do_b, _bf(s_t)) * xi_ref[h]
            dkh = _dot(_bf(dscores.T), _bf(qh)) + _dot(_bf(vh), _bf(g_t)) * zeta_ref[h]
            dv_ref[:, cols] = _dot(_bf(scores.T), do_b) + _dot_nt(_bf(kz), _bf(g_t))
            gstate[h] = chunk_decay[h] * g_t + _dot(_bf(do.T), _bf(qx))
            dqs.append(dqh)
            dks.append(dkh * (RET_DK ** -0.5))
        dq = jnp.concatenate(dqs, axis=1)
        dk = jnp.concatenate(dks, axis=1)
        dq_ref[...] = dq * cos + _swap_halves(dq * sin)
        dk_ref[...] = dk * cos + _swap_halves(dk * sin)

    row = lambda w: pl.BlockSpec((CH, w), lambda i: (N_CH - 1 - i, 0))
    const = lambda shape: pl.BlockSpec(shape, lambda i: (0,) * len(shape))
    return pl.pallas_call(
        body, name="retention_bwd", grid=(N_CH,),
        in_specs=[row(256), row(256), row(512), row(512), row(512),
                  pl.BlockSpec((1, RET_H, RET_DV, RET_DK), lambda i: (N_CH - 1 - i, 0, 0, 0)), row(512),
                  const((1, 512)), row(256), row(256),
                  const((RET_H, CH, CH)), const((RET_H, CH, RET_DK)), const((RET_H, CH, RET_DK))],
        out_specs=(row(256), row(256), row(512), row(512), const((1, 512))),
        out_shape=(jax.ShapeDtypeStruct((T, 256), F32), jax.ShapeDtypeStruct((T, 256), F32),
                   jax.ShapeDtypeStruct((T, 512), F32), jax.ShapeDtypeStruct((T, 512), F32),
                   jax.ShapeDtypeStruct((1, 512), F32)),
        scratch_shapes=[pltpu.VMEM((RET_H, RET_DV, RET_DK), F32)],
        compiler_params=_cparams(1),
    )(rq, rk, rv, rg, ro, states, dmix, ret_norm_w, cos_t, sin_t, decay, zeta_b, xi_b)


HQ_LANES = SWA_H * CH


def _head_lanes(hq):
    return slice(hq * CH, (hq + 1) * CH)


def _build_bias_t(bucket_ref, relb_ref, sinks_ref, bias_ref, sink_row):
    bt = bucket_ref[...]
    first = lax.broadcasted_iota(jnp.int32, bt.shape, 0) < CH
    for hq in range(SWA_H):
        b = jnp.full(bt.shape, NEG_INF, F32)
        for bk in range(N_BUCKETS):
            b = jnp.where(bt == bk, relb_ref[bk * SWA_H + hq], b)
        bias_ref[0, :, _head_lanes(hq)] = b
        bias_ref[1, :, _head_lanes(hq)] = jnp.where(first, NEG_INF, b)
        sink_row[:, _head_lanes(hq)] = jnp.full((1, CH), sinks_ref[hq], F32)


def _rms_t(t, w_col):
    r = lax.rsqrt(jnp.mean(t * t, axis=0, keepdims=True) + NORM_EPS)
    return t * r * w_col, r


def _rms_t_bwd(dn, t, r, w_col):
    u = dn * w_col
    return r * u - t * (r * r * r) * jnp.mean(u * t, axis=0, keepdims=True), dn * t * r


def _norm_kv(k_t, kw):
    return jnp.concatenate([_rms_t(k_t[g * SWA_D:(g + 1) * SWA_D], kw)[0] for g in range(SWA_KV)], axis=0)


def _kv_slot(a, kv):
    z = jnp.zeros_like(a)
    return jnp.concatenate([a, z] if kv == 0 else [z, a], axis=0)


def _softmax_t(s, sink):
    m = jnp.maximum(jnp.max(s, axis=0, keepdims=True), sink)
    p = jnp.exp(s - m)
    e_sink = jnp.exp(sink - m)
    inv = 1.0 / (jnp.sum(p, axis=0, keepdims=True) + e_sink)
    return p * inv, e_sink * inv


def _swa_fwd(sq, sk, sv, sg, q_norm_w, k_norm_w, sinks, rel_bias_flat, bucket_t):
    def body(sq_ref, skp_ref, skc_ref, svp_ref, svc_ref, sg_ref, qw_ref, kw_ref, sinks_ref, relb_ref, bucket_ref,
             so_ref, mix_ref, bias_ref, sink_row):
        n = pl.program_id(0)

        @pl.when(n == 0)
        def _():
            _build_bias_t(bucket_ref, relb_ref, sinks_ref, bias_ref, sink_row)

        var = (n == 0).astype(jnp.int32)
        qw, kw = qw_ref[...], kw_ref[...]
        kn_band = jnp.concatenate([_norm_kv(skp_ref[...], kw), _norm_kv(skc_ref[...], kw)], axis=1)
        kb_rm = _bf(kn_band.T)
        vband = _bf(jnp.concatenate([svp_ref[...], svc_ref[...]], axis=1))
        q_all = jnp.concatenate(
            [_bf(_kv_slot(_rms_t(sq_ref[hq * SWA_D:(hq + 1) * SWA_D, :], qw)[0] * (SWA_D ** -0.5), hq // SWA_G))
             for hq in range(SWA_H)], axis=1)
        probs, _ = _softmax_t(_dot(kb_rm, q_all) + bias_ref[var], sink_row[...])
        probs_b = _bf(probs)
        for kv in range(SWA_KV):
            o = _dot(vband[kv * SWA_D:(kv + 1) * SWA_D], probs_b[:, kv * SWA_G * CH:(kv + 1) * SWA_G * CH])
            for j in range(SWA_G):
                rows = slice((kv * SWA_G + j) * SWA_D, (kv * SWA_G + j + 1) * SWA_D)
                oh = o[:, j * CH:(j + 1) * CH]
                g = sg_ref[rows, :]
                so_ref[rows, :] = oh
                mix_ref[rows, :] = oh * (g * _sigmoid(g))

    col = lambda w: pl.BlockSpec((w, CH), lambda n: (0, n))
    prev = lambda w: pl.BlockSpec((w, CH), lambda n: (0, jnp.maximum(n - 1, 0)))
    const = lambda shape: pl.BlockSpec(shape, lambda n: (0,) * len(shape))
    smem = pl.BlockSpec(memory_space=pltpu.SMEM)
    return pl.pallas_call(
        body, name="swa_fwd", grid=(N_CH,),
        in_specs=[col(512), prev(128), col(128), prev(128), col(128), col(512), const((SWA_D, 1)), const((SWA_D, 1)),
                  smem, smem, const((2 * CH, CH))],
        out_specs=(col(512), col(512)),
        out_shape=(jax.ShapeDtypeStruct((512, T), F32), jax.ShapeDtypeStruct((512, T), F32)),
        scratch_shapes=[pltpu.VMEM((2, 2 * CH, HQ_LANES), F32), pltpu.VMEM((1, HQ_LANES), F32)],
        compiler_params=_cparams(1),
    )(sq, sk, sk, sv, sv, sg, q_norm_w, k_norm_w, sinks, rel_bias_flat, bucket_t)


def _swa_bwd(sq, sk, sv, sg, so, dmix, q_norm_w, k_norm_w, sinks, rel_bias_flat, bucket_t):
    def body(sq_ref, skp_ref, skc_ref, svp_ref, svc_ref, sg_ref, so_ref, dm_ref, qw_ref, kw_ref, sinks_ref, relb_ref,
             bucket_ref, dq_ref, dg_ref, dk_ref, dv_ref, dbias_ref, dsink_ref, dqw_ref, dkw_ref,
             bias_ref, sink_row, band_dk, band_dv, carry_dk, carry_dv):
        n = pl.program_id(0)

        @pl.when(n == 0)
        def _():
            _build_bias_t(bucket_ref, relb_ref, sinks_ref, bias_ref, sink_row)
            for ref in (dbias_ref, dsink_ref, dqw_ref, dkw_ref, carry_dk, carry_dv):
                ref[...] = jnp.zeros_like(ref)

        qw, kw = qw_ref[...], kw_ref[...]

        @pl.when(n < N_CH)
        def _():
            var = (n == 0).astype(jnp.int32)
            kn_band = jnp.concatenate([_norm_kv(skp_ref[...], kw), _norm_kv(skc_ref[...], kw)], axis=1)
            kb_rm = _bf(kn_band.T)
            kn_band_s = _bf(kn_band * (SWA_D ** -0.5))
            vband_f = jnp.concatenate([svp_ref[...], svc_ref[...]], axis=1)
            vb_rm = _bf(vband_f.T)
            q_raw, q_rstd, qs_b, do_b = [], [], [], []
            for hq in range(SWA_H):
                rows = slice(hq * SWA_D, (hq + 1) * SWA_D)
                q_t = sq_ref[rows, :]
                qn, rq = _rms_t(q_t, qw)
                g = sg_ref[rows, :]
                sig = _sigmoid(g)
                dm = dm_ref[rows, :]
                dg_ref[rows, :] = dm * so_ref[rows, :] * (sig * (1.0 + g * (1.0 - sig)))
                q_raw.append(q_t)
                q_rstd.append(rq)
                qs_b.append(_bf(qn * (SWA_D ** -0.5)))
                do_b.append(_bf(dm * (g * sig)))
            q_all = jnp.concatenate([_kv_slot(qs_b[hq], hq // SWA_G) for hq in range(SWA_H)], axis=1)
            do_all = jnp.concatenate([_kv_slot(do_b[hq], hq // SWA_G) for hq in range(SWA_H)], axis=1)
            probs, p_sink = _softmax_t(_dot(kb_rm, q_all) + bias_ref[var], sink_row[...])
            dprobs = _dot(vb_rm, do_all)
            t = jnp.sum(probs * dprobs, axis=0, keepdims=True)
            dlog = probs * (dprobs - t)
            dsink_ref[...] += -(p_sink * t)
            dbias_ref[...] += dlog
            dlog_b, probs_b = _bf(dlog), _bf(probs)
            dkn, dvv = [], []
            for kv in range(SWA_KV):
                heads = range(kv * SWA_G, (kv + 1) * SWA_G)
                lanes = slice(kv * SWA_G * CH, (kv + 1) * SWA_G * CH)
                dvv.append(_dot_nt(jnp.concatenate([do_b[hq] for hq in heads], axis=1), probs_b[:, lanes]))
                dkn.append(_dot_nt(jnp.concatenate([qs_b[hq] for hq in heads], axis=1), dlog_b[:, lanes]))
                dqn = _dot(kn_band_s[kv * SWA_D:(kv + 1) * SWA_D], dlog_b[:, lanes])
                for j, hq in enumerate(heads):
                    dq_t, dqw_terms = _rms_t_bwd(dqn[:, j * CH:(j + 1) * CH], q_raw[hq], q_rstd[hq], qw)
                    dq_ref[hq * SWA_D:(hq + 1) * SWA_D, :] = dq_t
                    dqw_ref[...] += dqw_terms
            band_dk[...] = jnp.concatenate(dkn, axis=0)
            band_dv[...] = jnp.concatenate(dvv, axis=0)

        @pl.when(n == N_CH)
        def _():
            band_dk[...] = jnp.zeros_like(band_dk)
            band_dv[...] = jnp.zeros_like(band_dv)

        @pl.when(n >= 1)
        def _():
            dkn_prev = carry_dk[...] + band_dk[:, 0:CH]
            k_t = skp_ref[...]
            for kv in range(SWA_KV):
                rows = slice(kv * SWA_D, (kv + 1) * SWA_D)
                _, rk = _rms_t(k_t[rows], kw)
                dk_t, dkw_terms = _rms_t_bwd(dkn_prev[rows], k_t[rows], rk, kw)
                dk_ref[rows, :] = dk_t
                dkw_ref[...] += dkw_terms
            dv_ref[...] = carry_dv[...] + band_dv[:, 0:CH]

        carry_dk[...] = band_dk[:, CH:2 * CH]
        carry_dv[...] = band_dv[:, CH:2 * CH]

    col = lambda w: pl.BlockSpec((w, CH), lambda n: (0, jnp.minimum(n, N_CH - 1)))
    prev = lambda w: pl.BlockSpec((w, CH), lambda n: (0, jnp.maximum(n - 1, 0)))
    const = lambda shape: pl.BlockSpec(shape, lambda n: (0,) * len(shape))
    smem = pl.BlockSpec(memory_space=pltpu.SMEM)
    return pl.pallas_call(
        body, name="swa_bwd", grid=(N_CH + 1,),
        in_specs=[col(512), prev(128), col(128), prev(128), col(128), col(512), col(512), col(512),
                  const((SWA_D, 1)), const((SWA_D, 1)), smem, smem, const((2 * CH, CH))],
        out_specs=(col(512), col(512), prev(128), prev(128), const((2 * CH, HQ_LANES)), const((1, HQ_LANES)),
                   const((SWA_D, CH)), const((SWA_D, CH))),
        out_shape=(jax.ShapeDtypeStruct((512, T), F32), jax.ShapeDtypeStruct((512, T), F32),
                   jax.ShapeDtypeStruct((128, T), F32), jax.ShapeDtypeStruct((128, T), F32),
                   jax.ShapeDtypeStruct((2 * CH, HQ_LANES), F32), jax.ShapeDtypeStruct((1, HQ_LANES), F32),
                   jax.ShapeDtypeStruct((SWA_D, CH), F32), jax.ShapeDtypeStruct((SWA_D, CH), F32)),
        scratch_shapes=[pltpu.VMEM((2, 2 * CH, HQ_LANES), F32), pltpu.VMEM((1, HQ_LANES), F32),
                        pltpu.VMEM((128, 2 * CH), F32), pltpu.VMEM((128, 2 * CH), F32),
                        pltpu.VMEM((128, CH), F32), pltpu.VMEM((128, CH), F32)],
        compiler_params=_cparams(1),
    )(sq, sk, sk, sv, sv, sg, so, dmix, q_norm_w, k_norm_w, sinks, rel_bias_flat, bucket_t)


def _swa_small_grads(dbias, bucket_t, dsink_acc, dqw_acc, dkw_acc):
    def body(dbias_ref, bucket_ref, dsink_ref, dqw_ref, dkw_ref, relb_o, sink_o, qw_o, kw_o):
        bt = bucket_ref[...]
        row = lax.broadcasted_iota(jnp.int32, (N_BUCKETS, SWA_H), 0)
        col = lax.broadcasted_iota(jnp.int32, (N_BUCKETS, SWA_H), 1)
        lane = lax.broadcasted_iota(jnp.int32, (1, CH), 1)
        acc = jnp.zeros((N_BUCKETS, SWA_H), F32)
        sink = jnp.zeros((1, CH), F32)
        for hq in range(SWA_H):
            d = dbias_ref[:, _head_lanes(hq)]
            for bk in range(N_BUCKETS):
                s = jnp.sum(jnp.sum(jnp.where(bt == bk, d, 0.0), axis=0, keepdims=True), axis=1, keepdims=True)
                acc = acc + jnp.where((row == bk) & (col == hq), s, 0.0)
            sink = sink + jnp.where(lane == hq, jnp.sum(dsink_ref[:, _head_lanes(hq)], axis=1, keepdims=True), 0.0)
        relb_o[...] = acc
        sink_o[...] = sink
        for src, dst in ((dqw_ref, qw_o), (dkw_ref, kw_o)):
            padded = jnp.concatenate([src[...], jnp.zeros((CH - SWA_D, CH), F32)], axis=0)
            dst[...] = jnp.sum(padded.T, axis=0, keepdims=True)

    vm = pl.BlockSpec(memory_space=pltpu.VMEM)
    row128 = jax.ShapeDtypeStruct((1, CH), F32)
    return pl.pallas_call(body, name="swa_small_grads",
                          out_shape=(jax.ShapeDtypeStruct((N_BUCKETS, SWA_H), F32), row128, row128, row128),
                          in_specs=[vm] * 5, out_specs=(vm,) * 4, compiler_params=_cparams(),
                          )(dbias, bucket_t, dsink_acc, dqw_acc, dkw_acc)


def _out_proj(mix_r, mix_s_t, wo, x, target):
    def body(mr_ref, mst_ref, w_ref, x_ref, t_ref, loss_ref, dy_ref, dmr_ref, dmst_ref, gw_ref):
        i = pl.program_id(0)

        @pl.when(i == 0)
        def _():
            loss_ref[...] = jnp.zeros_like(loss_ref)
            gw_ref[...] = jnp.zeros_like(gw_ref)

        mr, mst = mr_ref[...], mst_ref[...]
        w_r, w_s = w_ref[0:512, :], w_ref[512:D, :]
        err = x_ref[...] + _dot(_bf(mr), w_r) + _dot(_bf(mst.T), w_s) - t_ref[...]
        loss_ref[...] += jnp.sum(jnp.sum(err * err, axis=1, keepdims=True), axis=0, keepdims=True)
        dy = err * (1.0 / D)
        dy_ref[...] = dy
        dy_b = _bf(dy)
        dmr_ref[...] = _dot_nt(dy_b, w_r)
        dmst_ref[...] = _dot_nt(w_s, dy_b)
        gw_ref[0:512, :] += _dot(_bf(mr.T), dy_b)
        gw_ref[512:D, :] += _dot(_bf(mst), dy_b)

    row = lambda w: pl.BlockSpec((TM, w), lambda i: (i, 0))
    col = lambda w: pl.BlockSpec((w, TM), lambda i: (0, i))
    const = lambda shape: pl.BlockSpec(shape, lambda i: (0,) * len(shape))
    return pl.pallas_call(
        body, name="out_proj", grid=(T // TM,),
        in_specs=[row(512), col(512), const((D, D)), row(D), row(D)],
        out_specs=(const((1, 1)), row(D), row(512), col(512), const((D, D))),
        out_shape=(jax.ShapeDtypeStruct((1, 1), F32), jax.ShapeDtypeStruct((T, D), F32),
                   jax.ShapeDtypeStruct((T, 512), F32), jax.ShapeDtypeStruct((512, T), F32),
                   jax.ShapeDtypeStruct((D, D), F32)),
        compiler_params=_cparams(1),
    )(mix_r, mix_s_t, wo, x, target)


def _in_proj_bwd(dpieces, x, rstd, norm_w, dy, wt):
    def body(*refs):
        piece_refs = refs[:8]
        x_ref, r_ref, nw_ref, dy_ref, w_ref, gx_ref, gw_ref, gnw_ref = refs[8:]
        i = pl.program_id(0)

        @pl.when(i == 0)
        def _():
            gw_ref[...] = jnp.zeros_like(gw_ref)
            gnw_ref[...] = jnp.zeros_like(gnw_ref)

        dp_r = jnp.concatenate([ref[...] for ref in piece_refs[:N_ROWMAJOR]], axis=1)
        dp_s = jnp.concatenate([ref[...] for ref in piece_refs[N_ROWMAJOR:]], axis=0)
        xv, r, nw = x_ref[...], r_ref[...], nw_ref[...]
        xr = xv * r
        h_b = _bf(xr * nw)
        dh = _dot(_bf(dp_r), w_ref[0:RET_W, :]) + _dot(_bf(dp_s.T), w_ref[RET_W:D_IN, :])
        gw_ref[0:RET_W, :] += _dot(_bf(dp_r.T), h_b)
        gw_ref[RET_W:D_IN, :] += _dot(_bf(dp_s), h_b)
        u = dh * nw
        gx_ref[...] = dy_ref[...] + r * u - xv * (r * r * r) * jnp.mean(u * xv, axis=-1, keepdims=True)
        gnw_ref[...] += jnp.sum(dh * xr, axis=0, keepdims=True)

    row = lambda w: pl.BlockSpec((TM, w), lambda i: (i, 0))
    col = lambda w: pl.BlockSpec((w, TM), lambda i: (0, i))
    const = lambda shape, **kw: pl.BlockSpec(shape, lambda i: (0,) * len(shape), **kw)
    once = dict(pipeline_mode=pl.Buffered(1))
    return pl.pallas_call(
        body, name="in_proj_bwd", grid=(T // TM,),
        in_specs=([row(w) for w in PIECES[:N_ROWMAJOR]] + [col(w) for w in PIECES[N_ROWMAJOR:]]
                  + [row(D), row(1), const((1, D)), row(D), const((D_IN, D), **once)]),
        out_specs=(row(D), const((D_IN, D), **once), const((1, D))),
        out_shape=(jax.ShapeDtypeStruct((T, D), F32), jax.ShapeDtypeStruct((D_IN, D), F32),
                   jax.ShapeDtypeStruct((1, D), F32)),
        compiler_params=_cparams(1),
    )(*dpieces, x, rstd, norm_w, dy, wt)


SMALL_SHAPES = ((1, D), (1, 512), (1, SWA_D), (1, SWA_D), (1, SWA_H), (N_BUCKETS, SWA_H))
LOSS_ROW = 5


def _reduce_and_update(gwt, gwo, sse, small_g, w_in, m_in, v_in, w_out, m_out, v_out, small_w, small_m, small_v):
    n_small = len(SMALL_SHAPES)

    def body(*refs):
        gwt_ref, gwo_ref, sse_ref = refs[:3]
        sg_refs = refs[3:3 + n_small]
        w_in_ref, m_in_ref, v_in_ref, w_out_ref, m_out_ref, v_out_ref = refs[3 + n_small:9 + n_small]
        pos = 9 + n_small
        sw_refs, sm_refs, sv_refs = (refs[pos + i * n_small:pos + (i + 1) * n_small] for i in range(3))
        pos += 3 * n_small
        g_in_o, d_in_o, nm_in_o, nv_in_o, g_out_o, d_out_o, nm_out_o, nv_out_o = refs[pos:pos + 8]
        pos += 8
        sg_o, sd_o, snm_o, snv_o = (refs[pos + i * n_small:pos + (i + 1) * n_small] for i in range(4))
        pos += 4 * n_small
        loss_o = refs[pos]
        (own_t, own_o, rcv_t, rcv_o, snd_t, snd_o, got_t, got_o, tab_a, tab_b, sum_a, sum_b, pad_ref, tr_ref,
         loc_sems, a_send, a_recv, b_send, b_recv, s_send, s_recv) = refs[pos + 1:]
        x, y, c = _mesh_pos()
        me = 4 * x + 2 * y + c
        sibling = (x, y, 1 - c)
        all_chips = [(0, 0), (0, 1), (1, 0), (1, 1)]
        others = [(1 - x, y), (x, 1 - y), (1 - x, 1 - y)]

        def blk(px, py, pc):
            return 4 * px + 2 * py + pc

        mine_a, mine_b = tab_a.at[me], tab_b.at[me]
        mine_a[...] = jnp.zeros_like(mine_a)
        mine_b[...] = jnp.zeros_like(mine_b)
        for r in range(n_small - 1):
            mine_a[r:r + 1, 0:sg_refs[r].shape[1]] = sg_refs[r][...]
        mine_a[LOSS_ROW:LOSS_ROW + 1, 0:1] = sse_ref[...]
        mine_b[:, 0:SWA_H] = sg_refs[n_small - 1][...]
        small_copies = []
        for k in range(1, N_DEV):
            px, py, pc = x ^ (k >> 2), y ^ ((k >> 1) & 1), c ^ (k & 1)
            for a, tab in enumerate((tab_a, tab_b)):
                cp = pltpu.make_async_remote_copy(src_ref=tab.at[me], dst_ref=tab.at[me], send_sem=s_send.at[a, k - 1],
                                                  recv_sem=s_recv.at[a, k - 1], device_id=(px, py, pc),
                                                  device_id_type=MESH)
                cp.start()
                small_copies.append(cp)

        stage_a, local = [], []
        for j, (px, py) in enumerate(all_chips):
            for a, (src, dst) in enumerate(((gwt_ref, rcv_t), (gwo_ref, rcv_o))):
                cp = pltpu.make_async_remote_copy(src_ref=src.at[blk(px, py, 1 - c)], dst_ref=dst.at[j],
                                                  send_sem=a_send.at[a, j], recv_sem=a_recv.at[a, j],
                                                  device_id=sibling, device_id_type=MESH)
                cp.start()
                stage_a.append(cp)
            for a, (src, dst) in enumerate(((gwt_ref, own_t), (gwo_ref, own_o))):
                cp = pltpu.make_async_copy(src.at[blk(px, py, c)], dst.at[j], loc_sems.at[a, j])
                cp.start()
                local.append(cp)
        for cp in local:
            cp.wait()
        for cp in stage_a:
            cp.wait_recv()

        stage_b = []
        for k, (px, py) in enumerate(others):
            j = 2 * px + py
            snd_t[k] = _bf(own_t[j] + rcv_t[j])
            snd_o[k] = _bf(own_o[j] + rcv_o[j])
            for a, (src, dst) in enumerate(((snd_t, got_t), (snd_o, got_o))):
                cp = pltpu.make_async_remote_copy(src_ref=src.at[k], dst_ref=dst.at[k], send_sem=b_send.at[a, k],
                                                  recv_sem=b_recv.at[a, k], device_id=(px, py, c), device_id_type=MESH)
                cp.start()
                stage_b.append(cp)
        jm = 2 * x + y
        tot_t = own_t[jm] + rcv_t[jm]
        tot_o = own_o[jm] + rcv_o[jm]
        for k in range(3):
            stage_b[2 * k].wait_recv()
            stage_b[2 * k + 1].wait_recv()
            tot_t = tot_t + got_t[k].astype(F32)
            tot_o = tot_o + got_o[k].astype(F32)

        d, nm, nv = _adamw(w_out_ref[...], tot_o, m_out_ref[...], v_out_ref[...])
        g_out_o[...], d_out_o[...], nm_out_o[...], nv_out_o[...] = tot_o, d, nm, nv

        pad_ref[0:WIN_BLK, :] = tot_t
        pad_ref[WIN_BLK:WIN_PAD, :] = jnp.zeros((WIN_PAD - WIN_BLK, D), F32)
        tr_ref[...] = pad_ref[...].T
        g = tr_ref[:, 0:WIN_BLK]
        d, nm, nv = _adamw(w_in_ref[...], g, m_in_ref[...], v_in_ref[...])
        g_in_o[...], d_in_o[...], nm_in_o[...], nv_in_o[...] = g, d, nm, nv

        for cp in small_copies:
            cp.wait_recv()
        tot_a, tot_b = tab_a[0], tab_b[0]
        for b in range(1, N_DEV):
            tot_a = tot_a + tab_a[b]
            tot_b = tot_b + tab_b[b]
        sum_a[...] = tot_a
        sum_b[...] = tot_b
        loss_o[...] = sum_a[LOSS_ROW:LOSS_ROW + 1, 0:1] * (0.5 / D)
        for r, (rows, lanes) in enumerate(SMALL_SHAPES):
            g = sum_a[r:r + 1, 0:lanes] if rows == 1 else sum_b[:, 0:lanes]
            d, nm, nv = _adamw(sw_refs[r][...], g, sm_refs[r][...], sv_refs[r][...])
            sg_o[r][...], sd_o[r][...], snm_o[r][...], snv_o[r][...] = g, d, nm, nv

        for cp in stage_a + stage_b + small_copies:
            cp.wait_send()

    vm = pl.BlockSpec(memory_space=pltpu.VMEM)
    hbm = pl.BlockSpec(memory_space=pl.ANY)
    win = jax.ShapeDtypeStruct((D, WIN_BLK), F32)
    wout = jax.ShapeDtypeStruct((WOUT_BLK, D), F32)
    smalls = tuple(jax.ShapeDtypeStruct(s, F32) for s in SMALL_SHAPES)
    outs = pl.pallas_call(
        body, name="reduce_and_update",
        in_specs=[hbm, hbm] + [vm] * (7 + 4 * n_small), out_specs=(vm,) * (9 + 4 * n_small),
        out_shape=(win,) * 4 + (wout,) * 4 + smalls * 4 + (jax.ShapeDtypeStruct((1, 1), F32),),
        scratch_shapes=[
            pltpu.VMEM((4, WIN_BLK, D), F32), pltpu.VMEM((4, WOUT_BLK, D), F32),
            pltpu.VMEM((4, WIN_BLK, D), F32), pltpu.VMEM((4, WOUT_BLK, D), F32),
            pltpu.VMEM((3, WIN_BLK, D), BF16), pltpu.VMEM((3, WOUT_BLK, D), BF16),
            pltpu.VMEM((3, WIN_BLK, D), BF16), pltpu.VMEM((3, WOUT_BLK, D), BF16),
            pltpu.VMEM((N_DEV, 8, D), F32), pltpu.VMEM((N_DEV, N_BUCKETS, CH), F32),
            pltpu.VMEM((8, D), F32), pltpu.VMEM((N_BUCKETS, CH), F32),
            pltpu.VMEM((WIN_PAD, D), F32), pltpu.VMEM((D, WIN_PAD), F32),
            pltpu.SemaphoreType.DMA((2, 4)),
            pltpu.SemaphoreType.DMA((2, 4)), pltpu.SemaphoreType.DMA((2, 4)),
            pltpu.SemaphoreType.DMA((2, 3)), pltpu.SemaphoreType.DMA((2, 3)),
            pltpu.SemaphoreType.DMA((2, 7)), pltpu.SemaphoreType.DMA((2, 7)),
        ],
        compiler_params=_cparams(),
    )(gwt, gwo, sse, *small_g, w_in, m_in, v_in, w_out, m_out, v_out, *small_w, *small_m, *small_v)
    big, rest = outs[:8], outs[8:]
    return big, [rest[i * n_small:(i + 1) * n_small] for i in range(4)], rest[4 * n_small]


def _small_rows(norm_w, ret_norm_w, q_norm_w, k_norm_w, sinks, rel_bias):
    return (norm_w.reshape(1, D), ret_norm_w.reshape(1, 512), q_norm_w.reshape(1, SWA_D), k_norm_w.reshape(1, SWA_D),
            sinks.reshape(1, SWA_H), rel_bias)


def _small_leaves(rows):
    return (rows[0].reshape(D), rows[1].reshape(512), rows[2].reshape(SWA_D), rows[3].reshape(SWA_D),
            rows[4].reshape(SWA_H), rows[5])


def kernel(x, norm_w, w_in, ret_norm_w, q_norm_w, k_norm_w, sinks, rel_bias, w_out, loss_target, m_norm_w, m_w_in, m_ret_norm_w, m_q_norm_w, m_k_norm_w, m_sinks, m_rel_bias, m_w_out, v_norm_w, v_w_in, v_ret_norm_w, v_q_norm_w, v_k_norm_w, v_sinks, v_rel_bias, v_w_out):
    x2 = x.reshape(T, D)
    target = loss_target.reshape(T, D)
    nw = norm_w.reshape(1, D)
    rnw = ret_norm_w.reshape(1, 512)
    qnw_col = q_norm_w.reshape(SWA_D, 1)
    knw_col = k_norm_w.reshape(SWA_D, 1)
    relb = rel_bias.reshape(N_BUCKETS * SWA_H)
    cos_t, sin_t = _rotary_tables()
    decay, zeta_b, xi_b, chunk_decay = _retention_tables()
    bucket_t = _bucket_table_t()

    wt, wo = _gather_weights(w_in, w_out)

    rq, rk, rv, rg, sq, sk, sv, sg, rstd = _in_proj(x2, nw, wt)
    ro, mix_r, states = _retention_fwd(rq, rk, rv, rg, rnw, cos_t, sin_t, decay, zeta_b, xi_b, chunk_decay)
    so, mix_s = _swa_fwd(sq, sk, sv, sg, qnw_col, knw_col, sinks, relb, bucket_t)
    sse, dy, dmix_r, dmix_s, gwo = _out_proj(mix_r, mix_s, wo, x2, target)

    drq, drk, drv, drg, d_rnw = _retention_bwd(rq, rk, rv, rg, ro, states, dmix_r, rnw, cos_t, sin_t, decay, zeta_b,
                                               xi_b, chunk_decay)
    dsq, dsg, dsk, dsv, dbias, dsink_acc, dqw_acc, dkw_acc = _swa_bwd(sq, sk, sv, sg, so, dmix_s, qnw_col, knw_col,
                                                                      sinks, relb, bucket_t)
    d_relb, d_sinks, d_qnw, d_knw = _swa_small_grads(dbias, bucket_t, dsink_acc, dqw_acc, dkw_acc)
    grad_x, gwt, d_nw = _in_proj_bwd((drq, drk, drv, drg, dsq, dsk, dsv, dsg), x2, rstd, nw, dy, wt)

    small_g = (d_nw, d_rnw, d_qnw, d_knw, d_sinks, d_relb)
    small_w = _small_rows(norm_w, ret_norm_w, q_norm_w, k_norm_w, sinks, rel_bias)
    small_m = _small_rows(m_norm_w, m_ret_norm_w, m_q_norm_w, m_k_norm_w, m_sinks, m_rel_bias)
    small_v = _small_rows(v_norm_w, v_ret_norm_w, v_q_norm_w, v_k_norm_w, v_sinks, v_rel_bias)

    big, small, loss = _reduce_and_update(
        gwt.reshape(N_DEV, WIN_BLK, D), gwo.reshape(N_DEV, WOUT_BLK, D), sse, small_g,
        w_in, m_w_in, v_w_in, w_out, m_w_out, v_w_out, small_w, small_m, small_v)

    def leaves(i):
        a = _small_leaves(small[i])
        return (a[0], big[i], a[1], a[2], a[3], a[4], a[5], big[4 + i])

    return (loss.reshape(()), grad_x.reshape(1, T, D), *leaves(0), *leaves(1), *leaves(2), *leaves(3))
```

```python
import numpy as np
import jax
import jax.numpy as jnp
from jax import lax
from jax.experimental import pallas as pl
from jax.experimental.pallas import tpu as pltpu

F32 = jnp.float32
BF16 = jnp.bfloat16
MESH = pl.DeviceIdType.MESH

T = 2048
D = 1024
D_IN = 2816
N_DEV = 8
WIN_BLK = D_IN // N_DEV
WIN_PAD = 384
WOUT_BLK = D // N_DEV
CH = 128
N_CH = T // CH
RET_H, RET_DK, RET_DV = 4, 64, 128
RET_QK = RET_H * RET_DK
RET_V = RET_H * RET_DV
SWA_H, SWA_KV, SWA_D, SWA_G = 8, 2, 64, 4
N_BUCKETS = 32
NORM_EPS = 1e-6
GN_EPS = 1e-5
NEG_INF = -1e30
PIECES = (256, 256, 512, 512, 512, 128, 128, 512)
OFFS = tuple(int(v) for v in np.cumsum((0,) + PIECES))
TM = 256

ADAM_LR, ADAM_B1, ADAM_B2, ADAM_EPS, ADAM_WD, ADAM_STEP = 0.001, 0.9, 0.999, 1e-08, 0.01, 10

VMEM_LIMIT = 56 * 1024 * 1024


def _cparams(n_grid=0, vmem=VMEM_LIMIT):
    sem = ("arbitrary",) * n_grid if n_grid else None
    return pltpu.CompilerParams(dimension_semantics=sem, vmem_limit_bytes=vmem)


def _dot(a, b):
    return jnp.dot(a, b, preferred_element_type=F32)


def _dot_nt(a, b):
    return lax.dot_general(a, b, (((1,), (1,)), ((), ())), preferred_element_type=F32)


def _bf(a):
    return a.astype(BF16)


def _sigmoid(x):
    return 1.0 / (1.0 + jnp.exp(-x))


def _adamw(w, g, m, v):
    m = ADAM_B1 * m + (1.0 - ADAM_B1) * g
    v = ADAM_B2 * v + (1.0 - ADAM_B2) * (g * g)
    m_hat = m / (1.0 - ADAM_B1 ** ADAM_STEP)
    v_hat = v / (1.0 - ADAM_B2 ** ADAM_STEP)
    delta = -ADAM_LR * (m_hat / (jnp.sqrt(v_hat) + ADAM_EPS) + ADAM_WD * w)
    return delta, m, v


def _rotary_tables_t():
    half = RET_DK // 2
    inv_freq = np.float32(10000.0) ** (-np.arange(half, dtype=np.float32) / np.float32(half))
    ang = inv_freq[:, None] * np.arange(T, dtype=np.float32)[None, :]
    cos, sin = np.cos(ang).astype(np.float32), np.sin(ang).astype(np.float32)
    cos64 = np.concatenate([cos, cos], axis=0)
    sin64 = np.concatenate([-sin, sin], axis=0)
    return np.tile(cos64, (RET_H, 1)), np.tile(sin64, (RET_H, 1))


def _retention_tables_t():
    gamma = (1.0 - np.exp2(-5.0 - np.arange(RET_H, dtype=np.float32))).astype(np.float32)
    log_g = np.log(gamma).astype(np.float32)
    i = np.arange(CH, dtype=np.float32)
    diff = i[None, :] - i[:, None]
    decay = np.where(diff >= 0, np.exp(log_g[:, None, None] * np.maximum(diff, 0.0)), 0.0).astype(np.float32)
    decay_all = np.concatenate(list(decay), axis=1)
    zeta = np.exp(log_g[:, None] * (CH - 1.0 - i)).astype(np.float32)
    zeta_tab = np.repeat(zeta.T, RET_DK, axis=1)
    xi = np.exp(log_g[:, None] * (i + 1.0)).astype(np.float32)
    xi_tab = np.repeat(xi, RET_DK, axis=0)
    chunk_decay = np.exp(log_g * np.float32(CH)).astype(np.float32)
    row_head = np.arange(RET_V)[:, None] // RET_DV
    col_head = np.arange(RET_QK)[None, :] // RET_DK
    state_mask = (row_head == col_head).astype(np.float32)
    state_decay = (state_mask * chunk_decay[row_head]).astype(np.float32)
    q_mask = (np.arange(RET_QK)[:, None] // RET_DK == np.arange(RET_H * CH)[None, :] // CH).astype(np.float32)
    return (np.ascontiguousarray(decay_all), np.ascontiguousarray(zeta_tab), np.ascontiguousarray(xi_tab),
            state_mask, state_decay, q_mask)


def _bucket_table_t():
    qi = np.arange(CH)[:, None]
    kj = np.arange(2 * CH)[None, :]
    dist = qi + CH - kj
    n = np.maximum(dist, 0)
    max_exact = N_BUCKETS // 2
    nf = np.maximum(n, 1).astype(np.float32)
    large = max_exact + (np.log(nf / np.float32(max_exact)) / np.float32(np.log(CH / max_exact))
                         * np.float32(N_BUCKETS - max_exact)).astype(np.int32)
    large = np.minimum(large, N_BUCKETS - 1)
    bucket = np.where(n < max_exact, n, large)
    return np.ascontiguousarray(np.where((dist >= 0) & (dist < CH), bucket, -1).astype(np.int32).T)


def _mesh_pos():
    return lax.axis_index("x"), lax.axis_index("y"), lax.axis_index("c")


def _gather_weights(w_in_blk, w_out_blk):
    def body(win_ref, wout_ref, wt_ref, wo_ref, pad_ref, send_sems, recv_sems):
        x, y, c = _mesh_pos()
        me = 4 * x + 2 * y + c
        sibling = (x, y, 1 - c)
        chips = [(1 - x, y), (x, 1 - y), (1 - x, 1 - y)]

        pad_ref[...] = jnp.zeros_like(pad_ref)
        pad_ref[:, 0:WIN_BLK] = win_ref[...]
        wt_ref[me] = _bf(pad_ref[...].T[0:WIN_BLK, :])
        wo_ref[me] = _bf(wout_ref[...])

        def blk(px, py, pc):
            return 4 * px + 2 * py + pc

        def copies(k, b, to):
            return [pltpu.make_async_remote_copy(src_ref=ref.at[b], dst_ref=ref.at[b],
                                                 send_sem=send_sems.at[a, k], recv_sem=recv_sems.at[a, k],
                                                 device_id=to, device_id_type=MESH)
                    for a, ref in enumerate((wt_ref, wo_ref))]

        first = copies(0, me, sibling)
        for j, chip in enumerate(chips):
            first += copies(1 + j, me, (*chip, c))
        for cp in first:
            cp.start()
        passed = []
        for j, chip in enumerate(chips):
            for cp in copies(1 + j, blk(*chip, c), (x, y, c)):
                cp.wait_recv()
            fwd = copies(4 + j, blk(*chip, c), sibling)
            for cp in fwd:
                cp.start()
            passed += fwd
        for cp in copies(0, blk(x, y, 1 - c), (x, y, c)):
            cp.wait_recv()
        for j, chip in enumerate(chips):
            for cp in copies(4 + j, blk(*chip, 1 - c), (x, y, c)):
                cp.wait_recv()
        for cp in first + passed:
            cp.wait_send()

    vm = pl.BlockSpec(memory_space=pltpu.VMEM)
    wt, wo = pl.pallas_call(
        body, name="gather_weights",
        out_shape=(jax.ShapeDtypeStruct((N_DEV, WIN_BLK, D), BF16), jax.ShapeDtypeStruct((N_DEV, WOUT_BLK, D), BF16)),
        in_specs=[vm, vm], out_specs=(vm, vm),
        scratch_shapes=[pltpu.VMEM((D, WIN_PAD), F32), pltpu.SemaphoreType.DMA((2, 7)), pltpu.SemaphoreType.DMA((2, 7))],
        compiler_params=_cparams(vmem=32 * 1024 * 1024),
    )(w_in_blk, w_out_blk)
    return wt.reshape(D_IN, D), wo.reshape(D, D)


def _in_proj(x, norm_w, wt):
    def body(x_ref, nw_ref, w_ref, *outs):
        piece_refs, rstd_ref = outs[:8], outs[8]
        xv = x_ref[...]
        r = lax.rsqrt(jnp.mean(xv * xv, axis=-1, keepdims=True) + NORM_EPS)
        h = _bf(xv * r * nw_ref[...])
        for i, ref in enumerate(piece_refs):
            ref[...] = _dot_nt(w_ref[OFFS[i]:OFFS[i + 1], :], h)
        rstd_ref[...] = r

    row = lambda w: pl.BlockSpec((TM, w), lambda i: (i, 0))
    col = lambda w: pl.BlockSpec((w, TM), lambda i: (0, i))
    return pl.pallas_call(
        body, name="in_proj", grid=(T // TM,),
        in_specs=[row(D), pl.BlockSpec((1, D), lambda i: (0, 0)), pl.BlockSpec((D_IN, D), lambda i: (0, 0))],
        out_specs=tuple(col(w) for w in PIECES) + (row(1),),
        out_shape=tuple(jax.ShapeDtypeStruct((w, T), F32) for w in PIECES) + (jax.ShapeDtypeStruct((T, 1), F32),),
        compiler_params=_cparams(1),
    )(x, norm_w, wt)


def _swap_halves_t(t):
    half = RET_DK // 2
    parts = []
    for h in range(RET_H):
        parts += [t[h * RET_DK + half:(h + 1) * RET_DK], t[h * RET_DK:h * RET_DK + half]]
    return jnp.concatenate(parts, axis=0)


def _rotate(t, cos, sin):
    return t * cos + _swap_halves_t(t) * sin


def _group_norm_t(o):
    mu = jnp.mean(o, axis=0, keepdims=True)
    var = jnp.mean((o - mu) * (o - mu), axis=0, keepdims=True)
    rstd = lax.rsqrt(var + GN_EPS)
    return (o - mu) * rstd, rstd


def _retention_scores_t(q_t, k_rm_b, q_mask, decay_all):
    q_heads = _bf(jnp.concatenate([q_t] * RET_H, axis=1) * q_mask)
    return _dot(k_rm_b, q_heads) * decay_all


def _retention_fwd(rq, rk, rv, rg, ret_norm_w, tables):
    cos_t, sin_t, decay_all, zeta_tab, xi_tab, state_mask, state_decay, q_mask = tables

    def body(rq_ref, rk_ref, rv_ref, rg_ref, w_ref, cos_ref, sin_ref, decay_ref, zeta_ref, xi_ref, smask_ref,
             sdecay_ref, qmask_ref, ro_ref, mix_ref, st_ref, state):
        n = pl.program_id(0)

        @pl.when(n == 0)
        def _():
            state[...] = jnp.zeros_like(state)

        cos, sin = cos_ref[...], sin_ref[...]
        q_t = _rotate(rq_ref[...], cos, sin)
        k_t = _rotate(rk_ref[...], cos, sin) * (RET_DK ** -0.5)
        k_rm = k_t.T
        v_b = _bf(rv_ref[...])
        m_b = _bf(state[...])
        st_ref[0] = m_b
        scores_b = _bf(_retention_scores_t(q_t, _bf(k_rm), qmask_ref[...], decay_ref[...]))
        cross = _dot(m_b, _bf(q_t * xi_ref[...]))
        state[...] = state[...] * sdecay_ref[...] + _dot(v_b, _bf(k_rm * zeta_ref[...])) * smask_ref[...]
        heads = range(RET_H)
        rows = [slice(h * RET_DV, (h + 1) * RET_DV) for h in heads]
        o = [_dot(v_b[rows[h]], scores_b[:, h * CH:(h + 1) * CH]) + cross[rows[h]] for h in heads]
        for h in heads:
            rn, _ = _group_norm_t(o[h])
            g = rg_ref[rows[h], :]
            ro_ref[rows[h], :] = o[h]
            mix_ref[rows[h], :] = rn * w_ref[rows[h], :] * (g * _sigmoid(g))

    col = lambda w: pl.BlockSpec((w, CH), lambda n: (0, n))
    const = lambda shape: pl.BlockSpec(shape, lambda n: (0,) * len(shape))
    return pl.pallas_call(
        body, name="retention_fwd", grid=(N_CH,),
        in_specs=[col(RET_QK), col(RET_QK), col(RET_V), col(RET_V), const((RET_V, 1)), col(RET_QK), col(RET_QK),
                  const(decay_all.shape), const(zeta_tab.shape), const(xi_tab.shape), const(state_mask.shape),
                  const(state_decay.shape), const(q_mask.shape)],
        out_specs=(col(RET_V), col(RET_V), pl.BlockSpec((1, RET_V, RET_QK), lambda n: (n, 0, 0))),
        out_shape=(jax.ShapeDtypeStruct((RET_V, T), F32), jax.ShapeDtypeStruct((RET_V, T), F32),
                   jax.ShapeDtypeStruct((N_CH, RET_V, RET_QK), BF16)),
        scratch_shapes=[pltpu.VMEM((RET_V, RET_QK), F32)],
        compiler_params=_cparams(1),
    )(rq, rk, rv, rg, ret_norm_w, cos_t, sin_t, decay_all, zeta_tab, xi_tab, state_mask, state_decay, q_mask)


def _retention_bwd(rq, rk, rv, rg, ro, states, dmix, ret_norm_w, tables):
    cos_t, sin_t, decay_all, zeta_tab, xi_tab, state_mask, state_decay, q_mask = tables

    def body(rq_ref, rk_ref, rv_ref, rg_ref, ro_ref, st_ref, dm_ref, w_ref, cos_ref, sin_ref, decay_ref, zeta_ref,
             xi_ref, smask_ref, sdecay_ref, qmask_ref, dq_ref, dk_ref, dv_ref, dg_ref, dw_ref, gstate):
        i = pl.program_id(0)

        @pl.when(i == 0)
        def _():
            gstate[...] = jnp.zeros_like(gstate)
            dw_ref[...] = jnp.zeros_like(dw_ref)

        cos, sin = cos_ref[...], sin_ref[...]
        q_t = _rotate(rq_ref[...], cos, sin)
        k_t = _rotate(rk_ref[...], cos, sin) * (RET_DK ** -0.5)
        q_b, k_b = _bf(q_t), _bf(k_t)
        k_rm = k_t.T
        kz_b = _bf(k_rm * zeta_ref[...])
        qx_b = _bf(q_t * xi_ref[...])
        v_t = rv_ref[...]
        v_b = _bf(v_t)
        v_rm_b = _bf(v_t.T)
        decay = decay_ref[...]
        scores_b = _bf(_retention_scores_t(q_t, _bf(k_rm), qmask_ref[...], decay))
        heads = range(RET_H)
        rows = [slice(h * RET_DV, (h + 1) * RET_DV) for h in heads]
        qk_rows = [slice(h * RET_DK, (h + 1) * RET_DK) for h in heads]
        lanes = [slice(h * CH, (h + 1) * CH) for h in heads]
        do = []
        for h in heads:
            g, w, dm = rg_ref[rows[h], :], w_ref[rows[h], :], dm_ref[rows[h], :]
            rn, rstd = _group_norm_t(ro_ref[rows[h], :])
            sig = _sigmoid(g)
            silu = g * sig
            dg_ref[rows[h], :] = dm * rn * w * (sig * (1.0 + g * (1.0 - sig)))
            dw_ref[rows[h], :] += dm * silu * rn
            drn = dm * silu * w
            do.append(rstd * (drn - jnp.mean(drn, axis=0, keepdims=True)
                              - rn * jnp.mean(drn * rn, axis=0, keepdims=True)))
        do_b = _bf(jnp.concatenate(do, axis=0))
        m_b = st_ref[0]
        g_all = gstate[...]
        g_b = _bf(g_all)
        dscores_b = [_bf(_dot(v_rm_b[:, rows[h]], do_b[rows[h]]) * decay[:, lanes[h]]) for h in heads]
        dq_cross = lax.dot_general(m_b, do_b, (((0,), (0,)), ((), ())), preferred_element_type=F32)
        dkz = _dot(v_rm_b, g_b)
        dv_cross = _dot_nt(g_b, kz_b)
        gstate[...] = g_all * sdecay_ref[...] + _dot_nt(do_b, qx_b) * smask_ref[...]
        dq = jnp.concatenate([_dot(k_b[qk_rows[h]], dscores_b[h]) for h in heads], axis=0) + dq_cross * xi_ref[...]
        dk = (jnp.concatenate([_dot_nt(q_b[qk_rows[h]], dscores_b[h]) for h in heads], axis=0)
              + (dkz * zeta_ref[...]).T) * (RET_DK ** -0.5)
        for h in heads:
            dv_ref[rows[h], :] = _dot_nt(do_b[rows[h]], scores_b[:, lanes[h]]) + dv_cross[rows[h]]
        dq_ref[...] = dq * cos + _swap_halves_t(dq * sin)
        dk_ref[...] = dk * cos + _swap_halves_t(dk * sin)

    col = lambda w: pl.BlockSpec((w, CH), lambda i: (0, N_CH - 1 - i))
    const = lambda shape: pl.BlockSpec(shape, lambda i: (0,) * len(shape))
    return pl.pallas_call(
        body, name="retention_bwd", grid=(N_CH,),
        in_specs=[col(RET_QK), col(RET_QK), col(RET_V), col(RET_V), col(RET_V),
                  pl.BlockSpec((1, RET_V, RET_QK), lambda i: (N_CH - 1 - i, 0, 0)), col(RET_V),
                  const((RET_V, 1)), col(RET_QK), col(RET_QK),
                  const(decay_all.shape), const(zeta_tab.shape), const(xi_tab.shape), const(state_mask.shape),
                  const(state_decay.shape), const(q_mask.shape)],
        out_specs=(col(RET_QK), col(RET_QK), col(RET_V), col(RET_V), const((RET_V, CH))),
        out_shape=(jax.ShapeDtypeStruct((RET_QK, T), F32), jax.ShapeDtypeStruct((RET_QK, T), F32),
                   jax.ShapeDtypeStruct((RET_V, T), F32), jax.ShapeDtypeStruct((RET_V, T), F32),
                   jax.ShapeDtypeStruct((RET_V, CH), F32)),
        scratch_shapes=[pltpu.VMEM((RET_V, RET_QK), F32)],
        compiler_params=_cparams(1),
    )(rq, rk, rv, rg, ro, states, dmix, ret_norm_w, cos_t, sin_t, decay_all, zeta_tab, xi_tab, state_mask,
      state_decay, q_mask)


HQ_LANES = SWA_H * CH


def _head_lanes(hq):
    return slice(hq * CH, (hq + 1) * CH)


def _build_bias_t(bucket_ref, relb_ref, sinks_ref, bias_ref, sink_row):
    bt = bucket_ref[...]
    first = lax.broadcasted_iota(jnp.int32, bt.shape, 0) < CH
    for hq in range(SWA_H):
        b = jnp.full(bt.shape, NEG_INF, F32)
        for bk in range(N_BUCKETS):
            b = jnp.where(bt == bk, relb_ref[bk * SWA_H + hq], b)
        bias_ref[0, :, _head_lanes(hq)] = b
        bias_ref[1, :, _head_lanes(hq)] = jnp.where(first, NEG_INF, b)
        sink_row[:, _head_lanes(hq)] = jnp.full((1, CH), sinks_ref[hq], F32)


def _rms_t(t, w_col):
    r = lax.rsqrt(jnp.mean(t * t, axis=0, keepdims=True) + NORM_EPS)
    return t * r * w_col, r


def _rms_t_bwd(dn, t, r, w_col):
    u = dn * w_col
    return r * u - t * (r * r * r) * jnp.mean(u * t, axis=0, keepdims=True), dn * t * r


def _norm_kv(k_t, kw):
    return jnp.concatenate([_rms_t(k_t[g * SWA_D:(g + 1) * SWA_D], kw)[0] for g in range(SWA_KV)], axis=0)


def _kv_slot(a, kv):
    z = jnp.zeros_like(a)
    return jnp.concatenate([a, z] if kv == 0 else [z, a], axis=0)


def _softmax_t(s, sink):
    m = jnp.maximum(jnp.max(s, axis=0, keepdims=True), sink)
    p = jnp.exp(s - m)
    e_sink = jnp.exp(sink - m)
    inv = 1.0 / (jnp.sum(p, axis=0, keepdims=True) + e_sink)
    return p * inv, e_sink * inv


def _swa_fwd(sq, sk, sv, sg, q_norm_w, k_norm_w, sinks, rel_bias_flat, bucket_t):
    def body(sq_ref, skp_ref, skc_ref, svp_ref, svc_ref, sg_ref, qw_ref, kw_ref, sinks_ref, relb_ref, bucket_ref,
             so_ref, mix_ref, bias_ref, sink_row):
        n = pl.program_id(0)

        @pl.when(n == 0)
        def _():
            _build_bias_t(bucket_ref, relb_ref, sinks_ref, bias_ref, sink_row)

        var = (n == 0).astype(jnp.int32)
        qw, kw = qw_ref[...], kw_ref[...]
        kn_band = jnp.concatenate([_norm_kv(skp_ref[...], kw), _norm_kv(skc_ref[...], kw)], axis=1)
        kb_rm = _bf(kn_band.T)
        vband = _bf(jnp.concatenate([svp_ref[...], svc_ref[...]], axis=1))
        q_all = jnp.concatenate(
            [_bf(_kv_slot(_rms_t(sq_ref[hq * SWA_D:(hq + 1) * SWA_D, :], qw)[0] * (SWA_D ** -0.5), hq // SWA_G))
             for hq in range(SWA_H)], axis=1)
        probs, _ = _softmax_t(_dot(kb_rm, q_all) + bias_ref[var], sink_row[...])
        probs_b = _bf(probs)
        for kv in range(SWA_KV):
            o = _dot(vband[kv * SWA_D:(kv + 1) * SWA_D], probs_b[:, kv * SWA_G * CH:(kv + 1) * SWA_G * CH])
            for j in range(SWA_G):
                rows = slice((kv * SWA_G + j) * SWA_D, (kv * SWA_G + j + 1) * SWA_D)
                oh = o[:, j * CH:(j + 1) * CH]
                g = sg_ref[rows, :]
                so_ref[rows, :] = oh
                mix_ref[rows, :] = oh * (g * _sigmoid(g))

    col = lambda w: pl.BlockSpec((w, CH), lambda n: (0, n))
    prev = lambda w: pl.BlockSpec((w, CH), lambda n: (0, jnp.maximum(n - 1, 0)))
    const = lambda shape: pl.BlockSpec(shape, lambda n: (0,) * len(shape))
    smem = pl.BlockSpec(memory_space=pltpu.SMEM)
    return pl.pallas_call(
        body, name="swa_fwd", grid=(N_CH,),
        in_specs=[col(512), prev(128), col(128), prev(128), col(128), col(512), const((SWA_D, 1)), const((SWA_D, 1)),
                  smem, smem, const((2 * CH, CH))],
        out_specs=(col(512), col(512)),
        out_shape=(jax.ShapeDtypeStruct((512, T), F32), jax.ShapeDtypeStruct((512, T), F32)),
        scratch_shapes=[pltpu.VMEM((2, 2 * CH, HQ_LANES), F32), pltpu.VMEM((1, HQ_LANES), F32)],
        compiler_params=_cparams(1),
    )(sq, sk, sk, sv, sv, sg, q_norm_w, k_norm_w, sinks, rel_bias_flat, bucket_t)


def _swa_bwd(sq, sk, sv, sg, so, dmix, q_norm_w, k_norm_w, sinks, rel_bias_flat, bucket_t):
    def body(sq_ref, skp_ref, skc_ref, svp_ref, svc_ref, sg_ref, so_ref, dm_ref, qw_ref, kw_ref, sinks_ref, relb_ref,
             bucket_ref, dq_ref, dg_ref, dk_ref, dv_ref, dbias_ref, dsink_ref, dqw_ref, dkw_ref,
             bias_ref, sink_row, band_dk, band_dv, carry_dk, carry_dv):
        n = pl.program_id(0)

        @pl.when(n == 0)
        def _():
            _build_bias_t(bucket_ref, relb_ref, sinks_ref, bias_ref, sink_row)
            for ref in (dbias_ref, dsink_ref, dqw_ref, dkw_ref, carry_dk, carry_dv):
                ref[...] = jnp.zeros_like(ref)

        qw, kw = qw_ref[...], kw_ref[...]

        @pl.when(n < N_CH)
        def _():
            var = (n == 0).astype(jnp.int32)
            kn_band = jnp.concatenate([_norm_kv(skp_ref[...], kw), _norm_kv(skc_ref[...], kw)], axis=1)
            kb_rm = _bf(kn_band.T)
            kn_band_s = _bf(kn_band * (SWA_D ** -0.5))
            vband_f = jnp.concatenate([svp_ref[...], svc_ref[...]], axis=1)
            vb_rm = _bf(vband_f.T)
            q_raw, q_rstd, qs_b, do_b = [], [], [], []
            for hq in range(SWA_H):
                rows = slice(hq * SWA_D, (hq + 1) * SWA_D)
                q_t = sq_ref[rows, :]
                qn, rq = _rms_t(q_t, qw)
                g = sg_ref[rows, :]
                sig = _sigmoid(g)
                dm = dm_ref[rows, :]
                dg_ref[rows, :] = dm * so_ref[rows, :] * (sig * (1.0 + g * (1.0 - sig)))
                q_raw.append(q_t)
                q_rstd.append(rq)
                qs_b.append(_bf(qn * (SWA_D ** -0.5)))
                do_b.append(_bf(dm * (g * sig)))
            q_all = jnp.concatenate([_kv_slot(qs_b[hq], hq // SWA_G) for hq in range(SWA_H)], axis=1)
            do_all = jnp.concatenate([_kv_slot(do_b[hq], hq // SWA_G) for hq in range(SWA_H)], axis=1)
            probs, p_sink = _softmax_t(_dot(kb_rm, q_all) + bias_ref[var], sink_row[...])
            dprobs = _dot(vb_rm, do_all)
            t = jnp.sum(probs * dprobs, axis=0, keepdims=True)
            dlog = probs * (dprobs - t)
            dsink_ref[...] += -(p_sink * t)
            dbias_ref[...] += dlog
            dlog_b, probs_b = _bf(dlog), _bf(probs)
            dkn, dvv = [], []
            for kv in range(SWA_KV):
                heads = range(kv * SWA_G, (kv + 1) * SWA_G)
                lanes = slice(kv * SWA_G * CH, (kv + 1) * SWA_G * CH)
                dvv.append(_dot_nt(jnp.concatenate([do_b[hq] for hq in heads], axis=1), probs_b[:, lanes]))
                dkn.append(_dot_nt(jnp.concatenate([qs_b[hq] for hq in heads], axis=1), dlog_b[:, lanes]))
                dqn = _dot(kn_band_s[kv * SWA_D:(kv + 1) * SWA_D], dlog_b[:, lanes])
                for j, hq in enumerate(heads):
                    dq_t, dqw_terms = _rms_t_bwd(dqn[:, j * CH:(j + 1) * CH], q_raw[hq], q_rstd[hq], qw)
                    dq_ref[hq * SWA_D:(hq + 1) * SWA_D, :] = dq_t
                    dqw_ref[...] += dqw_terms
            band_dk[...] = jnp.concatenate(dkn, axis=0)
            band_dv[...] = jnp.concatenate(dvv, axis=0)

        @pl.when(n == N_CH)
        def _():
            band_dk[...] = jnp.zeros_like(band_dk)
            band_dv[...] = jnp.zeros_like(band_dv)

        @pl.when(n >= 1)
        def _():
            dkn_prev = carry_dk[...] + band_dk[:, 0:CH]
            k_t = skp_ref[...]
            for kv in range(SWA_KV):
                rows = slice(kv * SWA_D, (kv + 1) * SWA_D)
                _, rk = _rms_t(k_t[rows], kw)
                dk_t, dkw_terms = _rms_t_bwd(dkn_prev[rows], k_t[rows], rk, kw)
                dk_ref[rows, :] = dk_t
                dkw_ref[...] += dkw_terms
            dv_ref[...] = carry_dv[...] + band_dv[:, 0:CH]

        carry_dk[...] = band_dk[:, CH:2 * CH]
        carry_dv[...] = band_dv[:, CH:2 * CH]

    col = lambda w: pl.BlockSpec((w, CH), lambda n: (0, jnp.minimum(n, N_CH - 1)))
    prev = lambda w: pl.BlockSpec((w, CH), lambda n: (0, jnp.maximum(n - 1, 0)))
    const = lambda shape: pl.BlockSpec(shape, lambda n: (0,) * len(shape))
    smem = pl.BlockSpec(memory_space=pltpu.SMEM)
    return pl.pallas_call(
        body, name="swa_bwd", grid=(N_CH + 1,),
        in_specs=[col(512), prev(128), col(128), prev(128), col(128), col(512), col(512), col(512),
                  const((SWA_D, 1)), const((SWA_D, 1)), smem, smem, const((2 * CH, CH))],
        out_specs=(col(512), col(512), prev(128), prev(128), const((2 * CH, HQ_LANES)), const((1, HQ_LANES)),
                   const((SWA_D, CH)), const((SWA_D, CH))),
        out_shape=(jax.ShapeDtypeStruct((512, T), F32), jax.ShapeDtypeStruct((512, T), F32),
                   jax.ShapeDtypeStruct((128, T), F32), jax.ShapeDtypeStruct((128, T), F32),
                   jax.ShapeDtypeStruct((2 * CH, HQ_LANES), F32), jax.ShapeDtypeStruct((1, HQ_LANES), F32),
                   jax.ShapeDtypeStruct((SWA_D, CH), F32), jax.ShapeDtypeStruct((SWA_D, CH), F32)),
        scratch_shapes=[pltpu.VMEM((2, 2 * CH, HQ_LANES), F32), pltpu.VMEM((1, HQ_LANES), F32),
                        pltpu.VMEM((128, 2 * CH), F32), pltpu.VMEM((128, 2 * CH), F32),
                        pltpu.VMEM((128, CH), F32), pltpu.VMEM((128, CH), F32)],
        compiler_params=_cparams(1),
    )(sq, sk, sk, sv, sv, sg, so, dmix, q_norm_w, k_norm_w, sinks, rel_bias_flat, bucket_t)


def _small_grads(dbias, bucket_t, dsink_acc, dqw_acc, dkw_acc, drw_acc):
    def body(dbias_ref, bucket_ref, dsink_ref, dqw_ref, dkw_ref, drw_ref, relb_o, sink_o, qw_o, kw_o, rw_o):
        bt = bucket_ref[...]
        row = lax.broadcasted_iota(jnp.int32, (N_BUCKETS, SWA_H), 0)
        col = lax.broadcasted_iota(jnp.int32, (N_BUCKETS, SWA_H), 1)
        lane = lax.broadcasted_iota(jnp.int32, (1, CH), 1)
        acc = jnp.zeros((N_BUCKETS, SWA_H), F32)
        sink = jnp.zeros((1, CH), F32)
        for hq in range(SWA_H):
            d = dbias_ref[:, _head_lanes(hq)]
            for bk in range(N_BUCKETS):
                s = jnp.sum(jnp.sum(jnp.where(bt == bk, d, 0.0), axis=0, keepdims=True), axis=1, keepdims=True)
                acc = acc + jnp.where((row == bk) & (col == hq), s, 0.0)
            sink = sink + jnp.where(lane == hq, jnp.sum(dsink_ref[:, _head_lanes(hq)], axis=1, keepdims=True), 0.0)
        relb_o[...] = acc
        sink_o[...] = sink
        for src, dst in ((dqw_ref, qw_o), (dkw_ref, kw_o)):
            padded = jnp.concatenate([src[...], jnp.zeros((CH - SWA_D, CH), F32)], axis=0)
            dst[...] = jnp.sum(padded.T, axis=0, keepdims=True)
        rw_o[...] = jnp.sum(drw_ref[...].T, axis=0, keepdims=True)

    vm = pl.BlockSpec(memory_space=pltpu.VMEM)
    row128 = jax.ShapeDtypeStruct((1, CH), F32)
    return pl.pallas_call(body, name="small_grads",
                          out_shape=(jax.ShapeDtypeStruct((N_BUCKETS, SWA_H), F32), row128, row128, row128,
                                     jax.ShapeDtypeStruct((1, RET_V), F32)),
                          in_specs=[vm] * 6, out_specs=(vm,) * 5, compiler_params=_cparams(),
                          )(dbias, bucket_t, dsink_acc, dqw_acc, dkw_acc, drw_acc)


def _out_proj(mix_r, mix_s, wo, x, target):
    def body(mr_ref, ms_ref, w_ref, x_ref, t_ref, loss_ref, dy_ref, dmr_ref, dms_ref, gw_ref):
        i = pl.program_id(0)

        @pl.when(i == 0)
        def _():
            loss_ref[...] = jnp.zeros_like(loss_ref)
            gw_ref[...] = jnp.zeros_like(gw_ref)

        mixed = jnp.concatenate([mr_ref[...], ms_ref[...]], axis=0)
        w = w_ref[...]
        err = x_ref[...] + _dot(_bf(mixed.T), w) - t_ref[...]
        loss_ref[...] += jnp.sum(jnp.sum(err * err, axis=1, keepdims=True), axis=0, keepdims=True)
        dy = err * (1.0 / D)
        dy_ref[...] = dy
        dy_b = _bf(dy)
        dmix = _dot_nt(w, dy_b)
        dmr_ref[...] = dmix[0:512]
        dms_ref[...] = dmix[512:D]
        gw_ref[...] += _dot(_bf(mixed), dy_b)

    row = lambda w: pl.BlockSpec((TM, w), lambda i: (i, 0))
    col = lambda w: pl.BlockSpec((w, TM), lambda i: (0, i))
    const = lambda shape: pl.BlockSpec(shape, lambda i: (0,) * len(shape))
    return pl.pallas_call(
        body, name="out_proj", grid=(T // TM,),
        in_specs=[col(512), col(512), const((D, D)), row(D), row(D)],
        out_specs=(const((1, 1)), row(D), col(512), col(512), const((D, D))),
        out_shape=(jax.ShapeDtypeStruct((1, 1), F32), jax.ShapeDtypeStruct((T, D), F32),
                   jax.ShapeDtypeStruct((512, T), F32), jax.ShapeDtypeStruct((512, T), F32),
                   jax.ShapeDtypeStruct((D, D), F32)),
        compiler_params=_cparams(1),
    )(mix_r, mix_s, wo, x, target)


def _in_proj_bwd(dpieces, x, rstd, norm_w, dy, wt):
    def body(*refs):
        piece_refs = refs[:8]
        x_ref, r_ref, nw_ref, dy_ref, w_ref, gx_ref, gw_ref, gnw_ref = refs[8:]
        i = pl.program_id(0)

        @pl.when(i == 0)
        def _():
            gw_ref[...] = jnp.zeros_like(gw_ref)
            gnw_ref[...] = jnp.zeros_like(gnw_ref)

        dp = jnp.concatenate([ref[...] for ref in piece_refs], axis=0)
        xv, r, nw = x_ref[...], r_ref[...], nw_ref[...]
        xr = xv * r
        dh = _dot(_bf(dp.T), w_ref[...])
        gw_ref[...] += _dot(_bf(dp), _bf(xr * nw))
        u = dh * nw
        gx_ref[...] = dy_ref[...] + r * u - xv * (r * r * r) * jnp.mean(u * xv, axis=-1, keepdims=True)
        gnw_ref[...] += jnp.sum(dh * xr, axis=0, keepdims=True)

    row = lambda w: pl.BlockSpec((TM, w), lambda i: (i, 0))
    col = lambda w: pl.BlockSpec((w, TM), lambda i: (0, i))
    const = lambda shape, **kw: pl.BlockSpec(shape, lambda i: (0,) * len(shape), **kw)
    once = dict(pipeline_mode=pl.Buffered(1))
    return pl.pallas_call(
        body, name="in_proj_bwd", grid=(T // TM,),
        in_specs=([col(w) for w in PIECES] + [row(D), row(1), const((1, D)), row(D), const((D_IN, D), **once)]),
        out_specs=(row(D), const((D_IN, D), **once), const((1, D))),
        out_shape=(jax.ShapeDtypeStruct((T, D), F32), jax.ShapeDtypeStruct((D_IN, D), F32),
                   jax.ShapeDtypeStruct((1, D), F32)),
        compiler_params=_cparams(1),
    )(*dpieces, x, rstd, norm_w, dy, wt)


SMALL_SHAPES = ((1, D), (1, 512), (1, SWA_D), (1, SWA_D), (1, SWA_H), (N_BUCKETS, SWA_H))
LOSS_ROW = 5


def _reduce_and_update(gwt, gwo, sse, small_g, w_in, m_in, v_in, w_out, m_out, v_out, small_w, small_m, small_v):
    n_small = len(SMALL_SHAPES)

    def body(*refs):
        gwt_ref, gwo_ref, sse_ref = refs[:3]
        sg_refs = refs[3:3 + n_small]
        w_in_ref, m_in_ref, v_in_ref, w_out_ref, m_out_ref, v_out_ref = refs[3 + n_small:9 + n_small]
        pos = 9 + n_small
        sw_refs, sm_refs, sv_refs = (refs[pos + i * n_small:pos + (i + 1) * n_small] for i in range(3))
        pos += 3 * n_small
        g_in_o, d_in_o, nm_in_o, nv_in_o, g_out_o, d_out_o, nm_out_o, nv_out_o = refs[pos:pos + 8]
        pos += 8
        sg_o, sd_o, snm_o, snv_o = (refs[pos + i * n_small:pos + (i + 1) * n_small] for i in range(4))
        pos += 4 * n_small
        loss_o = refs[pos]
        (own_t, own_o, rcv_t, rcv_o, snd_t, snd_o, got_t, got_o, tab_a, tab_b, sum_a, sum_b, pad_ref, tr_ref,
         loc_sems, a_send, a_recv, b_send, b_recv, s_send, s_recv) = refs[pos + 1:]
        x, y, c = _mesh_pos()
        me = 4 * x + 2 * y + c
        sibling = (x, y, 1 - c)
        all_chips = [(0, 0), (0, 1), (1, 0), (1, 1)]
        others = [(1 - x, y), (x, 1 - y), (1 - x, 1 - y)]

        def blk(px, py, pc):
            return 4 * px + 2 * py + pc

        mine_a, mine_b = tab_a.at[me], tab_b.at[me]
        mine_a[...] = jnp.zeros_like(mine_a)
        mine_b[...] = jnp.zeros_like(mine_b)
        for r in range(n_small - 1):
            mine_a[r:r + 1, 0:sg_refs[r].shape[1]] = sg_refs[r][...]
        mine_a[LOSS_ROW:LOSS_ROW + 1, 0:1] = sse_ref[...]
        mine_b[:, 0:SWA_H] = sg_refs[n_small - 1][...]
        small_copies = []
        for k in range(1, N_DEV):
            px, py, pc = x ^ (k >> 2), y ^ ((k >> 1) & 1), c ^ (k & 1)
            for a, tab in enumerate((tab_a, tab_b)):
                cp = pltpu.make_async_remote_copy(src_ref=tab.at[me], dst_ref=tab.at[me], send_sem=s_send.at[a, k - 1],
                                                  recv_sem=s_recv.at[a, k - 1], device_id=(px, py, pc),
                                                  device_id_type=MESH)
                cp.start()
                small_copies.append(cp)

        stage_a, local = [], []
        for j, (px, py) in enumerate(all_chips):
            for a, (src, dst) in enumerate(((gwt_ref, rcv_t), (gwo_ref, rcv_o))):
                cp = pltpu.make_async_remote_copy(src_ref=src.at[blk(px, py, 1 - c)], dst_ref=dst.at[j],
                                                  send_sem=a_send.at[a, j], recv_sem=a_recv.at[a, j],
                                                  device_id=sibling, device_id_type=MESH)
                cp.start()
                stage_a.append(cp)
            for a, (src, dst) in enumerate(((gwt_ref, own_t), (gwo_ref, own_o))):
                cp = pltpu.make_async_copy(src.at[blk(px, py, c)], dst.at[j], loc_sems.at[a, j])
                cp.start()
                local.append(cp)
        for cp in local:
            cp.wait()
        for cp in stage_a:
            cp.wait_recv()

        stage_b = []
        for k, (px, py) in enumerate(others):
            j = 2 * px + py
            snd_t[k] = _bf(own_t[j] + rcv_t[j])
            snd_o[k] = _bf(own_o[j] + rcv_o[j])
            for a, (src, dst) in enumerate(((snd_t, got_t), (snd_o, got_o))):
                cp = pltpu.make_async_remote_copy(src_ref=src.at[k], dst_ref=dst.at[k], send_sem=b_send.at[a, k],
                                                  recv_sem=b_recv.at[a, k], device_id=(px, py, c), device_id_type=MESH)
                cp.start()
                stage_b.append(cp)
        jm = 2 * x + y
        tot_t = own_t[jm] + rcv_t[jm]
        tot_o = own_o[jm] + rcv_o[jm]
        for k in range(3):
            stage_b[2 * k].wait_recv()
            stage_b[2 * k + 1].wait_recv()
            tot_t = tot_t + got_t[k].astype(F32)
            tot_o = tot_o + got_o[k].astype(F32)

        d, nm, nv = _adamw(w_out_ref[...], tot_o, m_out_ref[...], v_out_ref[...])
        g_out_o[...], d_out_o[...], nm_out_o[...], nv_out_o[...] = tot_o, d, nm, nv

        pad_ref[0:WIN_BLK, :] = tot_t
        pad_ref[WIN_BLK:WIN_PAD, :] = jnp.zeros((WIN_PAD - WIN_BLK, D), F32)
        tr_ref[...] = pad_ref[...].T
        g = tr_ref[:, 0:WIN_BLK]
        d, nm, nv = _adamw(w_in_ref[...], g, m_in_ref[...], v_in_ref[...])
        g_in_o[...], d_in_o[...], nm_in_o[...], nv_in_o[...] = g, d, nm, nv

        for cp in small_copies:
            cp.wait_recv()
        tot_a, tot_b = tab_a[0], tab_b[0]
        for b in range(1, N_DEV):
            tot_a = tot_a + tab_a[b]
            tot_b = tot_b + tab_b[b]
        sum_a[...] = tot_a
        sum_b[...] = tot_b
        loss_o[...] = sum_a[LOSS_ROW:LOSS_ROW + 1, 0:1] * (0.5 / D)
        for r, (rows, lanes) in enumerate(SMALL_SHAPES):
            g = sum_a[r:r + 1, 0:lanes] if rows == 1 else sum_b[:, 0:lanes]
            d, nm, nv = _adamw(sw_refs[r][...], g, sm_refs[r][...], sv_refs[r][...])
            sg_o[r][...], sd_o[r][...], snm_o[r][...], snv_o[r][...] = g, d, nm, nv

        for cp in stage_a + stage_b + small_copies:
            cp.wait_send()

    vm = pl.BlockSpec(memory_space=pltpu.VMEM)
    hbm = pl.BlockSpec(memory_space=pl.ANY)
    win = jax.ShapeDtypeStruct((D, WIN_BLK), F32)
    wout = jax.ShapeDtypeStruct((WOUT_BLK, D), F32)
    smalls = tuple(jax.ShapeDtypeStruct(s, F32) for s in SMALL_SHAPES)
    outs = pl.pallas_call(
        body, name="reduce_and_update",
        in_specs=[hbm, hbm] + [vm] * (7 + 4 * n_small), out_specs=(vm,) * (9 + 4 * n_small),
        out_shape=(win,) * 4 + (wout,) * 4 + smalls * 4 + (jax.ShapeDtypeStruct((1, 1), F32),),
        scratch_shapes=[
            pltpu.VMEM((4, WIN_BLK, D), F32), pltpu.VMEM((4, WOUT_BLK, D), F32),
            pltpu.VMEM((4, WIN_BLK, D), F32), pltpu.VMEM((4, WOUT_BLK, D), F32),
            pltpu.VMEM((3, WIN_BLK, D), BF16), pltpu.VMEM((3, WOUT_BLK, D), BF16),
            pltpu.VMEM((3, WIN_BLK, D), BF16), pltpu.VMEM((3, WOUT_BLK, D), BF16),
            pltpu.VMEM((N_DEV, 8, D), F32), pltpu.VMEM((N_DEV, N_BUCKETS, CH), F32),
            pltpu.VMEM((8, D), F32), pltpu.VMEM((N_BUCKETS, CH), F32),
            pltpu.VMEM((WIN_PAD, D), F32), pltpu.VMEM((D, WIN_PAD), F32),
            pltpu.SemaphoreType.DMA((2, 4)),
            pltpu.SemaphoreType.DMA((2, 4)), pltpu.SemaphoreType.DMA((2, 4)),
            pltpu.SemaphoreType.DMA((2, 3)), pltpu.SemaphoreType.DMA((2, 3)),
            pltpu.SemaphoreType.DMA((2, 7)), pltpu.SemaphoreType.DMA((2, 7)),
        ],
        compiler_params=_cparams(),
    )(gwt, gwo, sse, *small_g, w_in, m_in, v_in, w_out, m_out, v_out, *small_w, *small_m, *small_v)
    big, rest = outs[:8], outs[8:]
    return big, [rest[i * n_small:(i + 1) * n_small] for i in range(4)], rest[4 * n_small]


def _small_rows(norm_w, ret_norm_w, q_norm_w, k_norm_w, sinks, rel_bias):
    return (norm_w.reshape(1, D), ret_norm_w.reshape(1, 512), q_norm_w.reshape(1, SWA_D), k_norm_w.reshape(1, SWA_D),
            sinks.reshape(1, SWA_H), rel_bias)


def _small_leaves(rows):
    return (rows[0].reshape(D), rows[1].reshape(512), rows[2].reshape(SWA_D), rows[3].reshape(SWA_D),
            rows[4].reshape(SWA_H), rows[5])


def kernel(x, norm_w, w_in, ret_norm_w, q_norm_w, k_norm_w, sinks, rel_bias, w_out, loss_target, m_norm_w, m_w_in, m_ret_norm_w, m_q_norm_w, m_k_norm_w, m_sinks, m_rel_bias, m_w_out, v_norm_w, v_w_in, v_ret_norm_w, v_q_norm_w, v_k_norm_w, v_sinks, v_rel_bias, v_w_out):
    x2 = x.reshape(T, D)
    target = loss_target.reshape(T, D)
    nw = norm_w.reshape(1, D)
    rnw_col = ret_norm_w.reshape(RET_V, 1)
    qnw_col = q_norm_w.reshape(SWA_D, 1)
    knw_col = k_norm_w.reshape(SWA_D, 1)
    relb = rel_bias.reshape(N_BUCKETS * SWA_H)
    ret_tables = _rotary_tables_t() + _retention_tables_t()
    bucket_t = _bucket_table_t()

    wt, wo = _gather_weights(w_in, w_out)

    rq, rk, rv, rg, sq, sk, sv, sg, rstd = _in_proj(x2, nw, wt)
    ro, mix_r, states = _retention_fwd(rq, rk, rv, rg, rnw_col, ret_tables)
    so, mix_s = _swa_fwd(sq, sk, sv, sg, qnw_col, knw_col, sinks, relb, bucket_t)
    sse, dy, dmix_r, dmix_s, gwo = _out_proj(mix_r, mix_s, wo, x2, target)

    drq, drk, drv, drg, drw_acc = _retention_bwd(rq, rk, rv, rg, ro, states, dmix_r, rnw_col, ret_tables)
    dsq, dsg, dsk, dsv, dbias, dsink_acc, dqw_acc, dkw_acc = _swa_bwd(sq, sk, sv, sg, so, dmix_s, qnw_col, knw_col,
                                                                      sinks, relb, bucket_t)
    d_relb, d_sinks, d_qnw, d_knw, d_rnw = _small_grads(dbias, bucket_t, dsink_acc, dqw_acc, dkw_acc, drw_acc)
    grad_x, gwt, d_nw = _in_proj_bwd((drq, drk, drv, drg, dsq, dsk, dsv, dsg), x2, rstd, nw, dy, wt)

    small_g = (d_nw, d_rnw, d_qnw, d_knw, d_sinks, d_relb)
    small_w = _small_rows(norm_w, ret_norm_w, q_norm_w, k_norm_w, sinks, rel_bias)
    small_m = _small_rows(m_norm_w, m_ret_norm_w, m_q_norm_w, m_k_norm_w, m_sinks, m_rel_bias)
    small_v = _small_rows(v_norm_w, v_ret_norm_w, v_q_norm_w, v_k_norm_w, v_sinks, v_rel_bias)

    big, small, loss = _reduce_and_update(
        gwt.reshape(N_DEV, WIN_BLK, D), gwo.reshape(N_DEV, WOUT_BLK, D), sse, small_g,
        w_in, m_w_in, v_w_in, w_out, m_w_out, v_w_out, small_w, small_m, small_v)

    def leaves(i):
        a = _small_leaves(small[i])
        return (a[0], big[i], a[1], a[2], a[3], a[4], a[5], big[4 + i])

    return (loss.reshape(()), grad_x.reshape(1, T, D), *leaves(0), *leaves(1), *leaves(2), *leaves(3))
```

```python
import numpy as np
import jax
import jax.numpy as jnp
from jax import lax
from jax.experimental import pallas as pl
from jax.experimental.pallas import tpu as pltpu

F32 = jnp.float32
BF16 = jnp.bfloat16
MESH = pl.DeviceIdType.MESH

T = 2048
D = 1024
D_IN = 2816
N_DEV = 8
WIN_BLK = D_IN // N_DEV
WIN_PAD = 384
WOUT_BLK = D // N_DEV
CH = 128
N_CH = T // CH
RET_H, RET_DK, RET_DV = 4, 64, 128
RET_QK = RET_H * RET_DK
RET_V = RET_H * RET_DV
SWA_H, SWA_KV, SWA_D, SWA_G = 8, 2, 64, 4
N_BUCKETS = 32
NORM_EPS = 1e-6
GN_EPS = 1e-5
NEG_INF = -1e30
PIECES = (256, 256, 512, 512, 512, 128, 128, 512)
OFFS = tuple(int(v) for v in np.cumsum((0,) + PIECES))
TM = 256

ADAM_LR, ADAM_B1, ADAM_B2, ADAM_EPS, ADAM_WD, ADAM_STEP = 0.001, 0.9, 0.999, 1e-08, 0.01, 10

VMEM_LIMIT = 56 * 1024 * 1024


def _cparams(n_grid=0, vmem=VMEM_LIMIT):
    sem = ("arbitrary",) * n_grid if n_grid else None
    return pltpu.CompilerParams(dimension_semantics=sem, vmem_limit_bytes=vmem)


def _dot(a, b):
    return jnp.dot(a, b, preferred_element_type=F32)


def _dot_nt(a, b):
    return lax.dot_general(a, b, (((1,), (1,)), ((), ())), preferred_element_type=F32)


def _bf(a):
    return a.astype(BF16)


def _sigmoid(x):
    return 1.0 / (1.0 + jnp.exp(-x))


def _adamw(w, g, m, v):
    m = ADAM_B1 * m + (1.0 - ADAM_B1) * g
    v = ADAM_B2 * v + (1.0 - ADAM_B2) * (g * g)
    m_hat = m / (1.0 - ADAM_B1 ** ADAM_STEP)
    v_hat = v / (1.0 - ADAM_B2 ** ADAM_STEP)
    delta = -ADAM_LR * (m_hat / (jnp.sqrt(v_hat) + ADAM_EPS) + ADAM_WD * w)
    return delta, m, v


def _rotary_tables_t():
    half = RET_DK // 2
    inv_freq = np.float32(10000.0) ** (-np.arange(half, dtype=np.float32) / np.float32(half))
    ang = inv_freq[:, None] * np.arange(T, dtype=np.float32)[None, :]
    cos, sin = np.cos(ang).astype(np.float32), np.sin(ang).astype(np.float32)
    cos64 = np.concatenate([cos, cos], axis=0)
    sin64 = np.concatenate([-sin, sin], axis=0)
    return np.tile(cos64, (RET_H, 1)), np.tile(sin64, (RET_H, 1))


def _retention_tables_t():
    gamma = (1.0 - np.exp2(-5.0 - np.arange(RET_H, dtype=np.float32))).astype(np.float32)
    log_g = np.log(gamma).astype(np.float32)
    i = np.arange(CH, dtype=np.float32)
    diff = i[None, :] - i[:, None]
    decay = np.where(diff >= 0, np.exp(log_g[:, None, None] * np.maximum(diff, 0.0)), 0.0).astype(np.float32)
    decay_all = np.concatenate(list(decay), axis=1)
    zeta = np.exp(log_g[:, None] * (CH - 1.0 - i)).astype(np.float32)
    zeta_tab = np.repeat(zeta.T, RET_DK, axis=1)
    xi = np.exp(log_g[:, None] * (i + 1.0)).astype(np.float32)
    xi_tab = np.repeat(xi, RET_DK, axis=0)
    chunk_decay = np.exp(log_g * np.float32(CH)).astype(np.float32)
    row_head = np.arange(RET_V)[:, None] // RET_DV
    col_head = np.arange(RET_QK)[None, :] // RET_DK
    state_mask = (row_head == col_head).astype(np.float32)
    state_decay = (state_mask * chunk_decay[row_head]).astype(np.float32)
    q_mask = (np.arange(RET_QK)[:, None] // RET_DK == np.arange(RET_H * CH)[None, :] // CH).astype(np.float32)
    return (np.ascontiguousarray(decay_all), np.ascontiguousarray(zeta_tab), np.ascontiguousarray(xi_tab),
            state_mask, state_decay, q_mask)


def _bucket_table_t():
    qi = np.arange(CH)[:, None]
    kj = np.arange(2 * CH)[None, :]
    dist = qi + CH - kj
    n = np.maximum(dist, 0)
    max_exact = N_BUCKETS // 2
    nf = np.maximum(n, 1).astype(np.float32)
    large = max_exact + (np.log(nf / np.float32(max_exact)) / np.float32(np.log(CH / max_exact))
                         * np.float32(N_BUCKETS - max_exact)).astype(np.int32)
    large = np.minimum(large, N_BUCKETS - 1)
    bucket = np.where(n < max_exact, n, large)
    return np.ascontiguousarray(np.where((dist >= 0) & (dist < CH), bucket, -1).astype(np.int32).T)


def _mesh_pos():
    return lax.axis_index("x"), lax.axis_index("y"), lax.axis_index("c")


def _blk(px, py, pc):
    return 4 * px + 2 * py + pc


def _in_proj_gather(x, norm_w, w_in_blk):
    tchunk = 512

    def body(x_ref, nw_ref, win_ref, proj_ref, wt_ref, rstd_ref, h_ref, pad_ref, stage, send_sems, recv_sems, out_sems):
        x, y, c = _mesh_pos()
        me = _blk(x, y, c)
        sibling = (x, y, 1 - c)
        chips = [(1 - x, y), (x, 1 - y), (1 - x, 1 - y)]

        def copy(k, b, to):
            return pltpu.make_async_remote_copy(src_ref=wt_ref.at[b], dst_ref=wt_ref.at[b], send_sem=send_sems.at[k],
                                                recv_sem=recv_sems.at[k], device_id=to, device_id_type=MESH)

        pad_ref[...] = jnp.zeros_like(pad_ref)
        pad_ref[:, 0:WIN_BLK] = win_ref[...]
        wt_ref[me] = _bf(pad_ref[...].T[0:WIN_BLK, :])
        first = [copy(0, me, sibling)] + [copy(1 + j, me, (*chip, c)) for j, chip in enumerate(chips)]
        for cp in first:
            cp.start()

        nw = nw_ref[...]
        for i in range(T // TM):
            rows = slice(i * TM, (i + 1) * TM)
            xv = x_ref[rows, :]
            r = lax.rsqrt(jnp.mean(xv * xv, axis=-1, keepdims=True) + NORM_EPS)
            h_ref[rows, :] = _bf(xv * r * nw)
            rstd_ref[rows, :] = r

        writes = []

        def project(b):
            k = len(writes)
            if k >= 2:
                writes[k - 2].wait()
            w = wt_ref[b]
            for t in range(T // tchunk):
                cols = slice(t * tchunk, (t + 1) * tchunk)
                stage[k % 2, :, cols] = _dot_nt(w, h_ref[cols, :])
            cp = pltpu.make_async_copy(stage.at[k % 2], proj_ref.at[b], out_sems.at[k % 2])
            cp.start()
            writes.append(cp)

        project(me)
        copy(0, _blk(x, y, 1 - c), (x, y, c)).wait_recv()
        project(_blk(x, y, 1 - c))
        passed = []
        for j, chip in enumerate(chips):
            copy(1 + j, _blk(*chip, c), (x, y, c)).wait_recv()
            fwd = copy(4 + j, _blk(*chip, c), sibling)
            fwd.start()
            passed.append(fwd)
            project(_blk(*chip, c))
            copy(4 + j, _blk(*chip, 1 - c), (x, y, c)).wait_recv()
            project(_blk(*chip, 1 - c))
        writes[-2].wait()
        writes[-1].wait()
        for cp in first + passed:
            cp.wait_send()

    vm = pl.BlockSpec(memory_space=pltpu.VMEM)
    proj, wt, rstd, h = pl.pallas_call(
        body, name="in_proj_gather",
        out_shape=(jax.ShapeDtypeStruct((N_DEV, WIN_BLK, T), F32), jax.ShapeDtypeStruct((N_DEV, WIN_BLK, D), BF16),
                   jax.ShapeDtypeStruct((T, 1), F32), jax.ShapeDtypeStruct((T, D), BF16)),
        in_specs=[vm, vm, vm], out_specs=(pl.BlockSpec(memory_space=pl.ANY), vm, vm, vm),
        scratch_shapes=[pltpu.VMEM((D, WIN_PAD), F32), pltpu.VMEM((2, WIN_BLK, T), F32),
                        pltpu.SemaphoreType.DMA((7,)), pltpu.SemaphoreType.DMA((7,)), pltpu.SemaphoreType.DMA((2,))],
        compiler_params=_cparams(vmem=48 * 1024 * 1024),
    )(x, norm_w, w_in_blk)
    return proj.reshape(D_IN, T), wt.reshape(D_IN, D), rstd, h


def _gather_w_out(w_out_ref, own_ref, wo_ref, local_sem, send_sems, recv_sems, start):
    x, y, c = _mesh_pos()
    me = _blk(x, y, c)
    local = pltpu.make_async_copy(own_ref, wo_ref.at[me], local_sem)
    copies = [pltpu.make_async_remote_copy(src_ref=own_ref, dst_ref=wo_ref.at[me], send_sem=send_sems.at[k - 1],
                                           recv_sem=recv_sems.at[k - 1],
                                           device_id=(x ^ (k >> 2), y ^ ((k >> 1) & 1), c ^ (k & 1)), device_id_type=MESH)
              for k in range(1, N_DEV)]
    if start:
        own_ref[...] = _bf(w_out_ref[...])
        local.start()
        for cp in copies:
            cp.start()
    else:
        local.wait()
        for cp in copies:
            cp.wait_recv()
        for cp in copies:
            cp.wait_send()


def _proj_piece(i, block_of):
    rows = PIECES[i]
    assert OFFS[i] % rows == 0
    return pl.BlockSpec((rows, CH), lambda n: (OFFS[i] // rows, block_of(n)))


def _proj_gate_halves(block_of):
    return [pl.BlockSpec((256, CH), lambda n, j=j: (OFFS[7] // 256 + j, block_of(n))) for j in range(2)]


def _swap_halves_t(t):
    half = RET_DK // 2
    parts = []
    for h in range(RET_H):
        parts += [t[h * RET_DK + half:(h + 1) * RET_DK], t[h * RET_DK:h * RET_DK + half]]
    return jnp.concatenate(parts, axis=0)


def _rotate(t, cos, sin):
    return t * cos + _swap_halves_t(t) * sin


def _group_norm_t(o):
    mu = jnp.mean(o, axis=0, keepdims=True)
    var = jnp.mean((o - mu) * (o - mu), axis=0, keepdims=True)
    rstd = lax.rsqrt(var + GN_EPS)
    return (o - mu) * rstd, rstd


def _retention_scores_t(q_t, k_rm_b, q_mask, decay_all):
    q_heads = _bf(jnp.concatenate([q_t] * RET_H, axis=1) * q_mask)
    return _dot(k_rm_b, q_heads) * decay_all


def _retention_fwd(proj, ret_norm_w, tables, w_out_blk):
    cos_t, sin_t, decay_all, zeta_tab, xi_tab, state_mask, state_decay, q_mask = tables

    def body(rq_ref, rk_ref, rv_ref, rg_ref, w_ref, cos_ref, sin_ref, decay_ref, zeta_ref, xi_ref, smask_ref,
             sdecay_ref, qmask_ref, wout_ref, ro_ref, mix_ref, st_ref, wo_ref, state, wo_own, wo_local, wo_send, wo_recv):
        n = pl.program_id(0)

        @pl.when(n == 0)
        def _():
            state[...] = jnp.zeros_like(state)
            _gather_w_out(wout_ref, wo_own, wo_ref, wo_local, wo_send, wo_recv, start=True)

        @pl.when(n == N_CH - 1)
        def _():
            _gather_w_out(wout_ref, wo_own, wo_ref, wo_local, wo_send, wo_recv, start=False)

        cos, sin = cos_ref[...], sin_ref[...]
        q_t = _rotate(rq_ref[...], cos, sin)
        k_t = _rotate(rk_ref[...], cos, sin) * (RET_DK ** -0.5)
        k_rm = k_t.T
        v_b = _bf(rv_ref[...])
        m_b = _bf(state[...])
        st_ref[0] = m_b
        scores_b = _bf(_retention_scores_t(q_t, _bf(k_rm), qmask_ref[...], decay_ref[...]))
        cross = _dot(m_b, _bf(q_t * xi_ref[...]))
        state[...] = state[...] * sdecay_ref[...] + _dot(v_b, _bf(k_rm * zeta_ref[...])) * smask_ref[...]
        heads = range(RET_H)
        rows = [slice(h * RET_DV, (h + 1) * RET_DV) for h in heads]
        o = [_dot(v_b[rows[h]], scores_b[:, h * CH:(h + 1) * CH]) + cross[rows[h]] for h in heads]
        for h in heads:
            rn, _ = _group_norm_t(o[h])
            g = rg_ref[rows[h], :]
            ro_ref[rows[h], :] = o[h]
            mix_ref[rows[h], :] = rn * w_ref[rows[h], :] * (g * _sigmoid(g))

    col = lambda w: pl.BlockSpec((w, CH), lambda n: (0, n))
    const = lambda shape: pl.BlockSpec(shape, lambda n: (0,) * len(shape))
    cur = lambda n: n
    ro, mix, states, wo = pl.pallas_call(
        body, name="retention_fwd", grid=(N_CH,),
        in_specs=[_proj_piece(i, cur) for i in range(4)] + [
            const((RET_V, 1)), col(RET_QK), col(RET_QK),
            const(decay_all.shape), const(zeta_tab.shape), const(xi_tab.shape), const(state_mask.shape),
            const(state_decay.shape), const(q_mask.shape), const((WOUT_BLK, D))],
        out_specs=(col(RET_V), col(RET_V), pl.BlockSpec((1, RET_V, RET_QK), lambda n: (n, 0, 0)),
                   pl.BlockSpec(memory_space=pl.ANY)),
        out_shape=(jax.ShapeDtypeStruct((RET_V, T), F32), jax.ShapeDtypeStruct((RET_V, T), F32),
                   jax.ShapeDtypeStruct((N_CH, RET_V, RET_QK), BF16), jax.ShapeDtypeStruct((N_DEV, WOUT_BLK, D), BF16)),
        scratch_shapes=[pltpu.VMEM((RET_V, RET_QK), F32), pltpu.VMEM((WOUT_BLK, D), BF16), pltpu.SemaphoreType.DMA,
                        pltpu.SemaphoreType.DMA((N_DEV - 1,)), pltpu.SemaphoreType.DMA((N_DEV - 1,))],
        compiler_params=_cparams(1),
    )(proj, proj, proj, proj, ret_norm_w, cos_t, sin_t, decay_all, zeta_tab, xi_tab, state_mask, state_decay, q_mask,
      w_out_blk)
    return ro, mix, states, wo.reshape(D, D)


def _retention_bwd(proj, ro, states, dmix, ret_norm_w, tables):
    cos_t, sin_t, decay_all, zeta_tab, xi_tab, state_mask, state_decay, q_mask = tables

    def body(rq_ref, rk_ref, rv_ref, rg_ref, ro_ref, st_ref, dm_ref, w_ref, cos_ref, sin_ref, decay_ref, zeta_ref,
             xi_ref, smask_ref, sdecay_ref, qmask_ref, dq_ref, dk_ref, dv_ref, dg_ref, dw_ref, gstate):
        i = pl.program_id(0)

        @pl.when(i == 0)
        def _():
            gstate[...] = jnp.zeros_like(gstate)
            dw_ref[...] = jnp.zeros_like(dw_ref)

        cos, sin = cos_ref[...], sin_ref[...]
        q_t = _rotate(rq_ref[...], cos, sin)
        k_t = _rotate(rk_ref[...], cos, sin) * (RET_DK ** -0.5)
        q_b, k_b = _bf(q_t), _bf(k_t)
        k_rm = k_t.T
        kz_b = _bf(k_rm * zeta_ref[...])
        qx_b = _bf(q_t * xi_ref[...])
        v_t = rv_ref[...]
        v_b = _bf(v_t)
        v_rm_b = _bf(v_t.T)
        decay = decay_ref[...]
        scores_b = _bf(_retention_scores_t(q_t, _bf(k_rm), qmask_ref[...], decay))
        heads = range(RET_H)
        rows = [slice(h * RET_DV, (h + 1) * RET_DV) for h in heads]
        qk_rows = [slice(h * RET_DK, (h + 1) * RET_DK) for h in heads]
        lanes = [slice(h * CH, (h + 1) * CH) for h in heads]
        do = []
        for h in heads:
            g, w, dm = rg_ref[rows[h], :], w_ref[rows[h], :], dm_ref[rows[h], :]
            rn, rstd = _group_norm_t(ro_ref[rows[h], :])
            sig = _sigmoid(g)
            silu = g * sig
            dg_ref[rows[h], :] = dm * rn * w * (sig * (1.0 + g * (1.0 - sig)))
            dw_ref[rows[h], :] += dm * silu * rn
            drn = dm * silu * w
            do.append(rstd * (drn - jnp.mean(drn, axis=0, keepdims=True)
                              - rn * jnp.mean(drn * rn, axis=0, keepdims=True)))
        do_b = _bf(jnp.concatenate(do, axis=0))
        m_b = st_ref[0]
        g_all = gstate[...]
        g_b = _bf(g_all)
        dscores_b = [_bf(_dot(v_rm_b[:, rows[h]], do_b[rows[h]]) * decay[:, lanes[h]]) for h in heads]
        dq_cross = lax.dot_general(m_b, do_b, (((0,), (0,)), ((), ())), preferred_element_type=F32)
        dkz = _dot(v_rm_b, g_b)
        dv_cross = _dot_nt(g_b, kz_b)
        gstate[...] = g_all * sdecay_ref[...] + _dot_nt(do_b, qx_b) * smask_ref[...]
        dq = jnp.concatenate([_dot(k_b[qk_rows[h]], dscores_b[h]) for h in heads], axis=0) + dq_cross * xi_ref[...]
        dk = (jnp.concatenate([_dot_nt(q_b[qk_rows[h]], dscores_b[h]) for h in heads], axis=0)
              + (dkz * zeta_ref[...]).T) * (RET_DK ** -0.5)
        for h in heads:
            dv_ref[rows[h], :] = _dot_nt(do_b[rows[h]], scores_b[:, lanes[h]]) + dv_cross[rows[h]]
        dq_ref[...] = dq * cos + _swap_halves_t(dq * sin)
        dk_ref[...] = dk * cos + _swap_halves_t(dk * sin)

    col = lambda w: pl.BlockSpec((w, CH), lambda i: (0, N_CH - 1 - i))
    const = lambda shape: pl.BlockSpec(shape, lambda i: (0,) * len(shape))
    return pl.pallas_call(
        body, name="retention_bwd", grid=(N_CH,),
        in_specs=[_proj_piece(j, lambda i: N_CH - 1 - i) for j in range(4)] + [
                  col(RET_V),
                  pl.BlockSpec((1, RET_V, RET_QK), lambda i: (N_CH - 1 - i, 0, 0)), col(RET_V),
                  const((RET_V, 1)), col(RET_QK), col(RET_QK),
                  const(decay_all.shape), const(zeta_tab.shape), const(xi_tab.shape), const(state_mask.shape),
                  const(state_decay.shape), const(q_mask.shape)],
        out_specs=(col(RET_QK), col(RET_QK), col(RET_V), col(RET_V), const((RET_V, CH))),
        out_shape=(jax.ShapeDtypeStruct((RET_QK, T), F32), jax.ShapeDtypeStruct((RET_QK, T), F32),
                   jax.ShapeDtypeStruct((RET_V, T), F32), jax.ShapeDtypeStruct((RET_V, T), F32),
                   jax.ShapeDtypeStruct((RET_V, CH), F32)),
        scratch_shapes=[pltpu.VMEM((RET_V, RET_QK), F32)],
        compiler_params=_cparams(1),
    )(proj, proj, proj, proj, ro, states, dmix, ret_norm_w, cos_t, sin_t, decay_all, zeta_tab, xi_tab, state_mask,
      state_decay, q_mask)


HQ_LANES = SWA_H * CH


def _head_lanes(hq):
    return slice(hq * CH, (hq + 1) * CH)


def _build_bias_t(bucket_ref, relb_ref, sinks_ref, bias_ref, sink_row):
    bt = bucket_ref[...]
    first = lax.broadcasted_iota(jnp.int32, bt.shape, 0) < CH
    for hq in range(SWA_H):
        b = jnp.full(bt.shape, NEG_INF, F32)
        for bk in range(N_BUCKETS):
            b = jnp.where(bt == bk, relb_ref[bk * SWA_H + hq], b)
        bias_ref[0, :, _head_lanes(hq)] = b
        bias_ref[1, :, _head_lanes(hq)] = jnp.where(first, NEG_INF, b)
        sink_row[:, _head_lanes(hq)] = jnp.full((1, CH), sinks_ref[hq], F32)


def _rms_t(t, w_col):
    r = lax.rsqrt(jnp.mean(t * t, axis=0, keepdims=True) + NORM_EPS)
    return t * r * w_col, r


def _rms_t_bwd(dn, t, r, w_col):
    u = dn * w_col
    return r * u - t * (r * r * r) * jnp.mean(u * t, axis=0, keepdims=True), dn * t * r


def _norm_kv(k_t, kw):
    return jnp.concatenate([_rms_t(k_t[g * SWA_D:(g + 1) * SWA_D], kw)[0] for g in range(SWA_KV)], axis=0)


def _kv_slot(a, kv):
    z = jnp.zeros_like(a)
    return jnp.concatenate([a, z] if kv == 0 else [z, a], axis=0)


def _softmax_t(s, sink):
    m = jnp.maximum(jnp.max(s, axis=0, keepdims=True), sink)
    p = jnp.exp(s - m)
    e_sink = jnp.exp(sink - m)
    inv = 1.0 / (jnp.sum(p, axis=0, keepdims=True) + e_sink)
    return p * inv, e_sink * inv


def _gate_rows(sg_halves, hq):
    per_half = SWA_H // 2
    return sg_halves[hq // per_half][(hq % per_half) * SWA_D:(hq % per_half + 1) * SWA_D, :]


def _swa_fwd(proj, q_norm_w, k_norm_w, sinks, rel_bias_flat, bucket_t):
    def body(sq_ref, skp_ref, skc_ref, svp_ref, svc_ref, sg_lo, sg_hi, qw_ref, kw_ref, sinks_ref, relb_ref, bucket_ref,
             so_ref, mix_ref, bias_ref, sink_row):
        n = pl.program_id(0)

        @pl.when(n == 0)
        def _():
            _build_bias_t(bucket_ref, relb_ref, sinks_ref, bias_ref, sink_row)

        var = (n == 0).astype(jnp.int32)
        qw, kw = qw_ref[...], kw_ref[...]
        kn_band = jnp.concatenate([_norm_kv(skp_ref[...], kw), _norm_kv(skc_ref[...], kw)], axis=1)
        kb_rm = _bf(kn_band.T)
        vband = _bf(jnp.concatenate([svp_ref[...], svc_ref[...]], axis=1))
        q_all = jnp.concatenate(
            [_bf(_kv_slot(_rms_t(sq_ref[hq * SWA_D:(hq + 1) * SWA_D, :], qw)[0] * (SWA_D ** -0.5), hq // SWA_G))
             for hq in range(SWA_H)], axis=1)
        probs, _ = _softmax_t(_dot(kb_rm, q_all) + bias_ref[var], sink_row[...])
        probs_b = _bf(probs)
        for kv in range(SWA_KV):
            o = _dot(vband[kv * SWA_D:(kv + 1) * SWA_D], probs_b[:, kv * SWA_G * CH:(kv + 1) * SWA_G * CH])
            for j in range(SWA_G):
                rows = slice((kv * SWA_G + j) * SWA_D, (kv * SWA_G + j + 1) * SWA_D)
                oh = o[:, j * CH:(j + 1) * CH]
                g = _gate_rows((sg_lo, sg_hi), kv * SWA_G + j)
                so_ref[rows, :] = oh
                mix_ref[rows, :] = oh * (g * _sigmoid(g))

    col = lambda w: pl.BlockSpec((w, CH), lambda n: (0, n))
    const = lambda shape: pl.BlockSpec(shape, lambda n: (0,) * len(shape))
    smem = pl.BlockSpec(memory_space=pltpu.SMEM)
    cur = lambda n: n
    prev = lambda n: jnp.maximum(n - 1, 0)
    return pl.pallas_call(
        body, name="swa_fwd", grid=(N_CH,),
        in_specs=[_proj_piece(4, cur), _proj_piece(5, prev), _proj_piece(5, cur), _proj_piece(6, prev),
                  _proj_piece(6, cur)] + _proj_gate_halves(cur) + [
                  const((SWA_D, 1)), const((SWA_D, 1)), smem, smem, const((2 * CH, CH))],
        out_specs=(col(512), col(512)),
        out_shape=(jax.ShapeDtypeStruct((512, T), F32), jax.ShapeDtypeStruct((512, T), F32)),
        scratch_shapes=[pltpu.VMEM((2, 2 * CH, HQ_LANES), F32), pltpu.VMEM((1, HQ_LANES), F32)],
        compiler_params=_cparams(1),
    )(proj, proj, proj, proj, proj, proj, proj, q_norm_w, k_norm_w, sinks, rel_bias_flat, bucket_t)


def _swa_bwd(proj, so, dmix, q_norm_w, k_norm_w, sinks, rel_bias_flat, bucket_t):
    def body(sq_ref, skp_ref, skc_ref, svp_ref, svc_ref, sg_lo, sg_hi, so_ref, dm_ref, qw_ref, kw_ref, sinks_ref,
             relb_ref, bucket_ref, dq_ref, dg_ref, dk_ref, dv_ref, dbias_ref, dsink_ref, dqw_ref, dkw_ref,
             bias_ref, sink_row, band_dk, band_dv, carry_dk, carry_dv):
        n = pl.program_id(0)

        @pl.when(n == 0)
        def _():
            _build_bias_t(bucket_ref, relb_ref, sinks_ref, bias_ref, sink_row)
            for ref in (dbias_ref, dsink_ref, dqw_ref, dkw_ref, carry_dk, carry_dv):
                ref[...] = jnp.zeros_like(ref)

        qw, kw = qw_ref[...], kw_ref[...]

        @pl.when(n < N_CH)
        def _():
            var = (n == 0).astype(jnp.int32)
            kn_band = jnp.concatenate([_norm_kv(skp_ref[...], kw), _norm_kv(skc_ref[...], kw)], axis=1)
            kb_rm = _bf(kn_band.T)
            kn_band_s = _bf(kn_band * (SWA_D ** -0.5))
            vband_f = jnp.concatenate([svp_ref[...], svc_ref[...]], axis=1)
            vb_rm = _bf(vband_f.T)
            q_raw, q_rstd, qs_b, do_b = [], [], [], []
            for hq in range(SWA_H):
                rows = slice(hq * SWA_D, (hq + 1) * SWA_D)
                q_t = sq_ref[rows, :]
                qn, rq = _rms_t(q_t, qw)
                g = _gate_rows((sg_lo, sg_hi), hq)
                sig = _sigmoid(g)
                dm = dm_ref[rows, :]
                dg_ref[rows, :] = dm * so_ref[rows, :] * (sig * (1.0 + g * (1.0 - sig)))
                q_raw.append(q_t)
                q_rstd.append(rq)
                qs_b.append(_bf(qn * (SWA_D ** -0.5)))
                do_b.append(_bf(dm * (g * sig)))
            q_all = jnp.concatenate([_kv_slot(qs_b[hq], hq // SWA_G) for hq in range(SWA_H)], axis=1)
            do_all = jnp.concatenate([_kv_slot(do_b[hq], hq // SWA_G) for hq in range(SWA_H)], axis=1)
            probs, p_sink = _softmax_t(_dot(kb_rm, q_all) + bias_ref[var], sink_row[...])
            dprobs = _dot(vb_rm, do_all)
            t = jnp.sum(probs * dprobs, axis=0, keepdims=True)
            dlog = probs * (dprobs - t)
            dsink_ref[...] += -(p_sink * t)
            dbias_ref[...] += dlog
            dlog_b, probs_b = _bf(dlog), _bf(probs)
            dkn, dvv = [], []
            for kv in range(SWA_KV):
                heads = range(kv * SWA_G, (kv + 1) * SWA_G)
                lanes = slice(kv * SWA_G * CH, (kv + 1) * SWA_G * CH)
                dvv.append(_dot_nt(jnp.concatenate([do_b[hq] for hq in heads], axis=1), probs_b[:, lanes]))
                dkn.append(_dot_nt(jnp.concatenate([qs_b[hq] for hq in heads], axis=1), dlog_b[:, lanes]))
                dqn = _dot(kn_band_s[kv * SWA_D:(kv + 1) * SWA_D], dlog_b[:, lanes])
                for j, hq in enumerate(heads):
                    dq_t, dqw_terms = _rms_t_bwd(dqn[:, j * CH:(j + 1) * CH], q_raw[hq], q_rstd[hq], qw)
                    dq_ref[hq * SWA_D:(hq + 1) * SWA_D, :] = dq_t
                    dqw_ref[...] += dqw_terms
            band_dk[...] = jnp.concatenate(dkn, axis=0)
            band_dv[...] = jnp.concatenate(dvv, axis=0)

        @pl.when(n == N_CH)
        def _():
            band_dk[...] = jnp.zeros_like(band_dk)
            band_dv[...] = jnp.zeros_like(band_dv)

        @pl.when(n >= 1)
        def _():
            dkn_prev = carry_dk[...] + band_dk[:, 0:CH]
            k_t = skp_ref[...]
            for kv in range(SWA_KV):
                rows = slice(kv * SWA_D, (kv + 1) * SWA_D)
                _, rk = _rms_t(k_t[rows], kw)
                dk_t, dkw_terms = _rms_t_bwd(dkn_prev[rows], k_t[rows], rk, kw)
                dk_ref[rows, :] = dk_t
                dkw_ref[...] += dkw_terms
            dv_ref[...] = carry_dv[...] + band_dv[:, 0:CH]

        carry_dk[...] = band_dk[:, CH:2 * CH]
        carry_dv[...] = band_dv[:, CH:2 * CH]

    cur_block = lambda n: jnp.minimum(n, N_CH - 1)
    prev_block = lambda n: jnp.maximum(n - 1, 0)
    col = lambda w: pl.BlockSpec((w, CH), lambda n: (0, cur_block(n)))
    prev = lambda w: pl.BlockSpec((w, CH), lambda n: (0, prev_block(n)))
    const = lambda shape: pl.BlockSpec(shape, lambda n: (0,) * len(shape))
    smem = pl.BlockSpec(memory_space=pltpu.SMEM)
    return pl.pallas_call(
        body, name="swa_bwd", grid=(N_CH + 1,),
        in_specs=[_proj_piece(4, cur_block), _proj_piece(5, prev_block), _proj_piece(5, cur_block),
                  _proj_piece(6, prev_block), _proj_piece(6, cur_block)] + _proj_gate_halves(cur_block) + [
                  col(512), col(512), const((SWA_D, 1)), const((SWA_D, 1)), smem, smem, const((2 * CH, CH))],
        out_specs=(col(512), col(512), prev(128), prev(128), const((2 * CH, HQ_LANES)), const((1, HQ_LANES)),
                   const((SWA_D, CH)), const((SWA_D, CH))),
        out_shape=(jax.ShapeDtypeStruct((512, T), F32), jax.ShapeDtypeStruct((512, T), F32),
                   jax.ShapeDtypeStruct((128, T), F32), jax.ShapeDtypeStruct((128, T), F32),
                   jax.ShapeDtypeStruct((2 * CH, HQ_LANES), F32), jax.ShapeDtypeStruct((1, HQ_LANES), F32),
                   jax.ShapeDtypeStruct((SWA_D, CH), F32), jax.ShapeDtypeStruct((SWA_D, CH), F32)),
        scratch_shapes=[pltpu.VMEM((2, 2 * CH, HQ_LANES), F32), pltpu.VMEM((1, HQ_LANES), F32),
                        pltpu.VMEM((128, 2 * CH), F32), pltpu.VMEM((128, 2 * CH), F32),
                        pltpu.VMEM((128, CH), F32), pltpu.VMEM((128, CH), F32)],
        compiler_params=_cparams(1),
    )(proj, proj, proj, proj, proj, proj, proj, so, dmix, q_norm_w, k_norm_w, sinks, rel_bias_flat, bucket_t)


def _small_grads(dbias, bucket_t, dsink_acc, dqw_acc, dkw_acc, drw_acc):
    def body(dbias_ref, bucket_ref, dsink_ref, dqw_ref, dkw_ref, drw_ref, relb_o, sink_o, qw_o, kw_o, rw_o):
        bt = bucket_ref[...]
        row = lax.broadcasted_iota(jnp.int32, (N_BUCKETS, SWA_H), 0)
        col = lax.broadcasted_iota(jnp.int32, (N_BUCKETS, SWA_H), 1)
        lane = lax.broadcasted_iota(jnp.int32, (1, CH), 1)
        acc = jnp.zeros((N_BUCKETS, SWA_H), F32)
        sink = jnp.zeros((1, CH), F32)
        for hq in range(SWA_H):
            d = dbias_ref[:, _head_lanes(hq)]
            for bk in range(N_BUCKETS):
                s = jnp.sum(jnp.sum(jnp.where(bt == bk, d, 0.0), axis=0, keepdims=True), axis=1, keepdims=True)
                acc = acc + jnp.where((row == bk) & (col == hq), s, 0.0)
            sink = sink + jnp.where(lane == hq, jnp.sum(dsink_ref[:, _head_lanes(hq)], axis=1, keepdims=True), 0.0)
        relb_o[...] = acc
        sink_o[...] = sink
        for src, dst in ((dqw_ref, qw_o), (dkw_ref, kw_o)):
            padded = jnp.concatenate([src[...], jnp.zeros((CH - SWA_D, CH), F32)], axis=0)
            dst[...] = jnp.sum(padded.T, axis=0, keepdims=True)
        rw_o[...] = jnp.sum(drw_ref[...].T, axis=0, keepdims=True)

    vm = pl.BlockSpec(memory_space=pltpu.VMEM)
    row128 = jax.ShapeDtypeStruct((1, CH), F32)
    return pl.pallas_call(body, name="small_grads",
                          out_shape=(jax.ShapeDtypeStruct((N_BUCKETS, SWA_H), F32), row128, row128, row128,
                                     jax.ShapeDtypeStruct((1, RET_V), F32)),
                          in_specs=[vm] * 6, out_specs=(vm,) * 5, compiler_params=_cparams(),
                          )(dbias, bucket_t, dsink_acc, dqw_acc, dkw_acc, drw_acc)


def _out_proj(mix_r, mix_s, wo, x, target):
    def body(mr_ref, ms_ref, w_ref, x_ref, t_ref, loss_ref, dy_ref, dmr_ref, dms_ref, gw_ref):
        i = pl.program_id(0)

        @pl.when(i == 0)
        def _():
            loss_ref[...] = jnp.zeros_like(loss_ref)
            gw_ref[...] = jnp.zeros_like(gw_ref)

        mixed = jnp.concatenate([mr_ref[...], ms_ref[...]], axis=0)
        w = w_ref[...]
        err = x_ref[...] + _dot(_bf(mixed.T), w) - t_ref[...]
        loss_ref[...] += jnp.sum(jnp.sum(err * err, axis=1, keepdims=True), axis=0, keepdims=True)
        dy = err * (1.0 / D)
        dy_ref[...] = dy
        dy_b = _bf(dy)
        dmix = _dot_nt(w, dy_b)
        dmr_ref[...] = dmix[0:512]
        dms_ref[...] = dmix[512:D]
        gw_ref[...] += _dot(_bf(mixed), dy_b)

    row = lambda w: pl.BlockSpec((TM, w), lambda i: (i, 0))
    col = lambda w: pl.BlockSpec((w, TM), lambda i: (0, i))
    const = lambda shape: pl.BlockSpec(shape, lambda i: (0,) * len(shape))
    return pl.pallas_call(
        body, name="out_proj", grid=(T // TM,),
        in_specs=[col(512), col(512), const((D, D)), row(D), row(D)],
        out_specs=(const((1, 1)), row(D), col(512), col(512), const((D, D))),
        out_shape=(jax.ShapeDtypeStruct((1, 1), F32), jax.ShapeDtypeStruct((T, D), F32),
                   jax.ShapeDtypeStruct((512, T), F32), jax.ShapeDtypeStruct((512, T), F32),
                   jax.ShapeDtypeStruct((D, D), F32)),
        compiler_params=_cparams(1),
    )(mix_r, mix_s, wo, x, target)


def _in_proj_bwd(dpieces, x, rstd, norm_w, dy, wt):
    def body(*refs):
        piece_refs = refs[:8]
        x_ref, r_ref, nw_ref, dy_ref, w_ref, gx_ref, gw_ref, gnw_ref = refs[8:]
        i = pl.program_id(0)

        @pl.when(i == 0)
        def _():
            gw_ref[...] = jnp.zeros_like(gw_ref)
            gnw_ref[...] = jnp.zeros_like(gnw_ref)

        dp = jnp.concatenate([ref[...] for ref in piece_refs], axis=0)
        xv, r, nw = x_ref[...], r_ref[...], nw_ref[...]
        xr = xv * r
        dh = _dot(_bf(dp.T), w_ref[...])
        gw_ref[...] += _dot(_bf(dp), _bf(xr * nw))
        u = dh * nw
        gx_ref[...] = dy_ref[...] + r * u - xv * (r * r * r) * jnp.mean(u * xv, axis=-1, keepdims=True)
        gnw_ref[...] += jnp.sum(dh * xr, axis=0, keepdims=True)

    row = lambda w: pl.BlockSpec((TM, w), lambda i: (i, 0))
    col = lambda w: pl.BlockSpec((w, TM), lambda i: (0, i))
    const = lambda shape, **kw: pl.BlockSpec(shape, lambda i: (0,) * len(shape), **kw)
    once = dict(pipeline_mode=pl.Buffered(1))
    return pl.pallas_call(
        body, name="in_proj_bwd", grid=(T // TM,),
        in_specs=([col(w) for w in PIECES] + [row(D), row(1), const((1, D)), row(D), const((D_IN, D), **once)]),
        out_specs=(row(D), const((D_IN, D), **once), const((1, D))),
        out_shape=(jax.ShapeDtypeStruct((T, D), F32), jax.ShapeDtypeStruct((D_IN, D), F32),
                   jax.ShapeDtypeStruct((1, D), F32)),
        compiler_params=_cparams(1),
    )(*dpieces, x, rstd, norm_w, dy, wt)


SMALL_SHAPES = ((1, D), (1, 512), (1, SWA_D), (1, SWA_D), (1, SWA_H), (N_BUCKETS, SWA_H))
LOSS_ROW = 5


def _reduce_and_update(gwt, gwo, sse, small_g, w_in, m_in, v_in, w_out, m_out, v_out, small_w, small_m, small_v):
    n_small = len(SMALL_SHAPES)

    def body(*refs):
        gwt_ref, gwo_ref, sse_ref = refs[:3]
        sg_refs = refs[3:3 + n_small]
        w_in_ref, m_in_ref, v_in_ref, w_out_ref, m_out_ref, v_out_ref = refs[3 + n_small:9 + n_small]
        pos = 9 + n_small
        sw_refs, sm_refs, sv_refs = (refs[pos + i * n_small:pos + (i + 1) * n_small] for i in range(3))
        pos += 3 * n_small
        g_in_o, d_in_o, nm_in_o, nv_in_o, g_out_o, d_out_o, nm_out_o, nv_out_o = refs[pos:pos + 8]
        pos += 8
        sg_o, sd_o, snm_o, snv_o = (refs[pos + i * n_small:pos + (i + 1) * n_small] for i in range(4))
        pos += 4 * n_small
        loss_o = refs[pos]
        (own_t, own_o, rcv_t, rcv_o, snd_t, snd_o, got_t, got_o, tab_a, tab_b, sum_a, sum_b, pad_ref, tr_ref,
         loc_sems, a_send, a_recv, b_send, b_recv, s_send, s_recv) = refs[pos + 1:]
        x, y, c = _mesh_pos()
        me = 4 * x + 2 * y + c
        sibling = (x, y, 1 - c)
        all_chips = [(0, 0), (0, 1), (1, 0), (1, 1)]
        others = [(1 - x, y), (x, 1 - y), (1 - x, 1 - y)]

        def blk(px, py, pc):
            return 4 * px + 2 * py + pc

        mine_a, mine_b = tab_a.at[me], tab_b.at[me]
        mine_a[...] = jnp.zeros_like(mine_a)
        mine_b[...] = jnp.zeros_like(mine_b)
        for r in range(n_small - 1):
            mine_a[r:r + 1, 0:sg_refs[r].shape[1]] = sg_refs[r][...]
        mine_a[LOSS_ROW:LOSS_ROW + 1, 0:1] = sse_ref[...]
        mine_b[:, 0:SWA_H] = sg_refs[n_small - 1][...]
        small_copies = []
        for k in range(1, N_DEV):
            px, py, pc = x ^ (k >> 2), y ^ ((k >> 1) & 1), c ^ (k & 1)
            for a, tab in enumerate((tab_a, tab_b)):
                cp = pltpu.make_async_remote_copy(src_ref=tab.at[me], dst_ref=tab.at[me], send_sem=s_send.at[a, k - 1],
                                                  recv_sem=s_recv.at[a, k - 1], device_id=(px, py, pc),
                                                  device_id_type=MESH)
                cp.start()
                small_copies.append(cp)

        stage_a, local = [], []
        for j, (px, py) in enumerate(all_chips):
            for a, (src, dst) in enumerate(((gwt_ref, rcv_t), (gwo_ref, rcv_o))):
                cp = pltpu.make_async_remote_copy(src_ref=src.at[blk(px, py, 1 - c)], dst_ref=dst.at[j],
                                                  send_sem=a_send.at[a, j], recv_sem=a_recv.at[a, j],
                                                  device_id=sibling, device_id_type=MESH)
                cp.start()
                stage_a.append(cp)
            for a, (src, dst) in enumerate(((gwt_ref, own_t), (gwo_ref, own_o))):
                cp = pltpu.make_async_copy(src.at[blk(px, py, c)], dst.at[j], loc_sems.at[a, j])
                cp.start()
                local.append(cp)
        for cp in local:
            cp.wait()
        for cp in stage_a:
            cp.wait_recv()

        stage_b = []
        for k, (px, py) in enumerate(others):
            j = 2 * px + py
            snd_t[k] = _bf(own_t[j] + rcv_t[j])
            snd_o[k] = _bf(own_o[j] + rcv_o[j])
            for a, (src, dst) in enumerate(((snd_t, got_t), (snd_o, got_o))):
                cp = pltpu.make_async_remote_copy(src_ref=src.at[k], dst_ref=dst.at[k], send_sem=b_send.at[a, k],
                                                  recv_sem=b_recv.at[a, k], device_id=(px, py, c), device_id_type=MESH)
                cp.start()
                stage_b.append(cp)
        jm = 2 * x + y
        tot_t = own_t[jm] + rcv_t[jm]
        tot_o = own_o[jm] + rcv_o[jm]
        for k in range(3):
            stage_b[2 * k].wait_recv()
            stage_b[2 * k + 1].wait_recv()
            tot_t = tot_t + got_t[k].astype(F32)
            tot_o = tot_o + got_o[k].astype(F32)

        d, nm, nv = _adamw(w_out_ref[...], tot_o, m_out_ref[...], v_out_ref[...])
        g_out_o[...], d_out_o[...], nm_out_o[...], nv_out_o[...] = tot_o, d, nm, nv

        pad_ref[0:WIN_BLK, :] = tot_t
        pad_ref[WIN_BLK:WIN_PAD, :] = jnp.zeros((WIN_PAD - WIN_BLK, D), F32)
        tr_ref[...] = pad_ref[...].T
        g = tr_ref[:, 0:WIN_BLK]
        d, nm, nv = _adamw(w_in_ref[...], g, m_in_ref[...], v_in_ref[...])
        g_in_o[...], d_in_o[...], nm_in_o[...], nv_in_o[...] = g, d, nm, nv

        for cp in small_copies:
            cp.wait_recv()
        tot_a, tot_b = tab_a[0], tab_b[0]
        for b in range(1, N_DEV):
            tot_a = tot_a + tab_a[b]
            tot_b = tot_b + tab_b[b]
        sum_a[...] = tot_a
        sum_b[...] = tot_b
        loss_o[...] = sum_a[LOSS_ROW:LOSS_ROW + 1, 0:1] * (0.5 / D)
        for r, (rows, lanes) in enumerate(SMALL_SHAPES):
            g = sum_a[r:r + 1, 0:lanes] if rows == 1 else sum_b[:, 0:lanes]
            d, nm, nv = _adamw(sw_refs[r][...], g, sm_refs[r][...], sv_refs[r][...])
            sg_o[r][...], sd_o[r][...], snm_o[r][...], snv_o[r][...] = g, d, nm, nv

        for cp in stage_a + stage_b + small_copies:
            cp.wait_send()

    vm = pl.BlockSpec(memory_space=pltpu.VMEM)
    hbm = pl.BlockSpec(memory_space=pl.ANY)
    win = jax.ShapeDtypeStruct((D, WIN_BLK), F32)
    wout = jax.ShapeDtypeStruct((WOUT_BLK, D), F32)
    smalls = tuple(jax.ShapeDtypeStruct(s, F32) for s in SMALL_SHAPES)
    outs = pl.pallas_call(
        body, name="reduce_and_update",
        in_specs=[hbm, hbm] + [vm] * (7 + 4 * n_small), out_specs=(vm,) * (9 + 4 * n_small),
        out_shape=(win,) * 4 + (wout,) * 4 + smalls * 4 + (jax.ShapeDtypeStruct((1, 1), F32),),
        scratch_shapes=[
            pltpu.VMEM((4, WIN_BLK, D), F32), pltpu.VMEM((4, WOUT_BLK, D), F32),
            pltpu.VMEM((4, WIN_BLK, D), F32), pltpu.VMEM((4, WOUT_BLK, D), F32),
            pltpu.VMEM((3, WIN_BLK, D), BF16), pltpu.VMEM((3, WOUT_BLK, D), BF16),
            pltpu.VMEM((3, WIN_BLK, D), BF16), pltpu.VMEM((3, WOUT_BLK, D), BF16),
            pltpu.VMEM((N_DEV, 8, D), F32), pltpu.VMEM((N_DEV, N_BUCKETS, CH), F32),
            pltpu.VMEM((8, D), F32), pltpu.VMEM((N_BUCKETS, CH), F32),
            pltpu.VMEM((WIN_PAD, D), F32), pltpu.VMEM((D, WIN_PAD), F32),
            pltpu.SemaphoreType.DMA((2, 4)),
            pltpu.SemaphoreType.DMA((2, 4)), pltpu.SemaphoreType.DMA((2, 4)),
            pltpu.SemaphoreType.DMA((2, 3)), pltpu.SemaphoreType.DMA((2, 3)),
            pltpu.SemaphoreType.DMA((2, 7)), pltpu.SemaphoreType.DMA((2, 7)),
        ],
        compiler_params=_cparams(),
    )(gwt, gwo, sse, *small_g, w_in, m_in, v_in, w_out, m_out, v_out, *small_w, *small_m, *small_v)
    big, rest = outs[:8], outs[8:]
    return big, [rest[i * n_small:(i + 1) * n_small] for i in range(4)], rest[4 * n_small]


def _small_rows(norm_w, ret_norm_w, q_norm_w, k_norm_w, sinks, rel_bias):
    return (norm_w.reshape(1, D), ret_norm_w.reshape(1, 512), q_norm_w.reshape(1, SWA_D), k_norm_w.reshape(1, SWA_D),
            sinks.reshape(1, SWA_H), rel_bias)


def _small_leaves(rows):
    return (rows[0].reshape(D), rows[1].reshape(512), rows[2].reshape(SWA_D), rows[3].reshape(SWA_D),
            rows[4].reshape(SWA_H), rows[5])


def kernel(x, norm_w, w_in, ret_norm_w, q_norm_w, k_norm_w, sinks, rel_bias, w_out, loss_target, m_norm_w, m_w_in, m_ret_norm_w, m_q_norm_w, m_k_norm_w, m_sinks, m_rel_bias, m_w_out, v_norm_w, v_w_in, v_ret_norm_w, v_q_norm_w, v_k_norm_w, v_sinks, v_rel_bias, v_w_out):
    x2 = x.reshape(T, D)
    target = loss_target.reshape(T, D)
    nw = norm_w.reshape(1, D)
    rnw_col = ret_norm_w.reshape(RET_V, 1)
    qnw_col = q_norm_w.reshape(SWA_D, 1)
    knw_col = k_norm_w.reshape(SWA_D, 1)
    relb = rel_bias.reshape(N_BUCKETS * SWA_H)
    ret_tables = _rotary_tables_t() + _retention_tables_t()
    bucket_t = _bucket_table_t()

    proj, wt, rstd, _ = _in_proj_gather(x2, nw, w_in)
    ro, mix_r, states, wo = _retention_fwd(proj, rnw_col, ret_tables, w_out)
    so, mix_s = _swa_fwd(proj, qnw_col, knw_col, sinks, relb, bucket_t)
    sse, dy, dmix_r, dmix_s, gwo = _out_proj(mix_r, mix_s, wo, x2, target)

    drq, drk, drv, drg, drw_acc = _retention_bwd(proj, ro, states, dmix_r, rnw_col, ret_tables)
    dsq, dsg, dsk, dsv, dbias, dsink_acc, dqw_acc, dkw_acc = _swa_bwd(proj, so, dmix_s, qnw_col, knw_col,
                                                                      sinks, relb, bucket_t)
    d_relb, d_sinks, d_qnw, d_knw, d_rnw = _small_grads(dbias, bucket_t, dsink_acc, dqw_acc, dkw_acc, drw_acc)
    grad_x, gwt, d_nw = _in_proj_bwd((drq, drk, drv, drg, dsq, dsk, dsv, dsg), x2, rstd, nw, dy, wt)

    small_g = (d_nw, d_rnw, d_qnw, d_knw, d_sinks, d_relb)
    small_w = _small_rows(norm_w, ret_norm_w, q_norm_w, k_norm_w, sinks, rel_bias)
    small_m = _small_rows(m_norm_w, m_ret_norm_w, m_q_norm_w, m_k_norm_w, m_sinks, m_rel_bias)
    small_v = _small_rows(v_norm_w, v_ret_norm_w, v_q_norm_w, v_k_norm_w, v_sinks, v_rel_bias)

    big, small, loss = _reduce_and_update(
        gwt.reshape(N_DEV, WIN_BLK, D), gwo.reshape(N_DEV, WOUT_BLK, D), sse, small_g,
        w_in, m_w_in, v_w_in, w_out, m_w_out, v_w_out, small_w, small_m, small_v)

    def leaves(i):
        a = _small_leaves(small[i])
        return (a[0], big[i], a[1], a[2], a[3], a[4], a[5], big[4 + i])

    return (loss.reshape(()), grad_x.reshape(1, T, D), *leaves(0), *leaves(1), *leaves(2), *leaves(3))
```

```python
import numpy as np
import jax
import jax.numpy as jnp
from jax import lax
from jax.experimental import pallas as pl
from jax.experimental.pallas import tpu as pltpu

F32 = jnp.float32
BF16 = jnp.bfloat16
MESH = pl.DeviceIdType.MESH

T = 2048
D = 1024
D_IN = 2816
N_DEV = 8
WIN_BLK = D_IN // N_DEV
WIN_PAD = 384
WOUT_BLK = D // N_DEV
CH = 128
N_CH = T // CH
RET_H, RET_DK, RET_DV = 4, 64, 128
RET_QK = RET_H * RET_DK
RET_V = RET_H * RET_DV
SWA_H, SWA_KV, SWA_D, SWA_G = 8, 2, 64, 4
N_BUCKETS = 32
NORM_EPS = 1e-6
GN_EPS = 1e-5
NEG_INF = -1e30
PIECES = (256, 256, 512, 512, 512, 128, 128, 512)
OFFS = tuple(int(v) for v in np.cumsum((0,) + PIECES))
RET_W = OFFS[4]
SWA_W = D_IN - RET_W
SWA_OFFS = tuple(o - RET_W for o in OFFS[4:])
TM = 256

ADAM_LR, ADAM_B1, ADAM_B2, ADAM_EPS, ADAM_WD, ADAM_STEP = 0.001, 0.9, 0.999, 1e-08, 0.01, 10

VMEM_LIMIT = 56 * 1024 * 1024


def _cparams(n_grid=0, vmem=VMEM_LIMIT):
    sem = ("arbitrary",) * n_grid if n_grid else None
    return pltpu.CompilerParams(dimension_semantics=sem, vmem_limit_bytes=vmem)


def _dot(a, b):
    return jnp.dot(a, b, preferred_element_type=F32)


def _dot_nt(a, b):
    return lax.dot_general(a, b, (((1,), (1,)), ((), ())), preferred_element_type=F32)


def _bf(a):
    return a.astype(BF16)


def _sigmoid(x):
    return 1.0 / (1.0 + jnp.exp(-x))


def _adamw(w, g, m, v):
    m = ADAM_B1 * m + (1.0 - ADAM_B1) * g
    v = ADAM_B2 * v + (1.0 - ADAM_B2) * (g * g)
    m_hat = m / (1.0 - ADAM_B1 ** ADAM_STEP)
    v_hat = v / (1.0 - ADAM_B2 ** ADAM_STEP)
    delta = -ADAM_LR * (m_hat / (jnp.sqrt(v_hat) + ADAM_EPS) + ADAM_WD * w)
    return delta, m, v


def _rotary_tables_t():
    half = RET_DK // 2
    inv_freq = np.float32(10000.0) ** (-np.arange(half, dtype=np.float32) / np.float32(half))
    ang = inv_freq[:, None] * np.arange(T, dtype=np.float32)[None, :]
    cos, sin = np.cos(ang).astype(np.float32), np.sin(ang).astype(np.float32)
    cos64 = np.concatenate([cos, cos], axis=0)
    sin64 = np.concatenate([-sin, sin], axis=0)
    return np.tile(cos64, (RET_H, 1)), np.tile(sin64, (RET_H, 1))


def _retention_tables_t():
    gamma = (1.0 - np.exp2(-5.0 - np.arange(RET_H, dtype=np.float32))).astype(np.float32)
    log_g = np.log(gamma).astype(np.float32)
    i = np.arange(CH, dtype=np.float32)
    diff = i[None, :] - i[:, None]
    decay = np.where(diff >= 0, np.exp(log_g[:, None, None] * np.maximum(diff, 0.0)), 0.0).astype(np.float32)
    decay_all = np.concatenate(list(decay), axis=1)
    zeta = np.exp(log_g[:, None] * (CH - 1.0 - i)).astype(np.float32)
    zeta_tab = np.repeat(zeta.T, RET_DK, axis=1)
    xi = np.exp(log_g[:, None] * (i + 1.0)).astype(np.float32)
    xi_tab = np.repeat(xi, RET_DK, axis=0)
    chunk_decay = np.exp(log_g * np.float32(CH)).astype(np.float32)
    row_head = np.arange(RET_V)[:, None] // RET_DV
    col_head = np.arange(RET_QK)[None, :] // RET_DK
    state_mask = (row_head == col_head).astype(np.float32)
    state_decay = (state_mask * chunk_decay[row_head]).astype(np.float32)
    q_mask = (np.arange(RET_QK)[:, None] // RET_DK == np.arange(RET_H * CH)[None, :] // CH).astype(np.float32)
    return (np.ascontiguousarray(decay_all), np.ascontiguousarray(zeta_tab), np.ascontiguousarray(xi_tab),
            state_mask, state_decay, q_mask)


def _bucket_table_t():
    qi = np.arange(CH)[:, None]
    kj = np.arange(2 * CH)[None, :]
    dist = qi + CH - kj
    n = np.maximum(dist, 0)
    max_exact = N_BUCKETS // 2
    nf = np.maximum(n, 1).astype(np.float32)
    large = max_exact + (np.log(nf / np.float32(max_exact)) / np.float32(np.log(CH / max_exact))
                         * np.float32(N_BUCKETS - max_exact)).astype(np.int32)
    large = np.minimum(large, N_BUCKETS - 1)
    bucket = np.where(n < max_exact, n, large)
    return np.ascontiguousarray(np.where((dist >= 0) & (dist < CH), bucket, -1).astype(np.int32).T)


def _mesh_pos():
    return lax.axis_index("x"), lax.axis_index("y"), lax.axis_index("c")


def _blk(px, py, pc):
    return 4 * px + 2 * py + pc


def _in_proj_gather(x, norm_w, w_in_blk):
    tchunk = 512

    def body(x_ref, nw_ref, win_ref, proj_ref, wt_ref, rstd_ref, h_ref, pad_ref, stage, send_sems, recv_sems, out_sems):
        x, y, c = _mesh_pos()
        me = _blk(x, y, c)
        sibling = (x, y, 1 - c)
        chips = [(1 - x, y), (x, 1 - y), (1 - x, 1 - y)]

        def copy(k, b, to):
            return pltpu.make_async_remote_copy(src_ref=wt_ref.at[b], dst_ref=wt_ref.at[b], send_sem=send_sems.at[k],
                                                recv_sem=recv_sems.at[k], device_id=to, device_id_type=MESH)

        pad_ref[...] = jnp.zeros_like(pad_ref)
        pad_ref[:, 0:WIN_BLK] = win_ref[...]
        wt_ref[me] = _bf(pad_ref[...].T[0:WIN_BLK, :])
        first = [copy(0, me, sibling)] + [copy(1 + j, me, (*chip, c)) for j, chip in enumerate(chips)]
        for cp in first:
            cp.start()

        nw = nw_ref[...]
        for i in range(T // TM):
            rows = slice(i * TM, (i + 1) * TM)
            xv = x_ref[rows, :]
            r = lax.rsqrt(jnp.mean(xv * xv, axis=-1, keepdims=True) + NORM_EPS)
            h_ref[rows, :] = _bf(xv * r * nw)
            rstd_ref[rows, :] = r

        writes = []

        def project(b):
            k = len(writes)
            if k >= 2:
                writes[k - 2].wait()
            w = wt_ref[b]
            for t in range(T // tchunk):
                cols = slice(t * tchunk, (t + 1) * tchunk)
                stage[k % 2, :, cols] = _dot_nt(w, h_ref[cols, :])
            cp = pltpu.make_async_copy(stage.at[k % 2], proj_ref.at[b], out_sems.at[k % 2])
            cp.start()
            writes.append(cp)

        project(me)
        copy(0, _blk(x, y, 1 - c), (x, y, c)).wait_recv()
        project(_blk(x, y, 1 - c))
        passed = []
        for j, chip in enumerate(chips):
            copy(1 + j, _blk(*chip, c), (x, y, c)).wait_recv()
            fwd = copy(4 + j, _blk(*chip, c), sibling)
            fwd.start()
            passed.append(fwd)
            project(_blk(*chip, c))
            copy(4 + j, _blk(*chip, 1 - c), (x, y, c)).wait_recv()
            project(_blk(*chip, 1 - c))
        writes[-2].wait()
        writes[-1].wait()
        for cp in first + passed:
            cp.wait_send()

    vm = pl.BlockSpec(memory_space=pltpu.VMEM)
    proj, wt, rstd, h = pl.pallas_call(
        body, name="in_proj_gather",
        out_shape=(jax.ShapeDtypeStruct((N_DEV, WIN_BLK, T), F32), jax.ShapeDtypeStruct((N_DEV, WIN_BLK, D), BF16),
                   jax.ShapeDtypeStruct((T, 1), F32), jax.ShapeDtypeStruct((T, D), BF16)),
        in_specs=[vm, vm, vm], out_specs=(pl.BlockSpec(memory_space=pl.ANY), vm, vm, vm),
        scratch_shapes=[pltpu.VMEM((D, WIN_PAD), F32), pltpu.VMEM((2, WIN_BLK, T), F32),
                        pltpu.SemaphoreType.DMA((7,)), pltpu.SemaphoreType.DMA((7,)), pltpu.SemaphoreType.DMA((2,))],
        compiler_params=_cparams(vmem=48 * 1024 * 1024),
    )(x, norm_w, w_in_blk)
    return proj.reshape(D_IN, T), wt.reshape(D_IN, D), rstd, h


def _gather_w_out(w_out_ref, own_ref, wo_ref, local_sem, send_sems, recv_sems, phase):
    x, y, c = _mesh_pos()
    me = _blk(x, y, c)
    sibling = (x, y, 1 - c)
    chips = [(1 - x, y), (x, 1 - y), (1 - x, 1 - y)]

    def copy(k, b, to, src=None):
        return pltpu.make_async_remote_copy(src_ref=wo_ref.at[b] if src is None else src, dst_ref=wo_ref.at[b],
                                            send_sem=send_sems.at[k], recv_sem=recv_sems.at[k], device_id=to,
                                            device_id_type=MESH)

    local = pltpu.make_async_copy(own_ref, wo_ref.at[me], local_sem)
    first = [copy(0, me, sibling, own_ref)] + [copy(1 + j, me, (*chip, c), own_ref) for j, chip in enumerate(chips)]
    passed = [copy(4 + j, _blk(*chip, c), sibling) for j, chip in enumerate(chips)]
    if phase == 0:
        own_ref[...] = _bf(w_out_ref[...])
        local.start()
        for cp in first:
            cp.start()
    elif phase == 1:
        for j, chip in enumerate(chips):
            copy(1 + j, _blk(*chip, c), (x, y, c)).wait_recv()
            passed[j].start()
    else:
        local.wait()
        copy(0, _blk(x, y, 1 - c), (x, y, c)).wait_recv()
        for j, chip in enumerate(chips):
            copy(4 + j, _blk(*chip, 1 - c), (x, y, c)).wait_recv()
        for cp in first + passed:
            cp.wait_send()


def _proj_piece(i, block_of):
    rows = PIECES[i]
    assert OFFS[i] % rows == 0
    return pl.BlockSpec((rows, CH), lambda n: (OFFS[i] // rows, block_of(n)))


def _proj_gate_halves(block_of):
    return [pl.BlockSpec((256, CH), lambda n, j=j: (OFFS[7] // 256 + j, block_of(n))) for j in range(2)]


def _swap_halves_t(t):
    half = RET_DK // 2
    parts = []
    for h in range(RET_H):
        parts += [t[h * RET_DK + half:(h + 1) * RET_DK], t[h * RET_DK:h * RET_DK + half]]
    return jnp.concatenate(parts, axis=0)


def _rotate(t, cos, sin):
    return t * cos + _swap_halves_t(t) * sin


def _group_norm_t(o):
    mu = jnp.mean(o, axis=0, keepdims=True)
    var = jnp.mean((o - mu) * (o - mu), axis=0, keepdims=True)
    rstd = lax.rsqrt(var + GN_EPS)
    return (o - mu) * rstd, rstd


def _retention_scores_t(q_t, k_rm_b, q_mask, decay_all):
    q_heads = _bf(jnp.concatenate([q_t] * RET_H, axis=1) * q_mask)
    return _dot(k_rm_b, q_heads) * decay_all


def _retention_fwd(proj, ret_norm_w, tables, w_out_blk):
    cos_t, sin_t, decay_all, zeta_tab, xi_tab, state_mask, state_decay, q_mask = tables

    def body(rq_ref, rk_ref, rv_ref, rg_ref, w_ref, cos_ref, sin_ref, decay_ref, zeta_ref, xi_ref, smask_ref,
             sdecay_ref, qmask_ref, wout_ref, ro_ref, mix_ref, st_ref, wo_ref, state, wo_own, wo_local, wo_send, wo_recv):
        n = pl.program_id(0)

        @pl.when(n == 0)
        def _():
            state[...] = jnp.zeros_like(state)
            _gather_w_out(wout_ref, wo_own, wo_ref, wo_local, wo_send, wo_recv, phase=0)

        @pl.when(n == N_CH // 2)
        def _():
            _gather_w_out(wout_ref, wo_own, wo_ref, wo_local, wo_send, wo_recv, phase=1)

        @pl.when(n == N_CH - 1)
        def _():
            _gather_w_out(wout_ref, wo_own, wo_ref, wo_local, wo_send, wo_recv, phase=2)

        cos, sin = cos_ref[...], sin_ref[...]
        q_t = _rotate(rq_ref[...], cos, sin)
        k_t = _rotate(rk_ref[...], cos, sin) * (RET_DK ** -0.5)
        k_rm = k_t.T
        v_b = _bf(rv_ref[...])
        m_b = _bf(state[...])
        st_ref[0] = m_b
        scores_b = _bf(_retention_scores_t(q_t, _bf(k_rm), qmask_ref[...], decay_ref[...]))
        cross = _dot(m_b, _bf(q_t * xi_ref[...]))
        state[...] = state[...] * sdecay_ref[...] + _dot(v_b, _bf(k_rm * zeta_ref[...])) * smask_ref[...]
        heads = range(RET_H)
        rows = [slice(h * RET_DV, (h + 1) * RET_DV) for h in heads]
        o = [_dot(v_b[rows[h]], scores_b[:, h * CH:(h + 1) * CH]) + cross[rows[h]] for h in heads]
        for h in heads:
            rn, _ = _group_norm_t(o[h])
            g = rg_ref[rows[h], :]
            ro_ref[rows[h], :] = o[h]
            mix_ref[rows[h], :] = rn * w_ref[rows[h], :] * (g * _sigmoid(g))

    col = lambda w: pl.BlockSpec((w, CH), lambda n: (0, n))
    const = lambda shape: pl.BlockSpec(shape, lambda n: (0,) * len(shape))
    cur = lambda n: n
    ro, mix, states, wo = pl.pallas_call(
        body, name="retention_fwd", grid=(N_CH,),
        in_specs=[_proj_piece(i, cur) for i in range(4)] + [
            const((RET_V, 1)), col(RET_QK), col(RET_QK),
            const(decay_all.shape), const(zeta_tab.shape), const(xi_tab.shape), const(state_mask.shape),
            const(state_decay.shape), const(q_mask.shape), const((WOUT_BLK, D))],
        out_specs=(col(RET_V), col(RET_V), pl.BlockSpec((1, RET_V, RET_QK), lambda n: (n, 0, 0)),
                   pl.BlockSpec(memory_space=pl.ANY)),
        out_shape=(jax.ShapeDtypeStruct((RET_V, T), F32), jax.ShapeDtypeStruct((RET_V, T), F32),
                   jax.ShapeDtypeStruct((N_CH, RET_V, RET_QK), BF16), jax.ShapeDtypeStruct((N_DEV, WOUT_BLK, D), BF16)),
        scratch_shapes=[pltpu.VMEM((RET_V, RET_QK), F32), pltpu.VMEM((WOUT_BLK, D), BF16), pltpu.SemaphoreType.DMA,
                        pltpu.SemaphoreType.DMA((N_DEV - 1,)), pltpu.SemaphoreType.DMA((N_DEV - 1,))],
        compiler_params=_cparams(1),
    )(proj, proj, proj, proj, ret_norm_w, cos_t, sin_t, decay_all, zeta_tab, xi_tab, state_mask, state_decay, q_mask,
      w_out_blk)
    return ro, mix, states, wo.reshape(D, D)


def _retention_bwd(proj, ro, states, dmix, ret_norm_w, tables):
    cos_t, sin_t, decay_all, zeta_tab, xi_tab, state_mask, state_decay, q_mask = tables

    def body(rq_ref, rk_ref, rv_ref, rg_ref, ro_ref, st_ref, dm_ref, w_ref, cos_ref, sin_ref, decay_ref, zeta_ref,
             xi_ref, smask_ref, sdecay_ref, qmask_ref, d_ref, dw_ref, gstate):
        i = pl.program_id(0)

        @pl.when(i == 0)
        def _():
            gstate[...] = jnp.zeros_like(gstate)
            dw_ref[...] = jnp.zeros_like(dw_ref)

        cos, sin = cos_ref[...], sin_ref[...]
        q_t = _rotate(rq_ref[...], cos, sin)
        k_t = _rotate(rk_ref[...], cos, sin) * (RET_DK ** -0.5)
        q_b, k_b = _bf(q_t), _bf(k_t)
        k_rm = k_t.T
        kz_b = _bf(k_rm * zeta_ref[...])
        qx_b = _bf(q_t * xi_ref[...])
        v_t = rv_ref[...]
        v_b = _bf(v_t)
        v_rm_b = _bf(v_t.T)
        decay = decay_ref[...]
        scores_b = _bf(_retention_scores_t(q_t, _bf(k_rm), qmask_ref[...], decay))
        heads = range(RET_H)
        rows = [slice(h * RET_DV, (h + 1) * RET_DV) for h in heads]
        qk_rows = [slice(h * RET_DK, (h + 1) * RET_DK) for h in heads]
        lanes = [slice(h * CH, (h + 1) * CH) for h in heads]
        do = []
        for h in heads:
            g, w, dm = rg_ref[rows[h], :], w_ref[rows[h], :], dm_ref[rows[h], :]
            rn, rstd = _group_norm_t(ro_ref[rows[h], :])
            sig = _sigmoid(g)
            silu = g * sig
            d_ref[OFFS[3] + h * RET_DV:OFFS[3] + (h + 1) * RET_DV, :] = _bf(dm * rn * w * (sig * (1.0 + g * (1.0 - sig))))
            dw_ref[rows[h], :] += dm * silu * rn
            drn = dm * silu * w
            do.append(rstd * (drn - jnp.mean(drn, axis=0, keepdims=True)
                              - rn * jnp.mean(drn * rn, axis=0, keepdims=True)))
        do_b = _bf(jnp.concatenate(do, axis=0))
        m_b = st_ref[0]
        g_all = gstate[...]
        g_b = _bf(g_all)
        dscores_b = [_bf(_dot(v_rm_b[:, rows[h]], do_b[rows[h]]) * decay[:, lanes[h]]) for h in heads]
        dq_cross = lax.dot_general(m_b, do_b, (((0,), (0,)), ((), ())), preferred_element_type=F32)
        dkz = _dot(v_rm_b, g_b)
        dv_cross = _dot_nt(g_b, kz_b)
        gstate[...] = g_all * sdecay_ref[...] + _dot_nt(do_b, qx_b) * smask_ref[...]
        dq = jnp.concatenate([_dot(k_b[qk_rows[h]], dscores_b[h]) for h in heads], axis=0) + dq_cross * xi_ref[...]
        dk = (jnp.concatenate([_dot_nt(q_b[qk_rows[h]], dscores_b[h]) for h in heads], axis=0)
              + (dkz * zeta_ref[...]).T) * (RET_DK ** -0.5)
        for h in heads:
            d_ref[OFFS[2] + h * RET_DV:OFFS[2] + (h + 1) * RET_DV, :] = _bf(
                _dot_nt(do_b[rows[h]], scores_b[:, lanes[h]]) + dv_cross[rows[h]])
        d_ref[OFFS[0]:OFFS[1], :] = _bf(dq * cos + _swap_halves_t(dq * sin))
        d_ref[OFFS[1]:OFFS[2], :] = _bf(dk * cos + _swap_halves_t(dk * sin))

    col = lambda w: pl.BlockSpec((w, CH), lambda i: (0, N_CH - 1 - i))
    const = lambda shape: pl.BlockSpec(shape, lambda i: (0,) * len(shape))
    return pl.pallas_call(
        body, name="retention_bwd", grid=(N_CH,),
        in_specs=[_proj_piece(j, lambda i: N_CH - 1 - i) for j in range(4)] + [
                  col(RET_V),
                  pl.BlockSpec((1, RET_V, RET_QK), lambda i: (N_CH - 1 - i, 0, 0)), col(RET_V),
                  const((RET_V, 1)), col(RET_QK), col(RET_QK),
                  const(decay_all.shape), const(zeta_tab.shape), const(xi_tab.shape), const(state_mask.shape),
                  const(state_decay.shape), const(q_mask.shape)],
        out_specs=(col(RET_W), const((RET_V, CH))),
        out_shape=(jax.ShapeDtypeStruct((RET_W, T), BF16), jax.ShapeDtypeStruct((RET_V, CH), F32)),
        scratch_shapes=[pltpu.VMEM((RET_V, RET_QK), F32)],
        compiler_params=_cparams(1),
    )(proj, proj, proj, proj, ro, states, dmix, ret_norm_w, cos_t, sin_t, decay_all, zeta_tab, xi_tab, state_mask,
      state_decay, q_mask)


HQ_LANES = SWA_H * CH


def _head_lanes(hq):
    return slice(hq * CH, (hq + 1) * CH)


def _build_bias_t(bucket_ref, relb_ref, sinks_ref, bias_ref, sink_row):
    bt = bucket_ref[...]
    first = lax.broadcasted_iota(jnp.int32, bt.shape, 0) < CH
    for hq in range(SWA_H):
        b = jnp.full(bt.shape, NEG_INF, F32)
        for bk in range(N_BUCKETS):
            b = jnp.where(bt == bk, relb_ref[bk * SWA_H + hq], b)
        bias_ref[0, :, _head_lanes(hq)] = b
        bias_ref[1, :, _head_lanes(hq)] = jnp.where(first, NEG_INF, b)
        sink_row[:, _head_lanes(hq)] = jnp.full((1, CH), sinks_ref[hq], F32)


def _rms_t(t, w_col):
    r = lax.rsqrt(jnp.mean(t * t, axis=0, keepdims=True) + NORM_EPS)
    return t * r * w_col, r


def _rms_t_bwd(dn, t, r, w_col):
    u = dn * w_col
    return r * u - t * (r * r * r) * jnp.mean(u * t, axis=0, keepdims=True), dn * t * r


def _norm_kv(k_t, kw):
    return jnp.concatenate([_rms_t(k_t[g * SWA_D:(g + 1) * SWA_D], kw)[0] for g in range(SWA_KV)], axis=0)


def _kv_slot(a, kv):
    z = jnp.zeros_like(a)
    return jnp.concatenate([a, z] if kv == 0 else [z, a], axis=0)


def _softmax_t(s, sink):
    m = jnp.maximum(jnp.max(s, axis=0, keepdims=True), sink)
    p = jnp.exp(s - m)
    e_sink = jnp.exp(sink - m)
    inv = 1.0 / (jnp.sum(p, axis=0, keepdims=True) + e_sink)
    return p * inv, e_sink * inv


def _gate_rows(sg_halves, hq):
    per_half = SWA_H // 2
    return sg_halves[hq // per_half][(hq % per_half) * SWA_D:(hq % per_half + 1) * SWA_D, :]


def _swa_fwd(proj, q_norm_w, k_norm_w, sinks, rel_bias_flat, bucket_t):
    def body(sq_ref, skp_ref, skc_ref, svp_ref, svc_ref, sg_lo, sg_hi, qw_ref, kw_ref, sinks_ref, relb_ref, bucket_ref,
             so_ref, mix_ref, bias_ref, sink_row):
        n = pl.program_id(0)

        @pl.when(n == 0)
        def _():
            _build_bias_t(bucket_ref, relb_ref, sinks_ref, bias_ref, sink_row)

        var = (n == 0).astype(jnp.int32)
        qw, kw = qw_ref[...], kw_ref[...]
        kn_band = jnp.concatenate([_norm_kv(skp_ref[...], kw), _norm_kv(skc_ref[...], kw)], axis=1)
        kb_rm = _bf(kn_band.T)
        vband = _bf(jnp.concatenate([svp_ref[...], svc_ref[...]], axis=1))
        q_all = jnp.concatenate(
            [_bf(_kv_slot(_rms_t(sq_ref[hq * SWA_D:(hq + 1) * SWA_D, :], qw)[0] * (SWA_D ** -0.5), hq // SWA_G))
             for hq in range(SWA_H)], axis=1)
        probs, _ = _softmax_t(_dot(kb_rm, q_all) + bias_ref[var], sink_row[...])
        probs_b = _bf(probs)
        for kv in range(SWA_KV):
            o = _dot(vband[kv * SWA_D:(kv + 1) * SWA_D], probs_b[:, kv * SWA_G * CH:(kv + 1) * SWA_G * CH])
            for j in range(SWA_G):
                rows = slice((kv * SWA_G + j) * SWA_D, (kv * SWA_G + j + 1) * SWA_D)
                oh = o[:, j * CH:(j + 1) * CH]
                g = _gate_rows((sg_lo, sg_hi), kv * SWA_G + j)
                so_ref[rows, :] = oh
                mix_ref[rows, :] = oh * (g * _sigmoid(g))

    col = lambda w: pl.BlockSpec((w, CH), lambda n: (0, n))
    const = lambda shape: pl.BlockSpec(shape, lambda n: (0,) * len(shape))
    smem = pl.BlockSpec(memory_space=pltpu.SMEM)
    cur = lambda n: n
    prev = lambda n: jnp.maximum(n - 1, 0)
    return pl.pallas_call(
        body, name="swa_fwd", grid=(N_CH,),
        in_specs=[_proj_piece(4, cur), _proj_piece(5, prev), _proj_piece(5, cur), _proj_piece(6, prev),
                  _proj_piece(6, cur)] + _proj_gate_halves(cur) + [
                  const((SWA_D, 1)), const((SWA_D, 1)), smem, smem, const((2 * CH, CH))],
        out_specs=(col(512), col(512)),
        out_shape=(jax.ShapeDtypeStruct((512, T), F32), jax.ShapeDtypeStruct((512, T), F32)),
        scratch_shapes=[pltpu.VMEM((2, 2 * CH, HQ_LANES), F32), pltpu.VMEM((1, HQ_LANES), F32)],
        compiler_params=_cparams(1),
    )(proj, proj, proj, proj, proj, proj, proj, q_norm_w, k_norm_w, sinks, rel_bias_flat, bucket_t)


def _swa_bwd(proj, so, dmix, q_norm_w, k_norm_w, sinks, rel_bias_flat, bucket_t):
    def body(sq_ref, skp_ref, skc_ref, svp_ref, svc_ref, sg_lo, sg_hi, so_ref, dm_ref, qw_ref, kw_ref, sinks_ref,
             relb_ref, bucket_ref, d_ref, dbias_ref, dsink_ref, dqw_ref, dkw_ref,
             bias_ref, sink_row, band_dk, band_dv, carry_dk, carry_dv, hold_dq, hold_dg):
        n = pl.program_id(0)
        slot = n % 2

        @pl.when(n == 0)
        def _():
            _build_bias_t(bucket_ref, relb_ref, sinks_ref, bias_ref, sink_row)
            for ref in (dbias_ref, dsink_ref, dqw_ref, dkw_ref, carry_dk, carry_dv):
                ref[...] = jnp.zeros_like(ref)

        qw, kw = qw_ref[...], kw_ref[...]

        @pl.when(n < N_CH)
        def _():
            var = (n == 0).astype(jnp.int32)
            kn_band = jnp.concatenate([_norm_kv(skp_ref[...], kw), _norm_kv(skc_ref[...], kw)], axis=1)
            kb_rm = _bf(kn_band.T)
            kn_band_s = _bf(kn_band * (SWA_D ** -0.5))
            vband_f = jnp.concatenate([svp_ref[...], svc_ref[...]], axis=1)
            vb_rm = _bf(vband_f.T)
            q_raw, q_rstd, qs_b, do_b = [], [], [], []
            for hq in range(SWA_H):
                rows = slice(hq * SWA_D, (hq + 1) * SWA_D)
                q_t = sq_ref[rows, :]
                qn, rq = _rms_t(q_t, qw)
                g = _gate_rows((sg_lo, sg_hi), hq)
                sig = _sigmoid(g)
                dm = dm_ref[rows, :]
                hold_dg[slot, rows, :] = _bf(dm * so_ref[rows, :] * (sig * (1.0 + g * (1.0 - sig))))
                q_raw.append(q_t)
                q_rstd.append(rq)
                qs_b.append(_bf(qn * (SWA_D ** -0.5)))
                do_b.append(_bf(dm * (g * sig)))
            q_all = jnp.concatenate([_kv_slot(qs_b[hq], hq // SWA_G) for hq in range(SWA_H)], axis=1)
            do_all = jnp.concatenate([_kv_slot(do_b[hq], hq // SWA_G) for hq in range(SWA_H)], axis=1)
            probs, p_sink = _softmax_t(_dot(kb_rm, q_all) + bias_ref[var], sink_row[...])
            dprobs = _dot(vb_rm, do_all)
            t = jnp.sum(probs * dprobs, axis=0, keepdims=True)
            dlog = probs * (dprobs - t)
            dsink_ref[...] += -(p_sink * t)
            dbias_ref[...] += dlog
            dlog_b, probs_b = _bf(dlog), _bf(probs)
            dkn, dvv = [], []
            for kv in range(SWA_KV):
                heads = range(kv * SWA_G, (kv + 1) * SWA_G)
                lanes = slice(kv * SWA_G * CH, (kv + 1) * SWA_G * CH)
                dvv.append(_dot_nt(jnp.concatenate([do_b[hq] for hq in heads], axis=1), probs_b[:, lanes]))
                dkn.append(_dot_nt(jnp.concatenate([qs_b[hq] for hq in heads], axis=1), dlog_b[:, lanes]))
                dqn = _dot(kn_band_s[kv * SWA_D:(kv + 1) * SWA_D], dlog_b[:, lanes])
                for j, hq in enumerate(heads):
                    dq_t, dqw_terms = _rms_t_bwd(dqn[:, j * CH:(j + 1) * CH], q_raw[hq], q_rstd[hq], qw)
                    hold_dq[slot, hq * SWA_D:(hq + 1) * SWA_D, :] = _bf(dq_t)
                    dqw_ref[...] += dqw_terms
            band_dk[...] = jnp.concatenate(dkn, axis=0)
            band_dv[...] = jnp.concatenate(dvv, axis=0)

        @pl.when(n == N_CH)
        def _():
            band_dk[...] = jnp.zeros_like(band_dk)
            band_dv[...] = jnp.zeros_like(band_dv)

        @pl.when(n >= 1)
        def _():
            dkn_prev = carry_dk[...] + band_dk[:, 0:CH]
            k_t = skp_ref[...]
            for kv in range(SWA_KV):
                rows = slice(kv * SWA_D, (kv + 1) * SWA_D)
                _, rk = _rms_t(k_t[rows], kw)
                dk_t, dkw_terms = _rms_t_bwd(dkn_prev[rows], k_t[rows], rk, kw)
                d_ref[SWA_OFFS[1] + kv * SWA_D:SWA_OFFS[1] + (kv + 1) * SWA_D, :] = _bf(dk_t)
                dkw_ref[...] += dkw_terms
            d_ref[SWA_OFFS[2]:SWA_OFFS[3], :] = _bf(carry_dv[...] + band_dv[:, 0:CH])
            d_ref[SWA_OFFS[0]:SWA_OFFS[1], :] = hold_dq[1 - slot]
            d_ref[SWA_OFFS[3]:SWA_OFFS[4], :] = hold_dg[1 - slot]

        carry_dk[...] = band_dk[:, CH:2 * CH]
        carry_dv[...] = band_dv[:, CH:2 * CH]

    cur_block = lambda n: jnp.minimum(n, N_CH - 1)
    prev_block = lambda n: jnp.maximum(n - 1, 0)
    col = lambda w: pl.BlockSpec((w, CH), lambda n: (0, cur_block(n)))
    prev = lambda w: pl.BlockSpec((w, CH), lambda n: (0, prev_block(n)))
    const = lambda shape: pl.BlockSpec(shape, lambda n: (0,) * len(shape))
    smem = pl.BlockSpec(memory_space=pltpu.SMEM)
    return pl.pallas_call(
        body, name="swa_bwd", grid=(N_CH + 1,),
        in_specs=[_proj_piece(4, cur_block), _proj_piece(5, prev_block), _proj_piece(5, cur_block),
                  _proj_piece(6, prev_block), _proj_piece(6, cur_block)] + _proj_gate_halves(cur_block) + [
                  col(512), col(512), const((SWA_D, 1)), const((SWA_D, 1)), smem, smem, const((2 * CH, CH))],
        out_specs=(prev(SWA_W), const((2 * CH, HQ_LANES)), const((1, HQ_LANES)),
                   const((SWA_D, CH)), const((SWA_D, CH))),
        out_shape=(jax.ShapeDtypeStruct((SWA_W, T), BF16),
                   jax.ShapeDtypeStruct((2 * CH, HQ_LANES), F32), jax.ShapeDtypeStruct((1, HQ_LANES), F32),
                   jax.ShapeDtypeStruct((SWA_D, CH), F32), jax.ShapeDtypeStruct((SWA_D, CH), F32)),
        scratch_shapes=[pltpu.VMEM((2, 2 * CH, HQ_LANES), F32), pltpu.VMEM((1, HQ_LANES), F32),
                        pltpu.VMEM((128, 2 * CH), F32), pltpu.VMEM((128, 2 * CH), F32),
                        pltpu.VMEM((128, CH), F32), pltpu.VMEM((128, CH), F32),
                        pltpu.VMEM((2, 512, CH), BF16), pltpu.VMEM((2, 512, CH), BF16)],
        compiler_params=_cparams(1),
    )(proj, proj, proj, proj, proj, proj, proj, so, dmix, q_norm_w, k_norm_w, sinks, rel_bias_flat, bucket_t)


def _small_grads(dbias, bucket_t, dsink_acc, dqw_acc, dkw_acc, drw_acc):
    def body(dbias_ref, bucket_ref, dsink_ref, dqw_ref, dkw_ref, drw_ref, relb_o, sink_o, qw_o, kw_o, rw_o):
        bt = bucket_ref[...]
        row = lax.broadcasted_iota(jnp.int32, (N_BUCKETS, SWA_H), 0)
        col = lax.broadcasted_iota(jnp.int32, (N_BUCKETS, SWA_H), 1)
        lane = lax.broadcasted_iota(jnp.int32, (1, CH), 1)
        acc = jnp.zeros((N_BUCKETS, SWA_H), F32)
        sink = jnp.zeros((1, CH), F32)
        for hq in range(SWA_H):
            d = dbias_ref[:, _head_lanes(hq)]
            for bk in range(N_BUCKETS):
                s = jnp.sum(jnp.sum(jnp.where(bt == bk, d, 0.0), axis=0, keepdims=True), axis=1, keepdims=True)
                acc = acc + jnp.where((row == bk) & (col == hq), s, 0.0)
            sink = sink + jnp.where(lane == hq, jnp.sum(dsink_ref[:, _head_lanes(hq)], axis=1, keepdims=True), 0.0)
        relb_o[...] = acc
        sink_o[...] = sink
        for src, dst in ((dqw_ref, qw_o), (dkw_ref, kw_o)):
            padded = jnp.concatenate([src[...], jnp.zeros((CH - SWA_D, CH), F32)], axis=0)
            dst[...] = jnp.sum(padded.T, axis=0, keepdims=True)
        rw_o[...] = jnp.sum(drw_ref[...].T, axis=0, keepdims=True)

    vm = pl.BlockSpec(memory_space=pltpu.VMEM)
    row128 = jax.ShapeDtypeStruct((1, CH), F32)
    return pl.pallas_call(body, name="small_grads",
                          out_shape=(jax.ShapeDtypeStruct((N_BUCKETS, SWA_H), F32), row128, row128, row128,
                                     jax.ShapeDtypeStruct((1, RET_V), F32)),
                          in_specs=[vm] * 6, out_specs=(vm,) * 5, compiler_params=_cparams(),
                          )(dbias, bucket_t, dsink_acc, dqw_acc, dkw_acc, drw_acc)


def _out_proj(mix_r, mix_s, wo, x, target):
    def body(mr_ref, ms_ref, w_ref, x_ref, t_ref, loss_ref, dy_ref, dmr_ref, dms_ref, gw_ref):
        i = pl.program_id(0)

        @pl.when(i == 0)
        def _():
            loss_ref[...] = jnp.zeros_like(loss_ref)
            gw_ref[...] = jnp.zeros_like(gw_ref)

        mixed = jnp.concatenate([mr_ref[...], ms_ref[...]], axis=0)
        w = w_ref[...]
        err = x_ref[...] + _dot(_bf(mixed.T), w) - t_ref[...]
        loss_ref[...] += jnp.sum(jnp.sum(err * err, axis=1, keepdims=True), axis=0, keepdims=True)
        dy = err * (1.0 / D)
        dy_ref[...] = dy
        dy_b = _bf(dy)
        dmix = _dot_nt(w, dy_b)
        dmr_ref[...] = dmix[0:512]
        dms_ref[...] = dmix[512:D]
        gw_ref[...] += _dot(_bf(mixed), dy_b)

    row = lambda w: pl.BlockSpec((TM, w), lambda i: (i, 0))
    col = lambda w: pl.BlockSpec((w, TM), lambda i: (0, i))
    const = lambda shape: pl.BlockSpec(shape, lambda i: (0,) * len(shape))
    return pl.pallas_call(
        body, name="out_proj", grid=(T // TM,),
        in_specs=[col(512), col(512), const((D, D)), row(D), row(D)],
        out_specs=(const((1, 1)), row(D), col(512), col(512), const((D, D))),
        out_shape=(jax.ShapeDtypeStruct((1, 1), F32), jax.ShapeDtypeStruct((T, D), F32),
                   jax.ShapeDtypeStruct((512, T), F32), jax.ShapeDtypeStruct((512, T), F32),
                   jax.ShapeDtypeStruct((D, D), F32)),
        compiler_params=_cparams(1),
    )(mix_r, mix_s, wo, x, target)


LOSS_ROW = 5
ALL_CHIPS = ((0, 0), (0, 1), (1, 0), (1, 1))


def _in_proj_bwd_rs(d_ret, d_swa, h, wt, x, rstd, norm_w, dy, gwo, sse, small_g):
    n_blocks, n_tiles = N_DEV, T // TM
    last = n_blocks + n_tiles - 1

    def body(da_any, db_any, da_ref, db_ref, h_ref, w_ref, x_ref, r_ref, nw_ref, dy_ref, gwo_ref, sse_ref,
             rw_ref, qw_ref, kw_ref, sk_ref, rb_ref, gx_ref, gin_o, gout_o, suma_o, sumb_o,
             stage, d2d_src, rcv, snd, got, tot, own_o, rcv_o, snd_o, got_o, tot_o, tab_a, tab_b, gnw_acc,
             dp_sems, d2d_send, d2d_recv, ici_send, ici_recv, o_local, oa_send, oa_recv, ob_send, ob_recv,
             s_send, s_recv):
        k = pl.program_id(0)
        x_, y_, c_ = _mesh_pos()
        me = _blk(x_, y_, c_)
        sibling = (x_, y_, 1 - c_)
        rel_chips = [(1 - x_, y_), (x_, 1 - y_), (1 - x_, 1 - y_), (x_, y_)]

        def block_of_step(s):
            return _blk(*rel_chips[s // 2], 1 - c_ if s % 2 == 0 else c_)

        def fetch(s, wait):
            slot, b = s % 2, block_of_step(s)
            split = RET_W - 4 * WIN_BLK

            def run(src, dst, sem):
                cp = pltpu.make_async_copy(src, dst, sem)
                cp.wait() if wait else cp.start()

            @pl.when(b < 4)
            def _():
                run(da_any.at[pl.ds(pl.multiple_of(b * WIN_BLK, 16), WIN_BLK), :], stage.at[slot], dp_sems.at[slot, 0])

            @pl.when(b == 4)
            def _():
                run(da_any.at[pl.ds(4 * WIN_BLK, split), :], stage.at[slot, pl.ds(0, split), :], dp_sems.at[slot, 0])
                run(db_any.at[pl.ds(0, WIN_BLK - split), :], stage.at[slot, pl.ds(split, WIN_BLK - split), :],
                    dp_sems.at[slot, 1])

            @pl.when(b > 4)
            def _():
                run(db_any.at[pl.ds(pl.multiple_of(b * WIN_BLK - RET_W, 16), WIN_BLK), :], stage.at[slot],
                    dp_sems.at[slot, 0])

        def d2d_copy(r):
            return pltpu.make_async_remote_copy(src_ref=d2d_src, dst_ref=rcv.at[r], send_sem=d2d_send.at[r],
                                                recv_sem=d2d_recv.at[r], device_id=sibling, device_id_type=MESH)

        def ici_copy(r, src, dst, send, recv):
            return pltpu.make_async_remote_copy(src_ref=src.at[r], dst_ref=dst.at[r], send_sem=send.at[r],
                                                recv_sem=recv.at[r], device_id=(*rel_chips[r], c_), device_id_type=MESH)

        def oa_copy(j):
            px, py = ALL_CHIPS[j]
            return pltpu.make_async_remote_copy(src_ref=gwo_ref.at[_blk(px, py, 1 - c_)], dst_ref=rcv_o.at[j],
                                                send_sem=oa_send.at[j], recv_sem=oa_recv.at[j], device_id=sibling,
                                                device_id_type=MESH)

        def o_local_copy(j):
            px, py = ALL_CHIPS[j]
            return pltpu.make_async_copy(gwo_ref.at[_blk(px, py, c_)], own_o.at[j], o_local.at[j])

        def table_copies():
            return [pltpu.make_async_remote_copy(src_ref=tab.at[me], dst_ref=tab.at[me], send_sem=s_send.at[a, p - 1],
                                                 recv_sem=s_recv.at[a, p - 1],
                                                 device_id=(x_ ^ (p >> 2), y_ ^ ((p >> 1) & 1), c_ ^ (p & 1)),
                                                 device_id_type=MESH)
                    for p in range(1, N_DEV) for a, tab in enumerate((tab_a, tab_b))]

        for s in range(n_blocks):
            @pl.when(k == s)
            def _(s=s):
                r = s // 2
                if s == 0:
                    gnw_acc[...] = jnp.zeros_like(gnw_acc)
                    fetch(0, wait=False)
                    for j in range(len(ALL_CHIPS)):
                        oa_copy(j).start()
                        o_local_copy(j).start()
                fetch(s, wait=True)
                if s + 1 < n_blocks:
                    fetch(s + 1, wait=False)
                gw = _dot(stage[s % 2], h_ref[...])
                if s % 2 == 0:
                    if s >= 2:
                        d2d_copy(r - 1).wait_send()
                    d2d_src[...] = gw
                    d2d_copy(r).start()
                else:
                    d2d_copy(r).wait_recv()
                    chip_sum = gw + rcv[r]
                    if r < 3:
                        snd[r] = _bf(chip_sum)
                        ici_copy(r, snd, got, ici_send, ici_recv).start()
                    else:
                        tot[...] = chip_sum
                if s == 2:
                    for j in range(len(ALL_CHIPS)):
                        o_local_copy(j).wait()
                        oa_copy(j).wait_recv()
                    for q in range(3):
                        j = 2 * rel_chips[q][0] + rel_chips[q][1]
                        snd_o[q] = _bf(own_o[j] + rcv_o[j])
                        ici_copy(q, snd_o, got_o, ob_send, ob_recv).start()
                    jm = 2 * x_ + y_
                    tot_o[...] = own_o[jm] + rcv_o[jm]

        @pl.when(k >= n_blocks)
        def _():
            dp = jnp.concatenate([da_ref[...], db_ref[...]], axis=0)
            xv, r, nw = x_ref[...], r_ref[...], nw_ref[...]
            dh = lax.dot_general(dp, w_ref[...], (((0,), (0,)), ((), ())), preferred_element_type=F32)
            u = dh * nw
            gx_ref[...] = dy_ref[...] + r * u - xv * (r * r * r) * jnp.mean(u * xv, axis=-1, keepdims=True)
            gnw_acc[...] += jnp.sum(dh * (xv * r), axis=0, keepdims=True)

        @pl.when(k == last)
        def _():
            mine_a, mine_b = tab_a.at[me], tab_b.at[me]
            mine_a[...] = jnp.zeros_like(mine_a)
            mine_b[...] = jnp.zeros_like(mine_b)
            for row, ref in enumerate((gnw_acc, rw_ref, qw_ref, kw_ref, sk_ref)):
                mine_a[row:row + 1, 0:ref.shape[1]] = ref[...]
            mine_a[LOSS_ROW:LOSS_ROW + 1, 0:1] = sse_ref[...]
            mine_b[:, 0:SWA_H] = rb_ref[...]
            tables = table_copies()
            for cp in tables:
                cp.start()
            g_in, g_out = tot[...], tot_o[...]
            for q in range(3):
                ici_copy(q, snd, got, ici_send, ici_recv).wait_recv()
                ici_copy(q, snd_o, got_o, ob_send, ob_recv).wait_recv()
                g_in = g_in + got[q].astype(F32)
                g_out = g_out + got_o[q].astype(F32)
            gin_o[...] = g_in
            gout_o[...] = g_out
            for cp in tables:
                cp.wait_recv()
            sum_a, sum_b = tab_a[0], tab_b[0]
            for b in range(1, N_DEV):
                sum_a = sum_a + tab_a[b]
                sum_b = sum_b + tab_b[b]
            suma_o[...] = sum_a
            sumb_o[...] = sum_b
            d2d_copy(3).wait_send()
            for j in range(len(ALL_CHIPS)):
                oa_copy(j).wait_send()
            for q in range(3):
                ici_copy(q, snd, got, ici_send, ici_recv).wait_send()
                ici_copy(q, snd_o, got_o, ob_send, ob_recv).wait_send()
            for cp in tables:
                cp.wait_send()

    tile = lambda k: jnp.maximum(k - n_blocks, 0)
    row = lambda w: pl.BlockSpec((TM, w), lambda k: (tile(k), 0))
    col = lambda w: pl.BlockSpec((w, TM), lambda k: (0, tile(k)))
    const = lambda shape, **kw: pl.BlockSpec(shape, lambda k: (0,) * len(shape), **kw)
    once = dict(pipeline_mode=pl.Buffered(1))
    hbm = pl.BlockSpec(memory_space=pl.ANY)
    dma = pltpu.SemaphoreType.DMA
    return pl.pallas_call(
        body, name="in_proj_bwd_rs", grid=(n_blocks + n_tiles,),
        in_specs=[hbm, hbm, col(RET_W), col(SWA_W), const((T, D), **once), const((D_IN, D), **once),
                  row(D), row(1), const((1, D)), row(D), hbm, const((1, 1))]
                 + [const(g.shape) for g in small_g],
        out_specs=(row(D), const((WIN_BLK, D)), const((WOUT_BLK, D)), const((8, D)), const((N_BUCKETS, CH))),
        out_shape=(jax.ShapeDtypeStruct((T, D), F32), jax.ShapeDtypeStruct((WIN_BLK, D), F32),
                   jax.ShapeDtypeStruct((WOUT_BLK, D), F32), jax.ShapeDtypeStruct((8, D), F32),
                   jax.ShapeDtypeStruct((N_BUCKETS, CH), F32)),
        scratch_shapes=[
            pltpu.VMEM((2, WIN_BLK, T), BF16), pltpu.VMEM((WIN_BLK, D), F32),
            pltpu.VMEM((4, WIN_BLK, D), F32), pltpu.VMEM((3, WIN_BLK, D), BF16),
            pltpu.VMEM((3, WIN_BLK, D), BF16), pltpu.VMEM((WIN_BLK, D), F32),
            pltpu.VMEM((4, WOUT_BLK, D), F32), pltpu.VMEM((4, WOUT_BLK, D), F32),
            pltpu.VMEM((3, WOUT_BLK, D), BF16), pltpu.VMEM((3, WOUT_BLK, D), BF16),
            pltpu.VMEM((WOUT_BLK, D), F32),
            pltpu.VMEM((N_DEV, 8, D), F32), pltpu.VMEM((N_DEV, N_BUCKETS, CH), F32),
            pltpu.VMEM((1, D), F32),
            dma((2, 2)), dma((4,)), dma((4,)), dma((3,)), dma((3,)),
            dma((4,)), dma((4,)), dma((4,)), dma((3,)), dma((3,)), dma((2, 7)), dma((2, 7)),
        ],
        compiler_params=_cparams(1),
    )(d_ret, d_swa, d_ret, d_swa, h, wt, x, rstd, norm_w, dy, gwo, sse, *small_g)


SMALL_SHAPES = ((1, D), (1, 512), (1, SWA_D), (1, SWA_D), (1, SWA_H), (N_BUCKETS, SWA_H))


def _adamw_all(g_in_t, g_out, sum_a, sum_b, w_in, m_in, v_in, w_out, m_out, v_out, small_w, small_m, small_v):
    n_small = len(SMALL_SHAPES)

    def body(*refs):
        gin_ref, gout_ref, suma_ref, sumb_ref, w_in_ref, m_in_ref, v_in_ref, w_out_ref, m_out_ref, v_out_ref = refs[:10]
        pos = 10
        sw_refs, sm_refs, sv_refs = (refs[pos + i * n_small:pos + (i + 1) * n_small] for i in range(3))
        pos += 3 * n_small
        g_in_o, d_in_o, nm_in_o, nv_in_o, g_out_o, d_out_o, nm_out_o, nv_out_o = refs[pos:pos + 8]
        pos += 8
        sg_o, sd_o, snm_o, snv_o = (refs[pos + i * n_small:pos + (i + 1) * n_small] for i in range(4))
        pos += 4 * n_small
        loss_o, pad_ref, tr_ref = refs[pos:]

        g = gout_ref[...]
        d, nm, nv = _adamw(w_out_ref[...], g, m_out_ref[...], v_out_ref[...])
        g_out_o[...], d_out_o[...], nm_out_o[...], nv_out_o[...] = g, d, nm, nv

        pad_ref[0:WIN_BLK, :] = gin_ref[...]
        pad_ref[WIN_BLK:WIN_PAD, :] = jnp.zeros((WIN_PAD - WIN_BLK, D), F32)
        tr_ref[...] = pad_ref[...].T
        g = tr_ref[:, 0:WIN_BLK]
        d, nm, nv = _adamw(w_in_ref[...], g, m_in_ref[...], v_in_ref[...])
        g_in_o[...], d_in_o[...], nm_in_o[...], nv_in_o[...] = g, d, nm, nv

        loss_o[...] = suma_ref[LOSS_ROW:LOSS_ROW + 1, 0:1] * (0.5 / D)
        for r, (rows, lanes) in enumerate(SMALL_SHAPES):
            g = suma_ref[r:r + 1, 0:lanes] if rows == 1 else sumb_ref[:, 0:lanes]
            d, nm, nv = _adamw(sw_refs[r][...], g, sm_refs[r][...], sv_refs[r][...])
            sg_o[r][...], sd_o[r][...], snm_o[r][...], snv_o[r][...] = g, d, nm, nv

    vm = pl.BlockSpec(memory_space=pltpu.VMEM)
    win = jax.ShapeDtypeStruct((D, WIN_BLK), F32)
    wout = jax.ShapeDtypeStruct((WOUT_BLK, D), F32)
    smalls = tuple(jax.ShapeDtypeStruct(s, F32) for s in SMALL_SHAPES)
    outs = pl.pallas_call(
        body, name="adamw_all",
        in_specs=[vm] * (10 + 3 * n_small), out_specs=(vm,) * (9 + 4 * n_small),
        out_shape=(win,) * 4 + (wout,) * 4 + smalls * 4 + (jax.ShapeDtypeStruct((1, 1), F32),),
        scratch_shapes=[pltpu.VMEM((WIN_PAD, D), F32), pltpu.VMEM((D, WIN_PAD), F32)],
        compiler_params=_cparams(vmem=32 * 1024 * 1024),
    )(g_in_t, g_out, sum_a, sum_b, w_in, m_in, v_in, w_out, m_out, v_out, *small_w, *small_m, *small_v)
    big, rest = outs[:8], outs[8:]
    return big, [rest[i * n_small:(i + 1) * n_small] for i in range(4)], rest[4 * n_small]


def _small_rows(norm_w, ret_norm_w, q_norm_w, k_norm_w, sinks, rel_bias):
    return (norm_w.reshape(1, D), ret_norm_w.reshape(1, 512), q_norm_w.reshape(1, SWA_D), k_norm_w.reshape(1, SWA_D),
            sinks.reshape(1, SWA_H), rel_bias)


def _small_leaves(rows):
    return (rows[0].reshape(D), rows[1].reshape(512), rows[2].reshape(SWA_D), rows[3].reshape(SWA_D),
            rows[4].reshape(SWA_H), rows[5])


def kernel(x, norm_w, w_in, ret_norm_w, q_norm_w, k_norm_w, sinks, rel_bias, w_out, loss_target, m_norm_w, m_w_in, m_ret_norm_w, m_q_norm_w, m_k_norm_w, m_sinks, m_rel_bias, m_w_out, v_norm_w, v_w_in, v_ret_norm_w, v_q_norm_w, v_k_norm_w, v_sinks, v_rel_bias, v_w_out):
    x2 = x.reshape(T, D)
    target = loss_target.reshape(T, D)
    nw = norm_w.reshape(1, D)
    rnw_col = ret_norm_w.reshape(RET_V, 1)
    qnw_col = q_norm_w.reshape(SWA_D, 1)
    knw_col = k_norm_w.reshape(SWA_D, 1)
    relb = rel_bias.reshape(N_BUCKETS * SWA_H)
    ret_tables = _rotary_tables_t() + _retention_tables_t()
    bucket_t = _bucket_table_t()

    proj, wt, rstd, h = _in_proj_gather(x2, nw, w_in)
    ro, mix_r, states, wo = _retention_fwd(proj, rnw_col, ret_tables, w_out)
    so, mix_s = _swa_fwd(proj, qnw_col, knw_col, sinks, relb, bucket_t)
    sse, dy, dmix_r, dmix_s, gwo = _out_proj(mix_r, mix_s, wo, x2, target)

    d_ret, drw_acc = _retention_bwd(proj, ro, states, dmix_r, rnw_col, ret_tables)
    d_swa, dbias, dsink_acc, dqw_acc, dkw_acc = _swa_bwd(proj, so, dmix_s, qnw_col, knw_col, sinks, relb, bucket_t)
    d_relb, d_sinks, d_qnw, d_knw, d_rnw = _small_grads(dbias, bucket_t, dsink_acc, dqw_acc, dkw_acc, drw_acc)
    grad_x, g_in_t, g_out, sum_a, sum_b = _in_proj_bwd_rs(
        d_ret, d_swa, h, wt, x2, rstd, nw, dy, gwo.reshape(N_DEV, WOUT_BLK, D), sse,
        (d_rnw, d_qnw, d_knw, d_sinks, d_relb))

    small_w = _small_rows(norm_w, ret_norm_w, q_norm_w, k_norm_w, sinks, rel_bias)
    small_m = _small_rows(m_norm_w, m_ret_norm_w, m_q_norm_w, m_k_norm_w, m_sinks, m_rel_bias)
    small_v = _small_rows(v_norm_w, v_ret_norm_w, v_q_norm_w, v_k_norm_w, v_sinks, v_rel_bias)
    big, small, loss = _adamw_all(g_in_t, g_out, sum_a, sum_b, w_in, m_w_in, v_w_in, w_out, m_w_out, v_w_out,
                                  small_w, small_m, small_v)

    def leaves(i):
        a = _small_leaves(small[i])
        return (a[0], big[i], a[1], a[2], a[3], a[4], a[5], big[4 + i])

    return (loss.reshape(()), grad_x.reshape(1, T, D), *leaves(0), *leaves(1), *leaves(2), *leaves(3))
```

```python
import numpy as np
import jax
import jax.numpy as jnp
from jax import lax
from jax.experimental import pallas as pl
from jax.experimental.pallas import tpu as pltpu

F32 = jnp.float32
BF16 = jnp.bfloat16
MESH = pl.DeviceIdType.MESH

T = 2048
D = 1024
D_IN = 2816
N_DEV = 8
WIN_BLK = D_IN // N_DEV
WOUT_BLK = D // N_DEV
CH = 128
N_CH = T // CH
RET_H, RET_DK, RET_DV = 4, 64, 128
RET_QK = RET_H * RET_DK
RET_V = RET_H * RET_DV
SWA_H, SWA_KV, SWA_D, SWA_G = 8, 2, 64, 4
N_BUCKETS = 32
NORM_EPS = 1e-6
GN_EPS = 1e-5
NEG_INF = -1e30
PIECES = (256, 256, 512, 512, 512, 128, 128, 512)
OFFS = tuple(int(v) for v in np.cumsum((0,) + PIECES))
RET_W = OFFS[4]
SWA_W = D_IN - RET_W
SWA_OFFS = tuple(o - RET_W for o in OFFS[4:])
TM = 256

ADAM_LR, ADAM_B1, ADAM_B2, ADAM_EPS, ADAM_WD, ADAM_STEP = 0.001, 0.9, 0.999, 1e-08, 0.01, 10

VMEM_LIMIT = 56 * 1024 * 1024


def _cparams(n_grid=0, vmem=VMEM_LIMIT):
    sem = ("arbitrary",) * n_grid if n_grid else None
    return pltpu.CompilerParams(dimension_semantics=sem, vmem_limit_bytes=vmem)


def _dot(a, b):
    return jnp.dot(a, b, preferred_element_type=F32)


def _dot_nt(a, b):
    return lax.dot_general(a, b, (((1,), (1,)), ((), ())), preferred_element_type=F32)


def _bf(a):
    return a.astype(BF16)


def _sigmoid(x):
    return 1.0 / (1.0 + jnp.exp(-x))


def _adamw(w, g, m, v):
    m = ADAM_B1 * m + (1.0 - ADAM_B1) * g
    v = ADAM_B2 * v + (1.0 - ADAM_B2) * (g * g)
    m_hat = m / (1.0 - ADAM_B1 ** ADAM_STEP)
    v_hat = v / (1.0 - ADAM_B2 ** ADAM_STEP)
    delta = -ADAM_LR * (m_hat / (jnp.sqrt(v_hat) + ADAM_EPS) + ADAM_WD * w)
    return delta, m, v


def _rotary_tables_t():
    half = RET_DK // 2
    inv_freq = np.float32(10000.0) ** (-np.arange(half, dtype=np.float32) / np.float32(half))
    ang = inv_freq[:, None] * np.arange(T, dtype=np.float32)[None, :]
    cos, sin = np.cos(ang).astype(np.float32), np.sin(ang).astype(np.float32)
    cos64 = np.concatenate([cos, cos], axis=0)
    sin64 = np.concatenate([-sin, sin], axis=0)
    return np.tile(cos64, (RET_H, 1)), np.tile(sin64, (RET_H, 1))


def _retention_tables_t():
    gamma = (1.0 - np.exp2(-5.0 - np.arange(RET_H, dtype=np.float32))).astype(np.float32)
    log_g = np.log(gamma).astype(np.float32)
    i = np.arange(CH, dtype=np.float32)
    diff = i[None, :] - i[:, None]
    decay = np.where(diff >= 0, np.exp(log_g[:, None, None] * np.maximum(diff, 0.0)), 0.0).astype(np.float32)
    decay_all = np.concatenate(list(decay), axis=1)
    zeta = np.exp(log_g[:, None] * (CH - 1.0 - i)).astype(np.float32)
    zeta_tab = np.repeat(zeta.T, RET_DK, axis=1)
    xi = np.exp(log_g[:, None] * (i + 1.0)).astype(np.float32)
    xi_tab = np.repeat(xi, RET_DK, axis=0)
    chunk_decay = np.exp(log_g * np.float32(CH)).astype(np.float32)
    row_head = np.arange(RET_V)[:, None] // RET_DV
    col_head = np.arange(RET_QK)[None, :] // RET_DK
    state_mask = (row_head == col_head).astype(np.float32)
    state_decay = (state_mask * chunk_decay[row_head]).astype(np.float32)
    q_mask = (np.arange(RET_QK)[:, None] // RET_DK == np.arange(RET_H * CH)[None, :] // CH).astype(np.float32)
    return (np.ascontiguousarray(decay_all), np.ascontiguousarray(zeta_tab), np.ascontiguousarray(xi_tab),
            state_mask, state_decay, q_mask)


def _bucket_table_t():
    qi = np.arange(CH)[:, None]
    kj = np.arange(2 * CH)[None, :]
    dist = qi + CH - kj
    n = np.maximum(dist, 0)
    max_exact = N_BUCKETS // 2
    nf = np.maximum(n, 1).astype(np.float32)
    large = max_exact + (np.log(nf / np.float32(max_exact)) / np.float32(np.log(CH / max_exact))
                         * np.float32(N_BUCKETS - max_exact)).astype(np.int32)
    large = np.minimum(large, N_BUCKETS - 1)
    bucket = np.where(n < max_exact, n, large)
    return np.ascontiguousarray(np.where((dist >= 0) & (dist < CH), bucket, -1).astype(np.int32).T)


def _mesh_pos():
    return lax.axis_index("x"), lax.axis_index("y"), lax.axis_index("c")


def _blk(px, py, pc):
    return 4 * px + 2 * py + pc


def _in_proj_gather(x, norm_w, w_in_t):
    tchunk = 512

    def body(x_ref, nw_ref, win_ref, proj_ref, wt_ref, rstd_ref, h_ref, stage, send_sems, recv_sems, out_sems):
        x, y, c = _mesh_pos()
        me = _blk(x, y, c)
        sibling = (x, y, 1 - c)
        chips = [(1 - x, y), (x, 1 - y), (1 - x, 1 - y)]

        def copy(k, b, to):
            return pltpu.make_async_remote_copy(src_ref=wt_ref.at[b], dst_ref=wt_ref.at[b], send_sem=send_sems.at[k],
                                                recv_sem=recv_sems.at[k], device_id=to, device_id_type=MESH)

        wt_ref[me] = _bf(win_ref[...])
        first = [copy(0, me, sibling)] + [copy(1 + j, me, (*chip, c)) for j, chip in enumerate(chips)]
        for cp in first:
            cp.start()

        nw = nw_ref[...]
        for i in range(T // TM):
            rows = slice(i * TM, (i + 1) * TM)
            xv = x_ref[rows, :]
            r = lax.rsqrt(jnp.mean(xv * xv, axis=-1, keepdims=True) + NORM_EPS)
            h_ref[rows, :] = _bf(xv * r * nw)
            rstd_ref[rows, :] = r

        writes = []

        def project(b):
            k = len(writes)
            if k >= 2:
                writes[k - 2].wait()
            w = wt_ref[b]
            for t in range(T // tchunk):
                cols = slice(t * tchunk, (t + 1) * tchunk)
                stage[k % 2, :, cols] = _dot_nt(w, h_ref[cols, :])
            cp = pltpu.make_async_copy(stage.at[k % 2], proj_ref.at[b], out_sems.at[k % 2])
            cp.start()
            writes.append(cp)

        project(me)
        copy(0, _blk(x, y, 1 - c), (x, y, c)).wait_recv()
        project(_blk(x, y, 1 - c))
        passed = []
        for j, chip in enumerate(chips):
            copy(1 + j, _blk(*chip, c), (x, y, c)).wait_recv()
            fwd = copy(4 + j, _blk(*chip, c), sibling)
            fwd.start()
            passed.append(fwd)
            project(_blk(*chip, c))
            copy(4 + j, _blk(*chip, 1 - c), (x, y, c)).wait_recv()
            project(_blk(*chip, 1 - c))
        writes[-2].wait()
        writes[-1].wait()
        for cp in first + passed:
            cp.wait_send()

    vm = pl.BlockSpec(memory_space=pltpu.VMEM)
    proj, wt, rstd, h = pl.pallas_call(
        body, name="in_proj_gather",
        out_shape=(jax.ShapeDtypeStruct((N_DEV, WIN_BLK, T), F32), jax.ShapeDtypeStruct((N_DEV, WIN_BLK, D), BF16),
                   jax.ShapeDtypeStruct((T, 1), F32), jax.ShapeDtypeStruct((T, D), BF16)),
        in_specs=[vm, vm, vm], out_specs=(pl.BlockSpec(memory_space=pl.ANY), vm, vm, vm),
        scratch_shapes=[pltpu.VMEM((2, WIN_BLK, T), F32),
                        pltpu.SemaphoreType.DMA((7,)), pltpu.SemaphoreType.DMA((7,)), pltpu.SemaphoreType.DMA((2,))],
        compiler_params=_cparams(vmem=48 * 1024 * 1024),
    )(x, norm_w, w_in_t)
    return proj.reshape(D_IN, T), wt.reshape(D_IN, D), rstd, h


def _gather_w_out(w_out_ref, own_ref, wo_ref, local_sem, send_sems, recv_sems, phase):
    x, y, c = _mesh_pos()
    me = _blk(x, y, c)
    sibling = (x, y, 1 - c)
    chips = [(1 - x, y), (x, 1 - y), (1 - x, 1 - y)]

    def copy(k, b, to, src=None):
        return pltpu.make_async_remote_copy(src_ref=wo_ref.at[b] if src is None else src, dst_ref=wo_ref.at[b],
                                            send_sem=send_sems.at[k], recv_sem=recv_sems.at[k], device_id=to,
                                            device_id_type=MESH)

    local = pltpu.make_async_copy(own_ref, wo_ref.at[me], local_sem)
    first = [copy(0, me, sibling, own_ref)] + [copy(1 + j, me, (*chip, c), own_ref) for j, chip in enumerate(chips)]
    passed = [copy(4 + j, _blk(*chip, c), sibling) for j, chip in enumerate(chips)]
    if phase == 0:
        own_ref[...] = _bf(w_out_ref[...])
        local.start()
        for cp in first:
            cp.start()
    elif phase == 1:
        for j, chip in enumerate(chips):
            copy(1 + j, _blk(*chip, c), (x, y, c)).wait_recv()
            passed[j].start()
    else:
        local.wait()
        copy(0, _blk(x, y, 1 - c), (x, y, c)).wait_recv()
        for j, chip in enumerate(chips):
            copy(4 + j, _blk(*chip, 1 - c), (x, y, c)).wait_recv()
        for cp in first + passed:
            cp.wait_send()


def _proj_piece(i, block_of):
    rows = PIECES[i]
    assert OFFS[i] % rows == 0
    return pl.BlockSpec((rows, CH), lambda n: (OFFS[i] // rows, block_of(n)))


def _proj_gate_halves(block_of):
    return [pl.BlockSpec((256, CH), lambda n, j=j: (OFFS[7] // 256 + j, block_of(n))) for j in range(2)]


def _swap_halves_t(t):
    half = RET_DK // 2
    parts = []
    for h in range(RET_H):
        parts += [t[h * RET_DK + half:(h + 1) * RET_DK], t[h * RET_DK:h * RET_DK + half]]
    return jnp.concatenate(parts, axis=0)


def _rotate(t, cos, sin):
    return t * cos + _swap_halves_t(t) * sin


def _group_norm_t(o):
    mu = jnp.mean(o, axis=0, keepdims=True)
    var = jnp.mean((o - mu) * (o - mu), axis=0, keepdims=True)
    rstd = lax.rsqrt(var + GN_EPS)
    return (o - mu) * rstd, rstd


def _retention_scores_t(q_t, k_rm_b, q_mask, decay_all):
    q_heads = _bf(jnp.concatenate([q_t] * RET_H, axis=1) * q_mask)
    return _dot(k_rm_b, q_heads) * decay_all


def _retention_fwd(proj, ret_norm_w, tables, w_out_blk):
    cos_t, sin_t, decay_all, zeta_tab, xi_tab, state_mask, state_decay, q_mask = tables

    def body(rq_ref, rk_ref, rv_ref, rg_ref, w_ref, cos_ref, sin_ref, decay_ref, zeta_ref, xi_ref, smask_ref,
             sdecay_ref, qmask_ref, wout_ref, ro_ref, mix_ref, st_ref, wo_ref, state, wo_own, wo_local, wo_send, wo_recv):
        n = pl.program_id(0)

        @pl.when(n == 0)
        def _():
            state[...] = jnp.zeros_like(state)
            _gather_w_out(wout_ref, wo_own, wo_ref, wo_local, wo_send, wo_recv, phase=0)

        @pl.when(n == N_CH // 2)
        def _():
            _gather_w_out(wout_ref, wo_own, wo_ref, wo_local, wo_send, wo_recv, phase=1)

        @pl.when(n == N_CH - 1)
        def _():
            _gather_w_out(wout_ref, wo_own, wo_ref, wo_local, wo_send, wo_recv, phase=2)

        cos, sin = cos_ref[...], sin_ref[...]
        q_t = _rotate(rq_ref[...], cos, sin)
        k_t = _rotate(rk_ref[...], cos, sin) * (RET_DK ** -0.5)
        k_rm = k_t.T
        v_b = _bf(rv_ref[...])
        m_b = _bf(state[...])
        st_ref[0] = m_b
        scores_b = _bf(_retention_scores_t(q_t, _bf(k_rm), qmask_ref[...], decay_ref[...]))
        cross = _dot(m_b, _bf(q_t * xi_ref[...]))
        state[...] = state[...] * sdecay_ref[...] + _dot(v_b, _bf(k_rm * zeta_ref[...])) * smask_ref[...]
        heads = range(RET_H)
        rows = [slice(h * RET_DV, (h + 1) * RET_DV) for h in heads]
        o = [_dot(v_b[rows[h]], scores_b[:, h * CH:(h + 1) * CH]) + cross[rows[h]] for h in heads]
        for h in heads:
            rn, _ = _group_norm_t(o[h])
            g = rg_ref[rows[h], :]
            ro_ref[rows[h], :] = o[h]
            mix_ref[rows[h], :] = rn * w_ref[rows[h], :] * (g * _sigmoid(g))

    col = lambda w: pl.BlockSpec((w, CH), lambda n: (0, n))
    const = lambda shape: pl.BlockSpec(shape, lambda n: (0,) * len(shape))
    cur = lambda n: n
    ro, mix, states, wo = pl.pallas_call(
        body, name="retention_fwd", grid=(N_CH,),
        in_specs=[_proj_piece(i, cur) for i in range(4)] + [
            const((RET_V, 1)), col(RET_QK), col(RET_QK),
            const(decay_all.shape), const(zeta_tab.shape), const(xi_tab.shape), const(state_mask.shape),
            const(state_decay.shape), const(q_mask.shape), const((WOUT_BLK, D))],
        out_specs=(col(RET_V), col(RET_V), pl.BlockSpec((1, RET_V, RET_QK), lambda n: (n, 0, 0)),
                   pl.BlockSpec(memory_space=pl.ANY)),
        out_shape=(jax.ShapeDtypeStruct((RET_V, T), F32), jax.ShapeDtypeStruct((RET_V, T), F32),
                   jax.ShapeDtypeStruct((N_CH, RET_V, RET_QK), BF16), jax.ShapeDtypeStruct((N_DEV, WOUT_BLK, D), BF16)),
        scratch_shapes=[pltpu.VMEM((RET_V, RET_QK), F32), pltpu.VMEM((WOUT_BLK, D), BF16), pltpu.SemaphoreType.DMA,
                        pltpu.SemaphoreType.DMA((N_DEV - 1,)), pltpu.SemaphoreType.DMA((N_DEV - 1,))],
        compiler_params=_cparams(1),
    )(proj, proj, proj, proj, ret_norm_w, cos_t, sin_t, decay_all, zeta_tab, xi_tab, state_mask, state_decay, q_mask,
      w_out_blk)
    return ro, mix, states, wo.reshape(D, D)


def _retention_bwd(proj, ro, states, dmix, ret_norm_w, tables):
    cos_t, sin_t, decay_all, zeta_tab, xi_tab, state_mask, state_decay, q_mask = tables

    def body(rq_ref, rk_ref, rv_ref, rg_ref, ro_ref, st_ref, dm_ref, w_ref, cos_ref, sin_ref, decay_ref, zeta_ref,
             xi_ref, smask_ref, sdecay_ref, qmask_ref, d_ref, dw_ref, gstate):
        i = pl.program_id(0)

        @pl.when(i == 0)
        def _():
            gstate[...] = jnp.zeros_like(gstate)
            dw_ref[...] = jnp.zeros_like(dw_ref)

        cos, sin = cos_ref[...], sin_ref[...]
        q_t = _rotate(rq_ref[...], cos, sin)
        k_t = _rotate(rk_ref[...], cos, sin) * (RET_DK ** -0.5)
        q_b, k_b = _bf(q_t), _bf(k_t)
        k_rm = k_t.T
        kz_b = _bf(k_rm * zeta_ref[...])
        qx_b = _bf(q_t * xi_ref[...])
        v_t = rv_ref[...]
        v_b = _bf(v_t)
        v_rm_b = _bf(v_t.T)
        decay = decay_ref[...]
        scores_b = _bf(_retention_scores_t(q_t, _bf(k_rm), qmask_ref[...], decay))
        heads = range(RET_H)
        rows = [slice(h * RET_DV, (h + 1) * RET_DV) for h in heads]
        qk_rows = [slice(h * RET_DK, (h + 1) * RET_DK) for h in heads]
        lanes = [slice(h * CH, (h + 1) * CH) for h in heads]
        do = []
        for h in heads:
            g, w, dm = rg_ref[rows[h], :], w_ref[rows[h], :], dm_ref[rows[h], :]
            rn, rstd = _group_norm_t(ro_ref[rows[h], :])
            sig = _sigmoid(g)
            silu = g * sig
            d_ref[OFFS[3] + h * RET_DV:OFFS[3] + (h + 1) * RET_DV, :] = _bf(dm * rn * w * (sig * (1.0 + g * (1.0 - sig))))
            dw_ref[rows[h], :] += dm * silu * rn
            drn = dm * silu * w
            do.append(rstd * (drn - jnp.mean(drn, axis=0, keepdims=True)
                              - rn * jnp.mean(drn * rn, axis=0, keepdims=True)))
        do_b = _bf(jnp.concatenate(do, axis=0))
        m_b = st_ref[0]
        g_all = gstate[...]
        g_b = _bf(g_all)
        dscores_b = [_bf(_dot(v_rm_b[:, rows[h]], do_b[rows[h]]) * decay[:, lanes[h]]) for h in heads]
        dq_cross = lax.dot_general(m_b, do_b, (((0,), (0,)), ((), ())), preferred_element_type=F32)
        dkz = _dot(v_rm_b, g_b)
        dv_cross = _dot_nt(g_b, kz_b)
        gstate[...] = g_all * sdecay_ref[...] + _dot_nt(do_b, qx_b) * smask_ref[...]
        dq = jnp.concatenate([_dot(k_b[qk_rows[h]], dscores_b[h]) for h in heads], axis=0) + dq_cross * xi_ref[...]
        dk = (jnp.concatenate([_dot_nt(q_b[qk_rows[h]], dscores_b[h]) for h in heads], axis=0)
              + (dkz * zeta_ref[...]).T) * (RET_DK ** -0.5)
        for h in heads:
            d_ref[OFFS[2] + h * RET_DV:OFFS[2] + (h + 1) * RET_DV, :] = _bf(
                _dot_nt(do_b[rows[h]], scores_b[:, lanes[h]]) + dv_cross[rows[h]])
        d_ref[OFFS[0]:OFFS[1], :] = _bf(dq * cos + _swap_halves_t(dq * sin))
        d_ref[OFFS[1]:OFFS[2], :] = _bf(dk * cos + _swap_halves_t(dk * sin))

    col = lambda w: pl.BlockSpec((w, CH), lambda i: (0, N_CH - 1 - i))
    const = lambda shape: pl.BlockSpec(shape, lambda i: (0,) * len(shape))
    return pl.pallas_call(
        body, name="retention_bwd", grid=(N_CH,),
        in_specs=[_proj_piece(j, lambda i: N_CH - 1 - i) for j in range(4)] + [
                  col(RET_V),
                  pl.BlockSpec((1, RET_V, RET_QK), lambda i: (N_CH - 1 - i, 0, 0)), col(RET_V),
                  const((RET_V, 1)), col(RET_QK), col(RET_QK),
                  const(decay_all.shape), const(zeta_tab.shape), const(xi_tab.shape), const(state_mask.shape),
                  const(state_decay.shape), const(q_mask.shape)],
        out_specs=(col(RET_W), const((RET_V, CH))),
        out_shape=(jax.ShapeDtypeStruct((RET_W, T), BF16), jax.ShapeDtypeStruct((RET_V, CH), F32)),
        scratch_shapes=[pltpu.VMEM((RET_V, RET_QK), F32)],
        compiler_params=_cparams(1),
    )(proj, proj, proj, proj, ro, states, dmix, ret_norm_w, cos_t, sin_t, decay_all, zeta_tab, xi_tab, state_mask,
      state_decay, q_mask)


HQ_LANES = SWA_H * CH


def _head_lanes(hq):
    return slice(hq * CH, (hq + 1) * CH)


def _build_bias_t(bucket_ref, relb_ref, sinks_ref, bias_ref, sink_row):
    bt = bucket_ref[...]
    first = lax.broadcasted_iota(jnp.int32, bt.shape, 0) < CH
    for hq in range(SWA_H):
        b = jnp.full(bt.shape, NEG_INF, F32)
        for bk in range(N_BUCKETS):
            b = jnp.where(bt == bk, relb_ref[hq * N_BUCKETS + bk], b)
        bias_ref[0, :, _head_lanes(hq)] = b
        bias_ref[1, :, _head_lanes(hq)] = jnp.where(first, NEG_INF, b)
        sink_row[:, _head_lanes(hq)] = jnp.full((1, CH), sinks_ref[hq], F32)


def _rms_t(t, w_col):
    r = lax.rsqrt(jnp.mean(t * t, axis=0, keepdims=True) + NORM_EPS)
    return t * r * w_col, r


def _rms_t_bwd(dn, t, r, w_col):
    u = dn * w_col
    return r * u - t * (r * r * r) * jnp.mean(u * t, axis=0, keepdims=True), dn * t * r


def _norm_kv(k_t, kw):
    return jnp.concatenate([_rms_t(k_t[g * SWA_D:(g + 1) * SWA_D], kw)[0] for g in range(SWA_KV)], axis=0)


def _kv_slot(a, kv):
    z = jnp.zeros_like(a)
    return jnp.concatenate([a, z] if kv == 0 else [z, a], axis=0)


def _softmax_t(s, sink):
    m = jnp.maximum(jnp.max(s, axis=0, keepdims=True), sink)
    p = jnp.exp(s - m)
    e_sink = jnp.exp(sink - m)
    inv = 1.0 / (jnp.sum(p, axis=0, keepdims=True) + e_sink)
    return p * inv, e_sink * inv


def _gate_rows(sg_halves, hq):
    per_half = SWA_H // 2
    return sg_halves[hq // per_half][(hq % per_half) * SWA_D:(hq % per_half + 1) * SWA_D, :]


def _swa_fwd(proj, q_norm_w, k_norm_w, sinks, rel_bias_flat, bucket_t):
    def body(sq_ref, skp_ref, skc_ref, svp_ref, svc_ref, sg_lo, sg_hi, qw_ref, kw_ref, sinks_ref, relb_ref, bucket_ref,
             so_ref, mix_ref, bias_ref, sink_row):
        n = pl.program_id(0)

        @pl.when(n == 0)
        def _():
            _build_bias_t(bucket_ref, relb_ref, sinks_ref, bias_ref, sink_row)

        var = (n == 0).astype(jnp.int32)
        qw, kw = qw_ref[...], kw_ref[...]
        kn_band = jnp.concatenate([_norm_kv(skp_ref[...], kw), _norm_kv(skc_ref[...], kw)], axis=1)
        kb_rm = _bf(kn_band.T)
        vband = _bf(jnp.concatenate([svp_ref[...], svc_ref[...]], axis=1))
        q_all = jnp.concatenate(
            [_bf(_kv_slot(_rms_t(sq_ref[hq * SWA_D:(hq + 1) * SWA_D, :], qw)[0] * (SWA_D ** -0.5), hq // SWA_G))
             for hq in range(SWA_H)], axis=1)
        probs, _ = _softmax_t(_dot(kb_rm, q_all) + bias_ref[var], sink_row[...])
        probs_b = _bf(probs)
        for kv in range(SWA_KV):
            o = _dot(vband[kv * SWA_D:(kv + 1) * SWA_D], probs_b[:, kv * SWA_G * CH:(kv + 1) * SWA_G * CH])
            for j in range(SWA_G):
                rows = slice((kv * SWA_G + j) * SWA_D, (kv * SWA_G + j + 1) * SWA_D)
                oh = o[:, j * CH:(j + 1) * CH]
                g = _gate_rows((sg_lo, sg_hi), kv * SWA_G + j)
                so_ref[rows, :] = oh
                mix_ref[rows, :] = oh * (g * _sigmoid(g))

    col = lambda w: pl.BlockSpec((w, CH), lambda n: (0, n))
    const = lambda shape: pl.BlockSpec(shape, lambda n: (0,) * len(shape))
    smem = pl.BlockSpec(memory_space=pltpu.SMEM)
    cur = lambda n: n
    prev = lambda n: jnp.maximum(n - 1, 0)
    return pl.pallas_call(
        body, name="swa_fwd", grid=(N_CH,),
        in_specs=[_proj_piece(4, cur), _proj_piece(5, prev), _proj_piece(5, cur), _proj_piece(6, prev),
                  _proj_piece(6, cur)] + _proj_gate_halves(cur) + [
                  const((SWA_D, 1)), const((SWA_D, 1)), smem, smem, const((2 * CH, CH))],
        out_specs=(col(512), col(512)),
        out_shape=(jax.ShapeDtypeStruct((512, T), F32), jax.ShapeDtypeStruct((512, T), F32)),
        scratch_shapes=[pltpu.VMEM((2, 2 * CH, HQ_LANES), F32), pltpu.VMEM((1, HQ_LANES), F32)],
        compiler_params=_cparams(1),
    )(proj, proj, proj, proj, proj, proj, proj, q_norm_w, k_norm_w, sinks, rel_bias_flat, bucket_t)


def _swa_bwd(proj, so, dmix, q_norm_w, k_norm_w, sinks, rel_bias_flat, bucket_t):
    def body(sq_ref, skp_ref, skc_ref, svp_ref, svc_ref, sg_lo, sg_hi, so_ref, dm_ref, qw_ref, kw_ref, sinks_ref,
             relb_ref, bucket_ref, d_ref, dbias_ref, dsink_ref, dqw_ref, dkw_ref,
             bias_ref, sink_row, band_dk, band_dv, carry_dk, carry_dv, hold_dq, hold_dg):
        n = pl.program_id(0)
        slot = n % 2

        @pl.when(n == 0)
        def _():
            _build_bias_t(bucket_ref, relb_ref, sinks_ref, bias_ref, sink_row)
            for ref in (dbias_ref, dsink_ref, dqw_ref, dkw_ref, carry_dk, carry_dv):
                ref[...] = jnp.zeros_like(ref)

        qw, kw = qw_ref[...], kw_ref[...]

        @pl.when(n < N_CH)
        def _():
            var = (n == 0).astype(jnp.int32)
            kn_band = jnp.concatenate([_norm_kv(skp_ref[...], kw), _norm_kv(skc_ref[...], kw)], axis=1)
            kb_rm = _bf(kn_band.T)
            kn_band_s = _bf(kn_band * (SWA_D ** -0.5))
            vband_f = jnp.concatenate([svp_ref[...], svc_ref[...]], axis=1)
            vb_rm = _bf(vband_f.T)
            q_raw, q_rstd, qs_b, do_b = [], [], [], []
            for hq in range(SWA_H):
                rows = slice(hq * SWA_D, (hq + 1) * SWA_D)
                q_t = sq_ref[rows, :]
                qn, rq = _rms_t(q_t, qw)
                g = _gate_rows((sg_lo, sg_hi), hq)
                sig = _sigmoid(g)
                dm = dm_ref[rows, :]
                hold_dg[slot, rows, :] = _bf(dm * so_ref[rows, :] * (sig * (1.0 + g * (1.0 - sig))))
                q_raw.append(q_t)
                q_rstd.append(rq)
                qs_b.append(_bf(qn * (SWA_D ** -0.5)))
                do_b.append(_bf(dm * (g * sig)))
            q_all = jnp.concatenate([_kv_slot(qs_b[hq], hq // SWA_G) for hq in range(SWA_H)], axis=1)
            do_all = jnp.concatenate([_kv_slot(do_b[hq], hq // SWA_G) for hq in range(SWA_H)], axis=1)
            probs, p_sink = _softmax_t(_dot(kb_rm, q_all) + bias_ref[var], sink_row[...])
            dprobs = _dot(vb_rm, do_all)
            t = jnp.sum(probs * dprobs, axis=0, keepdims=True)
            dlog = probs * (dprobs - t)
            dsink_ref[...] += -(p_sink * t)
            dbias_ref[...] += dlog
            dlog_b, probs_b = _bf(dlog), _bf(probs)
            dkn, dvv = [], []
            for kv in range(SWA_KV):
                heads = range(kv * SWA_G, (kv + 1) * SWA_G)
                lanes = slice(kv * SWA_G * CH, (kv + 1) * SWA_G * CH)
                dvv.append(_dot_nt(jnp.concatenate([do_b[hq] for hq in heads], axis=1), probs_b[:, lanes]))
                dkn.append(_dot_nt(jnp.concatenate([qs_b[hq] for hq in heads], axis=1), dlog_b[:, lanes]))
                dqn = _dot(kn_band_s[kv * SWA_D:(kv + 1) * SWA_D], dlog_b[:, lanes])
                for j, hq in enumerate(heads):
                    dq_t, dqw_terms = _rms_t_bwd(dqn[:, j * CH:(j + 1) * CH], q_raw[hq], q_rstd[hq], qw)
                    hold_dq[slot, hq * SWA_D:(hq + 1) * SWA_D, :] = _bf(dq_t)
                    dqw_ref[...] += dqw_terms
            band_dk[...] = jnp.concatenate(dkn, axis=0)
            band_dv[...] = jnp.concatenate(dvv, axis=0)

        @pl.when(n == N_CH)
        def _():
            band_dk[...] = jnp.zeros_like(band_dk)
            band_dv[...] = jnp.zeros_like(band_dv)

        @pl.when(n >= 1)
        def _():
            dkn_prev = carry_dk[...] + band_dk[:, 0:CH]
            k_t = skp_ref[...]
            for kv in range(SWA_KV):
                rows = slice(kv * SWA_D, (kv + 1) * SWA_D)
                _, rk = _rms_t(k_t[rows], kw)
                dk_t, dkw_terms = _rms_t_bwd(dkn_prev[rows], k_t[rows], rk, kw)
                d_ref[SWA_OFFS[1] + kv * SWA_D:SWA_OFFS[1] + (kv + 1) * SWA_D, :] = _bf(dk_t)
                dkw_ref[...] += dkw_terms
            d_ref[SWA_OFFS[2]:SWA_OFFS[3], :] = _bf(carry_dv[...] + band_dv[:, 0:CH])
            d_ref[SWA_OFFS[0]:SWA_OFFS[1], :] = hold_dq[1 - slot]
            d_ref[SWA_OFFS[3]:SWA_OFFS[4], :] = hold_dg[1 - slot]

        carry_dk[...] = band_dk[:, CH:2 * CH]
        carry_dv[...] = band_dv[:, CH:2 * CH]

    cur_block = lambda n: jnp.minimum(n, N_CH - 1)
    prev_block = lambda n: jnp.maximum(n - 1, 0)
    col = lambda w: pl.BlockSpec((w, CH), lambda n: (0, cur_block(n)))
    prev = lambda w: pl.BlockSpec((w, CH), lambda n: (0, prev_block(n)))
    const = lambda shape: pl.BlockSpec(shape, lambda n: (0,) * len(shape))
    smem = pl.BlockSpec(memory_space=pltpu.SMEM)
    return pl.pallas_call(
        body, name="swa_bwd", grid=(N_CH + 1,),
        in_specs=[_proj_piece(4, cur_block), _proj_piece(5, prev_block), _proj_piece(5, cur_block),
                  _proj_piece(6, prev_block), _proj_piece(6, cur_block)] + _proj_gate_halves(cur_block) + [
                  col(512), col(512), const((SWA_D, 1)), const((SWA_D, 1)), smem, smem, const((2 * CH, CH))],
        out_specs=(prev(SWA_W), const((2 * CH, HQ_LANES)), const((1, HQ_LANES)),
                   const((SWA_D, CH)), const((SWA_D, CH))),
        out_shape=(jax.ShapeDtypeStruct((SWA_W, T), BF16),
                   jax.ShapeDtypeStruct((2 * CH, HQ_LANES), F32), jax.ShapeDtypeStruct((1, HQ_LANES), F32),
                   jax.ShapeDtypeStruct((SWA_D, CH), F32), jax.ShapeDtypeStruct((SWA_D, CH), F32)),
        scratch_shapes=[pltpu.VMEM((2, 2 * CH, HQ_LANES), F32), pltpu.VMEM((1, HQ_LANES), F32),
                        pltpu.VMEM((128, 2 * CH), F32), pltpu.VMEM((128, 2 * CH), F32),
                        pltpu.VMEM((128, CH), F32), pltpu.VMEM((128, CH), F32),
                        pltpu.VMEM((2, 512, CH), BF16), pltpu.VMEM((2, 512, CH), BF16)],
        compiler_params=_cparams(1),
    )(proj, proj, proj, proj, proj, proj, proj, so, dmix, q_norm_w, k_norm_w, sinks, rel_bias_flat, bucket_t)


def _small_grads(dbias, bucket_t, dsink_acc, dqw_acc, dkw_acc, drw_acc):
    def body(dbias_ref, bucket_ref, dsink_ref, dqw_ref, dkw_ref, drw_ref, relb_o, sink_o, qw_o, kw_o, rw_o):
        bt = bucket_ref[...]
        row = lax.broadcasted_iota(jnp.int32, (SWA_H, N_BUCKETS), 0)
        col = lax.broadcasted_iota(jnp.int32, (SWA_H, N_BUCKETS), 1)
        lane = lax.broadcasted_iota(jnp.int32, (1, CH), 1)
        acc = jnp.zeros((SWA_H, N_BUCKETS), F32)
        sink = jnp.zeros((1, CH), F32)
        for hq in range(SWA_H):
            d = dbias_ref[:, _head_lanes(hq)]
            for bk in range(N_BUCKETS):
                s = jnp.sum(jnp.sum(jnp.where(bt == bk, d, 0.0), axis=0, keepdims=True), axis=1, keepdims=True)
                acc = acc + jnp.where((row == hq) & (col == bk), s, 0.0)
            sink = sink + jnp.where(lane == hq, jnp.sum(dsink_ref[:, _head_lanes(hq)], axis=1, keepdims=True), 0.0)
        relb_o[...] = acc
        sink_o[...] = sink
        for src, dst in ((dqw_ref, qw_o), (dkw_ref, kw_o)):
            padded = jnp.concatenate([src[...], jnp.zeros((CH - SWA_D, CH), F32)], axis=0)
            dst[...] = jnp.sum(padded.T, axis=0, keepdims=True)
        rw_o[...] = jnp.sum(drw_ref[...].T, axis=0, keepdims=True)

    vm = pl.BlockSpec(memory_space=pltpu.VMEM)
    row128 = jax.ShapeDtypeStruct((1, CH), F32)
    return pl.pallas_call(body, name="small_grads",
                          out_shape=(jax.ShapeDtypeStruct((SWA_H, N_BUCKETS), F32), row128, row128, row128,
                                     jax.ShapeDtypeStruct((1, RET_V), F32)),
                          in_specs=[vm] * 6, out_specs=(vm,) * 5, compiler_params=_cparams(),
                          )(dbias, bucket_t, dsink_acc, dqw_acc, dkw_acc, drw_acc)


def _out_proj(mix_r, mix_s, wo, x, target):
    def body(mr_ref, ms_ref, w_ref, x_ref, t_ref, loss_ref, dy_ref, dmr_ref, dms_ref, gw_ref):
        i = pl.program_id(0)

        @pl.when(i == 0)
        def _():
            loss_ref[...] = jnp.zeros_like(loss_ref)
            gw_ref[...] = jnp.zeros_like(gw_ref)

        mixed = jnp.concatenate([mr_ref[...], ms_ref[...]], axis=0)
        w = w_ref[...]
        err = x_ref[...] + _dot(_bf(mixed.T), w) - t_ref[...]
        loss_ref[...] += jnp.sum(jnp.sum(err * err, axis=1, keepdims=True), axis=0, keepdims=True)
        dy = err * (1.0 / D)
        dy_ref[...] = dy
        dy_b = _bf(dy)
        dmix = _dot_nt(w, dy_b)
        dmr_ref[...] = dmix[0:512]
        dms_ref[...] = dmix[512:D]
        gw_ref[...] += _dot(_bf(mixed), dy_b)

    row = lambda w: pl.BlockSpec((TM, w), lambda i: (i, 0))
    col = lambda w: pl.BlockSpec((w, TM), lambda i: (0, i))
    const = lambda shape: pl.BlockSpec(shape, lambda i: (0,) * len(shape))
    return pl.pallas_call(
        body, name="out_proj", grid=(T // TM,),
        in_specs=[col(512), col(512), const((D, D)), row(D), row(D)],
        out_specs=(const((1, 1)), row(D), col(512), col(512), const((D, D))),
        out_shape=(jax.ShapeDtypeStruct((1, 1), F32), jax.ShapeDtypeStruct((T, D), F32),
                   jax.ShapeDtypeStruct((512, T), F32), jax.ShapeDtypeStruct((512, T), F32),
                   jax.ShapeDtypeStruct((D, D), F32)),
        compiler_params=_cparams(1),
    )(mix_r, mix_s, wo, x, target)


LOSS_ROW = 5
ALL_CHIPS = ((0, 0), (0, 1), (1, 0), (1, 1))


def _in_proj_bwd_rs(d_ret, d_swa, h, wt, x, rstd, norm_w, dy, gwo, sse, small_g):
    n_blocks, n_tiles = N_DEV, T // TM
    last = n_blocks + n_tiles - 1

    def body(da_any, db_any, da_ref, db_ref, h_ref, w_ref, x_ref, r_ref, nw_ref, dy_ref, gwo_ref, sse_ref,
             rw_ref, qw_ref, kw_ref, sk_ref, rb_ref, gx_ref, gin_o, gout_o, suma_o, sumb_o,
             stage, d2d_src, rcv, snd, got, tot, own_o, rcv_o, snd_o, got_o, tot_o, tab_a, tab_b, gnw_acc,
             dp_sems, d2d_send, d2d_recv, ici_send, ici_recv, o_local, oa_send, oa_recv, ob_send, ob_recv,
             s_send, s_recv):
        k = pl.program_id(0)
        x_, y_, c_ = _mesh_pos()
        me = _blk(x_, y_, c_)
        sibling = (x_, y_, 1 - c_)
        rel_chips = [(1 - x_, y_), (x_, 1 - y_), (1 - x_, 1 - y_), (x_, y_)]

        def block_of_step(s):
            return _blk(*rel_chips[s // 2], 1 - c_ if s % 2 == 0 else c_)

        def fetch(s, wait):
            slot, b = s % 2, block_of_step(s)
            split = RET_W - 4 * WIN_BLK

            def run(src, dst, sem):
                cp = pltpu.make_async_copy(src, dst, sem)
                cp.wait() if wait else cp.start()

            @pl.when(b < 4)
            def _():
                run(da_any.at[pl.ds(pl.multiple_of(b * WIN_BLK, 16), WIN_BLK), :], stage.at[slot], dp_sems.at[slot, 0])

            @pl.when(b == 4)
            def _():
                run(da_any.at[pl.ds(4 * WIN_BLK, split), :], stage.at[slot, pl.ds(0, split), :], dp_sems.at[slot, 0])
                run(db_any.at[pl.ds(0, WIN_BLK - split), :], stage.at[slot, pl.ds(split, WIN_BLK - split), :],
                    dp_sems.at[slot, 1])

            @pl.when(b > 4)
            def _():
                run(db_any.at[pl.ds(pl.multiple_of(b * WIN_BLK - RET_W, 16), WIN_BLK), :], stage.at[slot],
                    dp_sems.at[slot, 0])

        def d2d_copy(r):
            return pltpu.make_async_remote_copy(src_ref=d2d_src, dst_ref=rcv.at[r], send_sem=d2d_send.at[r],
                                                recv_sem=d2d_recv.at[r], device_id=sibling, device_id_type=MESH)

        def ici_copy(r, src, dst, send, recv):
            return pltpu.make_async_remote_copy(src_ref=src.at[r], dst_ref=dst.at[r], send_sem=send.at[r],
                                                recv_sem=recv.at[r], device_id=(*rel_chips[r], c_), device_id_type=MESH)

        def oa_copy(j):
            px, py = ALL_CHIPS[j]
            return pltpu.make_async_remote_copy(src_ref=gwo_ref.at[_blk(px, py, 1 - c_)], dst_ref=rcv_o.at[j],
                                                send_sem=oa_send.at[j], recv_sem=oa_recv.at[j], device_id=sibling,
                                                device_id_type=MESH)

        def o_local_copy(j):
            px, py = ALL_CHIPS[j]
            return pltpu.make_async_copy(gwo_ref.at[_blk(px, py, c_)], own_o.at[j], o_local.at[j])

        def table_copies():
            return [pltpu.make_async_remote_copy(src_ref=tab.at[me], dst_ref=tab.at[me], send_sem=s_send.at[a, p - 1],
                                                 recv_sem=s_recv.at[a, p - 1],
                                                 device_id=(x_ ^ (p >> 2), y_ ^ ((p >> 1) & 1), c_ ^ (p & 1)),
                                                 device_id_type=MESH)
                    for p in range(1, N_DEV) for a, tab in enumerate((tab_a, tab_b))]

        for s in range(n_blocks):
            @pl.when(k == s)
            def _(s=s):
                r = s // 2
                if s == 0:
                    gnw_acc[...] = jnp.zeros_like(gnw_acc)
                    fetch(0, wait=False)
                    for j in range(len(ALL_CHIPS)):
                        oa_copy(j).start()
                        o_local_copy(j).start()
                fetch(s, wait=True)
                if s + 1 < n_blocks:
                    fetch(s + 1, wait=False)
                gw = _dot(stage[s % 2], h_ref[...])
                if s % 2 == 0:
                    if s >= 2:
                        d2d_copy(r - 1).wait_send()
                    d2d_src[...] = gw
                    d2d_copy(r).start()
                else:
                    d2d_copy(r).wait_recv()
                    chip_sum = gw + rcv[r]
                    if r < 3:
                        snd[r] = _bf(chip_sum)
                        ici_copy(r, snd, got, ici_send, ici_recv).start()
                    else:
                        tot[...] = chip_sum
                if s == 2:
                    for j in range(len(ALL_CHIPS)):
                        o_local_copy(j).wait()
                        oa_copy(j).wait_recv()
                    for q in range(3):
                        j = 2 * rel_chips[q][0] + rel_chips[q][1]
                        snd_o[q] = _bf(own_o[j] + rcv_o[j])
                        ici_copy(q, snd_o, got_o, ob_send, ob_recv).start()
                    jm = 2 * x_ + y_
                    tot_o[...] = own_o[jm] + rcv_o[jm]

        @pl.when(k >= n_blocks)
        def _():
            dp = jnp.concatenate([da_ref[...], db_ref[...]], axis=0)
            xv, r, nw = x_ref[...], r_ref[...], nw_ref[...]
            dh = lax.dot_general(dp, w_ref[...], (((0,), (0,)), ((), ())), preferred_element_type=F32)
            u = dh * nw
            gx_ref[...] = dy_ref[...] + r * u - xv * (r * r * r) * jnp.mean(u * xv, axis=-1, keepdims=True)
            gnw_acc[...] += jnp.sum(dh * (xv * r), axis=0, keepdims=True)

        @pl.when(k == last)
        def _():
            mine_a, mine_b = tab_a.at[me], tab_b.at[me]
            mine_a[...] = jnp.zeros_like(mine_a)
            mine_b[...] = jnp.zeros_like(mine_b)
            for row, ref in enumerate((gnw_acc, rw_ref, qw_ref, kw_ref, sk_ref)):
                mine_a[row:row + 1, 0:ref.shape[1]] = ref[...]
            mine_a[LOSS_ROW:LOSS_ROW + 1, 0:1] = sse_ref[...]
            mine_b[:, 0:N_BUCKETS] = rb_ref[...]
            tables = table_copies()
            for cp in tables:
                cp.start()
            g_in, g_out = tot[...], tot_o[...]
            for q in range(3):
                ici_copy(q, snd, got, ici_send, ici_recv).wait_recv()
                ici_copy(q, snd_o, got_o, ob_send, ob_recv).wait_recv()
                g_in = g_in + got[q].astype(F32)
                g_out = g_out + got_o[q].astype(F32)
            gin_o[...] = g_in
            gout_o[...] = g_out
            for cp in tables:
                cp.wait_recv()
            sum_a, sum_b = tab_a[0], tab_b[0]
            for b in range(1, N_DEV):
                sum_a = sum_a + tab_a[b]
                sum_b = sum_b + tab_b[b]
            suma_o[...] = sum_a
            sumb_o[...] = sum_b
            d2d_copy(3).wait_send()
            for j in range(len(ALL_CHIPS)):
                oa_copy(j).wait_send()
            for q in range(3):
                ici_copy(q, snd, got, ici_send, ici_recv).wait_send()
                ici_copy(q, snd_o, got_o, ob_send, ob_recv).wait_send()
            for cp in tables:
                cp.wait_send()

    tile = lambda k: jnp.maximum(k - n_blocks, 0)
    row = lambda w: pl.BlockSpec((TM, w), lambda k: (tile(k), 0))
    col = lambda w: pl.BlockSpec((w, TM), lambda k: (0, tile(k)))
    const = lambda shape, **kw: pl.BlockSpec(shape, lambda k: (0,) * len(shape), **kw)
    once = dict(pipeline_mode=pl.Buffered(1))
    hbm = pl.BlockSpec(memory_space=pl.ANY)
    dma = pltpu.SemaphoreType.DMA
    return pl.pallas_call(
        body, name="in_proj_bwd_rs", grid=(n_blocks + n_tiles,),
        in_specs=[hbm, hbm, col(RET_W), col(SWA_W), const((T, D), **once), const((D_IN, D), **once),
                  row(D), row(1), const((1, D)), row(D), hbm, const((1, 1))]
                 + [const(g.shape) for g in small_g],
        out_specs=(row(D), const((WIN_BLK, D)), const((WOUT_BLK, D)), const((8, D)), const((SWA_H, CH))),
        out_shape=(jax.ShapeDtypeStruct((T, D), F32), jax.ShapeDtypeStruct((WIN_BLK, D), F32),
                   jax.ShapeDtypeStruct((WOUT_BLK, D), F32), jax.ShapeDtypeStruct((8, D), F32),
                   jax.ShapeDtypeStruct((SWA_H, CH), F32)),
        scratch_shapes=[
            pltpu.VMEM((2, WIN_BLK, T), BF16), pltpu.VMEM((WIN_BLK, D), F32),
            pltpu.VMEM((4, WIN_BLK, D), F32), pltpu.VMEM((3, WIN_BLK, D), BF16),
            pltpu.VMEM((3, WIN_BLK, D), BF16), pltpu.VMEM((WIN_BLK, D), F32),
            pltpu.VMEM((4, WOUT_BLK, D), F32), pltpu.VMEM((4, WOUT_BLK, D), F32),
            pltpu.VMEM((3, WOUT_BLK, D), BF16), pltpu.VMEM((3, WOUT_BLK, D), BF16),
            pltpu.VMEM((WOUT_BLK, D), F32),
            pltpu.VMEM((N_DEV, 8, D), F32), pltpu.VMEM((N_DEV, SWA_H, CH), F32),
            pltpu.VMEM((1, D), F32),
            dma((2, 2)), dma((4,)), dma((4,)), dma((3,)), dma((3,)),
            dma((4,)), dma((4,)), dma((4,)), dma((3,)), dma((3,)), dma((2, 7)), dma((2, 7)),
        ],
        compiler_params=_cparams(1),
    )(d_ret, d_swa, d_ret, d_swa, h, wt, x, rstd, norm_w, dy, gwo, sse, *small_g)


SMALL_SHAPES = ((1, D), (1, 512), (1, SWA_D), (1, SWA_D), (1, SWA_H), (SWA_H, N_BUCKETS))


def _adamw_all(g_in_t, g_out, sum_a, sum_b, w_in_t, m_in_t, v_in_t, w_out, m_out, v_out, small_w, small_m, small_v):
    n_small = len(SMALL_SHAPES)
    halves = 2

    def body(*refs):
        gin_ref, gout_ref, suma_ref, sumb_ref, w_in_ref, m_in_ref, v_in_ref, w_out_ref, m_out_ref, v_out_ref = refs[:10]
        pos = 10
        sw_refs, sm_refs, sv_refs = (refs[pos + i * n_small:pos + (i + 1) * n_small] for i in range(3))
        pos += 3 * n_small
        g_in_o, d_in_o, nm_in_o, nv_in_o, g_out_o, d_out_o, nm_out_o, nv_out_o = refs[pos:pos + 8]
        pos += 8
        sg_o, sd_o, snm_o, snv_o = (refs[pos + i * n_small:pos + (i + 1) * n_small] for i in range(4))
        loss_o = refs[pos + 4 * n_small]

        g = gout_ref[...]
        d, nm, nv = _adamw(w_out_ref[...], g, m_out_ref[...], v_out_ref[...])
        g_out_o[...], d_out_o[...], nm_out_o[...], nv_out_o[...] = g, d, nm, nv
        g = gin_ref[...]
        d, nm, nv = _adamw(w_in_ref[...], g, m_in_ref[...], v_in_ref[...])
        g_in_o[...], d_in_o[...], nm_in_o[...], nv_in_o[...] = g, d, nm, nv

        @pl.when(pl.program_id(0) == 0)
        def _():
            loss_o[...] = suma_ref[LOSS_ROW:LOSS_ROW + 1, 0:1] * (0.5 / D)
            for r, (rows, lanes) in enumerate(SMALL_SHAPES):
                g = suma_ref[r:r + 1, 0:lanes] if rows == 1 else sumb_ref[:, 0:lanes]
                d, nm, nv = _adamw(sw_refs[r][...], g, sm_refs[r][...], sv_refs[r][...])
                sg_o[r][...], sd_o[r][...], snm_o[r][...], snv_o[r][...] = g, d, nm, nv

    half = lambda rows: pl.BlockSpec((rows // halves, D), lambda i: (i, 0))
    const = lambda shape: pl.BlockSpec(shape, lambda i: (0,) * len(shape))
    win = jax.ShapeDtypeStruct((WIN_BLK, D), F32)
    wout = jax.ShapeDtypeStruct((WOUT_BLK, D), F32)
    smalls = tuple(jax.ShapeDtypeStruct(s, F32) for s in SMALL_SHAPES)
    small_specs = [const(s) for s in SMALL_SHAPES]
    outs = pl.pallas_call(
        body, name="adamw_all", grid=(halves,),
        in_specs=[half(WIN_BLK), half(WOUT_BLK), const((8, D)), const((SWA_H, CH))] + [half(WIN_BLK)] * 3
                 + [half(WOUT_BLK)] * 3 + small_specs * 3,
        out_specs=tuple([half(WIN_BLK)] * 4 + [half(WOUT_BLK)] * 4 + small_specs * 4 + [const((1, 1))]),
        out_shape=(win,) * 4 + (wout,) * 4 + smalls * 4 + (jax.ShapeDtypeStruct((1, 1), F32),),
        compiler_params=_cparams(1, vmem=32 * 1024 * 1024),
    )(g_in_t, g_out, sum_a, sum_b, w_in_t, m_in_t, v_in_t, w_out, m_out, v_out, *small_w, *small_m, *small_v)
    big, rest = outs[:8], outs[8:]
    return big, [rest[i * n_small:(i + 1) * n_small] for i in range(4)], rest[4 * n_small]


def _small_rows(norm_w, ret_norm_w, q_norm_w, k_norm_w, sinks, rel_bias):
    return (norm_w.reshape(1, D), ret_norm_w.reshape(1, 512), q_norm_w.reshape(1, SWA_D), k_norm_w.reshape(1, SWA_D),
            sinks.reshape(1, SWA_H), rel_bias.T)


def _small_leaves(rows):
    return (rows[0].reshape(D), rows[1].reshape(512), rows[2].reshape(SWA_D), rows[3].reshape(SWA_D),
            rows[4].reshape(SWA_H), rows[5].T)


def kernel(x, norm_w, w_in, ret_norm_w, q_norm_w, k_norm_w, sinks, rel_bias, w_out, loss_target, m_norm_w, m_w_in, m_ret_norm_w, m_q_norm_w, m_k_norm_w, m_sinks, m_rel_bias, m_w_out, v_norm_w, v_w_in, v_ret_norm_w, v_q_norm_w, v_k_norm_w, v_sinks, v_rel_bias, v_w_out):
    x2 = x.reshape(T, D)
    target = loss_target.reshape(T, D)
    nw = norm_w.reshape(1, D)
    rnw_col = ret_norm_w.reshape(RET_V, 1)
    qnw_col = q_norm_w.reshape(SWA_D, 1)
    knw_col = k_norm_w.reshape(SWA_D, 1)
    relb = rel_bias.T.reshape(SWA_H * N_BUCKETS)
    ret_tables = _rotary_tables_t() + _retention_tables_t()
    bucket_t = _bucket_table_t()

    proj, wt, rstd, h = _in_proj_gather(x2, nw, w_in.T)
    ro, mix_r, states, wo = _retention_fwd(proj, rnw_col, ret_tables, w_out)
    so, mix_s = _swa_fwd(proj, qnw_col, knw_col, sinks, relb, bucket_t)
    sse, dy, dmix_r, dmix_s, gwo = _out_proj(mix_r, mix_s, wo, x2, target)

    d_ret, drw_acc = _retention_bwd(proj, ro, states, dmix_r, rnw_col, ret_tables)
    d_swa, dbias, dsink_acc, dqw_acc, dkw_acc = _swa_bwd(proj, so, dmix_s, qnw_col, knw_col, sinks, relb, bucket_t)
    d_relb, d_sinks, d_qnw, d_knw, d_rnw = _small_grads(dbias, bucket_t, dsink_acc, dqw_acc, dkw_acc, drw_acc)
    grad_x, g_in_t, g_out, sum_a, sum_b = _in_proj_bwd_rs(
        d_ret, d_swa, h, wt, x2, rstd, nw, dy, gwo.reshape(N_DEV, WOUT_BLK, D), sse,
        (d_rnw, d_qnw, d_knw, d_sinks, d_relb))

    small_w = _small_rows(norm_w, ret_norm_w, q_norm_w, k_norm_w, sinks, rel_bias)
    small_m = _small_rows(m_norm_w, m_ret_norm_w, m_q_norm_w, m_k_norm_w, m_sinks, m_rel_bias)
    small_v = _small_rows(v_norm_w, v_ret_norm_w, v_q_norm_w, v_k_norm_w, v_sinks, v_rel_bias)
    big, small, loss = _adamw_all(g_in_t, g_out, sum_a, sum_b, w_in.T, m_w_in.T, v_w_in.T, w_out, m_w_out, v_w_out,
                                  small_w, small_m, small_v)

    def leaves(i):
        a = _small_leaves(small[i])
        return (a[0], big[i].T, a[1], a[2], a[3], a[4], a[5], big[4 + i])

    return (loss.reshape(()), grad_x.reshape(1, T, D), *leaves(0), *leaves(1), *leaves(2), *leaves(3))
```

```python
from typing import Callable, NamedTuple

import numpy as np
import jax
import jax.numpy as jnp
from jax import lax
from jax.experimental import pallas as pl
from jax.experimental.pallas import tpu as pltpu

F32 = jnp.float32
BF16 = jnp.bfloat16
MESH = pl.DeviceIdType.MESH

T = 2048
D = 1024
D_IN = 2816
N_DEV = 8
WIN_BLK = D_IN // N_DEV
WOUT_BLK = D // N_DEV
CH = 128
N_CH = T // CH
RET_H, RET_DK, RET_DV = 4, 64, 128
RET_QK = RET_H * RET_DK
RET_V = RET_H * RET_DV
SWA_H, SWA_KV, SWA_D, SWA_G = 8, 2, 64, 4
N_BUCKETS = 32
NORM_EPS = 1e-6
GN_EPS = 1e-5
NEG_INF = -1e30
PIECES = (256, 256, 512, 512, 512, 128, 128, 512)
OFFS = tuple(int(v) for v in np.cumsum((0,) + PIECES))
RET_W = OFFS[4]
SWA_W = D_IN - RET_W
SWA_OFFS = tuple(o - RET_W for o in OFFS[4:])
TM = 256

ADAM_LR, ADAM_B1, ADAM_B2, ADAM_EPS, ADAM_WD, ADAM_STEP = 0.001, 0.9, 0.999, 1e-08, 0.01, 10

VMEM_LIMIT = 56 * 1024 * 1024


def _cparams(n_grid=0, vmem=VMEM_LIMIT):
    sem = ("arbitrary",) * n_grid if n_grid else None
    return pltpu.CompilerParams(dimension_semantics=sem, vmem_limit_bytes=vmem)


class _Part(NamedTuple):
    step: Callable
    inputs: list
    in_specs: list
    out_specs: list
    out_shape: list
    scratch_shapes: list


def _fused_call(name, n_steps, parts):
    n_in = [len(p.inputs) for p in parts]
    n_out = [len(p.out_shape) for p in parts]
    n_scr = [len(p.scratch_shapes) for p in parts]

    def body(*refs):
        ins, outs, scr = refs[:sum(n_in)], refs[sum(n_in):sum(n_in) + sum(n_out)], refs[sum(n_in) + sum(n_out):]
        for i, p in enumerate(parts):
            take = lambda seq, counts: seq[sum(counts[:i]):sum(counts[:i + 1])]
            p.step(*take(ins, n_in), *take(outs, n_out), *take(scr, n_scr))

    flat = lambda field: [v for p in parts for v in getattr(p, field)]
    outs = pl.pallas_call(
        body, name=name, grid=(n_steps,), in_specs=flat("in_specs"), out_specs=tuple(flat("out_specs")),
        out_shape=tuple(flat("out_shape")), scratch_shapes=flat("scratch_shapes"), compiler_params=_cparams(1),
    )(*flat("inputs"))
    return [list(outs[sum(n_out[:i]):sum(n_out[:i + 1])]) for i in range(len(parts))]


def _dot(a, b):
    return jnp.dot(a, b, preferred_element_type=F32)


def _dot_nt(a, b):
    return lax.dot_general(a, b, (((1,), (1,)), ((), ())), preferred_element_type=F32)


def _bf(a):
    return a.astype(BF16)


def _sigmoid(x):
    return 1.0 / (1.0 + jnp.exp(-x))


def _adamw(w, g, m, v):
    m = ADAM_B1 * m + (1.0 - ADAM_B1) * g
    v = ADAM_B2 * v + (1.0 - ADAM_B2) * (g * g)
    m_hat = m / (1.0 - ADAM_B1 ** ADAM_STEP)
    v_hat = v / (1.0 - ADAM_B2 ** ADAM_STEP)
    delta = -ADAM_LR * (m_hat / (jnp.sqrt(v_hat) + ADAM_EPS) + ADAM_WD * w)
    return delta, m, v


def _rotary_tables_t():
    half = RET_DK // 2
    inv_freq = np.float32(10000.0) ** (-np.arange(half, dtype=np.float32) / np.float32(half))
    ang = inv_freq[:, None] * np.arange(T, dtype=np.float32)[None, :]
    cos, sin = np.cos(ang).astype(np.float32), np.sin(ang).astype(np.float32)
    cos64 = np.concatenate([cos, cos], axis=0)
    sin64 = np.concatenate([-sin, sin], axis=0)
    return np.tile(cos64, (RET_H, 1)), np.tile(sin64, (RET_H, 1))


def _retention_tables_t():
    gamma = (1.0 - np.exp2(-5.0 - np.arange(RET_H, dtype=np.float32))).astype(np.float32)
    log_g = np.log(gamma).astype(np.float32)
    i = np.arange(CH, dtype=np.float32)
    diff = i[None, :] - i[:, None]
    decay = np.where(diff >= 0, np.exp(log_g[:, None, None] * np.maximum(diff, 0.0)), 0.0).astype(np.float32)
    decay_all = np.concatenate(list(decay), axis=1)
    zeta = np.exp(log_g[:, None] * (CH - 1.0 - i)).astype(np.float32)
    zeta_tab = np.repeat(zeta.T, RET_DK, axis=1)
    xi = np.exp(log_g[:, None] * (i + 1.0)).astype(np.float32)
    xi_tab = np.repeat(xi, RET_DK, axis=0)
    chunk_decay = np.exp(log_g * np.float32(CH)).astype(np.float32)
    row_head = np.arange(RET_V)[:, None] // RET_DV
    col_head = np.arange(RET_QK)[None, :] // RET_DK
    state_mask = (row_head == col_head).astype(np.float32)
    state_decay = (state_mask * chunk_decay[row_head]).astype(np.float32)
    q_mask = (np.arange(RET_QK)[:, None] // RET_DK == np.arange(RET_H * CH)[None, :] // CH).astype(np.float32)
    return (np.ascontiguousarray(decay_all), np.ascontiguousarray(zeta_tab), np.ascontiguousarray(xi_tab),
            state_mask, state_decay, q_mask)


def _bucket_table_t():
    qi = np.arange(CH)[:, None]
    kj = np.arange(2 * CH)[None, :]
    dist = qi + CH - kj
    n = np.maximum(dist, 0)
    max_exact = N_BUCKETS // 2
    nf = np.maximum(n, 1).astype(np.float32)
    large = max_exact + (np.log(nf / np.float32(max_exact)) / np.float32(np.log(CH / max_exact))
                         * np.float32(N_BUCKETS - max_exact)).astype(np.int32)
    large = np.minimum(large, N_BUCKETS - 1)
    bucket = np.where(n < max_exact, n, large)
    return np.ascontiguousarray(np.where((dist >= 0) & (dist < CH), bucket, -1).astype(np.int32).T)


def _mesh_pos():
    return lax.axis_index("x"), lax.axis_index("y"), lax.axis_index("c")


def _blk(px, py, pc):
    return 4 * px + 2 * py + pc


def _in_proj_gather(x, norm_w, w_in_t):
    tchunk = 512

    def body(x_ref, nw_ref, win_ref, proj_ref, wt_ref, rstd_ref, h_ref, stage, send_sems, recv_sems, out_sems):
        x, y, c = _mesh_pos()
        me = _blk(x, y, c)
        sibling = (x, y, 1 - c)
        chips = [(1 - x, y), (x, 1 - y), (1 - x, 1 - y)]

        def copy(k, b, to):
            return pltpu.make_async_remote_copy(src_ref=wt_ref.at[b], dst_ref=wt_ref.at[b], send_sem=send_sems.at[k],
                                                recv_sem=recv_sems.at[k], device_id=to, device_id_type=MESH)

        wt_ref[me] = _bf(win_ref[...])
        first = [copy(0, me, sibling)] + [copy(1 + j, me, (*chip, c)) for j, chip in enumerate(chips)]
        for cp in first:
            cp.start()

        nw = nw_ref[...]
        for i in range(T // TM):
            rows = slice(i * TM, (i + 1) * TM)
            xv = x_ref[rows, :]
            r = lax.rsqrt(jnp.mean(xv * xv, axis=-1, keepdims=True) + NORM_EPS)
            h_ref[rows, :] = _bf(xv * r * nw)
            rstd_ref[rows, :] = r

        writes = []

        def project(b):
            k = len(writes)
            if k >= 2:
                writes[k - 2].wait()
            w = wt_ref[b]
            for t in range(T // tchunk):
                cols = slice(t * tchunk, (t + 1) * tchunk)
                stage[k % 2, :, cols] = _dot_nt(w, h_ref[cols, :])
            cp = pltpu.make_async_copy(stage.at[k % 2], proj_ref.at[b], out_sems.at[k % 2])
            cp.start()
            writes.append(cp)

        project(me)
        copy(0, _blk(x, y, 1 - c), (x, y, c)).wait_recv()
        project(_blk(x, y, 1 - c))
        passed = []
        for j, chip in enumerate(chips):
            copy(1 + j, _blk(*chip, c), (x, y, c)).wait_recv()
            fwd = copy(4 + j, _blk(*chip, c), sibling)
            fwd.start()
            passed.append(fwd)
            project(_blk(*chip, c))
            copy(4 + j, _blk(*chip, 1 - c), (x, y, c)).wait_recv()
            project(_blk(*chip, 1 - c))
        writes[-2].wait()
        writes[-1].wait()
        for cp in first + passed:
            cp.wait_send()

    vm = pl.BlockSpec(memory_space=pltpu.VMEM)
    proj, wt, rstd, h = pl.pallas_call(
        body, name="in_proj_gather",
        out_shape=(jax.ShapeDtypeStruct((N_DEV, WIN_BLK, T), F32), jax.ShapeDtypeStruct((N_DEV, WIN_BLK, D), BF16),
                   jax.ShapeDtypeStruct((T, 1), F32), jax.ShapeDtypeStruct((T, D), BF16)),
        in_specs=[vm, vm, vm], out_specs=(pl.BlockSpec(memory_space=pl.ANY), vm, vm, vm),
        scratch_shapes=[pltpu.VMEM((2, WIN_BLK, T), F32),
                        pltpu.SemaphoreType.DMA((7,)), pltpu.SemaphoreType.DMA((7,)), pltpu.SemaphoreType.DMA((2,))],
        compiler_params=_cparams(vmem=48 * 1024 * 1024),
    )(x, norm_w, w_in_t)
    return proj.reshape(D_IN, T), wt.reshape(D_IN, D), rstd, h


def _gather_w_out(w_out_ref, own_ref, wo_ref, local_sem, send_sems, recv_sems, phase):
    x, y, c = _mesh_pos()
    me = _blk(x, y, c)
    sibling = (x, y, 1 - c)
    chips = [(1 - x, y), (x, 1 - y), (1 - x, 1 - y)]

    def copy(k, b, to, src=None):
        return pltpu.make_async_remote_copy(src_ref=wo_ref.at[b] if src is None else src, dst_ref=wo_ref.at[b],
                                            send_sem=send_sems.at[k], recv_sem=recv_sems.at[k], device_id=to,
                                            device_id_type=MESH)

    local = pltpu.make_async_copy(own_ref, wo_ref.at[me], local_sem)
    first = [copy(0, me, sibling, own_ref)] + [copy(1 + j, me, (*chip, c), own_ref) for j, chip in enumerate(chips)]
    passed = [copy(4 + j, _blk(*chip, c), sibling) for j, chip in enumerate(chips)]
    if phase == 0:
        own_ref[...] = _bf(w_out_ref[...])
        local.start()
        for cp in first:
            cp.start()
    elif phase == 1:
        for j, chip in enumerate(chips):
            copy(1 + j, _blk(*chip, c), (x, y, c)).wait_recv()
            passed[j].start()
    else:
        local.wait()
        copy(0, _blk(x, y, 1 - c), (x, y, c)).wait_recv()
        for j, chip in enumerate(chips):
            copy(4 + j, _blk(*chip, 1 - c), (x, y, c)).wait_recv()
        for cp in first + passed:
            cp.wait_send()


def _proj_piece(i, block_of):
    rows = PIECES[i]
    assert OFFS[i] % rows == 0
    return pl.BlockSpec((rows, CH), lambda n: (OFFS[i] // rows, block_of(n)))


def _proj_gate_halves(block_of):
    return [pl.BlockSpec((256, CH), lambda n, j=j: (OFFS[7] // 256 + j, block_of(n))) for j in range(2)]


def _swap_halves_t(t):
    half = RET_DK // 2
    parts = []
    for h in range(RET_H):
        parts += [t[h * RET_DK + half:(h + 1) * RET_DK], t[h * RET_DK:h * RET_DK + half]]
    return jnp.concatenate(parts, axis=0)


def _rotate(t, cos, sin):
    return t * cos + _swap_halves_t(t) * sin


def _group_norm_t(o):
    mu = jnp.mean(o, axis=0, keepdims=True)
    var = jnp.mean((o - mu) * (o - mu), axis=0, keepdims=True)
    rstd = lax.rsqrt(var + GN_EPS)
    return (o - mu) * rstd, rstd


def _retention_scores_t(q_t, k_rm_b, q_mask, decay_all):
    q_heads = _bf(jnp.concatenate([q_t] * RET_H, axis=1) * q_mask)
    return _dot(k_rm_b, q_heads) * decay_all


def _retention_fwd(proj, ret_norm_w, tables, w_out_blk):
    cos_t, sin_t, decay_all, zeta_tab, xi_tab, state_mask, state_decay, q_mask = tables

    def body(rq_ref, rk_ref, rv_ref, rg_ref, w_ref, cos_ref, sin_ref, decay_ref, zeta_ref, xi_ref, smask_ref,
             sdecay_ref, qmask_ref, wout_ref, ro_ref, mix_ref, st_ref, wo_ref, state, wo_own, wo_local, wo_send, wo_recv):
        n = pl.program_id(0)

        @pl.when(n == 0)
        def _():
            state[...] = jnp.zeros_like(state)
            _gather_w_out(wout_ref, wo_own, wo_ref, wo_local, wo_send, wo_recv, phase=0)

        @pl.when(n == N_CH // 2)
        def _():
            _gather_w_out(wout_ref, wo_own, wo_ref, wo_local, wo_send, wo_recv, phase=1)

        @pl.when(n == N_CH - 1)
        def _():
            _gather_w_out(wout_ref, wo_own, wo_ref, wo_local, wo_send, wo_recv, phase=2)

        cos, sin = cos_ref[...], sin_ref[...]
        q_t = _rotate(rq_ref[...], cos, sin)
        k_t = _rotate(rk_ref[...], cos, sin) * (RET_DK ** -0.5)
        k_rm = k_t.T
        v_b = _bf(rv_ref[...])
        m_b = _bf(state[...])
        st_ref[0] = m_b
        scores_b = _bf(_retention_scores_t(q_t, _bf(k_rm), qmask_ref[...], decay_ref[...]))
        cross = _dot(m_b, _bf(q_t * xi_ref[...]))
        state[...] = state[...] * sdecay_ref[...] + _dot(v_b, _bf(k_rm * zeta_ref[...])) * smask_ref[...]
        heads = range(RET_H)
        rows = [slice(h * RET_DV, (h + 1) * RET_DV) for h in heads]
        o = [_dot(v_b[rows[h]], scores_b[:, h * CH:(h + 1) * CH]) + cross[rows[h]] for h in heads]
        for h in heads:
            rn, _ = _group_norm_t(o[h])
            g = rg_ref[rows[h], :]
            ro_ref[rows[h], :] = o[h]
            mix_ref[rows[h], :] = rn * w_ref[rows[h], :] * (g * _sigmoid(g))

    col = lambda w: pl.BlockSpec((w, CH), lambda n: (0, n))
    const = lambda shape: pl.BlockSpec(shape, lambda n: (0,) * len(shape))
    cur = lambda n: n
    return _Part(
        body,
        inputs=[proj, proj, proj, proj, ret_norm_w, cos_t, sin_t, decay_all, zeta_tab, xi_tab, state_mask, state_decay,
                q_mask, w_out_blk],
        in_specs=[_proj_piece(i, cur) for i in range(4)] + [
            const((RET_V, 1)), col(RET_QK), col(RET_QK),
            const(decay_all.shape), const(zeta_tab.shape), const(xi_tab.shape), const(state_mask.shape),
            const(state_decay.shape), const(q_mask.shape), const((WOUT_BLK, D))],
        out_specs=[col(RET_V), col(RET_V), pl.BlockSpec((1, RET_V, RET_QK), lambda n: (n, 0, 0)),
                   pl.BlockSpec(memory_space=pl.ANY)],
        out_shape=[jax.ShapeDtypeStruct((RET_V, T), F32), jax.ShapeDtypeStruct((RET_V, T), F32),
                   jax.ShapeDtypeStruct((N_CH, RET_V, RET_QK), BF16), jax.ShapeDtypeStruct((N_DEV, WOUT_BLK, D), BF16)],
        scratch_shapes=[pltpu.VMEM((RET_V, RET_QK), F32), pltpu.VMEM((WOUT_BLK, D), BF16), pltpu.SemaphoreType.DMA,
                        pltpu.SemaphoreType.DMA((N_DEV - 1,)), pltpu.SemaphoreType.DMA((N_DEV - 1,))])


def _retention_bwd(proj, ro, states, dmix, ret_norm_w, tables):
    cos_t, sin_t, decay_all, zeta_tab, xi_tab, state_mask, state_decay, q_mask = tables

    def body(*refs):
        @pl.when(pl.program_id(0) < N_CH)
        def _():
            chunk(*refs)

    def chunk(rq_ref, rk_ref, rv_ref, rg_ref, ro_ref, st_ref, dm_ref, w_ref, cos_ref, sin_ref, decay_ref, zeta_ref,
              xi_ref, smask_ref, sdecay_ref, qmask_ref, d_ref, dw_ref, gstate):
        i = pl.program_id(0)

        @pl.when(i == 0)
        def _():
            gstate[...] = jnp.zeros_like(gstate)
            dw_ref[...] = jnp.zeros_like(dw_ref)

        cos, sin = cos_ref[...], sin_ref[...]
        q_t = _rotate(rq_ref[...], cos, sin)
        k_t = _rotate(rk_ref[...], cos, sin) * (RET_DK ** -0.5)
        q_b, k_b = _bf(q_t), _bf(k_t)
        k_rm = k_t.T
        kz_b = _bf(k_rm * zeta_ref[...])
        qx_b = _bf(q_t * xi_ref[...])
        v_t = rv_ref[...]
        v_b = _bf(v_t)
        v_rm_b = _bf(v_t.T)
        decay = decay_ref[...]
        scores_b = _bf(_retention_scores_t(q_t, _bf(k_rm), qmask_ref[...], decay))
        heads = range(RET_H)
        rows = [slice(h * RET_DV, (h + 1) * RET_DV) for h in heads]
        qk_rows = [slice(h * RET_DK, (h + 1) * RET_DK) for h in heads]
        lanes = [slice(h * CH, (h + 1) * CH) for h in heads]
        do = []
        for h in heads:
            g, w, dm = rg_ref[rows[h], :], w_ref[rows[h], :], dm_ref[rows[h], :]
            rn, rstd = _group_norm_t(ro_ref[rows[h], :])
            sig = _sigmoid(g)
            silu = g * sig
            d_ref[OFFS[3] + h * RET_DV:OFFS[3] + (h + 1) * RET_DV, :] = _bf(dm * rn * w * (sig * (1.0 + g * (1.0 - sig))))
            dw_ref[rows[h], :] += dm * silu * rn
            drn = dm * silu * w
            do.append(rstd * (drn - jnp.mean(drn, axis=0, keepdims=True)
                              - rn * jnp.mean(drn * rn, axis=0, keepdims=True)))
        do_b = _bf(jnp.concatenate(do, axis=0))
        m_b = st_ref[0]
        g_all = gstate[...]
        g_b = _bf(g_all)
        dscores_b = [_bf(_dot(v_rm_b[:, rows[h]], do_b[rows[h]]) * decay[:, lanes[h]]) for h in heads]
        dq_cross = lax.dot_general(m_b, do_b, (((0,), (0,)), ((), ())), preferred_element_type=F32)
        dkz = _dot(v_rm_b, g_b)
        dv_cross = _dot_nt(g_b, kz_b)
        gstate[...] = g_all * sdecay_ref[...] + _dot_nt(do_b, qx_b) * smask_ref[...]
        dq = jnp.concatenate([_dot(k_b[qk_rows[h]], dscores_b[h]) for h in heads], axis=0) + dq_cross * xi_ref[...]
        dk = (jnp.concatenate([_dot_nt(q_b[qk_rows[h]], dscores_b[h]) for h in heads], axis=0)
              + (dkz * zeta_ref[...]).T) * (RET_DK ** -0.5)
        for h in heads:
            d_ref[OFFS[2] + h * RET_DV:OFFS[2] + (h + 1) * RET_DV, :] = _bf(
                _dot_nt(do_b[rows[h]], scores_b[:, lanes[h]]) + dv_cross[rows[h]])
        d_ref[OFFS[0]:OFFS[1], :] = _bf(dq * cos + _swap_halves_t(dq * sin))
        d_ref[OFFS[1]:OFFS[2], :] = _bf(dk * cos + _swap_halves_t(dk * sin))

    chunk_of = lambda i: N_CH - 1 - jnp.minimum(i, N_CH - 1)
    col = lambda w: pl.BlockSpec((w, CH), lambda i: (0, chunk_of(i)))
    const = lambda shape: pl.BlockSpec(shape, lambda i: (0,) * len(shape))
    return _Part(
        body,
        inputs=[proj, proj, proj, proj, ro, states, dmix, ret_norm_w, cos_t, sin_t, decay_all, zeta_tab, xi_tab,
                state_mask, state_decay, q_mask],
        in_specs=[_proj_piece(j, chunk_of) for j in range(4)] + [
                  col(RET_V),
                  pl.BlockSpec((1, RET_V, RET_QK), lambda i: (chunk_of(i), 0, 0)), col(RET_V),
                  const((RET_V, 1)), col(RET_QK), col(RET_QK),
                  const(decay_all.shape), const(zeta_tab.shape), const(xi_tab.shape), const(state_mask.shape),
                  const(state_decay.shape), const(q_mask.shape)],
        out_specs=[col(RET_W), const((RET_V, CH))],
        out_shape=[jax.ShapeDtypeStruct((RET_W, T), BF16), jax.ShapeDtypeStruct((RET_V, CH), F32)],
        scratch_shapes=[pltpu.VMEM((RET_V, RET_QK), F32)])


HQ_LANES = SWA_H * CH


def _head_lanes(hq):
    return slice(hq * CH, (hq + 1) * CH)


def _build_bias_t(bucket_ref, relb_ref, sinks_ref, bias_ref, sink_row):
    bt = bucket_ref[...]
    first = lax.broadcasted_iota(jnp.int32, bt.shape, 0) < CH
    for hq in range(SWA_H):
        b = jnp.full(bt.shape, NEG_INF, F32)
        for bk in range(N_BUCKETS):
            b = jnp.where(bt == bk, relb_ref[hq * N_BUCKETS + bk], b)
        bias_ref[0, :, _head_lanes(hq)] = b
        bias_ref[1, :, _head_lanes(hq)] = jnp.where(first, NEG_INF, b)
        sink_row[:, _head_lanes(hq)] = jnp.full((1, CH), sinks_ref[hq], F32)


def _rms_t(t, w_col):
    r = lax.rsqrt(jnp.mean(t * t, axis=0, keepdims=True) + NORM_EPS)
    return t * r * w_col, r


def _rms_t_bwd(dn, t, r, w_col):
    u = dn * w_col
    return r * u - t * (r * r * r) * jnp.mean(u * t, axis=0, keepdims=True), dn * t * r


def _norm_kv(k_t, kw):
    return jnp.concatenate([_rms_t(k_t[g * SWA_D:(g + 1) * SWA_D], kw)[0] for g in range(SWA_KV)], axis=0)


def _kv_slot(a, kv):
    z = jnp.zeros_like(a)
    return jnp.concatenate([a, z] if kv == 0 else [z, a], axis=0)


def _softmax_t(s, sink):
    m = jnp.maximum(jnp.max(s, axis=0, keepdims=True), sink)
    p = jnp.exp(s - m)
    e_sink = jnp.exp(sink - m)
    inv = 1.0 / (jnp.sum(p, axis=0, keepdims=True) + e_sink)
    return p * inv, e_sink * inv


def _gate_rows(sg_halves, hq):
    per_half = SWA_H // 2
    return sg_halves[hq // per_half][(hq % per_half) * SWA_D:(hq % per_half + 1) * SWA_D, :]


def _swa_fwd(proj, q_norm_w, k_norm_w, sinks, rel_bias_flat, bucket_t):
    def body(sq_ref, skp_ref, skc_ref, svp_ref, svc_ref, sg_lo, sg_hi, qw_ref, kw_ref, sinks_ref, relb_ref, bucket_ref,
             so_ref, mix_ref, bias_ref, sink_row):
        n = pl.program_id(0)

        @pl.when(n == 0)
        def _():
            _build_bias_t(bucket_ref, relb_ref, sinks_ref, bias_ref, sink_row)

        var = (n == 0).astype(jnp.int32)
        qw, kw = qw_ref[...], kw_ref[...]
        kn_band = jnp.concatenate([_norm_kv(skp_ref[...], kw), _norm_kv(skc_ref[...], kw)], axis=1)
        kb_rm = _bf(kn_band.T)
        vband = _bf(jnp.concatenate([svp_ref[...], svc_ref[...]], axis=1))
        q_all = jnp.concatenate(
            [_bf(_kv_slot(_rms_t(sq_ref[hq * SWA_D:(hq + 1) * SWA_D, :], qw)[0] * (SWA_D ** -0.5), hq // SWA_G))
             for hq in range(SWA_H)], axis=1)
        probs, _ = _softmax_t(_dot(kb_rm, q_all) + bias_ref[var], sink_row[...])
        probs_b = _bf(probs)
        for kv in range(SWA_KV):
            o = _dot(vband[kv * SWA_D:(kv + 1) * SWA_D], probs_b[:, kv * SWA_G * CH:(kv + 1) * SWA_G * CH])
            for j in range(SWA_G):
                rows = slice((kv * SWA_G + j) * SWA_D, (kv * SWA_G + j + 1) * SWA_D)
                oh = o[:, j * CH:(j + 1) * CH]
                g = _gate_rows((sg_lo, sg_hi), kv * SWA_G + j)
                so_ref[rows, :] = oh
                mix_ref[rows, :] = oh * (g * _sigmoid(g))

    col = lambda w: pl.BlockSpec((w, CH), lambda n: (0, n))
    const = lambda shape: pl.BlockSpec(shape, lambda n: (0,) * len(shape))
    smem = pl.BlockSpec(memory_space=pltpu.SMEM)
    cur = lambda n: n
    prev = lambda n: jnp.maximum(n - 1, 0)
    return _Part(
        body,
        inputs=[proj, proj, proj, proj, proj, proj, proj, q_norm_w, k_norm_w, sinks, rel_bias_flat, bucket_t],
        in_specs=[_proj_piece(4, cur), _proj_piece(5, prev), _proj_piece(5, cur), _proj_piece(6, prev),
                  _proj_piece(6, cur)] + _proj_gate_halves(cur) + [
                  const((SWA_D, 1)), const((SWA_D, 1)), smem, smem, const((2 * CH, CH))],
        out_specs=[col(512), col(512)],
        out_shape=[jax.ShapeDtypeStruct((512, T), F32), jax.ShapeDtypeStruct((512, T), F32)],
        scratch_shapes=[pltpu.VMEM((2, 2 * CH, HQ_LANES), F32), pltpu.VMEM((1, HQ_LANES), F32)])


def _swa_bwd(proj, so, dmix, q_norm_w, k_norm_w, sinks, rel_bias_flat, bucket_t):
    def body(sq_ref, skp_ref, skc_ref, svp_ref, svc_ref, sg_lo, sg_hi, so_ref, dm_ref, qw_ref, kw_ref, sinks_ref,
             relb_ref, bucket_ref, d_ref, dbias_ref, dsink_ref, dqw_ref, dkw_ref,
             bias_ref, sink_row, band_dk, band_dv, carry_dk, carry_dv, hold_dq, hold_dg):
        n = pl.program_id(0)
        slot = n % 2

        @pl.when(n == 0)
        def _():
            _build_bias_t(bucket_ref, relb_ref, sinks_ref, bias_ref, sink_row)
            for ref in (dbias_ref, dsink_ref, dqw_ref, dkw_ref, carry_dk, carry_dv):
                ref[...] = jnp.zeros_like(ref)

        qw, kw = qw_ref[...], kw_ref[...]

        @pl.when(n < N_CH)
        def _():
            var = (n == 0).astype(jnp.int32)
            kn_band = jnp.concatenate([_norm_kv(skp_ref[...], kw), _norm_kv(skc_ref[...], kw)], axis=1)
            kb_rm = _bf(kn_band.T)
            kn_band_s = _bf(kn_band * (SWA_D ** -0.5))
            vband_f = jnp.concatenate([svp_ref[...], svc_ref[...]], axis=1)
            vb_rm = _bf(vband_f.T)
            q_raw, q_rstd, qs_b, do_b = [], [], [], []
            for hq in range(SWA_H):
                rows = slice(hq * SWA_D, (hq + 1) * SWA_D)
                q_t = sq_ref[rows, :]
                qn, rq = _rms_t(q_t, qw)
                g = _gate_rows((sg_lo, sg_hi), hq)
                sig = _sigmoid(g)
                dm = dm_ref[rows, :]
                hold_dg[slot, rows, :] = _bf(dm * so_ref[rows, :] * (sig * (1.0 + g * (1.0 - sig))))
                q_raw.append(q_t)
                q_rstd.append(rq)
                qs_b.append(_bf(qn * (SWA_D ** -0.5)))
                do_b.append(_bf(dm * (g * sig)))
            q_all = jnp.concatenate([_kv_slot(qs_b[hq], hq // SWA_G) for hq in range(SWA_H)], axis=1)
            do_all = jnp.concatenate([_kv_slot(do_b[hq], hq // SWA_G) for hq in range(SWA_H)], axis=1)
            probs, p_sink = _softmax_t(_dot(kb_rm, q_all) + bias_ref[var], sink_row[...])
            dprobs = _dot(vb_rm, do_all)
            t = jnp.sum(probs * dprobs, axis=0, keepdims=True)
            dlog = probs * (dprobs - t)
            dsink_ref[...] += -(p_sink * t)
            dbias_ref[...] += dlog
            dlog_b, probs_b = _bf(dlog), _bf(probs)
            dkn, dvv = [], []
            for kv in range(SWA_KV):
                heads = range(kv * SWA_G, (kv + 1) * SWA_G)
                lanes = slice(kv * SWA_G * CH, (kv + 1) * SWA_G * CH)
                dvv.append(_dot_nt(jnp.concatenate([do_b[hq] for hq in heads], axis=1), probs_b[:, lanes]))
                dkn.append(_dot_nt(jnp.concatenate([qs_b[hq] for hq in heads], axis=1), dlog_b[:, lanes]))
                dqn = _dot(kn_band_s[kv * SWA_D:(kv + 1) * SWA_D], dlog_b[:, lanes])
                for j, hq in enumerate(heads):
                    dq_t, dqw_terms = _rms_t_bwd(dqn[:, j * CH:(j + 1) * CH], q_raw[hq], q_rstd[hq], qw)
                    hold_dq[slot, hq * SWA_D:(hq + 1) * SWA_D, :] = _bf(dq_t)
                    dqw_ref[...] += dqw_terms
            band_dk[...] = jnp.concatenate(dkn, axis=0)
            band_dv[...] = jnp.concatenate(dvv, axis=0)

        @pl.when(n == N_CH)
        def _():
            band_dk[...] = jnp.zeros_like(band_dk)
            band_dv[...] = jnp.zeros_like(band_dv)

        @pl.when(n >= 1)
        def _():
            dkn_prev = carry_dk[...] + band_dk[:, 0:CH]
            k_t = skp_ref[...]
            for kv in range(SWA_KV):
                rows = slice(kv * SWA_D, (kv + 1) * SWA_D)
                _, rk = _rms_t(k_t[rows], kw)
                dk_t, dkw_terms = _rms_t_bwd(dkn_prev[rows], k_t[rows], rk, kw)
                d_ref[SWA_OFFS[1] + kv * SWA_D:SWA_OFFS[1] + (kv + 1) * SWA_D, :] = _bf(dk_t)
                dkw_ref[...] += dkw_terms
            d_ref[SWA_OFFS[2]:SWA_OFFS[3], :] = _bf(carry_dv[...] + band_dv[:, 0:CH])
            d_ref[SWA_OFFS[0]:SWA_OFFS[1], :] = hold_dq[1 - slot]
            d_ref[SWA_OFFS[3]:SWA_OFFS[4], :] = hold_dg[1 - slot]

        carry_dk[...] = band_dk[:, CH:2 * CH]
        carry_dv[...] = band_dv[:, CH:2 * CH]

    cur_block = lambda n: jnp.minimum(n, N_CH - 1)
    prev_block = lambda n: jnp.maximum(n - 1, 0)
    col = lambda w: pl.BlockSpec((w, CH), lambda n: (0, cur_block(n)))
    prev = lambda w: pl.BlockSpec((w, CH), lambda n: (0, prev_block(n)))
    const = lambda shape: pl.BlockSpec(shape, lambda n: (0,) * len(shape))
    smem = pl.BlockSpec(memory_space=pltpu.SMEM)
    return _Part(
        body,
        inputs=[proj, proj, proj, proj, proj, proj, proj, so, dmix, q_norm_w, k_norm_w, sinks, rel_bias_flat, bucket_t],
        in_specs=[_proj_piece(4, cur_block), _proj_piece(5, prev_block), _proj_piece(5, cur_block),
                  _proj_piece(6, prev_block), _proj_piece(6, cur_block)] + _proj_gate_halves(cur_block) + [
                  col(512), col(512), const((SWA_D, 1)), const((SWA_D, 1)), smem, smem, const((2 * CH, CH))],
        out_specs=[prev(SWA_W), const((2 * CH, HQ_LANES)), const((1, HQ_LANES)),
                   const((SWA_D, CH)), const((SWA_D, CH))],
        out_shape=[jax.ShapeDtypeStruct((SWA_W, T), BF16),
                   jax.ShapeDtypeStruct((2 * CH, HQ_LANES), F32), jax.ShapeDtypeStruct((1, HQ_LANES), F32),
                   jax.ShapeDtypeStruct((SWA_D, CH), F32), jax.ShapeDtypeStruct((SWA_D, CH), F32)],
        scratch_shapes=[pltpu.VMEM((2, 2 * CH, HQ_LANES), F32), pltpu.VMEM((1, HQ_LANES), F32),
                        pltpu.VMEM((128, 2 * CH), F32), pltpu.VMEM((128, 2 * CH), F32),
                        pltpu.VMEM((128, CH), F32), pltpu.VMEM((128, CH), F32),
                        pltpu.VMEM((2, 512, CH), BF16), pltpu.VMEM((2, 512, CH), BF16)])


def _small_grads(dbias, bucket_t, dsink_acc, dqw_acc, dkw_acc, drw_acc):
    def body(dbias_ref, bucket_ref, dsink_ref, dqw_ref, dkw_ref, drw_ref, relb_o, sink_o, qw_o, kw_o, rw_o):
        bt = bucket_ref[...]
        row = lax.broadcasted_iota(jnp.int32, (SWA_H, N_BUCKETS), 0)
        col = lax.broadcasted_iota(jnp.int32, (SWA_H, N_BUCKETS), 1)
        lane = lax.broadcasted_iota(jnp.int32, (1, CH), 1)
        acc = jnp.zeros((SWA_H, N_BUCKETS), F32)
        sink = jnp.zeros((1, CH), F32)
        for hq in range(SWA_H):
            d = dbias_ref[:, _head_lanes(hq)]
            for bk in range(N_BUCKETS):
                s = jnp.sum(jnp.sum(jnp.where(bt == bk, d, 0.0), axis=0, keepdims=True), axis=1, keepdims=True)
                acc = acc + jnp.where((row == hq) & (col == bk), s, 0.0)
            sink = sink + jnp.where(lane == hq, jnp.sum(dsink_ref[:, _head_lanes(hq)], axis=1, keepdims=True), 0.0)
        relb_o[...] = acc
        sink_o[...] = sink
        for src, dst in ((dqw_ref, qw_o), (dkw_ref, kw_o)):
            padded = jnp.concatenate([src[...], jnp.zeros((CH - SWA_D, CH), F32)], axis=0)
            dst[...] = jnp.sum(padded.T, axis=0, keepdims=True)
        rw_o[...] = jnp.sum(drw_ref[...].T, axis=0, keepdims=True)

    vm = pl.BlockSpec(memory_space=pltpu.VMEM)
    row128 = jax.ShapeDtypeStruct((1, CH), F32)
    return pl.pallas_call(body, name="small_grads",
                          out_shape=(jax.ShapeDtypeStruct((SWA_H, N_BUCKETS), F32), row128, row128, row128,
                                     jax.ShapeDtypeStruct((1, RET_V), F32)),
                          in_specs=[vm] * 6, out_specs=(vm,) * 5, compiler_params=_cparams(),
                          )(dbias, bucket_t, dsink_acc, dqw_acc, dkw_acc, drw_acc)


def _out_proj(mix_r, mix_s, wo, x, target):
    def body(mr_ref, ms_ref, w_ref, x_ref, t_ref, loss_ref, dy_ref, dmr_ref, dms_ref, gw_ref):
        i = pl.program_id(0)

        @pl.when(i == 0)
        def _():
            loss_ref[...] = jnp.zeros_like(loss_ref)
            gw_ref[...] = jnp.zeros_like(gw_ref)

        mixed = jnp.concatenate([mr_ref[...], ms_ref[...]], axis=0)
        w = w_ref[...]
        err = x_ref[...] + _dot(_bf(mixed.T), w) - t_ref[...]
        loss_ref[...] += jnp.sum(jnp.sum(err * err, axis=1, keepdims=True), axis=0, keepdims=True)
        dy = err * (1.0 / D)
        dy_ref[...] = dy
        dy_b = _bf(dy)
        dmix = _dot_nt(w, dy_b)
        dmr_ref[...] = dmix[0:512]
        dms_ref[...] = dmix[512:D]
        gw_ref[...] += _dot(_bf(mixed), dy_b)

    row = lambda w: pl.BlockSpec((TM, w), lambda i: (i, 0))
    col = lambda w: pl.BlockSpec((w, TM), lambda i: (0, i))
    const = lambda shape: pl.BlockSpec(shape, lambda i: (0,) * len(shape))
    return pl.pallas_call(
        body, name="out_proj", grid=(T // TM,),
        in_specs=[col(512), col(512), const((D, D)), row(D), row(D)],
        out_specs=(const((1, 1)), row(D), col(512), col(512), const((D, D))),
        out_shape=(jax.ShapeDtypeStruct((1, 1), F32), jax.ShapeDtypeStruct((T, D), F32),
                   jax.ShapeDtypeStruct((512, T), F32), jax.ShapeDtypeStruct((512, T), F32),
                   jax.ShapeDtypeStruct((D, D), F32)),
        compiler_params=_cparams(1),
    )(mix_r, mix_s, wo, x, target)


LOSS_ROW = 5
ALL_CHIPS = ((0, 0), (0, 1), (1, 0), (1, 1))


def _w_out_reduce_scatter(gwo):
    def body(gwo_ref, gout_o, own, rcv, snd, got, tot, local_sems, a_send, a_recv, b_send, b_recv):
        k = pl.program_id(0)
        x_, y_, c_ = _mesh_pos()
        sibling = (x_, y_, 1 - c_)
        rel_chips = [(1 - x_, y_), (x_, 1 - y_), (1 - x_, 1 - y_)]

        def to_sibling(j):
            px, py = ALL_CHIPS[j]
            return pltpu.make_async_remote_copy(src_ref=gwo_ref.at[_blk(px, py, 1 - c_)], dst_ref=rcv.at[j],
                                                send_sem=a_send.at[j], recv_sem=a_recv.at[j], device_id=sibling,
                                                device_id_type=MESH)

        def local(j):
            px, py = ALL_CHIPS[j]
            return pltpu.make_async_copy(gwo_ref.at[_blk(px, py, c_)], own.at[j], local_sems.at[j])

        def to_chip(q):
            return pltpu.make_async_remote_copy(src_ref=snd.at[q], dst_ref=got.at[q], send_sem=b_send.at[q],
                                                recv_sem=b_recv.at[q], device_id=(*rel_chips[q], c_), device_id_type=MESH)

        @pl.when(k == 0)
        def _():
            for j in range(len(ALL_CHIPS)):
                to_sibling(j).start()
                local(j).start()

        @pl.when(k == 2)
        def _():
            for j in range(len(ALL_CHIPS)):
                local(j).wait()
                to_sibling(j).wait_recv()
            for q in range(3):
                j = 2 * rel_chips[q][0] + rel_chips[q][1]
                snd[q] = _bf(own[j] + rcv[j])
                to_chip(q).start()
            jm = 2 * x_ + y_
            tot[...] = own[jm] + rcv[jm]

        @pl.when(k == N_CH)
        def _():
            g = tot[...]
            for q in range(3):
                to_chip(q).wait_recv()
                g = g + got[q].astype(F32)
            gout_o[...] = g
            for j in range(len(ALL_CHIPS)):
                to_sibling(j).wait_send()
            for q in range(3):
                to_chip(q).wait_send()

    dma = pltpu.SemaphoreType.DMA
    return _Part(
        body, inputs=[gwo], in_specs=[pl.BlockSpec(memory_space=pl.ANY)],
        out_specs=[pl.BlockSpec((WOUT_BLK, D), lambda k: (0, 0))], out_shape=[jax.ShapeDtypeStruct((WOUT_BLK, D), F32)],
        scratch_shapes=[pltpu.VMEM((4, WOUT_BLK, D), F32), pltpu.VMEM((4, WOUT_BLK, D), F32),
                        pltpu.VMEM((3, WOUT_BLK, D), BF16), pltpu.VMEM((3, WOUT_BLK, D), BF16),
                        pltpu.VMEM((WOUT_BLK, D), F32), dma((4,)), dma((4,)), dma((4,)), dma((3,)), dma((3,))])


def _in_proj_bwd_rs(d_ret, d_swa, h, wt, x, rstd, norm_w, dy, sse, small_g):
    n_blocks, n_tiles = N_DEV, T // TM
    last = n_blocks + n_tiles - 1

    def body(da_any, db_any, da_ref, db_ref, h_ref, w_ref, x_ref, r_ref, nw_ref, dy_ref, sse_ref,
             rw_ref, qw_ref, kw_ref, sk_ref, rb_ref, gx_ref, gin_o, suma_o, sumb_o,
             stage, d2d_src, rcv, snd, got, tot, tab_a, tab_b, gnw_acc,
             dp_sems, d2d_send, d2d_recv, ici_send, ici_recv, s_send, s_recv):
        k = pl.program_id(0)
        x_, y_, c_ = _mesh_pos()
        me = _blk(x_, y_, c_)
        sibling = (x_, y_, 1 - c_)
        rel_chips = [(1 - x_, y_), (x_, 1 - y_), (1 - x_, 1 - y_), (x_, y_)]

        def block_of_step(s):
            return _blk(*rel_chips[s // 2], 1 - c_ if s % 2 == 0 else c_)

        def fetch(s, wait):
            slot, b = s % 2, block_of_step(s)
            split = RET_W - 4 * WIN_BLK

            def run(src, dst, sem):
                cp = pltpu.make_async_copy(src, dst, sem)
                cp.wait() if wait else cp.start()

            @pl.when(b < 4)
            def _():
                run(da_any.at[pl.ds(pl.multiple_of(b * WIN_BLK, 16), WIN_BLK), :], stage.at[slot], dp_sems.at[slot, 0])

            @pl.when(b == 4)
            def _():
                run(da_any.at[pl.ds(4 * WIN_BLK, split), :], stage.at[slot, pl.ds(0, split), :], dp_sems.at[slot, 0])
                run(db_any.at[pl.ds(0, WIN_BLK - split), :], stage.at[slot, pl.ds(split, WIN_BLK - split), :],
                    dp_sems.at[slot, 1])

            @pl.when(b > 4)
            def _():
                run(db_any.at[pl.ds(pl.multiple_of(b * WIN_BLK - RET_W, 16), WIN_BLK), :], stage.at[slot],
                    dp_sems.at[slot, 0])

        def d2d_copy(r):
            return pltpu.make_async_remote_copy(src_ref=d2d_src, dst_ref=rcv.at[r], send_sem=d2d_send.at[r],
                                                recv_sem=d2d_recv.at[r], device_id=sibling, device_id_type=MESH)

        def ici_copy(r):
            return pltpu.make_async_remote_copy(src_ref=snd.at[r], dst_ref=got.at[r], send_sem=ici_send.at[r],
                                                recv_sem=ici_recv.at[r], device_id=(*rel_chips[r], c_),
                                                device_id_type=MESH)

        def table_copies():
            return [pltpu.make_async_remote_copy(src_ref=tab.at[me], dst_ref=tab.at[me], send_sem=s_send.at[a, p - 1],
                                                 recv_sem=s_recv.at[a, p - 1],
                                                 device_id=(x_ ^ (p >> 2), y_ ^ ((p >> 1) & 1), c_ ^ (p & 1)),
                                                 device_id_type=MESH)
                    for p in range(1, N_DEV) for a, tab in enumerate((tab_a, tab_b))]

        for s in range(n_blocks):
            @pl.when(k == s)
            def _(s=s):
                r = s // 2
                if s == 0:
                    gnw_acc[...] = jnp.zeros_like(gnw_acc)
                    fetch(0, wait=False)
                fetch(s, wait=True)
                if s + 1 < n_blocks:
                    fetch(s + 1, wait=False)
                gw = _dot(stage[s % 2], h_ref[...])
                if s % 2 == 0:
                    if s >= 2:
                        d2d_copy(r - 1).wait_send()
                    d2d_src[...] = gw
                    d2d_copy(r).start()
                else:
                    d2d_copy(r).wait_recv()
                    chip_sum = gw + rcv[r]
                    if r < 3:
                        snd[r] = _bf(chip_sum)
                        ici_copy(r).start()
                    else:
                        tot[...] = chip_sum

        @pl.when(k >= n_blocks)
        def _():
            dp = jnp.concatenate([da_ref[...], db_ref[...]], axis=0)
            xv, r, nw = x_ref[...], r_ref[...], nw_ref[...]
            dh = lax.dot_general(dp, w_ref[...], (((0,), (0,)), ((), ())), preferred_element_type=F32)
            u = dh * nw
            gx_ref[...] = dy_ref[...] + r * u - xv * (r * r * r) * jnp.mean(u * xv, axis=-1, keepdims=True)
            gnw_acc[...] += jnp.sum(dh * (xv * r), axis=0, keepdims=True)

        @pl.when(k == last)
        def _():
            mine_a, mine_b = tab_a.at[me], tab_b.at[me]
            mine_a[...] = jnp.zeros_like(mine_a)
            mine_b[...] = jnp.zeros_like(mine_b)
            for row, ref in enumerate((gnw_acc, rw_ref, qw_ref, kw_ref, sk_ref)):
                mine_a[row:row + 1, 0:ref.shape[1]] = ref[...]
            mine_a[LOSS_ROW:LOSS_ROW + 1, 0:1] = sse_ref[...]
            mine_b[:, 0:N_BUCKETS] = rb_ref[...]
            tables = table_copies()
            for cp in tables:
                cp.start()
            g_in = tot[...]
            for q in range(3):
                ici_copy(q).wait_recv()
                g_in = g_in + got[q].astype(F32)
            gin_o[...] = g_in
            for cp in tables:
                cp.wait_recv()
            sum_a, sum_b = tab_a[0], tab_b[0]
            for b in range(1, N_DEV):
                sum_a = sum_a + tab_a[b]
                sum_b = sum_b + tab_b[b]
            suma_o[...] = sum_a
            sumb_o[...] = sum_b
            d2d_copy(3).wait_send()
            for q in range(3):
                ici_copy(q).wait_send()
            for cp in tables:
                cp.wait_send()

    tile = lambda k: jnp.maximum(k - n_blocks, 0)
    row = lambda w: pl.BlockSpec((TM, w), lambda k: (tile(k), 0))
    col = lambda w: pl.BlockSpec((w, TM), lambda k: (0, tile(k)))
    const = lambda shape, **kw: pl.BlockSpec(shape, lambda k: (0,) * len(shape), **kw)
    once = dict(pipeline_mode=pl.Buffered(1))
    hbm = pl.BlockSpec(memory_space=pl.ANY)
    dma = pltpu.SemaphoreType.DMA
    return pl.pallas_call(
        body, name="in_proj_bwd_rs", grid=(n_blocks + n_tiles,),
        in_specs=[hbm, hbm, col(RET_W), col(SWA_W), const((T, D), **once), const((D_IN, D), **once),
                  row(D), row(1), const((1, D)), row(D), const((1, 1))]
                 + [const(g.shape) for g in small_g],
        out_specs=(row(D), const((WIN_BLK, D)), const((8, D)), const((SWA_H, CH))),
        out_shape=(jax.ShapeDtypeStruct((T, D), F32), jax.ShapeDtypeStruct((WIN_BLK, D), F32),
                   jax.ShapeDtypeStruct((8, D), F32), jax.ShapeDtypeStruct((SWA_H, CH), F32)),
        scratch_shapes=[
            pltpu.VMEM((2, WIN_BLK, T), BF16), pltpu.VMEM((WIN_BLK, D), F32),
            pltpu.VMEM((4, WIN_BLK, D), F32), pltpu.VMEM((3, WIN_BLK, D), BF16),
            pltpu.VMEM((3, WIN_BLK, D), BF16), pltpu.VMEM((WIN_BLK, D), F32),
            pltpu.VMEM((N_DEV, 8, D), F32), pltpu.VMEM((N_DEV, SWA_H, CH), F32),
            pltpu.VMEM((1, D), F32),
            dma((2, 2)), dma((4,)), dma((4,)), dma((3,)), dma((3,)), dma((2, 7)), dma((2, 7)),
        ],
        compiler_params=_cparams(1),
    )(d_ret, d_swa, d_ret, d_swa, h, wt, x, rstd, norm_w, dy, sse, *small_g)


SMALL_SHAPES = ((1, D), (1, 512), (1, SWA_D), (1, SWA_D), (1, SWA_H), (SWA_H, N_BUCKETS))


def _adamw_all(g_in_t, g_out, sum_a, sum_b, w_in_t, m_in_t, v_in_t, w_out, m_out, v_out, small_w, small_m, small_v):
    n_small = len(SMALL_SHAPES)
    halves = 2

    def body(*refs):
        gin_ref, gout_ref, suma_ref, sumb_ref, w_in_ref, m_in_ref, v_in_ref, w_out_ref, m_out_ref, v_out_ref = refs[:10]
        pos = 10
        sw_refs, sm_refs, sv_refs = (refs[pos + i * n_small:pos + (i + 1) * n_small] for i in range(3))
        pos += 3 * n_small
        g_in_o, d_in_o, nm_in_o, nv_in_o, g_out_o, d_out_o, nm_out_o, nv_out_o = refs[pos:pos + 8]
        pos += 8
        sg_o, sd_o, snm_o, snv_o = (refs[pos + i * n_small:pos + (i + 1) * n_small] for i in range(4))
        loss_o = refs[pos + 4 * n_small]

        g = gout_ref[...]
        d, nm, nv = _adamw(w_out_ref[...], g, m_out_ref[...], v_out_ref[...])
        g_out_o[...], d_out_o[...], nm_out_o[...], nv_out_o[...] = g, d, nm, nv
        g = gin_ref[...]
        d, nm, nv = _adamw(w_in_ref[...], g, m_in_ref[...], v_in_ref[...])
        g_in_o[...], d_in_o[...], nm_in_o[...], nv_in_o[...] = g, d, nm, nv

        @pl.when(pl.program_id(0) == 0)
        def _():
            loss_o[...] = suma_ref[LOSS_ROW:LOSS_ROW + 1, 0:1] * (0.5 / D)
            for r, (rows, lanes) in enumerate(SMALL_SHAPES):
                g = suma_ref[r:r + 1, 0:lanes] if rows == 1 else sumb_ref[:, 0:lanes]
                d, nm, nv = _adamw(sw_refs[r][...], g, sm_refs[r][...], sv_refs[r][...])
                sg_o[r][...], sd_o[r][...], snm_o[r][...], snv_o[r][...] = g, d, nm, nv

    half = lambda rows: pl.BlockSpec((rows // halves, D), lambda i: (i, 0))
    const = lambda shape: pl.BlockSpec(shape, lambda i: (0,) * len(shape))
    win = jax.ShapeDtypeStruct((WIN_BLK, D), F32)
    wout = jax.ShapeDtypeStruct((WOUT_BLK, D), F32)
    smalls = tuple(jax.ShapeDtypeStruct(s, F32) for s in SMALL_SHAPES)
    small_specs = [const(s) for s in SMALL_SHAPES]
    outs = pl.pallas_call(
        body, name="adamw_all", grid=(halves,),
        in_specs=[half(WIN_BLK), half(WOUT_BLK), const((8, D)), const((SWA_H, CH))] + [half(WIN_BLK)] * 3
                 + [half(WOUT_BLK)] * 3 + small_specs * 3,
        out_specs=tuple([half(WIN_BLK)] * 4 + [half(WOUT_BLK)] * 4 + small_specs * 4 + [const((1, 1))]),
        out_shape=(win,) * 4 + (wout,) * 4 + smalls * 4 + (jax.ShapeDtypeStruct((1, 1), F32),),
        compiler_params=_cparams(1, vmem=32 * 1024 * 1024),
    )(g_in_t, g_out, sum_a, sum_b, w_in_t, m_in_t, v_in_t, w_out, m_out, v_out, *small_w, *small_m, *small_v)
    big, rest = outs[:8], outs[8:]
    return big, [rest[i * n_small:(i + 1) * n_small] for i in range(4)], rest[4 * n_small]


def _small_rows(norm_w, ret_norm_w, q_norm_w, k_norm_w, sinks, rel_bias):
    return (norm_w.reshape(1, D), ret_norm_w.reshape(1, 512), q_norm_w.reshape(1, SWA_D), k_norm_w.reshape(1, SWA_D),
            sinks.reshape(1, SWA_H), rel_bias.T)


def _small_leaves(rows):
    return (rows[0].reshape(D), rows[1].reshape(512), rows[2].reshape(SWA_D), rows[3].reshape(SWA_D),
            rows[4].reshape(SWA_H), rows[5].T)


def kernel(x, norm_w, w_in, ret_norm_w, q_norm_w, k_norm_w, sinks, rel_bias, w_out, loss_target, m_norm_w, m_w_in, m_ret_norm_w, m_q_norm_w, m_k_norm_w, m_sinks, m_rel_bias, m_w_out, v_norm_w, v_w_in, v_ret_norm_w, v_q_norm_w, v_k_norm_w, v_sinks, v_rel_bias, v_w_out):
    x2 = x.reshape(T, D)
    target = loss_target.reshape(T, D)
    nw = norm_w.reshape(1, D)
    rnw_col = ret_norm_w.reshape(RET_V, 1)
    qnw_col = q_norm_w.reshape(SWA_D, 1)
    knw_col = k_norm_w.reshape(SWA_D, 1)
    relb = rel_bias.T.reshape(SWA_H * N_BUCKETS)
    ret_tables = _rotary_tables_t() + _retention_tables_t()
    bucket_t = _bucket_table_t()

    proj, wt, rstd, h = _in_proj_gather(x2, nw, w_in.T)
    (ro, mix_r, states, wo), (so, mix_s) = _fused_call("attn_fwd", N_CH, [
        _retention_fwd(proj, rnw_col, ret_tables, w_out),
        _swa_fwd(proj, qnw_col, knw_col, sinks, relb, bucket_t)])
    sse, dy, dmix_r, dmix_s, gwo = _out_proj(mix_r, mix_s, wo.reshape(D, D), x2, target)

    (d_ret, drw_acc), (d_swa, dbias, dsink_acc, dqw_acc, dkw_acc), (g_out,) = _fused_call("attn_bwd", N_CH + 1, [
        _retention_bwd(proj, ro, states, dmix_r, rnw_col, ret_tables),
        _swa_bwd(proj, so, dmix_s, qnw_col, knw_col, sinks, relb, bucket_t),
        _w_out_reduce_scatter(gwo.reshape(N_DEV, WOUT_BLK, D))])
    d_relb, d_sinks, d_qnw, d_knw, d_rnw = _small_grads(dbias, bucket_t, dsink_acc, dqw_acc, dkw_acc, drw_acc)
    grad_x, g_in_t, sum_a, sum_b = _in_proj_bwd_rs(d_ret, d_swa, h, wt, x2, rstd, nw, dy, sse,
                                                   (d_rnw, d_qnw, d_knw, d_sinks, d_relb))

    small_w = _small_rows(norm_w, ret_norm_w, q_norm_w, k_norm_w, sinks, rel_bias)
    small_m = _small_rows(m_norm_w, m_ret_norm_w, m_q_norm_w, m_k_norm_w, m_sinks, m_rel_bias)
    small_v = _small_rows(v_norm_w, v_ret_norm_w, v_q_norm_w, v_k_norm_w, v_sinks, v_rel_bias)
    big, small, loss = _adamw_all(g_in_t, g_out, sum_a, sum_b, w_in.T, m_w_in.T, v_w_in.T, w_out, m_w_out, v_w_out,
                                  small_w, small_m, small_v)

    def leaves(i):
        a = _small_leaves(small[i])
        return (a[0], big[i].T, a[1], a[2], a[3], a[4], a[5], big[4 + i])

    return (loss.reshape(()), grad_x.reshape(1, T, D), *leaves(0), *leaves(1), *leaves(2), *leaves(3))
```

```python
from typing import Callable, NamedTuple

import numpy as np
import jax
import jax.numpy as jnp
from jax import lax
from jax.experimental import pallas as pl
from jax.experimental.pallas import tpu as pltpu

F32 = jnp.float32
BF16 = jnp.bfloat16
MESH = pl.DeviceIdType.MESH

T = 2048
D = 1024
D_IN = 2816
N_DEV = 8
WIN_BLK = D_IN // N_DEV
WOUT_BLK = D // N_DEV
CH = 128
N_CH = T // CH
RET_H, RET_DK, RET_DV = 4, 64, 128
RET_QK = RET_H * RET_DK
RET_V = RET_H * RET_DV
SWA_H, SWA_KV, SWA_D, SWA_G = 8, 2, 64, 4
N_BUCKETS = 32
NORM_EPS = 1e-6
GN_EPS = 1e-5
NEG_INF = -1e30
PIECES = (256, 256, 512, 512, 512, 128, 128, 512)
OFFS = tuple(int(v) for v in np.cumsum((0,) + PIECES))
RET_W = OFFS[4]
SWA_W = D_IN - RET_W
SWA_OFFS = tuple(o - RET_W for o in OFFS[4:])
TM = 256

ADAM_LR, ADAM_B1, ADAM_B2, ADAM_EPS, ADAM_WD, ADAM_STEP = 0.001, 0.9, 0.999, 1e-08, 0.01, 10

VMEM_LIMIT = 56 * 1024 * 1024


def _cparams(n_grid=0, vmem=VMEM_LIMIT):
    sem = ("arbitrary",) * n_grid if n_grid else None
    return pltpu.CompilerParams(dimension_semantics=sem, vmem_limit_bytes=vmem)


class _Part(NamedTuple):
    step: Callable
    inputs: list
    in_specs: list
    out_specs: list
    out_shape: list
    scratch_shapes: list


def _fused_call(name, n_steps, parts):
    n_in = [len(p.inputs) for p in parts]
    n_out = [len(p.out_shape) for p in parts]
    n_scr = [len(p.scratch_shapes) for p in parts]

    def body(*refs):
        ins, outs, scr = refs[:sum(n_in)], refs[sum(n_in):sum(n_in) + sum(n_out)], refs[sum(n_in) + sum(n_out):]
        for i, p in enumerate(parts):
            take = lambda seq, counts: seq[sum(counts[:i]):sum(counts[:i + 1])]
            p.step(*take(ins, n_in), *take(outs, n_out), *take(scr, n_scr))

    flat = lambda field: [v for p in parts for v in getattr(p, field)]
    outs = pl.pallas_call(
        body, name=name, grid=(n_steps,), in_specs=flat("in_specs"), out_specs=tuple(flat("out_specs")),
        out_shape=tuple(flat("out_shape")), scratch_shapes=flat("scratch_shapes"), compiler_params=_cparams(1),
    )(*flat("inputs"))
    return [list(outs[sum(n_out[:i]):sum(n_out[:i + 1])]) for i in range(len(parts))]


def _dot(a, b):
    return jnp.dot(a, b, preferred_element_type=F32)


def _dot_nt(a, b):
    return lax.dot_general(a, b, (((1,), (1,)), ((), ())), preferred_element_type=F32)


def _bf(a):
    return a.astype(BF16)


def _sigmoid(x):
    return 1.0 / (1.0 + jnp.exp(-x))


def _adamw(w, g, m, v):
    m = ADAM_B1 * m + (1.0 - ADAM_B1) * g
    v = ADAM_B2 * v + (1.0 - ADAM_B2) * (g * g)
    m_hat = m / (1.0 - ADAM_B1 ** ADAM_STEP)
    v_hat = v / (1.0 - ADAM_B2 ** ADAM_STEP)
    delta = -ADAM_LR * (m_hat / (jnp.sqrt(v_hat) + ADAM_EPS) + ADAM_WD * w)
    return delta, m, v


def _rotary_tables_t():
    half = RET_DK // 2
    inv_freq = np.float32(10000.0) ** (-np.arange(half, dtype=np.float32) / np.float32(half))
    ang = inv_freq[:, None] * np.arange(T, dtype=np.float32)[None, :]
    cos, sin = np.cos(ang).astype(np.float32), np.sin(ang).astype(np.float32)
    cos64 = np.concatenate([cos, cos], axis=0)
    sin64 = np.concatenate([-sin, sin], axis=0)
    return np.tile(cos64, (RET_H, 1)), np.tile(sin64, (RET_H, 1))


def _retention_tables_t():
    gamma = (1.0 - np.exp2(-5.0 - np.arange(RET_H, dtype=np.float32))).astype(np.float32)
    log_g = np.log(gamma).astype(np.float32)
    i = np.arange(CH, dtype=np.float32)
    diff = i[None, :] - i[:, None]
    decay = np.where(diff >= 0, np.exp(log_g[:, None, None] * np.maximum(diff, 0.0)), 0.0).astype(np.float32)
    decay_all = np.concatenate(list(decay), axis=1)
    zeta = np.exp(log_g[:, None] * (CH - 1.0 - i)).astype(np.float32)
    zeta_tab = np.repeat(zeta.T, RET_DK, axis=1)
    xi = np.exp(log_g[:, None] * (i + 1.0)).astype(np.float32)
    xi_tab = np.repeat(xi, RET_DK, axis=0)
    chunk_decay = np.exp(log_g * np.float32(CH)).astype(np.float32)
    row_head = np.arange(RET_V)[:, None] // RET_DV
    col_head = np.arange(RET_QK)[None, :] // RET_DK
    state_mask = (row_head == col_head).astype(np.float32)
    state_decay = (state_mask * chunk_decay[row_head]).astype(np.float32)
    q_mask = (np.arange(RET_QK)[:, None] // RET_DK == np.arange(RET_H * CH)[None, :] // CH).astype(np.float32)
    return (np.ascontiguousarray(decay_all), np.ascontiguousarray(zeta_tab), np.ascontiguousarray(xi_tab),
            state_mask, state_decay, q_mask)


def _bucket_table_t():
    qi = np.arange(CH)[:, None]
    kj = np.arange(2 * CH)[None, :]
    dist = qi + CH - kj
    n = np.maximum(dist, 0)
    max_exact = N_BUCKETS // 2
    nf = np.maximum(n, 1).astype(np.float32)
    large = max_exact + (np.log(nf / np.float32(max_exact)) / np.float32(np.log(CH / max_exact))
                         * np.float32(N_BUCKETS - max_exact)).astype(np.int32)
    large = np.minimum(large, N_BUCKETS - 1)
    bucket = np.where(n < max_exact, n, large)
    return np.ascontiguousarray(np.where((dist >= 0) & (dist < CH), bucket, -1).astype(np.int32).T)


def _mesh_pos():
    return lax.axis_index("x"), lax.axis_index("y"), lax.axis_index("c")


def _blk(px, py, pc):
    return 4 * px + 2 * py + pc


def _in_proj_gather(x, norm_w, w_in_t):
    tchunk = 512

    def body(x_ref, nw_ref, win_ref, proj_ref, wt_ref, rstd_ref, h_ref, stage, send_sems, recv_sems, out_sems):
        x, y, c = _mesh_pos()
        me = _blk(x, y, c)
        sibling = (x, y, 1 - c)
        chips = [(1 - x, y), (x, 1 - y), (1 - x, 1 - y)]

        def copy(k, b, to):
            return pltpu.make_async_remote_copy(src_ref=wt_ref.at[b], dst_ref=wt_ref.at[b], send_sem=send_sems.at[k],
                                                recv_sem=recv_sems.at[k], device_id=to, device_id_type=MESH)

        wt_ref[me] = _bf(win_ref[...])
        first = [copy(0, me, sibling)] + [copy(1 + j, me, (*chip, c)) for j, chip in enumerate(chips)]
        for cp in first:
            cp.start()

        nw = nw_ref[...]
        for i in range(T // TM):
            rows = slice(i * TM, (i + 1) * TM)
            xv = x_ref[rows, :]
            r = lax.rsqrt(jnp.mean(xv * xv, axis=-1, keepdims=True) + NORM_EPS)
            h_ref[rows, :] = _bf(xv * r * nw)
            rstd_ref[rows, :] = r

        writes = []

        def project(b):
            k = len(writes)
            if k >= 2:
                writes[k - 2].wait()
            w = wt_ref[b]
            for t in range(T // tchunk):
                cols = slice(t * tchunk, (t + 1) * tchunk)
                stage[k % 2, :, cols] = _dot_nt(w, h_ref[cols, :])
            cp = pltpu.make_async_copy(stage.at[k % 2], proj_ref.at[b], out_sems.at[k % 2])
            cp.start()
            writes.append(cp)

        project(me)
        copy(0, _blk(x, y, 1 - c), (x, y, c)).wait_recv()
        project(_blk(x, y, 1 - c))
        passed = []
        for j, chip in enumerate(chips):
            copy(1 + j, _blk(*chip, c), (x, y, c)).wait_recv()
            fwd = copy(4 + j, _blk(*chip, c), sibling)
            fwd.start()
            passed.append(fwd)
            project(_blk(*chip, c))
            copy(4 + j, _blk(*chip, 1 - c), (x, y, c)).wait_recv()
            project(_blk(*chip, 1 - c))
        writes[-2].wait()
        writes[-1].wait()
        for cp in first + passed:
            cp.wait_send()

    vm = pl.BlockSpec(memory_space=pltpu.VMEM)
    proj, wt, rstd, h = pl.pallas_call(
        body, name="in_proj_gather",
        out_shape=(jax.ShapeDtypeStruct((N_DEV, WIN_BLK, T), F32), jax.ShapeDtypeStruct((N_DEV, WIN_BLK, D), BF16),
                   jax.ShapeDtypeStruct((T, 1), F32), jax.ShapeDtypeStruct((T, D), BF16)),
        in_specs=[vm, vm, vm], out_specs=(pl.BlockSpec(memory_space=pl.ANY), vm, vm, vm),
        scratch_shapes=[pltpu.VMEM((2, WIN_BLK, T), F32),
                        pltpu.SemaphoreType.DMA((7,)), pltpu.SemaphoreType.DMA((7,)), pltpu.SemaphoreType.DMA((2,))],
        compiler_params=_cparams(vmem=48 * 1024 * 1024),
    )(x, norm_w, w_in_t)
    return proj.reshape(D_IN, T), wt.reshape(D_IN, D), rstd, h


def _gather_w_out(w_out_ref, own_ref, wo_ref, local_sem, send_sems, recv_sems, phase):
    x, y, c = _mesh_pos()
    me = _blk(x, y, c)
    sibling = (x, y, 1 - c)
    chips = [(1 - x, y), (x, 1 - y), (1 - x, 1 - y)]

    def copy(k, b, to, src=None):
        return pltpu.make_async_remote_copy(src_ref=wo_ref.at[b] if src is None else src, dst_ref=wo_ref.at[b],
                                            send_sem=send_sems.at[k], recv_sem=recv_sems.at[k], device_id=to,
                                            device_id_type=MESH)

    local = pltpu.make_async_copy(own_ref, wo_ref.at[me], local_sem)
    first = [copy(0, me, sibling, own_ref)] + [copy(1 + j, me, (*chip, c), own_ref) for j, chip in enumerate(chips)]
    passed = [copy(4 + j, _blk(*chip, c), sibling) for j, chip in enumerate(chips)]
    if phase == 0:
        own_ref[...] = _bf(w_out_ref[...])
        local.start()
        for cp in first:
            cp.start()
    elif phase == 1:
        for j, chip in enumerate(chips):
            copy(1 + j, _blk(*chip, c), (x, y, c)).wait_recv()
            passed[j].start()
    else:
        local.wait()
        copy(0, _blk(x, y, 1 - c), (x, y, c)).wait_recv()
        for j, chip in enumerate(chips):
            copy(4 + j, _blk(*chip, 1 - c), (x, y, c)).wait_recv()
        for cp in first + passed:
            cp.wait_send()


def _proj_piece(i, block_of):
    rows = PIECES[i]
    assert OFFS[i] % rows == 0
    return pl.BlockSpec((rows, CH), lambda n: (OFFS[i] // rows, block_of(n)))


def _proj_gate_halves(block_of):
    return [pl.BlockSpec((256, CH), lambda n, j=j: (OFFS[7] // 256 + j, block_of(n))) for j in range(2)]


def _swap_halves_t(t):
    half = RET_DK // 2
    parts = []
    for h in range(RET_H):
        parts += [t[h * RET_DK + half:(h + 1) * RET_DK], t[h * RET_DK:h * RET_DK + half]]
    return jnp.concatenate(parts, axis=0)


def _rotate(t, cos, sin):
    return t * cos + _swap_halves_t(t) * sin


def _group_norm_t(o):
    mu = jnp.mean(o, axis=0, keepdims=True)
    var = jnp.mean((o - mu) * (o - mu), axis=0, keepdims=True)
    rstd = lax.rsqrt(var + GN_EPS)
    return (o - mu) * rstd, rstd


def _retention_scores_t(q_t, k_rm_b, q_mask, decay_all):
    q_heads = _bf(jnp.concatenate([q_t] * RET_H, axis=1) * q_mask)
    return _dot(k_rm_b, q_heads) * decay_all


def _retention_fwd(proj, ret_norm_w, tables, w_out_blk):
    cos_t, sin_t, decay_all, zeta_tab, xi_tab, state_mask, state_decay, q_mask = tables

    def body(rq_ref, rk_ref, rv_ref, rg_ref, w_ref, cos_ref, sin_ref, decay_ref, zeta_ref, xi_ref, smask_ref,
             sdecay_ref, qmask_ref, wout_ref, ro_ref, mix_ref, st_ref, wo_ref, state, wo_own, wo_local, wo_send, wo_recv):
        n = pl.program_id(0)

        @pl.when(n == 0)
        def _():
            state[...] = jnp.zeros_like(state)
            _gather_w_out(wout_ref, wo_own, wo_ref, wo_local, wo_send, wo_recv, phase=0)

        @pl.when(n == N_CH // 2)
        def _():
            _gather_w_out(wout_ref, wo_own, wo_ref, wo_local, wo_send, wo_recv, phase=1)

        @pl.when(n == N_CH - 1)
        def _():
            _gather_w_out(wout_ref, wo_own, wo_ref, wo_local, wo_send, wo_recv, phase=2)

        cos, sin = cos_ref[...], sin_ref[...]
        q_t = _rotate(rq_ref[...], cos, sin)
        k_t = _rotate(rk_ref[...], cos, sin) * (RET_DK ** -0.5)
        k_rm = k_t.T
        v_b = _bf(rv_ref[...])
        m_b = _bf(state[...])
        st_ref[0] = m_b
        scores_b = _bf(_retention_scores_t(q_t, _bf(k_rm), qmask_ref[...], decay_ref[...]))
        cross = _dot(m_b, _bf(q_t * xi_ref[...]))
        state[...] = state[...] * sdecay_ref[...] + _dot(v_b, _bf(k_rm * zeta_ref[...])) * smask_ref[...]
        heads = range(RET_H)
        rows = [slice(h * RET_DV, (h + 1) * RET_DV) for h in heads]
        o = [_dot(v_b[rows[h]], scores_b[:, h * CH:(h + 1) * CH]) + cross[rows[h]] for h in heads]
        for h in heads:
            rn, _ = _group_norm_t(o[h])
            g = rg_ref[rows[h], :]
            ro_ref[rows[h], :] = o[h]
            mix_ref[rows[h], :] = rn * w_ref[rows[h], :] * (g * _sigmoid(g))

    col = lambda w: pl.BlockSpec((w, CH), lambda n: (0, n))
    const = lambda shape: pl.BlockSpec(shape, lambda n: (0,) * len(shape))
    cur = lambda n: n
    return _Part(
        body,
        inputs=[proj, proj, proj, proj, ret_norm_w, cos_t, sin_t, decay_all, zeta_tab, xi_tab, state_mask, state_decay,
                q_mask, w_out_blk],
        in_specs=[_proj_piece(i, cur) for i in range(4)] + [
            const((RET_V, 1)), col(RET_QK), col(RET_QK),
            const(decay_all.shape), const(zeta_tab.shape), const(xi_tab.shape), const(state_mask.shape),
            const(state_decay.shape), const(q_mask.shape), const((WOUT_BLK, D))],
        out_specs=[col(RET_V), col(RET_V), pl.BlockSpec((1, RET_V, RET_QK), lambda n: (n, 0, 0)),
                   pl.BlockSpec(memory_space=pl.ANY)],
        out_shape=[jax.ShapeDtypeStruct((RET_V, T), F32), jax.ShapeDtypeStruct((RET_V, T), F32),
                   jax.ShapeDtypeStruct((N_CH, RET_V, RET_QK), BF16), jax.ShapeDtypeStruct((N_DEV, WOUT_BLK, D), BF16)],
        scratch_shapes=[pltpu.VMEM((RET_V, RET_QK), F32), pltpu.VMEM((WOUT_BLK, D), BF16), pltpu.SemaphoreType.DMA,
                        pltpu.SemaphoreType.DMA((N_DEV - 1,)), pltpu.SemaphoreType.DMA((N_DEV - 1,))])


def _retention_bwd(proj, ro, states, dmix, ret_norm_w, tables):
    cos_t, sin_t, decay_all, zeta_tab, xi_tab, state_mask, state_decay, q_mask = tables

    def body(*refs):
        @pl.when(pl.program_id(0) < N_CH)
        def _():
            chunk(*refs)

    def chunk(rq_ref, rk_ref, rv_ref, rg_ref, ro_ref, st_ref, dm_ref, w_ref, cos_ref, sin_ref, decay_ref, zeta_ref,
              xi_ref, smask_ref, sdecay_ref, qmask_ref, d_ref, dw_ref, gstate):
        i = pl.program_id(0)

        @pl.when(i == 0)
        def _():
            gstate[...] = jnp.zeros_like(gstate)
            dw_ref[...] = jnp.zeros_like(dw_ref)

        cos, sin = cos_ref[...], sin_ref[...]
        q_t = _rotate(rq_ref[...], cos, sin)
        k_t = _rotate(rk_ref[...], cos, sin) * (RET_DK ** -0.5)
        q_b, k_b = _bf(q_t), _bf(k_t)
        k_rm = k_t.T
        kz_b = _bf(k_rm * zeta_ref[...])
        qx_b = _bf(q_t * xi_ref[...])
        v_t = rv_ref[...]
        v_b = _bf(v_t)
        v_rm_b = _bf(v_t.T)
        decay = decay_ref[...]
        scores_b = _bf(_retention_scores_t(q_t, _bf(k_rm), qmask_ref[...], decay))
        heads = range(RET_H)
        rows = [slice(h * RET_DV, (h + 1) * RET_DV) for h in heads]
        qk_rows = [slice(h * RET_DK, (h + 1) * RET_DK) for h in heads]
        lanes = [slice(h * CH, (h + 1) * CH) for h in heads]
        do = []
        for h in heads:
            g, w, dm = rg_ref[rows[h], :], w_ref[rows[h], :], dm_ref[rows[h], :]
            rn, rstd = _group_norm_t(ro_ref[rows[h], :])
            sig = _sigmoid(g)
            silu = g * sig
            d_ref[OFFS[3] + h * RET_DV:OFFS[3] + (h + 1) * RET_DV, :] = _bf(dm * rn * w * (sig * (1.0 + g * (1.0 - sig))))
            dw_ref[rows[h], :] += dm * silu * rn
            drn = dm * silu * w
            do.append(rstd * (drn - jnp.mean(drn, axis=0, keepdims=True)
                              - rn * jnp.mean(drn * rn, axis=0, keepdims=True)))
        do_b = _bf(jnp.concatenate(do, axis=0))
        m_b = st_ref[0]
        g_all = gstate[...]
        g_b = _bf(g_all)
        dscores_b = [_bf(_dot(v_rm_b[:, rows[h]], do_b[rows[h]]) * decay[:, lanes[h]]) for h in heads]
        dq_cross = lax.dot_general(m_b, do_b, (((0,), (0,)), ((), ())), preferred_element_type=F32)
        dkz = _dot(v_rm_b, g_b)
        dv_cross = _dot_nt(g_b, kz_b)
        gstate[...] = g_all * sdecay_ref[...] + _dot_nt(do_b, qx_b) * smask_ref[...]
        dq = jnp.concatenate([_dot(k_b[qk_rows[h]], dscores_b[h]) for h in heads], axis=0) + dq_cross * xi_ref[...]
        dk = (jnp.concatenate([_dot_nt(q_b[qk_rows[h]], dscores_b[h]) for h in heads], axis=0)
              + (dkz * zeta_ref[...]).T) * (RET_DK ** -0.5)
        for h in heads:
            d_ref[OFFS[2] + h * RET_DV:OFFS[2] + (h + 1) * RET_DV, :] = _bf(
                _dot_nt(do_b[rows[h]], scores_b[:, lanes[h]]) + dv_cross[rows[h]])
        d_ref[OFFS[0]:OFFS[1], :] = _bf(dq * cos + _swap_halves_t(dq * sin))
        d_ref[OFFS[1]:OFFS[2], :] = _bf(dk * cos + _swap_halves_t(dk * sin))

    chunk_of = lambda i: N_CH - 1 - jnp.minimum(i, N_CH - 1)
    col = lambda w: pl.BlockSpec((w, CH), lambda i: (0, chunk_of(i)))
    const = lambda shape: pl.BlockSpec(shape, lambda i: (0,) * len(shape))
    return _Part(
        body,
        inputs=[proj, proj, proj, proj, ro, states, dmix, ret_norm_w, cos_t, sin_t, decay_all, zeta_tab, xi_tab,
                state_mask, state_decay, q_mask],
        in_specs=[_proj_piece(j, chunk_of) for j in range(4)] + [
                  col(RET_V),
                  pl.BlockSpec((1, RET_V, RET_QK), lambda i: (chunk_of(i), 0, 0)), col(RET_V),
                  const((RET_V, 1)), col(RET_QK), col(RET_QK),
                  const(decay_all.shape), const(zeta_tab.shape), const(xi_tab.shape), const(state_mask.shape),
                  const(state_decay.shape), const(q_mask.shape)],
        out_specs=[col(RET_W), const((RET_V, CH))],
        out_shape=[jax.ShapeDtypeStruct((RET_W, T), BF16), jax.ShapeDtypeStruct((RET_V, CH), F32)],
        scratch_shapes=[pltpu.VMEM((RET_V, RET_QK), F32)])


HQ_LANES = SWA_H * CH


def _head_lanes(hq):
    return slice(hq * CH, (hq + 1) * CH)


def _build_bias_t(bucket_ref, relb_ref, sinks_ref, bias_ref, sink_row):
    bt = bucket_ref[...]
    first = lax.broadcasted_iota(jnp.int32, bt.shape, 0) < CH
    for hq in range(SWA_H):
        b = jnp.full(bt.shape, NEG_INF, F32)
        for bk in range(N_BUCKETS):
            b = jnp.where(bt == bk, relb_ref[hq * N_BUCKETS + bk], b)
        bias_ref[0, :, _head_lanes(hq)] = b
        bias_ref[1, :, _head_lanes(hq)] = jnp.where(first, NEG_INF, b)
        sink_row[:, _head_lanes(hq)] = jnp.full((1, CH), sinks_ref[hq], F32)


def _rms_t(t, w_col):
    r = lax.rsqrt(jnp.mean(t * t, axis=0, keepdims=True) + NORM_EPS)
    return t * r * w_col, r


def _rms_t_bwd(dn, t, r, w_col):
    u = dn * w_col
    return r * u - t * (r * r * r) * jnp.mean(u * t, axis=0, keepdims=True), dn * t * r


def _norm_kv(k_t, kw):
    return jnp.concatenate([_rms_t(k_t[g * SWA_D:(g + 1) * SWA_D], kw)[0] for g in range(SWA_KV)], axis=0)


def _kv_slot(a, kv):
    z = jnp.zeros_like(a)
    return jnp.concatenate([a, z] if kv == 0 else [z, a], axis=0)


def _softmax_t(s, sink):
    m = jnp.maximum(jnp.max(s, axis=0, keepdims=True), sink)
    p = jnp.exp(s - m)
    e_sink = jnp.exp(sink - m)
    inv = 1.0 / (jnp.sum(p, axis=0, keepdims=True) + e_sink)
    return p * inv, e_sink * inv


def _gate_rows(sg_halves, hq):
    per_half = SWA_H // 2
    return sg_halves[hq // per_half][(hq % per_half) * SWA_D:(hq % per_half + 1) * SWA_D, :]


def _swa_fwd(proj, q_norm_w, k_norm_w, sinks, rel_bias_flat, bucket_t):
    def body(sq_ref, skp_ref, skc_ref, svp_ref, svc_ref, sg_lo, sg_hi, qw_ref, kw_ref, sinks_ref, relb_ref, bucket_ref,
             so_ref, mix_ref, bias_ref, sink_row):
        n = pl.program_id(0)

        @pl.when(n == 0)
        def _():
            _build_bias_t(bucket_ref, relb_ref, sinks_ref, bias_ref, sink_row)

        var = (n == 0).astype(jnp.int32)
        qw, kw = qw_ref[...], kw_ref[...]
        kn_band = jnp.concatenate([_norm_kv(skp_ref[...], kw), _norm_kv(skc_ref[...], kw)], axis=1)
        kb_rm = _bf(kn_band.T)
        vband = _bf(jnp.concatenate([svp_ref[...], svc_ref[...]], axis=1))
        q_all = jnp.concatenate(
            [_bf(_kv_slot(_rms_t(sq_ref[hq * SWA_D:(hq + 1) * SWA_D, :], qw)[0] * (SWA_D ** -0.5), hq // SWA_G))
             for hq in range(SWA_H)], axis=1)
        probs, _ = _softmax_t(_dot(kb_rm, q_all) + bias_ref[var], sink_row[...])
        probs_b = _bf(probs)
        for kv in range(SWA_KV):
            o = _dot(vband[kv * SWA_D:(kv + 1) * SWA_D], probs_b[:, kv * SWA_G * CH:(kv + 1) * SWA_G * CH])
            for j in range(SWA_G):
                rows = slice((kv * SWA_G + j) * SWA_D, (kv * SWA_G + j + 1) * SWA_D)
                oh = o[:, j * CH:(j + 1) * CH]
                g = _gate_rows((sg_lo, sg_hi), kv * SWA_G + j)
                so_ref[rows, :] = oh
                mix_ref[rows, :] = oh * (g * _sigmoid(g))

    col = lambda w: pl.BlockSpec((w, CH), lambda n: (0, n))
    const = lambda shape: pl.BlockSpec(shape, lambda n: (0,) * len(shape))
    smem = pl.BlockSpec(memory_space=pltpu.SMEM)
    cur = lambda n: n
    prev = lambda n: jnp.maximum(n - 1, 0)
    return _Part(
        body,
        inputs=[proj, proj, proj, proj, proj, proj, proj, q_norm_w, k_norm_w, sinks, rel_bias_flat, bucket_t],
        in_specs=[_proj_piece(4, cur), _proj_piece(5, prev), _proj_piece(5, cur), _proj_piece(6, prev),
                  _proj_piece(6, cur)] + _proj_gate_halves(cur) + [
                  const((SWA_D, 1)), const((SWA_D, 1)), smem, smem, const((2 * CH, CH))],
        out_specs=[col(512), col(512)],
        out_shape=[jax.ShapeDtypeStruct((512, T), F32), jax.ShapeDtypeStruct((512, T), F32)],
        scratch_shapes=[pltpu.VMEM((2, 2 * CH, HQ_LANES), F32), pltpu.VMEM((1, HQ_LANES), F32)])


def _swa_bwd(proj, so, dmix, q_norm_w, k_norm_w, sinks, rel_bias_flat, bucket_t):
    def body(sq_ref, skp_ref, skc_ref, svp_ref, svc_ref, sg_lo, sg_hi, so_ref, dm_ref, qw_ref, kw_ref, sinks_ref,
             relb_ref, bucket_ref, d_ref, dbias_ref, dsink_ref, dqw_ref, dkw_ref,
             bias_ref, sink_row, band_dk, band_dv, carry_dk, carry_dv, hold_dq, hold_dg):
        n = pl.program_id(0)
        slot = n % 2

        @pl.when(n == 0)
        def _():
            _build_bias_t(bucket_ref, relb_ref, sinks_ref, bias_ref, sink_row)
            for ref in (dbias_ref, dsink_ref, dqw_ref, dkw_ref, carry_dk, carry_dv):
                ref[...] = jnp.zeros_like(ref)

        qw, kw = qw_ref[...], kw_ref[...]

        @pl.when(n < N_CH)
        def _():
            var = (n == 0).astype(jnp.int32)
            kn_band = jnp.concatenate([_norm_kv(skp_ref[...], kw), _norm_kv(skc_ref[...], kw)], axis=1)
            kb_rm = _bf(kn_band.T)
            kn_band_s = _bf(kn_band * (SWA_D ** -0.5))
            vband_f = jnp.concatenate([svp_ref[...], svc_ref[...]], axis=1)
            vb_rm = _bf(vband_f.T)
            q_raw, q_rstd, qs_b, do_b = [], [], [], []
            for hq in range(SWA_H):
                rows = slice(hq * SWA_D, (hq + 1) * SWA_D)
                q_t = sq_ref[rows, :]
                qn, rq = _rms_t(q_t, qw)
                g = _gate_rows((sg_lo, sg_hi), hq)
                sig = _sigmoid(g)
                dm = dm_ref[rows, :]
                hold_dg[slot, rows, :] = _bf(dm * so_ref[rows, :] * (sig * (1.0 + g * (1.0 - sig))))
                q_raw.append(q_t)
                q_rstd.append(rq)
                qs_b.append(_bf(qn * (SWA_D ** -0.5)))
                do_b.append(_bf(dm * (g * sig)))
            q_all = jnp.concatenate([_kv_slot(qs_b[hq], hq // SWA_G) for hq in range(SWA_H)], axis=1)
            do_all = jnp.concatenate([_kv_slot(do_b[hq], hq // SWA_G) for hq in range(SWA_H)], axis=1)
            probs, p_sink = _softmax_t(_dot(kb_rm, q_all) + bias_ref[var], sink_row[...])
            dprobs = _dot(vb_rm, do_all)
            t = jnp.sum(probs * dprobs, axis=0, keepdims=True)
            dlog = probs * (dprobs - t)
            dsink_ref[...] += -(p_sink * t)
            dbias_ref[...] += dlog
            dlog_b, probs_b = _bf(dlog), _bf(probs)
            dkn, dvv = [], []
            for kv in range(SWA_KV):
                heads = range(kv * SWA_G, (kv + 1) * SWA_G)
                lanes = slice(kv * SWA_G * CH, (kv + 1) * SWA_G * CH)
                dvv.append(_dot_nt(jnp.concatenate([do_b[hq] for hq in heads], axis=1), probs_b[:, lanes]))
                dkn.append(_dot_nt(jnp.concatenate([qs_b[hq] for hq in heads], axis=1), dlog_b[:, lanes]))
                dqn = _dot(kn_band_s[kv * SWA_D:(kv + 1) * SWA_D], dlog_b[:, lanes])
                for j, hq in enumerate(heads):
                    dq_t, dqw_terms = _rms_t_bwd(dqn[:, j * CH:(j + 1) * CH], q_raw[hq], q_rstd[hq], qw)
                    hold_dq[slot, hq * SWA_D:(hq + 1) * SWA_D, :] = _bf(dq_t)
                    dqw_ref[...] += dqw_terms
            band_dk[...] = jnp.concatenate(dkn, axis=0)
            band_dv[...] = jnp.concatenate(dvv, axis=0)

        @pl.when(n == N_CH)
        def _():
            band_dk[...] = jnp.zeros_like(band_dk)
            band_dv[...] = jnp.zeros_like(band_dv)

        @pl.when(n >= 1)
        def _():
            dkn_prev = carry_dk[...] + band_dk[:, 0:CH]
            k_t = skp_ref[...]
            for kv in range(SWA_KV):
                rows = slice(kv * SWA_D, (kv + 1) * SWA_D)
                _, rk = _rms_t(k_t[rows], kw)
                dk_t, dkw_terms = _rms_t_bwd(dkn_prev[rows], k_t[rows], rk, kw)
                d_ref[SWA_OFFS[1] + kv * SWA_D:SWA_OFFS[1] + (kv + 1) * SWA_D, :] = _bf(dk_t)
                dkw_ref[...] += dkw_terms
            d_ref[SWA_OFFS[2]:SWA_OFFS[3], :] = _bf(carry_dv[...] + band_dv[:, 0:CH])
            d_ref[SWA_OFFS[0]:SWA_OFFS[1], :] = hold_dq[1 - slot]
            d_ref[SWA_OFFS[3]:SWA_OFFS[4], :] = hold_dg[1 - slot]

        carry_dk[...] = band_dk[:, CH:2 * CH]
        carry_dv[...] = band_dv[:, CH:2 * CH]

    cur_block = lambda n: jnp.minimum(n, N_CH - 1)
    prev_block = lambda n: jnp.maximum(n - 1, 0)
    col = lambda w: pl.BlockSpec((w, CH), lambda n: (0, cur_block(n)))
    prev = lambda w: pl.BlockSpec((w, CH), lambda n: (0, prev_block(n)))
    const = lambda shape: pl.BlockSpec(shape, lambda n: (0,) * len(shape))
    smem = pl.BlockSpec(memory_space=pltpu.SMEM)
    return _Part(
        body,
        inputs=[proj, proj, proj, proj, proj, proj, proj, so, dmix, q_norm_w, k_norm_w, sinks, rel_bias_flat, bucket_t],
        in_specs=[_proj_piece(4, cur_block), _proj_piece(5, prev_block), _proj_piece(5, cur_block),
                  _proj_piece(6, prev_block), _proj_piece(6, cur_block)] + _proj_gate_halves(cur_block) + [
                  col(512), col(512), const((SWA_D, 1)), const((SWA_D, 1)), smem, smem, const((2 * CH, CH))],
        out_specs=[prev(SWA_W), const((2 * CH, HQ_LANES)), const((1, HQ_LANES)),
                   const((SWA_D, CH)), const((SWA_D, CH))],
        out_shape=[jax.ShapeDtypeStruct((SWA_W, T), BF16),
                   jax.ShapeDtypeStruct((2 * CH, HQ_LANES), F32), jax.ShapeDtypeStruct((1, HQ_LANES), F32),
                   jax.ShapeDtypeStruct((SWA_D, CH), F32), jax.ShapeDtypeStruct((SWA_D, CH), F32)],
        scratch_shapes=[pltpu.VMEM((2, 2 * CH, HQ_LANES), F32), pltpu.VMEM((1, HQ_LANES), F32),
                        pltpu.VMEM((128, 2 * CH), F32), pltpu.VMEM((128, 2 * CH), F32),
                        pltpu.VMEM((128, CH), F32), pltpu.VMEM((128, CH), F32),
                        pltpu.VMEM((2, 512, CH), BF16), pltpu.VMEM((2, 512, CH), BF16)])


SMALL_GRAD_SHAPES = ((1, RET_V), (1, CH), (1, CH), (1, CH), (SWA_H, N_BUCKETS))


def _finish_small_grads(dbias_ref, bucket_ref, dsink_ref, dqw_ref, dkw_ref, drw_ref, rw_o, qw_o, kw_o, sink_o, relb_o):
    bt = bucket_ref[...]
    row = lax.broadcasted_iota(jnp.int32, (SWA_H, N_BUCKETS), 0)
    col = lax.broadcasted_iota(jnp.int32, (SWA_H, N_BUCKETS), 1)
    lane = lax.broadcasted_iota(jnp.int32, (1, CH), 1)
    acc = jnp.zeros((SWA_H, N_BUCKETS), F32)
    sink = jnp.zeros((1, CH), F32)
    for hq in range(SWA_H):
        d = dbias_ref[:, _head_lanes(hq)]
        for bk in range(N_BUCKETS):
            s = jnp.sum(jnp.sum(jnp.where(bt == bk, d, 0.0), axis=0, keepdims=True), axis=1, keepdims=True)
            acc = acc + jnp.where((row == hq) & (col == bk), s, 0.0)
        sink = sink + jnp.where(lane == hq, jnp.sum(dsink_ref[:, _head_lanes(hq)], axis=1, keepdims=True), 0.0)
    relb_o[...] = acc
    sink_o[...] = sink
    for src, dst in ((dqw_ref, qw_o), (dkw_ref, kw_o)):
        padded = jnp.concatenate([src[...], jnp.zeros((CH - SWA_D, CH), F32)], axis=0)
        dst[...] = jnp.sum(padded.T, axis=0, keepdims=True)
    rw_o[...] = jnp.sum(drw_ref[...].T, axis=0, keepdims=True)


def _out_proj(mix_r, mix_s, wo, x, target):
    def body(mr_ref, ms_ref, w_ref, x_ref, t_ref, loss_ref, dy_ref, dmr_ref, dms_ref, gw_ref):
        i = pl.program_id(0)

        @pl.when(i == 0)
        def _():
            loss_ref[...] = jnp.zeros_like(loss_ref)
            gw_ref[...] = jnp.zeros_like(gw_ref)

        mixed = jnp.concatenate([mr_ref[...], ms_ref[...]], axis=0)
        w = w_ref[...]
        err = x_ref[...] + _dot(_bf(mixed.T), w) - t_ref[...]
        loss_ref[...] += jnp.sum(jnp.sum(err * err, axis=1, keepdims=True), axis=0, keepdims=True)
        dy = err * (1.0 / D)
        dy_ref[...] = dy
        dy_b = _bf(dy)
        dmix = _dot_nt(w, dy_b)
        dmr_ref[...] = dmix[0:512]
        dms_ref[...] = dmix[512:D]
        gw_ref[...] += _dot(_bf(mixed), dy_b)

    row = lambda w: pl.BlockSpec((TM, w), lambda i: (i, 0))
    col = lambda w: pl.BlockSpec((w, TM), lambda i: (0, i))
    const = lambda shape: pl.BlockSpec(shape, lambda i: (0,) * len(shape))
    return pl.pallas_call(
        body, name="out_proj", grid=(T // TM,),
        in_specs=[col(512), col(512), const((D, D)), row(D), row(D)],
        out_specs=(const((1, 1)), row(D), col(512), col(512), const((D, D))),
        out_shape=(jax.ShapeDtypeStruct((1, 1), F32), jax.ShapeDtypeStruct((T, D), F32),
                   jax.ShapeDtypeStruct((512, T), F32), jax.ShapeDtypeStruct((512, T), F32),
                   jax.ShapeDtypeStruct((D, D), F32)),
        compiler_params=_cparams(1),
    )(mix_r, mix_s, wo, x, target)


LOSS_ROW = 5
ALL_CHIPS = ((0, 0), (0, 1), (1, 0), (1, 1))


def _w_out_reduce_scatter(gwo):
    def body(gwo_ref, gout_o, own, rcv, snd, got, tot, local_sems, a_send, a_recv, b_send, b_recv):
        k = pl.program_id(0)
        x_, y_, c_ = _mesh_pos()
        sibling = (x_, y_, 1 - c_)
        rel_chips = [(1 - x_, y_), (x_, 1 - y_), (1 - x_, 1 - y_)]

        def to_sibling(j):
            px, py = ALL_CHIPS[j]
            return pltpu.make_async_remote_copy(src_ref=gwo_ref.at[_blk(px, py, 1 - c_)], dst_ref=rcv.at[j],
                                                send_sem=a_send.at[j], recv_sem=a_recv.at[j], device_id=sibling,
                                                device_id_type=MESH)

        def local(j):
            px, py = ALL_CHIPS[j]
            return pltpu.make_async_copy(gwo_ref.at[_blk(px, py, c_)], own.at[j], local_sems.at[j])

        def to_chip(q):
            return pltpu.make_async_remote_copy(src_ref=snd.at[q], dst_ref=got.at[q], send_sem=b_send.at[q],
                                                recv_sem=b_recv.at[q], device_id=(*rel_chips[q], c_), device_id_type=MESH)

        @pl.when(k == 0)
        def _():
            for j in range(len(ALL_CHIPS)):
                to_sibling(j).start()
                local(j).start()

        @pl.when(k == 2)
        def _():
            for j in range(len(ALL_CHIPS)):
                local(j).wait()
                to_sibling(j).wait_recv()
            for q in range(3):
                j = 2 * rel_chips[q][0] + rel_chips[q][1]
                snd[q] = _bf(own[j] + rcv[j])
                to_chip(q).start()
            jm = 2 * x_ + y_
            tot[...] = own[jm] + rcv[jm]

        @pl.when(k == N_CH)
        def _():
            g = tot[...]
            for q in range(3):
                to_chip(q).wait_recv()
                g = g + got[q].astype(F32)
            gout_o[...] = g
            for j in range(len(ALL_CHIPS)):
                to_sibling(j).wait_send()
            for q in range(3):
                to_chip(q).wait_send()

    dma = pltpu.SemaphoreType.DMA
    return _Part(
        body, inputs=[gwo], in_specs=[pl.BlockSpec(memory_space=pl.ANY)],
        out_specs=[pl.BlockSpec((WOUT_BLK, D), lambda k: (0, 0))], out_shape=[jax.ShapeDtypeStruct((WOUT_BLK, D), F32)],
        scratch_shapes=[pltpu.VMEM((4, WOUT_BLK, D), F32), pltpu.VMEM((4, WOUT_BLK, D), F32),
                        pltpu.VMEM((3, WOUT_BLK, D), BF16), pltpu.VMEM((3, WOUT_BLK, D), BF16),
                        pltpu.VMEM((WOUT_BLK, D), F32), dma((4,)), dma((4,)), dma((4,)), dma((3,)), dma((3,))])


def _in_proj_bwd_rs(d_ret, d_swa, h, wt, x, rstd, norm_w, dy, sse, small_acc):
    n_blocks, n_tiles = N_DEV, T // TM
    last = n_blocks + n_tiles - 1

    def body(da_any, db_any, da_ref, db_ref, h_ref, w_ref, x_ref, r_ref, nw_ref, dy_ref, sse_ref,
             dbias_ref, bucket_ref, dsink_ref, dqw_ref, dkw_ref, drw_ref, gx_ref, gin_o, suma_o, sumb_o,
             stage, d2d_src, rcv, snd, got, tot, tab_a, tab_b, gnw_acc, rw_ref, qw_ref, kw_ref, sk_ref, rb_ref,
             dp_sems, d2d_send, d2d_recv, ici_send, ici_recv, s_send, s_recv):
        k = pl.program_id(0)
        x_, y_, c_ = _mesh_pos()
        me = _blk(x_, y_, c_)
        sibling = (x_, y_, 1 - c_)
        rel_chips = [(1 - x_, y_), (x_, 1 - y_), (1 - x_, 1 - y_), (x_, y_)]

        def block_of_step(s):
            return _blk(*rel_chips[s // 2], 1 - c_ if s % 2 == 0 else c_)

        def fetch(s, wait):
            slot, b = s % 2, block_of_step(s)
            split = RET_W - 4 * WIN_BLK

            def run(src, dst, sem):
                cp = pltpu.make_async_copy(src, dst, sem)
                cp.wait() if wait else cp.start()

            @pl.when(b < 4)
            def _():
                run(da_any.at[pl.ds(pl.multiple_of(b * WIN_BLK, 16), WIN_BLK), :], stage.at[slot], dp_sems.at[slot, 0])

            @pl.when(b == 4)
            def _():
                run(da_any.at[pl.ds(4 * WIN_BLK, split), :], stage.at[slot, pl.ds(0, split), :], dp_sems.at[slot, 0])
                run(db_any.at[pl.ds(0, WIN_BLK - split), :], stage.at[slot, pl.ds(split, WIN_BLK - split), :],
                    dp_sems.at[slot, 1])

            @pl.when(b > 4)
            def _():
                run(db_any.at[pl.ds(pl.multiple_of(b * WIN_BLK - RET_W, 16), WIN_BLK), :], stage.at[slot],
                    dp_sems.at[slot, 0])

        def d2d_copy(r):
            return pltpu.make_async_remote_copy(src_ref=d2d_src, dst_ref=rcv.at[r], send_sem=d2d_send.at[r],
                                                recv_sem=d2d_recv.at[r], device_id=sibling, device_id_type=MESH)

        def ici_copy(r):
            return pltpu.make_async_remote_copy(src_ref=snd.at[r], dst_ref=got.at[r], send_sem=ici_send.at[r],
                                                recv_sem=ici_recv.at[r], device_id=(*rel_chips[r], c_),
                                                device_id_type=MESH)

        def table_copies():
            return [pltpu.make_async_remote_copy(src_ref=tab.at[me], dst_ref=tab.at[me], send_sem=s_send.at[a, p - 1],
                                                 recv_sem=s_recv.at[a, p - 1],
                                                 device_id=(x_ ^ (p >> 2), y_ ^ ((p >> 1) & 1), c_ ^ (p & 1)),
                                                 device_id_type=MESH)
                    for p in range(1, N_DEV) for a, tab in enumerate((tab_a, tab_b))]

        for s in range(n_blocks):
            @pl.when(k == s)
            def _(s=s):
                r = s // 2
                if s == 0:
                    gnw_acc[...] = jnp.zeros_like(gnw_acc)
                    fetch(0, wait=False)
                fetch(s, wait=True)
                if s + 1 < n_blocks:
                    fetch(s + 1, wait=False)
                gw = _dot(stage[s % 2], h_ref[...])
                if s % 2 == 0:
                    if s >= 2:
                        d2d_copy(r - 1).wait_send()
                    d2d_src[...] = gw
                    d2d_copy(r).start()
                else:
                    d2d_copy(r).wait_recv()
                    chip_sum = gw + rcv[r]
                    if r < 3:
                        snd[r] = _bf(chip_sum)
                        ici_copy(r).start()
                    else:
                        tot[...] = chip_sum

        @pl.when(k >= n_blocks)
        def _():
            dp = jnp.concatenate([da_ref[...], db_ref[...]], axis=0)
            xv, r, nw = x_ref[...], r_ref[...], nw_ref[...]
            dh = lax.dot_general(dp, w_ref[...], (((0,), (0,)), ((), ())), preferred_element_type=F32)
            u = dh * nw
            gx_ref[...] = dy_ref[...] + r * u - xv * (r * r * r) * jnp.mean(u * xv, axis=-1, keepdims=True)
            gnw_acc[...] += jnp.sum(dh * (xv * r), axis=0, keepdims=True)

        @pl.when(k == n_blocks)
        def _():
            _finish_small_grads(dbias_ref, bucket_ref, dsink_ref, dqw_ref, dkw_ref, drw_ref,
                                rw_ref, qw_ref, kw_ref, sk_ref, rb_ref)

        @pl.when(k == last)
        def _():
            mine_a, mine_b = tab_a.at[me], tab_b.at[me]
            mine_a[...] = jnp.zeros_like(mine_a)
            mine_b[...] = jnp.zeros_like(mine_b)
            for row, ref in enumerate((gnw_acc, rw_ref, qw_ref, kw_ref, sk_ref)):
                mine_a[row:row + 1, 0:ref.shape[1]] = ref[...]
            mine_a[LOSS_ROW:LOSS_ROW + 1, 0:1] = sse_ref[...]
            mine_b[:, 0:N_BUCKETS] = rb_ref[...]
            tables = table_copies()
            for cp in tables:
                cp.start()
            g_in = tot[...]
            for q in range(3):
                ici_copy(q).wait_recv()
                g_in = g_in + got[q].astype(F32)
            gin_o[...] = g_in
            for cp in tables:
                cp.wait_recv()
            sum_a, sum_b = tab_a[0], tab_b[0]
            for b in range(1, N_DEV):
                sum_a = sum_a + tab_a[b]
                sum_b = sum_b + tab_b[b]
            suma_o[...] = sum_a
            sumb_o[...] = sum_b
            d2d_copy(3).wait_send()
            for q in range(3):
                ici_copy(q).wait_send()
            for cp in tables:
                cp.wait_send()

    tile = lambda k: jnp.maximum(k - n_blocks, 0)
    row = lambda w: pl.BlockSpec((TM, w), lambda k: (tile(k), 0))
    col = lambda w: pl.BlockSpec((w, TM), lambda k: (0, tile(k)))
    const = lambda shape, **kw: pl.BlockSpec(shape, lambda k: (0,) * len(shape), **kw)
    once = dict(pipeline_mode=pl.Buffered(1))
    hbm = pl.BlockSpec(memory_space=pl.ANY)
    dma = pltpu.SemaphoreType.DMA
    return pl.pallas_call(
        body, name="in_proj_bwd_rs", grid=(n_blocks + n_tiles,),
        in_specs=[hbm, hbm, col(RET_W), col(SWA_W), const((T, D), **once), const((D_IN, D), **once),
                  row(D), row(1), const((1, D)), row(D), const((1, 1))]
                 + [const(a.shape) for a in small_acc],
        out_specs=(row(D), const((WIN_BLK, D)), const((8, D)), const((SWA_H, CH))),
        out_shape=(jax.ShapeDtypeStruct((T, D), F32), jax.ShapeDtypeStruct((WIN_BLK, D), F32),
                   jax.ShapeDtypeStruct((8, D), F32), jax.ShapeDtypeStruct((SWA_H, CH), F32)),
        scratch_shapes=[
            pltpu.VMEM((2, WIN_BLK, T), BF16), pltpu.VMEM((WIN_BLK, D), F32),
            pltpu.VMEM((4, WIN_BLK, D), F32), pltpu.VMEM((3, WIN_BLK, D), BF16),
            pltpu.VMEM((3, WIN_BLK, D), BF16), pltpu.VMEM((WIN_BLK, D), F32),
            pltpu.VMEM((N_DEV, 8, D), F32), pltpu.VMEM((N_DEV, SWA_H, CH), F32),
            pltpu.VMEM((1, D), F32),
        ] + [pltpu.VMEM(s, F32) for s in SMALL_GRAD_SHAPES] + [
            dma((2, 2)), dma((4,)), dma((4,)), dma((3,)), dma((3,)), dma((2, 7)), dma((2, 7)),
        ],
        compiler_params=_cparams(1),
    )(d_ret, d_swa, d_ret, d_swa, h, wt, x, rstd, norm_w, dy, sse, *small_acc)


SMALL_SHAPES = ((1, D), (1, 512), (1, SWA_D), (1, SWA_D), (1, SWA_H), (SWA_H, N_BUCKETS))


def _adamw_all(g_in_t, g_out, sum_a, sum_b, w_in_t, m_in_t, v_in_t, w_out, m_out, v_out, small_w, small_m, small_v):
    n_small = len(SMALL_SHAPES)
    halves = 2

    def body(*refs):
        gin_ref, gout_ref, suma_ref, sumb_ref, w_in_ref, m_in_ref, v_in_ref, w_out_ref, m_out_ref, v_out_ref = refs[:10]
        pos = 10
        sw_refs, sm_refs, sv_refs = (refs[pos + i * n_small:pos + (i + 1) * n_small] for i in range(3))
        pos += 3 * n_small
        g_in_o, d_in_o, nm_in_o, nv_in_o, g_out_o, d_out_o, nm_out_o, nv_out_o = refs[pos:pos + 8]
        pos += 8
        sg_o, sd_o, snm_o, snv_o = (refs[pos + i * n_small:pos + (i + 1) * n_small] for i in range(4))
        loss_o = refs[pos + 4 * n_small]

        g = gout_ref[...]
        d, nm, nv = _adamw(w_out_ref[...], g, m_out_ref[...], v_out_ref[...])
        g_out_o[...], d_out_o[...], nm_out_o[...], nv_out_o[...] = g, d, nm, nv
        g = gin_ref[...]
        d, nm, nv = _adamw(w_in_ref[...], g, m_in_ref[...], v_in_ref[...])
        g_in_o[...], d_in_o[...], nm_in_o[...], nv_in_o[...] = g, d, nm, nv

        @pl.when(pl.program_id(0) == 0)
        def _():
            loss_o[...] = suma_ref[LOSS_ROW:LOSS_ROW + 1, 0:1] * (0.5 / D)
            for r, (rows, lanes) in enumerate(SMALL_SHAPES):
                g = suma_ref[r:r + 1, 0:lanes] if rows == 1 else sumb_ref[:, 0:lanes]
                d, nm, nv = _adamw(sw_refs[r][...], g, sm_refs[r][...], sv_refs[r][...])
                sg_o[r][...], sd_o[r][...], snm_o[r][...], snv_o[r][...] = g, d, nm, nv

    half = lambda rows: pl.BlockSpec((rows // halves, D), lambda i: (i, 0))
    const = lambda shape: pl.BlockSpec(shape, lambda i: (0,) * len(shape))
    win = jax.ShapeDtypeStruct((WIN_BLK, D), F32)
    wout = jax.ShapeDtypeStruct((WOUT_BLK, D), F32)
    smalls = tuple(jax.ShapeDtypeStruct(s, F32) for s in SMALL_SHAPES)
    small_specs = [const(s) for s in SMALL_SHAPES]
    outs = pl.pallas_call(
        body, name="adamw_all", grid=(halves,),
        in_specs=[half(WIN_BLK), half(WOUT_BLK), const((8, D)), const((SWA_H, CH))] + [half(WIN_BLK)] * 3
                 + [half(WOUT_BLK)] * 3 + small_specs * 3,
        out_specs=tuple([half(WIN_BLK)] * 4 + [half(WOUT_BLK)] * 4 + small_specs * 4 + [const((1, 1))]),
        out_shape=(win,) * 4 + (wout,) * 4 + smalls * 4 + (jax.ShapeDtypeStruct((1, 1), F32),),
        compiler_params=_cparams(1, vmem=32 * 1024 * 1024),
    )(g_in_t, g_out, sum_a, sum_b, w_in_t, m_in_t, v_in_t, w_out, m_out, v_out, *small_w, *small_m, *small_v)
    big, rest = outs[:8], outs[8:]
    return big, [rest[i * n_small:(i + 1) * n_small] for i in range(4)], rest[4 * n_small]


def _small_rows(norm_w, ret_norm_w, q_norm_w, k_norm_w, sinks, rel_bias):
    return (norm_w.reshape(1, D), ret_norm_w.reshape(1, 512), q_norm_w.reshape(1, SWA_D), k_norm_w.reshape(1, SWA_D),
            sinks.reshape(1, SWA_H), rel_bias.T)


def _small_leaves(rows):
    return (rows[0].reshape(D), rows[1].reshape(512), rows[2].reshape(SWA_D), rows[3].reshape(SWA_D),
            rows[4].reshape(SWA_H), rows[5].T)


def kernel(x, norm_w, w_in, ret_norm_w, q_norm_w, k_norm_w, sinks, rel_bias, w_out, loss_target, m_norm_w, m_w_in, m_ret_norm_w, m_q_norm_w, m_k_norm_w, m_sinks, m_rel_bias, m_w_out, v_norm_w, v_w_in, v_ret_norm_w, v_q_norm_w, v_k_norm_w, v_sinks, v_rel_bias, v_w_out):
    x2 = x.reshape(T, D)
    target = loss_target.reshape(T, D)
    nw = norm_w.reshape(1, D)
    rnw_col = ret_norm_w.reshape(RET_V, 1)
    qnw_col = q_norm_w.reshape(SWA_D, 1)
    knw_col = k_norm_w.reshape(SWA_D, 1)
    relb = rel_bias.T.reshape(SWA_H * N_BUCKETS)
    ret_tables = _rotary_tables_t() + _retention_tables_t()
    bucket_t = _bucket_table_t()

    proj, wt, rstd, h = _in_proj_gather(x2, nw, w_in.T)
    (ro, mix_r, states, wo), (so, mix_s) = _fused_call("attn_fwd", N_CH, [
        _retention_fwd(proj, rnw_col, ret_tables, w_out),
        _swa_fwd(proj, qnw_col, knw_col, sinks, relb, bucket_t)])
    sse, dy, dmix_r, dmix_s, gwo = _out_proj(mix_r, mix_s, wo.reshape(D, D), x2, target)

    (d_ret, drw_acc), (d_swa, dbias, dsink_acc, dqw_acc, dkw_acc), (g_out,) = _fused_call("attn_bwd", N_CH + 1, [
        _retention_bwd(proj, ro, states, dmix_r, rnw_col, ret_tables),
        _swa_bwd(proj, so, dmix_s, qnw_col, knw_col, sinks, relb, bucket_t),
        _w_out_reduce_scatter(gwo.reshape(N_DEV, WOUT_BLK, D))])
    grad_x, g_in_t, sum_a, sum_b = _in_proj_bwd_rs(d_ret, d_swa, h, wt, x2, rstd, nw, dy, sse,
                                                   (dbias, bucket_t, dsink_acc, dqw_acc, dkw_acc, drw_acc))

    small_w = _small_rows(norm_w, ret_norm_w, q_norm_w, k_norm_w, sinks, rel_bias)
    small_m = _small_rows(m_norm_w, m_ret_norm_w, m_q_norm_w, m_k_norm_w, m_sinks, m_rel_bias)
    small_v = _small_rows(v_norm_w, v_ret_norm_w, v_q_norm_w, v_k_norm_w, v_sinks, v_rel_bias)
    big, small, loss = _adamw_all(g_in_t, g_out, sum_a, sum_b, w_in.T, m_w_in.T, v_w_in.T, w_out, m_w_out, v_w_out,
                                  small_w, small_m, small_v)

    def leaves(i):
        a = _small_leaves(small[i])
        return (a[0], big[i].T, a[1], a[2], a[3], a[4], a[5], big[4 + i])

    return (loss.reshape(()), grad_x.reshape(1, T, D), *leaves(0), *leaves(1), *leaves(2), *leaves(3))
```

```python
from typing import Callable, NamedTuple

import numpy as np
import jax
import jax.numpy as jnp
from jax import lax
from jax.experimental import pallas as pl
from jax.experimental.pallas import tpu as pltpu

F32 = jnp.float32
BF16 = jnp.bfloat16
MESH = pl.DeviceIdType.MESH

T = 2048
D = 1024
D_IN = 2816
N_DEV = 8
WIN_BLK = D_IN // N_DEV
WOUT_BLK = D // N_DEV
CH = 128
N_CH = T // CH
RET_H, RET_DK, RET_DV = 4, 64, 128
RET_QK = RET_H * RET_DK
RET_V = RET_H * RET_DV
SWA_H, SWA_KV, SWA_D, SWA_G = 8, 2, 64, 4
N_BUCKETS = 32
NORM_EPS = 1e-6
GN_EPS = 1e-5
NEG_INF = -1e30
PIECES = (256, 256, 512, 512, 512, 128, 128, 512)
OFFS = tuple(int(v) for v in np.cumsum((0,) + PIECES))
RET_W = OFFS[4]
SWA_W = D_IN - RET_W
SWA_OFFS = tuple(o - RET_W for o in OFFS[4:])
TM = 256

ADAM_LR, ADAM_B1, ADAM_B2, ADAM_EPS, ADAM_WD, ADAM_STEP = 0.001, 0.9, 0.999, 1e-08, 0.01, 10

VMEM_LIMIT = 56 * 1024 * 1024


def _cparams(n_grid=0, vmem=VMEM_LIMIT):
    sem = ("arbitrary",) * n_grid if n_grid else None
    return pltpu.CompilerParams(dimension_semantics=sem, vmem_limit_bytes=vmem)


class _Part(NamedTuple):
    step: Callable
    inputs: list
    in_specs: list
    out_specs: list
    out_shape: list
    scratch_shapes: list


def _fused_call(name, n_steps, parts):
    n_in = [len(p.inputs) for p in parts]
    n_out = [len(p.out_shape) for p in parts]
    n_scr = [len(p.scratch_shapes) for p in parts]

    def body(*refs):
        ins, outs, scr = refs[:sum(n_in)], refs[sum(n_in):sum(n_in) + sum(n_out)], refs[sum(n_in) + sum(n_out):]
        for i, p in enumerate(parts):
            take = lambda seq, counts: seq[sum(counts[:i]):sum(counts[:i + 1])]
            p.step(*take(ins, n_in), *take(outs, n_out), *take(scr, n_scr))

    flat = lambda field: [v for p in parts for v in getattr(p, field)]
    outs = pl.pallas_call(
        body, name=name, grid=(n_steps,), in_specs=flat("in_specs"), out_specs=tuple(flat("out_specs")),
        out_shape=tuple(flat("out_shape")), scratch_shapes=flat("scratch_shapes"), compiler_params=_cparams(1),
    )(*flat("inputs"))
    return [list(outs[sum(n_out[:i]):sum(n_out[:i + 1])]) for i in range(len(parts))]


def _dot(a, b):
    return jnp.dot(a, b, preferred_element_type=F32)


def _dot_nt(a, b):
    return lax.dot_general(a, b, (((1,), (1,)), ((), ())), preferred_element_type=F32)


def _bf(a):
    return a.astype(BF16)


def _sigmoid(x):
    return 1.0 / (1.0 + jnp.exp(-x))


def _adamw(w, g, m, v):
    m = ADAM_B1 * m + (1.0 - ADAM_B1) * g
    v = ADAM_B2 * v + (1.0 - ADAM_B2) * (g * g)
    m_hat = m / (1.0 - ADAM_B1 ** ADAM_STEP)
    v_hat = v / (1.0 - ADAM_B2 ** ADAM_STEP)
    delta = -ADAM_LR * (m_hat / (jnp.sqrt(v_hat) + ADAM_EPS) + ADAM_WD * w)
    return delta, m, v


def _rotary_tables_t():
    half = RET_DK // 2
    inv_freq = np.float32(10000.0) ** (-np.arange(half, dtype=np.float32) / np.float32(half))
    ang = inv_freq[:, None] * np.arange(T, dtype=np.float32)[None, :]
    cos, sin = np.cos(ang).astype(np.float32), np.sin(ang).astype(np.float32)
    cos64 = np.concatenate([cos, cos], axis=0)
    sin64 = np.concatenate([-sin, sin], axis=0)
    return np.tile(cos64, (RET_H, 1)), np.tile(sin64, (RET_H, 1))


def _retention_tables_t():
    gamma = (1.0 - np.exp2(-5.0 - np.arange(RET_H, dtype=np.float32))).astype(np.float32)
    log_g = np.log(gamma).astype(np.float32)
    i = np.arange(CH, dtype=np.float32)
    diff = i[None, :] - i[:, None]
    decay = np.where(diff >= 0, np.exp(log_g[:, None, None] * np.maximum(diff, 0.0)), 0.0).astype(np.float32)
    decay_all = np.concatenate(list(decay), axis=1)
    zeta = np.exp(log_g[:, None] * (CH - 1.0 - i)).astype(np.float32)
    zeta_tab = np.repeat(zeta.T, RET_DK, axis=1)
    xi = np.exp(log_g[:, None] * (i + 1.0)).astype(np.float32)
    xi_tab = np.repeat(xi, RET_DK, axis=0)
    chunk_decay = np.exp(log_g * np.float32(CH)).astype(np.float32)
    row_head = np.arange(RET_V)[:, None] // RET_DV
    col_head = np.arange(RET_QK)[None, :] // RET_DK
    state_mask = (row_head == col_head).astype(np.float32)
    state_decay = (state_mask * chunk_decay[row_head]).astype(np.float32)
    q_mask = (np.arange(RET_QK)[:, None] // RET_DK == np.arange(RET_H * CH)[None, :] // CH).astype(np.float32)
    return (np.ascontiguousarray(decay_all), np.ascontiguousarray(zeta_tab), np.ascontiguousarray(xi_tab),
            state_mask, state_decay, q_mask)


def _bucket_table_t():
    qi = np.arange(CH)[:, None]
    kj = np.arange(2 * CH)[None, :]
    dist = qi + CH - kj
    n = np.maximum(dist, 0)
    max_exact = N_BUCKETS // 2
    nf = np.maximum(n, 1).astype(np.float32)
    large = max_exact + (np.log(nf / np.float32(max_exact)) / np.float32(np.log(CH / max_exact))
                         * np.float32(N_BUCKETS - max_exact)).astype(np.int32)
    large = np.minimum(large, N_BUCKETS - 1)
    bucket = np.where(n < max_exact, n, large)
    return np.ascontiguousarray(np.where((dist >= 0) & (dist < CH), bucket, -1).astype(np.int32).T)


def _mesh_pos():
    return lax.axis_index("x"), lax.axis_index("y"), lax.axis_index("c")


def _blk(px, py, pc):
    return 4 * px + 2 * py + pc


def _in_proj_gather(x, norm_w, w_in_t):
    tchunk = 512

    def body(x_ref, nw_ref, win_ref, proj_ref, wt_ref, rstd_ref, h_ref, stage, send_sems, recv_sems, out_sems):
        x, y, c = _mesh_pos()
        me = _blk(x, y, c)
        sibling = (x, y, 1 - c)
        chips = [(1 - x, y), (x, 1 - y), (1 - x, 1 - y)]

        def copy(k, b, to):
            return pltpu.make_async_remote_copy(src_ref=wt_ref.at[b], dst_ref=wt_ref.at[b], send_sem=send_sems.at[k],
                                                recv_sem=recv_sems.at[k], device_id=to, device_id_type=MESH)

        wt_ref[me] = _bf(win_ref[...])
        first = [copy(0, me, sibling)] + [copy(1 + j, me, (*chip, c)) for j, chip in enumerate(chips[:2])]
        for cp in first:
            cp.start()

        nw = nw_ref[...]
        for i in range(T // TM):
            rows = slice(i * TM, (i + 1) * TM)
            xv = x_ref[rows, :]
            r = lax.rsqrt(jnp.mean(xv * xv, axis=-1, keepdims=True) + NORM_EPS)
            h_ref[rows, :] = _bf(xv * r * nw)
            rstd_ref[rows, :] = r

        writes = []

        def project(b):
            k = len(writes)
            if k >= 2:
                writes[k - 2].wait()
            w = wt_ref[b]
            for t in range(T // tchunk):
                cols = slice(t * tchunk, (t + 1) * tchunk)
                stage[k % 2, :, cols] = _dot_nt(w, h_ref[cols, :])
            cp = pltpu.make_async_copy(stage.at[k % 2], proj_ref.at[b], out_sems.at[k % 2])
            cp.start()
            writes.append(cp)

        project(me)
        copy(0, _blk(x, y, 1 - c), (x, y, c)).wait_recv()
        project(_blk(x, y, 1 - c))
        copy(1, _blk(*chips[0], c), (x, y, c)).wait_recv()
        copy(2, _blk(*chips[1], c), (x, y, c)).wait_recv()
        relay = copy(3, _blk(x ^ (1 - c), y ^ c, c), (x ^ c, y ^ (1 - c), c))
        relay.start()
        passed = [relay]
        for j, chip in enumerate(chips):
            if j == 2:
                copy(3, _blk(*chip, c), (x, y, c)).wait_recv()
            fwd = copy(4 + j, _blk(*chip, c), sibling)
            fwd.start()
            passed.append(fwd)
            project(_blk(*chip, c))
            copy(4 + j, _blk(*chip, 1 - c), (x, y, c)).wait_recv()
            project(_blk(*chip, 1 - c))
        writes[-2].wait()
        writes[-1].wait()
        for cp in first + passed:
            cp.wait_send()

    vm = pl.BlockSpec(memory_space=pltpu.VMEM)
    proj, wt, rstd, h = pl.pallas_call(
        body, name="in_proj_gather",
        out_shape=(jax.ShapeDtypeStruct((N_DEV, WIN_BLK, T), F32), jax.ShapeDtypeStruct((N_DEV, WIN_BLK, D), BF16),
                   jax.ShapeDtypeStruct((T, 1), F32), jax.ShapeDtypeStruct((T, D), BF16)),
        in_specs=[vm, vm, vm], out_specs=(pl.BlockSpec(memory_space=pl.ANY), vm, vm, vm),
        scratch_shapes=[pltpu.VMEM((2, WIN_BLK, T), F32),
                        pltpu.SemaphoreType.DMA((7,)), pltpu.SemaphoreType.DMA((7,)), pltpu.SemaphoreType.DMA((2,))],
        compiler_params=_cparams(vmem=48 * 1024 * 1024),
    )(x, norm_w, w_in_t)
    return proj.reshape(D_IN, T), wt.reshape(D_IN, D), rstd, h


def _gather_w_out(w_out_ref, own_ref, wo_ref, local_sem, send_sems, recv_sems, phase):
    x, y, c = _mesh_pos()
    me = _blk(x, y, c)
    sibling = (x, y, 1 - c)
    chips = [(1 - x, y), (x, 1 - y), (1 - x, 1 - y)]

    def copy(k, b, to, src=None):
        return pltpu.make_async_remote_copy(src_ref=wo_ref.at[b] if src is None else src, dst_ref=wo_ref.at[b],
                                            send_sem=send_sems.at[k], recv_sem=recv_sems.at[k], device_id=to,
                                            device_id_type=MESH)

    local = pltpu.make_async_copy(own_ref, wo_ref.at[me], local_sem)
    first = [copy(0, me, sibling, own_ref)] + [copy(1 + j, me, (*chip, c), own_ref) for j, chip in enumerate(chips)]
    passed = [copy(4 + j, _blk(*chip, c), sibling) for j, chip in enumerate(chips)]
    if phase == 0:
        own_ref[...] = _bf(w_out_ref[...])
        local.start()
        for cp in first:
            cp.start()
    elif phase == 1:
        for j, chip in enumerate(chips):
            copy(1 + j, _blk(*chip, c), (x, y, c)).wait_recv()
            passed[j].start()
    else:
        local.wait()
        copy(0, _blk(x, y, 1 - c), (x, y, c)).wait_recv()
        for j, chip in enumerate(chips):
            copy(4 + j, _blk(*chip, 1 - c), (x, y, c)).wait_recv()
        for cp in first + passed:
            cp.wait_send()


def _proj_piece(i, block_of):
    rows = PIECES[i]
    assert OFFS[i] % rows == 0
    return pl.BlockSpec((rows, CH), lambda n: (OFFS[i] // rows, block_of(n)))


def _proj_gate_halves(block_of):
    return [pl.BlockSpec((256, CH), lambda n, j=j: (OFFS[7] // 256 + j, block_of(n))) for j in range(2)]


def _swap_halves_t(t):
    half = RET_DK // 2
    parts = []
    for h in range(RET_H):
        parts += [t[h * RET_DK + half:(h + 1) * RET_DK], t[h * RET_DK:h * RET_DK + half]]
    return jnp.concatenate(parts, axis=0)


def _rotate(t, cos, sin):
    return t * cos + _swap_halves_t(t) * sin


def _group_norm_t(o):
    mu = jnp.mean(o, axis=0, keepdims=True)
    var = jnp.mean((o - mu) * (o - mu), axis=0, keepdims=True)
    rstd = lax.rsqrt(var + GN_EPS)
    return (o - mu) * rstd, rstd


def _retention_scores_t(q_t, k_rm_b, q_mask, decay_all):
    q_heads = _bf(jnp.concatenate([q_t] * RET_H, axis=1) * q_mask)
    return _dot(k_rm_b, q_heads) * decay_all


def _retention_fwd(proj, ret_norm_w, tables, w_out_blk):
    cos_t, sin_t, decay_all, zeta_tab, xi_tab, state_mask, state_decay, q_mask = tables

    def body(rq_ref, rk_ref, rv_ref, rg_ref, w_ref, cos_ref, sin_ref, decay_ref, zeta_ref, xi_ref, smask_ref,
             sdecay_ref, qmask_ref, wout_ref, ro_ref, mix_ref, st_ref, wo_ref, state, wo_own, wo_local, wo_send, wo_recv):
        n = pl.program_id(0)

        @pl.when(n == 0)
        def _():
            state[...] = jnp.zeros_like(state)
            _gather_w_out(wout_ref, wo_own, wo_ref, wo_local, wo_send, wo_recv, phase=0)

        @pl.when(n == N_CH // 2)
        def _():
            _gather_w_out(wout_ref, wo_own, wo_ref, wo_local, wo_send, wo_recv, phase=1)

        @pl.when(n == N_CH - 1)
        def _():
            _gather_w_out(wout_ref, wo_own, wo_ref, wo_local, wo_send, wo_recv, phase=2)

        cos, sin = cos_ref[...], sin_ref[...]
        q_t = _rotate(rq_ref[...], cos, sin)
        k_t = _rotate(rk_ref[...], cos, sin) * (RET_DK ** -0.5)
        k_rm = k_t.T
        v_b = _bf(rv_ref[...])
        m_b = _bf(state[...])
        st_ref[0] = m_b
        scores_b = _bf(_retention_scores_t(q_t, _bf(k_rm), qmask_ref[...], decay_ref[...]))
        cross = _dot(m_b, _bf(q_t * xi_ref[...]))
        state[...] = state[...] * sdecay_ref[...] + _dot(v_b, _bf(k_rm * zeta_ref[...])) * smask_ref[...]
        heads = range(RET_H)
        rows = [slice(h * RET_DV, (h + 1) * RET_DV) for h in heads]
        o = [_dot(v_b[rows[h]], scores_b[:, h * CH:(h + 1) * CH]) + cross[rows[h]] for h in heads]
        for h in heads:
            rn, _ = _group_norm_t(o[h])
            g = rg_ref[rows[h], :]
            ro_ref[rows[h], :] = o[h]
            mix_ref[rows[h], :] = rn * w_ref[rows[h], :] * (g * _sigmoid(g))

    col = lambda w: pl.BlockSpec((w, CH), lambda n: (0, n))
    const = lambda shape: pl.BlockSpec(shape, lambda n: (0,) * len(shape))
    cur = lambda n: n
    return _Part(
        body,
        inputs=[proj, proj, proj, proj, ret_norm_w, cos_t, sin_t, decay_all, zeta_tab, xi_tab, state_mask, state_decay,
                q_mask, w_out_blk],
        in_specs=[_proj_piece(i, cur) for i in range(4)] + [
            const((RET_V, 1)), col(RET_QK), col(RET_QK),
            const(decay_all.shape), const(zeta_tab.shape), const(xi_tab.shape), const(state_mask.shape),
            const(state_decay.shape), const(q_mask.shape), const((WOUT_BLK, D))],
        out_specs=[col(RET_V), col(RET_V), pl.BlockSpec((1, RET_V, RET_QK), lambda n: (n, 0, 0)),
                   pl.BlockSpec(memory_space=pl.ANY)],
        out_shape=[jax.ShapeDtypeStruct((RET_V, T), F32), jax.ShapeDtypeStruct((RET_V, T), F32),
                   jax.ShapeDtypeStruct((N_CH, RET_V, RET_QK), BF16), jax.ShapeDtypeStruct((N_DEV, WOUT_BLK, D), BF16)],
        scratch_shapes=[pltpu.VMEM((RET_V, RET_QK), F32), pltpu.VMEM((WOUT_BLK, D), BF16), pltpu.SemaphoreType.DMA,
                        pltpu.SemaphoreType.DMA((N_DEV - 1,)), pltpu.SemaphoreType.DMA((N_DEV - 1,))])


def _retention_bwd(proj, ro, states, dmix, ret_norm_w, tables):
    cos_t, sin_t, decay_all, zeta_tab, xi_tab, state_mask, state_decay, q_mask = tables

    def body(*refs):
        @pl.when(pl.program_id(0) < N_CH)
        def _():
            chunk(*refs)

    def chunk(rq_ref, rk_ref, rv_ref, rg_ref, ro_ref, st_ref, dm_ref, w_ref, cos_ref, sin_ref, decay_ref, zeta_ref,
              xi_ref, smask_ref, sdecay_ref, qmask_ref, d_ref, dw_ref, gstate):
        i = pl.program_id(0)

        @pl.when(i == 0)
        def _():
            gstate[...] = jnp.zeros_like(gstate)
            dw_ref[...] = jnp.zeros_like(dw_ref)

        cos, sin = cos_ref[...], sin_ref[...]
        q_t = _rotate(rq_ref[...], cos, sin)
        k_t = _rotate(rk_ref[...], cos, sin) * (RET_DK ** -0.5)
        q_b, k_b = _bf(q_t), _bf(k_t)
        k_rm = k_t.T
        kz_b = _bf(k_rm * zeta_ref[...])
        qx_b = _bf(q_t * xi_ref[...])
        v_t = rv_ref[...]
        v_b = _bf(v_t)
        v_rm_b = _bf(v_t.T)
        decay = decay_ref[...]
        scores_b = _bf(_retention_scores_t(q_t, _bf(k_rm), qmask_ref[...], decay))
        heads = range(RET_H)
        rows = [slice(h * RET_DV, (h + 1) * RET_DV) for h in heads]
        qk_rows = [slice(h * RET_DK, (h + 1) * RET_DK) for h in heads]
        lanes = [slice(h * CH, (h + 1) * CH) for h in heads]
        do = []
        for h in heads:
            g, w, dm = rg_ref[rows[h], :], w_ref[rows[h], :], dm_ref[rows[h], :]
            rn, rstd = _group_norm_t(ro_ref[rows[h], :])
            sig = _sigmoid(g)
            silu = g * sig
            d_ref[OFFS[3] + h * RET_DV:OFFS[3] + (h + 1) * RET_DV, :] = _bf(dm * rn * w * (sig * (1.0 + g * (1.0 - sig))))
            dw_ref[rows[h], :] += dm * silu * rn
            drn = dm * silu * w
            do.append(rstd * (drn - jnp.mean(drn, axis=0, keepdims=True)
                              - rn * jnp.mean(drn * rn, axis=0, keepdims=True)))
        do_b = _bf(jnp.concatenate(do, axis=0))
        m_b = st_ref[0]
        g_all = gstate[...]
        g_b = _bf(g_all)
        dscores_b = [_bf(_dot(v_rm_b[:, rows[h]], do_b[rows[h]]) * decay[:, lanes[h]]) for h in heads]
        dq_cross = lax.dot_general(m_b, do_b, (((0,), (0,)), ((), ())), preferred_element_type=F32)
        dkz = _dot(v_rm_b, g_b)
        dv_cross = _dot_nt(g_b, kz_b)
        gstate[...] = g_all * sdecay_ref[...] + _dot_nt(do_b, qx_b) * smask_ref[...]
        dq = jnp.concatenate([_dot(k_b[qk_rows[h]], dscores_b[h]) for h in heads], axis=0) + dq_cross * xi_ref[...]
        dk = (jnp.concatenate([_dot_nt(q_b[qk_rows[h]], dscores_b[h]) for h in heads], axis=0)
              + (dkz * zeta_ref[...]).T) * (RET_DK ** -0.5)
        for h in heads:
            d_ref[OFFS[2] + h * RET_DV:OFFS[2] + (h + 1) * RET_DV, :] = _bf(
                _dot_nt(do_b[rows[h]], scores_b[:, lanes[h]]) + dv_cross[rows[h]])
        d_ref[OFFS[0]:OFFS[1], :] = _bf(dq * cos + _swap_halves_t(dq * sin))
        d_ref[OFFS[1]:OFFS[2], :] = _bf(dk * cos + _swap_halves_t(dk * sin))

    chunk_of = lambda i: N_CH - 1 - jnp.minimum(i, N_CH - 1)
    col = lambda w: pl.BlockSpec((w, CH), lambda i: (0, chunk_of(i)))
    const = lambda shape: pl.BlockSpec(shape, lambda i: (0,) * len(shape))
    return _Part(
        body,
        inputs=[proj, proj, proj, proj, ro, states, dmix, ret_norm_w, cos_t, sin_t, decay_all, zeta_tab, xi_tab,
                state_mask, state_decay, q_mask],
        in_specs=[_proj_piece(j, chunk_of) for j in range(4)] + [
                  col(RET_V),
                  pl.BlockSpec((1, RET_V, RET_QK), lambda i: (chunk_of(i), 0, 0)), col(RET_V),
                  const((RET_V, 1)), col(RET_QK), col(RET_QK),
                  const(decay_all.shape), const(zeta_tab.shape), const(xi_tab.shape), const(state_mask.shape),
                  const(state_decay.shape), const(q_mask.shape)],
        out_specs=[col(RET_W), const((RET_V, CH))],
        out_shape=[jax.ShapeDtypeStruct((RET_W, T), BF16), jax.ShapeDtypeStruct((RET_V, CH), F32)],
        scratch_shapes=[pltpu.VMEM((RET_V, RET_QK), F32)])


HQ_LANES = SWA_H * CH


def _head_lanes(hq):
    return slice(hq * CH, (hq + 1) * CH)


def _build_bias_t(bucket_ref, relb_ref, sinks_ref, bias_ref, sink_row):
    bt = bucket_ref[...]
    first = lax.broadcasted_iota(jnp.int32, bt.shape, 0) < CH
    for hq in range(SWA_H):
        b = jnp.full(bt.shape, NEG_INF, F32)
        for bk in range(N_BUCKETS):
            b = jnp.where(bt == bk, relb_ref[hq * N_BUCKETS + bk], b)
        bias_ref[0, :, _head_lanes(hq)] = b
        bias_ref[1, :, _head_lanes(hq)] = jnp.where(first, NEG_INF, b)
        sink_row[:, _head_lanes(hq)] = jnp.full((1, CH), sinks_ref[hq], F32)


def _rms_t(t, w_col):
    r = lax.rsqrt(jnp.mean(t * t, axis=0, keepdims=True) + NORM_EPS)
    return t * r * w_col, r


def _rms_t_bwd(dn, t, r, w_col):
    u = dn * w_col
    return r * u - t * (r * r * r) * jnp.mean(u * t, axis=0, keepdims=True), dn * t * r


def _norm_kv(k_t, kw):
    return jnp.concatenate([_rms_t(k_t[g * SWA_D:(g + 1) * SWA_D], kw)[0] for g in range(SWA_KV)], axis=0)


def _kv_slot(a, kv):
    z = jnp.zeros_like(a)
    return jnp.concatenate([a, z] if kv == 0 else [z, a], axis=0)


def _softmax_t(s, sink):
    m = jnp.maximum(jnp.max(s, axis=0, keepdims=True), sink)
    p = jnp.exp(s - m)
    e_sink = jnp.exp(sink - m)
    inv = 1.0 / (jnp.sum(p, axis=0, keepdims=True) + e_sink)
    return p * inv, e_sink * inv


def _gate_rows(sg_halves, hq):
    per_half = SWA_H // 2
    return sg_halves[hq // per_half][(hq % per_half) * SWA_D:(hq % per_half + 1) * SWA_D, :]


def _swa_fwd(proj, q_norm_w, k_norm_w, sinks, rel_bias_flat, bucket_t):
    def body(sq_ref, skp_ref, skc_ref, svp_ref, svc_ref, sg_lo, sg_hi, qw_ref, kw_ref, sinks_ref, relb_ref, bucket_ref,
             so_ref, mix_ref, bias_ref, sink_row):
        n = pl.program_id(0)

        @pl.when(n == 0)
        def _():
            _build_bias_t(bucket_ref, relb_ref, sinks_ref, bias_ref, sink_row)

        var = (n == 0).astype(jnp.int32)
        qw, kw = qw_ref[...], kw_ref[...]
        kn_band = jnp.concatenate([_norm_kv(skp_ref[...], kw), _norm_kv(skc_ref[...], kw)], axis=1)
        kb_rm = _bf(kn_band.T)
        vband = _bf(jnp.concatenate([svp_ref[...], svc_ref[...]], axis=1))
        q_all = jnp.concatenate(
            [_bf(_kv_slot(_rms_t(sq_ref[hq * SWA_D:(hq + 1) * SWA_D, :], qw)[0] * (SWA_D ** -0.5), hq // SWA_G))
             for hq in range(SWA_H)], axis=1)
        probs, _ = _softmax_t(_dot(kb_rm, q_all) + bias_ref[var], sink_row[...])
        probs_b = _bf(probs)
        for kv in range(SWA_KV):
            o = _dot(vband[kv * SWA_D:(kv + 1) * SWA_D], probs_b[:, kv * SWA_G * CH:(kv + 1) * SWA_G * CH])
            for j in range(SWA_G):
                rows = slice((kv * SWA_G + j) * SWA_D, (kv * SWA_G + j + 1) * SWA_D)
                oh = o[:, j * CH:(j + 1) * CH]
                g = _gate_rows((sg_lo, sg_hi), kv * SWA_G + j)
                so_ref[rows, :] = oh
                mix_ref[rows, :] = oh * (g * _sigmoid(g))

    col = lambda w: pl.BlockSpec((w, CH), lambda n: (0, n))
    const = lambda shape: pl.BlockSpec(shape, lambda n: (0,) * len(shape))
    smem = pl.BlockSpec(memory_space=pltpu.SMEM)
    cur = lambda n: n
    prev = lambda n: jnp.maximum(n - 1, 0)
    return _Part(
        body,
        inputs=[proj, proj, proj, proj, proj, proj, proj, q_norm_w, k_norm_w, sinks, rel_bias_flat, bucket_t],
        in_specs=[_proj_piece(4, cur), _proj_piece(5, prev), _proj_piece(5, cur), _proj_piece(6, prev),
                  _proj_piece(6, cur)] + _proj_gate_halves(cur) + [
                  const((SWA_D, 1)), const((SWA_D, 1)), smem, smem, const((2 * CH, CH))],
        out_specs=[col(512), col(512)],
        out_shape=[jax.ShapeDtypeStruct((512, T), F32), jax.ShapeDtypeStruct((512, T), F32)],
        scratch_shapes=[pltpu.VMEM((2, 2 * CH, HQ_LANES), F32), pltpu.VMEM((1, HQ_LANES), F32)])


def _swa_bwd(proj, so, dmix, q_norm_w, k_norm_w, sinks, rel_bias_flat, bucket_t):
    def body(sq_ref, skp_ref, skc_ref, svp_ref, svc_ref, sg_lo, sg_hi, so_ref, dm_ref, qw_ref, kw_ref, sinks_ref,
             relb_ref, bucket_ref, d_ref, dbias_ref, dsink_ref, dqw_ref, dkw_ref,
             bias_ref, sink_row, band_dk, band_dv, carry_dk, carry_dv, hold_dq, hold_dg):
        n = pl.program_id(0)
        slot = n % 2

        @pl.when(n == 0)
        def _():
            _build_bias_t(bucket_ref, relb_ref, sinks_ref, bias_ref, sink_row)
            for ref in (dbias_ref, dsink_ref, dqw_ref, dkw_ref, carry_dk, carry_dv):
                ref[...] = jnp.zeros_like(ref)

        qw, kw = qw_ref[...], kw_ref[...]

        @pl.when(n < N_CH)
        def _():
            var = (n == 0).astype(jnp.int32)
            kn_band = jnp.concatenate([_norm_kv(skp_ref[...], kw), _norm_kv(skc_ref[...], kw)], axis=1)
            kb_rm = _bf(kn_band.T)
            kn_band_s = _bf(kn_band * (SWA_D ** -0.5))
            vband_f = jnp.concatenate([svp_ref[...], svc_ref[...]], axis=1)
            vb_rm = _bf(vband_f.T)
            q_raw, q_rstd, qs_b, do_b = [], [], [], []
            for hq in range(SWA_H):
                rows = slice(hq * SWA_D, (hq + 1) * SWA_D)
                q_t = sq_ref[rows, :]
                qn, rq = _rms_t(q_t, qw)
                g = _gate_rows((sg_lo, sg_hi), hq)
                sig = _sigmoid(g)
                dm = dm_ref[rows, :]
                hold_dg[slot, rows, :] = _bf(dm * so_ref[rows, :] * (sig * (1.0 + g * (1.0 - sig))))
                q_raw.append(q_t)
                q_rstd.append(rq)
                qs_b.append(_bf(qn * (SWA_D ** -0.5)))
                do_b.append(_bf(dm * (g * sig)))
            q_all = jnp.concatenate([_kv_slot(qs_b[hq], hq // SWA_G) for hq in range(SWA_H)], axis=1)
            do_all = jnp.concatenate([_kv_slot(do_b[hq], hq // SWA_G) for hq in range(SWA_H)], axis=1)
            probs, p_sink = _softmax_t(_dot(kb_rm, q_all) + bias_ref[var], sink_row[...])
            dprobs = _dot(vb_rm, do_all)
            t = jnp.sum(probs * dprobs, axis=0, keepdims=True)
            dlog = probs * (dprobs - t)
            dsink_ref[...] += -(p_sink * t)
            dbias_ref[...] += dlog
            dlog_b, probs_b = _bf(dlog), _bf(probs)
            dkn, dvv = [], []
            for kv in range(SWA_KV):
                heads = range(kv * SWA_G, (kv + 1) * SWA_G)
                lanes = slice(kv * SWA_G * CH, (kv + 1) * SWA_G * CH)
                dvv.append(_dot_nt(jnp.concatenate([do_b[hq] for hq in heads], axis=1), probs_b[:, lanes]))
                dkn.append(_dot_nt(jnp.concatenate([qs_b[hq] for hq in heads], axis=1), dlog_b[:, lanes]))
                dqn = _dot(kn_band_s[kv * SWA_D:(kv + 1) * SWA_D], dlog_b[:, lanes])
                for j, hq in enumerate(heads):
                    dq_t, dqw_terms = _rms_t_bwd(dqn[:, j * CH:(j + 1) * CH], q_raw[hq], q_rstd[hq], qw)
                    hold_dq[slot, hq * SWA_D:(hq + 1) * SWA_D, :] = _bf(dq_t)
                    dqw_ref[...] += dqw_terms
            band_dk[...] = jnp.concatenate(dkn, axis=0)
            band_dv[...] = jnp.concatenate(dvv, axis=0)

        @pl.when(n == N_CH)
        def _():
            band_dk[...] = jnp.zeros_like(band_dk)
            band_dv[...] = jnp.zeros_like(band_dv)

        @pl.when(n >= 1)
        def _():
            dkn_prev = carry_dk[...] + band_dk[:, 0:CH]
            k_t = skp_ref[...]
            for kv in range(SWA_KV):
                rows = slice(kv * SWA_D, (kv + 1) * SWA_D)
                _, rk = _rms_t(k_t[rows], kw)
                dk_t, dkw_terms = _rms_t_bwd(dkn_prev[rows], k_t[rows], rk, kw)
                d_ref[SWA_OFFS[1] + kv * SWA_D:SWA_OFFS[1] + (kv + 1) * SWA_D, :] = _bf(dk_t)
                dkw_ref[...] += dkw_terms
            d_ref[SWA_OFFS[2]:SWA_OFFS[3], :] = _bf(carry_dv[...] + band_dv[:, 0:CH])
            d_ref[SWA_OFFS[0]:SWA_OFFS[1], :] = hold_dq[1 - slot]
            d_ref[SWA_OFFS[3]:SWA_OFFS[4], :] = hold_dg[1 - slot]

        carry_dk[...] = band_dk[:, CH:2 * CH]
        carry_dv[...] = band_dv[:, CH:2 * CH]

    cur_block = lambda n: jnp.minimum(n, N_CH - 1)
    prev_block = lambda n: jnp.maximum(n - 1, 0)
    col = lambda w: pl.BlockSpec((w, CH), lambda n: (0, cur_block(n)))
    prev = lambda w: pl.BlockSpec((w, CH), lambda n: (0, prev_block(n)))
    const = lambda shape: pl.BlockSpec(shape, lambda n: (0,) * len(shape))
    smem = pl.BlockSpec(memory_space=pltpu.SMEM)
    return _Part(
        body,
        inputs=[proj, proj, proj, proj, proj, proj, proj, so, dmix, q_norm_w, k_norm_w, sinks, rel_bias_flat, bucket_t],
        in_specs=[_proj_piece(4, cur_block), _proj_piece(5, prev_block), _proj_piece(5, cur_block),
                  _proj_piece(6, prev_block), _proj_piece(6, cur_block)] + _proj_gate_halves(cur_block) + [
                  col(512), col(512), const((SWA_D, 1)), const((SWA_D, 1)), smem, smem, const((2 * CH, CH))],
        out_specs=[prev(SWA_W), const((2 * CH, HQ_LANES)), const((1, HQ_LANES)),
                   const((SWA_D, CH)), const((SWA_D, CH))],
        out_shape=[jax.ShapeDtypeStruct((SWA_W, T), BF16),
                   jax.ShapeDtypeStruct((2 * CH, HQ_LANES), F32), jax.ShapeDtypeStruct((1, HQ_LANES), F32),
                   jax.ShapeDtypeStruct((SWA_D, CH), F32), jax.ShapeDtypeStruct((SWA_D, CH), F32)],
        scratch_shapes=[pltpu.VMEM((2, 2 * CH, HQ_LANES), F32), pltpu.VMEM((1, HQ_LANES), F32),
                        pltpu.VMEM((128, 2 * CH), F32), pltpu.VMEM((128, 2 * CH), F32),
                        pltpu.VMEM((128, CH), F32), pltpu.VMEM((128, CH), F32),
                        pltpu.VMEM((2, 512, CH), BF16), pltpu.VMEM((2, 512, CH), BF16)])


SMALL_GRAD_SHAPES = ((1, RET_V), (1, CH), (1, CH), (1, CH), (SWA_H, N_BUCKETS))


def _finish_small_grads(dbias_ref, bucket_ref, dsink_ref, dqw_ref, dkw_ref, drw_ref, rw_o, qw_o, kw_o, sink_o, relb_o):
    bt = bucket_ref[...]
    row = lax.broadcasted_iota(jnp.int32, (SWA_H, N_BUCKETS), 0)
    col = lax.broadcasted_iota(jnp.int32, (SWA_H, N_BUCKETS), 1)
    lane = lax.broadcasted_iota(jnp.int32, (1, CH), 1)
    acc = jnp.zeros((SWA_H, N_BUCKETS), F32)
    sink = jnp.zeros((1, CH), F32)
    for hq in range(SWA_H):
        d = dbias_ref[:, _head_lanes(hq)]
        for bk in range(N_BUCKETS):
            s = jnp.sum(jnp.sum(jnp.where(bt == bk, d, 0.0), axis=0, keepdims=True), axis=1, keepdims=True)
            acc = acc + jnp.where((row == hq) & (col == bk), s, 0.0)
        sink = sink + jnp.where(lane == hq, jnp.sum(dsink_ref[:, _head_lanes(hq)], axis=1, keepdims=True), 0.0)
    relb_o[...] = acc
    sink_o[...] = sink
    for src, dst in ((dqw_ref, qw_o), (dkw_ref, kw_o)):
        padded = jnp.concatenate([src[...], jnp.zeros((CH - SWA_D, CH), F32)], axis=0)
        dst[...] = jnp.sum(padded.T, axis=0, keepdims=True)
    rw_o[...] = jnp.sum(drw_ref[...].T, axis=0, keepdims=True)


def _out_proj(mix_r, mix_s, wo, x, target):
    def body(mr_ref, ms_ref, w_ref, x_ref, t_ref, loss_ref, dy_ref, dmr_ref, dms_ref, gw_ref):
        i = pl.program_id(0)

        @pl.when(i == 0)
        def _():
            loss_ref[...] = jnp.zeros_like(loss_ref)
            gw_ref[...] = jnp.zeros_like(gw_ref)

        mixed = jnp.concatenate([mr_ref[...], ms_ref[...]], axis=0)
        w = w_ref[...]
        err = x_ref[...] + _dot(_bf(mixed.T), w) - t_ref[...]
        loss_ref[...] += jnp.sum(jnp.sum(err * err, axis=1, keepdims=True), axis=0, keepdims=True)
        dy = err * (1.0 / D)
        dy_ref[...] = dy
        dy_b = _bf(dy)
        dmix = _dot_nt(w, dy_b)
        dmr_ref[...] = dmix[0:512]
        dms_ref[...] = dmix[512:D]
        gw_ref[...] += _dot(_bf(mixed), dy_b)

    row = lambda w: pl.BlockSpec((TM, w), lambda i: (i, 0))
    col = lambda w: pl.BlockSpec((w, TM), lambda i: (0, i))
    const = lambda shape: pl.BlockSpec(shape, lambda i: (0,) * len(shape))
    return pl.pallas_call(
        body, name="out_proj", grid=(T // TM,),
        in_specs=[col(512), col(512), const((D, D)), row(D), row(D)],
        out_specs=(const((1, 1)), row(D), col(512), col(512), const((D, D))),
        out_shape=(jax.ShapeDtypeStruct((1, 1), F32), jax.ShapeDtypeStruct((T, D), F32),
                   jax.ShapeDtypeStruct((512, T), F32), jax.ShapeDtypeStruct((512, T), F32),
                   jax.ShapeDtypeStruct((D, D), F32)),
        compiler_params=_cparams(1),
    )(mix_r, mix_s, wo, x, target)


LOSS_ROW = 5
ALL_CHIPS = ((0, 0), (0, 1), (1, 0), (1, 1))


def _w_out_reduce_scatter(gwo):
    def body(gwo_ref, gout_o, own, rcv, snd, got, tot, local_sems, a_send, a_recv, b_send, b_recv):
        k = pl.program_id(0)
        x_, y_, c_ = _mesh_pos()
        sibling = (x_, y_, 1 - c_)
        rel_chips = [(1 - x_, y_), (x_, 1 - y_), (1 - x_, 1 - y_)]

        def to_sibling(j):
            px, py = ALL_CHIPS[j]
            return pltpu.make_async_remote_copy(src_ref=gwo_ref.at[_blk(px, py, 1 - c_)], dst_ref=rcv.at[j],
                                                send_sem=a_send.at[j], recv_sem=a_recv.at[j], device_id=sibling,
                                                device_id_type=MESH)

        def local(j):
            px, py = ALL_CHIPS[j]
            return pltpu.make_async_copy(gwo_ref.at[_blk(px, py, c_)], own.at[j], local_sems.at[j])

        def to_chip(q):
            return pltpu.make_async_remote_copy(src_ref=snd.at[q], dst_ref=got.at[q], send_sem=b_send.at[q],
                                                recv_sem=b_recv.at[q], device_id=(*rel_chips[q], c_), device_id_type=MESH)

        @pl.when(k == 0)
        def _():
            for j in range(len(ALL_CHIPS)):
                to_sibling(j).start()
                local(j).start()

        @pl.when(k == 2)
        def _():
            for j in range(len(ALL_CHIPS)):
                local(j).wait()
                to_sibling(j).wait_recv()
            for q in range(3):
                j = 2 * rel_chips[q][0] + rel_chips[q][1]
                snd[q] = _bf(own[j] + rcv[j])
                to_chip(q).start()
            jm = 2 * x_ + y_
            tot[...] = own[jm] + rcv[jm]

        @pl.when(k == N_CH)
        def _():
            g = tot[...]
            for q in range(3):
                to_chip(q).wait_recv()
                g = g + got[q].astype(F32)
            gout_o[...] = g
            for j in range(len(ALL_CHIPS)):
                to_sibling(j).wait_send()
            for q in range(3):
                to_chip(q).wait_send()

    dma = pltpu.SemaphoreType.DMA
    return _Part(
        body, inputs=[gwo], in_specs=[pl.BlockSpec(memory_space=pl.ANY)],
        out_specs=[pl.BlockSpec((WOUT_BLK, D), lambda k: (0, 0))], out_shape=[jax.ShapeDtypeStruct((WOUT_BLK, D), F32)],
        scratch_shapes=[pltpu.VMEM((4, WOUT_BLK, D), F32), pltpu.VMEM((4, WOUT_BLK, D), F32),
                        pltpu.VMEM((3, WOUT_BLK, D), BF16), pltpu.VMEM((3, WOUT_BLK, D), BF16),
                        pltpu.VMEM((WOUT_BLK, D), F32), dma((4,)), dma((4,)), dma((4,)), dma((3,)), dma((3,))])


def _in_proj_bwd_rs(d_ret, d_swa, h, wt, x, rstd, norm_w, dy, sse, small_acc):
    n_blocks, n_tiles = N_DEV, T // TM
    last = n_blocks + n_tiles - 1

    def body(da_any, db_any, da_ref, db_ref, h_ref, w_ref, x_ref, r_ref, nw_ref, dy_ref, sse_ref,
             dbias_ref, bucket_ref, dsink_ref, dqw_ref, dkw_ref, drw_ref, gx_ref, gin_o, suma_o, sumb_o,
             stage, d2d_src, mine, rcv, snd, got, tot, tab_a, tab_b, gnw_acc, rw_ref, qw_ref, kw_ref, sk_ref, rb_ref,
             dp_sems, d2d_send, d2d_recv, ici_send, ici_recv, s_send, s_recv):
        k = pl.program_id(0)
        x_, y_, c_ = _mesh_pos()
        me = _blk(x_, y_, c_)
        sibling = (x_, y_, 1 - c_)
        rel_chips = [(1 - x_, y_), (x_, 1 - y_), (1 - x_, 1 - y_), (x_, y_)]

        def block_of_step(s):
            return _blk(*rel_chips[s // 2], 1 - c_ if s % 2 == 0 else c_)

        def fetch(s, wait):
            slot, b = s % 2, block_of_step(s)
            split = RET_W - 4 * WIN_BLK

            def run(src, dst, sem):
                cp = pltpu.make_async_copy(src, dst, sem)
                cp.wait() if wait else cp.start()

            @pl.when(b < 4)
            def _():
                run(da_any.at[pl.ds(pl.multiple_of(b * WIN_BLK, 16), WIN_BLK), :], stage.at[slot], dp_sems.at[slot, 0])

            @pl.when(b == 4)
            def _():
                run(da_any.at[pl.ds(4 * WIN_BLK, split), :], stage.at[slot, pl.ds(0, split), :], dp_sems.at[slot, 0])
                run(db_any.at[pl.ds(0, WIN_BLK - split), :], stage.at[slot, pl.ds(split, WIN_BLK - split), :],
                    dp_sems.at[slot, 1])

            @pl.when(b > 4)
            def _():
                run(db_any.at[pl.ds(pl.multiple_of(b * WIN_BLK - RET_W, 16), WIN_BLK), :], stage.at[slot],
                    dp_sems.at[slot, 0])

        def d2d_copy(r):
            return pltpu.make_async_remote_copy(src_ref=d2d_src, dst_ref=rcv.at[r], send_sem=d2d_send.at[r],
                                                recv_sem=d2d_recv.at[r], device_id=sibling, device_id_type=MESH)

        def ici_copy(r):
            return pltpu.make_async_remote_copy(src_ref=snd.at[r], dst_ref=got.at[r], send_sem=ici_send.at[r],
                                                recv_sem=ici_recv.at[r], device_id=(*rel_chips[r], c_),
                                                device_id_type=MESH)

        def table_copies():
            return [pltpu.make_async_remote_copy(src_ref=tab.at[me], dst_ref=tab.at[me], send_sem=s_send.at[a, p - 1],
                                                 recv_sem=s_recv.at[a, p - 1],
                                                 device_id=(x_ ^ (p >> 2), y_ ^ ((p >> 1) & 1), c_ ^ (p & 1)),
                                                 device_id_type=MESH)
                    for p in range(1, N_DEV) for a, tab in enumerate((tab_a, tab_b))]

        def chip_sum(r):
            d2d_copy(r).wait_recv()
            total = mine[...] + rcv[r]
            if r < 3:
                snd[r] = _bf(total)
                ici_copy(r).start()
            else:
                tot[...] = total

        for s in range(n_blocks):
            @pl.when(k == s)
            def _(s=s):
                r = s // 2
                if s == 0:
                    gnw_acc[...] = jnp.zeros_like(gnw_acc)
                    fetch(0, wait=False)
                fetch(s, wait=True)
                if s + 1 < n_blocks:
                    fetch(s + 1, wait=False)
                gw = _dot(stage[s % 2], h_ref[...])
                if s % 2 == 0:
                    if s >= 2:
                        chip_sum(r - 1)
                        d2d_copy(r - 1).wait_send()
                    d2d_src[...] = gw
                    d2d_copy(r).start()
                else:
                    mine[...] = gw

        @pl.when(k == n_blocks)
        def _():
            chip_sum(3)

        @pl.when(k >= n_blocks)
        def _():
            dp = jnp.concatenate([da_ref[...], db_ref[...]], axis=0)
            xv, r, nw = x_ref[...], r_ref[...], nw_ref[...]
            dh = lax.dot_general(dp, w_ref[...], (((0,), (0,)), ((), ())), preferred_element_type=F32)
            u = dh * nw
            gx_ref[...] = dy_ref[...] + r * u - xv * (r * r * r) * jnp.mean(u * xv, axis=-1, keepdims=True)
            gnw_acc[...] += jnp.sum(dh * (xv * r), axis=0, keepdims=True)

        @pl.when(k == n_blocks)
        def _():
            _finish_small_grads(dbias_ref, bucket_ref, dsink_ref, dqw_ref, dkw_ref, drw_ref,
                                rw_ref, qw_ref, kw_ref, sk_ref, rb_ref)

        @pl.when(k == last)
        def _():
            mine_a, mine_b = tab_a.at[me], tab_b.at[me]
            mine_a[...] = jnp.zeros_like(mine_a)
            mine_b[...] = jnp.zeros_like(mine_b)
            for row, ref in enumerate((gnw_acc, rw_ref, qw_ref, kw_ref, sk_ref)):
                mine_a[row:row + 1, 0:ref.shape[1]] = ref[...]
            mine_a[LOSS_ROW:LOSS_ROW + 1, 0:1] = sse_ref[...]
            mine_b[:, 0:N_BUCKETS] = rb_ref[...]
            tables = table_copies()
            for cp in tables:
                cp.start()
            g_in = tot[...]
            for q in range(3):
                ici_copy(q).wait_recv()
                g_in = g_in + got[q].astype(F32)
            gin_o[...] = g_in
            for cp in tables:
                cp.wait_recv()
            sum_a, sum_b = tab_a[0], tab_b[0]
            for b in range(1, N_DEV):
                sum_a = sum_a + tab_a[b]
                sum_b = sum_b + tab_b[b]
            suma_o[...] = sum_a
            sumb_o[...] = sum_b
            d2d_copy(3).wait_send()
            for q in range(3):
                ici_copy(q).wait_send()
            for cp in tables:
                cp.wait_send()

    tile = lambda k: jnp.maximum(k - n_blocks, 0)
    row = lambda w: pl.BlockSpec((TM, w), lambda k: (tile(k), 0))
    col = lambda w: pl.BlockSpec((w, TM), lambda k: (0, tile(k)))
    const = lambda shape, **kw: pl.BlockSpec(shape, lambda k: (0,) * len(shape), **kw)
    once = dict(pipeline_mode=pl.Buffered(1))
    hbm = pl.BlockSpec(memory_space=pl.ANY)
    dma = pltpu.SemaphoreType.DMA
    return pl.pallas_call(
        body, name="in_proj_bwd_rs", grid=(n_blocks + n_tiles,),
        in_specs=[hbm, hbm, col(RET_W), col(SWA_W), const((T, D), **once), const((D_IN, D), **once),
                  row(D), row(1), const((1, D)), row(D), const((1, 1))]
                 + [const(a.shape) for a in small_acc],
        out_specs=(row(D), const((WIN_BLK, D)), const((8, D)), const((SWA_H, CH))),
        out_shape=(jax.ShapeDtypeStruct((T, D), F32), jax.ShapeDtypeStruct((WIN_BLK, D), F32),
                   jax.ShapeDtypeStruct((8, D), F32), jax.ShapeDtypeStruct((SWA_H, CH), F32)),
        scratch_shapes=[
            pltpu.VMEM((2, WIN_BLK, T), BF16), pltpu.VMEM((WIN_BLK, D), F32),
            pltpu.VMEM((WIN_BLK, D), F32),
            pltpu.VMEM((4, WIN_BLK, D), F32), pltpu.VMEM((3, WIN_BLK, D), BF16),
            pltpu.VMEM((3, WIN_BLK, D), BF16), pltpu.VMEM((WIN_BLK, D), F32),
            pltpu.VMEM((N_DEV, 8, D), F32), pltpu.VMEM((N_DEV, SWA_H, CH), F32),
            pltpu.VMEM((1, D), F32),
        ] + [pltpu.VMEM(s, F32) for s in SMALL_GRAD_SHAPES] + [
            dma((2, 2)), dma((4,)), dma((4,)), dma((3,)), dma((3,)), dma((2, 7)), dma((2, 7)),
        ],
        compiler_params=_cparams(1),
    )(d_ret, d_swa, d_ret, d_swa, h, wt, x, rstd, norm_w, dy, sse, *small_acc)


SMALL_SHAPES = ((1, D), (1, 512), (1, SWA_D), (1, SWA_D), (1, SWA_H), (SWA_H, N_BUCKETS))


def _adamw_all(g_in_t, g_out, sum_a, sum_b, w_in_t, m_in_t, v_in_t, w_out, m_out, v_out, small_w, small_m, small_v):
    n_small = len(SMALL_SHAPES)
    halves = 2

    def body(*refs):
        gin_ref, gout_ref, suma_ref, sumb_ref, w_in_ref, m_in_ref, v_in_ref, w_out_ref, m_out_ref, v_out_ref = refs[:10]
        pos = 10
        sw_refs, sm_refs, sv_refs = (refs[pos + i * n_small:pos + (i + 1) * n_small] for i in range(3))
        pos += 3 * n_small
        g_in_o, d_in_o, nm_in_o, nv_in_o, g_out_o, d_out_o, nm_out_o, nv_out_o = refs[pos:pos + 8]
        pos += 8
        sg_o, sd_o, snm_o, snv_o = (refs[pos + i * n_small:pos + (i + 1) * n_small] for i in range(4))
        loss_o = refs[pos + 4 * n_small]

        g = gout_ref[...]
        d, nm, nv = _adamw(w_out_ref[...], g, m_out_ref[...], v_out_ref[...])
        g_out_o[...], d_out_o[...], nm_out_o[...], nv_out_o[...] = g, d, nm, nv
        g = gin_ref[...]
        d, nm, nv = _adamw(w_in_ref[...], g, m_in_ref[...], v_in_ref[...])
        g_in_o[...], d_in_o[...], nm_in_o[...], nv_in_o[...] = g, d, nm, nv

        @pl.when(pl.program_id(0) == 0)
        def _():
            loss_o[...] = suma_ref[LOSS_ROW:LOSS_ROW + 1, 0:1] * (0.5 / D)
            for r, (rows, lanes) in enumerate(SMALL_SHAPES):
                g = suma_ref[r:r + 1, 0:lanes] if rows == 1 else sumb_ref[:, 0:lanes]
                d, nm, nv = _adamw(sw_refs[r][...], g, sm_refs[r][...], sv_refs[r][...])
                sg_o[r][...], sd_o[r][...], snm_o[r][...], snv_o[r][...] = g, d, nm, nv

    half = lambda rows: pl.BlockSpec((rows // halves, D), lambda i: (i, 0))
    const = lambda shape: pl.BlockSpec(shape, lambda i: (0,) * len(shape))
    win = jax.ShapeDtypeStruct((WIN_BLK, D), F32)
    wout = jax.ShapeDtypeStruct((WOUT_BLK, D), F32)
    smalls = tuple(jax.ShapeDtypeStruct(s, F32) for s in SMALL_SHAPES)
    small_specs = [const(s) for s in SMALL_SHAPES]
    outs = pl.pallas_call(
        body, name="adamw_all", grid=(halves,),
        in_specs=[half(WIN_BLK), half(WOUT_BLK), const((8, D)), const((SWA_H, CH))] + [half(WIN_BLK)] * 3
                 + [half(WOUT_BLK)] * 3 + small_specs * 3,
        out_specs=tuple([half(WIN_BLK)] * 4 + [half(WOUT_BLK)] * 4 + small_specs * 4 + [const((1, 1))]),
        out_shape=(win,) * 4 + (wout,) * 4 + smalls * 4 + (jax.ShapeDtypeStruct((1, 1), F32),),
        compiler_params=_cparams(1, vmem=32 * 1024 * 1024),
    )(g_in_t, g_out, sum_a, sum_b, w_in_t, m_in_t, v_in_t, w_out, m_out, v_out, *small_w, *small_m, *small_v)
    big, rest = outs[:8], outs[8:]
    return big, [rest[i * n_small:(i + 1) * n_small] for i in range(4)], rest[4 * n_small]


def _small_rows(norm_w, ret_norm_w, q_norm_w, k_norm_w, sinks, rel_bias):
    return (norm_w.reshape(1, D), ret_norm_w.reshape(1, 512), q_norm_w.reshape(1, SWA_D), k_norm_w.reshape(1, SWA_D),
            sinks.reshape(1, SWA_H), rel_bias.T)


def _small_leaves(rows):
    return (rows[0].reshape(D), rows[1].reshape(512), rows[2].reshape(SWA_D), rows[3].reshape(SWA_D),
            rows[4].reshape(SWA_H), rows[5].T)


def kernel(x, norm_w, w_in, ret_norm_w, q_norm_w, k_norm_w, sinks, rel_bias, w_out, loss_target, m_norm_w, m_w_in, m_ret_norm_w, m_q_norm_w, m_k_norm_w, m_sinks, m_rel_bias, m_w_out, v_norm_w, v_w_in, v_ret_norm_w, v_q_norm_w, v_k_norm_w, v_sinks, v_rel_bias, v_w_out):
    x2 = x.reshape(T, D)
    target = loss_target.reshape(T, D)
    nw = norm_w.reshape(1, D)
    rnw_col = ret_norm_w.reshape(RET_V, 1)
    qnw_col = q_norm_w.reshape(SWA_D, 1)
    knw_col = k_norm_w.reshape(SWA_D, 1)
    relb = rel_bias.T.reshape(SWA_H * N_BUCKETS)
    ret_tables = _rotary_tables_t() + _retention_tables_t()
    bucket_t = _bucket_table_t()

    proj, wt, rstd, h = _in_proj_gather(x2, nw, w_in.T)
    (ro, mix_r, states, wo), (so, mix_s) = _fused_call("attn_fwd", N_CH, [
        _retention_fwd(proj, rnw_col, ret_tables, w_out),
        _swa_fwd(proj, qnw_col, knw_col, sinks, relb, bucket_t)])
    sse, dy, dmix_r, dmix_s, gwo = _out_proj(mix_r, mix_s, wo.reshape(D, D), x2, target)

    (d_ret, drw_acc), (d_swa, dbias, dsink_acc, dqw_acc, dkw_acc), (g_out,) = _fused_call("attn_bwd", N_CH + 1, [
        _retention_bwd(proj, ro, states, dmix_r, rnw_col, ret_tables),
        _swa_bwd(proj, so, dmix_s, qnw_col, knw_col, sinks, relb, bucket_t),
        _w_out_reduce_scatter(gwo.reshape(N_DEV, WOUT_BLK, D))])
    grad_x, g_in_t, sum_a, sum_b = _in_proj_bwd_rs(d_ret, d_swa, h, wt, x2, rstd, nw, dy, sse,
                                                   (dbias, bucket_t, dsink_acc, dqw_acc, dkw_acc, drw_acc))

    small_w = _small_rows(norm_w, ret_norm_w, q_norm_w, k_norm_w, sinks, rel_bias)
    small_m = _small_rows(m_norm_w, m_ret_norm_w, m_q_norm_w, m_k_norm_w, m_sinks, m_rel_bias)
    small_v = _small_rows(v_norm_w, v_ret_norm_w, v_q_norm_w, v_k_norm_w, v_sinks, v_rel_bias)
    big, small, loss = _adamw_all(g_in_t, g_out, sum_a, sum_b, w_in.T, m_w_in.T, v_w_in.T, w_out, m_w_out, v_w_out,
                                  small_w, small_m, small_v)

    def leaves(i):
        a = _small_leaves(small[i])
        return (a[0], big[i].T, a[1], a[2], a[3], a[4], a[5], big[4 + i])

    return (loss.reshape(()), grad_x.reshape(1, T, D), *leaves(0), *leaves(1), *leaves(2), *leaves(3))
```

```python
from typing import Callable, NamedTuple

import numpy as np
import jax
import jax.numpy as jnp
from jax import lax
from jax.experimental import pallas as pl
from jax.experimental.pallas import tpu as pltpu

F32 = jnp.float32
BF16 = jnp.bfloat16
MESH = pl.DeviceIdType.MESH

T = 2048
D = 1024
D_IN = 2816
N_DEV = 8
WIN_BLK = D_IN // N_DEV
WOUT_BLK = D // N_DEV
CH = 128
N_CH = T // CH
RET_H, RET_DK, RET_DV = 4, 64, 128
RET_QK = RET_H * RET_DK
RET_V = RET_H * RET_DV
SWA_H, SWA_KV, SWA_D, SWA_G = 8, 2, 64, 4
N_BUCKETS = 32
NORM_EPS = 1e-6
GN_EPS = 1e-5
NEG_INF = -1e30
PIECES = (256, 256, 512, 512, 512, 128, 128, 512)
OFFS = tuple(int(v) for v in np.cumsum((0,) + PIECES))
RET_W = OFFS[4]
SWA_W = D_IN - RET_W
SWA_OFFS = tuple(o - RET_W for o in OFFS[4:])
TM = 256

ADAM_LR, ADAM_B1, ADAM_B2, ADAM_EPS, ADAM_WD, ADAM_STEP = 0.001, 0.9, 0.999, 1e-08, 0.01, 10

VMEM_LIMIT = 56 * 1024 * 1024


def _cparams(n_grid=0, vmem=VMEM_LIMIT):
    sem = ("arbitrary",) * n_grid if n_grid else None
    return pltpu.CompilerParams(dimension_semantics=sem, vmem_limit_bytes=vmem)


class _Part(NamedTuple):
    step: Callable
    inputs: list
    in_specs: list
    out_specs: list
    out_shape: list
    scratch_shapes: list


def _fused_call(name, n_steps, parts):
    n_in = [len(p.inputs) for p in parts]
    n_out = [len(p.out_shape) for p in parts]
    n_scr = [len(p.scratch_shapes) for p in parts]

    def body(*refs):
        ins, outs, scr = refs[:sum(n_in)], refs[sum(n_in):sum(n_in) + sum(n_out)], refs[sum(n_in) + sum(n_out):]
        for i, p in enumerate(parts):
            take = lambda seq, counts: seq[sum(counts[:i]):sum(counts[:i + 1])]
            p.step(*take(ins, n_in), *take(outs, n_out), *take(scr, n_scr))

    flat = lambda field: [v for p in parts for v in getattr(p, field)]
    outs = pl.pallas_call(
        body, name=name, grid=(n_steps,), in_specs=flat("in_specs"), out_specs=tuple(flat("out_specs")),
        out_shape=tuple(flat("out_shape")), scratch_shapes=flat("scratch_shapes"), compiler_params=_cparams(1),
    )(*flat("inputs"))
    return [list(outs[sum(n_out[:i]):sum(n_out[:i + 1])]) for i in range(len(parts))]


def _dot(a, b):
    return jnp.dot(a, b, preferred_element_type=F32)


def _dot_nt(a, b):
    return lax.dot_general(a, b, (((1,), (1,)), ((), ())), preferred_element_type=F32)


def _bf(a):
    return a.astype(BF16)


def _sigmoid(x):
    return 1.0 / (1.0 + jnp.exp(-x))


def _adamw(w, g, m, v):
    m = ADAM_B1 * m + (1.0 - ADAM_B1) * g
    v = ADAM_B2 * v + (1.0 - ADAM_B2) * (g * g)
    m_hat = m / (1.0 - ADAM_B1 ** ADAM_STEP)
    v_hat = v / (1.0 - ADAM_B2 ** ADAM_STEP)
    delta = -ADAM_LR * (m_hat / (jnp.sqrt(v_hat) + ADAM_EPS) + ADAM_WD * w)
    return delta, m, v


def _rotary_tables_t():
    half = RET_DK // 2
    inv_freq = np.float32(10000.0) ** (-np.arange(half, dtype=np.float32) / np.float32(half))
    ang = inv_freq[:, None] * np.arange(T, dtype=np.float32)[None, :]
    cos, sin = np.cos(ang).astype(np.float32), np.sin(ang).astype(np.float32)
    cos64 = np.concatenate([cos, cos], axis=0)
    sin64 = np.concatenate([-sin, sin], axis=0)
    return np.tile(cos64, (RET_H, 1)), np.tile(sin64, (RET_H, 1))


def _retention_tables_t():
    gamma = (1.0 - np.exp2(-5.0 - np.arange(RET_H, dtype=np.float32))).astype(np.float32)
    log_g = np.log(gamma).astype(np.float32)
    i = np.arange(CH, dtype=np.float32)
    diff = i[None, :] - i[:, None]
    decay = np.where(diff >= 0, np.exp(log_g[:, None, None] * np.maximum(diff, 0.0)), 0.0).astype(np.float32)
    decay_all = np.concatenate(list(decay), axis=1)
    zeta = np.exp(log_g[:, None] * (CH - 1.0 - i)).astype(np.float32)
    zeta_tab = np.repeat(zeta.T, RET_DK, axis=1)
    xi = np.exp(log_g[:, None] * (i + 1.0)).astype(np.float32)
    xi_tab = np.repeat(xi, RET_DK, axis=0)
    chunk_decay = np.exp(log_g * np.float32(CH)).astype(np.float32)
    row_head = np.arange(RET_V)[:, None] // RET_DV
    col_head = np.arange(RET_QK)[None, :] // RET_DK
    state_mask = (row_head == col_head).astype(np.float32)
    state_decay = (state_mask * chunk_decay[row_head]).astype(np.float32)
    q_mask = (np.arange(RET_QK)[:, None] // RET_DK == np.arange(RET_H * CH)[None, :] // CH).astype(np.float32)
    return (np.ascontiguousarray(decay_all), np.ascontiguousarray(zeta_tab), np.ascontiguousarray(xi_tab),
            state_mask, state_decay, q_mask)


def _bucket_table_t():
    qi = np.arange(CH)[:, None]
    kj = np.arange(2 * CH)[None, :]
    dist = qi + CH - kj
    n = np.maximum(dist, 0)
    max_exact = N_BUCKETS // 2
    nf = np.maximum(n, 1).astype(np.float32)
    large = max_exact + (np.log(nf / np.float32(max_exact)) / np.float32(np.log(CH / max_exact))
                         * np.float32(N_BUCKETS - max_exact)).astype(np.int32)
    large = np.minimum(large, N_BUCKETS - 1)
    bucket = np.where(n < max_exact, n, large)
    return np.ascontiguousarray(np.where((dist >= 0) & (dist < CH), bucket, -1).astype(np.int32).T)


def _mesh_pos():
    return lax.axis_index("x"), lax.axis_index("y"), lax.axis_index("c")


def _blk(px, py, pc):
    return 4 * px + 2 * py + pc


def _in_proj_gather(x, norm_w, w_in_t):
    tchunk = 512

    def body(x_ref, nw_ref, win_ref, proj_ref, wt_ref, rstd_ref, h_ref, stage, send_sems, recv_sems, out_sems):
        x, y, c = _mesh_pos()
        me = _blk(x, y, c)
        sibling = (x, y, 1 - c)
        chips = [(1 - x, y), (x, 1 - y), (1 - x, 1 - y)]

        def copy(k, b, to):
            return pltpu.make_async_remote_copy(src_ref=wt_ref.at[b], dst_ref=wt_ref.at[b], send_sem=send_sems.at[k],
                                                recv_sem=recv_sems.at[k], device_id=to, device_id_type=MESH)

        wt_ref[me] = _bf(win_ref[...])
        first = [copy(0, me, sibling)] + [copy(1 + j, me, (*chip, c)) for j, chip in enumerate(chips[:2])]
        for cp in first:
            cp.start()

        nw = nw_ref[...]
        for i in range(T // TM):
            rows = slice(i * TM, (i + 1) * TM)
            xv = x_ref[rows, :]
            r = lax.rsqrt(jnp.mean(xv * xv, axis=-1, keepdims=True) + NORM_EPS)
            h_ref[rows, :] = _bf(xv * r * nw)
            rstd_ref[rows, :] = r

        writes = []

        def project(b):
            k = len(writes)
            if k >= 2:
                writes[k - 2].wait()
            w = wt_ref[b]
            for t in range(T // tchunk):
                cols = slice(t * tchunk, (t + 1) * tchunk)
                stage[k % 2, :, cols] = _dot_nt(w, h_ref[cols, :])
            cp = pltpu.make_async_copy(stage.at[k % 2], proj_ref.at[b], out_sems.at[k % 2])
            cp.start()
            writes.append(cp)

        project(me)
        copy(0, _blk(x, y, 1 - c), (x, y, c)).wait_recv()
        project(_blk(x, y, 1 - c))
        copy(1, _blk(*chips[0], c), (x, y, c)).wait_recv()
        copy(2, _blk(*chips[1], c), (x, y, c)).wait_recv()
        relay = copy(3, _blk(x ^ (1 - c), y ^ c, c), (x ^ c, y ^ (1 - c), c))
        relay.start()
        passed = [relay]
        for j, chip in enumerate(chips):
            if j == 2:
                copy(3, _blk(*chip, c), (x, y, c)).wait_recv()
            fwd = copy(4 + j, _blk(*chip, c), sibling)
            fwd.start()
            passed.append(fwd)
            project(_blk(*chip, c))
            copy(4 + j, _blk(*chip, 1 - c), (x, y, c)).wait_recv()
            project(_blk(*chip, 1 - c))
        writes[-2].wait()
        writes[-1].wait()
        for cp in first + passed:
            cp.wait_send()

    vm = pl.BlockSpec(memory_space=pltpu.VMEM)
    proj, wt, rstd, h = pl.pallas_call(
        body, name="in_proj_gather",
        out_shape=(jax.ShapeDtypeStruct((N_DEV, WIN_BLK, T), F32), jax.ShapeDtypeStruct((N_DEV, WIN_BLK, D), BF16),
                   jax.ShapeDtypeStruct((T, 1), F32), jax.ShapeDtypeStruct((T, D), BF16)),
        in_specs=[vm, vm, vm], out_specs=(pl.BlockSpec(memory_space=pl.ANY), vm, vm, vm),
        scratch_shapes=[pltpu.VMEM((2, WIN_BLK, T), F32),
                        pltpu.SemaphoreType.DMA((7,)), pltpu.SemaphoreType.DMA((7,)), pltpu.SemaphoreType.DMA((2,))],
        compiler_params=_cparams(vmem=48 * 1024 * 1024),
    )(x, norm_w, w_in_t)
    return proj.reshape(D_IN, T), wt.reshape(D_IN, D), rstd, h


def _gather_w_out(w_out_ref, own_ref, wo_ref, local_sem, send_sems, recv_sems, phase):
    x, y, c = _mesh_pos()
    me = _blk(x, y, c)
    sibling = (x, y, 1 - c)
    chips = [(1 - x, y), (x, 1 - y), (1 - x, 1 - y)]

    def copy(k, b, to, src=None):
        return pltpu.make_async_remote_copy(src_ref=wo_ref.at[b] if src is None else src, dst_ref=wo_ref.at[b],
                                            send_sem=send_sems.at[k], recv_sem=recv_sems.at[k], device_id=to,
                                            device_id_type=MESH)

    local = pltpu.make_async_copy(own_ref, wo_ref.at[me], local_sem)
    first = [copy(0, me, sibling, own_ref)] + [copy(1 + j, me, (*chip, c), own_ref) for j, chip in enumerate(chips)]
    passed = [copy(4 + j, _blk(*chip, c), sibling) for j, chip in enumerate(chips)]
    if phase == 0:
        own_ref[...] = _bf(w_out_ref[...])
        local.start()
        for cp in first:
            cp.start()
    elif phase == 1:
        for j, chip in enumerate(chips):
            copy(1 + j, _blk(*chip, c), (x, y, c)).wait_recv()
            passed[j].start()
    else:
        local.wait()
        copy(0, _blk(x, y, 1 - c), (x, y, c)).wait_recv()
        for j, chip in enumerate(chips):
            copy(4 + j, _blk(*chip, 1 - c), (x, y, c)).wait_recv()
        for cp in first + passed:
            cp.wait_send()


def _proj_piece(i, block_of):
    rows = PIECES[i]
    assert OFFS[i] % rows == 0
    return pl.BlockSpec((rows, CH), lambda n: (OFFS[i] // rows, block_of(n)))


def _proj_gate_halves(block_of):
    return [pl.BlockSpec((256, CH), lambda n, j=j: (OFFS[7] // 256 + j, block_of(n))) for j in range(2)]


def _swap_halves_t(t):
    half = RET_DK // 2
    parts = []
    for h in range(RET_H):
        parts += [t[h * RET_DK + half:(h + 1) * RET_DK], t[h * RET_DK:h * RET_DK + half]]
    return jnp.concatenate(parts, axis=0)


def _rotate(t, cos, sin):
    return t * cos + _swap_halves_t(t) * sin


def _group_norm_t(o):
    mu = jnp.mean(o, axis=0, keepdims=True)
    var = jnp.mean((o - mu) * (o - mu), axis=0, keepdims=True)
    rstd = lax.rsqrt(var + GN_EPS)
    return (o - mu) * rstd, rstd


def _retention_scores_t(q_t, k_rm_b, q_mask, decay_all):
    q_heads = _bf(jnp.concatenate([q_t] * RET_H, axis=1) * q_mask)
    return _dot(k_rm_b, q_heads) * decay_all


def _retention_fwd(proj, ret_norm_w, tables, w_out_blk):
    cos_t, sin_t, decay_all, zeta_tab, xi_tab, state_mask, state_decay, q_mask = tables

    def body(rq_ref, rk_ref, rv_ref, rg_ref, w_ref, cos_ref, sin_ref, decay_ref, zeta_ref, xi_ref, smask_ref,
             sdecay_ref, qmask_ref, wout_ref, ro_ref, mix_ref, st_ref, wo_ref, state, wo_own, wo_local, wo_send, wo_recv):
        n = pl.program_id(0)

        @pl.when(n == 0)
        def _():
            state[...] = jnp.zeros_like(state)
            _gather_w_out(wout_ref, wo_own, wo_ref, wo_local, wo_send, wo_recv, phase=0)

        @pl.when(n == N_CH // 2)
        def _():
            _gather_w_out(wout_ref, wo_own, wo_ref, wo_local, wo_send, wo_recv, phase=1)

        @pl.when(n == N_CH - 1)
        def _():
            _gather_w_out(wout_ref, wo_own, wo_ref, wo_local, wo_send, wo_recv, phase=2)

        cos, sin = cos_ref[...], sin_ref[...]
        q_t = _rotate(rq_ref[...], cos, sin)
        k_t = _rotate(rk_ref[...], cos, sin) * (RET_DK ** -0.5)
        k_rm = k_t.T
        v_b = _bf(rv_ref[...])
        m_b = _bf(state[...])
        st_ref[0] = m_b
        scores_b = _bf(_retention_scores_t(q_t, _bf(k_rm), qmask_ref[...], decay_ref[...]))
        cross = _dot(m_b, _bf(q_t * xi_ref[...]))
        state[...] = state[...] * sdecay_ref[...] + _dot(v_b, _bf(k_rm * zeta_ref[...])) * smask_ref[...]
        heads = range(RET_H)
        rows = [slice(h * RET_DV, (h + 1) * RET_DV) for h in heads]
        o = [_dot(v_b[rows[h]], scores_b[:, h * CH:(h + 1) * CH]) + cross[rows[h]] for h in heads]
        for h in heads:
            rn, _ = _group_norm_t(o[h])
            g = rg_ref[rows[h], :]
            ro_ref[rows[h], :] = o[h]
            mix_ref[rows[h], :] = rn * w_ref[rows[h], :] * (g * _sigmoid(g))

    col = lambda w: pl.BlockSpec((w, CH), lambda n: (0, n))
    const = lambda shape: pl.BlockSpec(shape, lambda n: (0,) * len(shape))
    cur = lambda n: n
    return _Part(
        body,
        inputs=[proj, proj, proj, proj, ret_norm_w, cos_t, sin_t, decay_all, zeta_tab, xi_tab, state_mask, state_decay,
                q_mask, w_out_blk],
        in_specs=[_proj_piece(i, cur) for i in range(4)] + [
            const((RET_V, 1)), col(RET_QK), col(RET_QK),
            const(decay_all.shape), const(zeta_tab.shape), const(xi_tab.shape), const(state_mask.shape),
            const(state_decay.shape), const(q_mask.shape), const((WOUT_BLK, D))],
        out_specs=[col(RET_V), col(RET_V), pl.BlockSpec((1, RET_V, RET_QK), lambda n: (n, 0, 0)),
                   pl.BlockSpec(memory_space=pl.ANY)],
        out_shape=[jax.ShapeDtypeStruct((RET_V, T), F32), jax.ShapeDtypeStruct((RET_V, T), F32),
                   jax.ShapeDtypeStruct((N_CH, RET_V, RET_QK), BF16), jax.ShapeDtypeStruct((N_DEV, WOUT_BLK, D), BF16)],
        scratch_shapes=[pltpu.VMEM((RET_V, RET_QK), F32), pltpu.VMEM((WOUT_BLK, D), BF16), pltpu.SemaphoreType.DMA,
                        pltpu.SemaphoreType.DMA((N_DEV - 1,)), pltpu.SemaphoreType.DMA((N_DEV - 1,))])


def _retention_bwd(proj, ro, states, dmix, ret_norm_w, tables):
    cos_t, sin_t, decay_all, zeta_tab, xi_tab, state_mask, state_decay, q_mask = tables

    def body(*refs):
        @pl.when(pl.program_id(0) < N_CH)
        def _():
            chunk(*refs)

    def chunk(rq_ref, rk_ref, rv_ref, rg_ref, ro_ref, st_ref, dm_ref, w_ref, cos_ref, sin_ref, decay_ref, zeta_ref,
              xi_ref, smask_ref, sdecay_ref, qmask_ref, d_ref, dw_ref, gstate):
        i = pl.program_id(0)

        @pl.when(i == 0)
        def _():
            gstate[...] = jnp.zeros_like(gstate)
            dw_ref[...] = jnp.zeros_like(dw_ref)

        cos, sin = cos_ref[...], sin_ref[...]
        q_t = _rotate(rq_ref[...], cos, sin)
        k_t = _rotate(rk_ref[...], cos, sin) * (RET_DK ** -0.5)
        q_b, k_b = _bf(q_t), _bf(k_t)
        k_rm = k_t.T
        kz_b = _bf(k_rm * zeta_ref[...])
        qx_b = _bf(q_t * xi_ref[...])
        v_t = rv_ref[...]
        v_b = _bf(v_t)
        v_rm_b = _bf(v_t.T)
        decay = decay_ref[...]
        scores_b = _bf(_retention_scores_t(q_t, _bf(k_rm), qmask_ref[...], decay))
        heads = range(RET_H)
        rows = [slice(h * RET_DV, (h + 1) * RET_DV) for h in heads]
        qk_rows = [slice(h * RET_DK, (h + 1) * RET_DK) for h in heads]
        lanes = [slice(h * CH, (h + 1) * CH) for h in heads]
        do = []
        for h in heads:
            g, w, dm = rg_ref[rows[h], :], w_ref[rows[h], :], dm_ref[rows[h], :]
            rn, rstd = _group_norm_t(ro_ref[rows[h], :])
            sig = _sigmoid(g)
            silu = g * sig
            d_ref[OFFS[3] + h * RET_DV:OFFS[3] + (h + 1) * RET_DV, :] = _bf(dm * rn * w * (sig * (1.0 + g * (1.0 - sig))))
            dw_ref[rows[h], :] += dm * silu * rn
            drn = dm * silu * w
            do.append(rstd * (drn - jnp.mean(drn, axis=0, keepdims=True)
                              - rn * jnp.mean(drn * rn, axis=0, keepdims=True)))
        do_b = _bf(jnp.concatenate(do, axis=0))
        m_b = st_ref[0]
        g_all = gstate[...]
        g_b = _bf(g_all)
        dscores_b = [_bf(_dot(v_rm_b[:, rows[h]], do_b[rows[h]]) * decay[:, lanes[h]]) for h in heads]
        dq_cross = lax.dot_general(m_b, do_b, (((0,), (0,)), ((), ())), preferred_element_type=F32)
        dkz = _dot(v_rm_b, g_b)
        dv_cross = _dot_nt(g_b, kz_b)
        gstate[...] = g_all * sdecay_ref[...] + _dot_nt(do_b, qx_b) * smask_ref[...]
        dq = jnp.concatenate([_dot(k_b[qk_rows[h]], dscores_b[h]) for h in heads], axis=0) + dq_cross * xi_ref[...]
        dk = (jnp.concatenate([_dot_nt(q_b[qk_rows[h]], dscores_b[h]) for h in heads], axis=0)
              + (dkz * zeta_ref[...]).T) * (RET_DK ** -0.5)
        for h in heads:
            d_ref[OFFS[2] + h * RET_DV:OFFS[2] + (h + 1) * RET_DV, :] = _bf(
                _dot_nt(do_b[rows[h]], scores_b[:, lanes[h]]) + dv_cross[rows[h]])
        d_ref[OFFS[0]:OFFS[1], :] = _bf(dq * cos + _swap_halves_t(dq * sin))
        d_ref[OFFS[1]:OFFS[2], :] = _bf(dk * cos + _swap_halves_t(dk * sin))

    chunk_of = lambda i: N_CH - 1 - jnp.minimum(i, N_CH - 1)
    col = lambda w: pl.BlockSpec((w, CH), lambda i: (0, chunk_of(i)))
    const = lambda shape: pl.BlockSpec(shape, lambda i: (0,) * len(shape))
    return _Part(
        body,
        inputs=[proj, proj, proj, proj, ro, states, dmix, ret_norm_w, cos_t, sin_t, decay_all, zeta_tab, xi_tab,
                state_mask, state_decay, q_mask],
        in_specs=[_proj_piece(j, chunk_of) for j in range(4)] + [
                  col(RET_V),
                  pl.BlockSpec((1, RET_V, RET_QK), lambda i: (chunk_of(i), 0, 0)), col(RET_V),
                  const((RET_V, 1)), col(RET_QK), col(RET_QK),
                  const(decay_all.shape), const(zeta_tab.shape), const(xi_tab.shape), const(state_mask.shape),
                  const(state_decay.shape), const(q_mask.shape)],
        out_specs=[col(RET_W), const((RET_V, CH))],
        out_shape=[jax.ShapeDtypeStruct((RET_W, T), BF16), jax.ShapeDtypeStruct((RET_V, CH), F32)],
        scratch_shapes=[pltpu.VMEM((RET_V, RET_QK), F32)])


HQ_LANES = SWA_H * CH


def _head_lanes(hq):
    return slice(hq * CH, (hq + 1) * CH)


def _build_bias_t(bucket_ref, relb_ref, sinks_ref, bias_ref, sink_row):
    bt = bucket_ref[...]
    first = lax.broadcasted_iota(jnp.int32, bt.shape, 0) < CH
    for hq in range(SWA_H):
        b = jnp.full(bt.shape, NEG_INF, F32)
        for bk in range(N_BUCKETS):
            b = jnp.where(bt == bk, relb_ref[hq * N_BUCKETS + bk], b)
        bias_ref[0, :, _head_lanes(hq)] = b
        bias_ref[1, :, _head_lanes(hq)] = jnp.where(first, NEG_INF, b)
        sink_row[:, _head_lanes(hq)] = jnp.full((1, CH), sinks_ref[hq], F32)


def _rms_t(t, w_col):
    r = lax.rsqrt(jnp.mean(t * t, axis=0, keepdims=True) + NORM_EPS)
    return t * r * w_col, r


def _rms_t_bwd(dn, t, r, w_col):
    u = dn * w_col
    return r * u - t * (r * r * r) * jnp.mean(u * t, axis=0, keepdims=True), dn * t * r


def _norm_kv(k_t, kw):
    return jnp.concatenate([_rms_t(k_t[g * SWA_D:(g + 1) * SWA_D], kw)[0] for g in range(SWA_KV)], axis=0)


def _kv_slot(a, kv):
    z = jnp.zeros_like(a)
    return jnp.concatenate([a, z] if kv == 0 else [z, a], axis=0)


def _softmax_t(s, sink):
    m = jnp.maximum(jnp.max(s, axis=0, keepdims=True), sink)
    p = jnp.exp(s - m)
    e_sink = jnp.exp(sink - m)
    inv = 1.0 / (jnp.sum(p, axis=0, keepdims=True) + e_sink)
    return p * inv, e_sink * inv


def _gate_rows(sg_halves, hq):
    per_half = SWA_H // 2
    return sg_halves[hq // per_half][(hq % per_half) * SWA_D:(hq % per_half + 1) * SWA_D, :]


def _swa_fwd(proj, q_norm_w, k_norm_w, sinks, rel_bias_flat, bucket_t):
    def body(sq_ref, skp_ref, skc_ref, svp_ref, svc_ref, sg_lo, sg_hi, qw_ref, kw_ref, sinks_ref, relb_ref, bucket_ref,
             so_ref, mix_ref, bias_ref, sink_row):
        n = pl.program_id(0)

        @pl.when(n == 0)
        def _():
            _build_bias_t(bucket_ref, relb_ref, sinks_ref, bias_ref, sink_row)

        var = (n == 0).astype(jnp.int32)
        qw, kw = qw_ref[...], kw_ref[...]
        kn_band = jnp.concatenate([_norm_kv(skp_ref[...], kw), _norm_kv(skc_ref[...], kw)], axis=1)
        kb_rm = _bf(kn_band.T)
        vband = _bf(jnp.concatenate([svp_ref[...], svc_ref[...]], axis=1))
        q_all = jnp.concatenate(
            [_bf(_kv_slot(_rms_t(sq_ref[hq * SWA_D:(hq + 1) * SWA_D, :], qw)[0] * (SWA_D ** -0.5), hq // SWA_G))
             for hq in range(SWA_H)], axis=1)
        probs, _ = _softmax_t(_dot(kb_rm, q_all) + bias_ref[var], sink_row[...])
        probs_b = _bf(probs)
        for kv in range(SWA_KV):
            o = _dot(vband[kv * SWA_D:(kv + 1) * SWA_D], probs_b[:, kv * SWA_G * CH:(kv + 1) * SWA_G * CH])
            for j in range(SWA_G):
                rows = slice((kv * SWA_G + j) * SWA_D, (kv * SWA_G + j + 1) * SWA_D)
                oh = o[:, j * CH:(j + 1) * CH]
                g = _gate_rows((sg_lo, sg_hi), kv * SWA_G + j)
                so_ref[rows, :] = oh
                mix_ref[rows, :] = oh * (g * _sigmoid(g))

    col = lambda w: pl.BlockSpec((w, CH), lambda n: (0, n))
    const = lambda shape: pl.BlockSpec(shape, lambda n: (0,) * len(shape))
    smem = pl.BlockSpec(memory_space=pltpu.SMEM)
    cur = lambda n: n
    prev = lambda n: jnp.maximum(n - 1, 0)
    return _Part(
        body,
        inputs=[proj, proj, proj, proj, proj, proj, proj, q_norm_w, k_norm_w, sinks, rel_bias_flat, bucket_t],
        in_specs=[_proj_piece(4, cur), _proj_piece(5, prev), _proj_piece(5, cur), _proj_piece(6, prev),
                  _proj_piece(6, cur)] + _proj_gate_halves(cur) + [
                  const((SWA_D, 1)), const((SWA_D, 1)), smem, smem, const((2 * CH, CH))],
        out_specs=[col(512), col(512)],
        out_shape=[jax.ShapeDtypeStruct((512, T), F32), jax.ShapeDtypeStruct((512, T), F32)],
        scratch_shapes=[pltpu.VMEM((2, 2 * CH, HQ_LANES), F32), pltpu.VMEM((1, HQ_LANES), F32)])


def _swa_bwd(proj, so, dmix, q_norm_w, k_norm_w, sinks, rel_bias_flat, bucket_t):
    def body(sq_ref, skp_ref, skc_ref, svp_ref, svc_ref, sg_lo, sg_hi, so_ref, dm_ref, qw_ref, kw_ref, sinks_ref,
             relb_ref, bucket_ref, d_ref, dbias_ref, dsink_ref, dqw_ref, dkw_ref,
             bias_ref, sink_row, band_dk, band_dv, carry_dk, carry_dv, hold_dq, hold_dg):
        n = pl.program_id(0)
        slot = n % 2

        @pl.when(n == 0)
        def _():
            _build_bias_t(bucket_ref, relb_ref, sinks_ref, bias_ref, sink_row)
            for ref in (dbias_ref, dsink_ref, dqw_ref, dkw_ref, carry_dk, carry_dv):
                ref[...] = jnp.zeros_like(ref)

        qw, kw = qw_ref[...], kw_ref[...]

        @pl.when(n < N_CH)
        def _():
            var = (n == 0).astype(jnp.int32)
            kn_band = jnp.concatenate([_norm_kv(skp_ref[...], kw), _norm_kv(skc_ref[...], kw)], axis=1)
            kb_rm = _bf(kn_band.T)
            kn_band_s = _bf(kn_band * (SWA_D ** -0.5))
            vband_f = jnp.concatenate([svp_ref[...], svc_ref[...]], axis=1)
            vb_rm = _bf(vband_f.T)
            q_raw, q_rstd, qs_b, do_b = [], [], [], []
            for hq in range(SWA_H):
                rows = slice(hq * SWA_D, (hq + 1) * SWA_D)
                q_t = sq_ref[rows, :]
                qn, rq = _rms_t(q_t, qw)
                g = _gate_rows((sg_lo, sg_hi), hq)
                sig = _sigmoid(g)
                dm = dm_ref[rows, :]
                hold_dg[slot, rows, :] = _bf(dm * so_ref[rows, :] * (sig * (1.0 + g * (1.0 - sig))))
                q_raw.append(q_t)
                q_rstd.append(rq)
                qs_b.append(_bf(qn * (SWA_D ** -0.5)))
                do_b.append(_bf(dm * (g * sig)))
            q_all = jnp.concatenate([_kv_slot(qs_b[hq], hq // SWA_G) for hq in range(SWA_H)], axis=1)
            do_all = jnp.concatenate([_kv_slot(do_b[hq], hq // SWA_G) for hq in range(SWA_H)], axis=1)
            probs, p_sink = _softmax_t(_dot(kb_rm, q_all) + bias_ref[var], sink_row[...])
            dprobs = _dot(vb_rm, do_all)
            t = jnp.sum(probs * dprobs, axis=0, keepdims=True)
            dlog = probs * (dprobs - t)
            dsink_ref[...] += -(p_sink * t)
            dbias_ref[...] += dlog
            dlog_b, probs_b = _bf(dlog), _bf(probs)
            dkn, dvv = [], []
            for kv in range(SWA_KV):
                heads = range(kv * SWA_G, (kv + 1) * SWA_G)
                lanes = slice(kv * SWA_G * CH, (kv + 1) * SWA_G * CH)
                dvv.append(_dot_nt(jnp.concatenate([do_b[hq] for hq in heads], axis=1), probs_b[:, lanes]))
                dkn.append(_dot_nt(jnp.concatenate([qs_b[hq] for hq in heads], axis=1), dlog_b[:, lanes]))
                dqn = _dot(kn_band_s[kv * SWA_D:(kv + 1) * SWA_D], dlog_b[:, lanes])
                for j, hq in enumerate(heads):
                    dq_t, dqw_terms = _rms_t_bwd(dqn[:, j * CH:(j + 1) * CH], q_raw[hq], q_rstd[hq], qw)
                    hold_dq[slot, hq * SWA_D:(hq + 1) * SWA_D, :] = _bf(dq_t)
                    dqw_ref[...] += dqw_terms
            band_dk[...] = jnp.concatenate(dkn, axis=0)
            band_dv[...] = jnp.concatenate(dvv, axis=0)

        @pl.when(n == N_CH)
        def _():
            band_dk[...] = jnp.zeros_like(band_dk)
            band_dv[...] = jnp.zeros_like(band_dv)

        @pl.when(n >= 1)
        def _():
            dkn_prev = carry_dk[...] + band_dk[:, 0:CH]
            k_t = skp_ref[...]
            for kv in range(SWA_KV):
                rows = slice(kv * SWA_D, (kv + 1) * SWA_D)
                _, rk = _rms_t(k_t[rows], kw)
                dk_t, dkw_terms = _rms_t_bwd(dkn_prev[rows], k_t[rows], rk, kw)
                d_ref[SWA_OFFS[1] + kv * SWA_D:SWA_OFFS[1] + (kv + 1) * SWA_D, :] = _bf(dk_t)
                dkw_ref[...] += dkw_terms
            d_ref[SWA_OFFS[2]:SWA_OFFS[3], :] = _bf(carry_dv[...] + band_dv[:, 0:CH])
            d_ref[SWA_OFFS[0]:SWA_OFFS[1], :] = hold_dq[1 - slot]
            d_ref[SWA_OFFS[3]:SWA_OFFS[4], :] = hold_dg[1 - slot]

        carry_dk[...] = band_dk[:, CH:2 * CH]
        carry_dv[...] = band_dv[:, CH:2 * CH]

    cur_block = lambda n: jnp.minimum(n, N_CH - 1)
    prev_block = lambda n: jnp.maximum(n - 1, 0)
    col = lambda w: pl.BlockSpec((w, CH), lambda n: (0, cur_block(n)))
    prev = lambda w: pl.BlockSpec((w, CH), lambda n: (0, prev_block(n)))
    const = lambda shape: pl.BlockSpec(shape, lambda n: (0,) * len(shape))
    smem = pl.BlockSpec(memory_space=pltpu.SMEM)
    return _Part(
        body,
        inputs=[proj, proj, proj, proj, proj, proj, proj, so, dmix, q_norm_w, k_norm_w, sinks, rel_bias_flat, bucket_t],
        in_specs=[_proj_piece(4, cur_block), _proj_piece(5, prev_block), _proj_piece(5, cur_block),
                  _proj_piece(6, prev_block), _proj_piece(6, cur_block)] + _proj_gate_halves(cur_block) + [
                  col(512), col(512), const((SWA_D, 1)), const((SWA_D, 1)), smem, smem, const((2 * CH, CH))],
        out_specs=[prev(SWA_W), const((2 * CH, HQ_LANES)), const((1, HQ_LANES)),
                   const((SWA_D, CH)), const((SWA_D, CH))],
        out_shape=[jax.ShapeDtypeStruct((SWA_W, T), BF16),
                   jax.ShapeDtypeStruct((2 * CH, HQ_LANES), F32), jax.ShapeDtypeStruct((1, HQ_LANES), F32),
                   jax.ShapeDtypeStruct((SWA_D, CH), F32), jax.ShapeDtypeStruct((SWA_D, CH), F32)],
        scratch_shapes=[pltpu.VMEM((2, 2 * CH, HQ_LANES), F32), pltpu.VMEM((1, HQ_LANES), F32),
                        pltpu.VMEM((128, 2 * CH), F32), pltpu.VMEM((128, 2 * CH), F32),
                        pltpu.VMEM((128, CH), F32), pltpu.VMEM((128, CH), F32),
                        pltpu.VMEM((2, 512, CH), BF16), pltpu.VMEM((2, 512, CH), BF16)])


SMALL_GRAD_SHAPES = ((1, RET_V), (1, CH), (1, CH), (1, CH), (SWA_H, N_BUCKETS))


def _finish_small_grads(dbias_ref, bucket_ref, dsink_ref, dqw_ref, dkw_ref, drw_ref, rw_o, qw_o, kw_o, sink_o, relb_o):
    bt = bucket_ref[...]
    row = lax.broadcasted_iota(jnp.int32, (SWA_H, N_BUCKETS), 0)
    col = lax.broadcasted_iota(jnp.int32, (SWA_H, N_BUCKETS), 1)
    lane = lax.broadcasted_iota(jnp.int32, (1, CH), 1)
    acc = jnp.zeros((SWA_H, N_BUCKETS), F32)
    sink = jnp.zeros((1, CH), F32)
    for hq in range(SWA_H):
        d = dbias_ref[:, _head_lanes(hq)]
        for bk in range(N_BUCKETS):
            s = jnp.sum(jnp.sum(jnp.where(bt == bk, d, 0.0), axis=0, keepdims=True), axis=1, keepdims=True)
            acc = acc + jnp.where((row == hq) & (col == bk), s, 0.0)
        sink = sink + jnp.where(lane == hq, jnp.sum(dsink_ref[:, _head_lanes(hq)], axis=1, keepdims=True), 0.0)
    relb_o[...] = acc
    sink_o[...] = sink
    for src, dst in ((dqw_ref, qw_o), (dkw_ref, kw_o)):
        padded = jnp.concatenate([src[...], jnp.zeros((CH - SWA_D, CH), F32)], axis=0)
        dst[...] = jnp.sum(padded.T, axis=0, keepdims=True)
    rw_o[...] = jnp.sum(drw_ref[...].T, axis=0, keepdims=True)


def _out_proj(mix_r, mix_s, wo, x, target):
    def body(mr_ref, ms_ref, w_ref, x_ref, t_ref, loss_ref, dy_ref, dmr_ref, dms_ref, gw_ref):
        i = pl.program_id(0)

        @pl.when(i == 0)
        def _():
            loss_ref[...] = jnp.zeros_like(loss_ref)
            gw_ref[...] = jnp.zeros_like(gw_ref)

        mixed = jnp.concatenate([mr_ref[...], ms_ref[...]], axis=0)
        w = w_ref[...]
        err = x_ref[...] + _dot(_bf(mixed.T), w) - t_ref[...]
        loss_ref[...] += jnp.sum(jnp.sum(err * err, axis=1, keepdims=True), axis=0, keepdims=True)
        dy = err * (1.0 / D)
        dy_ref[...] = dy
        dy_b = _bf(dy)
        dmix = _dot_nt(w, dy_b)
        dmr_ref[...] = dmix[0:512]
        dms_ref[...] = dmix[512:D]
        gw_ref[...] += _dot(_bf(mixed), dy_b)

    row = lambda w: pl.BlockSpec((TM, w), lambda i: (i, 0))
    col = lambda w: pl.BlockSpec((w, TM), lambda i: (0, i))
    const = lambda shape: pl.BlockSpec(shape, lambda i: (0,) * len(shape))
    return pl.pallas_call(
        body, name="out_proj", grid=(T // TM,),
        in_specs=[col(512), col(512), const((D, D)), row(D), row(D)],
        out_specs=(const((1, 1)), row(D), col(512), col(512), const((D, D))),
        out_shape=(jax.ShapeDtypeStruct((1, 1), F32), jax.ShapeDtypeStruct((T, D), F32),
                   jax.ShapeDtypeStruct((512, T), F32), jax.ShapeDtypeStruct((512, T), F32),
                   jax.ShapeDtypeStruct((D, D), F32)),
        compiler_params=_cparams(1),
    )(mix_r, mix_s, wo, x, target)


LOSS_ROW = 5
ALL_CHIPS = ((0, 0), (0, 1), (1, 0), (1, 1))


def _w_out_reduce_scatter(gwo):
    def body(gwo_ref, gout_o, own, rcv, snd, got, tot, local_sems, a_send, a_recv, b_send, b_recv):
        k = pl.program_id(0)
        x_, y_, c_ = _mesh_pos()
        sibling = (x_, y_, 1 - c_)
        rel_chips = [(1 - x_, y_), (x_, 1 - y_), (1 - x_, 1 - y_)]

        def to_sibling(j):
            px, py = ALL_CHIPS[j]
            return pltpu.make_async_remote_copy(src_ref=gwo_ref.at[_blk(px, py, 1 - c_)], dst_ref=rcv.at[j],
                                                send_sem=a_send.at[j], recv_sem=a_recv.at[j], device_id=sibling,
                                                device_id_type=MESH)

        def local(j):
            px, py = ALL_CHIPS[j]
            return pltpu.make_async_copy(gwo_ref.at[_blk(px, py, c_)], own.at[j], local_sems.at[j])

        def to_chip(q):
            return pltpu.make_async_remote_copy(src_ref=snd.at[q], dst_ref=got.at[q], send_sem=b_send.at[q],
                                                recv_sem=b_recv.at[q], device_id=(*rel_chips[q], c_), device_id_type=MESH)

        @pl.when(k == 0)
        def _():
            for j in range(len(ALL_CHIPS)):
                to_sibling(j).start()
                local(j).start()

        @pl.when(k == 2)
        def _():
            for j in range(len(ALL_CHIPS)):
                local(j).wait()
                to_sibling(j).wait_recv()
            for q in range(3):
                j = 2 * rel_chips[q][0] + rel_chips[q][1]
                snd[q] = _bf(own[j] + rcv[j])
                to_chip(q).start()
            jm = 2 * x_ + y_
            tot[...] = own[jm] + rcv[jm]

        @pl.when(k == N_CH)
        def _():
            g = tot[...]
            for q in range(3):
                to_chip(q).wait_recv()
                g = g + got[q].astype(F32)
            gout_o[...] = g
            for j in range(len(ALL_CHIPS)):
                to_sibling(j).wait_send()
            for q in range(3):
                to_chip(q).wait_send()

    dma = pltpu.SemaphoreType.DMA
    return _Part(
        body, inputs=[gwo], in_specs=[pl.BlockSpec(memory_space=pl.ANY)],
        out_specs=[pl.BlockSpec((WOUT_BLK, D), lambda k: (0, 0))], out_shape=[jax.ShapeDtypeStruct((WOUT_BLK, D), F32)],
        scratch_shapes=[pltpu.VMEM((4, WOUT_BLK, D), F32), pltpu.VMEM((4, WOUT_BLK, D), F32),
                        pltpu.VMEM((3, WOUT_BLK, D), BF16), pltpu.VMEM((3, WOUT_BLK, D), BF16),
                        pltpu.VMEM((WOUT_BLK, D), F32), dma((4,)), dma((4,)), dma((4,)), dma((3,)), dma((3,))])


def _in_proj_bwd_rs(d_ret, d_swa, h, wt, x, rstd, norm_w, dy, sse, small_acc):
    n_blocks, n_tiles = N_DEV, T // TM
    last = n_blocks + n_tiles - 1

    def body(da_any, db_any, da_ref, db_ref, h_ref, w_ref, x_ref, r_ref, nw_ref, dy_ref, sse_ref,
             dbias_ref, bucket_ref, dsink_ref, dqw_ref, dkw_ref, drw_ref, gx_ref, gin_o, suma_o, sumb_o,
             stage, d2d_src, mine, rcv, snd, got, relay_rcv, tot, tab_a, tab_b, gnw_acc,
             rw_ref, qw_ref, kw_ref, sk_ref, rb_ref,
             dp_sems, d2d_send, d2d_recv, ici_send, ici_recv, s_send, s_recv):
        k = pl.program_id(0)
        x_, y_, c_ = _mesh_pos()
        me = _blk(x_, y_, c_)
        sibling = (x_, y_, 1 - c_)
        rel_chips = [(1 - x_, 1 - y_), (x_ ^ (1 - c_), y_ ^ c_), (x_ ^ c_, y_ ^ (1 - c_)), (x_, y_)]

        def block_of_step(s):
            p = s // 2
            if s % 2 == 0:
                return _blk(*rel_chips[{1: 2, 2: 1}.get(p, p)], 1 - c_)
            return _blk(*rel_chips[p], c_)

        def fetch(s, wait):
            slot, b = s % 2, block_of_step(s)
            split = RET_W - 4 * WIN_BLK

            def run(src, dst, sem):
                cp = pltpu.make_async_copy(src, dst, sem)
                cp.wait() if wait else cp.start()

            @pl.when(b < 4)
            def _():
                run(da_any.at[pl.ds(pl.multiple_of(b * WIN_BLK, 16), WIN_BLK), :], stage.at[slot], dp_sems.at[slot, 0])

            @pl.when(b == 4)
            def _():
                run(da_any.at[pl.ds(4 * WIN_BLK, split), :], stage.at[slot, pl.ds(0, split), :], dp_sems.at[slot, 0])
                run(db_any.at[pl.ds(0, WIN_BLK - split), :], stage.at[slot, pl.ds(split, WIN_BLK - split), :],
                    dp_sems.at[slot, 1])

            @pl.when(b > 4)
            def _():
                run(db_any.at[pl.ds(pl.multiple_of(b * WIN_BLK - RET_W, 16), WIN_BLK), :], stage.at[slot],
                    dp_sems.at[slot, 0])

        def d2d_copy(r):
            return pltpu.make_async_remote_copy(src_ref=d2d_src, dst_ref=rcv.at[r], send_sem=d2d_send.at[r],
                                                recv_sem=d2d_recv.at[r], device_id=sibling, device_id_type=MESH)

        def ici_copy(r):
            to = rel_chips[max(r, 1)]
            dst, recv = (relay_rcv, 2) if r == 0 else (got.at[r - 1], r - 1)
            return pltpu.make_async_remote_copy(src_ref=snd.at[r], dst_ref=dst, send_sem=ici_send.at[r],
                                                recv_sem=ici_recv.at[recv], device_id=(*to, c_), device_id_type=MESH)

        def table_copies():
            return [pltpu.make_async_remote_copy(src_ref=tab.at[me], dst_ref=tab.at[me], send_sem=s_send.at[a, p - 1],
                                                 recv_sem=s_recv.at[a, p - 1],
                                                 device_id=(x_ ^ (p >> 2), y_ ^ ((p >> 1) & 1), c_ ^ (p & 1)),
                                                 device_id_type=MESH)
                    for p in range(1, N_DEV) for a, tab in enumerate((tab_a, tab_b))]

        def chip_sum(r):
            d2d_copy(r).wait_recv()
            total = mine[...] + rcv[r]
            if r == 2:
                ici_copy(0).wait_recv()
                total = total + relay_rcv[...].astype(F32)
            if r < 3:
                snd[r] = _bf(total)
                ici_copy(r).start()
            else:
                tot[...] = total

        for s in range(n_blocks):
            @pl.when(k == s)
            def _(s=s):
                r = s // 2
                if s == 0:
                    gnw_acc[...] = jnp.zeros_like(gnw_acc)
                    fetch(0, wait=False)
                fetch(s, wait=True)
                if s + 1 < n_blocks:
                    fetch(s + 1, wait=False)
                gw = _dot(stage[s % 2], h_ref[...])
                if s % 2 == 0:
                    if s >= 2:
                        chip_sum(r - 1)
                        d2d_copy(r - 1).wait_send()
                    d2d_src[...] = gw
                    d2d_copy(r).start()
                else:
                    mine[...] = gw

        @pl.when(k == n_blocks)
        def _():
            chip_sum(3)

        @pl.when(k >= n_blocks)
        def _():
            dp = jnp.concatenate([da_ref[...], db_ref[...]], axis=0)
            xv, r, nw = x_ref[...], r_ref[...], nw_ref[...]
            dh = lax.dot_general(dp, w_ref[...], (((0,), (0,)), ((), ())), preferred_element_type=F32)
            u = dh * nw
            gx_ref[...] = dy_ref[...] + r * u - xv * (r * r * r) * jnp.mean(u * xv, axis=-1, keepdims=True)
            gnw_acc[...] += jnp.sum(dh * (xv * r), axis=0, keepdims=True)

        @pl.when(k == n_blocks)
        def _():
            _finish_small_grads(dbias_ref, bucket_ref, dsink_ref, dqw_ref, dkw_ref, drw_ref,
                                rw_ref, qw_ref, kw_ref, sk_ref, rb_ref)

        @pl.when(k == last)
        def _():
            mine_a, mine_b = tab_a.at[me], tab_b.at[me]
            mine_a[...] = jnp.zeros_like(mine_a)
            mine_b[...] = jnp.zeros_like(mine_b)
            for row, ref in enumerate((gnw_acc, rw_ref, qw_ref, kw_ref, sk_ref)):
                mine_a[row:row + 1, 0:ref.shape[1]] = ref[...]
            mine_a[LOSS_ROW:LOSS_ROW + 1, 0:1] = sse_ref[...]
            mine_b[:, 0:N_BUCKETS] = rb_ref[...]
            tables = table_copies()
            for cp in tables:
                cp.start()
            g_in = tot[...]
            for q in range(2):
                ici_copy(q + 1).wait_recv()
                g_in = g_in + got[q].astype(F32)
            gin_o[...] = g_in
            for cp in tables:
                cp.wait_recv()
            sum_a, sum_b = tab_a[0], tab_b[0]
            for b in range(1, N_DEV):
                sum_a = sum_a + tab_a[b]
                sum_b = sum_b + tab_b[b]
            suma_o[...] = sum_a
            sumb_o[...] = sum_b
            d2d_copy(3).wait_send()
            for q in range(3):
                ici_copy(q).wait_send()
            for cp in tables:
                cp.wait_send()

    tile = lambda k: jnp.maximum(k - n_blocks, 0)
    row = lambda w: pl.BlockSpec((TM, w), lambda k: (tile(k), 0))
    col = lambda w: pl.BlockSpec((w, TM), lambda k: (0, tile(k)))
    const = lambda shape, **kw: pl.BlockSpec(shape, lambda k: (0,) * len(shape), **kw)
    once = dict(pipeline_mode=pl.Buffered(1))
    hbm = pl.BlockSpec(memory_space=pl.ANY)
    dma = pltpu.SemaphoreType.DMA
    return pl.pallas_call(
        body, name="in_proj_bwd_rs", grid=(n_blocks + n_tiles,),
        in_specs=[hbm, hbm, col(RET_W), col(SWA_W), const((T, D), **once), const((D_IN, D), **once),
                  row(D), row(1), const((1, D)), row(D), const((1, 1))]
                 + [const(a.shape) for a in small_acc],
        out_specs=(row(D), const((WIN_BLK, D)), const((8, D)), const((SWA_H, CH))),
        out_shape=(jax.ShapeDtypeStruct((T, D), F32), jax.ShapeDtypeStruct((WIN_BLK, D), F32),
                   jax.ShapeDtypeStruct((8, D), F32), jax.ShapeDtypeStruct((SWA_H, CH), F32)),
        scratch_shapes=[
            pltpu.VMEM((2, WIN_BLK, T), BF16), pltpu.VMEM((WIN_BLK, D), F32),
            pltpu.VMEM((WIN_BLK, D), F32),
            pltpu.VMEM((4, WIN_BLK, D), F32), pltpu.VMEM((3, WIN_BLK, D), BF16),
            pltpu.VMEM((2, WIN_BLK, D), BF16), pltpu.VMEM((WIN_BLK, D), BF16),
            pltpu.VMEM((WIN_BLK, D), F32),
            pltpu.VMEM((N_DEV, 8, D), F32), pltpu.VMEM((N_DEV, SWA_H, CH), F32),
            pltpu.VMEM((1, D), F32),
        ] + [pltpu.VMEM(s, F32) for s in SMALL_GRAD_SHAPES] + [
            dma((2, 2)), dma((4,)), dma((4,)), dma((3,)), dma((3,)), dma((2, 7)), dma((2, 7)),
        ],
        compiler_params=_cparams(1),
    )(d_ret, d_swa, d_ret, d_swa, h, wt, x, rstd, norm_w, dy, sse, *small_acc)


SMALL_SHAPES = ((1, D), (1, 512), (1, SWA_D), (1, SWA_D), (1, SWA_H), (SWA_H, N_BUCKETS))


def _adamw_all(g_in_t, g_out, sum_a, sum_b, w_in_t, m_in_t, v_in_t, w_out, m_out, v_out, small_w, small_m, small_v):
    n_small = len(SMALL_SHAPES)
    halves = 2

    def body(*refs):
        gin_ref, gout_ref, suma_ref, sumb_ref, w_in_ref, m_in_ref, v_in_ref, w_out_ref, m_out_ref, v_out_ref = refs[:10]
        pos = 10
        sw_refs, sm_refs, sv_refs = (refs[pos + i * n_small:pos + (i + 1) * n_small] for i in range(3))
        pos += 3 * n_small
        g_in_o, d_in_o, nm_in_o, nv_in_o, g_out_o, d_out_o, nm_out_o, nv_out_o = refs[pos:pos + 8]
        pos += 8
        sg_o, sd_o, snm_o, snv_o = (refs[pos + i * n_small:pos + (i + 1) * n_small] for i in range(4))
        loss_o = refs[pos + 4 * n_small]

        g = gout_ref[...]
        d, nm, nv = _adamw(w_out_ref[...], g, m_out_ref[...], v_out_ref[...])
        g_out_o[...], d_out_o[...], nm_out_o[...], nv_out_o[...] = g, d, nm, nv
        g = gin_ref[...]
        d, nm, nv = _adamw(w_in_ref[...], g, m_in_ref[...], v_in_ref[...])
        g_in_o[...], d_in_o[...], nm_in_o[...], nv_in_o[...] = g, d, nm, nv

        @pl.when(pl.program_id(0) == 0)
        def _():
            loss_o[...] = suma_ref[LOSS_ROW:LOSS_ROW + 1, 0:1] * (0.5 / D)
            for r, (rows, lanes) in enumerate(SMALL_SHAPES):
                g = suma_ref[r:r + 1, 0:lanes] if rows == 1 else sumb_ref[:, 0:lanes]
                d, nm, nv = _adamw(sw_refs[r][...], g, sm_refs[r][...], sv_refs[r][...])
                sg_o[r][...], sd_o[r][...], snm_o[r][...], snv_o[r][...] = g, d, nm, nv

    half = lambda rows: pl.BlockSpec((rows // halves, D), lambda i: (i, 0))
    const = lambda shape: pl.BlockSpec(shape, lambda i: (0,) * len(shape))
    win = jax.ShapeDtypeStruct((WIN_BLK, D), F32)
    wout = jax.ShapeDtypeStruct((WOUT_BLK, D), F32)
    smalls = tuple(jax.ShapeDtypeStruct(s, F32) for s in SMALL_SHAPES)
    small_specs = [const(s) for s in SMALL_SHAPES]
    outs = pl.pallas_call(
        body, name="adamw_all", grid=(halves,),
        in_specs=[half(WIN_BLK), half(WOUT_BLK), const((8, D)), const((SWA_H, CH))] + [half(WIN_BLK)] * 3
                 + [half(WOUT_BLK)] * 3 + small_specs * 3,
        out_specs=tuple([half(WIN_BLK)] * 4 + [half(WOUT_BLK)] * 4 + small_specs * 4 + [const((1, 1))]),
        out_shape=(win,) * 4 + (wout,) * 4 + smalls * 4 + (jax.ShapeDtypeStruct((1, 1), F32),),
        compiler_params=_cparams(1, vmem=32 * 1024 * 1024),
    )(g_in_t, g_out, sum_a, sum_b, w_in_t, m_in_t, v_in_t, w_out, m_out, v_out, *small_w, *small_m, *small_v)
    big, rest = outs[:8], outs[8:]
    return big, [rest[i * n_small:(i + 1) * n_small] for i in range(4)], rest[4 * n_small]


def _small_rows(norm_w, ret_norm_w, q_norm_w, k_norm_w, sinks, rel_bias):
    return (norm_w.reshape(1, D), ret_norm_w.reshape(1, 512), q_norm_w.reshape(1, SWA_D), k_norm_w.reshape(1, SWA_D),
            sinks.reshape(1, SWA_H), rel_bias.T)


def _small_leaves(rows):
    return (rows[0].reshape(D), rows[1].reshape(512), rows[2].reshape(SWA_D), rows[3].reshape(SWA_D),
            rows[4].reshape(SWA_H), rows[5].T)


def kernel(x, norm_w, w_in, ret_norm_w, q_norm_w, k_norm_w, sinks, rel_bias, w_out, loss_target, m_norm_w, m_w_in, m_ret_norm_w, m_q_norm_w, m_k_norm_w, m_sinks, m_rel_bias, m_w_out, v_norm_w, v_w_in, v_ret_norm_w, v_q_norm_w, v_k_norm_w, v_sinks, v_rel_bias, v_w_out):
    x2 = x.reshape(T, D)
    target = loss_target.reshape(T, D)
    nw = norm_w.reshape(1, D)
    rnw_col = ret_norm_w.reshape(RET_V, 1)
    qnw_col = q_norm_w.reshape(SWA_D, 1)
    knw_col = k_norm_w.reshape(SWA_D, 1)
    relb = rel_bias.T.reshape(SWA_H * N_BUCKETS)
    ret_tables = _rotary_tables_t() + _retention_tables_t()
    bucket_t = _bucket_table_t()

    proj, wt, rstd, h = _in_proj_gather(x2, nw, w_in.T)
    (ro, mix_r, states, wo), (so, mix_s) = _fused_call("attn_fwd", N_CH, [
        _retention_fwd(proj, rnw_col, ret_tables, w_out),
        _swa_fwd(proj, qnw_col, knw_col, sinks, relb, bucket_t)])
    sse, dy, dmix_r, dmix_s, gwo = _out_proj(mix_r, mix_s, wo.reshape(D, D), x2, target)

    (d_ret, drw_acc), (d_swa, dbias, dsink_acc, dqw_acc, dkw_acc), (g_out,) = _fused_call("attn_bwd", N_CH + 1, [
        _retention_bwd(proj, ro, states, dmix_r, rnw_col, ret_tables),
        _swa_bwd(proj, so, dmix_s, qnw_col, knw_col, sinks, relb, bucket_t),
        _w_out_reduce_scatter(gwo.reshape(N_DEV, WOUT_BLK, D))])
    grad_x, g_in_t, sum_a, sum_b = _in_proj_bwd_rs(d_ret, d_swa, h, wt, x2, rstd, nw, dy, sse,
                                                   (dbias, bucket_t, dsink_acc, dqw_acc, dkw_acc, drw_acc))

    small_w = _small_rows(norm_w, ret_norm_w, q_norm_w, k_norm_w, sinks, rel_bias)
    small_m = _small_rows(m_norm_w, m_ret_norm_w, m_q_norm_w, m_k_norm_w, m_sinks, m_rel_bias)
    small_v = _small_rows(v_norm_w, v_ret_norm_w, v_q_norm_w, v_k_norm_w, v_sinks, v_rel_bias)
    big, small, loss = _adamw_all(g_in_t, g_out, sum_a, sum_b, w_in.T, m_w_in.T, v_w_in.T, w_out, m_w_out, v_w_out,
                                  small_w, small_m, small_v)

    def leaves(i):
        a = _small_leaves(small[i])
        return (a[0], big[i].T, a[1], a[2], a[3], a[4], a[5], big[4 + i])

    return (loss.reshape(()), grad_x.reshape(1, T, D), *leaves(0), *leaves(1), *leaves(2), *leaves(3))
```

```python
from typing import Callable, NamedTuple

import numpy as np
import jax
import jax.numpy as jnp
from jax import lax
from jax.experimental import pallas as pl
from jax.experimental.pallas import tpu as pltpu

F32 = jnp.float32
BF16 = jnp.bfloat16
MESH = pl.DeviceIdType.MESH

T = 2048
D = 1024
D_IN = 2816
N_DEV = 8
WIN_BLK = D_IN // N_DEV
WOUT_BLK = D // N_DEV
CH = 128
N_CH = T // CH
RET_H, RET_DK, RET_DV = 4, 64, 128
RET_QK = RET_H * RET_DK
RET_V = RET_H * RET_DV
SWA_H, SWA_KV, SWA_D, SWA_G = 8, 2, 64, 4
N_BUCKETS = 32
NORM_EPS = 1e-6
GN_EPS = 1e-5
NEG_INF = -1e30
PIECES = (256, 256, 512, 512, 512, 128, 128, 512)
OFFS = tuple(int(v) for v in np.cumsum((0,) + PIECES))
RET_W = OFFS[4]
SWA_W = D_IN - RET_W
SWA_OFFS = tuple(o - RET_W for o in OFFS[4:])
TM = 256

ADAM_LR, ADAM_B1, ADAM_B2, ADAM_EPS, ADAM_WD, ADAM_STEP = 0.001, 0.9, 0.999, 1e-08, 0.01, 10

VMEM_LIMIT = 56 * 1024 * 1024


def _cparams(n_grid=0, vmem=VMEM_LIMIT):
    sem = ("arbitrary",) * n_grid if n_grid else None
    return pltpu.CompilerParams(dimension_semantics=sem, vmem_limit_bytes=vmem)


class _Part(NamedTuple):
    step: Callable
    inputs: list
    in_specs: list
    out_specs: list
    out_shape: list
    scratch_shapes: list


def _fused_call(name, n_steps, parts):
    n_in = [len(p.inputs) for p in parts]
    n_out = [len(p.out_shape) for p in parts]
    n_scr = [len(p.scratch_shapes) for p in parts]

    def body(*refs):
        ins, outs, scr = refs[:sum(n_in)], refs[sum(n_in):sum(n_in) + sum(n_out)], refs[sum(n_in) + sum(n_out):]
        for i, p in enumerate(parts):
            take = lambda seq, counts: seq[sum(counts[:i]):sum(counts[:i + 1])]
            p.step(*take(ins, n_in), *take(outs, n_out), *take(scr, n_scr))

    flat = lambda field: [v for p in parts for v in getattr(p, field)]
    outs = pl.pallas_call(
        body, name=name, grid=(n_steps,), in_specs=flat("in_specs"), out_specs=tuple(flat("out_specs")),
        out_shape=tuple(flat("out_shape")), scratch_shapes=flat("scratch_shapes"), compiler_params=_cparams(1),
    )(*flat("inputs"))
    return [list(outs[sum(n_out[:i]):sum(n_out[:i + 1])]) for i in range(len(parts))]


def _dot(a, b):
    return jnp.dot(a, b, preferred_element_type=F32)


def _dot_nt(a, b):
    return lax.dot_general(a, b, (((1,), (1,)), ((), ())), preferred_element_type=F32)


def _bf(a):
    return a.astype(BF16)


def _sigmoid(x):
    return 1.0 / (1.0 + jnp.exp(-x))


def _adamw(w, g, m, v):
    m = ADAM_B1 * m + (1.0 - ADAM_B1) * g
    v = ADAM_B2 * v + (1.0 - ADAM_B2) * (g * g)
    m_hat = m / (1.0 - ADAM_B1 ** ADAM_STEP)
    v_hat = v / (1.0 - ADAM_B2 ** ADAM_STEP)
    delta = -ADAM_LR * (m_hat / (jnp.sqrt(v_hat) + ADAM_EPS) + ADAM_WD * w)
    return delta, m, v


def _rotary_tables_t():
    half = RET_DK // 2
    inv_freq = np.float32(10000.0) ** (-np.arange(half, dtype=np.float32) / np.float32(half))
    ang = inv_freq[:, None] * np.arange(T, dtype=np.float32)[None, :]
    cos, sin = np.cos(ang).astype(np.float32), np.sin(ang).astype(np.float32)
    cos64 = np.concatenate([cos, cos], axis=0)
    sin64 = np.concatenate([-sin, sin], axis=0)
    return np.tile(cos64, (RET_H, 1)), np.tile(sin64, (RET_H, 1))


def _retention_tables_t():
    gamma = (1.0 - np.exp2(-5.0 - np.arange(RET_H, dtype=np.float32))).astype(np.float32)
    log_g = np.log(gamma).astype(np.float32)
    i = np.arange(CH, dtype=np.float32)
    diff = i[None, :] - i[:, None]
    decay = np.where(diff >= 0, np.exp(log_g[:, None, None] * np.maximum(diff, 0.0)), 0.0).astype(np.float32)
    decay_all = np.concatenate(list(decay), axis=1)
    zeta = np.exp(log_g[:, None] * (CH - 1.0 - i)).astype(np.float32)
    zeta_tab = np.repeat(zeta.T, RET_DK, axis=1)
    xi = np.exp(log_g[:, None] * (i + 1.0)).astype(np.float32)
    xi_tab = np.repeat(xi, RET_DK, axis=0)
    chunk_decay = np.exp(log_g * np.float32(CH)).astype(np.float32)
    row_head = np.arange(RET_V)[:, None] // RET_DV
    col_head = np.arange(RET_QK)[None, :] // RET_DK
    state_mask = (row_head == col_head).astype(np.float32)
    state_decay = (state_mask * chunk_decay[row_head]).astype(np.float32)
    q_mask = (np.arange(RET_QK)[:, None] // RET_DK == np.arange(RET_H * CH)[None, :] // CH).astype(np.float32)
    return (np.ascontiguousarray(decay_all), np.ascontiguousarray(zeta_tab), np.ascontiguousarray(xi_tab),
            state_mask, state_decay, q_mask)


def _bucket_table_t():
    qi = np.arange(CH)[:, None]
    kj = np.arange(2 * CH)[None, :]
    dist = qi + CH - kj
    n = np.maximum(dist, 0)
    max_exact = N_BUCKETS // 2
    nf = np.maximum(n, 1).astype(np.float32)
    large = max_exact + (np.log(nf / np.float32(max_exact)) / np.float32(np.log(CH / max_exact))
                         * np.float32(N_BUCKETS - max_exact)).astype(np.int32)
    large = np.minimum(large, N_BUCKETS - 1)
    bucket = np.where(n < max_exact, n, large)
    return np.ascontiguousarray(np.where((dist >= 0) & (dist < CH), bucket, -1).astype(np.int32).T)


def _mesh_pos():
    return lax.axis_index("x"), lax.axis_index("y"), lax.axis_index("c")


def _blk(px, py, pc):
    return 4 * px + 2 * py + pc


def _in_proj_gather(x, norm_w, w_in_t, w_out_blk):
    tchunk = 512

    def body(x_ref, nw_ref, win_ref, wout_ref, proj_ref, wt_ref, wo_ref, rstd_ref, h_ref, stage, send_sems, recv_sems,
             out_sems):
        x, y, c = _mesh_pos()
        me = _blk(x, y, c)
        sibling = (x, y, 1 - c)
        chips = [(1 - x, y), (x, 1 - y), (1 - x, 1 - y)]

        def copy(k, b, to, ref=wt_ref):
            a = 0 if ref is wt_ref else 1
            return pltpu.make_async_remote_copy(src_ref=ref.at[b], dst_ref=ref.at[b], send_sem=send_sems.at[a, k],
                                                recv_sem=recv_sems.at[a, k], device_id=to, device_id_type=MESH)

        wt_ref[me] = _bf(win_ref[...])
        wo_ref[me] = _bf(wout_ref[...])
        first = [copy(0, me, sibling)] + [copy(1 + j, me, (*chip, c)) for j, chip in enumerate(chips[:2])]
        first += [copy(0, me, sibling, wo_ref)] + [copy(1 + j, me, (*chip, c), wo_ref) for j, chip in enumerate(chips[:2])]
        for cp in first:
            cp.start()

        nw = nw_ref[...]
        for i in range(T // TM):
            rows = slice(i * TM, (i + 1) * TM)
            xv = x_ref[rows, :]
            r = lax.rsqrt(jnp.mean(xv * xv, axis=-1, keepdims=True) + NORM_EPS)
            h_ref[rows, :] = _bf(xv * r * nw)
            rstd_ref[rows, :] = r

        writes = []

        def project(b):
            k = len(writes)
            if k >= 2:
                writes[k - 2].wait()
            w = wt_ref[b]
            for t in range(T // tchunk):
                cols = slice(t * tchunk, (t + 1) * tchunk)
                stage[k % 2, :, cols] = _dot_nt(w, h_ref[cols, :])
            cp = pltpu.make_async_copy(stage.at[k % 2], proj_ref.at[b], out_sems.at[k % 2])
            cp.start()
            writes.append(cp)

        project(me)
        copy(0, _blk(x, y, 1 - c), (x, y, c)).wait_recv()
        project(_blk(x, y, 1 - c))
        copy(1, _blk(*chips[0], c), (x, y, c)).wait_recv()
        copy(2, _blk(*chips[1], c), (x, y, c)).wait_recv()
        relay_from, relay_to = _blk(x ^ (1 - c), y ^ c, c), (x ^ c, y ^ (1 - c), c)
        relay = copy(3, relay_from, relay_to)
        relay.start()
        passed = [relay]
        for j, chip in enumerate(chips):
            if j == 2:
                copy(3, _blk(*chip, c), (x, y, c)).wait_recv()
            fwd = copy(4 + j, _blk(*chip, c), sibling)
            fwd.start()
            passed.append(fwd)
            project(_blk(*chip, c))
            if j == 0:
                copy(1, _blk(*chips[0], c), (x, y, c), wo_ref).wait_recv()
                copy(2, _blk(*chips[1], c), (x, y, c), wo_ref).wait_recv()
                passed += [copy(3, relay_from, relay_to, wo_ref), copy(4, _blk(*chips[0], c), sibling, wo_ref),
                           copy(5, _blk(*chips[1], c), sibling, wo_ref)]
                for cp in passed[-3:]:
                    cp.start()
            copy(4 + j, _blk(*chip, 1 - c), (x, y, c)).wait_recv()
            project(_blk(*chip, 1 - c))
        copy(3, _blk(*chips[2], c), (x, y, c), wo_ref).wait_recv()
        passed.append(copy(6, _blk(*chips[2], c), sibling, wo_ref))
        passed[-1].start()
        copy(0, _blk(x, y, 1 - c), (x, y, c), wo_ref).wait_recv()
        for j, chip in enumerate(chips):
            copy(4 + j, _blk(*chip, 1 - c), (x, y, c), wo_ref).wait_recv()
        writes[-2].wait()
        writes[-1].wait()
        for cp in first + passed:
            cp.wait_send()

    vm = pl.BlockSpec(memory_space=pltpu.VMEM)
    proj, wt, wo, rstd, h = pl.pallas_call(
        body, name="in_proj_gather",
        out_shape=(jax.ShapeDtypeStruct((N_DEV, WIN_BLK, T), F32), jax.ShapeDtypeStruct((N_DEV, WIN_BLK, D), BF16),
                   jax.ShapeDtypeStruct((N_DEV, WOUT_BLK, D), BF16),
                   jax.ShapeDtypeStruct((T, 1), F32), jax.ShapeDtypeStruct((T, D), BF16)),
        in_specs=[vm, vm, vm, vm], out_specs=(pl.BlockSpec(memory_space=pl.ANY), vm, vm, vm, vm),
        scratch_shapes=[pltpu.VMEM((2, WIN_BLK, T), F32),
                        pltpu.SemaphoreType.DMA((2, 7)), pltpu.SemaphoreType.DMA((2, 7)), pltpu.SemaphoreType.DMA((2,))],
        compiler_params=_cparams(vmem=48 * 1024 * 1024),
    )(x, norm_w, w_in_t, w_out_blk)
    return proj.reshape(D_IN, T), wt.reshape(D_IN, D), wo.reshape(D, D), rstd, h


def _proj_piece(i, block_of):
    rows = PIECES[i]
    assert OFFS[i] % rows == 0
    return pl.BlockSpec((rows, CH), lambda n: (OFFS[i] // rows, block_of(n)))


def _proj_gate_halves(block_of):
    return [pl.BlockSpec((256, CH), lambda n, j=j: (OFFS[7] // 256 + j, block_of(n))) for j in range(2)]


def _swap_halves_t(t):
    half = RET_DK // 2
    parts = []
    for h in range(RET_H):
        parts += [t[h * RET_DK + half:(h + 1) * RET_DK], t[h * RET_DK:h * RET_DK + half]]
    return jnp.concatenate(parts, axis=0)


def _rotate(t, cos, sin):
    return t * cos + _swap_halves_t(t) * sin


def _group_norm_t(o):
    mu = jnp.mean(o, axis=0, keepdims=True)
    var = jnp.mean((o - mu) * (o - mu), axis=0, keepdims=True)
    rstd = lax.rsqrt(var + GN_EPS)
    return (o - mu) * rstd, rstd


def _retention_scores_t(q_t, k_rm_b, q_mask, decay_all):
    q_heads = _bf(jnp.concatenate([q_t] * RET_H, axis=1) * q_mask)
    return _dot(k_rm_b, q_heads) * decay_all


def _retention_fwd(proj, ret_norm_w, tables, gated):
    cos_t, sin_t, decay_all, zeta_tab, xi_tab, state_mask, state_decay, q_mask = tables

    def body(rq_ref, rk_ref, rv_ref, rg_ref, w_ref, cos_ref, sin_ref, decay_ref, zeta_ref, xi_ref, smask_ref,
             sdecay_ref, qmask_ref, ro_ref, st_ref, state):
        n = pl.program_id(0)

        @pl.when(n == 0)
        def _():
            state[...] = jnp.zeros_like(state)

        cos, sin = cos_ref[...], sin_ref[...]
        q_t = _rotate(rq_ref[...], cos, sin)
        k_t = _rotate(rk_ref[...], cos, sin) * (RET_DK ** -0.5)
        k_rm = k_t.T
        v_b = _bf(rv_ref[...])
        m_b = _bf(state[...])
        st_ref[0] = m_b
        scores_b = _bf(_retention_scores_t(q_t, _bf(k_rm), qmask_ref[...], decay_ref[...]))
        cross = _dot(m_b, _bf(q_t * xi_ref[...]))
        state[...] = state[...] * sdecay_ref[...] + _dot(v_b, _bf(k_rm * zeta_ref[...])) * smask_ref[...]
        heads = range(RET_H)
        rows = [slice(h * RET_DV, (h + 1) * RET_DV) for h in heads]
        o = [_dot(v_b[rows[h]], scores_b[:, h * CH:(h + 1) * CH]) + cross[rows[h]] for h in heads]
        mixed = []
        for h in heads:
            rn, _ = _group_norm_t(o[h])
            g = rg_ref[rows[h], :]
            ro_ref[rows[h], :] = o[h]
            mixed.append(rn * w_ref[rows[h], :] * (g * _sigmoid(g)))
        gated["r"] = jnp.concatenate(mixed, axis=0)

    col = lambda w: pl.BlockSpec((w, CH), lambda n: (0, n))
    const = lambda shape: pl.BlockSpec(shape, lambda n: (0,) * len(shape))
    cur = lambda n: n
    return _Part(
        body,
        inputs=[proj, proj, proj, proj, ret_norm_w, cos_t, sin_t, decay_all, zeta_tab, xi_tab, state_mask, state_decay,
                q_mask],
        in_specs=[_proj_piece(i, cur) for i in range(4)] + [
            const((RET_V, 1)), col(RET_QK), col(RET_QK),
            const(decay_all.shape), const(zeta_tab.shape), const(xi_tab.shape), const(state_mask.shape),
            const(state_decay.shape), const(q_mask.shape)],
        out_specs=[col(RET_V), pl.BlockSpec((1, RET_V, RET_QK), lambda n: (n, 0, 0))],
        out_shape=[jax.ShapeDtypeStruct((RET_V, T), F32), jax.ShapeDtypeStruct((N_CH, RET_V, RET_QK), BF16)],
        scratch_shapes=[pltpu.VMEM((RET_V, RET_QK), F32)])


def _retention_bwd(proj, ro, states, dmix, ret_norm_w, tables):
    cos_t, sin_t, decay_all, zeta_tab, xi_tab, state_mask, state_decay, q_mask = tables

    def body(*refs):
        @pl.when(pl.program_id(0) < N_CH)
        def _():
            chunk(*refs)

    def chunk(rq_ref, rk_ref, rv_ref, rg_ref, ro_ref, st_ref, dm_ref, w_ref, cos_ref, sin_ref, decay_ref, zeta_ref,
              xi_ref, smask_ref, sdecay_ref, qmask_ref, d_ref, dw_ref, gstate):
        i = pl.program_id(0)

        @pl.when(i == 0)
        def _():
            gstate[...] = jnp.zeros_like(gstate)
            dw_ref[...] = jnp.zeros_like(dw_ref)

        cos, sin = cos_ref[...], sin_ref[...]
        q_t = _rotate(rq_ref[...], cos, sin)
        k_t = _rotate(rk_ref[...], cos, sin) * (RET_DK ** -0.5)
        q_b, k_b = _bf(q_t), _bf(k_t)
        k_rm = k_t.T
        kz_b = _bf(k_rm * zeta_ref[...])
        qx_b = _bf(q_t * xi_ref[...])
        v_t = rv_ref[...]
        v_b = _bf(v_t)
        v_rm_b = _bf(v_t.T)
        decay = decay_ref[...]
        scores_b = _bf(_retention_scores_t(q_t, _bf(k_rm), qmask_ref[...], decay))
        heads = range(RET_H)
        rows = [slice(h * RET_DV, (h + 1) * RET_DV) for h in heads]
        qk_rows = [slice(h * RET_DK, (h + 1) * RET_DK) for h in heads]
        lanes = [slice(h * CH, (h + 1) * CH) for h in heads]
        do = []
        for h in heads:
            g, w, dm = rg_ref[rows[h], :], w_ref[rows[h], :], dm_ref[rows[h], :]
            rn, rstd = _group_norm_t(ro_ref[rows[h], :])
            sig = _sigmoid(g)
            silu = g * sig
            d_ref[OFFS[3] + h * RET_DV:OFFS[3] + (h + 1) * RET_DV, :] = _bf(dm * rn * w * (sig * (1.0 + g * (1.0 - sig))))
            dw_ref[rows[h], :] += dm * silu * rn
            drn = dm * silu * w
            do.append(rstd * (drn - jnp.mean(drn, axis=0, keepdims=True)
                              - rn * jnp.mean(drn * rn, axis=0, keepdims=True)))
        do_b = _bf(jnp.concatenate(do, axis=0))
        m_b = st_ref[0]
        g_all = gstate[...]
        g_b = _bf(g_all)
        dscores_b = [_bf(_dot(v_rm_b[:, rows[h]], do_b[rows[h]]) * decay[:, lanes[h]]) for h in heads]
        dq_cross = lax.dot_general(m_b, do_b, (((0,), (0,)), ((), ())), preferred_element_type=F32)
        dkz = _dot(v_rm_b, g_b)
        dv_cross = _dot_nt(g_b, kz_b)
        gstate[...] = g_all * sdecay_ref[...] + _dot_nt(do_b, qx_b) * smask_ref[...]
        dq = jnp.concatenate([_dot(k_b[qk_rows[h]], dscores_b[h]) for h in heads], axis=0) + dq_cross * xi_ref[...]
        dk = (jnp.concatenate([_dot_nt(q_b[qk_rows[h]], dscores_b[h]) for h in heads], axis=0)
              + (dkz * zeta_ref[...]).T) * (RET_DK ** -0.5)
        for h in heads:
            d_ref[OFFS[2] + h * RET_DV:OFFS[2] + (h + 1) * RET_DV, :] = _bf(
                _dot_nt(do_b[rows[h]], scores_b[:, lanes[h]]) + dv_cross[rows[h]])
        d_ref[OFFS[0]:OFFS[1], :] = _bf(dq * cos + _swap_halves_t(dq * sin))
        d_ref[OFFS[1]:OFFS[2], :] = _bf(dk * cos + _swap_halves_t(dk * sin))

    chunk_of = lambda i: N_CH - 1 - jnp.minimum(i, N_CH - 1)
    col = lambda w: pl.BlockSpec((w, CH), lambda i: (0, chunk_of(i)))
    const = lambda shape: pl.BlockSpec(shape, lambda i: (0,) * len(shape))
    return _Part(
        body,
        inputs=[proj, proj, proj, proj, ro, states, dmix, ret_norm_w, cos_t, sin_t, decay_all, zeta_tab, xi_tab,
                state_mask, state_decay, q_mask],
        in_specs=[_proj_piece(j, chunk_of) for j in range(4)] + [
                  col(RET_V),
                  pl.BlockSpec((1, RET_V, RET_QK), lambda i: (chunk_of(i), 0, 0)), col(RET_V),
                  const((RET_V, 1)), col(RET_QK), col(RET_QK),
                  const(decay_all.shape), const(zeta_tab.shape), const(xi_tab.shape), const(state_mask.shape),
                  const(state_decay.shape), const(q_mask.shape)],
        out_specs=[col(RET_W), const((RET_V, CH))],
        out_shape=[jax.ShapeDtypeStruct((RET_W, T), BF16), jax.ShapeDtypeStruct((RET_V, CH), F32)],
        scratch_shapes=[pltpu.VMEM((RET_V, RET_QK), F32)])


HQ_LANES = SWA_H * CH


def _head_lanes(hq):
    return slice(hq * CH, (hq + 1) * CH)


def _build_bias_t(bucket_ref, relb_ref, sinks_ref, bias_ref, sink_row):
    bt = bucket_ref[...]
    first = lax.broadcasted_iota(jnp.int32, bt.shape, 0) < CH
    for hq in range(SWA_H):
        b = jnp.full(bt.shape, NEG_INF, F32)
        for bk in range(N_BUCKETS):
            b = jnp.where(bt == bk, relb_ref[hq * N_BUCKETS + bk], b)
        bias_ref[0, :, _head_lanes(hq)] = b
        bias_ref[1, :, _head_lanes(hq)] = jnp.where(first, NEG_INF, b)
        sink_row[:, _head_lanes(hq)] = jnp.full((1, CH), sinks_ref[hq], F32)


def _rms_t(t, w_col):
    r = lax.rsqrt(jnp.mean(t * t, axis=0, keepdims=True) + NORM_EPS)
    return t * r * w_col, r


def _rms_t_bwd(dn, t, r, w_col):
    u = dn * w_col
    return r * u - t * (r * r * r) * jnp.mean(u * t, axis=0, keepdims=True), dn * t * r


def _norm_kv(k_t, kw):
    return jnp.concatenate([_rms_t(k_t[g * SWA_D:(g + 1) * SWA_D], kw)[0] for g in range(SWA_KV)], axis=0)


def _kv_slot(a, kv):
    z = jnp.zeros_like(a)
    return jnp.concatenate([a, z] if kv == 0 else [z, a], axis=0)


def _softmax_t(s, sink):
    m = jnp.maximum(jnp.max(s, axis=0, keepdims=True), sink)
    p = jnp.exp(s - m)
    e_sink = jnp.exp(sink - m)
    inv = 1.0 / (jnp.sum(p, axis=0, keepdims=True) + e_sink)
    return p * inv, e_sink * inv


def _gate_rows(sg_halves, hq):
    per_half = SWA_H // 2
    return sg_halves[hq // per_half][(hq % per_half) * SWA_D:(hq % per_half + 1) * SWA_D, :]


def _swa_fwd(proj, q_norm_w, k_norm_w, sinks, rel_bias_flat, bucket_t, gated):
    def body(sq_ref, skp_ref, skc_ref, svp_ref, svc_ref, sg_lo, sg_hi, qw_ref, kw_ref, sinks_ref, relb_ref, bucket_ref,
             so_ref, bias_ref, sink_row):
        n = pl.program_id(0)

        @pl.when(n == 0)
        def _():
            _build_bias_t(bucket_ref, relb_ref, sinks_ref, bias_ref, sink_row)

        var = (n == 0).astype(jnp.int32)
        qw, kw = qw_ref[...], kw_ref[...]
        kn_band = jnp.concatenate([_norm_kv(skp_ref[...], kw), _norm_kv(skc_ref[...], kw)], axis=1)
        kb_rm = _bf(kn_band.T)
        vband = _bf(jnp.concatenate([svp_ref[...], svc_ref[...]], axis=1))
        q_all = jnp.concatenate(
            [_bf(_kv_slot(_rms_t(sq_ref[hq * SWA_D:(hq + 1) * SWA_D, :], qw)[0] * (SWA_D ** -0.5), hq // SWA_G))
             for hq in range(SWA_H)], axis=1)
        probs, _ = _softmax_t(_dot(kb_rm, q_all) + bias_ref[var], sink_row[...])
        probs_b = _bf(probs)
        mixed = []
        for kv in range(SWA_KV):
            o = _dot(vband[kv * SWA_D:(kv + 1) * SWA_D], probs_b[:, kv * SWA_G * CH:(kv + 1) * SWA_G * CH])
            for j in range(SWA_G):
                rows = slice((kv * SWA_G + j) * SWA_D, (kv * SWA_G + j + 1) * SWA_D)
                oh = o[:, j * CH:(j + 1) * CH]
                g = _gate_rows((sg_lo, sg_hi), kv * SWA_G + j)
                so_ref[rows, :] = oh
                mixed.append(oh * (g * _sigmoid(g)))
        gated["s"] = jnp.concatenate(mixed, axis=0)

    col = lambda w: pl.BlockSpec((w, CH), lambda n: (0, n))
    const = lambda shape: pl.BlockSpec(shape, lambda n: (0,) * len(shape))
    smem = pl.BlockSpec(memory_space=pltpu.SMEM)
    cur = lambda n: n
    prev = lambda n: jnp.maximum(n - 1, 0)
    return _Part(
        body,
        inputs=[proj, proj, proj, proj, proj, proj, proj, q_norm_w, k_norm_w, sinks, rel_bias_flat, bucket_t],
        in_specs=[_proj_piece(4, cur), _proj_piece(5, prev), _proj_piece(5, cur), _proj_piece(6, prev),
                  _proj_piece(6, cur)] + _proj_gate_halves(cur) + [
                  const((SWA_D, 1)), const((SWA_D, 1)), smem, smem, const((2 * CH, CH))],
        out_specs=[col(512)],
        out_shape=[jax.ShapeDtypeStruct((512, T), F32)],
        scratch_shapes=[pltpu.VMEM((2, 2 * CH, HQ_LANES), F32), pltpu.VMEM((1, HQ_LANES), F32)])


def _swa_bwd(proj, so, dmix, q_norm_w, k_norm_w, sinks, rel_bias_flat, bucket_t):
    def body(sq_ref, skp_ref, skc_ref, svp_ref, svc_ref, sg_lo, sg_hi, so_ref, dm_ref, qw_ref, kw_ref, sinks_ref,
             relb_ref, bucket_ref, d_ref, dbias_ref, dsink_ref, dqw_ref, dkw_ref,
             bias_ref, sink_row, band_dk, band_dv, carry_dk, carry_dv, hold_dq, hold_dg):
        n = pl.program_id(0)
        slot = n % 2

        @pl.when(n == 0)
        def _():
            _build_bias_t(bucket_ref, relb_ref, sinks_ref, bias_ref, sink_row)
            for ref in (dbias_ref, dsink_ref, dqw_ref, dkw_ref, carry_dk, carry_dv):
                ref[...] = jnp.zeros_like(ref)

        qw, kw = qw_ref[...], kw_ref[...]

        @pl.when(n < N_CH)
        def _():
            var = (n == 0).astype(jnp.int32)
            kn_band = jnp.concatenate([_norm_kv(skp_ref[...], kw), _norm_kv(skc_ref[...], kw)], axis=1)
            kb_rm = _bf(kn_band.T)
            kn_band_s = _bf(kn_band * (SWA_D ** -0.5))
            vband_f = jnp.concatenate([svp_ref[...], svc_ref[...]], axis=1)
            vb_rm = _bf(vband_f.T)
            q_raw, q_rstd, qs_b, do_b = [], [], [], []
            for hq in range(SWA_H):
                rows = slice(hq * SWA_D, (hq + 1) * SWA_D)
                q_t = sq_ref[rows, :]
                qn, rq = _rms_t(q_t, qw)
                g = _gate_rows((sg_lo, sg_hi), hq)
                sig = _sigmoid(g)
                dm = dm_ref[rows, :]
                hold_dg[slot, rows, :] = _bf(dm * so_ref[rows, :] * (sig * (1.0 + g * (1.0 - sig))))
                q_raw.append(q_t)
                q_rstd.append(rq)
                qs_b.append(_bf(qn * (SWA_D ** -0.5)))
                do_b.append(_bf(dm * (g * sig)))
            q_all = jnp.concatenate([_kv_slot(qs_b[hq], hq // SWA_G) for hq in range(SWA_H)], axis=1)
            do_all = jnp.concatenate([_kv_slot(do_b[hq], hq // SWA_G) for hq in range(SWA_H)], axis=1)
            probs, p_sink = _softmax_t(_dot(kb_rm, q_all) + bias_ref[var], sink_row[...])
            dprobs = _dot(vb_rm, do_all)
            t = jnp.sum(probs * dprobs, axis=0, keepdims=True)
            dlog = probs * (dprobs - t)
            dsink_ref[...] += -(p_sink * t)
            dbias_ref[...] += dlog
            dlog_b, probs_b = _bf(dlog), _bf(probs)
            dkn, dvv = [], []
            for kv in range(SWA_KV):
                heads = range(kv * SWA_G, (kv + 1) * SWA_G)
                lanes = slice(kv * SWA_G * CH, (kv + 1) * SWA_G * CH)
                dvv.append(_dot_nt(jnp.concatenate([do_b[hq] for hq in heads], axis=1), probs_b[:, lanes]))
                dkn.append(_dot_nt(jnp.concatenate([qs_b[hq] for hq in heads], axis=1), dlog_b[:, lanes]))
                dqn = _dot(kn_band_s[kv * SWA_D:(kv + 1) * SWA_D], dlog_b[:, lanes])
                for j, hq in enumerate(heads):
                    dq_t, dqw_terms = _rms_t_bwd(dqn[:, j * CH:(j + 1) * CH], q_raw[hq], q_rstd[hq], qw)
                    hold_dq[slot, hq * SWA_D:(hq + 1) * SWA_D, :] = _bf(dq_t)
                    dqw_ref[...] += dqw_terms
            band_dk[...] = jnp.concatenate(dkn, axis=0)
            band_dv[...] = jnp.concatenate(dvv, axis=0)

        @pl.when(n == N_CH)
        def _():
            band_dk[...] = jnp.zeros_like(band_dk)
            band_dv[...] = jnp.zeros_like(band_dv)

        @pl.when(n >= 1)
        def _():
            dkn_prev = carry_dk[...] + band_dk[:, 0:CH]
            k_t = skp_ref[...]
            for kv in range(SWA_KV):
                rows = slice(kv * SWA_D, (kv + 1) * SWA_D)
                _, rk = _rms_t(k_t[rows], kw)
                dk_t, dkw_terms = _rms_t_bwd(dkn_prev[rows], k_t[rows], rk, kw)
                d_ref[SWA_OFFS[1] + kv * SWA_D:SWA_OFFS[1] + (kv + 1) * SWA_D, :] = _bf(dk_t)
                dkw_ref[...] += dkw_terms
            d_ref[SWA_OFFS[2]:SWA_OFFS[3], :] = _bf(carry_dv[...] + band_dv[:, 0:CH])
            d_ref[SWA_OFFS[0]:SWA_OFFS[1], :] = hold_dq[1 - slot]
            d_ref[SWA_OFFS[3]:SWA_OFFS[4], :] = hold_dg[1 - slot]

        carry_dk[...] = band_dk[:, CH:2 * CH]
        carry_dv[...] = band_dv[:, CH:2 * CH]

    cur_block = lambda n: jnp.minimum(n, N_CH - 1)
    prev_block = lambda n: jnp.maximum(n - 1, 0)
    col = lambda w: pl.BlockSpec((w, CH), lambda n: (0, cur_block(n)))
    prev = lambda w: pl.BlockSpec((w, CH), lambda n: (0, prev_block(n)))
    const = lambda shape: pl.BlockSpec(shape, lambda n: (0,) * len(shape))
    smem = pl.BlockSpec(memory_space=pltpu.SMEM)
    return _Part(
        body,
        inputs=[proj, proj, proj, proj, proj, proj, proj, so, dmix, q_norm_w, k_norm_w, sinks, rel_bias_flat, bucket_t],
        in_specs=[_proj_piece(4, cur_block), _proj_piece(5, prev_block), _proj_piece(5, cur_block),
                  _proj_piece(6, prev_block), _proj_piece(6, cur_block)] + _proj_gate_halves(cur_block) + [
                  col(512), col(512), const((SWA_D, 1)), const((SWA_D, 1)), smem, smem, const((2 * CH, CH))],
        out_specs=[prev(SWA_W), const((2 * CH, HQ_LANES)), const((1, HQ_LANES)),
                   const((SWA_D, CH)), const((SWA_D, CH))],
        out_shape=[jax.ShapeDtypeStruct((SWA_W, T), BF16),
                   jax.ShapeDtypeStruct((2 * CH, HQ_LANES), F32), jax.ShapeDtypeStruct((1, HQ_LANES), F32),
                   jax.ShapeDtypeStruct((SWA_D, CH), F32), jax.ShapeDtypeStruct((SWA_D, CH), F32)],
        scratch_shapes=[pltpu.VMEM((2, 2 * CH, HQ_LANES), F32), pltpu.VMEM((1, HQ_LANES), F32),
                        pltpu.VMEM((128, 2 * CH), F32), pltpu.VMEM((128, 2 * CH), F32),
                        pltpu.VMEM((128, CH), F32), pltpu.VMEM((128, CH), F32),
                        pltpu.VMEM((2, 512, CH), BF16), pltpu.VMEM((2, 512, CH), BF16)])


SMALL_GRAD_SHAPES = ((1, RET_V), (1, CH), (1, CH), (1, CH), (SWA_H, N_BUCKETS))


def _finish_small_grads(dbias_ref, bucket_ref, dsink_ref, dqw_ref, dkw_ref, drw_ref, rw_o, qw_o, kw_o, sink_o, relb_o):
    bt = bucket_ref[...]
    row = lax.broadcasted_iota(jnp.int32, (SWA_H, N_BUCKETS), 0)
    col = lax.broadcasted_iota(jnp.int32, (SWA_H, N_BUCKETS), 1)
    lane = lax.broadcasted_iota(jnp.int32, (1, CH), 1)
    acc = jnp.zeros((SWA_H, N_BUCKETS), F32)
    sink = jnp.zeros((1, CH), F32)
    for hq in range(SWA_H):
        d = dbias_ref[:, _head_lanes(hq)]
        for bk in range(N_BUCKETS):
            s = jnp.sum(jnp.sum(jnp.where(bt == bk, d, 0.0), axis=0, keepdims=True), axis=1, keepdims=True)
            acc = acc + jnp.where((row == hq) & (col == bk), s, 0.0)
        sink = sink + jnp.where(lane == hq, jnp.sum(dsink_ref[:, _head_lanes(hq)], axis=1, keepdims=True), 0.0)
    relb_o[...] = acc
    sink_o[...] = sink
    for src, dst in ((dqw_ref, qw_o), (dkw_ref, kw_o)):
        padded = jnp.concatenate([src[...], jnp.zeros((CH - SWA_D, CH), F32)], axis=0)
        dst[...] = jnp.sum(padded.T, axis=0, keepdims=True)
    rw_o[...] = jnp.sum(drw_ref[...].T, axis=0, keepdims=True)


def _out_proj(mixed_chunk, wo, x, target):
    def body(w_ref, x_ref, t_ref, loss_ref, dy_ref, dmr_ref, dms_ref, gw_ref, held):
        n = pl.program_id(0)

        @pl.when(n == 0)
        def _():
            loss_ref[...] = jnp.zeros_like(loss_ref)
            gw_ref[...] = jnp.zeros_like(gw_ref)

        @pl.when(n % 2 == 0)
        def _():
            held[...] = mixed_chunk()

        @pl.when(n % 2 == 1)
        def _():
            mixed = jnp.concatenate([held[...], mixed_chunk()], axis=1)
            w = w_ref[...]
            err = x_ref[...] + _dot(_bf(mixed.T), w) - t_ref[...]
            loss_ref[...] += jnp.sum(jnp.sum(err * err, axis=1, keepdims=True), axis=0, keepdims=True)
            dy = err * (1.0 / D)
            dy_ref[...] = dy
            dy_b = _bf(dy)
            dmix = _dot_nt(w, dy_b)
            dmr_ref[...] = dmix[0:RET_V]
            dms_ref[...] = dmix[RET_V:D]
            gw_ref[...] += _dot(_bf(mixed), dy_b)

    assert TM == 2 * CH
    row = lambda w: pl.BlockSpec((TM, w), lambda n: (n // 2, 0))
    col = lambda w: pl.BlockSpec((w, TM), lambda n: (0, n // 2))
    const = lambda shape, **kw: pl.BlockSpec(shape, lambda n: (0,) * len(shape), **kw)
    return _Part(
        body, inputs=[wo, x, target],
        in_specs=[const((D, D), pipeline_mode=pl.Buffered(1)), row(D), row(D)],
        out_specs=[const((1, 1)), row(D), col(RET_V), col(D - RET_V), const((D, D))],
        out_shape=[jax.ShapeDtypeStruct((1, 1), F32), jax.ShapeDtypeStruct((T, D), F32),
                   jax.ShapeDtypeStruct((RET_V, T), F32), jax.ShapeDtypeStruct((D - RET_V, T), F32),
                   jax.ShapeDtypeStruct((D, D), F32)],
        scratch_shapes=[pltpu.VMEM((D, CH), F32)])


LOSS_ROW = 5
ALL_CHIPS = ((0, 0), (0, 1), (1, 0), (1, 1))


def _w_out_reduce_scatter(gwo):
    def body(gwo_ref, gout_o, own, rcv, snd, got, tot, local_sems, a_send, a_recv, b_send, b_recv):
        k = pl.program_id(0)
        x_, y_, c_ = _mesh_pos()
        sibling = (x_, y_, 1 - c_)
        rel_chips = [(1 - x_, y_), (x_, 1 - y_), (1 - x_, 1 - y_)]

        def to_sibling(j):
            px, py = ALL_CHIPS[j]
            return pltpu.make_async_remote_copy(src_ref=gwo_ref.at[_blk(px, py, 1 - c_)], dst_ref=rcv.at[j],
                                                send_sem=a_send.at[j], recv_sem=a_recv.at[j], device_id=sibling,
                                                device_id_type=MESH)

        def local(j):
            px, py = ALL_CHIPS[j]
            return pltpu.make_async_copy(gwo_ref.at[_blk(px, py, c_)], own.at[j], local_sems.at[j])

        def to_chip(q):
            return pltpu.make_async_remote_copy(src_ref=snd.at[q], dst_ref=got.at[q], send_sem=b_send.at[q],
                                                recv_sem=b_recv.at[q], device_id=(*rel_chips[q], c_), device_id_type=MESH)

        @pl.when(k == 0)
        def _():
            for j in range(len(ALL_CHIPS)):
                to_sibling(j).start()
                local(j).start()

        @pl.when(k == 2)
        def _():
            for j in range(len(ALL_CHIPS)):
                local(j).wait()
                to_sibling(j).wait_recv()
            for q in range(3):
                j = 2 * rel_chips[q][0] + rel_chips[q][1]
                snd[q] = _bf(own[j] + rcv[j])
                to_chip(q).start()
            jm = 2 * x_ + y_
            tot[...] = own[jm] + rcv[jm]

        @pl.when(k == N_CH)
        def _():
            g = tot[...]
            for q in range(3):
                to_chip(q).wait_recv()
                g = g + got[q].astype(F32)
            gout_o[...] = g
            for j in range(len(ALL_CHIPS)):
                to_sibling(j).wait_send()
            for q in range(3):
                to_chip(q).wait_send()

    dma = pltpu.SemaphoreType.DMA
    return _Part(
        body, inputs=[gwo], in_specs=[pl.BlockSpec(memory_space=pl.ANY)],
        out_specs=[pl.BlockSpec((WOUT_BLK, D), lambda k: (0, 0))], out_shape=[jax.ShapeDtypeStruct((WOUT_BLK, D), F32)],
        scratch_shapes=[pltpu.VMEM((4, WOUT_BLK, D), F32), pltpu.VMEM((4, WOUT_BLK, D), F32),
                        pltpu.VMEM((3, WOUT_BLK, D), BF16), pltpu.VMEM((3, WOUT_BLK, D), BF16),
                        pltpu.VMEM((WOUT_BLK, D), F32), dma((4,)), dma((4,)), dma((4,)), dma((3,)), dma((3,))])


def _in_proj_bwd_rs(d_ret, d_swa, h, wt, x, rstd, norm_w, dy, sse, small_acc):
    n_blocks, n_tiles = N_DEV, T // TM
    last = n_blocks + n_tiles - 1

    def body(da_any, db_any, da_ref, db_ref, h_ref, w_ref, x_ref, r_ref, nw_ref, dy_ref, sse_ref,
             dbias_ref, bucket_ref, dsink_ref, dqw_ref, dkw_ref, drw_ref, gx_ref, gin_o, suma_o, sumb_o,
             stage, d2d_src, mine, rcv, snd, got, tot, tab_a, tab_b, gnw_acc, rw_ref, qw_ref, kw_ref, sk_ref, rb_ref,
             dp_sems, d2d_send, d2d_recv, ici_send, ici_recv, s_send, s_recv):
        k = pl.program_id(0)
        x_, y_, c_ = _mesh_pos()
        me = _blk(x_, y_, c_)
        sibling = (x_, y_, 1 - c_)
        rel_chips = [(1 - x_, y_), (x_, 1 - y_), (1 - x_, 1 - y_), (x_, y_)]

        def block_of_step(s):
            return _blk(*rel_chips[s // 2], 1 - c_ if s % 2 == 0 else c_)

        def fetch(s, wait):
            slot, b = s % 2, block_of_step(s)
            split = RET_W - 4 * WIN_BLK

            def run(src, dst, sem):
                cp = pltpu.make_async_copy(src, dst, sem)
                cp.wait() if wait else cp.start()

            @pl.when(b < 4)
            def _():
                run(da_any.at[pl.ds(pl.multiple_of(b * WIN_BLK, 16), WIN_BLK), :], stage.at[slot], dp_sems.at[slot, 0])

            @pl.when(b == 4)
            def _():
                run(da_any.at[pl.ds(4 * WIN_BLK, split), :], stage.at[slot, pl.ds(0, split), :], dp_sems.at[slot, 0])
                run(db_any.at[pl.ds(0, WIN_BLK - split), :], stage.at[slot, pl.ds(split, WIN_BLK - split), :],
                    dp_sems.at[slot, 1])

            @pl.when(b > 4)
            def _():
                run(db_any.at[pl.ds(pl.multiple_of(b * WIN_BLK - RET_W, 16), WIN_BLK), :], stage.at[slot],
                    dp_sems.at[slot, 0])

        def d2d_copy(r):
            return pltpu.make_async_remote_copy(src_ref=d2d_src, dst_ref=rcv.at[r], send_sem=d2d_send.at[r],
                                                recv_sem=d2d_recv.at[r], device_id=sibling, device_id_type=MESH)

        def ici_copy(r):
            return pltpu.make_async_remote_copy(src_ref=snd.at[r], dst_ref=got.at[r], send_sem=ici_send.at[r],
                                                recv_sem=ici_recv.at[r], device_id=(*rel_chips[r], c_),
                                                device_id_type=MESH)

        def table_copies():
            return [pltpu.make_async_remote_copy(src_ref=tab.at[me], dst_ref=tab.at[me], send_sem=s_send.at[a, p - 1],
                                                 recv_sem=s_recv.at[a, p - 1],
                                                 device_id=(x_ ^ (p >> 2), y_ ^ ((p >> 1) & 1), c_ ^ (p & 1)),
                                                 device_id_type=MESH)
                    for p in range(1, N_DEV) for a, tab in enumerate((tab_a, tab_b))]

        def chip_sum(r):
            d2d_copy(r).wait_recv()
            total = mine[...] + rcv[r]
            if r < 3:
                snd[r] = _bf(total)
                ici_copy(r).start()
            else:
                tot[...] = total

        for s in range(n_blocks):
            @pl.when(k == s)
            def _(s=s):
                r = s // 2
                if s == 0:
                    gnw_acc[...] = jnp.zeros_like(gnw_acc)
                    fetch(0, wait=False)
                fetch(s, wait=True)
                if s + 1 < n_blocks:
                    fetch(s + 1, wait=False)
                gw = _dot(stage[s % 2], h_ref[...])
                if s % 2 == 0:
                    if s >= 2:
                        chip_sum(r - 1)
                        d2d_copy(r - 1).wait_send()
                    d2d_src[...] = gw
                    d2d_copy(r).start()
                else:
                    mine[...] = gw

        @pl.when(k == n_blocks)
        def _():
            chip_sum(3)

        @pl.when(k >= n_blocks)
        def _():
            dp = jnp.concatenate([da_ref[...], db_ref[...]], axis=0)
            xv, r, nw = x_ref[...], r_ref[...], nw_ref[...]
            dh = lax.dot_general(dp, w_ref[...], (((0,), (0,)), ((), ())), preferred_element_type=F32)
            u = dh * nw
            gx_ref[...] = dy_ref[...] + r * u - xv * (r * r * r) * jnp.mean(u * xv, axis=-1, keepdims=True)
            gnw_acc[...] += jnp.sum(dh * (xv * r), axis=0, keepdims=True)

        @pl.when(k == n_blocks)
        def _():
            _finish_small_grads(dbias_ref, bucket_ref, dsink_ref, dqw_ref, dkw_ref, drw_ref,
                                rw_ref, qw_ref, kw_ref, sk_ref, rb_ref)

        @pl.when(k == last)
        def _():
            mine_a, mine_b = tab_a.at[me], tab_b.at[me]
            mine_a[...] = jnp.zeros_like(mine_a)
            mine_b[...] = jnp.zeros_like(mine_b)
            for row, ref in enumerate((gnw_acc, rw_ref, qw_ref, kw_ref, sk_ref)):
                mine_a[row:row + 1, 0:ref.shape[1]] = ref[...]
            mine_a[LOSS_ROW:LOSS_ROW + 1, 0:1] = sse_ref[...]
            mine_b[:, 0:N_BUCKETS] = rb_ref[...]
            tables = table_copies()
            for cp in tables:
                cp.start()
            g_in = tot[...]
            for q in range(3):
                ici_copy(q).wait_recv()
                g_in = g_in + got[q].astype(F32)
            gin_o[...] = g_in
            for cp in tables:
                cp.wait_recv()
            sum_a, sum_b = tab_a[0], tab_b[0]
            for b in range(1, N_DEV):
                sum_a = sum_a + tab_a[b]
                sum_b = sum_b + tab_b[b]
            suma_o[...] = sum_a
            sumb_o[...] = sum_b
            d2d_copy(3).wait_send()
            for q in range(3):
                ici_copy(q).wait_send()
            for cp in tables:
                cp.wait_send()

    tile = lambda k: jnp.maximum(k - n_blocks, 0)
    row = lambda w: pl.BlockSpec((TM, w), lambda k: (tile(k), 0))
    col = lambda w: pl.BlockSpec((w, TM), lambda k: (0, tile(k)))
    const = lambda shape, **kw: pl.BlockSpec(shape, lambda k: (0,) * len(shape), **kw)
    once = dict(pipeline_mode=pl.Buffered(1))
    hbm = pl.BlockSpec(memory_space=pl.ANY)
    dma = pltpu.SemaphoreType.DMA
    return pl.pallas_call(
        body, name="in_proj_bwd_rs", grid=(n_blocks + n_tiles,),
        in_specs=[hbm, hbm, col(RET_W), col(SWA_W), const((T, D), **once), const((D_IN, D), **once),
                  row(D), row(1), const((1, D)), row(D), const((1, 1))]
                 + [const(a.shape) for a in small_acc],
        out_specs=(row(D), const((WIN_BLK, D)), const((8, D)), const((SWA_H, CH))),
        out_shape=(jax.ShapeDtypeStruct((T, D), F32), jax.ShapeDtypeStruct((WIN_BLK, D), F32),
                   jax.ShapeDtypeStruct((8, D), F32), jax.ShapeDtypeStruct((SWA_H, CH), F32)),
        scratch_shapes=[
            pltpu.VMEM((2, WIN_BLK, T), BF16), pltpu.VMEM((WIN_BLK, D), F32),
            pltpu.VMEM((WIN_BLK, D), F32),
            pltpu.VMEM((4, WIN_BLK, D), F32), pltpu.VMEM((3, WIN_BLK, D), BF16),
            pltpu.VMEM((3, WIN_BLK, D), BF16), pltpu.VMEM((WIN_BLK, D), F32),
            pltpu.VMEM((N_DEV, 8, D), F32), pltpu.VMEM((N_DEV, SWA_H, CH), F32),
            pltpu.VMEM((1, D), F32),
        ] + [pltpu.VMEM(s, F32) for s in SMALL_GRAD_SHAPES] + [
            dma((2, 2)), dma((4,)), dma((4,)), dma((3,)), dma((3,)), dma((2, 7)), dma((2, 7)),
        ],
        compiler_params=_cparams(1),
    )(d_ret, d_swa, d_ret, d_swa, h, wt, x, rstd, norm_w, dy, sse, *small_acc)


SMALL_SHAPES = ((1, D), (1, 512), (1, SWA_D), (1, SWA_D), (1, SWA_H), (SWA_H, N_BUCKETS))


def _adamw_all(g_in_t, g_out, sum_a, sum_b, w_in_t, m_in_t, v_in_t, w_out, m_out, v_out, small_w, small_m, small_v):
    n_small = len(SMALL_SHAPES)
    halves = 2

    def body(*refs):
        gin_ref, gout_ref, suma_ref, sumb_ref, w_in_ref, m_in_ref, v_in_ref, w_out_ref, m_out_ref, v_out_ref = refs[:10]
        pos = 10
        sw_refs, sm_refs, sv_refs = (refs[pos + i * n_small:pos + (i + 1) * n_small] for i in range(3))
        pos += 3 * n_small
        g_in_o, d_in_o, nm_in_o, nv_in_o, g_out_o, d_out_o, nm_out_o, nv_out_o = refs[pos:pos + 8]
        pos += 8
        sg_o, sd_o, snm_o, snv_o = (refs[pos + i * n_small:pos + (i + 1) * n_small] for i in range(4))
        loss_o = refs[pos + 4 * n_small]

        g = gout_ref[...]
        d, nm, nv = _adamw(w_out_ref[...], g, m_out_ref[...], v_out_ref[...])
        g_out_o[...], d_out_o[...], nm_out_o[...], nv_out_o[...] = g, d, nm, nv
        g = gin_ref[...]
        d, nm, nv = _adamw(w_in_ref[...], g, m_in_ref[...], v_in_ref[...])
        g_in_o[...], d_in_o[...], nm_in_o[...], nv_in_o[...] = g, d, nm, nv

        @pl.when(pl.program_id(0) == 0)
        def _():
            loss_o[...] = suma_ref[LOSS_ROW:LOSS_ROW + 1, 0:1] * (0.5 / D)
            for r, (rows, lanes) in enumerate(SMALL_SHAPES):
                g = suma_ref[r:r + 1, 0:lanes] if rows == 1 else sumb_ref[:, 0:lanes]
                d, nm, nv = _adamw(sw_refs[r][...], g, sm_refs[r][...], sv_refs[r][...])
                sg_o[r][...], sd_o[r][...], snm_o[r][...], snv_o[r][...] = g, d, nm, nv

    half = lambda rows: pl.BlockSpec((rows // halves, D), lambda i: (i, 0))
    const = lambda shape: pl.BlockSpec(shape, lambda i: (0,) * len(shape))
    win = jax.ShapeDtypeStruct((WIN_BLK, D), F32)
    wout = jax.ShapeDtypeStruct((WOUT_BLK, D), F32)
    smalls = tuple(jax.ShapeDtypeStruct(s, F32) for s in SMALL_SHAPES)
    small_specs = [const(s) for s in SMALL_SHAPES]
    outs = pl.pallas_call(
        body, name="adamw_all", grid=(halves,),
        in_specs=[half(WIN_BLK), half(WOUT_BLK), const((8, D)), const((SWA_H, CH))] + [half(WIN_BLK)] * 3
                 + [half(WOUT_BLK)] * 3 + small_specs * 3,
        out_specs=tuple([half(WIN_BLK)] * 4 + [half(WOUT_BLK)] * 4 + small_specs * 4 + [const((1, 1))]),
        out_shape=(win,) * 4 + (wout,) * 4 + smalls * 4 + (jax.ShapeDtypeStruct((1, 1), F32),),
        compiler_params=_cparams(1, vmem=32 * 1024 * 1024),
    )(g_in_t, g_out, sum_a, sum_b, w_in_t, m_in_t, v_in_t, w_out, m_out, v_out, *small_w, *small_m, *small_v)
    big, rest = outs[:8], outs[8:]
    return big, [rest[i * n_small:(i + 1) * n_small] for i in range(4)], rest[4 * n_small]


def _small_rows(norm_w, ret_norm_w, q_norm_w, k_norm_w, sinks, rel_bias):
    return (norm_w.reshape(1, D), ret_norm_w.reshape(1, 512), q_norm_w.reshape(1, SWA_D), k_norm_w.reshape(1, SWA_D),
            sinks.reshape(1, SWA_H), rel_bias.T)


def _small_leaves(rows):
    return (rows[0].reshape(D), rows[1].reshape(512), rows[2].reshape(SWA_D), rows[3].reshape(SWA_D),
            rows[4].reshape(SWA_H), rows[5].T)


def kernel(x, norm_w, w_in, ret_norm_w, q_norm_w, k_norm_w, sinks, rel_bias, w_out, loss_target, m_norm_w, m_w_in, m_ret_norm_w, m_q_norm_w, m_k_norm_w, m_sinks, m_rel_bias, m_w_out, v_norm_w, v_w_in, v_ret_norm_w, v_q_norm_w, v_k_norm_w, v_sinks, v_rel_bias, v_w_out):
    x2 = x.reshape(T, D)
    target = loss_target.reshape(T, D)
    nw = norm_w.reshape(1, D)
    rnw_col = ret_norm_w.reshape(RET_V, 1)
    qnw_col = q_norm_w.reshape(SWA_D, 1)
    knw_col = k_norm_w.reshape(SWA_D, 1)
    relb = rel_bias.T.reshape(SWA_H * N_BUCKETS)
    ret_tables = _rotary_tables_t() + _retention_tables_t()
    bucket_t = _bucket_table_t()

    proj, wt, wo, rstd, h = _in_proj_gather(x2, nw, w_in.T, w_out)
    gated = {}
    (ro, states), (so,), (sse, dy, dmix_r, dmix_s, gwo) = _fused_call("layer_fwd", N_CH, [
        _retention_fwd(proj, rnw_col, ret_tables, gated),
        _swa_fwd(proj, qnw_col, knw_col, sinks, relb, bucket_t, gated),
        _out_proj(lambda: jnp.concatenate([gated["r"], gated["s"]], axis=0), wo, x2, target)])

    (d_ret, drw_acc), (d_swa, dbias, dsink_acc, dqw_acc, dkw_acc), (g_out,) = _fused_call("attn_bwd", N_CH + 1, [
        _retention_bwd(proj, ro, states, dmix_r, rnw_col, ret_tables),
        _swa_bwd(proj, so, dmix_s, qnw_col, knw_col, sinks, relb, bucket_t),
        _w_out_reduce_scatter(gwo.reshape(N_DEV, WOUT_BLK, D))])
    grad_x, g_in_t, sum_a, sum_b = _in_proj_bwd_rs(d_ret, d_swa, h, wt, x2, rstd, nw, dy, sse,
                                                   (dbias, bucket_t, dsink_acc, dqw_acc, dkw_acc, drw_acc))

    small_w = _small_rows(norm_w, ret_norm_w, q_norm_w, k_norm_w, sinks, rel_bias)
    small_m = _small_rows(m_norm_w, m_ret_norm_w, m_q_norm_w, m_k_norm_w, m_sinks, m_rel_bias)
    small_v = _small_rows(v_norm_w, v_ret_norm_w, v_q_norm_w, v_k_norm_w, v_sinks, v_rel_bias)
    big, small, loss = _adamw_all(g_in_t, g_out, sum_a, sum_b, w_in.T, m_w_in.T, v_w_in.T, w_out, m_w_out, v_w_out,
                                  small_w, small_m, small_v)

    def leaves(i):
        a = _small_leaves(small[i])
        return (a[0], big[i].T, a[1], a[2], a[3], a[4], a[5], big[4 + i])

    return (loss.reshape(()), grad_x.reshape(1, T, D), *leaves(0), *leaves(1), *leaves(2), *leaves(3))
```

```python
from typing import Callable, NamedTuple

import numpy as np
import jax
import jax.numpy as jnp
from jax import lax
from jax.experimental import pallas as pl
from jax.experimental.pallas import tpu as pltpu

F32 = jnp.float32
BF16 = jnp.bfloat16
MESH = pl.DeviceIdType.MESH

T = 2048
D = 1024
D_IN = 2816
N_DEV = 8
WIN_BLK = D_IN // N_DEV
WOUT_BLK = D // N_DEV
CH = 128
N_CH = T // CH
RET_H, RET_DK, RET_DV = 4, 64, 128
RET_QK = RET_H * RET_DK
RET_V = RET_H * RET_DV
SWA_H, SWA_KV, SWA_D, SWA_G = 8, 2, 64, 4
N_BUCKETS = 32
NORM_EPS = 1e-6
GN_EPS = 1e-5
NEG_INF = -1e30
PIECES = (256, 256, 512, 512, 512, 128, 128, 512)
OFFS = tuple(int(v) for v in np.cumsum((0,) + PIECES))
RET_W = OFFS[4]
SWA_W = D_IN - RET_W
SWA_OFFS = tuple(o - RET_W for o in OFFS[4:])
TM = 256

ADAM_LR, ADAM_B1, ADAM_B2, ADAM_EPS, ADAM_WD, ADAM_STEP = 0.001, 0.9, 0.999, 1e-08, 0.01, 10

VMEM_LIMIT = 56 * 1024 * 1024


def _cparams(n_grid=0, vmem=VMEM_LIMIT):
    sem = ("arbitrary",) * n_grid if n_grid else None
    return pltpu.CompilerParams(dimension_semantics=sem, vmem_limit_bytes=vmem)


class _Part(NamedTuple):
    step: Callable
    inputs: list
    in_specs: list
    out_specs: list
    out_shape: list
    scratch_shapes: list


def _fused_call(name, n_steps, parts):
    n_in = [len(p.inputs) for p in parts]
    n_out = [len(p.out_shape) for p in parts]
    n_scr = [len(p.scratch_shapes) for p in parts]

    def body(*refs):
        ins, outs, scr = refs[:sum(n_in)], refs[sum(n_in):sum(n_in) + sum(n_out)], refs[sum(n_in) + sum(n_out):]
        for i, p in enumerate(parts):
            take = lambda seq, counts: seq[sum(counts[:i]):sum(counts[:i + 1])]
            p.step(*take(ins, n_in), *take(outs, n_out), *take(scr, n_scr))

    flat = lambda field: [v for p in parts for v in getattr(p, field)]
    outs = pl.pallas_call(
        body, name=name, grid=(n_steps,), in_specs=flat("in_specs"), out_specs=tuple(flat("out_specs")),
        out_shape=tuple(flat("out_shape")), scratch_shapes=flat("scratch_shapes"), compiler_params=_cparams(1),
    )(*flat("inputs"))
    return [list(outs[sum(n_out[:i]):sum(n_out[:i + 1])]) for i in range(len(parts))]


def _dot(a, b):
    return jnp.dot(a, b, preferred_element_type=F32)


def _dot_nt(a, b):
    return lax.dot_general(a, b, (((1,), (1,)), ((), ())), preferred_element_type=F32)


def _bf(a):
    return a.astype(BF16)


def _sigmoid(x):
    return 1.0 / (1.0 + jnp.exp(-x))


def _row_to_cols(row):
    n = row.shape[1]
    if n < CH:
        row = jnp.concatenate([row, jnp.zeros((1, CH - n), row.dtype)], axis=1)
    return jnp.broadcast_to(row, (CH, CH)).T[0:n, :]


def _adamw(w, g, m, v):
    m = ADAM_B1 * m + (1.0 - ADAM_B1) * g
    v = ADAM_B2 * v + (1.0 - ADAM_B2) * (g * g)
    m_hat = m / (1.0 - ADAM_B1 ** ADAM_STEP)
    v_hat = v / (1.0 - ADAM_B2 ** ADAM_STEP)
    delta = -ADAM_LR * (m_hat / (jnp.sqrt(v_hat) + ADAM_EPS) + ADAM_WD * w)
    return delta, m, v


def _rotary_tables_t():
    half = RET_DK // 2
    inv_freq = np.float32(10000.0) ** (-np.arange(half, dtype=np.float32) / np.float32(half))
    ang = inv_freq[:, None] * np.arange(T, dtype=np.float32)[None, :]
    cos, sin = np.cos(ang).astype(np.float32), np.sin(ang).astype(np.float32)
    cos64 = np.concatenate([cos, cos], axis=0)
    sin64 = np.concatenate([-sin, sin], axis=0)
    return np.tile(cos64, (RET_H, 1)), np.tile(sin64, (RET_H, 1))


def _retention_tables_t():
    gamma = (1.0 - np.exp2(-5.0 - np.arange(RET_H, dtype=np.float32))).astype(np.float32)
    log_g = np.log(gamma).astype(np.float32)
    i = np.arange(CH, dtype=np.float32)
    diff = i[None, :] - i[:, None]
    decay = np.where(diff >= 0, np.exp(log_g[:, None, None] * np.maximum(diff, 0.0)), 0.0).astype(np.float32)
    decay_all = np.concatenate(list(decay), axis=1)
    zeta = np.exp(log_g[:, None] * (CH - 1.0 - i)).astype(np.float32)
    zeta_tab = np.repeat(zeta.T, RET_DK, axis=1)
    xi = np.exp(log_g[:, None] * (i + 1.0)).astype(np.float32)
    xi_tab = np.repeat(xi, RET_DK, axis=0)
    chunk_decay = np.exp(log_g * np.float32(CH)).astype(np.float32)
    row_head = np.arange(RET_V)[:, None] // RET_DV
    col_head = np.arange(RET_QK)[None, :] // RET_DK
    state_mask = (row_head == col_head).astype(np.float32)
    state_decay = (state_mask * chunk_decay[row_head]).astype(np.float32)
    q_mask = (np.arange(RET_QK)[:, None] // RET_DK == np.arange(RET_H * CH)[None, :] // CH).astype(np.float32)
    return (np.ascontiguousarray(decay_all), np.ascontiguousarray(zeta_tab), np.ascontiguousarray(xi_tab),
            state_mask, state_decay, q_mask)


def _bucket_table_t():
    qi = np.arange(CH)[:, None]
    kj = np.arange(2 * CH)[None, :]
    dist = qi + CH - kj
    n = np.maximum(dist, 0)
    max_exact = N_BUCKETS // 2
    nf = np.maximum(n, 1).astype(np.float32)
    large = max_exact + (np.log(nf / np.float32(max_exact)) / np.float32(np.log(CH / max_exact))
                         * np.float32(N_BUCKETS - max_exact)).astype(np.int32)
    large = np.minimum(large, N_BUCKETS - 1)
    bucket = np.where(n < max_exact, n, large)
    return np.ascontiguousarray(np.where((dist >= 0) & (dist < CH), bucket, -1).astype(np.int32).T)


def _mesh_pos():
    return lax.axis_index("x"), lax.axis_index("y"), lax.axis_index("c")


def _blk(px, py, pc):
    return 4 * px + 2 * py + pc


def _in_proj_gather(x, norm_w, w_in_t, w_out_blk):
    tchunk = 512

    def body(x_ref, nw_ref, win_ref, wout_ref, proj_ref, wt_ref, wo_ref, rstd_ref, h_ref, stage, send_sems, recv_sems,
             out_sems):
        x, y, c = _mesh_pos()
        me = _blk(x, y, c)
        sibling = (x, y, 1 - c)
        chips = [(1 - x, y), (x, 1 - y), (1 - x, 1 - y)]

        def copy(k, b, to, ref=wt_ref):
            a = 0 if ref is wt_ref else 1
            return pltpu.make_async_remote_copy(src_ref=ref.at[b], dst_ref=ref.at[b], send_sem=send_sems.at[a, k],
                                                recv_sem=recv_sems.at[a, k], device_id=to, device_id_type=MESH)

        wt_ref[me] = _bf(win_ref[...])
        wo_ref[me] = _bf(wout_ref[...])
        first = [copy(0, me, sibling)] + [copy(1 + j, me, (*chip, c)) for j, chip in enumerate(chips[:2])]
        first += [copy(0, me, sibling, wo_ref)] + [copy(1 + j, me, (*chip, c), wo_ref) for j, chip in enumerate(chips[:2])]
        for cp in first:
            cp.start()

        nw = nw_ref[...]
        for i in range(T // TM):
            rows = slice(i * TM, (i + 1) * TM)
            xv = x_ref[rows, :]
            r = lax.rsqrt(jnp.mean(xv * xv, axis=-1, keepdims=True) + NORM_EPS)
            h_ref[rows, :] = _bf(xv * r * nw)
            rstd_ref[rows, :] = r

        writes = []

        def project(b):
            k = len(writes)
            if k >= 2:
                writes[k - 2].wait()
            w = wt_ref[b]
            for t in range(T // tchunk):
                cols = slice(t * tchunk, (t + 1) * tchunk)
                stage[k % 2, :, cols] = _dot_nt(w, h_ref[cols, :])
            cp = pltpu.make_async_copy(stage.at[k % 2], proj_ref.at[b], out_sems.at[k % 2])
            cp.start()
            writes.append(cp)

        project(me)
        copy(0, _blk(x, y, 1 - c), (x, y, c)).wait_recv()
        project(_blk(x, y, 1 - c))
        copy(1, _blk(*chips[0], c), (x, y, c)).wait_recv()
        copy(2, _blk(*chips[1], c), (x, y, c)).wait_recv()
        relay_from, relay_to = _blk(x ^ (1 - c), y ^ c, c), (x ^ c, y ^ (1 - c), c)
        relay = copy(3, relay_from, relay_to)
        relay.start()
        passed = [relay]
        for j, chip in enumerate(chips):
            if j == 2:
                copy(3, _blk(*chip, c), (x, y, c)).wait_recv()
            fwd = copy(4 + j, _blk(*chip, c), sibling)
            fwd.start()
            passed.append(fwd)
            project(_blk(*chip, c))
            if j == 0:
                copy(1, _blk(*chips[0], c), (x, y, c), wo_ref).wait_recv()
                copy(2, _blk(*chips[1], c), (x, y, c), wo_ref).wait_recv()
                passed += [copy(3, relay_from, relay_to, wo_ref), copy(4, _blk(*chips[0], c), sibling, wo_ref),
                           copy(5, _blk(*chips[1], c), sibling, wo_ref)]
                for cp in passed[-3:]:
                    cp.start()
            copy(4 + j, _blk(*chip, 1 - c), (x, y, c)).wait_recv()
            project(_blk(*chip, 1 - c))
        copy(3, _blk(*chips[2], c), (x, y, c), wo_ref).wait_recv()
        passed.append(copy(6, _blk(*chips[2], c), sibling, wo_ref))
        passed[-1].start()
        copy(0, _blk(x, y, 1 - c), (x, y, c), wo_ref).wait_recv()
        for j, chip in enumerate(chips):
            copy(4 + j, _blk(*chip, 1 - c), (x, y, c), wo_ref).wait_recv()
        writes[-2].wait()
        writes[-1].wait()
        for cp in first + passed:
            cp.wait_send()

    vm = pl.BlockSpec(memory_space=pltpu.VMEM)
    proj, wt, wo, rstd, h = pl.pallas_call(
        body, name="in_proj_gather",
        out_shape=(jax.ShapeDtypeStruct((N_DEV, WIN_BLK, T), F32), jax.ShapeDtypeStruct((N_DEV, WIN_BLK, D), BF16),
                   jax.ShapeDtypeStruct((N_DEV, WOUT_BLK, D), BF16),
                   jax.ShapeDtypeStruct((T, 1), F32), jax.ShapeDtypeStruct((T, D), BF16)),
        in_specs=[vm, vm, vm, vm], out_specs=(pl.BlockSpec(memory_space=pl.ANY), vm, vm, vm, vm),
        scratch_shapes=[pltpu.VMEM((2, WIN_BLK, T), F32),
                        pltpu.SemaphoreType.DMA((2, 7)), pltpu.SemaphoreType.DMA((2, 7)), pltpu.SemaphoreType.DMA((2,))],
        compiler_params=_cparams(vmem=48 * 1024 * 1024),
    )(x, norm_w, w_in_t, w_out_blk)
    return proj.reshape(D_IN, T), wt.reshape(D_IN, D), wo.reshape(D, D), rstd, h


def _proj_piece(i, block_of):
    rows = PIECES[i]
    assert OFFS[i] % rows == 0
    return pl.BlockSpec((rows, CH), lambda n: (OFFS[i] // rows, block_of(n)))


def _proj_gate_halves(block_of):
    return [pl.BlockSpec((256, CH), lambda n, j=j: (OFFS[7] // 256 + j, block_of(n))) for j in range(2)]


def _swap_halves_t(t):
    half = RET_DK // 2
    parts = []
    for h in range(RET_H):
        parts += [t[h * RET_DK + half:(h + 1) * RET_DK], t[h * RET_DK:h * RET_DK + half]]
    return jnp.concatenate(parts, axis=0)


def _rotate(t, cos, sin):
    return t * cos + _swap_halves_t(t) * sin


def _group_norm_t(o):
    mu = jnp.mean(o, axis=0, keepdims=True)
    var = jnp.mean((o - mu) * (o - mu), axis=0, keepdims=True)
    rstd = lax.rsqrt(var + GN_EPS)
    return (o - mu) * rstd, rstd


def _retention_scores_t(q_t, k_rm_b, q_mask, decay_all):
    q_heads = _bf(jnp.concatenate([q_t] * RET_H, axis=1) * q_mask)
    return _dot(k_rm_b, q_heads) * decay_all


def _retention_fwd(proj, ret_norm_w, tables, gated):
    cos_t, sin_t, decay_all, zeta_tab, xi_tab, state_mask, state_decay, q_mask = tables

    def body(rq_ref, rk_ref, rv_ref, rg_ref, w_ref, cos_ref, sin_ref, decay_ref, zeta_ref, xi_ref, smask_ref,
             sdecay_ref, qmask_ref, ro_ref, st_ref, state):
        n = pl.program_id(0)

        @pl.when(n == 0)
        def _():
            state[...] = jnp.zeros_like(state)

        cos, sin = cos_ref[...], sin_ref[...]
        q_t = _rotate(rq_ref[...], cos, sin)
        k_t = _rotate(rk_ref[...], cos, sin) * (RET_DK ** -0.5)
        k_rm = k_t.T
        v_b = _bf(rv_ref[...])
        m_b = _bf(state[...])
        st_ref[0] = m_b
        scores_b = _bf(_retention_scores_t(q_t, _bf(k_rm), qmask_ref[...], decay_ref[...]))
        cross = _dot(m_b, _bf(q_t * xi_ref[...]))
        state[...] = state[...] * sdecay_ref[...] + _dot(v_b, _bf(k_rm * zeta_ref[...])) * smask_ref[...]
        heads = range(RET_H)
        rows = [slice(h * RET_DV, (h + 1) * RET_DV) for h in heads]
        o = [_dot(v_b[rows[h]], scores_b[:, h * CH:(h + 1) * CH]) + cross[rows[h]] for h in heads]
        mixed = []
        for h in heads:
            rn, _ = _group_norm_t(o[h])
            g = rg_ref[rows[h], :]
            ro_ref[rows[h], :] = o[h]
            mixed.append(rn * _row_to_cols(w_ref[:, rows[h]]) * (g * _sigmoid(g)))
        gated["r"] = jnp.concatenate(mixed, axis=0)

    col = lambda w: pl.BlockSpec((w, CH), lambda n: (0, n))
    const = lambda shape: pl.BlockSpec(shape, lambda n: (0,) * len(shape))
    cur = lambda n: n
    return _Part(
        body,
        inputs=[proj, proj, proj, proj, ret_norm_w, cos_t, sin_t, decay_all, zeta_tab, xi_tab, state_mask, state_decay,
                q_mask],
        in_specs=[_proj_piece(i, cur) for i in range(4)] + [
            const((1, RET_V)), col(RET_QK), col(RET_QK),
            const(decay_all.shape), const(zeta_tab.shape), const(xi_tab.shape), const(state_mask.shape),
            const(state_decay.shape), const(q_mask.shape)],
        out_specs=[col(RET_V), pl.BlockSpec((1, RET_V, RET_QK), lambda n: (n, 0, 0))],
        out_shape=[jax.ShapeDtypeStruct((RET_V, T), F32), jax.ShapeDtypeStruct((N_CH, RET_V, RET_QK), BF16)],
        scratch_shapes=[pltpu.VMEM((RET_V, RET_QK), F32)])


def _retention_bwd(proj, ro, states, dmix, ret_norm_w, tables):
    cos_t, sin_t, decay_all, zeta_tab, xi_tab, state_mask, state_decay, q_mask = tables

    def body(*refs):
        @pl.when(pl.program_id(0) < N_CH)
        def _():
            chunk(*refs)

    def chunk(rq_ref, rk_ref, rv_ref, rg_ref, ro_ref, st_ref, dm_ref, w_ref, cos_ref, sin_ref, decay_ref, zeta_ref,
              xi_ref, smask_ref, sdecay_ref, qmask_ref, d_ref, rw_o, gstate, dw_ref):
        i = pl.program_id(0)

        @pl.when(i == 0)
        def _():
            gstate[...] = jnp.zeros_like(gstate)
            dw_ref[...] = jnp.zeros_like(dw_ref)

        cos, sin = cos_ref[...], sin_ref[...]
        q_t = _rotate(rq_ref[...], cos, sin)
        k_t = _rotate(rk_ref[...], cos, sin) * (RET_DK ** -0.5)
        q_b, k_b = _bf(q_t), _bf(k_t)
        k_rm = k_t.T
        kz_b = _bf(k_rm * zeta_ref[...])
        qx_b = _bf(q_t * xi_ref[...])
        v_t = rv_ref[...]
        v_b = _bf(v_t)
        v_rm_b = _bf(v_t.T)
        decay = decay_ref[...]
        scores_b = _bf(_retention_scores_t(q_t, _bf(k_rm), qmask_ref[...], decay))
        heads = range(RET_H)
        rows = [slice(h * RET_DV, (h + 1) * RET_DV) for h in heads]
        qk_rows = [slice(h * RET_DK, (h + 1) * RET_DK) for h in heads]
        lanes = [slice(h * CH, (h + 1) * CH) for h in heads]
        do = []
        for h in heads:
            g, w, dm = rg_ref[rows[h], :], _row_to_cols(w_ref[:, rows[h]]), dm_ref[rows[h], :]
            rn, rstd = _group_norm_t(ro_ref[rows[h], :])
            sig = _sigmoid(g)
            silu = g * sig
            d_ref[OFFS[3] + h * RET_DV:OFFS[3] + (h + 1) * RET_DV, :] = _bf(dm * rn * w * (sig * (1.0 + g * (1.0 - sig))))
            dw_ref[rows[h], :] += dm * silu * rn
            drn = dm * silu * w
            do.append(rstd * (drn - jnp.mean(drn, axis=0, keepdims=True)
                              - rn * jnp.mean(drn * rn, axis=0, keepdims=True)))
        do_b = _bf(jnp.concatenate(do, axis=0))
        m_b = st_ref[0]
        g_all = gstate[...]
        g_b = _bf(g_all)
        dscores_b = [_bf(_dot(v_rm_b[:, rows[h]], do_b[rows[h]]) * decay[:, lanes[h]]) for h in heads]
        dq_cross = lax.dot_general(m_b, do_b, (((0,), (0,)), ((), ())), preferred_element_type=F32)
        dkz = _dot(v_rm_b, g_b)
        dv_cross = _dot_nt(g_b, kz_b)
        gstate[...] = g_all * sdecay_ref[...] + _dot_nt(do_b, qx_b) * smask_ref[...]
        dq = jnp.concatenate([_dot(k_b[qk_rows[h]], dscores_b[h]) for h in heads], axis=0) + dq_cross * xi_ref[...]
        dk = (jnp.concatenate([_dot_nt(q_b[qk_rows[h]], dscores_b[h]) for h in heads], axis=0)
              + (dkz * zeta_ref[...]).T) * (RET_DK ** -0.5)
        for h in heads:
            d_ref[OFFS[2] + h * RET_DV:OFFS[2] + (h + 1) * RET_DV, :] = _bf(
                _dot_nt(do_b[rows[h]], scores_b[:, lanes[h]]) + dv_cross[rows[h]])
        d_ref[OFFS[0]:OFFS[1], :] = _bf(dq * cos + _swap_halves_t(dq * sin))
        d_ref[OFFS[1]:OFFS[2], :] = _bf(dk * cos + _swap_halves_t(dk * sin))

        @pl.when(i == N_CH - 1)
        def _():
            rw_o[...] = jnp.sum(dw_ref[...].T, axis=0, keepdims=True)

    chunk_of = lambda i: N_CH - 1 - jnp.minimum(i, N_CH - 1)
    col = lambda w: pl.BlockSpec((w, CH), lambda i: (0, chunk_of(i)))
    const = lambda shape: pl.BlockSpec(shape, lambda i: (0,) * len(shape))
    return _Part(
        body,
        inputs=[proj, proj, proj, proj, ro, states, dmix, ret_norm_w, cos_t, sin_t, decay_all, zeta_tab, xi_tab,
                state_mask, state_decay, q_mask],
        in_specs=[_proj_piece(j, chunk_of) for j in range(4)] + [
                  col(RET_V),
                  pl.BlockSpec((1, RET_V, RET_QK), lambda i: (chunk_of(i), 0, 0)), col(RET_V),
                  const((1, RET_V)), col(RET_QK), col(RET_QK),
                  const(decay_all.shape), const(zeta_tab.shape), const(xi_tab.shape), const(state_mask.shape),
                  const(state_decay.shape), const(q_mask.shape)],
        out_specs=[col(RET_W), const((1, RET_V))],
        out_shape=[jax.ShapeDtypeStruct((RET_W, T), BF16), jax.ShapeDtypeStruct((1, RET_V), F32)],
        scratch_shapes=[pltpu.VMEM((RET_V, RET_QK), F32), pltpu.VMEM((RET_V, CH), F32)])


HQ_LANES = SWA_H * CH


def _head_lanes(hq):
    return slice(hq * CH, (hq + 1) * CH)


def _build_bias_t(bucket_ref, relb_ref, sinks_ref, bias_ref, sink_row):
    bt = bucket_ref[...]
    first = lax.broadcasted_iota(jnp.int32, bt.shape, 0) < CH
    for hq in range(SWA_H):
        b = jnp.full(bt.shape, NEG_INF, F32)
        for bk in range(N_BUCKETS):
            b = jnp.where(bt == bk, relb_ref[hq * N_BUCKETS + bk], b)
        bias_ref[0, :, _head_lanes(hq)] = b
        bias_ref[1, :, _head_lanes(hq)] = jnp.where(first, NEG_INF, b)
        sink_row[:, _head_lanes(hq)] = jnp.full((1, CH), sinks_ref[hq], F32)


def _rms_t(t, w_col):
    r = lax.rsqrt(jnp.mean(t * t, axis=0, keepdims=True) + NORM_EPS)
    return t * r * w_col, r


def _rms_t_bwd(dn, t, r, w_col):
    u = dn * w_col
    return r * u - t * (r * r * r) * jnp.mean(u * t, axis=0, keepdims=True), dn * t * r


def _norm_kv(k_t, kw):
    return jnp.concatenate([_rms_t(k_t[g * SWA_D:(g + 1) * SWA_D], kw)[0] for g in range(SWA_KV)], axis=0)


def _kv_slot(a, kv):
    z = jnp.zeros_like(a)
    return jnp.concatenate([a, z] if kv == 0 else [z, a], axis=0)


def _softmax_t(s, sink):
    m = jnp.maximum(jnp.max(s, axis=0, keepdims=True), sink)
    p = jnp.exp(s - m)
    e_sink = jnp.exp(sink - m)
    inv = 1.0 / (jnp.sum(p, axis=0, keepdims=True) + e_sink)
    return p * inv, e_sink * inv


def _gate_rows(sg_halves, hq):
    per_half = SWA_H // 2
    return sg_halves[hq // per_half][(hq % per_half) * SWA_D:(hq % per_half + 1) * SWA_D, :]


def _swa_fwd(proj, q_norm_w, k_norm_w, sinks, rel_bias_flat, bucket_t, gated):
    def body(sq_ref, skp_ref, skc_ref, svp_ref, svc_ref, sg_lo, sg_hi, qw_ref, kw_ref, sinks_ref, relb_ref, bucket_ref,
             so_ref, bias_ref, sink_row):
        n = pl.program_id(0)

        @pl.when(n == 0)
        def _():
            _build_bias_t(bucket_ref, relb_ref, sinks_ref, bias_ref, sink_row)

        var = (n == 0).astype(jnp.int32)
        qw, kw = _row_to_cols(qw_ref[...]), _row_to_cols(kw_ref[...])
        kn_band = jnp.concatenate([_norm_kv(skp_ref[...], kw), _norm_kv(skc_ref[...], kw)], axis=1)
        kb_rm = _bf(kn_band.T)
        vband = _bf(jnp.concatenate([svp_ref[...], svc_ref[...]], axis=1))
        q_all = jnp.concatenate(
            [_bf(_kv_slot(_rms_t(sq_ref[hq * SWA_D:(hq + 1) * SWA_D, :], qw)[0] * (SWA_D ** -0.5), hq // SWA_G))
             for hq in range(SWA_H)], axis=1)
        probs, _ = _softmax_t(_dot(kb_rm, q_all) + bias_ref[var], sink_row[...])
        probs_b = _bf(probs)
        mixed = []
        for kv in range(SWA_KV):
            o = _dot(vband[kv * SWA_D:(kv + 1) * SWA_D], probs_b[:, kv * SWA_G * CH:(kv + 1) * SWA_G * CH])
            for j in range(SWA_G):
                rows = slice((kv * SWA_G + j) * SWA_D, (kv * SWA_G + j + 1) * SWA_D)
                oh = o[:, j * CH:(j + 1) * CH]
                g = _gate_rows((sg_lo, sg_hi), kv * SWA_G + j)
                so_ref[rows, :] = oh
                mixed.append(oh * (g * _sigmoid(g)))
        gated["s"] = jnp.concatenate(mixed, axis=0)

    col = lambda w: pl.BlockSpec((w, CH), lambda n: (0, n))
    const = lambda shape: pl.BlockSpec(shape, lambda n: (0,) * len(shape))
    smem = pl.BlockSpec(memory_space=pltpu.SMEM)
    cur = lambda n: n
    prev = lambda n: jnp.maximum(n - 1, 0)
    return _Part(
        body,
        inputs=[proj, proj, proj, proj, proj, proj, proj, q_norm_w, k_norm_w, sinks, rel_bias_flat, bucket_t],
        in_specs=[_proj_piece(4, cur), _proj_piece(5, prev), _proj_piece(5, cur), _proj_piece(6, prev),
                  _proj_piece(6, cur)] + _proj_gate_halves(cur) + [
                  const((1, SWA_D)), const((1, SWA_D)), smem, smem, const((2 * CH, CH))],
        out_specs=[col(512)],
        out_shape=[jax.ShapeDtypeStruct((512, T), F32)],
        scratch_shapes=[pltpu.VMEM((2, 2 * CH, HQ_LANES), F32), pltpu.VMEM((1, HQ_LANES), F32)])


def _swa_bwd(proj, so, dmix, q_norm_w, k_norm_w, sinks, rel_bias_flat, bucket_t):
    def body(sq_ref, skp_ref, skc_ref, svp_ref, svc_ref, sg_lo, sg_hi, so_ref, dm_ref, qw_ref, kw_ref, sinks_ref,
             relb_ref, bucket_ref, d_ref, qw_o, kw_o, sink_o, relb_o,
             bias_ref, sink_row, band_dk, band_dv, carry_dk, carry_dv, hold_dq, hold_dg,
             dbias_ref, dsink_ref, dqw_ref, dkw_ref):
        n = pl.program_id(0)
        slot = n % 2

        @pl.when(n == 0)
        def _():
            _build_bias_t(bucket_ref, relb_ref, sinks_ref, bias_ref, sink_row)
            for ref in (dbias_ref, dsink_ref, dqw_ref, dkw_ref, carry_dk, carry_dv):
                ref[...] = jnp.zeros_like(ref)

        qw, kw = _row_to_cols(qw_ref[...]), _row_to_cols(kw_ref[...])

        @pl.when(n < N_CH)
        def _():
            var = (n == 0).astype(jnp.int32)
            kn_band = jnp.concatenate([_norm_kv(skp_ref[...], kw), _norm_kv(skc_ref[...], kw)], axis=1)
            kb_rm = _bf(kn_band.T)
            kn_band_s = _bf(kn_band * (SWA_D ** -0.5))
            vband_f = jnp.concatenate([svp_ref[...], svc_ref[...]], axis=1)
            vb_rm = _bf(vband_f.T)
            q_raw, q_rstd, qs_b, do_b = [], [], [], []
            for hq in range(SWA_H):
                rows = slice(hq * SWA_D, (hq + 1) * SWA_D)
                q_t = sq_ref[rows, :]
                qn, rq = _rms_t(q_t, qw)
                g = _gate_rows((sg_lo, sg_hi), hq)
                sig = _sigmoid(g)
                dm = dm_ref[rows, :]
                hold_dg[slot, rows, :] = _bf(dm * so_ref[rows, :] * (sig * (1.0 + g * (1.0 - sig))))
                q_raw.append(q_t)
                q_rstd.append(rq)
                qs_b.append(_bf(qn * (SWA_D ** -0.5)))
                do_b.append(_bf(dm * (g * sig)))
            q_all = jnp.concatenate([_kv_slot(qs_b[hq], hq // SWA_G) for hq in range(SWA_H)], axis=1)
            do_all = jnp.concatenate([_kv_slot(do_b[hq], hq // SWA_G) for hq in range(SWA_H)], axis=1)
            probs, p_sink = _softmax_t(_dot(kb_rm, q_all) + bias_ref[var], sink_row[...])
            dprobs = _dot(vb_rm, do_all)
            t = jnp.sum(probs * dprobs, axis=0, keepdims=True)
            dlog = probs * (dprobs - t)
            dsink_ref[...] += -(p_sink * t)
            dbias_ref[...] += dlog
            dlog_b, probs_b = _bf(dlog), _bf(probs)
            dkn, dvv = [], []
            for kv in range(SWA_KV):
                heads = range(kv * SWA_G, (kv + 1) * SWA_G)
                lanes = slice(kv * SWA_G * CH, (kv + 1) * SWA_G * CH)
                dvv.append(_dot_nt(jnp.concatenate([do_b[hq] for hq in heads], axis=1), probs_b[:, lanes]))
                dkn.append(_dot_nt(jnp.concatenate([qs_b[hq] for hq in heads], axis=1), dlog_b[:, lanes]))
                dqn = _dot(kn_band_s[kv * SWA_D:(kv + 1) * SWA_D], dlog_b[:, lanes])
                for j, hq in enumerate(heads):
                    dq_t, dqw_terms = _rms_t_bwd(dqn[:, j * CH:(j + 1) * CH], q_raw[hq], q_rstd[hq], qw)
                    hold_dq[slot, hq * SWA_D:(hq + 1) * SWA_D, :] = _bf(dq_t)
                    dqw_ref[...] += dqw_terms
            band_dk[...] = jnp.concatenate(dkn, axis=0)
            band_dv[...] = jnp.concatenate(dvv, axis=0)

        @pl.when(n == N_CH)
        def _():
            band_dk[...] = jnp.zeros_like(band_dk)
            band_dv[...] = jnp.zeros_like(band_dv)

        @pl.when(n >= 1)
        def _():
            dkn_prev = carry_dk[...] + band_dk[:, 0:CH]
            k_t = skp_ref[...]
            for kv in range(SWA_KV):
                rows = slice(kv * SWA_D, (kv + 1) * SWA_D)
                _, rk = _rms_t(k_t[rows], kw)
                dk_t, dkw_terms = _rms_t_bwd(dkn_prev[rows], k_t[rows], rk, kw)
                d_ref[SWA_OFFS[1] + kv * SWA_D:SWA_OFFS[1] + (kv + 1) * SWA_D, :] = _bf(dk_t)
                dkw_ref[...] += dkw_terms
            d_ref[SWA_OFFS[2]:SWA_OFFS[3], :] = _bf(carry_dv[...] + band_dv[:, 0:CH])
            d_ref[SWA_OFFS[0]:SWA_OFFS[1], :] = hold_dq[1 - slot]
            d_ref[SWA_OFFS[3]:SWA_OFFS[4], :] = hold_dg[1 - slot]

        carry_dk[...] = band_dk[:, CH:2 * CH]
        carry_dv[...] = band_dv[:, CH:2 * CH]

        @pl.when(n == N_CH)
        def _():
            _finish_small_grads(dbias_ref, bucket_ref, dsink_ref, dqw_ref, dkw_ref, qw_o, kw_o, sink_o, relb_o)

    cur_block = lambda n: jnp.minimum(n, N_CH - 1)
    prev_block = lambda n: jnp.maximum(n - 1, 0)
    col = lambda w: pl.BlockSpec((w, CH), lambda n: (0, cur_block(n)))
    prev = lambda w: pl.BlockSpec((w, CH), lambda n: (0, prev_block(n)))
    const = lambda shape: pl.BlockSpec(shape, lambda n: (0,) * len(shape))
    smem = pl.BlockSpec(memory_space=pltpu.SMEM)
    return _Part(
        body,
        inputs=[proj, proj, proj, proj, proj, proj, proj, so, dmix, q_norm_w, k_norm_w, sinks, rel_bias_flat, bucket_t],
        in_specs=[_proj_piece(4, cur_block), _proj_piece(5, prev_block), _proj_piece(5, cur_block),
                  _proj_piece(6, prev_block), _proj_piece(6, cur_block)] + _proj_gate_halves(cur_block) + [
                  col(512), col(512), const((1, SWA_D)), const((1, SWA_D)), smem, smem, const((2 * CH, CH))],
        out_specs=[prev(SWA_W), const((1, CH)), const((1, CH)), const((1, CH)), const((SWA_H, N_BUCKETS))],
        out_shape=[jax.ShapeDtypeStruct((SWA_W, T), BF16)] + [jax.ShapeDtypeStruct((1, CH), F32)] * 3
                  + [jax.ShapeDtypeStruct((SWA_H, N_BUCKETS), F32)],
        scratch_shapes=[pltpu.VMEM((2, 2 * CH, HQ_LANES), F32), pltpu.VMEM((1, HQ_LANES), F32),
                        pltpu.VMEM((128, 2 * CH), F32), pltpu.VMEM((128, 2 * CH), F32),
                        pltpu.VMEM((128, CH), F32), pltpu.VMEM((128, CH), F32),
                        pltpu.VMEM((2, 512, CH), BF16), pltpu.VMEM((2, 512, CH), BF16),
                        pltpu.VMEM((2 * CH, HQ_LANES), F32), pltpu.VMEM((1, HQ_LANES), F32),
                        pltpu.VMEM((SWA_D, CH), F32), pltpu.VMEM((SWA_D, CH), F32)])


def _finish_small_grads(dbias_ref, bucket_ref, dsink_ref, dqw_ref, dkw_ref, qw_o, kw_o, sink_o, relb_o):
    bt = bucket_ref[...]
    row = lax.broadcasted_iota(jnp.int32, (SWA_H, N_BUCKETS), 0)
    col = lax.broadcasted_iota(jnp.int32, (SWA_H, N_BUCKETS), 1)
    lane = lax.broadcasted_iota(jnp.int32, (1, CH), 1)
    acc = jnp.zeros((SWA_H, N_BUCKETS), F32)
    sink = jnp.zeros((1, CH), F32)
    for hq in range(SWA_H):
        d = dbias_ref[:, _head_lanes(hq)]
        for bk in range(N_BUCKETS):
            s = jnp.sum(jnp.sum(jnp.where(bt == bk, d, 0.0), axis=0, keepdims=True), axis=1, keepdims=True)
            acc = acc + jnp.where((row == hq) & (col == bk), s, 0.0)
        sink = sink + jnp.where(lane == hq, jnp.sum(dsink_ref[:, _head_lanes(hq)], axis=1, keepdims=True), 0.0)
    relb_o[...] = acc
    sink_o[...] = sink
    for src, dst in ((dqw_ref, qw_o), (dkw_ref, kw_o)):
        padded = jnp.concatenate([src[...], jnp.zeros((CH - SWA_D, CH), F32)], axis=0)
        dst[...] = jnp.sum(padded.T, axis=0, keepdims=True)


def _out_proj(mixed_chunk, wo, x, target):
    def body(w_ref, x_ref, t_ref, loss_ref, dy_ref, dmr_ref, dms_ref, gw_ref, held):
        n = pl.program_id(0)

        @pl.when(n == 0)
        def _():
            loss_ref[...] = jnp.zeros_like(loss_ref)
            gw_ref[...] = jnp.zeros_like(gw_ref)

        @pl.when(n % 2 == 0)
        def _():
            held[...] = mixed_chunk()

        @pl.when(n % 2 == 1)
        def _():
            mixed = jnp.concatenate([held[...], mixed_chunk()], axis=1)
            w = w_ref[...]
            err = x_ref[...] + _dot(_bf(mixed.T), w) - t_ref[...]
            loss_ref[...] += jnp.sum(jnp.sum(err * err, axis=1, keepdims=True), axis=0, keepdims=True)
            dy = err * (1.0 / D)
            dy_ref[...] = dy
            dy_b = _bf(dy)
            dmix = _dot_nt(w, dy_b)
            dmr_ref[...] = dmix[0:RET_V]
            dms_ref[...] = dmix[RET_V:D]
            gw_ref[...] += _dot(_bf(mixed), dy_b)

    assert TM == 2 * CH
    row = lambda w: pl.BlockSpec((TM, w), lambda n: (n // 2, 0))
    col = lambda w: pl.BlockSpec((w, TM), lambda n: (0, n // 2))
    const = lambda shape, **kw: pl.BlockSpec(shape, lambda n: (0,) * len(shape), **kw)
    return _Part(
        body, inputs=[wo, x, target],
        in_specs=[const((D, D), pipeline_mode=pl.Buffered(1)), row(D), row(D)],
        out_specs=[const((1, 1)), row(D), col(RET_V), col(D - RET_V), const((D, D))],
        out_shape=[jax.ShapeDtypeStruct((1, 1), F32), jax.ShapeDtypeStruct((T, D), F32),
                   jax.ShapeDtypeStruct((RET_V, T), F32), jax.ShapeDtypeStruct((D - RET_V, T), F32),
                   jax.ShapeDtypeStruct((D, D), F32)],
        scratch_shapes=[pltpu.VMEM((D, CH), F32)])


LOSS_ROW = 5
ALL_CHIPS = ((0, 0), (0, 1), (1, 0), (1, 1))


def _w_out_reduce_scatter(gwo):
    def body(gwo_ref, gout_o, own, rcv, snd, got, tot, local_sems, a_send, a_recv, b_send, b_recv):
        k = pl.program_id(0)
        x_, y_, c_ = _mesh_pos()
        sibling = (x_, y_, 1 - c_)
        rel_chips = [(1 - x_, y_), (x_, 1 - y_), (1 - x_, 1 - y_)]

        def to_sibling(j):
            px, py = ALL_CHIPS[j]
            return pltpu.make_async_remote_copy(src_ref=gwo_ref.at[_blk(px, py, 1 - c_)], dst_ref=rcv.at[j],
                                                send_sem=a_send.at[j], recv_sem=a_recv.at[j], device_id=sibling,
                                                device_id_type=MESH)

        def local(j):
            px, py = ALL_CHIPS[j]
            return pltpu.make_async_copy(gwo_ref.at[_blk(px, py, c_)], own.at[j], local_sems.at[j])

        def to_chip(q):
            return pltpu.make_async_remote_copy(src_ref=snd.at[q], dst_ref=got.at[q], send_sem=b_send.at[q],
                                                recv_sem=b_recv.at[q], device_id=(*rel_chips[q], c_), device_id_type=MESH)

        @pl.when(k == 0)
        def _():
            for j in range(len(ALL_CHIPS)):
                to_sibling(j).start()
                local(j).start()

        @pl.when(k == 2)
        def _():
            for j in range(len(ALL_CHIPS)):
                local(j).wait()
                to_sibling(j).wait_recv()
            for q in range(3):
                j = 2 * rel_chips[q][0] + rel_chips[q][1]
                snd[q] = _bf(own[j] + rcv[j])
                to_chip(q).start()
            jm = 2 * x_ + y_
            tot[...] = own[jm] + rcv[jm]

        @pl.when(k == N_CH)
        def _():
            g = tot[...]
            for q in range(3):
                to_chip(q).wait_recv()
                g = g + got[q].astype(F32)
            gout_o[...] = g
            for j in range(len(ALL_CHIPS)):
                to_sibling(j).wait_send()
            for q in range(3):
                to_chip(q).wait_send()

    dma = pltpu.SemaphoreType.DMA
    return _Part(
        body, inputs=[gwo], in_specs=[pl.BlockSpec(memory_space=pl.ANY)],
        out_specs=[pl.BlockSpec((WOUT_BLK, D), lambda k: (0, 0))], out_shape=[jax.ShapeDtypeStruct((WOUT_BLK, D), F32)],
        scratch_shapes=[pltpu.VMEM((4, WOUT_BLK, D), F32), pltpu.VMEM((4, WOUT_BLK, D), F32),
                        pltpu.VMEM((3, WOUT_BLK, D), BF16), pltpu.VMEM((3, WOUT_BLK, D), BF16),
                        pltpu.VMEM((WOUT_BLK, D), F32), dma((4,)), dma((4,)), dma((4,)), dma((3,)), dma((3,))])


def _in_proj_bwd_rs(d_ret, d_swa, h, wt, x, rstd, norm_w, dy, sse, small_g):
    n_blocks, n_tiles = N_DEV, T // TM
    last = n_blocks + n_tiles - 1

    def body(da_any, db_any, da_ref, db_ref, h_ref, w_ref, x_ref, r_ref, nw_ref, dy_ref, sse_ref,
             rw_ref, qw_ref, kw_ref, sk_ref, rb_ref, gx_ref, gin_o, suma_o, sumb_o,
             stage, d2d_src, mine, rcv, snd, got, tot, tab_a, tab_b, gnw_acc,
             dp_sems, d2d_send, d2d_recv, ici_send, ici_recv, s_send, s_recv):
        k = pl.program_id(0)
        x_, y_, c_ = _mesh_pos()
        me = _blk(x_, y_, c_)
        sibling = (x_, y_, 1 - c_)
        rel_chips = [(1 - x_, y_), (x_, 1 - y_), (1 - x_, 1 - y_), (x_, y_)]

        def block_of_step(s):
            return _blk(*rel_chips[s // 2], 1 - c_ if s % 2 == 0 else c_)

        def fetch(s, wait):
            slot, b = s % 2, block_of_step(s)
            split = RET_W - 4 * WIN_BLK

            def run(src, dst, sem):
                cp = pltpu.make_async_copy(src, dst, sem)
                cp.wait() if wait else cp.start()

            @pl.when(b < 4)
            def _():
                run(da_any.at[pl.ds(pl.multiple_of(b * WIN_BLK, 16), WIN_BLK), :], stage.at[slot], dp_sems.at[slot, 0])

            @pl.when(b == 4)
            def _():
                run(da_any.at[pl.ds(4 * WIN_BLK, split), :], stage.at[slot, pl.ds(0, split), :], dp_sems.at[slot, 0])
                run(db_any.at[pl.ds(0, WIN_BLK - split), :], stage.at[slot, pl.ds(split, WIN_BLK - split), :],
                    dp_sems.at[slot, 1])

            @pl.when(b > 4)
            def _():
                run(db_any.at[pl.ds(pl.multiple_of(b * WIN_BLK - RET_W, 16), WIN_BLK), :], stage.at[slot],
                    dp_sems.at[slot, 0])

        def d2d_copy(r):
            return pltpu.make_async_remote_copy(src_ref=d2d_src, dst_ref=rcv.at[r], send_sem=d2d_send.at[r],
                                                recv_sem=d2d_recv.at[r], device_id=sibling, device_id_type=MESH)

        def ici_copy(r):
            return pltpu.make_async_remote_copy(src_ref=snd.at[r], dst_ref=got.at[r], send_sem=ici_send.at[r],
                                                recv_sem=ici_recv.at[r], device_id=(*rel_chips[r], c_),
                                                device_id_type=MESH)

        def table_copies():
            return [pltpu.make_async_remote_copy(src_ref=tab.at[me], dst_ref=tab.at[me], send_sem=s_send.at[a, p - 1],
                                                 recv_sem=s_recv.at[a, p - 1],
                                                 device_id=(x_ ^ (p >> 2), y_ ^ ((p >> 1) & 1), c_ ^ (p & 1)),
                                                 device_id_type=MESH)
                    for p in range(1, N_DEV) for a, tab in enumerate((tab_a, tab_b))]

        def chip_sum(r):
            d2d_copy(r).wait_recv()
            total = mine[...] + rcv[r]
            if r < 3:
                snd[r] = _bf(total)
                ici_copy(r).start()
            else:
                tot[...] = total

        for s in range(n_blocks):
            @pl.when(k == s)
            def _(s=s):
                r = s // 2
                if s == 0:
                    gnw_acc[...] = jnp.zeros_like(gnw_acc)
                    fetch(0, wait=False)
                fetch(s, wait=True)
                if s + 1 < n_blocks:
                    fetch(s + 1, wait=False)
                gw = _dot(stage[s % 2], h_ref[...])
                if s % 2 == 0:
                    if s >= 2:
                        chip_sum(r - 1)
                        d2d_copy(r - 1).wait_send()
                    d2d_src[...] = gw
                    d2d_copy(r).start()
                else:
                    mine[...] = gw

        @pl.when(k == n_blocks)
        def _():
            chip_sum(3)

        @pl.when(k >= n_blocks)
        def _():
            dp = jnp.concatenate([da_ref[...], db_ref[...]], axis=0)
            xv, r, nw = x_ref[...], r_ref[...], nw_ref[...]
            dh = lax.dot_general(dp, w_ref[...], (((0,), (0,)), ((), ())), preferred_element_type=F32)
            u = dh * nw
            gx_ref[...] = dy_ref[...] + r * u - xv * (r * r * r) * jnp.mean(u * xv, axis=-1, keepdims=True)
            gnw_acc[...] += jnp.sum(dh * (xv * r), axis=0, keepdims=True)

        @pl.when(k == last)
        def _():
            mine_a, mine_b = tab_a.at[me], tab_b.at[me]
            mine_a[...] = jnp.zeros_like(mine_a)
            mine_b[...] = jnp.zeros_like(mine_b)
            for row, ref in enumerate((gnw_acc, rw_ref, qw_ref, kw_ref, sk_ref)):
                mine_a[row:row + 1, 0:ref.shape[1]] = ref[...]
            mine_a[LOSS_ROW:LOSS_ROW + 1, 0:1] = sse_ref[...]
            mine_b[:, 0:N_BUCKETS] = rb_ref[...]
            tables = table_copies()
            for cp in tables:
                cp.start()
            g_in = tot[...]
            for q in range(3):
                ici_copy(q).wait_recv()
                g_in = g_in + got[q].astype(F32)
            gin_o[...] = g_in
            for cp in tables:
                cp.wait_recv()
            sum_a, sum_b = tab_a[0], tab_b[0]
            for b in range(1, N_DEV):
                sum_a = sum_a + tab_a[b]
                sum_b = sum_b + tab_b[b]
            suma_o[...] = sum_a
            sumb_o[...] = sum_b
            d2d_copy(3).wait_send()
            for q in range(3):
                ici_copy(q).wait_send()
            for cp in tables:
                cp.wait_send()

    tile = lambda k: jnp.maximum(k - n_blocks, 0)
    row = lambda w: pl.BlockSpec((TM, w), lambda k: (tile(k), 0))
    col = lambda w: pl.BlockSpec((w, TM), lambda k: (0, tile(k)))
    const = lambda shape, **kw: pl.BlockSpec(shape, lambda k: (0,) * len(shape), **kw)
    once = dict(pipeline_mode=pl.Buffered(1))
    hbm = pl.BlockSpec(memory_space=pl.ANY)
    dma = pltpu.SemaphoreType.DMA
    return pl.pallas_call(
        body, name="in_proj_bwd_rs", grid=(n_blocks + n_tiles,),
        in_specs=[hbm, hbm, col(RET_W), col(SWA_W), const((T, D), **once), const((D_IN, D), **once),
                  row(D), row(1), const((1, D)), row(D), const((1, 1))]
                 + [const(g.shape) for g in small_g],
        out_specs=(row(D), const((WIN_BLK, D)), const((8, D)), const((SWA_H, CH))),
        out_shape=(jax.ShapeDtypeStruct((T, D), F32), jax.ShapeDtypeStruct((WIN_BLK, D), F32),
                   jax.ShapeDtypeStruct((8, D), F32), jax.ShapeDtypeStruct((SWA_H, CH), F32)),
        scratch_shapes=[
            pltpu.VMEM((2, WIN_BLK, T), BF16), pltpu.VMEM((WIN_BLK, D), F32),
            pltpu.VMEM((WIN_BLK, D), F32),
            pltpu.VMEM((4, WIN_BLK, D), F32), pltpu.VMEM((3, WIN_BLK, D), BF16),
            pltpu.VMEM((3, WIN_BLK, D), BF16), pltpu.VMEM((WIN_BLK, D), F32),
            pltpu.VMEM((N_DEV, 8, D), F32), pltpu.VMEM((N_DEV, SWA_H, CH), F32),
            pltpu.VMEM((1, D), F32),
            dma((2, 2)), dma((4,)), dma((4,)), dma((3,)), dma((3,)), dma((2, 7)), dma((2, 7)),
        ],
        compiler_params=_cparams(1),
    )(d_ret, d_swa, d_ret, d_swa, h, wt, x, rstd, norm_w, dy, sse, *small_g)


SMALL_SHAPES = ((1, D), (1, 512), (1, SWA_D), (1, SWA_D), (1, SWA_H), (SWA_H, N_BUCKETS))


def _adamw_all(g_in_t, g_out, sum_a, sum_b, w_in_t, m_in_t, v_in_t, w_out, m_out, v_out, small_w, small_m, small_v):
    n_small = len(SMALL_SHAPES)
    halves = 2

    def body(*refs):
        gin_ref, gout_ref, suma_ref, sumb_ref, w_in_ref, m_in_ref, v_in_ref, w_out_ref, m_out_ref, v_out_ref = refs[:10]
        pos = 10
        sw_refs, sm_refs, sv_refs = (refs[pos + i * n_small:pos + (i + 1) * n_small] for i in range(3))
        pos += 3 * n_small
        g_in_o, d_in_o, nm_in_o, nv_in_o, g_out_o, d_out_o, nm_out_o, nv_out_o = refs[pos:pos + 8]
        pos += 8
        sg_o, sd_o, snm_o, snv_o = (refs[pos + i * n_small:pos + (i + 1) * n_small] for i in range(4))
        loss_o = refs[pos + 4 * n_small]

        g = gout_ref[...]
        d, nm, nv = _adamw(w_out_ref[...], g, m_out_ref[...], v_out_ref[...])
        g_out_o[...], d_out_o[...], nm_out_o[...], nv_out_o[...] = g, d, nm, nv
        g = gin_ref[...]
        d, nm, nv = _adamw(w_in_ref[...], g, m_in_ref[...], v_in_ref[...])
        g_in_o[...], d_in_o[...], nm_in_o[...], nv_in_o[...] = g, d, nm, nv

        @pl.when(pl.program_id(0) == 0)
        def _():
            loss_o[...] = suma_ref[LOSS_ROW:LOSS_ROW + 1, 0:1] * (0.5 / D)
            for r, (rows, lanes) in enumerate(SMALL_SHAPES):
                g = suma_ref[r:r + 1, 0:lanes] if rows == 1 else sumb_ref[:, 0:lanes]
                d, nm, nv = _adamw(sw_refs[r][...], g, sm_refs[r][...], sv_refs[r][...])
                sg_o[r][...], sd_o[r][...], snm_o[r][...], snv_o[r][...] = g, d, nm, nv

    half = lambda rows: pl.BlockSpec((rows // halves, D), lambda i: (i, 0))
    const = lambda shape: pl.BlockSpec(shape, lambda i: (0,) * len(shape))
    win = jax.ShapeDtypeStruct((WIN_BLK, D), F32)
    wout = jax.ShapeDtypeStruct((WOUT_BLK, D), F32)
    smalls = tuple(jax.ShapeDtypeStruct(s, F32) for s in SMALL_SHAPES)
    small_specs = [const(s) for s in SMALL_SHAPES]
    outs = pl.pallas_call(
        body, name="adamw_all", grid=(halves,),
        in_specs=[half(WIN_BLK), half(WOUT_BLK), const((8, D)), const((SWA_H, CH))] + [half(WIN_BLK)] * 3
                 + [half(WOUT_BLK)] * 3 + small_specs * 3,
        out_specs=tuple([half(WIN_BLK)] * 4 + [half(WOUT_BLK)] * 4 + small_specs * 4 + [const((1, 1))]),
        out_shape=(win,) * 4 + (wout,) * 4 + smalls * 4 + (jax.ShapeDtypeStruct((1, 1), F32),),
        compiler_params=_cparams(1, vmem=32 * 1024 * 1024),
    )(g_in_t, g_out, sum_a, sum_b, w_in_t, m_in_t, v_in_t, w_out, m_out, v_out, *small_w, *small_m, *small_v)
    big, rest = outs[:8], outs[8:]
    return big, [rest[i * n_small:(i + 1) * n_small] for i in range(4)], rest[4 * n_small]


def _small_rows(norm_w, ret_norm_w, q_norm_w, k_norm_w, sinks, rel_bias):
    return (norm_w.reshape(1, D), ret_norm_w.reshape(1, 512), q_norm_w.reshape(1, SWA_D), k_norm_w.reshape(1, SWA_D),
            sinks.reshape(1, SWA_H), rel_bias.T)


def _small_leaves(rows):
    return (rows[0].reshape(D), rows[1].reshape(512), rows[2].reshape(SWA_D), rows[3].reshape(SWA_D),
            rows[4].reshape(SWA_H), rows[5].T)


def kernel(x, norm_w, w_in, ret_norm_w, q_norm_w, k_norm_w, sinks, rel_bias, w_out, loss_target, m_norm_w, m_w_in, m_ret_norm_w, m_q_norm_w, m_k_norm_w, m_sinks, m_rel_bias, m_w_out, v_norm_w, v_w_in, v_ret_norm_w, v_q_norm_w, v_k_norm_w, v_sinks, v_rel_bias, v_w_out):
    x2 = x.reshape(T, D)
    target = loss_target.reshape(T, D)
    nw = norm_w.reshape(1, D)
    rnw_row = ret_norm_w.reshape(1, RET_V)
    qnw_row = q_norm_w.reshape(1, SWA_D)
    knw_row = k_norm_w.reshape(1, SWA_D)
    relb = rel_bias.T.reshape(SWA_H * N_BUCKETS)
    ret_tables = _rotary_tables_t() + _retention_tables_t()
    bucket_t = _bucket_table_t()

    proj, wt, wo, rstd, h = _in_proj_gather(x2, nw, w_in.T, w_out)
    gated = {}
    (ro, states), (so,), (sse, dy, dmix_r, dmix_s, gwo) = _fused_call("layer_fwd", N_CH, [
        _retention_fwd(proj, rnw_row, ret_tables, gated),
        _swa_fwd(proj, qnw_row, knw_row, sinks, relb, bucket_t, gated),
        _out_proj(lambda: jnp.concatenate([gated["r"], gated["s"]], axis=0), wo, x2, target)])

    (d_ret, d_rnw), (d_swa, d_qnw, d_knw, d_sinks, d_relb), (g_out,) = _fused_call("attn_bwd", N_CH + 1, [
        _retention_bwd(proj, ro, states, dmix_r, rnw_row, ret_tables),
        _swa_bwd(proj, so, dmix_s, qnw_row, knw_row, sinks, relb, bucket_t),
        _w_out_reduce_scatter(gwo.reshape(N_DEV, WOUT_BLK, D))])
    grad_x, g_in_t, sum_a, sum_b = _in_proj_bwd_rs(d_ret, d_swa, h, wt, x2, rstd, nw, dy, sse,
                                                   (d_rnw, d_qnw, d_knw, d_sinks, d_relb))

    small_w = _small_rows(norm_w, ret_norm_w, q_norm_w, k_norm_w, sinks, rel_bias)
    small_m = _small_rows(m_norm_w, m_ret_norm_w, m_q_norm_w, m_k_norm_w, m_sinks, m_rel_bias)
    small_v = _small_rows(v_norm_w, v_ret_norm_w, v_q_norm_w, v_k_norm_w, v_sinks, v_rel_bias)
    big, small, loss = _adamw_all(g_in_t, g_out, sum_a, sum_b, w_in.T, m_w_in.T, v_w_in.T, w_out, m_w_out, v_w_out,
                                  small_w, small_m, small_v)

    def leaves(i):
        a = _small_leaves(small[i])
        return (a[0], big[i].T, a[1], a[2], a[3], a[4], a[5], big[4 + i])

    return (loss.reshape(()), grad_x.reshape(1, T, D), *leaves(0), *leaves(1), *leaves(2), *leaves(3))
```

```python
from typing import Callable, NamedTuple

import numpy as np
import jax
import jax.numpy as jnp
from jax import lax
from jax.experimental import pallas as pl
from jax.experimental.pallas import tpu as pltpu

F32 = jnp.float32
BF16 = jnp.bfloat16
MESH = pl.DeviceIdType.MESH

T = 2048
D = 1024
D_IN = 2816
N_DEV = 8
WIN_BLK = D_IN // N_DEV
WOUT_BLK = D // N_DEV
CH = 128
N_CH = T // CH
RET_H, RET_DK, RET_DV = 4, 64, 128
RET_QK = RET_H * RET_DK
RET_V = RET_H * RET_DV
SWA_H, SWA_KV, SWA_D, SWA_G = 8, 2, 64, 4
N_BUCKETS = 32
NORM_EPS = 1e-6
GN_EPS = 1e-5
NEG_INF = -1e30
PIECES = (256, 256, 512, 512, 512, 128, 128, 512)
OFFS = tuple(int(v) for v in np.cumsum((0,) + PIECES))
RET_W = OFFS[4]
SWA_W = D_IN - RET_W
SWA_OFFS = tuple(o - RET_W for o in OFFS[4:])
TM = 256

ADAM_LR, ADAM_B1, ADAM_B2, ADAM_EPS, ADAM_WD, ADAM_STEP = 0.001, 0.9, 0.999, 1e-08, 0.01, 10

VMEM_LIMIT = 56 * 1024 * 1024


def _cparams(n_grid=0, vmem=VMEM_LIMIT):
    sem = ("arbitrary",) * n_grid if n_grid else None
    return pltpu.CompilerParams(dimension_semantics=sem, vmem_limit_bytes=vmem)


class _Part(NamedTuple):
    step: Callable
    inputs: list
    in_specs: list
    out_specs: list
    out_shape: list
    scratch_shapes: list


def _fused_call(name, n_steps, parts):
    n_in = [len(p.inputs) for p in parts]
    n_out = [len(p.out_shape) for p in parts]
    n_scr = [len(p.scratch_shapes) for p in parts]

    def body(*refs):
        ins, outs, scr = refs[:sum(n_in)], refs[sum(n_in):sum(n_in) + sum(n_out)], refs[sum(n_in) + sum(n_out):]
        for i, p in enumerate(parts):
            take = lambda seq, counts: seq[sum(counts[:i]):sum(counts[:i + 1])]
            p.step(*take(ins, n_in), *take(outs, n_out), *take(scr, n_scr))

    flat = lambda field: [v for p in parts for v in getattr(p, field)]
    outs = pl.pallas_call(
        body, name=name, grid=(n_steps,), in_specs=flat("in_specs"), out_specs=tuple(flat("out_specs")),
        out_shape=tuple(flat("out_shape")), scratch_shapes=flat("scratch_shapes"), compiler_params=_cparams(1),
    )(*flat("inputs"))
    return [list(outs[sum(n_out[:i]):sum(n_out[:i + 1])]) for i in range(len(parts))]


def _dot(a, b):
    return jnp.dot(a, b, preferred_element_type=F32)


def _dot_nt(a, b):
    return lax.dot_general(a, b, (((1,), (1,)), ((), ())), preferred_element_type=F32)


def _bf(a):
    return a.astype(BF16)


def _sigmoid(x):
    return 1.0 / (1.0 + jnp.exp(-x))


def _adamw(w, g, m, v):
    m = ADAM_B1 * m + (1.0 - ADAM_B1) * g
    v = ADAM_B2 * v + (1.0 - ADAM_B2) * (g * g)
    m_hat = m / (1.0 - ADAM_B1 ** ADAM_STEP)
    v_hat = v / (1.0 - ADAM_B2 ** ADAM_STEP)
    delta = -ADAM_LR * (m_hat / (jnp.sqrt(v_hat) + ADAM_EPS) + ADAM_WD * w)
    return delta, m, v


def _rotary_tables_t():
    half = RET_DK // 2
    inv_freq = np.float32(10000.0) ** (-np.arange(half, dtype=np.float32) / np.float32(half))
    ang = inv_freq[:, None] * np.arange(T, dtype=np.float32)[None, :]
    cos, sin = np.cos(ang).astype(np.float32), np.sin(ang).astype(np.float32)
    cos64 = np.concatenate([cos, cos], axis=0)
    sin64 = np.concatenate([-sin, sin], axis=0)
    return np.tile(cos64, (RET_H, 1)), np.tile(sin64, (RET_H, 1))


def _retention_tables_t():
    gamma = (1.0 - np.exp2(-5.0 - np.arange(RET_H, dtype=np.float32))).astype(np.float32)
    log_g = np.log(gamma).astype(np.float32)
    i = np.arange(CH, dtype=np.float32)
    diff = i[None, :] - i[:, None]
    decay = np.where(diff >= 0, np.exp(log_g[:, None, None] * np.maximum(diff, 0.0)), 0.0).astype(np.float32)
    decay_all = np.concatenate(list(decay), axis=1)
    zeta = np.exp(log_g[:, None] * (CH - 1.0 - i)).astype(np.float32)
    zeta_tab = np.repeat(zeta.T, RET_DK, axis=1)
    xi = np.exp(log_g[:, None] * (i + 1.0)).astype(np.float32)
    xi_tab = np.repeat(xi, RET_DK, axis=0)
    chunk_decay = np.exp(log_g * np.float32(CH)).astype(np.float32)
    row_head = np.arange(RET_V)[:, None] // RET_DV
    col_head = np.arange(RET_QK)[None, :] // RET_DK
    state_mask = (row_head == col_head).astype(np.float32)
    state_decay = (state_mask * chunk_decay[row_head]).astype(np.float32)
    q_mask = (np.arange(RET_QK)[:, None] // RET_DK == np.arange(RET_H * CH)[None, :] // CH).astype(np.float32)
    return (np.ascontiguousarray(decay_all), np.ascontiguousarray(zeta_tab), np.ascontiguousarray(xi_tab),
            state_mask, state_decay, q_mask)


def _bucket_table_t():
    qi = np.arange(CH)[:, None]
    kj = np.arange(2 * CH)[None, :]
    dist = qi + CH - kj
    n = np.maximum(dist, 0)
    max_exact = N_BUCKETS // 2
    nf = np.maximum(n, 1).astype(np.float32)
    large = max_exact + (np.log(nf / np.float32(max_exact)) / np.float32(np.log(CH / max_exact))
                         * np.float32(N_BUCKETS - max_exact)).astype(np.int32)
    large = np.minimum(large, N_BUCKETS - 1)
    bucket = np.where(n < max_exact, n, large)
    return np.ascontiguousarray(np.where((dist >= 0) & (dist < CH), bucket, -1).astype(np.int32).T)


def _mesh_pos():
    return lax.axis_index("x"), lax.axis_index("y"), lax.axis_index("c")


def _blk(px, py, pc):
    return 4 * px + 2 * py + pc


def _in_proj_gather(x, norm_w, w_in_t):
    tchunk = 512

    def body(x_ref, nw_ref, win_ref, proj_ref, wt_ref, rstd_ref, h_ref, stage, send_sems, recv_sems, out_sems):
        x, y, c = _mesh_pos()
        me = _blk(x, y, c)
        sibling = (x, y, 1 - c)
        nbr_a, nbr_b, diag = (x ^ (1 - c), y ^ c), (x ^ c, y ^ (1 - c)), (1 - x, 1 - y)

        def copy(k, b, to):
            return pltpu.make_async_remote_copy(src_ref=wt_ref.at[b], dst_ref=wt_ref.at[b], send_sem=send_sems.at[k],
                                                recv_sem=recv_sems.at[k], device_id=to, device_id_type=MESH)

        wt_ref[me] = _bf(win_ref[...])
        first = [copy(0, me, sibling), copy(1, me, (*nbr_a, c))]
        for cp in first:
            cp.start()
        to_b = copy(2, me, (*nbr_b, c))

        nw = nw_ref[...]
        for i in range(T // TM):
            rows = slice(i * TM, (i + 1) * TM)
            xv = x_ref[rows, :]
            r = lax.rsqrt(jnp.mean(xv * xv, axis=-1, keepdims=True) + NORM_EPS)
            h_ref[rows, :] = _bf(xv * r * nw)
            rstd_ref[rows, :] = r

        writes = []

        def project(b):
            k = len(writes)
            if k >= 2:
                writes[k - 2].wait()
            w = wt_ref[b]
            for t in range(T // tchunk):
                cols = slice(t * tchunk, (t + 1) * tchunk)
                stage[k % 2, :, cols] = _dot_nt(w, h_ref[cols, :])
            cp = pltpu.make_async_copy(stage.at[k % 2], proj_ref.at[b], out_sems.at[k % 2])
            cp.start()
            writes.append(cp)

        here = (x, y, c)
        project(me)
        copy(0, _blk(x, y, 1 - c), here).wait_recv()
        project(_blk(x, y, 1 - c))
        to_b.start()
        passed = [to_b]

        def landed(k, chip, relay_to=None):
            copy(k, _blk(*chip, c), here).wait_recv()
            out = [copy({1: 4, 2: 5, 3: 6}[k], _blk(*chip, c), sibling)]
            if relay_to is not None:
                out.append(copy(3, _blk(*chip, c), (*relay_to, c)))
            for cp in out:
                cp.start()
            passed.extend(out)
            project(_blk(*chip, c))

        def from_sibling(k, chip):
            copy(k, _blk(*chip, 1 - c), here).wait_recv()
            project(_blk(*chip, 1 - c))

        landed(1, nbr_a, relay_to=nbr_b)
        from_sibling(4, nbr_b)
        landed(2, nbr_b)
        from_sibling(5, nbr_a)
        landed(3, diag)
        from_sibling(6, diag)
        writes[-2].wait()
        writes[-1].wait()
        for cp in first + passed:
            cp.wait_send()

    vm = pl.BlockSpec(memory_space=pltpu.VMEM)
    proj, wt, rstd, h = pl.pallas_call(
        body, name="in_proj_gather",
        out_shape=(jax.ShapeDtypeStruct((N_DEV, WIN_BLK, T), F32), jax.ShapeDtypeStruct((N_DEV, WIN_BLK, D), BF16),
                   jax.ShapeDtypeStruct((T, 1), F32), jax.ShapeDtypeStruct((T, D), BF16)),
        in_specs=[vm, vm, vm], out_specs=(pl.BlockSpec(memory_space=pl.ANY), vm, vm, vm),
        scratch_shapes=[pltpu.VMEM((2, WIN_BLK, T), F32),
                        pltpu.SemaphoreType.DMA((7,)), pltpu.SemaphoreType.DMA((7,)), pltpu.SemaphoreType.DMA((2,))],
        compiler_params=_cparams(vmem=48 * 1024 * 1024),
    )(x, norm_w, w_in_t)
    return proj.reshape(D_IN, T), wt.reshape(D_IN, D), rstd, h


def _gather_w_out(w_out_ref, own_ref, wo_ref, local_sem, send_sems, recv_sems, phase):
    x, y, c = _mesh_pos()
    me = _blk(x, y, c)
    sibling = (x, y, 1 - c)
    chips = [(1 - x, y), (x, 1 - y), (1 - x, 1 - y)]

    def copy(k, b, to, src=None):
        return pltpu.make_async_remote_copy(src_ref=wo_ref.at[b] if src is None else src, dst_ref=wo_ref.at[b],
                                            send_sem=send_sems.at[k], recv_sem=recv_sems.at[k], device_id=to,
                                            device_id_type=MESH)

    local = pltpu.make_async_copy(own_ref, wo_ref.at[me], local_sem)
    first = [copy(0, me, sibling, own_ref)] + [copy(1 + j, me, (*chip, c), own_ref) for j, chip in enumerate(chips)]
    passed = [copy(4 + j, _blk(*chip, c), sibling) for j, chip in enumerate(chips)]
    if phase == 0:
        own_ref[...] = _bf(w_out_ref[...])
        local.start()
        for cp in first:
            cp.start()
    elif phase == 1:
        for j, chip in enumerate(chips):
            copy(1 + j, _blk(*chip, c), (x, y, c)).wait_recv()
            passed[j].start()
    else:
        local.wait()
        copy(0, _blk(x, y, 1 - c), (x, y, c)).wait_recv()
        for j, chip in enumerate(chips):
            copy(4 + j, _blk(*chip, 1 - c), (x, y, c)).wait_recv()
        for cp in first + passed:
            cp.wait_send()


def _proj_piece(i, block_of):
    rows = PIECES[i]
    assert OFFS[i] % rows == 0
    return pl.BlockSpec((rows, CH), lambda n: (OFFS[i] // rows, block_of(n)))


def _proj_gate_halves(block_of):
    return [pl.BlockSpec((256, CH), lambda n, j=j: (OFFS[7] // 256 + j, block_of(n))) for j in range(2)]


def _swap_halves_t(t):
    half = RET_DK // 2
    parts = []
    for h in range(RET_H):
        parts += [t[h * RET_DK + half:(h + 1) * RET_DK], t[h * RET_DK:h * RET_DK + half]]
    return jnp.concatenate(parts, axis=0)


def _rotate(t, cos, sin):
    return t * cos + _swap_halves_t(t) * sin


def _group_norm_t(o):
    mu = jnp.mean(o, axis=0, keepdims=True)
    var = jnp.mean((o - mu) * (o - mu), axis=0, keepdims=True)
    rstd = lax.rsqrt(var + GN_EPS)
    return (o - mu) * rstd, rstd


def _retention_scores_t(q_t, k_rm_b, q_mask, decay_all):
    q_heads = _bf(jnp.concatenate([q_t] * RET_H, axis=1) * q_mask)
    return _dot(k_rm_b, q_heads) * decay_all


def _retention_fwd(proj, ret_norm_w, tables, w_out_blk):
    cos_t, sin_t, decay_all, zeta_tab, xi_tab, state_mask, state_decay, q_mask = tables

    def body(rq_ref, rk_ref, rv_ref, rg_ref, w_ref, cos_ref, sin_ref, decay_ref, zeta_ref, xi_ref, smask_ref,
             sdecay_ref, qmask_ref, wout_ref, ro_ref, mix_ref, st_ref, wo_ref, state, wo_own, wo_local, wo_send, wo_recv):
        n = pl.program_id(0)

        @pl.when(n == 0)
        def _():
            state[...] = jnp.zeros_like(state)
            _gather_w_out(wout_ref, wo_own, wo_ref, wo_local, wo_send, wo_recv, phase=0)

        @pl.when(n == N_CH // 2)
        def _():
            _gather_w_out(wout_ref, wo_own, wo_ref, wo_local, wo_send, wo_recv, phase=1)

        @pl.when(n == N_CH - 1)
        def _():
            _gather_w_out(wout_ref, wo_own, wo_ref, wo_local, wo_send, wo_recv, phase=2)

        cos, sin = cos_ref[...], sin_ref[...]
        q_t = _rotate(rq_ref[...], cos, sin)
        k_t = _rotate(rk_ref[...], cos, sin) * (RET_DK ** -0.5)
        k_rm = k_t.T
        v_b = _bf(rv_ref[...])
        m_b = _bf(state[...])
        st_ref[0] = m_b
        scores_b = _bf(_retention_scores_t(q_t, _bf(k_rm), qmask_ref[...], decay_ref[...]))
        cross = _dot(m_b, _bf(q_t * xi_ref[...]))
        state[...] = state[...] * sdecay_ref[...] + _dot(v_b, _bf(k_rm * zeta_ref[...])) * smask_ref[...]
        heads = range(RET_H)
        rows = [slice(h * RET_DV, (h + 1) * RET_DV) for h in heads]
        o = [_dot(v_b[rows[h]], scores_b[:, h * CH:(h + 1) * CH]) + cross[rows[h]] for h in heads]
        for h in heads:
            rn, _ = _group_norm_t(o[h])
            g = rg_ref[rows[h], :]
            ro_ref[rows[h], :] = o[h]
            mix_ref[rows[h], :] = rn * w_ref[rows[h], :] * (g * _sigmoid(g))

    col = lambda w: pl.BlockSpec((w, CH), lambda n: (0, n))
    const = lambda shape: pl.BlockSpec(shape, lambda n: (0,) * len(shape))
    cur = lambda n: n
    return _Part(
        body,
        inputs=[proj, proj, proj, proj, ret_norm_w, cos_t, sin_t, decay_all, zeta_tab, xi_tab, state_mask, state_decay,
                q_mask, w_out_blk],
        in_specs=[_proj_piece(i, cur) for i in range(4)] + [
            const((RET_V, 1)), col(RET_QK), col(RET_QK),
            const(decay_all.shape), const(zeta_tab.shape), const(xi_tab.shape), const(state_mask.shape),
            const(state_decay.shape), const(q_mask.shape), const((WOUT_BLK, D))],
        out_specs=[col(RET_V), col(RET_V), pl.BlockSpec((1, RET_V, RET_QK), lambda n: (n, 0, 0)),
                   pl.BlockSpec(memory_space=pl.ANY)],
        out_shape=[jax.ShapeDtypeStruct((RET_V, T), F32), jax.ShapeDtypeStruct((RET_V, T), F32),
                   jax.ShapeDtypeStruct((N_CH, RET_V, RET_QK), BF16), jax.ShapeDtypeStruct((N_DEV, WOUT_BLK, D), BF16)],
        scratch_shapes=[pltpu.VMEM((RET_V, RET_QK), F32), pltpu.VMEM((WOUT_BLK, D), BF16), pltpu.SemaphoreType.DMA,
                        pltpu.SemaphoreType.DMA((N_DEV - 1,)), pltpu.SemaphoreType.DMA((N_DEV - 1,))])


def _retention_bwd(proj, ro, states, dmix, ret_norm_w, tables):
    cos_t, sin_t, decay_all, zeta_tab, xi_tab, state_mask, state_decay, q_mask = tables

    def body(*refs):
        @pl.when(pl.program_id(0) < N_CH)
        def _():
            chunk(*refs)

    def chunk(rq_ref, rk_ref, rv_ref, rg_ref, ro_ref, st_ref, dm_ref, w_ref, cos_ref, sin_ref, decay_ref, zeta_ref,
              xi_ref, smask_ref, sdecay_ref, qmask_ref, d_ref, dw_ref, gstate):
        i = pl.program_id(0)

        @pl.when(i == 0)
        def _():
            gstate[...] = jnp.zeros_like(gstate)
            dw_ref[...] = jnp.zeros_like(dw_ref)

        cos, sin = cos_ref[...], sin_ref[...]
        q_t = _rotate(rq_ref[...], cos, sin)
        k_t = _rotate(rk_ref[...], cos, sin) * (RET_DK ** -0.5)
        q_b, k_b = _bf(q_t), _bf(k_t)
        k_rm = k_t.T
        kz_b = _bf(k_rm * zeta_ref[...])
        qx_b = _bf(q_t * xi_ref[...])
        v_t = rv_ref[...]
        v_b = _bf(v_t)
        v_rm_b = _bf(v_t.T)
        decay = decay_ref[...]
        scores_b = _bf(_retention_scores_t(q_t, _bf(k_rm), qmask_ref[...], decay))
        heads = range(RET_H)
        rows = [slice(h * RET_DV, (h + 1) * RET_DV) for h in heads]
        qk_rows = [slice(h * RET_DK, (h + 1) * RET_DK) for h in heads]
        lanes = [slice(h * CH, (h + 1) * CH) for h in heads]
        do = []
        for h in heads:
            g, w, dm = rg_ref[rows[h], :], w_ref[rows[h], :], dm_ref[rows[h], :]
            rn, rstd = _group_norm_t(ro_ref[rows[h], :])
            sig = _sigmoid(g)
            silu = g * sig
            d_ref[OFFS[3] + h * RET_DV:OFFS[3] + (h + 1) * RET_DV, :] = _bf(dm * rn * w * (sig * (1.0 + g * (1.0 - sig))))
            dw_ref[rows[h], :] += dm * silu * rn
            drn = dm * silu * w
            do.append(rstd * (drn - jnp.mean(drn, axis=0, keepdims=True)
                              - rn * jnp.mean(drn * rn, axis=0, keepdims=True)))
        do_b = _bf(jnp.concatenate(do, axis=0))
        m_b = st_ref[0]
        g_all = gstate[...]
        g_b = _bf(g_all)
        dscores_b = [_bf(_dot(v_rm_b[:, rows[h]], do_b[rows[h]]) * decay[:, lanes[h]]) for h in heads]
        dq_cross = lax.dot_general(m_b, do_b, (((0,), (0,)), ((), ())), preferred_element_type=F32)
        dkz = _dot(v_rm_b, g_b)
        dv_cross = _dot_nt(g_b, kz_b)
        gstate[...] = g_all * sdecay_ref[...] + _dot_nt(do_b, qx_b) * smask_ref[...]
        dq = jnp.concatenate([_dot(k_b[qk_rows[h]], dscores_b[h]) for h in heads], axis=0) + dq_cross * xi_ref[...]
        dk = (jnp.concatenate([_dot_nt(q_b[qk_rows[h]], dscores_b[h]) for h in heads], axis=0)
              + (dkz * zeta_ref[...]).T) * (RET_DK ** -0.5)
        for h in heads:
            d_ref[OFFS[2] + h * RET_DV:OFFS[2] + (h + 1) * RET_DV, :] = _bf(
                _dot_nt(do_b[rows[h]], scores_b[:, lanes[h]]) + dv_cross[rows[h]])
        d_ref[OFFS[0]:OFFS[1], :] = _bf(dq * cos + _swap_halves_t(dq * sin))
        d_ref[OFFS[1]:OFFS[2], :] = _bf(dk * cos + _swap_halves_t(dk * sin))

    chunk_of = lambda i: N_CH - 1 - jnp.minimum(i, N_CH - 1)
    col = lambda w: pl.BlockSpec((w, CH), lambda i: (0, chunk_of(i)))
    const = lambda shape: pl.BlockSpec(shape, lambda i: (0,) * len(shape))
    return _Part(
        body,
        inputs=[proj, proj, proj, proj, ro, states, dmix, ret_norm_w, cos_t, sin_t, decay_all, zeta_tab, xi_tab,
                state_mask, state_decay, q_mask],
        in_specs=[_proj_piece(j, chunk_of) for j in range(4)] + [
                  col(RET_V),
                  pl.BlockSpec((1, RET_V, RET_QK), lambda i: (chunk_of(i), 0, 0)), col(RET_V),
                  const((RET_V, 1)), col(RET_QK), col(RET_QK),
                  const(decay_all.shape), const(zeta_tab.shape), const(xi_tab.shape), const(state_mask.shape),
                  const(state_decay.shape), const(q_mask.shape)],
        out_specs=[col(RET_W), const((RET_V, CH))],
        out_shape=[jax.ShapeDtypeStruct((RET_W, T), BF16), jax.ShapeDtypeStruct((RET_V, CH), F32)],
        scratch_shapes=[pltpu.VMEM((RET_V, RET_QK), F32)])


HQ_LANES = SWA_H * CH


def _head_lanes(hq):
    return slice(hq * CH, (hq + 1) * CH)


def _build_bias_t(bucket_ref, relb_ref, sinks_ref, bias_ref, sink_row):
    bt = bucket_ref[...]
    first = lax.broadcasted_iota(jnp.int32, bt.shape, 0) < CH
    for hq in range(SWA_H):
        b = jnp.full(bt.shape, NEG_INF, F32)
        for bk in range(N_BUCKETS):
            b = jnp.where(bt == bk, relb_ref[hq * N_BUCKETS + bk], b)
        bias_ref[0, :, _head_lanes(hq)] = b
        bias_ref[1, :, _head_lanes(hq)] = jnp.where(first, NEG_INF, b)
        sink_row[:, _head_lanes(hq)] = jnp.full((1, CH), sinks_ref[hq], F32)


def _rms_t(t, w_col):
    r = lax.rsqrt(jnp.mean(t * t, axis=0, keepdims=True) + NORM_EPS)
    return t * r * w_col, r


def _rms_t_bwd(dn, t, r, w_col):
    u = dn * w_col
    return r * u - t * (r * r * r) * jnp.mean(u * t, axis=0, keepdims=True), dn * t * r


def _norm_kv(k_t, kw):
    return jnp.concatenate([_rms_t(k_t[g * SWA_D:(g + 1) * SWA_D], kw)[0] for g in range(SWA_KV)], axis=0)


def _kv_slot(a, kv):
    z = jnp.zeros_like(a)
    return jnp.concatenate([a, z] if kv == 0 else [z, a], axis=0)


def _softmax_t(s, sink):
    m = jnp.maximum(jnp.max(s, axis=0, keepdims=True), sink)
    p = jnp.exp(s - m)
    e_sink = jnp.exp(sink - m)
    inv = 1.0 / (jnp.sum(p, axis=0, keepdims=True) + e_sink)
    return p * inv, e_sink * inv


def _gate_rows(sg_halves, hq):
    per_half = SWA_H // 2
    return sg_halves[hq // per_half][(hq % per_half) * SWA_D:(hq % per_half + 1) * SWA_D, :]


def _swa_fwd(proj, q_norm_w, k_norm_w, sinks, rel_bias_flat, bucket_t):
    def body(sq_ref, skp_ref, skc_ref, svp_ref, svc_ref, sg_lo, sg_hi, qw_ref, kw_ref, sinks_ref, relb_ref, bucket_ref,
             so_ref, mix_ref, bias_ref, sink_row):
        n = pl.program_id(0)

        @pl.when(n == 0)
        def _():
            _build_bias_t(bucket_ref, relb_ref, sinks_ref, bias_ref, sink_row)

        var = (n == 0).astype(jnp.int32)
        qw, kw = qw_ref[...], kw_ref[...]
        kn_band = jnp.concatenate([_norm_kv(skp_ref[...], kw), _norm_kv(skc_ref[...], kw)], axis=1)
        kb_rm = _bf(kn_band.T)
        vband = _bf(jnp.concatenate([svp_ref[...], svc_ref[...]], axis=1))
        q_all = jnp.concatenate(
            [_bf(_kv_slot(_rms_t(sq_ref[hq * SWA_D:(hq + 1) * SWA_D, :], qw)[0] * (SWA_D ** -0.5), hq // SWA_G))
             for hq in range(SWA_H)], axis=1)
        probs, _ = _softmax_t(_dot(kb_rm, q_all) + bias_ref[var], sink_row[...])
        probs_b = _bf(probs)
        for kv in range(SWA_KV):
            o = _dot(vband[kv * SWA_D:(kv + 1) * SWA_D], probs_b[:, kv * SWA_G * CH:(kv + 1) * SWA_G * CH])
            for j in range(SWA_G):
                rows = slice((kv * SWA_G + j) * SWA_D, (kv * SWA_G + j + 1) * SWA_D)
                oh = o[:, j * CH:(j + 1) * CH]
                g = _gate_rows((sg_lo, sg_hi), kv * SWA_G + j)
                so_ref[rows, :] = oh
                mix_ref[rows, :] = oh * (g * _sigmoid(g))

    col = lambda w: pl.BlockSpec((w, CH), lambda n: (0, n))
    const = lambda shape: pl.BlockSpec(shape, lambda n: (0,) * len(shape))
    smem = pl.BlockSpec(memory_space=pltpu.SMEM)
    cur = lambda n: n
    prev = lambda n: jnp.maximum(n - 1, 0)
    return _Part(
        body,
        inputs=[proj, proj, proj, proj, proj, proj, proj, q_norm_w, k_norm_w, sinks, rel_bias_flat, bucket_t],
        in_specs=[_proj_piece(4, cur), _proj_piece(5, prev), _proj_piece(5, cur), _proj_piece(6, prev),
                  _proj_piece(6, cur)] + _proj_gate_halves(cur) + [
                  const((SWA_D, 1)), const((SWA_D, 1)), smem, smem, const((2 * CH, CH))],
        out_specs=[col(512), col(512)],
        out_shape=[jax.ShapeDtypeStruct((512, T), F32), jax.ShapeDtypeStruct((512, T), F32)],
        scratch_shapes=[pltpu.VMEM((2, 2 * CH, HQ_LANES), F32), pltpu.VMEM((1, HQ_LANES), F32)])


def _swa_bwd(proj, so, dmix, q_norm_w, k_norm_w, sinks, rel_bias_flat, bucket_t):
    def body(sq_ref, skp_ref, skc_ref, svp_ref, svc_ref, sg_lo, sg_hi, so_ref, dm_ref, qw_ref, kw_ref, sinks_ref,
             relb_ref, bucket_ref, d_ref, dbias_ref, dsink_ref, dqw_ref, dkw_ref,
             bias_ref, sink_row, band_dk, band_dv, carry_dk, carry_dv, hold_dq, hold_dg):
        n = pl.program_id(0)
        slot = n % 2

        @pl.when(n == 0)
        def _():
            _build_bias_t(bucket_ref, relb_ref, sinks_ref, bias_ref, sink_row)
            for ref in (dbias_ref, dsink_ref, dqw_ref, dkw_ref, carry_dk, carry_dv):
                ref[...] = jnp.zeros_like(ref)

        qw, kw = qw_ref[...], kw_ref[...]

        @pl.when(n < N_CH)
        def _():
            var = (n == 0).astype(jnp.int32)
            kn_band = jnp.concatenate([_norm_kv(skp_ref[...], kw), _norm_kv(skc_ref[...], kw)], axis=1)
            kb_rm = _bf(kn_band.T)
            kn_band_s = _bf(kn_band * (SWA_D ** -0.5))
            vband_f = jnp.concatenate([svp_ref[...], svc_ref[...]], axis=1)
            vb_rm = _bf(vband_f.T)
            q_raw, q_rstd, qs_b, do_b = [], [], [], []
            for hq in range(SWA_H):
                rows = slice(hq * SWA_D, (hq + 1) * SWA_D)
                q_t = sq_ref[rows, :]
                qn, rq = _rms_t(q_t, qw)
                g = _gate_rows((sg_lo, sg_hi), hq)
                sig = _sigmoid(g)
                dm = dm_ref[rows, :]
                hold_dg[slot, rows, :] = _bf(dm * so_ref[rows, :] * (sig * (1.0 + g * (1.0 - sig))))
                q_raw.append(q_t)
                q_rstd.append(rq)
                qs_b.append(_bf(qn * (SWA_D ** -0.5)))
                do_b.append(_bf(dm * (g * sig)))
            q_all = jnp.concatenate([_kv_slot(qs_b[hq], hq // SWA_G) for hq in range(SWA_H)], axis=1)
            do_all = jnp.concatenate([_kv_slot(do_b[hq], hq // SWA_G) for hq in range(SWA_H)], axis=1)
            probs, p_sink = _softmax_t(_dot(kb_rm, q_all) + bias_ref[var], sink_row[...])
            dprobs = _dot(vb_rm, do_all)
            t = jnp.sum(probs * dprobs, axis=0, keepdims=True)
            dlog = probs * (dprobs - t)
            dsink_ref[...] += -(p_sink * t)
            dbias_ref[...] += dlog
            dlog_b, probs_b = _bf(dlog), _bf(probs)
            dkn, dvv = [], []
            for kv in range(SWA_KV):
                heads = range(kv * SWA_G, (kv + 1) * SWA_G)
                lanes = slice(kv * SWA_G * CH, (kv + 1) * SWA_G * CH)
                dvv.append(_dot_nt(jnp.concatenate([do_b[hq] for hq in heads], axis=1), probs_b[:, lanes]))
                dkn.append(_dot_nt(jnp.concatenate([qs_b[hq] for hq in heads], axis=1), dlog_b[:, lanes]))
                dqn = _dot(kn_band_s[kv * SWA_D:(kv + 1) * SWA_D], dlog_b[:, lanes])
                for j, hq in enumerate(heads):
                    dq_t, dqw_terms = _rms_t_bwd(dqn[:, j * CH:(j + 1) * CH], q_raw[hq], q_rstd[hq], qw)
                    hold_dq[slot, hq * SWA_D:(hq + 1) * SWA_D, :] = _bf(dq_t)
                    dqw_ref[...] += dqw_terms
            band_dk[...] = jnp.concatenate(dkn, axis=0)
            band_dv[...] = jnp.concatenate(dvv, axis=0)

        @pl.when(n == N_CH)
        def _():
            band_dk[...] = jnp.zeros_like(band_dk)
            band_dv[...] = jnp.zeros_like(band_dv)

        @pl.when(n >= 1)
        def _():
            dkn_prev = carry_dk[...] + band_dk[:, 0:CH]
            k_t = skp_ref[...]
            for kv in range(SWA_KV):
                rows = slice(kv * SWA_D, (kv + 1) * SWA_D)
                _, rk = _rms_t(k_t[rows], kw)
                dk_t, dkw_terms = _rms_t_bwd(dkn_prev[rows], k_t[rows], rk, kw)
                d_ref[SWA_OFFS[1] + kv * SWA_D:SWA_OFFS[1] + (kv + 1) * SWA_D, :] = _bf(dk_t)
                dkw_ref[...] += dkw_terms
            d_ref[SWA_OFFS[2]:SWA_OFFS[3], :] = _bf(carry_dv[...] + band_dv[:, 0:CH])
            d_ref[SWA_OFFS[0]:SWA_OFFS[1], :] = hold_dq[1 - slot]
            d_ref[SWA_OFFS[3]:SWA_OFFS[4], :] = hold_dg[1 - slot]

        carry_dk[...] = band_dk[:, CH:2 * CH]
        carry_dv[...] = band_dv[:, CH:2 * CH]

    cur_block = lambda n: jnp.minimum(n, N_CH - 1)
    prev_block = lambda n: jnp.maximum(n - 1, 0)
    col = lambda w: pl.BlockSpec((w, CH), lambda n: (0, cur_block(n)))
    prev = lambda w: pl.BlockSpec((w, CH), lambda n: (0, prev_block(n)))
    const = lambda shape: pl.BlockSpec(shape, lambda n: (0,) * len(shape))
    smem = pl.BlockSpec(memory_space=pltpu.SMEM)
    return _Part(
        body,
        inputs=[proj, proj, proj, proj, proj, proj, proj, so, dmix, q_norm_w, k_norm_w, sinks, rel_bias_flat, bucket_t],
        in_specs=[_proj_piece(4, cur_block), _proj_piece(5, prev_block), _proj_piece(5, cur_block),
                  _proj_piece(6, prev_block), _proj_piece(6, cur_block)] + _proj_gate_halves(cur_block) + [
                  col(512), col(512), const((SWA_D, 1)), const((SWA_D, 1)), smem, smem, const((2 * CH, CH))],
        out_specs=[prev(SWA_W), const((2 * CH, HQ_LANES)), const((1, HQ_LANES)),
                   const((SWA_D, CH)), const((SWA_D, CH))],
        out_shape=[jax.ShapeDtypeStruct((SWA_W, T), BF16),
                   jax.ShapeDtypeStruct((2 * CH, HQ_LANES), F32), jax.ShapeDtypeStruct((1, HQ_LANES), F32),
                   jax.ShapeDtypeStruct((SWA_D, CH), F32), jax.ShapeDtypeStruct((SWA_D, CH), F32)],
        scratch_shapes=[pltpu.VMEM((2, 2 * CH, HQ_LANES), F32), pltpu.VMEM((1, HQ_LANES), F32),
                        pltpu.VMEM((128, 2 * CH), F32), pltpu.VMEM((128, 2 * CH), F32),
                        pltpu.VMEM((128, CH), F32), pltpu.VMEM((128, CH), F32),
                        pltpu.VMEM((2, 512, CH), BF16), pltpu.VMEM((2, 512, CH), BF16)])


SMALL_GRAD_SHAPES = ((1, RET_V), (1, CH), (1, CH), (1, CH), (SWA_H, N_BUCKETS))


def _finish_small_grads(dbias_ref, bucket_ref, dsink_ref, dqw_ref, dkw_ref, drw_ref, rw_o, qw_o, kw_o, sink_o, relb_o):
    bt = bucket_ref[...]
    row = lax.broadcasted_iota(jnp.int32, (SWA_H, N_BUCKETS), 0)
    col = lax.broadcasted_iota(jnp.int32, (SWA_H, N_BUCKETS), 1)
    lane = lax.broadcasted_iota(jnp.int32, (1, CH), 1)
    acc = jnp.zeros((SWA_H, N_BUCKETS), F32)
    sink = jnp.zeros((1, CH), F32)
    for hq in range(SWA_H):
        d = dbias_ref[:, _head_lanes(hq)]
        for bk in range(N_BUCKETS):
            s = jnp.sum(jnp.sum(jnp.where(bt == bk, d, 0.0), axis=0, keepdims=True), axis=1, keepdims=True)
            acc = acc + jnp.where((row == hq) & (col == bk), s, 0.0)
        sink = sink + jnp.where(lane == hq, jnp.sum(dsink_ref[:, _head_lanes(hq)], axis=1, keepdims=True), 0.0)
    relb_o[...] = acc
    sink_o[...] = sink
    for src, dst in ((dqw_ref, qw_o), (dkw_ref, kw_o)):
        padded = jnp.concatenate([src[...], jnp.zeros((CH - SWA_D, CH), F32)], axis=0)
        dst[...] = jnp.sum(padded.T, axis=0, keepdims=True)
    rw_o[...] = jnp.sum(drw_ref[...].T, axis=0, keepdims=True)


def _out_proj(mix_r, mix_s, wo, x, target):
    def body(mr_ref, ms_ref, w_ref, x_ref, t_ref, loss_ref, dy_ref, dmr_ref, dms_ref, gw_ref):
        i = pl.program_id(0)

        @pl.when(i == 0)
        def _():
            loss_ref[...] = jnp.zeros_like(loss_ref)
            gw_ref[...] = jnp.zeros_like(gw_ref)

        mixed = jnp.concatenate([mr_ref[...], ms_ref[...]], axis=0)
        w = w_ref[...]
        err = x_ref[...] + _dot(_bf(mixed.T), w) - t_ref[...]
        loss_ref[...] += jnp.sum(jnp.sum(err * err, axis=1, keepdims=True), axis=0, keepdims=True)
        dy = err * (1.0 / D)
        dy_ref[...] = dy
        dy_b = _bf(dy)
        dmix = _dot_nt(w, dy_b)
        dmr_ref[...] = dmix[0:512]
        dms_ref[...] = dmix[512:D]
        gw_ref[...] += _dot(_bf(mixed), dy_b)

    row = lambda w: pl.BlockSpec((TM, w), lambda i: (i, 0))
    col = lambda w: pl.BlockSpec((w, TM), lambda i: (0, i))
    const = lambda shape: pl.BlockSpec(shape, lambda i: (0,) * len(shape))
    return pl.pallas_call(
        body, name="out_proj", grid=(T // TM,),
        in_specs=[col(512), col(512), const((D, D)), row(D), row(D)],
        out_specs=(const((1, 1)), row(D), col(512), col(512), const((D, D))),
        out_shape=(jax.ShapeDtypeStruct((1, 1), F32), jax.ShapeDtypeStruct((T, D), F32),
                   jax.ShapeDtypeStruct((512, T), F32), jax.ShapeDtypeStruct((512, T), F32),
                   jax.ShapeDtypeStruct((D, D), F32)),
        compiler_params=_cparams(1),
    )(mix_r, mix_s, wo, x, target)


LOSS_ROW = 5
ALL_CHIPS = ((0, 0), (0, 1), (1, 0), (1, 1))


def _w_out_reduce_scatter(gwo):
    def body(gwo_ref, gout_o, own, rcv, snd, got, tot, local_sems, a_send, a_recv, b_send, b_recv):
        k = pl.program_id(0)
        x_, y_, c_ = _mesh_pos()
        sibling = (x_, y_, 1 - c_)
        rel_chips = [(1 - x_, y_), (x_, 1 - y_), (1 - x_, 1 - y_)]

        def to_sibling(j):
            px, py = ALL_CHIPS[j]
            return pltpu.make_async_remote_copy(src_ref=gwo_ref.at[_blk(px, py, 1 - c_)], dst_ref=rcv.at[j],
                                                send_sem=a_send.at[j], recv_sem=a_recv.at[j], device_id=sibling,
                                                device_id_type=MESH)

        def local(j):
            px, py = ALL_CHIPS[j]
            return pltpu.make_async_copy(gwo_ref.at[_blk(px, py, c_)], own.at[j], local_sems.at[j])

        def to_chip(q):
            return pltpu.make_async_remote_copy(src_ref=snd.at[q], dst_ref=got.at[q], send_sem=b_send.at[q],
                                                recv_sem=b_recv.at[q], device_id=(*rel_chips[q], c_), device_id_type=MESH)

        @pl.when(k == 0)
        def _():
            for j in range(len(ALL_CHIPS)):
                to_sibling(j).start()
                local(j).start()

        @pl.when(k == 2)
        def _():
            for j in range(len(ALL_CHIPS)):
                local(j).wait()
                to_sibling(j).wait_recv()
            for q in range(3):
                j = 2 * rel_chips[q][0] + rel_chips[q][1]
                snd[q] = _bf(own[j] + rcv[j])
                to_chip(q).start()
            jm = 2 * x_ + y_
            tot[...] = own[jm] + rcv[jm]

        @pl.when(k == N_CH)
        def _():
            g = tot[...]
            for q in range(3):
                to_chip(q).wait_recv()
                g = g + got[q].astype(F32)
            gout_o[...] = g
            for j in range(len(ALL_CHIPS)):
                to_sibling(j).wait_send()
            for q in range(3):
                to_chip(q).wait_send()

    dma = pltpu.SemaphoreType.DMA
    return _Part(
        body, inputs=[gwo], in_specs=[pl.BlockSpec(memory_space=pl.ANY)],
        out_specs=[pl.BlockSpec((WOUT_BLK, D), lambda k: (0, 0))], out_shape=[jax.ShapeDtypeStruct((WOUT_BLK, D), F32)],
        scratch_shapes=[pltpu.VMEM((4, WOUT_BLK, D), F32), pltpu.VMEM((4, WOUT_BLK, D), F32),
                        pltpu.VMEM((3, WOUT_BLK, D), BF16), pltpu.VMEM((3, WOUT_BLK, D), BF16),
                        pltpu.VMEM((WOUT_BLK, D), F32), dma((4,)), dma((4,)), dma((4,)), dma((3,)), dma((3,))])


def _in_proj_bwd_rs(d_ret, d_swa, h, wt, x, rstd, norm_w, dy, sse, small_acc):
    n_blocks, n_tiles = N_DEV, T // TM
    last = n_blocks + n_tiles - 1

    def body(da_any, db_any, da_ref, db_ref, h_ref, w_ref, x_ref, r_ref, nw_ref, dy_ref, sse_ref,
             dbias_ref, bucket_ref, dsink_ref, dqw_ref, dkw_ref, drw_ref, gx_ref, gin_o, suma_o, sumb_o,
             stage, d2d_src, mine, rcv, snd, got, tot, tab_a, tab_b, gnw_acc, rw_ref, qw_ref, kw_ref, sk_ref, rb_ref,
             dp_sems, d2d_send, d2d_recv, ici_send, ici_recv, s_send, s_recv):
        k = pl.program_id(0)
        x_, y_, c_ = _mesh_pos()
        me = _blk(x_, y_, c_)
        sibling = (x_, y_, 1 - c_)
        rel_chips = [(1 - x_, y_), (x_, 1 - y_), (1 - x_, 1 - y_), (x_, y_)]

        def block_of_step(s):
            return _blk(*rel_chips[s // 2], 1 - c_ if s % 2 == 0 else c_)

        def fetch(s, wait):
            slot, b = s % 2, block_of_step(s)
            split = RET_W - 4 * WIN_BLK

            def run(src, dst, sem):
                cp = pltpu.make_async_copy(src, dst, sem)
                cp.wait() if wait else cp.start()

            @pl.when(b < 4)
            def _():
                run(da_any.at[pl.ds(pl.multiple_of(b * WIN_BLK, 16), WIN_BLK), :], stage.at[slot], dp_sems.at[slot, 0])

            @pl.when(b == 4)
            def _():
                run(da_any.at[pl.ds(4 * WIN_BLK, split), :], stage.at[slot, pl.ds(0, split), :], dp_sems.at[slot, 0])
                run(db_any.at[pl.ds(0, WIN_BLK - split), :], stage.at[slot, pl.ds(split, WIN_BLK - split), :],
                    dp_sems.at[slot, 1])

            @pl.when(b > 4)
            def _():
                run(db_any.at[pl.ds(pl.multiple_of(b * WIN_BLK - RET_W, 16), WIN_BLK), :], stage.at[slot],
                    dp_sems.at[slot, 0])

        def d2d_copy(r):
            return pltpu.make_async_remote_copy(src_ref=d2d_src, dst_ref=rcv.at[r], send_sem=d2d_send.at[r],
                                                recv_sem=d2d_recv.at[r], device_id=sibling, device_id_type=MESH)

        def ici_copy(r):
            return pltpu.make_async_remote_copy(src_ref=snd.at[r], dst_ref=got.at[r], send_sem=ici_send.at[r],
                                                recv_sem=ici_recv.at[r], device_id=(*rel_chips[r], c_),
                                                device_id_type=MESH)

        def table_copies():
            return [pltpu.make_async_remote_copy(src_ref=tab.at[me], dst_ref=tab.at[me], send_sem=s_send.at[a, p - 1],
                                                 recv_sem=s_recv.at[a, p - 1],
                                                 device_id=(x_ ^ (p >> 2), y_ ^ ((p >> 1) & 1), c_ ^ (p & 1)),
                                                 device_id_type=MESH)
                    for p in range(1, N_DEV) for a, tab in enumerate((tab_a, tab_b))]

        def chip_sum(r):
            d2d_copy(r).wait_recv()
            total = mine[...] + rcv[r]
            if r < 3:
                snd[r] = _bf(total)
                ici_copy(r).start()
            else:
                tot[...] = total

        for s in range(n_blocks):
            @pl.when(k == s)
            def _(s=s):
                r = s // 2
                if s == 0:
                    gnw_acc[...] = jnp.zeros_like(gnw_acc)
                    fetch(0, wait=False)
                fetch(s, wait=True)
                if s + 1 < n_blocks:
                    fetch(s + 1, wait=False)
                gw = _dot(stage[s % 2], h_ref[...])
                if s % 2 == 0:
                    if s >= 2:
                        chip_sum(r - 1)
                        d2d_copy(r - 1).wait_send()
                    d2d_src[...] = gw
                    d2d_copy(r).start()
                else:
                    mine[...] = gw

        @pl.when(k == n_blocks)
        def _():
            chip_sum(3)

        @pl.when(k >= n_blocks)
        def _():
            dp = jnp.concatenate([da_ref[...], db_ref[...]], axis=0)
            xv, r, nw = x_ref[...], r_ref[...], nw_ref[...]
            dh = lax.dot_general(dp, w_ref[...], (((0,), (0,)), ((), ())), preferred_element_type=F32)
            u = dh * nw
            gx_ref[...] = dy_ref[...] + r * u - xv * (r * r * r) * jnp.mean(u * xv, axis=-1, keepdims=True)
            gnw_acc[...] += jnp.sum(dh * (xv * r), axis=0, keepdims=True)

        @pl.when(k == n_blocks)
        def _():
            _finish_small_grads(dbias_ref, bucket_ref, dsink_ref, dqw_ref, dkw_ref, drw_ref,
                                rw_ref, qw_ref, kw_ref, sk_ref, rb_ref)

        @pl.when(k == last)
        def _():
            mine_a, mine_b = tab_a.at[me], tab_b.at[me]
            mine_a[...] = jnp.zeros_like(mine_a)
            mine_b[...] = jnp.zeros_like(mine_b)
            for row, ref in enumerate((gnw_acc, rw_ref, qw_ref, kw_ref, sk_ref)):
                mine_a[row:row + 1, 0:ref.shape[1]] = ref[...]
            mine_a[LOSS_ROW:LOSS_ROW + 1, 0:1] = sse_ref[...]
            mine_b[:, 0:N_BUCKETS] = rb_ref[...]
            tables = table_copies()
            for cp in tables:
                cp.start()
            g_in = tot[...]
            for q in range(3):
                ici_copy(q).wait_recv()
                g_in = g_in + got[q].astype(F32)
            gin_o[...] = g_in
            for cp in tables:
                cp.wait_recv()
            sum_a, sum_b = tab_a[0], tab_b[0]
            for b in range(1, N_DEV):
                sum_a = sum_a + tab_a[b]
                sum_b = sum_b + tab_b[b]
            suma_o[...] = sum_a
            sumb_o[...] = sum_b
            d2d_copy(3).wait_send()
            for q in range(3):
                ici_copy(q).wait_send()
            for cp in tables:
                cp.wait_send()

    tile = lambda k: jnp.maximum(k - n_blocks, 0)
    row = lambda w: pl.BlockSpec((TM, w), lambda k: (tile(k), 0))
    col = lambda w: pl.BlockSpec((w, TM), lambda k: (0, tile(k)))
    const = lambda shape, **kw: pl.BlockSpec(shape, lambda k: (0,) * len(shape), **kw)
    once = dict(pipeline_mode=pl.Buffered(1))
    hbm = pl.BlockSpec(memory_space=pl.ANY)
    dma = pltpu.SemaphoreType.DMA
    return pl.pallas_call(
        body, name="in_proj_bwd_rs", grid=(n_blocks + n_tiles,),
        in_specs=[hbm, hbm, col(RET_W), col(SWA_W), const((T, D), **once), const((D_IN, D), **once),
                  row(D), row(1), const((1, D)), row(D), const((1, 1))]
                 + [const(a.shape) for a in small_acc],
        out_specs=(row(D), const((WIN_BLK, D)), const((8, D)), const((SWA_H, CH))),
        out_shape=(jax.ShapeDtypeStruct((T, D), F32), jax.ShapeDtypeStruct((WIN_BLK, D), F32),
                   jax.ShapeDtypeStruct((8, D), F32), jax.ShapeDtypeStruct((SWA_H, CH), F32)),
        scratch_shapes=[
            pltpu.VMEM((2, WIN_BLK, T), BF16), pltpu.VMEM((WIN_BLK, D), F32),
            pltpu.VMEM((WIN_BLK, D), F32),
            pltpu.VMEM((4, WIN_BLK, D), F32), pltpu.VMEM((3, WIN_BLK, D), BF16),
            pltpu.VMEM((3, WIN_BLK, D), BF16), pltpu.VMEM((WIN_BLK, D), F32),
            pltpu.VMEM((N_DEV, 8, D), F32), pltpu.VMEM((N_DEV, SWA_H, CH), F32),
            pltpu.VMEM((1, D), F32),
        ] + [pltpu.VMEM(s, F32) for s in SMALL_GRAD_SHAPES] + [
            dma((2, 2)), dma((4,)), dma((4,)), dma((3,)), dma((3,)), dma((2, 7)), dma((2, 7)),
        ],
        compiler_params=_cparams(1),
    )(d_ret, d_swa, d_ret, d_swa, h, wt, x, rstd, norm_w, dy, sse, *small_acc)


SMALL_SHAPES = ((1, D), (1, 512), (1, SWA_D), (1, SWA_D), (1, SWA_H), (SWA_H, N_BUCKETS))


def _adamw_all(g_in_t, g_out, sum_a, sum_b, w_in_t, m_in_t, v_in_t, w_out, m_out, v_out, small_w, small_m, small_v):
    n_small = len(SMALL_SHAPES)
    halves = 2

    def body(*refs):
        gin_ref, gout_ref, suma_ref, sumb_ref, w_in_ref, m_in_ref, v_in_ref, w_out_ref, m_out_ref, v_out_ref = refs[:10]
        pos = 10
        sw_refs, sm_refs, sv_refs = (refs[pos + i * n_small:pos + (i + 1) * n_small] for i in range(3))
        pos += 3 * n_small
        g_in_o, d_in_o, nm_in_o, nv_in_o, g_out_o, d_out_o, nm_out_o, nv_out_o = refs[pos:pos + 8]
        pos += 8
        sg_o, sd_o, snm_o, snv_o = (refs[pos + i * n_small:pos + (i + 1) * n_small] for i in range(4))
        loss_o = refs[pos + 4 * n_small]

        g = gout_ref[...]
        d, nm, nv = _adamw(w_out_ref[...], g, m_out_ref[...], v_out_ref[...])
        g_out_o[...], d_out_o[...], nm_out_o[...], nv_out_o[...] = g, d, nm, nv
        g = gin_ref[...]
        d, nm, nv = _adamw(w_in_ref[...], g, m_in_ref[...], v_in_ref[...])
        g_in_o[...], d_in_o[...], nm_in_o[...], nv_in_o[...] = g, d, nm, nv

        @pl.when(pl.program_id(0) == 0)
        def _():
            loss_o[...] = suma_ref[LOSS_ROW:LOSS_ROW + 1, 0:1] * (0.5 / D)
            for r, (rows, lanes) in enumerate(SMALL_SHAPES):
                g = suma_ref[r:r + 1, 0:lanes] if rows == 1 else sumb_ref[:, 0:lanes]
                d, nm, nv = _adamw(sw_refs[r][...], g, sm_refs[r][...], sv_refs[r][...])
                sg_o[r][...], sd_o[r][...], snm_o[r][...], snv_o[r][...] = g, d, nm, nv

    half = lambda rows: pl.BlockSpec((rows // halves, D), lambda i: (i, 0))
    const = lambda shape: pl.BlockSpec(shape, lambda i: (0,) * len(shape))
    win = jax.ShapeDtypeStruct((WIN_BLK, D), F32)
    wout = jax.ShapeDtypeStruct((WOUT_BLK, D), F32)
    smalls = tuple(jax.ShapeDtypeStruct(s, F32) for s in SMALL_SHAPES)
    small_specs = [const(s) for s in SMALL_SHAPES]
    outs = pl.pallas_call(
        body, name="adamw_all", grid=(halves,),
        in_specs=[half(WIN_BLK), half(WOUT_BLK), const((8, D)), const((SWA_H, CH))] + [half(WIN_BLK)] * 3
                 + [half(WOUT_BLK)] * 3 + small_specs * 3,
        out_specs=tuple([half(WIN_BLK)] * 4 + [half(WOUT_BLK)] * 4 + small_specs * 4 + [const((1, 1))]),
        out_shape=(win,) * 4 + (wout,) * 4 + smalls * 4 + (jax.ShapeDtypeStruct((1, 1), F32),),
        compiler_params=_cparams(1, vmem=32 * 1024 * 1024),
    )(g_in_t, g_out, sum_a, sum_b, w_in_t, m_in_t, v_in_t, w_out, m_out, v_out, *small_w, *small_m, *small_v)
    big, rest = outs[:8], outs[8:]
    return big, [rest[i * n_small:(i + 1) * n_small] for i in range(4)], rest[4 * n_small]


def _small_rows(norm_w, ret_norm_w, q_norm_w, k_norm_w, sinks, rel_bias):
    return (norm_w.reshape(1, D), ret_norm_w.reshape(1, 512), q_norm_w.reshape(1, SWA_D), k_norm_w.reshape(1, SWA_D),
            sinks.reshape(1, SWA_H), rel_bias.T)


def _small_leaves(rows):
    return (rows[0].reshape(D), rows[1].reshape(512), rows[2].reshape(SWA_D), rows[3].reshape(SWA_D),
            rows[4].reshape(SWA_H), rows[5].T)


def kernel(x, norm_w, w_in, ret_norm_w, q_norm_w, k_norm_w, sinks, rel_bias, w_out, loss_target, m_norm_w, m_w_in, m_ret_norm_w, m_q_norm_w, m_k_norm_w, m_sinks, m_rel_bias, m_w_out, v_norm_w, v_w_in, v_ret_norm_w, v_q_norm_w, v_k_norm_w, v_sinks, v_rel_bias, v_w_out):
    x2 = x.reshape(T, D)
    target = loss_target.reshape(T, D)
    nw = norm_w.reshape(1, D)
    rnw_col = ret_norm_w.reshape(RET_V, 1)
    qnw_col = q_norm_w.reshape(SWA_D, 1)
    knw_col = k_norm_w.reshape(SWA_D, 1)
    relb = rel_bias.T.reshape(SWA_H * N_BUCKETS)
    ret_tables = _rotary_tables_t() + _retention_tables_t()
    bucket_t = _bucket_table_t()

    proj, wt, rstd, h = _in_proj_gather(x2, nw, w_in.T)
    (ro, mix_r, states, wo), (so, mix_s) = _fused_call("attn_fwd", N_CH, [
        _retention_fwd(proj, rnw_col, ret_tables, w_out),
        _swa_fwd(proj, qnw_col, knw_col, sinks, relb, bucket_t)])
    sse, dy, dmix_r, dmix_s, gwo = _out_proj(mix_r, mix_s, wo.reshape(D, D), x2, target)

    (d_ret, drw_acc), (d_swa, dbias, dsink_acc, dqw_acc, dkw_acc), (g_out,) = _fused_call("attn_bwd", N_CH + 1, [
        _retention_bwd(proj, ro, states, dmix_r, rnw_col, ret_tables),
        _swa_bwd(proj, so, dmix_s, qnw_col, knw_col, sinks, relb, bucket_t),
        _w_out_reduce_scatter(gwo.reshape(N_DEV, WOUT_BLK, D))])
    grad_x, g_in_t, sum_a, sum_b = _in_proj_bwd_rs(d_ret, d_swa, h, wt, x2, rstd, nw, dy, sse,
                                                   (dbias, bucket_t, dsink_acc, dqw_acc, dkw_acc, drw_acc))

    small_w = _small_rows(norm_w, ret_norm_w, q_norm_w, k_norm_w, sinks, rel_bias)
    small_m = _small_rows(m_norm_w, m_ret_norm_w, m_q_norm_w, m_k_norm_w, m_sinks, m_rel_bias)
    small_v = _small_rows(v_norm_w, v_ret_norm_w, v_q_norm_w, v_k_norm_w, v_sinks, v_rel_bias)
    big, small, loss = _adamw_all(g_in_t, g_out, sum_a, sum_b, w_in.T, m_w_in.T, v_w_in.T, w_out, m_w_out, v_w_out,
                                  small_w, small_m, small_v)

    def leaves(i):
        a = _small_leaves(small[i])
        return (a[0], big[i].T, a[1], a[2], a[3], a[4], a[5], big[4 + i])

    return (loss.reshape(()), grad_x.reshape(1, T, D), *leaves(0), *leaves(1), *leaves(2), *leaves(3))
```

```python
from typing import Callable, NamedTuple

import numpy as np
import jax
import jax.numpy as jnp
from jax import lax
from jax.experimental import pallas as pl
from jax.experimental.pallas import tpu as pltpu

F32 = jnp.float32
BF16 = jnp.bfloat16
MESH = pl.DeviceIdType.MESH

T = 2048
D = 1024
D_IN = 2816
N_DEV = 8
WIN_BLK = D_IN // N_DEV
WOUT_BLK = D // N_DEV
CH = 128
N_CH = T // CH
RET_H, RET_DK, RET_DV = 4, 64, 128
RET_QK = RET_H * RET_DK
RET_V = RET_H * RET_DV
SWA_H, SWA_KV, SWA_D, SWA_G = 8, 2, 64, 4
N_BUCKETS = 32
NORM_EPS = 1e-6
GN_EPS = 1e-5
NEG_INF = -1e30
PIECES = (256, 256, 512, 512, 512, 128, 128, 512)
OFFS = tuple(int(v) for v in np.cumsum((0,) + PIECES))
RET_W = OFFS[4]
SWA_W = D_IN - RET_W
SWA_OFFS = tuple(o - RET_W for o in OFFS[4:])
TM = 256

ADAM_LR, ADAM_B1, ADAM_B2, ADAM_EPS, ADAM_WD, ADAM_STEP = 0.001, 0.9, 0.999, 1e-08, 0.01, 10

VMEM_LIMIT = 56 * 1024 * 1024


def _cparams(n_grid=0, vmem=VMEM_LIMIT):
    sem = ("arbitrary",) * n_grid if n_grid else None
    return pltpu.CompilerParams(dimension_semantics=sem, vmem_limit_bytes=vmem)


class _Part(NamedTuple):
    step: Callable
    inputs: list
    in_specs: list
    out_specs: list
    out_shape: list
    scratch_shapes: list


def _fused_call(name, n_steps, parts):
    n_in = [len(p.inputs) for p in parts]
    n_out = [len(p.out_shape) for p in parts]
    n_scr = [len(p.scratch_shapes) for p in parts]

    def body(*refs):
        ins, outs, scr = refs[:sum(n_in)], refs[sum(n_in):sum(n_in) + sum(n_out)], refs[sum(n_in) + sum(n_out):]
        for i, p in enumerate(parts):
            take = lambda seq, counts: seq[sum(counts[:i]):sum(counts[:i + 1])]
            p.step(*take(ins, n_in), *take(outs, n_out), *take(scr, n_scr))

    flat = lambda field: [v for p in parts for v in getattr(p, field)]
    outs = pl.pallas_call(
        body, name=name, grid=(n_steps,), in_specs=flat("in_specs"), out_specs=tuple(flat("out_specs")),
        out_shape=tuple(flat("out_shape")), scratch_shapes=flat("scratch_shapes"), compiler_params=_cparams(1),
    )(*flat("inputs"))
    return [list(outs[sum(n_out[:i]):sum(n_out[:i + 1])]) for i in range(len(parts))]


def _dot(a, b):
    return jnp.dot(a, b, preferred_element_type=F32)


def _dot_nt(a, b):
    return lax.dot_general(a, b, (((1,), (1,)), ((), ())), preferred_element_type=F32)


def _bf(a):
    return a.astype(BF16)


def _sigmoid(x):
    return 1.0 / (1.0 + jnp.exp(-x))


def _adamw(w, g, m, v):
    m = ADAM_B1 * m + (1.0 - ADAM_B1) * g
    v = ADAM_B2 * v + (1.0 - ADAM_B2) * (g * g)
    m_hat = m / (1.0 - ADAM_B1 ** ADAM_STEP)
    v_hat = v / (1.0 - ADAM_B2 ** ADAM_STEP)
    delta = -ADAM_LR * (m_hat / (jnp.sqrt(v_hat) + ADAM_EPS) + ADAM_WD * w)
    return delta, m, v


def _rotary_tables_t():
    half = RET_DK // 2
    inv_freq = np.float32(10000.0) ** (-np.arange(half, dtype=np.float32) / np.float32(half))
    ang = inv_freq[:, None] * np.arange(T, dtype=np.float32)[None, :]
    cos, sin = np.cos(ang).astype(np.float32), np.sin(ang).astype(np.float32)
    cos64 = np.concatenate([cos, cos], axis=0)
    sin64 = np.concatenate([-sin, sin], axis=0)
    return np.tile(cos64, (RET_H, 1)), np.tile(sin64, (RET_H, 1))


def _retention_tables_t():
    gamma = (1.0 - np.exp2(-5.0 - np.arange(RET_H, dtype=np.float32))).astype(np.float32)
    log_g = np.log(gamma).astype(np.float32)
    i = np.arange(CH, dtype=np.float32)
    diff = i[None, :] - i[:, None]
    decay = np.where(diff >= 0, np.exp(log_g[:, None, None] * np.maximum(diff, 0.0)), 0.0).astype(np.float32)
    decay_all = np.concatenate(list(decay), axis=1)
    zeta = np.exp(log_g[:, None] * (CH - 1.0 - i)).astype(np.float32)
    zeta_tab = np.repeat(zeta.T, RET_DK, axis=1)
    xi = np.exp(log_g[:, None] * (i + 1.0)).astype(np.float32)
    xi_tab = np.repeat(xi, RET_DK, axis=0)
    chunk_decay = np.exp(log_g * np.float32(CH)).astype(np.float32)
    row_head = np.arange(RET_V)[:, None] // RET_DV
    col_head = np.arange(RET_QK)[None, :] // RET_DK
    state_mask = (row_head == col_head).astype(np.float32)
    state_decay = (state_mask * chunk_decay[row_head]).astype(np.float32)
    q_mask = (np.arange(RET_QK)[:, None] // RET_DK == np.arange(RET_H * CH)[None, :] // CH).astype(np.float32)
    return (np.ascontiguousarray(decay_all), np.ascontiguousarray(zeta_tab), np.ascontiguousarray(xi_tab),
            state_mask, state_decay, q_mask)


def _bucket_table_t():
    qi = np.arange(CH)[:, None]
    kj = np.arange(2 * CH)[None, :]
    dist = qi + CH - kj
    n = np.maximum(dist, 0)
    max_exact = N_BUCKETS // 2
    nf = np.maximum(n, 1).astype(np.float32)
    large = max_exact + (np.log(nf / np.float32(max_exact)) / np.float32(np.log(CH / max_exact))
                         * np.float32(N_BUCKETS - max_exact)).astype(np.int32)
    large = np.minimum(large, N_BUCKETS - 1)
    bucket = np.where(n < max_exact, n, large)
    return np.ascontiguousarray(np.where((dist >= 0) & (dist < CH), bucket, -1).astype(np.int32).T)


def _mesh_pos():
    return lax.axis_index("x"), lax.axis_index("y"), lax.axis_index("c")


def _blk(px, py, pc):
    return 4 * px + 2 * py + pc


def _in_proj_gather(x, norm_w, w_in_t):
    tchunk = 512

    def body(x_ref, nw_ref, win_ref, proj_ref, wt_out, h_out, wt_ref, h_ref, stage, send_sems, recv_sems, out_sems,
             wb_sems, h_sem):
        x, y, c = _mesh_pos()
        me = _blk(x, y, c)
        sibling = (x, y, 1 - c)
        nbr_a, nbr_b, diag = (x ^ (1 - c), y ^ c), (x ^ c, y ^ (1 - c)), (1 - x, 1 - y)

        def copy(k, b, to):
            return pltpu.make_async_remote_copy(src_ref=wt_ref.at[b], dst_ref=wt_ref.at[b], send_sem=send_sems.at[k],
                                                recv_sem=recv_sems.at[k], device_id=to, device_id_type=MESH)

        wt_ref[me] = _bf(win_ref[...])
        first = [copy(0, me, sibling), copy(1, me, (*nbr_a, c))]
        for cp in first:
            cp.start()
        to_b = copy(2, me, (*nbr_b, c))

        nw = nw_ref[...]
        for i in range(T // TM):
            rows = slice(i * TM, (i + 1) * TM)
            xv = x_ref[rows, :]
            r = lax.rsqrt(jnp.mean(xv * xv, axis=-1, keepdims=True) + NORM_EPS)
            h_ref[rows, :] = _bf(xv * r * nw)
        keep = [pltpu.make_async_copy(h_ref, h_out, h_sem)]
        keep[0].start()

        writes = []

        def project(b):
            k = len(writes)
            if k >= 2:
                writes[k - 2].wait()
            keep.append(pltpu.make_async_copy(wt_ref.at[b], wt_out.at[b], wb_sems.at[k]))
            keep[-1].start()
            w = wt_ref[b]
            for t in range(T // tchunk):
                cols = slice(t * tchunk, (t + 1) * tchunk)
                stage[k % 2, :, cols] = _dot_nt(w, h_ref[cols, :])
            cp = pltpu.make_async_copy(stage.at[k % 2], proj_ref.at[b], out_sems.at[k % 2])
            cp.start()
            writes.append(cp)

        here = (x, y, c)
        project(me)
        copy(0, _blk(x, y, 1 - c), here).wait_recv()
        project(_blk(x, y, 1 - c))
        to_b.start()
        passed = [to_b]

        def landed(k, chip, relay_to=None):
            copy(k, _blk(*chip, c), here).wait_recv()
            out = [copy({1: 4, 2: 5, 3: 6}[k], _blk(*chip, c), sibling)]
            if relay_to is not None:
                out.append(copy(3, _blk(*chip, c), (*relay_to, c)))
            for cp in out:
                cp.start()
            passed.extend(out)
            project(_blk(*chip, c))

        def from_sibling(k, chip):
            copy(k, _blk(*chip, 1 - c), here).wait_recv()
            project(_blk(*chip, 1 - c))

        landed(1, nbr_a, relay_to=nbr_b)
        from_sibling(4, nbr_b)
        landed(2, nbr_b)
        from_sibling(5, nbr_a)
        landed(3, diag)
        from_sibling(6, diag)
        for cp in writes[-2:] + keep:
            cp.wait()
        for cp in first + passed:
            cp.wait_send()

    vm = pl.BlockSpec(memory_space=pltpu.VMEM)
    hbm = pl.BlockSpec(memory_space=pl.ANY)
    dma = pltpu.SemaphoreType.DMA
    proj, wt, h = pl.pallas_call(
        body, name="in_proj_gather",
        out_shape=(jax.ShapeDtypeStruct((N_DEV, WIN_BLK, T), F32), jax.ShapeDtypeStruct((N_DEV, WIN_BLK, D), BF16),
                   jax.ShapeDtypeStruct((T, D), BF16)),
        in_specs=[vm, vm, vm], out_specs=(hbm, hbm, hbm),
        scratch_shapes=[pltpu.VMEM((N_DEV, WIN_BLK, D), BF16), pltpu.VMEM((T, D), BF16), pltpu.VMEM((2, WIN_BLK, T), F32),
                        dma((7,)), dma((7,)), dma((2,)), dma((N_DEV,)), dma],
        compiler_params=_cparams(vmem=48 * 1024 * 1024),
    )(x, norm_w, w_in_t)
    return proj.reshape(D_IN, T), wt.reshape(D_IN, D), h


def _gather_w_out(w_out_ref, own_ref, wo_ref, local_sem, send_sems, recv_sems, phase):
    x, y, c = _mesh_pos()
    me = _blk(x, y, c)
    sibling = (x, y, 1 - c)
    chips = [(1 - x, y), (x, 1 - y), (1 - x, 1 - y)]

    def copy(k, b, to, src=None):
        return pltpu.make_async_remote_copy(src_ref=wo_ref.at[b] if src is None else src, dst_ref=wo_ref.at[b],
                                            send_sem=send_sems.at[k], recv_sem=recv_sems.at[k], device_id=to,
                                            device_id_type=MESH)

    local = pltpu.make_async_copy(own_ref, wo_ref.at[me], local_sem)
    first = [copy(0, me, sibling, own_ref)] + [copy(1 + j, me, (*chip, c), own_ref) for j, chip in enumerate(chips)]
    passed = [copy(4 + j, _blk(*chip, c), sibling) for j, chip in enumerate(chips)]
    if phase == 0:
        own_ref[...] = _bf(w_out_ref[...])
        local.start()
        for cp in first:
            cp.start()
    elif phase == 1:
        for j, chip in enumerate(chips):
            copy(1 + j, _blk(*chip, c), (x, y, c)).wait_recv()
            passed[j].start()
    else:
        local.wait()
        copy(0, _blk(x, y, 1 - c), (x, y, c)).wait_recv()
        for j, chip in enumerate(chips):
            copy(4 + j, _blk(*chip, 1 - c), (x, y, c)).wait_recv()
        for cp in first + passed:
            cp.wait_send()


def _proj_piece(i, block_of):
    rows = PIECES[i]
    assert OFFS[i] % rows == 0
    return pl.BlockSpec((rows, CH), lambda n: (OFFS[i] // rows, block_of(n)))


def _proj_gate_halves(block_of):
    return [pl.BlockSpec((256, CH), lambda n, j=j: (OFFS[7] // 256 + j, block_of(n))) for j in range(2)]


def _swap_halves_t(t):
    half = RET_DK // 2
    parts = []
    for h in range(RET_H):
        parts += [t[h * RET_DK + half:(h + 1) * RET_DK], t[h * RET_DK:h * RET_DK + half]]
    return jnp.concatenate(parts, axis=0)


def _rotate(t, cos, sin):
    return t * cos + _swap_halves_t(t) * sin


def _group_norm_t(o):
    mu = jnp.mean(o, axis=0, keepdims=True)
    var = jnp.mean((o - mu) * (o - mu), axis=0, keepdims=True)
    rstd = lax.rsqrt(var + GN_EPS)
    return (o - mu) * rstd, rstd


def _retention_scores_t(q_t, k_rm_b, q_mask, decay_all):
    q_heads = _bf(jnp.concatenate([q_t] * RET_H, axis=1) * q_mask)
    return _dot(k_rm_b, q_heads) * decay_all


def _retention_fwd(proj, ret_norm_w, tables, w_out_blk):
    cos_t, sin_t, decay_all, zeta_tab, xi_tab, state_mask, state_decay, q_mask = tables

    def body(rq_ref, rk_ref, rv_ref, rg_ref, w_ref, cos_ref, sin_ref, decay_ref, zeta_ref, xi_ref, smask_ref,
             sdecay_ref, qmask_ref, wout_ref, ro_ref, mix_ref, st_ref, wo_ref, state, wo_own, wo_local, wo_send, wo_recv):
        n = pl.program_id(0)

        @pl.when(n == 0)
        def _():
            state[...] = jnp.zeros_like(state)
            _gather_w_out(wout_ref, wo_own, wo_ref, wo_local, wo_send, wo_recv, phase=0)

        @pl.when(n == N_CH // 2)
        def _():
            _gather_w_out(wout_ref, wo_own, wo_ref, wo_local, wo_send, wo_recv, phase=1)

        @pl.when(n == N_CH - 1)
        def _():
            _gather_w_out(wout_ref, wo_own, wo_ref, wo_local, wo_send, wo_recv, phase=2)

        cos, sin = cos_ref[...], sin_ref[...]
        q_t = _rotate(rq_ref[...], cos, sin)
        k_t = _rotate(rk_ref[...], cos, sin) * (RET_DK ** -0.5)
        k_rm = k_t.T
        v_b = _bf(rv_ref[...])
        m_b = _bf(state[...])
        st_ref[0] = m_b
        scores_b = _bf(_retention_scores_t(q_t, _bf(k_rm), qmask_ref[...], decay_ref[...]))
        cross = _dot(m_b, _bf(q_t * xi_ref[...]))
        state[...] = state[...] * sdecay_ref[...] + _dot(v_b, _bf(k_rm * zeta_ref[...])) * smask_ref[...]
        heads = range(RET_H)
        rows = [slice(h * RET_DV, (h + 1) * RET_DV) for h in heads]
        o = [_dot(v_b[rows[h]], scores_b[:, h * CH:(h + 1) * CH]) + cross[rows[h]] for h in heads]
        for h in heads:
            rn, _ = _group_norm_t(o[h])
            g = rg_ref[rows[h], :]
            ro_ref[rows[h], :] = o[h]
            mix_ref[rows[h], :] = rn * w_ref[rows[h], :] * (g * _sigmoid(g))

    col = lambda w: pl.BlockSpec((w, CH), lambda n: (0, n))
    const = lambda shape: pl.BlockSpec(shape, lambda n: (0,) * len(shape))
    cur = lambda n: n
    return _Part(
        body,
        inputs=[proj, proj, proj, proj, ret_norm_w, cos_t, sin_t, decay_all, zeta_tab, xi_tab, state_mask, state_decay,
                q_mask, w_out_blk],
        in_specs=[_proj_piece(i, cur) for i in range(4)] + [
            const((RET_V, 1)), col(RET_QK), col(RET_QK),
            const(decay_all.shape), const(zeta_tab.shape), const(xi_tab.shape), const(state_mask.shape),
            const(state_decay.shape), const(q_mask.shape), const((WOUT_BLK, D))],
        out_specs=[col(RET_V), col(RET_V), pl.BlockSpec((1, RET_V, RET_QK), lambda n: (n, 0, 0)),
                   pl.BlockSpec(memory_space=pl.ANY)],
        out_shape=[jax.ShapeDtypeStruct((RET_V, T), F32), jax.ShapeDtypeStruct((RET_V, T), F32),
                   jax.ShapeDtypeStruct((N_CH, RET_V, RET_QK), BF16), jax.ShapeDtypeStruct((N_DEV, WOUT_BLK, D), BF16)],
        scratch_shapes=[pltpu.VMEM((RET_V, RET_QK), F32), pltpu.VMEM((WOUT_BLK, D), BF16), pltpu.SemaphoreType.DMA,
                        pltpu.SemaphoreType.DMA((N_DEV - 1,)), pltpu.SemaphoreType.DMA((N_DEV - 1,))])


def _retention_bwd(proj, ro, states, dmix, ret_norm_w, tables):
    cos_t, sin_t, decay_all, zeta_tab, xi_tab, state_mask, state_decay, q_mask = tables

    def body(*refs):
        @pl.when(pl.program_id(0) < N_CH)
        def _():
            chunk(*refs)

    def chunk(rq_ref, rk_ref, rv_ref, rg_ref, ro_ref, st_ref, dm_ref, w_ref, cos_ref, sin_ref, decay_ref, zeta_ref,
              xi_ref, smask_ref, sdecay_ref, qmask_ref, d_ref, dw_ref, gstate):
        i = pl.program_id(0)

        @pl.when(i == 0)
        def _():
            gstate[...] = jnp.zeros_like(gstate)
            dw_ref[...] = jnp.zeros_like(dw_ref)

        cos, sin = cos_ref[...], sin_ref[...]
        q_t = _rotate(rq_ref[...], cos, sin)
        k_t = _rotate(rk_ref[...], cos, sin) * (RET_DK ** -0.5)
        q_b, k_b = _bf(q_t), _bf(k_t)
        k_rm = k_t.T
        kz_b = _bf(k_rm * zeta_ref[...])
        qx_b = _bf(q_t * xi_ref[...])
        v_t = rv_ref[...]
        v_b = _bf(v_t)
        v_rm_b = _bf(v_t.T)
        decay = decay_ref[...]
        scores_b = _bf(_retention_scores_t(q_t, _bf(k_rm), qmask_ref[...], decay))
        heads = range(RET_H)
        rows = [slice(h * RET_DV, (h + 1) * RET_DV) for h in heads]
        qk_rows = [slice(h * RET_DK, (h + 1) * RET_DK) for h in heads]
        lanes = [slice(h * CH, (h + 1) * CH) for h in heads]
        do = []
        for h in heads:
            g, w, dm = rg_ref[rows[h], :], w_ref[rows[h], :], dm_ref[rows[h], :]
            rn, rstd = _group_norm_t(ro_ref[rows[h], :])
            sig = _sigmoid(g)
            silu = g * sig
            d_ref[OFFS[3] + h * RET_DV:OFFS[3] + (h + 1) * RET_DV, :] = _bf(dm * rn * w * (sig * (1.0 + g * (1.0 - sig))))
            dw_ref[rows[h], :] += dm * silu * rn
            drn = dm * silu * w
            do.append(rstd * (drn - jnp.mean(drn, axis=0, keepdims=True)
                              - rn * jnp.mean(drn * rn, axis=0, keepdims=True)))
        do_b = _bf(jnp.concatenate(do, axis=0))
        m_b = st_ref[0]
        g_all = gstate[...]
        g_b = _bf(g_all)
        dscores_b = [_bf(_dot(v_rm_b[:, rows[h]], do_b[rows[h]]) * decay[:, lanes[h]]) for h in heads]
        dq_cross = lax.dot_general(m_b, do_b, (((0,), (0,)), ((), ())), preferred_element_type=F32)
        dkz = _dot(v_rm_b, g_b)
        dv_cross = _dot_nt(g_b, kz_b)
        gstate[...] = g_all * sdecay_ref[...] + _dot_nt(do_b, qx_b) * smask_ref[...]
        dq = jnp.concatenate([_dot(k_b[qk_rows[h]], dscores_b[h]) for h in heads], axis=0) + dq_cross * xi_ref[...]
        dk = (jnp.concatenate([_dot_nt(q_b[qk_rows[h]], dscores_b[h]) for h in heads], axis=0)
              + (dkz * zeta_ref[...]).T) * (RET_DK ** -0.5)
        for h in heads:
            d_ref[OFFS[2] + h * RET_DV:OFFS[2] + (h + 1) * RET_DV, :] = _bf(
                _dot_nt(do_b[rows[h]], scores_b[:, lanes[h]]) + dv_cross[rows[h]])
        d_ref[OFFS[0]:OFFS[1], :] = _bf(dq * cos + _swap_halves_t(dq * sin))
        d_ref[OFFS[1]:OFFS[2], :] = _bf(dk * cos + _swap_halves_t(dk * sin))

    chunk_of = lambda i: N_CH - 1 - jnp.minimum(i, N_CH - 1)
    col = lambda w: pl.BlockSpec((w, CH), lambda i: (0, chunk_of(i)))
    const = lambda shape: pl.BlockSpec(shape, lambda i: (0,) * len(shape))
    return _Part(
        body,
        inputs=[proj, proj, proj, proj, ro, states, dmix, ret_norm_w, cos_t, sin_t, decay_all, zeta_tab, xi_tab,
                state_mask, state_decay, q_mask],
        in_specs=[_proj_piece(j, chunk_of) for j in range(4)] + [
                  col(RET_V),
                  pl.BlockSpec((1, RET_V, RET_QK), lambda i: (chunk_of(i), 0, 0)), col(RET_V),
                  const((RET_V, 1)), col(RET_QK), col(RET_QK),
                  const(decay_all.shape), const(zeta_tab.shape), const(xi_tab.shape), const(state_mask.shape),
                  const(state_decay.shape), const(q_mask.shape)],
        out_specs=[col(RET_W), const((RET_V, CH))],
        out_shape=[jax.ShapeDtypeStruct((RET_W, T), BF16), jax.ShapeDtypeStruct((RET_V, CH), F32)],
        scratch_shapes=[pltpu.VMEM((RET_V, RET_QK), F32)])


HQ_LANES = SWA_H * CH


def _head_lanes(hq):
    return slice(hq * CH, (hq + 1) * CH)


def _build_bias_t(bucket_ref, relb_ref, sinks_ref, bias_ref, sink_row):
    bt = bucket_ref[...]
    first = lax.broadcasted_iota(jnp.int32, bt.shape, 0) < CH
    for hq in range(SWA_H):
        b = jnp.full(bt.shape, NEG_INF, F32)
        for bk in range(N_BUCKETS):
            b = jnp.where(bt == bk, relb_ref[hq * N_BUCKETS + bk], b)
        bias_ref[0, :, _head_lanes(hq)] = b
        bias_ref[1, :, _head_lanes(hq)] = jnp.where(first, NEG_INF, b)
        sink_row[:, _head_lanes(hq)] = jnp.full((1, CH), sinks_ref[hq], F32)


def _rms_t(t, w_col):
    r = lax.rsqrt(jnp.mean(t * t, axis=0, keepdims=True) + NORM_EPS)
    return t * r * w_col, r


def _rms_t_bwd(dn, t, r, w_col):
    u = dn * w_col
    return r * u - t * (r * r * r) * jnp.mean(u * t, axis=0, keepdims=True), dn * t * r


def _norm_kv(k_t, kw):
    return jnp.concatenate([_rms_t(k_t[g * SWA_D:(g + 1) * SWA_D], kw)[0] for g in range(SWA_KV)], axis=0)


def _kv_slot(a, kv):
    z = jnp.zeros_like(a)
    return jnp.concatenate([a, z] if kv == 0 else [z, a], axis=0)


def _softmax_t(s, sink):
    m = jnp.maximum(jnp.max(s, axis=0, keepdims=True), sink)
    p = jnp.exp(s - m)
    e_sink = jnp.exp(sink - m)
    inv = 1.0 / (jnp.sum(p, axis=0, keepdims=True) + e_sink)
    return p * inv, e_sink * inv


def _gate_rows(sg_halves, hq):
    per_half = SWA_H // 2
    return sg_halves[hq // per_half][(hq % per_half) * SWA_D:(hq % per_half + 1) * SWA_D, :]


def _swa_fwd(proj, q_norm_w, k_norm_w, sinks, rel_bias_flat, bucket_t):
    def body(sq_ref, skp_ref, skc_ref, svp_ref, svc_ref, sg_lo, sg_hi, qw_ref, kw_ref, sinks_ref, relb_ref, bucket_ref,
             so_ref, mix_ref, bias_ref, sink_row):
        n = pl.program_id(0)

        @pl.when(n == 0)
        def _():
            _build_bias_t(bucket_ref, relb_ref, sinks_ref, bias_ref, sink_row)

        var = (n == 0).astype(jnp.int32)
        qw, kw = qw_ref[...], kw_ref[...]
        kn_band = jnp.concatenate([_norm_kv(skp_ref[...], kw), _norm_kv(skc_ref[...], kw)], axis=1)
        kb_rm = _bf(kn_band.T)
        vband = _bf(jnp.concatenate([svp_ref[...], svc_ref[...]], axis=1))
        q_all = jnp.concatenate(
            [_bf(_kv_slot(_rms_t(sq_ref[hq * SWA_D:(hq + 1) * SWA_D, :], qw)[0] * (SWA_D ** -0.5), hq // SWA_G))
             for hq in range(SWA_H)], axis=1)
        probs, _ = _softmax_t(_dot(kb_rm, q_all) + bias_ref[var], sink_row[...])
        probs_b = _bf(probs)
        for kv in range(SWA_KV):
            o = _dot(vband[kv * SWA_D:(kv + 1) * SWA_D], probs_b[:, kv * SWA_G * CH:(kv + 1) * SWA_G * CH])
            for j in range(SWA_G):
                rows = slice((kv * SWA_G + j) * SWA_D, (kv * SWA_G + j + 1) * SWA_D)
                oh = o[:, j * CH:(j + 1) * CH]
                g = _gate_rows((sg_lo, sg_hi), kv * SWA_G + j)
                so_ref[rows, :] = oh
                mix_ref[rows, :] = oh * (g * _sigmoid(g))

    col = lambda w: pl.BlockSpec((w, CH), lambda n: (0, n))
    const = lambda shape: pl.BlockSpec(shape, lambda n: (0,) * len(shape))
    smem = pl.BlockSpec(memory_space=pltpu.SMEM)
    cur = lambda n: n
    prev = lambda n: jnp.maximum(n - 1, 0)
    return _Part(
        body,
        inputs=[proj, proj, proj, proj, proj, proj, proj, q_norm_w, k_norm_w, sinks, rel_bias_flat, bucket_t],
        in_specs=[_proj_piece(4, cur), _proj_piece(5, prev), _proj_piece(5, cur), _proj_piece(6, prev),
                  _proj_piece(6, cur)] + _proj_gate_halves(cur) + [
                  const((SWA_D, 1)), const((SWA_D, 1)), smem, smem, const((2 * CH, CH))],
        out_specs=[col(512), col(512)],
        out_shape=[jax.ShapeDtypeStruct((512, T), F32), jax.ShapeDtypeStruct((512, T), F32)],
        scratch_shapes=[pltpu.VMEM((2, 2 * CH, HQ_LANES), F32), pltpu.VMEM((1, HQ_LANES), F32)])


def _swa_bwd(proj, so, dmix, q_norm_w, k_norm_w, sinks, rel_bias_flat, bucket_t):
    def body(sq_ref, skp_ref, skc_ref, svp_ref, svc_ref, sg_lo, sg_hi, so_ref, dm_ref, qw_ref, kw_ref, sinks_ref,
             relb_ref, bucket_ref, d_ref, dbias_ref, dsink_ref, dqw_ref, dkw_ref,
             bias_ref, sink_row, band_dk, band_dv, carry_dk, carry_dv, hold_dq, hold_dg):
        n = pl.program_id(0)
        slot = n % 2

        @pl.when(n == 0)
        def _():
            _build_bias_t(bucket_ref, relb_ref, sinks_ref, bias_ref, sink_row)
            for ref in (dbias_ref, dsink_ref, dqw_ref, dkw_ref, carry_dk, carry_dv):
                ref[...] = jnp.zeros_like(ref)

        qw, kw = qw_ref[...], kw_ref[...]

        @pl.when(n < N_CH)
        def _():
            var = (n == 0).astype(jnp.int32)
            kn_band = jnp.concatenate([_norm_kv(skp_ref[...], kw), _norm_kv(skc_ref[...], kw)], axis=1)
            kb_rm = _bf(kn_band.T)
            kn_band_s = _bf(kn_band * (SWA_D ** -0.5))
            vband_f = jnp.concatenate([svp_ref[...], svc_ref[...]], axis=1)
            vb_rm = _bf(vband_f.T)
            q_raw, q_rstd, qs_b, do_b = [], [], [], []
            for hq in range(SWA_H):
                rows = slice(hq * SWA_D, (hq + 1) * SWA_D)
                q_t = sq_ref[rows, :]
                qn, rq = _rms_t(q_t, qw)
                g = _gate_rows((sg_lo, sg_hi), hq)
                sig = _sigmoid(g)
                dm = dm_ref[rows, :]
                hold_dg[slot, rows, :] = _bf(dm * so_ref[rows, :] * (sig * (1.0 + g * (1.0 - sig))))
                q_raw.append(q_t)
                q_rstd.append(rq)
                qs_b.append(_bf(qn * (SWA_D ** -0.5)))
                do_b.append(_bf(dm * (g * sig)))
            q_all = jnp.concatenate([_kv_slot(qs_b[hq], hq // SWA_G) for hq in range(SWA_H)], axis=1)
            do_all = jnp.concatenate([_kv_slot(do_b[hq], hq // SWA_G) for hq in range(SWA_H)], axis=1)
            probs, p_sink = _softmax_t(_dot(kb_rm, q_all) + bias_ref[var], sink_row[...])
            dprobs = _dot(vb_rm, do_all)
            t = jnp.sum(probs * dprobs, axis=0, keepdims=True)
            dlog = probs * (dprobs - t)
            dsink_ref[...] += -(p_sink * t)
            dbias_ref[...] += dlog
            dlog_b, probs_b = _bf(dlog), _bf(probs)
            dkn, dvv = [], []
            for kv in range(SWA_KV):
                heads = range(kv * SWA_G, (kv + 1) * SWA_G)
                lanes = slice(kv * SWA_G * CH, (kv + 1) * SWA_G * CH)
                dvv.append(_dot_nt(jnp.concatenate([do_b[hq] for hq in heads], axis=1), probs_b[:, lanes]))
                dkn.append(_dot_nt(jnp.concatenate([qs_b[hq] for hq in heads], axis=1), dlog_b[:, lanes]))
                dqn = _dot(kn_band_s[kv * SWA_D:(kv + 1) * SWA_D], dlog_b[:, lanes])
                for j, hq in enumerate(heads):
                    dq_t, dqw_terms = _rms_t_bwd(dqn[:, j * CH:(j + 1) * CH], q_raw[hq], q_rstd[hq], qw)
                    hold_dq[slot, hq * SWA_D:(hq + 1) * SWA_D, :] = _bf(dq_t)
                    dqw_ref[...] += dqw_terms
            band_dk[...] = jnp.concatenate(dkn, axis=0)
            band_dv[...] = jnp.concatenate(dvv, axis=0)

        @pl.when(n == N_CH)
        def _():
            band_dk[...] = jnp.zeros_like(band_dk)
            band_dv[...] = jnp.zeros_like(band_dv)

        @pl.when(n >= 1)
        def _():
            dkn_prev = carry_dk[...] + band_dk[:, 0:CH]
            k_t = skp_ref[...]
            for kv in range(SWA_KV):
                rows = slice(kv * SWA_D, (kv + 1) * SWA_D)
                _, rk = _rms_t(k_t[rows], kw)
                dk_t, dkw_terms = _rms_t_bwd(dkn_prev[rows], k_t[rows], rk, kw)
                d_ref[SWA_OFFS[1] + kv * SWA_D:SWA_OFFS[1] + (kv + 1) * SWA_D, :] = _bf(dk_t)
                dkw_ref[...] += dkw_terms
            d_ref[SWA_OFFS[2]:SWA_OFFS[3], :] = _bf(carry_dv[...] + band_dv[:, 0:CH])
            d_ref[SWA_OFFS[0]:SWA_OFFS[1], :] = hold_dq[1 - slot]
            d_ref[SWA_OFFS[3]:SWA_OFFS[4], :] = hold_dg[1 - slot]

        carry_dk[...] = band_dk[:, CH:2 * CH]
        carry_dv[...] = band_dv[:, CH:2 * CH]

    cur_block = lambda n: jnp.minimum(n, N_CH - 1)
    prev_block = lambda n: jnp.maximum(n - 1, 0)
    col = lambda w: pl.BlockSpec((w, CH), lambda n: (0, cur_block(n)))
    prev = lambda w: pl.BlockSpec((w, CH), lambda n: (0, prev_block(n)))
    const = lambda shape: pl.BlockSpec(shape, lambda n: (0,) * len(shape))
    smem = pl.BlockSpec(memory_space=pltpu.SMEM)
    return _Part(
        body,
        inputs=[proj, proj, proj, proj, proj, proj, proj, so, dmix, q_norm_w, k_norm_w, sinks, rel_bias_flat, bucket_t],
        in_specs=[_proj_piece(4, cur_block), _proj_piece(5, prev_block), _proj_piece(5, cur_block),
                  _proj_piece(6, prev_block), _proj_piece(6, cur_block)] + _proj_gate_halves(cur_block) + [
                  col(512), col(512), const((SWA_D, 1)), const((SWA_D, 1)), smem, smem, const((2 * CH, CH))],
        out_specs=[prev(SWA_W), const((2 * CH, HQ_LANES)), const((1, HQ_LANES)),
                   const((SWA_D, CH)), const((SWA_D, CH))],
        out_shape=[jax.ShapeDtypeStruct((SWA_W, T), BF16),
                   jax.ShapeDtypeStruct((2 * CH, HQ_LANES), F32), jax.ShapeDtypeStruct((1, HQ_LANES), F32),
                   jax.ShapeDtypeStruct((SWA_D, CH), F32), jax.ShapeDtypeStruct((SWA_D, CH), F32)],
        scratch_shapes=[pltpu.VMEM((2, 2 * CH, HQ_LANES), F32), pltpu.VMEM((1, HQ_LANES), F32),
                        pltpu.VMEM((128, 2 * CH), F32), pltpu.VMEM((128, 2 * CH), F32),
                        pltpu.VMEM((128, CH), F32), pltpu.VMEM((128, CH), F32),
                        pltpu.VMEM((2, 512, CH), BF16), pltpu.VMEM((2, 512, CH), BF16)])


SMALL_GRAD_SHAPES = ((1, RET_V), (1, CH), (1, CH), (1, CH), (SWA_H, N_BUCKETS))


def _finish_small_grads(dbias_ref, bucket_ref, dsink_ref, dqw_ref, dkw_ref, drw_ref, rw_o, qw_o, kw_o, sink_o, relb_o):
    bt = bucket_ref[...]
    row = lax.broadcasted_iota(jnp.int32, (SWA_H, N_BUCKETS), 0)
    col = lax.broadcasted_iota(jnp.int32, (SWA_H, N_BUCKETS), 1)
    lane = lax.broadcasted_iota(jnp.int32, (1, CH), 1)
    acc = jnp.zeros((SWA_H, N_BUCKETS), F32)
    sink = jnp.zeros((1, CH), F32)
    for hq in range(SWA_H):
        d = dbias_ref[:, _head_lanes(hq)]
        for bk in range(N_BUCKETS):
            s = jnp.sum(jnp.sum(jnp.where(bt == bk, d, 0.0), axis=0, keepdims=True), axis=1, keepdims=True)
            acc = acc + jnp.where((row == hq) & (col == bk), s, 0.0)
        sink = sink + jnp.where(lane == hq, jnp.sum(dsink_ref[:, _head_lanes(hq)], axis=1, keepdims=True), 0.0)
    relb_o[...] = acc
    sink_o[...] = sink
    for src, dst in ((dqw_ref, qw_o), (dkw_ref, kw_o)):
        padded = jnp.concatenate([src[...], jnp.zeros((CH - SWA_D, CH), F32)], axis=0)
        dst[...] = jnp.sum(padded.T, axis=0, keepdims=True)
    rw_o[...] = jnp.sum(drw_ref[...].T, axis=0, keepdims=True)


def _out_proj(mix_r, mix_s, wo, x, target):
    def body(mr_ref, ms_ref, w_ref, x_ref, t_ref, loss_ref, dy_ref, dmr_ref, dms_ref, gw_ref):
        i = pl.program_id(0)

        @pl.when(i == 0)
        def _():
            loss_ref[...] = jnp.zeros_like(loss_ref)
            gw_ref[...] = jnp.zeros_like(gw_ref)

        mixed = jnp.concatenate([mr_ref[...], ms_ref[...]], axis=0)
        w = w_ref[...]
        err = x_ref[...] + _dot(_bf(mixed.T), w) - t_ref[...]
        loss_ref[...] += jnp.sum(jnp.sum(err * err, axis=1, keepdims=True), axis=0, keepdims=True)
        dy = err * (1.0 / D)
        dy_ref[...] = dy
        dy_b = _bf(dy)
        dmix = _dot_nt(w, dy_b)
        dmr_ref[...] = dmix[0:512]
        dms_ref[...] = dmix[512:D]
        gw_ref[...] += _dot(_bf(mixed), dy_b)

    row = lambda w: pl.BlockSpec((TM, w), lambda i: (i, 0))
    col = lambda w: pl.BlockSpec((w, TM), lambda i: (0, i))
    const = lambda shape: pl.BlockSpec(shape, lambda i: (0,) * len(shape))
    return pl.pallas_call(
        body, name="out_proj", grid=(T // TM,),
        in_specs=[col(512), col(512), const((D, D)), row(D), row(D)],
        out_specs=(const((1, 1)), row(D), col(512), col(512), const((D, D))),
        out_shape=(jax.ShapeDtypeStruct((1, 1), F32), jax.ShapeDtypeStruct((T, D), F32),
                   jax.ShapeDtypeStruct((512, T), F32), jax.ShapeDtypeStruct((512, T), F32),
                   jax.ShapeDtypeStruct((D, D), F32)),
        compiler_params=_cparams(1),
    )(mix_r, mix_s, wo, x, target)


LOSS_ROW = 5
ALL_CHIPS = ((0, 0), (0, 1), (1, 0), (1, 1))


def _w_out_reduce_scatter(gwo):
    def body(gwo_ref, gout_o, own, rcv, snd, got, tot, local_sems, a_send, a_recv, b_send, b_recv):
        k = pl.program_id(0)
        x_, y_, c_ = _mesh_pos()
        sibling = (x_, y_, 1 - c_)
        rel_chips = [(1 - x_, y_), (x_, 1 - y_), (1 - x_, 1 - y_)]

        def to_sibling(j):
            px, py = ALL_CHIPS[j]
            return pltpu.make_async_remote_copy(src_ref=gwo_ref.at[_blk(px, py, 1 - c_)], dst_ref=rcv.at[j],
                                                send_sem=a_send.at[j], recv_sem=a_recv.at[j], device_id=sibling,
                                                device_id_type=MESH)

        def local(j):
            px, py = ALL_CHIPS[j]
            return pltpu.make_async_copy(gwo_ref.at[_blk(px, py, c_)], own.at[j], local_sems.at[j])

        def to_chip(q):
            return pltpu.make_async_remote_copy(src_ref=snd.at[q], dst_ref=got.at[q], send_sem=b_send.at[q],
                                                recv_sem=b_recv.at[q], device_id=(*rel_chips[q], c_), device_id_type=MESH)

        @pl.when(k == 0)
        def _():
            for j in range(len(ALL_CHIPS)):
                to_sibling(j).start()
                local(j).start()

        @pl.when(k == 2)
        def _():
            for j in range(len(ALL_CHIPS)):
                local(j).wait()
                to_sibling(j).wait_recv()
            for q in range(3):
                j = 2 * rel_chips[q][0] + rel_chips[q][1]
                snd[q] = _bf(own[j] + rcv[j])
                to_chip(q).start()
            jm = 2 * x_ + y_
            tot[...] = own[jm] + rcv[jm]

        @pl.when(k == N_CH)
        def _():
            g = tot[...]
            for q in range(3):
                to_chip(q).wait_recv()
                g = g + got[q].astype(F32)
            gout_o[...] = g
            for j in range(len(ALL_CHIPS)):
                to_sibling(j).wait_send()
            for q in range(3):
                to_chip(q).wait_send()

    dma = pltpu.SemaphoreType.DMA
    return _Part(
        body, inputs=[gwo], in_specs=[pl.BlockSpec(memory_space=pl.ANY)],
        out_specs=[pl.BlockSpec((WOUT_BLK, D), lambda k: (0, 0))], out_shape=[jax.ShapeDtypeStruct((WOUT_BLK, D), F32)],
        scratch_shapes=[pltpu.VMEM((4, WOUT_BLK, D), F32), pltpu.VMEM((4, WOUT_BLK, D), F32),
                        pltpu.VMEM((3, WOUT_BLK, D), BF16), pltpu.VMEM((3, WOUT_BLK, D), BF16),
                        pltpu.VMEM((WOUT_BLK, D), F32), dma((4,)), dma((4,)), dma((4,)), dma((3,)), dma((3,))])


def _in_proj_bwd_rs(d_ret, d_swa, h, wt, x, norm_w, dy, sse, small_acc):
    n_blocks, n_tiles = N_DEV, T // TM
    last = n_blocks + n_tiles - 1

    def body(da_any, db_any, da_ref, db_ref, h_ref, w_ref, x_ref, nw_ref, dy_ref, sse_ref,
             dbias_ref, bucket_ref, dsink_ref, dqw_ref, dkw_ref, drw_ref, gx_ref, gin_o, suma_o, sumb_o,
             stage, d2d_src, mine, rcv, snd, got, tot, tab_a, tab_b, gnw_acc, rw_ref, qw_ref, kw_ref, sk_ref, rb_ref,
             dp_sems, d2d_send, d2d_recv, ici_send, ici_recv, s_send, s_recv):
        k = pl.program_id(0)
        x_, y_, c_ = _mesh_pos()
        me = _blk(x_, y_, c_)
        sibling = (x_, y_, 1 - c_)
        rel_chips = [(1 - x_, y_), (x_, 1 - y_), (1 - x_, 1 - y_), (x_, y_)]

        def block_of_step(s):
            return _blk(*rel_chips[s // 2], 1 - c_ if s % 2 == 0 else c_)

        def fetch(s, wait):
            slot, b = s % 2, block_of_step(s)
            split = RET_W - 4 * WIN_BLK

            def run(src, dst, sem):
                cp = pltpu.make_async_copy(src, dst, sem)
                cp.wait() if wait else cp.start()

            @pl.when(b < 4)
            def _():
                run(da_any.at[pl.ds(pl.multiple_of(b * WIN_BLK, 16), WIN_BLK), :], stage.at[slot], dp_sems.at[slot, 0])

            @pl.when(b == 4)
            def _():
                run(da_any.at[pl.ds(4 * WIN_BLK, split), :], stage.at[slot, pl.ds(0, split), :], dp_sems.at[slot, 0])
                run(db_any.at[pl.ds(0, WIN_BLK - split), :], stage.at[slot, pl.ds(split, WIN_BLK - split), :],
                    dp_sems.at[slot, 1])

            @pl.when(b > 4)
            def _():
                run(db_any.at[pl.ds(pl.multiple_of(b * WIN_BLK - RET_W, 16), WIN_BLK), :], stage.at[slot],
                    dp_sems.at[slot, 0])

        def d2d_copy(r):
            return pltpu.make_async_remote_copy(src_ref=d2d_src, dst_ref=rcv.at[r], send_sem=d2d_send.at[r],
                                                recv_sem=d2d_recv.at[r], device_id=sibling, device_id_type=MESH)

        def ici_copy(r):
            return pltpu.make_async_remote_copy(src_ref=snd.at[r], dst_ref=got.at[r], send_sem=ici_send.at[r],
                                                recv_sem=ici_recv.at[r], device_id=(*rel_chips[r], c_),
                                                device_id_type=MESH)

        def table_copies():
            return [pltpu.make_async_remote_copy(src_ref=tab.at[me], dst_ref=tab.at[me], send_sem=s_send.at[a, p - 1],
                                                 recv_sem=s_recv.at[a, p - 1],
                                                 device_id=(x_ ^ (p >> 2), y_ ^ ((p >> 1) & 1), c_ ^ (p & 1)),
                                                 device_id_type=MESH)
                    for p in range(1, N_DEV) for a, tab in enumerate((tab_a, tab_b))]

        def chip_sum(r):
            d2d_copy(r).wait_recv()
            total = mine[...] + rcv[r]
            if r < 3:
                snd[r] = _bf(total)
                ici_copy(r).start()
            else:
                tot[...] = total

        for s in range(n_blocks):
            @pl.when(k == s)
            def _(s=s):
                r = s // 2
                if s == 0:
                    gnw_acc[...] = jnp.zeros_like(gnw_acc)
                    fetch(0, wait=False)
                fetch(s, wait=True)
                if s + 1 < n_blocks:
                    fetch(s + 1, wait=False)
                gw = _dot(stage[s % 2], h_ref[...])
                if s % 2 == 0:
                    if s >= 2:
                        chip_sum(r - 1)
                        d2d_copy(r - 1).wait_send()
                    d2d_src[...] = gw
                    d2d_copy(r).start()
                else:
                    mine[...] = gw

        @pl.when(k == n_blocks)
        def _():
            chip_sum(3)

        @pl.when(k >= n_blocks)
        def _():
            dp = jnp.concatenate([da_ref[...], db_ref[...]], axis=0)
            xv, nw = x_ref[...], nw_ref[...]
            r = lax.rsqrt(jnp.mean(xv * xv, axis=-1, keepdims=True) + NORM_EPS)
            dh = lax.dot_general(dp, w_ref[...], (((0,), (0,)), ((), ())), preferred_element_type=F32)
            u = dh * nw
            gx_ref[...] = dy_ref[...] + r * u - xv * (r * r * r) * jnp.mean(u * xv, axis=-1, keepdims=True)
            gnw_acc[...] += jnp.sum(dh * (xv * r), axis=0, keepdims=True)

        @pl.when(k == n_blocks)
        def _():
            _finish_small_grads(dbias_ref, bucket_ref, dsink_ref, dqw_ref, dkw_ref, drw_ref,
                                rw_ref, qw_ref, kw_ref, sk_ref, rb_ref)

        @pl.when(k == last)
        def _():
            mine_a, mine_b = tab_a.at[me], tab_b.at[me]
            mine_a[...] = jnp.zeros_like(mine_a)
            mine_b[...] = jnp.zeros_like(mine_b)
            for row, ref in enumerate((gnw_acc, rw_ref, qw_ref, kw_ref, sk_ref)):
                mine_a[row:row + 1, 0:ref.shape[1]] = ref[...]
            mine_a[LOSS_ROW:LOSS_ROW + 1, 0:1] = sse_ref[...]
            mine_b[:, 0:N_BUCKETS] = rb_ref[...]
            tables = table_copies()
            for cp in tables:
                cp.start()
            g_in = tot[...]
            for q in range(3):
                ici_copy(q).wait_recv()
                g_in = g_in + got[q].astype(F32)
            gin_o[...] = g_in
            for cp in tables:
                cp.wait_recv()
            sum_a, sum_b = tab_a[0], tab_b[0]
            for b in range(1, N_DEV):
                sum_a = sum_a + tab_a[b]
                sum_b = sum_b + tab_b[b]
            suma_o[...] = sum_a
            sumb_o[...] = sum_b
            d2d_copy(3).wait_send()
            for q in range(3):
                ici_copy(q).wait_send()
            for cp in tables:
                cp.wait_send()

    tile = lambda k: jnp.maximum(k - n_blocks, 0)
    row = lambda w: pl.BlockSpec((TM, w), lambda k: (tile(k), 0))
    col = lambda w: pl.BlockSpec((w, TM), lambda k: (0, tile(k)))
    const = lambda shape, **kw: pl.BlockSpec(shape, lambda k: (0,) * len(shape), **kw)
    once = dict(pipeline_mode=pl.Buffered(1))
    hbm = pl.BlockSpec(memory_space=pl.ANY)
    dma = pltpu.SemaphoreType.DMA
    return pl.pallas_call(
        body, name="in_proj_bwd_rs", grid=(n_blocks + n_tiles,),
        in_specs=[hbm, hbm, col(RET_W), col(SWA_W), const((T, D), **once), const((D_IN, D), **once),
                  row(D), const((1, D)), row(D), const((1, 1))]
                 + [const(a.shape) for a in small_acc],
        out_specs=(row(D), const((WIN_BLK, D)), const((8, D)), const((SWA_H, CH))),
        out_shape=(jax.ShapeDtypeStruct((T, D), F32), jax.ShapeDtypeStruct((WIN_BLK, D), F32),
                   jax.ShapeDtypeStruct((8, D), F32), jax.ShapeDtypeStruct((SWA_H, CH), F32)),
        scratch_shapes=[
            pltpu.VMEM((2, WIN_BLK, T), BF16), pltpu.VMEM((WIN_BLK, D), F32),
            pltpu.VMEM((WIN_BLK, D), F32),
            pltpu.VMEM((4, WIN_BLK, D), F32), pltpu.VMEM((3, WIN_BLK, D), BF16),
            pltpu.VMEM((3, WIN_BLK, D), BF16), pltpu.VMEM((WIN_BLK, D), F32),
            pltpu.VMEM((N_DEV, 8, D), F32), pltpu.VMEM((N_DEV, SWA_H, CH), F32),
            pltpu.VMEM((1, D), F32),
        ] + [pltpu.VMEM(s, F32) for s in SMALL_GRAD_SHAPES] + [
            dma((2, 2)), dma((4,)), dma((4,)), dma((3,)), dma((3,)), dma((2, 7)), dma((2, 7)),
        ],
        compiler_params=_cparams(1),
    )(d_ret, d_swa, d_ret, d_swa, h, wt, x, norm_w, dy, sse, *small_acc)


SMALL_SHAPES = ((1, D), (1, 512), (1, SWA_D), (1, SWA_D), (1, SWA_H), (SWA_H, N_BUCKETS))


def _adamw_all(g_in_t, g_out, sum_a, sum_b, w_in_t, m_in_t, v_in_t, w_out, m_out, v_out, small_w, small_m, small_v):
    n_small = len(SMALL_SHAPES)
    halves = 2

    def body(*refs):
        gin_ref, gout_ref, suma_ref, sumb_ref, w_in_ref, m_in_ref, v_in_ref, w_out_ref, m_out_ref, v_out_ref = refs[:10]
        pos = 10
        sw_refs, sm_refs, sv_refs = (refs[pos + i * n_small:pos + (i + 1) * n_small] for i in range(3))
        pos += 3 * n_small
        g_in_o, d_in_o, nm_in_o, nv_in_o, g_out_o, d_out_o, nm_out_o, nv_out_o = refs[pos:pos + 8]
        pos += 8
        sg_o, sd_o, snm_o, snv_o = (refs[pos + i * n_small:pos + (i + 1) * n_small] for i in range(4))
        loss_o = refs[pos + 4 * n_small]

        g = gout_ref[...]
        d, nm, nv = _adamw(w_out_ref[...], g, m_out_ref[...], v_out_ref[...])
        g_out_o[...], d_out_o[...], nm_out_o[...], nv_out_o[...] = g, d, nm, nv
        g = gin_ref[...]
        d, nm, nv = _adamw(w_in_ref[...], g, m_in_ref[...], v_in_ref[...])
        g_in_o[...], d_in_o[...], nm_in_o[...], nv_in_o[...] = g, d, nm, nv

        @pl.when(pl.program_id(0) == 0)
        def _():
            loss_o[...] = suma_ref[LOSS_ROW:LOSS_ROW + 1, 0:1] * (0.5 / D)
            for r, (rows, lanes) in enumerate(SMALL_SHAPES):
                g = suma_ref[r:r + 1, 0:lanes] if rows == 1 else sumb_ref[:, 0:lanes]
                d, nm, nv = _adamw(sw_refs[r][...], g, sm_refs[r][...], sv_refs[r][...])
                sg_o[r][...], sd_o[r][...], snm_o[r][...], snv_o[r][...] = g, d, nm, nv

    half = lambda rows: pl.BlockSpec((rows // halves, D), lambda i: (i, 0))
    const = lambda shape: pl.BlockSpec(shape, lambda i: (0,) * len(shape))
    win = jax.ShapeDtypeStruct((WIN_BLK, D), F32)
    wout = jax.ShapeDtypeStruct((WOUT_BLK, D), F32)
    smalls = tuple(jax.ShapeDtypeStruct(s, F32) for s in SMALL_SHAPES)
    small_specs = [const(s) for s in SMALL_SHAPES]
    outs = pl.pallas_call(
        body, name="adamw_all", grid=(halves,),
        in_specs=[half(WIN_BLK), half(WOUT_BLK), const((8, D)), const((SWA_H, CH))] + [half(WIN_BLK)] * 3
                 + [half(WOUT_BLK)] * 3 + small_specs * 3,
        out_specs=tuple([half(WIN_BLK)] * 4 + [half(WOUT_BLK)] * 4 + small_specs * 4 + [const((1, 1))]),
        out_shape=(win,) * 4 + (wout,) * 4 + smalls * 4 + (jax.ShapeDtypeStruct((1, 1), F32),),
        compiler_params=_cparams(1, vmem=32 * 1024 * 1024),
    )(g_in_t, g_out, sum_a, sum_b, w_in_t, m_in_t, v_in_t, w_out, m_out, v_out, *small_w, *small_m, *small_v)
    big, rest = outs[:8], outs[8:]
    return big, [rest[i * n_small:(i + 1) * n_small] for i in range(4)], rest[4 * n_small]


def _small_rows(norm_w, ret_norm_w, q_norm_w, k_norm_w, sinks, rel_bias):
    return (norm_w.reshape(1, D), ret_norm_w.reshape(1, 512), q_norm_w.reshape(1, SWA_D), k_norm_w.reshape(1, SWA_D),
            sinks.reshape(1, SWA_H), rel_bias.T)


def _small_leaves(rows):
    return (rows[0].reshape(D), rows[1].reshape(512), rows[2].reshape(SWA_D), rows[3].reshape(SWA_D),
            rows[4].reshape(SWA_H), rows[5].T)


def kernel(x, norm_w, w_in, ret_norm_w, q_norm_w, k_norm_w, sinks, rel_bias, w_out, loss_target, m_norm_w, m_w_in, m_ret_norm_w, m_q_norm_w, m_k_norm_w, m_sinks, m_rel_bias, m_w_out, v_norm_w, v_w_in, v_ret_norm_w, v_q_norm_w, v_k_norm_w, v_sinks, v_rel_bias, v_w_out):
    x2 = x.reshape(T, D)
    target = loss_target.reshape(T, D)
    nw = norm_w.reshape(1, D)
    rnw_col = ret_norm_w.reshape(RET_V, 1)
    qnw_col = q_norm_w.reshape(SWA_D, 1)
    knw_col = k_norm_w.reshape(SWA_D, 1)
    relb = rel_bias.T.reshape(SWA_H * N_BUCKETS)
    ret_tables = _rotary_tables_t() + _retention_tables_t()
    bucket_t = _bucket_table_t()

    proj, wt, h = _in_proj_gather(x2, nw, w_in.T)
    (ro, mix_r, states, wo), (so, mix_s) = _fused_call("attn_fwd", N_CH, [
        _retention_fwd(proj, rnw_col, ret_tables, w_out),
        _swa_fwd(proj, qnw_col, knw_col, sinks, relb, bucket_t)])
    sse, dy, dmix_r, dmix_s, gwo = _out_proj(mix_r, mix_s, wo.reshape(D, D), x2, target)

    (d_ret, drw_acc), (d_swa, dbias, dsink_acc, dqw_acc, dkw_acc), (g_out,) = _fused_call("attn_bwd", N_CH + 1, [
        _retention_bwd(proj, ro, states, dmix_r, rnw_col, ret_tables),
        _swa_bwd(proj, so, dmix_s, qnw_col, knw_col, sinks, relb, bucket_t),
        _w_out_reduce_scatter(gwo.reshape(N_DEV, WOUT_BLK, D))])
    grad_x, g_in_t, sum_a, sum_b = _in_proj_bwd_rs(d_ret, d_swa, h, wt, x2, nw, dy, sse,
                                                   (dbias, bucket_t, dsink_acc, dqw_acc, dkw_acc, drw_acc))

    small_w = _small_rows(norm_w, ret_norm_w, q_norm_w, k_norm_w, sinks, rel_bias)
    small_m = _small_rows(m_norm_w, m_ret_norm_w, m_q_norm_w, m_k_norm_w, m_sinks, m_rel_bias)
    small_v = _small_rows(v_norm_w, v_ret_norm_w, v_q_norm_w, v_k_norm_w, v_sinks, v_rel_bias)
    big, small, loss = _adamw_all(g_in_t, g_out, sum_a, sum_b, w_in.T, m_w_in.T, v_w_in.T, w_out, m_w_out, v_w_out,
                                  small_w, small_m, small_v)

    def leaves(i):
        a = _small_leaves(small[i])
        return (a[0], big[i].T, a[1], a[2], a[3], a[4], a[5], big[4 + i])

    return (loss.reshape(()), grad_x.reshape(1, T, D), *leaves(0), *leaves(1), *leaves(2), *leaves(3))
```

```python
from typing import Callable, NamedTuple

import numpy as np
import jax
import jax.numpy as jnp
from jax import lax
from jax.experimental import pallas as pl
from jax.experimental.pallas import tpu as pltpu

F32 = jnp.float32
BF16 = jnp.bfloat16
MESH = pl.DeviceIdType.MESH

T = 2048
D = 1024
D_IN = 2816
N_DEV = 8
WIN_BLK = D_IN // N_DEV
WOUT_BLK = D // N_DEV
CH = 128
N_CH = T // CH
RET_H, RET_DK, RET_DV = 4, 64, 128
RET_QK = RET_H * RET_DK
RET_V = RET_H * RET_DV
SWA_H, SWA_KV, SWA_D, SWA_G = 8, 2, 64, 4
N_BUCKETS = 32
NORM_EPS = 1e-6
GN_EPS = 1e-5
NEG_INF = -1e30
PIECES = (256, 256, 512, 512, 512, 128, 128, 512)
OFFS = tuple(int(v) for v in np.cumsum((0,) + PIECES))
RET_W = OFFS[4]
SWA_W = D_IN - RET_W
SWA_OFFS = tuple(o - RET_W for o in OFFS[4:])
TM = 256

ADAM_LR, ADAM_B1, ADAM_B2, ADAM_EPS, ADAM_WD, ADAM_STEP = 0.001, 0.9, 0.999, 1e-08, 0.01, 10

VMEM_LIMIT = 56 * 1024 * 1024


def _cparams(n_grid=0, vmem=VMEM_LIMIT):
    sem = ("arbitrary",) * n_grid if n_grid else None
    return pltpu.CompilerParams(dimension_semantics=sem, vmem_limit_bytes=vmem)


class _Part(NamedTuple):
    step: Callable
    inputs: list
    in_specs: list
    out_specs: list
    out_shape: list
    scratch_shapes: list


def _fused_call(name, n_steps, parts):
    n_in = [len(p.inputs) for p in parts]
    n_out = [len(p.out_shape) for p in parts]
    n_scr = [len(p.scratch_shapes) for p in parts]

    def body(*refs):
        ins, outs, scr = refs[:sum(n_in)], refs[sum(n_in):sum(n_in) + sum(n_out)], refs[sum(n_in) + sum(n_out):]
        for i, p in enumerate(parts):
            take = lambda seq, counts: seq[sum(counts[:i]):sum(counts[:i + 1])]
            p.step(*take(ins, n_in), *take(outs, n_out), *take(scr, n_scr))

    flat = lambda field: [v for p in parts for v in getattr(p, field)]
    outs = pl.pallas_call(
        body, name=name, grid=(n_steps,), in_specs=flat("in_specs"), out_specs=tuple(flat("out_specs")),
        out_shape=tuple(flat("out_shape")), scratch_shapes=flat("scratch_shapes"), compiler_params=_cparams(1),
    )(*flat("inputs"))
    return [list(outs[sum(n_out[:i]):sum(n_out[:i + 1])]) for i in range(len(parts))]


def _dot(a, b):
    return jnp.dot(a, b, preferred_element_type=F32)


def _dot_nt(a, b):
    return lax.dot_general(a, b, (((1,), (1,)), ((), ())), preferred_element_type=F32)


def _bf(a):
    return a.astype(BF16)


def _sigmoid(x):
    return 1.0 / (1.0 + jnp.exp(-x))


def _adamw(w, g, m, v):
    m = ADAM_B1 * m + (1.0 - ADAM_B1) * g
    v = ADAM_B2 * v + (1.0 - ADAM_B2) * (g * g)
    m_hat = m / (1.0 - ADAM_B1 ** ADAM_STEP)
    v_hat = v / (1.0 - ADAM_B2 ** ADAM_STEP)
    delta = -ADAM_LR * (m_hat / (jnp.sqrt(v_hat) + ADAM_EPS) + ADAM_WD * w)
    return delta, m, v


def _rotary_tables_t():
    half = RET_DK // 2
    inv_freq = np.float32(10000.0) ** (-np.arange(half, dtype=np.float32) / np.float32(half))
    ang = inv_freq[:, None] * np.arange(T, dtype=np.float32)[None, :]
    cos, sin = np.cos(ang).astype(np.float32), np.sin(ang).astype(np.float32)
    cos64 = np.concatenate([cos, cos], axis=0)
    sin64 = np.concatenate([-sin, sin], axis=0)
    return np.tile(cos64, (RET_H, 1)), np.tile(sin64, (RET_H, 1))


def _retention_tables_t():
    gamma = (1.0 - np.exp2(-5.0 - np.arange(RET_H, dtype=np.float32))).astype(np.float32)
    log_g = np.log(gamma).astype(np.float32)
    i = np.arange(CH, dtype=np.float32)
    diff = i[None, :] - i[:, None]
    decay = np.where(diff >= 0, np.exp(log_g[:, None, None] * np.maximum(diff, 0.0)), 0.0).astype(np.float32)
    decay_all = np.concatenate(list(decay), axis=1)
    zeta = np.exp(log_g[:, None] * (CH - 1.0 - i)).astype(np.float32)
    zeta_tab = np.repeat(zeta.T, RET_DK, axis=1)
    xi = np.exp(log_g[:, None] * (i + 1.0)).astype(np.float32)
    xi_tab = np.repeat(xi, RET_DK, axis=0)
    chunk_decay = np.exp(log_g * np.float32(CH)).astype(np.float32)
    row_head = np.arange(RET_V)[:, None] // RET_DV
    col_head = np.arange(RET_QK)[None, :] // RET_DK
    state_mask = (row_head == col_head).astype(np.float32)
    state_decay = (state_mask * chunk_decay[row_head]).astype(np.float32)
    q_mask = (np.arange(RET_QK)[:, None] // RET_DK == np.arange(RET_H * CH)[None, :] // CH).astype(np.float32)
    return (np.ascontiguousarray(decay_all), np.ascontiguousarray(zeta_tab), np.ascontiguousarray(xi_tab),
            state_mask, state_decay, q_mask)


def _bucket_table_t():
    qi = np.arange(CH)[:, None]
    kj = np.arange(2 * CH)[None, :]
    dist = qi + CH - kj
    n = np.maximum(dist, 0)
    max_exact = N_BUCKETS // 2
    nf = np.maximum(n, 1).astype(np.float32)
    large = max_exact + (np.log(nf / np.float32(max_exact)) / np.float32(np.log(CH / max_exact))
                         * np.float32(N_BUCKETS - max_exact)).astype(np.int32)
    large = np.minimum(large, N_BUCKETS - 1)
    bucket = np.where(n < max_exact, n, large)
    return np.ascontiguousarray(np.where((dist >= 0) & (dist < CH), bucket, -1).astype(np.int32).T)


def _mesh_pos():
    return lax.axis_index("x"), lax.axis_index("y"), lax.axis_index("c")


def _blk(px, py, pc):
    return 4 * px + 2 * py + pc


def _in_proj_gather(x, norm_w, w_in_t):
    tchunk = 512

    def body(x_any, nw_any, win_any, proj_ref, wt_out, h_out, wt_ref, h_ref, stage, x_ref, nw_ref, win_ref,
             send_sems, recv_sems, out_sems, wb_sems, h_sem, in_sems):
        n_tiles = T // TM
        loads = [pltpu.make_async_copy(win_any, win_ref, in_sems.at[0]),
                 pltpu.make_async_copy(nw_any, nw_ref, in_sems.at[1])]
        loads += [pltpu.make_async_copy(x_any.at[pl.ds(i * TM, TM), :], x_ref.at[pl.ds(i * TM, TM), :], in_sems.at[2 + i])
                  for i in range(n_tiles)]
        for cp in loads:
            cp.start()
        loads[0].wait()
        x, y, c = _mesh_pos()
        me = _blk(x, y, c)
        sibling = (x, y, 1 - c)
        nbr_a, nbr_b, diag = (x ^ (1 - c), y ^ c), (x ^ c, y ^ (1 - c)), (1 - x, 1 - y)

        def copy(k, b, to):
            return pltpu.make_async_remote_copy(src_ref=wt_ref.at[b], dst_ref=wt_ref.at[b], send_sem=send_sems.at[k],
                                                recv_sem=recv_sems.at[k], device_id=to, device_id_type=MESH)

        wt_ref[me] = _bf(win_ref[...])
        first = [copy(0, me, sibling), copy(1, me, (*nbr_a, c))]
        for cp in first:
            cp.start()
        to_b = copy(2, me, (*nbr_b, c))

        loads[1].wait()
        nw = nw_ref[...]
        for i in range(n_tiles):
            rows = slice(i * TM, (i + 1) * TM)
            loads[2 + i].wait()
            xv = x_ref[rows, :]
            r = lax.rsqrt(jnp.mean(xv * xv, axis=-1, keepdims=True) + NORM_EPS)
            h_ref[rows, :] = _bf(xv * r * nw)
        keep = [pltpu.make_async_copy(h_ref, h_out, h_sem)]
        keep[0].start()

        writes = []

        def project(b):
            k = len(writes)
            if k >= 2:
                writes[k - 2].wait()
            keep.append(pltpu.make_async_copy(wt_ref.at[b], wt_out.at[b], wb_sems.at[k]))
            keep[-1].start()
            w = wt_ref[b]
            for t in range(T // tchunk):
                cols = slice(t * tchunk, (t + 1) * tchunk)
                stage[k % 2, :, cols] = _dot_nt(w, h_ref[cols, :])
            cp = pltpu.make_async_copy(stage.at[k % 2], proj_ref.at[b], out_sems.at[k % 2])
            cp.start()
            writes.append(cp)

        here = (x, y, c)
        project(me)
        copy(0, _blk(x, y, 1 - c), here).wait_recv()
        project(_blk(x, y, 1 - c))
        to_b.start()
        passed = [to_b]

        def landed(k, chip, relay_to=None):
            copy(k, _blk(*chip, c), here).wait_recv()
            out = [copy({1: 4, 2: 5, 3: 6}[k], _blk(*chip, c), sibling)]
            if relay_to is not None:
                out.append(copy(3, _blk(*chip, c), (*relay_to, c)))
            for cp in out:
                cp.start()
            passed.extend(out)
            project(_blk(*chip, c))

        def from_sibling(k, chip):
            copy(k, _blk(*chip, 1 - c), here).wait_recv()
            project(_blk(*chip, 1 - c))

        landed(1, nbr_a, relay_to=nbr_b)
        from_sibling(4, nbr_b)
        landed(2, nbr_b)
        from_sibling(5, nbr_a)
        landed(3, diag)
        from_sibling(6, diag)
        for cp in writes[-2:] + keep:
            cp.wait()
        for cp in first + passed:
            cp.wait_send()

    vm = pl.BlockSpec(memory_space=pltpu.VMEM)
    hbm = pl.BlockSpec(memory_space=pl.ANY)
    dma = pltpu.SemaphoreType.DMA
    proj, wt, h = pl.pallas_call(
        body, name="in_proj_gather",
        out_shape=(jax.ShapeDtypeStruct((N_DEV, WIN_BLK, T), F32), jax.ShapeDtypeStruct((N_DEV, WIN_BLK, D), BF16),
                   jax.ShapeDtypeStruct((T, D), BF16)),
        in_specs=[hbm, hbm, hbm], out_specs=(hbm, hbm, hbm),
        scratch_shapes=[pltpu.VMEM((N_DEV, WIN_BLK, D), BF16), pltpu.VMEM((T, D), BF16), pltpu.VMEM((2, WIN_BLK, T), F32),
                        pltpu.VMEM((T, D), F32), pltpu.VMEM((1, D), F32), pltpu.VMEM((WIN_BLK, D), F32),
                        dma((7,)), dma((7,)), dma((2,)), dma((N_DEV,)), dma, dma((2 + T // TM,))],
        compiler_params=_cparams(vmem=48 * 1024 * 1024),
    )(x, norm_w, w_in_t)
    return proj.reshape(D_IN, T), wt.reshape(D_IN, D), h


def _gather_w_out(w_out_ref, own_ref, wo_ref, local_sem, send_sems, recv_sems, phase):
    x, y, c = _mesh_pos()
    me = _blk(x, y, c)
    sibling = (x, y, 1 - c)
    chips = [(1 - x, y), (x, 1 - y), (1 - x, 1 - y)]

    def copy(k, b, to, src=None):
        return pltpu.make_async_remote_copy(src_ref=wo_ref.at[b] if src is None else src, dst_ref=wo_ref.at[b],
                                            send_sem=send_sems.at[k], recv_sem=recv_sems.at[k], device_id=to,
                                            device_id_type=MESH)

    local = pltpu.make_async_copy(own_ref, wo_ref.at[me], local_sem)
    first = [copy(0, me, sibling, own_ref)] + [copy(1 + j, me, (*chip, c), own_ref) for j, chip in enumerate(chips)]
    passed = [copy(4 + j, _blk(*chip, c), sibling) for j, chip in enumerate(chips)]
    if phase == 0:
        own_ref[...] = _bf(w_out_ref[...])
        local.start()
        for cp in first:
            cp.start()
    elif phase == 1:
        for j, chip in enumerate(chips):
            copy(1 + j, _blk(*chip, c), (x, y, c)).wait_recv()
            passed[j].start()
    else:
        local.wait()
        copy(0, _blk(x, y, 1 - c), (x, y, c)).wait_recv()
        for j, chip in enumerate(chips):
            copy(4 + j, _blk(*chip, 1 - c), (x, y, c)).wait_recv()
        for cp in first + passed:
            cp.wait_send()


def _proj_piece(i, block_of):
    rows = PIECES[i]
    assert OFFS[i] % rows == 0
    return pl.BlockSpec((rows, CH), lambda n: (OFFS[i] // rows, block_of(n)))


def _proj_gate_halves(block_of):
    return [pl.BlockSpec((256, CH), lambda n, j=j: (OFFS[7] // 256 + j, block_of(n))) for j in range(2)]


def _swap_halves_t(t):
    half = RET_DK // 2
    parts = []
    for h in range(RET_H):
        parts += [t[h * RET_DK + half:(h + 1) * RET_DK], t[h * RET_DK:h * RET_DK + half]]
    return jnp.concatenate(parts, axis=0)


def _rotate(t, cos, sin):
    return t * cos + _swap_halves_t(t) * sin


def _group_norm_t(o):
    mu = jnp.mean(o, axis=0, keepdims=True)
    var = jnp.mean((o - mu) * (o - mu), axis=0, keepdims=True)
    rstd = lax.rsqrt(var + GN_EPS)
    return (o - mu) * rstd, rstd


def _retention_scores_t(q_t, k_rm_b, q_mask, decay_all):
    q_heads = _bf(jnp.concatenate([q_t] * RET_H, axis=1) * q_mask)
    return _dot(k_rm_b, q_heads) * decay_all


def _retention_fwd(proj, ret_norm_w, tables, w_out_blk):
    cos_t, sin_t, decay_all, zeta_tab, xi_tab, state_mask, state_decay, q_mask = tables

    def body(rq_ref, rk_ref, rv_ref, rg_ref, w_ref, cos_ref, sin_ref, decay_ref, zeta_ref, xi_ref, smask_ref,
             sdecay_ref, qmask_ref, wout_ref, ro_ref, mix_ref, st_ref, wo_ref, state, wo_own, wo_local, wo_send, wo_recv):
        n = pl.program_id(0)

        @pl.when(n == 0)
        def _():
            state[...] = jnp.zeros_like(state)
            _gather_w_out(wout_ref, wo_own, wo_ref, wo_local, wo_send, wo_recv, phase=0)

        @pl.when(n == N_CH // 2)
        def _():
            _gather_w_out(wout_ref, wo_own, wo_ref, wo_local, wo_send, wo_recv, phase=1)

        @pl.when(n == N_CH - 1)
        def _():
            _gather_w_out(wout_ref, wo_own, wo_ref, wo_local, wo_send, wo_recv, phase=2)

        cos, sin = cos_ref[...], sin_ref[...]
        q_t = _rotate(rq_ref[...], cos, sin)
        k_t = _rotate(rk_ref[...], cos, sin) * (RET_DK ** -0.5)
        k_rm = k_t.T
        v_b = _bf(rv_ref[...])
        m_b = _bf(state[...])
        st_ref[0] = m_b
        scores_b = _bf(_retention_scores_t(q_t, _bf(k_rm), qmask_ref[...], decay_ref[...]))
        cross = _dot(m_b, _bf(q_t * xi_ref[...]))
        state[...] = state[...] * sdecay_ref[...] + _dot(v_b, _bf(k_rm * zeta_ref[...])) * smask_ref[...]
        heads = range(RET_H)
        rows = [slice(h * RET_DV, (h + 1) * RET_DV) for h in heads]
        o = [_dot(v_b[rows[h]], scores_b[:, h * CH:(h + 1) * CH]) + cross[rows[h]] for h in heads]
        for h in heads:
            rn, _ = _group_norm_t(o[h])
            g = rg_ref[rows[h], :]
            ro_ref[rows[h], :] = o[h]
            mix_ref[rows[h], :] = rn * w_ref[rows[h], :] * (g * _sigmoid(g))

    col = lambda w: pl.BlockSpec((w, CH), lambda n: (0, n))
    const = lambda shape: pl.BlockSpec(shape, lambda n: (0,) * len(shape))
    cur = lambda n: n
    return _Part(
        body,
        inputs=[proj, proj, proj, proj, ret_norm_w, cos_t, sin_t, decay_all, zeta_tab, xi_tab, state_mask, state_decay,
                q_mask, w_out_blk],
        in_specs=[_proj_piece(i, cur) for i in range(4)] + [
            const((RET_V, 1)), col(RET_QK), col(RET_QK),
            const(decay_all.shape), const(zeta_tab.shape), const(xi_tab.shape), const(state_mask.shape),
            const(state_decay.shape), const(q_mask.shape), const((WOUT_BLK, D))],
        out_specs=[col(RET_V), col(RET_V), pl.BlockSpec((1, RET_V, RET_QK), lambda n: (n, 0, 0)),
                   pl.BlockSpec(memory_space=pl.ANY)],
        out_shape=[jax.ShapeDtypeStruct((RET_V, T), F32), jax.ShapeDtypeStruct((RET_V, T), F32),
                   jax.ShapeDtypeStruct((N_CH, RET_V, RET_QK), BF16), jax.ShapeDtypeStruct((N_DEV, WOUT_BLK, D), BF16)],
        scratch_shapes=[pltpu.VMEM((RET_V, RET_QK), F32), pltpu.VMEM((WOUT_BLK, D), BF16), pltpu.SemaphoreType.DMA,
                        pltpu.SemaphoreType.DMA((N_DEV - 1,)), pltpu.SemaphoreType.DMA((N_DEV - 1,))])


def _retention_bwd(proj, ro, states, dmix, ret_norm_w, tables):
    cos_t, sin_t, decay_all, zeta_tab, xi_tab, state_mask, state_decay, q_mask = tables

    def body(*refs):
        @pl.when(pl.program_id(0) < N_CH)
        def _():
            chunk(*refs)

    def chunk(rq_ref, rk_ref, rv_ref, rg_ref, ro_ref, st_ref, dm_ref, w_ref, cos_ref, sin_ref, decay_ref, zeta_ref,
              xi_ref, smask_ref, sdecay_ref, qmask_ref, d_ref, dw_ref, gstate):
        i = pl.program_id(0)

        @pl.when(i == 0)
        def _():
            gstate[...] = jnp.zeros_like(gstate)
            dw_ref[...] = jnp.zeros_like(dw_ref)

        cos, sin = cos_ref[...], sin_ref[...]
        q_t = _rotate(rq_ref[...], cos, sin)
        k_t = _rotate(rk_ref[...], cos, sin) * (RET_DK ** -0.5)
        q_b, k_b = _bf(q_t), _bf(k_t)
        k_rm = k_t.T
        kz_b = _bf(k_rm * zeta_ref[...])
        qx_b = _bf(q_t * xi_ref[...])
        v_t = rv_ref[...]
        v_b = _bf(v_t)
        v_rm_b = _bf(v_t.T)
        decay = decay_ref[...]
        scores_b = _bf(_retention_scores_t(q_t, _bf(k_rm), qmask_ref[...], decay))
        heads = range(RET_H)
        rows = [slice(h * RET_DV, (h + 1) * RET_DV) for h in heads]
        qk_rows = [slice(h * RET_DK, (h + 1) * RET_DK) for h in heads]
        lanes = [slice(h * CH, (h + 1) * CH) for h in heads]
        do = []
        for h in heads:
            g, w, dm = rg_ref[rows[h], :], w_ref[rows[h], :], dm_ref[rows[h], :]
            rn, rstd = _group_norm_t(ro_ref[rows[h], :])
            sig = _sigmoid(g)
            silu = g * sig
            d_ref[OFFS[3] + h * RET_DV:OFFS[3] + (h + 1) * RET_DV, :] = _bf(dm * rn * w * (sig * (1.0 + g * (1.0 - sig))))
            dw_ref[rows[h], :] += dm * silu * rn
            drn = dm * silu * w
            do.append(rstd * (drn - jnp.mean(drn, axis=0, keepdims=True)
                              - rn * jnp.mean(drn * rn, axis=0, keepdims=True)))
        do_b = _bf(jnp.concatenate(do, axis=0))
        m_b = st_ref[0]
        g_all = gstate[...]
        g_b = _bf(g_all)
        dscores_b = [_bf(_dot(v_rm_b[:, rows[h]], do_b[rows[h]]) * decay[:, lanes[h]]) for h in heads]
        dq_cross = lax.dot_general(m_b, do_b, (((0,), (0,)), ((), ())), preferred_element_type=F32)
        dkz = _dot(v_rm_b, g_b)
        dv_cross = _dot_nt(g_b, kz_b)
        gstate[...] = g_all * sdecay_ref[...] + _dot_nt(do_b, qx_b) * smask_ref[...]
        dq = jnp.concatenate([_dot(k_b[qk_rows[h]], dscores_b[h]) for h in heads], axis=0) + dq_cross * xi_ref[...]
        dk = (jnp.concatenate([_dot_nt(q_b[qk_rows[h]], dscores_b[h]) for h in heads], axis=0)
              + (dkz * zeta_ref[...]).T) * (RET_DK ** -0.5)
        for h in heads:
            d_ref[OFFS[2] + h * RET_DV:OFFS[2] + (h + 1) * RET_DV, :] = _bf(
                _dot_nt(do_b[rows[h]], scores_b[:, lanes[h]]) + dv_cross[rows[h]])
        d_ref[OFFS[0]:OFFS[1], :] = _bf(dq * cos + _swap_halves_t(dq * sin))
        d_ref[OFFS[1]:OFFS[2], :] = _bf(dk * cos + _swap_halves_t(dk * sin))

    chunk_of = lambda i: N_CH - 1 - jnp.minimum(i, N_CH - 1)
    col = lambda w: pl.BlockSpec((w, CH), lambda i: (0, chunk_of(i)))
    const = lambda shape: pl.BlockSpec(shape, lambda i: (0,) * len(shape))
    return _Part(
        body,
        inputs=[proj, proj, proj, proj, ro, states, dmix, ret_norm_w, cos_t, sin_t, decay_all, zeta_tab, xi_tab,
                state_mask, state_decay, q_mask],
        in_specs=[_proj_piece(j, chunk_of) for j in range(4)] + [
                  col(RET_V),
                  pl.BlockSpec((1, RET_V, RET_QK), lambda i: (chunk_of(i), 0, 0)), col(RET_V),
                  const((RET_V, 1)), col(RET_QK), col(RET_QK),
                  const(decay_all.shape), const(zeta_tab.shape), const(xi_tab.shape), const(state_mask.shape),
                  const(state_decay.shape), const(q_mask.shape)],
        out_specs=[col(RET_W), const((RET_V, CH))],
        out_shape=[jax.ShapeDtypeStruct((RET_W, T), BF16), jax.ShapeDtypeStruct((RET_V, CH), F32)],
        scratch_shapes=[pltpu.VMEM((RET_V, RET_QK), F32)])


HQ_LANES = SWA_H * CH


def _head_lanes(hq):
    return slice(hq * CH, (hq + 1) * CH)


def _build_bias_t(bucket_ref, relb_ref, sinks_ref, bias_ref, sink_row):
    bt = bucket_ref[...]
    first = lax.broadcasted_iota(jnp.int32, bt.shape, 0) < CH
    for hq in range(SWA_H):
        b = jnp.full(bt.shape, NEG_INF, F32)
        for bk in range(N_BUCKETS):
            b = jnp.where(bt == bk, relb_ref[hq * N_BUCKETS + bk], b)
        bias_ref[0, :, _head_lanes(hq)] = b
        bias_ref[1, :, _head_lanes(hq)] = jnp.where(first, NEG_INF, b)
        sink_row[:, _head_lanes(hq)] = jnp.full((1, CH), sinks_ref[hq], F32)


def _rms_t(t, w_col):
    r = lax.rsqrt(jnp.mean(t * t, axis=0, keepdims=True) + NORM_EPS)
    return t * r * w_col, r


def _rms_t_bwd(dn, t, r, w_col):
    u = dn * w_col
    return r * u - t * (r * r * r) * jnp.mean(u * t, axis=0, keepdims=True), dn * t * r


def _norm_kv(k_t, kw):
    return jnp.concatenate([_rms_t(k_t[g * SWA_D:(g + 1) * SWA_D], kw)[0] for g in range(SWA_KV)], axis=0)


def _kv_slot(a, kv):
    z = jnp.zeros_like(a)
    return jnp.concatenate([a, z] if kv == 0 else [z, a], axis=0)


def _softmax_t(s, sink):
    m = jnp.maximum(jnp.max(s, axis=0, keepdims=True), sink)
    p = jnp.exp(s - m)
    e_sink = jnp.exp(sink - m)
    inv = 1.0 / (jnp.sum(p, axis=0, keepdims=True) + e_sink)
    return p * inv, e_sink * inv


def _gate_rows(sg_halves, hq):
    per_half = SWA_H // 2
    return sg_halves[hq // per_half][(hq % per_half) * SWA_D:(hq % per_half + 1) * SWA_D, :]


def _swa_fwd(proj, q_norm_w, k_norm_w, sinks, rel_bias_flat, bucket_t):
    def body(sq_ref, skp_ref, skc_ref, svp_ref, svc_ref, sg_lo, sg_hi, qw_ref, kw_ref, sinks_ref, relb_ref, bucket_ref,
             so_ref, mix_ref, bias_ref, sink_row):
        n = pl.program_id(0)

        @pl.when(n == 0)
        def _():
            _build_bias_t(bucket_ref, relb_ref, sinks_ref, bias_ref, sink_row)

        var = (n == 0).astype(jnp.int32)
        qw, kw = qw_ref[...], kw_ref[...]
        kn_band = jnp.concatenate([_norm_kv(skp_ref[...], kw), _norm_kv(skc_ref[...], kw)], axis=1)
        kb_rm = _bf(kn_band.T)
        vband = _bf(jnp.concatenate([svp_ref[...], svc_ref[...]], axis=1))
        q_all = jnp.concatenate(
            [_bf(_kv_slot(_rms_t(sq_ref[hq * SWA_D:(hq + 1) * SWA_D, :], qw)[0] * (SWA_D ** -0.5), hq // SWA_G))
             for hq in range(SWA_H)], axis=1)
        probs, _ = _softmax_t(_dot(kb_rm, q_all) + bias_ref[var], sink_row[...])
        probs_b = _bf(probs)
        for kv in range(SWA_KV):
            o = _dot(vband[kv * SWA_D:(kv + 1) * SWA_D], probs_b[:, kv * SWA_G * CH:(kv + 1) * SWA_G * CH])
            for j in range(SWA_G):
                rows = slice((kv * SWA_G + j) * SWA_D, (kv * SWA_G + j + 1) * SWA_D)
                oh = o[:, j * CH:(j + 1) * CH]
                g = _gate_rows((sg_lo, sg_hi), kv * SWA_G + j)
                so_ref[rows, :] = oh
                mix_ref[rows, :] = oh * (g * _sigmoid(g))

    col = lambda w: pl.BlockSpec((w, CH), lambda n: (0, n))
    const = lambda shape: pl.BlockSpec(shape, lambda n: (0,) * len(shape))
    smem = pl.BlockSpec(memory_space=pltpu.SMEM)
    cur = lambda n: n
    prev = lambda n: jnp.maximum(n - 1, 0)
    return _Part(
        body,
        inputs=[proj, proj, proj, proj, proj, proj, proj, q_norm_w, k_norm_w, sinks, rel_bias_flat, bucket_t],
        in_specs=[_proj_piece(4, cur), _proj_piece(5, prev), _proj_piece(5, cur), _proj_piece(6, prev),
                  _proj_piece(6, cur)] + _proj_gate_halves(cur) + [
                  const((SWA_D, 1)), const((SWA_D, 1)), smem, smem, const((2 * CH, CH))],
        out_specs=[col(512), col(512)],
        out_shape=[jax.ShapeDtypeStruct((512, T), F32), jax.ShapeDtypeStruct((512, T), F32)],
        scratch_shapes=[pltpu.VMEM((2, 2 * CH, HQ_LANES), F32), pltpu.VMEM((1, HQ_LANES), F32)])


def _swa_bwd(proj, so, dmix, q_norm_w, k_norm_w, sinks, rel_bias_flat, bucket_t):
    def body(sq_ref, skp_ref, skc_ref, svp_ref, svc_ref, sg_lo, sg_hi, so_ref, dm_ref, qw_ref, kw_ref, sinks_ref,
             relb_ref, bucket_ref, d_ref, dbias_ref, dsink_ref, dqw_ref, dkw_ref,
             bias_ref, sink_row, band_dk, band_dv, carry_dk, carry_dv, hold_dq, hold_dg):
        n = pl.program_id(0)
        slot = n % 2

        @pl.when(n == 0)
        def _():
            _build_bias_t(bucket_ref, relb_ref, sinks_ref, bias_ref, sink_row)
            for ref in (dbias_ref, dsink_ref, dqw_ref, dkw_ref, carry_dk, carry_dv):
                ref[...] = jnp.zeros_like(ref)

        qw, kw = qw_ref[...], kw_ref[...]

        @pl.when(n < N_CH)
        def _():
            var = (n == 0).astype(jnp.int32)
            kn_band = jnp.concatenate([_norm_kv(skp_ref[...], kw), _norm_kv(skc_ref[...], kw)], axis=1)
            kb_rm = _bf(kn_band.T)
            kn_band_s = _bf(kn_band * (SWA_D ** -0.5))
            vband_f = jnp.concatenate([svp_ref[...], svc_ref[...]], axis=1)
            vb_rm = _bf(vband_f.T)
            q_raw, q_rstd, qs_b, do_b = [], [], [], []
            for hq in range(SWA_H):
                rows = slice(hq * SWA_D, (hq + 1) * SWA_D)
                q_t = sq_ref[rows, :]
                qn, rq = _rms_t(q_t, qw)
                g = _gate_rows((sg_lo, sg_hi), hq)
                sig = _sigmoid(g)
                dm = dm_ref[rows, :]
                hold_dg[slot, rows, :] = _bf(dm * so_ref[rows, :] * (sig * (1.0 + g * (1.0 - sig))))
                q_raw.append(q_t)
                q_rstd.append(rq)
                qs_b.append(_bf(qn * (SWA_D ** -0.5)))
                do_b.append(_bf(dm * (g * sig)))
            q_all = jnp.concatenate([_kv_slot(qs_b[hq], hq // SWA_G) for hq in range(SWA_H)], axis=1)
            do_all = jnp.concatenate([_kv_slot(do_b[hq], hq // SWA_G) for hq in range(SWA_H)], axis=1)
            probs, p_sink = _softmax_t(_dot(kb_rm, q_all) + bias_ref[var], sink_row[...])
            dprobs = _dot(vb_rm, do_all)
            t = jnp.sum(probs * dprobs, axis=0, keepdims=True)
            dlog = probs * (dprobs - t)
            dsink_ref[...] += -(p_sink * t)
            dbias_ref[...] += dlog
            dlog_b, probs_b = _bf(dlog), _bf(probs)
            dkn, dvv = [], []
            for kv in range(SWA_KV):
                heads = range(kv * SWA_G, (kv + 1) * SWA_G)
                lanes = slice(kv * SWA_G * CH, (kv + 1) * SWA_G * CH)
                dvv.append(_dot_nt(jnp.concatenate([do_b[hq] for hq in heads], axis=1), probs_b[:, lanes]))
                dkn.append(_dot_nt(jnp.concatenate([qs_b[hq] for hq in heads], axis=1), dlog_b[:, lanes]))
                dqn = _dot(kn_band_s[kv * SWA_D:(kv + 1) * SWA_D], dlog_b[:, lanes])
                for j, hq in enumerate(heads):
                    dq_t, dqw_terms = _rms_t_bwd(dqn[:, j * CH:(j + 1) * CH], q_raw[hq], q_rstd[hq], qw)
                    hold_dq[slot, hq * SWA_D:(hq + 1) * SWA_D, :] = _bf(dq_t)
                    dqw_ref[...] += dqw_terms
            band_dk[...] = jnp.concatenate(dkn, axis=0)
            band_dv[...] = jnp.concatenate(dvv, axis=0)

        @pl.when(n == N_CH)
        def _():
            band_dk[...] = jnp.zeros_like(band_dk)
            band_dv[...] = jnp.zeros_like(band_dv)

        @pl.when(n >= 1)
        def _():
            dkn_prev = carry_dk[...] + band_dk[:, 0:CH]
            k_t = skp_ref[...]
            for kv in range(SWA_KV):
                rows = slice(kv * SWA_D, (kv + 1) * SWA_D)
                _, rk = _rms_t(k_t[rows], kw)
                dk_t, dkw_terms = _rms_t_bwd(dkn_prev[rows], k_t[rows], rk, kw)
                d_ref[SWA_OFFS[1] + kv * SWA_D:SWA_OFFS[1] + (kv + 1) * SWA_D, :] = _bf(dk_t)
                dkw_ref[...] += dkw_terms
            d_ref[SWA_OFFS[2]:SWA_OFFS[3], :] = _bf(carry_dv[...] + band_dv[:, 0:CH])
            d_ref[SWA_OFFS[0]:SWA_OFFS[1], :] = hold_dq[1 - slot]
            d_ref[SWA_OFFS[3]:SWA_OFFS[4], :] = hold_dg[1 - slot]

        carry_dk[...] = band_dk[:, CH:2 * CH]
        carry_dv[...] = band_dv[:, CH:2 * CH]

    cur_block = lambda n: jnp.minimum(n, N_CH - 1)
    prev_block = lambda n: jnp.maximum(n - 1, 0)
    col = lambda w: pl.BlockSpec((w, CH), lambda n: (0, cur_block(n)))
    prev = lambda w: pl.BlockSpec((w, CH), lambda n: (0, prev_block(n)))
    const = lambda shape: pl.BlockSpec(shape, lambda n: (0,) * len(shape))
    smem = pl.BlockSpec(memory_space=pltpu.SMEM)
    return _Part(
        body,
        inputs=[proj, proj, proj, proj, proj, proj, proj, so, dmix, q_norm_w, k_norm_w, sinks, rel_bias_flat, bucket_t],
        in_specs=[_proj_piece(4, cur_block), _proj_piece(5, prev_block), _proj_piece(5, cur_block),
                  _proj_piece(6, prev_block), _proj_piece(6, cur_block)] + _proj_gate_halves(cur_block) + [
                  col(512), col(512), const((SWA_D, 1)), const((SWA_D, 1)), smem, smem, const((2 * CH, CH))],
        out_specs=[prev(SWA_W), const((2 * CH, HQ_LANES)), const((1, HQ_LANES)),
                   const((SWA_D, CH)), const((SWA_D, CH))],
        out_shape=[jax.ShapeDtypeStruct((SWA_W, T), BF16),
                   jax.ShapeDtypeStruct((2 * CH, HQ_LANES), F32), jax.ShapeDtypeStruct((1, HQ_LANES), F32),
                   jax.ShapeDtypeStruct((SWA_D, CH), F32), jax.ShapeDtypeStruct((SWA_D, CH), F32)],
        scratch_shapes=[pltpu.VMEM((2, 2 * CH, HQ_LANES), F32), pltpu.VMEM((1, HQ_LANES), F32),
                        pltpu.VMEM((128, 2 * CH), F32), pltpu.VMEM((128, 2 * CH), F32),
                        pltpu.VMEM((128, CH), F32), pltpu.VMEM((128, CH), F32),
                        pltpu.VMEM((2, 512, CH), BF16), pltpu.VMEM((2, 512, CH), BF16)])


SMALL_GRAD_SHAPES = ((1, RET_V), (1, CH), (1, CH), (1, CH), (SWA_H, N_BUCKETS))


def _finish_small_grads(dbias_ref, bucket_ref, dsink_ref, dqw_ref, dkw_ref, drw_ref, rw_o, qw_o, kw_o, sink_o, relb_o):
    bt = bucket_ref[...]
    row = lax.broadcasted_iota(jnp.int32, (SWA_H, N_BUCKETS), 0)
    col = lax.broadcasted_iota(jnp.int32, (SWA_H, N_BUCKETS), 1)
    lane = lax.broadcasted_iota(jnp.int32, (1, CH), 1)
    acc = jnp.zeros((SWA_H, N_BUCKETS), F32)
    sink = jnp.zeros((1, CH), F32)
    for hq in range(SWA_H):
        d = dbias_ref[:, _head_lanes(hq)]
        for bk in range(N_BUCKETS):
            s = jnp.sum(jnp.sum(jnp.where(bt == bk, d, 0.0), axis=0, keepdims=True), axis=1, keepdims=True)
            acc = acc + jnp.where((row == hq) & (col == bk), s, 0.0)
        sink = sink + jnp.where(lane == hq, jnp.sum(dsink_ref[:, _head_lanes(hq)], axis=1, keepdims=True), 0.0)
    relb_o[...] = acc
    sink_o[...] = sink
    for src, dst in ((dqw_ref, qw_o), (dkw_ref, kw_o)):
        padded = jnp.concatenate([src[...], jnp.zeros((CH - SWA_D, CH), F32)], axis=0)
        dst[...] = jnp.sum(padded.T, axis=0, keepdims=True)
    rw_o[...] = jnp.sum(drw_ref[...].T, axis=0, keepdims=True)


def _out_proj(mix_r, mix_s, wo, x, target):
    def body(mr_ref, ms_ref, w_ref, x_ref, t_ref, loss_ref, dy_ref, dmr_ref, dms_ref, gw_ref):
        i = pl.program_id(0)

        @pl.when(i == 0)
        def _():
            loss_ref[...] = jnp.zeros_like(loss_ref)
            gw_ref[...] = jnp.zeros_like(gw_ref)

        mixed = jnp.concatenate([mr_ref[...], ms_ref[...]], axis=0)
        w = w_ref[...]
        err = x_ref[...] + _dot(_bf(mixed.T), w) - t_ref[...]
        loss_ref[...] += jnp.sum(jnp.sum(err * err, axis=1, keepdims=True), axis=0, keepdims=True)
        dy = err * (1.0 / D)
        dy_ref[...] = dy
        dy_b = _bf(dy)
        dmix = _dot_nt(w, dy_b)
        dmr_ref[...] = dmix[0:512]
        dms_ref[...] = dmix[512:D]
        gw_ref[...] += _dot(_bf(mixed), dy_b)

    row = lambda w: pl.BlockSpec((TM, w), lambda i: (i, 0))
    col = lambda w: pl.BlockSpec((w, TM), lambda i: (0, i))
    const = lambda shape: pl.BlockSpec(shape, lambda i: (0,) * len(shape))
    return pl.pallas_call(
        body, name="out_proj", grid=(T // TM,),
        in_specs=[col(512), col(512), const((D, D)), row(D), row(D)],
        out_specs=(const((1, 1)), row(D), col(512), col(512), const((D, D))),
        out_shape=(jax.ShapeDtypeStruct((1, 1), F32), jax.ShapeDtypeStruct((T, D), F32),
                   jax.ShapeDtypeStruct((512, T), F32), jax.ShapeDtypeStruct((512, T), F32),
                   jax.ShapeDtypeStruct((D, D), F32)),
        compiler_params=_cparams(1),
    )(mix_r, mix_s, wo, x, target)


LOSS_ROW = 5
ALL_CHIPS = ((0, 0), (0, 1), (1, 0), (1, 1))


def _w_out_reduce_scatter(gwo):
    def body(gwo_ref, gout_o, own, rcv, snd, got, tot, local_sems, a_send, a_recv, b_send, b_recv):
        k = pl.program_id(0)
        x_, y_, c_ = _mesh_pos()
        sibling = (x_, y_, 1 - c_)
        rel_chips = [(1 - x_, y_), (x_, 1 - y_), (1 - x_, 1 - y_)]

        def to_sibling(j):
            px, py = ALL_CHIPS[j]
            return pltpu.make_async_remote_copy(src_ref=gwo_ref.at[_blk(px, py, 1 - c_)], dst_ref=rcv.at[j],
                                                send_sem=a_send.at[j], recv_sem=a_recv.at[j], device_id=sibling,
                                                device_id_type=MESH)

        def local(j):
            px, py = ALL_CHIPS[j]
            return pltpu.make_async_copy(gwo_ref.at[_blk(px, py, c_)], own.at[j], local_sems.at[j])

        def to_chip(q):
            return pltpu.make_async_remote_copy(src_ref=snd.at[q], dst_ref=got.at[q], send_sem=b_send.at[q],
                                                recv_sem=b_recv.at[q], device_id=(*rel_chips[q], c_), device_id_type=MESH)

        @pl.when(k == 0)
        def _():
            for j in range(len(ALL_CHIPS)):
                to_sibling(j).start()
                local(j).start()

        @pl.when(k == 2)
        def _():
            for j in range(len(ALL_CHIPS)):
                local(j).wait()
                to_sibling(j).wait_recv()
            for q in range(3):
                j = 2 * rel_chips[q][0] + rel_chips[q][1]
                snd[q] = _bf(own[j] + rcv[j])
                to_chip(q).start()
            jm = 2 * x_ + y_
            tot[...] = own[jm] + rcv[jm]

        @pl.when(k == N_CH)
        def _():
            g = tot[...]
            for q in range(3):
                to_chip(q).wait_recv()
                g = g + got[q].astype(F32)
            gout_o[...] = g
            for j in range(len(ALL_CHIPS)):
                to_sibling(j).wait_send()
            for q in range(3):
                to_chip(q).wait_send()

    dma = pltpu.SemaphoreType.DMA
    return _Part(
        body, inputs=[gwo], in_specs=[pl.BlockSpec(memory_space=pl.ANY)],
        out_specs=[pl.BlockSpec((WOUT_BLK, D), lambda k: (0, 0))], out_shape=[jax.ShapeDtypeStruct((WOUT_BLK, D), F32)],
        scratch_shapes=[pltpu.VMEM((4, WOUT_BLK, D), F32), pltpu.VMEM((4, WOUT_BLK, D), F32),
                        pltpu.VMEM((3, WOUT_BLK, D), BF16), pltpu.VMEM((3, WOUT_BLK, D), BF16),
                        pltpu.VMEM((WOUT_BLK, D), F32), dma((4,)), dma((4,)), dma((4,)), dma((3,)), dma((3,))])


def _in_proj_bwd_rs(d_ret, d_swa, h, wt, x, norm_w, dy, sse, small_acc):
    n_blocks, n_tiles = N_DEV, T // TM
    last = n_blocks + n_tiles - 1

    def body(da_any, db_any, da_ref, db_ref, h_any, w_any, x_ref, nw_ref, dy_ref, sse_ref,
             dbias_ref, bucket_ref, dsink_ref, dqw_ref, dkw_ref, drw_ref, gx_ref, gin_o, suma_o, sumb_o,
             stage, d2d_src, mine, rcv, snd, got, tot, tab_a, tab_b, gnw_acc, rw_ref, qw_ref, kw_ref, sk_ref, rb_ref,
             h_ref, w_ref, dp_sems, d2d_send, d2d_recv, ici_send, ici_recv, s_send, s_recv, hw_sems):
        k = pl.program_id(0)
        x_, y_, c_ = _mesh_pos()
        h_load = pltpu.make_async_copy(h_any, h_ref, hw_sems.at[0])
        w_load = pltpu.make_async_copy(w_any, w_ref, hw_sems.at[1])
        me = _blk(x_, y_, c_)
        sibling = (x_, y_, 1 - c_)
        rel_chips = [(1 - x_, y_), (x_, 1 - y_), (1 - x_, 1 - y_), (x_, y_)]

        def block_of_step(s):
            return _blk(*rel_chips[s // 2], 1 - c_ if s % 2 == 0 else c_)

        def fetch(s, wait):
            slot, b = s % 2, block_of_step(s)
            split = RET_W - 4 * WIN_BLK

            def run(src, dst, sem):
                cp = pltpu.make_async_copy(src, dst, sem)
                cp.wait() if wait else cp.start()

            @pl.when(b < 4)
            def _():
                run(da_any.at[pl.ds(pl.multiple_of(b * WIN_BLK, 16), WIN_BLK), :], stage.at[slot], dp_sems.at[slot, 0])

            @pl.when(b == 4)
            def _():
                run(da_any.at[pl.ds(4 * WIN_BLK, split), :], stage.at[slot, pl.ds(0, split), :], dp_sems.at[slot, 0])
                run(db_any.at[pl.ds(0, WIN_BLK - split), :], stage.at[slot, pl.ds(split, WIN_BLK - split), :],
                    dp_sems.at[slot, 1])

            @pl.when(b > 4)
            def _():
                run(db_any.at[pl.ds(pl.multiple_of(b * WIN_BLK - RET_W, 16), WIN_BLK), :], stage.at[slot],
                    dp_sems.at[slot, 0])

        def d2d_copy(r):
            return pltpu.make_async_remote_copy(src_ref=d2d_src, dst_ref=rcv.at[r], send_sem=d2d_send.at[r],
                                                recv_sem=d2d_recv.at[r], device_id=sibling, device_id_type=MESH)

        def ici_copy(r):
            return pltpu.make_async_remote_copy(src_ref=snd.at[r], dst_ref=got.at[r], send_sem=ici_send.at[r],
                                                recv_sem=ici_recv.at[r], device_id=(*rel_chips[r], c_),
                                                device_id_type=MESH)

        def table_copies():
            return [pltpu.make_async_remote_copy(src_ref=tab.at[me], dst_ref=tab.at[me], send_sem=s_send.at[a, p - 1],
                                                 recv_sem=s_recv.at[a, p - 1],
                                                 device_id=(x_ ^ (p >> 2), y_ ^ ((p >> 1) & 1), c_ ^ (p & 1)),
                                                 device_id_type=MESH)
                    for p in range(1, N_DEV) for a, tab in enumerate((tab_a, tab_b))]

        def chip_sum(r):
            d2d_copy(r).wait_recv()
            total = mine[...] + rcv[r]
            if r < 3:
                snd[r] = _bf(total)
                ici_copy(r).start()
            else:
                tot[...] = total

        for s in range(n_blocks):
            @pl.when(k == s)
            def _(s=s):
                r = s // 2
                if s == 0:
                    gnw_acc[...] = jnp.zeros_like(gnw_acc)
                    h_load.start()
                    fetch(0, wait=False)
                    w_load.start()
                fetch(s, wait=True)
                if s + 1 < n_blocks:
                    fetch(s + 1, wait=False)
                if s == 0:
                    h_load.wait()
                gw = _dot(stage[s % 2], h_ref[...])
                if s % 2 == 0:
                    if s >= 2:
                        chip_sum(r - 1)
                        d2d_copy(r - 1).wait_send()
                    d2d_src[...] = gw
                    d2d_copy(r).start()
                else:
                    mine[...] = gw

        @pl.when(k == n_blocks)
        def _():
            chip_sum(3)
            w_load.wait()

        @pl.when(k >= n_blocks)
        def _():
            dp = jnp.concatenate([da_ref[...], db_ref[...]], axis=0)
            xv, nw = x_ref[...], nw_ref[...]
            r = lax.rsqrt(jnp.mean(xv * xv, axis=-1, keepdims=True) + NORM_EPS)
            dh = lax.dot_general(dp, w_ref[...], (((0,), (0,)), ((), ())), preferred_element_type=F32)
            u = dh * nw
            gx_ref[...] = dy_ref[...] + r * u - xv * (r * r * r) * jnp.mean(u * xv, axis=-1, keepdims=True)
            gnw_acc[...] += jnp.sum(dh * (xv * r), axis=0, keepdims=True)

        @pl.when(k == n_blocks)
        def _():
            _finish_small_grads(dbias_ref, bucket_ref, dsink_ref, dqw_ref, dkw_ref, drw_ref,
                                rw_ref, qw_ref, kw_ref, sk_ref, rb_ref)

        @pl.when(k == last)
        def _():
            mine_a, mine_b = tab_a.at[me], tab_b.at[me]
            mine_a[...] = jnp.zeros_like(mine_a)
            mine_b[...] = jnp.zeros_like(mine_b)
            for row, ref in enumerate((gnw_acc, rw_ref, qw_ref, kw_ref, sk_ref)):
                mine_a[row:row + 1, 0:ref.shape[1]] = ref[...]
            mine_a[LOSS_ROW:LOSS_ROW + 1, 0:1] = sse_ref[...]
            mine_b[:, 0:N_BUCKETS] = rb_ref[...]
            tables = table_copies()
            for cp in tables:
                cp.start()
            g_in = tot[...]
            for q in range(3):
                ici_copy(q).wait_recv()
                g_in = g_in + got[q].astype(F32)
            gin_o[...] = g_in
            for cp in tables:
                cp.wait_recv()
            sum_a, sum_b = tab_a[0], tab_b[0]
            for b in range(1, N_DEV):
                sum_a = sum_a + tab_a[b]
                sum_b = sum_b + tab_b[b]
            suma_o[...] = sum_a
            sumb_o[...] = sum_b
            d2d_copy(3).wait_send()
            for q in range(3):
                ici_copy(q).wait_send()
            for cp in tables:
                cp.wait_send()

    tile = lambda k: jnp.maximum(k - n_blocks, 0)
    row = lambda w: pl.BlockSpec((TM, w), lambda k: (tile(k), 0))
    col = lambda w: pl.BlockSpec((w, TM), lambda k: (0, tile(k)))
    const = lambda shape, **kw: pl.BlockSpec(shape, lambda k: (0,) * len(shape), **kw)
    once = dict(pipeline_mode=pl.Buffered(1))
    hbm = pl.BlockSpec(memory_space=pl.ANY)
    dma = pltpu.SemaphoreType.DMA
    return pl.pallas_call(
        body, name="in_proj_bwd_rs", grid=(n_blocks + n_tiles,),
        in_specs=[hbm, hbm, col(RET_W), col(SWA_W), hbm, hbm,
                  row(D), const((1, D)), row(D), const((1, 1))]
                 + [const(a.shape) for a in small_acc],
        out_specs=(row(D), const((WIN_BLK, D)), const((8, D)), const((SWA_H, CH))),
        out_shape=(jax.ShapeDtypeStruct((T, D), F32), jax.ShapeDtypeStruct((WIN_BLK, D), F32),
                   jax.ShapeDtypeStruct((8, D), F32), jax.ShapeDtypeStruct((SWA_H, CH), F32)),
        scratch_shapes=[
            pltpu.VMEM((2, WIN_BLK, T), BF16), pltpu.VMEM((WIN_BLK, D), F32),
            pltpu.VMEM((WIN_BLK, D), F32),
            pltpu.VMEM((4, WIN_BLK, D), F32), pltpu.VMEM((3, WIN_BLK, D), BF16),
            pltpu.VMEM((3, WIN_BLK, D), BF16), pltpu.VMEM((WIN_BLK, D), F32),
            pltpu.VMEM((N_DEV, 8, D), F32), pltpu.VMEM((N_DEV, SWA_H, CH), F32),
            pltpu.VMEM((1, D), F32),
        ] + [pltpu.VMEM(s, F32) for s in SMALL_GRAD_SHAPES] + [
            pltpu.VMEM((T, D), BF16), pltpu.VMEM((D_IN, D), BF16),
            dma((2, 2)), dma((4,)), dma((4,)), dma((3,)), dma((3,)), dma((2, 7)), dma((2, 7)), dma((2,)),
        ],
        compiler_params=_cparams(1),
    )(d_ret, d_swa, d_ret, d_swa, h, wt, x, norm_w, dy, sse, *small_acc)


SMALL_SHAPES = ((1, D), (1, 512), (1, SWA_D), (1, SWA_D), (1, SWA_H), (SWA_H, N_BUCKETS))


def _adamw_all(g_in_t, g_out, sum_a, sum_b, w_in_t, m_in_t, v_in_t, w_out, m_out, v_out, small_w, small_m, small_v):
    n_small = len(SMALL_SHAPES)
    halves = 2

    def body(*refs):
        gin_any, gout_any, suma_any, sumb_any, w_in_ref, m_in_ref, v_in_ref, w_out_ref, m_out_ref, v_out_ref = refs[:10]
        pos = 10
        sw_refs, sm_refs, sv_refs = (refs[pos + i * n_small:pos + (i + 1) * n_small] for i in range(3))
        pos += 3 * n_small
        g_in_o, d_in_o, nm_in_o, nv_in_o, g_out_o, d_out_o, nm_out_o, nv_out_o = refs[pos:pos + 8]
        pos += 8
        sg_o, sd_o, snm_o, snv_o = (refs[pos + i * n_small:pos + (i + 1) * n_small] for i in range(4))
        pos += 4 * n_small
        loss_o, gin_ref, gout_ref, suma_ref, sumb_ref, load_sems = refs[pos:]
        i = pl.program_id(0)

        @pl.when(i == 0)
        def _():
            loads = [pltpu.make_async_copy(src, dst, load_sems.at[j]) for j, (src, dst) in enumerate(
                ((gin_any, gin_ref), (gout_any, gout_ref), (suma_any, suma_ref), (sumb_any, sumb_ref)))]
            for cp in loads:
                cp.start()
            for cp in loads:
                cp.wait()

        g = gout_ref[pl.ds(pl.multiple_of(i * (WOUT_BLK // halves), 8), WOUT_BLK // halves), :]
        d, nm, nv = _adamw(w_out_ref[...], g, m_out_ref[...], v_out_ref[...])
        g_out_o[...], d_out_o[...], nm_out_o[...], nv_out_o[...] = g, d, nm, nv
        g = gin_ref[pl.ds(pl.multiple_of(i * (WIN_BLK // halves), 8), WIN_BLK // halves), :]
        d, nm, nv = _adamw(w_in_ref[...], g, m_in_ref[...], v_in_ref[...])
        g_in_o[...], d_in_o[...], nm_in_o[...], nv_in_o[...] = g, d, nm, nv

        @pl.when(pl.program_id(0) == 0)
        def _():
            loss_o[...] = suma_ref[LOSS_ROW:LOSS_ROW + 1, 0:1] * (0.5 / D)
            for r, (rows, lanes) in enumerate(SMALL_SHAPES):
                g = suma_ref[r:r + 1, 0:lanes] if rows == 1 else sumb_ref[:, 0:lanes]
                d, nm, nv = _adamw(sw_refs[r][...], g, sm_refs[r][...], sv_refs[r][...])
                sg_o[r][...], sd_o[r][...], snm_o[r][...], snv_o[r][...] = g, d, nm, nv

    half = lambda rows: pl.BlockSpec((rows // halves, D), lambda i: (i, 0))
    const = lambda shape: pl.BlockSpec(shape, lambda i: (0,) * len(shape))
    win = jax.ShapeDtypeStruct((WIN_BLK, D), F32)
    wout = jax.ShapeDtypeStruct((WOUT_BLK, D), F32)
    smalls = tuple(jax.ShapeDtypeStruct(s, F32) for s in SMALL_SHAPES)
    small_specs = [const(s) for s in SMALL_SHAPES]
    outs = pl.pallas_call(
        body, name="adamw_all", grid=(halves,),
        in_specs=[pl.BlockSpec(memory_space=pl.ANY)] * 4 + [half(WIN_BLK)] * 3 + [half(WOUT_BLK)] * 3 + small_specs * 3,
        out_specs=tuple([half(WIN_BLK)] * 4 + [half(WOUT_BLK)] * 4 + small_specs * 4 + [const((1, 1))]),
        out_shape=(win,) * 4 + (wout,) * 4 + smalls * 4 + (jax.ShapeDtypeStruct((1, 1), F32),),
        scratch_shapes=[pltpu.VMEM((WIN_BLK, D), F32), pltpu.VMEM((WOUT_BLK, D), F32), pltpu.VMEM((8, D), F32),
                        pltpu.VMEM((SWA_H, CH), F32), pltpu.SemaphoreType.DMA((4,))],
        compiler_params=_cparams(1, vmem=32 * 1024 * 1024),
    )(g_in_t, g_out, sum_a, sum_b, w_in_t, m_in_t, v_in_t, w_out, m_out, v_out, *small_w, *small_m, *small_v)
    big, rest = outs[:8], outs[8:]
    return big, [rest[i * n_small:(i + 1) * n_small] for i in range(4)], rest[4 * n_small]


def _small_rows(norm_w, ret_norm_w, q_norm_w, k_norm_w, sinks, rel_bias):
    return (norm_w.reshape(1, D), ret_norm_w.reshape(1, 512), q_norm_w.reshape(1, SWA_D), k_norm_w.reshape(1, SWA_D),
            sinks.reshape(1, SWA_H), rel_bias.T)


def _small_leaves(rows):
    return (rows[0].reshape(D), rows[1].reshape(512), rows[2].reshape(SWA_D), rows[3].reshape(SWA_D),
            rows[4].reshape(SWA_H), rows[5].T)


def kernel(x, norm_w, w_in, ret_norm_w, q_norm_w, k_norm_w, sinks, rel_bias, w_out, loss_target, m_norm_w, m_w_in, m_ret_norm_w, m_q_norm_w, m_k_norm_w, m_sinks, m_rel_bias, m_w_out, v_norm_w, v_w_in, v_ret_norm_w, v_q_norm_w, v_k_norm_w, v_sinks, v_rel_bias, v_w_out):
    x2 = x.reshape(T, D)
    target = loss_target.reshape(T, D)
    nw = norm_w.reshape(1, D)
    rnw_col = ret_norm_w.reshape(RET_V, 1)
    qnw_col = q_norm_w.reshape(SWA_D, 1)
    knw_col = k_norm_w.reshape(SWA_D, 1)
    relb = rel_bias.T.reshape(SWA_H * N_BUCKETS)
    ret_tables = _rotary_tables_t() + _retention_tables_t()
    bucket_t = _bucket_table_t()

    proj, wt, h = _in_proj_gather(x2, nw, w_in.T)
    (ro, mix_r, states, wo), (so, mix_s) = _fused_call("attn_fwd", N_CH, [
        _retention_fwd(proj, rnw_col, ret_tables, w_out),
        _swa_fwd(proj, qnw_col, knw_col, sinks, relb, bucket_t)])
    sse, dy, dmix_r, dmix_s, gwo = _out_proj(mix_r, mix_s, wo.reshape(D, D), x2, target)

    (d_ret, drw_acc), (d_swa, dbias, dsink_acc, dqw_acc, dkw_acc), (g_out,) = _fused_call("attn_bwd", N_CH + 1, [
        _retention_bwd(proj, ro, states, dmix_r, rnw_col, ret_tables),
        _swa_bwd(proj, so, dmix_s, qnw_col, knw_col, sinks, relb, bucket_t),
        _w_out_reduce_scatter(gwo.reshape(N_DEV, WOUT_BLK, D))])
    grad_x, g_in_t, sum_a, sum_b = _in_proj_bwd_rs(d_ret, d_swa, h, wt, x2, nw, dy, sse,
                                                   (dbias, bucket_t, dsink_acc, dqw_acc, dkw_acc, drw_acc))

    small_w = _small_rows(norm_w, ret_norm_w, q_norm_w, k_norm_w, sinks, rel_bias)
    small_m = _small_rows(m_norm_w, m_ret_norm_w, m_q_norm_w, m_k_norm_w, m_sinks, m_rel_bias)
    small_v = _small_rows(v_norm_w, v_ret_norm_w, v_q_norm_w, v_k_norm_w, v_sinks, v_rel_bias)
    big, small, loss = _adamw_all(g_in_t, g_out, sum_a, sum_b, w_in.T, m_w_in.T, v_w_in.T, w_out, m_w_out, v_w_out,
                                  small_w, small_m, small_v)

    def leaves(i):
        a = _small_leaves(small[i])
        return (a[0], big[i].T, a[1], a[2], a[3], a[4], a[5], big[4 + i])

    return (loss.reshape(()), grad_x.reshape(1, T, D), *leaves(0), *leaves(1), *leaves(2), *leaves(3))
```

```python
from typing import Callable, NamedTuple

import numpy as np
import jax
import jax.numpy as jnp
from jax import lax
from jax.experimental import pallas as pl
from jax.experimental.pallas import tpu as pltpu

F32 = jnp.float32
BF16 = jnp.bfloat16
MESH = pl.DeviceIdType.MESH

T = 2048
D = 1024
D_IN = 2816
N_DEV = 8
WIN_BLK = D_IN // N_DEV
WOUT_BLK = D // N_DEV
CH = 128
N_CH = T // CH
RET_H, RET_DK, RET_DV = 4, 64, 128
RET_QK = RET_H * RET_DK
RET_V = RET_H * RET_DV
SWA_H, SWA_KV, SWA_D, SWA_G = 8, 2, 64, 4
N_BUCKETS = 32
NORM_EPS = 1e-6
GN_EPS = 1e-5
NEG_INF = -1e30
PIECES = (256, 256, 512, 512, 512, 128, 128, 512)
OFFS = tuple(int(v) for v in np.cumsum((0,) + PIECES))
RET_W = OFFS[4]
SWA_W = D_IN - RET_W
SWA_OFFS = tuple(o - RET_W for o in OFFS[4:])
TM = 256

ADAM_LR, ADAM_B1, ADAM_B2, ADAM_EPS, ADAM_WD, ADAM_STEP = 0.001, 0.9, 0.999, 1e-08, 0.01, 10

VMEM_LIMIT = 56 * 1024 * 1024


def _cparams(n_grid=0, vmem=VMEM_LIMIT):
    sem = ("arbitrary",) * n_grid if n_grid else None
    return pltpu.CompilerParams(dimension_semantics=sem, vmem_limit_bytes=vmem)


class _Part(NamedTuple):
    step: Callable
    inputs: list
    in_specs: list
    out_specs: list
    out_shape: list
    scratch_shapes: list


def _fused_call(name, n_steps, parts):
    n_in = [len(p.inputs) for p in parts]
    n_out = [len(p.out_shape) for p in parts]
    n_scr = [len(p.scratch_shapes) for p in parts]

    def body(*refs):
        ins, outs, scr = refs[:sum(n_in)], refs[sum(n_in):sum(n_in) + sum(n_out)], refs[sum(n_in) + sum(n_out):]
        for i, p in enumerate(parts):
            take = lambda seq, counts: seq[sum(counts[:i]):sum(counts[:i + 1])]
            p.step(*take(ins, n_in), *take(outs, n_out), *take(scr, n_scr))

    flat = lambda field: [v for p in parts for v in getattr(p, field)]
    outs = pl.pallas_call(
        body, name=name, grid=(n_steps,), in_specs=flat("in_specs"), out_specs=tuple(flat("out_specs")),
        out_shape=tuple(flat("out_shape")), scratch_shapes=flat("scratch_shapes"), compiler_params=_cparams(1),
    )(*flat("inputs"))
    return [list(outs[sum(n_out[:i]):sum(n_out[:i + 1])]) for i in range(len(parts))]


def _dot(a, b):
    return jnp.dot(a, b, preferred_element_type=F32)


def _dot_nt(a, b):
    return lax.dot_general(a, b, (((1,), (1,)), ((), ())), preferred_element_type=F32)


def _bf(a):
    return a.astype(BF16)


def _sigmoid(x):
    return 1.0 / (1.0 + jnp.exp(-x))


def _adamw(w, g, m, v):
    m = ADAM_B1 * m + (1.0 - ADAM_B1) * g
    v = ADAM_B2 * v + (1.0 - ADAM_B2) * (g * g)
    m_hat = m / (1.0 - ADAM_B1 ** ADAM_STEP)
    v_hat = v / (1.0 - ADAM_B2 ** ADAM_STEP)
    delta = -ADAM_LR * (m_hat / (jnp.sqrt(v_hat) + ADAM_EPS) + ADAM_WD * w)
    return delta, m, v


def _rotary_tables_t():
    half = RET_DK // 2
    inv_freq = np.float32(10000.0) ** (-np.arange(half, dtype=np.float32) / np.float32(half))
    ang = inv_freq[:, None] * np.arange(T, dtype=np.float32)[None, :]
    cos, sin = np.cos(ang).astype(np.float32), np.sin(ang).astype(np.float32)
    cos64 = np.concatenate([cos, cos], axis=0)
    sin64 = np.concatenate([-sin, sin], axis=0)
    return np.tile(cos64, (RET_H, 1)), np.tile(sin64, (RET_H, 1))


def _retention_tables_t():
    gamma = (1.0 - np.exp2(-5.0 - np.arange(RET_H, dtype=np.float32))).astype(np.float32)
    log_g = np.log(gamma).astype(np.float32)
    i = np.arange(CH, dtype=np.float32)
    diff = i[None, :] - i[:, None]
    decay = np.where(diff >= 0, np.exp(log_g[:, None, None] * np.maximum(diff, 0.0)), 0.0).astype(np.float32)
    decay_all = np.concatenate(list(decay), axis=1)
    zeta = np.exp(log_g[:, None] * (CH - 1.0 - i)).astype(np.float32)
    zeta_tab = np.repeat(zeta.T, RET_DK, axis=1)
    xi = np.exp(log_g[:, None] * (i + 1.0)).astype(np.float32)
    xi_tab = np.repeat(xi, RET_DK, axis=0)
    chunk_decay = np.exp(log_g * np.float32(CH)).astype(np.float32)
    row_head = np.arange(RET_V)[:, None] // RET_DV
    col_head = np.arange(RET_QK)[None, :] // RET_DK
    state_mask = (row_head == col_head).astype(np.float32)
    state_decay = (state_mask * chunk_decay[row_head]).astype(np.float32)
    q_mask = (np.arange(RET_QK)[:, None] // RET_DK == np.arange(RET_H * CH)[None, :] // CH).astype(np.float32)
    return (np.ascontiguousarray(decay_all), np.ascontiguousarray(zeta_tab), np.ascontiguousarray(xi_tab),
            state_mask, state_decay, q_mask)


def _bucket_table_t():
    qi = np.arange(CH)[:, None]
    kj = np.arange(2 * CH)[None, :]
    dist = qi + CH - kj
    n = np.maximum(dist, 0)
    max_exact = N_BUCKETS // 2
    nf = np.maximum(n, 1).astype(np.float32)
    large = max_exact + (np.log(nf / np.float32(max_exact)) / np.float32(np.log(CH / max_exact))
                         * np.float32(N_BUCKETS - max_exact)).astype(np.int32)
    large = np.minimum(large, N_BUCKETS - 1)
    bucket = np.where(n < max_exact, n, large)
    return np.ascontiguousarray(np.where((dist >= 0) & (dist < CH), bucket, -1).astype(np.int32).T)


def _mesh_pos():
    return lax.axis_index("x"), lax.axis_index("y"), lax.axis_index("c")


def _blk(px, py, pc):
    return 4 * px + 2 * py + pc


def _in_proj_gather(x, norm_w, w_in_t):
    tchunk = 512

    def body(x_any, nw_any, win_any, proj_ref, wt_out, h_out, wt_ref, h_ref, stage, x_ref, nw_ref, win_ref,
             send_sems, recv_sems, out_sems, wb_sems, h_sem, in_sems):
        n_tiles = T // TM
        loads = [pltpu.make_async_copy(win_any, win_ref, in_sems.at[0]),
                 pltpu.make_async_copy(nw_any, nw_ref, in_sems.at[1])]
        loads += [pltpu.make_async_copy(x_any.at[pl.ds(i * TM, TM), :], x_ref.at[pl.ds(i * TM, TM), :], in_sems.at[2 + i])
                  for i in range(n_tiles)]
        for cp in loads:
            cp.start()
        loads[0].wait()
        x, y, c = _mesh_pos()
        me = _blk(x, y, c)
        sibling = (x, y, 1 - c)
        nbr_a, nbr_b, diag = (x ^ (1 - c), y ^ c), (x ^ c, y ^ (1 - c)), (1 - x, 1 - y)

        def copy(k, b, to):
            return pltpu.make_async_remote_copy(src_ref=wt_ref.at[b], dst_ref=wt_ref.at[b], send_sem=send_sems.at[k],
                                                recv_sem=recv_sems.at[k], device_id=to, device_id_type=MESH)

        wt_ref[me] = _bf(win_ref[...])
        first = [copy(0, me, sibling), copy(1, me, (*nbr_a, c))]
        for cp in first:
            cp.start()
        to_b = copy(2, me, (*nbr_b, c))

        loads[1].wait()
        nw = nw_ref[...]
        for i in range(n_tiles):
            rows = slice(i * TM, (i + 1) * TM)
            loads[2 + i].wait()
            xv = x_ref[rows, :]
            r = lax.rsqrt(jnp.mean(xv * xv, axis=-1, keepdims=True) + NORM_EPS)
            h_ref[rows, :] = _bf(xv * r * nw)
        keep = [pltpu.make_async_copy(h_ref, h_out, h_sem)]
        keep[0].start()

        writes = []

        def project(b):
            k = len(writes)
            if k >= 2:
                writes[k - 2].wait()
            keep.append(pltpu.make_async_copy(wt_ref.at[b], wt_out.at[b], wb_sems.at[k]))
            keep[-1].start()
            w = wt_ref[b]
            for t in range(T // tchunk):
                cols = slice(t * tchunk, (t + 1) * tchunk)
                stage[k % 2, :, cols] = _dot_nt(w, h_ref[cols, :])
            cp = pltpu.make_async_copy(stage.at[k % 2], proj_ref.at[b], out_sems.at[k % 2])
            cp.start()
            writes.append(cp)

        here = (x, y, c)
        project(me)
        copy(0, _blk(x, y, 1 - c), here).wait_recv()
        project(_blk(x, y, 1 - c))
        to_b.start()
        passed = [to_b]

        def landed(k, chip, relay_to=None):
            copy(k, _blk(*chip, c), here).wait_recv()
            out = [copy({1: 4, 2: 5, 3: 6}[k], _blk(*chip, c), sibling)]
            if relay_to is not None:
                out.append(copy(3, _blk(*chip, c), (*relay_to, c)))
            for cp in out:
                cp.start()
            passed.extend(out)
            project(_blk(*chip, c))

        def from_sibling(k, chip):
            copy(k, _blk(*chip, 1 - c), here).wait_recv()
            project(_blk(*chip, 1 - c))

        landed(1, nbr_a, relay_to=nbr_b)
        from_sibling(4, nbr_b)
        landed(2, nbr_b)
        from_sibling(5, nbr_a)
        landed(3, diag)
        from_sibling(6, diag)
        for cp in writes[-2:] + keep:
            cp.wait()
        for cp in first + passed:
            cp.wait_send()

    vm = pl.BlockSpec(memory_space=pltpu.VMEM)
    hbm = pl.BlockSpec(memory_space=pltpu.HBM)
    dma = pltpu.SemaphoreType.DMA
    proj, wt, h = pl.pallas_call(
        body, name="in_proj_gather",
        out_shape=(jax.ShapeDtypeStruct((N_DEV, WIN_BLK, T), F32), jax.ShapeDtypeStruct((N_DEV, WIN_BLK, D), BF16),
                   jax.ShapeDtypeStruct((T, D), BF16)),
        in_specs=[hbm, hbm, hbm], out_specs=(hbm, hbm, hbm),
        scratch_shapes=[pltpu.VMEM((N_DEV, WIN_BLK, D), BF16), pltpu.VMEM((T, D), BF16), pltpu.VMEM((2, WIN_BLK, T), F32),
                        pltpu.VMEM((T, D), F32), pltpu.VMEM((1, D), F32), pltpu.VMEM((WIN_BLK, D), F32),
                        dma((7,)), dma((7,)), dma((2,)), dma((N_DEV,)), dma, dma((2 + T // TM,))],
        compiler_params=_cparams(vmem=48 * 1024 * 1024),
    )(x, norm_w, w_in_t)
    return proj.reshape(D_IN, T), wt.reshape(D_IN, D), h


def _gather_w_out(w_out_ref, own_ref, wo_ref, local_sem, send_sems, recv_sems, phase):
    x, y, c = _mesh_pos()
    me = _blk(x, y, c)
    sibling = (x, y, 1 - c)
    chips = [(1 - x, y), (x, 1 - y), (1 - x, 1 - y)]

    def copy(k, b, to, src=None):
        return pltpu.make_async_remote_copy(src_ref=wo_ref.at[b] if src is None else src, dst_ref=wo_ref.at[b],
                                            send_sem=send_sems.at[k], recv_sem=recv_sems.at[k], device_id=to,
                                            device_id_type=MESH)

    local = pltpu.make_async_copy(own_ref, wo_ref.at[me], local_sem)
    first = [copy(0, me, sibling, own_ref)] + [copy(1 + j, me, (*chip, c), own_ref) for j, chip in enumerate(chips)]
    passed = [copy(4 + j, _blk(*chip, c), sibling) for j, chip in enumerate(chips)]
    if phase == 0:
        own_ref[...] = _bf(w_out_ref[...])
        local.start()
        for cp in first:
            cp.start()
    elif phase == 1:
        for j, chip in enumerate(chips):
            copy(1 + j, _blk(*chip, c), (x, y, c)).wait_recv()
            passed[j].start()
    else:
        local.wait()
        copy(0, _blk(x, y, 1 - c), (x, y, c)).wait_recv()
        for j, chip in enumerate(chips):
            copy(4 + j, _blk(*chip, 1 - c), (x, y, c)).wait_recv()
        for cp in first + passed:
            cp.wait_send()


def _proj_piece(i, block_of):
    rows = PIECES[i]
    assert OFFS[i] % rows == 0
    return pl.BlockSpec((rows, CH), lambda n: (OFFS[i] // rows, block_of(n)))


def _proj_gate_halves(block_of):
    return [pl.BlockSpec((256, CH), lambda n, j=j: (OFFS[7] // 256 + j, block_of(n))) for j in range(2)]


def _swap_halves_t(t):
    half = RET_DK // 2
    parts = []
    for h in range(RET_H):
        parts += [t[h * RET_DK + half:(h + 1) * RET_DK], t[h * RET_DK:h * RET_DK + half]]
    return jnp.concatenate(parts, axis=0)


def _rotate(t, cos, sin):
    return t * cos + _swap_halves_t(t) * sin


def _group_norm_t(o):
    mu = jnp.mean(o, axis=0, keepdims=True)
    var = jnp.mean((o - mu) * (o - mu), axis=0, keepdims=True)
    rstd = lax.rsqrt(var + GN_EPS)
    return (o - mu) * rstd, rstd


def _retention_scores_t(q_t, k_rm_b, q_mask, decay_all):
    q_heads = _bf(jnp.concatenate([q_t] * RET_H, axis=1) * q_mask)
    return _dot(k_rm_b, q_heads) * decay_all


def _retention_fwd(proj, ret_norm_w, tables, w_out_blk):
    cos_t, sin_t, decay_all, zeta_tab, xi_tab, state_mask, state_decay, q_mask = tables

    def body(rq_ref, rk_ref, rv_ref, rg_ref, w_ref, cos_ref, sin_ref, decay_ref, zeta_ref, xi_ref, smask_ref,
             sdecay_ref, qmask_ref, wout_ref, ro_ref, mix_ref, st_ref, wo_ref, state, wo_own, wo_local, wo_send, wo_recv):
        n = pl.program_id(0)

        @pl.when(n == 0)
        def _():
            state[...] = jnp.zeros_like(state)
            _gather_w_out(wout_ref, wo_own, wo_ref, wo_local, wo_send, wo_recv, phase=0)

        @pl.when(n == N_CH // 2)
        def _():
            _gather_w_out(wout_ref, wo_own, wo_ref, wo_local, wo_send, wo_recv, phase=1)

        @pl.when(n == N_CH - 1)
        def _():
            _gather_w_out(wout_ref, wo_own, wo_ref, wo_local, wo_send, wo_recv, phase=2)

        cos, sin = cos_ref[...], sin_ref[...]
        q_t = _rotate(rq_ref[...], cos, sin)
        k_t = _rotate(rk_ref[...], cos, sin) * (RET_DK ** -0.5)
        k_rm = k_t.T
        v_b = _bf(rv_ref[...])
        m_b = _bf(state[...])
        st_ref[0] = m_b
        scores_b = _bf(_retention_scores_t(q_t, _bf(k_rm), qmask_ref[...], decay_ref[...]))
        cross = _dot(m_b, _bf(q_t * xi_ref[...]))
        state[...] = state[...] * sdecay_ref[...] + _dot(v_b, _bf(k_rm * zeta_ref[...])) * smask_ref[...]
        heads = range(RET_H)
        rows = [slice(h * RET_DV, (h + 1) * RET_DV) for h in heads]
        o = [_dot(v_b[rows[h]], scores_b[:, h * CH:(h + 1) * CH]) + cross[rows[h]] for h in heads]
        for h in heads:
            rn, _ = _group_norm_t(o[h])
            g = rg_ref[rows[h], :]
            ro_ref[rows[h], :] = o[h]
            mix_ref[rows[h], :] = rn * w_ref[rows[h], :] * (g * _sigmoid(g))

    col = lambda w: pl.BlockSpec((w, CH), lambda n: (0, n))
    const = lambda shape: pl.BlockSpec(shape, lambda n: (0,) * len(shape))
    cur = lambda n: n
    return _Part(
        body,
        inputs=[proj, proj, proj, proj, ret_norm_w, cos_t, sin_t, decay_all, zeta_tab, xi_tab, state_mask, state_decay,
                q_mask, w_out_blk],
        in_specs=[_proj_piece(i, cur) for i in range(4)] + [
            const((RET_V, 1)), col(RET_QK), col(RET_QK),
            const(decay_all.shape), const(zeta_tab.shape), const(xi_tab.shape), const(state_mask.shape),
            const(state_decay.shape), const(q_mask.shape), const((WOUT_BLK, D))],
        out_specs=[col(RET_V), col(RET_V), pl.BlockSpec((1, RET_V, RET_QK), lambda n: (n, 0, 0)),
                   pl.BlockSpec(memory_space=pl.ANY)],
        out_shape=[jax.ShapeDtypeStruct((RET_V, T), F32), jax.ShapeDtypeStruct((RET_V, T), F32),
                   jax.ShapeDtypeStruct((N_CH, RET_V, RET_QK), BF16), jax.ShapeDtypeStruct((N_DEV, WOUT_BLK, D), BF16)],
        scratch_shapes=[pltpu.VMEM((RET_V, RET_QK), F32), pltpu.VMEM((WOUT_BLK, D), BF16), pltpu.SemaphoreType.DMA,
                        pltpu.SemaphoreType.DMA((N_DEV - 1,)), pltpu.SemaphoreType.DMA((N_DEV - 1,))])


def _retention_bwd(proj, ro, states, dmix, ret_norm_w, tables):
    cos_t, sin_t, decay_all, zeta_tab, xi_tab, state_mask, state_decay, q_mask = tables

    def body(*refs):
        @pl.when(pl.program_id(0) < N_CH)
        def _():
            chunk(*refs)

    def chunk(rq_ref, rk_ref, rv_ref, rg_ref, ro_ref, st_ref, dm_ref, w_ref, cos_ref, sin_ref, decay_ref, zeta_ref,
              xi_ref, smask_ref, sdecay_ref, qmask_ref, d_ref, dw_ref, gstate):
        i = pl.program_id(0)

        @pl.when(i == 0)
        def _():
            gstate[...] = jnp.zeros_like(gstate)
            dw_ref[...] = jnp.zeros_like(dw_ref)

        cos, sin = cos_ref[...], sin_ref[...]
        q_t = _rotate(rq_ref[...], cos, sin)
        k_t = _rotate(rk_ref[...], cos, sin) * (RET_DK ** -0.5)
        q_b, k_b = _bf(q_t), _bf(k_t)
        k_rm = k_t.T
        kz_b = _bf(k_rm * zeta_ref[...])
        qx_b = _bf(q_t * xi_ref[...])
        v_t = rv_ref[...]
        v_b = _bf(v_t)
        v_rm_b = _bf(v_t.T)
        decay = decay_ref[...]
        scores_b = _bf(_retention_scores_t(q_t, _bf(k_rm), qmask_ref[...], decay))
        heads = range(RET_H)
        rows = [slice(h * RET_DV, (h + 1) * RET_DV) for h in heads]
        qk_rows = [slice(h * RET_DK, (h + 1) * RET_DK) for h in heads]
        lanes = [slice(h * CH, (h + 1) * CH) for h in heads]
        do = []
        for h in heads:
            g, w, dm = rg_ref[rows[h], :], w_ref[rows[h], :], dm_ref[rows[h], :]
            rn, rstd = _group_norm_t(ro_ref[rows[h], :])
            sig = _sigmoid(g)
            silu = g * sig
            d_ref[OFFS[3] + h * RET_DV:OFFS[3] + (h + 1) * RET_DV, :] = _bf(dm * rn * w * (sig * (1.0 + g * (1.0 - sig))))
            dw_ref[rows[h], :] += dm * silu * rn
            drn = dm * silu * w
            do.append(rstd * (drn - jnp.mean(drn, axis=0, keepdims=True)
                              - rn * jnp.mean(drn * rn, axis=0, keepdims=True)))
        do_b = _bf(jnp.concatenate(do, axis=0))
        m_b = st_ref[0]
        g_all = gstate[...]
        g_b = _bf(g_all)
        dscores_b = [_bf(_dot(v_rm_b[:, rows[h]], do_b[rows[h]]) * decay[:, lanes[h]]) for h in heads]
        dq_cross = lax.dot_general(m_b, do_b, (((0,), (0,)), ((), ())), preferred_element_type=F32)
        dkz = _dot(v_rm_b, g_b)
        dv_cross = _dot_nt(g_b, kz_b)
        gstate[...] = g_all * sdecay_ref[...] + _dot_nt(do_b, qx_b) * smask_ref[...]
        dq = jnp.concatenate([_dot(k_b[qk_rows[h]], dscores_b[h]) for h in heads], axis=0) + dq_cross * xi_ref[...]
        dk = (jnp.concatenate([_dot_nt(q_b[qk_rows[h]], dscores_b[h]) for h in heads], axis=0)
              + (dkz * zeta_ref[...]).T) * (RET_DK ** -0.5)
        for h in heads:
            d_ref[OFFS[2] + h * RET_DV:OFFS[2] + (h + 1) * RET_DV, :] = _bf(
                _dot_nt(do_b[rows[h]], scores_b[:, lanes[h]]) + dv_cross[rows[h]])
        d_ref[OFFS[0]:OFFS[1], :] = _bf(dq * cos + _swap_halves_t(dq * sin))
        d_ref[OFFS[1]:OFFS[2], :] = _bf(dk * cos + _swap_halves_t(dk * sin))

    chunk_of = lambda i: N_CH - 1 - jnp.minimum(i, N_CH - 1)
    col = lambda w: pl.BlockSpec((w, CH), lambda i: (0, chunk_of(i)))
    const = lambda shape: pl.BlockSpec(shape, lambda i: (0,) * len(shape))
    return _Part(
        body,
        inputs=[proj, proj, proj, proj, ro, states, dmix, ret_norm_w, cos_t, sin_t, decay_all, zeta_tab, xi_tab,
                state_mask, state_decay, q_mask],
        in_specs=[_proj_piece(j, chunk_of) for j in range(4)] + [
                  col(RET_V),
                  pl.BlockSpec((1, RET_V, RET_QK), lambda i: (chunk_of(i), 0, 0)), col(RET_V),
                  const((RET_V, 1)), col(RET_QK), col(RET_QK),
                  const(decay_all.shape), const(zeta_tab.shape), const(xi_tab.shape), const(state_mask.shape),
                  const(state_decay.shape), const(q_mask.shape)],
        out_specs=[col(RET_W), const((RET_V, CH))],
        out_shape=[jax.ShapeDtypeStruct((RET_W, T), BF16), jax.ShapeDtypeStruct((RET_V, CH), F32)],
        scratch_shapes=[pltpu.VMEM((RET_V, RET_QK), F32)])


HQ_LANES = SWA_H * CH


def _head_lanes(hq):
    return slice(hq * CH, (hq + 1) * CH)


def _build_bias_t(bucket_ref, relb_ref, sinks_ref, bias_ref, sink_row):
    bt = bucket_ref[...]
    first = lax.broadcasted_iota(jnp.int32, bt.shape, 0) < CH
    for hq in range(SWA_H):
        b = jnp.full(bt.shape, NEG_INF, F32)
        for bk in range(N_BUCKETS):
            b = jnp.where(bt == bk, relb_ref[hq, bk], b)
        bias_ref[0, :, _head_lanes(hq)] = b
        bias_ref[1, :, _head_lanes(hq)] = jnp.where(first, NEG_INF, b)
        sink_row[:, _head_lanes(hq)] = jnp.full((1, CH), sinks_ref[hq], F32)


def _rms_t(t, w_col):
    r = lax.rsqrt(jnp.mean(t * t, axis=0, keepdims=True) + NORM_EPS)
    return t * r * w_col, r


def _rms_t_bwd(dn, t, r, w_col):
    u = dn * w_col
    return r * u - t * (r * r * r) * jnp.mean(u * t, axis=0, keepdims=True), dn * t * r


def _norm_kv(k_t, kw):
    return jnp.concatenate([_rms_t(k_t[g * SWA_D:(g + 1) * SWA_D], kw)[0] for g in range(SWA_KV)], axis=0)


def _kv_slot(a, kv):
    z = jnp.zeros_like(a)
    return jnp.concatenate([a, z] if kv == 0 else [z, a], axis=0)


def _softmax_t(s, sink):
    m = jnp.maximum(jnp.max(s, axis=0, keepdims=True), sink)
    p = jnp.exp(s - m)
    e_sink = jnp.exp(sink - m)
    inv = 1.0 / (jnp.sum(p, axis=0, keepdims=True) + e_sink)
    return p * inv, e_sink * inv


def _gate_rows(sg_halves, hq):
    per_half = SWA_H // 2
    return sg_halves[hq // per_half][(hq % per_half) * SWA_D:(hq % per_half + 1) * SWA_D, :]


def _swa_fwd(proj, q_norm_w, k_norm_w, sinks, rel_bias_t, bucket_t):
    def body(sq_ref, skp_ref, skc_ref, svp_ref, svc_ref, sg_lo, sg_hi, qw_ref, kw_ref, sinks_ref, relb_ref, bucket_ref,
             so_ref, mix_ref, bias_ref, sink_row):
        n = pl.program_id(0)

        @pl.when(n == 0)
        def _():
            _build_bias_t(bucket_ref, relb_ref, sinks_ref, bias_ref, sink_row)

        var = (n == 0).astype(jnp.int32)
        qw, kw = qw_ref[...], kw_ref[...]
        kn_band = jnp.concatenate([_norm_kv(skp_ref[...], kw), _norm_kv(skc_ref[...], kw)], axis=1)
        kb_rm = _bf(kn_band.T)
        vband = _bf(jnp.concatenate([svp_ref[...], svc_ref[...]], axis=1))
        q_all = jnp.concatenate(
            [_bf(_kv_slot(_rms_t(sq_ref[hq * SWA_D:(hq + 1) * SWA_D, :], qw)[0] * (SWA_D ** -0.5), hq // SWA_G))
             for hq in range(SWA_H)], axis=1)
        probs, _ = _softmax_t(_dot(kb_rm, q_all) + bias_ref[var], sink_row[...])
        probs_b = _bf(probs)
        for kv in range(SWA_KV):
            o = _dot(vband[kv * SWA_D:(kv + 1) * SWA_D], probs_b[:, kv * SWA_G * CH:(kv + 1) * SWA_G * CH])
            for j in range(SWA_G):
                rows = slice((kv * SWA_G + j) * SWA_D, (kv * SWA_G + j + 1) * SWA_D)
                oh = o[:, j * CH:(j + 1) * CH]
                g = _gate_rows((sg_lo, sg_hi), kv * SWA_G + j)
                so_ref[rows, :] = oh
                mix_ref[rows, :] = oh * (g * _sigmoid(g))

    col = lambda w: pl.BlockSpec((w, CH), lambda n: (0, n))
    const = lambda shape: pl.BlockSpec(shape, lambda n: (0,) * len(shape))
    smem = pl.BlockSpec(memory_space=pltpu.SMEM)
    cur = lambda n: n
    prev = lambda n: jnp.maximum(n - 1, 0)
    return _Part(
        body,
        inputs=[proj, proj, proj, proj, proj, proj, proj, q_norm_w, k_norm_w, sinks, rel_bias_t, bucket_t],
        in_specs=[_proj_piece(4, cur), _proj_piece(5, prev), _proj_piece(5, cur), _proj_piece(6, prev),
                  _proj_piece(6, cur)] + _proj_gate_halves(cur) + [
                  const((SWA_D, 1)), const((SWA_D, 1)), smem, smem, const((2 * CH, CH))],
        out_specs=[col(512), col(512)],
        out_shape=[jax.ShapeDtypeStruct((512, T), F32), jax.ShapeDtypeStruct((512, T), F32)],
        scratch_shapes=[pltpu.VMEM((2, 2 * CH, HQ_LANES), F32), pltpu.VMEM((1, HQ_LANES), F32)])


def _swa_bwd(proj, so, dmix, q_norm_w, k_norm_w, sinks, rel_bias_t, bucket_t):
    def body(sq_ref, skp_ref, skc_ref, svp_ref, svc_ref, sg_lo, sg_hi, so_ref, dm_ref, qw_ref, kw_ref, sinks_ref,
             relb_ref, bucket_ref, d_ref, dbias_ref, dsink_ref, dqw_ref, dkw_ref,
             bias_ref, sink_row, band_dk, band_dv, carry_dk, carry_dv, hold_dq, hold_dg):
        n = pl.program_id(0)
        slot = n % 2

        @pl.when(n == 0)
        def _():
            _build_bias_t(bucket_ref, relb_ref, sinks_ref, bias_ref, sink_row)
            for ref in (dbias_ref, dsink_ref, dqw_ref, dkw_ref, carry_dk, carry_dv):
                ref[...] = jnp.zeros_like(ref)

        qw, kw = qw_ref[...], kw_ref[...]

        @pl.when(n < N_CH)
        def _():
            var = (n == 0).astype(jnp.int32)
            kn_band = jnp.concatenate([_norm_kv(skp_ref[...], kw), _norm_kv(skc_ref[...], kw)], axis=1)
            kb_rm = _bf(kn_band.T)
            kn_band_s = _bf(kn_band * (SWA_D ** -0.5))
            vband_f = jnp.concatenate([svp_ref[...], svc_ref[...]], axis=1)
            vb_rm = _bf(vband_f.T)
            q_raw, q_rstd, qs_b, do_b = [], [], [], []
            for hq in range(SWA_H):
                rows = slice(hq * SWA_D, (hq + 1) * SWA_D)
                q_t = sq_ref[rows, :]
                qn, rq = _rms_t(q_t, qw)
                g = _gate_rows((sg_lo, sg_hi), hq)
                sig = _sigmoid(g)
                dm = dm_ref[rows, :]
                hold_dg[slot, rows, :] = _bf(dm * so_ref[rows, :] * (sig * (1.0 + g * (1.0 - sig))))
                q_raw.append(q_t)
                q_rstd.append(rq)
                qs_b.append(_bf(qn * (SWA_D ** -0.5)))
                do_b.append(_bf(dm * (g * sig)))
            q_all = jnp.concatenate([_kv_slot(qs_b[hq], hq // SWA_G) for hq in range(SWA_H)], axis=1)
            do_all = jnp.concatenate([_kv_slot(do_b[hq], hq // SWA_G) for hq in range(SWA_H)], axis=1)
            probs, p_sink = _softmax_t(_dot(kb_rm, q_all) + bias_ref[var], sink_row[...])
            dprobs = _dot(vb_rm, do_all)
            t = jnp.sum(probs * dprobs, axis=0, keepdims=True)
            dlog = probs * (dprobs - t)
            dsink_ref[...] += -(p_sink * t)
            dbias_ref[...] += dlog
            dlog_b, probs_b = _bf(dlog), _bf(probs)
            dkn, dvv = [], []
            for kv in range(SWA_KV):
                heads = range(kv * SWA_G, (kv + 1) * SWA_G)
                lanes = slice(kv * SWA_G * CH, (kv + 1) * SWA_G * CH)
                dvv.append(_dot_nt(jnp.concatenate([do_b[hq] for hq in heads], axis=1), probs_b[:, lanes]))
                dkn.append(_dot_nt(jnp.concatenate([qs_b[hq] for hq in heads], axis=1), dlog_b[:, lanes]))
                dqn = _dot(kn_band_s[kv * SWA_D:(kv + 1) * SWA_D], dlog_b[:, lanes])
                for j, hq in enumerate(heads):
                    dq_t, dqw_terms = _rms_t_bwd(dqn[:, j * CH:(j + 1) * CH], q_raw[hq], q_rstd[hq], qw)
                    hold_dq[slot, hq * SWA_D:(hq + 1) * SWA_D, :] = _bf(dq_t)
                    dqw_ref[...] += dqw_terms
            band_dk[...] = jnp.concatenate(dkn, axis=0)
            band_dv[...] = jnp.concatenate(dvv, axis=0)

        @pl.when(n == N_CH)
        def _():
            band_dk[...] = jnp.zeros_like(band_dk)
            band_dv[...] = jnp.zeros_like(band_dv)

        @pl.when(n >= 1)
        def _():
            dkn_prev = carry_dk[...] + band_dk[:, 0:CH]
            k_t = skp_ref[...]
            for kv in range(SWA_KV):
                rows = slice(kv * SWA_D, (kv + 1) * SWA_D)
                _, rk = _rms_t(k_t[rows], kw)
                dk_t, dkw_terms = _rms_t_bwd(dkn_prev[rows], k_t[rows], rk, kw)
                d_ref[SWA_OFFS[1] + kv * SWA_D:SWA_OFFS[1] + (kv + 1) * SWA_D, :] = _bf(dk_t)
                dkw_ref[...] += dkw_terms
            d_ref[SWA_OFFS[2]:SWA_OFFS[3], :] = _bf(carry_dv[...] + band_dv[:, 0:CH])
            d_ref[SWA_OFFS[0]:SWA_OFFS[1], :] = hold_dq[1 - slot]
            d_ref[SWA_OFFS[3]:SWA_OFFS[4], :] = hold_dg[1 - slot]

        carry_dk[...] = band_dk[:, CH:2 * CH]
        carry_dv[...] = band_dv[:, CH:2 * CH]

    cur_block = lambda n: jnp.minimum(n, N_CH - 1)
    prev_block = lambda n: jnp.maximum(n - 1, 0)
    col = lambda w: pl.BlockSpec((w, CH), lambda n: (0, cur_block(n)))
    prev = lambda w: pl.BlockSpec((w, CH), lambda n: (0, prev_block(n)))
    const = lambda shape: pl.BlockSpec(shape, lambda n: (0,) * len(shape))
    smem = pl.BlockSpec(memory_space=pltpu.SMEM)
    return _Part(
        body,
        inputs=[proj, proj, proj, proj, proj, proj, proj, so, dmix, q_norm_w, k_norm_w, sinks, rel_bias_t, bucket_t],
        in_specs=[_proj_piece(4, cur_block), _proj_piece(5, prev_block), _proj_piece(5, cur_block),
                  _proj_piece(6, prev_block), _proj_piece(6, cur_block)] + _proj_gate_halves(cur_block) + [
                  col(512), col(512), const((SWA_D, 1)), const((SWA_D, 1)), smem, smem, const((2 * CH, CH))],
        out_specs=[prev(SWA_W), const((2 * CH, HQ_LANES)), const((1, HQ_LANES)),
                   const((SWA_D, CH)), const((SWA_D, CH))],
        out_shape=[jax.ShapeDtypeStruct((SWA_W, T), BF16),
                   jax.ShapeDtypeStruct((2 * CH, HQ_LANES), F32), jax.ShapeDtypeStruct((1, HQ_LANES), F32),
                   jax.ShapeDtypeStruct((SWA_D, CH), F32), jax.ShapeDtypeStruct((SWA_D, CH), F32)],
        scratch_shapes=[pltpu.VMEM((2, 2 * CH, HQ_LANES), F32), pltpu.VMEM((1, HQ_LANES), F32),
                        pltpu.VMEM((128, 2 * CH), F32), pltpu.VMEM((128, 2 * CH), F32),
                        pltpu.VMEM((128, CH), F32), pltpu.VMEM((128, CH), F32),
                        pltpu.VMEM((2, 512, CH), BF16), pltpu.VMEM((2, 512, CH), BF16)])


SMALL_GRAD_SHAPES = ((1, RET_V), (1, CH), (1, CH), (1, CH), (SWA_H, N_BUCKETS))


def _finish_small_grads(dbias_ref, bucket_ref, dsink_ref, dqw_ref, dkw_ref, drw_ref, rw_o, qw_o, kw_o, sink_o, relb_o):
    bt = bucket_ref[...]
    row = lax.broadcasted_iota(jnp.int32, (SWA_H, N_BUCKETS), 0)
    col = lax.broadcasted_iota(jnp.int32, (SWA_H, N_BUCKETS), 1)
    lane = lax.broadcasted_iota(jnp.int32, (1, CH), 1)
    acc = jnp.zeros((SWA_H, N_BUCKETS), F32)
    sink = jnp.zeros((1, CH), F32)
    for hq in range(SWA_H):
        d = dbias_ref[:, _head_lanes(hq)]
        for bk in range(N_BUCKETS):
            s = jnp.sum(jnp.sum(jnp.where(bt == bk, d, 0.0), axis=0, keepdims=True), axis=1, keepdims=True)
            acc = acc + jnp.where((row == hq) & (col == bk), s, 0.0)
        sink = sink + jnp.where(lane == hq, jnp.sum(dsink_ref[:, _head_lanes(hq)], axis=1, keepdims=True), 0.0)
    relb_o[...] = acc
    sink_o[...] = sink
    for src, dst in ((dqw_ref, qw_o), (dkw_ref, kw_o)):
        padded = jnp.concatenate([src[...], jnp.zeros((CH - SWA_D, CH), F32)], axis=0)
        dst[...] = jnp.sum(padded.T, axis=0, keepdims=True)
    rw_o[...] = jnp.sum(drw_ref[...].T, axis=0, keepdims=True)


def _out_proj(mix_r, mix_s, wo, x, target):
    def body(mr_ref, ms_ref, w_ref, x_ref, t_ref, loss_ref, dy_ref, dmr_ref, dms_ref, gw_ref):
        i = pl.program_id(0)

        @pl.when(i == 0)
        def _():
            loss_ref[...] = jnp.zeros_like(loss_ref)
            gw_ref[...] = jnp.zeros_like(gw_ref)

        mixed = jnp.concatenate([mr_ref[...], ms_ref[...]], axis=0)
        w = w_ref[...]
        err = x_ref[...] + _dot(_bf(mixed.T), w) - t_ref[...]
        loss_ref[...] += jnp.sum(jnp.sum(err * err, axis=1, keepdims=True), axis=0, keepdims=True)
        dy = err * (1.0 / D)
        dy_ref[...] = dy
        dy_b = _bf(dy)
        dmix = _dot_nt(w, dy_b)
        dmr_ref[...] = dmix[0:512]
        dms_ref[...] = dmix[512:D]
        gw_ref[...] += _dot(_bf(mixed), dy_b)

    row = lambda w: pl.BlockSpec((TM, w), lambda i: (i, 0))
    col = lambda w: pl.BlockSpec((w, TM), lambda i: (0, i))
    const = lambda shape: pl.BlockSpec(shape, lambda i: (0,) * len(shape))
    return pl.pallas_call(
        body, name="out_proj", grid=(T // TM,),
        in_specs=[col(512), col(512), const((D, D)), row(D), row(D)],
        out_specs=(const((1, 1)), row(D), col(512), col(512), const((D, D))),
        out_shape=(jax.ShapeDtypeStruct((1, 1), F32), jax.ShapeDtypeStruct((T, D), F32),
                   jax.ShapeDtypeStruct((512, T), F32), jax.ShapeDtypeStruct((512, T), F32),
                   jax.ShapeDtypeStruct((D, D), F32)),
        compiler_params=_cparams(1),
    )(mix_r, mix_s, wo, x, target)


LOSS_ROW = 5
ALL_CHIPS = ((0, 0), (0, 1), (1, 0), (1, 1))


def _w_out_reduce_scatter(gwo):
    def body(gwo_ref, gout_o, own, rcv, snd, got, tot, local_sems, a_send, a_recv, b_send, b_recv):
        k = pl.program_id(0)
        x_, y_, c_ = _mesh_pos()
        sibling = (x_, y_, 1 - c_)
        rel_chips = [(1 - x_, y_), (x_, 1 - y_), (1 - x_, 1 - y_)]

        def to_sibling(j):
            px, py = ALL_CHIPS[j]
            return pltpu.make_async_remote_copy(src_ref=gwo_ref.at[_blk(px, py, 1 - c_)], dst_ref=rcv.at[j],
                                                send_sem=a_send.at[j], recv_sem=a_recv.at[j], device_id=sibling,
                                                device_id_type=MESH)

        def local(j):
            px, py = ALL_CHIPS[j]
            return pltpu.make_async_copy(gwo_ref.at[_blk(px, py, c_)], own.at[j], local_sems.at[j])

        def to_chip(q):
            return pltpu.make_async_remote_copy(src_ref=snd.at[q], dst_ref=got.at[q], send_sem=b_send.at[q],
                                                recv_sem=b_recv.at[q], device_id=(*rel_chips[q], c_), device_id_type=MESH)

        @pl.when(k == 0)
        def _():
            for j in range(len(ALL_CHIPS)):
                to_sibling(j).start()
                local(j).start()

        @pl.when(k == 2)
        def _():
            for j in range(len(ALL_CHIPS)):
                local(j).wait()
                to_sibling(j).wait_recv()
            for q in range(3):
                j = 2 * rel_chips[q][0] + rel_chips[q][1]
                snd[q] = _bf(own[j] + rcv[j])
                to_chip(q).start()
            jm = 2 * x_ + y_
            tot[...] = own[jm] + rcv[jm]

        @pl.when(k == N_CH)
        def _():
            g = tot[...]
            for q in range(3):
                to_chip(q).wait_recv()
                g = g + got[q].astype(F32)
            gout_o[...] = g
            for j in range(len(ALL_CHIPS)):
                to_sibling(j).wait_send()
            for q in range(3):
                to_chip(q).wait_send()

    dma = pltpu.SemaphoreType.DMA
    return _Part(
        body, inputs=[gwo], in_specs=[pl.BlockSpec(memory_space=pl.ANY)],
        out_specs=[pl.BlockSpec((WOUT_BLK, D), lambda k: (0, 0))], out_shape=[jax.ShapeDtypeStruct((WOUT_BLK, D), F32)],
        scratch_shapes=[pltpu.VMEM((4, WOUT_BLK, D), F32), pltpu.VMEM((4, WOUT_BLK, D), F32),
                        pltpu.VMEM((3, WOUT_BLK, D), BF16), pltpu.VMEM((3, WOUT_BLK, D), BF16),
                        pltpu.VMEM((WOUT_BLK, D), F32), dma((4,)), dma((4,)), dma((4,)), dma((3,)), dma((3,))])


def _in_proj_bwd_rs(d_ret, d_swa, h, wt, x, norm_w, dy, sse, small_acc):
    n_blocks, n_tiles = N_DEV, T // TM
    last = n_blocks + n_tiles - 1

    def body(da_any, db_any, da_ref, db_ref, h_any, w_any, x_ref, nw_ref, dy_ref, sse_ref,
             dbias_ref, bucket_ref, dsink_ref, dqw_ref, dkw_ref, drw_ref, gx_ref, gin_o, suma_o, sumb_o,
             stage, d2d_src, mine, rcv, snd, got, tot, tab_a, tab_b, gnw_acc, rw_ref, qw_ref, kw_ref, sk_ref, rb_ref,
             h_ref, w_ref, dp_sems, d2d_send, d2d_recv, ici_send, ici_recv, s_send, s_recv, hw_sems):
        k = pl.program_id(0)
        x_, y_, c_ = _mesh_pos()
        h_load = pltpu.make_async_copy(h_any, h_ref, hw_sems.at[0])
        w_load = pltpu.make_async_copy(w_any, w_ref, hw_sems.at[1])
        me = _blk(x_, y_, c_)
        sibling = (x_, y_, 1 - c_)
        rel_chips = [(1 - x_, y_), (x_, 1 - y_), (1 - x_, 1 - y_), (x_, y_)]

        def block_of_step(s):
            return _blk(*rel_chips[s // 2], 1 - c_ if s % 2 == 0 else c_)

        def fetch(s, wait):
            slot, b = s % 2, block_of_step(s)
            split = RET_W - 4 * WIN_BLK

            def run(src, dst, sem):
                cp = pltpu.make_async_copy(src, dst, sem)
                cp.wait() if wait else cp.start()

            @pl.when(b < 4)
            def _():
                run(da_any.at[pl.ds(pl.multiple_of(b * WIN_BLK, 16), WIN_BLK), :], stage.at[slot], dp_sems.at[slot, 0])

            @pl.when(b == 4)
            def _():
                run(da_any.at[pl.ds(4 * WIN_BLK, split), :], stage.at[slot, pl.ds(0, split), :], dp_sems.at[slot, 0])
                run(db_any.at[pl.ds(0, WIN_BLK - split), :], stage.at[slot, pl.ds(split, WIN_BLK - split), :],
                    dp_sems.at[slot, 1])

            @pl.when(b > 4)
            def _():
                run(db_any.at[pl.ds(pl.multiple_of(b * WIN_BLK - RET_W, 16), WIN_BLK), :], stage.at[slot],
                    dp_sems.at[slot, 0])

        def d2d_copy(r):
            return pltpu.make_async_remote_copy(src_ref=d2d_src, dst_ref=rcv.at[r], send_sem=d2d_send.at[r],
                                                recv_sem=d2d_recv.at[r], device_id=sibling, device_id_type=MESH)

        def ici_copy(r):
            return pltpu.make_async_remote_copy(src_ref=snd.at[r], dst_ref=got.at[r], send_sem=ici_send.at[r],
                                                recv_sem=ici_recv.at[r], device_id=(*rel_chips[r], c_),
                                                device_id_type=MESH)

        def table_copies():
            return [pltpu.make_async_remote_copy(src_ref=tab.at[me], dst_ref=tab.at[me], send_sem=s_send.at[a, p - 1],
                                                 recv_sem=s_recv.at[a, p - 1],
                                                 device_id=(x_ ^ (p >> 2), y_ ^ ((p >> 1) & 1), c_ ^ (p & 1)),
                                                 device_id_type=MESH)
                    for p in range(1, N_DEV) for a, tab in enumerate((tab_a, tab_b))]

        def chip_sum(r):
            d2d_copy(r).wait_recv()
            total = mine[...] + rcv[r]
            if r < 3:
                snd[r] = _bf(total)
                ici_copy(r).start()
            else:
                tot[...] = total

        for s in range(n_blocks):
            @pl.when(k == s)
            def _(s=s):
                r = s // 2
                if s == 0:
                    gnw_acc[...] = jnp.zeros_like(gnw_acc)
                    h_load.start()
                    fetch(0, wait=False)
                    w_load.start()
                fetch(s, wait=True)
                if s + 1 < n_blocks:
                    fetch(s + 1, wait=False)
                if s == 0:
                    h_load.wait()
                gw = _dot(stage[s % 2], h_ref[...])
                if s % 2 == 0:
                    if s >= 2:
                        chip_sum(r - 1)
                        d2d_copy(r - 1).wait_send()
                    d2d_src[...] = gw
                    d2d_copy(r).start()
                else:
                    mine[...] = gw

        @pl.when(k == n_blocks)
        def _():
            chip_sum(3)
            w_load.wait()

        @pl.when(k >= n_blocks)
        def _():
            dp = jnp.concatenate([da_ref[...], db_ref[...]], axis=0)
            xv, nw = x_ref[...], nw_ref[...]
            r = lax.rsqrt(jnp.mean(xv * xv, axis=-1, keepdims=True) + NORM_EPS)
            dh = lax.dot_general(dp, w_ref[...], (((0,), (0,)), ((), ())), preferred_element_type=F32)
            u = dh * nw
            gx_ref[...] = dy_ref[...] + r * u - xv * (r * r * r) * jnp.mean(u * xv, axis=-1, keepdims=True)
            gnw_acc[...] += jnp.sum(dh * (xv * r), axis=0, keepdims=True)

        @pl.when(k == n_blocks)
        def _():
            _finish_small_grads(dbias_ref, bucket_ref, dsink_ref, dqw_ref, dkw_ref, drw_ref,
                                rw_ref, qw_ref, kw_ref, sk_ref, rb_ref)

        @pl.when(k == last)
        def _():
            mine_a, mine_b = tab_a.at[me], tab_b.at[me]
            mine_a[...] = jnp.zeros_like(mine_a)
            mine_b[...] = jnp.zeros_like(mine_b)
            for row, ref in enumerate((gnw_acc, rw_ref, qw_ref, kw_ref, sk_ref)):
                mine_a[row:row + 1, 0:ref.shape[1]] = ref[...]
            mine_a[LOSS_ROW:LOSS_ROW + 1, 0:1] = sse_ref[...]
            mine_b[:, 0:N_BUCKETS] = rb_ref[...]
            tables = table_copies()
            for cp in tables:
                cp.start()
            g_in = tot[...]
            for q in range(3):
                ici_copy(q).wait_recv()
                g_in = g_in + got[q].astype(F32)
            gin_o[...] = g_in
            for cp in tables:
                cp.wait_recv()
            sum_a, sum_b = tab_a[0], tab_b[0]
            for b in range(1, N_DEV):
                sum_a = sum_a + tab_a[b]
                sum_b = sum_b + tab_b[b]
            suma_o[...] = sum_a
            sumb_o[...] = sum_b
            d2d_copy(3).wait_send()
            for q in range(3):
                ici_copy(q).wait_send()
            for cp in tables:
                cp.wait_send()

    tile = lambda k: jnp.maximum(k - n_blocks, 0)
    row = lambda w: pl.BlockSpec((TM, w), lambda k: (tile(k), 0))
    col = lambda w: pl.BlockSpec((w, TM), lambda k: (0, tile(k)))
    const = lambda shape, **kw: pl.BlockSpec(shape, lambda k: (0,) * len(shape), **kw)
    once = dict(pipeline_mode=pl.Buffered(1))
    hbm = pl.BlockSpec(memory_space=pltpu.HBM)
    dma = pltpu.SemaphoreType.DMA
    return pl.pallas_call(
        body, name="in_proj_bwd_rs", grid=(n_blocks + n_tiles,),
        in_specs=[hbm, hbm, col(RET_W), col(SWA_W), hbm, hbm,
                  row(D), const((1, D)), row(D), const((1, 1))]
                 + [const(a.shape) for a in small_acc],
        out_specs=(row(D), const((WIN_BLK, D)), const((8, D)), const((SWA_H, CH))),
        out_shape=(jax.ShapeDtypeStruct((T, D), F32), jax.ShapeDtypeStruct((WIN_BLK, D), F32),
                   jax.ShapeDtypeStruct((8, D), F32), jax.ShapeDtypeStruct((SWA_H, CH), F32)),
        scratch_shapes=[
            pltpu.VMEM((2, WIN_BLK, T), BF16), pltpu.VMEM((WIN_BLK, D), F32),
            pltpu.VMEM((WIN_BLK, D), F32),
            pltpu.VMEM((4, WIN_BLK, D), F32), pltpu.VMEM((3, WIN_BLK, D), BF16),
            pltpu.VMEM((3, WIN_BLK, D), BF16), pltpu.VMEM((WIN_BLK, D), F32),
            pltpu.VMEM((N_DEV, 8, D), F32), pltpu.VMEM((N_DEV, SWA_H, CH), F32),
            pltpu.VMEM((1, D), F32),
        ] + [pltpu.VMEM(s, F32) for s in SMALL_GRAD_SHAPES] + [
            pltpu.VMEM((T, D), BF16), pltpu.VMEM((D_IN, D), BF16),
            dma((2, 2)), dma((4,)), dma((4,)), dma((3,)), dma((3,)), dma((2, 7)), dma((2, 7)), dma((2,)),
        ],
        compiler_params=_cparams(1),
    )(d_ret, d_swa, d_ret, d_swa, h, wt, x, norm_w, dy, sse, *small_acc)


SMALL_SHAPES = ((1, D), (1, 512), (1, SWA_D), (1, SWA_D), (1, SWA_H), (SWA_H, N_BUCKETS))


def _adamw_all(g_in_t, g_out, sum_a, sum_b, w_in_t, m_in_t, v_in_t, w_out, m_out, v_out, small_w, small_m, small_v):
    n_small = len(SMALL_SHAPES)
    halves = 2

    def body(*refs):
        gin_any, gout_any, suma_any, sumb_any, w_in_ref, m_in_ref, v_in_ref, w_out_ref, m_out_ref, v_out_ref = refs[:10]
        pos = 10
        sw_refs, sm_refs, sv_refs = (refs[pos + i * n_small:pos + (i + 1) * n_small] for i in range(3))
        pos += 3 * n_small
        g_in_o, d_in_o, nm_in_o, nv_in_o, g_out_o, d_out_o, nm_out_o, nv_out_o = refs[pos:pos + 8]
        pos += 8
        sg_o, sd_o, snm_o, snv_o = (refs[pos + i * n_small:pos + (i + 1) * n_small] for i in range(4))
        pos += 4 * n_small
        loss_o, gin_ref, gout_ref, suma_ref, sumb_ref, load_sems = refs[pos:]
        i = pl.program_id(0)

        @pl.when(i == 0)
        def _():
            loads = [pltpu.make_async_copy(src, dst, load_sems.at[j]) for j, (src, dst) in enumerate(
                ((gin_any, gin_ref), (gout_any, gout_ref), (suma_any, suma_ref), (sumb_any, sumb_ref)))]
            for cp in loads:
                cp.start()
            for cp in loads:
                cp.wait()

        g = gout_ref[pl.ds(pl.multiple_of(i * (WOUT_BLK // halves), 8), WOUT_BLK // halves), :]
        d, nm, nv = _adamw(w_out_ref[...], g, m_out_ref[...], v_out_ref[...])
        g_out_o[...], d_out_o[...], nm_out_o[...], nv_out_o[...] = g, d, nm, nv
        g = gin_ref[pl.ds(pl.multiple_of(i * (WIN_BLK // halves), 8), WIN_BLK // halves), :]
        d, nm, nv = _adamw(w_in_ref[...], g, m_in_ref[...], v_in_ref[...])
        g_in_o[...], d_in_o[...], nm_in_o[...], nv_in_o[...] = g, d, nm, nv

        @pl.when(pl.program_id(0) == 0)
        def _():
            loss_o[...] = suma_ref[LOSS_ROW:LOSS_ROW + 1, 0:1] * (0.5 / D)
            for r, (rows, lanes) in enumerate(SMALL_SHAPES):
                g = suma_ref[r:r + 1, 0:lanes] if rows == 1 else sumb_ref[:, 0:lanes]
                d, nm, nv = _adamw(sw_refs[r][...], g, sm_refs[r][...], sv_refs[r][...])
                sg_o[r][...], sd_o[r][...], snm_o[r][...], snv_o[r][...] = g, d, nm, nv

    half = lambda rows: pl.BlockSpec((rows // halves, D), lambda i: (i, 0))
    const = lambda shape: pl.BlockSpec(shape, lambda i: (0,) * len(shape))
    win = jax.ShapeDtypeStruct((WIN_BLK, D), F32)
    wout = jax.ShapeDtypeStruct((WOUT_BLK, D), F32)
    smalls = tuple(jax.ShapeDtypeStruct(s, F32) for s in SMALL_SHAPES)
    small_specs = [const(s) for s in SMALL_SHAPES]
    outs = pl.pallas_call(
        body, name="adamw_all", grid=(halves,),
        in_specs=[pl.BlockSpec(memory_space=pltpu.HBM)] * 4 + [half(WIN_BLK)] * 3 + [half(WOUT_BLK)] * 3 + small_specs * 3,
        out_specs=tuple([half(WIN_BLK)] * 4 + [half(WOUT_BLK)] * 4 + small_specs * 4 + [const((1, 1))]),
        out_shape=(win,) * 4 + (wout,) * 4 + smalls * 4 + (jax.ShapeDtypeStruct((1, 1), F32),),
        scratch_shapes=[pltpu.VMEM((WIN_BLK, D), F32), pltpu.VMEM((WOUT_BLK, D), F32), pltpu.VMEM((8, D), F32),
                        pltpu.VMEM((SWA_H, CH), F32), pltpu.SemaphoreType.DMA((4,))],
        compiler_params=_cparams(1, vmem=32 * 1024 * 1024),
    )(g_in_t, g_out, sum_a, sum_b, w_in_t, m_in_t, v_in_t, w_out, m_out, v_out, *small_w, *small_m, *small_v)
    big, rest = outs[:8], outs[8:]
    return big, [rest[i * n_small:(i + 1) * n_small] for i in range(4)], rest[4 * n_small]


def _small_rows(norm_w, ret_norm_w, q_norm_w, k_norm_w, sinks, rel_bias):
    return (norm_w.reshape(1, D), ret_norm_w.reshape(1, 512), q_norm_w.reshape(1, SWA_D), k_norm_w.reshape(1, SWA_D),
            sinks.reshape(1, SWA_H), rel_bias.T)


def _small_leaves(rows):
    return (rows[0].reshape(D), rows[1].reshape(512), rows[2].reshape(SWA_D), rows[3].reshape(SWA_D),
            rows[4].reshape(SWA_H), rows[5].T)


def kernel(x, norm_w, w_in, ret_norm_w, q_norm_w, k_norm_w, sinks, rel_bias, w_out, loss_target, m_norm_w, m_w_in, m_ret_norm_w, m_q_norm_w, m_k_norm_w, m_sinks, m_rel_bias, m_w_out, v_norm_w, v_w_in, v_ret_norm_w, v_q_norm_w, v_k_norm_w, v_sinks, v_rel_bias, v_w_out):
    x2 = x.reshape(T, D)
    target = loss_target.reshape(T, D)
    nw = norm_w.reshape(1, D)
    rnw_col = ret_norm_w.reshape(RET_V, 1)
    qnw_col = q_norm_w.reshape(SWA_D, 1)
    knw_col = k_norm_w.reshape(SWA_D, 1)
    relb = rel_bias.T
    ret_tables = _rotary_tables_t() + _retention_tables_t()
    bucket_t = _bucket_table_t()

    proj, wt, h = _in_proj_gather(x2, nw, w_in.T)
    (ro, mix_r, states, wo), (so, mix_s) = _fused_call("attn_fwd", N_CH, [
        _retention_fwd(proj, rnw_col, ret_tables, w_out),
        _swa_fwd(proj, qnw_col, knw_col, sinks, relb, bucket_t)])
    sse, dy, dmix_r, dmix_s, gwo = _out_proj(mix_r, mix_s, wo.reshape(D, D), x2, target)

    (d_ret, drw_acc), (d_swa, dbias, dsink_acc, dqw_acc, dkw_acc), (g_out,) = _fused_call("attn_bwd", N_CH + 1, [
        _retention_bwd(proj, ro, states, dmix_r, rnw_col, ret_tables),
        _swa_bwd(proj, so, dmix_s, qnw_col, knw_col, sinks, relb, bucket_t),
        _w_out_reduce_scatter(gwo.reshape(N_DEV, WOUT_BLK, D))])
    grad_x, g_in_t, sum_a, sum_b = _in_proj_bwd_rs(d_ret, d_swa, h, wt, x2, nw, dy, sse,
                                                   (dbias, bucket_t, dsink_acc, dqw_acc, dkw_acc, drw_acc))

    small_w = _small_rows(norm_w, ret_norm_w, q_norm_w, k_norm_w, sinks, rel_bias)
    small_m = _small_rows(m_norm_w, m_ret_norm_w, m_q_norm_w, m_k_norm_w, m_sinks, m_rel_bias)
    small_v = _small_rows(v_norm_w, v_ret_norm_w, v_q_norm_w, v_k_norm_w, v_sinks, v_rel_bias)
    big, small, loss = _adamw_all(g_in_t, g_out, sum_a, sum_b, w_in.T, m_w_in.T, v_w_in.T, w_out, m_w_out, v_w_out,
                                  small_w, small_m, small_v)

    def leaves(i):
        a = _small_leaves(small[i])
        return (a[0], big[i].T, a[1], a[2], a[3], a[4], a[5], big[4 + i])

    return (loss.reshape(()), grad_x.reshape(1, T, D), *leaves(0), *leaves(1), *leaves(2), *leaves(3))
```

```python
from typing import Callable, NamedTuple

import numpy as np
import jax
import jax.numpy as jnp
from jax import lax
from jax.experimental import pallas as pl
from jax.experimental.pallas import tpu as pltpu

F32 = jnp.float32
BF16 = jnp.bfloat16
MESH = pl.DeviceIdType.MESH

T = 2048
D = 1024
D_IN = 2816
N_DEV = 8
WIN_BLK = D_IN // N_DEV
WOUT_BLK = D // N_DEV
CH = 128
N_CH = T // CH
RET_H, RET_DK, RET_DV = 4, 64, 128
RET_QK = RET_H * RET_DK
RET_V = RET_H * RET_DV
SWA_H, SWA_KV, SWA_D, SWA_G = 8, 2, 64, 4
N_BUCKETS = 32
NORM_EPS = 1e-6
GN_EPS = 1e-5
NEG_INF = -1e30
PIECES = (256, 256, 512, 512, 512, 128, 128, 512)
OFFS = tuple(int(v) for v in np.cumsum((0,) + PIECES))
RET_W = OFFS[4]
SWA_W = D_IN - RET_W
SWA_OFFS = tuple(o - RET_W for o in OFFS[4:])
TM = 256

ADAM_LR, ADAM_B1, ADAM_B2, ADAM_EPS, ADAM_WD, ADAM_STEP = 0.001, 0.9, 0.999, 1e-08, 0.01, 10

VMEM_LIMIT = 56 * 1024 * 1024


def _cparams(n_grid=0, vmem=VMEM_LIMIT):
    sem = ("arbitrary",) * n_grid if n_grid else None
    return pltpu.CompilerParams(dimension_semantics=sem, vmem_limit_bytes=vmem)


class _Part(NamedTuple):
    step: Callable
    inputs: list
    in_specs: list
    out_specs: list
    out_shape: list
    scratch_shapes: list


def _fused_call(name, n_steps, parts):
    n_in = [len(p.inputs) for p in parts]
    n_out = [len(p.out_shape) for p in parts]
    n_scr = [len(p.scratch_shapes) for p in parts]

    def body(*refs):
        ins, outs, scr = refs[:sum(n_in)], refs[sum(n_in):sum(n_in) + sum(n_out)], refs[sum(n_in) + sum(n_out):]
        for i, p in enumerate(parts):
            take = lambda seq, counts: seq[sum(counts[:i]):sum(counts[:i + 1])]
            p.step(*take(ins, n_in), *take(outs, n_out), *take(scr, n_scr))

    flat = lambda field: [v for p in parts for v in getattr(p, field)]
    outs = pl.pallas_call(
        body, name=name, grid=(n_steps,), in_specs=flat("in_specs"), out_specs=tuple(flat("out_specs")),
        out_shape=tuple(flat("out_shape")), scratch_shapes=flat("scratch_shapes"), compiler_params=_cparams(1),
    )(*flat("inputs"))
    return [list(outs[sum(n_out[:i]):sum(n_out[:i + 1])]) for i in range(len(parts))]


def _dot(a, b):
    return jnp.dot(a, b, preferred_element_type=F32)


def _dot_nt(a, b):
    return lax.dot_general(a, b, (((1,), (1,)), ((), ())), preferred_element_type=F32)


def _bf(a):
    return a.astype(BF16)


def _sigmoid(x):
    return 1.0 / (1.0 + jnp.exp(-x))


def _adamw(w, g, m, v):
    m = ADAM_B1 * m + (1.0 - ADAM_B1) * g
    v = ADAM_B2 * v + (1.0 - ADAM_B2) * (g * g)
    m_hat = m / (1.0 - ADAM_B1 ** ADAM_STEP)
    v_hat = v / (1.0 - ADAM_B2 ** ADAM_STEP)
    delta = -ADAM_LR * (m_hat / (jnp.sqrt(v_hat) + ADAM_EPS) + ADAM_WD * w)
    return delta, m, v


def _rotary_tables_t():
    half = RET_DK // 2
    inv_freq = np.float32(10000.0) ** (-np.arange(half, dtype=np.float32) / np.float32(half))
    ang = inv_freq[:, None] * np.arange(T, dtype=np.float32)[None, :]
    cos, sin = np.cos(ang).astype(np.float32), np.sin(ang).astype(np.float32)
    cos64 = np.concatenate([cos, cos], axis=0)
    sin64 = np.concatenate([-sin, sin], axis=0)
    return np.tile(cos64, (RET_H, 1)), np.tile(sin64, (RET_H, 1))


def _retention_tables_t():
    gamma = (1.0 - np.exp2(-5.0 - np.arange(RET_H, dtype=np.float32))).astype(np.float32)
    log_g = np.log(gamma).astype(np.float32)
    i = np.arange(CH, dtype=np.float32)
    diff = i[None, :] - i[:, None]
    decay = np.where(diff >= 0, np.exp(log_g[:, None, None] * np.maximum(diff, 0.0)), 0.0).astype(np.float32)
    decay_all = np.concatenate(list(decay), axis=1)
    zeta = np.exp(log_g[:, None] * (CH - 1.0 - i)).astype(np.float32)
    zeta_tab = np.repeat(zeta.T, RET_DK, axis=1)
    xi = np.exp(log_g[:, None] * (i + 1.0)).astype(np.float32)
    xi_tab = np.repeat(xi, RET_DK, axis=0)
    chunk_decay = np.exp(log_g * np.float32(CH)).astype(np.float32)
    row_head = np.arange(RET_V)[:, None] // RET_DV
    col_head = np.arange(RET_QK)[None, :] // RET_DK
    state_mask = (row_head == col_head).astype(np.float32)
    state_decay = (state_mask * chunk_decay[row_head]).astype(np.float32)
    q_mask = (np.arange(RET_QK)[:, None] // RET_DK == np.arange(RET_H * CH)[None, :] // CH).astype(np.float32)
    return (np.ascontiguousarray(decay_all), np.ascontiguousarray(zeta_tab), np.ascontiguousarray(xi_tab),
            state_mask, state_decay, q_mask)


def _bucket_table_t():
    qi = np.arange(CH)[:, None]
    kj = np.arange(2 * CH)[None, :]
    dist = qi + CH - kj
    n = np.maximum(dist, 0)
    max_exact = N_BUCKETS // 2
    nf = np.maximum(n, 1).astype(np.float32)
    large = max_exact + (np.log(nf / np.float32(max_exact)) / np.float32(np.log(CH / max_exact))
                         * np.float32(N_BUCKETS - max_exact)).astype(np.int32)
    large = np.minimum(large, N_BUCKETS - 1)
    bucket = np.where(n < max_exact, n, large)
    return np.ascontiguousarray(np.where((dist >= 0) & (dist < CH), bucket, -1).astype(np.int32).T)


def _mesh_pos():
    return lax.axis_index("x"), lax.axis_index("y"), lax.axis_index("c")


def _blk(px, py, pc):
    return 4 * px + 2 * py + pc


def _in_proj_gather(x, norm_w, w_in_t):
    tchunk = 512

    def body(x_any, nw_any, win_any, proj_ref, wt_out, h_out, wt_ref, h_ref, stage, x_ref, nw_ref, win_ref,
             send_sems, recv_sems, out_sems, wb_sems, h_sem, in_sems):
        n_tiles = T // TM
        loads = [pltpu.make_async_copy(win_any, win_ref, in_sems.at[0]),
                 pltpu.make_async_copy(nw_any, nw_ref, in_sems.at[1])]
        loads += [pltpu.make_async_copy(x_any.at[pl.ds(i * TM, TM), :], x_ref.at[pl.ds(i * TM, TM), :], in_sems.at[2 + i])
                  for i in range(n_tiles)]
        for cp in loads:
            cp.start()
        loads[0].wait()
        x, y, c = _mesh_pos()
        me = _blk(x, y, c)
        sibling = (x, y, 1 - c)
        nbr_a, nbr_b, diag = (x ^ (1 - c), y ^ c), (x ^ c, y ^ (1 - c)), (1 - x, 1 - y)

        def copy(k, b, to):
            return pltpu.make_async_remote_copy(src_ref=wt_ref.at[b], dst_ref=wt_ref.at[b], send_sem=send_sems.at[k],
                                                recv_sem=recv_sems.at[k], device_id=to, device_id_type=MESH)

        wt_ref[me] = _bf(win_ref[...])
        first = [copy(0, me, sibling), copy(1, me, (*nbr_a, c))]
        for cp in first:
            cp.start()
        to_b = copy(2, me, (*nbr_b, c))

        loads[1].wait()
        nw = nw_ref[...]
        for i in range(n_tiles):
            rows = slice(i * TM, (i + 1) * TM)
            loads[2 + i].wait()
            xv = x_ref[rows, :]
            r = lax.rsqrt(jnp.mean(xv * xv, axis=-1, keepdims=True) + NORM_EPS)
            h_ref[rows, :] = _bf(xv * r * nw)
        keep = [pltpu.make_async_copy(h_ref, h_out, h_sem)]
        keep[0].start()

        writes = []

        def project(b):
            k = len(writes)
            if k >= 2:
                writes[k - 2].wait()
            keep.append(pltpu.make_async_copy(wt_ref.at[b], wt_out.at[b], wb_sems.at[k]))
            keep[-1].start()
            w = wt_ref[b]
            for t in range(T // tchunk):
                cols = slice(t * tchunk, (t + 1) * tchunk)
                stage[k % 2, :, cols] = _dot_nt(w, h_ref[cols, :])
            cp = pltpu.make_async_copy(stage.at[k % 2], proj_ref.at[b], out_sems.at[k % 2])
            cp.start()
            writes.append(cp)

        here = (x, y, c)
        project(me)
        copy(0, _blk(x, y, 1 - c), here).wait_recv()
        project(_blk(x, y, 1 - c))
        to_b.start()
        passed = [to_b]

        def landed(k, chip, relay_to=None):
            copy(k, _blk(*chip, c), here).wait_recv()
            out = [copy({1: 4, 2: 5, 3: 6}[k], _blk(*chip, c), sibling)]
            if relay_to is not None:
                out.append(copy(3, _blk(*chip, c), (*relay_to, c)))
            for cp in out:
                cp.start()
            passed.extend(out)
            project(_blk(*chip, c))

        def from_sibling(k, chip):
            copy(k, _blk(*chip, 1 - c), here).wait_recv()
            project(_blk(*chip, 1 - c))

        landed(1, nbr_a, relay_to=nbr_b)
        from_sibling(4, nbr_b)
        landed(2, nbr_b)
        from_sibling(5, nbr_a)
        landed(3, diag)
        from_sibling(6, diag)
        for cp in writes[-2:] + keep:
            cp.wait()
        for cp in first + passed:
            cp.wait_send()

    vm = pl.BlockSpec(memory_space=pltpu.VMEM)
    hbm = pl.BlockSpec(memory_space=pltpu.HBM)
    dma = pltpu.SemaphoreType.DMA
    proj, wt, h = pl.pallas_call(
        body, name="in_proj_gather",
        out_shape=(jax.ShapeDtypeStruct((N_DEV, WIN_BLK, T), F32), jax.ShapeDtypeStruct((N_DEV, WIN_BLK, D), BF16),
                   jax.ShapeDtypeStruct((T, D), BF16)),
        in_specs=[hbm, hbm, hbm], out_specs=(hbm, hbm, hbm),
        scratch_shapes=[pltpu.VMEM((N_DEV, WIN_BLK, D), BF16), pltpu.VMEM((T, D), BF16), pltpu.VMEM((2, WIN_BLK, T), F32),
                        pltpu.VMEM((T, D), F32), pltpu.VMEM((1, D), F32), pltpu.VMEM((WIN_BLK, D), F32),
                        dma((7,)), dma((7,)), dma((2,)), dma((N_DEV,)), dma, dma((2 + T // TM,))],
        compiler_params=_cparams(vmem=48 * 1024 * 1024),
    )(x, norm_w, w_in_t)
    return proj.reshape(D_IN, T), wt.reshape(D_IN, D), h


def _gather_w_out(w_out_ref, own_ref, wo_ref, local_sem, send_sems, recv_sems, phase):
    x, y, c = _mesh_pos()
    me = _blk(x, y, c)
    sibling = (x, y, 1 - c)
    chips = [(1 - x, y), (x, 1 - y), (1 - x, 1 - y)]

    def copy(k, b, to, src=None):
        return pltpu.make_async_remote_copy(src_ref=wo_ref.at[b] if src is None else src, dst_ref=wo_ref.at[b],
                                            send_sem=send_sems.at[k], recv_sem=recv_sems.at[k], device_id=to,
                                            device_id_type=MESH)

    local = pltpu.make_async_copy(own_ref, wo_ref.at[me], local_sem)
    first = [copy(0, me, sibling, own_ref)] + [copy(1 + j, me, (*chip, c), own_ref) for j, chip in enumerate(chips)]
    passed = [copy(4 + j, _blk(*chip, c), sibling) for j, chip in enumerate(chips)]
    if phase == 0:
        own_ref[...] = _bf(w_out_ref[...])
        local.start()
        for cp in first:
            cp.start()
    elif phase == 1:
        for j, chip in enumerate(chips):
            copy(1 + j, _blk(*chip, c), (x, y, c)).wait_recv()
            passed[j].start()
    else:
        local.wait()
        copy(0, _blk(x, y, 1 - c), (x, y, c)).wait_recv()
        for j, chip in enumerate(chips):
            copy(4 + j, _blk(*chip, 1 - c), (x, y, c)).wait_recv()
        for cp in first + passed:
            cp.wait_send()


def _proj_piece(i, block_of):
    rows = PIECES[i]
    assert OFFS[i] % rows == 0
    return pl.BlockSpec((rows, CH), lambda n: (OFFS[i] // rows, block_of(n)))


def _proj_gate_halves(block_of):
    return [pl.BlockSpec((256, CH), lambda n, j=j: (OFFS[7] // 256 + j, block_of(n))) for j in range(2)]


def _swap_halves_t(t):
    half = RET_DK // 2
    parts = []
    for h in range(RET_H):
        parts += [t[h * RET_DK + half:(h + 1) * RET_DK], t[h * RET_DK:h * RET_DK + half]]
    return jnp.concatenate(parts, axis=0)


def _rotate(t, cos, sin):
    return t * cos + _swap_halves_t(t) * sin


def _group_norm_t(o):
    mu = jnp.mean(o, axis=0, keepdims=True)
    var = jnp.mean((o - mu) * (o - mu), axis=0, keepdims=True)
    rstd = lax.rsqrt(var + GN_EPS)
    return (o - mu) * rstd, rstd


def _retention_scores_t(q_t, k_rm_b, q_mask, decay_all):
    q_heads = _bf(jnp.concatenate([q_t] * RET_H, axis=1) * q_mask)
    return _dot(k_rm_b, q_heads) * decay_all


def _retention_fwd(proj, ret_norm_w, tables, w_out_blk):
    cos_t, sin_t, decay_all, zeta_tab, xi_tab, state_mask, state_decay, q_mask = tables

    def body(rq_ref, rk_ref, rv_ref, rg_ref, w_ref, cos_ref, sin_ref, decay_ref, zeta_ref, xi_ref, smask_ref,
             sdecay_ref, qmask_ref, wout_ref, ro_ref, mix_ref, st_ref, wo_ref, state, wo_own, wo_local, wo_send, wo_recv):
        n = pl.program_id(0)

        @pl.when(n == 0)
        def _():
            state[...] = jnp.zeros_like(state)
            _gather_w_out(wout_ref, wo_own, wo_ref, wo_local, wo_send, wo_recv, phase=0)

        @pl.when(n == N_CH // 2)
        def _():
            _gather_w_out(wout_ref, wo_own, wo_ref, wo_local, wo_send, wo_recv, phase=1)

        @pl.when(n == N_CH - 1)
        def _():
            _gather_w_out(wout_ref, wo_own, wo_ref, wo_local, wo_send, wo_recv, phase=2)

        cos, sin = cos_ref[...], sin_ref[...]
        q_t = _rotate(rq_ref[...], cos, sin)
        k_t = _rotate(rk_ref[...], cos, sin) * (RET_DK ** -0.5)
        k_rm = k_t.T
        v_b = _bf(rv_ref[...])
        m_b = _bf(state[...])
        st_ref[0] = m_b
        scores_b = _bf(_retention_scores_t(q_t, _bf(k_rm), qmask_ref[...], decay_ref[...]))
        cross = _dot(m_b, _bf(q_t * xi_ref[...]))
        state[...] = state[...] * sdecay_ref[...] + _dot(v_b, _bf(k_rm * zeta_ref[...])) * smask_ref[...]
        heads = range(RET_H)
        rows = [slice(h * RET_DV, (h + 1) * RET_DV) for h in heads]
        o = [_dot(v_b[rows[h]], scores_b[:, h * CH:(h + 1) * CH]) + cross[rows[h]] for h in heads]
        for h in heads:
            rn, _ = _group_norm_t(o[h])
            g = rg_ref[rows[h], :]
            ro_ref[rows[h], :] = o[h]
            mix_ref[rows[h], :] = rn * w_ref[rows[h], :] * (g * _sigmoid(g))

    col = lambda w: pl.BlockSpec((w, CH), lambda n: (0, n))
    const = lambda shape: pl.BlockSpec(shape, lambda n: (0,) * len(shape))
    cur = lambda n: n
    return _Part(
        body,
        inputs=[proj, proj, proj, proj, ret_norm_w, cos_t, sin_t, decay_all, zeta_tab, xi_tab, state_mask, state_decay,
                q_mask, w_out_blk],
        in_specs=[_proj_piece(i, cur) for i in range(4)] + [
            const((RET_V, 1)), col(RET_QK), col(RET_QK),
            const(decay_all.shape), const(zeta_tab.shape), const(xi_tab.shape), const(state_mask.shape),
            const(state_decay.shape), const(q_mask.shape), const((WOUT_BLK, D))],
        out_specs=[col(RET_V), col(RET_V), pl.BlockSpec((1, RET_V, RET_QK), lambda n: (n, 0, 0)),
                   pl.BlockSpec(memory_space=pl.ANY)],
        out_shape=[jax.ShapeDtypeStruct((RET_V, T), F32), jax.ShapeDtypeStruct((RET_V, T), F32),
                   jax.ShapeDtypeStruct((N_CH, RET_V, RET_QK), BF16), jax.ShapeDtypeStruct((N_DEV, WOUT_BLK, D), BF16)],
        scratch_shapes=[pltpu.VMEM((RET_V, RET_QK), F32), pltpu.VMEM((WOUT_BLK, D), BF16), pltpu.SemaphoreType.DMA,
                        pltpu.SemaphoreType.DMA((N_DEV - 1,)), pltpu.SemaphoreType.DMA((N_DEV - 1,))])


def _retention_bwd(proj, ro, states, dmix, ret_norm_w, tables):
    cos_t, sin_t, decay_all, zeta_tab, xi_tab, state_mask, state_decay, q_mask = tables

    def body(*refs):
        @pl.when(pl.program_id(0) < N_CH)
        def _():
            chunk(*refs)

    def chunk(rq_ref, rk_ref, rv_ref, rg_ref, ro_ref, st_ref, dm_ref, w_ref, cos_ref, sin_ref, decay_ref, zeta_ref,
              xi_ref, smask_ref, sdecay_ref, qmask_ref, d_ref, dw_ref, gstate):
        i = pl.program_id(0)

        @pl.when(i == 0)
        def _():
            gstate[...] = jnp.zeros_like(gstate)
            dw_ref[...] = jnp.zeros_like(dw_ref)

        cos, sin = cos_ref[...], sin_ref[...]
        q_t = _rotate(rq_ref[...], cos, sin)
        k_t = _rotate(rk_ref[...], cos, sin) * (RET_DK ** -0.5)
        q_b, k_b = _bf(q_t), _bf(k_t)
        k_rm = k_t.T
        kz_b = _bf(k_rm * zeta_ref[...])
        qx_b = _bf(q_t * xi_ref[...])
        v_t = rv_ref[...]
        v_b = _bf(v_t)
        v_rm_b = _bf(v_t.T)
        decay = decay_ref[...]
        scores_b = _bf(_retention_scores_t(q_t, _bf(k_rm), qmask_ref[...], decay))
        heads = range(RET_H)
        rows = [slice(h * RET_DV, (h + 1) * RET_DV) for h in heads]
        qk_rows = [slice(h * RET_DK, (h + 1) * RET_DK) for h in heads]
        lanes = [slice(h * CH, (h + 1) * CH) for h in heads]
        do = []
        for h in heads:
            g, w, dm = rg_ref[rows[h], :], w_ref[rows[h], :], dm_ref[rows[h], :]
            rn, rstd = _group_norm_t(ro_ref[rows[h], :])
            sig = _sigmoid(g)
            silu = g * sig
            d_ref[OFFS[3] + h * RET_DV:OFFS[3] + (h + 1) * RET_DV, :] = _bf(dm * rn * w * (sig * (1.0 + g * (1.0 - sig))))
            dw_ref[rows[h], :] += dm * silu * rn
            drn = dm * silu * w
            do.append(rstd * (drn - jnp.mean(drn, axis=0, keepdims=True)
                              - rn * jnp.mean(drn * rn, axis=0, keepdims=True)))
        do_b = _bf(jnp.concatenate(do, axis=0))
        m_b = st_ref[0]
        g_all = gstate[...]
        g_b = _bf(g_all)
        dscores_b = [_bf(_dot(v_rm_b[:, rows[h]], do_b[rows[h]]) * decay[:, lanes[h]]) for h in heads]
        dq_cross = lax.dot_general(m_b, do_b, (((0,), (0,)), ((), ())), preferred_element_type=F32)
        dkz = _dot(v_rm_b, g_b)
        dv_cross = _dot_nt(g_b, kz_b)
        gstate[...] = g_all * sdecay_ref[...] + _dot_nt(do_b, qx_b) * smask_ref[...]
        dq = jnp.concatenate([_dot(k_b[qk_rows[h]], dscores_b[h]) for h in heads], axis=0) + dq_cross * xi_ref[...]
        dk = (jnp.concatenate([_dot_nt(q_b[qk_rows[h]], dscores_b[h]) for h in heads], axis=0)
              + (dkz * zeta_ref[...]).T) * (RET_DK ** -0.5)
        for h in heads:
            d_ref[OFFS[2] + h * RET_DV:OFFS[2] + (h + 1) * RET_DV, :] = _bf(
                _dot_nt(do_b[rows[h]], scores_b[:, lanes[h]]) + dv_cross[rows[h]])
        d_ref[OFFS[0]:OFFS[1], :] = _bf(dq * cos + _swap_halves_t(dq * sin))
        d_ref[OFFS[1]:OFFS[2], :] = _bf(dk * cos + _swap_halves_t(dk * sin))

    chunk_of = lambda i: N_CH - 1 - jnp.minimum(i, N_CH - 1)
    col = lambda w: pl.BlockSpec((w, CH), lambda i: (0, chunk_of(i)))
    const = lambda shape: pl.BlockSpec(shape, lambda i: (0,) * len(shape))
    return _Part(
        body,
        inputs=[proj, proj, proj, proj, ro, states, dmix, ret_norm_w, cos_t, sin_t, decay_all, zeta_tab, xi_tab,
                state_mask, state_decay, q_mask],
        in_specs=[_proj_piece(j, chunk_of) for j in range(4)] + [
                  col(RET_V),
                  pl.BlockSpec((1, RET_V, RET_QK), lambda i: (chunk_of(i), 0, 0)), col(RET_V),
                  const((RET_V, 1)), col(RET_QK), col(RET_QK),
                  const(decay_all.shape), const(zeta_tab.shape), const(xi_tab.shape), const(state_mask.shape),
                  const(state_decay.shape), const(q_mask.shape)],
        out_specs=[col(RET_W), const((RET_V, CH))],
        out_shape=[jax.ShapeDtypeStruct((RET_W, T), BF16), jax.ShapeDtypeStruct((RET_V, CH), F32)],
        scratch_shapes=[pltpu.VMEM((RET_V, RET_QK), F32)])


HQ_LANES = SWA_H * CH


def _head_lanes(hq):
    return slice(hq * CH, (hq + 1) * CH)


def _build_bias_t(bucket_ref, relb_ref, sinks_ref, bias_ref, sink_row):
    bt = bucket_ref[...]
    first = lax.broadcasted_iota(jnp.int32, bt.shape, 0) < CH
    for hq in range(SWA_H):
        b = jnp.full(bt.shape, NEG_INF, F32)
        for bk in range(N_BUCKETS):
            b = jnp.where(bt == bk, relb_ref[hq, bk], b)
        bias_ref[0, :, _head_lanes(hq)] = b
        bias_ref[1, :, _head_lanes(hq)] = jnp.where(first, NEG_INF, b)
        sink_row[:, _head_lanes(hq)] = jnp.full((1, CH), sinks_ref[hq], F32)


def _rms_t(t, w_col):
    r = lax.rsqrt(jnp.mean(t * t, axis=0, keepdims=True) + NORM_EPS)
    return t * r * w_col, r


def _rms_t_bwd(dn, t, r, w_col):
    u = dn * w_col
    return r * u - t * (r * r * r) * jnp.mean(u * t, axis=0, keepdims=True), dn * t * r


def _norm_kv(k_t, kw):
    return jnp.concatenate([_rms_t(k_t[g * SWA_D:(g + 1) * SWA_D], kw)[0] for g in range(SWA_KV)], axis=0)


def _kv_slot(a, kv):
    z = jnp.zeros_like(a)
    return jnp.concatenate([a, z] if kv == 0 else [z, a], axis=0)


def _softmax_t(s, sink):
    m = jnp.maximum(jnp.max(s, axis=0, keepdims=True), sink)
    p = jnp.exp(s - m)
    e_sink = jnp.exp(sink - m)
    inv = 1.0 / (jnp.sum(p, axis=0, keepdims=True) + e_sink)
    return p * inv, e_sink * inv


def _gate_rows(sg_halves, hq):
    per_half = SWA_H // 2
    return sg_halves[hq // per_half][(hq % per_half) * SWA_D:(hq % per_half + 1) * SWA_D, :]


def _swa_fwd(proj, q_norm_w, k_norm_w, sinks, rel_bias_t, bucket_t):
    def body(sq_ref, skp_ref, skc_ref, svp_ref, svc_ref, sg_lo, sg_hi, qw_ref, kw_ref, sinks_ref, relb_ref, bucket_ref,
             so_ref, mix_ref, bias_ref, sink_row):
        n = pl.program_id(0)

        @pl.when(n == 0)
        def _():
            _build_bias_t(bucket_ref, relb_ref, sinks_ref, bias_ref, sink_row)

        var = (n == 0).astype(jnp.int32)
        qw, kw = qw_ref[...], kw_ref[...]
        kn_band = jnp.concatenate([_norm_kv(skp_ref[...], kw), _norm_kv(skc_ref[...], kw)], axis=1)
        kb_rm = _bf(kn_band.T)
        vband = _bf(jnp.concatenate([svp_ref[...], svc_ref[...]], axis=1))
        q_all = jnp.concatenate(
            [_bf(_kv_slot(_rms_t(sq_ref[hq * SWA_D:(hq + 1) * SWA_D, :], qw)[0] * (SWA_D ** -0.5), hq // SWA_G))
             for hq in range(SWA_H)], axis=1)
        probs, _ = _softmax_t(_dot(kb_rm, q_all) + bias_ref[var], sink_row[...])
        probs_b = _bf(probs)
        for kv in range(SWA_KV):
            o = _dot(vband[kv * SWA_D:(kv + 1) * SWA_D], probs_b[:, kv * SWA_G * CH:(kv + 1) * SWA_G * CH])
            for j in range(SWA_G):
                rows = slice((kv * SWA_G + j) * SWA_D, (kv * SWA_G + j + 1) * SWA_D)
                oh = o[:, j * CH:(j + 1) * CH]
                g = _gate_rows((sg_lo, sg_hi), kv * SWA_G + j)
                so_ref[rows, :] = oh
                mix_ref[rows, :] = oh * (g * _sigmoid(g))

    col = lambda w: pl.BlockSpec((w, CH), lambda n: (0, n))
    const = lambda shape: pl.BlockSpec(shape, lambda n: (0,) * len(shape))
    smem = pl.BlockSpec(memory_space=pltpu.SMEM)
    cur = lambda n: n
    prev = lambda n: jnp.maximum(n - 1, 0)
    return _Part(
        body,
        inputs=[proj, proj, proj, proj, proj, proj, proj, q_norm_w, k_norm_w, sinks, rel_bias_t, bucket_t],
        in_specs=[_proj_piece(4, cur), _proj_piece(5, prev), _proj_piece(5, cur), _proj_piece(6, prev),
                  _proj_piece(6, cur)] + _proj_gate_halves(cur) + [
                  const((SWA_D, 1)), const((SWA_D, 1)), smem, smem, const((2 * CH, CH))],
        out_specs=[col(512), col(512)],
        out_shape=[jax.ShapeDtypeStruct((512, T), F32), jax.ShapeDtypeStruct((512, T), F32)],
        scratch_shapes=[pltpu.VMEM((2, 2 * CH, HQ_LANES), F32), pltpu.VMEM((1, HQ_LANES), F32)])


def _swa_bwd(proj, so, dmix, q_norm_w, k_norm_w, sinks, rel_bias_t, bucket_t):
    def body(sq_ref, skp_ref, skc_ref, svp_ref, svc_ref, sg_lo, sg_hi, so_ref, dm_ref, qw_ref, kw_ref, sinks_ref,
             relb_ref, bucket_ref, d_ref, dbias_ref, dsink_ref, dqw_ref, dkw_ref,
             bias_ref, sink_row, band_dk, band_dv, carry_dk, carry_dv, hold_dq, hold_dg):
        n = pl.program_id(0)
        slot = n % 2

        @pl.when(n == 0)
        def _():
            _build_bias_t(bucket_ref, relb_ref, sinks_ref, bias_ref, sink_row)
            for ref in (dbias_ref, dsink_ref, dqw_ref, dkw_ref, carry_dk, carry_dv):
                ref[...] = jnp.zeros_like(ref)

        qw, kw = qw_ref[...], kw_ref[...]

        @pl.when(n < N_CH)
        def _():
            var = (n == 0).astype(jnp.int32)
            kn_band = jnp.concatenate([_norm_kv(skp_ref[...], kw), _norm_kv(skc_ref[...], kw)], axis=1)
            kb_rm = _bf(kn_band.T)
            kn_band_s = _bf(kn_band * (SWA_D ** -0.5))
            vband_f = jnp.concatenate([svp_ref[...], svc_ref[...]], axis=1)
            vb_rm = _bf(vband_f.T)
            q_raw, q_rstd, qs_b, do_b = [], [], [], []
            for hq in range(SWA_H):
                rows = slice(hq * SWA_D, (hq + 1) * SWA_D)
                q_t = sq_ref[rows, :]
                qn, rq = _rms_t(q_t, qw)
                g = _gate_rows((sg_lo, sg_hi), hq)
                sig = _sigmoid(g)
                dm = dm_ref[rows, :]
                hold_dg[slot, rows, :] = _bf(dm * so_ref[rows, :] * (sig * (1.0 + g * (1.0 - sig))))
                q_raw.append(q_t)
                q_rstd.append(rq)
                qs_b.append(_bf(qn * (SWA_D ** -0.5)))
                do_b.append(_bf(dm * (g * sig)))
            q_all = jnp.concatenate([_kv_slot(qs_b[hq], hq // SWA_G) for hq in range(SWA_H)], axis=1)
            do_all = jnp.concatenate([_kv_slot(do_b[hq], hq // SWA_G) for hq in range(SWA_H)], axis=1)
            probs, p_sink = _softmax_t(_dot(kb_rm, q_all) + bias_ref[var], sink_row[...])
            dprobs = _dot(vb_rm, do_all)
            t = jnp.sum(probs * dprobs, axis=0, keepdims=True)
            dlog = probs * (dprobs - t)
            dsink_ref[...] += -(p_sink * t)
            dbias_ref[...] += dlog
            dlog_b, probs_b = _bf(dlog), _bf(probs)
            dkn, dvv = [], []
            for kv in range(SWA_KV):
                heads = range(kv * SWA_G, (kv + 1) * SWA_G)
                lanes = slice(kv * SWA_G * CH, (kv + 1) * SWA_G * CH)
                dvv.append(_dot_nt(jnp.concatenate([do_b[hq] for hq in heads], axis=1), probs_b[:, lanes]))
                dkn.append(_dot_nt(jnp.concatenate([qs_b[hq] for hq in heads], axis=1), dlog_b[:, lanes]))
                dqn = _dot(kn_band_s[kv * SWA_D:(kv + 1) * SWA_D], dlog_b[:, lanes])
                for j, hq in enumerate(heads):
                    dq_t, dqw_terms = _rms_t_bwd(dqn[:, j * CH:(j + 1) * CH], q_raw[hq], q_rstd[hq], qw)
                    hold_dq[slot, hq * SWA_D:(hq + 1) * SWA_D, :] = _bf(dq_t)
                    dqw_ref[...] += dqw_terms
            band_dk[...] = jnp.concatenate(dkn, axis=0)
            band_dv[...] = jnp.concatenate(dvv, axis=0)

        @pl.when(n == N_CH)
        def _():
            band_dk[...] = jnp.zeros_like(band_dk)
            band_dv[...] = jnp.zeros_like(band_dv)

        @pl.when(n >= 1)
        def _():
            dkn_prev = carry_dk[...] + band_dk[:, 0:CH]
            k_t = skp_ref[...]
            for kv in range(SWA_KV):
                rows = slice(kv * SWA_D, (kv + 1) * SWA_D)
                _, rk = _rms_t(k_t[rows], kw)
                dk_t, dkw_terms = _rms_t_bwd(dkn_prev[rows], k_t[rows], rk, kw)
                d_ref[SWA_OFFS[1] + kv * SWA_D:SWA_OFFS[1] + (kv + 1) * SWA_D, :] = _bf(dk_t)
                dkw_ref[...] += dkw_terms
            d_ref[SWA_OFFS[2]:SWA_OFFS[3], :] = _bf(carry_dv[...] + band_dv[:, 0:CH])
            d_ref[SWA_OFFS[0]:SWA_OFFS[1], :] = hold_dq[1 - slot]
            d_ref[SWA_OFFS[3]:SWA_OFFS[4], :] = hold_dg[1 - slot]

        carry_dk[...] = band_dk[:, CH:2 * CH]
        carry_dv[...] = band_dv[:, CH:2 * CH]

    cur_block = lambda n: jnp.minimum(n, N_CH - 1)
    prev_block = lambda n: jnp.maximum(n - 1, 0)
    col = lambda w: pl.BlockSpec((w, CH), lambda n: (0, cur_block(n)))
    prev = lambda w: pl.BlockSpec((w, CH), lambda n: (0, prev_block(n)))
    const = lambda shape: pl.BlockSpec(shape, lambda n: (0,) * len(shape))
    smem = pl.BlockSpec(memory_space=pltpu.SMEM)
    return _Part(
        body,
        inputs=[proj, proj, proj, proj, proj, proj, proj, so, dmix, q_norm_w, k_norm_w, sinks, rel_bias_t, bucket_t],
        in_specs=[_proj_piece(4, cur_block), _proj_piece(5, prev_block), _proj_piece(5, cur_block),
                  _proj_piece(6, prev_block), _proj_piece(6, cur_block)] + _proj_gate_halves(cur_block) + [
                  col(512), col(512), const((SWA_D, 1)), const((SWA_D, 1)), smem, smem, const((2 * CH, CH))],
        out_specs=[prev(SWA_W), const((2 * CH, HQ_LANES)), const((1, HQ_LANES)),
                   const((SWA_D, CH)), const((SWA_D, CH))],
        out_shape=[jax.ShapeDtypeStruct((SWA_W, T), BF16),
                   jax.ShapeDtypeStruct((2 * CH, HQ_LANES), F32), jax.ShapeDtypeStruct((1, HQ_LANES), F32),
                   jax.ShapeDtypeStruct((SWA_D, CH), F32), jax.ShapeDtypeStruct((SWA_D, CH), F32)],
        scratch_shapes=[pltpu.VMEM((2, 2 * CH, HQ_LANES), F32), pltpu.VMEM((1, HQ_LANES), F32),
                        pltpu.VMEM((128, 2 * CH), F32), pltpu.VMEM((128, 2 * CH), F32),
                        pltpu.VMEM((128, CH), F32), pltpu.VMEM((128, CH), F32),
                        pltpu.VMEM((2, 512, CH), BF16), pltpu.VMEM((2, 512, CH), BF16)])


SMALL_GRAD_SHAPES = ((1, RET_V), (1, CH), (1, CH), (1, CH), (SWA_H, N_BUCKETS))


def _finish_small_grads(dbias_ref, bucket_ref, dsink_ref, dqw_ref, dkw_ref, drw_ref, rw_o, qw_o, kw_o, sink_o, relb_o):
    bt = bucket_ref[...]
    row = lax.broadcasted_iota(jnp.int32, (SWA_H, N_BUCKETS), 0)
    col = lax.broadcasted_iota(jnp.int32, (SWA_H, N_BUCKETS), 1)
    lane = lax.broadcasted_iota(jnp.int32, (1, CH), 1)
    acc = jnp.zeros((SWA_H, N_BUCKETS), F32)
    sink = jnp.zeros((1, CH), F32)
    for hq in range(SWA_H):
        d = dbias_ref[:, _head_lanes(hq)]
        for bk in range(N_BUCKETS):
            s = jnp.sum(jnp.sum(jnp.where(bt == bk, d, 0.0), axis=0, keepdims=True), axis=1, keepdims=True)
            acc = acc + jnp.where((row == hq) & (col == bk), s, 0.0)
        sink = sink + jnp.where(lane == hq, jnp.sum(dsink_ref[:, _head_lanes(hq)], axis=1, keepdims=True), 0.0)
    relb_o[...] = acc
    sink_o[...] = sink
    for src, dst in ((dqw_ref, qw_o), (dkw_ref, kw_o)):
        padded = jnp.concatenate([src[...], jnp.zeros((CH - SWA_D, CH), F32)], axis=0)
        dst[...] = jnp.sum(padded.T, axis=0, keepdims=True)
    rw_o[...] = jnp.sum(drw_ref[...].T, axis=0, keepdims=True)


def _out_proj(mix_r, mix_s, wo, x, target):
    def body(mr_ref, ms_ref, w_ref, x_ref, t_ref, loss_ref, dy_ref, dmr_ref, dms_ref, gw_ref):
        i = pl.program_id(0)

        @pl.when(i == 0)
        def _():
            loss_ref[...] = jnp.zeros_like(loss_ref)
            gw_ref[...] = jnp.zeros_like(gw_ref)

        mixed = jnp.concatenate([mr_ref[...], ms_ref[...]], axis=0)
        w = w_ref[...]
        err = x_ref[...] + _dot(_bf(mixed.T), w) - t_ref[...]
        loss_ref[...] += jnp.sum(jnp.sum(err * err, axis=1, keepdims=True), axis=0, keepdims=True)
        dy = err * (1.0 / D)
        dy_ref[...] = dy
        dy_b = _bf(dy)
        dmix = _dot_nt(w, dy_b)
        dmr_ref[...] = dmix[0:512]
        dms_ref[...] = dmix[512:D]
        gw_ref[...] += _dot(_bf(mixed), dy_b)

    row = lambda w: pl.BlockSpec((TM, w), lambda i: (i, 0))
    col = lambda w: pl.BlockSpec((w, TM), lambda i: (0, i))
    const = lambda shape: pl.BlockSpec(shape, lambda i: (0,) * len(shape))
    return pl.pallas_call(
        body, name="out_proj", grid=(T // TM,),
        in_specs=[col(512), col(512), const((D, D)), row(D), row(D)],
        out_specs=(const((1, 1)), row(D), col(512), col(512), const((D, D))),
        out_shape=(jax.ShapeDtypeStruct((1, 1), F32), jax.ShapeDtypeStruct((T, D), F32),
                   jax.ShapeDtypeStruct((512, T), F32), jax.ShapeDtypeStruct((512, T), F32),
                   jax.ShapeDtypeStruct((D, D), F32)),
        compiler_params=_cparams(1),
    )(mix_r, mix_s, wo, x, target)


LOSS_ROW = 5
ALL_CHIPS = ((0, 0), (0, 1), (1, 0), (1, 1))


def _w_out_reduce_scatter(gwo):
    def body(gwo_ref, gout_o, own, rcv, snd, got, tot, local_sems, a_send, a_recv, b_send, b_recv):
        k = pl.program_id(0)
        x_, y_, c_ = _mesh_pos()
        sibling = (x_, y_, 1 - c_)
        rel_chips = [(1 - x_, y_), (x_, 1 - y_), (1 - x_, 1 - y_)]

        def to_sibling(j):
            px, py = ALL_CHIPS[j]
            return pltpu.make_async_remote_copy(src_ref=gwo_ref.at[_blk(px, py, 1 - c_)], dst_ref=rcv.at[j],
                                                send_sem=a_send.at[j], recv_sem=a_recv.at[j], device_id=sibling,
                                                device_id_type=MESH)

        def local(j):
            px, py = ALL_CHIPS[j]
            return pltpu.make_async_copy(gwo_ref.at[_blk(px, py, c_)], own.at[j], local_sems.at[j])

        def to_chip(q):
            return pltpu.make_async_remote_copy(src_ref=snd.at[q], dst_ref=got.at[q], send_sem=b_send.at[q],
                                                recv_sem=b_recv.at[q], device_id=(*rel_chips[q], c_), device_id_type=MESH)

        @pl.when(k == 0)
        def _():
            for j in range(len(ALL_CHIPS)):
                to_sibling(j).start()
                local(j).start()

        @pl.when(k == 2)
        def _():
            for j in range(len(ALL_CHIPS)):
                local(j).wait()
                to_sibling(j).wait_recv()
            for q in range(3):
                j = 2 * rel_chips[q][0] + rel_chips[q][1]
                snd[q] = _bf(own[j] + rcv[j])
                to_chip(q).start()
            jm = 2 * x_ + y_
            tot[...] = own[jm] + rcv[jm]

        @pl.when(k == N_CH)
        def _():
            g = tot[...]
            for q in range(3):
                to_chip(q).wait_recv()
                g = g + got[q].astype(F32)
            gout_o[...] = g
            for j in range(len(ALL_CHIPS)):
                to_sibling(j).wait_send()
            for q in range(3):
                to_chip(q).wait_send()

    dma = pltpu.SemaphoreType.DMA
    return _Part(
        body, inputs=[gwo], in_specs=[pl.BlockSpec(memory_space=pl.ANY)],
        out_specs=[pl.BlockSpec((WOUT_BLK, D), lambda k: (0, 0))], out_shape=[jax.ShapeDtypeStruct((WOUT_BLK, D), F32)],
        scratch_shapes=[pltpu.VMEM((4, WOUT_BLK, D), F32), pltpu.VMEM((4, WOUT_BLK, D), F32),
                        pltpu.VMEM((3, WOUT_BLK, D), BF16), pltpu.VMEM((3, WOUT_BLK, D), BF16),
                        pltpu.VMEM((WOUT_BLK, D), F32), dma((4,)), dma((4,)), dma((4,)), dma((3,)), dma((3,))])


def _in_proj_bwd(d_ret, d_swa, wt, x, norm_w, dy, sse, small_acc):
    n_tiles = T // TM

    def body(da_ref, db_ref, w_any, x_ref, nw_ref, dy_ref, sse_ref,
             dbias_ref, bucket_ref, dsink_ref, dqw_ref, dkw_ref, drw_ref, gx_ref, taba_o, tabb_o,
             w_ref, gnw_acc, rw_ref, qw_ref, kw_ref, sk_ref, rb_ref, w_sem):
        k = pl.program_id(0)

        @pl.when(k == 0)
        def _():
            w_load = pltpu.make_async_copy(w_any, w_ref, w_sem.at[0])
            w_load.start()
            gnw_acc[...] = jnp.zeros_like(gnw_acc)
            _finish_small_grads(dbias_ref, bucket_ref, dsink_ref, dqw_ref, dkw_ref, drw_ref,
                                rw_ref, qw_ref, kw_ref, sk_ref, rb_ref)
            w_load.wait()

        dp = jnp.concatenate([da_ref[...], db_ref[...]], axis=0)
        xv, nw = x_ref[...], nw_ref[...]
        r = lax.rsqrt(jnp.mean(xv * xv, axis=-1, keepdims=True) + NORM_EPS)
        dh = lax.dot_general(dp, w_ref[...], (((0,), (0,)), ((), ())), preferred_element_type=F32)
        u = dh * nw
        gx_ref[...] = dy_ref[...] + r * u - xv * (r * r * r) * jnp.mean(u * xv, axis=-1, keepdims=True)
        gnw_acc[...] += jnp.sum(dh * (xv * r), axis=0, keepdims=True)

        @pl.when(k == n_tiles - 1)
        def _():
            taba_o[...] = jnp.zeros_like(taba_o)
            tabb_o[...] = jnp.zeros_like(tabb_o)
            for row, ref in enumerate((gnw_acc, rw_ref, qw_ref, kw_ref, sk_ref)):
                taba_o[row:row + 1, 0:ref.shape[1]] = ref[...]
            taba_o[LOSS_ROW:LOSS_ROW + 1, 0:1] = sse_ref[...]
            tabb_o[:, 0:N_BUCKETS] = rb_ref[...]

    row = lambda w: pl.BlockSpec((TM, w), lambda k: (k, 0))
    col = lambda w: pl.BlockSpec((w, TM), lambda k: (0, k))
    const = lambda shape: pl.BlockSpec(shape, lambda k: (0,) * len(shape))
    return pl.pallas_call(
        body, name="in_proj_bwd", grid=(n_tiles,),
        in_specs=[col(RET_W), col(SWA_W), pl.BlockSpec(memory_space=pltpu.HBM),
                  row(D), const((1, D)), row(D), const((1, 1))]
                 + [const(a.shape) for a in small_acc],
        out_specs=(row(D), const((8, D)), const((SWA_H, CH))),
        out_shape=(jax.ShapeDtypeStruct((T, D), F32), jax.ShapeDtypeStruct((8, D), F32),
                   jax.ShapeDtypeStruct((SWA_H, CH), F32)),
        scratch_shapes=[pltpu.VMEM((D_IN, D), BF16), pltpu.VMEM((1, D), F32)]
                       + [pltpu.VMEM(s, F32) for s in SMALL_GRAD_SHAPES] + [pltpu.SemaphoreType.DMA((1,))],
        compiler_params=_cparams(1),
    )(d_ret, d_swa, wt, x, norm_w, dy, sse, *small_acc)


def _w_in_grad_rs(d_ret, d_swa, h, tab_a_mine, tab_b_mine):
    n_blocks = N_DEV

    def body(da_any, db_any, h_any, taba_ref, tabb_ref, gin_o, suma_o, sumb_o,
             stage, d2d_src, mine, rcv, snd, got, tot, tab_a, tab_b,
             h_ref, dp_sems, d2d_send, d2d_recv, ici_send, ici_recv, s_send, s_recv, h_sem):
        k = pl.program_id(0)
        x_, y_, c_ = _mesh_pos()
        h_load = pltpu.make_async_copy(h_any, h_ref, h_sem.at[0])
        me = _blk(x_, y_, c_)
        sibling = (x_, y_, 1 - c_)
        rel_chips = [(1 - x_, y_), (x_, 1 - y_), (1 - x_, 1 - y_), (x_, y_)]

        def block_of_step(s):
            return _blk(*rel_chips[s // 2], 1 - c_ if s % 2 == 0 else c_)

        def fetch(s, wait):
            slot, b = s % 2, block_of_step(s)
            split = RET_W - 4 * WIN_BLK

            def run(src, dst, sem):
                cp = pltpu.make_async_copy(src, dst, sem)
                cp.wait() if wait else cp.start()

            @pl.when(b < 4)
            def _():
                run(da_any.at[pl.ds(pl.multiple_of(b * WIN_BLK, 16), WIN_BLK), :], stage.at[slot], dp_sems.at[slot, 0])

            @pl.when(b == 4)
            def _():
                run(da_any.at[pl.ds(4 * WIN_BLK, split), :], stage.at[slot, pl.ds(0, split), :], dp_sems.at[slot, 0])
                run(db_any.at[pl.ds(0, WIN_BLK - split), :], stage.at[slot, pl.ds(split, WIN_BLK - split), :],
                    dp_sems.at[slot, 1])

            @pl.when(b > 4)
            def _():
                run(db_any.at[pl.ds(pl.multiple_of(b * WIN_BLK - RET_W, 16), WIN_BLK), :], stage.at[slot],
                    dp_sems.at[slot, 0])

        def d2d_copy(r):
            return pltpu.make_async_remote_copy(src_ref=d2d_src, dst_ref=rcv.at[r], send_sem=d2d_send.at[r],
                                                recv_sem=d2d_recv.at[r], device_id=sibling, device_id_type=MESH)

        def ici_copy(r):
            return pltpu.make_async_remote_copy(src_ref=snd.at[r], dst_ref=got.at[r], send_sem=ici_send.at[r],
                                                recv_sem=ici_recv.at[r], device_id=(*rel_chips[r], c_),
                                                device_id_type=MESH)

        def table_copies():
            return [pltpu.make_async_remote_copy(src_ref=tab.at[me], dst_ref=tab.at[me], send_sem=s_send.at[a, p - 1],
                                                 recv_sem=s_recv.at[a, p - 1],
                                                 device_id=(x_ ^ (p >> 2), y_ ^ ((p >> 1) & 1), c_ ^ (p & 1)),
                                                 device_id_type=MESH)
                    for p in range(1, N_DEV) for a, tab in enumerate((tab_a, tab_b))]

        def chip_sum(r):
            d2d_copy(r).wait_recv()
            total = mine[...] + rcv[r]
            if r < 3:
                snd[r] = _bf(total)
                ici_copy(r).start()
            else:
                tot[...] = total

        for s in range(n_blocks):
            @pl.when(k == s)
            def _(s=s):
                r = s // 2
                if s == 0:
                    h_load.start()
                    fetch(0, wait=False)
                    tab_a[me] = taba_ref[...]
                    tab_b[me] = tabb_ref[...]
                    for cp in table_copies():
                        cp.start()
                fetch(s, wait=True)
                if s + 1 < n_blocks:
                    fetch(s + 1, wait=False)
                if s == 0:
                    h_load.wait()
                gw = _dot(stage[s % 2], h_ref[...])
                if s % 2 == 0:
                    if s >= 2:
                        chip_sum(r - 1)
                        d2d_copy(r - 1).wait_send()
                    d2d_src[...] = gw
                    d2d_copy(r).start()
                else:
                    mine[...] = gw

        @pl.when(k == n_blocks)
        def _():
            chip_sum(3)
            tables = table_copies()
            g_in = tot[...]
            for q in range(3):
                ici_copy(q).wait_recv()
                g_in = g_in + got[q].astype(F32)
            gin_o[...] = g_in
            for cp in tables:
                cp.wait_recv()
            sum_a, sum_b = tab_a[0], tab_b[0]
            for b in range(1, N_DEV):
                sum_a = sum_a + tab_a[b]
                sum_b = sum_b + tab_b[b]
            suma_o[...] = sum_a
            sumb_o[...] = sum_b
            d2d_copy(3).wait_send()
            for q in range(3):
                ici_copy(q).wait_send()
            for cp in tables:
                cp.wait_send()

    const = lambda shape: pl.BlockSpec(shape, lambda k: (0,) * len(shape))
    hbm = pl.BlockSpec(memory_space=pltpu.HBM)
    dma = pltpu.SemaphoreType.DMA
    return pl.pallas_call(
        body, name="w_in_grad_rs", grid=(n_blocks + 1,),
        in_specs=[hbm, hbm, hbm, const((8, D)), const((SWA_H, CH))],
        out_specs=(const((WIN_BLK, D)), const((8, D)), const((SWA_H, CH))),
        out_shape=(jax.ShapeDtypeStruct((WIN_BLK, D), F32), jax.ShapeDtypeStruct((8, D), F32),
                   jax.ShapeDtypeStruct((SWA_H, CH), F32)),
        scratch_shapes=[
            pltpu.VMEM((2, WIN_BLK, T), BF16), pltpu.VMEM((WIN_BLK, D), F32),
            pltpu.VMEM((WIN_BLK, D), F32),
            pltpu.VMEM((4, WIN_BLK, D), F32), pltpu.VMEM((3, WIN_BLK, D), BF16),
            pltpu.VMEM((3, WIN_BLK, D), BF16), pltpu.VMEM((WIN_BLK, D), F32),
            pltpu.VMEM((N_DEV, 8, D), F32), pltpu.VMEM((N_DEV, SWA_H, CH), F32),
            pltpu.VMEM((T, D), BF16),
            dma((2, 2)), dma((4,)), dma((4,)), dma((3,)), dma((3,)), dma((2, 7)), dma((2, 7)), dma((1,)),
        ],
        compiler_params=_cparams(1),
    )(d_ret, d_swa, h, tab_a_mine, tab_b_mine)


SMALL_SHAPES = ((1, D), (1, 512), (1, SWA_D), (1, SWA_D), (1, SWA_H), (SWA_H, N_BUCKETS))


def _adamw_all(g_in_t, g_out, sum_a, sum_b, w_in_t, m_in_t, v_in_t, w_out, m_out, v_out, small_w, small_m, small_v):
    n_small = len(SMALL_SHAPES)
    halves = 2

    def body(*refs):
        gin_any, gout_any, suma_any, sumb_any, w_in_ref, m_in_ref, v_in_ref, w_out_ref, m_out_ref, v_out_ref = refs[:10]
        pos = 10
        sw_refs, sm_refs, sv_refs = (refs[pos + i * n_small:pos + (i + 1) * n_small] for i in range(3))
        pos += 3 * n_small
        g_in_o, d_in_o, nm_in_o, nv_in_o, g_out_o, d_out_o, nm_out_o, nv_out_o = refs[pos:pos + 8]
        pos += 8
        sg_o, sd_o, snm_o, snv_o = (refs[pos + i * n_small:pos + (i + 1) * n_small] for i in range(4))
        pos += 4 * n_small
        loss_o, gin_ref, gout_ref, suma_ref, sumb_ref, load_sems = refs[pos:]
        i = pl.program_id(0)

        @pl.when(i == 0)
        def _():
            loads = [pltpu.make_async_copy(src, dst, load_sems.at[j]) for j, (src, dst) in enumerate(
                ((gin_any, gin_ref), (gout_any, gout_ref), (suma_any, suma_ref), (sumb_any, sumb_ref)))]
            for cp in loads:
                cp.start()
            for cp in loads:
                cp.wait()

        g = gout_ref[pl.ds(pl.multiple_of(i * (WOUT_BLK // halves), 8), WOUT_BLK // halves), :]
        d, nm, nv = _adamw(w_out_ref[...], g, m_out_ref[...], v_out_ref[...])
        g_out_o[...], d_out_o[...], nm_out_o[...], nv_out_o[...] = g, d, nm, nv
        g = gin_ref[pl.ds(pl.multiple_of(i * (WIN_BLK // halves), 8), WIN_BLK // halves), :]
        d, nm, nv = _adamw(w_in_ref[...], g, m_in_ref[...], v_in_ref[...])
        g_in_o[...], d_in_o[...], nm_in_o[...], nv_in_o[...] = g, d, nm, nv

        @pl.when(pl.program_id(0) == 0)
        def _():
            loss_o[...] = suma_ref[LOSS_ROW:LOSS_ROW + 1, 0:1] * (0.5 / D)
            for r, (rows, lanes) in enumerate(SMALL_SHAPES):
                g = suma_ref[r:r + 1, 0:lanes] if rows == 1 else sumb_ref[:, 0:lanes]
                d, nm, nv = _adamw(sw_refs[r][...], g, sm_refs[r][...], sv_refs[r][...])
                sg_o[r][...], sd_o[r][...], snm_o[r][...], snv_o[r][...] = g, d, nm, nv

    half = lambda rows: pl.BlockSpec((rows // halves, D), lambda i: (i, 0))
    const = lambda shape: pl.BlockSpec(shape, lambda i: (0,) * len(shape))
    win = jax.ShapeDtypeStruct((WIN_BLK, D), F32)
    wout = jax.ShapeDtypeStruct((WOUT_BLK, D), F32)
    smalls = tuple(jax.ShapeDtypeStruct(s, F32) for s in SMALL_SHAPES)
    small_specs = [const(s) for s in SMALL_SHAPES]
    outs = pl.pallas_call(
        body, name="adamw_all", grid=(halves,),
        in_specs=[pl.BlockSpec(memory_space=pltpu.HBM)] * 4 + [half(WIN_BLK)] * 3 + [half(WOUT_BLK)] * 3 + small_specs * 3,
        out_specs=tuple([half(WIN_BLK)] * 4 + [half(WOUT_BLK)] * 4 + small_specs * 4 + [const((1, 1))]),
        out_shape=(win,) * 4 + (wout,) * 4 + smalls * 4 + (jax.ShapeDtypeStruct((1, 1), F32),),
        scratch_shapes=[pltpu.VMEM((WIN_BLK, D), F32), pltpu.VMEM((WOUT_BLK, D), F32), pltpu.VMEM((8, D), F32),
                        pltpu.VMEM((SWA_H, CH), F32), pltpu.SemaphoreType.DMA((4,))],
        compiler_params=_cparams(1, vmem=32 * 1024 * 1024),
    )(g_in_t, g_out, sum_a, sum_b, w_in_t, m_in_t, v_in_t, w_out, m_out, v_out, *small_w, *small_m, *small_v)
    big, rest = outs[:8], outs[8:]
    return big, [rest[i * n_small:(i + 1) * n_small] for i in range(4)], rest[4 * n_small]


def _small_rows(norm_w, ret_norm_w, q_norm_w, k_norm_w, sinks, rel_bias):
    return (norm_w.reshape(1, D), ret_norm_w.reshape(1, 512), q_norm_w.reshape(1, SWA_D), k_norm_w.reshape(1, SWA_D),
            sinks.reshape(1, SWA_H), rel_bias.T)


def _small_leaves(rows):
    return (rows[0].reshape(D), rows[1].reshape(512), rows[2].reshape(SWA_D), rows[3].reshape(SWA_D),
            rows[4].reshape(SWA_H), rows[5].T)


def kernel(x, norm_w, w_in, ret_norm_w, q_norm_w, k_norm_w, sinks, rel_bias, w_out, loss_target, m_norm_w, m_w_in, m_ret_norm_w, m_q_norm_w, m_k_norm_w, m_sinks, m_rel_bias, m_w_out, v_norm_w, v_w_in, v_ret_norm_w, v_q_norm_w, v_k_norm_w, v_sinks, v_rel_bias, v_w_out):
    x2 = x.reshape(T, D)
    target = loss_target.reshape(T, D)
    nw = norm_w.reshape(1, D)
    rnw_col = ret_norm_w.reshape(RET_V, 1)
    qnw_col = q_norm_w.reshape(SWA_D, 1)
    knw_col = k_norm_w.reshape(SWA_D, 1)
    relb = rel_bias.T
    ret_tables = _rotary_tables_t() + _retention_tables_t()
    bucket_t = _bucket_table_t()

    proj, wt, h = _in_proj_gather(x2, nw, w_in.T)
    (ro, mix_r, states, wo), (so, mix_s) = _fused_call("attn_fwd", N_CH, [
        _retention_fwd(proj, rnw_col, ret_tables, w_out),
        _swa_fwd(proj, qnw_col, knw_col, sinks, relb, bucket_t)])
    sse, dy, dmix_r, dmix_s, gwo = _out_proj(mix_r, mix_s, wo.reshape(D, D), x2, target)

    (d_ret, drw_acc), (d_swa, dbias, dsink_acc, dqw_acc, dkw_acc), (g_out,) = _fused_call("attn_bwd", N_CH + 1, [
        _retention_bwd(proj, ro, states, dmix_r, rnw_col, ret_tables),
        _swa_bwd(proj, so, dmix_s, qnw_col, knw_col, sinks, relb, bucket_t),
        _w_out_reduce_scatter(gwo.reshape(N_DEV, WOUT_BLK, D))])
    grad_x, tab_a_mine, tab_b_mine = _in_proj_bwd(d_ret, d_swa, wt, x2, nw, dy, sse,
                                                  (dbias, bucket_t, dsink_acc, dqw_acc, dkw_acc, drw_acc))
    g_in_t, sum_a, sum_b = _w_in_grad_rs(d_ret, d_swa, h, tab_a_mine, tab_b_mine)

    small_w = _small_rows(norm_w, ret_norm_w, q_norm_w, k_norm_w, sinks, rel_bias)
    small_m = _small_rows(m_norm_w, m_ret_norm_w, m_q_norm_w, m_k_norm_w, m_sinks, m_rel_bias)
    small_v = _small_rows(v_norm_w, v_ret_norm_w, v_q_norm_w, v_k_norm_w, v_sinks, v_rel_bias)
    big, small, loss = _adamw_all(g_in_t, g_out, sum_a, sum_b, w_in.T, m_w_in.T, v_w_in.T, w_out, m_w_out, v_w_out,
                                  small_w, small_m, small_v)

    def leaves(i):
        a = _small_leaves(small[i])
        return (a[0], big[i].T, a[1], a[2], a[3], a[4], a[5], big[4 + i])

    return (loss.reshape(()), grad_x.reshape(1, T, D), *leaves(0), *leaves(1), *leaves(2), *leaves(3))
```

```python
from typing import Callable, NamedTuple

import numpy as np
import jax
import jax.numpy as jnp
from jax import lax
from jax.experimental import pallas as pl
from jax.experimental.pallas import tpu as pltpu

F32 = jnp.float32
BF16 = jnp.bfloat16
MESH = pl.DeviceIdType.MESH

T = 2048
D = 1024
D_IN = 2816
N_DEV = 8
WIN_BLK = D_IN // N_DEV
WOUT_BLK = D // N_DEV
CH = 128
N_CH = T // CH
RET_H, RET_DK, RET_DV = 4, 64, 128
RET_QK = RET_H * RET_DK
RET_V = RET_H * RET_DV
SWA_H, SWA_KV, SWA_D, SWA_G = 8, 2, 64, 4
N_BUCKETS = 32
NORM_EPS = 1e-6
GN_EPS = 1e-5
NEG_INF = -1e30
PIECES = (256, 256, 512, 512, 512, 128, 128, 512)
OFFS = tuple(int(v) for v in np.cumsum((0,) + PIECES))
RET_W = OFFS[4]
SWA_W = D_IN - RET_W
SWA_OFFS = tuple(o - RET_W for o in OFFS[4:])
TM = 256

ADAM_LR, ADAM_B1, ADAM_B2, ADAM_EPS, ADAM_WD, ADAM_STEP = 0.001, 0.9, 0.999, 1e-08, 0.01, 10

VMEM_LIMIT = 56 * 1024 * 1024
VMEM_WHOLE = 63 * 1024 * 1024


def _cparams(n_grid=0, vmem=VMEM_LIMIT):
    sem = ("arbitrary",) * n_grid if n_grid else None
    return pltpu.CompilerParams(dimension_semantics=sem, vmem_limit_bytes=vmem)


class _Part(NamedTuple):
    step: Callable
    inputs: list
    in_specs: list
    out_specs: list
    out_shape: list
    scratch_shapes: list


def _fused_call(name, n_steps, parts):
    n_in = [len(p.inputs) for p in parts]
    n_out = [len(p.out_shape) for p in parts]
    n_scr = [len(p.scratch_shapes) for p in parts]

    def body(*refs):
        ins, outs, scr = refs[:sum(n_in)], refs[sum(n_in):sum(n_in) + sum(n_out)], refs[sum(n_in) + sum(n_out):]
        for i, p in enumerate(parts):
            take = lambda seq, counts: seq[sum(counts[:i]):sum(counts[:i + 1])]
            p.step(*take(ins, n_in), *take(outs, n_out), *take(scr, n_scr))

    flat = lambda field: [v for p in parts for v in getattr(p, field)]
    outs = pl.pallas_call(
        body, name=name, grid=(n_steps,), in_specs=flat("in_specs"), out_specs=tuple(flat("out_specs")),
        out_shape=tuple(flat("out_shape")), scratch_shapes=flat("scratch_shapes"), compiler_params=_cparams(1),
    )(*flat("inputs"))
    return [list(outs[sum(n_out[:i]):sum(n_out[:i + 1])]) for i in range(len(parts))]


def _dot(a, b):
    return jnp.dot(a, b, preferred_element_type=F32)


def _dot_nt(a, b):
    return lax.dot_general(a, b, (((1,), (1,)), ((), ())), preferred_element_type=F32)


def _bf(a):
    return a.astype(BF16)


def _sigmoid(x):
    return 1.0 / (1.0 + jnp.exp(-x))


def _adamw(w, g, m, v):
    m = ADAM_B1 * m + (1.0 - ADAM_B1) * g
    v = ADAM_B2 * v + (1.0 - ADAM_B2) * (g * g)
    m_hat = m / (1.0 - ADAM_B1 ** ADAM_STEP)
    v_hat = v / (1.0 - ADAM_B2 ** ADAM_STEP)
    delta = -ADAM_LR * (m_hat / (jnp.sqrt(v_hat) + ADAM_EPS) + ADAM_WD * w)
    return delta, m, v


def _rotary_tables_t():
    half = RET_DK // 2
    inv_freq = np.float32(10000.0) ** (-np.arange(half, dtype=np.float32) / np.float32(half))
    ang = inv_freq[:, None] * np.arange(T, dtype=np.float32)[None, :]
    cos, sin = np.cos(ang).astype(np.float32), np.sin(ang).astype(np.float32)
    cos64 = np.concatenate([cos, cos], axis=0)
    sin64 = np.concatenate([-sin, sin], axis=0)
    return np.tile(cos64, (RET_H, 1)), np.tile(sin64, (RET_H, 1))


def _retention_tables_t():
    gamma = (1.0 - np.exp2(-5.0 - np.arange(RET_H, dtype=np.float32))).astype(np.float32)
    log_g = np.log(gamma).astype(np.float32)
    i = np.arange(CH, dtype=np.float32)
    diff = i[None, :] - i[:, None]
    decay = np.where(diff >= 0, np.exp(log_g[:, None, None] * np.maximum(diff, 0.0)), 0.0).astype(np.float32)
    decay_all = np.concatenate(list(decay), axis=1)
    zeta = np.exp(log_g[:, None] * (CH - 1.0 - i)).astype(np.float32)
    zeta_tab = np.repeat(zeta.T, RET_DK, axis=1)
    xi = np.exp(log_g[:, None] * (i + 1.0)).astype(np.float32)
    xi_tab = np.repeat(xi, RET_DK, axis=0)
    chunk_decay = np.exp(log_g * np.float32(CH)).astype(np.float32)
    row_head = np.arange(RET_V)[:, None] // RET_DV
    col_head = np.arange(RET_QK)[None, :] // RET_DK
    state_mask = (row_head == col_head).astype(np.float32)
    state_decay = (state_mask * chunk_decay[row_head]).astype(np.float32)
    q_mask = (np.arange(RET_QK)[:, None] // RET_DK == np.arange(RET_H * CH)[None, :] // CH).astype(np.float32)
    return (np.ascontiguousarray(decay_all), np.ascontiguousarray(zeta_tab), np.ascontiguousarray(xi_tab),
            state_mask, state_decay, q_mask)


def _bucket_table_t():
    qi = np.arange(CH)[:, None]
    kj = np.arange(2 * CH)[None, :]
    dist = qi + CH - kj
    n = np.maximum(dist, 0)
    max_exact = N_BUCKETS // 2
    nf = np.maximum(n, 1).astype(np.float32)
    large = max_exact + (np.log(nf / np.float32(max_exact)) / np.float32(np.log(CH / max_exact))
                         * np.float32(N_BUCKETS - max_exact)).astype(np.int32)
    large = np.minimum(large, N_BUCKETS - 1)
    bucket = np.where(n < max_exact, n, large)
    return np.ascontiguousarray(np.where((dist >= 0) & (dist < CH), bucket, -1).astype(np.int32).T)


def _mesh_pos():
    return lax.axis_index("x"), lax.axis_index("y"), lax.axis_index("c")


def _blk(px, py, pc):
    return 4 * px + 2 * py + pc


def _in_proj_gather(x, norm_w, w_in_t):
    tchunk = 512

    def body(x_any, nw_any, win_any, proj_ref, wt_out, h_out, wt_ref, h_ref, stage, x_ref, nw_ref, win_ref,
             send_sems, recv_sems, out_sems, wb_sems, h_sem, in_sems):
        n_tiles = T // TM
        loads = [pltpu.make_async_copy(win_any, win_ref, in_sems.at[0]),
                 pltpu.make_async_copy(nw_any, nw_ref, in_sems.at[1])]
        loads += [pltpu.make_async_copy(x_any.at[pl.ds(i * TM, TM), :], x_ref.at[pl.ds(i * TM, TM), :], in_sems.at[2 + i])
                  for i in range(n_tiles)]
        for cp in loads:
            cp.start()
        loads[0].wait()
        x, y, c = _mesh_pos()
        me = _blk(x, y, c)
        sibling = (x, y, 1 - c)
        nbr_a, nbr_b, diag = (x ^ (1 - c), y ^ c), (x ^ c, y ^ (1 - c)), (1 - x, 1 - y)

        def copy(k, b, to):
            return pltpu.make_async_remote_copy(src_ref=wt_ref.at[b], dst_ref=wt_ref.at[b], send_sem=send_sems.at[k],
                                                recv_sem=recv_sems.at[k], device_id=to, device_id_type=MESH)

        wt_ref[me] = _bf(win_ref[...])
        first = [copy(0, me, sibling), copy(1, me, (*nbr_a, c))]
        for cp in first:
            cp.start()
        to_b = copy(2, me, (*nbr_b, c))

        loads[1].wait()
        nw = nw_ref[...]
        for i in range(n_tiles):
            rows = slice(i * TM, (i + 1) * TM)
            loads[2 + i].wait()
            xv = x_ref[rows, :]
            r = lax.rsqrt(jnp.mean(xv * xv, axis=-1, keepdims=True) + NORM_EPS)
            h_ref[rows, :] = _bf(xv * r * nw)
        keep = [pltpu.make_async_copy(h_ref, h_out, h_sem)]
        keep[0].start()

        writes = []

        def project(b):
            k = len(writes)
            if k >= 2:
                writes[k - 2].wait()
            keep.append(pltpu.make_async_copy(wt_ref.at[b], wt_out.at[b], wb_sems.at[k]))
            keep[-1].start()
            w = wt_ref[b]
            for t in range(T // tchunk):
                cols = slice(t * tchunk, (t + 1) * tchunk)
                stage[k % 2, :, cols] = _dot_nt(w, h_ref[cols, :])
            cp = pltpu.make_async_copy(stage.at[k % 2], proj_ref.at[b], out_sems.at[k % 2])
            cp.start()
            writes.append(cp)

        here = (x, y, c)
        project(me)
        copy(0, _blk(x, y, 1 - c), here).wait_recv()
        project(_blk(x, y, 1 - c))
        to_b.start()
        passed = [to_b]

        def landed(k, chip, relay_to=None):
            copy(k, _blk(*chip, c), here).wait_recv()
            out = [copy({1: 4, 2: 5, 3: 6}[k], _blk(*chip, c), sibling)]
            if relay_to is not None:
                out.append(copy(3, _blk(*chip, c), (*relay_to, c)))
            for cp in out:
                cp.start()
            passed.extend(out)
            project(_blk(*chip, c))

        def from_sibling(k, chip):
            copy(k, _blk(*chip, 1 - c), here).wait_recv()
            project(_blk(*chip, 1 - c))

        landed(1, nbr_a, relay_to=nbr_b)
        from_sibling(4, nbr_b)
        landed(2, nbr_b)
        from_sibling(5, nbr_a)
        landed(3, diag)
        from_sibling(6, diag)
        for cp in writes[-2:] + keep:
            cp.wait()
        for cp in first + passed:
            cp.wait_send()

    vm = pl.BlockSpec(memory_space=pltpu.VMEM)
    hbm = pl.BlockSpec(memory_space=pltpu.HBM)
    dma = pltpu.SemaphoreType.DMA
    proj, wt, h = pl.pallas_call(
        body, name="in_proj_gather",
        out_shape=(jax.ShapeDtypeStruct((N_DEV, WIN_BLK, T), F32), jax.ShapeDtypeStruct((N_DEV, WIN_BLK, D), BF16),
                   jax.ShapeDtypeStruct((T, D), BF16)),
        in_specs=[hbm, hbm, hbm], out_specs=(hbm, hbm, hbm),
        scratch_shapes=[pltpu.VMEM((N_DEV, WIN_BLK, D), BF16), pltpu.VMEM((T, D), BF16), pltpu.VMEM((2, WIN_BLK, T), F32),
                        pltpu.VMEM((T, D), F32), pltpu.VMEM((1, D), F32), pltpu.VMEM((WIN_BLK, D), F32),
                        dma((7,)), dma((7,)), dma((2,)), dma((N_DEV,)), dma, dma((2 + T // TM,))],
        compiler_params=_cparams(vmem=VMEM_WHOLE),
    )(x, norm_w, w_in_t)
    return proj.reshape(D_IN, T), wt.reshape(D_IN, D), h


def _gather_w_out(w_out_ref, own_ref, wo_ref, local_sem, send_sems, recv_sems, phase):
    x, y, c = _mesh_pos()
    me = _blk(x, y, c)
    sibling = (x, y, 1 - c)
    chips = [(1 - x, y), (x, 1 - y), (1 - x, 1 - y)]

    def copy(k, b, to, src=None):
        return pltpu.make_async_remote_copy(src_ref=wo_ref.at[b] if src is None else src, dst_ref=wo_ref.at[b],
                                            send_sem=send_sems.at[k], recv_sem=recv_sems.at[k], device_id=to,
                                            device_id_type=MESH)

    local = pltpu.make_async_copy(own_ref, wo_ref.at[me], local_sem)
    first = [copy(0, me, sibling, own_ref)] + [copy(1 + j, me, (*chip, c), own_ref) for j, chip in enumerate(chips)]
    passed = [copy(4 + j, _blk(*chip, c), sibling) for j, chip in enumerate(chips)]
    if phase == 0:
        own_ref[...] = _bf(w_out_ref[...])
        local.start()
        for cp in first:
            cp.start()
    elif phase == 1:
        for j, chip in enumerate(chips):
            copy(1 + j, _blk(*chip, c), (x, y, c)).wait_recv()
            passed[j].start()
    else:
        local.wait()
        copy(0, _blk(x, y, 1 - c), (x, y, c)).wait_recv()
        for j, chip in enumerate(chips):
            copy(4 + j, _blk(*chip, 1 - c), (x, y, c)).wait_recv()
        for cp in first + passed:
            cp.wait_send()


def _proj_piece(i, block_of):
    rows = PIECES[i]
    assert OFFS[i] % rows == 0
    return pl.BlockSpec((rows, CH), lambda n: (OFFS[i] // rows, block_of(n)))


def _proj_gate_halves(block_of):
    return [pl.BlockSpec((256, CH), lambda n, j=j: (OFFS[7] // 256 + j, block_of(n))) for j in range(2)]


def _swap_halves_t(t):
    half = RET_DK // 2
    parts = []
    for h in range(RET_H):
        parts += [t[h * RET_DK + half:(h + 1) * RET_DK], t[h * RET_DK:h * RET_DK + half]]
    return jnp.concatenate(parts, axis=0)


def _rotate(t, cos, sin):
    return t * cos + _swap_halves_t(t) * sin


def _group_norm_t(o):
    mu = jnp.mean(o, axis=0, keepdims=True)
    var = jnp.mean((o - mu) * (o - mu), axis=0, keepdims=True)
    rstd = lax.rsqrt(var + GN_EPS)
    return (o - mu) * rstd, rstd


def _retention_scores_t(q_t, k_rm_b, q_mask, decay_all):
    q_heads = _bf(jnp.concatenate([q_t] * RET_H, axis=1) * q_mask)
    return _dot(k_rm_b, q_heads) * decay_all


def _row_to_col(row):
    n = row.shape[1]
    diagonal = lax.broadcasted_iota(jnp.int32, (n, n), 0) == lax.broadcasted_iota(jnp.int32, (n, n), 1)
    return jnp.sum(jnp.where(diagonal, row, 0.0), axis=1, keepdims=True)


def _retention_fwd(proj, ret_norm_w, tables, w_out_blk):
    cos_t, sin_t, decay_all, zeta_tab, xi_tab, state_mask, state_decay, q_mask = tables

    def body(rq_ref, rk_ref, rv_ref, rg_ref, w_row, cos_ref, sin_ref, decay_ref, zeta_ref, xi_ref, smask_ref,
             sdecay_ref, qmask_ref, wout_ref, ro_ref, mix_ref, st_ref, wo_ref, w_ref,
             state, wo_own, wo_local, wo_send, wo_recv):
        n = pl.program_id(0)

        @pl.when(n == 0)
        def _():
            w_ref[...] = _row_to_col(w_row[...])
            state[...] = jnp.zeros_like(state)
            _gather_w_out(wout_ref, wo_own, wo_ref, wo_local, wo_send, wo_recv, phase=0)

        @pl.when(n == N_CH // 2)
        def _():
            _gather_w_out(wout_ref, wo_own, wo_ref, wo_local, wo_send, wo_recv, phase=1)

        @pl.when(n == N_CH - 1)
        def _():
            _gather_w_out(wout_ref, wo_own, wo_ref, wo_local, wo_send, wo_recv, phase=2)

        cos, sin = cos_ref[...], sin_ref[...]
        q_t = _rotate(rq_ref[...], cos, sin)
        k_t = _rotate(rk_ref[...], cos, sin) * (RET_DK ** -0.5)
        k_rm = k_t.T
        v_b = _bf(rv_ref[...])
        m_b = _bf(state[...])
        st_ref[0] = m_b
        scores_b = _bf(_retention_scores_t(q_t, _bf(k_rm), qmask_ref[...], decay_ref[...]))
        cross = _dot(m_b, _bf(q_t * xi_ref[...]))
        state[...] = state[...] * sdecay_ref[...] + _dot(v_b, _bf(k_rm * zeta_ref[...])) * smask_ref[...]
        heads = range(RET_H)
        rows = [slice(h * RET_DV, (h + 1) * RET_DV) for h in heads]
        o = [_dot(v_b[rows[h]], scores_b[:, h * CH:(h + 1) * CH]) + cross[rows[h]] for h in heads]
        for h in heads:
            rn, _ = _group_norm_t(o[h])
            g = rg_ref[rows[h], :]
            ro_ref[rows[h], :] = o[h]
            mix_ref[rows[h], :] = rn * w_ref[rows[h], :] * (g * _sigmoid(g))

    col = lambda w: pl.BlockSpec((w, CH), lambda n: (0, n))
    const = lambda shape: pl.BlockSpec(shape, lambda n: (0,) * len(shape))
    cur = lambda n: n
    return _Part(
        body,
        inputs=[proj, proj, proj, proj, ret_norm_w, cos_t, sin_t, decay_all, zeta_tab, xi_tab, state_mask, state_decay,
                q_mask, w_out_blk],
        in_specs=[_proj_piece(i, cur) for i in range(4)] + [
            const((1, RET_V)), col(RET_QK), col(RET_QK),
            const(decay_all.shape), const(zeta_tab.shape), const(xi_tab.shape), const(state_mask.shape),
            const(state_decay.shape), const(q_mask.shape), const((WOUT_BLK, D))],
        out_specs=[col(RET_V), col(RET_V), pl.BlockSpec((1, RET_V, RET_QK), lambda n: (n, 0, 0)),
                   pl.BlockSpec(memory_space=pl.ANY), const((RET_V, 1))],
        out_shape=[jax.ShapeDtypeStruct((RET_V, T), F32), jax.ShapeDtypeStruct((RET_V, T), F32),
                   jax.ShapeDtypeStruct((N_CH, RET_V, RET_QK), BF16), jax.ShapeDtypeStruct((N_DEV, WOUT_BLK, D), BF16),
                   jax.ShapeDtypeStruct((RET_V, 1), F32)],
        scratch_shapes=[pltpu.VMEM((RET_V, RET_QK), F32), pltpu.VMEM((WOUT_BLK, D), BF16), pltpu.SemaphoreType.DMA,
                        pltpu.SemaphoreType.DMA((N_DEV - 1,)), pltpu.SemaphoreType.DMA((N_DEV - 1,))])


def _retention_bwd(proj, ro, states, dmix, ret_norm_w, tables):
    cos_t, sin_t, decay_all, zeta_tab, xi_tab, state_mask, state_decay, q_mask = tables

    def body(*refs):
        @pl.when(pl.program_id(0) < N_CH)
        def _():
            chunk(*refs)

    def chunk(rq_ref, rk_ref, rv_ref, rg_ref, ro_ref, st_ref, dm_ref, w_ref, cos_ref, sin_ref, decay_ref, zeta_ref,
              xi_ref, smask_ref, sdecay_ref, qmask_ref, d_ref, dw_ref, gstate):
        i = pl.program_id(0)

        @pl.when(i == 0)
        def _():
            gstate[...] = jnp.zeros_like(gstate)
            dw_ref[...] = jnp.zeros_like(dw_ref)

        cos, sin = cos_ref[...], sin_ref[...]
        q_t = _rotate(rq_ref[...], cos, sin)
        k_t = _rotate(rk_ref[...], cos, sin) * (RET_DK ** -0.5)
        q_b, k_b = _bf(q_t), _bf(k_t)
        k_rm = k_t.T
        kz_b = _bf(k_rm * zeta_ref[...])
        qx_b = _bf(q_t * xi_ref[...])
        v_t = rv_ref[...]
        v_b = _bf(v_t)
        v_rm_b = _bf(v_t.T)
        decay = decay_ref[...]
        scores_b = _bf(_retention_scores_t(q_t, _bf(k_rm), qmask_ref[...], decay))
        heads = range(RET_H)
        rows = [slice(h * RET_DV, (h + 1) * RET_DV) for h in heads]
        qk_rows = [slice(h * RET_DK, (h + 1) * RET_DK) for h in heads]
        lanes = [slice(h * CH, (h + 1) * CH) for h in heads]
        do = []
        for h in heads:
            g, w, dm = rg_ref[rows[h], :], w_ref[rows[h], :], dm_ref[rows[h], :]
            rn, rstd = _group_norm_t(ro_ref[rows[h], :])
            sig = _sigmoid(g)
            silu = g * sig
            d_ref[OFFS[3] + h * RET_DV:OFFS[3] + (h + 1) * RET_DV, :] = _bf(dm * rn * w * (sig * (1.0 + g * (1.0 - sig))))
            dw_ref[rows[h], :] += dm * silu * rn
            drn = dm * silu * w
            do.append(rstd * (drn - jnp.mean(drn, axis=0, keepdims=True)
                              - rn * jnp.mean(drn * rn, axis=0, keepdims=True)))
        do_b = _bf(jnp.concatenate(do, axis=0))
        m_b = st_ref[0]
        g_all = gstate[...]
        g_b = _bf(g_all)
        dscores_b = [_bf(_dot(v_rm_b[:, rows[h]], do_b[rows[h]]) * decay[:, lanes[h]]) for h in heads]
        dq_cross = lax.dot_general(m_b, do_b, (((0,), (0,)), ((), ())), preferred_element_type=F32)
        dkz = _dot(v_rm_b, g_b)
        dv_cross = _dot_nt(g_b, kz_b)
        gstate[...] = g_all * sdecay_ref[...] + _dot_nt(do_b, qx_b) * smask_ref[...]
        dq = jnp.concatenate([_dot(k_b[qk_rows[h]], dscores_b[h]) for h in heads], axis=0) + dq_cross * xi_ref[...]
        dk = (jnp.concatenate([_dot_nt(q_b[qk_rows[h]], dscores_b[h]) for h in heads], axis=0)
              + (dkz * zeta_ref[...]).T) * (RET_DK ** -0.5)
        for h in heads:
            d_ref[OFFS[2] + h * RET_DV:OFFS[2] + (h + 1) * RET_DV, :] = _bf(
                _dot_nt(do_b[rows[h]], scores_b[:, lanes[h]]) + dv_cross[rows[h]])
        d_ref[OFFS[0]:OFFS[1], :] = _bf(dq * cos + _swap_halves_t(dq * sin))
        d_ref[OFFS[1]:OFFS[2], :] = _bf(dk * cos + _swap_halves_t(dk * sin))

    chunk_of = lambda i: N_CH - 1 - jnp.minimum(i, N_CH - 1)
    col = lambda w: pl.BlockSpec((w, CH), lambda i: (0, chunk_of(i)))
    const = lambda shape: pl.BlockSpec(shape, lambda i: (0,) * len(shape))
    return _Part(
        body,
        inputs=[proj, proj, proj, proj, ro, states, dmix, ret_norm_w, cos_t, sin_t, decay_all, zeta_tab, xi_tab,
                state_mask, state_decay, q_mask],
        in_specs=[_proj_piece(j, chunk_of) for j in range(4)] + [
                  col(RET_V),
                  pl.BlockSpec((1, RET_V, RET_QK), lambda i: (chunk_of(i), 0, 0)), col(RET_V),
                  const((RET_V, 1)), col(RET_QK), col(RET_QK),
                  const(decay_all.shape), const(zeta_tab.shape), const(xi_tab.shape), const(state_mask.shape),
                  const(state_decay.shape), const(q_mask.shape)],
        out_specs=[col(RET_W), const((RET_V, CH))],
        out_shape=[jax.ShapeDtypeStruct((RET_W, T), BF16), jax.ShapeDtypeStruct((RET_V, CH), F32)],
        scratch_shapes=[pltpu.VMEM((RET_V, RET_QK), F32)])


HQ_LANES = SWA_H * CH


def _head_lanes(hq):
    return slice(hq * CH, (hq + 1) * CH)


def _build_bias_t(bucket_ref, relb_ref, sinks_ref, bias_ref, sink_row):
    bt = bucket_ref[...]
    first = lax.broadcasted_iota(jnp.int32, bt.shape, 0) < CH
    for hq in range(SWA_H):
        b = jnp.full(bt.shape, NEG_INF, F32)
        for bk in range(N_BUCKETS):
            b = jnp.where(bt == bk, relb_ref[hq, bk], b)
        bias_ref[0, :, _head_lanes(hq)] = b
        bias_ref[1, :, _head_lanes(hq)] = jnp.where(first, NEG_INF, b)
        sink_row[:, _head_lanes(hq)] = jnp.full((1, CH), sinks_ref[hq], F32)


def _rms_t(t, w_col):
    r = lax.rsqrt(jnp.mean(t * t, axis=0, keepdims=True) + NORM_EPS)
    return t * r * w_col, r


def _rms_t_bwd(dn, t, r, w_col):
    u = dn * w_col
    return r * u - t * (r * r * r) * jnp.mean(u * t, axis=0, keepdims=True), dn * t * r


def _norm_kv(k_t, kw):
    return jnp.concatenate([_rms_t(k_t[g * SWA_D:(g + 1) * SWA_D], kw)[0] for g in range(SWA_KV)], axis=0)


def _kv_slot(a, kv):
    z = jnp.zeros_like(a)
    return jnp.concatenate([a, z] if kv == 0 else [z, a], axis=0)


def _softmax_t(s, sink):
    m = jnp.maximum(jnp.max(s, axis=0, keepdims=True), sink)
    p = jnp.exp(s - m)
    e_sink = jnp.exp(sink - m)
    inv = 1.0 / (jnp.sum(p, axis=0, keepdims=True) + e_sink)
    return p * inv, e_sink * inv


def _gate_rows(sg_halves, hq):
    per_half = SWA_H // 2
    return sg_halves[hq // per_half][(hq % per_half) * SWA_D:(hq % per_half + 1) * SWA_D, :]


def _swa_fwd(proj, q_norm_w, k_norm_w, sinks, rel_bias_t, bucket_t):
    def body(sq_ref, skp_ref, skc_ref, svp_ref, svc_ref, sg_lo, sg_hi, qw_row, kw_row, sinks_ref, relb_ref, bucket_ref,
             so_ref, mix_ref, qw_ref, kw_ref, bias_ref, sink_row):
        n = pl.program_id(0)

        @pl.when(n == 0)
        def _():
            qw_ref[...] = _row_to_col(qw_row[...])
            kw_ref[...] = _row_to_col(kw_row[...])
            _build_bias_t(bucket_ref, relb_ref, sinks_ref, bias_ref, sink_row)

        var = (n == 0).astype(jnp.int32)
        qw, kw = qw_ref[...], kw_ref[...]
        kn_band = jnp.concatenate([_norm_kv(skp_ref[...], kw), _norm_kv(skc_ref[...], kw)], axis=1)
        kb_rm = _bf(kn_band.T)
        vband = _bf(jnp.concatenate([svp_ref[...], svc_ref[...]], axis=1))
        q_all = jnp.concatenate(
            [_bf(_kv_slot(_rms_t(sq_ref[hq * SWA_D:(hq + 1) * SWA_D, :], qw)[0] * (SWA_D ** -0.5), hq // SWA_G))
             for hq in range(SWA_H)], axis=1)
        probs, _ = _softmax_t(_dot(kb_rm, q_all) + bias_ref[var], sink_row[...])
        probs_b = _bf(probs)
        for kv in range(SWA_KV):
            o = _dot(vband[kv * SWA_D:(kv + 1) * SWA_D], probs_b[:, kv * SWA_G * CH:(kv + 1) * SWA_G * CH])
            for j in range(SWA_G):
                rows = slice((kv * SWA_G + j) * SWA_D, (kv * SWA_G + j + 1) * SWA_D)
                oh = o[:, j * CH:(j + 1) * CH]
                g = _gate_rows((sg_lo, sg_hi), kv * SWA_G + j)
                so_ref[rows, :] = oh
                mix_ref[rows, :] = oh * (g * _sigmoid(g))

    col = lambda w: pl.BlockSpec((w, CH), lambda n: (0, n))
    const = lambda shape: pl.BlockSpec(shape, lambda n: (0,) * len(shape))
    smem = pl.BlockSpec(memory_space=pltpu.SMEM)
    cur = lambda n: n
    prev = lambda n: jnp.maximum(n - 1, 0)
    return _Part(
        body,
        inputs=[proj, proj, proj, proj, proj, proj, proj, q_norm_w, k_norm_w, sinks, rel_bias_t, bucket_t],
        in_specs=[_proj_piece(4, cur), _proj_piece(5, prev), _proj_piece(5, cur), _proj_piece(6, prev),
                  _proj_piece(6, cur)] + _proj_gate_halves(cur) + [
                  const((1, SWA_D)), const((1, SWA_D)), smem, smem, const((2 * CH, CH))],
        out_specs=[col(512), col(512), const((SWA_D, 1)), const((SWA_D, 1))],
        out_shape=[jax.ShapeDtypeStruct((512, T), F32), jax.ShapeDtypeStruct((512, T), F32),
                   jax.ShapeDtypeStruct((SWA_D, 1), F32), jax.ShapeDtypeStruct((SWA_D, 1), F32)],
        scratch_shapes=[pltpu.VMEM((2, 2 * CH, HQ_LANES), F32), pltpu.VMEM((1, HQ_LANES), F32)])


def _swa_bwd(proj, so, dmix, q_norm_w, k_norm_w, sinks, rel_bias_t, bucket_t):
    def body(sq_ref, skp_ref, skc_ref, svp_ref, svc_ref, sg_lo, sg_hi, so_ref, dm_ref, qw_ref, kw_ref, sinks_ref,
             relb_ref, bucket_ref, d_ref, dbias_ref, dsink_ref, dqw_ref, dkw_ref,
             bias_ref, sink_row, band_dk, band_dv, carry_dk, carry_dv, hold_dq, hold_dg):
        n = pl.program_id(0)
        slot = n % 2

        @pl.when(n == 0)
        def _():
            _build_bias_t(bucket_ref, relb_ref, sinks_ref, bias_ref, sink_row)
            for ref in (dbias_ref, dsink_ref, dqw_ref, dkw_ref, carry_dk, carry_dv):
                ref[...] = jnp.zeros_like(ref)

        qw, kw = qw_ref[...], kw_ref[...]

        @pl.when(n < N_CH)
        def _():
            var = (n == 0).astype(jnp.int32)
            kn_band = jnp.concatenate([_norm_kv(skp_ref[...], kw), _norm_kv(skc_ref[...], kw)], axis=1)
            kb_rm = _bf(kn_band.T)
            kn_band_s = _bf(kn_band * (SWA_D ** -0.5))
            vband_f = jnp.concatenate([svp_ref[...], svc_ref[...]], axis=1)
            vb_rm = _bf(vband_f.T)
            q_raw, q_rstd, qs_b, do_b = [], [], [], []
            for hq in range(SWA_H):
                rows = slice(hq * SWA_D, (hq + 1) * SWA_D)
                q_t = sq_ref[rows, :]
                qn, rq = _rms_t(q_t, qw)
                g = _gate_rows((sg_lo, sg_hi), hq)
                sig = _sigmoid(g)
                dm = dm_ref[rows, :]
                hold_dg[slot, rows, :] = _bf(dm * so_ref[rows, :] * (sig * (1.0 + g * (1.0 - sig))))
                q_raw.append(q_t)
                q_rstd.append(rq)
                qs_b.append(_bf(qn * (SWA_D ** -0.5)))
                do_b.append(_bf(dm * (g * sig)))
            q_all = jnp.concatenate([_kv_slot(qs_b[hq], hq // SWA_G) for hq in range(SWA_H)], axis=1)
            do_all = jnp.concatenate([_kv_slot(do_b[hq], hq // SWA_G) for hq in range(SWA_H)], axis=1)
            probs, p_sink = _softmax_t(_dot(kb_rm, q_all) + bias_ref[var], sink_row[...])
            dprobs = _dot(vb_rm, do_all)
            t = jnp.sum(probs * dprobs, axis=0, keepdims=True)
            dlog = probs * (dprobs - t)
            dsink_ref[...] += -(p_sink * t)
            dbias_ref[...] += dlog
            dlog_b, probs_b = _bf(dlog), _bf(probs)
            dkn, dvv = [], []
            for kv in range(SWA_KV):
                heads = range(kv * SWA_G, (kv + 1) * SWA_G)
                lanes = slice(kv * SWA_G * CH, (kv + 1) * SWA_G * CH)
                dvv.append(_dot_nt(jnp.concatenate([do_b[hq] for hq in heads], axis=1), probs_b[:, lanes]))
                dkn.append(_dot_nt(jnp.concatenate([qs_b[hq] for hq in heads], axis=1), dlog_b[:, lanes]))
                dqn = _dot(kn_band_s[kv * SWA_D:(kv + 1) * SWA_D], dlog_b[:, lanes])
                for j, hq in enumerate(heads):
                    dq_t, dqw_terms = _rms_t_bwd(dqn[:, j * CH:(j + 1) * CH], q_raw[hq], q_rstd[hq], qw)
                    hold_dq[slot, hq * SWA_D:(hq + 1) * SWA_D, :] = _bf(dq_t)
                    dqw_ref[...] += dqw_terms
            band_dk[...] = jnp.concatenate(dkn, axis=0)
            band_dv[...] = jnp.concatenate(dvv, axis=0)

        @pl.when(n == N_CH)
        def _():
            band_dk[...] = jnp.zeros_like(band_dk)
            band_dv[...] = jnp.zeros_like(band_dv)

        @pl.when(n >= 1)
        def _():
            dkn_prev = carry_dk[...] + band_dk[:, 0:CH]
            k_t = skp_ref[...]
            for kv in range(SWA_KV):
                rows = slice(kv * SWA_D, (kv + 1) * SWA_D)
                _, rk = _rms_t(k_t[rows], kw)
                dk_t, dkw_terms = _rms_t_bwd(dkn_prev[rows], k_t[rows], rk, kw)
                d_ref[SWA_OFFS[1] + kv * SWA_D:SWA_OFFS[1] + (kv + 1) * SWA_D, :] = _bf(dk_t)
                dkw_ref[...] += dkw_terms
            d_ref[SWA_OFFS[2]:SWA_OFFS[3], :] = _bf(carry_dv[...] + band_dv[:, 0:CH])
            d_ref[SWA_OFFS[0]:SWA_OFFS[1], :] = hold_dq[1 - slot]
            d_ref[SWA_OFFS[3]:SWA_OFFS[4], :] = hold_dg[1 - slot]

        carry_dk[...] = band_dk[:, CH:2 * CH]
        carry_dv[...] = band_dv[:, CH:2 * CH]

    cur_block = lambda n: jnp.minimum(n, N_CH - 1)
    prev_block = lambda n: jnp.maximum(n - 1, 0)
    col = lambda w: pl.BlockSpec((w, CH), lambda n: (0, cur_block(n)))
    prev = lambda w: pl.BlockSpec((w, CH), lambda n: (0, prev_block(n)))
    const = lambda shape: pl.BlockSpec(shape, lambda n: (0,) * len(shape))
    smem = pl.BlockSpec(memory_space=pltpu.SMEM)
    return _Part(
        body,
        inputs=[proj, proj, proj, proj, proj, proj, proj, so, dmix, q_norm_w, k_norm_w, sinks, rel_bias_t, bucket_t],
        in_specs=[_proj_piece(4, cur_block), _proj_piece(5, prev_block), _proj_piece(5, cur_block),
                  _proj_piece(6, prev_block), _proj_piece(6, cur_block)] + _proj_gate_halves(cur_block) + [
                  col(512), col(512), const((SWA_D, 1)), const((SWA_D, 1)), smem, smem, const((2 * CH, CH))],
        out_specs=[prev(SWA_W), const((2 * CH, HQ_LANES)), const((1, HQ_LANES)),
                   const((SWA_D, CH)), const((SWA_D, CH))],
        out_shape=[jax.ShapeDtypeStruct((SWA_W, T), BF16),
                   jax.ShapeDtypeStruct((2 * CH, HQ_LANES), F32), jax.ShapeDtypeStruct((1, HQ_LANES), F32),
                   jax.ShapeDtypeStruct((SWA_D, CH), F32), jax.ShapeDtypeStruct((SWA_D, CH), F32)],
        scratch_shapes=[pltpu.VMEM((2, 2 * CH, HQ_LANES), F32), pltpu.VMEM((1, HQ_LANES), F32),
                        pltpu.VMEM((128, 2 * CH), F32), pltpu.VMEM((128, 2 * CH), F32),
                        pltpu.VMEM((128, CH), F32), pltpu.VMEM((128, CH), F32),
                        pltpu.VMEM((2, 512, CH), BF16), pltpu.VMEM((2, 512, CH), BF16)])


SMALL_GRAD_SHAPES = ((1, RET_V), (1, CH), (1, CH), (1, CH), (SWA_H, N_BUCKETS))


def _finish_small_grads(dbias_ref, bucket_ref, dsink_ref, dqw_ref, dkw_ref, drw_ref, rw_o, qw_o, kw_o, sink_o, relb_o):
    bt = bucket_ref[...]
    row = lax.broadcasted_iota(jnp.int32, (SWA_H, N_BUCKETS), 0)
    col = lax.broadcasted_iota(jnp.int32, (SWA_H, N_BUCKETS), 1)
    lane = lax.broadcasted_iota(jnp.int32, (1, CH), 1)
    acc = jnp.zeros((SWA_H, N_BUCKETS), F32)
    sink = jnp.zeros((1, CH), F32)
    for hq in range(SWA_H):
        d = dbias_ref[:, _head_lanes(hq)]
        for bk in range(N_BUCKETS):
            s = jnp.sum(jnp.sum(jnp.where(bt == bk, d, 0.0), axis=0, keepdims=True), axis=1, keepdims=True)
            acc = acc + jnp.where((row == hq) & (col == bk), s, 0.0)
        sink = sink + jnp.where(lane == hq, jnp.sum(dsink_ref[:, _head_lanes(hq)], axis=1, keepdims=True), 0.0)
    relb_o[...] = acc
    sink_o[...] = sink
    for src, dst in ((dqw_ref, qw_o), (dkw_ref, kw_o)):
        padded = jnp.concatenate([src[...], jnp.zeros((CH - SWA_D, CH), F32)], axis=0)
        dst[...] = jnp.sum(padded.T, axis=0, keepdims=True)
    rw_o[...] = jnp.sum(drw_ref[...].T, axis=0, keepdims=True)


def _out_proj(mix_r, mix_s, wo, x, target):
    def body(mr_ref, ms_ref, w_ref, x_ref, t_ref, loss_ref, dy_ref, dmr_ref, dms_ref, gw_ref):
        i = pl.program_id(0)

        @pl.when(i == 0)
        def _():
            loss_ref[...] = jnp.zeros_like(loss_ref)
            gw_ref[...] = jnp.zeros_like(gw_ref)

        mixed = jnp.concatenate([mr_ref[...], ms_ref[...]], axis=0)
        w = w_ref[...]
        err = x_ref[...] + _dot(_bf(mixed.T), w) - t_ref[...]
        loss_ref[...] += jnp.sum(jnp.sum(err * err, axis=1, keepdims=True), axis=0, keepdims=True)
        dy = err * (1.0 / D)
        dy_ref[...] = dy
        dy_b = _bf(dy)
        dmix = _dot_nt(w, dy_b)
        dmr_ref[...] = dmix[0:512]
        dms_ref[...] = dmix[512:D]
        gw_ref[...] += _dot(_bf(mixed), dy_b)

    row = lambda w: pl.BlockSpec((TM, w), lambda i: (i, 0))
    col = lambda w: pl.BlockSpec((w, TM), lambda i: (0, i))
    const = lambda shape: pl.BlockSpec(shape, lambda i: (0,) * len(shape))
    return pl.pallas_call(
        body, name="out_proj", grid=(T // TM,),
        in_specs=[col(512), col(512), const((D, D)), row(D), row(D)],
        out_specs=(const((1, 1)), row(D), col(512), col(512), const((D, D))),
        out_shape=(jax.ShapeDtypeStruct((1, 1), F32), jax.ShapeDtypeStruct((T, D), F32),
                   jax.ShapeDtypeStruct((512, T), F32), jax.ShapeDtypeStruct((512, T), F32),
                   jax.ShapeDtypeStruct((D, D), F32)),
        compiler_params=_cparams(1),
    )(mix_r, mix_s, wo, x, target)


LOSS_ROW = 5
ALL_CHIPS = ((0, 0), (0, 1), (1, 0), (1, 1))


def _w_out_reduce_scatter(gwo):
    def body(gwo_ref, gout_o, own, rcv, snd, got, tot, local_sems, a_send, a_recv, b_send, b_recv):
        k = pl.program_id(0)
        x_, y_, c_ = _mesh_pos()
        sibling = (x_, y_, 1 - c_)
        rel_chips = [(1 - x_, y_), (x_, 1 - y_), (1 - x_, 1 - y_)]

        def to_sibling(j):
            px, py = ALL_CHIPS[j]
            return pltpu.make_async_remote_copy(src_ref=gwo_ref.at[_blk(px, py, 1 - c_)], dst_ref=rcv.at[j],
                                                send_sem=a_send.at[j], recv_sem=a_recv.at[j], device_id=sibling,
                                                device_id_type=MESH)

        def local(j):
            px, py = ALL_CHIPS[j]
            return pltpu.make_async_copy(gwo_ref.at[_blk(px, py, c_)], own.at[j], local_sems.at[j])

        def to_chip(q):
            return pltpu.make_async_remote_copy(src_ref=snd.at[q], dst_ref=got.at[q], send_sem=b_send.at[q],
                                                recv_sem=b_recv.at[q], device_id=(*rel_chips[q], c_), device_id_type=MESH)

        @pl.when(k == 0)
        def _():
            for j in range(len(ALL_CHIPS)):
                to_sibling(j).start()
                local(j).start()

        @pl.when(k == 2)
        def _():
            for j in range(len(ALL_CHIPS)):
                local(j).wait()
                to_sibling(j).wait_recv()
            for q in range(3):
                j = 2 * rel_chips[q][0] + rel_chips[q][1]
                snd[q] = _bf(own[j] + rcv[j])
                to_chip(q).start()
            jm = 2 * x_ + y_
            tot[...] = own[jm] + rcv[jm]

        @pl.when(k == N_CH)
        def _():
            g = tot[...]
            for q in range(3):
                to_chip(q).wait_recv()
                g = g + got[q].astype(F32)
            gout_o[...] = g
            for j in range(len(ALL_CHIPS)):
                to_sibling(j).wait_send()
            for q in range(3):
                to_chip(q).wait_send()

    dma = pltpu.SemaphoreType.DMA
    return _Part(
        body, inputs=[gwo], in_specs=[pl.BlockSpec(memory_space=pl.ANY)],
        out_specs=[pl.BlockSpec((WOUT_BLK, D), lambda k: (0, 0))], out_shape=[jax.ShapeDtypeStruct((WOUT_BLK, D), F32)],
        scratch_shapes=[pltpu.VMEM((4, WOUT_BLK, D), F32), pltpu.VMEM((4, WOUT_BLK, D), F32),
                        pltpu.VMEM((3, WOUT_BLK, D), BF16), pltpu.VMEM((3, WOUT_BLK, D), BF16),
                        pltpu.VMEM((WOUT_BLK, D), F32), dma((4,)), dma((4,)), dma((4,)), dma((3,)), dma((3,))])


def _in_proj_bwd_rs(d_ret, d_swa, h, wt, x, norm_w, dy, sse, small_acc):
    n_blocks, n_tiles = N_DEV, T // TM
    last = n_blocks + n_tiles - 1

    def body(da_any, db_any, da_ref, db_ref, h_any, w_any, x_ref, nw_ref, dy_ref, sse_ref,
             dbias_ref, bucket_ref, dsink_ref, dqw_ref, dkw_ref, drw_ref, gx_ref, gin_o, suma_o, sumb_o,
             stage, d2d_src, mine, rcv, snd, got, tot, tab_a, tab_b, gnw_acc, rw_ref, qw_ref, kw_ref, sk_ref, rb_ref,
             h_ref, w_ref, dp_sems, d2d_send, d2d_recv, ici_send, ici_recv, s_send, s_recv, hw_sems):
        k = pl.program_id(0)
        x_, y_, c_ = _mesh_pos()
        h_load = pltpu.make_async_copy(h_any, h_ref, hw_sems.at[0])
        w_load = pltpu.make_async_copy(w_any, w_ref, hw_sems.at[1])
        me = _blk(x_, y_, c_)
        sibling = (x_, y_, 1 - c_)
        rel_chips = [(1 - x_, y_), (x_, 1 - y_), (1 - x_, 1 - y_), (x_, y_)]

        def block_of_step(s):
            return _blk(*rel_chips[s // 2], 1 - c_ if s % 2 == 0 else c_)

        def fetch(s, wait):
            slot, b = s % 2, block_of_step(s)
            split = RET_W - 4 * WIN_BLK

            def run(src, dst, sem):
                cp = pltpu.make_async_copy(src, dst, sem)
                cp.wait() if wait else cp.start()

            @pl.when(b < 4)
            def _():
                run(da_any.at[pl.ds(pl.multiple_of(b * WIN_BLK, 16), WIN_BLK), :], stage.at[slot], dp_sems.at[slot, 0])

            @pl.when(b == 4)
            def _():
                run(da_any.at[pl.ds(4 * WIN_BLK, split), :], stage.at[slot, pl.ds(0, split), :], dp_sems.at[slot, 0])
                run(db_any.at[pl.ds(0, WIN_BLK - split), :], stage.at[slot, pl.ds(split, WIN_BLK - split), :],
                    dp_sems.at[slot, 1])

            @pl.when(b > 4)
            def _():
                run(db_any.at[pl.ds(pl.multiple_of(b * WIN_BLK - RET_W, 16), WIN_BLK), :], stage.at[slot],
                    dp_sems.at[slot, 0])

        def d2d_copy(r):
            return pltpu.make_async_remote_copy(src_ref=d2d_src, dst_ref=rcv.at[r], send_sem=d2d_send.at[r],
                                                recv_sem=d2d_recv.at[r], device_id=sibling, device_id_type=MESH)

        def ici_copy(r):
            return pltpu.make_async_remote_copy(src_ref=snd.at[r], dst_ref=got.at[r], send_sem=ici_send.at[r],
                                                recv_sem=ici_recv.at[r], device_id=(*rel_chips[r], c_),
                                                device_id_type=MESH)

        def table_copies():
            return [pltpu.make_async_remote_copy(src_ref=tab.at[me], dst_ref=tab.at[me], send_sem=s_send.at[a, p - 1],
                                                 recv_sem=s_recv.at[a, p - 1],
                                                 device_id=(x_ ^ (p >> 2), y_ ^ ((p >> 1) & 1), c_ ^ (p & 1)),
                                                 device_id_type=MESH)
                    for p in range(1, N_DEV) for a, tab in enumerate((tab_a, tab_b))]

        def chip_sum(r):
            d2d_copy(r).wait_recv()
            total = mine[...] + rcv[r]
            if r < 3:
                snd[r] = _bf(total)
                ici_copy(r).start()
            else:
                tot[...] = total

        for s in range(n_blocks):
            @pl.when(k == s)
            def _(s=s):
                r = s // 2
                if s == 0:
                    gnw_acc[...] = jnp.zeros_like(gnw_acc)
                    h_load.start()
                    fetch(0, wait=False)
                    w_load.start()
                fetch(s, wait=True)
                if s + 1 < n_blocks:
                    fetch(s + 1, wait=False)
                if s == 0:
                    h_load.wait()
                gw = _dot(stage[s % 2], h_ref[...])
                if s % 2 == 0:
                    if s >= 2:
                        chip_sum(r - 1)
                        d2d_copy(r - 1).wait_send()
                    d2d_src[...] = gw
                    d2d_copy(r).start()
                else:
                    mine[...] = gw

        @pl.when(k == n_blocks)
        def _():
            chip_sum(3)
            w_load.wait()

        @pl.when(k >= n_blocks)
        def _():
            dp = jnp.concatenate([da_ref[...], db_ref[...]], axis=0)
            xv, nw = x_ref[...], nw_ref[...]
            r = lax.rsqrt(jnp.mean(xv * xv, axis=-1, keepdims=True) + NORM_EPS)
            dh = lax.dot_general(dp, w_ref[...], (((0,), (0,)), ((), ())), preferred_element_type=F32)
            u = dh * nw
            gx_ref[...] = dy_ref[...] + r * u - xv * (r * r * r) * jnp.mean(u * xv, axis=-1, keepdims=True)
            gnw_acc[...] += jnp.sum(dh * (xv * r), axis=0, keepdims=True)

        @pl.when(k == n_blocks)
        def _():
            _finish_small_grads(dbias_ref, bucket_ref, dsink_ref, dqw_ref, dkw_ref, drw_ref,
                                rw_ref, qw_ref, kw_ref, sk_ref, rb_ref)

        @pl.when(k == last)
        def _():
            mine_a, mine_b = tab_a.at[me], tab_b.at[me]
            mine_a[...] = jnp.zeros_like(mine_a)
            mine_b[...] = jnp.zeros_like(mine_b)
            for row, ref in enumerate((gnw_acc, rw_ref, qw_ref, kw_ref, sk_ref)):
                mine_a[row:row + 1, 0:ref.shape[1]] = ref[...]
            mine_a[LOSS_ROW:LOSS_ROW + 1, 0:1] = sse_ref[...]
            mine_b[:, 0:N_BUCKETS] = rb_ref[...]
            tables = table_copies()
            for cp in tables:
                cp.start()
            g_in = tot[...]
            for q in range(3):
                ici_copy(q).wait_recv()
                g_in = g_in + got[q].astype(F32)
            gin_o[...] = g_in
            for cp in tables:
                cp.wait_recv()
            sum_a, sum_b = tab_a[0], tab_b[0]
            for b in range(1, N_DEV):
                sum_a = sum_a + tab_a[b]
                sum_b = sum_b + tab_b[b]
            suma_o[...] = sum_a
            sumb_o[...] = sum_b
            d2d_copy(3).wait_send()
            for q in range(3):
                ici_copy(q).wait_send()
            for cp in tables:
                cp.wait_send()

    tile = lambda k: jnp.maximum(k - n_blocks, 0)
    row = lambda w: pl.BlockSpec((TM, w), lambda k: (tile(k), 0))
    col = lambda w: pl.BlockSpec((w, TM), lambda k: (0, tile(k)))
    const = lambda shape, **kw: pl.BlockSpec(shape, lambda k: (0,) * len(shape), **kw)
    once = dict(pipeline_mode=pl.Buffered(1))
    hbm = pl.BlockSpec(memory_space=pltpu.HBM)
    dma = pltpu.SemaphoreType.DMA
    return pl.pallas_call(
        body, name="in_proj_bwd_rs", grid=(n_blocks + n_tiles,),
        in_specs=[hbm, hbm, col(RET_W), col(SWA_W), hbm, hbm,
                  row(D), const((1, D)), row(D), const((1, 1))]
                 + [const(a.shape) for a in small_acc],
        out_specs=(row(D), const((WIN_BLK, D)), const((8, D)), const((SWA_H, CH))),
        out_shape=(jax.ShapeDtypeStruct((T, D), F32), jax.ShapeDtypeStruct((WIN_BLK, D), F32),
                   jax.ShapeDtypeStruct((8, D), F32), jax.ShapeDtypeStruct((SWA_H, CH), F32)),
        scratch_shapes=[
            pltpu.VMEM((2, WIN_BLK, T), BF16), pltpu.VMEM((WIN_BLK, D), F32),
            pltpu.VMEM((WIN_BLK, D), F32),
            pltpu.VMEM((4, WIN_BLK, D), F32), pltpu.VMEM((3, WIN_BLK, D), BF16),
            pltpu.VMEM((3, WIN_BLK, D), BF16), pltpu.VMEM((WIN_BLK, D), F32),
            pltpu.VMEM((N_DEV, 8, D), F32), pltpu.VMEM((N_DEV, SWA_H, CH), F32),
            pltpu.VMEM((1, D), F32),
        ] + [pltpu.VMEM(s, F32) for s in SMALL_GRAD_SHAPES] + [
            pltpu.VMEM((T, D), BF16), pltpu.VMEM((D_IN, D), BF16),
            dma((2, 2)), dma((4,)), dma((4,)), dma((3,)), dma((3,)), dma((2, 7)), dma((2, 7)), dma((2,)),
        ],
        compiler_params=_cparams(1),
    )(d_ret, d_swa, d_ret, d_swa, h, wt, x, norm_w, dy, sse, *small_acc)


SMALL_SHAPES = ((1, D), (1, 512), (1, SWA_D), (1, SWA_D), (1, SWA_H), (SWA_H, N_BUCKETS))


def _adamw_all(grad_x, g_in_t, g_out, sum_a, sum_b, w_in_t, m_in_t, v_in_t, w_out, m_out, v_out, small_w, small_m, small_v):
    n_small = len(SMALL_SHAPES)
    halves = 2

    def body(*refs):
        gx_any = refs[0]
        gin_any, gout_any, suma_any, sumb_any, w_in_ref, m_in_ref, v_in_ref, w_out_ref, m_out_ref, v_out_ref = refs[1:11]
        pos = 11
        sw_refs, sm_refs, sv_refs = (refs[pos + i * n_small:pos + (i + 1) * n_small] for i in range(3))
        pos += 3 * n_small
        g_in_o, d_in_o, nm_in_o, nv_in_o, g_out_o, d_out_o, nm_out_o, nv_out_o = refs[pos:pos + 8]
        pos += 8
        sg_o, sd_o, snm_o, snv_o = (refs[pos + i * n_small:pos + (i + 1) * n_small] for i in range(4))
        pos += 4 * n_small
        loss_o, gx_out, gin_ref, gout_ref, suma_ref, sumb_ref, load_sems, gx_sem = refs[pos:]
        i = pl.program_id(0)
        gx_copy = pltpu.make_async_copy(gx_any, gx_out, gx_sem.at[0])

        @pl.when(i == 0)
        def _():
            gx_copy.start()

        @pl.when(i == halves - 1)
        def _():
            gx_copy.wait()

        @pl.when(i == 0)
        def _():
            loads = [pltpu.make_async_copy(src, dst, load_sems.at[j]) for j, (src, dst) in enumerate(
                ((gin_any, gin_ref), (gout_any, gout_ref), (suma_any, suma_ref), (sumb_any, sumb_ref)))]
            for cp in loads:
                cp.start()
            for cp in loads:
                cp.wait()

        g = gout_ref[pl.ds(pl.multiple_of(i * (WOUT_BLK // halves), 8), WOUT_BLK // halves), :]
        d, nm, nv = _adamw(w_out_ref[...], g, m_out_ref[...], v_out_ref[...])
        g_out_o[...], d_out_o[...], nm_out_o[...], nv_out_o[...] = g, d, nm, nv
        g = gin_ref[pl.ds(pl.multiple_of(i * (WIN_BLK // halves), 8), WIN_BLK // halves), :]
        d, nm, nv = _adamw(w_in_ref[...], g, m_in_ref[...], v_in_ref[...])
        g_in_o[...], d_in_o[...], nm_in_o[...], nv_in_o[...] = g, d, nm, nv

        @pl.when(pl.program_id(0) == 0)
        def _():
            loss_o[...] = suma_ref[LOSS_ROW:LOSS_ROW + 1, 0:1] * (0.5 / D)
            for r, (rows, lanes) in enumerate(SMALL_SHAPES):
                g = suma_ref[r:r + 1, 0:lanes] if rows == 1 else sumb_ref[:, 0:lanes]
                d, nm, nv = _adamw(sw_refs[r][...], g, sm_refs[r][...], sv_refs[r][...])
                sg_o[r][...], sd_o[r][...], snm_o[r][...], snv_o[r][...] = g, d, nm, nv

    half = lambda rows: pl.BlockSpec((rows // halves, D), lambda i: (i, 0))
    const = lambda shape: pl.BlockSpec(shape, lambda i: (0,) * len(shape))
    win = jax.ShapeDtypeStruct((WIN_BLK, D), F32)
    wout = jax.ShapeDtypeStruct((WOUT_BLK, D), F32)
    smalls = tuple(jax.ShapeDtypeStruct(s, F32) for s in SMALL_SHAPES)
    small_specs = [const(s) for s in SMALL_SHAPES]
    hbm = pl.BlockSpec(memory_space=pltpu.HBM)
    outs = pl.pallas_call(
        body, name="adamw_all", grid=(halves,),
        in_specs=[hbm] * 5 + [half(WIN_BLK)] * 3 + [half(WOUT_BLK)] * 3 + small_specs * 3,
        out_specs=tuple([half(WIN_BLK)] * 4 + [half(WOUT_BLK)] * 4 + small_specs * 4 + [const((1, 1)), hbm]),
        out_shape=(win,) * 4 + (wout,) * 4 + smalls * 4 + (jax.ShapeDtypeStruct((1, 1), F32),
                                                           jax.ShapeDtypeStruct(grad_x.shape, F32)),
        scratch_shapes=[pltpu.VMEM((WIN_BLK, D), F32), pltpu.VMEM((WOUT_BLK, D), F32), pltpu.VMEM((8, D), F32),
                        pltpu.VMEM((SWA_H, CH), F32), pltpu.SemaphoreType.DMA((4,)), pltpu.SemaphoreType.DMA((1,))],
        compiler_params=_cparams(1, vmem=VMEM_WHOLE),
    )(grad_x, g_in_t, g_out, sum_a, sum_b, w_in_t, m_in_t, v_in_t, w_out, m_out, v_out, *small_w, *small_m, *small_v)
    big, rest = outs[:8], outs[8:]
    return big, [rest[i * n_small:(i + 1) * n_small] for i in range(4)], rest[4 * n_small], rest[4 * n_small + 1]


def _small_rows(norm_w, ret_norm_w, q_norm_w, k_norm_w, sinks, rel_bias):
    return (norm_w.reshape(1, D), ret_norm_w.reshape(1, 512), q_norm_w.reshape(1, SWA_D), k_norm_w.reshape(1, SWA_D),
            sinks.reshape(1, SWA_H), rel_bias.T)


def _small_leaves(rows):
    return (rows[0].reshape(D), rows[1].reshape(512), rows[2].reshape(SWA_D), rows[3].reshape(SWA_D),
            rows[4].reshape(SWA_H), rows[5].T)


def kernel(x, norm_w, w_in, ret_norm_w, q_norm_w, k_norm_w, sinks, rel_bias, w_out, loss_target, m_norm_w, m_w_in, m_ret_norm_w, m_q_norm_w, m_k_norm_w, m_sinks, m_rel_bias, m_w_out, v_norm_w, v_w_in, v_ret_norm_w, v_q_norm_w, v_k_norm_w, v_sinks, v_rel_bias, v_w_out):
    x2 = x.reshape(T, D)
    target = loss_target.reshape(T, D)
    nw = norm_w.reshape(1, D)
    relb = rel_bias.T
    ret_tables = _rotary_tables_t() + _retention_tables_t()
    bucket_t = _bucket_table_t()

    proj, wt, h = _in_proj_gather(x2, nw, w_in.T)
    (ro, mix_r, states, wo, rnw_col), (so, mix_s, qnw_col, knw_col) = _fused_call("attn_fwd", N_CH, [
        _retention_fwd(proj, ret_norm_w.reshape(1, RET_V), ret_tables, w_out),
        _swa_fwd(proj, q_norm_w.reshape(1, SWA_D), k_norm_w.reshape(1, SWA_D), sinks, relb, bucket_t)])
    sse, dy, dmix_r, dmix_s, gwo = _out_proj(mix_r, mix_s, wo.reshape(D, D), x2, target)

    (d_ret, drw_acc), (d_swa, dbias, dsink_acc, dqw_acc, dkw_acc), (g_out,) = _fused_call("attn_bwd", N_CH + 1, [
        _retention_bwd(proj, ro, states, dmix_r, rnw_col, ret_tables),
        _swa_bwd(proj, so, dmix_s, qnw_col, knw_col, sinks, relb, bucket_t),
        _w_out_reduce_scatter(gwo.reshape(N_DEV, WOUT_BLK, D))])
    grad_x, g_in_t, sum_a, sum_b = _in_proj_bwd_rs(d_ret, d_swa, h, wt, x2, nw, dy, sse,
                                                   (dbias, bucket_t, dsink_acc, dqw_acc, dkw_acc, drw_acc))

    small_w = _small_rows(norm_w, ret_norm_w, q_norm_w, k_norm_w, sinks, rel_bias)
    small_m = _small_rows(m_norm_w, m_ret_norm_w, m_q_norm_w, m_k_norm_w, m_sinks, m_rel_bias)
    small_v = _small_rows(v_norm_w, v_ret_norm_w, v_q_norm_w, v_k_norm_w, v_sinks, v_rel_bias)
    big, small, loss, grad_x = _adamw_all(grad_x, g_in_t, g_out, sum_a, sum_b, w_in.T, m_w_in.T, v_w_in.T, w_out, m_w_out, v_w_out,
                                  small_w, small_m, small_v)

    def leaves(i):
        a = _small_leaves(small[i])
        return (a[0], big[i].T, a[1], a[2], a[3], a[4], a[5], big[4 + i])

    return (loss.reshape(()), grad_x.reshape(1, T, D), *leaves(0), *leaves(1), *leaves(2), *leaves(3))
```

```python
from typing import Callable, NamedTuple

import numpy as np
import jax
import jax.numpy as jnp
from jax import lax
from jax.experimental import pallas as pl
from jax.experimental.pallas import tpu as pltpu

F32 = jnp.float32
BF16 = jnp.bfloat16
MESH = pl.DeviceIdType.MESH

T = 2048
D = 1024
D_IN = 2816
N_DEV = 8
WIN_BLK = D_IN // N_DEV
WOUT_BLK = D // N_DEV
CH = 128
N_CH = T // CH
RET_H, RET_DK, RET_DV = 4, 64, 128
RET_QK = RET_H * RET_DK
RET_V = RET_H * RET_DV
SWA_H, SWA_KV, SWA_D, SWA_G = 8, 2, 64, 4
N_BUCKETS = 32
NORM_EPS = 1e-6
GN_EPS = 1e-5
NEG_INF = -1e30
PIECES = (256, 256, 512, 512, 512, 128, 128, 512)
OFFS = tuple(int(v) for v in np.cumsum((0,) + PIECES))
RET_W = OFFS[4]
SWA_W = D_IN - RET_W
SWA_OFFS = tuple(o - RET_W for o in OFFS[4:])
TM = 256

ADAM_LR, ADAM_B1, ADAM_B2, ADAM_EPS, ADAM_WD, ADAM_STEP = 0.001, 0.9, 0.999, 1e-08, 0.01, 10

VMEM_LIMIT = 56 * 1024 * 1024
VMEM_WHOLE = 63 * 1024 * 1024


def _cparams(n_grid=0, vmem=VMEM_LIMIT):
    sem = ("arbitrary",) * n_grid if n_grid else None
    return pltpu.CompilerParams(dimension_semantics=sem, vmem_limit_bytes=vmem)


class _Part(NamedTuple):
    step: Callable
    inputs: list
    in_specs: list
    out_specs: list
    out_shape: list
    scratch_shapes: list


def _fused_call(name, n_steps, parts):
    n_in = [len(p.inputs) for p in parts]
    n_out = [len(p.out_shape) for p in parts]
    n_scr = [len(p.scratch_shapes) for p in parts]

    def body(*refs):
        ins, outs, scr = refs[:sum(n_in)], refs[sum(n_in):sum(n_in) + sum(n_out)], refs[sum(n_in) + sum(n_out):]
        for i, p in enumerate(parts):
            take = lambda seq, counts: seq[sum(counts[:i]):sum(counts[:i + 1])]
            p.step(*take(ins, n_in), *take(outs, n_out), *take(scr, n_scr))

    flat = lambda field: [v for p in parts for v in getattr(p, field)]
    outs = pl.pallas_call(
        body, name=name, grid=(n_steps,), in_specs=flat("in_specs"), out_specs=tuple(flat("out_specs")),
        out_shape=tuple(flat("out_shape")), scratch_shapes=flat("scratch_shapes"), compiler_params=_cparams(1),
    )(*flat("inputs"))
    return [list(outs[sum(n_out[:i]):sum(n_out[:i + 1])]) for i in range(len(parts))]


def _dot(a, b):
    return jnp.dot(a, b, preferred_element_type=F32)


def _dot_nt(a, b):
    return lax.dot_general(a, b, (((1,), (1,)), ((), ())), preferred_element_type=F32)


def _bf(a):
    return a.astype(BF16)


def _sigmoid(x):
    return 1.0 / (1.0 + jnp.exp(-x))


def _adamw(w, g, m, v):
    m = ADAM_B1 * m + (1.0 - ADAM_B1) * g
    v = ADAM_B2 * v + (1.0 - ADAM_B2) * (g * g)
    m_hat = m / (1.0 - ADAM_B1 ** ADAM_STEP)
    v_hat = v / (1.0 - ADAM_B2 ** ADAM_STEP)
    delta = -ADAM_LR * (m_hat / (jnp.sqrt(v_hat) + ADAM_EPS) + ADAM_WD * w)
    return delta, m, v


def _rotary_tables_t():
    half = RET_DK // 2
    inv_freq = np.float32(10000.0) ** (-np.arange(half, dtype=np.float32) / np.float32(half))
    ang = inv_freq[:, None] * np.arange(T, dtype=np.float32)[None, :]
    cos, sin = np.cos(ang).astype(np.float32), np.sin(ang).astype(np.float32)
    cos64 = np.concatenate([cos, cos], axis=0)
    sin64 = np.concatenate([-sin, sin], axis=0)
    return np.tile(cos64, (RET_H, 1)), np.tile(sin64, (RET_H, 1))


def _retention_tables_t():
    gamma = (1.0 - np.exp2(-5.0 - np.arange(RET_H, dtype=np.float32))).astype(np.float32)
    log_g = np.log(gamma).astype(np.float32)
    i = np.arange(CH, dtype=np.float32)
    diff = i[None, :] - i[:, None]
    decay = np.where(diff >= 0, np.exp(log_g[:, None, None] * np.maximum(diff, 0.0)), 0.0).astype(np.float32)
    decay_all = np.concatenate(list(decay), axis=1)
    zeta = np.exp(log_g[:, None] * (CH - 1.0 - i)).astype(np.float32)
    zeta_tab = np.repeat(zeta.T, RET_DK, axis=1)
    xi = np.exp(log_g[:, None] * (i + 1.0)).astype(np.float32)
    xi_tab = np.repeat(xi, RET_DK, axis=0)
    chunk_decay = np.exp(log_g * np.float32(CH)).astype(np.float32)
    row_head = np.arange(RET_V)[:, None] // RET_DV
    col_head = np.arange(RET_QK)[None, :] // RET_DK
    state_mask = (row_head == col_head).astype(np.float32)
    state_decay = (state_mask * chunk_decay[row_head]).astype(np.float32)
    q_mask = (np.arange(RET_QK)[:, None] // RET_DK == np.arange(RET_H * CH)[None, :] // CH).astype(np.float32)
    return (np.ascontiguousarray(decay_all), np.ascontiguousarray(zeta_tab), np.ascontiguousarray(xi_tab),
            state_mask, state_decay, q_mask)


def _bucket_table_t():
    qi = np.arange(CH)[:, None]
    kj = np.arange(2 * CH)[None, :]
    dist = qi + CH - kj
    n = np.maximum(dist, 0)
    max_exact = N_BUCKETS // 2
    nf = np.maximum(n, 1).astype(np.float32)
    large = max_exact + (np.log(nf / np.float32(max_exact)) / np.float32(np.log(CH / max_exact))
                         * np.float32(N_BUCKETS - max_exact)).astype(np.int32)
    large = np.minimum(large, N_BUCKETS - 1)
    bucket = np.where(n < max_exact, n, large)
    return np.ascontiguousarray(np.where((dist >= 0) & (dist < CH), bucket, -1).astype(np.int32).T)


def _mesh_pos():
    return lax.axis_index("x"), lax.axis_index("y"), lax.axis_index("c")


def _blk(px, py, pc):
    return 4 * px + 2 * py + pc


def _in_proj_gather(x, norm_w, w_in_t):
    tchunk = 512

    def body(x_any, nw_any, win_any, proj_ref, wt_out, h_out, wt_ref, h_ref, stage, x_ref, nw_ref, win_ref,
             send_sems, recv_sems, out_sems, wb_sems, h_sem, in_sems):
        n_tiles = T // TM
        loads = [pltpu.make_async_copy(win_any, win_ref, in_sems.at[0]),
                 pltpu.make_async_copy(nw_any, nw_ref, in_sems.at[1])]
        loads += [pltpu.make_async_copy(x_any.at[pl.ds(i * TM, TM), :], x_ref.at[pl.ds(i * TM, TM), :], in_sems.at[2 + i])
                  for i in range(n_tiles)]
        for cp in loads:
            cp.start()
        loads[0].wait()
        x, y, c = _mesh_pos()
        me = _blk(x, y, c)
        sibling = (x, y, 1 - c)
        nbr_a, nbr_b, diag = (x ^ (1 - c), y ^ c), (x ^ c, y ^ (1 - c)), (1 - x, 1 - y)

        def copy(k, b, to):
            return pltpu.make_async_remote_copy(src_ref=wt_ref.at[b], dst_ref=wt_ref.at[b], send_sem=send_sems.at[k],
                                                recv_sem=recv_sems.at[k], device_id=to, device_id_type=MESH)

        wt_ref[me] = _bf(win_ref[...])
        first = [copy(0, me, sibling), copy(1, me, (*nbr_a, c))]
        for cp in first:
            cp.start()
        to_b = copy(2, me, (*nbr_b, c))

        loads[1].wait()
        nw = nw_ref[...]
        for i in range(n_tiles):
            rows = slice(i * TM, (i + 1) * TM)
            loads[2 + i].wait()
            xv = x_ref[rows, :]
            r = lax.rsqrt(jnp.mean(xv * xv, axis=-1, keepdims=True) + NORM_EPS)
            h_ref[rows, :] = _bf(xv * r * nw)
        keep = [pltpu.make_async_copy(h_ref, h_out, h_sem)]
        keep[0].start()

        writes = []

        def project(b):
            k = len(writes)
            if k >= 2:
                writes[k - 2].wait()
            keep.append(pltpu.make_async_copy(wt_ref.at[b], wt_out.at[b], wb_sems.at[k]))
            keep[-1].start()
            w = wt_ref[b]
            for t in range(T // tchunk):
                cols = slice(t * tchunk, (t + 1) * tchunk)
                stage[k % 2, :, cols] = _dot_nt(w, h_ref[cols, :])
            cp = pltpu.make_async_copy(stage.at[k % 2], proj_ref.at[b], out_sems.at[k % 2])
            cp.start()
            writes.append(cp)

        here = (x, y, c)
        project(me)
        copy(0, _blk(x, y, 1 - c), here).wait_recv()
        project(_blk(x, y, 1 - c))
        to_b.start()
        passed = [to_b]

        def landed(k, chip, relay_to=None):
            copy(k, _blk(*chip, c), here).wait_recv()
            out = [copy({1: 4, 2: 5, 3: 6}[k], _blk(*chip, c), sibling)]
            if relay_to is not None:
                out.append(copy(3, _blk(*chip, c), (*relay_to, c)))
            for cp in out:
                cp.start()
            passed.extend(out)
            project(_blk(*chip, c))

        def from_sibling(k, chip):
            copy(k, _blk(*chip, 1 - c), here).wait_recv()
            project(_blk(*chip, 1 - c))

        landed(1, nbr_a, relay_to=nbr_b)
        from_sibling(4, nbr_b)
        landed(2, nbr_b)
        from_sibling(5, nbr_a)
        landed(3, diag)
        from_sibling(6, diag)
        for cp in writes[-2:] + keep:
            cp.wait()
        for cp in first + passed:
            cp.wait_send()

    vm = pl.BlockSpec(memory_space=pltpu.VMEM)
    hbm = pl.BlockSpec(memory_space=pltpu.HBM)
    dma = pltpu.SemaphoreType.DMA
    proj, wt, h = pl.pallas_call(
        body, name="in_proj_gather",
        out_shape=(jax.ShapeDtypeStruct((N_DEV, WIN_BLK, T), F32), jax.ShapeDtypeStruct((N_DEV, WIN_BLK, D), BF16),
                   jax.ShapeDtypeStruct((T, D), BF16)),
        in_specs=[hbm, hbm, hbm], out_specs=(hbm, hbm, hbm),
        scratch_shapes=[pltpu.VMEM((N_DEV, WIN_BLK, D), BF16), pltpu.VMEM((T, D), BF16), pltpu.VMEM((2, WIN_BLK, T), F32),
                        pltpu.VMEM((T, D), F32), pltpu.VMEM((1, D), F32), pltpu.VMEM((WIN_BLK, D), F32),
                        dma((7,)), dma((7,)), dma((2,)), dma((N_DEV,)), dma, dma((2 + T // TM,))],
        compiler_params=_cparams(vmem=VMEM_WHOLE),
    )(x, norm_w, w_in_t)
    return proj.reshape(D_IN, T), wt.reshape(D_IN, D), h


def _gather_w_out(w_out_ref, own_ref, wo_ref, local_sem, send_sems, recv_sems, phase):
    x, y, c = _mesh_pos()
    me = _blk(x, y, c)
    sibling = (x, y, 1 - c)
    chips = [(1 - x, y), (x, 1 - y), (1 - x, 1 - y)]

    def copy(k, b, to, src=None):
        return pltpu.make_async_remote_copy(src_ref=wo_ref.at[b] if src is None else src, dst_ref=wo_ref.at[b],
                                            send_sem=send_sems.at[k], recv_sem=recv_sems.at[k], device_id=to,
                                            device_id_type=MESH)

    local = pltpu.make_async_copy(own_ref, wo_ref.at[me], local_sem)
    first = [copy(0, me, sibling, own_ref)] + [copy(1 + j, me, (*chip, c), own_ref) for j, chip in enumerate(chips)]
    passed = [copy(4 + j, _blk(*chip, c), sibling) for j, chip in enumerate(chips)]
    if phase == 0:
        own_ref[...] = _bf(w_out_ref[...])
        local.start()
        for cp in first:
            cp.start()
    elif phase == 1:
        for j, chip in enumerate(chips):
            copy(1 + j, _blk(*chip, c), (x, y, c)).wait_recv()
            passed[j].start()
    else:
        local.wait()
        copy(0, _blk(x, y, 1 - c), (x, y, c)).wait_recv()
        for j, chip in enumerate(chips):
            copy(4 + j, _blk(*chip, 1 - c), (x, y, c)).wait_recv()
        for cp in first + passed:
            cp.wait_send()


def _proj_piece(i, block_of):
    rows = PIECES[i]
    assert OFFS[i] % rows == 0
    return pl.BlockSpec((rows, CH), lambda n: (OFFS[i] // rows, block_of(n)))


def _proj_gate_halves(block_of):
    return [pl.BlockSpec((256, CH), lambda n, j=j: (OFFS[7] // 256 + j, block_of(n))) for j in range(2)]


def _swap_halves_t(t):
    half = RET_DK // 2
    parts = []
    for h in range(RET_H):
        parts += [t[h * RET_DK + half:(h + 1) * RET_DK], t[h * RET_DK:h * RET_DK + half]]
    return jnp.concatenate(parts, axis=0)


def _rotate(t, cos, sin):
    return t * cos + _swap_halves_t(t) * sin


def _group_norm_t(o):
    mu = jnp.mean(o, axis=0, keepdims=True)
    var = jnp.mean((o - mu) * (o - mu), axis=0, keepdims=True)
    rstd = lax.rsqrt(var + GN_EPS)
    return (o - mu) * rstd, rstd


def _retention_scores_t(q_t, k_rm_b, q_mask, decay_all):
    q_heads = _bf(jnp.concatenate([q_t] * RET_H, axis=1) * q_mask)
    return _dot(k_rm_b, q_heads) * decay_all


def _row_to_col(row):
    n = row.shape[1]
    diagonal = lax.broadcasted_iota(jnp.int32, (n, n), 0) == lax.broadcasted_iota(jnp.int32, (n, n), 1)
    return jnp.sum(jnp.where(diagonal, row, 0.0), axis=1, keepdims=True)


def _retention_fwd(proj, ret_norm_w, tables, w_out_blk):
    cos_t, sin_t, decay_all, zeta_tab, xi_tab, state_mask, state_decay, q_mask = tables

    def body(rq_ref, rk_ref, rv_ref, rg_ref, w_row, cos_ref, sin_ref, decay_ref, zeta_ref, xi_ref, smask_ref,
             sdecay_ref, qmask_ref, wout_ref, ro_ref, mix_ref, st_ref, wo_ref, w_ref,
             state, wo_own, wo_local, wo_send, wo_recv):
        n = pl.program_id(0)

        @pl.when(n == 0)
        def _():
            w_ref[...] = _row_to_col(w_row[...])
            state[...] = jnp.zeros_like(state)
            _gather_w_out(wout_ref, wo_own, wo_ref, wo_local, wo_send, wo_recv, phase=0)

        @pl.when(n == N_CH // 2)
        def _():
            _gather_w_out(wout_ref, wo_own, wo_ref, wo_local, wo_send, wo_recv, phase=1)

        @pl.when(n == N_CH - 1)
        def _():
            _gather_w_out(wout_ref, wo_own, wo_ref, wo_local, wo_send, wo_recv, phase=2)

        cos, sin = cos_ref[...], sin_ref[...]
        q_t = _rotate(rq_ref[...], cos, sin)
        k_t = _rotate(rk_ref[...], cos, sin) * (RET_DK ** -0.5)
        k_rm = k_t.T
        v_b = _bf(rv_ref[...])
        m_b = _bf(state[...])
        st_ref[0] = m_b
        scores_b = _bf(_retention_scores_t(q_t, _bf(k_rm), qmask_ref[...], decay_ref[...]))
        cross = _dot(m_b, _bf(q_t * xi_ref[...]))
        state[...] = state[...] * sdecay_ref[...] + _dot(v_b, _bf(k_rm * zeta_ref[...])) * smask_ref[...]
        heads = range(RET_H)
        rows = [slice(h * RET_DV, (h + 1) * RET_DV) for h in heads]
        o = [_dot(v_b[rows[h]], scores_b[:, h * CH:(h + 1) * CH]) + cross[rows[h]] for h in heads]
        for h in heads:
            rn, _ = _group_norm_t(o[h])
            g = rg_ref[rows[h], :]
            ro_ref[rows[h], :] = o[h]
            mix_ref[rows[h], :] = rn * w_ref[rows[h], :] * (g * _sigmoid(g))

    col = lambda w: pl.BlockSpec((w, CH), lambda n: (0, n))
    const = lambda shape: pl.BlockSpec(shape, lambda n: (0,) * len(shape))
    cur = lambda n: n
    return _Part(
        body,
        inputs=[proj, proj, proj, proj, ret_norm_w, cos_t, sin_t, decay_all, zeta_tab, xi_tab, state_mask, state_decay,
                q_mask, w_out_blk],
        in_specs=[_proj_piece(i, cur) for i in range(4)] + [
            const((1, RET_V)), col(RET_QK), col(RET_QK),
            const(decay_all.shape), const(zeta_tab.shape), const(xi_tab.shape), const(state_mask.shape),
            const(state_decay.shape), const(q_mask.shape), const((WOUT_BLK, D))],
        out_specs=[col(RET_V), col(RET_V), pl.BlockSpec((1, RET_V, RET_QK), lambda n: (n, 0, 0)),
                   pl.BlockSpec(memory_space=pl.ANY), const((RET_V, 1))],
        out_shape=[jax.ShapeDtypeStruct((RET_V, T), F32), jax.ShapeDtypeStruct((RET_V, T), F32),
                   jax.ShapeDtypeStruct((N_CH, RET_V, RET_QK), BF16), jax.ShapeDtypeStruct((N_DEV, WOUT_BLK, D), BF16),
                   jax.ShapeDtypeStruct((RET_V, 1), F32)],
        scratch_shapes=[pltpu.VMEM((RET_V, RET_QK), F32), pltpu.VMEM((WOUT_BLK, D), BF16), pltpu.SemaphoreType.DMA,
                        pltpu.SemaphoreType.DMA((N_DEV - 1,)), pltpu.SemaphoreType.DMA((N_DEV - 1,))])


def _retention_bwd(proj, ro, states, dmix, ret_norm_w, tables):
    cos_t, sin_t, decay_all, zeta_tab, xi_tab, state_mask, state_decay, q_mask = tables

    def body(*refs):
        @pl.when(pl.program_id(0) < N_CH)
        def _():
            chunk(*refs)

    def chunk(rq_ref, rk_ref, rv_ref, rg_ref, ro_ref, st_ref, dm_ref, w_ref, cos_ref, sin_ref, decay_ref, zeta_ref,
              xi_ref, smask_ref, sdecay_ref, qmask_ref, d_ref, dw_ref, gstate):
        i = pl.program_id(0)

        @pl.when(i == 0)
        def _():
            gstate[...] = jnp.zeros_like(gstate)
            dw_ref[...] = jnp.zeros_like(dw_ref)

        cos, sin = cos_ref[...], sin_ref[...]
        q_t = _rotate(rq_ref[...], cos, sin)
        k_t = _rotate(rk_ref[...], cos, sin) * (RET_DK ** -0.5)
        q_b, k_b = _bf(q_t), _bf(k_t)
        k_rm = k_t.T
        kz_b = _bf(k_rm * zeta_ref[...])
        qx_b = _bf(q_t * xi_ref[...])
        v_t = rv_ref[...]
        v_b = _bf(v_t)
        v_rm_b = _bf(v_t.T)
        decay = decay_ref[...]
        scores_b = _bf(_retention_scores_t(q_t, _bf(k_rm), qmask_ref[...], decay))
        heads = range(RET_H)
        rows = [slice(h * RET_DV, (h + 1) * RET_DV) for h in heads]
        qk_rows = [slice(h * RET_DK, (h + 1) * RET_DK) for h in heads]
        lanes = [slice(h * CH, (h + 1) * CH) for h in heads]
        do = []
        for h in heads:
            g, w, dm = rg_ref[rows[h], :], w_ref[rows[h], :], dm_ref[rows[h], :]
            rn, rstd = _group_norm_t(ro_ref[rows[h], :])
            sig = _sigmoid(g)
            silu = g * sig
            d_ref[OFFS[3] + h * RET_DV:OFFS[3] + (h + 1) * RET_DV, :] = _bf(dm * rn * w * (sig * (1.0 + g * (1.0 - sig))))
            dw_ref[rows[h], :] += dm * silu * rn
            drn = dm * silu * w
            do.append(rstd * (drn - jnp.mean(drn, axis=0, keepdims=True)
                              - rn * jnp.mean(drn * rn, axis=0, keepdims=True)))
        do_b = _bf(jnp.concatenate(do, axis=0))
        m_b = st_ref[0]
        g_all = gstate[...]
        g_b = _bf(g_all)
        dscores_b = [_bf(_dot(v_rm_b[:, rows[h]], do_b[rows[h]]) * decay[:, lanes[h]]) for h in heads]
        dq_cross = lax.dot_general(m_b, do_b, (((0,), (0,)), ((), ())), preferred_element_type=F32)
        dkz = _dot(v_rm_b, g_b)
        dv_cross = _dot_nt(g_b, kz_b)
        gstate[...] = g_all * sdecay_ref[...] + _dot_nt(do_b, qx_b) * smask_ref[...]
        dq = jnp.concatenate([_dot(k_b[qk_rows[h]], dscores_b[h]) for h in heads], axis=0) + dq_cross * xi_ref[...]
        dk = (jnp.concatenate([_dot_nt(q_b[qk_rows[h]], dscores_b[h]) for h in heads], axis=0)
              + (dkz * zeta_ref[...]).T) * (RET_DK ** -0.5)
        for h in heads:
            d_ref[OFFS[2] + h * RET_DV:OFFS[2] + (h + 1) * RET_DV, :] = _bf(
                _dot_nt(do_b[rows[h]], scores_b[:, lanes[h]]) + dv_cross[rows[h]])
        d_ref[OFFS[0]:OFFS[1], :] = _bf(dq * cos + _swap_halves_t(dq * sin))
        d_ref[OFFS[1]:OFFS[2], :] = _bf(dk * cos + _swap_halves_t(dk * sin))

    chunk_of = lambda i: N_CH - 1 - jnp.minimum(i, N_CH - 1)
    col = lambda w: pl.BlockSpec((w, CH), lambda i: (0, chunk_of(i)))
    const = lambda shape: pl.BlockSpec(shape, lambda i: (0,) * len(shape))
    return _Part(
        body,
        inputs=[proj, proj, proj, proj, ro, states, dmix, ret_norm_w, cos_t, sin_t, decay_all, zeta_tab, xi_tab,
                state_mask, state_decay, q_mask],
        in_specs=[_proj_piece(j, chunk_of) for j in range(4)] + [
                  col(RET_V),
                  pl.BlockSpec((1, RET_V, RET_QK), lambda i: (chunk_of(i), 0, 0)), col(RET_V),
                  const((RET_V, 1)), col(RET_QK), col(RET_QK),
                  const(decay_all.shape), const(zeta_tab.shape), const(xi_tab.shape), const(state_mask.shape),
                  const(state_decay.shape), const(q_mask.shape)],
        out_specs=[col(RET_W), const((RET_V, CH))],
        out_shape=[jax.ShapeDtypeStruct((RET_W, T), BF16), jax.ShapeDtypeStruct((RET_V, CH), F32)],
        scratch_shapes=[pltpu.VMEM((RET_V, RET_QK), F32)])


HQ_LANES = SWA_H * CH


def _head_lanes(hq):
    return slice(hq * CH, (hq + 1) * CH)


def _build_bias_t(bucket_ref, relb_ref, sinks_ref, bias_ref, sink_row):
    bt = bucket_ref[...]
    first = lax.broadcasted_iota(jnp.int32, bt.shape, 0) < CH
    for hq in range(SWA_H):
        b = jnp.full(bt.shape, NEG_INF, F32)
        for bk in range(N_BUCKETS):
            b = jnp.where(bt == bk, relb_ref[hq, bk], b)
        bias_ref[0, :, _head_lanes(hq)] = b
        bias_ref[1, :, _head_lanes(hq)] = jnp.where(first, NEG_INF, b)
        sink_row[:, _head_lanes(hq)] = jnp.full((1, CH), sinks_ref[hq], F32)


def _rms_t(t, w_col):
    r = lax.rsqrt(jnp.mean(t * t, axis=0, keepdims=True) + NORM_EPS)
    return t * r * w_col, r


def _rms_t_bwd(dn, t, r, w_col):
    u = dn * w_col
    return r * u - t * (r * r * r) * jnp.mean(u * t, axis=0, keepdims=True), dn * t * r


def _norm_kv(k_t, kw):
    return jnp.concatenate([_rms_t(k_t[g * SWA_D:(g + 1) * SWA_D], kw)[0] for g in range(SWA_KV)], axis=0)


def _kv_slot(a, kv):
    z = jnp.zeros_like(a)
    return jnp.concatenate([a, z] if kv == 0 else [z, a], axis=0)


def _softmax_t(s, sink):
    m = jnp.maximum(jnp.max(s, axis=0, keepdims=True), sink)
    p = jnp.exp(s - m)
    e_sink = jnp.exp(sink - m)
    inv = 1.0 / (jnp.sum(p, axis=0, keepdims=True) + e_sink)
    return p * inv, e_sink * inv


def _gate_rows(sg_halves, hq):
    per_half = SWA_H // 2
    return sg_halves[hq // per_half][(hq % per_half) * SWA_D:(hq % per_half + 1) * SWA_D, :]


def _swa_fwd(proj, q_norm_w, k_norm_w, sinks, rel_bias_t, bucket_t):
    def body(sq_ref, skp_ref, skc_ref, svp_ref, svc_ref, sg_lo, sg_hi, qw_row, kw_row, sinks_ref, relb_ref, bucket_ref,
             so_ref, mix_ref, qw_ref, kw_ref, bias_ref, sink_row):
        n = pl.program_id(0)

        @pl.when(n == 0)
        def _():
            qw_ref[...] = _row_to_col(qw_row[...])
            kw_ref[...] = _row_to_col(kw_row[...])
            _build_bias_t(bucket_ref, relb_ref, sinks_ref, bias_ref, sink_row)

        var = (n == 0).astype(jnp.int32)
        qw, kw = qw_ref[...], kw_ref[...]
        kn_band = jnp.concatenate([_norm_kv(skp_ref[...], kw), _norm_kv(skc_ref[...], kw)], axis=1)
        kb_rm = _bf(kn_band.T)
        vband = _bf(jnp.concatenate([svp_ref[...], svc_ref[...]], axis=1))
        q_all = jnp.concatenate(
            [_bf(_kv_slot(_rms_t(sq_ref[hq * SWA_D:(hq + 1) * SWA_D, :], qw)[0] * (SWA_D ** -0.5), hq // SWA_G))
             for hq in range(SWA_H)], axis=1)
        probs, _ = _softmax_t(_dot(kb_rm, q_all) + bias_ref[var], sink_row[...])
        probs_b = _bf(probs)
        for kv in range(SWA_KV):
            o = _dot(vband[kv * SWA_D:(kv + 1) * SWA_D], probs_b[:, kv * SWA_G * CH:(kv + 1) * SWA_G * CH])
            for j in range(SWA_G):
                rows = slice((kv * SWA_G + j) * SWA_D, (kv * SWA_G + j + 1) * SWA_D)
                oh = o[:, j * CH:(j + 1) * CH]
                g = _gate_rows((sg_lo, sg_hi), kv * SWA_G + j)
                so_ref[rows, :] = oh
                mix_ref[rows, :] = oh * (g * _sigmoid(g))

    col = lambda w: pl.BlockSpec((w, CH), lambda n: (0, n))
    const = lambda shape: pl.BlockSpec(shape, lambda n: (0,) * len(shape))
    smem = pl.BlockSpec(memory_space=pltpu.SMEM)
    cur = lambda n: n
    prev = lambda n: jnp.maximum(n - 1, 0)
    return _Part(
        body,
        inputs=[proj, proj, proj, proj, proj, proj, proj, q_norm_w, k_norm_w, sinks, rel_bias_t, bucket_t],
        in_specs=[_proj_piece(4, cur), _proj_piece(5, prev), _proj_piece(5, cur), _proj_piece(6, prev),
                  _proj_piece(6, cur)] + _proj_gate_halves(cur) + [
                  const((1, SWA_D)), const((1, SWA_D)), smem, smem, const((2 * CH, CH))],
        out_specs=[col(512), col(512), const((SWA_D, 1)), const((SWA_D, 1))],
        out_shape=[jax.ShapeDtypeStruct((512, T), F32), jax.ShapeDtypeStruct((512, T), F32),
                   jax.ShapeDtypeStruct((SWA_D, 1), F32), jax.ShapeDtypeStruct((SWA_D, 1), F32)],
        scratch_shapes=[pltpu.VMEM((2, 2 * CH, HQ_LANES), F32), pltpu.VMEM((1, HQ_LANES), F32)])


def _swa_bwd(proj, so, dmix, q_norm_w, k_norm_w, sinks, rel_bias_t, bucket_t):
    def body(sq_ref, skp_ref, skc_ref, svp_ref, svc_ref, sg_lo, sg_hi, so_ref, dm_ref, qw_ref, kw_ref, sinks_ref,
             relb_ref, bucket_ref, d_ref, dbias_ref, dsink_ref, dqw_ref, dkw_ref,
             bias_ref, sink_row, band_dk, band_dv, carry_dk, carry_dv, hold_dq, hold_dg):
        n = pl.program_id(0)
        slot = n % 2

        @pl.when(n == 0)
        def _():
            _build_bias_t(bucket_ref, relb_ref, sinks_ref, bias_ref, sink_row)
            for ref in (dbias_ref, dsink_ref, dqw_ref, dkw_ref, carry_dk, carry_dv):
                ref[...] = jnp.zeros_like(ref)

        qw, kw = qw_ref[...], kw_ref[...]

        @pl.when(n < N_CH)
        def _():
            var = (n == 0).astype(jnp.int32)
            kn_band = jnp.concatenate([_norm_kv(skp_ref[...], kw), _norm_kv(skc_ref[...], kw)], axis=1)
            kb_rm = _bf(kn_band.T)
            kn_band_s = _bf(kn_band * (SWA_D ** -0.5))
            vband_f = jnp.concatenate([svp_ref[...], svc_ref[...]], axis=1)
            vb_rm = _bf(vband_f.T)
            q_raw, q_rstd, qs_b, do_b = [], [], [], []
            for hq in range(SWA_H):
                rows = slice(hq * SWA_D, (hq + 1) * SWA_D)
                q_t = sq_ref[rows, :]
                qn, rq = _rms_t(q_t, qw)
                g = _gate_rows((sg_lo, sg_hi), hq)
                sig = _sigmoid(g)
                dm = dm_ref[rows, :]
                hold_dg[slot, rows, :] = _bf(dm * so_ref[rows, :] * (sig * (1.0 + g * (1.0 - sig))))
                q_raw.append(q_t)
                q_rstd.append(rq)
                qs_b.append(_bf(qn * (SWA_D ** -0.5)))
                do_b.append(_bf(dm * (g * sig)))
            q_all = jnp.concatenate([_kv_slot(qs_b[hq], hq // SWA_G) for hq in range(SWA_H)], axis=1)
            do_all = jnp.concatenate([_kv_slot(do_b[hq], hq // SWA_G) for hq in range(SWA_H)], axis=1)
            probs, p_sink = _softmax_t(_dot(kb_rm, q_all) + bias_ref[var], sink_row[...])
            dprobs = _dot(vb_rm, do_all)
            t = jnp.sum(probs * dprobs, axis=0, keepdims=True)
            dlog = probs * (dprobs - t)
            dsink_ref[...] += -(p_sink * t)
            dbias_ref[...] += dlog
            dlog_b, probs_b = _bf(dlog), _bf(probs)
            dkn, dvv = [], []
            for kv in range(SWA_KV):
                heads = range(kv * SWA_G, (kv + 1) * SWA_G)
                lanes = slice(kv * SWA_G * CH, (kv + 1) * SWA_G * CH)
                dvv.append(_dot_nt(jnp.concatenate([do_b[hq] for hq in heads], axis=1), probs_b[:, lanes]))
                dkn.append(_dot_nt(jnp.concatenate([qs_b[hq] for hq in heads], axis=1), dlog_b[:, lanes]))
                dqn = _dot(kn_band_s[kv * SWA_D:(kv + 1) * SWA_D], dlog_b[:, lanes])
                for j, hq in enumerate(heads):
                    dq_t, dqw_terms = _rms_t_bwd(dqn[:, j * CH:(j + 1) * CH], q_raw[hq], q_rstd[hq], qw)
                    hold_dq[slot, hq * SWA_D:(hq + 1) * SWA_D, :] = _bf(dq_t)
                    dqw_ref[...] += dqw_terms
            band_dk[...] = jnp.concatenate(dkn, axis=0)
            band_dv[...] = jnp.concatenate(dvv, axis=0)

        @pl.when(n == N_CH)
        def _():
            band_dk[...] = jnp.zeros_like(band_dk)
            band_dv[...] = jnp.zeros_like(band_dv)

        @pl.when(n >= 1)
        def _():
            dkn_prev = carry_dk[...] + band_dk[:, 0:CH]
            k_t = skp_ref[...]
            for kv in range(SWA_KV):
                rows = slice(kv * SWA_D, (kv + 1) * SWA_D)
                _, rk = _rms_t(k_t[rows], kw)
                dk_t, dkw_terms = _rms_t_bwd(dkn_prev[rows], k_t[rows], rk, kw)
                d_ref[SWA_OFFS[1] + kv * SWA_D:SWA_OFFS[1] + (kv + 1) * SWA_D, :] = _bf(dk_t)
                dkw_ref[...] += dkw_terms
            d_ref[SWA_OFFS[2]:SWA_OFFS[3], :] = _bf(carry_dv[...] + band_dv[:, 0:CH])
            d_ref[SWA_OFFS[0]:SWA_OFFS[1], :] = hold_dq[1 - slot]
            d_ref[SWA_OFFS[3]:SWA_OFFS[4], :] = hold_dg[1 - slot]

        carry_dk[...] = band_dk[:, CH:2 * CH]
        carry_dv[...] = band_dv[:, CH:2 * CH]

    cur_block = lambda n: jnp.minimum(n, N_CH - 1)
    prev_block = lambda n: jnp.maximum(n - 1, 0)
    col = lambda w: pl.BlockSpec((w, CH), lambda n: (0, cur_block(n)))
    prev = lambda w: pl.BlockSpec((w, CH), lambda n: (0, prev_block(n)))
    const = lambda shape: pl.BlockSpec(shape, lambda n: (0,) * len(shape))
    smem = pl.BlockSpec(memory_space=pltpu.SMEM)
    return _Part(
        body,
        inputs=[proj, proj, proj, proj, proj, proj, proj, so, dmix, q_norm_w, k_norm_w, sinks, rel_bias_t, bucket_t],
        in_specs=[_proj_piece(4, cur_block), _proj_piece(5, prev_block), _proj_piece(5, cur_block),
                  _proj_piece(6, prev_block), _proj_piece(6, cur_block)] + _proj_gate_halves(cur_block) + [
                  col(512), col(512), const((SWA_D, 1)), const((SWA_D, 1)), smem, smem, const((2 * CH, CH))],
        out_specs=[prev(SWA_W), const((2 * CH, HQ_LANES)), const((1, HQ_LANES)),
                   const((SWA_D, CH)), const((SWA_D, CH))],
        out_shape=[jax.ShapeDtypeStruct((SWA_W, T), BF16),
                   jax.ShapeDtypeStruct((2 * CH, HQ_LANES), F32), jax.ShapeDtypeStruct((1, HQ_LANES), F32),
                   jax.ShapeDtypeStruct((SWA_D, CH), F32), jax.ShapeDtypeStruct((SWA_D, CH), F32)],
        scratch_shapes=[pltpu.VMEM((2, 2 * CH, HQ_LANES), F32), pltpu.VMEM((1, HQ_LANES), F32),
                        pltpu.VMEM((128, 2 * CH), F32), pltpu.VMEM((128, 2 * CH), F32),
                        pltpu.VMEM((128, CH), F32), pltpu.VMEM((128, CH), F32),
                        pltpu.VMEM((2, 512, CH), BF16), pltpu.VMEM((2, 512, CH), BF16)])


SMALL_GRAD_SHAPES = ((1, RET_V), (1, CH), (1, CH), (1, CH), (SWA_H, N_BUCKETS))


def _finish_small_grads(dbias_ref, bucket_ref, dsink_ref, dqw_ref, dkw_ref, drw_ref, rw_o, qw_o, kw_o, sink_o, relb_o):
    bt = bucket_ref[...]
    row = lax.broadcasted_iota(jnp.int32, (SWA_H, N_BUCKETS), 0)
    col = lax.broadcasted_iota(jnp.int32, (SWA_H, N_BUCKETS), 1)
    lane = lax.broadcasted_iota(jnp.int32, (1, CH), 1)
    acc = jnp.zeros((SWA_H, N_BUCKETS), F32)
    sink = jnp.zeros((1, CH), F32)
    for hq in range(SWA_H):
        d = dbias_ref[:, _head_lanes(hq)]
        for bk in range(N_BUCKETS):
            s = jnp.sum(jnp.sum(jnp.where(bt == bk, d, 0.0), axis=0, keepdims=True), axis=1, keepdims=True)
            acc = acc + jnp.where((row == hq) & (col == bk), s, 0.0)
        sink = sink + jnp.where(lane == hq, jnp.sum(dsink_ref[:, _head_lanes(hq)], axis=1, keepdims=True), 0.0)
    relb_o[...] = acc
    sink_o[...] = sink
    for src, dst in ((dqw_ref, qw_o), (dkw_ref, kw_o)):
        padded = jnp.concatenate([src[...], jnp.zeros((CH - SWA_D, CH), F32)], axis=0)
        dst[...] = jnp.sum(padded.T, axis=0, keepdims=True)
    rw_o[...] = jnp.sum(drw_ref[...].T, axis=0, keepdims=True)


def _out_proj(mix_r, mix_s, wo, x, target):
    def body(mr_ref, ms_ref, w_ref, x_ref, t_ref, loss_ref, dy_ref, dmr_ref, dms_ref, gw_ref):
        i = pl.program_id(0)

        @pl.when(i == 0)
        def _():
            loss_ref[...] = jnp.zeros_like(loss_ref)
            gw_ref[...] = jnp.zeros_like(gw_ref)

        mixed = jnp.concatenate([mr_ref[...], ms_ref[...]], axis=0)
        w = w_ref[...]
        err = x_ref[...] + _dot(_bf(mixed.T), w) - t_ref[...]
        loss_ref[...] += jnp.sum(jnp.sum(err * err, axis=1, keepdims=True), axis=0, keepdims=True)
        dy = err * (1.0 / D)
        dy_ref[...] = dy
        dy_b = _bf(dy)
        dmix = _dot_nt(w, dy_b)
        dmr_ref[...] = dmix[0:512]
        dms_ref[...] = dmix[512:D]
        gw_ref[...] += _dot(_bf(mixed), dy_b)

    row = lambda w: pl.BlockSpec((TM, w), lambda i: (i, 0))
    col = lambda w: pl.BlockSpec((w, TM), lambda i: (0, i))
    const = lambda shape: pl.BlockSpec(shape, lambda i: (0,) * len(shape))
    return pl.pallas_call(
        body, name="out_proj", grid=(T // TM,),
        in_specs=[col(512), col(512), const((D, D)), row(D), row(D)],
        out_specs=(const((1, 1)), row(D), col(512), col(512), const((D, D))),
        out_shape=(jax.ShapeDtypeStruct((1, 1), F32), jax.ShapeDtypeStruct((T, D), F32),
                   jax.ShapeDtypeStruct((512, T), F32), jax.ShapeDtypeStruct((512, T), F32),
                   jax.ShapeDtypeStruct((D, D), F32)),
        compiler_params=_cparams(1),
    )(mix_r, mix_s, wo, x, target)


LOSS_ROW = 5
ALL_CHIPS = ((0, 0), (0, 1), (1, 0), (1, 1))


def _w_out_reduce_scatter(gwo):
    def body(gwo_ref, gout_o, own, rcv, snd, got, tot, local_sems, a_send, a_recv, b_send, b_recv):
        k = pl.program_id(0)
        x_, y_, c_ = _mesh_pos()
        sibling = (x_, y_, 1 - c_)
        rel_chips = [(1 - x_, y_), (x_, 1 - y_), (1 - x_, 1 - y_)]

        def to_sibling(j):
            px, py = ALL_CHIPS[j]
            return pltpu.make_async_remote_copy(src_ref=gwo_ref.at[_blk(px, py, 1 - c_)], dst_ref=rcv.at[j],
                                                send_sem=a_send.at[j], recv_sem=a_recv.at[j], device_id=sibling,
                                                device_id_type=MESH)

        def local(j):
            px, py = ALL_CHIPS[j]
            return pltpu.make_async_copy(gwo_ref.at[_blk(px, py, c_)], own.at[j], local_sems.at[j])

        def to_chip(q):
            return pltpu.make_async_remote_copy(src_ref=snd.at[q], dst_ref=got.at[q], send_sem=b_send.at[q],
                                                recv_sem=b_recv.at[q], device_id=(*rel_chips[q], c_), device_id_type=MESH)

        @pl.when(k == 0)
        def _():
            for j in range(len(ALL_CHIPS)):
                to_sibling(j).start()
                local(j).start()

        @pl.when(k == 2)
        def _():
            for j in range(len(ALL_CHIPS)):
                local(j).wait()
                to_sibling(j).wait_recv()
            for q in range(3):
                j = 2 * rel_chips[q][0] + rel_chips[q][1]
                snd[q] = _bf(own[j] + rcv[j])
                to_chip(q).start()
            jm = 2 * x_ + y_
            tot[...] = own[jm] + rcv[jm]

        @pl.when(k == N_CH)
        def _():
            g = tot[...]
            for q in range(3):
                to_chip(q).wait_recv()
                g = g + got[q].astype(F32)
            gout_o[...] = g
            for j in range(len(ALL_CHIPS)):
                to_sibling(j).wait_send()
            for q in range(3):
                to_chip(q).wait_send()

    dma = pltpu.SemaphoreType.DMA
    return _Part(
        body, inputs=[gwo], in_specs=[pl.BlockSpec(memory_space=pl.ANY)],
        out_specs=[pl.BlockSpec((WOUT_BLK, D), lambda k: (0, 0))], out_shape=[jax.ShapeDtypeStruct((WOUT_BLK, D), F32)],
        scratch_shapes=[pltpu.VMEM((4, WOUT_BLK, D), F32), pltpu.VMEM((4, WOUT_BLK, D), F32),
                        pltpu.VMEM((3, WOUT_BLK, D), BF16), pltpu.VMEM((3, WOUT_BLK, D), BF16),
                        pltpu.VMEM((WOUT_BLK, D), F32), dma((4,)), dma((4,)), dma((4,)), dma((3,)), dma((3,))])


def _in_proj_bwd_rs(d_ret, d_swa, h, wt, x, norm_w, dy, sse, small_acc):
    n_blocks, n_tiles = N_DEV, T // TM
    last = n_blocks + n_tiles - 1

    def body(da_any, db_any, da_ref, db_ref, h_any, w_any, x_ref, nw_ref, dy_ref, sse_ref,
             dbias_ref, bucket_ref, dsink_ref, dqw_ref, dkw_ref, drw_ref, gx_ref, gin_o, suma_o, sumb_o,
             stage, d2d_src, mine, rcv, snd, got, tot, tab_a, tab_b, gnw_acc, rw_ref, qw_ref, kw_ref, sk_ref, rb_ref,
             h_ref, w_ref, dp_sems, d2d_send, d2d_recv, ici_send, ici_recv, s_send, s_recv, hw_sems):
        k = pl.program_id(0)
        x_, y_, c_ = _mesh_pos()
        h_load = pltpu.make_async_copy(h_any, h_ref, hw_sems.at[0])
        w_load = pltpu.make_async_copy(w_any, w_ref, hw_sems.at[1])
        me = _blk(x_, y_, c_)
        sibling = (x_, y_, 1 - c_)
        rel_chips = [(1 - x_, y_), (x_, 1 - y_), (1 - x_, 1 - y_), (x_, y_)]

        def block_of_step(s):
            return _blk(*rel_chips[s // 2], 1 - c_ if s % 2 == 0 else c_)

        def fetch(s, wait):
            slot, b = s % 2, block_of_step(s)
            split = RET_W - 4 * WIN_BLK

            def run(src, dst, sem):
                cp = pltpu.make_async_copy(src, dst, sem)
                cp.wait() if wait else cp.start()

            @pl.when(b < 4)
            def _():
                run(da_any.at[pl.ds(pl.multiple_of(b * WIN_BLK, 16), WIN_BLK), :], stage.at[slot], dp_sems.at[slot, 0])

            @pl.when(b == 4)
            def _():
                run(da_any.at[pl.ds(4 * WIN_BLK, split), :], stage.at[slot, pl.ds(0, split), :], dp_sems.at[slot, 0])
                run(db_any.at[pl.ds(0, WIN_BLK - split), :], stage.at[slot, pl.ds(split, WIN_BLK - split), :],
                    dp_sems.at[slot, 1])

            @pl.when(b > 4)
            def _():
                run(db_any.at[pl.ds(pl.multiple_of(b * WIN_BLK - RET_W, 16), WIN_BLK), :], stage.at[slot],
                    dp_sems.at[slot, 0])

        def d2d_copy(r):
            return pltpu.make_async_remote_copy(src_ref=d2d_src, dst_ref=rcv.at[r], send_sem=d2d_send.at[r],
                                                recv_sem=d2d_recv.at[r], device_id=sibling, device_id_type=MESH)

        def ici_copy(r):
            return pltpu.make_async_remote_copy(src_ref=snd.at[r], dst_ref=got.at[r], send_sem=ici_send.at[r],
                                                recv_sem=ici_recv.at[r], device_id=(*rel_chips[r], c_),
                                                device_id_type=MESH)

        def table_copies():
            return [pltpu.make_async_remote_copy(src_ref=tab.at[me], dst_ref=tab.at[me], send_sem=s_send.at[a, p - 1],
                                                 recv_sem=s_recv.at[a, p - 1],
                                                 device_id=(x_ ^ (p >> 2), y_ ^ ((p >> 1) & 1), c_ ^ (p & 1)),
                                                 device_id_type=MESH)
                    for p in range(1, N_DEV) for a, tab in enumerate((tab_a, tab_b))]

        def chip_sum(r):
            d2d_copy(r).wait_recv()
            total = mine[...] + rcv[r]
            if r < 3:
                snd[r] = _bf(total)
                ici_copy(r).start()
            else:
                tot[...] = total

        for s in range(n_blocks):
            @pl.when(k == s)
            def _(s=s):
                r = s // 2
                if s == 0:
                    gnw_acc[...] = jnp.zeros_like(gnw_acc)
                    h_load.start()
                    fetch(0, wait=False)
                    w_load.start()
                fetch(s, wait=True)
                if s + 1 < n_blocks:
                    fetch(s + 1, wait=False)
                if s == 0:
                    h_load.wait()
                gw = _dot(stage[s % 2], h_ref[...])
                if s % 2 == 0:
                    if s >= 2:
                        chip_sum(r - 1)
                        d2d_copy(r - 1).wait_send()
                    d2d_src[...] = gw
                    d2d_copy(r).start()
                else:
                    mine[...] = gw

        @pl.when(k == n_blocks)
        def _():
            chip_sum(3)
            w_load.wait()

        @pl.when(k >= n_blocks)
        def _():
            dp = jnp.concatenate([da_ref[...], db_ref[...]], axis=0)
            xv, nw = x_ref[...], nw_ref[...]
            r = lax.rsqrt(jnp.mean(xv * xv, axis=-1, keepdims=True) + NORM_EPS)
            dh = lax.dot_general(dp, w_ref[...], (((0,), (0,)), ((), ())), preferred_element_type=F32)
            u = dh * nw
            gx_ref[...] = dy_ref[...] + r * u - xv * (r * r * r) * jnp.mean(u * xv, axis=-1, keepdims=True)
            gnw_acc[...] += jnp.sum(dh * (xv * r), axis=0, keepdims=True)

        @pl.when(k == n_blocks)
        def _():
            _finish_small_grads(dbias_ref, bucket_ref, dsink_ref, dqw_ref, dkw_ref, drw_ref,
                                rw_ref, qw_ref, kw_ref, sk_ref, rb_ref)

        @pl.when(k == last)
        def _():
            mine_a, mine_b = tab_a.at[me], tab_b.at[me]
            mine_a[...] = jnp.zeros_like(mine_a)
            mine_b[...] = jnp.zeros_like(mine_b)
            for row, ref in enumerate((gnw_acc, rw_ref, qw_ref, kw_ref, sk_ref)):
                mine_a[row:row + 1, 0:ref.shape[1]] = ref[...]
            mine_a[LOSS_ROW:LOSS_ROW + 1, 0:1] = sse_ref[...]
            mine_b[:, 0:N_BUCKETS] = rb_ref[...]
            tables = table_copies()
            for cp in tables:
                cp.start()
            g_in = tot[...]
            for q in range(3):
                ici_copy(q).wait_recv()
                g_in = g_in + got[q].astype(F32)
            gin_o[...] = g_in
            for cp in tables:
                cp.wait_recv()
            sum_a, sum_b = tab_a[0], tab_b[0]
            for b in range(1, N_DEV):
                sum_a = sum_a + tab_a[b]
                sum_b = sum_b + tab_b[b]
            suma_o[...] = sum_a
            sumb_o[...] = sum_b
            d2d_copy(3).wait_send()
            for q in range(3):
                ici_copy(q).wait_send()
            for cp in tables:
                cp.wait_send()

    tile = lambda k: jnp.maximum(k - n_blocks, 0)
    row = lambda w: pl.BlockSpec((TM, w), lambda k: (tile(k), 0))
    col = lambda w: pl.BlockSpec((w, TM), lambda k: (0, tile(k)))
    const = lambda shape, **kw: pl.BlockSpec(shape, lambda k: (0,) * len(shape), **kw)
    once = dict(pipeline_mode=pl.Buffered(1))
    hbm = pl.BlockSpec(memory_space=pltpu.HBM)
    dma = pltpu.SemaphoreType.DMA
    return pl.pallas_call(
        body, name="in_proj_bwd_rs", grid=(n_blocks + n_tiles,),
        in_specs=[hbm, hbm, col(RET_W), col(SWA_W), hbm, hbm,
                  row(D), const((1, D)), row(D), const((1, 1))]
                 + [const(a.shape) for a in small_acc],
        out_specs=(row(D), const((WIN_BLK, D)), const((8, D)), const((SWA_H, CH))),
        out_shape=(jax.ShapeDtypeStruct((T, D), F32), jax.ShapeDtypeStruct((WIN_BLK, D), F32),
                   jax.ShapeDtypeStruct((8, D), F32), jax.ShapeDtypeStruct((SWA_H, CH), F32)),
        scratch_shapes=[
            pltpu.VMEM((2, WIN_BLK, T), BF16), pltpu.VMEM((WIN_BLK, D), F32),
            pltpu.VMEM((WIN_BLK, D), F32),
            pltpu.VMEM((4, WIN_BLK, D), F32), pltpu.VMEM((3, WIN_BLK, D), BF16),
            pltpu.VMEM((3, WIN_BLK, D), BF16), pltpu.VMEM((WIN_BLK, D), F32),
            pltpu.VMEM((N_DEV, 8, D), F32), pltpu.VMEM((N_DEV, SWA_H, CH), F32),
            pltpu.VMEM((1, D), F32),
        ] + [pltpu.VMEM(s, F32) for s in SMALL_GRAD_SHAPES] + [
            pltpu.VMEM((T, D), BF16), pltpu.VMEM((D_IN, D), BF16),
            dma((2, 2)), dma((4,)), dma((4,)), dma((3,)), dma((3,)), dma((2, 7)), dma((2, 7)), dma((2,)),
        ],
        compiler_params=_cparams(1),
    )(d_ret, d_swa, d_ret, d_swa, h, wt, x, norm_w, dy, sse, *small_acc)


SMALL_SHAPES = ((1, D), (1, 512), (1, SWA_D), (1, SWA_D), (1, SWA_H), (SWA_H, N_BUCKETS))


def _adamw_all(grad_x, g_in_t, g_out, sum_a, sum_b, w_in_t, m_in_t, v_in_t, w_out, m_out, v_out, small_w, small_m, small_v):
    n_small = len(SMALL_SHAPES)
    halves = 2

    def body(*refs):
        gx_in = refs[0]
        gin_any, gout_any, suma_any, sumb_any, w_in_ref, m_in_ref, v_in_ref, w_out_ref, m_out_ref, v_out_ref = refs[1:11]
        pos = 11
        sw_refs, sm_refs, sv_refs = (refs[pos + i * n_small:pos + (i + 1) * n_small] for i in range(3))
        pos += 3 * n_small
        g_in_o, d_in_o, nm_in_o, nv_in_o, g_out_o, d_out_o, nm_out_o, nv_out_o = refs[pos:pos + 8]
        pos += 8
        sg_o, sd_o, snm_o, snv_o = (refs[pos + i * n_small:pos + (i + 1) * n_small] for i in range(4))
        pos += 4 * n_small
        loss_o, gx_out, gin_ref, gout_ref, suma_ref, sumb_ref, load_sems = refs[pos:]
        i = pl.program_id(0)
        gx_out[...] = gx_in[...]

        @pl.when(i == 0)
        def _():
            loads = [pltpu.make_async_copy(src, dst, load_sems.at[j]) for j, (src, dst) in enumerate(
                ((gin_any, gin_ref), (gout_any, gout_ref), (suma_any, suma_ref), (sumb_any, sumb_ref)))]
            for cp in loads:
                cp.start()
            for cp in loads:
                cp.wait()

        g = gout_ref[pl.ds(pl.multiple_of(i * (WOUT_BLK // halves), 8), WOUT_BLK // halves), :]
        d, nm, nv = _adamw(w_out_ref[...], g, m_out_ref[...], v_out_ref[...])
        g_out_o[...], d_out_o[...], nm_out_o[...], nv_out_o[...] = g, d, nm, nv
        g = gin_ref[pl.ds(pl.multiple_of(i * (WIN_BLK // halves), 8), WIN_BLK // halves), :]
        d, nm, nv = _adamw(w_in_ref[...], g, m_in_ref[...], v_in_ref[...])
        g_in_o[...], d_in_o[...], nm_in_o[...], nv_in_o[...] = g, d, nm, nv

        @pl.when(pl.program_id(0) == 0)
        def _():
            loss_o[...] = suma_ref[LOSS_ROW:LOSS_ROW + 1, 0:1] * (0.5 / D)
            for r, (rows, lanes) in enumerate(SMALL_SHAPES):
                g = suma_ref[r:r + 1, 0:lanes] if rows == 1 else sumb_ref[:, 0:lanes]
                d, nm, nv = _adamw(sw_refs[r][...], g, sm_refs[r][...], sv_refs[r][...])
                sg_o[r][...], sd_o[r][...], snm_o[r][...], snv_o[r][...] = g, d, nm, nv

    half = lambda rows: pl.BlockSpec((rows // halves, D), lambda i: (i, 0))
    const = lambda shape: pl.BlockSpec(shape, lambda i: (0,) * len(shape))
    win = jax.ShapeDtypeStruct((WIN_BLK, D), F32)
    wout = jax.ShapeDtypeStruct((WOUT_BLK, D), F32)
    smalls = tuple(jax.ShapeDtypeStruct(s, F32) for s in SMALL_SHAPES)
    small_specs = [const(s) for s in SMALL_SHAPES]
    hbm = pl.BlockSpec(memory_space=pltpu.HBM)
    outs = pl.pallas_call(
        body, name="adamw_all", grid=(halves,),
        in_specs=[half(T)] + [hbm] * 4 + [half(WIN_BLK)] * 3 + [half(WOUT_BLK)] * 3 + small_specs * 3,
        out_specs=tuple([half(WIN_BLK)] * 4 + [half(WOUT_BLK)] * 4 + small_specs * 4 + [const((1, 1)), half(T)]),
        out_shape=(win,) * 4 + (wout,) * 4 + smalls * 4 + (jax.ShapeDtypeStruct((1, 1), F32),
                                                           jax.ShapeDtypeStruct(grad_x.shape, F32)),
        scratch_shapes=[pltpu.VMEM((WIN_BLK, D), F32), pltpu.VMEM((WOUT_BLK, D), F32), pltpu.VMEM((8, D), F32),
                        pltpu.VMEM((SWA_H, CH), F32), pltpu.SemaphoreType.DMA((4,))],
        compiler_params=_cparams(1, vmem=VMEM_WHOLE),
    )(grad_x, g_in_t, g_out, sum_a, sum_b, w_in_t, m_in_t, v_in_t, w_out, m_out, v_out, *small_w, *small_m, *small_v)
    big, rest = outs[:8], outs[8:]
    return big, [rest[i * n_small:(i + 1) * n_small] for i in range(4)], rest[4 * n_small], rest[4 * n_small + 1]


def _small_rows(norm_w, ret_norm_w, q_norm_w, k_norm_w, sinks, rel_bias):
    return (norm_w.reshape(1, D), ret_norm_w.reshape(1, 512), q_norm_w.reshape(1, SWA_D), k_norm_w.reshape(1, SWA_D),
            sinks.reshape(1, SWA_H), rel_bias.T)


def _small_leaves(rows):
    return (rows[0].reshape(D), rows[1].reshape(512), rows[2].reshape(SWA_D), rows[3].reshape(SWA_D),
            rows[4].reshape(SWA_H), rows[5].T)


def kernel(x, norm_w, w_in, ret_norm_w, q_norm_w, k_norm_w, sinks, rel_bias, w_out, loss_target, m_norm_w, m_w_in, m_ret_norm_w, m_q_norm_w, m_k_norm_w, m_sinks, m_rel_bias, m_w_out, v_norm_w, v_w_in, v_ret_norm_w, v_q_norm_w, v_k_norm_w, v_sinks, v_rel_bias, v_w_out):
    x2 = x.reshape(T, D)
    target = loss_target.reshape(T, D)
    nw = norm_w.reshape(1, D)
    relb = rel_bias.T
    ret_tables = _rotary_tables_t() + _retention_tables_t()
    bucket_t = _bucket_table_t()

    proj, wt, h = _in_proj_gather(x2, nw, w_in.T)
    (ro, mix_r, states, wo, rnw_col), (so, mix_s, qnw_col, knw_col) = _fused_call("attn_fwd", N_CH, [
        _retention_fwd(proj, ret_norm_w.reshape(1, RET_V), ret_tables, w_out),
        _swa_fwd(proj, q_norm_w.reshape(1, SWA_D), k_norm_w.reshape(1, SWA_D), sinks, relb, bucket_t)])
    sse, dy, dmix_r, dmix_s, gwo = _out_proj(mix_r, mix_s, wo.reshape(D, D), x2, target)

    (d_ret, drw_acc), (d_swa, dbias, dsink_acc, dqw_acc, dkw_acc), (g_out,) = _fused_call("attn_bwd", N_CH + 1, [
        _retention_bwd(proj, ro, states, dmix_r, rnw_col, ret_tables),
        _swa_bwd(proj, so, dmix_s, qnw_col, knw_col, sinks, relb, bucket_t),
        _w_out_reduce_scatter(gwo.reshape(N_DEV, WOUT_BLK, D))])
    grad_x, g_in_t, sum_a, sum_b = _in_proj_bwd_rs(d_ret, d_swa, h, wt, x2, nw, dy, sse,
                                                   (dbias, bucket_t, dsink_acc, dqw_acc, dkw_acc, drw_acc))

    small_w = _small_rows(norm_w, ret_norm_w, q_norm_w, k_norm_w, sinks, rel_bias)
    small_m = _small_rows(m_norm_w, m_ret_norm_w, m_q_norm_w, m_k_norm_w, m_sinks, m_rel_bias)
    small_v = _small_rows(v_norm_w, v_ret_norm_w, v_q_norm_w, v_k_norm_w, v_sinks, v_rel_bias)
    big, small, loss, grad_x = _adamw_all(grad_x, g_in_t, g_out, sum_a, sum_b, w_in.T, m_w_in.T, v_w_in.T, w_out, m_w_out, v_w_out,
                                  small_w, small_m, small_v)

    def leaves(i):
        a = _small_leaves(small[i])
        return (a[0], big[i].T, a[1], a[2], a[3], a[4], a[5], big[4 + i])

    return (loss.reshape(()), grad_x.reshape(1, T, D), *leaves(0), *leaves(1), *leaves(2), *leaves(3))
```

```python
from typing import Callable, NamedTuple

import numpy as np
import jax
import jax.numpy as jnp
from jax import lax
from jax.experimental import pallas as pl
from jax.experimental.pallas import tpu as pltpu

F32 = jnp.float32
BF16 = jnp.bfloat16
MESH = pl.DeviceIdType.MESH

T = 2048
D = 1024
D_IN = 2816
N_DEV = 8
WIN_BLK = D_IN // N_DEV
WOUT_BLK = D // N_DEV
CH = 128
N_CH = T // CH
RET_H, RET_DK, RET_DV = 4, 64, 128
RET_QK = RET_H * RET_DK
RET_V = RET_H * RET_DV
SWA_H, SWA_KV, SWA_D, SWA_G = 8, 2, 64, 4
N_BUCKETS = 32
NORM_EPS = 1e-6
GN_EPS = 1e-5
NEG_INF = -1e30
PIECES = (256, 256, 512, 512, 512, 128, 128, 512)
OFFS = tuple(int(v) for v in np.cumsum((0,) + PIECES))
RET_W = OFFS[4]
SWA_W = D_IN - RET_W
SWA_OFFS = tuple(o - RET_W for o in OFFS[4:])
TM = 256

ADAM_LR, ADAM_B1, ADAM_B2, ADAM_EPS, ADAM_WD, ADAM_STEP = 0.001, 0.9, 0.999, 1e-08, 0.01, 10

VMEM_LIMIT = 56 * 1024 * 1024
VMEM_WHOLE = 63 * 1024 * 1024


def _cparams(n_grid=0, vmem=VMEM_LIMIT):
    sem = ("arbitrary",) * n_grid if n_grid else None
    return pltpu.CompilerParams(dimension_semantics=sem, vmem_limit_bytes=vmem)


class _Part(NamedTuple):
    step: Callable
    inputs: list
    in_specs: list
    out_specs: list
    out_shape: list
    scratch_shapes: list


def _fused_call(name, n_steps, parts):
    n_in = [len(p.inputs) for p in parts]
    n_out = [len(p.out_shape) for p in parts]
    n_scr = [len(p.scratch_shapes) for p in parts]

    def body(*refs):
        ins, outs, scr = refs[:sum(n_in)], refs[sum(n_in):sum(n_in) + sum(n_out)], refs[sum(n_in) + sum(n_out):]
        for i, p in enumerate(parts):
            take = lambda seq, counts: seq[sum(counts[:i]):sum(counts[:i + 1])]
            p.step(*take(ins, n_in), *take(outs, n_out), *take(scr, n_scr))

    flat = lambda field: [v for p in parts for v in getattr(p, field)]
    outs = pl.pallas_call(
        body, name=name, grid=(n_steps,), in_specs=flat("in_specs"), out_specs=tuple(flat("out_specs")),
        out_shape=tuple(flat("out_shape")), scratch_shapes=flat("scratch_shapes"), compiler_params=_cparams(1),
    )(*flat("inputs"))
    return [list(outs[sum(n_out[:i]):sum(n_out[:i + 1])]) for i in range(len(parts))]


def _dot(a, b):
    return jnp.dot(a, b, preferred_element_type=F32)


def _dot_nt(a, b):
    return lax.dot_general(a, b, (((1,), (1,)), ((), ())), preferred_element_type=F32)


def _bf(a):
    return a.astype(BF16)


def _sigmoid(x):
    return 1.0 / (1.0 + jnp.exp(-x))


def _adamw(w, g, m, v):
    m = ADAM_B1 * m + (1.0 - ADAM_B1) * g
    v = ADAM_B2 * v + (1.0 - ADAM_B2) * (g * g)
    m_hat = m / (1.0 - ADAM_B1 ** ADAM_STEP)
    v_hat = v / (1.0 - ADAM_B2 ** ADAM_STEP)
    delta = -ADAM_LR * (m_hat / (jnp.sqrt(v_hat) + ADAM_EPS) + ADAM_WD * w)
    return delta, m, v


def _rotary_tables_t():
    half = RET_DK // 2
    inv_freq = np.float32(10000.0) ** (-np.arange(half, dtype=np.float32) / np.float32(half))
    ang = inv_freq[:, None] * np.arange(T, dtype=np.float32)[None, :]
    cos, sin = np.cos(ang).astype(np.float32), np.sin(ang).astype(np.float32)
    cos64 = np.concatenate([cos, cos], axis=0)
    sin64 = np.concatenate([-sin, sin], axis=0)
    return np.tile(cos64, (RET_H, 1)), np.tile(sin64, (RET_H, 1))


def _retention_tables_t():
    gamma = (1.0 - np.exp2(-5.0 - np.arange(RET_H, dtype=np.float32))).astype(np.float32)
    log_g = np.log(gamma).astype(np.float32)
    i = np.arange(CH, dtype=np.float32)
    diff = i[None, :] - i[:, None]
    decay = np.where(diff >= 0, np.exp(log_g[:, None, None] * np.maximum(diff, 0.0)), 0.0).astype(np.float32)
    decay_all = np.concatenate(list(decay), axis=1)
    zeta = np.exp(log_g[:, None] * (CH - 1.0 - i)).astype(np.float32)
    zeta_tab = np.repeat(zeta.T, RET_DK, axis=1)
    xi = np.exp(log_g[:, None] * (i + 1.0)).astype(np.float32)
    xi_tab = np.repeat(xi, RET_DK, axis=0)
    chunk_decay = np.exp(log_g * np.float32(CH)).astype(np.float32)
    row_head = np.arange(RET_V)[:, None] // RET_DV
    col_head = np.arange(RET_QK)[None, :] // RET_DK
    state_mask = (row_head == col_head).astype(np.float32)
    state_decay = (state_mask * chunk_decay[row_head]).astype(np.float32)
    q_mask = (np.arange(RET_QK)[:, None] // RET_DK == np.arange(RET_H * CH)[None, :] // CH).astype(np.float32)
    return (np.ascontiguousarray(decay_all), np.ascontiguousarray(zeta_tab), np.ascontiguousarray(xi_tab),
            state_mask, state_decay, q_mask)


def _bucket_table_t():
    qi = np.arange(CH)[:, None]
    kj = np.arange(2 * CH)[None, :]
    dist = qi + CH - kj
    n = np.maximum(dist, 0)
    max_exact = N_BUCKETS // 2
    nf = np.maximum(n, 1).astype(np.float32)
    large = max_exact + (np.log(nf / np.float32(max_exact)) / np.float32(np.log(CH / max_exact))
                         * np.float32(N_BUCKETS - max_exact)).astype(np.int32)
    large = np.minimum(large, N_BUCKETS - 1)
    bucket = np.where(n < max_exact, n, large)
    return np.ascontiguousarray(np.where((dist >= 0) & (dist < CH), bucket, -1).astype(np.int32).T)


def _mesh_pos():
    return lax.axis_index("x"), lax.axis_index("y"), lax.axis_index("c")


def _blk(px, py, pc):
    return 4 * px + 2 * py + pc


def _in_proj_gather(x, norm_w, w_in_t):
    tchunk = 512

    def body(x_any, nw_any, win_any, proj_ref, wt_out, h_out, wt_ref, h_ref, stage, x_ref, nw_ref, win_ref,
             send_sems, recv_sems, out_sems, wb_sems, h_sem, in_sems):
        n_tiles = T // TM
        loads = [pltpu.make_async_copy(win_any, win_ref, in_sems.at[0]),
                 pltpu.make_async_copy(nw_any, nw_ref, in_sems.at[1])]
        loads += [pltpu.make_async_copy(x_any.at[pl.ds(i * TM, TM), :], x_ref.at[pl.ds(i * TM, TM), :], in_sems.at[2 + i])
                  for i in range(n_tiles)]
        for cp in loads:
            cp.start()
        loads[0].wait()
        x, y, c = _mesh_pos()
        me = _blk(x, y, c)
        sibling = (x, y, 1 - c)
        nbr_a, nbr_b, diag = (x ^ (1 - c), y ^ c), (x ^ c, y ^ (1 - c)), (1 - x, 1 - y)

        def copy(k, b, to):
            return pltpu.make_async_remote_copy(src_ref=wt_ref.at[b], dst_ref=wt_ref.at[b], send_sem=send_sems.at[k],
                                                recv_sem=recv_sems.at[k], device_id=to, device_id_type=MESH)

        wt_ref[me] = _bf(win_ref[...])
        first = [copy(0, me, sibling), copy(1, me, (*nbr_a, c))]
        for cp in first:
            cp.start()
        to_b = copy(2, me, (*nbr_b, c))

        loads[1].wait()
        nw = nw_ref[...]
        for i in range(n_tiles):
            rows = slice(i * TM, (i + 1) * TM)
            loads[2 + i].wait()
            xv = x_ref[rows, :]
            r = lax.rsqrt(jnp.mean(xv * xv, axis=-1, keepdims=True) + NORM_EPS)
            h_ref[rows, :] = _bf(xv * r * nw)
        keep = [pltpu.make_async_copy(h_ref, h_out, h_sem)]
        keep[0].start()

        writes = []

        def project(b):
            k = len(writes)
            if k >= 2:
                writes[k - 2].wait()
            keep.append(pltpu.make_async_copy(wt_ref.at[b], wt_out.at[b], wb_sems.at[k]))
            keep[-1].start()
            w = wt_ref[b]
            for t in range(T // tchunk):
                cols = slice(t * tchunk, (t + 1) * tchunk)
                stage[k % 2, :, cols] = _dot_nt(w, h_ref[cols, :])
            cp = pltpu.make_async_copy(stage.at[k % 2], proj_ref.at[b], out_sems.at[k % 2])
            cp.start()
            writes.append(cp)

        here = (x, y, c)
        project(me)
        copy(0, _blk(x, y, 1 - c), here).wait_recv()
        project(_blk(x, y, 1 - c))
        to_b.start()
        passed = [to_b]

        def landed(k, chip, relay_to=None):
            copy(k, _blk(*chip, c), here).wait_recv()
            out = [copy({1: 4, 2: 5, 3: 6}[k], _blk(*chip, c), sibling)]
            if relay_to is not None:
                out.append(copy(3, _blk(*chip, c), (*relay_to, c)))
            for cp in out:
                cp.start()
            passed.extend(out)
            project(_blk(*chip, c))

        def from_sibling(k, chip):
            copy(k, _blk(*chip, 1 - c), here).wait_recv()
            project(_blk(*chip, 1 - c))

        landed(1, nbr_a, relay_to=nbr_b)
        from_sibling(4, nbr_b)
        landed(2, nbr_b)
        from_sibling(5, nbr_a)
        landed(3, diag)
        from_sibling(6, diag)
        for cp in writes[-2:] + keep:
            cp.wait()
        for cp in first + passed:
            cp.wait_send()

    vm = pl.BlockSpec(memory_space=pltpu.VMEM)
    hbm = pl.BlockSpec(memory_space=pltpu.HBM)
    dma = pltpu.SemaphoreType.DMA
    proj, wt, h = pl.pallas_call(
        body, name="in_proj_gather",
        out_shape=(jax.ShapeDtypeStruct((N_DEV, WIN_BLK, T), F32), jax.ShapeDtypeStruct((N_DEV, WIN_BLK, D), BF16),
                   jax.ShapeDtypeStruct((T, D), BF16)),
        in_specs=[hbm, hbm, hbm], out_specs=(hbm, hbm, hbm),
        scratch_shapes=[pltpu.VMEM((N_DEV, WIN_BLK, D), BF16), pltpu.VMEM((T, D), BF16), pltpu.VMEM((2, WIN_BLK, T), F32),
                        pltpu.VMEM((T, D), F32), pltpu.VMEM((1, D), F32), pltpu.VMEM((WIN_BLK, D), F32),
                        dma((7,)), dma((7,)), dma((2,)), dma((N_DEV,)), dma, dma((2 + T // TM,))],
        compiler_params=_cparams(vmem=VMEM_WHOLE),
    )(x, norm_w, w_in_t)
    return proj.reshape(D_IN, T), wt.reshape(D_IN, D), h


def _gather_w_out(w_out_ref, own_ref, wo_ref, local_sem, send_sems, recv_sems, phase):
    x, y, c = _mesh_pos()
    me = _blk(x, y, c)
    sibling = (x, y, 1 - c)
    chips = [(1 - x, y), (x, 1 - y), (1 - x, 1 - y)]

    def copy(k, b, to, src=None):
        return pltpu.make_async_remote_copy(src_ref=wo_ref.at[b] if src is None else src, dst_ref=wo_ref.at[b],
                                            send_sem=send_sems.at[k], recv_sem=recv_sems.at[k], device_id=to,
                                            device_id_type=MESH)

    local = pltpu.make_async_copy(own_ref, wo_ref.at[me], local_sem)
    first = [copy(0, me, sibling, own_ref)] + [copy(1 + j, me, (*chip, c), own_ref) for j, chip in enumerate(chips)]
    passed = [copy(4 + j, _blk(*chip, c), sibling) for j, chip in enumerate(chips)]
    if phase == 0:
        own_ref[...] = _bf(w_out_ref[...])
        local.start()
        for cp in first:
            cp.start()
    elif phase == 1:
        for j, chip in enumerate(chips):
            copy(1 + j, _blk(*chip, c), (x, y, c)).wait_recv()
            passed[j].start()
    else:
        local.wait()
        copy(0, _blk(x, y, 1 - c), (x, y, c)).wait_recv()
        for j, chip in enumerate(chips):
            copy(4 + j, _blk(*chip, 1 - c), (x, y, c)).wait_recv()
        for cp in first + passed:
            cp.wait_send()


def _proj_piece(i, block_of):
    rows = PIECES[i]
    assert OFFS[i] % rows == 0
    return pl.BlockSpec((rows, CH), lambda n: (OFFS[i] // rows, block_of(n)))


def _proj_gate_halves(block_of):
    return [pl.BlockSpec((256, CH), lambda n, j=j: (OFFS[7] // 256 + j, block_of(n))) for j in range(2)]


def _swap_halves_t(t):
    half = RET_DK // 2
    parts = []
    for h in range(RET_H):
        parts += [t[h * RET_DK + half:(h + 1) * RET_DK], t[h * RET_DK:h * RET_DK + half]]
    return jnp.concatenate(parts, axis=0)


def _rotate(t, cos, sin):
    return t * cos + _swap_halves_t(t) * sin


def _group_norm_t(o):
    mu = jnp.mean(o, axis=0, keepdims=True)
    var = jnp.mean((o - mu) * (o - mu), axis=0, keepdims=True)
    rstd = lax.rsqrt(var + GN_EPS)
    return (o - mu) * rstd, rstd


def _retention_scores_t(q_t, k_rm_b, q_mask, decay_all):
    q_heads = _bf(jnp.concatenate([q_t] * RET_H, axis=1) * q_mask)
    return _dot(k_rm_b, q_heads) * decay_all


def _row_to_col(row):
    n = row.shape[1]
    diagonal = lax.broadcasted_iota(jnp.int32, (n, n), 0) == lax.broadcasted_iota(jnp.int32, (n, n), 1)
    return jnp.sum(jnp.where(diagonal, row, 0.0), axis=1, keepdims=True)


def _retention_fwd(proj, ret_norm_w, tables, w_out_blk):
    cos_t, sin_t, decay_all, zeta_tab, xi_tab, state_mask, state_decay, q_mask = tables

    def body(rq_ref, rk_ref, rv_ref, rg_ref, w_row, cos_ref, sin_ref, decay_ref, zeta_ref, xi_ref, smask_ref,
             sdecay_ref, qmask_ref, wout_ref, ro_ref, mix_ref, st_ref, wo_ref, w_ref,
             state, wo_own, wo_local, wo_send, wo_recv):
        n = pl.program_id(0)

        @pl.when(n == 0)
        def _():
            w_ref[...] = _row_to_col(w_row[...])
            state[...] = jnp.zeros_like(state)
            _gather_w_out(wout_ref, wo_own, wo_ref, wo_local, wo_send, wo_recv, phase=0)

        @pl.when(n == N_CH // 2)
        def _():
            _gather_w_out(wout_ref, wo_own, wo_ref, wo_local, wo_send, wo_recv, phase=1)

        @pl.when(n == N_CH - 1)
        def _():
            _gather_w_out(wout_ref, wo_own, wo_ref, wo_local, wo_send, wo_recv, phase=2)

        cos, sin = cos_ref[...], sin_ref[...]
        q_t = _rotate(rq_ref[...], cos, sin)
        k_t = _rotate(rk_ref[...], cos, sin) * (RET_DK ** -0.5)
        k_rm = k_t.T
        v_b = _bf(rv_ref[...])
        m_b = _bf(state[...])
        st_ref[0] = m_b
        scores_b = _bf(_retention_scores_t(q_t, _bf(k_rm), qmask_ref[...], decay_ref[...]))
        cross = _dot(m_b, _bf(q_t * xi_ref[...]))
        state[...] = state[...] * sdecay_ref[...] + _dot(v_b, _bf(k_rm * zeta_ref[...])) * smask_ref[...]
        heads = range(RET_H)
        rows = [slice(h * RET_DV, (h + 1) * RET_DV) for h in heads]
        o = [_dot(v_b[rows[h]], scores_b[:, h * CH:(h + 1) * CH]) + cross[rows[h]] for h in heads]
        for h in heads:
            rn, _ = _group_norm_t(o[h])
            g = rg_ref[rows[h], :]
            ro_ref[rows[h], :] = o[h]
            mix_ref[rows[h], :] = rn * w_ref[rows[h], :] * (g * _sigmoid(g))

    col = lambda w: pl.BlockSpec((w, CH), lambda n: (0, n))
    const = lambda shape: pl.BlockSpec(shape, lambda n: (0,) * len(shape))
    cur = lambda n: n
    return _Part(
        body,
        inputs=[proj, proj, proj, proj, ret_norm_w, cos_t, sin_t, decay_all, zeta_tab, xi_tab, state_mask, state_decay,
                q_mask, w_out_blk],
        in_specs=[_proj_piece(i, cur) for i in range(4)] + [
            const((1, RET_V)), col(RET_QK), col(RET_QK),
            const(decay_all.shape), const(zeta_tab.shape), const(xi_tab.shape), const(state_mask.shape),
            const(state_decay.shape), const(q_mask.shape), const((WOUT_BLK, D))],
        out_specs=[col(RET_V), col(RET_V), pl.BlockSpec((1, RET_V, RET_QK), lambda n: (n, 0, 0)),
                   pl.BlockSpec(memory_space=pl.ANY), const((RET_V, 1))],
        out_shape=[jax.ShapeDtypeStruct((RET_V, T), F32), jax.ShapeDtypeStruct((RET_V, T), F32),
                   jax.ShapeDtypeStruct((N_CH, RET_V, RET_QK), BF16), jax.ShapeDtypeStruct((N_DEV, WOUT_BLK, D), BF16),
                   jax.ShapeDtypeStruct((RET_V, 1), F32)],
        scratch_shapes=[pltpu.VMEM((RET_V, RET_QK), F32), pltpu.VMEM((WOUT_BLK, D), BF16), pltpu.SemaphoreType.DMA,
                        pltpu.SemaphoreType.DMA((N_DEV - 1,)), pltpu.SemaphoreType.DMA((N_DEV - 1,))])


def _retention_bwd(proj, ro, states, dmix, ret_norm_w, tables):
    cos_t, sin_t, decay_all, zeta_tab, xi_tab, state_mask, state_decay, q_mask = tables

    def body(*refs):
        @pl.when(pl.program_id(0) < N_CH)
        def _():
            chunk(*refs)

    def chunk(rq_ref, rk_ref, rv_ref, rg_ref, ro_ref, st_ref, dm_ref, w_ref, cos_ref, sin_ref, decay_ref, zeta_ref,
              xi_ref, smask_ref, sdecay_ref, qmask_ref, d_ref, dw_ref, gstate):
        i = pl.program_id(0)

        @pl.when(i == 0)
        def _():
            gstate[...] = jnp.zeros_like(gstate)
            dw_ref[...] = jnp.zeros_like(dw_ref)

        cos, sin = cos_ref[...], sin_ref[...]
        q_t = _rotate(rq_ref[...], cos, sin)
        k_t = _rotate(rk_ref[...], cos, sin) * (RET_DK ** -0.5)
        q_b, k_b = _bf(q_t), _bf(k_t)
        k_rm = k_t.T
        kz_b = _bf(k_rm * zeta_ref[...])
        qx_b = _bf(q_t * xi_ref[...])
        v_t = rv_ref[...]
        v_b = _bf(v_t)
        v_rm_b = _bf(v_t.T)
        decay = decay_ref[...]
        scores_b = _bf(_retention_scores_t(q_t, _bf(k_rm), qmask_ref[...], decay))
        heads = range(RET_H)
        rows = [slice(h * RET_DV, (h + 1) * RET_DV) for h in heads]
        qk_rows = [slice(h * RET_DK, (h + 1) * RET_DK) for h in heads]
        lanes = [slice(h * CH, (h + 1) * CH) for h in heads]
        do = []
        for h in heads:
            g, w, dm = rg_ref[rows[h], :], w_ref[rows[h], :], dm_ref[rows[h], :]
            rn, rstd = _group_norm_t(ro_ref[rows[h], :])
            sig = _sigmoid(g)
            silu = g * sig
            d_ref[OFFS[3] + h * RET_DV:OFFS[3] + (h + 1) * RET_DV, :] = _bf(dm * rn * w * (sig * (1.0 + g * (1.0 - sig))))
            dw_ref[rows[h], :] += dm * silu * rn
            drn = dm * silu * w
            do.append(rstd * (drn - jnp.mean(drn, axis=0, keepdims=True)
                              - rn * jnp.mean(drn * rn, axis=0, keepdims=True)))
        do_b = _bf(jnp.concatenate(do, axis=0))
        m_b = st_ref[0]
        g_all = gstate[...]
        g_b = _bf(g_all)
        dscores_b = [_bf(_dot(v_rm_b[:, rows[h]], do_b[rows[h]]) * decay[:, lanes[h]]) for h in heads]
        dq_cross = lax.dot_general(m_b, do_b, (((0,), (0,)), ((), ())), preferred_element_type=F32)
        dkz = _dot(v_rm_b, g_b)
        dv_cross = _dot_nt(g_b, kz_b)
        gstate[...] = g_all * sdecay_ref[...] + _dot_nt(do_b, qx_b) * smask_ref[...]
        dq = jnp.concatenate([_dot(k_b[qk_rows[h]], dscores_b[h]) for h in heads], axis=0) + dq_cross * xi_ref[...]
        dk = (jnp.concatenate([_dot_nt(q_b[qk_rows[h]], dscores_b[h]) for h in heads], axis=0)
              + (dkz * zeta_ref[...]).T) * (RET_DK ** -0.5)
        for h in heads:
            d_ref[OFFS[2] + h * RET_DV:OFFS[2] + (h + 1) * RET_DV, :] = _bf(
                _dot_nt(do_b[rows[h]], scores_b[:, lanes[h]]) + dv_cross[rows[h]])
        d_ref[OFFS[0]:OFFS[1], :] = _bf(dq * cos + _swap_halves_t(dq * sin))
        d_ref[OFFS[1]:OFFS[2], :] = _bf(dk * cos + _swap_halves_t(dk * sin))

    chunk_of = lambda i: N_CH - 1 - jnp.minimum(i, N_CH - 1)
    col = lambda w: pl.BlockSpec((w, CH), lambda i: (0, chunk_of(i)))
    const = lambda shape: pl.BlockSpec(shape, lambda i: (0,) * len(shape))
    return _Part(
        body,
        inputs=[proj, proj, proj, proj, ro, states, dmix, ret_norm_w, cos_t, sin_t, decay_all, zeta_tab, xi_tab,
                state_mask, state_decay, q_mask],
        in_specs=[_proj_piece(j, chunk_of) for j in range(4)] + [
                  col(RET_V),
                  pl.BlockSpec((1, RET_V, RET_QK), lambda i: (chunk_of(i), 0, 0)), col(RET_V),
                  const((RET_V, 1)), col(RET_QK), col(RET_QK),
                  const(decay_all.shape), const(zeta_tab.shape), const(xi_tab.shape), const(state_mask.shape),
                  const(state_decay.shape), const(q_mask.shape)],
        out_specs=[col(RET_W), const((RET_V, CH))],
        out_shape=[jax.ShapeDtypeStruct((RET_W, T), BF16), jax.ShapeDtypeStruct((RET_V, CH), F32)],
        scratch_shapes=[pltpu.VMEM((RET_V, RET_QK), F32)])


HQ_LANES = SWA_H * CH


def _head_lanes(hq):
    return slice(hq * CH, (hq + 1) * CH)


def _build_bias_t(bucket_ref, relb_ref, sinks_ref, bias_ref, sink_row):
    bt = bucket_ref[...]
    first = lax.broadcasted_iota(jnp.int32, bt.shape, 0) < CH
    for hq in range(SWA_H):
        b = jnp.full(bt.shape, NEG_INF, F32)
        for bk in range(N_BUCKETS):
            b = jnp.where(bt == bk, relb_ref[hq, bk], b)
        bias_ref[0, :, _head_lanes(hq)] = b
        bias_ref[1, :, _head_lanes(hq)] = jnp.where(first, NEG_INF, b)
        sink_row[:, _head_lanes(hq)] = jnp.full((1, CH), sinks_ref[hq], F32)


def _rms_t(t, w_col):
    r = lax.rsqrt(jnp.mean(t * t, axis=0, keepdims=True) + NORM_EPS)
    return t * r * w_col, r


def _rms_t_bwd(dn, t, r, w_col):
    u = dn * w_col
    return r * u - t * (r * r * r) * jnp.mean(u * t, axis=0, keepdims=True), dn * t * r


def _norm_kv(k_t, kw):
    return jnp.concatenate([_rms_t(k_t[g * SWA_D:(g + 1) * SWA_D], kw)[0] for g in range(SWA_KV)], axis=0)


def _kv_slot(a, kv):
    z = jnp.zeros_like(a)
    return jnp.concatenate([a, z] if kv == 0 else [z, a], axis=0)


def _softmax_t(s, sink):
    m = jnp.maximum(jnp.max(s, axis=0, keepdims=True), sink)
    p = jnp.exp(s - m)
    e_sink = jnp.exp(sink - m)
    inv = 1.0 / (jnp.sum(p, axis=0, keepdims=True) + e_sink)
    return p * inv, e_sink * inv


def _gate_rows(sg_halves, hq):
    per_half = SWA_H // 2
    return sg_halves[hq // per_half][(hq % per_half) * SWA_D:(hq % per_half + 1) * SWA_D, :]


def _swa_fwd(proj, q_norm_w, k_norm_w, sinks, rel_bias_t, bucket_t):
    def body(sq_ref, skp_ref, skc_ref, svp_ref, svc_ref, sg_lo, sg_hi, qw_row, kw_row, sinks_ref, relb_ref, bucket_ref,
             so_ref, mix_ref, qw_ref, kw_ref, bias_ref, sink_row):
        n = pl.program_id(0)

        @pl.when(n == 0)
        def _():
            qw_ref[...] = _row_to_col(qw_row[...])
            kw_ref[...] = _row_to_col(kw_row[...])
            _build_bias_t(bucket_ref, relb_ref, sinks_ref, bias_ref, sink_row)

        var = (n == 0).astype(jnp.int32)
        qw, kw = qw_ref[...], kw_ref[...]
        kn_band = jnp.concatenate([_norm_kv(skp_ref[...], kw), _norm_kv(skc_ref[...], kw)], axis=1)
        kb_rm = _bf(kn_band.T)
        vband = _bf(jnp.concatenate([svp_ref[...], svc_ref[...]], axis=1))
        q_all = jnp.concatenate(
            [_bf(_kv_slot(_rms_t(sq_ref[hq * SWA_D:(hq + 1) * SWA_D, :], qw)[0] * (SWA_D ** -0.5), hq // SWA_G))
             for hq in range(SWA_H)], axis=1)
        probs, _ = _softmax_t(_dot(kb_rm, q_all) + bias_ref[var], sink_row[...])
        probs_b = _bf(probs)
        for kv in range(SWA_KV):
            o = _dot(vband[kv * SWA_D:(kv + 1) * SWA_D], probs_b[:, kv * SWA_G * CH:(kv + 1) * SWA_G * CH])
            for j in range(SWA_G):
                rows = slice((kv * SWA_G + j) * SWA_D, (kv * SWA_G + j + 1) * SWA_D)
                oh = o[:, j * CH:(j + 1) * CH]
                g = _gate_rows((sg_lo, sg_hi), kv * SWA_G + j)
                so_ref[rows, :] = oh
                mix_ref[rows, :] = oh * (g * _sigmoid(g))

    col = lambda w: pl.BlockSpec((w, CH), lambda n: (0, n))
    const = lambda shape: pl.BlockSpec(shape, lambda n: (0,) * len(shape))
    smem = pl.BlockSpec(memory_space=pltpu.SMEM)
    cur = lambda n: n
    prev = lambda n: jnp.maximum(n - 1, 0)
    return _Part(
        body,
        inputs=[proj, proj, proj, proj, proj, proj, proj, q_norm_w, k_norm_w, sinks, rel_bias_t, bucket_t],
        in_specs=[_proj_piece(4, cur), _proj_piece(5, prev), _proj_piece(5, cur), _proj_piece(6, prev),
                  _proj_piece(6, cur)] + _proj_gate_halves(cur) + [
                  const((1, SWA_D)), const((1, SWA_D)), smem, smem, const((2 * CH, CH))],
        out_specs=[col(512), col(512), const((SWA_D, 1)), const((SWA_D, 1))],
        out_shape=[jax.ShapeDtypeStruct((512, T), F32), jax.ShapeDtypeStruct((512, T), F32),
                   jax.ShapeDtypeStruct((SWA_D, 1), F32), jax.ShapeDtypeStruct((SWA_D, 1), F32)],
        scratch_shapes=[pltpu.VMEM((2, 2 * CH, HQ_LANES), F32), pltpu.VMEM((1, HQ_LANES), F32)])


def _swa_bwd(proj, so, dmix, q_norm_w, k_norm_w, sinks, rel_bias_t, bucket_t):
    def body(sq_ref, skp_ref, skc_ref, svp_ref, svc_ref, sg_lo, sg_hi, so_ref, dm_ref, qw_ref, kw_ref, sinks_ref,
             relb_ref, bucket_ref, d_ref, dbias_ref, dsink_ref, dqw_ref, dkw_ref,
             bias_ref, sink_row, band_dk, band_dv, carry_dk, carry_dv, hold_dq, hold_dg):
        n = pl.program_id(0)
        slot = n % 2

        @pl.when(n == 0)
        def _():
            _build_bias_t(bucket_ref, relb_ref, sinks_ref, bias_ref, sink_row)
            for ref in (dbias_ref, dsink_ref, dqw_ref, dkw_ref, carry_dk, carry_dv):
                ref[...] = jnp.zeros_like(ref)

        qw, kw = qw_ref[...], kw_ref[...]

        @pl.when(n < N_CH)
        def _():
            var = (n == 0).astype(jnp.int32)
            kn_band = jnp.concatenate([_norm_kv(skp_ref[...], kw), _norm_kv(skc_ref[...], kw)], axis=1)
            kb_rm = _bf(kn_band.T)
            kn_band_s = _bf(kn_band * (SWA_D ** -0.5))
            vband_f = jnp.concatenate([svp_ref[...], svc_ref[...]], axis=1)
            vb_rm = _bf(vband_f.T)
            q_raw, q_rstd, qs_b, do_b = [], [], [], []
            for hq in range(SWA_H):
                rows = slice(hq * SWA_D, (hq + 1) * SWA_D)
                q_t = sq_ref[rows, :]
                qn, rq = _rms_t(q_t, qw)
                g = _gate_rows((sg_lo, sg_hi), hq)
                sig = _sigmoid(g)
                dm = dm_ref[rows, :]
                hold_dg[slot, rows, :] = _bf(dm * so_ref[rows, :] * (sig * (1.0 + g * (1.0 - sig))))
                q_raw.append(q_t)
                q_rstd.append(rq)
                qs_b.append(_bf(qn * (SWA_D ** -0.5)))
                do_b.append(_bf(dm * (g * sig)))
            q_all = jnp.concatenate([_kv_slot(qs_b[hq], hq // SWA_G) for hq in range(SWA_H)], axis=1)
            do_all = jnp.concatenate([_kv_slot(do_b[hq], hq // SWA_G) for hq in range(SWA_H)], axis=1)
            probs, p_sink = _softmax_t(_dot(kb_rm, q_all) + bias_ref[var], sink_row[...])
            dprobs = _dot(vb_rm, do_all)
            t = jnp.sum(probs * dprobs, axis=0, keepdims=True)
            dlog = probs * (dprobs - t)
            dsink_ref[...] += -(p_sink * t)
            dbias_ref[...] += dlog
            dlog_b, probs_b = _bf(dlog), _bf(probs)
            dkn, dvv = [], []
            for kv in range(SWA_KV):
                heads = range(kv * SWA_G, (kv + 1) * SWA_G)
                lanes = slice(kv * SWA_G * CH, (kv + 1) * SWA_G * CH)
                dvv.append(_dot_nt(jnp.concatenate([do_b[hq] for hq in heads], axis=1), probs_b[:, lanes]))
                dkn.append(_dot_nt(jnp.concatenate([qs_b[hq] for hq in heads], axis=1), dlog_b[:, lanes]))
                dqn = _dot(kn_band_s[kv * SWA_D:(kv + 1) * SWA_D], dlog_b[:, lanes])
                for j, hq in enumerate(heads):
                    dq_t, dqw_terms = _rms_t_bwd(dqn[:, j * CH:(j + 1) * CH], q_raw[hq], q_rstd[hq], qw)
                    hold_dq[slot, hq * SWA_D:(hq + 1) * SWA_D, :] = _bf(dq_t)
                    dqw_ref[...] += dqw_terms
            band_dk[...] = jnp.concatenate(dkn, axis=0)
            band_dv[...] = jnp.concatenate(dvv, axis=0)

        @pl.when(n == N_CH)
        def _():
            band_dk[...] = jnp.zeros_like(band_dk)
            band_dv[...] = jnp.zeros_like(band_dv)

        @pl.when(n >= 1)
        def _():
            dkn_prev = carry_dk[...] + band_dk[:, 0:CH]
            k_t = skp_ref[...]
            for kv in range(SWA_KV):
                rows = slice(kv * SWA_D, (kv + 1) * SWA_D)
                _, rk = _rms_t(k_t[rows], kw)
                dk_t, dkw_terms = _rms_t_bwd(dkn_prev[rows], k_t[rows], rk, kw)
                d_ref[SWA_OFFS[1] + kv * SWA_D:SWA_OFFS[1] + (kv + 1) * SWA_D, :] = _bf(dk_t)
                dkw_ref[...] += dkw_terms
            d_ref[SWA_OFFS[2]:SWA_OFFS[3], :] = _bf(carry_dv[...] + band_dv[:, 0:CH])
            d_ref[SWA_OFFS[0]:SWA_OFFS[1], :] = hold_dq[1 - slot]
            d_ref[SWA_OFFS[3]:SWA_OFFS[4], :] = hold_dg[1 - slot]

        carry_dk[...] = band_dk[:, CH:2 * CH]
        carry_dv[...] = band_dv[:, CH:2 * CH]

    cur_block = lambda n: jnp.minimum(n, N_CH - 1)
    prev_block = lambda n: jnp.maximum(n - 1, 0)
    col = lambda w: pl.BlockSpec((w, CH), lambda n: (0, cur_block(n)))
    prev = lambda w: pl.BlockSpec((w, CH), lambda n: (0, prev_block(n)))
    const = lambda shape: pl.BlockSpec(shape, lambda n: (0,) * len(shape))
    smem = pl.BlockSpec(memory_space=pltpu.SMEM)
    return _Part(
        body,
        inputs=[proj, proj, proj, proj, proj, proj, proj, so, dmix, q_norm_w, k_norm_w, sinks, rel_bias_t, bucket_t],
        in_specs=[_proj_piece(4, cur_block), _proj_piece(5, prev_block), _proj_piece(5, cur_block),
                  _proj_piece(6, prev_block), _proj_piece(6, cur_block)] + _proj_gate_halves(cur_block) + [
                  col(512), col(512), const((SWA_D, 1)), const((SWA_D, 1)), smem, smem, const((2 * CH, CH))],
        out_specs=[prev(SWA_W), const((2 * CH, HQ_LANES)), const((1, HQ_LANES)),
                   const((SWA_D, CH)), const((SWA_D, CH))],
        out_shape=[jax.ShapeDtypeStruct((SWA_W, T), BF16),
                   jax.ShapeDtypeStruct((2 * CH, HQ_LANES), F32), jax.ShapeDtypeStruct((1, HQ_LANES), F32),
                   jax.ShapeDtypeStruct((SWA_D, CH), F32), jax.ShapeDtypeStruct((SWA_D, CH), F32)],
        scratch_shapes=[pltpu.VMEM((2, 2 * CH, HQ_LANES), F32), pltpu.VMEM((1, HQ_LANES), F32),
                        pltpu.VMEM((128, 2 * CH), F32), pltpu.VMEM((128, 2 * CH), F32),
                        pltpu.VMEM((128, CH), F32), pltpu.VMEM((128, CH), F32),
                        pltpu.VMEM((2, 512, CH), BF16), pltpu.VMEM((2, 512, CH), BF16)])


SMALL_GRAD_SHAPES = ((1, RET_V), (1, CH), (1, CH), (1, CH), (SWA_H, N_BUCKETS))


def _finish_small_grads(dbias_ref, bucket_ref, dsink_ref, dqw_ref, dkw_ref, drw_ref, rw_o, qw_o, kw_o, sink_o, relb_o):
    bt = bucket_ref[...]
    row = lax.broadcasted_iota(jnp.int32, (SWA_H, N_BUCKETS), 0)
    col = lax.broadcasted_iota(jnp.int32, (SWA_H, N_BUCKETS), 1)
    lane = lax.broadcasted_iota(jnp.int32, (1, CH), 1)
    acc = jnp.zeros((SWA_H, N_BUCKETS), F32)
    sink = jnp.zeros((1, CH), F32)
    for hq in range(SWA_H):
        d = dbias_ref[:, _head_lanes(hq)]
        for bk in range(N_BUCKETS):
            s = jnp.sum(jnp.sum(jnp.where(bt == bk, d, 0.0), axis=0, keepdims=True), axis=1, keepdims=True)
            acc = acc + jnp.where((row == hq) & (col == bk), s, 0.0)
        sink = sink + jnp.where(lane == hq, jnp.sum(dsink_ref[:, _head_lanes(hq)], axis=1, keepdims=True), 0.0)
    relb_o[...] = acc
    sink_o[...] = sink
    for src, dst in ((dqw_ref, qw_o), (dkw_ref, kw_o)):
        padded = jnp.concatenate([src[...], jnp.zeros((CH - SWA_D, CH), F32)], axis=0)
        dst[...] = jnp.sum(padded.T, axis=0, keepdims=True)
    rw_o[...] = jnp.sum(drw_ref[...].T, axis=0, keepdims=True)


def _out_proj(mix_r, mix_s, wo, x, target):
    def body(mr_ref, ms_ref, w_ref, x_ref, t_ref, loss_ref, dy_ref, dmr_ref, dms_ref, gw_ref):
        i = pl.program_id(0)

        @pl.when(i == 0)
        def _():
            loss_ref[...] = jnp.zeros_like(loss_ref)
            gw_ref[...] = jnp.zeros_like(gw_ref)

        mixed = jnp.concatenate([mr_ref[...], ms_ref[...]], axis=0)
        w = w_ref[...]
        err = x_ref[...] + _dot(_bf(mixed.T), w) - t_ref[...]
        loss_ref[...] += jnp.sum(jnp.sum(err * err, axis=1, keepdims=True), axis=0, keepdims=True)
        dy = err * (1.0 / D)
        dy_ref[...] = dy
        dy_b = _bf(dy)
        dmix = _dot_nt(w, dy_b)
        dmr_ref[...] = dmix[0:512]
        dms_ref[...] = dmix[512:D]
        gw_ref[...] += _dot(_bf(mixed), dy_b)

    tm = 2 * TM
    row = lambda w: pl.BlockSpec((tm, w), lambda i: (i, 0))
    col = lambda w: pl.BlockSpec((w, tm), lambda i: (0, i))
    const = lambda shape: pl.BlockSpec(shape, lambda i: (0,) * len(shape))
    return pl.pallas_call(
        body, name="out_proj", grid=(T // tm,),
        in_specs=[col(512), col(512), const((D, D)), row(D), row(D)],
        out_specs=(const((1, 1)), row(D), col(512), col(512), const((D, D))),
        out_shape=(jax.ShapeDtypeStruct((1, 1), F32), jax.ShapeDtypeStruct((T, D), F32),
                   jax.ShapeDtypeStruct((512, T), F32), jax.ShapeDtypeStruct((512, T), F32),
                   jax.ShapeDtypeStruct((D, D), F32)),
        compiler_params=_cparams(1),
    )(mix_r, mix_s, wo, x, target)


LOSS_ROW = 5
ALL_CHIPS = ((0, 0), (0, 1), (1, 0), (1, 1))


def _w_out_reduce_scatter(gwo):
    def body(gwo_ref, gout_o, own, rcv, snd, got, tot, local_sems, a_send, a_recv, b_send, b_recv):
        k = pl.program_id(0)
        x_, y_, c_ = _mesh_pos()
        sibling = (x_, y_, 1 - c_)
        rel_chips = [(1 - x_, y_), (x_, 1 - y_), (1 - x_, 1 - y_)]

        def to_sibling(j):
            px, py = ALL_CHIPS[j]
            return pltpu.make_async_remote_copy(src_ref=gwo_ref.at[_blk(px, py, 1 - c_)], dst_ref=rcv.at[j],
                                                send_sem=a_send.at[j], recv_sem=a_recv.at[j], device_id=sibling,
                                                device_id_type=MESH)

        def local(j):
            px, py = ALL_CHIPS[j]
            return pltpu.make_async_copy(gwo_ref.at[_blk(px, py, c_)], own.at[j], local_sems.at[j])

        def to_chip(q):
            return pltpu.make_async_remote_copy(src_ref=snd.at[q], dst_ref=got.at[q], send_sem=b_send.at[q],
                                                recv_sem=b_recv.at[q], device_id=(*rel_chips[q], c_), device_id_type=MESH)

        @pl.when(k == 0)
        def _():
            for j in range(len(ALL_CHIPS)):
                to_sibling(j).start()
                local(j).start()

        @pl.when(k == 2)
        def _():
            for j in range(len(ALL_CHIPS)):
                local(j).wait()
                to_sibling(j).wait_recv()
            for q in range(3):
                j = 2 * rel_chips[q][0] + rel_chips[q][1]
                snd[q] = _bf(own[j] + rcv[j])
                to_chip(q).start()
            jm = 2 * x_ + y_
            tot[...] = own[jm] + rcv[jm]

        @pl.when(k == N_CH)
        def _():
            g = tot[...]
            for q in range(3):
                to_chip(q).wait_recv()
                g = g + got[q].astype(F32)
            gout_o[...] = g
            for j in range(len(ALL_CHIPS)):
                to_sibling(j).wait_send()
            for q in range(3):
                to_chip(q).wait_send()

    dma = pltpu.SemaphoreType.DMA
    return _Part(
        body, inputs=[gwo], in_specs=[pl.BlockSpec(memory_space=pl.ANY)],
        out_specs=[pl.BlockSpec((WOUT_BLK, D), lambda k: (0, 0))], out_shape=[jax.ShapeDtypeStruct((WOUT_BLK, D), F32)],
        scratch_shapes=[pltpu.VMEM((4, WOUT_BLK, D), F32), pltpu.VMEM((4, WOUT_BLK, D), F32),
                        pltpu.VMEM((3, WOUT_BLK, D), BF16), pltpu.VMEM((3, WOUT_BLK, D), BF16),
                        pltpu.VMEM((WOUT_BLK, D), F32), dma((4,)), dma((4,)), dma((4,)), dma((3,)), dma((3,))])


def _in_proj_bwd_rs(d_ret, d_swa, h, wt, x, norm_w, dy, sse, small_acc):
    n_blocks, n_tiles = N_DEV, T // TM
    last = n_blocks + n_tiles - 1

    def body(da_any, db_any, da_ref, db_ref, h_any, w_any, x_ref, nw_ref, dy_ref, sse_ref,
             dbias_ref, bucket_ref, dsink_ref, dqw_ref, dkw_ref, drw_ref, gx_ref, gin_o, suma_o, sumb_o,
             stage, d2d_src, mine, rcv, snd, got, tot, tab_a, tab_b, gnw_acc, rw_ref, qw_ref, kw_ref, sk_ref, rb_ref,
             h_ref, w_ref, dp_sems, d2d_send, d2d_recv, ici_send, ici_recv, s_send, s_recv, hw_sems):
        k = pl.program_id(0)
        x_, y_, c_ = _mesh_pos()
        h_load = pltpu.make_async_copy(h_any, h_ref, hw_sems.at[0])
        w_load = pltpu.make_async_copy(w_any, w_ref, hw_sems.at[1])
        me = _blk(x_, y_, c_)
        sibling = (x_, y_, 1 - c_)
        rel_chips = [(1 - x_, 1 - y_), (1 - x_, y_), (x_, 1 - y_), (x_, y_)]

        def block_of_step(s):
            return _blk(*rel_chips[s // 2], 1 - c_ if s % 2 == 0 else c_)

        def fetch(s, wait):
            slot, b = s % 2, block_of_step(s)
            split = RET_W - 4 * WIN_BLK

            def run(src, dst, sem):
                cp = pltpu.make_async_copy(src, dst, sem)
                cp.wait() if wait else cp.start()

            @pl.when(b < 4)
            def _():
                run(da_any.at[pl.ds(pl.multiple_of(b * WIN_BLK, 16), WIN_BLK), :], stage.at[slot], dp_sems.at[slot, 0])

            @pl.when(b == 4)
            def _():
                run(da_any.at[pl.ds(4 * WIN_BLK, split), :], stage.at[slot, pl.ds(0, split), :], dp_sems.at[slot, 0])
                run(db_any.at[pl.ds(0, WIN_BLK - split), :], stage.at[slot, pl.ds(split, WIN_BLK - split), :],
                    dp_sems.at[slot, 1])

            @pl.when(b > 4)
            def _():
                run(db_any.at[pl.ds(pl.multiple_of(b * WIN_BLK - RET_W, 16), WIN_BLK), :], stage.at[slot],
                    dp_sems.at[slot, 0])

        def d2d_copy(r):
            return pltpu.make_async_remote_copy(src_ref=d2d_src, dst_ref=rcv.at[r], send_sem=d2d_send.at[r],
                                                recv_sem=d2d_recv.at[r], device_id=sibling, device_id_type=MESH)

        def ici_copy(r):
            return pltpu.make_async_remote_copy(src_ref=snd.at[r], dst_ref=got.at[r], send_sem=ici_send.at[r],
                                                recv_sem=ici_recv.at[r], device_id=(*rel_chips[r], c_),
                                                device_id_type=MESH)

        def table_copies():
            return [pltpu.make_async_remote_copy(src_ref=tab.at[me], dst_ref=tab.at[me], send_sem=s_send.at[a, p - 1],
                                                 recv_sem=s_recv.at[a, p - 1],
                                                 device_id=(x_ ^ (p >> 2), y_ ^ ((p >> 1) & 1), c_ ^ (p & 1)),
                                                 device_id_type=MESH)
                    for p in range(1, N_DEV) for a, tab in enumerate((tab_a, tab_b))]

        def chip_sum(r):
            d2d_copy(r).wait_recv()
            total = mine[...] + rcv[r]
            if r < 3:
                snd[r] = _bf(total)
                ici_copy(r).start()
            else:
                tot[...] = total

        for s in range(n_blocks):
            @pl.when(k == s)
            def _(s=s):
                r = s // 2
                if s == 0:
                    gnw_acc[...] = jnp.zeros_like(gnw_acc)
                    h_load.start()
                    fetch(0, wait=False)
                    w_load.start()
                fetch(s, wait=True)
                if s + 1 < n_blocks:
                    fetch(s + 1, wait=False)
                if s == 0:
                    h_load.wait()
                gw = _dot(stage[s % 2], h_ref[...])
                if s % 2 == 0:
                    if s >= 2:
                        chip_sum(r - 1)
                        d2d_copy(r - 1).wait_send()
                    d2d_src[...] = gw
                    d2d_copy(r).start()
                else:
                    mine[...] = gw

        @pl.when(k == n_blocks)
        def _():
            chip_sum(3)
            w_load.wait()

        @pl.when(k >= n_blocks)
        def _():
            dp = jnp.concatenate([da_ref[...], db_ref[...]], axis=0)
            xv, nw = x_ref[...], nw_ref[...]
            r = lax.rsqrt(jnp.mean(xv * xv, axis=-1, keepdims=True) + NORM_EPS)
            dh = lax.dot_general(dp, w_ref[...], (((0,), (0,)), ((), ())), preferred_element_type=F32)
            u = dh * nw
            gx_ref[...] = dy_ref[...] + r * u - xv * (r * r * r) * jnp.mean(u * xv, axis=-1, keepdims=True)
            gnw_acc[...] += jnp.sum(dh * (xv * r), axis=0, keepdims=True)

        @pl.when(k == n_blocks)
        def _():
            _finish_small_grads(dbias_ref, bucket_ref, dsink_ref, dqw_ref, dkw_ref, drw_ref,
                                rw_ref, qw_ref, kw_ref, sk_ref, rb_ref)

        @pl.when(k == last)
        def _():
            mine_a, mine_b = tab_a.at[me], tab_b.at[me]
            mine_a[...] = jnp.zeros_like(mine_a)
            mine_b[...] = jnp.zeros_like(mine_b)
            for row, ref in enumerate((gnw_acc, rw_ref, qw_ref, kw_ref, sk_ref)):
                mine_a[row:row + 1, 0:ref.shape[1]] = ref[...]
            mine_a[LOSS_ROW:LOSS_ROW + 1, 0:1] = sse_ref[...]
            mine_b[:, 0:N_BUCKETS] = rb_ref[...]
            tables = table_copies()
            for cp in tables:
                cp.start()
            g_in = tot[...]
            for q in range(3):
                ici_copy(q).wait_recv()
                g_in = g_in + got[q].astype(F32)
            gin_o[...] = g_in
            for cp in tables:
                cp.wait_recv()
            sum_a, sum_b = tab_a[0], tab_b[0]
            for b in range(1, N_DEV):
                sum_a = sum_a + tab_a[b]
                sum_b = sum_b + tab_b[b]
            suma_o[...] = sum_a
            sumb_o[...] = sum_b
            d2d_copy(3).wait_send()
            for q in range(3):
                ici_copy(q).wait_send()
            for cp in tables:
                cp.wait_send()

    tile = lambda k: jnp.maximum(k - n_blocks, 0)
    row = lambda w: pl.BlockSpec((TM, w), lambda k: (tile(k), 0))
    col = lambda w: pl.BlockSpec((w, TM), lambda k: (0, tile(k)))
    const = lambda shape, **kw: pl.BlockSpec(shape, lambda k: (0,) * len(shape), **kw)
    once = dict(pipeline_mode=pl.Buffered(1))
    hbm = pl.BlockSpec(memory_space=pltpu.HBM)
    dma = pltpu.SemaphoreType.DMA
    return pl.pallas_call(
        body, name="in_proj_bwd_rs", grid=(n_blocks + n_tiles,),
        in_specs=[hbm, hbm, col(RET_W), col(SWA_W), hbm, hbm,
                  row(D), const((1, D)), row(D), const((1, 1))]
                 + [const(a.shape) for a in small_acc],
        out_specs=(row(D), const((WIN_BLK, D)), const((8, D)), const((SWA_H, CH))),
        out_shape=(jax.ShapeDtypeStruct((T, D), F32), jax.ShapeDtypeStruct((WIN_BLK, D), F32),
                   jax.ShapeDtypeStruct((8, D), F32), jax.ShapeDtypeStruct((SWA_H, CH), F32)),
        scratch_shapes=[
            pltpu.VMEM((2, WIN_BLK, T), BF16), pltpu.VMEM((WIN_BLK, D), F32),
            pltpu.VMEM((WIN_BLK, D), F32),
            pltpu.VMEM((4, WIN_BLK, D), F32), pltpu.VMEM((3, WIN_BLK, D), BF16),
            pltpu.VMEM((3, WIN_BLK, D), BF16), pltpu.VMEM((WIN_BLK, D), F32),
            pltpu.VMEM((N_DEV, 8, D), F32), pltpu.VMEM((N_DEV, SWA_H, CH), F32),
            pltpu.VMEM((1, D), F32),
        ] + [pltpu.VMEM(s, F32) for s in SMALL_GRAD_SHAPES] + [
            pltpu.VMEM((T, D), BF16), pltpu.VMEM((D_IN, D), BF16),
            dma((2, 2)), dma((4,)), dma((4,)), dma((3,)), dma((3,)), dma((2, 7)), dma((2, 7)), dma((2,)),
        ],
        compiler_params=_cparams(1),
    )(d_ret, d_swa, d_ret, d_swa, h, wt, x, norm_w, dy, sse, *small_acc)


SMALL_SHAPES = ((1, D), (1, 512), (1, SWA_D), (1, SWA_D), (1, SWA_H), (SWA_H, N_BUCKETS))


def _adamw_all(grad_x, g_in_t, g_out, sum_a, sum_b, w_in_t, m_in_t, v_in_t, w_out, m_out, v_out, small_w, small_m, small_v):
    n_small = len(SMALL_SHAPES)
    n_parts = 4

    def body(*refs):
        gx_in = refs[0]
        gin_any, gout_any, suma_any, sumb_any, w_in_ref, m_in_ref, v_in_ref, w_out_ref, m_out_ref, v_out_ref = refs[1:11]
        pos = 11
        sw_refs, sm_refs, sv_refs = (refs[pos + i * n_small:pos + (i + 1) * n_small] for i in range(3))
        pos += 3 * n_small
        g_in_o, d_in_o, nm_in_o, nv_in_o, g_out_o, d_out_o, nm_out_o, nv_out_o = refs[pos:pos + 8]
        pos += 8
        sg_o, sd_o, snm_o, snv_o = (refs[pos + i * n_small:pos + (i + 1) * n_small] for i in range(4))
        pos += 4 * n_small
        loss_o, gx_out, gin_ref, gout_ref, suma_ref, sumb_ref, load_sems = refs[pos:]
        i = pl.program_id(0)
        loads = [pltpu.make_async_copy(src, dst, load_sems.at[j]) for j, (src, dst) in enumerate(
            ((gin_any, gin_ref), (gout_any, gout_ref), (suma_any, suma_ref), (sumb_any, sumb_ref)))]

        @pl.when(i == 0)
        def _():
            for cp in loads:
                cp.start()

        gx_out[...] = gx_in[...]

        @pl.when(i == 0)
        def _():
            for cp in loads:
                cp.wait()

        g = gout_ref[pl.ds(pl.multiple_of(i * (WOUT_BLK // n_parts), 8), WOUT_BLK // n_parts), :]
        d, nm, nv = _adamw(w_out_ref[...], g, m_out_ref[...], v_out_ref[...])
        g_out_o[...], d_out_o[...], nm_out_o[...], nv_out_o[...] = g, d, nm, nv
        g = gin_ref[pl.ds(pl.multiple_of(i * (WIN_BLK // n_parts), 8), WIN_BLK // n_parts), :]
        d, nm, nv = _adamw(w_in_ref[...], g, m_in_ref[...], v_in_ref[...])
        g_in_o[...], d_in_o[...], nm_in_o[...], nv_in_o[...] = g, d, nm, nv

        @pl.when(pl.program_id(0) == 0)
        def _():
            loss_o[...] = suma_ref[LOSS_ROW:LOSS_ROW + 1, 0:1] * (0.5 / D)
            for r, (rows, lanes) in enumerate(SMALL_SHAPES):
                g = suma_ref[r:r + 1, 0:lanes] if rows == 1 else sumb_ref[:, 0:lanes]
                d, nm, nv = _adamw(sw_refs[r][...], g, sm_refs[r][...], sv_refs[r][...])
                sg_o[r][...], sd_o[r][...], snm_o[r][...], snv_o[r][...] = g, d, nm, nv

    part = lambda rows: pl.BlockSpec((rows // n_parts, D), lambda i: (i, 0))
    const = lambda shape: pl.BlockSpec(shape, lambda i: (0,) * len(shape))
    win = jax.ShapeDtypeStruct((WIN_BLK, D), F32)
    wout = jax.ShapeDtypeStruct((WOUT_BLK, D), F32)
    smalls = tuple(jax.ShapeDtypeStruct(s, F32) for s in SMALL_SHAPES)
    small_specs = [const(s) for s in SMALL_SHAPES]
    hbm = pl.BlockSpec(memory_space=pltpu.HBM)
    outs = pl.pallas_call(
        body, name="adamw_all", grid=(n_parts,),
        in_specs=[part(T)] + [hbm] * 4 + [part(WIN_BLK)] * 3 + [part(WOUT_BLK)] * 3 + small_specs * 3,
        out_specs=tuple([part(WIN_BLK)] * 4 + [part(WOUT_BLK)] * 4 + small_specs * 4 + [const((1, 1)), part(T)]),
        out_shape=(win,) * 4 + (wout,) * 4 + smalls * 4 + (jax.ShapeDtypeStruct((1, 1), F32),
                                                           jax.ShapeDtypeStruct(grad_x.shape, F32)),
        scratch_shapes=[pltpu.VMEM((WIN_BLK, D), F32), pltpu.VMEM((WOUT_BLK, D), F32), pltpu.VMEM((8, D), F32),
                        pltpu.VMEM((SWA_H, CH), F32), pltpu.SemaphoreType.DMA((4,))],
        compiler_params=_cparams(1, vmem=VMEM_WHOLE),
    )(grad_x, g_in_t, g_out, sum_a, sum_b, w_in_t, m_in_t, v_in_t, w_out, m_out, v_out, *small_w, *small_m, *small_v)
    big, rest = outs[:8], outs[8:]
    return big, [rest[i * n_small:(i + 1) * n_small] for i in range(4)], rest[4 * n_small], rest[4 * n_small + 1]


def _small_rows(norm_w, ret_norm_w, q_norm_w, k_norm_w, sinks, rel_bias):
    return (norm_w.reshape(1, D), ret_norm_w.reshape(1, 512), q_norm_w.reshape(1, SWA_D), k_norm_w.reshape(1, SWA_D),
            sinks.reshape(1, SWA_H), rel_bias.T)


def _small_leaves(rows):
    return (rows[0].reshape(D), rows[1].reshape(512), rows[2].reshape(SWA_D), rows[3].reshape(SWA_D),
            rows[4].reshape(SWA_H), rows[5].T)


def kernel(x, norm_w, w_in, ret_norm_w, q_norm_w, k_norm_w, sinks, rel_bias, w_out, loss_target, m_norm_w, m_w_in, m_ret_norm_w, m_q_norm_w, m_k_norm_w, m_sinks, m_rel_bias, m_w_out, v_norm_w, v_w_in, v_ret_norm_w, v_q_norm_w, v_k_norm_w, v_sinks, v_rel_bias, v_w_out):
    x2 = x.reshape(T, D)
    target = loss_target.reshape(T, D)
    nw = norm_w.reshape(1, D)
    relb = rel_bias.T
    ret_tables = _rotary_tables_t() + _retention_tables_t()
    bucket_t = _bucket_table_t()

    proj, wt, h = _in_proj_gather(x2, nw, w_in.T)
    (ro, mix_r, states, wo, rnw_col), (so, mix_s, qnw_col, knw_col) = _fused_call("attn_fwd", N_CH, [
        _retention_fwd(proj, ret_norm_w.reshape(1, RET_V), ret_tables, w_out),
        _swa_fwd(proj, q_norm_w.reshape(1, SWA_D), k_norm_w.reshape(1, SWA_D), sinks, relb, bucket_t)])
    sse, dy, dmix_r, dmix_s, gwo = _out_proj(mix_r, mix_s, wo.reshape(D, D), x2, target)

    (d_ret, drw_acc), (d_swa, dbias, dsink_acc, dqw_acc, dkw_acc), (g_out,) = _fused_call("attn_bwd", N_CH + 1, [
        _retention_bwd(proj, ro, states, dmix_r, rnw_col, ret_tables),
        _swa_bwd(proj, so, dmix_s, qnw_col, knw_col, sinks, relb, bucket_t),
        _w_out_reduce_scatter(gwo.reshape(N_DEV, WOUT_BLK, D))])
    grad_x, g_in_t, sum_a, sum_b = _in_proj_bwd_rs(d_ret, d_swa, h, wt, x2, nw, dy, sse,
                                                   (dbias, bucket_t, dsink_acc, dqw_acc, dkw_acc, drw_acc))

    small_w = _small_rows(norm_w, ret_norm_w, q_norm_w, k_norm_w, sinks, rel_bias)
    small_m = _small_rows(m_norm_w, m_ret_norm_w, m_q_norm_w, m_k_norm_w, m_sinks, m_rel_bias)
    small_v = _small_rows(v_norm_w, v_ret_norm_w, v_q_norm_w, v_k_norm_w, v_sinks, v_rel_bias)
    big, small, loss, grad_x = _adamw_all(grad_x, g_in_t, g_out, sum_a, sum_b, w_in.T, m_w_in.T, v_w_in.T, w_out, m_w_out, v_w_out,
                                  small_w, small_m, small_v)

    def leaves(i):
        a = _small_leaves(small[i])
        return (a[0], big[i].T, a[1], a[2], a[3], a[4], a[5], big[4 + i])

    return (loss.reshape(()), grad_x.reshape(1, T, D), *leaves(0), *leaves(1), *leaves(2), *leaves(3))
```

```python
from typing import Callable, NamedTuple

import numpy as np
import jax
import jax.numpy as jnp
from jax import lax
from jax.experimental import pallas as pl
from jax.experimental.pallas import tpu as pltpu

F32 = jnp.float32
BF16 = jnp.bfloat16
MESH = pl.DeviceIdType.MESH

T = 2048
D = 1024
D_IN = 2816
N_DEV = 8
WIN_BLK = D_IN // N_DEV
WOUT_BLK = D // N_DEV
CH = 128
N_CH = T // CH
RET_H, RET_DK, RET_DV = 4, 64, 128
RET_QK = RET_H * RET_DK
RET_V = RET_H * RET_DV
SWA_H, SWA_KV, SWA_D, SWA_G = 8, 2, 64, 4
N_BUCKETS = 32
NORM_EPS = 1e-6
GN_EPS = 1e-5
NEG_INF = -1e30
PIECES = (256, 256, 512, 512, 512, 128, 128, 512)
OFFS = tuple(int(v) for v in np.cumsum((0,) + PIECES))
RET_W = OFFS[4]
SWA_W = D_IN - RET_W
SWA_OFFS = tuple(o - RET_W for o in OFFS[4:])
TM = 512

ADAM_LR, ADAM_B1, ADAM_B2, ADAM_EPS, ADAM_WD, ADAM_STEP = 0.001, 0.9, 0.999, 1e-08, 0.01, 10

VMEM_LIMIT = 56 * 1024 * 1024
VMEM_WHOLE = 63 * 1024 * 1024


def _cparams(n_grid=0, vmem=VMEM_LIMIT):
    sem = ("arbitrary",) * n_grid if n_grid else None
    return pltpu.CompilerParams(dimension_semantics=sem, vmem_limit_bytes=vmem)


class _Part(NamedTuple):
    step: Callable
    inputs: list
    in_specs: list
    out_specs: list
    out_shape: list
    scratch_shapes: list


def _fused_call(name, n_steps, parts):
    n_in = [len(p.inputs) for p in parts]
    n_out = [len(p.out_shape) for p in parts]
    n_scr = [len(p.scratch_shapes) for p in parts]

    def body(*refs):
        ins, outs, scr = refs[:sum(n_in)], refs[sum(n_in):sum(n_in) + sum(n_out)], refs[sum(n_in) + sum(n_out):]
        for i, p in enumerate(parts):
            take = lambda seq, counts: seq[sum(counts[:i]):sum(counts[:i + 1])]
            p.step(*take(ins, n_in), *take(outs, n_out), *take(scr, n_scr))

    flat = lambda field: [v for p in parts for v in getattr(p, field)]
    outs = pl.pallas_call(
        body, name=name, grid=(n_steps,), in_specs=flat("in_specs"), out_specs=tuple(flat("out_specs")),
        out_shape=tuple(flat("out_shape")), scratch_shapes=flat("scratch_shapes"), compiler_params=_cparams(1),
    )(*flat("inputs"))
    return [list(outs[sum(n_out[:i]):sum(n_out[:i + 1])]) for i in range(len(parts))]


def _dot(a, b):
    return jnp.dot(a, b, preferred_element_type=F32)


def _dot_nt(a, b):
    return lax.dot_general(a, b, (((1,), (1,)), ((), ())), preferred_element_type=F32)


def _bf(a):
    return a.astype(BF16)


def _sigmoid(x):
    return 1.0 / (1.0 + jnp.exp(-x))


def _adamw(w, g, m, v):
    m = ADAM_B1 * m + (1.0 - ADAM_B1) * g
    v = ADAM_B2 * v + (1.0 - ADAM_B2) * (g * g)
    m_hat = m / (1.0 - ADAM_B1 ** ADAM_STEP)
    v_hat = v / (1.0 - ADAM_B2 ** ADAM_STEP)
    delta = -ADAM_LR * (m_hat / (jnp.sqrt(v_hat) + ADAM_EPS) + ADAM_WD * w)
    return delta, m, v


def _rotary_tables_t():
    half = RET_DK // 2
    inv_freq = np.float32(10000.0) ** (-np.arange(half, dtype=np.float32) / np.float32(half))
    ang = inv_freq[:, None] * np.arange(T, dtype=np.float32)[None, :]
    cos, sin = np.cos(ang).astype(np.float32), np.sin(ang).astype(np.float32)
    cos64 = np.concatenate([cos, cos], axis=0)
    sin64 = np.concatenate([-sin, sin], axis=0)
    return np.tile(cos64, (RET_H, 1)), np.tile(sin64, (RET_H, 1))


def _retention_tables_t():
    gamma = (1.0 - np.exp2(-5.0 - np.arange(RET_H, dtype=np.float32))).astype(np.float32)
    log_g = np.log(gamma).astype(np.float32)
    i = np.arange(CH, dtype=np.float32)
    diff = i[None, :] - i[:, None]
    decay = np.where(diff >= 0, np.exp(log_g[:, None, None] * np.maximum(diff, 0.0)), 0.0).astype(np.float32)
    decay_all = np.concatenate(list(decay), axis=1)
    zeta = np.exp(log_g[:, None] * (CH - 1.0 - i)).astype(np.float32)
    zeta_tab = np.repeat(zeta.T, RET_DK, axis=1)
    xi = np.exp(log_g[:, None] * (i + 1.0)).astype(np.float32)
    xi_tab = np.repeat(xi, RET_DK, axis=0)
    chunk_decay = np.exp(log_g * np.float32(CH)).astype(np.float32)
    row_head = np.arange(RET_V)[:, None] // RET_DV
    col_head = np.arange(RET_QK)[None, :] // RET_DK
    state_mask = (row_head == col_head).astype(np.float32)
    state_decay = (state_mask * chunk_decay[row_head]).astype(np.float32)
    q_mask = (np.arange(RET_QK)[:, None] // RET_DK == np.arange(RET_H * CH)[None, :] // CH).astype(np.float32)
    return (np.ascontiguousarray(decay_all), np.ascontiguousarray(zeta_tab), np.ascontiguousarray(xi_tab),
            state_mask, state_decay, q_mask)


def _bucket_table_t():
    qi = np.arange(CH)[:, None]
    kj = np.arange(2 * CH)[None, :]
    dist = qi + CH - kj
    n = np.maximum(dist, 0)
    max_exact = N_BUCKETS // 2
    nf = np.maximum(n, 1).astype(np.float32)
    large = max_exact + (np.log(nf / np.float32(max_exact)) / np.float32(np.log(CH / max_exact))
                         * np.float32(N_BUCKETS - max_exact)).astype(np.int32)
    large = np.minimum(large, N_BUCKETS - 1)
    bucket = np.where(n < max_exact, n, large)
    return np.ascontiguousarray(np.where((dist >= 0) & (dist < CH), bucket, -1).astype(np.int32).T)


def _mesh_pos():
    return lax.axis_index("x"), lax.axis_index("y"), lax.axis_index("c")


def _blk(px, py, pc):
    return 4 * px + 2 * py + pc


def _in_proj_gather(x, norm_w, w_in_t):
    tchunk = 512

    def body(x_any, nw_any, win_any, proj_ref, wt_out, h_out, wt_ref, h_ref, stage, x_ref, nw_ref, win_ref,
             send_sems, recv_sems, out_sems, wb_sems, h_sem, in_sems):
        n_tiles = T // TM
        loads = [pltpu.make_async_copy(win_any, win_ref, in_sems.at[0]),
                 pltpu.make_async_copy(nw_any, nw_ref, in_sems.at[1])]
        loads += [pltpu.make_async_copy(x_any.at[pl.ds(i * TM, TM), :], x_ref.at[pl.ds(i * TM, TM), :], in_sems.at[2 + i])
                  for i in range(n_tiles)]
        for cp in loads:
            cp.start()
        loads[0].wait()
        x, y, c = _mesh_pos()
        me = _blk(x, y, c)
        sibling = (x, y, 1 - c)
        nbr_a, nbr_b, diag = (x ^ (1 - c), y ^ c), (x ^ c, y ^ (1 - c)), (1 - x, 1 - y)

        def copy(k, b, to):
            return pltpu.make_async_remote_copy(src_ref=wt_ref.at[b], dst_ref=wt_ref.at[b], send_sem=send_sems.at[k],
                                                recv_sem=recv_sems.at[k], device_id=to, device_id_type=MESH)

        wt_ref[me] = _bf(win_ref[...])
        first = [copy(0, me, sibling), copy(1, me, (*nbr_a, c))]
        for cp in first:
            cp.start()
        to_b = copy(2, me, (*nbr_b, c))

        loads[1].wait()
        nw = nw_ref[...]
        for i in range(n_tiles):
            rows = slice(i * TM, (i + 1) * TM)
            loads[2 + i].wait()
            xv = x_ref[rows, :]
            r = lax.rsqrt(jnp.mean(xv * xv, axis=-1, keepdims=True) + NORM_EPS)
            h_ref[rows, :] = _bf(xv * r * nw)
        keep = [pltpu.make_async_copy(h_ref, h_out, h_sem)]
        keep[0].start()

        writes = []

        def project(b):
            k = len(writes)
            if k >= 2:
                writes[k - 2].wait()
            keep.append(pltpu.make_async_copy(wt_ref.at[b], wt_out.at[b], wb_sems.at[k]))
            keep[-1].start()
            w = wt_ref[b]
            for t in range(T // tchunk):
                cols = slice(t * tchunk, (t + 1) * tchunk)
                stage[k % 2, :, cols] = _dot_nt(w, h_ref[cols, :])
            cp = pltpu.make_async_copy(stage.at[k % 2], proj_ref.at[b], out_sems.at[k % 2])
            cp.start()
            writes.append(cp)

        here = (x, y, c)
        project(me)
        copy(0, _blk(x, y, 1 - c), here).wait_recv()
        project(_blk(x, y, 1 - c))
        to_b.start()
        passed = [to_b]

        def landed(k, chip, relay_to=None):
            copy(k, _blk(*chip, c), here).wait_recv()
            out = [copy({1: 4, 2: 5, 3: 6}[k], _blk(*chip, c), sibling)]
            if relay_to is not None:
                out.append(copy(3, _blk(*chip, c), (*relay_to, c)))
            for cp in out:
                cp.start()
            passed.extend(out)
            project(_blk(*chip, c))

        def from_sibling(k, chip):
            copy(k, _blk(*chip, 1 - c), here).wait_recv()
            project(_blk(*chip, 1 - c))

        landed(1, nbr_a, relay_to=nbr_b)
        from_sibling(4, nbr_b)
        landed(2, nbr_b)
        from_sibling(5, nbr_a)
        landed(3, diag)
        from_sibling(6, diag)
        for cp in writes[-2:] + keep:
            cp.wait()
        for cp in first + passed:
            cp.wait_send()

    vm = pl.BlockSpec(memory_space=pltpu.VMEM)
    hbm = pl.BlockSpec(memory_space=pltpu.HBM)
    dma = pltpu.SemaphoreType.DMA
    proj, wt, h = pl.pallas_call(
        body, name="in_proj_gather",
        out_shape=(jax.ShapeDtypeStruct((N_DEV, WIN_BLK, T), F32), jax.ShapeDtypeStruct((N_DEV, WIN_BLK, D), BF16),
                   jax.ShapeDtypeStruct((T, D), BF16)),
        in_specs=[hbm, hbm, hbm], out_specs=(hbm, hbm, hbm),
        scratch_shapes=[pltpu.VMEM((N_DEV, WIN_BLK, D), BF16), pltpu.VMEM((T, D), BF16), pltpu.VMEM((2, WIN_BLK, T), F32),
                        pltpu.VMEM((T, D), F32), pltpu.VMEM((1, D), F32), pltpu.VMEM((WIN_BLK, D), F32),
                        dma((7,)), dma((7,)), dma((2,)), dma((N_DEV,)), dma, dma((2 + T // TM,))],
        compiler_params=_cparams(vmem=VMEM_WHOLE),
    )(x, norm_w, w_in_t)
    return proj.reshape(D_IN, T), wt.reshape(D_IN, D), h


def _gather_w_out(w_out_ref, own_ref, wo_ref, local_sem, send_sems, recv_sems, phase):
    x, y, c = _mesh_pos()
    me = _blk(x, y, c)
    sibling = (x, y, 1 - c)
    chips = [(1 - x, y), (x, 1 - y), (1 - x, 1 - y)]

    def copy(k, b, to, src=None):
        return pltpu.make_async_remote_copy(src_ref=wo_ref.at[b] if src is None else src, dst_ref=wo_ref.at[b],
                                            send_sem=send_sems.at[k], recv_sem=recv_sems.at[k], device_id=to,
                                            device_id_type=MESH)

    local = pltpu.make_async_copy(own_ref, wo_ref.at[me], local_sem)
    first = [copy(0, me, sibling, own_ref)] + [copy(1 + j, me, (*chip, c), own_ref) for j, chip in enumerate(chips)]
    passed = [copy(4 + j, _blk(*chip, c), sibling) for j, chip in enumerate(chips)]
    if phase == 0:
        own_ref[...] = _bf(w_out_ref[...])
        local.start()
        for cp in first:
            cp.start()
    elif phase == 1:
        for j, chip in enumerate(chips):
            copy(1 + j, _blk(*chip, c), (x, y, c)).wait_recv()
            passed[j].start()
    else:
        local.wait()
        copy(0, _blk(x, y, 1 - c), (x, y, c)).wait_recv()
        for j, chip in enumerate(chips):
            copy(4 + j, _blk(*chip, 1 - c), (x, y, c)).wait_recv()
        for cp in first + passed:
            cp.wait_send()


def _proj_piece(i, block_of):
    rows = PIECES[i]
    assert OFFS[i] % rows == 0
    return pl.BlockSpec((rows, CH), lambda n: (OFFS[i] // rows, block_of(n)))


def _proj_gate_halves(block_of):
    return [pl.BlockSpec((256, CH), lambda n, j=j: (OFFS[7] // 256 + j, block_of(n))) for j in range(2)]


def _swap_halves_t(t):
    half = RET_DK // 2
    parts = []
    for h in range(RET_H):
        parts += [t[h * RET_DK + half:(h + 1) * RET_DK], t[h * RET_DK:h * RET_DK + half]]
    return jnp.concatenate(parts, axis=0)


def _rotate(t, cos, sin):
    return t * cos + _swap_halves_t(t) * sin


def _group_norm_t(o):
    mu = jnp.mean(o, axis=0, keepdims=True)
    var = jnp.mean((o - mu) * (o - mu), axis=0, keepdims=True)
    rstd = lax.rsqrt(var + GN_EPS)
    return (o - mu) * rstd, rstd


def _retention_scores_t(q_t, k_rm_b, q_mask, decay_all):
    q_heads = _bf(jnp.concatenate([q_t] * RET_H, axis=1) * q_mask)
    return _dot(k_rm_b, q_heads) * decay_all


def _row_to_col(row):
    n = row.shape[1]
    diagonal = lax.broadcasted_iota(jnp.int32, (n, n), 0) == lax.broadcasted_iota(jnp.int32, (n, n), 1)
    return jnp.sum(jnp.where(diagonal, row, 0.0), axis=1, keepdims=True)


def _retention_fwd(proj, ret_norm_w, tables, w_out_blk):
    cos_t, sin_t, decay_all, zeta_tab, xi_tab, state_mask, state_decay, q_mask = tables

    def body(rq_ref, rk_ref, rv_ref, rg_ref, w_row, cos_ref, sin_ref, decay_ref, zeta_ref, xi_ref, smask_ref,
             sdecay_ref, qmask_ref, wout_ref, ro_ref, mix_ref, st_ref, wo_ref, w_ref,
             state, wo_own, wo_local, wo_send, wo_recv):
        n = pl.program_id(0)

        @pl.when(n == 0)
        def _():
            w_ref[...] = _row_to_col(w_row[...])
            state[...] = jnp.zeros_like(state)
            _gather_w_out(wout_ref, wo_own, wo_ref, wo_local, wo_send, wo_recv, phase=0)

        @pl.when(n == N_CH // 2)
        def _():
            _gather_w_out(wout_ref, wo_own, wo_ref, wo_local, wo_send, wo_recv, phase=1)

        @pl.when(n == N_CH - 1)
        def _():
            _gather_w_out(wout_ref, wo_own, wo_ref, wo_local, wo_send, wo_recv, phase=2)

        cos, sin = cos_ref[...], sin_ref[...]
        q_t = _rotate(rq_ref[...], cos, sin)
        k_t = _rotate(rk_ref[...], cos, sin) * (RET_DK ** -0.5)
        k_rm = k_t.T
        v_b = _bf(rv_ref[...])
        m_b = _bf(state[...])
        st_ref[0] = m_b
        scores_b = _bf(_retention_scores_t(q_t, _bf(k_rm), qmask_ref[...], decay_ref[...]))
        cross = _dot(m_b, _bf(q_t * xi_ref[...]))
        state[...] = state[...] * sdecay_ref[...] + _dot(v_b, _bf(k_rm * zeta_ref[...])) * smask_ref[...]
        heads = range(RET_H)
        rows = [slice(h * RET_DV, (h + 1) * RET_DV) for h in heads]
        o = [_dot(v_b[rows[h]], scores_b[:, h * CH:(h + 1) * CH]) + cross[rows[h]] for h in heads]
        for h in heads:
            rn, _ = _group_norm_t(o[h])
            g = rg_ref[rows[h], :]
            ro_ref[rows[h], :] = o[h]
            mix_ref[rows[h], :] = rn * w_ref[rows[h], :] * (g * _sigmoid(g))

    col = lambda w: pl.BlockSpec((w, CH), lambda n: (0, n))
    const = lambda shape: pl.BlockSpec(shape, lambda n: (0,) * len(shape))
    cur = lambda n: n
    return _Part(
        body,
        inputs=[proj, proj, proj, proj, ret_norm_w, cos_t, sin_t, decay_all, zeta_tab, xi_tab, state_mask, state_decay,
                q_mask, w_out_blk],
        in_specs=[_proj_piece(i, cur) for i in range(4)] + [
            const((1, RET_V)), col(RET_QK), col(RET_QK),
            const(decay_all.shape), const(zeta_tab.shape), const(xi_tab.shape), const(state_mask.shape),
            const(state_decay.shape), const(q_mask.shape), const((WOUT_BLK, D))],
        out_specs=[col(RET_V), col(RET_V), pl.BlockSpec((1, RET_V, RET_QK), lambda n: (n, 0, 0)),
                   pl.BlockSpec(memory_space=pl.ANY), const((RET_V, 1))],
        out_shape=[jax.ShapeDtypeStruct((RET_V, T), F32), jax.ShapeDtypeStruct((RET_V, T), F32),
                   jax.ShapeDtypeStruct((N_CH, RET_V, RET_QK), BF16), jax.ShapeDtypeStruct((N_DEV, WOUT_BLK, D), BF16),
                   jax.ShapeDtypeStruct((RET_V, 1), F32)],
        scratch_shapes=[pltpu.VMEM((RET_V, RET_QK), F32), pltpu.VMEM((WOUT_BLK, D), BF16), pltpu.SemaphoreType.DMA,
                        pltpu.SemaphoreType.DMA((N_DEV - 1,)), pltpu.SemaphoreType.DMA((N_DEV - 1,))])


def _retention_bwd(proj, ro, states, dmix, ret_norm_w, tables):
    cos_t, sin_t, decay_all, zeta_tab, xi_tab, state_mask, state_decay, q_mask = tables

    def body(*refs):
        @pl.when(pl.program_id(0) < N_CH)
        def _():
            chunk(*refs)

    def chunk(rq_ref, rk_ref, rv_ref, rg_ref, ro_ref, st_ref, dm_ref, w_ref, cos_ref, sin_ref, decay_ref, zeta_ref,
              xi_ref, smask_ref, sdecay_ref, qmask_ref, d_ref, dw_ref, gstate):
        i = pl.program_id(0)

        @pl.when(i == 0)
        def _():
            gstate[...] = jnp.zeros_like(gstate)
            dw_ref[...] = jnp.zeros_like(dw_ref)

        cos, sin = cos_ref[...], sin_ref[...]
        q_t = _rotate(rq_ref[...], cos, sin)
        k_t = _rotate(rk_ref[...], cos, sin) * (RET_DK ** -0.5)
        q_b, k_b = _bf(q_t), _bf(k_t)
        k_rm = k_t.T
        kz_b = _bf(k_rm * zeta_ref[...])
        qx_b = _bf(q_t * xi_ref[...])
        v_t = rv_ref[...]
        v_b = _bf(v_t)
        v_rm_b = _bf(v_t.T)
        decay = decay_ref[...]
        scores_b = _bf(_retention_scores_t(q_t, _bf(k_rm), qmask_ref[...], decay))
        heads = range(RET_H)
        rows = [slice(h * RET_DV, (h + 1) * RET_DV) for h in heads]
        qk_rows = [slice(h * RET_DK, (h + 1) * RET_DK) for h in heads]
        lanes = [slice(h * CH, (h + 1) * CH) for h in heads]
        do = []
        for h in heads:
            g, w, dm = rg_ref[rows[h], :], w_ref[rows[h], :], dm_ref[rows[h], :]
            rn, rstd = _group_norm_t(ro_ref[rows[h], :])
            sig = _sigmoid(g)
            silu = g * sig
            d_ref[OFFS[3] + h * RET_DV:OFFS[3] + (h + 1) * RET_DV, :] = _bf(dm * rn * w * (sig * (1.0 + g * (1.0 - sig))))
            dw_ref[rows[h], :] += dm * silu * rn
            drn = dm * silu * w
            do.append(rstd * (drn - jnp.mean(drn, axis=0, keepdims=True)
                              - rn * jnp.mean(drn * rn, axis=0, keepdims=True)))
        do_b = _bf(jnp.concatenate(do, axis=0))
        m_b = st_ref[0]
        g_all = gstate[...]
        g_b = _bf(g_all)
        dscores_b = [_bf(_dot(v_rm_b[:, rows[h]], do_b[rows[h]]) * decay[:, lanes[h]]) for h in heads]
        dq_cross = lax.dot_general(m_b, do_b, (((0,), (0,)), ((), ())), preferred_element_type=F32)
        dkz = _dot(v_rm_b, g_b)
        dv_cross = _dot_nt(g_b, kz_b)
        gstate[...] = g_all * sdecay_ref[...] + _dot_nt(do_b, qx_b) * smask_ref[...]
        dq = jnp.concatenate([_dot(k_b[qk_rows[h]], dscores_b[h]) for h in heads], axis=0) + dq_cross * xi_ref[...]
        dk = (jnp.concatenate([_dot_nt(q_b[qk_rows[h]], dscores_b[h]) for h in heads], axis=0)
              + (dkz * zeta_ref[...]).T) * (RET_DK ** -0.5)
        for h in heads:
            d_ref[OFFS[2] + h * RET_DV:OFFS[2] + (h + 1) * RET_DV, :] = _bf(
                _dot_nt(do_b[rows[h]], scores_b[:, lanes[h]]) + dv_cross[rows[h]])
        d_ref[OFFS[0]:OFFS[1], :] = _bf(dq * cos + _swap_halves_t(dq * sin))
        d_ref[OFFS[1]:OFFS[2], :] = _bf(dk * cos + _swap_halves_t(dk * sin))

    chunk_of = lambda i: N_CH - 1 - jnp.minimum(i, N_CH - 1)
    col = lambda w: pl.BlockSpec((w, CH), lambda i: (0, chunk_of(i)))
    const = lambda shape: pl.BlockSpec(shape, lambda i: (0,) * len(shape))
    return _Part(
        body,
        inputs=[proj, proj, proj, proj, ro, states, dmix, ret_norm_w, cos_t, sin_t, decay_all, zeta_tab, xi_tab,
                state_mask, state_decay, q_mask],
        in_specs=[_proj_piece(j, chunk_of) for j in range(4)] + [
                  col(RET_V),
                  pl.BlockSpec((1, RET_V, RET_QK), lambda i: (chunk_of(i), 0, 0)), col(RET_V),
                  const((RET_V, 1)), col(RET_QK), col(RET_QK),
                  const(decay_all.shape), const(zeta_tab.shape), const(xi_tab.shape), const(state_mask.shape),
                  const(state_decay.shape), const(q_mask.shape)],
        out_specs=[col(RET_W), const((RET_V, CH))],
        out_shape=[jax.ShapeDtypeStruct((RET_W, T), BF16), jax.ShapeDtypeStruct((RET_V, CH), F32)],
        scratch_shapes=[pltpu.VMEM((RET_V, RET_QK), F32)])


HQ_LANES = SWA_H * CH


def _head_lanes(hq):
    return slice(hq * CH, (hq + 1) * CH)


def _build_bias_t(bucket_ref, relb_ref, sinks_ref, bias_ref, sink_row):
    bt = bucket_ref[...]
    first = lax.broadcasted_iota(jnp.int32, bt.shape, 0) < CH
    for hq in range(SWA_H):
        b = jnp.full(bt.shape, NEG_INF, F32)
        for bk in range(N_BUCKETS):
            b = jnp.where(bt == bk, relb_ref[hq, bk], b)
        bias_ref[0, :, _head_lanes(hq)] = b
        bias_ref[1, :, _head_lanes(hq)] = jnp.where(first, NEG_INF, b)
        sink_row[:, _head_lanes(hq)] = jnp.full((1, CH), sinks_ref[hq], F32)


def _rms_t(t, w_col):
    r = lax.rsqrt(jnp.mean(t * t, axis=0, keepdims=True) + NORM_EPS)
    return t * r * w_col, r


def _rms_t_bwd(dn, t, r, w_col):
    u = dn * w_col
    return r * u - t * (r * r * r) * jnp.mean(u * t, axis=0, keepdims=True), dn * t * r


def _norm_kv(k_t, kw):
    return jnp.concatenate([_rms_t(k_t[g * SWA_D:(g + 1) * SWA_D], kw)[0] for g in range(SWA_KV)], axis=0)


def _kv_slot(a, kv):
    z = jnp.zeros_like(a)
    return jnp.concatenate([a, z] if kv == 0 else [z, a], axis=0)


def _softmax_t(s, sink):
    m = jnp.maximum(jnp.max(s, axis=0, keepdims=True), sink)
    p = jnp.exp(s - m)
    e_sink = jnp.exp(sink - m)
    inv = 1.0 / (jnp.sum(p, axis=0, keepdims=True) + e_sink)
    return p * inv, e_sink * inv


def _gate_rows(sg_halves, hq):
    per_half = SWA_H // 2
    return sg_halves[hq // per_half][(hq % per_half) * SWA_D:(hq % per_half + 1) * SWA_D, :]


def _swa_fwd(proj, q_norm_w, k_norm_w, sinks, rel_bias_t, bucket_t):
    def body(sq_ref, skp_ref, skc_ref, svp_ref, svc_ref, sg_lo, sg_hi, qw_row, kw_row, sinks_ref, relb_ref, bucket_ref,
             so_ref, mix_ref, qw_ref, kw_ref, bias_ref, sink_row):
        n = pl.program_id(0)

        @pl.when(n == 0)
        def _():
            qw_ref[...] = _row_to_col(qw_row[...])
            kw_ref[...] = _row_to_col(kw_row[...])
            _build_bias_t(bucket_ref, relb_ref, sinks_ref, bias_ref, sink_row)

        var = (n == 0).astype(jnp.int32)
        qw, kw = qw_ref[...], kw_ref[...]
        kn_band = jnp.concatenate([_norm_kv(skp_ref[...], kw), _norm_kv(skc_ref[...], kw)], axis=1)
        kb_rm = _bf(kn_band.T)
        vband = _bf(jnp.concatenate([svp_ref[...], svc_ref[...]], axis=1))
        q_all = jnp.concatenate(
            [_bf(_kv_slot(_rms_t(sq_ref[hq * SWA_D:(hq + 1) * SWA_D, :], qw)[0] * (SWA_D ** -0.5), hq // SWA_G))
             for hq in range(SWA_H)], axis=1)
        probs, _ = _softmax_t(_dot(kb_rm, q_all) + bias_ref[var], sink_row[...])
        probs_b = _bf(probs)
        for kv in range(SWA_KV):
            o = _dot(vband[kv * SWA_D:(kv + 1) * SWA_D], probs_b[:, kv * SWA_G * CH:(kv + 1) * SWA_G * CH])
            for j in range(SWA_G):
                rows = slice((kv * SWA_G + j) * SWA_D, (kv * SWA_G + j + 1) * SWA_D)
                oh = o[:, j * CH:(j + 1) * CH]
                g = _gate_rows((sg_lo, sg_hi), kv * SWA_G + j)
                so_ref[rows, :] = oh
                mix_ref[rows, :] = oh * (g * _sigmoid(g))

    col = lambda w: pl.BlockSpec((w, CH), lambda n: (0, n))
    const = lambda shape: pl.BlockSpec(shape, lambda n: (0,) * len(shape))
    smem = pl.BlockSpec(memory_space=pltpu.SMEM)
    cur = lambda n: n
    prev = lambda n: jnp.maximum(n - 1, 0)
    return _Part(
        body,
        inputs=[proj, proj, proj, proj, proj, proj, proj, q_norm_w, k_norm_w, sinks, rel_bias_t, bucket_t],
        in_specs=[_proj_piece(4, cur), _proj_piece(5, prev), _proj_piece(5, cur), _proj_piece(6, prev),
                  _proj_piece(6, cur)] + _proj_gate_halves(cur) + [
                  const((1, SWA_D)), const((1, SWA_D)), smem, smem, const((2 * CH, CH))],
        out_specs=[col(512), col(512), const((SWA_D, 1)), const((SWA_D, 1))],
        out_shape=[jax.ShapeDtypeStruct((512, T), F32), jax.ShapeDtypeStruct((512, T), F32),
                   jax.ShapeDtypeStruct((SWA_D, 1), F32), jax.ShapeDtypeStruct((SWA_D, 1), F32)],
        scratch_shapes=[pltpu.VMEM((2, 2 * CH, HQ_LANES), F32), pltpu.VMEM((1, HQ_LANES), F32)])


def _swa_bwd(proj, so, dmix, q_norm_w, k_norm_w, sinks, rel_bias_t, bucket_t):
    def body(sq_ref, skp_ref, skc_ref, svp_ref, svc_ref, sg_lo, sg_hi, so_ref, dm_ref, qw_ref, kw_ref, sinks_ref,
             relb_ref, bucket_ref, d_ref, dbias_ref, dsink_ref, dqw_ref, dkw_ref,
             bias_ref, sink_row, band_dk, band_dv, carry_dk, carry_dv, hold_dq, hold_dg):
        n = pl.program_id(0)
        slot = n % 2

        @pl.when(n == 0)
        def _():
            _build_bias_t(bucket_ref, relb_ref, sinks_ref, bias_ref, sink_row)
            for ref in (dbias_ref, dsink_ref, dqw_ref, dkw_ref, carry_dk, carry_dv):
                ref[...] = jnp.zeros_like(ref)

        qw, kw = qw_ref[...], kw_ref[...]

        @pl.when(n < N_CH)
        def _():
            var = (n == 0).astype(jnp.int32)
            kn_band = jnp.concatenate([_norm_kv(skp_ref[...], kw), _norm_kv(skc_ref[...], kw)], axis=1)
            kb_rm = _bf(kn_band.T)
            kn_band_s = _bf(kn_band * (SWA_D ** -0.5))
            vband_f = jnp.concatenate([svp_ref[...], svc_ref[...]], axis=1)
            vb_rm = _bf(vband_f.T)
            q_raw, q_rstd, qs_b, do_b = [], [], [], []
            for hq in range(SWA_H):
                rows = slice(hq * SWA_D, (hq + 1) * SWA_D)
                q_t = sq_ref[rows, :]
                qn, rq = _rms_t(q_t, qw)
                g = _gate_rows((sg_lo, sg_hi), hq)
                sig = _sigmoid(g)
                dm = dm_ref[rows, :]
                hold_dg[slot, rows, :] = _bf(dm * so_ref[rows, :] * (sig * (1.0 + g * (1.0 - sig))))
                q_raw.append(q_t)
                q_rstd.append(rq)
                qs_b.append(_bf(qn * (SWA_D ** -0.5)))
                do_b.append(_bf(dm * (g * sig)))
            q_all = jnp.concatenate([_kv_slot(qs_b[hq], hq // SWA_G) for hq in range(SWA_H)], axis=1)
            do_all = jnp.concatenate([_kv_slot(do_b[hq], hq // SWA_G) for hq in range(SWA_H)], axis=1)
            probs, p_sink = _softmax_t(_dot(kb_rm, q_all) + bias_ref[var], sink_row[...])
            dprobs = _dot(vb_rm, do_all)
            t = jnp.sum(probs * dprobs, axis=0, keepdims=True)
            dlog = probs * (dprobs - t)
            dsink_ref[...] += -(p_sink * t)
            dbias_ref[...] += dlog
            dlog_b, probs_b = _bf(dlog), _bf(probs)
            dkn, dvv = [], []
            for kv in range(SWA_KV):
                heads = range(kv * SWA_G, (kv + 1) * SWA_G)
                lanes = slice(kv * SWA_G * CH, (kv + 1) * SWA_G * CH)
                dvv.append(_dot_nt(jnp.concatenate([do_b[hq] for hq in heads], axis=1), probs_b[:, lanes]))
                dkn.append(_dot_nt(jnp.concatenate([qs_b[hq] for hq in heads], axis=1), dlog_b[:, lanes]))
                dqn = _dot(kn_band_s[kv * SWA_D:(kv + 1) * SWA_D], dlog_b[:, lanes])
                for j, hq in enumerate(heads):
                    dq_t, dqw_terms = _rms_t_bwd(dqn[:, j * CH:(j + 1) * CH], q_raw[hq], q_rstd[hq], qw)
                    hold_dq[slot, hq * SWA_D:(hq + 1) * SWA_D, :] = _bf(dq_t)
                    dqw_ref[...] += dqw_terms
            band_dk[...] = jnp.concatenate(dkn, axis=0)
            band_dv[...] = jnp.concatenate(dvv, axis=0)

        @pl.when(n == N_CH)
        def _():
            band_dk[...] = jnp.zeros_like(band_dk)
            band_dv[...] = jnp.zeros_like(band_dv)

        @pl.when(n >= 1)
        def _():
            dkn_prev = carry_dk[...] + band_dk[:, 0:CH]
            k_t = skp_ref[...]
            for kv in range(SWA_KV):
                rows = slice(kv * SWA_D, (kv + 1) * SWA_D)
                _, rk = _rms_t(k_t[rows], kw)
                dk_t, dkw_terms = _rms_t_bwd(dkn_prev[rows], k_t[rows], rk, kw)
                d_ref[SWA_OFFS[1] + kv * SWA_D:SWA_OFFS[1] + (kv + 1) * SWA_D, :] = _bf(dk_t)
                dkw_ref[...] += dkw_terms
            d_ref[SWA_OFFS[2]:SWA_OFFS[3], :] = _bf(carry_dv[...] + band_dv[:, 0:CH])
            d_ref[SWA_OFFS[0]:SWA_OFFS[1], :] = hold_dq[1 - slot]
            d_ref[SWA_OFFS[3]:SWA_OFFS[4], :] = hold_dg[1 - slot]

        carry_dk[...] = band_dk[:, CH:2 * CH]
        carry_dv[...] = band_dv[:, CH:2 * CH]

    cur_block = lambda n: jnp.minimum(n, N_CH - 1)
    prev_block = lambda n: jnp.maximum(n - 1, 0)
    col = lambda w: pl.BlockSpec((w, CH), lambda n: (0, cur_block(n)))
    prev = lambda w: pl.BlockSpec((w, CH), lambda n: (0, prev_block(n)))
    const = lambda shape: pl.BlockSpec(shape, lambda n: (0,) * len(shape))
    smem = pl.BlockSpec(memory_space=pltpu.SMEM)
    return _Part(
        body,
        inputs=[proj, proj, proj, proj, proj, proj, proj, so, dmix, q_norm_w, k_norm_w, sinks, rel_bias_t, bucket_t],
        in_specs=[_proj_piece(4, cur_block), _proj_piece(5, prev_block), _proj_piece(5, cur_block),
                  _proj_piece(6, prev_block), _proj_piece(6, cur_block)] + _proj_gate_halves(cur_block) + [
                  col(512), col(512), const((SWA_D, 1)), const((SWA_D, 1)), smem, smem, const((2 * CH, CH))],
        out_specs=[prev(SWA_W), const((2 * CH, HQ_LANES)), const((1, HQ_LANES)),
                   const((SWA_D, CH)), const((SWA_D, CH))],
        out_shape=[jax.ShapeDtypeStruct((SWA_W, T), BF16),
                   jax.ShapeDtypeStruct((2 * CH, HQ_LANES), F32), jax.ShapeDtypeStruct((1, HQ_LANES), F32),
                   jax.ShapeDtypeStruct((SWA_D, CH), F32), jax.ShapeDtypeStruct((SWA_D, CH), F32)],
        scratch_shapes=[pltpu.VMEM((2, 2 * CH, HQ_LANES), F32), pltpu.VMEM((1, HQ_LANES), F32),
                        pltpu.VMEM((128, 2 * CH), F32), pltpu.VMEM((128, 2 * CH), F32),
                        pltpu.VMEM((128, CH), F32), pltpu.VMEM((128, CH), F32),
                        pltpu.VMEM((2, 512, CH), BF16), pltpu.VMEM((2, 512, CH), BF16)])


SMALL_GRAD_SHAPES = ((1, RET_V), (1, CH), (1, CH), (1, CH), (SWA_H, N_BUCKETS))


def _finish_small_grads(dbias_ref, bucket_ref, dsink_ref, dqw_ref, dkw_ref, drw_ref, rw_o, qw_o, kw_o, sink_o, relb_o):
    bt = bucket_ref[...]
    row = lax.broadcasted_iota(jnp.int32, (SWA_H, N_BUCKETS), 0)
    col = lax.broadcasted_iota(jnp.int32, (SWA_H, N_BUCKETS), 1)
    lane = lax.broadcasted_iota(jnp.int32, (1, CH), 1)
    acc = jnp.zeros((SWA_H, N_BUCKETS), F32)
    sink = jnp.zeros((1, CH), F32)
    for hq in range(SWA_H):
        d = dbias_ref[:, _head_lanes(hq)]
        for bk in range(N_BUCKETS):
            s = jnp.sum(jnp.sum(jnp.where(bt == bk, d, 0.0), axis=0, keepdims=True), axis=1, keepdims=True)
            acc = acc + jnp.where((row == hq) & (col == bk), s, 0.0)
        sink = sink + jnp.where(lane == hq, jnp.sum(dsink_ref[:, _head_lanes(hq)], axis=1, keepdims=True), 0.0)
    relb_o[...] = acc
    sink_o[...] = sink
    for src, dst in ((dqw_ref, qw_o), (dkw_ref, kw_o)):
        padded = jnp.concatenate([src[...], jnp.zeros((CH - SWA_D, CH), F32)], axis=0)
        dst[...] = jnp.sum(padded.T, axis=0, keepdims=True)
    rw_o[...] = jnp.sum(drw_ref[...].T, axis=0, keepdims=True)


def _out_proj(mix_r, mix_s, wo, x, target):
    def body(mr_ref, ms_ref, w_ref, x_ref, t_ref, loss_ref, dy_ref, dmr_ref, dms_ref, gw_ref):
        i = pl.program_id(0)

        @pl.when(i == 0)
        def _():
            loss_ref[...] = jnp.zeros_like(loss_ref)
            gw_ref[...] = jnp.zeros_like(gw_ref)

        mixed = jnp.concatenate([mr_ref[...], ms_ref[...]], axis=0)
        w = w_ref[...]
        err = x_ref[...] + _dot(_bf(mixed.T), w) - t_ref[...]
        loss_ref[...] += jnp.sum(jnp.sum(err * err, axis=1, keepdims=True), axis=0, keepdims=True)
        dy = err * (1.0 / D)
        dy_ref[...] = dy
        dy_b = _bf(dy)
        dmix = _dot_nt(w, dy_b)
        dmr_ref[...] = dmix[0:512]
        dms_ref[...] = dmix[512:D]
        gw_ref[...] += _dot(_bf(mixed), dy_b)

    row = lambda w: pl.BlockSpec((TM, w), lambda i: (i, 0))
    col = lambda w: pl.BlockSpec((w, TM), lambda i: (0, i))
    const = lambda shape: pl.BlockSpec(shape, lambda i: (0,) * len(shape))
    return pl.pallas_call(
        body, name="out_proj", grid=(T // TM,),
        in_specs=[col(512), col(512), const((D, D)), row(D), row(D)],
        out_specs=(const((1, 1)), row(D), col(512), col(512), const((D, D))),
        out_shape=(jax.ShapeDtypeStruct((1, 1), F32), jax.ShapeDtypeStruct((T, D), F32),
                   jax.ShapeDtypeStruct((512, T), F32), jax.ShapeDtypeStruct((512, T), F32),
                   jax.ShapeDtypeStruct((D, D), F32)),
        compiler_params=_cparams(1),
    )(mix_r, mix_s, wo, x, target)


LOSS_ROW = 5
ALL_CHIPS = ((0, 0), (0, 1), (1, 0), (1, 1))


def _w_out_reduce_scatter(gwo):
    def body(gwo_ref, gout_o, own, rcv, snd, got, tot, local_sems, a_send, a_recv, b_send, b_recv):
        k = pl.program_id(0)
        x_, y_, c_ = _mesh_pos()
        sibling = (x_, y_, 1 - c_)
        rel_chips = [(1 - x_, y_), (x_, 1 - y_), (1 - x_, 1 - y_)]

        def to_sibling(j):
            px, py = ALL_CHIPS[j]
            return pltpu.make_async_remote_copy(src_ref=gwo_ref.at[_blk(px, py, 1 - c_)], dst_ref=rcv.at[j],
                                                send_sem=a_send.at[j], recv_sem=a_recv.at[j], device_id=sibling,
                                                device_id_type=MESH)

        def local(j):
            px, py = ALL_CHIPS[j]
            return pltpu.make_async_copy(gwo_ref.at[_blk(px, py, c_)], own.at[j], local_sems.at[j])

        def to_chip(q):
            return pltpu.make_async_remote_copy(src_ref=snd.at[q], dst_ref=got.at[q], send_sem=b_send.at[q],
                                                recv_sem=b_recv.at[q], device_id=(*rel_chips[q], c_), device_id_type=MESH)

        @pl.when(k == 0)
        def _():
            for j in range(len(ALL_CHIPS)):
                to_sibling(j).start()
                local(j).start()

        @pl.when(k == 2)
        def _():
            for j in range(len(ALL_CHIPS)):
                local(j).wait()
                to_sibling(j).wait_recv()
            for q in range(3):
                j = 2 * rel_chips[q][0] + rel_chips[q][1]
                snd[q] = _bf(own[j] + rcv[j])
                to_chip(q).start()
            jm = 2 * x_ + y_
            tot[...] = own[jm] + rcv[jm]

        @pl.when(k == N_CH)
        def _():
            g = tot[...]
            for q in range(3):
                to_chip(q).wait_recv()
                g = g + got[q].astype(F32)
            gout_o[...] = g
            for j in range(len(ALL_CHIPS)):
                to_sibling(j).wait_send()
            for q in range(3):
                to_chip(q).wait_send()

    dma = pltpu.SemaphoreType.DMA
    return _Part(
        body, inputs=[gwo], in_specs=[pl.BlockSpec(memory_space=pl.ANY)],
        out_specs=[pl.BlockSpec((WOUT_BLK, D), lambda k: (0, 0))], out_shape=[jax.ShapeDtypeStruct((WOUT_BLK, D), F32)],
        scratch_shapes=[pltpu.VMEM((4, WOUT_BLK, D), F32), pltpu.VMEM((4, WOUT_BLK, D), F32),
                        pltpu.VMEM((3, WOUT_BLK, D), BF16), pltpu.VMEM((3, WOUT_BLK, D), BF16),
                        pltpu.VMEM((WOUT_BLK, D), F32), dma((4,)), dma((4,)), dma((4,)), dma((3,)), dma((3,))])


def _in_proj_bwd_rs(d_ret, d_swa, h, wt, x, norm_w, dy, sse, small_acc):
    n_blocks, n_tiles = N_DEV, T // TM
    last = n_blocks + n_tiles - 1

    def body(da_any, db_any, da_ref, db_ref, h_any, w_any, x_ref, nw_ref, dy_ref, sse_ref,
             dbias_ref, bucket_ref, dsink_ref, dqw_ref, dkw_ref, drw_ref, gx_ref, gin_o, suma_o, sumb_o,
             stage, d2d_src, mine, rcv, snd, got, tot, tab_a, tab_b, gnw_acc, rw_ref, qw_ref, kw_ref, sk_ref, rb_ref,
             h_ref, w_ref, dp_sems, d2d_send, d2d_recv, ici_send, ici_recv, s_send, s_recv, hw_sems):
        k = pl.program_id(0)
        x_, y_, c_ = _mesh_pos()
        h_load = pltpu.make_async_copy(h_any, h_ref, hw_sems.at[0])
        w_load = pltpu.make_async_copy(w_any, w_ref, hw_sems.at[1])
        me = _blk(x_, y_, c_)
        sibling = (x_, y_, 1 - c_)
        rel_chips = [(1 - x_, 1 - y_), (1 - x_, y_), (x_, 1 - y_), (x_, y_)]

        def block_of_step(s):
            return _blk(*rel_chips[s // 2], 1 - c_ if s % 2 == 0 else c_)

        def fetch(s, wait):
            slot, b = s % 2, block_of_step(s)
            split = RET_W - 4 * WIN_BLK

            def run(src, dst, sem):
                cp = pltpu.make_async_copy(src, dst, sem)
                cp.wait() if wait else cp.start()

            @pl.when(b < 4)
            def _():
                run(da_any.at[pl.ds(pl.multiple_of(b * WIN_BLK, 16), WIN_BLK), :], stage.at[slot], dp_sems.at[slot, 0])

            @pl.when(b == 4)
            def _():
                run(da_any.at[pl.ds(4 * WIN_BLK, split), :], stage.at[slot, pl.ds(0, split), :], dp_sems.at[slot, 0])
                run(db_any.at[pl.ds(0, WIN_BLK - split), :], stage.at[slot, pl.ds(split, WIN_BLK - split), :],
                    dp_sems.at[slot, 1])

            @pl.when(b > 4)
            def _():
                run(db_any.at[pl.ds(pl.multiple_of(b * WIN_BLK - RET_W, 16), WIN_BLK), :], stage.at[slot],
                    dp_sems.at[slot, 0])

        def d2d_copy(r):
            return pltpu.make_async_remote_copy(src_ref=d2d_src, dst_ref=rcv.at[r], send_sem=d2d_send.at[r],
                                                recv_sem=d2d_recv.at[r], device_id=sibling, device_id_type=MESH)

        def ici_copy(r):
            return pltpu.make_async_remote_copy(src_ref=snd.at[r], dst_ref=got.at[r], send_sem=ici_send.at[r],
                                                recv_sem=ici_recv.at[r], device_id=(*rel_chips[r], c_),
                                                device_id_type=MESH)

        def table_copies():
            return [pltpu.make_async_remote_copy(src_ref=tab.at[me], dst_ref=tab.at[me], send_sem=s_send.at[a, p - 1],
                                                 recv_sem=s_recv.at[a, p - 1],
                                                 device_id=(x_ ^ (p >> 2), y_ ^ ((p >> 1) & 1), c_ ^ (p & 1)),
                                                 device_id_type=MESH)
                    for p in range(1, N_DEV) for a, tab in enumerate((tab_a, tab_b))]

        def chip_sum(r):
            d2d_copy(r).wait_recv()
            total = mine[...] + rcv[r]
            if r < 3:
                snd[r] = _bf(total)
                ici_copy(r).start()
            else:
                tot[...] = total

        for s in range(n_blocks):
            @pl.when(k == s)
            def _(s=s):
                r = s // 2
                if s == 0:
                    gnw_acc[...] = jnp.zeros_like(gnw_acc)
                    h_load.start()
                    fetch(0, wait=False)
                    w_load.start()
                fetch(s, wait=True)
                if s + 1 < n_blocks:
                    fetch(s + 1, wait=False)
                if s == 0:
                    h_load.wait()
                gw = _dot(stage[s % 2], h_ref[...])
                if s % 2 == 0:
                    if s >= 2:
                        chip_sum(r - 1)
                        d2d_copy(r - 1).wait_send()
                    d2d_src[...] = gw
                    d2d_copy(r).start()
                else:
                    mine[...] = gw

        @pl.when(k == n_blocks)
        def _():
            chip_sum(3)
            w_load.wait()

        @pl.when(k >= n_blocks)
        def _():
            dp = jnp.concatenate([da_ref[...], db_ref[...]], axis=0)
            xv, nw = x_ref[...], nw_ref[...]
            r = lax.rsqrt(jnp.mean(xv * xv, axis=-1, keepdims=True) + NORM_EPS)
            dh = lax.dot_general(dp, w_ref[...], (((0,), (0,)), ((), ())), preferred_element_type=F32)
            u = dh * nw
            gx_ref[...] = dy_ref[...] + r * u - xv * (r * r * r) * jnp.mean(u * xv, axis=-1, keepdims=True)
            gnw_acc[...] += jnp.sum(dh * (xv * r), axis=0, keepdims=True)

        @pl.when(k == n_blocks)
        def _():
            _finish_small_grads(dbias_ref, bucket_ref, dsink_ref, dqw_ref, dkw_ref, drw_ref,
                                rw_ref, qw_ref, kw_ref, sk_ref, rb_ref)

        @pl.when(k == last)
        def _():
            mine_a, mine_b = tab_a.at[me], tab_b.at[me]
            mine_a[...] = jnp.zeros_like(mine_a)
            mine_b[...] = jnp.zeros_like(mine_b)
            for row, ref in enumerate((gnw_acc, rw_ref, qw_ref, kw_ref, sk_ref)):
                mine_a[row:row + 1, 0:ref.shape[1]] = ref[...]
            mine_a[LOSS_ROW:LOSS_ROW + 1, 0:1] = sse_ref[...]
            mine_b[:, 0:N_BUCKETS] = rb_ref[...]
            tables = table_copies()
            for cp in tables:
                cp.start()
            g_in = tot[...]
            for q in range(3):
                ici_copy(q).wait_recv()
                g_in = g_in + got[q].astype(F32)
            gin_o[...] = g_in
            for cp in tables:
                cp.wait_recv()
            sum_a, sum_b = tab_a[0], tab_b[0]
            for b in range(1, N_DEV):
                sum_a = sum_a + tab_a[b]
                sum_b = sum_b + tab_b[b]
            suma_o[...] = sum_a
            sumb_o[...] = sum_b
            d2d_copy(3).wait_send()
            for q in range(3):
                ici_copy(q).wait_send()
            for cp in tables:
                cp.wait_send()

    tile = lambda k: jnp.maximum(k - n_blocks, 0)
    row = lambda w: pl.BlockSpec((TM, w), lambda k: (tile(k), 0))
    col = lambda w: pl.BlockSpec((w, TM), lambda k: (0, tile(k)))
    const = lambda shape, **kw: pl.BlockSpec(shape, lambda k: (0,) * len(shape), **kw)
    once = dict(pipeline_mode=pl.Buffered(1))
    hbm = pl.BlockSpec(memory_space=pltpu.HBM)
    dma = pltpu.SemaphoreType.DMA
    return pl.pallas_call(
        body, name="in_proj_bwd_rs", grid=(n_blocks + n_tiles,),
        in_specs=[hbm, hbm, col(RET_W), col(SWA_W), hbm, hbm,
                  row(D), const((1, D)), row(D), const((1, 1))]
                 + [const(a.shape) for a in small_acc],
        out_specs=(row(D), const((WIN_BLK, D)), const((8, D)), const((SWA_H, CH))),
        out_shape=(jax.ShapeDtypeStruct((T, D), F32), jax.ShapeDtypeStruct((WIN_BLK, D), F32),
                   jax.ShapeDtypeStruct((8, D), F32), jax.ShapeDtypeStruct((SWA_H, CH), F32)),
        scratch_shapes=[
            pltpu.VMEM((2, WIN_BLK, T), BF16), pltpu.VMEM((WIN_BLK, D), F32),
            pltpu.VMEM((WIN_BLK, D), F32),
            pltpu.VMEM((4, WIN_BLK, D), F32), pltpu.VMEM((3, WIN_BLK, D), BF16),
            pltpu.VMEM((3, WIN_BLK, D), BF16), pltpu.VMEM((WIN_BLK, D), F32),
            pltpu.VMEM((N_DEV, 8, D), F32), pltpu.VMEM((N_DEV, SWA_H, CH), F32),
            pltpu.VMEM((1, D), F32),
        ] + [pltpu.VMEM(s, F32) for s in SMALL_GRAD_SHAPES] + [
            pltpu.VMEM((T, D), BF16), pltpu.VMEM((D_IN, D), BF16),
            dma((2, 2)), dma((4,)), dma((4,)), dma((3,)), dma((3,)), dma((2, 7)), dma((2, 7)), dma((2,)),
        ],
        compiler_params=_cparams(1),
    )(d_ret, d_swa, d_ret, d_swa, h, wt, x, norm_w, dy, sse, *small_acc)


SMALL_SHAPES = ((1, D), (1, 512), (1, SWA_D), (1, SWA_D), (1, SWA_H), (SWA_H, N_BUCKETS))


def _adamw_all(grad_x, g_in_t, g_out, sum_a, sum_b, w_in_t, m_in_t, v_in_t, w_out, m_out, v_out, small_w, small_m, small_v):
    n_small = len(SMALL_SHAPES)
    n_parts = 2

    def body(*refs):
        gx_in = refs[0]
        gin_any, gout_any, suma_any, sumb_any, w_in_ref, m_in_ref, v_in_ref, w_out_ref, m_out_ref, v_out_ref = refs[1:11]
        pos = 11
        sw_refs, sm_refs, sv_refs = (refs[pos + i * n_small:pos + (i + 1) * n_small] for i in range(3))
        pos += 3 * n_small
        g_in_o, d_in_o, nm_in_o, nv_in_o, g_out_o, d_out_o, nm_out_o, nv_out_o = refs[pos:pos + 8]
        pos += 8
        sg_o, sd_o, snm_o, snv_o = (refs[pos + i * n_small:pos + (i + 1) * n_small] for i in range(4))
        pos += 4 * n_small
        loss_o, gx_out, gin_ref, gout_ref, suma_ref, sumb_ref, load_sems = refs[pos:]
        i = pl.program_id(0)
        loads = [pltpu.make_async_copy(src, dst, load_sems.at[j]) for j, (src, dst) in enumerate(
            ((gin_any, gin_ref), (gout_any, gout_ref), (suma_any, suma_ref), (sumb_any, sumb_ref)))]

        @pl.when(i == 0)
        def _():
            for cp in loads:
                cp.start()

        gx_out[...] = gx_in[...]

        @pl.when(i == 0)
        def _():
            for cp in loads:
                cp.wait()

        g = gout_ref[pl.ds(pl.multiple_of(i * (WOUT_BLK // n_parts), 8), WOUT_BLK // n_parts), :]
        d, nm, nv = _adamw(w_out_ref[...], g, m_out_ref[...], v_out_ref[...])
        g_out_o[...], d_out_o[...], nm_out_o[...], nv_out_o[...] = g, d, nm, nv
        g = gin_ref[pl.ds(pl.multiple_of(i * (WIN_BLK // n_parts), 8), WIN_BLK // n_parts), :]
        d, nm, nv = _adamw(w_in_ref[...], g, m_in_ref[...], v_in_ref[...])
        g_in_o[...], d_in_o[...], nm_in_o[...], nv_in_o[...] = g, d, nm, nv

        @pl.when(pl.program_id(0) == 0)
        def _():
            loss_o[...] = suma_ref[LOSS_ROW:LOSS_ROW + 1, 0:1] * (0.5 / D)
            for r, (rows, lanes) in enumerate(SMALL_SHAPES):
                g = suma_ref[r:r + 1, 0:lanes] if rows == 1 else sumb_ref[:, 0:lanes]
                d, nm, nv = _adamw(sw_refs[r][...], g, sm_refs[r][...], sv_refs[r][...])
                sg_o[r][...], sd_o[r][...], snm_o[r][...], snv_o[r][...] = g, d, nm, nv

    part = lambda rows: pl.BlockSpec((rows // n_parts, D), lambda i: (i, 0))
    const = lambda shape: pl.BlockSpec(shape, lambda i: (0,) * len(shape))
    win = jax.ShapeDtypeStruct((WIN_BLK, D), F32)
    wout = jax.ShapeDtypeStruct((WOUT_BLK, D), F32)
    smalls = tuple(jax.ShapeDtypeStruct(s, F32) for s in SMALL_SHAPES)
    small_specs = [const(s) for s in SMALL_SHAPES]
    hbm = pl.BlockSpec(memory_space=pltpu.HBM)
    outs = pl.pallas_call(
        body, name="adamw_all", grid=(n_parts,),
        in_specs=[part(T)] + [hbm] * 4 + [part(WIN_BLK)] * 3 + [part(WOUT_BLK)] * 3 + small_specs * 3,
        out_specs=tuple([part(WIN_BLK)] * 4 + [part(WOUT_BLK)] * 4 + small_specs * 4 + [const((1, 1)), part(T)]),
        out_shape=(win,) * 4 + (wout,) * 4 + smalls * 4 + (jax.ShapeDtypeStruct((1, 1), F32),
                                                           jax.ShapeDtypeStruct(grad_x.shape, F32)),
        scratch_shapes=[pltpu.VMEM((WIN_BLK, D), F32), pltpu.VMEM((WOUT_BLK, D), F32), pltpu.VMEM((8, D), F32),
                        pltpu.VMEM((SWA_H, CH), F32), pltpu.SemaphoreType.DMA((4,))],
        compiler_params=_cparams(1, vmem=VMEM_WHOLE),
    )(grad_x, g_in_t, g_out, sum_a, sum_b, w_in_t, m_in_t, v_in_t, w_out, m_out, v_out, *small_w, *small_m, *small_v)
    big, rest = outs[:8], outs[8:]
    return big, [rest[i * n_small:(i + 1) * n_small] for i in range(4)], rest[4 * n_small], rest[4 * n_small + 1]


def _small_rows(norm_w, ret_norm_w, q_norm_w, k_norm_w, sinks, rel_bias):
    return (norm_w.reshape(1, D), ret_norm_w.reshape(1, 512), q_norm_w.reshape(1, SWA_D), k_norm_w.reshape(1, SWA_D),
            sinks.reshape(1, SWA_H), rel_bias.T)


def _small_leaves(rows):
    return (rows[0].reshape(D), rows[1].reshape(512), rows[2].reshape(SWA_D), rows[3].reshape(SWA_D),
            rows[4].reshape(SWA_H), rows[5].T)


def kernel(x, norm_w, w_in, ret_norm_w, q_norm_w, k_norm_w, sinks, rel_bias, w_out, loss_target, m_norm_w, m_w_in, m_ret_norm_w, m_q_norm_w, m_k_norm_w, m_sinks, m_rel_bias, m_w_out, v_norm_w, v_w_in, v_ret_norm_w, v_q_norm_w, v_k_norm_w, v_sinks, v_rel_bias, v_w_out):
    x2 = x.reshape(T, D)
    target = loss_target.reshape(T, D)
    nw = norm_w.reshape(1, D)
    relb = rel_bias.T
    ret_tables = _rotary_tables_t() + _retention_tables_t()
    bucket_t = _bucket_table_t()

    proj, wt, h = _in_proj_gather(x2, nw, w_in.T)
    (ro, mix_r, states, wo, rnw_col), (so, mix_s, qnw_col, knw_col) = _fused_call("attn_fwd", N_CH, [
        _retention_fwd(proj, ret_norm_w.reshape(1, RET_V), ret_tables, w_out),
        _swa_fwd(proj, q_norm_w.reshape(1, SWA_D), k_norm_w.reshape(1, SWA_D), sinks, relb, bucket_t)])
    sse, dy, dmix_r, dmix_s, gwo = _out_proj(mix_r, mix_s, wo.reshape(D, D), x2, target)

    (d_ret, drw_acc), (d_swa, dbias, dsink_acc, dqw_acc, dkw_acc), (g_out,) = _fused_call("attn_bwd", N_CH + 1, [
        _retention_bwd(proj, ro, states, dmix_r, rnw_col, ret_tables),
        _swa_bwd(proj, so, dmix_s, qnw_col, knw_col, sinks, relb, bucket_t),
        _w_out_reduce_scatter(gwo.reshape(N_DEV, WOUT_BLK, D))])
    grad_x, g_in_t, sum_a, sum_b = _in_proj_bwd_rs(d_ret, d_swa, h, wt, x2, nw, dy, sse,
                                                   (dbias, bucket_t, dsink_acc, dqw_acc, dkw_acc, drw_acc))

    small_w = _small_rows(norm_w, ret_norm_w, q_norm_w, k_norm_w, sinks, rel_bias)
    small_m = _small_rows(m_norm_w, m_ret_norm_w, m_q_norm_w, m_k_norm_w, m_sinks, m_rel_bias)
    small_v = _small_rows(v_norm_w, v_ret_norm_w, v_q_norm_w, v_k_norm_w, v_sinks, v_rel_bias)
    big, small, loss, grad_x = _adamw_all(grad_x, g_in_t, g_out, sum_a, sum_b, w_in.T, m_w_in.T, v_w_in.T, w_out, m_w_out, v_w_out,
                                  small_w, small_m, small_v)

    def leaves(i):
        a = _small_leaves(small[i])
        return (a[0], big[i].T, a[1], a[2], a[3], a[4], a[5], big[4 + i])

    return (loss.reshape(()), grad_x.reshape(1, T, D), *leaves(0), *leaves(1), *leaves(2), *leaves(3))
```

```python
from typing import Callable, NamedTuple

import numpy as np
import jax
import jax.numpy as jnp
from jax import lax
from jax.experimental import pallas as pl
from jax.experimental.pallas import tpu as pltpu

F32 = jnp.float32
BF16 = jnp.bfloat16
MESH = pl.DeviceIdType.MESH

T = 2048
D = 1024
D_IN = 2816
N_DEV = 8
WIN_BLK = D_IN // N_DEV
WOUT_BLK = D // N_DEV
CH = 128
N_CH = T // CH
RET_H, RET_DK, RET_DV = 4, 64, 128
RET_QK = RET_H * RET_DK
RET_V = RET_H * RET_DV
SWA_H, SWA_KV, SWA_D, SWA_G = 8, 2, 64, 4
N_BUCKETS = 32
NORM_EPS = 1e-6
GN_EPS = 1e-5
NEG_INF = -1e30
PIECES = (256, 256, 512, 512, 512, 128, 128, 512)
OFFS = tuple(int(v) for v in np.cumsum((0,) + PIECES))
RET_W = OFFS[4]
SWA_W = D_IN - RET_W
SWA_OFFS = tuple(o - RET_W for o in OFFS[4:])
TM = 512

ADAM_LR, ADAM_B1, ADAM_B2, ADAM_EPS, ADAM_WD, ADAM_STEP = 0.001, 0.9, 0.999, 1e-08, 0.01, 10

VMEM_LIMIT = 56 * 1024 * 1024
VMEM_WHOLE = 63 * 1024 * 1024


def _cparams(n_grid=0, vmem=VMEM_LIMIT):
    sem = ("arbitrary",) * n_grid if n_grid else None
    return pltpu.CompilerParams(dimension_semantics=sem, vmem_limit_bytes=vmem)


class _Part(NamedTuple):
    step: Callable
    inputs: list
    in_specs: list
    out_specs: list
    out_shape: list
    scratch_shapes: list


def _fused_call(name, n_steps, parts):
    n_in = [len(p.inputs) for p in parts]
    n_out = [len(p.out_shape) for p in parts]
    n_scr = [len(p.scratch_shapes) for p in parts]

    def body(*refs):
        ins, outs, scr = refs[:sum(n_in)], refs[sum(n_in):sum(n_in) + sum(n_out)], refs[sum(n_in) + sum(n_out):]
        for i, p in enumerate(parts):
            take = lambda seq, counts: seq[sum(counts[:i]):sum(counts[:i + 1])]
            p.step(*take(ins, n_in), *take(outs, n_out), *take(scr, n_scr))

    flat = lambda field: [v for p in parts for v in getattr(p, field)]
    outs = pl.pallas_call(
        body, name=name, grid=(n_steps,), in_specs=flat("in_specs"), out_specs=tuple(flat("out_specs")),
        out_shape=tuple(flat("out_shape")), scratch_shapes=flat("scratch_shapes"), compiler_params=_cparams(1),
    )(*flat("inputs"))
    return [list(outs[sum(n_out[:i]):sum(n_out[:i + 1])]) for i in range(len(parts))]


def _dot(a, b):
    return jnp.dot(a, b, preferred_element_type=F32)


def _dot_nt(a, b):
    return lax.dot_general(a, b, (((1,), (1,)), ((), ())), preferred_element_type=F32)


def _bf(a):
    return a.astype(BF16)


def _sigmoid(x):
    return 1.0 / (1.0 + jnp.exp(-x))


def _adamw(w, g, m, v):
    m = ADAM_B1 * m + (1.0 - ADAM_B1) * g
    v = ADAM_B2 * v + (1.0 - ADAM_B2) * (g * g)
    m_hat = m / (1.0 - ADAM_B1 ** ADAM_STEP)
    v_hat = v / (1.0 - ADAM_B2 ** ADAM_STEP)
    delta = -ADAM_LR * (m_hat / (jnp.sqrt(v_hat) + ADAM_EPS) + ADAM_WD * w)
    return delta, m, v


def _rotary_tables_t():
    half = RET_DK // 2
    inv_freq = np.float32(10000.0) ** (-np.arange(half, dtype=np.float32) / np.float32(half))
    ang = inv_freq[:, None] * np.arange(T, dtype=np.float32)[None, :]
    cos, sin = np.cos(ang).astype(np.float32), np.sin(ang).astype(np.float32)
    cos64 = np.concatenate([cos, cos], axis=0)
    sin64 = np.concatenate([-sin, sin], axis=0)
    return np.tile(cos64, (RET_H, 1)), np.tile(sin64, (RET_H, 1))


def _retention_tables_t():
    gamma = (1.0 - np.exp2(-5.0 - np.arange(RET_H, dtype=np.float32))).astype(np.float32)
    log_g = np.log(gamma).astype(np.float32)
    i = np.arange(CH, dtype=np.float32)
    diff = i[None, :] - i[:, None]
    decay = np.where(diff >= 0, np.exp(log_g[:, None, None] * np.maximum(diff, 0.0)), 0.0).astype(np.float32)
    decay_all = np.concatenate(list(decay), axis=1)
    zeta = np.exp(log_g[:, None] * (CH - 1.0 - i)).astype(np.float32)
    zeta_tab = np.repeat(zeta.T, RET_DK, axis=1)
    xi = np.exp(log_g[:, None] * (i + 1.0)).astype(np.float32)
    xi_tab = np.repeat(xi, RET_DK, axis=0)
    chunk_decay = np.exp(log_g * np.float32(CH)).astype(np.float32)
    row_head = np.arange(RET_V)[:, None] // RET_DV
    col_head = np.arange(RET_QK)[None, :] // RET_DK
    state_mask = (row_head == col_head).astype(np.float32)
    state_decay = (state_mask * chunk_decay[row_head]).astype(np.float32)
    q_mask = (np.arange(RET_QK)[:, None] // RET_DK == np.arange(RET_H * CH)[None, :] // CH).astype(np.float32)
    return (np.ascontiguousarray(decay_all), np.ascontiguousarray(zeta_tab), np.ascontiguousarray(xi_tab),
            state_mask, state_decay, q_mask)


def _bucket_table_t():
    qi = np.arange(CH)[:, None]
    kj = np.arange(2 * CH)[None, :]
    dist = qi + CH - kj
    n = np.maximum(dist, 0)
    max_exact = N_BUCKETS // 2
    nf = np.maximum(n, 1).astype(np.float32)
    large = max_exact + (np.log(nf / np.float32(max_exact)) / np.float32(np.log(CH / max_exact))
                         * np.float32(N_BUCKETS - max_exact)).astype(np.int32)
    large = np.minimum(large, N_BUCKETS - 1)
    bucket = np.where(n < max_exact, n, large)
    return np.ascontiguousarray(np.where((dist >= 0) & (dist < CH), bucket, -1).astype(np.int32).T)


def _mesh_pos():
    return lax.axis_index("x"), lax.axis_index("y"), lax.axis_index("c")


def _blk(px, py, pc):
    return 4 * px + 2 * py + pc


def _in_proj_gather(x, norm_w, w_in_t):
    tchunk = 512

    def body(x_any, nw_any, win_any, proj_ref, wt_out, h_out, wt_ref, h_ref, stage, x_ref, nw_ref, win_ref,
             send_sems, recv_sems, out_sems, wb_sems, h_sem, in_sems):
        n_tiles = T // TM
        loads = [pltpu.make_async_copy(win_any, win_ref, in_sems.at[0]),
                 pltpu.make_async_copy(nw_any, nw_ref, in_sems.at[1])]
        loads += [pltpu.make_async_copy(x_any.at[pl.ds(i * TM, TM), :], x_ref.at[pl.ds(i * TM, TM), :], in_sems.at[2 + i])
                  for i in range(n_tiles)]
        for cp in loads:
            cp.start()
        loads[0].wait()
        x, y, c = _mesh_pos()
        me = _blk(x, y, c)
        sibling = (x, y, 1 - c)
        nbr_a, nbr_b, diag = (x ^ (1 - c), y ^ c), (x ^ c, y ^ (1 - c)), (1 - x, 1 - y)

        def copy(k, b, to):
            return pltpu.make_async_remote_copy(src_ref=wt_ref.at[b], dst_ref=wt_ref.at[b], send_sem=send_sems.at[k],
                                                recv_sem=recv_sems.at[k], device_id=to, device_id_type=MESH)

        wt_ref[me] = _bf(win_ref[...])
        first = [copy(0, me, sibling), copy(1, me, (*nbr_a, c))]
        for cp in first:
            cp.start()
        to_b = copy(2, me, (*nbr_b, c))

        loads[1].wait()
        nw = nw_ref[...]
        w_mine = wt_ref[me]
        for i in range(n_tiles):
            rows = slice(i * TM, (i + 1) * TM)
            loads[2 + i].wait()
            xv = x_ref[rows, :]
            r = lax.rsqrt(jnp.mean(xv * xv, axis=-1, keepdims=True) + NORM_EPS)
            h_ref[rows, :] = _bf(xv * r * nw)
            stage[0, :, rows] = _dot_nt(w_mine, h_ref[rows, :])
        keep = [pltpu.make_async_copy(h_ref, h_out, h_sem)]
        keep[0].start()

        writes = []

        def project(b, staged=False):
            k = len(writes)
            if k >= 2:
                writes[k - 2].wait()
            keep.append(pltpu.make_async_copy(wt_ref.at[b], wt_out.at[b], wb_sems.at[k]))
            keep[-1].start()
            if not staged:
                w = wt_ref[b]
                for t in range(T // tchunk):
                    cols = slice(t * tchunk, (t + 1) * tchunk)
                    stage[k % 2, :, cols] = _dot_nt(w, h_ref[cols, :])
            cp = pltpu.make_async_copy(stage.at[k % 2], proj_ref.at[b], out_sems.at[k % 2])
            cp.start()
            writes.append(cp)

        here = (x, y, c)
        project(me, staged=True)
        copy(0, _blk(x, y, 1 - c), here).wait_recv()
        project(_blk(x, y, 1 - c))
        to_b.start()
        passed = [to_b]

        def landed(k, chip, relay_to=None):
            copy(k, _blk(*chip, c), here).wait_recv()
            out = [copy({1: 4, 2: 5, 3: 6}[k], _blk(*chip, c), sibling)]
            if relay_to is not None:
                out.append(copy(3, _blk(*chip, c), (*relay_to, c)))
            for cp in out:
                cp.start()
            passed.extend(out)
            project(_blk(*chip, c))

        def from_sibling(k, chip):
            copy(k, _blk(*chip, 1 - c), here).wait_recv()
            project(_blk(*chip, 1 - c))

        landed(1, nbr_a, relay_to=nbr_b)
        from_sibling(4, nbr_b)
        landed(2, nbr_b)
        from_sibling(5, nbr_a)
        landed(3, diag)
        from_sibling(6, diag)
        for cp in writes[-2:] + keep:
            cp.wait()
        for cp in first + passed:
            cp.wait_send()

    vm = pl.BlockSpec(memory_space=pltpu.VMEM)
    hbm = pl.BlockSpec(memory_space=pltpu.HBM)
    dma = pltpu.SemaphoreType.DMA
    proj, wt, h = pl.pallas_call(
        body, name="in_proj_gather",
        out_shape=(jax.ShapeDtypeStruct((N_DEV, WIN_BLK, T), F32), jax.ShapeDtypeStruct((N_DEV, WIN_BLK, D), BF16),
                   jax.ShapeDtypeStruct((T, D), BF16)),
        in_specs=[hbm, hbm, hbm], out_specs=(hbm, hbm, hbm),
        scratch_shapes=[pltpu.VMEM((N_DEV, WIN_BLK, D), BF16), pltpu.VMEM((T, D), BF16), pltpu.VMEM((2, WIN_BLK, T), F32),
                        pltpu.VMEM((T, D), F32), pltpu.VMEM((1, D), F32), pltpu.VMEM((WIN_BLK, D), F32),
                        dma((7,)), dma((7,)), dma((2,)), dma((N_DEV,)), dma, dma((2 + T // TM,))],
        compiler_params=_cparams(vmem=VMEM_WHOLE),
    )(x, norm_w, w_in_t)
    return proj.reshape(D_IN, T), wt.reshape(D_IN, D), h


def _gather_w_out(w_out_ref, own_ref, wo_ref, local_sem, send_sems, recv_sems, phase):
    x, y, c = _mesh_pos()
    me = _blk(x, y, c)
    sibling = (x, y, 1 - c)
    chips = [(1 - x, y), (x, 1 - y), (1 - x, 1 - y)]

    def copy(k, b, to, src=None):
        return pltpu.make_async_remote_copy(src_ref=wo_ref.at[b] if src is None else src, dst_ref=wo_ref.at[b],
                                            send_sem=send_sems.at[k], recv_sem=recv_sems.at[k], device_id=to,
                                            device_id_type=MESH)

    local = pltpu.make_async_copy(own_ref, wo_ref.at[me], local_sem)
    first = [copy(0, me, sibling, own_ref)] + [copy(1 + j, me, (*chip, c), own_ref) for j, chip in enumerate(chips)]
    passed = [copy(4 + j, _blk(*chip, c), sibling) for j, chip in enumerate(chips)]
    if phase == 0:
        own_ref[...] = _bf(w_out_ref[...])
        local.start()
        for cp in first:
            cp.start()
    elif phase == 1:
        for j, chip in enumerate(chips):
            copy(1 + j, _blk(*chip, c), (x, y, c)).wait_recv()
            passed[j].start()
    else:
        local.wait()
        copy(0, _blk(x, y, 1 - c), (x, y, c)).wait_recv()
        for j, chip in enumerate(chips):
            copy(4 + j, _blk(*chip, 1 - c), (x, y, c)).wait_recv()
        for cp in first + passed:
            cp.wait_send()


def _proj_piece(i, block_of):
    rows = PIECES[i]
    assert OFFS[i] % rows == 0
    return pl.BlockSpec((rows, CH), lambda n: (OFFS[i] // rows, block_of(n)))


def _proj_gate_halves(block_of):
    return [pl.BlockSpec((256, CH), lambda n, j=j: (OFFS[7] // 256 + j, block_of(n))) for j in range(2)]


def _swap_halves_t(t):
    half = RET_DK // 2
    parts = []
    for h in range(RET_H):
        parts += [t[h * RET_DK + half:(h + 1) * RET_DK], t[h * RET_DK:h * RET_DK + half]]
    return jnp.concatenate(parts, axis=0)


def _rotate(t, cos, sin):
    return t * cos + _swap_halves_t(t) * sin


def _group_norm_t(o):
    mu = jnp.mean(o, axis=0, keepdims=True)
    var = jnp.mean((o - mu) * (o - mu), axis=0, keepdims=True)
    rstd = lax.rsqrt(var + GN_EPS)
    return (o - mu) * rstd, rstd


def _retention_scores_t(q_t, k_rm_b, q_mask, decay_all):
    q_heads = _bf(jnp.concatenate([q_t] * RET_H, axis=1) * q_mask)
    return _dot(k_rm_b, q_heads) * decay_all


def _row_to_col(row):
    n = row.shape[1]
    diagonal = lax.broadcasted_iota(jnp.int32, (n, n), 0) == lax.broadcasted_iota(jnp.int32, (n, n), 1)
    return jnp.sum(jnp.where(diagonal, row, 0.0), axis=1, keepdims=True)


def _retention_fwd(proj, ret_norm_w, tables, w_out_blk):
    cos_t, sin_t, decay_all, zeta_tab, xi_tab, state_mask, state_decay, q_mask = tables

    def body(rq_ref, rk_ref, rv_ref, rg_ref, w_row, cos_ref, sin_ref, decay_ref, zeta_ref, xi_ref, smask_ref,
             sdecay_ref, qmask_ref, wout_ref, ro_ref, mix_ref, st_ref, wo_ref, w_ref,
             state, wo_own, wo_local, wo_send, wo_recv):
        n = pl.program_id(0)

        @pl.when(n == 0)
        def _():
            w_ref[...] = _row_to_col(w_row[...])
            state[...] = jnp.zeros_like(state)
            _gather_w_out(wout_ref, wo_own, wo_ref, wo_local, wo_send, wo_recv, phase=0)

        @pl.when(n == N_CH // 2)
        def _():
            _gather_w_out(wout_ref, wo_own, wo_ref, wo_local, wo_send, wo_recv, phase=1)

        @pl.when(n == N_CH - 1)
        def _():
            _gather_w_out(wout_ref, wo_own, wo_ref, wo_local, wo_send, wo_recv, phase=2)

        cos, sin = cos_ref[...], sin_ref[...]
        q_t = _rotate(rq_ref[...], cos, sin)
        k_t = _rotate(rk_ref[...], cos, sin) * (RET_DK ** -0.5)
        k_rm = k_t.T
        v_b = _bf(rv_ref[...])
        m_b = _bf(state[...])
        st_ref[0] = m_b
        scores_b = _bf(_retention_scores_t(q_t, _bf(k_rm), qmask_ref[...], decay_ref[...]))
        cross = _dot(m_b, _bf(q_t * xi_ref[...]))
        state[...] = state[...] * sdecay_ref[...] + _dot(v_b, _bf(k_rm * zeta_ref[...])) * smask_ref[...]
        heads = range(RET_H)
        rows = [slice(h * RET_DV, (h + 1) * RET_DV) for h in heads]
        o = [_dot(v_b[rows[h]], scores_b[:, h * CH:(h + 1) * CH]) + cross[rows[h]] for h in heads]
        for h in heads:
            rn, _ = _group_norm_t(o[h])
            g = rg_ref[rows[h], :]
            ro_ref[rows[h], :] = o[h]
            mix_ref[rows[h], :] = rn * w_ref[rows[h], :] * (g * _sigmoid(g))

    col = lambda w: pl.BlockSpec((w, CH), lambda n: (0, n))
    const = lambda shape: pl.BlockSpec(shape, lambda n: (0,) * len(shape))
    cur = lambda n: n
    return _Part(
        body,
        inputs=[proj, proj, proj, proj, ret_norm_w, cos_t, sin_t, decay_all, zeta_tab, xi_tab, state_mask, state_decay,
                q_mask, w_out_blk],
        in_specs=[_proj_piece(i, cur) for i in range(4)] + [
            const((1, RET_V)), col(RET_QK), col(RET_QK),
            const(decay_all.shape), const(zeta_tab.shape), const(xi_tab.shape), const(state_mask.shape),
            const(state_decay.shape), const(q_mask.shape), const((WOUT_BLK, D))],
        out_specs=[col(RET_V), col(RET_V), pl.BlockSpec((1, RET_V, RET_QK), lambda n: (n, 0, 0)),
                   pl.BlockSpec(memory_space=pl.ANY), const((RET_V, 1))],
        out_shape=[jax.ShapeDtypeStruct((RET_V, T), F32), jax.ShapeDtypeStruct((RET_V, T), F32),
                   jax.ShapeDtypeStruct((N_CH, RET_V, RET_QK), BF16), jax.ShapeDtypeStruct((N_DEV, WOUT_BLK, D), BF16),
                   jax.ShapeDtypeStruct((RET_V, 1), F32)],
        scratch_shapes=[pltpu.VMEM((RET_V, RET_QK), F32), pltpu.VMEM((WOUT_BLK, D), BF16), pltpu.SemaphoreType.DMA,
                        pltpu.SemaphoreType.DMA((N_DEV - 1,)), pltpu.SemaphoreType.DMA((N_DEV - 1,))])


def _retention_bwd(proj, ro, states, dmix, ret_norm_w, tables):
    cos_t, sin_t, decay_all, zeta_tab, xi_tab, state_mask, state_decay, q_mask = tables

    def body(*refs):
        @pl.when(pl.program_id(0) < N_CH)
        def _():
            chunk(*refs)

    def chunk(rq_ref, rk_ref, rv_ref, rg_ref, ro_ref, st_ref, dm_ref, w_ref, cos_ref, sin_ref, decay_ref, zeta_ref,
              xi_ref, smask_ref, sdecay_ref, qmask_ref, d_ref, dw_ref, gstate):
        i = pl.program_id(0)

        @pl.when(i == 0)
        def _():
            gstate[...] = jnp.zeros_like(gstate)
            dw_ref[...] = jnp.zeros_like(dw_ref)

        cos, sin = cos_ref[...], sin_ref[...]
        q_t = _rotate(rq_ref[...], cos, sin)
        k_t = _rotate(rk_ref[...], cos, sin) * (RET_DK ** -0.5)
        q_b, k_b = _bf(q_t), _bf(k_t)
        k_rm = k_t.T
        kz_b = _bf(k_rm * zeta_ref[...])
        qx_b = _bf(q_t * xi_ref[...])
        v_t = rv_ref[...]
        v_b = _bf(v_t)
        v_rm_b = _bf(v_t.T)
        decay = decay_ref[...]
        scores_b = _bf(_retention_scores_t(q_t, _bf(k_rm), qmask_ref[...], decay))
        heads = range(RET_H)
        rows = [slice(h * RET_DV, (h + 1) * RET_DV) for h in heads]
        qk_rows = [slice(h * RET_DK, (h + 1) * RET_DK) for h in heads]
        lanes = [slice(h * CH, (h + 1) * CH) for h in heads]
        do = []
        for h in heads:
            g, w, dm = rg_ref[rows[h], :], w_ref[rows[h], :], dm_ref[rows[h], :]
            rn, rstd = _group_norm_t(ro_ref[rows[h], :])
            sig = _sigmoid(g)
            silu = g * sig
            d_ref[OFFS[3] + h * RET_DV:OFFS[3] + (h + 1) * RET_DV, :] = _bf(dm * rn * w * (sig * (1.0 + g * (1.0 - sig))))
            dw_ref[rows[h], :] += dm * silu * rn
            drn = dm * silu * w
            do.append(rstd * (drn - jnp.mean(drn, axis=0, keepdims=True)
                              - rn * jnp.mean(drn * rn, axis=0, keepdims=True)))
        do_b = _bf(jnp.concatenate(do, axis=0))
        m_b = st_ref[0]
        g_all = gstate[...]
        g_b = _bf(g_all)
        dscores_b = [_bf(_dot(v_rm_b[:, rows[h]], do_b[rows[h]]) * decay[:, lanes[h]]) for h in heads]
        dq_cross = lax.dot_general(m_b, do_b, (((0,), (0,)), ((), ())), preferred_element_type=F32)
        dkz = _dot(v_rm_b, g_b)
        dv_cross = _dot_nt(g_b, kz_b)
        gstate[...] = g_all * sdecay_ref[...] + _dot_nt(do_b, qx_b) * smask_ref[...]
        dq = jnp.concatenate([_dot(k_b[qk_rows[h]], dscores_b[h]) for h in heads], axis=0) + dq_cross * xi_ref[...]
        dk = (jnp.concatenate([_dot_nt(q_b[qk_rows[h]], dscores_b[h]) for h in heads], axis=0)
              + (dkz * zeta_ref[...]).T) * (RET_DK ** -0.5)
        for h in heads:
            d_ref[OFFS[2] + h * RET_DV:OFFS[2] + (h + 1) * RET_DV, :] = _bf(
                _dot_nt(do_b[rows[h]], scores_b[:, lanes[h]]) + dv_cross[rows[h]])
        d_ref[OFFS[0]:OFFS[1], :] = _bf(dq * cos + _swap_halves_t(dq * sin))
        d_ref[OFFS[1]:OFFS[2], :] = _bf(dk * cos + _swap_halves_t(dk * sin))

    chunk_of = lambda i: N_CH - 1 - jnp.minimum(i, N_CH - 1)
    col = lambda w: pl.BlockSpec((w, CH), lambda i: (0, chunk_of(i)))
    const = lambda shape: pl.BlockSpec(shape, lambda i: (0,) * len(shape))
    return _Part(
        body,
        inputs=[proj, proj, proj, proj, ro, states, dmix, ret_norm_w, cos_t, sin_t, decay_all, zeta_tab, xi_tab,
                state_mask, state_decay, q_mask],
        in_specs=[_proj_piece(j, chunk_of) for j in range(4)] + [
                  col(RET_V),
                  pl.BlockSpec((1, RET_V, RET_QK), lambda i: (chunk_of(i), 0, 0)), col(RET_V),
                  const((RET_V, 1)), col(RET_QK), col(RET_QK),
                  const(decay_all.shape), const(zeta_tab.shape), const(xi_tab.shape), const(state_mask.shape),
                  const(state_decay.shape), const(q_mask.shape)],
        out_specs=[col(RET_W), const((RET_V, CH))],
        out_shape=[jax.ShapeDtypeStruct((RET_W, T), BF16), jax.ShapeDtypeStruct((RET_V, CH), F32)],
        scratch_shapes=[pltpu.VMEM((RET_V, RET_QK), F32)])


HQ_LANES = SWA_H * CH


def _head_lanes(hq):
    return slice(hq * CH, (hq + 1) * CH)


def _build_bias_t(bucket_ref, relb_ref, sinks_ref, bias_ref, sink_row):
    bt = bucket_ref[...]
    first = lax.broadcasted_iota(jnp.int32, bt.shape, 0) < CH
    for hq in range(SWA_H):
        b = jnp.full(bt.shape, NEG_INF, F32)
        for bk in range(N_BUCKETS):
            b = jnp.where(bt == bk, relb_ref[hq, bk], b)
        bias_ref[0, :, _head_lanes(hq)] = b
        bias_ref[1, :, _head_lanes(hq)] = jnp.where(first, NEG_INF, b)
        sink_row[:, _head_lanes(hq)] = jnp.full((1, CH), sinks_ref[hq], F32)


def _rms_t(t, w_col):
    r = lax.rsqrt(jnp.mean(t * t, axis=0, keepdims=True) + NORM_EPS)
    return t * r * w_col, r


def _rms_t_bwd(dn, t, r, w_col):
    u = dn * w_col
    return r * u - t * (r * r * r) * jnp.mean(u * t, axis=0, keepdims=True), dn * t * r


def _norm_kv(k_t, kw):
    return jnp.concatenate([_rms_t(k_t[g * SWA_D:(g + 1) * SWA_D], kw)[0] for g in range(SWA_KV)], axis=0)


def _kv_slot(a, kv):
    z = jnp.zeros_like(a)
    return jnp.concatenate([a, z] if kv == 0 else [z, a], axis=0)


def _softmax_t(s, sink):
    m = jnp.maximum(jnp.max(s, axis=0, keepdims=True), sink)
    p = jnp.exp(s - m)
    e_sink = jnp.exp(sink - m)
    inv = 1.0 / (jnp.sum(p, axis=0, keepdims=True) + e_sink)
    return p * inv, e_sink * inv


def _gate_rows(sg_halves, hq):
    per_half = SWA_H // 2
    return sg_halves[hq // per_half][(hq % per_half) * SWA_D:(hq % per_half + 1) * SWA_D, :]


def _swa_fwd(proj, q_norm_w, k_norm_w, sinks, rel_bias_t, bucket_t):
    def body(sq_ref, skp_ref, skc_ref, svp_ref, svc_ref, sg_lo, sg_hi, qw_row, kw_row, sinks_ref, relb_ref, bucket_ref,
             so_ref, mix_ref, qw_ref, kw_ref, bias_ref, sink_row):
        n = pl.program_id(0)

        @pl.when(n == 0)
        def _():
            qw_ref[...] = _row_to_col(qw_row[...])
            kw_ref[...] = _row_to_col(kw_row[...])
            _build_bias_t(bucket_ref, relb_ref, sinks_ref, bias_ref, sink_row)

        var = (n == 0).astype(jnp.int32)
        qw, kw = qw_ref[...], kw_ref[...]
        kn_band = jnp.concatenate([_norm_kv(skp_ref[...], kw), _norm_kv(skc_ref[...], kw)], axis=1)
        kb_rm = _bf(kn_band.T)
        vband = _bf(jnp.concatenate([svp_ref[...], svc_ref[...]], axis=1))
        q_all = jnp.concatenate(
            [_bf(_kv_slot(_rms_t(sq_ref[hq * SWA_D:(hq + 1) * SWA_D, :], qw)[0] * (SWA_D ** -0.5), hq // SWA_G))
             for hq in range(SWA_H)], axis=1)
        probs, _ = _softmax_t(_dot(kb_rm, q_all) + bias_ref[var], sink_row[...])
        probs_b = _bf(probs)
        for kv in range(SWA_KV):
            o = _dot(vband[kv * SWA_D:(kv + 1) * SWA_D], probs_b[:, kv * SWA_G * CH:(kv + 1) * SWA_G * CH])
            for j in range(SWA_G):
                rows = slice((kv * SWA_G + j) * SWA_D, (kv * SWA_G + j + 1) * SWA_D)
                oh = o[:, j * CH:(j + 1) * CH]
                g = _gate_rows((sg_lo, sg_hi), kv * SWA_G + j)
                so_ref[rows, :] = oh
                mix_ref[rows, :] = oh * (g * _sigmoid(g))

    col = lambda w: pl.BlockSpec((w, CH), lambda n: (0, n))
    const = lambda shape: pl.BlockSpec(shape, lambda n: (0,) * len(shape))
    smem = pl.BlockSpec(memory_space=pltpu.SMEM)
    cur = lambda n: n
    prev = lambda n: jnp.maximum(n - 1, 0)
    return _Part(
        body,
        inputs=[proj, proj, proj, proj, proj, proj, proj, q_norm_w, k_norm_w, sinks, rel_bias_t, bucket_t],
        in_specs=[_proj_piece(4, cur), _proj_piece(5, prev), _proj_piece(5, cur), _proj_piece(6, prev),
                  _proj_piece(6, cur)] + _proj_gate_halves(cur) + [
                  const((1, SWA_D)), const((1, SWA_D)), smem, smem, const((2 * CH, CH))],
        out_specs=[col(512), col(512), const((SWA_D, 1)), const((SWA_D, 1))],
        out_shape=[jax.ShapeDtypeStruct((512, T), F32), jax.ShapeDtypeStruct((512, T), F32),
                   jax.ShapeDtypeStruct((SWA_D, 1), F32), jax.ShapeDtypeStruct((SWA_D, 1), F32)],
        scratch_shapes=[pltpu.VMEM((2, 2 * CH, HQ_LANES), F32), pltpu.VMEM((1, HQ_LANES), F32)])


def _swa_bwd(proj, so, dmix, q_norm_w, k_norm_w, sinks, rel_bias_t, bucket_t):
    def body(sq_ref, skp_ref, skc_ref, svp_ref, svc_ref, sg_lo, sg_hi, so_ref, dm_ref, qw_ref, kw_ref, sinks_ref,
             relb_ref, bucket_ref, d_ref, dbias_ref, dsink_ref, dqw_ref, dkw_ref,
             bias_ref, sink_row, band_dk, band_dv, carry_dk, carry_dv, hold_dq, hold_dg):
        n = pl.program_id(0)
        slot = n % 2

        @pl.when(n == 0)
        def _():
            _build_bias_t(bucket_ref, relb_ref, sinks_ref, bias_ref, sink_row)
            for ref in (dbias_ref, dsink_ref, dqw_ref, dkw_ref, carry_dk, carry_dv):
                ref[...] = jnp.zeros_like(ref)

        qw, kw = qw_ref[...], kw_ref[...]

        @pl.when(n < N_CH)
        def _():
            var = (n == 0).astype(jnp.int32)
            kn_band = jnp.concatenate([_norm_kv(skp_ref[...], kw), _norm_kv(skc_ref[...], kw)], axis=1)
            kb_rm = _bf(kn_band.T)
            kn_band_s = _bf(kn_band * (SWA_D ** -0.5))
            vband_f = jnp.concatenate([svp_ref[...], svc_ref[...]], axis=1)
            vb_rm = _bf(vband_f.T)
            q_raw, q_rstd, qs_b, do_b = [], [], [], []
            for hq in range(SWA_H):
                rows = slice(hq * SWA_D, (hq + 1) * SWA_D)
                q_t = sq_ref[rows, :]
                qn, rq = _rms_t(q_t, qw)
                g = _gate_rows((sg_lo, sg_hi), hq)
                sig = _sigmoid(g)
                dm = dm_ref[rows, :]
                hold_dg[slot, rows, :] = _bf(dm * so_ref[rows, :] * (sig * (1.0 + g * (1.0 - sig))))
                q_raw.append(q_t)
                q_rstd.append(rq)
                qs_b.append(_bf(qn * (SWA_D ** -0.5)))
                do_b.append(_bf(dm * (g * sig)))
            q_all = jnp.concatenate([_kv_slot(qs_b[hq], hq // SWA_G) for hq in range(SWA_H)], axis=1)
            do_all = jnp.concatenate([_kv_slot(do_b[hq], hq // SWA_G) for hq in range(SWA_H)], axis=1)
            probs, p_sink = _softmax_t(_dot(kb_rm, q_all) + bias_ref[var], sink_row[...])
            dprobs = _dot(vb_rm, do_all)
            t = jnp.sum(probs * dprobs, axis=0, keepdims=True)
            dlog = probs * (dprobs - t)
            dsink_ref[...] += -(p_sink * t)
            dbias_ref[...] += dlog
            dlog_b, probs_b = _bf(dlog), _bf(probs)
            dkn, dvv = [], []
            for kv in range(SWA_KV):
                heads = range(kv * SWA_G, (kv + 1) * SWA_G)
                lanes = slice(kv * SWA_G * CH, (kv + 1) * SWA_G * CH)
                dvv.append(_dot_nt(jnp.concatenate([do_b[hq] for hq in heads], axis=1), probs_b[:, lanes]))
                dkn.append(_dot_nt(jnp.concatenate([qs_b[hq] for hq in heads], axis=1), dlog_b[:, lanes]))
                dqn = _dot(kn_band_s[kv * SWA_D:(kv + 1) * SWA_D], dlog_b[:, lanes])
                for j, hq in enumerate(heads):
                    dq_t, dqw_terms = _rms_t_bwd(dqn[:, j * CH:(j + 1) * CH], q_raw[hq], q_rstd[hq], qw)
                    hold_dq[slot, hq * SWA_D:(hq + 1) * SWA_D, :] = _bf(dq_t)
                    dqw_ref[...] += dqw_terms
            band_dk[...] = jnp.concatenate(dkn, axis=0)
            band_dv[...] = jnp.concatenate(dvv, axis=0)

        @pl.when(n == N_CH)
        def _():
            band_dk[...] = jnp.zeros_like(band_dk)
            band_dv[...] = jnp.zeros_like(band_dv)

        @pl.when(n >= 1)
        def _():
            dkn_prev = carry_dk[...] + band_dk[:, 0:CH]
            k_t = skp_ref[...]
            for kv in range(SWA_KV):
                rows = slice(kv * SWA_D, (kv + 1) * SWA_D)
                _, rk = _rms_t(k_t[rows], kw)
                dk_t, dkw_terms = _rms_t_bwd(dkn_prev[rows], k_t[rows], rk, kw)
                d_ref[SWA_OFFS[1] + kv * SWA_D:SWA_OFFS[1] + (kv + 1) * SWA_D, :] = _bf(dk_t)
                dkw_ref[...] += dkw_terms
            d_ref[SWA_OFFS[2]:SWA_OFFS[3], :] = _bf(carry_dv[...] + band_dv[:, 0:CH])
            d_ref[SWA_OFFS[0]:SWA_OFFS[1], :] = hold_dq[1 - slot]
            d_ref[SWA_OFFS[3]:SWA_OFFS[4], :] = hold_dg[1 - slot]

        carry_dk[...] = band_dk[:, CH:2 * CH]
        carry_dv[...] = band_dv[:, CH:2 * CH]

    cur_block = lambda n: jnp.minimum(n, N_CH - 1)
    prev_block = lambda n: jnp.maximum(n - 1, 0)
    col = lambda w: pl.BlockSpec((w, CH), lambda n: (0, cur_block(n)))
    prev = lambda w: pl.BlockSpec((w, CH), lambda n: (0, prev_block(n)))
    const = lambda shape: pl.BlockSpec(shape, lambda n: (0,) * len(shape))
    smem = pl.BlockSpec(memory_space=pltpu.SMEM)
    return _Part(
        body,
        inputs=[proj, proj, proj, proj, proj, proj, proj, so, dmix, q_norm_w, k_norm_w, sinks, rel_bias_t, bucket_t],
        in_specs=[_proj_piece(4, cur_block), _proj_piece(5, prev_block), _proj_piece(5, cur_block),
                  _proj_piece(6, prev_block), _proj_piece(6, cur_block)] + _proj_gate_halves(cur_block) + [
                  col(512), col(512), const((SWA_D, 1)), const((SWA_D, 1)), smem, smem, const((2 * CH, CH))],
        out_specs=[prev(SWA_W), const((2 * CH, HQ_LANES)), const((1, HQ_LANES)),
                   const((SWA_D, CH)), const((SWA_D, CH))],
        out_shape=[jax.ShapeDtypeStruct((SWA_W, T), BF16),
                   jax.ShapeDtypeStruct((2 * CH, HQ_LANES), F32), jax.ShapeDtypeStruct((1, HQ_LANES), F32),
                   jax.ShapeDtypeStruct((SWA_D, CH), F32), jax.ShapeDtypeStruct((SWA_D, CH), F32)],
        scratch_shapes=[pltpu.VMEM((2, 2 * CH, HQ_LANES), F32), pltpu.VMEM((1, HQ_LANES), F32),
                        pltpu.VMEM((128, 2 * CH), F32), pltpu.VMEM((128, 2 * CH), F32),
                        pltpu.VMEM((128, CH), F32), pltpu.VMEM((128, CH), F32),
                        pltpu.VMEM((2, 512, CH), BF16), pltpu.VMEM((2, 512, CH), BF16)])


SMALL_GRAD_SHAPES = ((1, RET_V), (1, CH), (1, CH), (1, CH), (SWA_H, N_BUCKETS))


def _finish_small_grads(dbias_ref, bucket_ref, dsink_ref, dqw_ref, dkw_ref, drw_ref, rw_o, qw_o, kw_o, sink_o, relb_o):
    bt = bucket_ref[...]
    col = lax.broadcasted_iota(jnp.int32, (SWA_H, N_BUCKETS), 1)
    lane = lax.broadcasted_iota(jnp.int32, (1, CH), 1)
    sink = jnp.zeros((1, CH), F32)
    flip = (lax.broadcasted_iota(jnp.int32, (CH, CH), 0) + lax.broadcasted_iota(jnp.int32, (CH, CH), 1) == CH - 1)
    reverse = lambda a: jnp.dot(a, flip.astype(F32), precision=lax.Precision.HIGHEST, preferred_element_type=F32)
    per_distance = []
    for hq in range(SWA_H):
        d = reverse(jnp.where(bt >= 0, dbias_ref[:, _head_lanes(hq)], 0.0))
        d = pltpu.roll(d, 0, 1, stride=1, stride_axis=0)
        per_distance.append(jnp.sum(d, axis=0, keepdims=True))
        sink = sink + jnp.where(lane == hq, jnp.sum(dsink_ref[:, _head_lanes(hq)], axis=1, keepdims=True), 0.0)
    per_distance = reverse(jnp.concatenate(per_distance, axis=0))
    bucket_of_distance = bt[CH:CH + 1, :]
    acc = jnp.zeros((SWA_H, N_BUCKETS), F32)
    for bk in range(N_BUCKETS):
        s = jnp.sum(jnp.where(bucket_of_distance == bk, per_distance, 0.0), axis=1, keepdims=True)
        acc = acc + jnp.where(col == bk, s, 0.0)
    relb_o[...] = acc
    sink_o[...] = sink
    for src, dst in ((dqw_ref, qw_o), (dkw_ref, kw_o)):
        padded = jnp.concatenate([src[...], jnp.zeros((CH - SWA_D, CH), F32)], axis=0)
        dst[...] = jnp.sum(padded.T, axis=0, keepdims=True)
    rw_o[...] = jnp.sum(drw_ref[...].T, axis=0, keepdims=True)


def _out_proj(mix_r, mix_s, wo, x, target):
    def body(mr_ref, ms_ref, w_ref, x_ref, t_ref, loss_ref, dy_ref, dmr_ref, dms_ref, gw_ref):
        i = pl.program_id(0)

        @pl.when(i == 0)
        def _():
            loss_ref[...] = jnp.zeros_like(loss_ref)
            gw_ref[...] = jnp.zeros_like(gw_ref)

        mixed = jnp.concatenate([mr_ref[...], ms_ref[...]], axis=0)
        w = w_ref[...]
        err = x_ref[...] + _dot(_bf(mixed.T), w) - t_ref[...]
        loss_ref[...] += jnp.sum(jnp.sum(err * err, axis=1, keepdims=True), axis=0, keepdims=True)
        dy = err * (1.0 / D)
        dy_ref[...] = dy
        dy_b = _bf(dy)
        dmix = _dot_nt(w, dy_b)
        dmr_ref[...] = dmix[0:512]
        dms_ref[...] = dmix[512:D]
        gw_ref[...] += _dot(_bf(mixed), dy_b)

    row = lambda w: pl.BlockSpec((TM, w), lambda i: (i, 0))
    col = lambda w: pl.BlockSpec((w, TM), lambda i: (0, i))
    const = lambda shape: pl.BlockSpec(shape, lambda i: (0,) * len(shape))
    return pl.pallas_call(
        body, name="out_proj", grid=(T // TM,),
        in_specs=[col(512), col(512), const((D, D)), row(D), row(D)],
        out_specs=(const((1, 1)), row(D), col(512), col(512), const((D, D))),
        out_shape=(jax.ShapeDtypeStruct((1, 1), F32), jax.ShapeDtypeStruct((T, D), F32),
                   jax.ShapeDtypeStruct((512, T), F32), jax.ShapeDtypeStruct((512, T), F32),
                   jax.ShapeDtypeStruct((D, D), F32)),
        compiler_params=_cparams(1),
    )(mix_r, mix_s, wo, x, target)


LOSS_ROW = 5
ALL_CHIPS = ((0, 0), (0, 1), (1, 0), (1, 1))


def _w_out_reduce_scatter(gwo):
    def body(gwo_ref, gout_o, own, rcv, snd, got, tot, local_sems, a_send, a_recv, b_send, b_recv):
        k = pl.program_id(0)
        x_, y_, c_ = _mesh_pos()
        sibling = (x_, y_, 1 - c_)
        rel_chips = [(1 - x_, y_), (x_, 1 - y_), (1 - x_, 1 - y_)]

        def to_sibling(j):
            px, py = ALL_CHIPS[j]
            return pltpu.make_async_remote_copy(src_ref=gwo_ref.at[_blk(px, py, 1 - c_)], dst_ref=rcv.at[j],
                                                send_sem=a_send.at[j], recv_sem=a_recv.at[j], device_id=sibling,
                                                device_id_type=MESH)

        def local(j):
            px, py = ALL_CHIPS[j]
            return pltpu.make_async_copy(gwo_ref.at[_blk(px, py, c_)], own.at[j], local_sems.at[j])

        def to_chip(q):
            return pltpu.make_async_remote_copy(src_ref=snd.at[q], dst_ref=got.at[q], send_sem=b_send.at[q],
                                                recv_sem=b_recv.at[q], device_id=(*rel_chips[q], c_), device_id_type=MESH)

        @pl.when(k == 0)
        def _():
            for j in range(len(ALL_CHIPS)):
                to_sibling(j).start()
                local(j).start()

        @pl.when(k == 2)
        def _():
            for j in range(len(ALL_CHIPS)):
                local(j).wait()
                to_sibling(j).wait_recv()
            for q in range(3):
                j = 2 * rel_chips[q][0] + rel_chips[q][1]
                snd[q] = _bf(own[j] + rcv[j])
                to_chip(q).start()
            jm = 2 * x_ + y_
            tot[...] = own[jm] + rcv[jm]

        @pl.when(k == N_CH)
        def _():
            g = tot[...]
            for q in range(3):
                to_chip(q).wait_recv()
                g = g + got[q].astype(F32)
            gout_o[...] = g
            for j in range(len(ALL_CHIPS)):
                to_sibling(j).wait_send()
            for q in range(3):
                to_chip(q).wait_send()

    dma = pltpu.SemaphoreType.DMA
    return _Part(
        body, inputs=[gwo], in_specs=[pl.BlockSpec(memory_space=pl.ANY)],
        out_specs=[pl.BlockSpec((WOUT_BLK, D), lambda k: (0, 0))], out_shape=[jax.ShapeDtypeStruct((WOUT_BLK, D), F32)],
        scratch_shapes=[pltpu.VMEM((4, WOUT_BLK, D), F32), pltpu.VMEM((4, WOUT_BLK, D), F32),
                        pltpu.VMEM((3, WOUT_BLK, D), BF16), pltpu.VMEM((3, WOUT_BLK, D), BF16),
                        pltpu.VMEM((WOUT_BLK, D), F32), dma((4,)), dma((4,)), dma((4,)), dma((3,)), dma((3,))])


def _in_proj_bwd_rs(d_ret, d_swa, h, wt, x, norm_w, dy, sse, small_acc):
    n_blocks, n_tiles = N_DEV, T // TM
    last = n_blocks + n_tiles - 1

    def body(da_any, db_any, da_ref, db_ref, h_any, w_any, x_ref, nw_ref, dy_ref, sse_ref,
             dbias_ref, bucket_ref, dsink_ref, dqw_ref, dkw_ref, drw_ref, gx_ref, gin_o, suma_o, sumb_o,
             stage, d2d_src, mine, rcv, snd, got, tot, tab_a, tab_b, gnw_acc, rw_ref, qw_ref, kw_ref, sk_ref, rb_ref,
             h_ref, w_ref, dp_sems, d2d_send, d2d_recv, ici_send, ici_recv, s_send, s_recv, hw_sems):
        k = pl.program_id(0)
        x_, y_, c_ = _mesh_pos()
        h_load = pltpu.make_async_copy(h_any, h_ref, hw_sems.at[0])
        w_load = pltpu.make_async_copy(w_any, w_ref, hw_sems.at[1])
        me = _blk(x_, y_, c_)
        sibling = (x_, y_, 1 - c_)
        rel_chips = [(1 - x_, 1 - y_), (1 - x_, y_), (x_, 1 - y_), (x_, y_)]

        def block_of_step(s):
            return _blk(*rel_chips[s // 2], 1 - c_ if s % 2 == 0 else c_)

        def fetch(s, wait):
            slot, b = s % 2, block_of_step(s)
            split = RET_W - 4 * WIN_BLK

            def run(src, dst, sem):
                cp = pltpu.make_async_copy(src, dst, sem)
                cp.wait() if wait else cp.start()

            @pl.when(b < 4)
            def _():
                run(da_any.at[pl.ds(pl.multiple_of(b * WIN_BLK, 16), WIN_BLK), :], stage.at[slot], dp_sems.at[slot, 0])

            @pl.when(b == 4)
            def _():
                run(da_any.at[pl.ds(4 * WIN_BLK, split), :], stage.at[slot, pl.ds(0, split), :], dp_sems.at[slot, 0])
                run(db_any.at[pl.ds(0, WIN_BLK - split), :], stage.at[slot, pl.ds(split, WIN_BLK - split), :],
                    dp_sems.at[slot, 1])

            @pl.when(b > 4)
            def _():
                run(db_any.at[pl.ds(pl.multiple_of(b * WIN_BLK - RET_W, 16), WIN_BLK), :], stage.at[slot],
                    dp_sems.at[slot, 0])

        def d2d_copy(r):
            return pltpu.make_async_remote_copy(src_ref=d2d_src, dst_ref=rcv.at[r], send_sem=d2d_send.at[r],
                                                recv_sem=d2d_recv.at[r], device_id=sibling, device_id_type=MESH)

        def ici_copy(r):
            return pltpu.make_async_remote_copy(src_ref=snd.at[r], dst_ref=got.at[r], send_sem=ici_send.at[r],
                                                recv_sem=ici_recv.at[r], device_id=(*rel_chips[r], c_),
                                                device_id_type=MESH)

        def table_copies():
            return [pltpu.make_async_remote_copy(src_ref=tab.at[me], dst_ref=tab.at[me], send_sem=s_send.at[a, p - 1],
                                                 recv_sem=s_recv.at[a, p - 1],
                                                 device_id=(x_ ^ (p >> 2), y_ ^ ((p >> 1) & 1), c_ ^ (p & 1)),
                                                 device_id_type=MESH)
                    for p in range(1, N_DEV) for a, tab in enumerate((tab_a, tab_b))]

        def chip_sum(r):
            d2d_copy(r).wait_recv()
            total = mine[...] + rcv[r]
            if r < 3:
                snd[r] = _bf(total)
                ici_copy(r).start()
            else:
                tot[...] = total

        for s in range(n_blocks):
            @pl.when(k == s)
            def _(s=s):
                r = s // 2
                if s == 0:
                    gnw_acc[...] = jnp.zeros_like(gnw_acc)
                    h_load.start()
                    fetch(0, wait=False)
                    w_load.start()
                fetch(s, wait=True)
                if s + 1 < n_blocks:
                    fetch(s + 1, wait=False)
                if s == 0:
                    h_load.wait()
                gw = _dot(stage[s % 2], h_ref[...])
                if s % 2 == 0:
                    if s >= 2:
                        chip_sum(r - 1)
                        d2d_copy(r - 1).wait_send()
                    d2d_src[...] = gw
                    d2d_copy(r).start()
                else:
                    mine[...] = gw

        @pl.when(k == n_blocks)
        def _():
            chip_sum(3)
            w_load.wait()

        @pl.when(k >= n_blocks)
        def _():
            dp = jnp.concatenate([da_ref[...], db_ref[...]], axis=0)
            xv, nw = x_ref[...], nw_ref[...]
            r = lax.rsqrt(jnp.mean(xv * xv, axis=-1, keepdims=True) + NORM_EPS)
            dh = lax.dot_general(dp, w_ref[...], (((0,), (0,)), ((), ())), preferred_element_type=F32)
            u = dh * nw
            gx_ref[...] = dy_ref[...] + r * u - xv * (r * r * r) * jnp.mean(u * xv, axis=-1, keepdims=True)
            gnw_acc[...] += jnp.sum(dh * (xv * r), axis=0, keepdims=True)

        @pl.when(k == n_blocks)
        def _():
            _finish_small_grads(dbias_ref, bucket_ref, dsink_ref, dqw_ref, dkw_ref, drw_ref,
                                rw_ref, qw_ref, kw_ref, sk_ref, rb_ref)

        @pl.when(k == last)
        def _():
            mine_a, mine_b = tab_a.at[me], tab_b.at[me]
            mine_a[...] = jnp.zeros_like(mine_a)
            mine_b[...] = jnp.zeros_like(mine_b)
            for row, ref in enumerate((gnw_acc, rw_ref, qw_ref, kw_ref, sk_ref)):
                mine_a[row:row + 1, 0:ref.shape[1]] = ref[...]
            mine_a[LOSS_ROW:LOSS_ROW + 1, 0:1] = sse_ref[...]
            mine_b[:, 0:N_BUCKETS] = rb_ref[...]
            tables = table_copies()
            for cp in tables:
                cp.start()
            g_in = tot[...]
            for q in range(3):
                ici_copy(q).wait_recv()
                g_in = g_in + got[q].astype(F32)
            gin_o[...] = g_in
            for cp in tables:
                cp.wait_recv()
            sum_a, sum_b = tab_a[0], tab_b[0]
            for b in range(1, N_DEV):
                sum_a = sum_a + tab_a[b]
                sum_b = sum_b + tab_b[b]
            suma_o[...] = sum_a
            sumb_o[...] = sum_b
            d2d_copy(3).wait_send()
            for q in range(3):
                ici_copy(q).wait_send()
            for cp in tables:
                cp.wait_send()

    tile = lambda k: jnp.maximum(k - n_blocks, 0)
    row = lambda w: pl.BlockSpec((TM, w), lambda k: (tile(k), 0))
    col = lambda w: pl.BlockSpec((w, TM), lambda k: (0, tile(k)))
    const = lambda shape, **kw: pl.BlockSpec(shape, lambda k: (0,) * len(shape), **kw)
    once = dict(pipeline_mode=pl.Buffered(1))
    hbm = pl.BlockSpec(memory_space=pltpu.HBM)
    dma = pltpu.SemaphoreType.DMA
    return pl.pallas_call(
        body, name="in_proj_bwd_rs", grid=(n_blocks + n_tiles,),
        in_specs=[hbm, hbm, col(RET_W), col(SWA_W), hbm, hbm,
                  row(D), const((1, D)), row(D), const((1, 1))]
                 + [const(a.shape) for a in small_acc],
        out_specs=(row(D), const((WIN_BLK, D)), const((8, D)), const((SWA_H, CH))),
        out_shape=(jax.ShapeDtypeStruct((T, D), F32), jax.ShapeDtypeStruct((WIN_BLK, D), F32),
                   jax.ShapeDtypeStruct((8, D), F32), jax.ShapeDtypeStruct((SWA_H, CH), F32)),
        scratch_shapes=[
            pltpu.VMEM((2, WIN_BLK, T), BF16), pltpu.VMEM((WIN_BLK, D), F32),
            pltpu.VMEM((WIN_BLK, D), F32),
            pltpu.VMEM((4, WIN_BLK, D), F32), pltpu.VMEM((3, WIN_BLK, D), BF16),
            pltpu.VMEM((3, WIN_BLK, D), BF16), pltpu.VMEM((WIN_BLK, D), F32),
            pltpu.VMEM((N_DEV, 8, D), F32), pltpu.VMEM((N_DEV, SWA_H, CH), F32),
            pltpu.VMEM((1, D), F32),
        ] + [pltpu.VMEM(s, F32) for s in SMALL_GRAD_SHAPES] + [
            pltpu.VMEM((T, D), BF16), pltpu.VMEM((D_IN, D), BF16),
            dma((2, 2)), dma((4,)), dma((4,)), dma((3,)), dma((3,)), dma((2, 7)), dma((2, 7)), dma((2,)),
        ],
        compiler_params=_cparams(1),
    )(d_ret, d_swa, d_ret, d_swa, h, wt, x, norm_w, dy, sse, *small_acc)


SMALL_SHAPES = ((1, D), (1, 512), (1, SWA_D), (1, SWA_D), (1, SWA_H), (SWA_H, N_BUCKETS))


def _adamw_all(grad_x, g_in_t, g_out, sum_a, sum_b, w_in_t, m_in_t, v_in_t, w_out, m_out, v_out, small_w, small_m, small_v):
    n_small = len(SMALL_SHAPES)
    n_parts = 2

    def body(*refs):
        gx_in = refs[0]
        gin_any, gout_any, suma_any, sumb_any, w_in_ref, m_in_ref, v_in_ref, w_out_ref, m_out_ref, v_out_ref = refs[1:11]
        pos = 11
        sw_refs, sm_refs, sv_refs = (refs[pos + i * n_small:pos + (i + 1) * n_small] for i in range(3))
        pos += 3 * n_small
        g_in_o, d_in_o, nm_in_o, nv_in_o, g_out_o, d_out_o, nm_out_o, nv_out_o = refs[pos:pos + 8]
        pos += 8
        sg_o, sd_o, snm_o, snv_o = (refs[pos + i * n_small:pos + (i + 1) * n_small] for i in range(4))
        pos += 4 * n_small
        loss_o, gx_out, gin_ref, gout_ref, suma_ref, sumb_ref, load_sems = refs[pos:]
        i = pl.program_id(0)
        loads = [pltpu.make_async_copy(src, dst, load_sems.at[j]) for j, (src, dst) in enumerate(
            ((gin_any, gin_ref), (gout_any, gout_ref), (suma_any, suma_ref), (sumb_any, sumb_ref)))]

        @pl.when(i == 0)
        def _():
            for cp in loads:
                cp.start()

        gx_out[...] = gx_in[...]

        @pl.when(i == 0)
        def _():
            for cp in loads:
                cp.wait()

        g = gout_ref[pl.ds(pl.multiple_of(i * (WOUT_BLK // n_parts), 8), WOUT_BLK // n_parts), :]
        d, nm, nv = _adamw(w_out_ref[...], g, m_out_ref[...], v_out_ref[...])
        g_out_o[...], d_out_o[...], nm_out_o[...], nv_out_o[...] = g, d, nm, nv
        g = gin_ref[pl.ds(pl.multiple_of(i * (WIN_BLK // n_parts), 8), WIN_BLK // n_parts), :]
        d, nm, nv = _adamw(w_in_ref[...], g, m_in_ref[...], v_in_ref[...])
        g_in_o[...], d_in_o[...], nm_in_o[...], nv_in_o[...] = g, d, nm, nv

        @pl.when(pl.program_id(0) == 0)
        def _():
            loss_o[...] = suma_ref[LOSS_ROW:LOSS_ROW + 1, 0:1] * (0.5 / D)
            for r, (rows, lanes) in enumerate(SMALL_SHAPES):
                g = suma_ref[r:r + 1, 0:lanes] if rows == 1 else sumb_ref[:, 0:lanes]
                d, nm, nv = _adamw(sw_refs[r][...], g, sm_refs[r][...], sv_refs[r][...])
                sg_o[r][...], sd_o[r][...], snm_o[r][...], snv_o[r][...] = g, d, nm, nv

    part = lambda rows: pl.BlockSpec((rows // n_parts, D), lambda i: (i, 0))
    const = lambda shape: pl.BlockSpec(shape, lambda i: (0,) * len(shape))
    win = jax.ShapeDtypeStruct((WIN_BLK, D), F32)
    wout = jax.ShapeDtypeStruct((WOUT_BLK, D), F32)
    smalls = tuple(jax.ShapeDtypeStruct(s, F32) for s in SMALL_SHAPES)
    small_specs = [const(s) for s in SMALL_SHAPES]
    hbm = pl.BlockSpec(memory_space=pltpu.HBM)
    outs = pl.pallas_call(
        body, name="adamw_all", grid=(n_parts,),
        in_specs=[part(T)] + [hbm] * 4 + [part(WIN_BLK)] * 3 + [part(WOUT_BLK)] * 3 + small_specs * 3,
        out_specs=tuple([part(WIN_BLK)] * 4 + [part(WOUT_BLK)] * 4 + small_specs * 4 + [const((1, 1)), part(T)]),
        out_shape=(win,) * 4 + (wout,) * 4 + smalls * 4 + (jax.ShapeDtypeStruct((1, 1), F32),
                                                           jax.ShapeDtypeStruct(grad_x.shape, F32)),
        scratch_shapes=[pltpu.VMEM((WIN_BLK, D), F32), pltpu.VMEM((WOUT_BLK, D), F32), pltpu.VMEM((8, D), F32),
                        pltpu.VMEM((SWA_H, CH), F32), pltpu.SemaphoreType.DMA((4,))],
        compiler_params=_cparams(1, vmem=VMEM_WHOLE),
    )(grad_x, g_in_t, g_out, sum_a, sum_b, w_in_t, m_in_t, v_in_t, w_out, m_out, v_out, *small_w, *small_m, *small_v)
    big, rest = outs[:8], outs[8:]
    return big, [rest[i * n_small:(i + 1) * n_small] for i in range(4)], rest[4 * n_small], rest[4 * n_small + 1]


def _small_rows(norm_w, ret_norm_w, q_norm_w, k_norm_w, sinks, rel_bias):
    return (norm_w.reshape(1, D), ret_norm_w.reshape(1, 512), q_norm_w.reshape(1, SWA_D), k_norm_w.reshape(1, SWA_D),
            sinks.reshape(1, SWA_H), rel_bias.T)


def _small_leaves(rows):
    return (rows[0].reshape(D), rows[1].reshape(512), rows[2].reshape(SWA_D), rows[3].reshape(SWA_D),
            rows[4].reshape(SWA_H), rows[5].T)


def kernel(x, norm_w, w_in, ret_norm_w, q_norm_w, k_norm_w, sinks, rel_bias, w_out, loss_target, m_norm_w, m_w_in, m_ret_norm_w, m_q_norm_w, m_k_norm_w, m_sinks, m_rel_bias, m_w_out, v_norm_w, v_w_in, v_ret_norm_w, v_q_norm_w, v_k_norm_w, v_sinks, v_rel_bias, v_w_out):
    x2 = x.reshape(T, D)
    target = loss_target.reshape(T, D)
    nw = norm_w.reshape(1, D)
    relb = rel_bias.T
    ret_tables = _rotary_tables_t() + _retention_tables_t()
    bucket_t = _bucket_table_t()

    proj, wt, h = _in_proj_gather(x2, nw, w_in.T)
    (ro, mix_r, states, wo, rnw_col), (so, mix_s, qnw_col, knw_col) = _fused_call("attn_fwd", N_CH, [
        _retention_fwd(proj, ret_norm_w.reshape(1, RET_V), ret_tables, w_out),
        _swa_fwd(proj, q_norm_w.reshape(1, SWA_D), k_norm_w.reshape(1, SWA_D), sinks, relb, bucket_t)])
    sse, dy, dmix_r, dmix_s, gwo = _out_proj(mix_r, mix_s, wo.reshape(D, D), x2, target)

    (d_ret, drw_acc), (d_swa, dbias, dsink_acc, dqw_acc, dkw_acc), (g_out,) = _fused_call("attn_bwd", N_CH + 1, [
        _retention_bwd(proj, ro, states, dmix_r, rnw_col, ret_tables),
        _swa_bwd(proj, so, dmix_s, qnw_col, knw_col, sinks, relb, bucket_t),
        _w_out_reduce_scatter(gwo.reshape(N_DEV, WOUT_BLK, D))])
    grad_x, g_in_t, sum_a, sum_b = _in_proj_bwd_rs(d_ret, d_swa, h, wt, x2, nw, dy, sse,
                                                   (dbias, bucket_t, dsink_acc, dqw_acc, dkw_acc, drw_acc))

    small_w = _small_rows(norm_w, ret_norm_w, q_norm_w, k_norm_w, sinks, rel_bias)
    small_m = _small_rows(m_norm_w, m_ret_norm_w, m_q_norm_w, m_k_norm_w, m_sinks, m_rel_bias)
    small_v = _small_rows(v_norm_w, v_ret_norm_w, v_q_norm_w, v_k_norm_w, v_sinks, v_rel_bias)
    big, small, loss, grad_x = _adamw_all(grad_x, g_in_t, g_out, sum_a, sum_b, w_in.T, m_w_in.T, v_w_in.T, w_out, m_w_out, v_w_out,
                                  small_w, small_m, small_v)

    def leaves(i):
        a = _small_leaves(small[i])
        return (a[0], big[i].T, a[1], a[2], a[3], a[4], a[5], big[4 + i])

    return (loss.reshape(()), grad_x.reshape(1, T, D), *leaves(0), *leaves(1), *leaves(2), *leaves(3))
```

```python
from typing import Callable, NamedTuple

import numpy as np
import jax
import jax.numpy as jnp
from jax import lax
from jax.experimental import pallas as pl
from jax.experimental.pallas import tpu as pltpu

F32 = jnp.float32
BF16 = jnp.bfloat16
MESH = pl.DeviceIdType.MESH

T = 2048
D = 1024
D_IN = 2816
N_DEV = 8
WIN_BLK = D_IN // N_DEV
WOUT_BLK = D // N_DEV
CH = 128
N_CH = T // CH
RET_H, RET_DK, RET_DV = 4, 64, 128
RET_QK = RET_H * RET_DK
RET_V = RET_H * RET_DV
SWA_H, SWA_KV, SWA_D, SWA_G = 8, 2, 64, 4
N_BUCKETS = 32
NORM_EPS = 1e-6
GN_EPS = 1e-5
NEG_INF = -1e30
PIECES = (256, 256, 512, 512, 512, 128, 128, 512)
OFFS = tuple(int(v) for v in np.cumsum((0,) + PIECES))
RET_W = OFFS[4]
SWA_W = D_IN - RET_W
SWA_OFFS = tuple(o - RET_W for o in OFFS[4:])
TM = 512

ADAM_LR, ADAM_B1, ADAM_B2, ADAM_EPS, ADAM_WD, ADAM_STEP = 0.001, 0.9, 0.999, 1e-08, 0.01, 10

VMEM_LIMIT = 56 * 1024 * 1024
VMEM_V7X = 64 * 1024 * 1024
VMEM_WHOLE = VMEM_V7X - 1024 * 1024


def _cparams(n_grid=0, vmem=VMEM_LIMIT):
    sem = ("arbitrary",) * n_grid if n_grid else None
    return pltpu.CompilerParams(dimension_semantics=sem, vmem_limit_bytes=vmem)


class _Part(NamedTuple):
    step: Callable
    inputs: list
    in_specs: list
    out_specs: list
    out_shape: list
    scratch_shapes: list


def _fused_call(name, n_steps, parts):
    n_in = [len(p.inputs) for p in parts]
    n_out = [len(p.out_shape) for p in parts]
    n_scr = [len(p.scratch_shapes) for p in parts]

    def body(*refs):
        ins, outs, scr = refs[:sum(n_in)], refs[sum(n_in):sum(n_in) + sum(n_out)], refs[sum(n_in) + sum(n_out):]
        for i, p in enumerate(parts):
            take = lambda seq, counts: seq[sum(counts[:i]):sum(counts[:i + 1])]
            p.step(*take(ins, n_in), *take(outs, n_out), *take(scr, n_scr))

    flat = lambda field: [v for p in parts for v in getattr(p, field)]
    outs = pl.pallas_call(
        body, name=name, grid=(n_steps,), in_specs=flat("in_specs"), out_specs=tuple(flat("out_specs")),
        out_shape=tuple(flat("out_shape")), scratch_shapes=flat("scratch_shapes"), compiler_params=_cparams(1),
    )(*flat("inputs"))
    return [list(outs[sum(n_out[:i]):sum(n_out[:i + 1])]) for i in range(len(parts))]


def _dot(a, b):
    return jnp.dot(a, b, preferred_element_type=F32)


def _dot_nt(a, b):
    return lax.dot_general(a, b, (((1,), (1,)), ((), ())), preferred_element_type=F32)


def _dot_tn(a, b):
    return lax.dot_general(a, b, (((0,), (0,)), ((), ())), preferred_element_type=F32)


def _bf(a):
    return a.astype(BF16)


def _sigmoid(x):
    return 1.0 / (1.0 + jnp.exp(-x))


def _adamw(w, g, m, v):
    m = ADAM_B1 * m + (1.0 - ADAM_B1) * g
    v = ADAM_B2 * v + (1.0 - ADAM_B2) * (g * g)
    m_hat = m / (1.0 - ADAM_B1 ** ADAM_STEP)
    v_hat = v / (1.0 - ADAM_B2 ** ADAM_STEP)
    delta = -ADAM_LR * (m_hat / (jnp.sqrt(v_hat) + ADAM_EPS) + ADAM_WD * w)
    return delta, m, v


def _rotary_tables_t():
    half = RET_DK // 2
    inv_freq = np.float32(10000.0) ** (-np.arange(half, dtype=np.float32) / np.float32(half))
    ang = inv_freq[:, None] * np.arange(T, dtype=np.float32)[None, :]
    cos, sin = np.cos(ang).astype(np.float32), np.sin(ang).astype(np.float32)
    cos64 = np.concatenate([cos, cos], axis=0)
    sin64 = np.concatenate([-sin, sin], axis=0)
    return np.tile(cos64, (RET_H, 1)), np.tile(sin64, (RET_H, 1))


def _retention_tables_t():
    gamma = (1.0 - np.exp2(-5.0 - np.arange(RET_H, dtype=np.float32))).astype(np.float32)
    log_g = np.log(gamma).astype(np.float32)
    i = np.arange(CH, dtype=np.float32)
    diff = i[None, :] - i[:, None]
    decay = np.where(diff >= 0, np.exp(log_g[:, None, None] * np.maximum(diff, 0.0)), 0.0).astype(np.float32)
    decay_all = np.concatenate(list(decay), axis=1)
    zeta = np.exp(log_g[:, None] * (CH - 1.0 - i)).astype(np.float32)
    zeta_tab = np.repeat(zeta.T, RET_DK, axis=1)
    xi = np.exp(log_g[:, None] * (i + 1.0)).astype(np.float32)
    xi_tab = np.repeat(xi, RET_DK, axis=0)
    chunk_decay = np.exp(log_g * np.float32(CH)).astype(np.float32)
    row_head = np.arange(RET_V)[:, None] // RET_DV
    col_head = np.arange(RET_QK)[None, :] // RET_DK
    state_mask = (row_head == col_head).astype(np.float32)
    state_decay = (state_mask * chunk_decay[row_head]).astype(np.float32)
    q_mask = (np.arange(RET_QK)[:, None] // RET_DK == np.arange(RET_H * CH)[None, :] // CH).astype(np.float32)
    return (np.ascontiguousarray(decay_all), np.ascontiguousarray(zeta_tab), np.ascontiguousarray(xi_tab),
            state_mask, state_decay, q_mask)


def _bucket_table_t():
    qi = np.arange(CH)[:, None]
    kj = np.arange(2 * CH)[None, :]
    dist = qi + CH - kj
    n = np.maximum(dist, 0)
    max_exact = N_BUCKETS // 2
    nf = np.maximum(n, 1).astype(np.float32)
    large = max_exact + (np.log(nf / np.float32(max_exact)) / np.float32(np.log(CH / max_exact))
                         * np.float32(N_BUCKETS - max_exact)).astype(np.int32)
    large = np.minimum(large, N_BUCKETS - 1)
    bucket = np.where(n < max_exact, n, large)
    return np.ascontiguousarray(np.where((dist >= 0) & (dist < CH), bucket, -1).astype(np.int32).T)


def _mesh_pos():
    return lax.axis_index("x"), lax.axis_index("y"), lax.axis_index("c")


def _blk(px, py, pc):
    return 4 * px + 2 * py + pc


def _in_proj_gather(x, norm_w, w_in_t):
    tchunk = 512

    def body(x_any, nw_any, win_any, proj_ref, wt_out, h_out, wt_ref, h_ref, stage, x_ref, nw_ref, win_ref,
             send_sems, recv_sems, out_sems, wb_sems, h_sem, in_sems):
        n_tiles = T // TM
        loads = [pltpu.make_async_copy(win_any, win_ref, in_sems.at[0]),
                 pltpu.make_async_copy(nw_any, nw_ref, in_sems.at[1])]
        loads += [pltpu.make_async_copy(x_any.at[pl.ds(i * TM, TM), :], x_ref.at[pl.ds(i * TM, TM), :], in_sems.at[2 + i])
                  for i in range(n_tiles)]
        for cp in loads:
            cp.start()
        loads[0].wait()
        x, y, c = _mesh_pos()
        me = _blk(x, y, c)
        sibling = (x, y, 1 - c)
        nbr_a, nbr_b, diag = (x ^ (1 - c), y ^ c), (x ^ c, y ^ (1 - c)), (1 - x, 1 - y)

        def copy(k, b, to):
            return pltpu.make_async_remote_copy(src_ref=wt_ref.at[b], dst_ref=wt_ref.at[b], send_sem=send_sems.at[k],
                                                recv_sem=recv_sems.at[k], device_id=to, device_id_type=MESH)

        wt_ref[me] = _bf(win_ref[...])
        first = [copy(0, me, sibling), copy(1, me, (*nbr_a, c))]
        for cp in first:
            cp.start()
        to_b = copy(2, me, (*nbr_b, c))

        loads[1].wait()
        nw = nw_ref[...]
        w_mine = wt_ref[me]
        for i in range(n_tiles):
            rows = slice(i * TM, (i + 1) * TM)
            loads[2 + i].wait()
            xv = x_ref[rows, :]
            r = lax.rsqrt(jnp.mean(xv * xv, axis=-1, keepdims=True) + NORM_EPS)
            h_ref[rows, :] = _bf(xv * r * nw)
            stage[0, :, rows] = _dot_nt(w_mine, h_ref[rows, :])
        keep = [pltpu.make_async_copy(h_ref, h_out, h_sem)]
        keep[0].start()

        writes = []

        def project(b, staged=False):
            k = len(writes)
            if k >= 2:
                writes[k - 2].wait()
            keep.append(pltpu.make_async_copy(wt_ref.at[b], wt_out.at[b], wb_sems.at[k]))
            keep[-1].start()
            if not staged:
                w = wt_ref[b]
                for t in range(T // tchunk):
                    cols = slice(t * tchunk, (t + 1) * tchunk)
                    stage[k % 2, :, cols] = _dot_nt(w, h_ref[cols, :])
            cp = pltpu.make_async_copy(stage.at[k % 2], proj_ref.at[b], out_sems.at[k % 2])
            cp.start()
            writes.append(cp)

        here = (x, y, c)
        project(me, staged=True)
        copy(0, _blk(x, y, 1 - c), here).wait_recv()
        project(_blk(x, y, 1 - c))
        to_b.start()
        passed = [to_b]

        def landed(k, chip, relay_to=None):
            copy(k, _blk(*chip, c), here).wait_recv()
            out = [copy({1: 4, 2: 5, 3: 6}[k], _blk(*chip, c), sibling)]
            if relay_to is not None:
                out.append(copy(3, _blk(*chip, c), (*relay_to, c)))
            for cp in out:
                cp.start()
            passed.extend(out)
            project(_blk(*chip, c))

        def from_sibling(k, chip):
            copy(k, _blk(*chip, 1 - c), here).wait_recv()
            project(_blk(*chip, 1 - c))

        landed(1, nbr_a, relay_to=nbr_b)
        from_sibling(4, nbr_b)
        landed(2, nbr_b)
        from_sibling(5, nbr_a)
        landed(3, diag)
        from_sibling(6, diag)
        for cp in writes[-2:] + keep:
            cp.wait()
        for cp in first + passed:
            cp.wait_send()

    vm = pl.BlockSpec(memory_space=pltpu.VMEM)
    hbm = pl.BlockSpec(memory_space=pltpu.HBM)
    dma = pltpu.SemaphoreType.DMA
    proj, wt, h = pl.pallas_call(
        body, name="in_proj_gather",
        out_shape=(jax.ShapeDtypeStruct((N_DEV, WIN_BLK, T), F32), jax.ShapeDtypeStruct((N_DEV, WIN_BLK, D), BF16),
                   jax.ShapeDtypeStruct((T, D), BF16)),
        in_specs=[hbm, hbm, hbm], out_specs=(hbm, hbm, hbm),
        scratch_shapes=[pltpu.VMEM((N_DEV, WIN_BLK, D), BF16), pltpu.VMEM((T, D), BF16), pltpu.VMEM((2, WIN_BLK, T), F32),
                        pltpu.VMEM((T, D), F32), pltpu.VMEM((1, D), F32), pltpu.VMEM((WIN_BLK, D), F32),
                        dma((7,)), dma((7,)), dma((2,)), dma((N_DEV,)), dma, dma((2 + T // TM,))],
        compiler_params=_cparams(vmem=VMEM_WHOLE),
    )(x, norm_w, w_in_t)
    return proj.reshape(D_IN, T), wt.reshape(D_IN, D), h


def _gather_w_out(w_out_ref, own_ref, wo_ref, local_sem, send_sems, recv_sems, phase):
    x, y, c = _mesh_pos()
    me = _blk(x, y, c)
    sibling = (x, y, 1 - c)
    chips = [(1 - x, y), (x, 1 - y), (1 - x, 1 - y)]

    def copy(k, b, to, src=None):
        return pltpu.make_async_remote_copy(src_ref=wo_ref.at[b] if src is None else src, dst_ref=wo_ref.at[b],
                                            send_sem=send_sems.at[k], recv_sem=recv_sems.at[k], device_id=to,
                                            device_id_type=MESH)

    local = pltpu.make_async_copy(own_ref, wo_ref.at[me], local_sem)
    first = [copy(0, me, sibling, own_ref)] + [copy(1 + j, me, (*chip, c), own_ref) for j, chip in enumerate(chips)]
    passed = [copy(4 + j, _blk(*chip, c), sibling) for j, chip in enumerate(chips)]
    if phase == 0:
        own_ref[...] = _bf(w_out_ref[...])
        local.start()
        for cp in first:
            cp.start()
    elif phase == 1:
        for j, chip in enumerate(chips):
            copy(1 + j, _blk(*chip, c), (x, y, c)).wait_recv()
            passed[j].start()
    else:
        local.wait()
        copy(0, _blk(x, y, 1 - c), (x, y, c)).wait_recv()
        for j, chip in enumerate(chips):
            copy(4 + j, _blk(*chip, 1 - c), (x, y, c)).wait_recv()
        for cp in first + passed:
            cp.wait_send()


def _proj_piece(i, block_of):
    rows = PIECES[i]
    assert OFFS[i] % rows == 0
    return pl.BlockSpec((rows, CH), lambda n: (OFFS[i] // rows, block_of(n)))


def _proj_gate_halves(block_of):
    return [pl.BlockSpec((256, CH), lambda n, j=j: (OFFS[7] // 256 + j, block_of(n))) for j in range(2)]


def _swap_halves_t(t):
    half = RET_DK // 2
    parts = []
    for h in range(RET_H):
        parts += [t[h * RET_DK + half:(h + 1) * RET_DK], t[h * RET_DK:h * RET_DK + half]]
    return jnp.concatenate(parts, axis=0)


def _rotate(t, cos, sin):
    return t * cos + _swap_halves_t(t) * sin


def _group_norm_t(o):
    mu = jnp.mean(o, axis=0, keepdims=True)
    var = jnp.mean((o - mu) * (o - mu), axis=0, keepdims=True)
    rstd = lax.rsqrt(var + GN_EPS)
    return (o - mu) * rstd, rstd


def _retention_scores_t(q_t, k_rm_b, q_mask, decay_all):
    q_heads = _bf(jnp.concatenate([q_t] * RET_H, axis=1) * q_mask)
    return _dot(k_rm_b, q_heads) * decay_all


def _row_to_col(row):
    n = row.shape[1]
    diagonal = lax.broadcasted_iota(jnp.int32, (n, n), 0) == lax.broadcasted_iota(jnp.int32, (n, n), 1)
    return jnp.sum(jnp.where(diagonal, row, 0.0), axis=1, keepdims=True)


def _retention_fwd(proj, ret_norm_w, tables, w_out_blk):
    cos_t, sin_t, decay_all, zeta_tab, xi_tab, state_mask, state_decay, q_mask = tables

    def body(rq_ref, rk_ref, rv_ref, rg_ref, w_row, cos_ref, sin_ref, decay_ref, zeta_ref, xi_ref, smask_ref,
             sdecay_ref, qmask_ref, wout_ref, ro_ref, mix_ref, st_ref, wo_ref, w_ref,
             state, wo_own, wo_local, wo_send, wo_recv):
        n = pl.program_id(0)

        @pl.when(n == 0)
        def _():
            w_ref[...] = _row_to_col(w_row[...])
            state[...] = jnp.zeros_like(state)
            _gather_w_out(wout_ref, wo_own, wo_ref, wo_local, wo_send, wo_recv, phase=0)

        @pl.when(n == N_CH // 2)
        def _():
            _gather_w_out(wout_ref, wo_own, wo_ref, wo_local, wo_send, wo_recv, phase=1)

        @pl.when(n == N_CH - 1)
        def _():
            _gather_w_out(wout_ref, wo_own, wo_ref, wo_local, wo_send, wo_recv, phase=2)

        cos, sin = cos_ref[...], sin_ref[...]
        q_t = _rotate(rq_ref[...], cos, sin)
        k_t = _rotate(rk_ref[...], cos, sin) * (RET_DK ** -0.5)
        k_rm = k_t.T
        v_b = _bf(rv_ref[...])
        m_b = _bf(state[...])
        st_ref[0] = m_b
        scores_b = _bf(_retention_scores_t(q_t, _bf(k_rm), qmask_ref[...], decay_ref[...]))
        cross = _dot(m_b, _bf(q_t * xi_ref[...]))
        state[...] = state[...] * sdecay_ref[...] + _dot(v_b, _bf(k_rm * zeta_ref[...])) * smask_ref[...]
        heads = range(RET_H)
        rows = [slice(h * RET_DV, (h + 1) * RET_DV) for h in heads]
        o = [_dot(v_b[rows[h]], scores_b[:, h * CH:(h + 1) * CH]) + cross[rows[h]] for h in heads]
        for h in heads:
            rn, _ = _group_norm_t(o[h])
            g = rg_ref[rows[h], :]
            ro_ref[rows[h], :] = o[h]
            mix_ref[rows[h], :] = rn * w_ref[rows[h], :] * (g * _sigmoid(g))

    col = lambda w: pl.BlockSpec((w, CH), lambda n: (0, n))
    const = lambda shape: pl.BlockSpec(shape, lambda n: (0,) * len(shape))
    cur = lambda n: n
    return _Part(
        body,
        inputs=[proj, proj, proj, proj, ret_norm_w, cos_t, sin_t, decay_all, zeta_tab, xi_tab, state_mask, state_decay,
                q_mask, w_out_blk],
        in_specs=[_proj_piece(i, cur) for i in range(4)] + [
            const((1, RET_V)), col(RET_QK), col(RET_QK),
            const(decay_all.shape), const(zeta_tab.shape), const(xi_tab.shape), const(state_mask.shape),
            const(state_decay.shape), const(q_mask.shape), const((WOUT_BLK, D))],
        out_specs=[col(RET_V), col(RET_V), pl.BlockSpec((1, RET_V, RET_QK), lambda n: (n, 0, 0)),
                   pl.BlockSpec(memory_space=pl.ANY), const((RET_V, 1))],
        out_shape=[jax.ShapeDtypeStruct((RET_V, T), F32), jax.ShapeDtypeStruct((RET_V, T), F32),
                   jax.ShapeDtypeStruct((N_CH, RET_V, RET_QK), BF16), jax.ShapeDtypeStruct((N_DEV, WOUT_BLK, D), BF16),
                   jax.ShapeDtypeStruct((RET_V, 1), F32)],
        scratch_shapes=[pltpu.VMEM((RET_V, RET_QK), F32), pltpu.VMEM((WOUT_BLK, D), BF16), pltpu.SemaphoreType.DMA,
                        pltpu.SemaphoreType.DMA((N_DEV - 1,)), pltpu.SemaphoreType.DMA((N_DEV - 1,))])


def _retention_bwd(proj, ro, states, dmix, ret_norm_w, tables):
    cos_t, sin_t, decay_all, zeta_tab, xi_tab, state_mask, state_decay, q_mask = tables

    def body(*refs):
        @pl.when(pl.program_id(0) < N_CH)
        def _():
            chunk(*refs)

    def chunk(rq_ref, rk_ref, rv_ref, rg_ref, ro_ref, st_ref, dm_ref, w_ref, cos_ref, sin_ref, decay_ref, zeta_ref,
              xi_ref, smask_ref, sdecay_ref, qmask_ref, d_ref, dw_ref, gstate):
        i = pl.program_id(0)

        @pl.when(i == 0)
        def _():
            gstate[...] = jnp.zeros_like(gstate)
            dw_ref[...] = jnp.zeros_like(dw_ref)

        cos, sin = cos_ref[...], sin_ref[...]
        q_t = _rotate(rq_ref[...], cos, sin)
        k_t = _rotate(rk_ref[...], cos, sin) * (RET_DK ** -0.5)
        q_b, k_b = _bf(q_t), _bf(k_t)
        k_rm = k_t.T
        kz_b = _bf(k_rm * zeta_ref[...])
        qx_b = _bf(q_t * xi_ref[...])
        v_t = rv_ref[...]
        v_b = _bf(v_t)
        v_rm_b = _bf(v_t.T)
        decay = decay_ref[...]
        scores_b = _bf(_retention_scores_t(q_t, _bf(k_rm), qmask_ref[...], decay))
        heads = range(RET_H)
        rows = [slice(h * RET_DV, (h + 1) * RET_DV) for h in heads]
        qk_rows = [slice(h * RET_DK, (h + 1) * RET_DK) for h in heads]
        lanes = [slice(h * CH, (h + 1) * CH) for h in heads]
        do = []
        for h in heads:
            g, w, dm = rg_ref[rows[h], :], w_ref[rows[h], :], dm_ref[rows[h], :]
            rn, rstd = _group_norm_t(ro_ref[rows[h], :])
            sig = _sigmoid(g)
            silu = g * sig
            d_ref[OFFS[3] + h * RET_DV:OFFS[3] + (h + 1) * RET_DV, :] = _bf(dm * rn * w * (sig * (1.0 + g * (1.0 - sig))))
            dw_ref[rows[h], :] += dm * silu * rn
            drn = dm * silu * w
            do.append(rstd * (drn - jnp.mean(drn, axis=0, keepdims=True)
                              - rn * jnp.mean(drn * rn, axis=0, keepdims=True)))
        do_b = _bf(jnp.concatenate(do, axis=0))
        m_b = st_ref[0]
        g_all = gstate[...]
        g_b = _bf(g_all)
        dscores_b = [_bf(_dot(v_rm_b[:, rows[h]], do_b[rows[h]]) * decay[:, lanes[h]]) for h in heads]
        dq_cross = lax.dot_general(m_b, do_b, (((0,), (0,)), ((), ())), preferred_element_type=F32)
        dkz = _dot(v_rm_b, g_b)
        dv_cross = _dot_nt(g_b, kz_b)
        gstate[...] = g_all * sdecay_ref[...] + _dot_nt(do_b, qx_b) * smask_ref[...]
        dq = jnp.concatenate([_dot(k_b[qk_rows[h]], dscores_b[h]) for h in heads], axis=0) + dq_cross * xi_ref[...]
        dk = (jnp.concatenate([_dot_nt(q_b[qk_rows[h]], dscores_b[h]) for h in heads], axis=0)
              + (dkz * zeta_ref[...]).T) * (RET_DK ** -0.5)
        for h in heads:
            d_ref[OFFS[2] + h * RET_DV:OFFS[2] + (h + 1) * RET_DV, :] = _bf(
                _dot_nt(do_b[rows[h]], scores_b[:, lanes[h]]) + dv_cross[rows[h]])
        d_ref[OFFS[0]:OFFS[1], :] = _bf(dq * cos + _swap_halves_t(dq * sin))
        d_ref[OFFS[1]:OFFS[2], :] = _bf(dk * cos + _swap_halves_t(dk * sin))

    chunk_of = lambda i: N_CH - 1 - jnp.minimum(i, N_CH - 1)
    col = lambda w: pl.BlockSpec((w, CH), lambda i: (0, chunk_of(i)))
    const = lambda shape: pl.BlockSpec(shape, lambda i: (0,) * len(shape))
    return _Part(
        body,
        inputs=[proj, proj, proj, proj, ro, states, dmix, ret_norm_w, cos_t, sin_t, decay_all, zeta_tab, xi_tab,
                state_mask, state_decay, q_mask],
        in_specs=[_proj_piece(j, chunk_of) for j in range(4)] + [
                  col(RET_V),
                  pl.BlockSpec((1, RET_V, RET_QK), lambda i: (chunk_of(i), 0, 0)), col(RET_V),
                  const((RET_V, 1)), col(RET_QK), col(RET_QK),
                  const(decay_all.shape), const(zeta_tab.shape), const(xi_tab.shape), const(state_mask.shape),
                  const(state_decay.shape), const(q_mask.shape)],
        out_specs=[col(RET_W), const((RET_V, CH))],
        out_shape=[jax.ShapeDtypeStruct((RET_W, T), BF16), jax.ShapeDtypeStruct((RET_V, CH), F32)],
        scratch_shapes=[pltpu.VMEM((RET_V, RET_QK), F32)])


HQ_LANES = SWA_H * CH


def _head_lanes(hq):
    return slice(hq * CH, (hq + 1) * CH)


def _build_bias_t(bucket_ref, relb_ref, sinks_ref, bias_ref, sink_row):
    bt = bucket_ref[...]
    first = lax.broadcasted_iota(jnp.int32, bt.shape, 0) < CH
    for hq in range(SWA_H):
        b = jnp.full(bt.shape, NEG_INF, F32)
        for bk in range(N_BUCKETS):
            b = jnp.where(bt == bk, relb_ref[hq, bk], b)
        bias_ref[0, :, _head_lanes(hq)] = b
        bias_ref[1, :, _head_lanes(hq)] = jnp.where(first, NEG_INF, b)
        sink_row[:, _head_lanes(hq)] = jnp.full((1, CH), sinks_ref[hq], F32)


def _rms_t(t, w_col):
    r = lax.rsqrt(jnp.mean(t * t, axis=0, keepdims=True) + NORM_EPS)
    return t * r * w_col, r


def _rms_t_bwd(dn, t, r, w_col):
    u = dn * w_col
    return r * u - t * (r * r * r) * jnp.mean(u * t, axis=0, keepdims=True), dn * t * r


def _norm_kv(k_t, kw):
    return jnp.concatenate([_rms_t(k_t[g * SWA_D:(g + 1) * SWA_D], kw)[0] for g in range(SWA_KV)], axis=0)


def _kv_slot(a, kv):
    z = jnp.zeros_like(a)
    return jnp.concatenate([a, z] if kv == 0 else [z, a], axis=0)


def _softmax_t(s, sink):
    m = jnp.maximum(jnp.max(s, axis=0, keepdims=True), sink)
    p = jnp.exp(s - m)
    e_sink = jnp.exp(sink - m)
    inv = 1.0 / (jnp.sum(p, axis=0, keepdims=True) + e_sink)
    return p * inv, e_sink * inv


def _gate_rows(sg_halves, hq):
    per_half = SWA_H // 2
    return sg_halves[hq // per_half][(hq % per_half) * SWA_D:(hq % per_half + 1) * SWA_D, :]


def _swa_fwd(proj, q_norm_w, k_norm_w, sinks, rel_bias_t, bucket_t):
    def body(sq_ref, skp_ref, skc_ref, svp_ref, svc_ref, sg_lo, sg_hi, qw_row, kw_row, sinks_ref, relb_ref, bucket_ref,
             so_ref, mix_ref, qw_ref, kw_ref, bias_ref, sink_row):
        n = pl.program_id(0)

        @pl.when(n == 0)
        def _():
            qw_ref[...] = _row_to_col(qw_row[...])
            kw_ref[...] = _row_to_col(kw_row[...])
            _build_bias_t(bucket_ref, relb_ref, sinks_ref, bias_ref, sink_row)

        var = (n == 0).astype(jnp.int32)
        qw, kw = qw_ref[...], kw_ref[...]
        kn_band = jnp.concatenate([_norm_kv(skp_ref[...], kw), _norm_kv(skc_ref[...], kw)], axis=1)
        kb_rm = _bf(kn_band.T)
        vband = _bf(jnp.concatenate([svp_ref[...], svc_ref[...]], axis=1))
        q_all = jnp.concatenate(
            [_bf(_kv_slot(_rms_t(sq_ref[hq * SWA_D:(hq + 1) * SWA_D, :], qw)[0] * (SWA_D ** -0.5), hq // SWA_G))
             for hq in range(SWA_H)], axis=1)
        probs, _ = _softmax_t(_dot(kb_rm, q_all) + bias_ref[var], sink_row[...])
        probs_b = _bf(probs)
        for kv in range(SWA_KV):
            o = _dot(vband[kv * SWA_D:(kv + 1) * SWA_D], probs_b[:, kv * SWA_G * CH:(kv + 1) * SWA_G * CH])
            for j in range(SWA_G):
                rows = slice((kv * SWA_G + j) * SWA_D, (kv * SWA_G + j + 1) * SWA_D)
                oh = o[:, j * CH:(j + 1) * CH]
                g = _gate_rows((sg_lo, sg_hi), kv * SWA_G + j)
                so_ref[rows, :] = oh
                mix_ref[rows, :] = oh * (g * _sigmoid(g))

    col = lambda w: pl.BlockSpec((w, CH), lambda n: (0, n))
    const = lambda shape: pl.BlockSpec(shape, lambda n: (0,) * len(shape))
    smem = pl.BlockSpec(memory_space=pltpu.SMEM)
    cur = lambda n: n
    prev = lambda n: jnp.maximum(n - 1, 0)
    return _Part(
        body,
        inputs=[proj, proj, proj, proj, proj, proj, proj, q_norm_w, k_norm_w, sinks, rel_bias_t, bucket_t],
        in_specs=[_proj_piece(4, cur), _proj_piece(5, prev), _proj_piece(5, cur), _proj_piece(6, prev),
                  _proj_piece(6, cur)] + _proj_gate_halves(cur) + [
                  const((1, SWA_D)), const((1, SWA_D)), smem, smem, const((2 * CH, CH))],
        out_specs=[col(512), col(512), const((SWA_D, 1)), const((SWA_D, 1))],
        out_shape=[jax.ShapeDtypeStruct((512, T), F32), jax.ShapeDtypeStruct((512, T), F32),
                   jax.ShapeDtypeStruct((SWA_D, 1), F32), jax.ShapeDtypeStruct((SWA_D, 1), F32)],
        scratch_shapes=[pltpu.VMEM((2, 2 * CH, HQ_LANES), F32), pltpu.VMEM((1, HQ_LANES), F32)])


def _swa_bwd(proj, so, dmix, q_norm_w, k_norm_w, sinks, rel_bias_t, bucket_t):
    def body(sq_ref, skp_ref, skc_ref, svp_ref, svc_ref, sg_lo, sg_hi, so_ref, dm_ref, qw_ref, kw_ref, sinks_ref,
             relb_ref, bucket_ref, d_ref, dbias_ref, dsink_ref, dqw_ref, dkw_ref,
             bias_ref, sink_row, band_dk, band_dv, carry_dk, carry_dv, hold_dq, hold_dg):
        n = pl.program_id(0)
        slot = n % 2

        @pl.when(n == 0)
        def _():
            _build_bias_t(bucket_ref, relb_ref, sinks_ref, bias_ref, sink_row)
            for ref in (dbias_ref, dsink_ref, dqw_ref, dkw_ref, carry_dk, carry_dv):
                ref[...] = jnp.zeros_like(ref)

        qw, kw = qw_ref[...], kw_ref[...]

        @pl.when(n < N_CH)
        def _():
            var = (n == 0).astype(jnp.int32)
            kn_band = jnp.concatenate([_norm_kv(skp_ref[...], kw), _norm_kv(skc_ref[...], kw)], axis=1)
            kb_rm = _bf(kn_band.T)
            kn_band_s = _bf(kn_band * (SWA_D ** -0.5))
            vband_f = jnp.concatenate([svp_ref[...], svc_ref[...]], axis=1)
            vb_rm = _bf(vband_f.T)
            q_raw, q_rstd, qs_b, do_b = [], [], [], []
            for hq in range(SWA_H):
                rows = slice(hq * SWA_D, (hq + 1) * SWA_D)
                q_t = sq_ref[rows, :]
                qn, rq = _rms_t(q_t, qw)
                g = _gate_rows((sg_lo, sg_hi), hq)
                sig = _sigmoid(g)
                dm = dm_ref[rows, :]
                hold_dg[slot, rows, :] = _bf(dm * so_ref[rows, :] * (sig * (1.0 + g * (1.0 - sig))))
                q_raw.append(q_t)
                q_rstd.append(rq)
                qs_b.append(_bf(qn * (SWA_D ** -0.5)))
                do_b.append(_bf(dm * (g * sig)))
            q_all = jnp.concatenate([_kv_slot(qs_b[hq], hq // SWA_G) for hq in range(SWA_H)], axis=1)
            do_all = jnp.concatenate([_kv_slot(do_b[hq], hq // SWA_G) for hq in range(SWA_H)], axis=1)
            probs, p_sink = _softmax_t(_dot(kb_rm, q_all) + bias_ref[var], sink_row[...])
            dprobs = _dot(vb_rm, do_all)
            t = jnp.sum(probs * dprobs, axis=0, keepdims=True)
            dlog = probs * (dprobs - t)
            dsink_ref[...] += -(p_sink * t)
            dbias_ref[...] += dlog
            dlog_b, probs_b = _bf(dlog), _bf(probs)
            dkn, dvv = [], []
            for kv in range(SWA_KV):
                heads = range(kv * SWA_G, (kv + 1) * SWA_G)
                lanes = slice(kv * SWA_G * CH, (kv + 1) * SWA_G * CH)
                dvv.append(_dot_nt(jnp.concatenate([do_b[hq] for hq in heads], axis=1), probs_b[:, lanes]))
                dkn.append(_dot_nt(jnp.concatenate([qs_b[hq] for hq in heads], axis=1), dlog_b[:, lanes]))
                dqn = _dot(kn_band_s[kv * SWA_D:(kv + 1) * SWA_D], dlog_b[:, lanes])
                for j, hq in enumerate(heads):
                    dq_t, dqw_terms = _rms_t_bwd(dqn[:, j * CH:(j + 1) * CH], q_raw[hq], q_rstd[hq], qw)
                    hold_dq[slot, hq * SWA_D:(hq + 1) * SWA_D, :] = _bf(dq_t)
                    dqw_ref[...] += dqw_terms
            band_dk[...] = jnp.concatenate(dkn, axis=0)
            band_dv[...] = jnp.concatenate(dvv, axis=0)

        @pl.when(n == N_CH)
        def _():
            band_dk[...] = jnp.zeros_like(band_dk)
            band_dv[...] = jnp.zeros_like(band_dv)

        @pl.when(n >= 1)
        def _():
            dkn_prev = carry_dk[...] + band_dk[:, 0:CH]
            k_t = skp_ref[...]
            for kv in range(SWA_KV):
                rows = slice(kv * SWA_D, (kv + 1) * SWA_D)
                _, rk = _rms_t(k_t[rows], kw)
                dk_t, dkw_terms = _rms_t_bwd(dkn_prev[rows], k_t[rows], rk, kw)
                d_ref[SWA_OFFS[1] + kv * SWA_D:SWA_OFFS[1] + (kv + 1) * SWA_D, :] = _bf(dk_t)
                dkw_ref[...] += dkw_terms
            d_ref[SWA_OFFS[2]:SWA_OFFS[3], :] = _bf(carry_dv[...] + band_dv[:, 0:CH])
            d_ref[SWA_OFFS[0]:SWA_OFFS[1], :] = hold_dq[1 - slot]
            d_ref[SWA_OFFS[3]:SWA_OFFS[4], :] = hold_dg[1 - slot]

        carry_dk[...] = band_dk[:, CH:2 * CH]
        carry_dv[...] = band_dv[:, CH:2 * CH]

    cur_block = lambda n: jnp.minimum(n, N_CH - 1)
    prev_block = lambda n: jnp.maximum(n - 1, 0)
    col = lambda w: pl.BlockSpec((w, CH), lambda n: (0, cur_block(n)))
    prev = lambda w: pl.BlockSpec((w, CH), lambda n: (0, prev_block(n)))
    const = lambda shape: pl.BlockSpec(shape, lambda n: (0,) * len(shape))
    smem = pl.BlockSpec(memory_space=pltpu.SMEM)
    return _Part(
        body,
        inputs=[proj, proj, proj, proj, proj, proj, proj, so, dmix, q_norm_w, k_norm_w, sinks, rel_bias_t, bucket_t],
        in_specs=[_proj_piece(4, cur_block), _proj_piece(5, prev_block), _proj_piece(5, cur_block),
                  _proj_piece(6, prev_block), _proj_piece(6, cur_block)] + _proj_gate_halves(cur_block) + [
                  col(512), col(512), const((SWA_D, 1)), const((SWA_D, 1)), smem, smem, const((2 * CH, CH))],
        out_specs=[prev(SWA_W), const((2 * CH, HQ_LANES)), const((1, HQ_LANES)),
                   const((SWA_D, CH)), const((SWA_D, CH))],
        out_shape=[jax.ShapeDtypeStruct((SWA_W, T), BF16),
                   jax.ShapeDtypeStruct((2 * CH, HQ_LANES), F32), jax.ShapeDtypeStruct((1, HQ_LANES), F32),
                   jax.ShapeDtypeStruct((SWA_D, CH), F32), jax.ShapeDtypeStruct((SWA_D, CH), F32)],
        scratch_shapes=[pltpu.VMEM((2, 2 * CH, HQ_LANES), F32), pltpu.VMEM((1, HQ_LANES), F32),
                        pltpu.VMEM((128, 2 * CH), F32), pltpu.VMEM((128, 2 * CH), F32),
                        pltpu.VMEM((128, CH), F32), pltpu.VMEM((128, CH), F32),
                        pltpu.VMEM((2, 512, CH), BF16), pltpu.VMEM((2, 512, CH), BF16)])


SMALL_GRAD_SHAPES = ((1, RET_V), (1, CH), (1, CH), (1, CH), (SWA_H, N_BUCKETS))


def _finish_small_grads(dbias_ref, bucket_ref, dsink_ref, dqw_ref, dkw_ref, drw_ref, rw_o, qw_o, kw_o, sink_o, relb_o):
    bt = bucket_ref[...]
    col = lax.broadcasted_iota(jnp.int32, (SWA_H, N_BUCKETS), 1)
    lane = lax.broadcasted_iota(jnp.int32, (1, CH), 1)
    sink = jnp.zeros((1, CH), F32)
    flip = (lax.broadcasted_iota(jnp.int32, (CH, CH), 0) + lax.broadcasted_iota(jnp.int32, (CH, CH), 1) == CH - 1)
    reverse = lambda a: jnp.dot(a, flip.astype(F32), precision=lax.Precision.HIGHEST, preferred_element_type=F32)
    per_distance = []
    for hq in range(SWA_H):
        d = reverse(jnp.where(bt >= 0, dbias_ref[:, _head_lanes(hq)], 0.0))
        d = pltpu.roll(d, 0, 1, stride=1, stride_axis=0)
        per_distance.append(jnp.sum(d, axis=0, keepdims=True))
        sink = sink + jnp.where(lane == hq, jnp.sum(dsink_ref[:, _head_lanes(hq)], axis=1, keepdims=True), 0.0)
    per_distance = reverse(jnp.concatenate(per_distance, axis=0))
    bucket_of_distance = bt[CH:CH + 1, :]
    acc = jnp.zeros((SWA_H, N_BUCKETS), F32)
    for bk in range(N_BUCKETS):
        s = jnp.sum(jnp.where(bucket_of_distance == bk, per_distance, 0.0), axis=1, keepdims=True)
        acc = acc + jnp.where(col == bk, s, 0.0)
    relb_o[...] = acc
    sink_o[...] = sink
    for src, dst in ((dqw_ref, qw_o), (dkw_ref, kw_o)):
        padded = jnp.concatenate([src[...], jnp.zeros((CH - SWA_D, CH), F32)], axis=0)
        dst[...] = jnp.sum(padded.T, axis=0, keepdims=True)
    rw_o[...] = jnp.sum(drw_ref[...].T, axis=0, keepdims=True)


def _out_proj(mix_r, mix_s, wo, x, target):
    def body(mr_ref, ms_ref, w_ref, x_ref, t_ref, loss_ref, dy_ref, dmr_ref, dms_ref, gw_ref):
        i = pl.program_id(0)

        @pl.when(i == 0)
        def _():
            loss_ref[...] = jnp.zeros_like(loss_ref)
            gw_ref[...] = jnp.zeros_like(gw_ref)

        mixed = jnp.concatenate([mr_ref[...], ms_ref[...]], axis=0)
        w = w_ref[...]
        mixed_b = _bf(mixed)
        err = x_ref[...] + _dot_tn(mixed_b, w) - t_ref[...]
        loss_ref[...] += jnp.sum(jnp.sum(err * err, axis=1, keepdims=True), axis=0, keepdims=True)
        dy = err * (1.0 / D)
        dy_ref[...] = dy
        dy_b = _bf(dy)
        dmix = _dot_nt(w, dy_b)
        dmr_ref[...] = dmix[0:512]
        dms_ref[...] = dmix[512:D]
        gw_ref[...] += _dot(mixed_b, dy_b)

    row = lambda w: pl.BlockSpec((TM, w), lambda i: (i, 0))
    col = lambda w: pl.BlockSpec((w, TM), lambda i: (0, i))
    const = lambda shape: pl.BlockSpec(shape, lambda i: (0,) * len(shape))
    return pl.pallas_call(
        body, name="out_proj", grid=(T // TM,),
        in_specs=[col(512), col(512), const((D, D)), row(D), row(D)],
        out_specs=(const((1, 1)), row(D), col(512), col(512), const((D, D))),
        out_shape=(jax.ShapeDtypeStruct((1, 1), F32), jax.ShapeDtypeStruct((T, D), F32),
                   jax.ShapeDtypeStruct((512, T), F32), jax.ShapeDtypeStruct((512, T), F32),
                   jax.ShapeDtypeStruct((D, D), F32)),
        compiler_params=_cparams(1),
    )(mix_r, mix_s, wo, x, target)


LOSS_ROW = 5
ALL_CHIPS = ((0, 0), (0, 1), (1, 0), (1, 1))


def _w_out_reduce_scatter(gwo):
    def body(gwo_ref, gout_o, own, rcv, snd, got, tot, local_sems, a_send, a_recv, b_send, b_recv):
        k = pl.program_id(0)
        x_, y_, c_ = _mesh_pos()
        sibling = (x_, y_, 1 - c_)
        rel_chips = [(1 - x_, y_), (x_, 1 - y_), (1 - x_, 1 - y_)]

        def to_sibling(j):
            px, py = ALL_CHIPS[j]
            return pltpu.make_async_remote_copy(src_ref=gwo_ref.at[_blk(px, py, 1 - c_)], dst_ref=rcv.at[j],
                                                send_sem=a_send.at[j], recv_sem=a_recv.at[j], device_id=sibling,
                                                device_id_type=MESH)

        def local(j):
            px, py = ALL_CHIPS[j]
            return pltpu.make_async_copy(gwo_ref.at[_blk(px, py, c_)], own.at[j], local_sems.at[j])

        def to_chip(q):
            return pltpu.make_async_remote_copy(src_ref=snd.at[q], dst_ref=got.at[q], send_sem=b_send.at[q],
                                                recv_sem=b_recv.at[q], device_id=(*rel_chips[q], c_), device_id_type=MESH)

        @pl.when(k == 0)
        def _():
            for j in range(len(ALL_CHIPS)):
                to_sibling(j).start()
                local(j).start()

        @pl.when(k == 2)
        def _():
            for j in range(len(ALL_CHIPS)):
                local(j).wait()
                to_sibling(j).wait_recv()
            for q in range(3):
                j = 2 * rel_chips[q][0] + rel_chips[q][1]
                snd[q] = _bf(own[j] + rcv[j])
                to_chip(q).start()
            jm = 2 * x_ + y_
            tot[...] = own[jm] + rcv[jm]

        @pl.when(k == N_CH)
        def _():
            g = tot[...]
            for q in range(3):
                to_chip(q).wait_recv()
                g = g + got[q].astype(F32)
            gout_o[...] = g
            for j in range(len(ALL_CHIPS)):
                to_sibling(j).wait_send()
            for q in range(3):
                to_chip(q).wait_send()

    dma = pltpu.SemaphoreType.DMA
    return _Part(
        body, inputs=[gwo], in_specs=[pl.BlockSpec(memory_space=pl.ANY)],
        out_specs=[pl.BlockSpec((WOUT_BLK, D), lambda k: (0, 0))], out_shape=[jax.ShapeDtypeStruct((WOUT_BLK, D), F32)],
        scratch_shapes=[pltpu.VMEM((4, WOUT_BLK, D), F32), pltpu.VMEM((4, WOUT_BLK, D), F32),
                        pltpu.VMEM((3, WOUT_BLK, D), BF16), pltpu.VMEM((3, WOUT_BLK, D), BF16),
                        pltpu.VMEM((WOUT_BLK, D), F32), dma((4,)), dma((4,)), dma((4,)), dma((3,)), dma((3,))])


def _in_proj_bwd_rs(d_ret, d_swa, h, wt, x, norm_w, dy, sse, small_acc):
    n_blocks, n_tiles = N_DEV, T // TM
    last = n_blocks + n_tiles - 1

    def body(da_any, db_any, da_ref, db_ref, h_any, w_any, x_ref, nw_ref, dy_ref, sse_ref,
             dbias_ref, bucket_ref, dsink_ref, dqw_ref, dkw_ref, drw_ref, gx_ref, gin_o, suma_o, sumb_o,
             stage, d2d_src, mine, rcv, snd, got, tot, tab_a, tab_b, gnw_acc, rw_ref, qw_ref, kw_ref, sk_ref, rb_ref,
             h_ref, w_ref, dp_sems, d2d_send, d2d_recv, ici_send, ici_recv, s_send, s_recv, hw_sems):
        k = pl.program_id(0)
        x_, y_, c_ = _mesh_pos()
        h_load = pltpu.make_async_copy(h_any, h_ref, hw_sems.at[0])
        w_load = pltpu.make_async_copy(w_any, w_ref, hw_sems.at[1])
        me = _blk(x_, y_, c_)
        sibling = (x_, y_, 1 - c_)
        rel_chips = [(1 - x_, 1 - y_), (1 - x_, y_), (x_, 1 - y_), (x_, y_)]

        def block_of_step(s):
            return _blk(*rel_chips[s // 2], 1 - c_ if s % 2 == 0 else c_)

        def fetch(s, wait):
            slot, b = s % 2, block_of_step(s)
            split = RET_W - 4 * WIN_BLK

            def run(src, dst, sem):
                cp = pltpu.make_async_copy(src, dst, sem)
                cp.wait() if wait else cp.start()

            @pl.when(b < 4)
            def _():
                run(da_any.at[pl.ds(pl.multiple_of(b * WIN_BLK, 16), WIN_BLK), :], stage.at[slot], dp_sems.at[slot, 0])

            @pl.when(b == 4)
            def _():
                run(da_any.at[pl.ds(4 * WIN_BLK, split), :], stage.at[slot, pl.ds(0, split), :], dp_sems.at[slot, 0])
                run(db_any.at[pl.ds(0, WIN_BLK - split), :], stage.at[slot, pl.ds(split, WIN_BLK - split), :],
                    dp_sems.at[slot, 1])

            @pl.when(b > 4)
            def _():
                run(db_any.at[pl.ds(pl.multiple_of(b * WIN_BLK - RET_W, 16), WIN_BLK), :], stage.at[slot],
                    dp_sems.at[slot, 0])

        def d2d_copy(r):
            return pltpu.make_async_remote_copy(src_ref=d2d_src, dst_ref=rcv.at[r], send_sem=d2d_send.at[r],
                                                recv_sem=d2d_recv.at[r], device_id=sibling, device_id_type=MESH)

        def ici_copy(r):
            return pltpu.make_async_remote_copy(src_ref=snd.at[r], dst_ref=got.at[r], send_sem=ici_send.at[r],
                                                recv_sem=ici_recv.at[r], device_id=(*rel_chips[r], c_),
                                                device_id_type=MESH)

        def table_copies():
            return [pltpu.make_async_remote_copy(src_ref=tab.at[me], dst_ref=tab.at[me], send_sem=s_send.at[a, p - 1],
                                                 recv_sem=s_recv.at[a, p - 1],
                                                 device_id=(x_ ^ (p >> 2), y_ ^ ((p >> 1) & 1), c_ ^ (p & 1)),
                                                 device_id_type=MESH)
                    for p in range(1, N_DEV) for a, tab in enumerate((tab_a, tab_b))]

        def chip_sum(r):
            d2d_copy(r).wait_recv()
            total = mine[...] + rcv[r]
            if r < 3:
                snd[r] = _bf(total)
                ici_copy(r).start()
            else:
                tot[...] = total

        for s in range(n_blocks):
            @pl.when(k == s)
            def _(s=s):
                r = s // 2
                if s == 0:
                    gnw_acc[...] = jnp.zeros_like(gnw_acc)
                    h_load.start()
                    fetch(0, wait=False)
                    w_load.start()
                fetch(s, wait=True)
                if s + 1 < n_blocks:
                    fetch(s + 1, wait=False)
                if s == 0:
                    h_load.wait()
                gw = _dot(stage[s % 2], h_ref[...])
                if s % 2 == 0:
                    if s >= 2:
                        chip_sum(r - 1)
                        d2d_copy(r - 1).wait_send()
                    d2d_src[...] = gw
                    d2d_copy(r).start()
                else:
                    mine[...] = gw

        @pl.when(k == n_blocks)
        def _():
            chip_sum(3)
            w_load.wait()

        @pl.when(k >= n_blocks)
        def _():
            dp = jnp.concatenate([da_ref[...], db_ref[...]], axis=0)
            xv, nw = x_ref[...], nw_ref[...]
            r = lax.rsqrt(jnp.mean(xv * xv, axis=-1, keepdims=True) + NORM_EPS)
            dh = _dot_tn(dp, w_ref[...])
            u = dh * nw
            gx_ref[...] = dy_ref[...] + r * u - xv * (r * r * r) * jnp.mean(u * xv, axis=-1, keepdims=True)
            gnw_acc[...] += jnp.sum(dh * (xv * r), axis=0, keepdims=True)

        @pl.when(k == n_blocks)
        def _():
            _finish_small_grads(dbias_ref, bucket_ref, dsink_ref, dqw_ref, dkw_ref, drw_ref,
                                rw_ref, qw_ref, kw_ref, sk_ref, rb_ref)

        @pl.when(k == last)
        def _():
            mine_a, mine_b = tab_a.at[me], tab_b.at[me]
            mine_a[...] = jnp.zeros_like(mine_a)
            mine_b[...] = jnp.zeros_like(mine_b)
            for row, ref in enumerate((gnw_acc, rw_ref, qw_ref, kw_ref, sk_ref)):
                mine_a[row:row + 1, 0:ref.shape[1]] = ref[...]
            mine_a[LOSS_ROW:LOSS_ROW + 1, 0:1] = sse_ref[...]
            mine_b[:, 0:N_BUCKETS] = rb_ref[...]
            tables = table_copies()
            for cp in tables:
                cp.start()
            g_in = tot[...]
            for q in range(3):
                ici_copy(q).wait_recv()
                g_in = g_in + got[q].astype(F32)
            gin_o[...] = g_in
            for cp in tables:
                cp.wait_recv()
            sum_a, sum_b = tab_a[0], tab_b[0]
            for b in range(1, N_DEV):
                sum_a = sum_a + tab_a[b]
                sum_b = sum_b + tab_b[b]
            suma_o[...] = sum_a
            sumb_o[...] = sum_b
            d2d_copy(3).wait_send()
            for q in range(3):
                ici_copy(q).wait_send()
            for cp in tables:
                cp.wait_send()

    tile = lambda k: jnp.maximum(k - n_blocks, 0)
    row = lambda w: pl.BlockSpec((TM, w), lambda k: (tile(k), 0))
    col = lambda w: pl.BlockSpec((w, TM), lambda k: (0, tile(k)))
    const = lambda shape, **kw: pl.BlockSpec(shape, lambda k: (0,) * len(shape), **kw)
    once = dict(pipeline_mode=pl.Buffered(1))
    hbm = pl.BlockSpec(memory_space=pltpu.HBM)
    dma = pltpu.SemaphoreType.DMA
    return pl.pallas_call(
        body, name="in_proj_bwd_rs", grid=(n_blocks + n_tiles,),
        in_specs=[hbm, hbm, col(RET_W), col(SWA_W), hbm, hbm,
                  row(D), const((1, D)), row(D), const((1, 1))]
                 + [const(a.shape) for a in small_acc],
        out_specs=(row(D), const((WIN_BLK, D)), const((8, D)), const((SWA_H, CH))),
        out_shape=(jax.ShapeDtypeStruct((T, D), F32), jax.ShapeDtypeStruct((WIN_BLK, D), F32),
                   jax.ShapeDtypeStruct((8, D), F32), jax.ShapeDtypeStruct((SWA_H, CH), F32)),
        scratch_shapes=[
            pltpu.VMEM((2, WIN_BLK, T), BF16), pltpu.VMEM((WIN_BLK, D), F32),
            pltpu.VMEM((WIN_BLK, D), F32),
            pltpu.VMEM((4, WIN_BLK, D), F32), pltpu.VMEM((3, WIN_BLK, D), BF16),
            pltpu.VMEM((3, WIN_BLK, D), BF16), pltpu.VMEM((WIN_BLK, D), F32),
            pltpu.VMEM((N_DEV, 8, D), F32), pltpu.VMEM((N_DEV, SWA_H, CH), F32),
            pltpu.VMEM((1, D), F32),
        ] + [pltpu.VMEM(s, F32) for s in SMALL_GRAD_SHAPES] + [
            pltpu.VMEM((T, D), BF16), pltpu.VMEM((D_IN, D), BF16),
            dma((2, 2)), dma((4,)), dma((4,)), dma((3,)), dma((3,)), dma((2, 7)), dma((2, 7)), dma((2,)),
        ],
        compiler_params=_cparams(1),
    )(d_ret, d_swa, d_ret, d_swa, h, wt, x, norm_w, dy, sse, *small_acc)


SMALL_SHAPES = ((1, D), (1, 512), (1, SWA_D), (1, SWA_D), (1, SWA_H), (SWA_H, N_BUCKETS))


def _adamw_all(grad_x, g_in_t, g_out, sum_a, sum_b, w_in_t, m_in_t, v_in_t, w_out, m_out, v_out, small_w, small_m, small_v):
    n_small = len(SMALL_SHAPES)
    n_parts = 2

    def body(*refs):
        gx_in = refs[0]
        gin_any, gout_any, suma_any, sumb_any, w_in_ref, m_in_ref, v_in_ref, w_out_ref, m_out_ref, v_out_ref = refs[1:11]
        pos = 11
        sw_refs, sm_refs, sv_refs = (refs[pos + i * n_small:pos + (i + 1) * n_small] for i in range(3))
        pos += 3 * n_small
        g_in_o, d_in_o, nm_in_o, nv_in_o, g_out_o, d_out_o, nm_out_o, nv_out_o = refs[pos:pos + 8]
        pos += 8
        sg_o, sd_o, snm_o, snv_o = (refs[pos + i * n_small:pos + (i + 1) * n_small] for i in range(4))
        pos += 4 * n_small
        loss_o, gx_out, gin_ref, gout_ref, suma_ref, sumb_ref, load_sems = refs[pos:]
        i = pl.program_id(0)
        loads = [pltpu.make_async_copy(src, dst, load_sems.at[j]) for j, (src, dst) in enumerate(
            ((gin_any, gin_ref), (gout_any, gout_ref), (suma_any, suma_ref), (sumb_any, sumb_ref)))]

        @pl.when(i == 0)
        def _():
            for cp in loads:
                cp.start()

        gx_out[...] = gx_in[...]

        @pl.when(i == 0)
        def _():
            for cp in loads:
                cp.wait()

        g = gout_ref[pl.ds(pl.multiple_of(i * (WOUT_BLK // n_parts), 8), WOUT_BLK // n_parts), :]
        d, nm, nv = _adamw(w_out_ref[...], g, m_out_ref[...], v_out_ref[...])
        g_out_o[...], d_out_o[...], nm_out_o[...], nv_out_o[...] = g, d, nm, nv
        g = gin_ref[pl.ds(pl.multiple_of(i * (WIN_BLK // n_parts), 8), WIN_BLK // n_parts), :]
        d, nm, nv = _adamw(w_in_ref[...], g, m_in_ref[...], v_in_ref[...])
        g_in_o[...], d_in_o[...], nm_in_o[...], nv_in_o[...] = g, d, nm, nv

        @pl.when(pl.program_id(0) == 0)
        def _():
            loss_o[...] = suma_ref[LOSS_ROW:LOSS_ROW + 1, 0:1] * (0.5 / D)
            for r, (rows, lanes) in enumerate(SMALL_SHAPES):
                g = suma_ref[r:r + 1, 0:lanes] if rows == 1 else sumb_ref[:, 0:lanes]
                d, nm, nv = _adamw(sw_refs[r][...], g, sm_refs[r][...], sv_refs[r][...])
                sg_o[r][...], sd_o[r][...], snm_o[r][...], snv_o[r][...] = g, d, nm, nv

    part = lambda rows: pl.BlockSpec((rows // n_parts, D), lambda i: (i, 0))
    const = lambda shape: pl.BlockSpec(shape, lambda i: (0,) * len(shape))
    win = jax.ShapeDtypeStruct((WIN_BLK, D), F32)
    wout = jax.ShapeDtypeStruct((WOUT_BLK, D), F32)
    smalls = tuple(jax.ShapeDtypeStruct(s, F32) for s in SMALL_SHAPES)
    small_specs = [const(s) for s in SMALL_SHAPES]
    hbm = pl.BlockSpec(memory_space=pltpu.HBM)
    outs = pl.pallas_call(
        body, name="adamw_all", grid=(n_parts,),
        in_specs=[part(T)] + [hbm] * 4 + [part(WIN_BLK)] * 3 + [part(WOUT_BLK)] * 3 + small_specs * 3,
        out_specs=tuple([part(WIN_BLK)] * 4 + [part(WOUT_BLK)] * 4 + small_specs * 4 + [const((1, 1)), part(T)]),
        out_shape=(win,) * 4 + (wout,) * 4 + smalls * 4 + (jax.ShapeDtypeStruct((1, 1), F32),
                                                           jax.ShapeDtypeStruct(grad_x.shape, F32)),
        scratch_shapes=[pltpu.VMEM((WIN_BLK, D), F32), pltpu.VMEM((WOUT_BLK, D), F32), pltpu.VMEM((8, D), F32),
                        pltpu.VMEM((SWA_H, CH), F32), pltpu.SemaphoreType.DMA((4,))],
        compiler_params=_cparams(1, vmem=VMEM_WHOLE),
    )(grad_x, g_in_t, g_out, sum_a, sum_b, w_in_t, m_in_t, v_in_t, w_out, m_out, v_out, *small_w, *small_m, *small_v)
    big, rest = outs[:8], outs[8:]
    return big, [rest[i * n_small:(i + 1) * n_small] for i in range(4)], rest[4 * n_small], rest[4 * n_small + 1]


def _small_rows(norm_w, ret_norm_w, q_norm_w, k_norm_w, sinks, rel_bias):
    return (norm_w.reshape(1, D), ret_norm_w.reshape(1, 512), q_norm_w.reshape(1, SWA_D), k_norm_w.reshape(1, SWA_D),
            sinks.reshape(1, SWA_H), rel_bias.T)


def _small_leaves(rows):
    return (rows[0].reshape(D), rows[1].reshape(512), rows[2].reshape(SWA_D), rows[3].reshape(SWA_D),
            rows[4].reshape(SWA_H), rows[5].T)


def kernel(x, norm_w, w_in, ret_norm_w, q_norm_w, k_norm_w, sinks, rel_bias, w_out, loss_target, m_norm_w, m_w_in, m_ret_norm_w, m_q_norm_w, m_k_norm_w, m_sinks, m_rel_bias, m_w_out, v_norm_w, v_w_in, v_ret_norm_w, v_q_norm_w, v_k_norm_w, v_sinks, v_rel_bias, v_w_out):
    x2 = x.reshape(T, D)
    target = loss_target.reshape(T, D)
    nw = norm_w.reshape(1, D)
    relb = rel_bias.T
    ret_tables = _rotary_tables_t() + _retention_tables_t()
    bucket_t = _bucket_table_t()

    proj, wt, h = _in_proj_gather(x2, nw, w_in.T)
    (ro, mix_r, states, wo, rnw_col), (so, mix_s, qnw_col, knw_col) = _fused_call("attn_fwd", N_CH, [
        _retention_fwd(proj, ret_norm_w.reshape(1, RET_V), ret_tables, w_out),
        _swa_fwd(proj, q_norm_w.reshape(1, SWA_D), k_norm_w.reshape(1, SWA_D), sinks, relb, bucket_t)])
    sse, dy, dmix_r, dmix_s, gwo = _out_proj(mix_r, mix_s, wo.reshape(D, D), x2, target)

    (d_ret, drw_acc), (d_swa, dbias, dsink_acc, dqw_acc, dkw_acc), (g_out,) = _fused_call("attn_bwd", N_CH + 1, [
        _retention_bwd(proj, ro, states, dmix_r, rnw_col, ret_tables),
        _swa_bwd(proj, so, dmix_s, qnw_col, knw_col, sinks, relb, bucket_t),
        _w_out_reduce_scatter(gwo.reshape(N_DEV, WOUT_BLK, D))])
    grad_x, g_in_t, sum_a, sum_b = _in_proj_bwd_rs(d_ret, d_swa, h, wt, x2, nw, dy, sse,
                                                   (dbias, bucket_t, dsink_acc, dqw_acc, dkw_acc, drw_acc))

    small_w = _small_rows(norm_w, ret_norm_w, q_norm_w, k_norm_w, sinks, rel_bias)
    small_m = _small_rows(m_norm_w, m_ret_norm_w, m_q_norm_w, m_k_norm_w, m_sinks, m_rel_bias)
    small_v = _small_rows(v_norm_w, v_ret_norm_w, v_q_norm_w, v_k_norm_w, v_sinks, v_rel_bias)
    big, small, loss, grad_x = _adamw_all(grad_x, g_in_t, g_out, sum_a, sum_b, w_in.T, m_w_in.T, v_w_in.T, w_out, m_w_out, v_w_out,
                                  small_w, small_m, small_v)

    def leaves(i):
        a = _small_leaves(small[i])
        return (a[0], big[i].T, a[1], a[2], a[3], a[4], a[5], big[4 + i])

    return (loss.reshape(()), grad_x.reshape(1, T, D), *leaves(0), *leaves(1), *leaves(2), *leaves(3))
```

```python
from typing import Callable, NamedTuple

import numpy as np
import jax
import jax.numpy as jnp
from jax import lax
from jax.experimental import pallas as pl
from jax.experimental.pallas import tpu as pltpu

F32 = jnp.float32
BF16 = jnp.bfloat16
MESH = pl.DeviceIdType.MESH

T = 2048
D = 1024
D_IN = 2816
N_DEV = 8
WIN_BLK = D_IN // N_DEV
WOUT_BLK = D // N_DEV
CH = 128
N_CH = T // CH
RET_H, RET_DK, RET_DV = 4, 64, 128
RET_QK = RET_H * RET_DK
RET_V = RET_H * RET_DV
SWA_H, SWA_KV, SWA_D, SWA_G = 8, 2, 64, 4
N_BUCKETS = 32
NORM_EPS = 1e-6
GN_EPS = 1e-5
NEG_INF = -1e30
PIECES = (256, 256, 512, 512, 512, 128, 128, 512)
OFFS = tuple(int(v) for v in np.cumsum((0,) + PIECES))
RET_W = OFFS[4]
SWA_W = D_IN - RET_W
SWA_OFFS = tuple(o - RET_W for o in OFFS[4:])
TM = 512

ADAM_LR, ADAM_B1, ADAM_B2, ADAM_EPS, ADAM_WD, ADAM_STEP = 0.001, 0.9, 0.999, 1e-08, 0.01, 10

VMEM_LIMIT = 56 * 1024 * 1024
VMEM_V7X = 64 * 1024 * 1024
VMEM_WHOLE = VMEM_V7X - 1024 * 1024


def _cparams(n_grid=0, vmem=VMEM_LIMIT):
    sem = ("arbitrary",) * n_grid if n_grid else None
    return pltpu.CompilerParams(dimension_semantics=sem, vmem_limit_bytes=vmem)


class _Part(NamedTuple):
    step: Callable
    inputs: list
    in_specs: list
    out_specs: list
    out_shape: list
    scratch_shapes: list


def _fused_call(name, n_steps, parts):
    n_in = [len(p.inputs) for p in parts]
    n_out = [len(p.out_shape) for p in parts]
    n_scr = [len(p.scratch_shapes) for p in parts]

    def body(*refs):
        ins, outs, scr = refs[:sum(n_in)], refs[sum(n_in):sum(n_in) + sum(n_out)], refs[sum(n_in) + sum(n_out):]
        for i, p in enumerate(parts):
            take = lambda seq, counts: seq[sum(counts[:i]):sum(counts[:i + 1])]
            p.step(*take(ins, n_in), *take(outs, n_out), *take(scr, n_scr))

    flat = lambda field: [v for p in parts for v in getattr(p, field)]
    outs = pl.pallas_call(
        body, name=name, grid=(n_steps,), in_specs=flat("in_specs"), out_specs=tuple(flat("out_specs")),
        out_shape=tuple(flat("out_shape")), scratch_shapes=flat("scratch_shapes"), compiler_params=_cparams(1),
    )(*flat("inputs"))
    return [list(outs[sum(n_out[:i]):sum(n_out[:i + 1])]) for i in range(len(parts))]


def _dot(a, b):
    return jnp.dot(a, b, preferred_element_type=F32)


def _dot_nt(a, b):
    return lax.dot_general(a, b, (((1,), (1,)), ((), ())), preferred_element_type=F32)


def _dot_tn(a, b):
    return lax.dot_general(a, b, (((0,), (0,)), ((), ())), preferred_element_type=F32)


def _bf(a):
    return a.astype(BF16)


def _sigmoid(x):
    return 1.0 / (1.0 + jnp.exp(-x))


def _adamw(w, g, m, v):
    m = ADAM_B1 * m + (1.0 - ADAM_B1) * g
    v = ADAM_B2 * v + (1.0 - ADAM_B2) * (g * g)
    m_hat = m / (1.0 - ADAM_B1 ** ADAM_STEP)
    v_hat = v / (1.0 - ADAM_B2 ** ADAM_STEP)
    delta = -ADAM_LR * (m_hat / (jnp.sqrt(v_hat) + ADAM_EPS) + ADAM_WD * w)
    return delta, m, v


def _rotary_tables_t():
    half = RET_DK // 2
    inv_freq = np.float32(10000.0) ** (-np.arange(half, dtype=np.float32) / np.float32(half))
    ang = inv_freq[:, None] * np.arange(T, dtype=np.float32)[None, :]
    cos, sin = np.cos(ang).astype(np.float32), np.sin(ang).astype(np.float32)
    cos64 = np.concatenate([cos, cos], axis=0)
    sin64 = np.concatenate([-sin, sin], axis=0)
    return np.tile(cos64, (RET_H, 1)), np.tile(sin64, (RET_H, 1))


def _retention_tables_t():
    gamma = (1.0 - np.exp2(-5.0 - np.arange(RET_H, dtype=np.float32))).astype(np.float32)
    log_g = np.log(gamma).astype(np.float32)
    i = np.arange(CH, dtype=np.float32)
    diff = i[None, :] - i[:, None]
    decay = np.where(diff >= 0, np.exp(log_g[:, None, None] * np.maximum(diff, 0.0)), 0.0).astype(np.float32)
    decay_all = np.concatenate(list(decay), axis=1)
    zeta = np.exp(log_g[:, None] * (CH - 1.0 - i)).astype(np.float32)
    zeta_tab = np.repeat(zeta.T, RET_DK, axis=1)
    xi = np.exp(log_g[:, None] * (i + 1.0)).astype(np.float32)
    xi_tab = np.repeat(xi, RET_DK, axis=0)
    chunk_decay = np.exp(log_g * np.float32(CH)).astype(np.float32)
    row_head = np.arange(RET_V)[:, None] // RET_DV
    col_head = np.arange(RET_QK)[None, :] // RET_DK
    state_mask = (row_head == col_head).astype(np.float32)
    state_decay = (state_mask * chunk_decay[row_head]).astype(np.float32)
    q_mask = (np.arange(RET_QK)[:, None] // RET_DK == np.arange(RET_H * CH)[None, :] // CH).astype(np.float32)
    return (np.ascontiguousarray(decay_all), np.ascontiguousarray(zeta_tab), np.ascontiguousarray(xi_tab),
            state_mask, state_decay, q_mask)


def _bucket_table_t():
    qi = np.arange(CH)[:, None]
    kj = np.arange(2 * CH)[None, :]
    dist = qi + CH - kj
    n = np.maximum(dist, 0)
    max_exact = N_BUCKETS // 2
    nf = np.maximum(n, 1).astype(np.float32)
    large = max_exact + (np.log(nf / np.float32(max_exact)) / np.float32(np.log(CH / max_exact))
                         * np.float32(N_BUCKETS - max_exact)).astype(np.int32)
    large = np.minimum(large, N_BUCKETS - 1)
    bucket = np.where(n < max_exact, n, large)
    return np.ascontiguousarray(np.where((dist >= 0) & (dist < CH), bucket, -1).astype(np.int32).T)


def _mesh_pos():
    return lax.axis_index("x"), lax.axis_index("y"), lax.axis_index("c")


def _blk(px, py, pc):
    return 4 * px + 2 * py + pc


def _in_proj_gather(x, norm_w, w_in_t):
    tchunk = 512

    def body(x_any, nw_any, win_any, proj_ref, wt_out, h_out, wt_ref, h_ref, stage, x_ref, nw_ref, win_ref,
             send_sems, recv_sems, out_sems, wb_sems, h_sem, in_sems):
        n_tiles = T // TM
        loads = [pltpu.make_async_copy(win_any, win_ref, in_sems.at[0]),
                 pltpu.make_async_copy(nw_any, nw_ref, in_sems.at[1])]
        loads += [pltpu.make_async_copy(x_any.at[pl.ds(i * TM, TM), :], x_ref.at[pl.ds(i * TM, TM), :], in_sems.at[2 + i])
                  for i in range(n_tiles)]
        for cp in loads[:2]:
            cp.start()
        loads[0].wait()
        for cp in loads[2:]:
            cp.start()
        x, y, c = _mesh_pos()
        me = _blk(x, y, c)
        sibling = (x, y, 1 - c)
        nbr_a, nbr_b, diag = (x ^ (1 - c), y ^ c), (x ^ c, y ^ (1 - c)), (1 - x, 1 - y)

        def copy(k, b, to):
            return pltpu.make_async_remote_copy(src_ref=wt_ref.at[b], dst_ref=wt_ref.at[b], send_sem=send_sems.at[k],
                                                recv_sem=recv_sems.at[k], device_id=to, device_id_type=MESH)

        wt_ref[me] = _bf(win_ref[...])
        first = [copy(0, me, sibling), copy(1, me, (*nbr_a, c))]
        for cp in first:
            cp.start()
        to_b = copy(2, me, (*nbr_b, c))

        loads[1].wait()
        nw = nw_ref[...]
        w_mine = wt_ref[me]
        for i in range(n_tiles):
            rows = slice(i * TM, (i + 1) * TM)
            loads[2 + i].wait()
            xv = x_ref[rows, :]
            r = lax.rsqrt(jnp.mean(xv * xv, axis=-1, keepdims=True) + NORM_EPS)
            h_ref[rows, :] = _bf(xv * r * nw)
            stage[0, :, rows] = _dot_nt(w_mine, h_ref[rows, :])
        keep = [pltpu.make_async_copy(h_ref, h_out, h_sem)]
        keep[0].start()

        writes = []

        def project(b, staged=False):
            k = len(writes)
            if k >= 2:
                writes[k - 2].wait()
            keep.append(pltpu.make_async_copy(wt_ref.at[b], wt_out.at[b], wb_sems.at[k]))
            keep[-1].start()
            if not staged:
                w = wt_ref[b]
                for t in range(T // tchunk):
                    cols = slice(t * tchunk, (t + 1) * tchunk)
                    stage[k % 2, :, cols] = _dot_nt(w, h_ref[cols, :])
            cp = pltpu.make_async_copy(stage.at[k % 2], proj_ref.at[b], out_sems.at[k % 2])
            cp.start()
            writes.append(cp)

        here = (x, y, c)
        project(me, staged=True)
        copy(0, _blk(x, y, 1 - c), here).wait_recv()
        project(_blk(x, y, 1 - c))
        to_b.start()
        passed = [to_b]

        def landed(k, chip, relay_to=None):
            copy(k, _blk(*chip, c), here).wait_recv()
            out = [copy({1: 4, 2: 5, 3: 6}[k], _blk(*chip, c), sibling)]
            if relay_to is not None:
                out.append(copy(3, _blk(*chip, c), (*relay_to, c)))
            for cp in out:
                cp.start()
            passed.extend(out)
            project(_blk(*chip, c))

        def from_sibling(k, chip):
            copy(k, _blk(*chip, 1 - c), here).wait_recv()
            project(_blk(*chip, 1 - c))

        landed(1, nbr_a, relay_to=nbr_b)
        from_sibling(4, nbr_b)
        landed(2, nbr_b)
        from_sibling(5, nbr_a)
        landed(3, diag)
        from_sibling(6, diag)
        for cp in writes[-2:] + keep:
            cp.wait()
        for cp in first + passed:
            cp.wait_send()

    vm = pl.BlockSpec(memory_space=pltpu.VMEM)
    hbm = pl.BlockSpec(memory_space=pltpu.HBM)
    dma = pltpu.SemaphoreType.DMA
    proj, wt, h = pl.pallas_call(
        body, name="in_proj_gather",
        out_shape=(jax.ShapeDtypeStruct((N_DEV, WIN_BLK, T), F32), jax.ShapeDtypeStruct((N_DEV, WIN_BLK, D), BF16),
                   jax.ShapeDtypeStruct((T, D), BF16)),
        in_specs=[hbm, hbm, hbm], out_specs=(hbm, hbm, hbm),
        scratch_shapes=[pltpu.VMEM((N_DEV, WIN_BLK, D), BF16), pltpu.VMEM((T, D), BF16), pltpu.VMEM((2, WIN_BLK, T), F32),
                        pltpu.VMEM((T, D), F32), pltpu.VMEM((1, D), F32), pltpu.VMEM((WIN_BLK, D), F32),
                        dma((7,)), dma((7,)), dma((2,)), dma((N_DEV,)), dma, dma((2 + T // TM,))],
        compiler_params=_cparams(vmem=VMEM_WHOLE),
    )(x, norm_w, w_in_t)
    return proj.reshape(D_IN, T), wt.reshape(D_IN, D), h


def _gather_w_out(w_out_ref, own_ref, wo_ref, local_sem, send_sems, recv_sems, phase):
    x, y, c = _mesh_pos()
    me = _blk(x, y, c)
    sibling = (x, y, 1 - c)
    chips = [(1 - x, y), (x, 1 - y), (1 - x, 1 - y)]

    def copy(k, b, to, src=None):
        return pltpu.make_async_remote_copy(src_ref=wo_ref.at[b] if src is None else src, dst_ref=wo_ref.at[b],
                                            send_sem=send_sems.at[k], recv_sem=recv_sems.at[k], device_id=to,
                                            device_id_type=MESH)

    local = pltpu.make_async_copy(own_ref, wo_ref.at[me], local_sem)
    first = [copy(0, me, sibling, own_ref)] + [copy(1 + j, me, (*chip, c), own_ref) for j, chip in enumerate(chips)]
    passed = [copy(4 + j, _blk(*chip, c), sibling) for j, chip in enumerate(chips)]
    if phase == 0:
        own_ref[...] = _bf(w_out_ref[...])
        local.start()
        for cp in first:
            cp.start()
    elif phase == 1:
        for j, chip in enumerate(chips):
            copy(1 + j, _blk(*chip, c), (x, y, c)).wait_recv()
            passed[j].start()
    else:
        local.wait()
        copy(0, _blk(x, y, 1 - c), (x, y, c)).wait_recv()
        for j, chip in enumerate(chips):
            copy(4 + j, _blk(*chip, 1 - c), (x, y, c)).wait_recv()
        for cp in first + passed:
            cp.wait_send()


def _proj_piece(i, block_of):
    rows = PIECES[i]
    assert OFFS[i] % rows == 0
    return pl.BlockSpec((rows, CH), lambda n: (OFFS[i] // rows, block_of(n)))


def _proj_gate_halves(block_of):
    return [pl.BlockSpec((256, CH), lambda n, j=j: (OFFS[7] // 256 + j, block_of(n))) for j in range(2)]


def _swap_halves_t(t):
    half = RET_DK // 2
    parts = []
    for h in range(RET_H):
        parts += [t[h * RET_DK + half:(h + 1) * RET_DK], t[h * RET_DK:h * RET_DK + half]]
    return jnp.concatenate(parts, axis=0)


def _rotate(t, cos, sin):
    return t * cos + _swap_halves_t(t) * sin


def _group_norm_t(o):
    mu = jnp.mean(o, axis=0, keepdims=True)
    var = jnp.mean((o - mu) * (o - mu), axis=0, keepdims=True)
    rstd = lax.rsqrt(var + GN_EPS)
    return (o - mu) * rstd, rstd


def _retention_scores_t(q_t, k_rm_b, q_mask, decay_all):
    q_heads = _bf(jnp.concatenate([q_t] * RET_H, axis=1) * q_mask)
    return _dot(k_rm_b, q_heads) * decay_all


def _row_to_col(row):
    n = row.shape[1]
    diagonal = lax.broadcasted_iota(jnp.int32, (n, n), 0) == lax.broadcasted_iota(jnp.int32, (n, n), 1)
    return jnp.sum(jnp.where(diagonal, row, 0.0), axis=1, keepdims=True)


def _retention_fwd(proj, ret_norm_w, tables, w_out_blk):
    cos_t, sin_t, decay_all, zeta_tab, xi_tab, state_mask, state_decay, q_mask = tables

    def body(rq_ref, rk_ref, rv_ref, rg_ref, w_row, cos_ref, sin_ref, decay_ref, zeta_ref, xi_ref, smask_ref,
             sdecay_ref, qmask_ref, wout_ref, ro_ref, mix_ref, st_ref, wo_ref, w_ref,
             state, wo_own, wo_local, wo_send, wo_recv):
        n = pl.program_id(0)

        @pl.when(n == 0)
        def _():
            w_ref[...] = _row_to_col(w_row[...])
            state[...] = jnp.zeros_like(state)
            _gather_w_out(wout_ref, wo_own, wo_ref, wo_local, wo_send, wo_recv, phase=0)

        @pl.when(n == N_CH // 2)
        def _():
            _gather_w_out(wout_ref, wo_own, wo_ref, wo_local, wo_send, wo_recv, phase=1)

        @pl.when(n == N_CH - 1)
        def _():
            _gather_w_out(wout_ref, wo_own, wo_ref, wo_local, wo_send, wo_recv, phase=2)

        cos, sin = cos_ref[...], sin_ref[...]
        q_t = _rotate(rq_ref[...], cos, sin)
        k_t = _rotate(rk_ref[...], cos, sin) * (RET_DK ** -0.5)
        k_rm = k_t.T
        v_b = _bf(rv_ref[...])
        m_b = _bf(state[...])
        st_ref[0] = m_b
        scores_b = _bf(_retention_scores_t(q_t, _bf(k_rm), qmask_ref[...], decay_ref[...]))
        cross = _dot(m_b, _bf(q_t * xi_ref[...]))
        state[...] = state[...] * sdecay_ref[...] + _dot(v_b, _bf(k_rm * zeta_ref[...])) * smask_ref[...]
        heads = range(RET_H)
        rows = [slice(h * RET_DV, (h + 1) * RET_DV) for h in heads]
        o = [_dot(v_b[rows[h]], scores_b[:, h * CH:(h + 1) * CH]) + cross[rows[h]] for h in heads]
        for h in heads:
            rn, _ = _group_norm_t(o[h])
            g = rg_ref[rows[h], :]
            ro_ref[rows[h], :] = o[h]
            mix_ref[rows[h], :] = rn * w_ref[rows[h], :] * (g * _sigmoid(g))

    col = lambda w: pl.BlockSpec((w, CH), lambda n: (0, n))
    const = lambda shape: pl.BlockSpec(shape, lambda n: (0,) * len(shape))
    cur = lambda n: n
    return _Part(
        body,
        inputs=[proj, proj, proj, proj, ret_norm_w, cos_t, sin_t, decay_all, zeta_tab, xi_tab, state_mask, state_decay,
                q_mask, w_out_blk],
        in_specs=[_proj_piece(i, cur) for i in range(4)] + [
            const((1, RET_V)), col(RET_QK), col(RET_QK),
            const(decay_all.shape), const(zeta_tab.shape), const(xi_tab.shape), const(state_mask.shape),
            const(state_decay.shape), const(q_mask.shape), const((WOUT_BLK, D))],
        out_specs=[col(RET_V), col(RET_V), pl.BlockSpec((1, RET_V, RET_QK), lambda n: (n, 0, 0)),
                   pl.BlockSpec(memory_space=pl.ANY), const((RET_V, 1))],
        out_shape=[jax.ShapeDtypeStruct((RET_V, T), F32), jax.ShapeDtypeStruct((RET_V, T), F32),
                   jax.ShapeDtypeStruct((N_CH, RET_V, RET_QK), BF16), jax.ShapeDtypeStruct((N_DEV, WOUT_BLK, D), BF16),
                   jax.ShapeDtypeStruct((RET_V, 1), F32)],
        scratch_shapes=[pltpu.VMEM((RET_V, RET_QK), F32), pltpu.VMEM((WOUT_BLK, D), BF16), pltpu.SemaphoreType.DMA,
                        pltpu.SemaphoreType.DMA((N_DEV - 1,)), pltpu.SemaphoreType.DMA((N_DEV - 1,))])


def _retention_bwd(proj, ro, states, dmix, ret_norm_w, tables):
    cos_t, sin_t, decay_all, zeta_tab, xi_tab, state_mask, state_decay, q_mask = tables

    def body(*refs):
        @pl.when(pl.program_id(0) < N_CH)
        def _():
            chunk(*refs)

    def chunk(rq_ref, rk_ref, rv_ref, rg_ref, ro_ref, st_ref, dm_ref, w_ref, cos_ref, sin_ref, decay_ref, zeta_ref,
              xi_ref, smask_ref, sdecay_ref, qmask_ref, d_ref, dw_ref, gstate):
        i = pl.program_id(0)

        @pl.when(i == 0)
        def _():
            gstate[...] = jnp.zeros_like(gstate)
            dw_ref[...] = jnp.zeros_like(dw_ref)

        cos, sin = cos_ref[...], sin_ref[...]
        q_t = _rotate(rq_ref[...], cos, sin)
        k_t = _rotate(rk_ref[...], cos, sin) * (RET_DK ** -0.5)
        q_b, k_b = _bf(q_t), _bf(k_t)
        k_rm = k_t.T
        kz_b = _bf(k_rm * zeta_ref[...])
        qx_b = _bf(q_t * xi_ref[...])
        v_t = rv_ref[...]
        v_b = _bf(v_t)
        v_rm_b = _bf(v_t.T)
        decay = decay_ref[...]
        scores_b = _bf(_retention_scores_t(q_t, _bf(k_rm), qmask_ref[...], decay))
        heads = range(RET_H)
        rows = [slice(h * RET_DV, (h + 1) * RET_DV) for h in heads]
        qk_rows = [slice(h * RET_DK, (h + 1) * RET_DK) for h in heads]
        lanes = [slice(h * CH, (h + 1) * CH) for h in heads]
        do = []
        for h in heads:
            g, w, dm = rg_ref[rows[h], :], w_ref[rows[h], :], dm_ref[rows[h], :]
            rn, rstd = _group_norm_t(ro_ref[rows[h], :])
            sig = _sigmoid(g)
            silu = g * sig
            d_ref[OFFS[3] + h * RET_DV:OFFS[3] + (h + 1) * RET_DV, :] = _bf(dm * rn * w * (sig * (1.0 + g * (1.0 - sig))))
            dw_ref[rows[h], :] += dm * silu * rn
            drn = dm * silu * w
            do.append(rstd * (drn - jnp.mean(drn, axis=0, keepdims=True)
                              - rn * jnp.mean(drn * rn, axis=0, keepdims=True)))
        do_b = _bf(jnp.concatenate(do, axis=0))
        m_b = st_ref[0]
        g_all = gstate[...]
        g_b = _bf(g_all)
        dscores_b = [_bf(_dot(v_rm_b[:, rows[h]], do_b[rows[h]]) * decay[:, lanes[h]]) for h in heads]
        dq_cross = lax.dot_general(m_b, do_b, (((0,), (0,)), ((), ())), preferred_element_type=F32)
        dkz = _dot(v_rm_b, g_b)
        dv_cross = _dot_nt(g_b, kz_b)
        gstate[...] = g_all * sdecay_ref[...] + _dot_nt(do_b, qx_b) * smask_ref[...]
        dq = jnp.concatenate([_dot(k_b[qk_rows[h]], dscores_b[h]) for h in heads], axis=0) + dq_cross * xi_ref[...]
        dk = (jnp.concatenate([_dot_nt(q_b[qk_rows[h]], dscores_b[h]) for h in heads], axis=0)
              + (dkz * zeta_ref[...]).T) * (RET_DK ** -0.5)
        for h in heads:
            d_ref[OFFS[2] + h * RET_DV:OFFS[2] + (h + 1) * RET_DV, :] = _bf(
                _dot_nt(do_b[rows[h]], scores_b[:, lanes[h]]) + dv_cross[rows[h]])
        d_ref[OFFS[0]:OFFS[1], :] = _bf(dq * cos + _swap_halves_t(dq * sin))
        d_ref[OFFS[1]:OFFS[2], :] = _bf(dk * cos + _swap_halves_t(dk * sin))

    chunk_of = lambda i: N_CH - 1 - jnp.minimum(i, N_CH - 1)
    col = lambda w: pl.BlockSpec((w, CH), lambda i: (0, chunk_of(i)))
    const = lambda shape: pl.BlockSpec(shape, lambda i: (0,) * len(shape))
    return _Part(
        body,
        inputs=[proj, proj, proj, proj, ro, states, dmix, ret_norm_w, cos_t, sin_t, decay_all, zeta_tab, xi_tab,
                state_mask, state_decay, q_mask],
        in_specs=[_proj_piece(j, chunk_of) for j in range(4)] + [
                  col(RET_V),
                  pl.BlockSpec((1, RET_V, RET_QK), lambda i: (chunk_of(i), 0, 0)), col(RET_V),
                  const((RET_V, 1)), col(RET_QK), col(RET_QK),
                  const(decay_all.shape), const(zeta_tab.shape), const(xi_tab.shape), const(state_mask.shape),
                  const(state_decay.shape), const(q_mask.shape)],
        out_specs=[col(RET_W), const((RET_V, CH))],
        out_shape=[jax.ShapeDtypeStruct((RET_W, T), BF16), jax.ShapeDtypeStruct((RET_V, CH), F32)],
        scratch_shapes=[pltpu.VMEM((RET_V, RET_QK), F32)])


HQ_LANES = SWA_H * CH


def _head_lanes(hq):
    return slice(hq * CH, (hq + 1) * CH)


def _build_bias_t(bucket_ref, relb_ref, sinks_ref, bias_ref, sink_row):
    bt = bucket_ref[...]
    first = lax.broadcasted_iota(jnp.int32, bt.shape, 0) < CH
    for hq in range(SWA_H):
        b = jnp.full(bt.shape, NEG_INF, F32)
        for bk in range(N_BUCKETS):
            b = jnp.where(bt == bk, relb_ref[hq, bk], b)
        bias_ref[0, :, _head_lanes(hq)] = b
        bias_ref[1, :, _head_lanes(hq)] = jnp.where(first, NEG_INF, b)
        sink_row[:, _head_lanes(hq)] = jnp.full((1, CH), sinks_ref[hq], F32)


def _rms_t(t, w_col):
    r = lax.rsqrt(jnp.mean(t * t, axis=0, keepdims=True) + NORM_EPS)
    return t * r * w_col, r


def _rms_t_bwd(dn, t, r, w_col):
    u = dn * w_col
    return r * u - t * (r * r * r) * jnp.mean(u * t, axis=0, keepdims=True), dn * t * r


def _norm_kv(k_t, kw):
    return jnp.concatenate([_rms_t(k_t[g * SWA_D:(g + 1) * SWA_D], kw)[0] for g in range(SWA_KV)], axis=0)


def _kv_slot(a, kv):
    z = jnp.zeros_like(a)
    return jnp.concatenate([a, z] if kv == 0 else [z, a], axis=0)


def _softmax_t(s, sink):
    m = jnp.maximum(jnp.max(s, axis=0, keepdims=True), sink)
    p = jnp.exp(s - m)
    e_sink = jnp.exp(sink - m)
    inv = 1.0 / (jnp.sum(p, axis=0, keepdims=True) + e_sink)
    return p * inv, e_sink * inv


def _gate_rows(sg_halves, hq):
    per_half = SWA_H // 2
    return sg_halves[hq // per_half][(hq % per_half) * SWA_D:(hq % per_half + 1) * SWA_D, :]


def _swa_fwd(proj, q_norm_w, k_norm_w, sinks, rel_bias_t, bucket_t):
    def body(sq_ref, skp_ref, skc_ref, svp_ref, svc_ref, sg_lo, sg_hi, qw_row, kw_row, sinks_ref, relb_ref, bucket_ref,
             so_ref, mix_ref, qw_ref, kw_ref, bias_ref, sink_row):
        n = pl.program_id(0)

        @pl.when(n == 0)
        def _():
            qw_ref[...] = _row_to_col(qw_row[...])
            kw_ref[...] = _row_to_col(kw_row[...])
            _build_bias_t(bucket_ref, relb_ref, sinks_ref, bias_ref, sink_row)

        var = (n == 0).astype(jnp.int32)
        qw, kw = qw_ref[...], kw_ref[...]
        kn_band = jnp.concatenate([_norm_kv(skp_ref[...], kw), _norm_kv(skc_ref[...], kw)], axis=1)
        kb_rm = _bf(kn_band.T)
        vband = _bf(jnp.concatenate([svp_ref[...], svc_ref[...]], axis=1))
        q_all = jnp.concatenate(
            [_bf(_kv_slot(_rms_t(sq_ref[hq * SWA_D:(hq + 1) * SWA_D, :], qw)[0] * (SWA_D ** -0.5), hq // SWA_G))
             for hq in range(SWA_H)], axis=1)
        probs, _ = _softmax_t(_dot(kb_rm, q_all) + bias_ref[var], sink_row[...])
        probs_b = _bf(probs)
        for kv in range(SWA_KV):
            o = _dot(vband[kv * SWA_D:(kv + 1) * SWA_D], probs_b[:, kv * SWA_G * CH:(kv + 1) * SWA_G * CH])
            for j in range(SWA_G):
                rows = slice((kv * SWA_G + j) * SWA_D, (kv * SWA_G + j + 1) * SWA_D)
                oh = o[:, j * CH:(j + 1) * CH]
                g = _gate_rows((sg_lo, sg_hi), kv * SWA_G + j)
                so_ref[rows, :] = oh
                mix_ref[rows, :] = oh * (g * _sigmoid(g))

    col = lambda w: pl.BlockSpec((w, CH), lambda n: (0, n))
    const = lambda shape: pl.BlockSpec(shape, lambda n: (0,) * len(shape))
    smem = pl.BlockSpec(memory_space=pltpu.SMEM)
    cur = lambda n: n
    prev = lambda n: jnp.maximum(n - 1, 0)
    return _Part(
        body,
        inputs=[proj, proj, proj, proj, proj, proj, proj, q_norm_w, k_norm_w, sinks, rel_bias_t, bucket_t],
        in_specs=[_proj_piece(4, cur), _proj_piece(5, prev), _proj_piece(5, cur), _proj_piece(6, prev),
                  _proj_piece(6, cur)] + _proj_gate_halves(cur) + [
                  const((1, SWA_D)), const((1, SWA_D)), smem, smem, const((2 * CH, CH))],
        out_specs=[col(512), col(512), const((SWA_D, 1)), const((SWA_D, 1))],
        out_shape=[jax.ShapeDtypeStruct((512, T), F32), jax.ShapeDtypeStruct((512, T), F32),
                   jax.ShapeDtypeStruct((SWA_D, 1), F32), jax.ShapeDtypeStruct((SWA_D, 1), F32)],
        scratch_shapes=[pltpu.VMEM((2, 2 * CH, HQ_LANES), F32), pltpu.VMEM((1, HQ_LANES), F32)])


def _swa_bwd(proj, so, dmix, q_norm_w, k_norm_w, sinks, rel_bias_t, bucket_t):
    def body(sq_ref, skp_ref, skc_ref, svp_ref, svc_ref, sg_lo, sg_hi, so_ref, dm_ref, qw_ref, kw_ref, sinks_ref,
             relb_ref, bucket_ref, d_ref, dbias_ref, dsink_ref, dqw_ref, dkw_ref,
             bias_ref, sink_row, band_dk, band_dv, carry_dk, carry_dv, hold_dq, hold_dg):
        n = pl.program_id(0)
        slot = n % 2

        @pl.when(n == 0)
        def _():
            _build_bias_t(bucket_ref, relb_ref, sinks_ref, bias_ref, sink_row)
            for ref in (dbias_ref, dsink_ref, dqw_ref, dkw_ref, carry_dk, carry_dv):
                ref[...] = jnp.zeros_like(ref)

        qw, kw = qw_ref[...], kw_ref[...]

        @pl.when(n < N_CH)
        def _():
            var = (n == 0).astype(jnp.int32)
            kn_band = jnp.concatenate([_norm_kv(skp_ref[...], kw), _norm_kv(skc_ref[...], kw)], axis=1)
            kb_rm = _bf(kn_band.T)
            kn_band_s = _bf(kn_band * (SWA_D ** -0.5))
            vband_f = jnp.concatenate([svp_ref[...], svc_ref[...]], axis=1)
            vb_rm = _bf(vband_f.T)
            q_raw, q_rstd, qs_b, do_b = [], [], [], []
            for hq in range(SWA_H):
                rows = slice(hq * SWA_D, (hq + 1) * SWA_D)
                q_t = sq_ref[rows, :]
                qn, rq = _rms_t(q_t, qw)
                g = _gate_rows((sg_lo, sg_hi), hq)
                sig = _sigmoid(g)
                dm = dm_ref[rows, :]
                hold_dg[slot, rows, :] = _bf(dm * so_ref[rows, :] * (sig * (1.0 + g * (1.0 - sig))))
                q_raw.append(q_t)
                q_rstd.append(rq)
                qs_b.append(_bf(qn * (SWA_D ** -0.5)))
                do_b.append(_bf(dm * (g * sig)))
            q_all = jnp.concatenate([_kv_slot(qs_b[hq], hq // SWA_G) for hq in range(SWA_H)], axis=1)
            do_all = jnp.concatenate([_kv_slot(do_b[hq], hq // SWA_G) for hq in range(SWA_H)], axis=1)
            probs, p_sink = _softmax_t(_dot(kb_rm, q_all) + bias_ref[var], sink_row[...])
            dprobs = _dot(vb_rm, do_all)
            t = jnp.sum(probs * dprobs, axis=0, keepdims=True)
            dlog = probs * (dprobs - t)
            dsink_ref[...] += -(p_sink * t)
            dbias_ref[...] += dlog
            dlog_b, probs_b = _bf(dlog), _bf(probs)
            dkn, dvv = [], []
            for kv in range(SWA_KV):
                heads = range(kv * SWA_G, (kv + 1) * SWA_G)
                lanes = slice(kv * SWA_G * CH, (kv + 1) * SWA_G * CH)
                dvv.append(_dot_nt(jnp.concatenate([do_b[hq] for hq in heads], axis=1), probs_b[:, lanes]))
                dkn.append(_dot_nt(jnp.concatenate([qs_b[hq] for hq in heads], axis=1), dlog_b[:, lanes]))
                dqn = _dot(kn_band_s[kv * SWA_D:(kv + 1) * SWA_D], dlog_b[:, lanes])
                for j, hq in enumerate(heads):
                    dq_t, dqw_terms = _rms_t_bwd(dqn[:, j * CH:(j + 1) * CH], q_raw[hq], q_rstd[hq], qw)
                    hold_dq[slot, hq * SWA_D:(hq + 1) * SWA_D, :] = _bf(dq_t)
                    dqw_ref[...] += dqw_terms
            band_dk[...] = jnp.concatenate(dkn, axis=0)
            band_dv[...] = jnp.concatenate(dvv, axis=0)

        @pl.when(n == N_CH)
        def _():
            band_dk[...] = jnp.zeros_like(band_dk)
            band_dv[...] = jnp.zeros_like(band_dv)

        @pl.when(n >= 1)
        def _():
            dkn_prev = carry_dk[...] + band_dk[:, 0:CH]
            k_t = skp_ref[...]
            for kv in range(SWA_KV):
                rows = slice(kv * SWA_D, (kv + 1) * SWA_D)
                _, rk = _rms_t(k_t[rows], kw)
                dk_t, dkw_terms = _rms_t_bwd(dkn_prev[rows], k_t[rows], rk, kw)
                d_ref[SWA_OFFS[1] + kv * SWA_D:SWA_OFFS[1] + (kv + 1) * SWA_D, :] = _bf(dk_t)
                dkw_ref[...] += dkw_terms
            d_ref[SWA_OFFS[2]:SWA_OFFS[3], :] = _bf(carry_dv[...] + band_dv[:, 0:CH])
            d_ref[SWA_OFFS[0]:SWA_OFFS[1], :] = hold_dq[1 - slot]
            d_ref[SWA_OFFS[3]:SWA_OFFS[4], :] = hold_dg[1 - slot]

        carry_dk[...] = band_dk[:, CH:2 * CH]
        carry_dv[...] = band_dv[:, CH:2 * CH]

    cur_block = lambda n: jnp.minimum(n, N_CH - 1)
    prev_block = lambda n: jnp.maximum(n - 1, 0)
    col = lambda w: pl.BlockSpec((w, CH), lambda n: (0, cur_block(n)))
    prev = lambda w: pl.BlockSpec((w, CH), lambda n: (0, prev_block(n)))
    const = lambda shape: pl.BlockSpec(shape, lambda n: (0,) * len(shape))
    smem = pl.BlockSpec(memory_space=pltpu.SMEM)
    return _Part(
        body,
        inputs=[proj, proj, proj, proj, proj, proj, proj, so, dmix, q_norm_w, k_norm_w, sinks, rel_bias_t, bucket_t],
        in_specs=[_proj_piece(4, cur_block), _proj_piece(5, prev_block), _proj_piece(5, cur_block),
                  _proj_piece(6, prev_block), _proj_piece(6, cur_block)] + _proj_gate_halves(cur_block) + [
                  col(512), col(512), const((SWA_D, 1)), const((SWA_D, 1)), smem, smem, const((2 * CH, CH))],
        out_specs=[prev(SWA_W), const((2 * CH, HQ_LANES)), const((1, HQ_LANES)),
                   const((SWA_D, CH)), const((SWA_D, CH))],
        out_shape=[jax.ShapeDtypeStruct((SWA_W, T), BF16),
                   jax.ShapeDtypeStruct((2 * CH, HQ_LANES), F32), jax.ShapeDtypeStruct((1, HQ_LANES), F32),
                   jax.ShapeDtypeStruct((SWA_D, CH), F32), jax.ShapeDtypeStruct((SWA_D, CH), F32)],
        scratch_shapes=[pltpu.VMEM((2, 2 * CH, HQ_LANES), F32), pltpu.VMEM((1, HQ_LANES), F32),
                        pltpu.VMEM((128, 2 * CH), F32), pltpu.VMEM((128, 2 * CH), F32),
                        pltpu.VMEM((128, CH), F32), pltpu.VMEM((128, CH), F32),
                        pltpu.VMEM((2, 512, CH), BF16), pltpu.VMEM((2, 512, CH), BF16)])


SMALL_GRAD_SHAPES = ((1, RET_V), (1, CH), (1, CH), (1, CH), (SWA_H, N_BUCKETS))


def _finish_small_grads(dbias_ref, bucket_ref, dsink_ref, dqw_ref, dkw_ref, drw_ref, rw_o, qw_o, kw_o, sink_o, relb_o):
    bt = bucket_ref[...]
    col = lax.broadcasted_iota(jnp.int32, (SWA_H, N_BUCKETS), 1)
    lane = lax.broadcasted_iota(jnp.int32, (1, CH), 1)
    sink = jnp.zeros((1, CH), F32)
    flip = (lax.broadcasted_iota(jnp.int32, (CH, CH), 0) + lax.broadcasted_iota(jnp.int32, (CH, CH), 1) == CH - 1)
    reverse = lambda a: jnp.dot(a, flip.astype(F32), precision=lax.Precision.HIGHEST, preferred_element_type=F32)
    per_distance = []
    for hq in range(SWA_H):
        d = reverse(jnp.where(bt >= 0, dbias_ref[:, _head_lanes(hq)], 0.0))
        d = pltpu.roll(d, 0, 1, stride=1, stride_axis=0)
        per_distance.append(jnp.sum(d, axis=0, keepdims=True))
        sink = sink + jnp.where(lane == hq, jnp.sum(dsink_ref[:, _head_lanes(hq)], axis=1, keepdims=True), 0.0)
    per_distance = reverse(jnp.concatenate(per_distance, axis=0))
    bucket_of_distance = bt[CH:CH + 1, :]
    acc = jnp.zeros((SWA_H, N_BUCKETS), F32)
    for bk in range(N_BUCKETS):
        s = jnp.sum(jnp.where(bucket_of_distance == bk, per_distance, 0.0), axis=1, keepdims=True)
        acc = acc + jnp.where(col == bk, s, 0.0)
    relb_o[...] = acc
    sink_o[...] = sink
    for src, dst in ((dqw_ref, qw_o), (dkw_ref, kw_o)):
        padded = jnp.concatenate([src[...], jnp.zeros((CH - SWA_D, CH), F32)], axis=0)
        dst[...] = jnp.sum(padded.T, axis=0, keepdims=True)
    rw_o[...] = jnp.sum(drw_ref[...].T, axis=0, keepdims=True)


def _out_proj(mix_r, mix_s, wo, x, target):
    def body(mr_ref, ms_ref, w_ref, x_ref, t_ref, loss_ref, dy_ref, dmr_ref, dms_ref, gw_ref):
        i = pl.program_id(0)

        @pl.when(i == 0)
        def _():
            loss_ref[...] = jnp.zeros_like(loss_ref)
            gw_ref[...] = jnp.zeros_like(gw_ref)

        mixed = jnp.concatenate([mr_ref[...], ms_ref[...]], axis=0)
        w = w_ref[...]
        mixed_b = _bf(mixed)
        err = x_ref[...] + _dot_tn(mixed_b, w) - t_ref[...]
        loss_ref[...] += jnp.sum(jnp.sum(err * err, axis=1, keepdims=True), axis=0, keepdims=True)
        dy = err * (1.0 / D)
        dy_ref[...] = dy
        dy_b = _bf(dy)
        dmix = _dot_nt(w, dy_b)
        dmr_ref[...] = dmix[0:512]
        dms_ref[...] = dmix[512:D]
        gw_ref[...] += _dot(mixed_b, dy_b)

    row = lambda w: pl.BlockSpec((TM, w), lambda i: (i, 0))
    col = lambda w: pl.BlockSpec((w, TM), lambda i: (0, i))
    const = lambda shape: pl.BlockSpec(shape, lambda i: (0,) * len(shape))
    return pl.pallas_call(
        body, name="out_proj", grid=(T // TM,),
        in_specs=[col(512), col(512), const((D, D)), row(D), row(D)],
        out_specs=(const((1, 1)), row(D), col(512), col(512), const((D, D))),
        out_shape=(jax.ShapeDtypeStruct((1, 1), F32), jax.ShapeDtypeStruct((T, D), F32),
                   jax.ShapeDtypeStruct((512, T), F32), jax.ShapeDtypeStruct((512, T), F32),
                   jax.ShapeDtypeStruct((D, D), F32)),
        compiler_params=_cparams(1),
    )(mix_r, mix_s, wo, x, target)


LOSS_ROW = 5
ALL_CHIPS = ((0, 0), (0, 1), (1, 0), (1, 1))


def _w_out_reduce_scatter(gwo):
    def body(gwo_ref, gout_o, own, rcv, snd, got, tot, local_sems, a_send, a_recv, b_send, b_recv):
        k = pl.program_id(0)
        x_, y_, c_ = _mesh_pos()
        sibling = (x_, y_, 1 - c_)
        rel_chips = [(1 - x_, y_), (x_, 1 - y_), (1 - x_, 1 - y_)]

        def to_sibling(j):
            px, py = ALL_CHIPS[j]
            return pltpu.make_async_remote_copy(src_ref=gwo_ref.at[_blk(px, py, 1 - c_)], dst_ref=rcv.at[j],
                                                send_sem=a_send.at[j], recv_sem=a_recv.at[j], device_id=sibling,
                                                device_id_type=MESH)

        def local(j):
            px, py = ALL_CHIPS[j]
            return pltpu.make_async_copy(gwo_ref.at[_blk(px, py, c_)], own.at[j], local_sems.at[j])

        def to_chip(q):
            return pltpu.make_async_remote_copy(src_ref=snd.at[q], dst_ref=got.at[q], send_sem=b_send.at[q],
                                                recv_sem=b_recv.at[q], device_id=(*rel_chips[q], c_), device_id_type=MESH)

        @pl.when(k == 0)
        def _():
            for j in range(len(ALL_CHIPS)):
                to_sibling(j).start()
                local(j).start()

        @pl.when(k == 2)
        def _():
            for j in range(len(ALL_CHIPS)):
                local(j).wait()
                to_sibling(j).wait_recv()
            for q in range(3):
                j = 2 * rel_chips[q][0] + rel_chips[q][1]
                snd[q] = _bf(own[j] + rcv[j])
                to_chip(q).start()
            jm = 2 * x_ + y_
            tot[...] = own[jm] + rcv[jm]

        @pl.when(k == N_CH)
        def _():
            g = tot[...]
            for q in range(3):
                to_chip(q).wait_recv()
                g = g + got[q].astype(F32)
            gout_o[...] = g
            for j in range(len(ALL_CHIPS)):
                to_sibling(j).wait_send()
            for q in range(3):
                to_chip(q).wait_send()

    dma = pltpu.SemaphoreType.DMA
    return _Part(
        body, inputs=[gwo], in_specs=[pl.BlockSpec(memory_space=pl.ANY)],
        out_specs=[pl.BlockSpec((WOUT_BLK, D), lambda k: (0, 0))], out_shape=[jax.ShapeDtypeStruct((WOUT_BLK, D), F32)],
        scratch_shapes=[pltpu.VMEM((4, WOUT_BLK, D), F32), pltpu.VMEM((4, WOUT_BLK, D), F32),
                        pltpu.VMEM((3, WOUT_BLK, D), BF16), pltpu.VMEM((3, WOUT_BLK, D), BF16),
                        pltpu.VMEM((WOUT_BLK, D), F32), dma((4,)), dma((4,)), dma((4,)), dma((3,)), dma((3,))])


def _in_proj_bwd_rs(d_ret, d_swa, h, wt, x, norm_w, dy, sse, small_acc):
    n_blocks, n_tiles = N_DEV, T // TM
    last = n_blocks + n_tiles - 1

    def body(da_any, db_any, da_ref, db_ref, h_any, w_any, x_ref, nw_ref, dy_ref, sse_ref,
             dbias_ref, bucket_ref, dsink_ref, dqw_ref, dkw_ref, drw_ref, gx_ref, gin_o, suma_o, sumb_o,
             stage, d2d_src, mine, rcv, snd, got, tot, tab_a, tab_b, gnw_acc, rw_ref, qw_ref, kw_ref, sk_ref, rb_ref,
             h_ref, w_ref, dp_sems, d2d_send, d2d_recv, ici_send, ici_recv, s_send, s_recv, hw_sems):
        k = pl.program_id(0)
        x_, y_, c_ = _mesh_pos()
        h_load = pltpu.make_async_copy(h_any, h_ref, hw_sems.at[0])
        w_load = pltpu.make_async_copy(w_any, w_ref, hw_sems.at[1])
        me = _blk(x_, y_, c_)
        sibling = (x_, y_, 1 - c_)
        rel_chips = [(1 - x_, 1 - y_), (1 - x_, y_), (x_, 1 - y_), (x_, y_)]

        def block_of_step(s):
            return _blk(*rel_chips[s // 2], 1 - c_ if s % 2 == 0 else c_)

        def fetch(s, wait):
            slot, b = s % 2, block_of_step(s)
            split = RET_W - 4 * WIN_BLK

            def run(src, dst, sem):
                cp = pltpu.make_async_copy(src, dst, sem)
                cp.wait() if wait else cp.start()

            @pl.when(b < 4)
            def _():
                run(da_any.at[pl.ds(pl.multiple_of(b * WIN_BLK, 16), WIN_BLK), :], stage.at[slot], dp_sems.at[slot, 0])

            @pl.when(b == 4)
            def _():
                run(da_any.at[pl.ds(4 * WIN_BLK, split), :], stage.at[slot, pl.ds(0, split), :], dp_sems.at[slot, 0])
                run(db_any.at[pl.ds(0, WIN_BLK - split), :], stage.at[slot, pl.ds(split, WIN_BLK - split), :],
                    dp_sems.at[slot, 1])

            @pl.when(b > 4)
            def _():
                run(db_any.at[pl.ds(pl.multiple_of(b * WIN_BLK - RET_W, 16), WIN_BLK), :], stage.at[slot],
                    dp_sems.at[slot, 0])

        def d2d_copy(r):
            return pltpu.make_async_remote_copy(src_ref=d2d_src, dst_ref=rcv.at[r], send_sem=d2d_send.at[r],
                                                recv_sem=d2d_recv.at[r], device_id=sibling, device_id_type=MESH)

        def ici_copy(r):
            return pltpu.make_async_remote_copy(src_ref=snd.at[r], dst_ref=got.at[r], send_sem=ici_send.at[r],
                                                recv_sem=ici_recv.at[r], device_id=(*rel_chips[r], c_),
                                                device_id_type=MESH)

        def table_copies():
            return [pltpu.make_async_remote_copy(src_ref=tab.at[me], dst_ref=tab.at[me], send_sem=s_send.at[a, p - 1],
                                                 recv_sem=s_recv.at[a, p - 1],
                                                 device_id=(x_ ^ (p >> 2), y_ ^ ((p >> 1) & 1), c_ ^ (p & 1)),
                                                 device_id_type=MESH)
                    for p in range(1, N_DEV) for a, tab in enumerate((tab_a, tab_b))]

        def chip_sum(r):
            d2d_copy(r).wait_recv()
            total = mine[...] + rcv[r]
            if r < 3:
                snd[r] = _bf(total)
                ici_copy(r).start()
            else:
                tot[...] = total

        for s in range(n_blocks):
            @pl.when(k == s)
            def _(s=s):
                r = s // 2
                if s == 0:
                    gnw_acc[...] = jnp.zeros_like(gnw_acc)
                    h_load.start()
                    fetch(0, wait=False)
                if s == 1:
                    w_load.start()
                fetch(s, wait=True)
                if s + 1 < n_blocks:
                    fetch(s + 1, wait=False)
                if s == 0:
                    h_load.wait()
                gw = _dot(stage[s % 2], h_ref[...])
                if s % 2 == 0:
                    if s >= 2:
                        chip_sum(r - 1)
                        d2d_copy(r - 1).wait_send()
                    d2d_src[...] = gw
                    d2d_copy(r).start()
                else:
                    mine[...] = gw

        @pl.when(k == n_blocks)
        def _():
            chip_sum(3)
            w_load.wait()

        @pl.when(k >= n_blocks)
        def _():
            dp = jnp.concatenate([da_ref[...], db_ref[...]], axis=0)
            xv, nw = x_ref[...], nw_ref[...]
            r = lax.rsqrt(jnp.mean(xv * xv, axis=-1, keepdims=True) + NORM_EPS)
            dh = _dot_tn(dp, w_ref[...])
            gnw_acc[...] += jnp.sum(dh * (xv * r), axis=0, keepdims=True)

            @pl.when(k == last)
            def _():
                mine_a, mine_b = tab_a.at[me], tab_b.at[me]
                mine_a[...] = jnp.zeros_like(mine_a)
                mine_b[...] = jnp.zeros_like(mine_b)
                for row, ref in enumerate((gnw_acc, rw_ref, qw_ref, kw_ref, sk_ref)):
                    mine_a[row:row + 1, 0:ref.shape[1]] = ref[...]
                mine_a[LOSS_ROW:LOSS_ROW + 1, 0:1] = sse_ref[...]
                mine_b[:, 0:N_BUCKETS] = rb_ref[...]
                for cp in table_copies():
                    cp.start()

            u = dh * nw
            gx_ref[...] = dy_ref[...] + r * u - xv * (r * r * r) * jnp.mean(u * xv, axis=-1, keepdims=True)

        @pl.when(k == n_blocks)
        def _():
            _finish_small_grads(dbias_ref, bucket_ref, dsink_ref, dqw_ref, dkw_ref, drw_ref,
                                rw_ref, qw_ref, kw_ref, sk_ref, rb_ref)

        @pl.when(k == last)
        def _():
            tables = table_copies()
            g_in = tot[...]
            for q in range(3):
                ici_copy(q).wait_recv()
                g_in = g_in + got[q].astype(F32)
            gin_o[...] = g_in
            for cp in tables:
                cp.wait_recv()
            sum_a, sum_b = tab_a[0], tab_b[0]
            for b in range(1, N_DEV):
                sum_a = sum_a + tab_a[b]
                sum_b = sum_b + tab_b[b]
            suma_o[...] = sum_a
            sumb_o[...] = sum_b
            d2d_copy(3).wait_send()
            for q in range(3):
                ici_copy(q).wait_send()
            for cp in tables:
                cp.wait_send()

    tile = lambda k: jnp.maximum(k - n_blocks, 0)
    row = lambda w: pl.BlockSpec((TM, w), lambda k: (tile(k), 0))
    col = lambda w: pl.BlockSpec((w, TM), lambda k: (0, tile(k)))
    const = lambda shape, **kw: pl.BlockSpec(shape, lambda k: (0,) * len(shape), **kw)
    once = dict(pipeline_mode=pl.Buffered(1))
    hbm = pl.BlockSpec(memory_space=pltpu.HBM)
    dma = pltpu.SemaphoreType.DMA
    return pl.pallas_call(
        body, name="in_proj_bwd_rs", grid=(n_blocks + n_tiles,),
        in_specs=[hbm, hbm, col(RET_W), col(SWA_W), hbm, hbm,
                  row(D), const((1, D)), row(D), const((1, 1))]
                 + [const(a.shape) for a in small_acc],
        out_specs=(row(D), const((WIN_BLK, D)), const((8, D)), const((SWA_H, CH))),
        out_shape=(jax.ShapeDtypeStruct((T, D), F32), jax.ShapeDtypeStruct((WIN_BLK, D), F32),
                   jax.ShapeDtypeStruct((8, D), F32), jax.ShapeDtypeStruct((SWA_H, CH), F32)),
        scratch_shapes=[
            pltpu.VMEM((2, WIN_BLK, T), BF16), pltpu.VMEM((WIN_BLK, D), F32),
            pltpu.VMEM((WIN_BLK, D), F32),
            pltpu.VMEM((4, WIN_BLK, D), F32), pltpu.VMEM((3, WIN_BLK, D), BF16),
            pltpu.VMEM((3, WIN_BLK, D), BF16), pltpu.VMEM((WIN_BLK, D), F32),
            pltpu.VMEM((N_DEV, 8, D), F32), pltpu.VMEM((N_DEV, SWA_H, CH), F32),
            pltpu.VMEM((1, D), F32),
        ] + [pltpu.VMEM(s, F32) for s in SMALL_GRAD_SHAPES] + [
            pltpu.VMEM((T, D), BF16), pltpu.VMEM((D_IN, D), BF16),
            dma((2, 2)), dma((4,)), dma((4,)), dma((3,)), dma((3,)), dma((2, 7)), dma((2, 7)), dma((2,)),
        ],
        compiler_params=_cparams(1),
    )(d_ret, d_swa, d_ret, d_swa, h, wt, x, norm_w, dy, sse, *small_acc)


SMALL_SHAPES = ((1, D), (1, 512), (1, SWA_D), (1, SWA_D), (1, SWA_H), (SWA_H, N_BUCKETS))


def _adamw_all(grad_x, g_in_t, g_out, sum_a, sum_b, w_in_t, m_in_t, v_in_t, w_out, m_out, v_out, small_w, small_m, small_v):
    n_small = len(SMALL_SHAPES)
    n_parts = 2

    def body(*refs):
        gx_in = refs[0]
        gin_any, gout_any, suma_any, sumb_any, w_in_ref, m_in_ref, v_in_ref, w_out_ref, m_out_ref, v_out_ref = refs[1:11]
        pos = 11
        sw_refs, sm_refs, sv_refs = (refs[pos + i * n_small:pos + (i + 1) * n_small] for i in range(3))
        pos += 3 * n_small
        g_in_o, d_in_o, nm_in_o, nv_in_o, g_out_o, d_out_o, nm_out_o, nv_out_o = refs[pos:pos + 8]
        pos += 8
        sg_o, sd_o, snm_o, snv_o = (refs[pos + i * n_small:pos + (i + 1) * n_small] for i in range(4))
        pos += 4 * n_small
        loss_o, gx_out, gin_ref, gout_ref, suma_ref, sumb_ref, load_sems = refs[pos:]
        i = pl.program_id(0)
        loads = [pltpu.make_async_copy(src, dst, load_sems.at[j]) for j, (src, dst) in enumerate(
            ((gin_any, gin_ref), (gout_any, gout_ref), (suma_any, suma_ref), (sumb_any, sumb_ref)))]

        @pl.when(i == 0)
        def _():
            for cp in loads:
                cp.start()

        gx_out[...] = gx_in[...]

        @pl.when(i == 0)
        def _():
            for cp in loads:
                cp.wait()

        g = gout_ref[pl.ds(pl.multiple_of(i * (WOUT_BLK // n_parts), 8), WOUT_BLK // n_parts), :]
        d, nm, nv = _adamw(w_out_ref[...], g, m_out_ref[...], v_out_ref[...])
        g_out_o[...], d_out_o[...], nm_out_o[...], nv_out_o[...] = g, d, nm, nv
        g = gin_ref[pl.ds(pl.multiple_of(i * (WIN_BLK // n_parts), 8), WIN_BLK // n_parts), :]
        d, nm, nv = _adamw(w_in_ref[...], g, m_in_ref[...], v_in_ref[...])
        g_in_o[...], d_in_o[...], nm_in_o[...], nv_in_o[...] = g, d, nm, nv

        @pl.when(pl.program_id(0) == 0)
        def _():
            loss_o[...] = suma_ref[LOSS_ROW:LOSS_ROW + 1, 0:1] * (0.5 / D)
            for r, (rows, lanes) in enumerate(SMALL_SHAPES):
                g = suma_ref[r:r + 1, 0:lanes] if rows == 1 else sumb_ref[:, 0:lanes]
                d, nm, nv = _adamw(sw_refs[r][...], g, sm_refs[r][...], sv_refs[r][...])
                sg_o[r][...], sd_o[r][...], snm_o[r][...], snv_o[r][...] = g, d, nm, nv

    part = lambda rows: pl.BlockSpec((rows // n_parts, D), lambda i: (i, 0))
    const = lambda shape: pl.BlockSpec(shape, lambda i: (0,) * len(shape))
    win = jax.ShapeDtypeStruct((WIN_BLK, D), F32)
    wout = jax.ShapeDtypeStruct((WOUT_BLK, D), F32)
    smalls = tuple(jax.ShapeDtypeStruct(s, F32) for s in SMALL_SHAPES)
    small_specs = [const(s) for s in SMALL_SHAPES]
    hbm = pl.BlockSpec(memory_space=pltpu.HBM)
    outs = pl.pallas_call(
        body, name="adamw_all", grid=(n_parts,),
        in_specs=[part(T)] + [hbm] * 4 + [part(WIN_BLK)] * 3 + [part(WOUT_BLK)] * 3 + small_specs * 3,
        out_specs=tuple([part(WIN_BLK)] * 4 + [part(WOUT_BLK)] * 4 + small_specs * 4 + [const((1, 1)), part(T)]),
        out_shape=(win,) * 4 + (wout,) * 4 + smalls * 4 + (jax.ShapeDtypeStruct((1, 1), F32),
                                                           jax.ShapeDtypeStruct(grad_x.shape, F32)),
        scratch_shapes=[pltpu.VMEM((WIN_BLK, D), F32), pltpu.VMEM((WOUT_BLK, D), F32), pltpu.VMEM((8, D), F32),
                        pltpu.VMEM((SWA_H, CH), F32), pltpu.SemaphoreType.DMA((4,))],
        compiler_params=_cparams(1, vmem=VMEM_WHOLE),
    )(grad_x, g_in_t, g_out, sum_a, sum_b, w_in_t, m_in_t, v_in_t, w_out, m_out, v_out, *small_w, *small_m, *small_v)
    big, rest = outs[:8], outs[8:]
    return big, [rest[i * n_small:(i + 1) * n_small] for i in range(4)], rest[4 * n_small], rest[4 * n_small + 1]


def _small_rows(norm_w, ret_norm_w, q_norm_w, k_norm_w, sinks, rel_bias):
    return (norm_w.reshape(1, D), ret_norm_w.reshape(1, 512), q_norm_w.reshape(1, SWA_D), k_norm_w.reshape(1, SWA_D),
            sinks.reshape(1, SWA_H), rel_bias.T)


def _small_leaves(rows):
    return (rows[0].reshape(D), rows[1].reshape(512), rows[2].reshape(SWA_D), rows[3].reshape(SWA_D),
            rows[4].reshape(SWA_H), rows[5].T)


def kernel(x, norm_w, w_in, ret_norm_w, q_norm_w, k_norm_w, sinks, rel_bias, w_out, loss_target, m_norm_w, m_w_in, m_ret_norm_w, m_q_norm_w, m_k_norm_w, m_sinks, m_rel_bias, m_w_out, v_norm_w, v_w_in, v_ret_norm_w, v_q_norm_w, v_k_norm_w, v_sinks, v_rel_bias, v_w_out):
    x2 = x.reshape(T, D)
    target = loss_target.reshape(T, D)
    nw = norm_w.reshape(1, D)
    relb = rel_bias.T
    ret_tables = _rotary_tables_t() + _retention_tables_t()
    bucket_t = _bucket_table_t()

    proj, wt, h = _in_proj_gather(x2, nw, w_in.T)
    (ro, mix_r, states, wo, rnw_col), (so, mix_s, qnw_col, knw_col) = _fused_call("attn_fwd", N_CH, [
        _retention_fwd(proj, ret_norm_w.reshape(1, RET_V), ret_tables, w_out),
        _swa_fwd(proj, q_norm_w.reshape(1, SWA_D), k_norm_w.reshape(1, SWA_D), sinks, relb, bucket_t)])
    sse, dy, dmix_r, dmix_s, gwo = _out_proj(mix_r, mix_s, wo.reshape(D, D), x2, target)

    (d_ret, drw_acc), (d_swa, dbias, dsink_acc, dqw_acc, dkw_acc), (g_out,) = _fused_call("attn_bwd", N_CH + 1, [
        _retention_bwd(proj, ro, states, dmix_r, rnw_col, ret_tables),
        _swa_bwd(proj, so, dmix_s, qnw_col, knw_col, sinks, relb, bucket_t),
        _w_out_reduce_scatter(gwo.reshape(N_DEV, WOUT_BLK, D))])
    grad_x, g_in_t, sum_a, sum_b = _in_proj_bwd_rs(d_ret, d_swa, h, wt, x2, nw, dy, sse,
                                                   (dbias, bucket_t, dsink_acc, dqw_acc, dkw_acc, drw_acc))

    small_w = _small_rows(norm_w, ret_norm_w, q_norm_w, k_norm_w, sinks, rel_bias)
    small_m = _small_rows(m_norm_w, m_ret_norm_w, m_q_norm_w, m_k_norm_w, m_sinks, m_rel_bias)
    small_v = _small_rows(v_norm_w, v_ret_norm_w, v_q_norm_w, v_k_norm_w, v_sinks, v_rel_bias)
    big, small, loss, grad_x = _adamw_all(grad_x, g_in_t, g_out, sum_a, sum_b, w_in.T, m_w_in.T, v_w_in.T, w_out, m_w_out, v_w_out,
                                  small_w, small_m, small_v)

    def leaves(i):
        a = _small_leaves(small[i])
        return (a[0], big[i].T, a[1], a[2], a[3], a[4], a[5], big[4 + i])

    return (loss.reshape(()), grad_x.reshape(1, T, D), *leaves(0), *leaves(1), *leaves(2), *leaves(3))
```

```python
from typing import Callable, NamedTuple

import numpy as np
import jax
import jax.numpy as jnp
from jax import lax
from jax.experimental import pallas as pl
from jax.experimental.pallas import tpu as pltpu

F32 = jnp.float32
BF16 = jnp.bfloat16
MESH = pl.DeviceIdType.MESH

T = 2048
D = 1024
D_IN = 2816
N_DEV = 8
WIN_BLK = D_IN // N_DEV
WOUT_BLK = D // N_DEV
CH = 128
N_CH = T // CH
RET_H, RET_DK, RET_DV = 4, 64, 128
RET_QK = RET_H * RET_DK
RET_V = RET_H * RET_DV
SWA_H, SWA_KV, SWA_D, SWA_G = 8, 2, 64, 4
N_BUCKETS = 32
NORM_EPS = 1e-6
GN_EPS = 1e-5
NEG_INF = -1e30
PIECES = (256, 256, 512, 512, 512, 128, 128, 512)
OFFS = tuple(int(v) for v in np.cumsum((0,) + PIECES))
RET_W = OFFS[4]
SWA_W = D_IN - RET_W
SWA_OFFS = tuple(o - RET_W for o in OFFS[4:])
TM = 512

ADAM_LR, ADAM_B1, ADAM_B2, ADAM_EPS, ADAM_WD, ADAM_STEP = 0.001, 0.9, 0.999, 1e-08, 0.01, 10

VMEM_LIMIT = 56 * 1024 * 1024
VMEM_V7X = 64 * 1024 * 1024
VMEM_WHOLE = VMEM_V7X - 1024 * 1024


def _cparams(n_grid=0, vmem=VMEM_LIMIT):
    sem = ("arbitrary",) * n_grid if n_grid else None
    return pltpu.CompilerParams(dimension_semantics=sem, vmem_limit_bytes=vmem)


class _Part(NamedTuple):
    step: Callable
    inputs: list
    in_specs: list
    out_specs: list
    out_shape: list
    scratch_shapes: list


def _fused_call(name, n_steps, parts):
    n_in = [len(p.inputs) for p in parts]
    n_out = [len(p.out_shape) for p in parts]
    n_scr = [len(p.scratch_shapes) for p in parts]

    def body(*refs):
        ins, outs, scr = refs[:sum(n_in)], refs[sum(n_in):sum(n_in) + sum(n_out)], refs[sum(n_in) + sum(n_out):]
        for i, p in enumerate(parts):
            take = lambda seq, counts: seq[sum(counts[:i]):sum(counts[:i + 1])]
            p.step(*take(ins, n_in), *take(outs, n_out), *take(scr, n_scr))

    flat = lambda field: [v for p in parts for v in getattr(p, field)]
    outs = pl.pallas_call(
        body, name=name, grid=(n_steps,), in_specs=flat("in_specs"), out_specs=tuple(flat("out_specs")),
        out_shape=tuple(flat("out_shape")), scratch_shapes=flat("scratch_shapes"), compiler_params=_cparams(1),
    )(*flat("inputs"))
    return [list(outs[sum(n_out[:i]):sum(n_out[:i + 1])]) for i in range(len(parts))]


def _dot(a, b):
    return jnp.dot(a, b, preferred_element_type=F32)


def _dot_nt(a, b):
    return lax.dot_general(a, b, (((1,), (1,)), ((), ())), preferred_element_type=F32)


def _dot_tn(a, b):
    return lax.dot_general(a, b, (((0,), (0,)), ((), ())), preferred_element_type=F32)


def _bf(a):
    return a.astype(BF16)


def _sigmoid(x):
    return 1.0 / (1.0 + jnp.exp(-x))


def _adamw(w, g, m, v):
    m = ADAM_B1 * m + (1.0 - ADAM_B1) * g
    v = ADAM_B2 * v + (1.0 - ADAM_B2) * (g * g)
    m_hat = m / (1.0 - ADAM_B1 ** ADAM_STEP)
    v_hat = v / (1.0 - ADAM_B2 ** ADAM_STEP)
    delta = -ADAM_LR * (m_hat / (jnp.sqrt(v_hat) + ADAM_EPS) + ADAM_WD * w)
    return delta, m, v


def _rotary_tables_t():
    half = RET_DK // 2
    inv_freq = np.float32(10000.0) ** (-np.arange(half, dtype=np.float32) / np.float32(half))
    ang = inv_freq[:, None] * np.arange(T, dtype=np.float32)[None, :]
    cos, sin = np.cos(ang).astype(np.float32), np.sin(ang).astype(np.float32)
    cos64 = np.concatenate([cos, cos], axis=0)
    sin64 = np.concatenate([-sin, sin], axis=0)
    return np.tile(cos64, (RET_H, 1)), np.tile(sin64, (RET_H, 1))


def _retention_tables_t():
    gamma = (1.0 - np.exp2(-5.0 - np.arange(RET_H, dtype=np.float32))).astype(np.float32)
    log_g = np.log(gamma).astype(np.float32)
    i = np.arange(CH, dtype=np.float32)
    diff = i[None, :] - i[:, None]
    decay = np.where(diff >= 0, np.exp(log_g[:, None, None] * np.maximum(diff, 0.0)), 0.0).astype(np.float32)
    decay_all = np.concatenate(list(decay), axis=1)
    zeta = np.exp(log_g[:, None] * (CH - 1.0 - i)).astype(np.float32)
    zeta_tab = np.repeat(zeta.T, RET_DK, axis=1)
    xi = np.exp(log_g[:, None] * (i + 1.0)).astype(np.float32)
    xi_tab = np.repeat(xi, RET_DK, axis=0)
    chunk_decay = np.exp(log_g * np.float32(CH)).astype(np.float32)
    row_head = np.arange(RET_V)[:, None] // RET_DV
    col_head = np.arange(RET_QK)[None, :] // RET_DK
    state_mask = (row_head == col_head).astype(np.float32)
    state_decay = (state_mask * chunk_decay[row_head]).astype(np.float32)
    q_mask = (np.arange(RET_QK)[:, None] // RET_DK == np.arange(RET_H * CH)[None, :] // CH).astype(np.float32)
    return (np.ascontiguousarray(decay_all), np.ascontiguousarray(zeta_tab), np.ascontiguousarray(xi_tab),
            state_mask, state_decay, q_mask)


def _bucket_table_t():
    qi = np.arange(CH)[:, None]
    kj = np.arange(2 * CH)[None, :]
    dist = qi + CH - kj
    n = np.maximum(dist, 0)
    max_exact = N_BUCKETS // 2
    nf = np.maximum(n, 1).astype(np.float32)
    large = max_exact + (np.log(nf / np.float32(max_exact)) / np.float32(np.log(CH / max_exact))
                         * np.float32(N_BUCKETS - max_exact)).astype(np.int32)
    large = np.minimum(large, N_BUCKETS - 1)
    bucket = np.where(n < max_exact, n, large)
    return np.ascontiguousarray(np.where((dist >= 0) & (dist < CH), bucket, -1).astype(np.int32).T)


def _mesh_pos():
    return lax.axis_index("x"), lax.axis_index("y"), lax.axis_index("c")


def _blk(px, py, pc):
    return 4 * px + 2 * py + pc


def _in_proj_gather(x, norm_w, w_in_t):
    tchunk = 512

    def body(x_any, nw_any, win_any, proj_ref, wt_out, h_out, wt_ref, h_ref, stage, x_ref, nw_ref, win_ref,
             send_sems, recv_sems, out_sems, wb_sems, h_sem, in_sems):
        n_tiles = T // TM
        loads = [pltpu.make_async_copy(win_any, win_ref, in_sems.at[0]),
                 pltpu.make_async_copy(nw_any, nw_ref, in_sems.at[1])]
        loads += [pltpu.make_async_copy(x_any.at[pl.ds(i * TM, TM), :], x_ref.at[pl.ds(i * TM, TM), :], in_sems.at[2 + i])
                  for i in range(n_tiles)]
        for cp in loads[:2]:
            cp.start()
        loads[0].wait()
        for cp in loads[2:]:
            cp.start()
        x, y, c = _mesh_pos()
        me = _blk(x, y, c)
        sibling = (x, y, 1 - c)
        nbr_a, nbr_b, diag = (x ^ (1 - c), y ^ c), (x ^ c, y ^ (1 - c)), (1 - x, 1 - y)

        def copy(k, b, to):
            return pltpu.make_async_remote_copy(src_ref=wt_ref.at[b], dst_ref=wt_ref.at[b], send_sem=send_sems.at[k],
                                                recv_sem=recv_sems.at[k], device_id=to, device_id_type=MESH)

        wt_ref[me] = _bf(win_ref[...])
        first = [copy(0, me, sibling), copy(1, me, (*nbr_a, c))]
        for cp in first:
            cp.start()
        to_b = copy(2, me, (*nbr_b, c))

        loads[1].wait()
        nw = nw_ref[...]
        w_mine = wt_ref[me]
        for i in range(n_tiles):
            rows = slice(i * TM, (i + 1) * TM)
            loads[2 + i].wait()
            xv = x_ref[rows, :]
            r = lax.rsqrt(jnp.mean(xv * xv, axis=-1, keepdims=True) + NORM_EPS)
            h_ref[rows, :] = _bf(xv * r * nw)
            stage[0, :, rows] = _dot_nt(w_mine, h_ref[rows, :])
        keep = [pltpu.make_async_copy(h_ref, h_out, h_sem)]
        keep[0].start()

        writes = []

        def project(b, staged=False):
            k = len(writes)
            if k >= 2:
                writes[k - 2].wait()
            keep.append(pltpu.make_async_copy(wt_ref.at[b], wt_out.at[b], wb_sems.at[k]))
            keep[-1].start()
            if not staged:
                w = wt_ref[b]
                for t in range(T // tchunk):
                    cols = slice(t * tchunk, (t + 1) * tchunk)
                    stage[k % 2, :, cols] = _dot_nt(w, h_ref[cols, :])
            cp = pltpu.make_async_copy(stage.at[k % 2], proj_ref.at[b], out_sems.at[k % 2])
            cp.start()
            writes.append(cp)

        here = (x, y, c)
        project(me, staged=True)
        copy(0, _blk(x, y, 1 - c), here).wait_recv()
        project(_blk(x, y, 1 - c))
        to_b.start()
        passed = [to_b]

        def landed(k, chip, relay_to=None):
            copy(k, _blk(*chip, c), here).wait_recv()
            out = [copy({1: 4, 2: 5, 3: 6}[k], _blk(*chip, c), sibling)]
            if relay_to is not None:
                out.append(copy(3, _blk(*chip, c), (*relay_to, c)))
            for cp in out:
                cp.start()
            passed.extend(out)
            project(_blk(*chip, c))

        def from_sibling(k, chip):
            copy(k, _blk(*chip, 1 - c), here).wait_recv()
            project(_blk(*chip, 1 - c))

        landed(1, nbr_a, relay_to=nbr_b)
        from_sibling(4, nbr_b)
        landed(2, nbr_b)
        from_sibling(5, nbr_a)
        landed(3, diag)
        from_sibling(6, diag)
        for cp in writes[-2:] + keep:
            cp.wait()
        for cp in first + passed:
            cp.wait_send()

    vm = pl.BlockSpec(memory_space=pltpu.VMEM)
    hbm = pl.BlockSpec(memory_space=pltpu.HBM)
    dma = pltpu.SemaphoreType.DMA
    proj, wt, h = pl.pallas_call(
        body, name="in_proj_gather",
        out_shape=(jax.ShapeDtypeStruct((N_DEV, WIN_BLK, T), F32), jax.ShapeDtypeStruct((N_DEV, WIN_BLK, D), BF16),
                   jax.ShapeDtypeStruct((T, D), BF16)),
        in_specs=[hbm, hbm, hbm], out_specs=(hbm, hbm, hbm),
        scratch_shapes=[pltpu.VMEM((N_DEV, WIN_BLK, D), BF16), pltpu.VMEM((T, D), BF16), pltpu.VMEM((2, WIN_BLK, T), F32),
                        pltpu.VMEM((T, D), F32), pltpu.VMEM((1, D), F32), pltpu.VMEM((WIN_BLK, D), F32),
                        dma((7,)), dma((7,)), dma((2,)), dma((N_DEV,)), dma, dma((2 + T // TM,))],
        compiler_params=_cparams(vmem=VMEM_WHOLE),
    )(x, norm_w, w_in_t)
    return proj.reshape(D_IN, T), wt.reshape(D_IN, D), h


def _gather_w_out(w_out_ref, own_ref, wo_ref, local_sem, send_sems, recv_sems, phase):
    x, y, c = _mesh_pos()
    me = _blk(x, y, c)
    sibling = (x, y, 1 - c)
    chips = [(1 - x, y), (x, 1 - y), (1 - x, 1 - y)]

    def copy(k, b, to, src=None):
        return pltpu.make_async_remote_copy(src_ref=wo_ref.at[b] if src is None else src, dst_ref=wo_ref.at[b],
                                            send_sem=send_sems.at[k], recv_sem=recv_sems.at[k], device_id=to,
                                            device_id_type=MESH)

    local = pltpu.make_async_copy(own_ref, wo_ref.at[me], local_sem)
    first = [copy(0, me, sibling, own_ref)] + [copy(1 + j, me, (*chip, c), own_ref) for j, chip in enumerate(chips)]
    passed = [copy(4 + j, _blk(*chip, c), sibling) for j, chip in enumerate(chips)]
    if phase == 0:
        own_ref[...] = _bf(w_out_ref[...])
        local.start()
        for cp in first:
            cp.start()
    elif phase == 1:
        for j, chip in enumerate(chips):
            copy(1 + j, _blk(*chip, c), (x, y, c)).wait_recv()
            passed[j].start()
    else:
        local.wait()
        copy(0, _blk(x, y, 1 - c), (x, y, c)).wait_recv()
        for j, chip in enumerate(chips):
            copy(4 + j, _blk(*chip, 1 - c), (x, y, c)).wait_recv()
        for cp in first + passed:
            cp.wait_send()


def _proj_piece(i, block_of):
    rows = PIECES[i]
    assert OFFS[i] % rows == 0
    return pl.BlockSpec((rows, CH), lambda n: (OFFS[i] // rows, block_of(n)))


def _proj_gate_halves(block_of):
    return [pl.BlockSpec((256, CH), lambda n, j=j: (OFFS[7] // 256 + j, block_of(n))) for j in range(2)]


def _swap_halves_t(t):
    half = RET_DK // 2
    parts = []
    for h in range(RET_H):
        parts += [t[h * RET_DK + half:(h + 1) * RET_DK], t[h * RET_DK:h * RET_DK + half]]
    return jnp.concatenate(parts, axis=0)


def _rotate(t, cos, sin):
    return t * cos + _swap_halves_t(t) * sin


def _group_norm_t(o):
    mu = jnp.mean(o, axis=0, keepdims=True)
    var = jnp.mean((o - mu) * (o - mu), axis=0, keepdims=True)
    rstd = lax.rsqrt(var + GN_EPS)
    return (o - mu) * rstd, rstd


def _retention_scores_t(q_t, k_rm_b, q_mask, decay_all):
    q_heads = _bf(jnp.concatenate([q_t] * RET_H, axis=1) * q_mask)
    return _dot(k_rm_b, q_heads) * decay_all


def _row_to_col(row):
    n = row.shape[1]
    diagonal = lax.broadcasted_iota(jnp.int32, (n, n), 0) == lax.broadcasted_iota(jnp.int32, (n, n), 1)
    return jnp.sum(jnp.where(diagonal, row, 0.0), axis=1, keepdims=True)


def _retention_fwd(proj, ret_norm_w, tables, w_out_blk):
    cos_t, sin_t, decay_all, zeta_tab, xi_tab, state_mask, state_decay, q_mask = tables

    def body(rq_ref, rk_ref, rv_ref, rg_ref, w_row, cos_ref, sin_ref, decay_ref, zeta_ref, xi_ref, smask_ref,
             sdecay_ref, qmask_ref, wout_ref, ro_ref, mix_ref, st_ref, wo_ref, w_ref,
             state, wo_own, wo_local, wo_send, wo_recv):
        n = pl.program_id(0)

        @pl.when(n == 0)
        def _():
            w_ref[...] = _row_to_col(w_row[...])
            state[...] = jnp.zeros_like(state)
            _gather_w_out(wout_ref, wo_own, wo_ref, wo_local, wo_send, wo_recv, phase=0)

        @pl.when(n == N_CH // 2)
        def _():
            _gather_w_out(wout_ref, wo_own, wo_ref, wo_local, wo_send, wo_recv, phase=1)

        @pl.when(n == N_CH - 1)
        def _():
            _gather_w_out(wout_ref, wo_own, wo_ref, wo_local, wo_send, wo_recv, phase=2)

        cos, sin = cos_ref[...], sin_ref[...]
        q_t = _rotate(rq_ref[...], cos, sin)
        k_t = _rotate(rk_ref[...], cos, sin) * (RET_DK ** -0.5)
        k_rm = k_t.T
        v_b = _bf(rv_ref[...])
        m_b = _bf(state[...])
        st_ref[0] = m_b
        scores_b = _bf(_retention_scores_t(q_t, _bf(k_rm), qmask_ref[...], decay_ref[...]))
        cross = _dot(m_b, _bf(q_t * xi_ref[...]))
        state[...] = state[...] * sdecay_ref[...] + _dot(v_b, _bf(k_rm * zeta_ref[...])) * smask_ref[...]
        heads = range(RET_H)
        rows = [slice(h * RET_DV, (h + 1) * RET_DV) for h in heads]
        o = [_dot(v_b[rows[h]], scores_b[:, h * CH:(h + 1) * CH]) + cross[rows[h]] for h in heads]
        for h in heads:
            rn, _ = _group_norm_t(o[h])
            g = rg_ref[rows[h], :]
            ro_ref[rows[h], :] = o[h]
            mix_ref[rows[h], :] = rn * w_ref[rows[h], :] * (g * _sigmoid(g))

    col = lambda w: pl.BlockSpec((w, CH), lambda n: (0, n))
    const = lambda shape: pl.BlockSpec(shape, lambda n: (0,) * len(shape))
    cur = lambda n: n
    return _Part(
        body,
        inputs=[proj, proj, proj, proj, ret_norm_w, cos_t, sin_t, decay_all, zeta_tab, xi_tab, state_mask, state_decay,
                q_mask, w_out_blk],
        in_specs=[_proj_piece(i, cur) for i in range(4)] + [
            const((1, RET_V)), col(RET_QK), col(RET_QK),
            const(decay_all.shape), const(zeta_tab.shape), const(xi_tab.shape), const(state_mask.shape),
            const(state_decay.shape), const(q_mask.shape), const((WOUT_BLK, D))],
        out_specs=[col(RET_V), col(RET_V), pl.BlockSpec((1, RET_V, RET_QK), lambda n: (n, 0, 0)),
                   pl.BlockSpec(memory_space=pl.ANY), const((RET_V, 1))],
        out_shape=[jax.ShapeDtypeStruct((RET_V, T), F32), jax.ShapeDtypeStruct((RET_V, T), F32),
                   jax.ShapeDtypeStruct((N_CH, RET_V, RET_QK), BF16), jax.ShapeDtypeStruct((N_DEV, WOUT_BLK, D), BF16),
                   jax.ShapeDtypeStruct((RET_V, 1), F32)],
        scratch_shapes=[pltpu.VMEM((RET_V, RET_QK), F32), pltpu.VMEM((WOUT_BLK, D), BF16), pltpu.SemaphoreType.DMA,
                        pltpu.SemaphoreType.DMA((N_DEV - 1,)), pltpu.SemaphoreType.DMA((N_DEV - 1,))])


def _retention_bwd(proj, ro, states, dmix, ret_norm_w, tables):
    cos_t, sin_t, decay_all, zeta_tab, xi_tab, state_mask, state_decay, q_mask = tables

    def body(*refs):
        @pl.when(pl.program_id(0) < N_CH)
        def _():
            chunk(*refs)

    def chunk(rq_ref, rk_ref, rv_ref, rg_ref, ro_ref, st_ref, dm_ref, w_ref, cos_ref, sin_ref, decay_ref, zeta_ref,
              xi_ref, smask_ref, sdecay_ref, qmask_ref, d_ref, dw_ref, gstate):
        i = pl.program_id(0)

        @pl.when(i == 0)
        def _():
            gstate[...] = jnp.zeros_like(gstate)
            dw_ref[...] = jnp.zeros_like(dw_ref)

        cos, sin = cos_ref[...], sin_ref[...]
        q_t = _rotate(rq_ref[...], cos, sin)
        k_t = _rotate(rk_ref[...], cos, sin) * (RET_DK ** -0.5)
        q_b, k_b = _bf(q_t), _bf(k_t)
        k_rm = k_t.T
        kz_b = _bf(k_rm * zeta_ref[...])
        qx_b = _bf(q_t * xi_ref[...])
        v_t = rv_ref[...]
        v_b = _bf(v_t)
        v_rm_b = _bf(v_t.T)
        decay = decay_ref[...]
        scores_b = _bf(_retention_scores_t(q_t, _bf(k_rm), qmask_ref[...], decay))
        heads = range(RET_H)
        rows = [slice(h * RET_DV, (h + 1) * RET_DV) for h in heads]
        qk_rows = [slice(h * RET_DK, (h + 1) * RET_DK) for h in heads]
        lanes = [slice(h * CH, (h + 1) * CH) for h in heads]
        do = []
        for h in heads:
            g, w, dm = rg_ref[rows[h], :], w_ref[rows[h], :], dm_ref[rows[h], :]
            rn, rstd = _group_norm_t(ro_ref[rows[h], :])
            sig = _sigmoid(g)
            silu = g * sig
            d_ref[OFFS[3] + h * RET_DV:OFFS[3] + (h + 1) * RET_DV, :] = _bf(dm * rn * w * (sig * (1.0 + g * (1.0 - sig))))
            dw_ref[rows[h], :] += dm * silu * rn
            drn = dm * silu * w
            do.append(rstd * (drn - jnp.mean(drn, axis=0, keepdims=True)
                              - rn * jnp.mean(drn * rn, axis=0, keepdims=True)))
        do_b = _bf(jnp.concatenate(do, axis=0))
        m_b = st_ref[0]
        g_all = gstate[...]
        g_b = _bf(g_all)
        dscores_b = [_bf(_dot(v_rm_b[:, rows[h]], do_b[rows[h]]) * decay[:, lanes[h]]) for h in heads]
        dq_cross = lax.dot_general(m_b, do_b, (((0,), (0,)), ((), ())), preferred_element_type=F32)
        dkz = _dot(v_rm_b, g_b)
        dv_cross = _dot_nt(g_b, kz_b)
        gstate[...] = g_all * sdecay_ref[...] + _dot_nt(do_b, qx_b) * smask_ref[...]
        dq = jnp.concatenate([_dot(k_b[qk_rows[h]], dscores_b[h]) for h in heads], axis=0) + dq_cross * xi_ref[...]
        dk = (jnp.concatenate([_dot_nt(q_b[qk_rows[h]], dscores_b[h]) for h in heads], axis=0)
              + (dkz * zeta_ref[...]).T) * (RET_DK ** -0.5)
        for h in heads:
            d_ref[OFFS[2] + h * RET_DV:OFFS[2] + (h + 1) * RET_DV, :] = _bf(
                _dot_nt(do_b[rows[h]], scores_b[:, lanes[h]]) + dv_cross[rows[h]])
        d_ref[OFFS[0]:OFFS[1], :] = _bf(dq * cos + _swap_halves_t(dq * sin))
        d_ref[OFFS[1]:OFFS[2], :] = _bf(dk * cos + _swap_halves_t(dk * sin))

    chunk_of = lambda i: N_CH - 1 - jnp.minimum(i, N_CH - 1)
    col = lambda w: pl.BlockSpec((w, CH), lambda i: (0, chunk_of(i)))
    const = lambda shape: pl.BlockSpec(shape, lambda i: (0,) * len(shape))
    return _Part(
        body,
        inputs=[proj, proj, proj, proj, ro, states, dmix, ret_norm_w, cos_t, sin_t, decay_all, zeta_tab, xi_tab,
                state_mask, state_decay, q_mask],
        in_specs=[_proj_piece(j, chunk_of) for j in range(4)] + [
                  col(RET_V),
                  pl.BlockSpec((1, RET_V, RET_QK), lambda i: (chunk_of(i), 0, 0)), col(RET_V),
                  const((RET_V, 1)), col(RET_QK), col(RET_QK),
                  const(decay_all.shape), const(zeta_tab.shape), const(xi_tab.shape), const(state_mask.shape),
                  const(state_decay.shape), const(q_mask.shape)],
        out_specs=[col(RET_W), const((RET_V, CH))],
        out_shape=[jax.ShapeDtypeStruct((RET_W, T), BF16), jax.ShapeDtypeStruct((RET_V, CH), F32)],
        scratch_shapes=[pltpu.VMEM((RET_V, RET_QK), F32)])


HQ_LANES = SWA_H * CH


def _head_lanes(hq):
    return slice(hq * CH, (hq + 1) * CH)


def _build_bias_t(bucket_ref, relb_ref, sinks_ref, bias_ref, sink_row):
    bt = bucket_ref[...]
    first = lax.broadcasted_iota(jnp.int32, bt.shape, 0) < CH
    for hq in range(SWA_H):
        b = jnp.full(bt.shape, NEG_INF, F32)
        for bk in range(N_BUCKETS):
            b = jnp.where(bt == bk, relb_ref[hq, bk], b)
        bias_ref[0, :, _head_lanes(hq)] = b
        bias_ref[1, :, _head_lanes(hq)] = jnp.where(first, NEG_INF, b)
        sink_row[:, _head_lanes(hq)] = jnp.full((1, CH), sinks_ref[hq], F32)


def _rms_t(t, w_col):
    r = lax.rsqrt(jnp.mean(t * t, axis=0, keepdims=True) + NORM_EPS)
    return t * r * w_col, r


def _rms_t_bwd(dn, t, r, w_col):
    u = dn * w_col
    return r * u - t * (r * r * r) * jnp.mean(u * t, axis=0, keepdims=True), dn * t * r


def _norm_kv(k_t, kw):
    return jnp.concatenate([_rms_t(k_t[g * SWA_D:(g + 1) * SWA_D], kw)[0] for g in range(SWA_KV)], axis=0)


def _kv_slot(a, kv):
    z = jnp.zeros_like(a)
    return jnp.concatenate([a, z] if kv == 0 else [z, a], axis=0)


def _softmax_t(s, sink):
    m = jnp.maximum(jnp.max(s, axis=0, keepdims=True), sink)
    p = jnp.exp(s - m)
    e_sink = jnp.exp(sink - m)
    inv = 1.0 / (jnp.sum(p, axis=0, keepdims=True) + e_sink)
    return p * inv, e_sink * inv


def _gate_rows(sg_halves, hq):
    per_half = SWA_H // 2
    return sg_halves[hq // per_half][(hq % per_half) * SWA_D:(hq % per_half + 1) * SWA_D, :]


def _swa_fwd(proj, q_norm_w, k_norm_w, sinks, rel_bias_t, bucket_t):
    def body(sq_ref, skp_ref, skc_ref, svp_ref, svc_ref, sg_lo, sg_hi, qw_row, kw_row, sinks_ref, relb_ref, bucket_ref,
             so_ref, mix_ref, qw_ref, kw_ref, bias_ref, sink_row):
        n = pl.program_id(0)

        @pl.when(n == 0)
        def _():
            qw_ref[...] = _row_to_col(qw_row[...])
            kw_ref[...] = _row_to_col(kw_row[...])
            _build_bias_t(bucket_ref, relb_ref, sinks_ref, bias_ref, sink_row)

        var = (n == 0).astype(jnp.int32)
        qw, kw = qw_ref[...], kw_ref[...]
        kn_band = jnp.concatenate([_norm_kv(skp_ref[...], kw), _norm_kv(skc_ref[...], kw)], axis=1)
        kb_rm = _bf(kn_band.T)
        vband = _bf(jnp.concatenate([svp_ref[...], svc_ref[...]], axis=1))
        q_all = jnp.concatenate(
            [_bf(_kv_slot(_rms_t(sq_ref[hq * SWA_D:(hq + 1) * SWA_D, :], qw)[0] * (SWA_D ** -0.5), hq // SWA_G))
             for hq in range(SWA_H)], axis=1)
        probs, _ = _softmax_t(_dot(kb_rm, q_all) + bias_ref[var], sink_row[...])
        probs_b = _bf(probs)
        for kv in range(SWA_KV):
            o = _dot(vband[kv * SWA_D:(kv + 1) * SWA_D], probs_b[:, kv * SWA_G * CH:(kv + 1) * SWA_G * CH])
            for j in range(SWA_G):
                rows = slice((kv * SWA_G + j) * SWA_D, (kv * SWA_G + j + 1) * SWA_D)
                oh = o[:, j * CH:(j + 1) * CH]
                g = _gate_rows((sg_lo, sg_hi), kv * SWA_G + j)
                so_ref[rows, :] = oh
                mix_ref[rows, :] = oh * (g * _sigmoid(g))

    col = lambda w: pl.BlockSpec((w, CH), lambda n: (0, n))
    const = lambda shape: pl.BlockSpec(shape, lambda n: (0,) * len(shape))
    smem = pl.BlockSpec(memory_space=pltpu.SMEM)
    cur = lambda n: n
    prev = lambda n: jnp.maximum(n - 1, 0)
    return _Part(
        body,
        inputs=[proj, proj, proj, proj, proj, proj, proj, q_norm_w, k_norm_w, sinks, rel_bias_t, bucket_t],
        in_specs=[_proj_piece(4, cur), _proj_piece(5, prev), _proj_piece(5, cur), _proj_piece(6, prev),
                  _proj_piece(6, cur)] + _proj_gate_halves(cur) + [
                  const((1, SWA_D)), const((1, SWA_D)), smem, smem, const((2 * CH, CH))],
        out_specs=[col(512), col(512), const((SWA_D, 1)), const((SWA_D, 1))],
        out_shape=[jax.ShapeDtypeStruct((512, T), F32), jax.ShapeDtypeStruct((512, T), F32),
                   jax.ShapeDtypeStruct((SWA_D, 1), F32), jax.ShapeDtypeStruct((SWA_D, 1), F32)],
        scratch_shapes=[pltpu.VMEM((2, 2 * CH, HQ_LANES), F32), pltpu.VMEM((1, HQ_LANES), F32)])


def _swa_bwd(proj, so, dmix, q_norm_w, k_norm_w, sinks, rel_bias_t, bucket_t):
    def body(sq_ref, skp_ref, skc_ref, svp_ref, svc_ref, sg_lo, sg_hi, so_ref, dm_ref, qw_ref, kw_ref, sinks_ref,
             relb_ref, bucket_ref, d_ref, dbias_ref, dsink_ref, dqw_ref, dkw_ref,
             bias_ref, sink_row, band_dk, band_dv, carry_dk, carry_dv, hold_dq, hold_dg):
        n = pl.program_id(0)
        slot = n % 2

        @pl.when(n == 0)
        def _():
            _build_bias_t(bucket_ref, relb_ref, sinks_ref, bias_ref, sink_row)
            for ref in (dbias_ref, dsink_ref, dqw_ref, dkw_ref, carry_dk, carry_dv):
                ref[...] = jnp.zeros_like(ref)

        qw, kw = qw_ref[...], kw_ref[...]

        @pl.when(n < N_CH)
        def _():
            var = (n == 0).astype(jnp.int32)
            kn_band = jnp.concatenate([_norm_kv(skp_ref[...], kw), _norm_kv(skc_ref[...], kw)], axis=1)
            kb_rm = _bf(kn_band.T)
            kn_band_s = _bf(kn_band * (SWA_D ** -0.5))
            vband_f = jnp.concatenate([svp_ref[...], svc_ref[...]], axis=1)
            vb_rm = _bf(vband_f.T)
            q_raw, q_rstd, qs_b, do_b = [], [], [], []
            for hq in range(SWA_H):
                rows = slice(hq * SWA_D, (hq + 1) * SWA_D)
                q_t = sq_ref[rows, :]
                qn, rq = _rms_t(q_t, qw)
                g = _gate_rows((sg_lo, sg_hi), hq)
                sig = _sigmoid(g)
                dm = dm_ref[rows, :]
                hold_dg[slot, rows, :] = _bf(dm * so_ref[rows, :] * (sig * (1.0 + g * (1.0 - sig))))
                q_raw.append(q_t)
                q_rstd.append(rq)
                qs_b.append(_bf(qn * (SWA_D ** -0.5)))
                do_b.append(_bf(dm * (g * sig)))
            q_all = jnp.concatenate([_kv_slot(qs_b[hq], hq // SWA_G) for hq in range(SWA_H)], axis=1)
            do_all = jnp.concatenate([_kv_slot(do_b[hq], hq // SWA_G) for hq in range(SWA_H)], axis=1)
            probs, p_sink = _softmax_t(_dot(kb_rm, q_all) + bias_ref[var], sink_row[...])
            dprobs = _dot(vb_rm, do_all)
            t = jnp.sum(probs * dprobs, axis=0, keepdims=True)
            dlog = probs * (dprobs - t)
            dsink_ref[...] += -(p_sink * t)
            dbias_ref[...] += dlog
            dlog_b, probs_b = _bf(dlog), _bf(probs)
            dkn, dvv = [], []
            for kv in range(SWA_KV):
                heads = range(kv * SWA_G, (kv + 1) * SWA_G)
                lanes = slice(kv * SWA_G * CH, (kv + 1) * SWA_G * CH)
                dvv.append(_dot_nt(jnp.concatenate([do_b[hq] for hq in heads], axis=1), probs_b[:, lanes]))
                dkn.append(_dot_nt(jnp.concatenate([qs_b[hq] for hq in heads], axis=1), dlog_b[:, lanes]))
                dqn = _dot(kn_band_s[kv * SWA_D:(kv + 1) * SWA_D], dlog_b[:, lanes])
                for j, hq in enumerate(heads):
                    dq_t, dqw_terms = _rms_t_bwd(dqn[:, j * CH:(j + 1) * CH], q_raw[hq], q_rstd[hq], qw)
                    hold_dq[slot, hq * SWA_D:(hq + 1) * SWA_D, :] = _bf(dq_t)
                    dqw_ref[...] += dqw_terms
            band_dk[...] = jnp.concatenate(dkn, axis=0)
            band_dv[...] = jnp.concatenate(dvv, axis=0)

        @pl.when(n == N_CH)
        def _():
            band_dk[...] = jnp.zeros_like(band_dk)
            band_dv[...] = jnp.zeros_like(band_dv)

        @pl.when(n >= 1)
        def _():
            dkn_prev = carry_dk[...] + band_dk[:, 0:CH]
            k_t = skp_ref[...]
            for kv in range(SWA_KV):
                rows = slice(kv * SWA_D, (kv + 1) * SWA_D)
                _, rk = _rms_t(k_t[rows], kw)
                dk_t, dkw_terms = _rms_t_bwd(dkn_prev[rows], k_t[rows], rk, kw)
                d_ref[SWA_OFFS[1] + kv * SWA_D:SWA_OFFS[1] + (kv + 1) * SWA_D, :] = _bf(dk_t)
                dkw_ref[...] += dkw_terms
            d_ref[SWA_OFFS[2]:SWA_OFFS[3], :] = _bf(carry_dv[...] + band_dv[:, 0:CH])
            d_ref[SWA_OFFS[0]:SWA_OFFS[1], :] = hold_dq[1 - slot]
            d_ref[SWA_OFFS[3]:SWA_OFFS[4], :] = hold_dg[1 - slot]

        carry_dk[...] = band_dk[:, CH:2 * CH]
        carry_dv[...] = band_dv[:, CH:2 * CH]

    cur_block = lambda n: jnp.minimum(n, N_CH - 1)
    prev_block = lambda n: jnp.maximum(n - 1, 0)
    col = lambda w: pl.BlockSpec((w, CH), lambda n: (0, cur_block(n)))
    prev = lambda w: pl.BlockSpec((w, CH), lambda n: (0, prev_block(n)))
    const = lambda shape: pl.BlockSpec(shape, lambda n: (0,) * len(shape))
    smem = pl.BlockSpec(memory_space=pltpu.SMEM)
    return _Part(
        body,
        inputs=[proj, proj, proj, proj, proj, proj, proj, so, dmix, q_norm_w, k_norm_w, sinks, rel_bias_t, bucket_t],
        in_specs=[_proj_piece(4, cur_block), _proj_piece(5, prev_block), _proj_piece(5, cur_block),
                  _proj_piece(6, prev_block), _proj_piece(6, cur_block)] + _proj_gate_halves(cur_block) + [
                  col(512), col(512), const((SWA_D, 1)), const((SWA_D, 1)), smem, smem, const((2 * CH, CH))],
        out_specs=[prev(SWA_W), const((2 * CH, HQ_LANES)), const((1, HQ_LANES)),
                   const((SWA_D, CH)), const((SWA_D, CH))],
        out_shape=[jax.ShapeDtypeStruct((SWA_W, T), BF16),
                   jax.ShapeDtypeStruct((2 * CH, HQ_LANES), F32), jax.ShapeDtypeStruct((1, HQ_LANES), F32),
                   jax.ShapeDtypeStruct((SWA_D, CH), F32), jax.ShapeDtypeStruct((SWA_D, CH), F32)],
        scratch_shapes=[pltpu.VMEM((2, 2 * CH, HQ_LANES), F32), pltpu.VMEM((1, HQ_LANES), F32),
                        pltpu.VMEM((128, 2 * CH), F32), pltpu.VMEM((128, 2 * CH), F32),
                        pltpu.VMEM((128, CH), F32), pltpu.VMEM((128, CH), F32),
                        pltpu.VMEM((2, 512, CH), BF16), pltpu.VMEM((2, 512, CH), BF16)])


SMALL_GRAD_SHAPES = ((1, RET_V), (1, CH), (1, CH), (1, CH), (SWA_H, N_BUCKETS))


def _finish_small_grads(dbias_ref, bucket_ref, dsink_ref, dqw_ref, dkw_ref, drw_ref, rw_o, qw_o, kw_o, sink_o, relb_o):
    bt = bucket_ref[...]
    col = lax.broadcasted_iota(jnp.int32, (SWA_H, N_BUCKETS), 1)
    lane = lax.broadcasted_iota(jnp.int32, (1, CH), 1)
    sink = jnp.zeros((1, CH), F32)
    flip = (lax.broadcasted_iota(jnp.int32, (CH, CH), 0) + lax.broadcasted_iota(jnp.int32, (CH, CH), 1) == CH - 1)
    reverse = lambda a: jnp.dot(a, flip.astype(F32), precision=lax.Precision.HIGHEST, preferred_element_type=F32)
    per_distance = []
    for hq in range(SWA_H):
        d = reverse(jnp.where(bt >= 0, dbias_ref[:, _head_lanes(hq)], 0.0))
        d = pltpu.roll(d, 0, 1, stride=1, stride_axis=0)
        per_distance.append(jnp.sum(d, axis=0, keepdims=True))
        sink = sink + jnp.where(lane == hq, jnp.sum(dsink_ref[:, _head_lanes(hq)], axis=1, keepdims=True), 0.0)
    per_distance = reverse(jnp.concatenate(per_distance, axis=0))
    bucket_of_distance = bt[CH:CH + 1, :]
    acc = jnp.zeros((SWA_H, N_BUCKETS), F32)
    for bk in range(N_BUCKETS):
        s = jnp.sum(jnp.where(bucket_of_distance == bk, per_distance, 0.0), axis=1, keepdims=True)
        acc = acc + jnp.where(col == bk, s, 0.0)
    relb_o[...] = acc
    sink_o[...] = sink
    for src, dst in ((dqw_ref, qw_o), (dkw_ref, kw_o)):
        padded = jnp.concatenate([src[...], jnp.zeros((CH - SWA_D, CH), F32)], axis=0)
        dst[...] = jnp.sum(padded.T, axis=0, keepdims=True)
    rw_o[...] = jnp.sum(drw_ref[...].T, axis=0, keepdims=True)


def _out_proj(mix_r, mix_s, wo, x, target):
    def body(mr_ref, ms_ref, w_ref, x_ref, t_ref, loss_ref, dy_ref, dmr_ref, dms_ref, gw_ref):
        i = pl.program_id(0)

        @pl.when(i == 0)
        def _():
            loss_ref[...] = jnp.zeros_like(loss_ref)
            gw_ref[...] = jnp.zeros_like(gw_ref)

        mixed = jnp.concatenate([mr_ref[...], ms_ref[...]], axis=0)
        w = w_ref[...]
        mixed_b = _bf(mixed)
        err = x_ref[...] + _dot_tn(mixed_b, w) - t_ref[...]
        loss_ref[...] += jnp.sum(jnp.sum(err * err, axis=1, keepdims=True), axis=0, keepdims=True)
        dy = err * (1.0 / D)
        dy_ref[...] = dy
        dy_b = _bf(dy)
        dmix = _dot_nt(w, dy_b)
        dmr_ref[...] = dmix[0:512]
        dms_ref[...] = dmix[512:D]
        gw_ref[...] += _dot(mixed_b, dy_b)

    row = lambda w: pl.BlockSpec((TM, w), lambda i: (i, 0))
    col = lambda w: pl.BlockSpec((w, TM), lambda i: (0, i))
    const = lambda shape: pl.BlockSpec(shape, lambda i: (0,) * len(shape))
    return pl.pallas_call(
        body, name="out_proj", grid=(T // TM,),
        in_specs=[col(512), col(512), const((D, D)), row(D), row(D)],
        out_specs=(const((1, 1)), row(D), col(512), col(512), const((D, D))),
        out_shape=(jax.ShapeDtypeStruct((1, 1), F32), jax.ShapeDtypeStruct((T, D), F32),
                   jax.ShapeDtypeStruct((512, T), F32), jax.ShapeDtypeStruct((512, T), F32),
                   jax.ShapeDtypeStruct((D, D), F32)),
        compiler_params=_cparams(1),
    )(mix_r, mix_s, wo, x, target)


LOSS_ROW = 5
ALL_CHIPS = ((0, 0), (0, 1), (1, 0), (1, 1))


def _w_out_reduce_scatter(gwo):
    def body(gwo_ref, gout_o, own, rcv, snd, got, tot, local_sems, a_send, a_recv, b_send, b_recv):
        k = pl.program_id(0)
        x_, y_, c_ = _mesh_pos()
        sibling = (x_, y_, 1 - c_)
        rel_chips = [(1 - x_, y_), (x_, 1 - y_), (1 - x_, 1 - y_)]

        def to_sibling(j):
            px, py = ALL_CHIPS[j]
            return pltpu.make_async_remote_copy(src_ref=gwo_ref.at[_blk(px, py, 1 - c_)], dst_ref=rcv.at[j],
                                                send_sem=a_send.at[j], recv_sem=a_recv.at[j], device_id=sibling,
                                                device_id_type=MESH)

        def local(j):
            px, py = ALL_CHIPS[j]
            return pltpu.make_async_copy(gwo_ref.at[_blk(px, py, c_)], own.at[j], local_sems.at[j])

        def to_chip(q):
            return pltpu.make_async_remote_copy(src_ref=snd.at[q], dst_ref=got.at[q], send_sem=b_send.at[q],
                                                recv_sem=b_recv.at[q], device_id=(*rel_chips[q], c_), device_id_type=MESH)

        @pl.when(k == 0)
        def _():
            for j in range(len(ALL_CHIPS)):
                to_sibling(j).start()
                local(j).start()

        @pl.when(k == 2)
        def _():
            for j in range(len(ALL_CHIPS)):
                local(j).wait()
                to_sibling(j).wait_recv()
            for q in range(3):
                j = 2 * rel_chips[q][0] + rel_chips[q][1]
                snd[q] = _bf(own[j] + rcv[j])
                to_chip(q).start()
            jm = 2 * x_ + y_
            tot[...] = own[jm] + rcv[jm]

        @pl.when(k == N_CH)
        def _():
            g = tot[...]
            for q in range(3):
                to_chip(q).wait_recv()
                g = g + got[q].astype(F32)
            gout_o[...] = g
            for j in range(len(ALL_CHIPS)):
                to_sibling(j).wait_send()
            for q in range(3):
                to_chip(q).wait_send()

    dma = pltpu.SemaphoreType.DMA
    return _Part(
        body, inputs=[gwo], in_specs=[pl.BlockSpec(memory_space=pl.ANY)],
        out_specs=[pl.BlockSpec((WOUT_BLK, D), lambda k: (0, 0))], out_shape=[jax.ShapeDtypeStruct((WOUT_BLK, D), F32)],
        scratch_shapes=[pltpu.VMEM((4, WOUT_BLK, D), F32), pltpu.VMEM((4, WOUT_BLK, D), F32),
                        pltpu.VMEM((3, WOUT_BLK, D), BF16), pltpu.VMEM((3, WOUT_BLK, D), BF16),
                        pltpu.VMEM((WOUT_BLK, D), F32), dma((4,)), dma((4,)), dma((4,)), dma((3,)), dma((3,))])


def _in_proj_bwd_rs(d_ret, d_swa, h, wt, x, norm_w, dy, sse, small_acc):
    n_blocks, n_tiles = N_DEV, T // TM
    last = n_blocks + n_tiles - 1

    def body(da_any, db_any, da_ref, db_ref, h_any, w_any, x_ref, nw_ref, dy_ref, sse_ref,
             dbias_ref, bucket_ref, dsink_ref, dqw_ref, dkw_ref, drw_ref, gx_ref, gin_o, suma_o, sumb_o,
             stage, d2d_src, mine, rcv, snd, got, tot, tab_a, tab_b, gnw_acc, rw_ref, qw_ref, kw_ref, sk_ref, rb_ref,
             h_ref, w_ref, dp_sems, d2d_send, d2d_recv, ici_send, ici_recv, s_send, s_recv, hw_sems):
        k = pl.program_id(0)
        x_, y_, c_ = _mesh_pos()
        h_load = pltpu.make_async_copy(h_any, h_ref, hw_sems.at[0])
        w_load = pltpu.make_async_copy(w_any, w_ref, hw_sems.at[1])
        me = _blk(x_, y_, c_)
        sibling = (x_, y_, 1 - c_)
        rel_chips = [(1 - x_, 1 - y_), (1 - x_, y_), (x_, 1 - y_), (x_, y_)]

        def block_of_step(s):
            return _blk(*rel_chips[s // 2], 1 - c_ if s % 2 == 0 else c_)

        def fetch(s, wait):
            slot, b = s % 2, block_of_step(s)
            split = RET_W - 4 * WIN_BLK

            def run(src, dst, sem):
                cp = pltpu.make_async_copy(src, dst, sem)
                cp.wait() if wait else cp.start()

            @pl.when(b < 4)
            def _():
                run(da_any.at[pl.ds(pl.multiple_of(b * WIN_BLK, 16), WIN_BLK), :], stage.at[slot], dp_sems.at[slot, 0])

            @pl.when(b == 4)
            def _():
                run(da_any.at[pl.ds(4 * WIN_BLK, split), :], stage.at[slot, pl.ds(0, split), :], dp_sems.at[slot, 0])
                run(db_any.at[pl.ds(0, WIN_BLK - split), :], stage.at[slot, pl.ds(split, WIN_BLK - split), :],
                    dp_sems.at[slot, 1])

            @pl.when(b > 4)
            def _():
                run(db_any.at[pl.ds(pl.multiple_of(b * WIN_BLK - RET_W, 16), WIN_BLK), :], stage.at[slot],
                    dp_sems.at[slot, 0])

        def d2d_copy(r):
            return pltpu.make_async_remote_copy(src_ref=d2d_src, dst_ref=rcv.at[r], send_sem=d2d_send.at[r],
                                                recv_sem=d2d_recv.at[r], device_id=sibling, device_id_type=MESH)

        def ici_copy(r):
            return pltpu.make_async_remote_copy(src_ref=snd.at[r], dst_ref=got.at[r], send_sem=ici_send.at[r],
                                                recv_sem=ici_recv.at[r], device_id=(*rel_chips[r], c_),
                                                device_id_type=MESH)

        def table_copies():
            return [pltpu.make_async_remote_copy(src_ref=tab.at[me], dst_ref=tab.at[me], send_sem=s_send.at[a, p - 1],
                                                 recv_sem=s_recv.at[a, p - 1],
                                                 device_id=(x_ ^ (p >> 2), y_ ^ ((p >> 1) & 1), c_ ^ (p & 1)),
                                                 device_id_type=MESH)
                    for p in range(1, N_DEV) for a, tab in enumerate((tab_a, tab_b))]

        def chip_sum(r):
            d2d_copy(r).wait_recv()
            total = mine[...] + rcv[r]
            if r < 3:
                snd[r] = _bf(total)
                ici_copy(r).start()
            else:
                tot[...] = total

        for s in range(n_blocks):
            @pl.when(k == s)
            def _(s=s):
                r = s // 2
                if s == 0:
                    gnw_acc[...] = jnp.zeros_like(gnw_acc)
                    h_load.start()
                    fetch(0, wait=False)
                if s == 1:
                    w_load.start()
                fetch(s, wait=True)
                if s + 1 < n_blocks:
                    fetch(s + 1, wait=False)
                if s == 0:
                    h_load.wait()
                gw = _dot(stage[s % 2], h_ref[...])
                if s % 2 == 0:
                    if s >= 2:
                        chip_sum(r - 1)
                        d2d_copy(r - 1).wait_send()
                    d2d_src[...] = gw
                    d2d_copy(r).start()
                else:
                    mine[...] = gw

        @pl.when(k == n_blocks)
        def _():
            chip_sum(3)
            w_load.wait()

        @pl.when(k >= n_blocks)
        def _():
            dp = jnp.concatenate([da_ref[...], db_ref[...]], axis=0)
            xv, nw = x_ref[...], nw_ref[...]
            r = lax.rsqrt(jnp.mean(xv * xv, axis=-1, keepdims=True) + NORM_EPS)
            dh = _dot_tn(dp, w_ref[...])
            u = dh * nw
            gx_ref[...] = dy_ref[...] + r * u - xv * (r * r * r) * jnp.mean(u * xv, axis=-1, keepdims=True)
            gnw_acc[...] += jnp.sum(dh * (xv * r), axis=0, keepdims=True)

        @pl.when(k == n_blocks)
        def _():
            _finish_small_grads(dbias_ref, bucket_ref, dsink_ref, dqw_ref, dkw_ref, drw_ref,
                                rw_ref, qw_ref, kw_ref, sk_ref, rb_ref)

        @pl.when(k == last)
        def _():
            mine_a, mine_b = tab_a.at[me], tab_b.at[me]
            mine_a[...] = jnp.zeros_like(mine_a)
            mine_b[...] = jnp.zeros_like(mine_b)
            for row, ref in enumerate((gnw_acc, rw_ref, qw_ref, kw_ref, sk_ref)):
                mine_a[row:row + 1, 0:ref.shape[1]] = ref[...]
            mine_a[LOSS_ROW:LOSS_ROW + 1, 0:1] = sse_ref[...]
            mine_b[:, 0:N_BUCKETS] = rb_ref[...]
            tables = table_copies()
            for cp in tables:
                cp.start()
            g_in = tot[...]
            for q in range(3):
                ici_copy(q).wait_recv()
                g_in = g_in + got[q].astype(F32)
            gin_o[...] = g_in
            for cp in tables:
                cp.wait_recv()
            sum_a, sum_b = tab_a[0], tab_b[0]
            for b in range(1, N_DEV):
                sum_a = sum_a + tab_a[b]
                sum_b = sum_b + tab_b[b]
            suma_o[...] = sum_a
            sumb_o[...] = sum_b
            d2d_copy(3).wait_send()
            for q in range(3):
                ici_copy(q).wait_send()
            for cp in tables:
                cp.wait_send()

    tile = lambda k: jnp.maximum(k - n_blocks, 0)
    row = lambda w: pl.BlockSpec((TM, w), lambda k: (tile(k), 0))
    col = lambda w: pl.BlockSpec((w, TM), lambda k: (0, tile(k)))
    const = lambda shape, **kw: pl.BlockSpec(shape, lambda k: (0,) * len(shape), **kw)
    once = dict(pipeline_mode=pl.Buffered(1))
    hbm = pl.BlockSpec(memory_space=pltpu.HBM)
    dma = pltpu.SemaphoreType.DMA
    return pl.pallas_call(
        body, name="in_proj_bwd_rs", grid=(n_blocks + n_tiles,),
        in_specs=[hbm, hbm, col(RET_W), col(SWA_W), hbm, hbm,
                  row(D), const((1, D)), row(D), const((1, 1))]
                 + [const(a.shape) for a in small_acc],
        out_specs=(row(D), const((WIN_BLK, D)), const((8, D)), const((SWA_H, CH))),
        out_shape=(jax.ShapeDtypeStruct((T, D), F32), jax.ShapeDtypeStruct((WIN_BLK, D), F32),
                   jax.ShapeDtypeStruct((8, D), F32), jax.ShapeDtypeStruct((SWA_H, CH), F32)),
        scratch_shapes=[
            pltpu.VMEM((2, WIN_BLK, T), BF16), pltpu.VMEM((WIN_BLK, D), F32),
            pltpu.VMEM((WIN_BLK, D), F32),
            pltpu.VMEM((4, WIN_BLK, D), F32), pltpu.VMEM((3, WIN_BLK, D), BF16),
            pltpu.VMEM((3, WIN_BLK, D), BF16), pltpu.VMEM((WIN_BLK, D), F32),
            pltpu.VMEM((N_DEV, 8, D), F32), pltpu.VMEM((N_DEV, SWA_H, CH), F32),
            pltpu.VMEM((1, D), F32),
        ] + [pltpu.VMEM(s, F32) for s in SMALL_GRAD_SHAPES] + [
            pltpu.VMEM((T, D), BF16), pltpu.VMEM((D_IN, D), BF16),
            dma((2, 2)), dma((4,)), dma((4,)), dma((3,)), dma((3,)), dma((2, 7)), dma((2, 7)), dma((2,)),
        ],
        compiler_params=_cparams(1),
    )(d_ret, d_swa, d_ret, d_swa, h, wt, x, norm_w, dy, sse, *small_acc)


SMALL_SHAPES = ((1, D), (1, 512), (1, SWA_D), (1, SWA_D), (1, SWA_H), (SWA_H, N_BUCKETS))


def _adamw_all(grad_x, g_in_t, g_out, sum_a, sum_b, w_in_t, m_in_t, v_in_t, w_out, m_out, v_out, small_w, small_m, small_v):
    n_small = len(SMALL_SHAPES)
    n_parts = 2

    def body(*refs):
        gx_in = refs[0]
        gin_any, gout_any, suma_any, sumb_any, w_in_ref, m_in_ref, v_in_ref, w_out_ref, m_out_ref, v_out_ref = refs[1:11]
        pos = 11
        sw_refs, sm_refs, sv_refs = (refs[pos + i * n_small:pos + (i + 1) * n_small] for i in range(3))
        pos += 3 * n_small
        g_in_o, d_in_o, nm_in_o, nv_in_o, g_out_o, d_out_o, nm_out_o, nv_out_o = refs[pos:pos + 8]
        pos += 8
        sg_o, sd_o, snm_o, snv_o = (refs[pos + i * n_small:pos + (i + 1) * n_small] for i in range(4))
        pos += 4 * n_small
        loss_o, gx_out, gin_ref, gout_ref, suma_ref, sumb_ref, load_sems = refs[pos:]
        i = pl.program_id(0)
        loads = [pltpu.make_async_copy(src, dst, load_sems.at[j]) for j, (src, dst) in enumerate(
            ((gin_any, gin_ref), (gout_any, gout_ref), (suma_any, suma_ref), (sumb_any, sumb_ref)))]

        @pl.when(i == 0)
        def _():
            for cp in loads:
                cp.start()

        gx_out[...] = gx_in[...]

        @pl.when(i == 0)
        def _():
            for cp in loads:
                cp.wait()

        g = gout_ref[pl.ds(pl.multiple_of(i * (WOUT_BLK // n_parts), 8), WOUT_BLK // n_parts), :]
        d, nm, nv = _adamw(w_out_ref[...], g, m_out_ref[...], v_out_ref[...])
        g_out_o[...], d_out_o[...], nm_out_o[...], nv_out_o[...] = g, d, nm, nv
        g = gin_ref[pl.ds(pl.multiple_of(i * (WIN_BLK // n_parts), 8), WIN_BLK // n_parts), :]
        d, nm, nv = _adamw(w_in_ref[...], g, m_in_ref[...], v_in_ref[...])
        g_in_o[...], d_in_o[...], nm_in_o[...], nv_in_o[...] = g, d, nm, nv

        @pl.when(pl.program_id(0) == 0)
        def _():
            loss_o[...] = suma_ref[LOSS_ROW:LOSS_ROW + 1, 0:1] * (0.5 / D)
            for r, (rows, lanes) in enumerate(SMALL_SHAPES):
                g = suma_ref[r:r + 1, 0:lanes] if rows == 1 else sumb_ref[:, 0:lanes]
                d, nm, nv = _adamw(sw_refs[r][...], g, sm_refs[r][...], sv_refs[r][...])
                sg_o[r][...], sd_o[r][...], snm_o[r][...], snv_o[r][...] = g, d, nm, nv

    part = lambda rows: pl.BlockSpec((rows // n_parts, D), lambda i: (i, 0))
    const = lambda shape: pl.BlockSpec(shape, lambda i: (0,) * len(shape))
    win = jax.ShapeDtypeStruct((WIN_BLK, D), F32)
    wout = jax.ShapeDtypeStruct((WOUT_BLK, D), F32)
    smalls = tuple(jax.ShapeDtypeStruct(s, F32) for s in SMALL_SHAPES)
    small_specs = [const(s) for s in SMALL_SHAPES]
    hbm = pl.BlockSpec(memory_space=pltpu.HBM)
    outs = pl.pallas_call(
        body, name="adamw_all", grid=(n_parts,),
        in_specs=[part(T)] + [hbm] * 4 + [part(WIN_BLK)] * 3 + [part(WOUT_BLK)] * 3 + small_specs * 3,
        out_specs=tuple([part(WIN_BLK)] * 4 + [part(WOUT_BLK)] * 4 + small_specs * 4 + [const((1, 1)), part(T)]),
        out_shape=(win,) * 4 + (wout,) * 4 + smalls * 4 + (jax.ShapeDtypeStruct((1, 1), F32),
                                                           jax.ShapeDtypeStruct(grad_x.shape, F32)),
        scratch_shapes=[pltpu.VMEM((WIN_BLK, D), F32), pltpu.VMEM((WOUT_BLK, D), F32), pltpu.VMEM((8, D), F32),
                        pltpu.VMEM((SWA_H, CH), F32), pltpu.SemaphoreType.DMA((4,))],
        compiler_params=_cparams(1, vmem=VMEM_WHOLE),
    )(grad_x, g_in_t, g_out, sum_a, sum_b, w_in_t, m_in_t, v_in_t, w_out, m_out, v_out, *small_w, *small_m, *small_v)
    big, rest = outs[:8], outs[8:]
    return big, [rest[i * n_small:(i + 1) * n_small] for i in range(4)], rest[4 * n_small], rest[4 * n_small + 1]


def _small_rows(norm_w, ret_norm_w, q_norm_w, k_norm_w, sinks, rel_bias):
    return (norm_w.reshape(1, D), ret_norm_w.reshape(1, 512), q_norm_w.reshape(1, SWA_D), k_norm_w.reshape(1, SWA_D),
            sinks.reshape(1, SWA_H), rel_bias.T)


def _small_leaves(rows):
    return (rows[0].reshape(D), rows[1].reshape(512), rows[2].reshape(SWA_D), rows[3].reshape(SWA_D),
            rows[4].reshape(SWA_H), rows[5].T)


def kernel(x, norm_w, w_in, ret_norm_w, q_norm_w, k_norm_w, sinks, rel_bias, w_out, loss_target, m_norm_w, m_w_in, m_ret_norm_w, m_q_norm_w, m_k_norm_w, m_sinks, m_rel_bias, m_w_out, v_norm_w, v_w_in, v_ret_norm_w, v_q_norm_w, v_k_norm_w, v_sinks, v_rel_bias, v_w_out):
    x2 = x.reshape(T, D)
    target = loss_target.reshape(T, D)
    nw = norm_w.reshape(1, D)
    relb = rel_bias.T
    ret_tables = _rotary_tables_t() + _retention_tables_t()
    bucket_t = _bucket_table_t()

    proj, wt, h = _in_proj_gather(x2, nw, w_in.T)
    (ro, mix_r, states, wo, rnw_col), (so, mix_s, qnw_col, knw_col) = _fused_call("attn_fwd", N_CH, [
        _retention_fwd(proj, ret_norm_w.reshape(1, RET_V), ret_tables, w_out),
        _swa_fwd(proj, q_norm_w.reshape(1, SWA_D), k_norm_w.reshape(1, SWA_D), sinks, relb, bucket_t)])
    sse, dy, dmix_r, dmix_s, gwo = _out_proj(mix_r, mix_s, wo.reshape(D, D), x2, target)

    (d_ret, drw_acc), (d_swa, dbias, dsink_acc, dqw_acc, dkw_acc), (g_out,) = _fused_call("attn_bwd", N_CH + 1, [
        _retention_bwd(proj, ro, states, dmix_r, rnw_col, ret_tables),
        _swa_bwd(proj, so, dmix_s, qnw_col, knw_col, sinks, relb, bucket_t),
        _w_out_reduce_scatter(gwo.reshape(N_DEV, WOUT_BLK, D))])
    grad_x, g_in_t, sum_a, sum_b = _in_proj_bwd_rs(d_ret, d_swa, h, wt, x2, nw, dy, sse,
                                                   (dbias, bucket_t, dsink_acc, dqw_acc, dkw_acc, drw_acc))

    small_w = _small_rows(norm_w, ret_norm_w, q_norm_w, k_norm_w, sinks, rel_bias)
    small_m = _small_rows(m_norm_w, m_ret_norm_w, m_q_norm_w, m_k_norm_w, m_sinks, m_rel_bias)
    small_v = _small_rows(v_norm_w, v_ret_norm_w, v_q_norm_w, v_k_norm_w, v_sinks, v_rel_bias)
    big, small, loss, grad_x = _adamw_all(grad_x, g_in_t, g_out, sum_a, sum_b, w_in.T, m_w_in.T, v_w_in.T, w_out, m_w_out, v_w_out,
                                  small_w, small_m, small_v)

    def leaves(i):
        a = _small_leaves(small[i])
        return (a[0], big[i].T, a[1], a[2], a[3], a[4], a[5], big[4 + i])

    return (loss.reshape(()), grad_x.reshape(1, T, D), *leaves(0), *leaves(1), *leaves(2), *leaves(3))
```

```python
from typing import Callable, NamedTuple

import numpy as np
import jax
import jax.numpy as jnp
from jax import lax
from jax.experimental import pallas as pl
from jax.experimental.pallas import tpu as pltpu

F32 = jnp.float32
BF16 = jnp.bfloat16
MESH = pl.DeviceIdType.MESH

T = 2048
D = 1024
D_IN = 2816
N_DEV = 8
WIN_BLK = D_IN // N_DEV
WOUT_BLK = D // N_DEV
CH = 128
N_CH = T // CH
RET_H, RET_DK, RET_DV = 4, 64, 128
RET_QK = RET_H * RET_DK
RET_V = RET_H * RET_DV
SWA_H, SWA_KV, SWA_D, SWA_G = 8, 2, 64, 4
N_BUCKETS = 32
NORM_EPS = 1e-6
GN_EPS = 1e-5
NEG_INF = -1e30
PIECES = (256, 256, 512, 512, 512, 128, 128, 512)
OFFS = tuple(int(v) for v in np.cumsum((0,) + PIECES))
RET_W = OFFS[4]
SWA_W = D_IN - RET_W
SWA_OFFS = tuple(o - RET_W for o in OFFS[4:])
TM = 512

ADAM_LR, ADAM_B1, ADAM_B2, ADAM_EPS, ADAM_WD, ADAM_STEP = 0.001, 0.9, 0.999, 1e-08, 0.01, 10

VMEM_LIMIT = 56 * 1024 * 1024
VMEM_V7X = 64 * 1024 * 1024
VMEM_WHOLE = VMEM_V7X - 1024 * 1024


def _cparams(n_grid=0, vmem=VMEM_LIMIT):
    sem = ("arbitrary",) * n_grid if n_grid else None
    return pltpu.CompilerParams(dimension_semantics=sem, vmem_limit_bytes=vmem)


class _Part(NamedTuple):
    step: Callable
    inputs: list
    in_specs: list
    out_specs: list
    out_shape: list
    scratch_shapes: list


def _fused_call(name, n_steps, parts):
    n_in = [len(p.inputs) for p in parts]
    n_out = [len(p.out_shape) for p in parts]
    n_scr = [len(p.scratch_shapes) for p in parts]

    def body(*refs):
        ins, outs, scr = refs[:sum(n_in)], refs[sum(n_in):sum(n_in) + sum(n_out)], refs[sum(n_in) + sum(n_out):]
        for i, p in enumerate(parts):
            take = lambda seq, counts: seq[sum(counts[:i]):sum(counts[:i + 1])]
            p.step(*take(ins, n_in), *take(outs, n_out), *take(scr, n_scr))

    flat = lambda field: [v for p in parts for v in getattr(p, field)]
    outs = pl.pallas_call(
        body, name=name, grid=(n_steps,), in_specs=flat("in_specs"), out_specs=tuple(flat("out_specs")),
        out_shape=tuple(flat("out_shape")), scratch_shapes=flat("scratch_shapes"), compiler_params=_cparams(1),
    )(*flat("inputs"))
    return [list(outs[sum(n_out[:i]):sum(n_out[:i + 1])]) for i in range(len(parts))]


def _dot(a, b):
    return jnp.dot(a, b, preferred_element_type=F32)


def _dot_nt(a, b):
    return lax.dot_general(a, b, (((1,), (1,)), ((), ())), preferred_element_type=F32)


def _dot_tn(a, b):
    return lax.dot_general(a, b, (((0,), (0,)), ((), ())), preferred_element_type=F32)


def _bf(a):
    return a.astype(BF16)


def _sigmoid(x):
    return 1.0 / (1.0 + jnp.exp(-x))


def _adamw(w, g, m, v):
    m = ADAM_B1 * m + (1.0 - ADAM_B1) * g
    v = ADAM_B2 * v + (1.0 - ADAM_B2) * (g * g)
    m_hat = m / (1.0 - ADAM_B1 ** ADAM_STEP)
    v_hat = v / (1.0 - ADAM_B2 ** ADAM_STEP)
    delta = -ADAM_LR * (m_hat / (jnp.sqrt(v_hat) + ADAM_EPS) + ADAM_WD * w)
    return delta, m, v


def _rotary_tables_t():
    half = RET_DK // 2
    inv_freq = np.float32(10000.0) ** (-np.arange(half, dtype=np.float32) / np.float32(half))
    ang = inv_freq[:, None] * np.arange(T, dtype=np.float32)[None, :]
    cos, sin = np.cos(ang).astype(np.float32), np.sin(ang).astype(np.float32)
    cos64 = np.concatenate([cos, cos], axis=0)
    sin64 = np.concatenate([-sin, sin], axis=0)
    return np.tile(cos64, (RET_H, 1)), np.tile(sin64, (RET_H, 1))


def _retention_tables_t():
    gamma = (1.0 - np.exp2(-5.0 - np.arange(RET_H, dtype=np.float32))).astype(np.float32)
    log_g = np.log(gamma).astype(np.float32)
    i = np.arange(CH, dtype=np.float32)
    diff = i[None, :] - i[:, None]
    decay = np.where(diff >= 0, np.exp(log_g[:, None, None] * np.maximum(diff, 0.0)), 0.0).astype(np.float32)
    decay_all = np.concatenate(list(decay), axis=1)
    zeta = np.exp(log_g[:, None] * (CH - 1.0 - i)).astype(np.float32)
    zeta_tab = np.repeat(zeta.T, RET_DK, axis=1)
    xi = np.exp(log_g[:, None] * (i + 1.0)).astype(np.float32)
    xi_tab = np.repeat(xi, RET_DK, axis=0)
    chunk_decay = np.exp(log_g * np.float32(CH)).astype(np.float32)
    row_head = np.arange(RET_V)[:, None] // RET_DV
    col_head = np.arange(RET_QK)[None, :] // RET_DK
    state_mask = (row_head == col_head).astype(np.float32)
    state_decay = (state_mask * chunk_decay[row_head]).astype(np.float32)
    q_mask = (np.arange(RET_QK)[:, None] // RET_DK == np.arange(RET_H * CH)[None, :] // CH).astype(np.float32)
    return (np.ascontiguousarray(decay_all), np.ascontiguousarray(zeta_tab), np.ascontiguousarray(xi_tab),
            state_mask, state_decay, q_mask)


def _bucket_table_t():
    qi = np.arange(CH)[:, None]
    kj = np.arange(2 * CH)[None, :]
    dist = qi + CH - kj
    n = np.maximum(dist, 0)
    max_exact = N_BUCKETS // 2
    nf = np.maximum(n, 1).astype(np.float32)
    large = max_exact + (np.log(nf / np.float32(max_exact)) / np.float32(np.log(CH / max_exact))
                         * np.float32(N_BUCKETS - max_exact)).astype(np.int32)
    large = np.minimum(large, N_BUCKETS - 1)
    bucket = np.where(n < max_exact, n, large)
    return np.ascontiguousarray(np.where((dist >= 0) & (dist < CH), bucket, -1).astype(np.int32).T)


def _mesh_pos():
    return lax.axis_index("x"), lax.axis_index("y"), lax.axis_index("c")


def _blk(px, py, pc):
    return 4 * px + 2 * py + pc


def _in_proj_gather(x, norm_w, w_in_t):
    tchunk = 1024

    def body(x_any, nw_any, win_any, proj_ref, wt_out, h_out, wt_ref, h_ref, stage, x_ref, nw_ref, win_ref,
             send_sems, recv_sems, out_sems, wb_sems, h_sem, in_sems):
        n_tiles = T // TM
        loads = [pltpu.make_async_copy(win_any, win_ref, in_sems.at[0]),
                 pltpu.make_async_copy(nw_any, nw_ref, in_sems.at[1])]
        loads += [pltpu.make_async_copy(x_any.at[pl.ds(i * TM, TM), :], x_ref.at[pl.ds(i * TM, TM), :], in_sems.at[2 + i])
                  for i in range(n_tiles)]
        for cp in loads:
            cp.start()
        loads[0].wait()
        x, y, c = _mesh_pos()
        me = _blk(x, y, c)
        sibling = (x, y, 1 - c)
        nbr_a, nbr_b, diag = (x ^ (1 - c), y ^ c), (x ^ c, y ^ (1 - c)), (1 - x, 1 - y)

        def copy(k, b, to):
            return pltpu.make_async_remote_copy(src_ref=wt_ref.at[b], dst_ref=wt_ref.at[b], send_sem=send_sems.at[k],
                                                recv_sem=recv_sems.at[k], device_id=to, device_id_type=MESH)

        wt_ref[me] = _bf(win_ref[...])
        first = [copy(0, me, sibling), copy(1, me, (*nbr_a, c))]
        for cp in first:
            cp.start()
        to_b = copy(2, me, (*nbr_b, c))

        loads[1].wait()
        nw = nw_ref[...]
        w_mine = wt_ref[me]
        for i in range(n_tiles):
            rows = slice(i * TM, (i + 1) * TM)
            loads[2 + i].wait()
            xv = x_ref[rows, :]
            r = lax.rsqrt(jnp.mean(xv * xv, axis=-1, keepdims=True) + NORM_EPS)
            h_ref[rows, :] = _bf(xv * r * nw)
            stage[0, :, rows] = _dot_nt(w_mine, h_ref[rows, :])
        keep = [pltpu.make_async_copy(h_ref, h_out, h_sem)]
        keep[0].start()

        writes = []

        def project(b, staged=False):
            k = len(writes)
            if k >= 2:
                writes[k - 2].wait()
            keep.append(pltpu.make_async_copy(wt_ref.at[b], wt_out.at[b], wb_sems.at[k]))
            keep[-1].start()
            if not staged:
                w = wt_ref[b]
                for t in range(T // tchunk):
                    cols = slice(t * tchunk, (t + 1) * tchunk)
                    stage[k % 2, :, cols] = _dot_nt(w, h_ref[cols, :])
            cp = pltpu.make_async_copy(stage.at[k % 2], proj_ref.at[b], out_sems.at[k % 2])
            cp.start()
            writes.append(cp)

        here = (x, y, c)
        project(me, staged=True)
        copy(0, _blk(x, y, 1 - c), here).wait_recv()
        project(_blk(x, y, 1 - c))
        to_b.start()
        passed = [to_b]

        def landed(k, chip, relay_to=None):
            copy(k, _blk(*chip, c), here).wait_recv()
            out = [copy({1: 4, 2: 5, 3: 6}[k], _blk(*chip, c), sibling)]
            if relay_to is not None:
                out.append(copy(3, _blk(*chip, c), (*relay_to, c)))
            for cp in out:
                cp.start()
            passed.extend(out)
            project(_blk(*chip, c))

        def from_sibling(k, chip):
            copy(k, _blk(*chip, 1 - c), here).wait_recv()
            project(_blk(*chip, 1 - c))

        landed(1, nbr_a, relay_to=nbr_b)
        from_sibling(4, nbr_b)
        landed(2, nbr_b)
        from_sibling(5, nbr_a)
        landed(3, diag)
        from_sibling(6, diag)
        for cp in writes[-2:] + keep:
            cp.wait()
        for cp in first + passed:
            cp.wait_send()

    vm = pl.BlockSpec(memory_space=pltpu.VMEM)
    hbm = pl.BlockSpec(memory_space=pltpu.HBM)
    dma = pltpu.SemaphoreType.DMA
    proj, wt, h = pl.pallas_call(
        body, name="in_proj_gather",
        out_shape=(jax.ShapeDtypeStruct((N_DEV, WIN_BLK, T), F32), jax.ShapeDtypeStruct((N_DEV, WIN_BLK, D), BF16),
                   jax.ShapeDtypeStruct((T, D), BF16)),
        in_specs=[hbm, hbm, hbm], out_specs=(hbm, hbm, hbm),
        scratch_shapes=[pltpu.VMEM((N_DEV, WIN_BLK, D), BF16), pltpu.VMEM((T, D), BF16), pltpu.VMEM((2, WIN_BLK, T), F32),
                        pltpu.VMEM((T, D), F32), pltpu.VMEM((1, D), F32), pltpu.VMEM((WIN_BLK, D), F32),
                        dma((7,)), dma((7,)), dma((2,)), dma((N_DEV,)), dma, dma((2 + T // TM,))],
        compiler_params=_cparams(vmem=VMEM_WHOLE),
    )(x, norm_w, w_in_t)
    return proj.reshape(D_IN, T), wt.reshape(D_IN, D), h


def _gather_w_out(w_out_ref, own_ref, wo_ref, local_sem, send_sems, recv_sems, phase):
    x, y, c = _mesh_pos()
    me = _blk(x, y, c)
    sibling = (x, y, 1 - c)
    chips = [(1 - x, y), (x, 1 - y), (1 - x, 1 - y)]

    def copy(k, b, to, src=None):
        return pltpu.make_async_remote_copy(src_ref=wo_ref.at[b] if src is None else src, dst_ref=wo_ref.at[b],
                                            send_sem=send_sems.at[k], recv_sem=recv_sems.at[k], device_id=to,
                                            device_id_type=MESH)

    local = pltpu.make_async_copy(own_ref, wo_ref.at[me], local_sem)
    first = [copy(0, me, sibling, own_ref)] + [copy(1 + j, me, (*chip, c), own_ref) for j, chip in enumerate(chips)]
    passed = [copy(4 + j, _blk(*chip, c), sibling) for j, chip in enumerate(chips)]
    if phase == 0:
        own_ref[...] = _bf(w_out_ref[...])
        local.start()
        for cp in first:
            cp.start()
    elif phase == 1:
        for j, chip in enumerate(chips):
            copy(1 + j, _blk(*chip, c), (x, y, c)).wait_recv()
            passed[j].start()
    else:
        local.wait()
        copy(0, _blk(x, y, 1 - c), (x, y, c)).wait_recv()
        for j, chip in enumerate(chips):
            copy(4 + j, _blk(*chip, 1 - c), (x, y, c)).wait_recv()
        for cp in first + passed:
            cp.wait_send()


def _proj_piece(i, block_of):
    rows = PIECES[i]
    assert OFFS[i] % rows == 0
    return pl.BlockSpec((rows, CH), lambda n: (OFFS[i] // rows, block_of(n)))


def _proj_gate_halves(block_of):
    return [pl.BlockSpec((256, CH), lambda n, j=j: (OFFS[7] // 256 + j, block_of(n))) for j in range(2)]


def _swap_halves_t(t):
    half = RET_DK // 2
    parts = []
    for h in range(RET_H):
        parts += [t[h * RET_DK + half:(h + 1) * RET_DK], t[h * RET_DK:h * RET_DK + half]]
    return jnp.concatenate(parts, axis=0)


def _rotate(t, cos, sin):
    return t * cos + _swap_halves_t(t) * sin


def _group_norm_t(o):
    mu = jnp.mean(o, axis=0, keepdims=True)
    var = jnp.mean((o - mu) * (o - mu), axis=0, keepdims=True)
    rstd = lax.rsqrt(var + GN_EPS)
    return (o - mu) * rstd, rstd


def _retention_scores_t(q_t, k_rm_b, q_mask, decay_all):
    q_heads = _bf(jnp.concatenate([q_t] * RET_H, axis=1) * q_mask)
    return _dot(k_rm_b, q_heads) * decay_all


def _row_to_col(row):
    n = row.shape[1]
    diagonal = lax.broadcasted_iota(jnp.int32, (n, n), 0) == lax.broadcasted_iota(jnp.int32, (n, n), 1)
    return jnp.sum(jnp.where(diagonal, row, 0.0), axis=1, keepdims=True)


def _retention_fwd(proj, ret_norm_w, tables, w_out_blk):
    cos_t, sin_t, decay_all, zeta_tab, xi_tab, state_mask, state_decay, q_mask = tables

    def body(rq_ref, rk_ref, rv_ref, rg_ref, w_row, cos_ref, sin_ref, decay_ref, zeta_ref, xi_ref, smask_ref,
             sdecay_ref, qmask_ref, wout_ref, ro_ref, mix_ref, st_ref, wo_ref, w_ref,
             state, wo_own, wo_local, wo_send, wo_recv):
        n = pl.program_id(0)

        @pl.when(n == 0)
        def _():
            w_ref[...] = _row_to_col(w_row[...])
            state[...] = jnp.zeros_like(state)
            _gather_w_out(wout_ref, wo_own, wo_ref, wo_local, wo_send, wo_recv, phase=0)

        @pl.when(n == N_CH // 2)
        def _():
            _gather_w_out(wout_ref, wo_own, wo_ref, wo_local, wo_send, wo_recv, phase=1)

        @pl.when(n == N_CH - 1)
        def _():
            _gather_w_out(wout_ref, wo_own, wo_ref, wo_local, wo_send, wo_recv, phase=2)

        cos, sin = cos_ref[...], sin_ref[...]
        q_t = _rotate(rq_ref[...], cos, sin)
        k_t = _rotate(rk_ref[...], cos, sin) * (RET_DK ** -0.5)
        k_rm = k_t.T
        v_b = _bf(rv_ref[...])
        m_b = _bf(state[...])
        st_ref[0] = m_b
        scores_b = _bf(_retention_scores_t(q_t, _bf(k_rm), qmask_ref[...], decay_ref[...]))
        cross = _dot(m_b, _bf(q_t * xi_ref[...]))
        state[...] = state[...] * sdecay_ref[...] + _dot(v_b, _bf(k_rm * zeta_ref[...])) * smask_ref[...]
        heads = range(RET_H)
        rows = [slice(h * RET_DV, (h + 1) * RET_DV) for h in heads]
        o = [_dot(v_b[rows[h]], scores_b[:, h * CH:(h + 1) * CH]) + cross[rows[h]] for h in heads]
        for h in heads:
            rn, _ = _group_norm_t(o[h])
            g = rg_ref[rows[h], :]
            ro_ref[rows[h], :] = o[h]
            mix_ref[rows[h], :] = rn * w_ref[rows[h], :] * (g * _sigmoid(g))

    col = lambda w: pl.BlockSpec((w, CH), lambda n: (0, n))
    const = lambda shape: pl.BlockSpec(shape, lambda n: (0,) * len(shape))
    cur = lambda n: n
    return _Part(
        body,
        inputs=[proj, proj, proj, proj, ret_norm_w, cos_t, sin_t, decay_all, zeta_tab, xi_tab, state_mask, state_decay,
                q_mask, w_out_blk],
        in_specs=[_proj_piece(i, cur) for i in range(4)] + [
            const((1, RET_V)), col(RET_QK), col(RET_QK),
            const(decay_all.shape), const(zeta_tab.shape), const(xi_tab.shape), const(state_mask.shape),
            const(state_decay.shape), const(q_mask.shape), const((WOUT_BLK, D))],
        out_specs=[col(RET_V), col(RET_V), pl.BlockSpec((1, RET_V, RET_QK), lambda n: (n, 0, 0)),
                   pl.BlockSpec(memory_space=pl.ANY), const((RET_V, 1))],
        out_shape=[jax.ShapeDtypeStruct((RET_V, T), F32), jax.ShapeDtypeStruct((RET_V, T), F32),
                   jax.ShapeDtypeStruct((N_CH, RET_V, RET_QK), BF16), jax.ShapeDtypeStruct((N_DEV, WOUT_BLK, D), BF16),
                   jax.ShapeDtypeStruct((RET_V, 1), F32)],
        scratch_shapes=[pltpu.VMEM((RET_V, RET_QK), F32), pltpu.VMEM((WOUT_BLK, D), BF16), pltpu.SemaphoreType.DMA,
                        pltpu.SemaphoreType.DMA((N_DEV - 1,)), pltpu.SemaphoreType.DMA((N_DEV - 1,))])


def _retention_bwd(proj, ro, states, dmix, ret_norm_w, tables):
    cos_t, sin_t, decay_all, zeta_tab, xi_tab, state_mask, state_decay, q_mask = tables

    def body(*refs):
        @pl.when(pl.program_id(0) < N_CH)
        def _():
            chunk(*refs)

    def chunk(rq_ref, rk_ref, rv_ref, rg_ref, ro_ref, st_ref, dm_ref, w_ref, cos_ref, sin_ref, decay_ref, zeta_ref,
              xi_ref, smask_ref, sdecay_ref, qmask_ref, d_ref, dw_ref, gstate):
        i = pl.program_id(0)

        @pl.when(i == 0)
        def _():
            gstate[...] = jnp.zeros_like(gstate)
            dw_ref[...] = jnp.zeros_like(dw_ref)

        cos, sin = cos_ref[...], sin_ref[...]
        q_t = _rotate(rq_ref[...], cos, sin)
        k_t = _rotate(rk_ref[...], cos, sin) * (RET_DK ** -0.5)
        q_b, k_b = _bf(q_t), _bf(k_t)
        k_rm = k_t.T
        kz_b = _bf(k_rm * zeta_ref[...])
        qx_b = _bf(q_t * xi_ref[...])
        v_t = rv_ref[...]
        v_b = _bf(v_t)
        v_rm_b = _bf(v_t.T)
        decay = decay_ref[...]
        scores_b = _bf(_retention_scores_t(q_t, _bf(k_rm), qmask_ref[...], decay))
        heads = range(RET_H)
        rows = [slice(h * RET_DV, (h + 1) * RET_DV) for h in heads]
        qk_rows = [slice(h * RET_DK, (h + 1) * RET_DK) for h in heads]
        lanes = [slice(h * CH, (h + 1) * CH) for h in heads]
        do = []
        for h in heads:
            g, w, dm = rg_ref[rows[h], :], w_ref[rows[h], :], dm_ref[rows[h], :]
            rn, rstd = _group_norm_t(ro_ref[rows[h], :])
            sig = _sigmoid(g)
            silu = g * sig
            d_ref[OFFS[3] + h * RET_DV:OFFS[3] + (h + 1) * RET_DV, :] = _bf(dm * rn * w * (sig * (1.0 + g * (1.0 - sig))))
            dw_ref[rows[h], :] += dm * silu * rn
            drn = dm * silu * w
            do.append(rstd * (drn - jnp.mean(drn, axis=0, keepdims=True)
                              - rn * jnp.mean(drn * rn, axis=0, keepdims=True)))
        do_b = _bf(jnp.concatenate(do, axis=0))
        m_b = st_ref[0]
        g_all = gstate[...]
        g_b = _bf(g_all)
        dscores_b = [_bf(_dot(v_rm_b[:, rows[h]], do_b[rows[h]]) * decay[:, lanes[h]]) for h in heads]
        dq_cross = lax.dot_general(m_b, do_b, (((0,), (0,)), ((), ())), preferred_element_type=F32)
        dkz = _dot(v_rm_b, g_b)
        dv_cross = _dot_nt(g_b, kz_b)
        gstate[...] = g_all * sdecay_ref[...] + _dot_nt(do_b, qx_b) * smask_ref[...]
        dq = jnp.concatenate([_dot(k_b[qk_rows[h]], dscores_b[h]) for h in heads], axis=0) + dq_cross * xi_ref[...]
        dk = (jnp.concatenate([_dot_nt(q_b[qk_rows[h]], dscores_b[h]) for h in heads], axis=0)
              + (dkz * zeta_ref[...]).T) * (RET_DK ** -0.5)
        for h in heads:
            d_ref[OFFS[2] + h * RET_DV:OFFS[2] + (h + 1) * RET_DV, :] = _bf(
                _dot_nt(do_b[rows[h]], scores_b[:, lanes[h]]) + dv_cross[rows[h]])
        d_ref[OFFS[0]:OFFS[1], :] = _bf(dq * cos + _swap_halves_t(dq * sin))
        d_ref[OFFS[1]:OFFS[2], :] = _bf(dk * cos + _swap_halves_t(dk * sin))

    chunk_of = lambda i: N_CH - 1 - jnp.minimum(i, N_CH - 1)
    col = lambda w: pl.BlockSpec((w, CH), lambda i: (0, chunk_of(i)))
    const = lambda shape: pl.BlockSpec(shape, lambda i: (0,) * len(shape))
    return _Part(
        body,
        inputs=[proj, proj, proj, proj, ro, states, dmix, ret_norm_w, cos_t, sin_t, decay_all, zeta_tab, xi_tab,
                state_mask, state_decay, q_mask],
        in_specs=[_proj_piece(j, chunk_of) for j in range(4)] + [
                  col(RET_V),
                  pl.BlockSpec((1, RET_V, RET_QK), lambda i: (chunk_of(i), 0, 0)), col(RET_V),
                  const((RET_V, 1)), col(RET_QK), col(RET_QK),
                  const(decay_all.shape), const(zeta_tab.shape), const(xi_tab.shape), const(state_mask.shape),
                  const(state_decay.shape), const(q_mask.shape)],
        out_specs=[col(RET_W), const((RET_V, CH))],
        out_shape=[jax.ShapeDtypeStruct((RET_W, T), BF16), jax.ShapeDtypeStruct((RET_V, CH), F32)],
        scratch_shapes=[pltpu.VMEM((RET_V, RET_QK), F32)])


HQ_LANES = SWA_H * CH


def _head_lanes(hq):
    return slice(hq * CH, (hq + 1) * CH)


def _build_bias_t(bucket_ref, relb_ref, sinks_ref, bias_ref, sink_row):
    bt = bucket_ref[...]
    first = lax.broadcasted_iota(jnp.int32, bt.shape, 0) < CH
    for hq in range(SWA_H):
        b = jnp.full(bt.shape, NEG_INF, F32)
        for bk in range(N_BUCKETS):
            b = jnp.where(bt == bk, relb_ref[hq, bk], b)
        bias_ref[0, :, _head_lanes(hq)] = b
        bias_ref[1, :, _head_lanes(hq)] = jnp.where(first, NEG_INF, b)
        sink_row[:, _head_lanes(hq)] = jnp.full((1, CH), sinks_ref[hq], F32)


def _rms_t(t, w_col):
    r = lax.rsqrt(jnp.mean(t * t, axis=0, keepdims=True) + NORM_EPS)
    return t * r * w_col, r


def _rms_t_bwd(dn, t, r, w_col):
    u = dn * w_col
    return r * u - t * (r * r * r) * jnp.mean(u * t, axis=0, keepdims=True), dn * t * r


def _norm_kv(k_t, kw):
    return jnp.concatenate([_rms_t(k_t[g * SWA_D:(g + 1) * SWA_D], kw)[0] for g in range(SWA_KV)], axis=0)


def _kv_slot(a, kv):
    z = jnp.zeros_like(a)
    return jnp.concatenate([a, z] if kv == 0 else [z, a], axis=0)


def _softmax_t(s, sink):
    m = jnp.maximum(jnp.max(s, axis=0, keepdims=True), sink)
    p = jnp.exp(s - m)
    e_sink = jnp.exp(sink - m)
    inv = 1.0 / (jnp.sum(p, axis=0, keepdims=True) + e_sink)
    return p * inv, e_sink * inv


def _gate_rows(sg_halves, hq):
    per_half = SWA_H // 2
    return sg_halves[hq // per_half][(hq % per_half) * SWA_D:(hq % per_half + 1) * SWA_D, :]


def _swa_fwd(proj, q_norm_w, k_norm_w, sinks, rel_bias_t, bucket_t):
    def body(sq_ref, skp_ref, skc_ref, svp_ref, svc_ref, sg_lo, sg_hi, qw_row, kw_row, sinks_ref, relb_ref, bucket_ref,
             so_ref, mix_ref, qw_ref, kw_ref, bias_ref, sink_row):
        n = pl.program_id(0)

        @pl.when(n == 0)
        def _():
            qw_ref[...] = _row_to_col(qw_row[...])
            kw_ref[...] = _row_to_col(kw_row[...])
            _build_bias_t(bucket_ref, relb_ref, sinks_ref, bias_ref, sink_row)

        var = (n == 0).astype(jnp.int32)
        qw, kw = qw_ref[...], kw_ref[...]
        kn_band = jnp.concatenate([_norm_kv(skp_ref[...], kw), _norm_kv(skc_ref[...], kw)], axis=1)
        kb_rm = _bf(kn_band.T)
        vband = _bf(jnp.concatenate([svp_ref[...], svc_ref[...]], axis=1))
        q_all = jnp.concatenate(
            [_bf(_kv_slot(_rms_t(sq_ref[hq * SWA_D:(hq + 1) * SWA_D, :], qw)[0] * (SWA_D ** -0.5), hq // SWA_G))
             for hq in range(SWA_H)], axis=1)
        probs, _ = _softmax_t(_dot(kb_rm, q_all) + bias_ref[var], sink_row[...])
        probs_b = _bf(probs)
        for kv in range(SWA_KV):
            o = _dot(vband[kv * SWA_D:(kv + 1) * SWA_D], probs_b[:, kv * SWA_G * CH:(kv + 1) * SWA_G * CH])
            for j in range(SWA_G):
                rows = slice((kv * SWA_G + j) * SWA_D, (kv * SWA_G + j + 1) * SWA_D)
                oh = o[:, j * CH:(j + 1) * CH]
                g = _gate_rows((sg_lo, sg_hi), kv * SWA_G + j)
                so_ref[rows, :] = oh
                mix_ref[rows, :] = oh * (g * _sigmoid(g))

    col = lambda w: pl.BlockSpec((w, CH), lambda n: (0, n))
    const = lambda shape: pl.BlockSpec(shape, lambda n: (0,) * len(shape))
    smem = pl.BlockSpec(memory_space=pltpu.SMEM)
    cur = lambda n: n
    prev = lambda n: jnp.maximum(n - 1, 0)
    return _Part(
        body,
        inputs=[proj, proj, proj, proj, proj, proj, proj, q_norm_w, k_norm_w, sinks, rel_bias_t, bucket_t],
        in_specs=[_proj_piece(4, cur), _proj_piece(5, prev), _proj_piece(5, cur), _proj_piece(6, prev),
                  _proj_piece(6, cur)] + _proj_gate_halves(cur) + [
                  const((1, SWA_D)), const((1, SWA_D)), smem, smem, const((2 * CH, CH))],
        out_specs=[col(512), col(512), const((SWA_D, 1)), const((SWA_D, 1))],
        out_shape=[jax.ShapeDtypeStruct((512, T), F32), jax.ShapeDtypeStruct((512, T), F32),
                   jax.ShapeDtypeStruct((SWA_D, 1), F32), jax.ShapeDtypeStruct((SWA_D, 1), F32)],
        scratch_shapes=[pltpu.VMEM((2, 2 * CH, HQ_LANES), F32), pltpu.VMEM((1, HQ_LANES), F32)])


def _swa_bwd(proj, so, dmix, q_norm_w, k_norm_w, sinks, rel_bias_t, bucket_t):
    def body(sq_ref, skp_ref, skc_ref, svp_ref, svc_ref, sg_lo, sg_hi, so_ref, dm_ref, qw_ref, kw_ref, sinks_ref,
             relb_ref, bucket_ref, d_ref, dbias_ref, dsink_ref, dqw_ref, dkw_ref,
             bias_ref, sink_row, band_dk, band_dv, carry_dk, carry_dv, hold_dq, hold_dg):
        n = pl.program_id(0)
        slot = n % 2

        @pl.when(n == 0)
        def _():
            _build_bias_t(bucket_ref, relb_ref, sinks_ref, bias_ref, sink_row)
            for ref in (dbias_ref, dsink_ref, dqw_ref, dkw_ref, carry_dk, carry_dv):
                ref[...] = jnp.zeros_like(ref)

        qw, kw = qw_ref[...], kw_ref[...]

        @pl.when(n < N_CH)
        def _():
            var = (n == 0).astype(jnp.int32)
            kn_band = jnp.concatenate([_norm_kv(skp_ref[...], kw), _norm_kv(skc_ref[...], kw)], axis=1)
            kb_rm = _bf(kn_band.T)
            kn_band_s = _bf(kn_band * (SWA_D ** -0.5))
            vband_f = jnp.concatenate([svp_ref[...], svc_ref[...]], axis=1)
            vb_rm = _bf(vband_f.T)
            q_raw, q_rstd, qs_b, do_b = [], [], [], []
            for hq in range(SWA_H):
                rows = slice(hq * SWA_D, (hq + 1) * SWA_D)
                q_t = sq_ref[rows, :]
                qn, rq = _rms_t(q_t, qw)
                g = _gate_rows((sg_lo, sg_hi), hq)
                sig = _sigmoid(g)
                dm = dm_ref[rows, :]
                hold_dg[slot, rows, :] = _bf(dm * so_ref[rows, :] * (sig * (1.0 + g * (1.0 - sig))))
                q_raw.append(q_t)
                q_rstd.append(rq)
                qs_b.append(_bf(qn * (SWA_D ** -0.5)))
                do_b.append(_bf(dm * (g * sig)))
            q_all = jnp.concatenate([_kv_slot(qs_b[hq], hq // SWA_G) for hq in range(SWA_H)], axis=1)
            do_all = jnp.concatenate([_kv_slot(do_b[hq], hq // SWA_G) for hq in range(SWA_H)], axis=1)
            probs, p_sink = _softmax_t(_dot(kb_rm, q_all) + bias_ref[var], sink_row[...])
            dprobs = _dot(vb_rm, do_all)
            t = jnp.sum(probs * dprobs, axis=0, keepdims=True)
            dlog = probs * (dprobs - t)
            dsink_ref[...] += -(p_sink * t)
            dbias_ref[...] += dlog
            dlog_b, probs_b = _bf(dlog), _bf(probs)
            dkn, dvv = [], []
            for kv in range(SWA_KV):
                heads = range(kv * SWA_G, (kv + 1) * SWA_G)
                lanes = slice(kv * SWA_G * CH, (kv + 1) * SWA_G * CH)
                dvv.append(_dot_nt(jnp.concatenate([do_b[hq] for hq in heads], axis=1), probs_b[:, lanes]))
                dkn.append(_dot_nt(jnp.concatenate([qs_b[hq] for hq in heads], axis=1), dlog_b[:, lanes]))
                dqn = _dot(kn_band_s[kv * SWA_D:(kv + 1) * SWA_D], dlog_b[:, lanes])
                for j, hq in enumerate(heads):
                    dq_t, dqw_terms = _rms_t_bwd(dqn[:, j * CH:(j + 1) * CH], q_raw[hq], q_rstd[hq], qw)
                    hold_dq[slot, hq * SWA_D:(hq + 1) * SWA_D, :] = _bf(dq_t)
                    dqw_ref[...] += dqw_terms
            band_dk[...] = jnp.concatenate(dkn, axis=0)
            band_dv[...] = jnp.concatenate(dvv, axis=0)

        @pl.when(n == N_CH)
        def _():
            band_dk[...] = jnp.zeros_like(band_dk)
            band_dv[...] = jnp.zeros_like(band_dv)

        @pl.when(n >= 1)
        def _():
            dkn_prev = carry_dk[...] + band_dk[:, 0:CH]
            k_t = skp_ref[...]
            for kv in range(SWA_KV):
                rows = slice(kv * SWA_D, (kv + 1) * SWA_D)
                _, rk = _rms_t(k_t[rows], kw)
                dk_t, dkw_terms = _rms_t_bwd(dkn_prev[rows], k_t[rows], rk, kw)
                d_ref[SWA_OFFS[1] + kv * SWA_D:SWA_OFFS[1] + (kv + 1) * SWA_D, :] = _bf(dk_t)
                dkw_ref[...] += dkw_terms
            d_ref[SWA_OFFS[2]:SWA_OFFS[3], :] = _bf(carry_dv[...] + band_dv[:, 0:CH])
            d_ref[SWA_OFFS[0]:SWA_OFFS[1], :] = hold_dq[1 - slot]
            d_ref[SWA_OFFS[3]:SWA_OFFS[4], :] = hold_dg[1 - slot]

        carry_dk[...] = band_dk[:, CH:2 * CH]
        carry_dv[...] = band_dv[:, CH:2 * CH]

    cur_block = lambda n: jnp.minimum(n, N_CH - 1)
    prev_block = lambda n: jnp.maximum(n - 1, 0)
    col = lambda w: pl.BlockSpec((w, CH), lambda n: (0, cur_block(n)))
    prev = lambda w: pl.BlockSpec((w, CH), lambda n: (0, prev_block(n)))
    const = lambda shape: pl.BlockSpec(shape, lambda n: (0,) * len(shape))
    smem = pl.BlockSpec(memory_space=pltpu.SMEM)
    return _Part(
        body,
        inputs=[proj, proj, proj, proj, proj, proj, proj, so, dmix, q_norm_w, k_norm_w, sinks, rel_bias_t, bucket_t],
        in_specs=[_proj_piece(4, cur_block), _proj_piece(5, prev_block), _proj_piece(5, cur_block),
                  _proj_piece(6, prev_block), _proj_piece(6, cur_block)] + _proj_gate_halves(cur_block) + [
                  col(512), col(512), const((SWA_D, 1)), const((SWA_D, 1)), smem, smem, const((2 * CH, CH))],
        out_specs=[prev(SWA_W), const((2 * CH, HQ_LANES)), const((1, HQ_LANES)),
                   const((SWA_D, CH)), const((SWA_D, CH))],
        out_shape=[jax.ShapeDtypeStruct((SWA_W, T), BF16),
                   jax.ShapeDtypeStruct((2 * CH, HQ_LANES), F32), jax.ShapeDtypeStruct((1, HQ_LANES), F32),
                   jax.ShapeDtypeStruct((SWA_D, CH), F32), jax.ShapeDtypeStruct((SWA_D, CH), F32)],
        scratch_shapes=[pltpu.VMEM((2, 2 * CH, HQ_LANES), F32), pltpu.VMEM((1, HQ_LANES), F32),
                        pltpu.VMEM((128, 2 * CH), F32), pltpu.VMEM((128, 2 * CH), F32),
                        pltpu.VMEM((128, CH), F32), pltpu.VMEM((128, CH), F32),
                        pltpu.VMEM((2, 512, CH), BF16), pltpu.VMEM((2, 512, CH), BF16)])


SMALL_GRAD_SHAPES = ((1, RET_V), (1, CH), (1, CH), (1, CH), (SWA_H, N_BUCKETS))


def _finish_small_grads(dbias_ref, bucket_ref, dsink_ref, dqw_ref, dkw_ref, drw_ref, rw_o, qw_o, kw_o, sink_o, relb_o):
    bt = bucket_ref[...]
    col = lax.broadcasted_iota(jnp.int32, (SWA_H, N_BUCKETS), 1)
    lane = lax.broadcasted_iota(jnp.int32, (1, CH), 1)
    sink = jnp.zeros((1, CH), F32)
    flip = (lax.broadcasted_iota(jnp.int32, (CH, CH), 0) + lax.broadcasted_iota(jnp.int32, (CH, CH), 1) == CH - 1)
    reverse = lambda a: jnp.dot(a, flip.astype(F32), precision=lax.Precision.HIGHEST, preferred_element_type=F32)
    per_distance = []
    for hq in range(SWA_H):
        d = reverse(jnp.where(bt >= 0, dbias_ref[:, _head_lanes(hq)], 0.0))
        d = pltpu.roll(d, 0, 1, stride=1, stride_axis=0)
        per_distance.append(jnp.sum(d, axis=0, keepdims=True))
        sink = sink + jnp.where(lane == hq, jnp.sum(dsink_ref[:, _head_lanes(hq)], axis=1, keepdims=True), 0.0)
    per_distance = reverse(jnp.concatenate(per_distance, axis=0))
    bucket_of_distance = bt[CH:CH + 1, :]
    acc = jnp.zeros((SWA_H, N_BUCKETS), F32)
    for bk in range(N_BUCKETS):
        s = jnp.sum(jnp.where(bucket_of_distance == bk, per_distance, 0.0), axis=1, keepdims=True)
        acc = acc + jnp.where(col == bk, s, 0.0)
    relb_o[...] = acc
    sink_o[...] = sink
    for src, dst in ((dqw_ref, qw_o), (dkw_ref, kw_o)):
        padded = jnp.concatenate([src[...], jnp.zeros((CH - SWA_D, CH), F32)], axis=0)
        dst[...] = jnp.sum(padded.T, axis=0, keepdims=True)
    rw_o[...] = jnp.sum(drw_ref[...].T, axis=0, keepdims=True)


def _out_proj(mix_r, mix_s, wo, x, target):
    def body(mr_ref, ms_ref, w_ref, x_ref, t_ref, loss_ref, dy_ref, dmr_ref, dms_ref, gw_ref):
        i = pl.program_id(0)

        @pl.when(i == 0)
        def _():
            loss_ref[...] = jnp.zeros_like(loss_ref)
            gw_ref[...] = jnp.zeros_like(gw_ref)

        mixed = jnp.concatenate([mr_ref[...], ms_ref[...]], axis=0)
        w = w_ref[...]
        mixed_b = _bf(mixed)
        err = x_ref[...] + _dot_tn(mixed_b, w) - t_ref[...]
        loss_ref[...] += jnp.sum(jnp.sum(err * err, axis=1, keepdims=True), axis=0, keepdims=True)
        dy = err * (1.0 / D)
        dy_ref[...] = dy
        dy_b = _bf(dy)
        dmix = _dot_nt(w, dy_b)
        dmr_ref[...] = dmix[0:512]
        dms_ref[...] = dmix[512:D]
        gw_ref[...] += _dot(mixed_b, dy_b)

    row = lambda w: pl.BlockSpec((TM, w), lambda i: (i, 0))
    col = lambda w: pl.BlockSpec((w, TM), lambda i: (0, i))
    const = lambda shape: pl.BlockSpec(shape, lambda i: (0,) * len(shape))
    return pl.pallas_call(
        body, name="out_proj", grid=(T // TM,),
        in_specs=[col(512), col(512), const((D, D)), row(D), row(D)],
        out_specs=(const((1, 1)), row(D), col(512), col(512), const((D, D))),
        out_shape=(jax.ShapeDtypeStruct((1, 1), F32), jax.ShapeDtypeStruct((T, D), F32),
                   jax.ShapeDtypeStruct((512, T), F32), jax.ShapeDtypeStruct((512, T), F32),
                   jax.ShapeDtypeStruct((D, D), F32)),
        compiler_params=_cparams(1),
    )(mix_r, mix_s, wo, x, target)


LOSS_ROW = 5
ALL_CHIPS = ((0, 0), (0, 1), (1, 0), (1, 1))


def _w_out_reduce_scatter(gwo):
    def body(gwo_ref, gout_o, own, rcv, snd, got, tot, local_sems, a_send, a_recv, b_send, b_recv):
        k = pl.program_id(0)
        x_, y_, c_ = _mesh_pos()
        sibling = (x_, y_, 1 - c_)
        rel_chips = [(1 - x_, y_), (x_, 1 - y_), (1 - x_, 1 - y_)]

        def to_sibling(j):
            px, py = ALL_CHIPS[j]
            return pltpu.make_async_remote_copy(src_ref=gwo_ref.at[_blk(px, py, 1 - c_)], dst_ref=rcv.at[j],
                                                send_sem=a_send.at[j], recv_sem=a_recv.at[j], device_id=sibling,
                                                device_id_type=MESH)

        def local(j):
            px, py = ALL_CHIPS[j]
            return pltpu.make_async_copy(gwo_ref.at[_blk(px, py, c_)], own.at[j], local_sems.at[j])

        def to_chip(q):
            return pltpu.make_async_remote_copy(src_ref=snd.at[q], dst_ref=got.at[q], send_sem=b_send.at[q],
                                                recv_sem=b_recv.at[q], device_id=(*rel_chips[q], c_), device_id_type=MESH)

        @pl.when(k == 0)
        def _():
            for j in range(len(ALL_CHIPS)):
                to_sibling(j).start()
                local(j).start()

        @pl.when(k == 2)
        def _():
            for j in range(len(ALL_CHIPS)):
                local(j).wait()
                to_sibling(j).wait_recv()
            for q in range(3):
                j = 2 * rel_chips[q][0] + rel_chips[q][1]
                snd[q] = _bf(own[j] + rcv[j])
                to_chip(q).start()
            jm = 2 * x_ + y_
            tot[...] = own[jm] + rcv[jm]

        @pl.when(k == N_CH)
        def _():
            g = tot[...]
            for q in range(3):
                to_chip(q).wait_recv()
                g = g + got[q].astype(F32)
            gout_o[...] = g
            for j in range(len(ALL_CHIPS)):
                to_sibling(j).wait_send()
            for q in range(3):
                to_chip(q).wait_send()

    dma = pltpu.SemaphoreType.DMA
    return _Part(
        body, inputs=[gwo], in_specs=[pl.BlockSpec(memory_space=pl.ANY)],
        out_specs=[pl.BlockSpec((WOUT_BLK, D), lambda k: (0, 0))], out_shape=[jax.ShapeDtypeStruct((WOUT_BLK, D), F32)],
        scratch_shapes=[pltpu.VMEM((4, WOUT_BLK, D), F32), pltpu.VMEM((4, WOUT_BLK, D), F32),
                        pltpu.VMEM((3, WOUT_BLK, D), BF16), pltpu.VMEM((3, WOUT_BLK, D), BF16),
                        pltpu.VMEM((WOUT_BLK, D), F32), dma((4,)), dma((4,)), dma((4,)), dma((3,)), dma((3,))])


def _in_proj_bwd_rs(d_ret, d_swa, h, wt, x, norm_w, dy, sse, small_acc):
    n_blocks, n_tiles = N_DEV, T // TM
    last = n_blocks + n_tiles - 1

    def body(da_any, db_any, da_ref, db_ref, h_any, w_any, x_ref, nw_ref, dy_ref, sse_ref,
             dbias_ref, bucket_ref, dsink_ref, dqw_ref, dkw_ref, drw_ref, gx_ref, gin_o, suma_o, sumb_o,
             stage, d2d_src, mine, rcv, snd, got, tot, tab_a, tab_b, gnw_acc, rw_ref, qw_ref, kw_ref, sk_ref, rb_ref,
             h_ref, w_ref, dp_sems, d2d_send, d2d_recv, ici_send, ici_recv, s_send, s_recv, hw_sems):
        k = pl.program_id(0)
        x_, y_, c_ = _mesh_pos()
        h_load = pltpu.make_async_copy(h_any, h_ref, hw_sems.at[0])
        w_load = pltpu.make_async_copy(w_any, w_ref, hw_sems.at[1])
        me = _blk(x_, y_, c_)
        sibling = (x_, y_, 1 - c_)
        rel_chips = [(1 - x_, 1 - y_), (1 - x_, y_), (x_, 1 - y_), (x_, y_)]

        def block_of_step(s):
            return _blk(*rel_chips[s // 2], 1 - c_ if s % 2 == 0 else c_)

        def fetch(s, wait):
            slot, b = s % 2, block_of_step(s)
            split = RET_W - 4 * WIN_BLK

            def run(src, dst, sem):
                cp = pltpu.make_async_copy(src, dst, sem)
                cp.wait() if wait else cp.start()

            @pl.when(b < 4)
            def _():
                run(da_any.at[pl.ds(pl.multiple_of(b * WIN_BLK, 16), WIN_BLK), :], stage.at[slot], dp_sems.at[slot, 0])

            @pl.when(b == 4)
            def _():
                run(da_any.at[pl.ds(4 * WIN_BLK, split), :], stage.at[slot, pl.ds(0, split), :], dp_sems.at[slot, 0])
                run(db_any.at[pl.ds(0, WIN_BLK - split), :], stage.at[slot, pl.ds(split, WIN_BLK - split), :],
                    dp_sems.at[slot, 1])

            @pl.when(b > 4)
            def _():
                run(db_any.at[pl.ds(pl.multiple_of(b * WIN_BLK - RET_W, 16), WIN_BLK), :], stage.at[slot],
                    dp_sems.at[slot, 0])

        def d2d_copy(r):
            return pltpu.make_async_remote_copy(src_ref=d2d_src, dst_ref=rcv.at[r], send_sem=d2d_send.at[r],
                                                recv_sem=d2d_recv.at[r], device_id=sibling, device_id_type=MESH)

        def ici_copy(r):
            return pltpu.make_async_remote_copy(src_ref=snd.at[r], dst_ref=got.at[r], send_sem=ici_send.at[r],
                                                recv_sem=ici_recv.at[r], device_id=(*rel_chips[r], c_),
                                                device_id_type=MESH)

        def table_copies():
            return [pltpu.make_async_remote_copy(src_ref=tab.at[me], dst_ref=tab.at[me], send_sem=s_send.at[a, p - 1],
                                                 recv_sem=s_recv.at[a, p - 1],
                                                 device_id=(x_ ^ (p >> 2), y_ ^ ((p >> 1) & 1), c_ ^ (p & 1)),
                                                 device_id_type=MESH)
                    for p in range(1, N_DEV) for a, tab in enumerate((tab_a, tab_b))]

        def chip_sum(r):
            d2d_copy(r).wait_recv()
            total = mine[...] + rcv[r]
            if r < 3:
                snd[r] = _bf(total)
                ici_copy(r).start()
            else:
                tot[...] = total

        for s in range(n_blocks):
            @pl.when(k == s)
            def _(s=s):
                r = s // 2
                if s == 0:
                    gnw_acc[...] = jnp.zeros_like(gnw_acc)
                    h_load.start()
                    fetch(0, wait=False)
                    w_load.start()
                fetch(s, wait=True)
                if s + 1 < n_blocks:
                    fetch(s + 1, wait=False)
                if s == 0:
                    h_load.wait()
                gw = _dot(stage[s % 2], h_ref[...])
                if s % 2 == 0:
                    if s >= 2:
                        chip_sum(r - 1)
                        d2d_copy(r - 1).wait_send()
                    d2d_src[...] = gw
                    d2d_copy(r).start()
                else:
                    mine[...] = gw

        @pl.when(k == n_blocks)
        def _():
            chip_sum(3)
            w_load.wait()

        @pl.when(k >= n_blocks)
        def _():
            dp = jnp.concatenate([da_ref[...], db_ref[...]], axis=0)
            xv, nw = x_ref[...], nw_ref[...]
            r = lax.rsqrt(jnp.mean(xv * xv, axis=-1, keepdims=True) + NORM_EPS)
            dh = _dot_tn(dp, w_ref[...])
            u = dh * nw
            gx_ref[...] = dy_ref[...] + r * u - xv * (r * r * r) * jnp.mean(u * xv, axis=-1, keepdims=True)
            gnw_acc[...] += jnp.sum(dh * (xv * r), axis=0, keepdims=True)

        @pl.when(k == n_blocks)
        def _():
            _finish_small_grads(dbias_ref, bucket_ref, dsink_ref, dqw_ref, dkw_ref, drw_ref,
                                rw_ref, qw_ref, kw_ref, sk_ref, rb_ref)

        @pl.when(k == last)
        def _():
            mine_a, mine_b = tab_a.at[me], tab_b.at[me]
            mine_a[...] = jnp.zeros_like(mine_a)
            mine_b[...] = jnp.zeros_like(mine_b)
            for row, ref in enumerate((gnw_acc, rw_ref, qw_ref, kw_ref, sk_ref)):
                mine_a[row:row + 1, 0:ref.shape[1]] = ref[...]
            mine_a[LOSS_ROW:LOSS_ROW + 1, 0:1] = sse_ref[...]
            mine_b[:, 0:N_BUCKETS] = rb_ref[...]
            tables = table_copies()
            for cp in tables:
                cp.start()
            g_in = tot[...]
            for q in range(3):
                ici_copy(q).wait_recv()
                g_in = g_in + got[q].astype(F32)
            gin_o[...] = g_in
            for cp in tables:
                cp.wait_recv()
            sum_a, sum_b = tab_a[0], tab_b[0]
            for b in range(1, N_DEV):
                sum_a = sum_a + tab_a[b]
                sum_b = sum_b + tab_b[b]
            suma_o[...] = sum_a
            sumb_o[...] = sum_b
            d2d_copy(3).wait_send()
            for q in range(3):
                ici_copy(q).wait_send()
            for cp in tables:
                cp.wait_send()

    tile = lambda k: jnp.maximum(k - n_blocks, 0)
    row = lambda w: pl.BlockSpec((TM, w), lambda k: (tile(k), 0))
    col = lambda w: pl.BlockSpec((w, TM), lambda k: (0, tile(k)))
    const = lambda shape, **kw: pl.BlockSpec(shape, lambda k: (0,) * len(shape), **kw)
    once = dict(pipeline_mode=pl.Buffered(1))
    hbm = pl.BlockSpec(memory_space=pltpu.HBM)
    dma = pltpu.SemaphoreType.DMA
    return pl.pallas_call(
        body, name="in_proj_bwd_rs", grid=(n_blocks + n_tiles,),
        in_specs=[hbm, hbm, col(RET_W), col(SWA_W), hbm, hbm,
                  row(D), const((1, D)), row(D), const((1, 1))]
                 + [const(a.shape) for a in small_acc],
        out_specs=(row(D), const((WIN_BLK, D)), const((8, D)), const((SWA_H, CH))),
        out_shape=(jax.ShapeDtypeStruct((T, D), F32), jax.ShapeDtypeStruct((WIN_BLK, D), F32),
                   jax.ShapeDtypeStruct((8, D), F32), jax.ShapeDtypeStruct((SWA_H, CH), F32)),
        scratch_shapes=[
            pltpu.VMEM((2, WIN_BLK, T), BF16), pltpu.VMEM((WIN_BLK, D), F32),
            pltpu.VMEM((WIN_BLK, D), F32),
            pltpu.VMEM((4, WIN_BLK, D), F32), pltpu.VMEM((3, WIN_BLK, D), BF16),
            pltpu.VMEM((3, WIN_BLK, D), BF16), pltpu.VMEM((WIN_BLK, D), F32),
            pltpu.VMEM((N_DEV, 8, D), F32), pltpu.VMEM((N_DEV, SWA_H, CH), F32),
            pltpu.VMEM((1, D), F32),
        ] + [pltpu.VMEM(s, F32) for s in SMALL_GRAD_SHAPES] + [
            pltpu.VMEM((T, D), BF16), pltpu.VMEM((D_IN, D), BF16),
            dma((2, 2)), dma((4,)), dma((4,)), dma((3,)), dma((3,)), dma((2, 7)), dma((2, 7)), dma((2,)),
        ],
        compiler_params=_cparams(1),
    )(d_ret, d_swa, d_ret, d_swa, h, wt, x, norm_w, dy, sse, *small_acc)


SMALL_SHAPES = ((1, D), (1, 512), (1, SWA_D), (1, SWA_D), (1, SWA_H), (SWA_H, N_BUCKETS))


def _adamw_all(grad_x, g_in_t, g_out, sum_a, sum_b, w_in_t, m_in_t, v_in_t, w_out, m_out, v_out, small_w, small_m, small_v):
    n_small = len(SMALL_SHAPES)
    n_parts = 2

    def body(*refs):
        gx_in = refs[0]
        gin_any, gout_any, suma_any, sumb_any, w_in_ref, m_in_ref, v_in_ref, w_out_ref, m_out_ref, v_out_ref = refs[1:11]
        pos = 11
        sw_refs, sm_refs, sv_refs = (refs[pos + i * n_small:pos + (i + 1) * n_small] for i in range(3))
        pos += 3 * n_small
        g_in_o, d_in_o, nm_in_o, nv_in_o, g_out_o, d_out_o, nm_out_o, nv_out_o = refs[pos:pos + 8]
        pos += 8
        sg_o, sd_o, snm_o, snv_o = (refs[pos + i * n_small:pos + (i + 1) * n_small] for i in range(4))
        pos += 4 * n_small
        loss_o, gx_out, gin_ref, gout_ref, suma_ref, sumb_ref, load_sems = refs[pos:]
        i = pl.program_id(0)
        loads = [pltpu.make_async_copy(src, dst, load_sems.at[j]) for j, (src, dst) in enumerate(
            ((gin_any, gin_ref), (gout_any, gout_ref), (suma_any, suma_ref), (sumb_any, sumb_ref)))]

        @pl.when(i == 0)
        def _():
            for cp in loads:
                cp.start()

        gx_out[...] = gx_in[...]

        @pl.when(i == 0)
        def _():
            for cp in loads:
                cp.wait()

        g = gout_ref[pl.ds(pl.multiple_of(i * (WOUT_BLK // n_parts), 8), WOUT_BLK // n_parts), :]
        d, nm, nv = _adamw(w_out_ref[...], g, m_out_ref[...], v_out_ref[...])
        g_out_o[...], d_out_o[...], nm_out_o[...], nv_out_o[...] = g, d, nm, nv
        g = gin_ref[pl.ds(pl.multiple_of(i * (WIN_BLK // n_parts), 8), WIN_BLK // n_parts), :]
        d, nm, nv = _adamw(w_in_ref[...], g, m_in_ref[...], v_in_ref[...])
        g_in_o[...], d_in_o[...], nm_in_o[...], nv_in_o[...] = g, d, nm, nv

        @pl.when(pl.program_id(0) == 0)
        def _():
            loss_o[...] = suma_ref[LOSS_ROW:LOSS_ROW + 1, 0:1] * (0.5 / D)
            for r, (rows, lanes) in enumerate(SMALL_SHAPES):
                g = suma_ref[r:r + 1, 0:lanes] if rows == 1 else sumb_ref[:, 0:lanes]
                d, nm, nv = _adamw(sw_refs[r][...], g, sm_refs[r][...], sv_refs[r][...])
                sg_o[r][...], sd_o[r][...], snm_o[r][...], snv_o[r][...] = g, d, nm, nv

    part = lambda rows: pl.BlockSpec((rows // n_parts, D), lambda i: (i, 0))
    const = lambda shape: pl.BlockSpec(shape, lambda i: (0,) * len(shape))
    win = jax.ShapeDtypeStruct((WIN_BLK, D), F32)
    wout = jax.ShapeDtypeStruct((WOUT_BLK, D), F32)
    smalls = tuple(jax.ShapeDtypeStruct(s, F32) for s in SMALL_SHAPES)
    small_specs = [const(s) for s in SMALL_SHAPES]
    hbm = pl.BlockSpec(memory_space=pltpu.HBM)
    outs = pl.pallas_call(
        body, name="adamw_all", grid=(n_parts,),
        in_specs=[part(T)] + [hbm] * 4 + [part(WIN_BLK)] * 3 + [part(WOUT_BLK)] * 3 + small_specs * 3,
        out_specs=tuple([part(WIN_BLK)] * 4 + [part(WOUT_BLK)] * 4 + small_specs * 4 + [const((1, 1)), part(T)]),
        out_shape=(win,) * 4 + (wout,) * 4 + smalls * 4 + (jax.ShapeDtypeStruct((1, 1), F32),
                                                           jax.ShapeDtypeStruct(grad_x.shape, F32)),
        scratch_shapes=[pltpu.VMEM((WIN_BLK, D), F32), pltpu.VMEM((WOUT_BLK, D), F32), pltpu.VMEM((8, D), F32),
                        pltpu.VMEM((SWA_H, CH), F32), pltpu.SemaphoreType.DMA((4,))],
        compiler_params=_cparams(1, vmem=VMEM_WHOLE),
    )(grad_x, g_in_t, g_out, sum_a, sum_b, w_in_t, m_in_t, v_in_t, w_out, m_out, v_out, *small_w, *small_m, *small_v)
    big, rest = outs[:8], outs[8:]
    return big, [rest[i * n_small:(i + 1) * n_small] for i in range(4)], rest[4 * n_small], rest[4 * n_small + 1]


def _small_rows(norm_w, ret_norm_w, q_norm_w, k_norm_w, sinks, rel_bias):
    return (norm_w.reshape(1, D), ret_norm_w.reshape(1, 512), q_norm_w.reshape(1, SWA_D), k_norm_w.reshape(1, SWA_D),
            sinks.reshape(1, SWA_H), rel_bias.T)


def _small_leaves(rows):
    return (rows[0].reshape(D), rows[1].reshape(512), rows[2].reshape(SWA_D), rows[3].reshape(SWA_D),
            rows[4].reshape(SWA_H), rows[5].T)


def kernel(x, norm_w, w_in, ret_norm_w, q_norm_w, k_norm_w, sinks, rel_bias, w_out, loss_target, m_norm_w, m_w_in, m_ret_norm_w, m_q_norm_w, m_k_norm_w, m_sinks, m_rel_bias, m_w_out, v_norm_w, v_w_in, v_ret_norm_w, v_q_norm_w, v_k_norm_w, v_sinks, v_rel_bias, v_w_out):
    x2 = x.reshape(T, D)
    target = loss_target.reshape(T, D)
    nw = norm_w.reshape(1, D)
    relb = rel_bias.T
    ret_tables = _rotary_tables_t() + _retention_tables_t()
    bucket_t = _bucket_table_t()

    proj, wt, h = _in_proj_gather(x2, nw, w_in.T)
    (ro, mix_r, states, wo, rnw_col), (so, mix_s, qnw_col, knw_col) = _fused_call("attn_fwd", N_CH, [
        _retention_fwd(proj, ret_norm_w.reshape(1, RET_V), ret_tables, w_out),
        _swa_fwd(proj, q_norm_w.reshape(1, SWA_D), k_norm_w.reshape(1, SWA_D), sinks, relb, bucket_t)])
    sse, dy, dmix_r, dmix_s, gwo = _out_proj(mix_r, mix_s, wo.reshape(D, D), x2, target)

    (d_ret, drw_acc), (d_swa, dbias, dsink_acc, dqw_acc, dkw_acc), (g_out,) = _fused_call("attn_bwd", N_CH + 1, [
        _retention_bwd(proj, ro, states, dmix_r, rnw_col, ret_tables),
        _swa_bwd(proj, so, dmix_s, qnw_col, knw_col, sinks, relb, bucket_t),
        _w_out_reduce_scatter(gwo.reshape(N_DEV, WOUT_BLK, D))])
    grad_x, g_in_t, sum_a, sum_b = _in_proj_bwd_rs(d_ret, d_swa, h, wt, x2, nw, dy, sse,
                                                   (dbias, bucket_t, dsink_acc, dqw_acc, dkw_acc, drw_acc))

    small_w = _small_rows(norm_w, ret_norm_w, q_norm_w, k_norm_w, sinks, rel_bias)
    small_m = _small_rows(m_norm_w, m_ret_norm_w, m_q_norm_w, m_k_norm_w, m_sinks, m_rel_bias)
    small_v = _small_rows(v_norm_w, v_ret_norm_w, v_q_norm_w, v_k_norm_w, v_sinks, v_rel_bias)
    big, small, loss, grad_x = _adamw_all(grad_x, g_in_t, g_out, sum_a, sum_b, w_in.T, m_w_in.T, v_w_in.T, w_out, m_w_out, v_w_out,
                                  small_w, small_m, small_v)

    def leaves(i):
        a = _small_leaves(small[i])
        return (a[0], big[i].T, a[1], a[2], a[3], a[4], a[5], big[4 + i])

    return (loss.reshape(()), grad_x.reshape(1, T, D), *leaves(0), *leaves(1), *leaves(2), *leaves(3))
```

```python
from typing import Callable, NamedTuple

import numpy as np
import jax
import jax.numpy as jnp
from jax import lax
from jax.experimental import pallas as pl
from jax.experimental.pallas import tpu as pltpu

F32 = jnp.float32
BF16 = jnp.bfloat16
MESH = pl.DeviceIdType.MESH

T = 2048
D = 1024
D_IN = 2816
N_DEV = 8
WIN_BLK = D_IN // N_DEV
WOUT_BLK = D // N_DEV
CH = 128
N_CH = T // CH
RET_H, RET_DK, RET_DV = 4, 64, 128
RET_QK = RET_H * RET_DK
RET_V = RET_H * RET_DV
SWA_H, SWA_KV, SWA_D, SWA_G = 8, 2, 64, 4
N_BUCKETS = 32
NORM_EPS = 1e-6
GN_EPS = 1e-5
NEG_INF = -1e30
PIECES = (256, 256, 512, 512, 512, 128, 128, 512)
OFFS = tuple(int(v) for v in np.cumsum((0,) + PIECES))
RET_W = OFFS[4]
SWA_W = D_IN - RET_W
SWA_OFFS = tuple(o - RET_W for o in OFFS[4:])
TM = 512

ADAM_LR, ADAM_B1, ADAM_B2, ADAM_EPS, ADAM_WD, ADAM_STEP = 0.001, 0.9, 0.999, 1e-08, 0.01, 10

VMEM_LIMIT = 56 * 1024 * 1024
VMEM_V7X = 64 * 1024 * 1024
VMEM_WHOLE = VMEM_V7X - 1024 * 1024


def _cparams(n_grid=0, vmem=VMEM_LIMIT):
    sem = ("arbitrary",) * n_grid if n_grid else None
    return pltpu.CompilerParams(dimension_semantics=sem, vmem_limit_bytes=vmem)


class _Part(NamedTuple):
    step: Callable
    inputs: list
    in_specs: list
    out_specs: list
    out_shape: list
    scratch_shapes: list


def _fused_call(name, n_steps, parts):
    n_in = [len(p.inputs) for p in parts]
    n_out = [len(p.out_shape) for p in parts]
    n_scr = [len(p.scratch_shapes) for p in parts]

    def body(*refs):
        ins, outs, scr = refs[:sum(n_in)], refs[sum(n_in):sum(n_in) + sum(n_out)], refs[sum(n_in) + sum(n_out):]
        for i, p in enumerate(parts):
            take = lambda seq, counts: seq[sum(counts[:i]):sum(counts[:i + 1])]
            p.step(*take(ins, n_in), *take(outs, n_out), *take(scr, n_scr))

    flat = lambda field: [v for p in parts for v in getattr(p, field)]
    outs = pl.pallas_call(
        body, name=name, grid=(n_steps,), in_specs=flat("in_specs"), out_specs=tuple(flat("out_specs")),
        out_shape=tuple(flat("out_shape")), scratch_shapes=flat("scratch_shapes"), compiler_params=_cparams(1),
    )(*flat("inputs"))
    return [list(outs[sum(n_out[:i]):sum(n_out[:i + 1])]) for i in range(len(parts))]


def _dot(a, b):
    return jnp.dot(a, b, preferred_element_type=F32)


def _dot_nt(a, b):
    return lax.dot_general(a, b, (((1,), (1,)), ((), ())), preferred_element_type=F32)


def _dot_tn(a, b):
    return lax.dot_general(a, b, (((0,), (0,)), ((), ())), preferred_element_type=F32)


def _bf(a):
    return a.astype(BF16)


def _sigmoid(x):
    return 1.0 / (1.0 + jnp.exp(-x))


def _adamw(w, g, m, v):
    m = ADAM_B1 * m + (1.0 - ADAM_B1) * g
    v = ADAM_B2 * v + (1.0 - ADAM_B2) * (g * g)
    m_hat = m / (1.0 - ADAM_B1 ** ADAM_STEP)
    v_hat = v / (1.0 - ADAM_B2 ** ADAM_STEP)
    delta = -ADAM_LR * (m_hat / (jnp.sqrt(v_hat) + ADAM_EPS) + ADAM_WD * w)
    return delta, m, v


def _rotary_tables_t():
    half = RET_DK // 2
    inv_freq = np.float32(10000.0) ** (-np.arange(half, dtype=np.float32) / np.float32(half))
    ang = inv_freq[:, None] * np.arange(T, dtype=np.float32)[None, :]
    cos, sin = np.cos(ang).astype(np.float32), np.sin(ang).astype(np.float32)
    cos64 = np.concatenate([cos, cos], axis=0)
    sin64 = np.concatenate([-sin, sin], axis=0)
    return np.tile(cos64, (RET_H, 1)), np.tile(sin64, (RET_H, 1))


def _retention_tables_t():
    gamma = (1.0 - np.exp2(-5.0 - np.arange(RET_H, dtype=np.float32))).astype(np.float32)
    log_g = np.log(gamma).astype(np.float32)
    i = np.arange(CH, dtype=np.float32)
    diff = i[None, :] - i[:, None]
    decay = np.where(diff >= 0, np.exp(log_g[:, None, None] * np.maximum(diff, 0.0)), 0.0).astype(np.float32)
    decay_all = np.concatenate(list(decay), axis=1)
    zeta = np.exp(log_g[:, None] * (CH - 1.0 - i)).astype(np.float32)
    zeta_tab = np.repeat(zeta.T, RET_DK, axis=1)
    xi = np.exp(log_g[:, None] * (i + 1.0)).astype(np.float32)
    xi_tab = np.repeat(xi, RET_DK, axis=0)
    chunk_decay = np.exp(log_g * np.float32(CH)).astype(np.float32)
    row_head = np.arange(RET_V)[:, None] // RET_DV
    col_head = np.arange(RET_QK)[None, :] // RET_DK
    state_mask = (row_head == col_head).astype(np.float32)
    state_decay = (state_mask * chunk_decay[row_head]).astype(np.float32)
    q_mask = (np.arange(RET_QK)[:, None] // RET_DK == np.arange(RET_H * CH)[None, :] // CH).astype(np.float32)
    return (np.ascontiguousarray(decay_all), np.ascontiguousarray(zeta_tab), np.ascontiguousarray(xi_tab),
            state_mask, state_decay, q_mask)


def _bucket_table_t():
    qi = np.arange(CH)[:, None]
    kj = np.arange(2 * CH)[None, :]
    dist = qi + CH - kj
    n = np.maximum(dist, 0)
    max_exact = N_BUCKETS // 2
    nf = np.maximum(n, 1).astype(np.float32)
    large = max_exact + (np.log(nf / np.float32(max_exact)) / np.float32(np.log(CH / max_exact))
                         * np.float32(N_BUCKETS - max_exact)).astype(np.int32)
    large = np.minimum(large, N_BUCKETS - 1)
    bucket = np.where(n < max_exact, n, large)
    return np.ascontiguousarray(np.where((dist >= 0) & (dist < CH), bucket, -1).astype(np.int32).T)


def _mesh_pos():
    return lax.axis_index("x"), lax.axis_index("y"), lax.axis_index("c")


def _blk(px, py, pc):
    return 4 * px + 2 * py + pc


def _in_proj_gather(x, norm_w, w_in_t):
    tchunk = 512

    def body(x_any, nw_any, win_any, proj_ref, wt_out, h_out, wt_ref, h_ref, stage, x_ref, nw_ref, win_ref,
             send_sems, recv_sems, out_sems, wb_sems, h_sem, in_sems):
        n_tiles = T // TM
        loads = [pltpu.make_async_copy(win_any, win_ref, in_sems.at[0]),
                 pltpu.make_async_copy(nw_any, nw_ref, in_sems.at[1])]
        loads += [pltpu.make_async_copy(x_any.at[pl.ds(i * TM, TM), :], x_ref.at[pl.ds(i * TM, TM), :], in_sems.at[2 + i])
                  for i in range(n_tiles)]
        for cp in loads:
            cp.start()
        loads[0].wait()
        x, y, c = _mesh_pos()
        me = _blk(x, y, c)
        sibling = (x, y, 1 - c)
        nbr_a, nbr_b, diag = (x ^ (1 - c), y ^ c), (x ^ c, y ^ (1 - c)), (1 - x, 1 - y)

        def copy(k, b, to):
            return pltpu.make_async_remote_copy(src_ref=wt_ref.at[b], dst_ref=wt_ref.at[b], send_sem=send_sems.at[k],
                                                recv_sem=recv_sems.at[k], device_id=to, device_id_type=MESH)

        wt_ref[me] = _bf(win_ref[...])
        first = [copy(0, me, sibling), copy(1, me, (*nbr_a, c))]
        for cp in first:
            cp.start()
        to_b = copy(2, me, (*nbr_b, c))

        loads[1].wait()
        nw = nw_ref[...]
        w_mine = wt_ref[me]
        for i in range(n_tiles):
            rows = slice(i * TM, (i + 1) * TM)
            loads[2 + i].wait()
            xv = x_ref[rows, :]
            r = lax.rsqrt(jnp.mean(xv * xv, axis=-1, keepdims=True) + NORM_EPS)
            h_ref[rows, :] = _bf(xv * r * nw)
            stage[0, :, rows] = _dot_nt(w_mine, h_ref[rows, :])
        keep = [pltpu.make_async_copy(h_ref, h_out, h_sem)]
        keep[0].start()

        writes = []

        def project(b, staged=False):
            k = len(writes)
            if k >= 2:
                writes[k - 2].wait()
            keep.append(pltpu.make_async_copy(wt_ref.at[b], wt_out.at[b], wb_sems.at[k]))
            keep[-1].start()
            if not staged:
                w = wt_ref[b]
                for t in range(T // tchunk):
                    cols = slice(t * tchunk, (t + 1) * tchunk)
                    stage[k % 2, :, cols] = _dot_nt(w, h_ref[cols, :])
            cp = pltpu.make_async_copy(stage.at[k % 2], proj_ref.at[b], out_sems.at[k % 2])
            cp.start()
            writes.append(cp)

        here = (x, y, c)
        project(me, staged=True)
        copy(0, _blk(x, y, 1 - c), here).wait_recv()
        project(_blk(x, y, 1 - c))
        to_b.start()
        passed = [to_b]

        def landed(k, chip, relay_to=None):
            copy(k, _blk(*chip, c), here).wait_recv()
            out = [copy({1: 4, 2: 5, 3: 6}[k], _blk(*chip, c), sibling)]
            if relay_to is not None:
                out.append(copy(3, _blk(*chip, c), (*relay_to, c)))
            for cp in out:
                cp.start()
            passed.extend(out)
            project(_blk(*chip, c))

        def from_sibling(k, chip):
            copy(k, _blk(*chip, 1 - c), here).wait_recv()
            project(_blk(*chip, 1 - c))

        landed(1, nbr_a, relay_to=nbr_b)
        from_sibling(4, nbr_b)
        landed(2, nbr_b)
        from_sibling(5, nbr_a)
        landed(3, diag)
        from_sibling(6, diag)
        for cp in writes[-2:] + keep:
            cp.wait()
        for cp in first + passed:
            cp.wait_send()

    vm = pl.BlockSpec(memory_space=pltpu.VMEM)
    hbm = pl.BlockSpec(memory_space=pltpu.HBM)
    dma = pltpu.SemaphoreType.DMA
    proj, wt, h = pl.pallas_call(
        body, name="in_proj_gather",
        out_shape=(jax.ShapeDtypeStruct((N_DEV, WIN_BLK, T), F32), jax.ShapeDtypeStruct((N_DEV, WIN_BLK, D), BF16),
                   jax.ShapeDtypeStruct((T, D), BF16)),
        in_specs=[hbm, hbm, hbm], out_specs=(hbm, hbm, hbm),
        scratch_shapes=[pltpu.VMEM((N_DEV, WIN_BLK, D), BF16), pltpu.VMEM((T, D), BF16), pltpu.VMEM((2, WIN_BLK, T), F32),
                        pltpu.VMEM((T, D), F32), pltpu.VMEM((1, D), F32), pltpu.VMEM((WIN_BLK, D), F32),
                        dma((7,)), dma((7,)), dma((2,)), dma((N_DEV,)), dma, dma((2 + T // TM,))],
        compiler_params=_cparams(vmem=VMEM_WHOLE),
    )(x, norm_w, w_in_t)
    return proj.reshape(D_IN, T), wt.reshape(D_IN, D), h


def _gather_w_out(w_out_ref, own_ref, wo_ref, local_sem, send_sems, recv_sems, phase):
    x, y, c = _mesh_pos()
    me = _blk(x, y, c)
    sibling = (x, y, 1 - c)
    chips = [(1 - x, y), (x, 1 - y), (1 - x, 1 - y)]

    def copy(k, b, to, src=None):
        return pltpu.make_async_remote_copy(src_ref=wo_ref.at[b] if src is None else src, dst_ref=wo_ref.at[b],
                                            send_sem=send_sems.at[k], recv_sem=recv_sems.at[k], device_id=to,
                                            device_id_type=MESH)

    local = pltpu.make_async_copy(own_ref, wo_ref.at[me], local_sem)
    first = [copy(0, me, sibling, own_ref)] + [copy(1 + j, me, (*chip, c), own_ref) for j, chip in enumerate(chips)]
    passed = [copy(4 + j, _blk(*chip, c), sibling) for j, chip in enumerate(chips)]
    if phase == 0:
        own_ref[...] = _bf(w_out_ref[...])
        local.start()
        for cp in first:
            cp.start()
    elif phase == 1:
        for j, chip in enumerate(chips):
            copy(1 + j, _blk(*chip, c), (x, y, c)).wait_recv()
            passed[j].start()
    else:
        local.wait()
        copy(0, _blk(x, y, 1 - c), (x, y, c)).wait_recv()
        for j, chip in enumerate(chips):
            copy(4 + j, _blk(*chip, 1 - c), (x, y, c)).wait_recv()
        for cp in first + passed:
            cp.wait_send()


def _proj_piece(i, block_of):
    rows = PIECES[i]
    assert OFFS[i] % rows == 0
    return pl.BlockSpec((rows, CH), lambda n: (OFFS[i] // rows, block_of(n)))


def _proj_gate_halves(block_of):
    return [pl.BlockSpec((256, CH), lambda n, j=j: (OFFS[7] // 256 + j, block_of(n))) for j in range(2)]


def _swap_halves_t(t):
    half = RET_DK // 2
    parts = []
    for h in range(RET_H):
        parts += [t[h * RET_DK + half:(h + 1) * RET_DK], t[h * RET_DK:h * RET_DK + half]]
    return jnp.concatenate(parts, axis=0)


def _rotate(t, cos, sin):
    return t * cos + _swap_halves_t(t) * sin


def _group_norm_t(o):
    mu = jnp.mean(o, axis=0, keepdims=True)
    var = jnp.mean((o - mu) * (o - mu), axis=0, keepdims=True)
    rstd = lax.rsqrt(var + GN_EPS)
    return (o - mu) * rstd, rstd


def _retention_scores_t(q_t, k_rm_b, q_mask, decay_all):
    q_heads = _bf(jnp.concatenate([q_t] * RET_H, axis=1) * q_mask)
    return _dot(k_rm_b, q_heads) * decay_all


def _row_to_col(row):
    n = row.shape[1]
    diagonal = lax.broadcasted_iota(jnp.int32, (n, n), 0) == lax.broadcasted_iota(jnp.int32, (n, n), 1)
    return jnp.sum(jnp.where(diagonal, row, 0.0), axis=1, keepdims=True)


def _retention_fwd(proj, ret_norm_w, tables, w_out_blk):
    cos_t, sin_t, decay_all, zeta_tab, xi_tab, state_mask, state_decay, q_mask = tables

    def body(rq_ref, rk_ref, rv_ref, rg_ref, w_row, cos_ref, sin_ref, decay_ref, zeta_ref, xi_ref, smask_ref,
             sdecay_ref, qmask_ref, wout_ref, ro_ref, mix_ref, st_ref, wo_ref, w_ref,
             state, wo_own, wo_local, wo_send, wo_recv):
        n = pl.program_id(0)

        @pl.when(n == 0)
        def _():
            w_ref[...] = _row_to_col(w_row[...])
            state[...] = jnp.zeros_like(state)
            _gather_w_out(wout_ref, wo_own, wo_ref, wo_local, wo_send, wo_recv, phase=0)

        @pl.when(n == N_CH // 2)
        def _():
            _gather_w_out(wout_ref, wo_own, wo_ref, wo_local, wo_send, wo_recv, phase=1)

        @pl.when(n == N_CH - 1)
        def _():
            _gather_w_out(wout_ref, wo_own, wo_ref, wo_local, wo_send, wo_recv, phase=2)

        cos, sin = cos_ref[...], sin_ref[...]
        q_t = _rotate(rq_ref[...], cos, sin)
        k_t = _rotate(rk_ref[...], cos, sin) * (RET_DK ** -0.5)
        k_rm = k_t.T
        v_b = _bf(rv_ref[...])
        m_b = _bf(state[...])
        st_ref[0] = m_b
        scores_b = _bf(_retention_scores_t(q_t, _bf(k_rm), qmask_ref[...], decay_ref[...]))
        cross = _dot(m_b, _bf(q_t * xi_ref[...]))
        state[...] = state[...] * sdecay_ref[...] + _dot(v_b, _bf(k_rm * zeta_ref[...])) * smask_ref[...]
        heads = range(RET_H)
        rows = [slice(h * RET_DV, (h + 1) * RET_DV) for h in heads]
        o = [_dot(v_b[rows[h]], scores_b[:, h * CH:(h + 1) * CH]) + cross[rows[h]] for h in heads]
        for h in heads:
            rn, _ = _group_norm_t(o[h])
            g = rg_ref[rows[h], :]
            ro_ref[rows[h], :] = o[h]
            mix_ref[rows[h], :] = rn * w_ref[rows[h], :] * (g * _sigmoid(g))

    col = lambda w: pl.BlockSpec((w, CH), lambda n: (0, n))
    const = lambda shape: pl.BlockSpec(shape, lambda n: (0,) * len(shape))
    cur = lambda n: n
    return _Part(
        body,
        inputs=[proj, proj, proj, proj, ret_norm_w, cos_t, sin_t, decay_all, zeta_tab, xi_tab, state_mask, state_decay,
                q_mask, w_out_blk],
        in_specs=[_proj_piece(i, cur) for i in range(4)] + [
            const((1, RET_V)), col(RET_QK), col(RET_QK),
            const(decay_all.shape), const(zeta_tab.shape), const(xi_tab.shape), const(state_mask.shape),
            const(state_decay.shape), const(q_mask.shape), const((WOUT_BLK, D))],
        out_specs=[col(RET_V), col(RET_V), pl.BlockSpec((1, RET_V, RET_QK), lambda n: (n, 0, 0)),
                   pl.BlockSpec(memory_space=pl.ANY), const((RET_V, 1))],
        out_shape=[jax.ShapeDtypeStruct((RET_V, T), F32), jax.ShapeDtypeStruct((RET_V, T), F32),
                   jax.ShapeDtypeStruct((N_CH, RET_V, RET_QK), BF16), jax.ShapeDtypeStruct((N_DEV, WOUT_BLK, D), BF16),
                   jax.ShapeDtypeStruct((RET_V, 1), F32)],
        scratch_shapes=[pltpu.VMEM((RET_V, RET_QK), F32), pltpu.VMEM((WOUT_BLK, D), BF16), pltpu.SemaphoreType.DMA,
                        pltpu.SemaphoreType.DMA((N_DEV - 1,)), pltpu.SemaphoreType.DMA((N_DEV - 1,))])


def _retention_bwd(proj, ro, states, dmix, ret_norm_w, tables):
    cos_t, sin_t, decay_all, zeta_tab, xi_tab, state_mask, state_decay, q_mask = tables

    def body(*refs):
        @pl.when(pl.program_id(0) < N_CH)
        def _():
            chunk(*refs)

    def chunk(rq_ref, rk_ref, rv_ref, rg_ref, ro_ref, st_ref, dm_ref, w_ref, cos_ref, sin_ref, decay_ref, zeta_ref,
              xi_ref, smask_ref, sdecay_ref, qmask_ref, d_ref, dw_ref, gstate):
        i = pl.program_id(0)

        @pl.when(i == 0)
        def _():
            gstate[...] = jnp.zeros_like(gstate)
            dw_ref[...] = jnp.zeros_like(dw_ref)

        cos, sin = cos_ref[...], sin_ref[...]
        q_t = _rotate(rq_ref[...], cos, sin)
        k_t = _rotate(rk_ref[...], cos, sin) * (RET_DK ** -0.5)
        q_b, k_b = _bf(q_t), _bf(k_t)
        k_rm = k_t.T
        kz_b = _bf(k_rm * zeta_ref[...])
        qx_b = _bf(q_t * xi_ref[...])
        v_t = rv_ref[...]
        v_b = _bf(v_t)
        v_rm_b = _bf(v_t.T)
        decay = decay_ref[...]
        scores_b = _bf(_retention_scores_t(q_t, _bf(k_rm), qmask_ref[...], decay))
        heads = range(RET_H)
        rows = [slice(h * RET_DV, (h + 1) * RET_DV) for h in heads]
        qk_rows = [slice(h * RET_DK, (h + 1) * RET_DK) for h in heads]
        lanes = [slice(h * CH, (h + 1) * CH) for h in heads]
        do = []
        for h in heads:
            g, w, dm = rg_ref[rows[h], :], w_ref[rows[h], :], dm_ref[rows[h], :]
            rn, rstd = _group_norm_t(ro_ref[rows[h], :])
            sig = _sigmoid(g)
            silu = g * sig
            d_ref[OFFS[3] + h * RET_DV:OFFS[3] + (h + 1) * RET_DV, :] = _bf(dm * rn * w * (sig * (1.0 + g * (1.0 - sig))))
            dw_ref[rows[h], :] += dm * silu * rn
            drn = dm * silu * w
            do.append(rstd * (drn - jnp.mean(drn, axis=0, keepdims=True)
                              - rn * jnp.mean(drn * rn, axis=0, keepdims=True)))
        do_b = _bf(jnp.concatenate(do, axis=0))
        m_b = st_ref[0]
        g_all = gstate[...]
        g_b = _bf(g_all)
        dscores_b = [_bf(_dot(v_rm_b[:, rows[h]], do_b[rows[h]]) * decay[:, lanes[h]]) for h in heads]
        dq_cross = lax.dot_general(m_b, do_b, (((0,), (0,)), ((), ())), preferred_element_type=F32)
        dkz = _dot(v_rm_b, g_b)
        dv_cross = _dot_nt(g_b, kz_b)
        gstate[...] = g_all * sdecay_ref[...] + _dot_nt(do_b, qx_b) * smask_ref[...]
        dq = jnp.concatenate([_dot(k_b[qk_rows[h]], dscores_b[h]) for h in heads], axis=0) + dq_cross * xi_ref[...]
        dk = (jnp.concatenate([_dot_nt(q_b[qk_rows[h]], dscores_b[h]) for h in heads], axis=0)
              + (dkz * zeta_ref[...]).T) * (RET_DK ** -0.5)
        for h in heads:
            d_ref[OFFS[2] + h * RET_DV:OFFS[2] + (h + 1) * RET_DV, :] = _bf(
                _dot_nt(do_b[rows[h]], scores_b[:, lanes[h]]) + dv_cross[rows[h]])
        d_ref[OFFS[0]:OFFS[1], :] = _bf(dq * cos + _swap_halves_t(dq * sin))
        d_ref[OFFS[1]:OFFS[2], :] = _bf(dk * cos + _swap_halves_t(dk * sin))

    chunk_of = lambda i: N_CH - 1 - jnp.minimum(i, N_CH - 1)
    col = lambda w: pl.BlockSpec((w, CH), lambda i: (0, chunk_of(i)))
    const = lambda shape: pl.BlockSpec(shape, lambda i: (0,) * len(shape))
    return _Part(
        body,
        inputs=[proj, proj, proj, proj, ro, states, dmix, ret_norm_w, cos_t, sin_t, decay_all, zeta_tab, xi_tab,
                state_mask, state_decay, q_mask],
        in_specs=[_proj_piece(j, chunk_of) for j in range(4)] + [
                  col(RET_V),
                  pl.BlockSpec((1, RET_V, RET_QK), lambda i: (chunk_of(i), 0, 0)), col(RET_V),
                  const((RET_V, 1)), col(RET_QK), col(RET_QK),
                  const(decay_all.shape), const(zeta_tab.shape), const(xi_tab.shape), const(state_mask.shape),
                  const(state_decay.shape), const(q_mask.shape)],
        out_specs=[col(RET_W), const((RET_V, CH))],
        out_shape=[jax.ShapeDtypeStruct((RET_W, T), BF16), jax.ShapeDtypeStruct((RET_V, CH), F32)],
        scratch_shapes=[pltpu.VMEM((RET_V, RET_QK), F32)])


HQ_LANES = SWA_H * CH


def _head_lanes(hq):
    return slice(hq * CH, (hq + 1) * CH)


def _build_bias_t(bucket_ref, relb_ref, sinks_ref, bias_ref, sink_row):
    bt = bucket_ref[...]
    first = lax.broadcasted_iota(jnp.int32, bt.shape, 0) < CH
    for hq in range(SWA_H):
        b = jnp.full(bt.shape, NEG_INF, F32)
        for bk in range(N_BUCKETS):
            b = jnp.where(bt == bk, relb_ref[hq, bk], b)
        bias_ref[0, :, _head_lanes(hq)] = b
        bias_ref[1, :, _head_lanes(hq)] = jnp.where(first, NEG_INF, b)
        sink_row[:, _head_lanes(hq)] = jnp.full((1, CH), sinks_ref[hq], F32)


def _rms_t(t, w_col):
    r = lax.rsqrt(jnp.mean(t * t, axis=0, keepdims=True) + NORM_EPS)
    return t * r * w_col, r


def _rms_t_bwd(dn, t, r, w_col):
    u = dn * w_col
    return r * u - t * (r * r * r) * jnp.mean(u * t, axis=0, keepdims=True), dn * t * r


def _norm_kv(k_t, kw):
    return jnp.concatenate([_rms_t(k_t[g * SWA_D:(g + 1) * SWA_D], kw)[0] for g in range(SWA_KV)], axis=0)


def _kv_slot(a, kv):
    z = jnp.zeros_like(a)
    return jnp.concatenate([a, z] if kv == 0 else [z, a], axis=0)


def _softmax_t(s, sink):
    m = jnp.maximum(jnp.max(s, axis=0, keepdims=True), sink)
    p = jnp.exp(s - m)
    e_sink = jnp.exp(sink - m)
    inv = 1.0 / (jnp.sum(p, axis=0, keepdims=True) + e_sink)
    return p * inv, e_sink * inv


def _gate_rows(sg_halves, hq):
    per_half = SWA_H // 2
    return sg_halves[hq // per_half][(hq % per_half) * SWA_D:(hq % per_half + 1) * SWA_D, :]


def _swa_fwd(proj, q_norm_w, k_norm_w, sinks, rel_bias_t, bucket_t):
    def body(sq_ref, skp_ref, skc_ref, svp_ref, svc_ref, sg_lo, sg_hi, qw_row, kw_row, sinks_ref, relb_ref, bucket_ref,
             so_ref, mix_ref, qw_ref, kw_ref, bias_ref, sink_row):
        n = pl.program_id(0)

        @pl.when(n == 0)
        def _():
            qw_ref[...] = _row_to_col(qw_row[...])
            kw_ref[...] = _row_to_col(kw_row[...])
            _build_bias_t(bucket_ref, relb_ref, sinks_ref, bias_ref, sink_row)

        var = (n == 0).astype(jnp.int32)
        qw, kw = qw_ref[...], kw_ref[...]
        kn_band = jnp.concatenate([_norm_kv(skp_ref[...], kw), _norm_kv(skc_ref[...], kw)], axis=1)
        kb_rm = _bf(kn_band.T)
        vband = _bf(jnp.concatenate([svp_ref[...], svc_ref[...]], axis=1))
        q_all = jnp.concatenate(
            [_bf(_kv_slot(_rms_t(sq_ref[hq * SWA_D:(hq + 1) * SWA_D, :], qw)[0] * (SWA_D ** -0.5), hq // SWA_G))
             for hq in range(SWA_H)], axis=1)
        probs, _ = _softmax_t(_dot(kb_rm, q_all) + bias_ref[var], sink_row[...])
        probs_b = _bf(probs)
        for kv in range(SWA_KV):
            o = _dot(vband[kv * SWA_D:(kv + 1) * SWA_D], probs_b[:, kv * SWA_G * CH:(kv + 1) * SWA_G * CH])
            for j in range(SWA_G):
                rows = slice((kv * SWA_G + j) * SWA_D, (kv * SWA_G + j + 1) * SWA_D)
                oh = o[:, j * CH:(j + 1) * CH]
                g = _gate_rows((sg_lo, sg_hi), kv * SWA_G + j)
                so_ref[rows, :] = oh
                mix_ref[rows, :] = oh * (g * _sigmoid(g))

    col = lambda w: pl.BlockSpec((w, CH), lambda n: (0, n))
    const = lambda shape: pl.BlockSpec(shape, lambda n: (0,) * len(shape))
    smem = pl.BlockSpec(memory_space=pltpu.SMEM)
    cur = lambda n: n
    prev = lambda n: jnp.maximum(n - 1, 0)
    return _Part(
        body,
        inputs=[proj, proj, proj, proj, proj, proj, proj, q_norm_w, k_norm_w, sinks, rel_bias_t, bucket_t],
        in_specs=[_proj_piece(4, cur), _proj_piece(5, prev), _proj_piece(5, cur), _proj_piece(6, prev),
                  _proj_piece(6, cur)] + _proj_gate_halves(cur) + [
                  const((1, SWA_D)), const((1, SWA_D)), smem, smem, const((2 * CH, CH))],
        out_specs=[col(512), col(512), const((SWA_D, 1)), const((SWA_D, 1))],
        out_shape=[jax.ShapeDtypeStruct((512, T), F32), jax.ShapeDtypeStruct((512, T), F32),
                   jax.ShapeDtypeStruct((SWA_D, 1), F32), jax.ShapeDtypeStruct((SWA_D, 1), F32)],
        scratch_shapes=[pltpu.VMEM((2, 2 * CH, HQ_LANES), F32), pltpu.VMEM((1, HQ_LANES), F32)])


def _swa_bwd(proj, so, dmix, q_norm_w, k_norm_w, sinks, rel_bias_t, bucket_t):
    def body(sq_ref, skp_ref, skc_ref, svp_ref, svc_ref, sg_lo, sg_hi, so_ref, dm_ref, qw_ref, kw_ref, sinks_ref,
             relb_ref, bucket_ref, d_ref, dbias_ref, dsink_ref, dqw_ref, dkw_ref,
             bias_ref, sink_row, band_dk, band_dv, carry_dk, carry_dv, hold_dq, hold_dg):
        n = pl.program_id(0)
        slot = n % 2

        @pl.when(n == 0)
        def _():
            _build_bias_t(bucket_ref, relb_ref, sinks_ref, bias_ref, sink_row)
            for ref in (dbias_ref, dsink_ref, dqw_ref, dkw_ref, carry_dk, carry_dv):
                ref[...] = jnp.zeros_like(ref)

        qw, kw = qw_ref[...], kw_ref[...]

        @pl.when(n < N_CH)
        def _():
            var = (n == 0).astype(jnp.int32)
            kn_band = jnp.concatenate([_norm_kv(skp_ref[...], kw), _norm_kv(skc_ref[...], kw)], axis=1)
            kb_rm = _bf(kn_band.T)
            kn_band_s = _bf(kn_band * (SWA_D ** -0.5))
            vband_f = jnp.concatenate([svp_ref[...], svc_ref[...]], axis=1)
            vb_rm = _bf(vband_f.T)
            q_raw, q_rstd, qs_b, do_b = [], [], [], []
            for hq in range(SWA_H):
                rows = slice(hq * SWA_D, (hq + 1) * SWA_D)
                q_t = sq_ref[rows, :]
                qn, rq = _rms_t(q_t, qw)
                g = _gate_rows((sg_lo, sg_hi), hq)
                sig = _sigmoid(g)
                dm = dm_ref[rows, :]
                hold_dg[slot, rows, :] = _bf(dm * so_ref[rows, :] * (sig * (1.0 + g * (1.0 - sig))))
                q_raw.append(q_t)
                q_rstd.append(rq)
                qs_b.append(_bf(qn * (SWA_D ** -0.5)))
                do_b.append(_bf(dm * (g * sig)))
            q_all = jnp.concatenate([_kv_slot(qs_b[hq], hq // SWA_G) for hq in range(SWA_H)], axis=1)
            do_all = jnp.concatenate([_kv_slot(do_b[hq], hq // SWA_G) for hq in range(SWA_H)], axis=1)
            probs, p_sink = _softmax_t(_dot(kb_rm, q_all) + bias_ref[var], sink_row[...])
            dprobs = _dot(vb_rm, do_all)
            t = jnp.sum(probs * dprobs, axis=0, keepdims=True)
            dlog = probs * (dprobs - t)
            dsink_ref[...] += -(p_sink * t)
            dbias_ref[...] += dlog
            dlog_b, probs_b = _bf(dlog), _bf(probs)
            dkn, dvv = [], []
            for kv in range(SWA_KV):
                heads = range(kv * SWA_G, (kv + 1) * SWA_G)
                lanes = slice(kv * SWA_G * CH, (kv + 1) * SWA_G * CH)
                dvv.append(_dot_nt(jnp.concatenate([do_b[hq] for hq in heads], axis=1), probs_b[:, lanes]))
                dkn.append(_dot_nt(jnp.concatenate([qs_b[hq] for hq in heads], axis=1), dlog_b[:, lanes]))
                dqn = _dot(kn_band_s[kv * SWA_D:(kv + 1) * SWA_D], dlog_b[:, lanes])
                for j, hq in enumerate(heads):
                    dq_t, dqw_terms = _rms_t_bwd(dqn[:, j * CH:(j + 1) * CH], q_raw[hq], q_rstd[hq], qw)
                    hold_dq[slot, hq * SWA_D:(hq + 1) * SWA_D, :] = _bf(dq_t)
                    dqw_ref[...] += dqw_terms
            band_dk[...] = jnp.concatenate(dkn, axis=0)
            band_dv[...] = jnp.concatenate(dvv, axis=0)

        @pl.when(n == N_CH)
        def _():
            band_dk[...] = jnp.zeros_like(band_dk)
            band_dv[...] = jnp.zeros_like(band_dv)

        @pl.when(n >= 1)
        def _():
            dkn_prev = carry_dk[...] + band_dk[:, 0:CH]
            k_t = skp_ref[...]
            for kv in range(SWA_KV):
                rows = slice(kv * SWA_D, (kv + 1) * SWA_D)
                _, rk = _rms_t(k_t[rows], kw)
                dk_t, dkw_terms = _rms_t_bwd(dkn_prev[rows], k_t[rows], rk, kw)
                d_ref[SWA_OFFS[1] + kv * SWA_D:SWA_OFFS[1] + (kv + 1) * SWA_D, :] = _bf(dk_t)
                dkw_ref[...] += dkw_terms
            d_ref[SWA_OFFS[2]:SWA_OFFS[3], :] = _bf(carry_dv[...] + band_dv[:, 0:CH])
            d_ref[SWA_OFFS[0]:SWA_OFFS[1], :] = hold_dq[1 - slot]
            d_ref[SWA_OFFS[3]:SWA_OFFS[4], :] = hold_dg[1 - slot]

        carry_dk[...] = band_dk[:, CH:2 * CH]
        carry_dv[...] = band_dv[:, CH:2 * CH]

    cur_block = lambda n: jnp.minimum(n, N_CH - 1)
    prev_block = lambda n: jnp.maximum(n - 1, 0)
    col = lambda w: pl.BlockSpec((w, CH), lambda n: (0, cur_block(n)))
    prev = lambda w: pl.BlockSpec((w, CH), lambda n: (0, prev_block(n)))
    const = lambda shape: pl.BlockSpec(shape, lambda n: (0,) * len(shape))
    smem = pl.BlockSpec(memory_space=pltpu.SMEM)
    return _Part(
        body,
        inputs=[proj, proj, proj, proj, proj, proj, proj, so, dmix, q_norm_w, k_norm_w, sinks, rel_bias_t, bucket_t],
        in_specs=[_proj_piece(4, cur_block), _proj_piece(5, prev_block), _proj_piece(5, cur_block),
                  _proj_piece(6, prev_block), _proj_piece(6, cur_block)] + _proj_gate_halves(cur_block) + [
                  col(512), col(512), const((SWA_D, 1)), const((SWA_D, 1)), smem, smem, const((2 * CH, CH))],
        out_specs=[prev(SWA_W), const((2 * CH, HQ_LANES)), const((1, HQ_LANES)),
                   const((SWA_D, CH)), const((SWA_D, CH))],
        out_shape=[jax.ShapeDtypeStruct((SWA_W, T), BF16),
                   jax.ShapeDtypeStruct((2 * CH, HQ_LANES), F32), jax.ShapeDtypeStruct((1, HQ_LANES), F32),
                   jax.ShapeDtypeStruct((SWA_D, CH), F32), jax.ShapeDtypeStruct((SWA_D, CH), F32)],
        scratch_shapes=[pltpu.VMEM((2, 2 * CH, HQ_LANES), F32), pltpu.VMEM((1, HQ_LANES), F32),
                        pltpu.VMEM((128, 2 * CH), F32), pltpu.VMEM((128, 2 * CH), F32),
                        pltpu.VMEM((128, CH), F32), pltpu.VMEM((128, CH), F32),
                        pltpu.VMEM((2, 512, CH), BF16), pltpu.VMEM((2, 512, CH), BF16)])


SMALL_GRAD_SHAPES = ((1, RET_V), (1, CH), (1, CH), (1, CH), (SWA_H, N_BUCKETS))


def _finish_small_grads(dbias_ref, bucket_ref, dsink_ref, dqw_ref, dkw_ref, drw_ref, rw_o, qw_o, kw_o, sink_o, relb_o):
    bt = bucket_ref[...]
    col = lax.broadcasted_iota(jnp.int32, (SWA_H, N_BUCKETS), 1)
    lane = lax.broadcasted_iota(jnp.int32, (1, CH), 1)
    sink = jnp.zeros((1, CH), F32)
    flip = (lax.broadcasted_iota(jnp.int32, (CH, CH), 0) + lax.broadcasted_iota(jnp.int32, (CH, CH), 1) == CH - 1)
    reverse = lambda a: jnp.dot(a, flip.astype(F32), precision=lax.Precision.HIGHEST, preferred_element_type=F32)
    per_distance = []
    for hq in range(SWA_H):
        d = reverse(jnp.where(bt >= 0, dbias_ref[:, _head_lanes(hq)], 0.0))
        d = pltpu.roll(d, 0, 1, stride=1, stride_axis=0)
        per_distance.append(jnp.sum(d, axis=0, keepdims=True))
        sink = sink + jnp.where(lane == hq, jnp.sum(dsink_ref[:, _head_lanes(hq)], axis=1, keepdims=True), 0.0)
    per_distance = reverse(jnp.concatenate(per_distance, axis=0))
    bucket_of_distance = bt[CH:CH + 1, :]
    acc = jnp.zeros((SWA_H, N_BUCKETS), F32)
    for bk in range(N_BUCKETS):
        s = jnp.sum(jnp.where(bucket_of_distance == bk, per_distance, 0.0), axis=1, keepdims=True)
        acc = acc + jnp.where(col == bk, s, 0.0)
    relb_o[...] = acc
    sink_o[...] = sink
    for src, dst in ((dqw_ref, qw_o), (dkw_ref, kw_o)):
        padded = jnp.concatenate([src[...], jnp.zeros((CH - SWA_D, CH), F32)], axis=0)
        dst[...] = jnp.sum(padded.T, axis=0, keepdims=True)
    rw_o[...] = jnp.sum(drw_ref[...].T, axis=0, keepdims=True)


def _out_proj(mix_r, mix_s, wo, x, target):
    def body(mr_ref, ms_ref, w_ref, x_ref, t_ref, loss_ref, dy_ref, dmr_ref, dms_ref, gw_ref):
        i = pl.program_id(0)

        @pl.when(i == 0)
        def _():
            loss_ref[...] = jnp.zeros_like(loss_ref)
            gw_ref[...] = jnp.zeros_like(gw_ref)

        mixed = jnp.concatenate([mr_ref[...], ms_ref[...]], axis=0)
        w = w_ref[...]
        mixed_b = _bf(mixed)
        err = x_ref[...] + _dot_tn(mixed_b, w) - t_ref[...]
        loss_ref[...] += jnp.sum(jnp.sum(err * err, axis=1, keepdims=True), axis=0, keepdims=True)
        dy = err * (1.0 / D)
        dy_ref[...] = dy
        dy_b = _bf(dy)
        dmix = _dot_nt(w, dy_b)
        dmr_ref[...] = dmix[0:512]
        dms_ref[...] = dmix[512:D]
        gw_ref[...] += _dot(mixed_b, dy_b)

    row = lambda w: pl.BlockSpec((TM, w), lambda i: (i, 0))
    col = lambda w: pl.BlockSpec((w, TM), lambda i: (0, i))
    const = lambda shape: pl.BlockSpec(shape, lambda i: (0,) * len(shape))
    return pl.pallas_call(
        body, name="out_proj", grid=(T // TM,),
        in_specs=[col(512), col(512), const((D, D)), row(D), row(D)],
        out_specs=(const((1, 1)), row(D), col(512), col(512), const((D, D))),
        out_shape=(jax.ShapeDtypeStruct((1, 1), F32), jax.ShapeDtypeStruct((T, D), F32),
                   jax.ShapeDtypeStruct((512, T), F32), jax.ShapeDtypeStruct((512, T), F32),
                   jax.ShapeDtypeStruct((D, D), F32)),
        compiler_params=_cparams(1),
    )(mix_r, mix_s, wo, x, target)


LOSS_ROW = 5
ALL_CHIPS = ((0, 0), (0, 1), (1, 0), (1, 1))


def _w_out_reduce_scatter(gwo):
    def body(gwo_ref, gout_o, own, rcv, snd, got, tot, local_sems, a_send, a_recv, b_send, b_recv):
        k = pl.program_id(0)
        x_, y_, c_ = _mesh_pos()
        sibling = (x_, y_, 1 - c_)
        rel_chips = [(1 - x_, y_), (x_, 1 - y_), (1 - x_, 1 - y_)]

        def to_sibling(j):
            px, py = ALL_CHIPS[j]
            return pltpu.make_async_remote_copy(src_ref=gwo_ref.at[_blk(px, py, 1 - c_)], dst_ref=rcv.at[j],
                                                send_sem=a_send.at[j], recv_sem=a_recv.at[j], device_id=sibling,
                                                device_id_type=MESH)

        def local(j):
            px, py = ALL_CHIPS[j]
            return pltpu.make_async_copy(gwo_ref.at[_blk(px, py, c_)], own.at[j], local_sems.at[j])

        def to_chip(q):
            return pltpu.make_async_remote_copy(src_ref=snd.at[q], dst_ref=got.at[q], send_sem=b_send.at[q],
                                                recv_sem=b_recv.at[q], device_id=(*rel_chips[q], c_), device_id_type=MESH)

        @pl.when(k == 0)
        def _():
            for j in range(len(ALL_CHIPS)):
                to_sibling(j).start()
                local(j).start()

        @pl.when(k == 2)
        def _():
            for j in range(len(ALL_CHIPS)):
                local(j).wait()
                to_sibling(j).wait_recv()
            for q in range(3):
                j = 2 * rel_chips[q][0] + rel_chips[q][1]
                snd[q] = _bf(own[j] + rcv[j])
                to_chip(q).start()
            jm = 2 * x_ + y_
            tot[...] = own[jm] + rcv[jm]

        @pl.when(k == N_CH)
        def _():
            g = tot[...]
            for q in range(3):
                to_chip(q).wait_recv()
                g = g + got[q].astype(F32)
            gout_o[...] = g
            for j in range(len(ALL_CHIPS)):
                to_sibling(j).wait_send()
            for q in range(3):
                to_chip(q).wait_send()

    dma = pltpu.SemaphoreType.DMA
    return _Part(
        body, inputs=[gwo], in_specs=[pl.BlockSpec(memory_space=pl.ANY)],
        out_specs=[pl.BlockSpec((WOUT_BLK, D), lambda k: (0, 0))], out_shape=[jax.ShapeDtypeStruct((WOUT_BLK, D), F32)],
        scratch_shapes=[pltpu.VMEM((4, WOUT_BLK, D), F32), pltpu.VMEM((4, WOUT_BLK, D), F32),
                        pltpu.VMEM((3, WOUT_BLK, D), BF16), pltpu.VMEM((3, WOUT_BLK, D), BF16),
                        pltpu.VMEM((WOUT_BLK, D), F32), dma((4,)), dma((4,)), dma((4,)), dma((3,)), dma((3,))])


def _in_proj_bwd_rs(d_ret, d_swa, h, wt, x, norm_w, dy, sse, small_acc):
    n_blocks, n_tiles = N_DEV, T // TM
    last = n_blocks + n_tiles - 1

    def body(da_any, db_any, da_ref, db_ref, h_any, w_any, x_ref, nw_ref, dy_ref, sse_ref,
             dbias_ref, bucket_ref, dsink_ref, dqw_ref, dkw_ref, drw_ref, gx_ref, gin_o, suma_o, sumb_o,
             stage, d2d_src, mine, rcv, snd, got, tot, tab_a, tab_b, gnw_acc, rw_ref, qw_ref, kw_ref, sk_ref, rb_ref,
             h_ref, w_ref, dp_sems, d2d_send, d2d_recv, ici_send, ici_recv, s_send, s_recv, hw_sems):
        k = pl.program_id(0)
        x_, y_, c_ = _mesh_pos()
        h_load = pltpu.make_async_copy(h_any, h_ref, hw_sems.at[0])
        w_load = pltpu.make_async_copy(w_any, w_ref, hw_sems.at[1])
        me = _blk(x_, y_, c_)
        sibling = (x_, y_, 1 - c_)
        rel_chips = [(1 - x_, 1 - y_), (1 - x_, y_), (x_, 1 - y_), (x_, y_)]

        def block_of_step(s):
            return _blk(*rel_chips[s // 2], 1 - c_ if s % 2 == 0 else c_)

        def fetch(s, wait):
            slot, b = s % 2, block_of_step(s)
            split = RET_W - 4 * WIN_BLK

            def run(src, dst, sem):
                cp = pltpu.make_async_copy(src, dst, sem)
                cp.wait() if wait else cp.start()

            @pl.when(b < 4)
            def _():
                run(da_any.at[pl.ds(pl.multiple_of(b * WIN_BLK, 16), WIN_BLK), :], stage.at[slot], dp_sems.at[slot, 0])

            @pl.when(b == 4)
            def _():
                run(da_any.at[pl.ds(4 * WIN_BLK, split), :], stage.at[slot, pl.ds(0, split), :], dp_sems.at[slot, 0])
                run(db_any.at[pl.ds(0, WIN_BLK - split), :], stage.at[slot, pl.ds(split, WIN_BLK - split), :],
                    dp_sems.at[slot, 1])

            @pl.when(b > 4)
            def _():
                run(db_any.at[pl.ds(pl.multiple_of(b * WIN_BLK - RET_W, 16), WIN_BLK), :], stage.at[slot],
                    dp_sems.at[slot, 0])

        def d2d_copy(r):
            return pltpu.make_async_remote_copy(src_ref=d2d_src, dst_ref=rcv.at[r], send_sem=d2d_send.at[r],
                                                recv_sem=d2d_recv.at[r], device_id=sibling, device_id_type=MESH)

        def ici_copy(r):
            return pltpu.make_async_remote_copy(src_ref=snd.at[r], dst_ref=got.at[r], send_sem=ici_send.at[r],
                                                recv_sem=ici_recv.at[r], device_id=(*rel_chips[r], c_),
                                                device_id_type=MESH)

        def table_copies():
            return [pltpu.make_async_remote_copy(src_ref=tab.at[me], dst_ref=tab.at[me], send_sem=s_send.at[a, p - 1],
                                                 recv_sem=s_recv.at[a, p - 1],
                                                 device_id=(x_ ^ (p >> 2), y_ ^ ((p >> 1) & 1), c_ ^ (p & 1)),
                                                 device_id_type=MESH)
                    for p in range(1, N_DEV) for a, tab in enumerate((tab_a, tab_b))]

        def chip_sum(r):
            d2d_copy(r).wait_recv()
            total = mine[...] + rcv[r]
            if r < 3:
                snd[r] = _bf(total)
                ici_copy(r).start()
            else:
                tot[...] = total

        for s in range(n_blocks):
            @pl.when(k == s)
            def _(s=s):
                r = s // 2
                if s == 0:
                    gnw_acc[...] = jnp.zeros_like(gnw_acc)
                    h_load.start()
                    fetch(0, wait=False)
                    w_load.start()
                    _finish_small_grads(dbias_ref, bucket_ref, dsink_ref, dqw_ref, dkw_ref, drw_ref,
                                        rw_ref, qw_ref, kw_ref, sk_ref, rb_ref)
                fetch(s, wait=True)
                if s + 1 < n_blocks:
                    fetch(s + 1, wait=False)
                if s == 0:
                    h_load.wait()
                gw = _dot(stage[s % 2], h_ref[...])
                if s % 2 == 0:
                    if s >= 2:
                        chip_sum(r - 1)
                        d2d_copy(r - 1).wait_send()
                    d2d_src[...] = gw
                    d2d_copy(r).start()
                else:
                    mine[...] = gw

        @pl.when(k == n_blocks)
        def _():
            chip_sum(3)
            w_load.wait()

        @pl.when(k >= n_blocks)
        def _():
            dp = jnp.concatenate([da_ref[...], db_ref[...]], axis=0)
            xv, nw = x_ref[...], nw_ref[...]
            r = lax.rsqrt(jnp.mean(xv * xv, axis=-1, keepdims=True) + NORM_EPS)
            dh = _dot_tn(dp, w_ref[...])
            u = dh * nw
            gx_ref[...] = dy_ref[...] + r * u - xv * (r * r * r) * jnp.mean(u * xv, axis=-1, keepdims=True)
            gnw_acc[...] += jnp.sum(dh * (xv * r), axis=0, keepdims=True)

        @pl.when(k == last)
        def _():
            mine_a, mine_b = tab_a.at[me], tab_b.at[me]
            mine_a[...] = jnp.zeros_like(mine_a)
            mine_b[...] = jnp.zeros_like(mine_b)
            for row, ref in enumerate((gnw_acc, rw_ref, qw_ref, kw_ref, sk_ref)):
                mine_a[row:row + 1, 0:ref.shape[1]] = ref[...]
            mine_a[LOSS_ROW:LOSS_ROW + 1, 0:1] = sse_ref[...]
            mine_b[:, 0:N_BUCKETS] = rb_ref[...]
            tables = table_copies()
            for cp in tables:
                cp.start()
            g_in = tot[...]
            for q in range(3):
                ici_copy(q).wait_recv()
                g_in = g_in + got[q].astype(F32)
            gin_o[...] = g_in
            for cp in tables:
                cp.wait_recv()
            sum_a, sum_b = tab_a[0], tab_b[0]
            for b in range(1, N_DEV):
                sum_a = sum_a + tab_a[b]
                sum_b = sum_b + tab_b[b]
            suma_o[...] = sum_a
            sumb_o[...] = sum_b
            d2d_copy(3).wait_send()
            for q in range(3):
                ici_copy(q).wait_send()
            for cp in tables:
                cp.wait_send()

    tile = lambda k: jnp.maximum(k - n_blocks, 0)
    row = lambda w: pl.BlockSpec((TM, w), lambda k: (tile(k), 0))
    col = lambda w: pl.BlockSpec((w, TM), lambda k: (0, tile(k)))
    const = lambda shape, **kw: pl.BlockSpec(shape, lambda k: (0,) * len(shape), **kw)
    once = dict(pipeline_mode=pl.Buffered(1))
    hbm = pl.BlockSpec(memory_space=pltpu.HBM)
    dma = pltpu.SemaphoreType.DMA
    return pl.pallas_call(
        body, name="in_proj_bwd_rs", grid=(n_blocks + n_tiles,),
        in_specs=[hbm, hbm, col(RET_W), col(SWA_W), hbm, hbm,
                  row(D), const((1, D)), row(D), const((1, 1))]
                 + [const(a.shape) for a in small_acc],
        out_specs=(row(D), const((WIN_BLK, D)), const((8, D)), const((SWA_H, CH))),
        out_shape=(jax.ShapeDtypeStruct((T, D), F32), jax.ShapeDtypeStruct((WIN_BLK, D), F32),
                   jax.ShapeDtypeStruct((8, D), F32), jax.ShapeDtypeStruct((SWA_H, CH), F32)),
        scratch_shapes=[
            pltpu.VMEM((2, WIN_BLK, T), BF16), pltpu.VMEM((WIN_BLK, D), F32),
            pltpu.VMEM((WIN_BLK, D), F32),
            pltpu.VMEM((4, WIN_BLK, D), F32), pltpu.VMEM((3, WIN_BLK, D), BF16),
            pltpu.VMEM((3, WIN_BLK, D), BF16), pltpu.VMEM((WIN_BLK, D), F32),
            pltpu.VMEM((N_DEV, 8, D), F32), pltpu.VMEM((N_DEV, SWA_H, CH), F32),
            pltpu.VMEM((1, D), F32),
        ] + [pltpu.VMEM(s, F32) for s in SMALL_GRAD_SHAPES] + [
            pltpu.VMEM((T, D), BF16), pltpu.VMEM((D_IN, D), BF16),
            dma((2, 2)), dma((4,)), dma((4,)), dma((3,)), dma((3,)), dma((2, 7)), dma((2, 7)), dma((2,)),
        ],
        compiler_params=_cparams(1),
    )(d_ret, d_swa, d_ret, d_swa, h, wt, x, norm_w, dy, sse, *small_acc)


SMALL_SHAPES = ((1, D), (1, 512), (1, SWA_D), (1, SWA_D), (1, SWA_H), (SWA_H, N_BUCKETS))


def _adamw_all(grad_x, g_in_t, g_out, sum_a, sum_b, w_in_t, m_in_t, v_in_t, w_out, m_out, v_out, small_w, small_m, small_v):
    n_small = len(SMALL_SHAPES)
    n_parts = 2

    def body(*refs):
        gx_in = refs[0]
        gin_any, gout_any, suma_any, sumb_any, w_in_ref, m_in_ref, v_in_ref, w_out_ref, m_out_ref, v_out_ref = refs[1:11]
        pos = 11
        sw_refs, sm_refs, sv_refs = (refs[pos + i * n_small:pos + (i + 1) * n_small] for i in range(3))
        pos += 3 * n_small
        g_in_o, d_in_o, nm_in_o, nv_in_o, g_out_o, d_out_o, nm_out_o, nv_out_o = refs[pos:pos + 8]
        pos += 8
        sg_o, sd_o, snm_o, snv_o = (refs[pos + i * n_small:pos + (i + 1) * n_small] for i in range(4))
        pos += 4 * n_small
        loss_o, gx_out, gin_ref, gout_ref, suma_ref, sumb_ref, load_sems = refs[pos:]
        i = pl.program_id(0)
        loads = [pltpu.make_async_copy(src, dst, load_sems.at[j]) for j, (src, dst) in enumerate(
            ((gin_any, gin_ref), (gout_any, gout_ref), (suma_any, suma_ref), (sumb_any, sumb_ref)))]

        @pl.when(i == 0)
        def _():
            for cp in loads:
                cp.start()

        gx_out[...] = gx_in[...]

        @pl.when(i == 0)
        def _():
            for cp in loads:
                cp.wait()

        g = gout_ref[pl.ds(pl.multiple_of(i * (WOUT_BLK // n_parts), 8), WOUT_BLK // n_parts), :]
        d, nm, nv = _adamw(w_out_ref[...], g, m_out_ref[...], v_out_ref[...])
        g_out_o[...], d_out_o[...], nm_out_o[...], nv_out_o[...] = g, d, nm, nv
        g = gin_ref[pl.ds(pl.multiple_of(i * (WIN_BLK // n_parts), 8), WIN_BLK // n_parts), :]
        d, nm, nv = _adamw(w_in_ref[...], g, m_in_ref[...], v_in_ref[...])
        g_in_o[...], d_in_o[...], nm_in_o[...], nv_in_o[...] = g, d, nm, nv

        @pl.when(pl.program_id(0) == 0)
        def _():
            loss_o[...] = suma_ref[LOSS_ROW:LOSS_ROW + 1, 0:1] * (0.5 / D)
            for r, (rows, lanes) in enumerate(SMALL_SHAPES):
                g = suma_ref[r:r + 1, 0:lanes] if rows == 1 else sumb_ref[:, 0:lanes]
                d, nm, nv = _adamw(sw_refs[r][...], g, sm_refs[r][...], sv_refs[r][...])
                sg_o[r][...], sd_o[r][...], snm_o[r][...], snv_o[r][...] = g, d, nm, nv

    part = lambda rows: pl.BlockSpec((rows // n_parts, D), lambda i: (i, 0))
    const = lambda shape: pl.BlockSpec(shape, lambda i: (0,) * len(shape))
    win = jax.ShapeDtypeStruct((WIN_BLK, D), F32)
    wout = jax.ShapeDtypeStruct((WOUT_BLK, D), F32)
    smalls = tuple(jax.ShapeDtypeStruct(s, F32) for s in SMALL_SHAPES)
    small_specs = [const(s) for s in SMALL_SHAPES]
    hbm = pl.BlockSpec(memory_space=pltpu.HBM)
    outs = pl.pallas_call(
        body, name="adamw_all", grid=(n_parts,),
        in_specs=[part(T)] + [hbm] * 4 + [part(WIN_BLK)] * 3 + [part(WOUT_BLK)] * 3 + small_specs * 3,
        out_specs=tuple([part(WIN_BLK)] * 4 + [part(WOUT_BLK)] * 4 + small_specs * 4 + [const((1, 1)), part(T)]),
        out_shape=(win,) * 4 + (wout,) * 4 + smalls * 4 + (jax.ShapeDtypeStruct((1, 1), F32),
                                                           jax.ShapeDtypeStruct(grad_x.shape, F32)),
        scratch_shapes=[pltpu.VMEM((WIN_BLK, D), F32), pltpu.VMEM((WOUT_BLK, D), F32), pltpu.VMEM((8, D), F32),
                        pltpu.VMEM((SWA_H, CH), F32), pltpu.SemaphoreType.DMA((4,))],
        compiler_params=_cparams(1, vmem=VMEM_WHOLE),
    )(grad_x, g_in_t, g_out, sum_a, sum_b, w_in_t, m_in_t, v_in_t, w_out, m_out, v_out, *small_w, *small_m, *small_v)
    big, rest = outs[:8], outs[8:]
    return big, [rest[i * n_small:(i + 1) * n_small] for i in range(4)], rest[4 * n_small], rest[4 * n_small + 1]


def _small_rows(norm_w, ret_norm_w, q_norm_w, k_norm_w, sinks, rel_bias):
    return (norm_w.reshape(1, D), ret_norm_w.reshape(1, 512), q_norm_w.reshape(1, SWA_D), k_norm_w.reshape(1, SWA_D),
            sinks.reshape(1, SWA_H), rel_bias.T)


def _small_leaves(rows):
    return (rows[0].reshape(D), rows[1].reshape(512), rows[2].reshape(SWA_D), rows[3].reshape(SWA_D),
            rows[4].reshape(SWA_H), rows[5].T)


def kernel(x, norm_w, w_in, ret_norm_w, q_norm_w, k_norm_w, sinks, rel_bias, w_out, loss_target, m_norm_w, m_w_in, m_ret_norm_w, m_q_norm_w, m_k_norm_w, m_sinks, m_rel_bias, m_w_out, v_norm_w, v_w_in, v_ret_norm_w, v_q_norm_w, v_k_norm_w, v_sinks, v_rel_bias, v_w_out):
    x2 = x.reshape(T, D)
    target = loss_target.reshape(T, D)
    nw = norm_w.reshape(1, D)
    relb = rel_bias.T
    ret_tables = _rotary_tables_t() + _retention_tables_t()
    bucket_t = _bucket_table_t()

    proj, wt, h = _in_proj_gather(x2, nw, w_in.T)
    (ro, mix_r, states, wo, rnw_col), (so, mix_s, qnw_col, knw_col) = _fused_call("attn_fwd", N_CH, [
        _retention_fwd(proj, ret_norm_w.reshape(1, RET_V), ret_tables, w_out),
        _swa_fwd(proj, q_norm_w.reshape(1, SWA_D), k_norm_w.reshape(1, SWA_D), sinks, relb, bucket_t)])
    sse, dy, dmix_r, dmix_s, gwo = _out_proj(mix_r, mix_s, wo.reshape(D, D), x2, target)

    (d_ret, drw_acc), (d_swa, dbias, dsink_acc, dqw_acc, dkw_acc), (g_out,) = _fused_call("attn_bwd", N_CH + 1, [
        _retention_bwd(proj, ro, states, dmix_r, rnw_col, ret_tables),
        _swa_bwd(proj, so, dmix_s, qnw_col, knw_col, sinks, relb, bucket_t),
        _w_out_reduce_scatter(gwo.reshape(N_DEV, WOUT_BLK, D))])
    grad_x, g_in_t, sum_a, sum_b = _in_proj_bwd_rs(d_ret, d_swa, h, wt, x2, nw, dy, sse,
                                                   (dbias, bucket_t, dsink_acc, dqw_acc, dkw_acc, drw_acc))

    small_w = _small_rows(norm_w, ret_norm_w, q_norm_w, k_norm_w, sinks, rel_bias)
    small_m = _small_rows(m_norm_w, m_ret_norm_w, m_q_norm_w, m_k_norm_w, m_sinks, m_rel_bias)
    small_v = _small_rows(v_norm_w, v_ret_norm_w, v_q_norm_w, v_k_norm_w, v_sinks, v_rel_bias)
    big, small, loss, grad_x = _adamw_all(grad_x, g_in_t, g_out, sum_a, sum_b, w_in.T, m_w_in.T, v_w_in.T, w_out, m_w_out, v_w_out,
                                  small_w, small_m, small_v)

    def leaves(i):
        a = _small_leaves(small[i])
        return (a[0], big[i].T, a[1], a[2], a[3], a[4], a[5], big[4 + i])

    return (loss.reshape(()), grad_x.reshape(1, T, D), *leaves(0), *leaves(1), *leaves(2), *leaves(3))
```

```python
from typing import Callable, NamedTuple

import numpy as np
import jax
import jax.numpy as jnp
from jax import lax
from jax.experimental import pallas as pl
from jax.experimental.pallas import tpu as pltpu

F32 = jnp.float32
BF16 = jnp.bfloat16
MESH = pl.DeviceIdType.MESH

T = 2048
D = 1024
D_IN = 2816
N_DEV = 8
WIN_BLK = D_IN // N_DEV
WOUT_BLK = D // N_DEV
CH = 128
N_CH = T // CH
RET_H, RET_DK, RET_DV = 4, 64, 128
RET_QK = RET_H * RET_DK
RET_V = RET_H * RET_DV
SWA_H, SWA_KV, SWA_D, SWA_G = 8, 2, 64, 4
N_BUCKETS = 32
NORM_EPS = 1e-6
GN_EPS = 1e-5
NEG_INF = -1e30
PIECES = (256, 256, 512, 512, 512, 128, 128, 512)
OFFS = tuple(int(v) for v in np.cumsum((0,) + PIECES))
RET_W = OFFS[4]
SWA_W = D_IN - RET_W
SWA_OFFS = tuple(o - RET_W for o in OFFS[4:])
TM = 512

ADAM_LR, ADAM_B1, ADAM_B2, ADAM_EPS, ADAM_WD, ADAM_STEP = 0.001, 0.9, 0.999, 1e-08, 0.01, 10

VMEM_LIMIT = 56 * 1024 * 1024
VMEM_V7X = 64 * 1024 * 1024
VMEM_WHOLE = VMEM_V7X - 1024 * 1024


def _cparams(n_grid=0, vmem=VMEM_LIMIT):
    sem = ("arbitrary",) * n_grid if n_grid else None
    return pltpu.CompilerParams(dimension_semantics=sem, vmem_limit_bytes=vmem)


class _Part(NamedTuple):
    step: Callable
    inputs: list
    in_specs: list
    out_specs: list
    out_shape: list
    scratch_shapes: list


def _fused_call(name, n_steps, parts):
    n_in = [len(p.inputs) for p in parts]
    n_out = [len(p.out_shape) for p in parts]
    n_scr = [len(p.scratch_shapes) for p in parts]

    def body(*refs):
        ins, outs, scr = refs[:sum(n_in)], refs[sum(n_in):sum(n_in) + sum(n_out)], refs[sum(n_in) + sum(n_out):]
        for i, p in enumerate(parts):
            take = lambda seq, counts: seq[sum(counts[:i]):sum(counts[:i + 1])]
            p.step(*take(ins, n_in), *take(outs, n_out), *take(scr, n_scr))

    flat = lambda field: [v for p in parts for v in getattr(p, field)]
    outs = pl.pallas_call(
        body, name=name, grid=(n_steps,), in_specs=flat("in_specs"), out_specs=tuple(flat("out_specs")),
        out_shape=tuple(flat("out_shape")), scratch_shapes=flat("scratch_shapes"), compiler_params=_cparams(1),
    )(*flat("inputs"))
    return [list(outs[sum(n_out[:i]):sum(n_out[:i + 1])]) for i in range(len(parts))]


def _dot(a, b):
    return jnp.dot(a, b, preferred_element_type=F32)


def _dot_nt(a, b):
    return lax.dot_general(a, b, (((1,), (1,)), ((), ())), preferred_element_type=F32)


def _dot_tn(a, b):
    return lax.dot_general(a, b, (((0,), (0,)), ((), ())), preferred_element_type=F32)


def _bf(a):
    return a.astype(BF16)


def _sigmoid(x):
    return 1.0 / (1.0 + jnp.exp(-x))


def _adamw(w, g, m, v):
    m = ADAM_B1 * m + (1.0 - ADAM_B1) * g
    v = ADAM_B2 * v + (1.0 - ADAM_B2) * (g * g)
    m_hat = m / (1.0 - ADAM_B1 ** ADAM_STEP)
    v_hat = v / (1.0 - ADAM_B2 ** ADAM_STEP)
    delta = -ADAM_LR * (m_hat / (jnp.sqrt(v_hat) + ADAM_EPS) + ADAM_WD * w)
    return delta, m, v


def _rotary_tables_t():
    half = RET_DK // 2
    inv_freq = np.float32(10000.0) ** (-np.arange(half, dtype=np.float32) / np.float32(half))
    ang = inv_freq[:, None] * np.arange(T, dtype=np.float32)[None, :]
    cos, sin = np.cos(ang).astype(np.float32), np.sin(ang).astype(np.float32)
    cos64 = np.concatenate([cos, cos], axis=0)
    sin64 = np.concatenate([-sin, sin], axis=0)
    return np.tile(cos64, (RET_H, 1)), np.tile(sin64, (RET_H, 1))


def _retention_tables_t():
    gamma = (1.0 - np.exp2(-5.0 - np.arange(RET_H, dtype=np.float32))).astype(np.float32)
    log_g = np.log(gamma).astype(np.float32)
    i = np.arange(CH, dtype=np.float32)
    diff = i[None, :] - i[:, None]
    decay = np.where(diff >= 0, np.exp(log_g[:, None, None] * np.maximum(diff, 0.0)), 0.0).astype(np.float32)
    decay_all = np.concatenate(list(decay), axis=1)
    zeta = np.exp(log_g[:, None] * (CH - 1.0 - i)).astype(np.float32)
    zeta_tab = np.repeat(zeta.T, RET_DK, axis=1)
    xi = np.exp(log_g[:, None] * (i + 1.0)).astype(np.float32)
    xi_tab = np.repeat(xi, RET_DK, axis=0)
    chunk_decay = np.exp(log_g * np.float32(CH)).astype(np.float32)
    row_head = np.arange(RET_V)[:, None] // RET_DV
    col_head = np.arange(RET_QK)[None, :] // RET_DK
    state_mask = (row_head == col_head).astype(np.float32)
    state_decay = (state_mask * chunk_decay[row_head]).astype(np.float32)
    q_mask = (np.arange(RET_QK)[:, None] // RET_DK == np.arange(RET_H * CH)[None, :] // CH).astype(np.float32)
    return (np.ascontiguousarray(decay_all), np.ascontiguousarray(zeta_tab), np.ascontiguousarray(xi_tab),
            state_mask, state_decay, q_mask)


def _bucket_table_t():
    qi = np.arange(CH)[:, None]
    kj = np.arange(2 * CH)[None, :]
    dist = qi + CH - kj
    n = np.maximum(dist, 0)
    max_exact = N_BUCKETS // 2
    nf = np.maximum(n, 1).astype(np.float32)
    large = max_exact + (np.log(nf / np.float32(max_exact)) / np.float32(np.log(CH / max_exact))
                         * np.float32(N_BUCKETS - max_exact)).astype(np.int32)
    large = np.minimum(large, N_BUCKETS - 1)
    bucket = np.where(n < max_exact, n, large)
    return np.ascontiguousarray(np.where((dist >= 0) & (dist < CH), bucket, -1).astype(np.int32).T)


def _mesh_pos():
    return lax.axis_index("x"), lax.axis_index("y"), lax.axis_index("c")


def _blk(px, py, pc):
    return 4 * px + 2 * py + pc


def _in_proj_gather(x, norm_w, w_in_t):
    tchunk = 512

    def body(x_any, nw_any, win_any, proj_ref, wt_out, h_out, wt_ref, h_ref, stage, x_ref, nw_ref, win_ref,
             send_sems, recv_sems, out_sems, wb_sems, h_sem, in_sems):
        n_tiles = T // TM
        loads = [pltpu.make_async_copy(win_any, win_ref, in_sems.at[0]),
                 pltpu.make_async_copy(nw_any, nw_ref, in_sems.at[1])]
        loads += [pltpu.make_async_copy(x_any.at[pl.ds(i * TM, TM), :], x_ref.at[pl.ds(i * TM, TM), :], in_sems.at[2 + i])
                  for i in range(n_tiles)]
        for cp in loads:
            cp.start()
        loads[0].wait()
        x, y, c = _mesh_pos()
        me = _blk(x, y, c)
        sibling = (x, y, 1 - c)
        nbr_a, nbr_b, diag = (x ^ (1 - c), y ^ c), (x ^ c, y ^ (1 - c)), (1 - x, 1 - y)

        def copy(k, b, to):
            return pltpu.make_async_remote_copy(src_ref=wt_ref.at[b], dst_ref=wt_ref.at[b], send_sem=send_sems.at[k],
                                                recv_sem=recv_sems.at[k], device_id=to, device_id_type=MESH)

        wt_ref[me] = _bf(win_ref[...])
        first = [copy(0, me, sibling), copy(1, me, (*nbr_a, c))]
        for cp in first:
            cp.start()
        to_b = copy(2, me, (*nbr_b, c))

        loads[1].wait()
        nw = nw_ref[...]
        w_mine = wt_ref[me]
        for i in range(n_tiles):
            rows = slice(i * TM, (i + 1) * TM)
            loads[2 + i].wait()
            xv = x_ref[rows, :]
            r = lax.rsqrt(jnp.mean(xv * xv, axis=-1, keepdims=True) + NORM_EPS)
            h_ref[rows, :] = _bf(xv * r * nw)
            stage[0, :, rows] = _dot_nt(w_mine, h_ref[rows, :])
        keep = [pltpu.make_async_copy(h_ref, h_out, h_sem)]
        keep[0].start()

        writes = []

        def project(b, staged=False):
            k = len(writes)
            if k >= 2:
                writes[k - 2].wait()
            keep.append(pltpu.make_async_copy(wt_ref.at[b], wt_out.at[b], wb_sems.at[k]))
            keep[-1].start()
            if not staged:
                w = wt_ref[b]
                for t in range(T // tchunk):
                    cols = slice(t * tchunk, (t + 1) * tchunk)
                    stage[k % 2, :, cols] = _dot_nt(w, h_ref[cols, :])
            cp = pltpu.make_async_copy(stage.at[k % 2], proj_ref.at[b], out_sems.at[k % 2])
            cp.start()
            writes.append(cp)

        here = (x, y, c)
        project(me, staged=True)
        copy(0, _blk(x, y, 1 - c), here).wait_recv()
        project(_blk(x, y, 1 - c))
        to_b.start()
        passed = [to_b]

        def landed(k, chip, relay_to=None):
            copy(k, _blk(*chip, c), here).wait_recv()
            out = [copy({1: 4, 2: 5, 3: 6}[k], _blk(*chip, c), sibling)]
            if relay_to is not None:
                out.append(copy(3, _blk(*chip, c), (*relay_to, c)))
            for cp in out:
                cp.start()
            passed.extend(out)
            project(_blk(*chip, c))

        def from_sibling(k, chip):
            copy(k, _blk(*chip, 1 - c), here).wait_recv()
            project(_blk(*chip, 1 - c))

        landed(1, nbr_a, relay_to=nbr_b)
        from_sibling(4, nbr_b)
        landed(2, nbr_b)
        from_sibling(5, nbr_a)
        landed(3, diag)
        from_sibling(6, diag)
        for cp in writes[-2:] + keep:
            cp.wait()
        for cp in first + passed:
            cp.wait_send()

    vm = pl.BlockSpec(memory_space=pltpu.VMEM)
    hbm = pl.BlockSpec(memory_space=pltpu.HBM)
    dma = pltpu.SemaphoreType.DMA
    proj, wt, h = pl.pallas_call(
        body, name="in_proj_gather",
        out_shape=(jax.ShapeDtypeStruct((N_DEV, WIN_BLK, T), F32), jax.ShapeDtypeStruct((N_DEV, WIN_BLK, D), BF16),
                   jax.ShapeDtypeStruct((T, D), BF16)),
        in_specs=[hbm, hbm, hbm], out_specs=(hbm, hbm, hbm),
        scratch_shapes=[pltpu.VMEM((N_DEV, WIN_BLK, D), BF16), pltpu.VMEM((T, D), BF16), pltpu.VMEM((2, WIN_BLK, T), F32),
                        pltpu.VMEM((T, D), F32), pltpu.VMEM((1, D), F32), pltpu.VMEM((WIN_BLK, D), F32),
                        dma((7,)), dma((7,)), dma((2,)), dma((N_DEV,)), dma, dma((2 + T // TM,))],
        compiler_params=_cparams(vmem=VMEM_WHOLE),
    )(x, norm_w, w_in_t)
    return proj.reshape(D_IN, T), wt.reshape(D_IN, D), h


def _gather_w_out(w_out_ref, own_ref, wo_ref, local_sem, send_sems, recv_sems, phase):
    x, y, c = _mesh_pos()
    me = _blk(x, y, c)
    sibling = (x, y, 1 - c)
    chips = [(1 - x, y), (x, 1 - y), (1 - x, 1 - y)]

    def copy(k, b, to, src=None):
        return pltpu.make_async_remote_copy(src_ref=wo_ref.at[b] if src is None else src, dst_ref=wo_ref.at[b],
                                            send_sem=send_sems.at[k], recv_sem=recv_sems.at[k], device_id=to,
                                            device_id_type=MESH)

    local = pltpu.make_async_copy(own_ref, wo_ref.at[me], local_sem)
    first = [copy(0, me, sibling, own_ref)] + [copy(1 + j, me, (*chip, c), own_ref) for j, chip in enumerate(chips)]
    passed = [copy(4 + j, _blk(*chip, c), sibling) for j, chip in enumerate(chips)]
    if phase == 0:
        own_ref[...] = _bf(w_out_ref[...])
        local.start()
        for cp in first:
            cp.start()
    elif phase == 1:
        for j, chip in enumerate(chips):
            copy(1 + j, _blk(*chip, c), (x, y, c)).wait_recv()
            passed[j].start()
    else:
        local.wait()
        copy(0, _blk(x, y, 1 - c), (x, y, c)).wait_recv()
        for j, chip in enumerate(chips):
            copy(4 + j, _blk(*chip, 1 - c), (x, y, c)).wait_recv()
        for cp in first + passed:
            cp.wait_send()


def _proj_piece(i, block_of):
    rows = PIECES[i]
    assert OFFS[i] % rows == 0
    return pl.BlockSpec((rows, CH), lambda n: (OFFS[i] // rows, block_of(n)))


def _proj_gate_halves(block_of):
    return [pl.BlockSpec((256, CH), lambda n, j=j: (OFFS[7] // 256 + j, block_of(n))) for j in range(2)]


def _swap_halves_t(t):
    half = RET_DK // 2
    parts = []
    for h in range(RET_H):
        parts += [t[h * RET_DK + half:(h + 1) * RET_DK], t[h * RET_DK:h * RET_DK + half]]
    return jnp.concatenate(parts, axis=0)


def _rotate(t, cos, sin):
    return t * cos + _swap_halves_t(t) * sin


def _group_norm_t(o):
    mu = jnp.mean(o, axis=0, keepdims=True)
    var = jnp.mean((o - mu) * (o - mu), axis=0, keepdims=True)
    rstd = lax.rsqrt(var + GN_EPS)
    return (o - mu) * rstd, rstd


def _retention_scores_t(q_t, k_rm_b, q_mask, decay_all):
    q_heads = _bf(jnp.concatenate([q_t] * RET_H, axis=1) * q_mask)
    return _dot(k_rm_b, q_heads) * decay_all


def _row_to_col(row):
    n = row.shape[1]
    diagonal = lax.broadcasted_iota(jnp.int32, (n, n), 0) == lax.broadcasted_iota(jnp.int32, (n, n), 1)
    return jnp.sum(jnp.where(diagonal, row, 0.0), axis=1, keepdims=True)


def _retention_fwd(proj, ret_norm_w, tables, w_out_blk):
    cos_t, sin_t, decay_all, zeta_tab, xi_tab, state_mask, state_decay, q_mask = tables

    def body(rq_ref, rk_ref, rv_ref, rg_ref, w_row, cos_ref, sin_ref, decay_ref, zeta_ref, xi_ref, smask_ref,
             sdecay_ref, qmask_ref, wout_ref, ro_ref, mix_ref, st_ref, wo_ref, w_ref,
             state, wo_own, wo_local, wo_send, wo_recv):
        n = pl.program_id(0)

        @pl.when(n == 0)
        def _():
            w_ref[...] = _row_to_col(w_row[...])
            state[...] = jnp.zeros_like(state)
            _gather_w_out(wout_ref, wo_own, wo_ref, wo_local, wo_send, wo_recv, phase=0)

        @pl.when(n == N_CH // 2)
        def _():
            _gather_w_out(wout_ref, wo_own, wo_ref, wo_local, wo_send, wo_recv, phase=1)

        @pl.when(n == N_CH - 1)
        def _():
            _gather_w_out(wout_ref, wo_own, wo_ref, wo_local, wo_send, wo_recv, phase=2)

        cos, sin = cos_ref[...], sin_ref[...]
        q_t = _rotate(rq_ref[...], cos, sin)
        k_t = _rotate(rk_ref[...], cos, sin) * (RET_DK ** -0.5)
        k_rm = k_t.T
        v_b = _bf(rv_ref[...])
        m_b = _bf(state[...])
        st_ref[0] = m_b
        scores_b = _bf(_retention_scores_t(q_t, _bf(k_rm), qmask_ref[...], decay_ref[...]))
        cross = _dot(m_b, _bf(q_t * xi_ref[...]))
        state[...] = state[...] * sdecay_ref[...] + _dot(v_b, _bf(k_rm * zeta_ref[...])) * smask_ref[...]
        heads = range(RET_H)
        rows = [slice(h * RET_DV, (h + 1) * RET_DV) for h in heads]
        o = [_dot(v_b[rows[h]], scores_b[:, h * CH:(h + 1) * CH]) + cross[rows[h]] for h in heads]
        for h in heads:
            rn, _ = _group_norm_t(o[h])
            g = rg_ref[rows[h], :]
            ro_ref[rows[h], :] = o[h]
            mix_ref[rows[h], :] = rn * w_ref[rows[h], :] * (g * _sigmoid(g))

    col = lambda w: pl.BlockSpec((w, CH), lambda n: (0, n))
    const = lambda shape: pl.BlockSpec(shape, lambda n: (0,) * len(shape))
    cur = lambda n: n
    return _Part(
        body,
        inputs=[proj, proj, proj, proj, ret_norm_w, cos_t, sin_t, decay_all, zeta_tab, xi_tab, state_mask, state_decay,
                q_mask, w_out_blk],
        in_specs=[_proj_piece(i, cur) for i in range(4)] + [
            const((1, RET_V)), col(RET_QK), col(RET_QK),
            const(decay_all.shape), const(zeta_tab.shape), const(xi_tab.shape), const(state_mask.shape),
            const(state_decay.shape), const(q_mask.shape), const((WOUT_BLK, D))],
        out_specs=[col(RET_V), col(RET_V), pl.BlockSpec((1, RET_V, RET_QK), lambda n: (n, 0, 0)),
                   pl.BlockSpec(memory_space=pl.ANY), const((RET_V, 1))],
        out_shape=[jax.ShapeDtypeStruct((RET_V, T), F32), jax.ShapeDtypeStruct((RET_V, T), F32),
                   jax.ShapeDtypeStruct((N_CH, RET_V, RET_QK), BF16), jax.ShapeDtypeStruct((N_DEV, WOUT_BLK, D), BF16),
                   jax.ShapeDtypeStruct((RET_V, 1), F32)],
        scratch_shapes=[pltpu.VMEM((RET_V, RET_QK), F32), pltpu.VMEM((WOUT_BLK, D), BF16), pltpu.SemaphoreType.DMA,
                        pltpu.SemaphoreType.DMA((N_DEV - 1,)), pltpu.SemaphoreType.DMA((N_DEV - 1,))])


def _retention_bwd(proj, ro, states, dmix, ret_norm_w, tables):
    cos_t, sin_t, decay_all, zeta_tab, xi_tab, state_mask, state_decay, q_mask = tables

    def body(*refs):
        @pl.when(pl.program_id(0) < N_CH)
        def _():
            chunk(*refs)

    def chunk(rq_ref, rk_ref, rv_ref, rg_ref, ro_ref, st_ref, dm_ref, w_ref, cos_ref, sin_ref, decay_ref, zeta_ref,
              xi_ref, smask_ref, sdecay_ref, qmask_ref, d_ref, dw_ref, gstate):
        i = pl.program_id(0)

        @pl.when(i == 0)
        def _():
            gstate[...] = jnp.zeros_like(gstate)
            dw_ref[...] = jnp.zeros_like(dw_ref)

        cos, sin = cos_ref[...], sin_ref[...]
        q_t = _rotate(rq_ref[...], cos, sin)
        k_t = _rotate(rk_ref[...], cos, sin) * (RET_DK ** -0.5)
        q_b, k_b = _bf(q_t), _bf(k_t)
        k_rm = k_t.T
        kz_b = _bf(k_rm * zeta_ref[...])
        qx_b = _bf(q_t * xi_ref[...])
        v_t = rv_ref[...]
        v_b = _bf(v_t)
        v_rm_b = _bf(v_t.T)
        decay = decay_ref[...]
        scores_b = _bf(_retention_scores_t(q_t, _bf(k_rm), qmask_ref[...], decay))
        heads = range(RET_H)
        rows = [slice(h * RET_DV, (h + 1) * RET_DV) for h in heads]
        qk_rows = [slice(h * RET_DK, (h + 1) * RET_DK) for h in heads]
        lanes = [slice(h * CH, (h + 1) * CH) for h in heads]
        do = []
        for h in heads:
            g, w, dm = rg_ref[rows[h], :], w_ref[rows[h], :], dm_ref[rows[h], :]
            rn, rstd = _group_norm_t(ro_ref[rows[h], :])
            sig = _sigmoid(g)
            silu = g * sig
            d_ref[OFFS[3] + h * RET_DV:OFFS[3] + (h + 1) * RET_DV, :] = _bf(dm * rn * w * (sig * (1.0 + g * (1.0 - sig))))
            dw_ref[rows[h], :] += dm * silu * rn
            drn = dm * silu * w
            do.append(rstd * (drn - jnp.mean(drn, axis=0, keepdims=True)
                              - rn * jnp.mean(drn * rn, axis=0, keepdims=True)))
        do_b = _bf(jnp.concatenate(do, axis=0))
        m_b = st_ref[0]
        g_all = gstate[...]
        g_b = _bf(g_all)
        dscores_b = [_bf(_dot(v_rm_b[:, rows[h]], do_b[rows[h]]) * decay[:, lanes[h]]) for h in heads]
        dq_cross = lax.dot_general(m_b, do_b, (((0,), (0,)), ((), ())), preferred_element_type=F32)
        dkz = _dot(v_rm_b, g_b)
        dv_cross = _dot_nt(g_b, kz_b)
        gstate[...] = g_all * sdecay_ref[...] + _dot_nt(do_b, qx_b) * smask_ref[...]
        dq = jnp.concatenate([_dot(k_b[qk_rows[h]], dscores_b[h]) for h in heads], axis=0) + dq_cross * xi_ref[...]
        dk = (jnp.concatenate([_dot_nt(q_b[qk_rows[h]], dscores_b[h]) for h in heads], axis=0)
              + (dkz * zeta_ref[...]).T) * (RET_DK ** -0.5)
        for h in heads:
            d_ref[OFFS[2] + h * RET_DV:OFFS[2] + (h + 1) * RET_DV, :] = _bf(
                _dot_nt(do_b[rows[h]], scores_b[:, lanes[h]]) + dv_cross[rows[h]])
        d_ref[OFFS[0]:OFFS[1], :] = _bf(dq * cos + _swap_halves_t(dq * sin))
        d_ref[OFFS[1]:OFFS[2], :] = _bf(dk * cos + _swap_halves_t(dk * sin))

    chunk_of = lambda i: N_CH - 1 - jnp.minimum(i, N_CH - 1)
    col = lambda w: pl.BlockSpec((w, CH), lambda i: (0, chunk_of(i)))
    const = lambda shape: pl.BlockSpec(shape, lambda i: (0,) * len(shape))
    return _Part(
        body,
        inputs=[proj, proj, proj, proj, ro, states, dmix, ret_norm_w, cos_t, sin_t, decay_all, zeta_tab, xi_tab,
                state_mask, state_decay, q_mask],
        in_specs=[_proj_piece(j, chunk_of) for j in range(4)] + [
                  col(RET_V),
                  pl.BlockSpec((1, RET_V, RET_QK), lambda i: (chunk_of(i), 0, 0)), col(RET_V),
                  const((RET_V, 1)), col(RET_QK), col(RET_QK),
                  const(decay_all.shape), const(zeta_tab.shape), const(xi_tab.shape), const(state_mask.shape),
                  const(state_decay.shape), const(q_mask.shape)],
        out_specs=[col(RET_W), const((RET_V, CH))],
        out_shape=[jax.ShapeDtypeStruct((RET_W, T), BF16), jax.ShapeDtypeStruct((RET_V, CH), F32)],
        scratch_shapes=[pltpu.VMEM((RET_V, RET_QK), F32)])


HQ_LANES = SWA_H * CH


def _head_lanes(hq):
    return slice(hq * CH, (hq + 1) * CH)


def _build_bias_t(bucket_ref, relb_ref, sinks_ref, bias_ref, sink_row):
    bt = bucket_ref[...]
    first = lax.broadcasted_iota(jnp.int32, bt.shape, 0) < CH
    for hq in range(SWA_H):
        b = jnp.full(bt.shape, NEG_INF, F32)
        for bk in range(N_BUCKETS):
            b = jnp.where(bt == bk, relb_ref[hq, bk], b)
        bias_ref[0, :, _head_lanes(hq)] = b
        bias_ref[1, :, _head_lanes(hq)] = jnp.where(first, NEG_INF, b)
        sink_row[:, _head_lanes(hq)] = jnp.full((1, CH), sinks_ref[hq], F32)


def _rms_t(t, w_col):
    r = lax.rsqrt(jnp.mean(t * t, axis=0, keepdims=True) + NORM_EPS)
    return t * r * w_col, r


def _rms_t_bwd(dn, t, r, w_col):
    u = dn * w_col
    return r * u - t * (r * r * r) * jnp.mean(u * t, axis=0, keepdims=True), dn * t * r


def _norm_kv(k_t, kw):
    return jnp.concatenate([_rms_t(k_t[g * SWA_D:(g + 1) * SWA_D], kw)[0] for g in range(SWA_KV)], axis=0)


def _kv_slot(a, kv):
    z = jnp.zeros_like(a)
    return jnp.concatenate([a, z] if kv == 0 else [z, a], axis=0)


def _softmax_t(s, sink):
    m = jnp.maximum(jnp.max(s, axis=0, keepdims=True), sink)
    p = jnp.exp(s - m)
    e_sink = jnp.exp(sink - m)
    inv = 1.0 / (jnp.sum(p, axis=0, keepdims=True) + e_sink)
    return p * inv, e_sink * inv


def _gate_rows(sg_halves, hq):
    per_half = SWA_H // 2
    return sg_halves[hq // per_half][(hq % per_half) * SWA_D:(hq % per_half + 1) * SWA_D, :]


def _swa_fwd(proj, q_norm_w, k_norm_w, sinks, rel_bias_t, bucket_t):
    def body(sq_ref, skp_ref, skc_ref, svp_ref, svc_ref, sg_lo, sg_hi, qw_row, kw_row, sinks_ref, relb_ref, bucket_ref,
             so_ref, mix_ref, qw_ref, kw_ref, bias_ref, sink_row):
        n = pl.program_id(0)

        @pl.when(n == 0)
        def _():
            qw_ref[...] = _row_to_col(qw_row[...])
            kw_ref[...] = _row_to_col(kw_row[...])
            _build_bias_t(bucket_ref, relb_ref, sinks_ref, bias_ref, sink_row)

        var = (n == 0).astype(jnp.int32)
        qw, kw = qw_ref[...], kw_ref[...]
        kn_band = jnp.concatenate([_norm_kv(skp_ref[...], kw), _norm_kv(skc_ref[...], kw)], axis=1)
        kb_rm = _bf(kn_band.T)
        vband = _bf(jnp.concatenate([svp_ref[...], svc_ref[...]], axis=1))
        q_all = jnp.concatenate(
            [_bf(_kv_slot(_rms_t(sq_ref[hq * SWA_D:(hq + 1) * SWA_D, :], qw)[0] * (SWA_D ** -0.5), hq // SWA_G))
             for hq in range(SWA_H)], axis=1)
        probs, _ = _softmax_t(_dot(kb_rm, q_all) + bias_ref[var], sink_row[...])
        probs_b = _bf(probs)
        for kv in range(SWA_KV):
            o = _dot(vband[kv * SWA_D:(kv + 1) * SWA_D], probs_b[:, kv * SWA_G * CH:(kv + 1) * SWA_G * CH])
            for j in range(SWA_G):
                rows = slice((kv * SWA_G + j) * SWA_D, (kv * SWA_G + j + 1) * SWA_D)
                oh = o[:, j * CH:(j + 1) * CH]
                g = _gate_rows((sg_lo, sg_hi), kv * SWA_G + j)
                so_ref[rows, :] = oh
                mix_ref[rows, :] = oh * (g * _sigmoid(g))

    col = lambda w: pl.BlockSpec((w, CH), lambda n: (0, n))
    const = lambda shape: pl.BlockSpec(shape, lambda n: (0,) * len(shape))
    smem = pl.BlockSpec(memory_space=pltpu.SMEM)
    cur = lambda n: n
    prev = lambda n: jnp.maximum(n - 1, 0)
    return _Part(
        body,
        inputs=[proj, proj, proj, proj, proj, proj, proj, q_norm_w, k_norm_w, sinks, rel_bias_t, bucket_t],
        in_specs=[_proj_piece(4, cur), _proj_piece(5, prev), _proj_piece(5, cur), _proj_piece(6, prev),
                  _proj_piece(6, cur)] + _proj_gate_halves(cur) + [
                  const((1, SWA_D)), const((1, SWA_D)), smem, smem, const((2 * CH, CH))],
        out_specs=[col(512), col(512), const((SWA_D, 1)), const((SWA_D, 1))],
        out_shape=[jax.ShapeDtypeStruct((512, T), F32), jax.ShapeDtypeStruct((512, T), F32),
                   jax.ShapeDtypeStruct((SWA_D, 1), F32), jax.ShapeDtypeStruct((SWA_D, 1), F32)],
        scratch_shapes=[pltpu.VMEM((2, 2 * CH, HQ_LANES), F32), pltpu.VMEM((1, HQ_LANES), F32)])


def _swa_bwd(proj, so, dmix, q_norm_w, k_norm_w, sinks, rel_bias_t, bucket_t):
    def body(sq_ref, skp_ref, skc_ref, svp_ref, svc_ref, sg_lo, sg_hi, so_ref, dm_ref, qw_ref, kw_ref, sinks_ref,
             relb_ref, bucket_ref, d_ref, dbias_ref, dsink_ref, dqw_ref, dkw_ref,
             bias_ref, sink_row, band_dk, band_dv, carry_dk, carry_dv, hold_dq, hold_dg):
        n = pl.program_id(0)
        slot = n % 2

        @pl.when(n == 0)
        def _():
            _build_bias_t(bucket_ref, relb_ref, sinks_ref, bias_ref, sink_row)
            for ref in (dbias_ref, dsink_ref, dqw_ref, dkw_ref, carry_dk, carry_dv):
                ref[...] = jnp.zeros_like(ref)

        qw, kw = qw_ref[...], kw_ref[...]

        @pl.when(n < N_CH)
        def _():
            var = (n == 0).astype(jnp.int32)
            kn_band = jnp.concatenate([_norm_kv(skp_ref[...], kw), _norm_kv(skc_ref[...], kw)], axis=1)
            kb_rm = _bf(kn_band.T)
            kn_band_s = _bf(kn_band * (SWA_D ** -0.5))
            vband_f = jnp.concatenate([svp_ref[...], svc_ref[...]], axis=1)
            vb_rm = _bf(vband_f.T)
            q_raw, q_rstd, qs_b, do_b = [], [], [], []
            for hq in range(SWA_H):
                rows = slice(hq * SWA_D, (hq + 1) * SWA_D)
                q_t = sq_ref[rows, :]
                qn, rq = _rms_t(q_t, qw)
                g = _gate_rows((sg_lo, sg_hi), hq)
                sig = _sigmoid(g)
                dm = dm_ref[rows, :]
                hold_dg[slot, rows, :] = _bf(dm * so_ref[rows, :] * (sig * (1.0 + g * (1.0 - sig))))
                q_raw.append(q_t)
                q_rstd.append(rq)
                qs_b.append(_bf(qn * (SWA_D ** -0.5)))
                do_b.append(_bf(dm * (g * sig)))
            q_all = jnp.concatenate([_kv_slot(qs_b[hq], hq // SWA_G) for hq in range(SWA_H)], axis=1)
            do_all = jnp.concatenate([_kv_slot(do_b[hq], hq // SWA_G) for hq in range(SWA_H)], axis=1)
            probs, p_sink = _softmax_t(_dot(kb_rm, q_all) + bias_ref[var], sink_row[...])
            dprobs = _dot(vb_rm, do_all)
            t = jnp.sum(probs * dprobs, axis=0, keepdims=True)
            dlog = probs * (dprobs - t)
            dsink_ref[...] += -(p_sink * t)
            dbias_ref[...] += dlog
            dlog_b, probs_b = _bf(dlog), _bf(probs)
            dkn, dvv = [], []
            for kv in range(SWA_KV):
                heads = range(kv * SWA_G, (kv + 1) * SWA_G)
                lanes = slice(kv * SWA_G * CH, (kv + 1) * SWA_G * CH)
                dvv.append(_dot_nt(jnp.concatenate([do_b[hq] for hq in heads], axis=1), probs_b[:, lanes]))
                dkn.append(_dot_nt(jnp.concatenate([qs_b[hq] for hq in heads], axis=1), dlog_b[:, lanes]))
                dqn = _dot(kn_band_s[kv * SWA_D:(kv + 1) * SWA_D], dlog_b[:, lanes])
                for j, hq in enumerate(heads):
                    dq_t, dqw_terms = _rms_t_bwd(dqn[:, j * CH:(j + 1) * CH], q_raw[hq], q_rstd[hq], qw)
                    hold_dq[slot, hq * SWA_D:(hq + 1) * SWA_D, :] = _bf(dq_t)
                    dqw_ref[...] += dqw_terms
            band_dk[...] = jnp.concatenate(dkn, axis=0)
            band_dv[...] = jnp.concatenate(dvv, axis=0)

        @pl.when(n == N_CH)
        def _():
            band_dk[...] = jnp.zeros_like(band_dk)
            band_dv[...] = jnp.zeros_like(band_dv)

        @pl.when(n >= 1)
        def _():
            dkn_prev = carry_dk[...] + band_dk[:, 0:CH]
            k_t = skp_ref[...]
            for kv in range(SWA_KV):
                rows = slice(kv * SWA_D, (kv + 1) * SWA_D)
                _, rk = _rms_t(k_t[rows], kw)
                dk_t, dkw_terms = _rms_t_bwd(dkn_prev[rows], k_t[rows], rk, kw)
                d_ref[SWA_OFFS[1] + kv * SWA_D:SWA_OFFS[1] + (kv + 1) * SWA_D, :] = _bf(dk_t)
                dkw_ref[...] += dkw_terms
            d_ref[SWA_OFFS[2]:SWA_OFFS[3], :] = _bf(carry_dv[...] + band_dv[:, 0:CH])
            d_ref[SWA_OFFS[0]:SWA_OFFS[1], :] = hold_dq[1 - slot]
            d_ref[SWA_OFFS[3]:SWA_OFFS[4], :] = hold_dg[1 - slot]

        carry_dk[...] = band_dk[:, CH:2 * CH]
        carry_dv[...] = band_dv[:, CH:2 * CH]

    cur_block = lambda n: jnp.minimum(n, N_CH - 1)
    prev_block = lambda n: jnp.maximum(n - 1, 0)
    col = lambda w: pl.BlockSpec((w, CH), lambda n: (0, cur_block(n)))
    prev = lambda w: pl.BlockSpec((w, CH), lambda n: (0, prev_block(n)))
    const = lambda shape: pl.BlockSpec(shape, lambda n: (0,) * len(shape))
    smem = pl.BlockSpec(memory_space=pltpu.SMEM)
    return _Part(
        body,
        inputs=[proj, proj, proj, proj, proj, proj, proj, so, dmix, q_norm_w, k_norm_w, sinks, rel_bias_t, bucket_t],
        in_specs=[_proj_piece(4, cur_block), _proj_piece(5, prev_block), _proj_piece(5, cur_block),
                  _proj_piece(6, prev_block), _proj_piece(6, cur_block)] + _proj_gate_halves(cur_block) + [
                  col(512), col(512), const((SWA_D, 1)), const((SWA_D, 1)), smem, smem, const((2 * CH, CH))],
        out_specs=[prev(SWA_W), const((2 * CH, HQ_LANES)), const((1, HQ_LANES)),
                   const((SWA_D, CH)), const((SWA_D, CH))],
        out_shape=[jax.ShapeDtypeStruct((SWA_W, T), BF16),
                   jax.ShapeDtypeStruct((2 * CH, HQ_LANES), F32), jax.ShapeDtypeStruct((1, HQ_LANES), F32),
                   jax.ShapeDtypeStruct((SWA_D, CH), F32), jax.ShapeDtypeStruct((SWA_D, CH), F32)],
        scratch_shapes=[pltpu.VMEM((2, 2 * CH, HQ_LANES), F32), pltpu.VMEM((1, HQ_LANES), F32),
                        pltpu.VMEM((128, 2 * CH), F32), pltpu.VMEM((128, 2 * CH), F32),
                        pltpu.VMEM((128, CH), F32), pltpu.VMEM((128, CH), F32),
                        pltpu.VMEM((2, 512, CH), BF16), pltpu.VMEM((2, 512, CH), BF16)])


SMALL_GRAD_SHAPES = ((1, RET_V), (1, CH), (1, CH), (1, CH), (SWA_H, N_BUCKETS))


def _finish_small_grads(dbias_ref, bucket_ref, dsink_ref, dqw_ref, dkw_ref, drw_ref, rw_o, qw_o, kw_o, sink_o, relb_o):
    bt = bucket_ref[...]
    col = lax.broadcasted_iota(jnp.int32, (SWA_H, N_BUCKETS), 1)
    lane = lax.broadcasted_iota(jnp.int32, (1, CH), 1)
    sink = jnp.zeros((1, CH), F32)
    flip = (lax.broadcasted_iota(jnp.int32, (CH, CH), 0) + lax.broadcasted_iota(jnp.int32, (CH, CH), 1) == CH - 1)
    reverse = lambda a: jnp.dot(a, flip.astype(F32), precision=lax.Precision.HIGHEST, preferred_element_type=F32)
    per_distance = []
    for hq in range(SWA_H):
        d = reverse(jnp.where(bt >= 0, dbias_ref[:, _head_lanes(hq)], 0.0))
        d = pltpu.roll(d, 0, 1, stride=1, stride_axis=0)
        per_distance.append(jnp.sum(d, axis=0, keepdims=True))
        sink = sink + jnp.where(lane == hq, jnp.sum(dsink_ref[:, _head_lanes(hq)], axis=1, keepdims=True), 0.0)
    per_distance = reverse(jnp.concatenate(per_distance, axis=0))
    bucket_of_distance = bt[CH:CH + 1, :]
    acc = jnp.zeros((SWA_H, N_BUCKETS), F32)
    for bk in range(N_BUCKETS):
        s = jnp.sum(jnp.where(bucket_of_distance == bk, per_distance, 0.0), axis=1, keepdims=True)
        acc = acc + jnp.where(col == bk, s, 0.0)
    relb_o[...] = acc
    sink_o[...] = sink
    for src, dst in ((dqw_ref, qw_o), (dkw_ref, kw_o)):
        padded = jnp.concatenate([src[...], jnp.zeros((CH - SWA_D, CH), F32)], axis=0)
        dst[...] = jnp.sum(padded.T, axis=0, keepdims=True)
    rw_o[...] = jnp.sum(drw_ref[...].T, axis=0, keepdims=True)


def _out_proj(mix_r, mix_s, wo, x, target):
    def body(mr_ref, ms_ref, w_ref, x_ref, t_ref, loss_ref, dy_ref, dmr_ref, dms_ref, gw_ref):
        i = pl.program_id(0)

        @pl.when(i == 0)
        def _():
            loss_ref[...] = jnp.zeros_like(loss_ref)
            gw_ref[...] = jnp.zeros_like(gw_ref)

        mixed = jnp.concatenate([mr_ref[...], ms_ref[...]], axis=0)
        w = w_ref[...]
        mixed_b = _bf(mixed)
        err = x_ref[...] + _dot_tn(mixed_b, w) - t_ref[...]
        loss_ref[...] += jnp.sum(jnp.sum(err * err, axis=1, keepdims=True), axis=0, keepdims=True)
        dy = err * (1.0 / D)
        dy_ref[...] = dy
        dy_b = _bf(dy)
        dmix = _dot_nt(w, dy_b)
        dmr_ref[...] = dmix[0:512]
        dms_ref[...] = dmix[512:D]
        gw_ref[...] += _dot(mixed_b, dy_b)

    row = lambda w: pl.BlockSpec((TM, w), lambda i: (i, 0))
    col = lambda w: pl.BlockSpec((w, TM), lambda i: (0, i))
    const = lambda shape: pl.BlockSpec(shape, lambda i: (0,) * len(shape))
    return pl.pallas_call(
        body, name="out_proj", grid=(T // TM,),
        in_specs=[col(512), col(512), const((D, D)), row(D), row(D)],
        out_specs=(const((1, 1)), row(D), col(512), col(512), const((D, D))),
        out_shape=(jax.ShapeDtypeStruct((1, 1), F32), jax.ShapeDtypeStruct((T, D), F32),
                   jax.ShapeDtypeStruct((512, T), F32), jax.ShapeDtypeStruct((512, T), F32),
                   jax.ShapeDtypeStruct((D, D), F32)),
        compiler_params=_cparams(1),
    )(mix_r, mix_s, wo, x, target)


LOSS_ROW = 5
ALL_CHIPS = ((0, 0), (0, 1), (1, 0), (1, 1))


def _w_out_reduce_scatter(gwo):
    def body(gwo_ref, gout_o, own, rcv, snd, got, tot, local_sems, a_send, a_recv, b_send, b_recv):
        k = pl.program_id(0)
        x_, y_, c_ = _mesh_pos()
        sibling = (x_, y_, 1 - c_)
        rel_chips = [(1 - x_, y_), (x_, 1 - y_), (1 - x_, 1 - y_)]

        def to_sibling(j):
            px, py = ALL_CHIPS[j]
            return pltpu.make_async_remote_copy(src_ref=gwo_ref.at[_blk(px, py, 1 - c_)], dst_ref=rcv.at[j],
                                                send_sem=a_send.at[j], recv_sem=a_recv.at[j], device_id=sibling,
                                                device_id_type=MESH)

        def local(j):
            px, py = ALL_CHIPS[j]
            return pltpu.make_async_copy(gwo_ref.at[_blk(px, py, c_)], own.at[j], local_sems.at[j])

        def to_chip(q):
            return pltpu.make_async_remote_copy(src_ref=snd.at[q], dst_ref=got.at[q], send_sem=b_send.at[q],
                                                recv_sem=b_recv.at[q], device_id=(*rel_chips[q], c_), device_id_type=MESH)

        @pl.when(k == 0)
        def _():
            for j in range(len(ALL_CHIPS)):
                to_sibling(j).start()
                local(j).start()

        @pl.when(k == 2)
        def _():
            for j in range(len(ALL_CHIPS)):
                local(j).wait()
                to_sibling(j).wait_recv()
            for q in range(3):
                j = 2 * rel_chips[q][0] + rel_chips[q][1]
                snd[q] = _bf(own[j] + rcv[j])
                to_chip(q).start()
            jm = 2 * x_ + y_
            tot[...] = own[jm] + rcv[jm]

        @pl.when(k == N_CH)
        def _():
            g = tot[...]
            for q in range(3):
                to_chip(q).wait_recv()
                g = g + got[q].astype(F32)
            gout_o[...] = g
            for j in range(len(ALL_CHIPS)):
                to_sibling(j).wait_send()
            for q in range(3):
                to_chip(q).wait_send()

    dma = pltpu.SemaphoreType.DMA
    return _Part(
        body, inputs=[gwo], in_specs=[pl.BlockSpec(memory_space=pl.ANY)],
        out_specs=[pl.BlockSpec((WOUT_BLK, D), lambda k: (0, 0))], out_shape=[jax.ShapeDtypeStruct((WOUT_BLK, D), F32)],
        scratch_shapes=[pltpu.VMEM((4, WOUT_BLK, D), F32), pltpu.VMEM((4, WOUT_BLK, D), F32),
                        pltpu.VMEM((3, WOUT_BLK, D), BF16), pltpu.VMEM((3, WOUT_BLK, D), BF16),
                        pltpu.VMEM((WOUT_BLK, D), F32), dma((4,)), dma((4,)), dma((4,)), dma((3,)), dma((3,))])


def _in_proj_bwd_rs(d_ret, d_swa, h, wt, x, norm_w, dy, sse, small_acc):
    n_blocks, n_tiles = N_DEV, T // TM
    last = n_blocks + n_tiles - 1

    def body(da_any, db_any, da_ref, db_ref, h_any, w_any, x_ref, nw_ref, dy_ref, sse_ref,
             dbias_ref, bucket_ref, dsink_ref, dqw_ref, dkw_ref, drw_ref, gx_ref, gin_o, suma_o, sumb_o,
             stage, d2d_src, mine, rcv, snd, got, tot, tab_a, tab_b, gnw_acc, rw_ref, qw_ref, kw_ref, sk_ref, rb_ref,
             h_ref, w_ref, dp_sems, d2d_send, d2d_recv, ici_send, ici_recv, s_send, s_recv, hw_sems):
        k = pl.program_id(0)
        x_, y_, c_ = _mesh_pos()
        h_load = pltpu.make_async_copy(h_any, h_ref, hw_sems.at[0])
        w_load = pltpu.make_async_copy(w_any, w_ref, hw_sems.at[1])
        me = _blk(x_, y_, c_)
        sibling = (x_, y_, 1 - c_)
        rel_chips = [(1 - x_, 1 - y_), (1 - x_, y_), (x_, 1 - y_), (x_, y_)]

        def block_of_step(s):
            return _blk(*rel_chips[s // 2], 1 - c_ if s % 2 == 0 else c_)

        def fetch(s, wait):
            slot, b = s % 2, block_of_step(s)
            split = RET_W - 4 * WIN_BLK

            def run(src, dst, sem):
                cp = pltpu.make_async_copy(src, dst, sem)
                cp.wait() if wait else cp.start()

            @pl.when(b < 4)
            def _():
                run(da_any.at[pl.ds(pl.multiple_of(b * WIN_BLK, 16), WIN_BLK), :], stage.at[slot], dp_sems.at[slot, 0])

            @pl.when(b == 4)
            def _():
                run(da_any.at[pl.ds(4 * WIN_BLK, split), :], stage.at[slot, pl.ds(0, split), :], dp_sems.at[slot, 0])
                run(db_any.at[pl.ds(0, WIN_BLK - split), :], stage.at[slot, pl.ds(split, WIN_BLK - split), :],
                    dp_sems.at[slot, 1])

            @pl.when(b > 4)
            def _():
                run(db_any.at[pl.ds(pl.multiple_of(b * WIN_BLK - RET_W, 16), WIN_BLK), :], stage.at[slot],
                    dp_sems.at[slot, 0])

        def d2d_copy(r):
            return pltpu.make_async_remote_copy(src_ref=d2d_src, dst_ref=rcv.at[r], send_sem=d2d_send.at[r],
                                                recv_sem=d2d_recv.at[r], device_id=sibling, device_id_type=MESH)

        def ici_copy(r):
            return pltpu.make_async_remote_copy(src_ref=snd.at[r], dst_ref=got.at[r], send_sem=ici_send.at[r],
                                                recv_sem=ici_recv.at[r], device_id=(*rel_chips[r], c_),
                                                device_id_type=MESH)

        def table_copies():
            return [pltpu.make_async_remote_copy(src_ref=tab.at[me], dst_ref=tab.at[me], send_sem=s_send.at[a, p - 1],
                                                 recv_sem=s_recv.at[a, p - 1],
                                                 device_id=(x_ ^ (p >> 2), y_ ^ ((p >> 1) & 1), c_ ^ (p & 1)),
                                                 device_id_type=MESH)
                    for p in range(1, N_DEV) for a, tab in enumerate((tab_a, tab_b))]

        def chip_sum(r):
            d2d_copy(r).wait_recv()
            total = mine[...] + rcv[r]
            if r < 3:
                snd[r] = _bf(total)
                ici_copy(r).start()
            else:
                tot[...] = total

        for s in range(n_blocks):
            @pl.when(k == s)
            def _(s=s):
                r = s // 2
                if s == 0:
                    gnw_acc[...] = jnp.zeros_like(gnw_acc)
                    h_load.start()
                    fetch(0, wait=False)
                    w_load.start()
                fetch(s, wait=True)
                if s + 1 < n_blocks:
                    fetch(s + 1, wait=False)
                if s == 0:
                    h_load.wait()
                gw = _dot(stage[s % 2], h_ref[...])
                if s % 2 == 0:
                    if s >= 2:
                        chip_sum(r - 1)
                        d2d_copy(r - 1).wait_send()
                    d2d_src[...] = gw
                    d2d_copy(r).start()
                else:
                    mine[...] = gw

        @pl.when(k == n_blocks)
        def _():
            chip_sum(3)
            w_load.wait()

        @pl.when(k >= n_blocks)
        def _():
            dp = jnp.concatenate([da_ref[...], db_ref[...]], axis=0)
            xv, nw = x_ref[...], nw_ref[...]
            r = lax.rsqrt(jnp.mean(xv * xv, axis=-1, keepdims=True) + NORM_EPS)
            dh = _dot_tn(dp, w_ref[...])
            u = dh * nw
            gx_ref[...] = dy_ref[...] + r * u - xv * (r * r * r) * jnp.mean(u * xv, axis=-1, keepdims=True)
            gnw_acc[...] += jnp.sum(dh * (xv * r), axis=0, keepdims=True)

        @pl.when(k == n_blocks)
        def _():
            _finish_small_grads(dbias_ref, bucket_ref, dsink_ref, dqw_ref, dkw_ref, drw_ref,
                                rw_ref, qw_ref, kw_ref, sk_ref, rb_ref)

        @pl.when(k == last)
        def _():
            mine_a, mine_b = tab_a.at[me], tab_b.at[me]
            mine_a[...] = jnp.zeros_like(mine_a)
            mine_b[...] = jnp.zeros_like(mine_b)
            for row, ref in enumerate((gnw_acc, rw_ref, qw_ref, kw_ref, sk_ref)):
                mine_a[row:row + 1, 0:ref.shape[1]] = ref[...]
            mine_a[LOSS_ROW:LOSS_ROW + 1, 0:1] = sse_ref[...]
            mine_b[:, 0:N_BUCKETS] = rb_ref[...]
            tables = table_copies()
            for cp in tables:
                cp.start()
            g_in = tot[...]
            for q in range(3):
                ici_copy(q).wait_recv()
                g_in = g_in + got[q].astype(F32)
            gin_o[...] = g_in
            for cp in tables:
                cp.wait_recv()
            sum_a, sum_b = tab_a[0], tab_b[0]
            for b in range(1, N_DEV):
                sum_a = sum_a + tab_a[b]
                sum_b = sum_b + tab_b[b]
            suma_o[...] = sum_a
            sumb_o[...] = sum_b
            d2d_copy(3).wait_send()
            for q in range(3):
                ici_copy(q).wait_send()
            for cp in tables:
                cp.wait_send()

    tile = lambda k: jnp.maximum(k - n_blocks, 0)
    row = lambda w: pl.BlockSpec((TM, w), lambda k: (tile(k), 0))
    col = lambda w: pl.BlockSpec((w, TM), lambda k: (0, tile(k)))
    const = lambda shape, **kw: pl.BlockSpec(shape, lambda k: (0,) * len(shape), **kw)
    once = dict(pipeline_mode=pl.Buffered(1))
    hbm = pl.BlockSpec(memory_space=pltpu.HBM)
    dma = pltpu.SemaphoreType.DMA
    return pl.pallas_call(
        body, name="in_proj_bwd_rs", grid=(n_blocks + n_tiles,),
        in_specs=[hbm, hbm, col(RET_W), col(SWA_W), hbm, hbm,
                  row(D), const((1, D)), row(D), const((1, 1))]
                 + [const(a.shape) for a in small_acc],
        out_specs=(row(D), const((WIN_BLK, D)), const((8, D)), const((SWA_H, CH))),
        out_shape=(jax.ShapeDtypeStruct((T, D), F32), jax.ShapeDtypeStruct((WIN_BLK, D), F32),
                   jax.ShapeDtypeStruct((8, D), F32), jax.ShapeDtypeStruct((SWA_H, CH), F32)),
        scratch_shapes=[
            pltpu.VMEM((2, WIN_BLK, T), BF16), pltpu.VMEM((WIN_BLK, D), F32),
            pltpu.VMEM((WIN_BLK, D), F32),
            pltpu.VMEM((4, WIN_BLK, D), F32), pltpu.VMEM((3, WIN_BLK, D), BF16),
            pltpu.VMEM((3, WIN_BLK, D), BF16), pltpu.VMEM((WIN_BLK, D), F32),
            pltpu.VMEM((N_DEV, 8, D), F32), pltpu.VMEM((N_DEV, SWA_H, CH), F32),
            pltpu.VMEM((1, D), F32),
        ] + [pltpu.VMEM(s, F32) for s in SMALL_GRAD_SHAPES] + [
            pltpu.VMEM((T, D), BF16), pltpu.VMEM((D_IN, D), BF16),
            dma((2, 2)), dma((4,)), dma((4,)), dma((3,)), dma((3,)), dma((2, 7)), dma((2, 7)), dma((2,)),
        ],
        compiler_params=_cparams(1),
    )(d_ret, d_swa, d_ret, d_swa, h, wt, x, norm_w, dy, sse, *small_acc)


SMALL_SHAPES = ((1, D), (1, 512), (1, SWA_D), (1, SWA_D), (1, SWA_H), (SWA_H, N_BUCKETS))


def _adamw_all(grad_x, g_in_t, g_out, sum_a, sum_b, w_in_t, m_in_t, v_in_t, w_out, m_out, v_out, small_w, small_m, small_v):
    n_small = len(SMALL_SHAPES)
    n_parts = 2
    gx_chunks = 4

    def body(*refs):
        gx_in = refs[0]
        gin_any, gout_any, suma_any, sumb_any, w_in_ref, m_in_ref, v_in_ref, w_out_ref, m_out_ref, v_out_ref = refs[1:11]
        pos = 11
        sw_refs, sm_refs, sv_refs = (refs[pos + i * n_small:pos + (i + 1) * n_small] for i in range(3))
        pos += 3 * n_small
        g_in_o, d_in_o, nm_in_o, nv_in_o, g_out_o, d_out_o, nm_out_o, nv_out_o = refs[pos:pos + 8]
        pos += 8
        sg_o, sd_o, snm_o, snv_o = (refs[pos + i * n_small:pos + (i + 1) * n_small] for i in range(4))
        pos += 4 * n_small
        loss_o, gx_out, gin_ref, gout_ref, suma_ref, sumb_ref, gx_buf, load_sems, gx_sems = refs[pos:]
        i = pl.program_id(0)
        loads = [pltpu.make_async_copy(src, dst, load_sems.at[j]) for j, (src, dst) in enumerate(
            ((gin_any, gin_ref), (gout_any, gout_ref), (suma_any, suma_ref), (sumb_any, sumb_ref)))]

        def gx_rows(j):
            return pl.ds(j * (T // gx_chunks), T // gx_chunks)

        def gx_load(j):
            return pltpu.make_async_copy(gx_in.at[gx_rows(j), :], gx_buf.at[gx_rows(j), :], gx_sems.at[0, j])

        def gx_store(j):
            return pltpu.make_async_copy(gx_buf.at[gx_rows(j), :], gx_out.at[gx_rows(j), :], gx_sems.at[1, j])

        def gx_forward(chunks):
            for j in chunks:
                gx_load(j).wait()
                gx_store(j).start()

        @pl.when(i == 0)
        def _():
            for cp in loads:
                cp.start()
            for j in range(gx_chunks):
                gx_load(j).start()
            for cp in loads:
                cp.wait()

        g = gout_ref[pl.ds(pl.multiple_of(i * (WOUT_BLK // n_parts), 8), WOUT_BLK // n_parts), :]
        d, nm, nv = _adamw(w_out_ref[...], g, m_out_ref[...], v_out_ref[...])
        g_out_o[...], d_out_o[...], nm_out_o[...], nv_out_o[...] = g, d, nm, nv
        g = gin_ref[pl.ds(pl.multiple_of(i * (WIN_BLK // n_parts), 8), WIN_BLK // n_parts), :]
        d, nm, nv = _adamw(w_in_ref[...], g, m_in_ref[...], v_in_ref[...])
        g_in_o[...], d_in_o[...], nm_in_o[...], nv_in_o[...] = g, d, nm, nv

        @pl.when(pl.program_id(0) == 0)
        def _():
            loss_o[...] = suma_ref[LOSS_ROW:LOSS_ROW + 1, 0:1] * (0.5 / D)
            for r, (rows, lanes) in enumerate(SMALL_SHAPES):
                g = suma_ref[r:r + 1, 0:lanes] if rows == 1 else sumb_ref[:, 0:lanes]
                d, nm, nv = _adamw(sw_refs[r][...], g, sm_refs[r][...], sv_refs[r][...])
                sg_o[r][...], sd_o[r][...], snm_o[r][...], snv_o[r][...] = g, d, nm, nv
            gx_forward(range(gx_chunks // 2))

        @pl.when(i == n_parts - 1)
        def _():
            gx_forward(range(gx_chunks // 2, gx_chunks))
            for j in range(gx_chunks):
                gx_store(j).wait()

    part = lambda rows: pl.BlockSpec((rows // n_parts, D), lambda i: (i, 0))
    const = lambda shape: pl.BlockSpec(shape, lambda i: (0,) * len(shape))
    win = jax.ShapeDtypeStruct((WIN_BLK, D), F32)
    wout = jax.ShapeDtypeStruct((WOUT_BLK, D), F32)
    smalls = tuple(jax.ShapeDtypeStruct(s, F32) for s in SMALL_SHAPES)
    small_specs = [const(s) for s in SMALL_SHAPES]
    hbm = pl.BlockSpec(memory_space=pltpu.HBM)
    outs = pl.pallas_call(
        body, name="adamw_all", grid=(n_parts,),
        in_specs=[hbm] * 5 + [part(WIN_BLK)] * 3 + [part(WOUT_BLK)] * 3 + small_specs * 3,
        out_specs=tuple([part(WIN_BLK)] * 4 + [part(WOUT_BLK)] * 4 + small_specs * 4 + [const((1, 1)), hbm]),
        out_shape=(win,) * 4 + (wout,) * 4 + smalls * 4 + (jax.ShapeDtypeStruct((1, 1), F32),
                                                           jax.ShapeDtypeStruct(grad_x.shape, F32)),
        scratch_shapes=[pltpu.VMEM((WIN_BLK, D), F32), pltpu.VMEM((WOUT_BLK, D), F32), pltpu.VMEM((8, D), F32),
                        pltpu.VMEM((SWA_H, CH), F32), pltpu.VMEM(grad_x.shape, F32), pltpu.SemaphoreType.DMA((4,)),
                        pltpu.SemaphoreType.DMA((2, gx_chunks))],
        compiler_params=_cparams(1, vmem=VMEM_WHOLE),
    )(grad_x, g_in_t, g_out, sum_a, sum_b, w_in_t, m_in_t, v_in_t, w_out, m_out, v_out, *small_w, *small_m, *small_v)
    big, rest = outs[:8], outs[8:]
    return big, [rest[i * n_small:(i + 1) * n_small] for i in range(4)], rest[4 * n_small], rest[4 * n_small + 1]


def _small_rows(norm_w, ret_norm_w, q_norm_w, k_norm_w, sinks, rel_bias):
    return (norm_w.reshape(1, D), ret_norm_w.reshape(1, 512), q_norm_w.reshape(1, SWA_D), k_norm_w.reshape(1, SWA_D),
            sinks.reshape(1, SWA_H), rel_bias.T)


def _small_leaves(rows):
    return (rows[0].reshape(D), rows[1].reshape(512), rows[2].reshape(SWA_D), rows[3].reshape(SWA_D),
            rows[4].reshape(SWA_H), rows[5].T)


def kernel(x, norm_w, w_in, ret_norm_w, q_norm_w, k_norm_w, sinks, rel_bias, w_out, loss_target, m_norm_w, m_w_in, m_ret_norm_w, m_q_norm_w, m_k_norm_w, m_sinks, m_rel_bias, m_w_out, v_norm_w, v_w_in, v_ret_norm_w, v_q_norm_w, v_k_norm_w, v_sinks, v_rel_bias, v_w_out):
    x2 = x.reshape(T, D)
    target = loss_target.reshape(T, D)
    nw = norm_w.reshape(1, D)
    relb = rel_bias.T
    ret_tables = _rotary_tables_t() + _retention_tables_t()
    bucket_t = _bucket_table_t()

    proj, wt, h = _in_proj_gather(x2, nw, w_in.T)
    (ro, mix_r, states, wo, rnw_col), (so, mix_s, qnw_col, knw_col) = _fused_call("attn_fwd", N_CH, [
        _retention_fwd(proj, ret_norm_w.reshape(1, RET_V), ret_tables, w_out),
        _swa_fwd(proj, q_norm_w.reshape(1, SWA_D), k_norm_w.reshape(1, SWA_D), sinks, relb, bucket_t)])
    sse, dy, dmix_r, dmix_s, gwo = _out_proj(mix_r, mix_s, wo.reshape(D, D), x2, target)

    (d_ret, drw_acc), (d_swa, dbias, dsink_acc, dqw_acc, dkw_acc), (g_out,) = _fused_call("attn_bwd", N_CH + 1, [
        _retention_bwd(proj, ro, states, dmix_r, rnw_col, ret_tables),
        _swa_bwd(proj, so, dmix_s, qnw_col, knw_col, sinks, relb, bucket_t),
        _w_out_reduce_scatter(gwo.reshape(N_DEV, WOUT_BLK, D))])
    grad_x, g_in_t, sum_a, sum_b = _in_proj_bwd_rs(d_ret, d_swa, h, wt, x2, nw, dy, sse,
                                                   (dbias, bucket_t, dsink_acc, dqw_acc, dkw_acc, drw_acc))

    small_w = _small_rows(norm_w, ret_norm_w, q_norm_w, k_norm_w, sinks, rel_bias)
    small_m = _small_rows(m_norm_w, m_ret_norm_w, m_q_norm_w, m_k_norm_w, m_sinks, m_rel_bias)
    small_v = _small_rows(v_norm_w, v_ret_norm_w, v_q_norm_w, v_k_norm_w, v_sinks, v_rel_bias)
    big, small, loss, grad_x = _adamw_all(grad_x, g_in_t, g_out, sum_a, sum_b, w_in.T, m_w_in.T, v_w_in.T, w_out, m_w_out, v_w_out,
                                  small_w, small_m, small_v)

    def leaves(i):
        a = _small_leaves(small[i])
        return (a[0], big[i].T, a[1], a[2], a[3], a[4], a[5], big[4 + i])

    return (loss.reshape(()), grad_x.reshape(1, T, D), *leaves(0), *leaves(1), *leaves(2), *leaves(3))
```
